```python
import math
import jax, jax.numpy as jnp
from jax import lax
import numpy as np

D_MODEL = 2048
BATCH = 8
SEQ = 2048
DEPTH = 1

HEAD_DIM = 128
N_HEADS = D_MODEL // HEAD_DIM
N_HEADS_SB = N_HEADS // 2
N_HEADS_DIL = N_HEADS - N_HEADS_SB
D_SB = N_HEADS_SB * HEAD_DIM
D_DIL = N_HEADS_DIL * HEAD_DIM
QKV_WIDTH = 3 * (D_SB + D_DIL)
DILATED_BRANCHES = ((128, 1), (512, 4), (2048, 16))
QUERY_BLOCK = 128
D_FF = 5504
CONV_WIDTH = 3
ROPE_THETA = 10000.0
RMS_EPS = 1e-6

kernel_name = 'hybrid_stickbreaking_dilated_convffn_layer'


def rmsnorm(x, gain):
    xf = x.astype(jnp.float32)
    y = xf * lax.rsqrt(jnp.mean(xf * xf, axis=-1, keepdims=True) + RMS_EPS)
    return (y * gain.astype(jnp.float32)).astype(x.dtype)


def head_rmsnorm(o, gain):
    H, Dh = o.shape[1], o.shape[3]
    of = o.astype(jnp.float32)
    y = of * lax.rsqrt(jnp.mean(of * of, axis=-1, keepdims=True) + RMS_EPS)
    return (y * gain.astype(jnp.float32).reshape(1, H, 1, Dh)).astype(o.dtype)


def apply_rope(x):
    S, Dh = x.shape[2], x.shape[3]
    inv_freq = ROPE_THETA ** (-jnp.arange(0, Dh, 2, dtype=jnp.float32) / Dh)
    ang = jnp.arange(S, dtype=jnp.float32)[:, None] * inv_freq[None, :]
    cos, sin = jnp.cos(ang), jnp.sin(ang)
    x1, x2 = jnp.split(x.astype(jnp.float32), 2, axis=-1)
    out = jnp.concatenate([x1 * cos - x2 * sin, x2 * cos + x1 * sin], axis=-1)
    return out.astype(x.dtype)


def to_heads(t, n_heads):
    B, S, _ = t.shape
    return t.reshape(B, S, n_heads, HEAD_DIM).transpose(0, 2, 1, 3)


def stick_breaking_attention(q, k, v):
    S, Dh = q.shape[2], q.shape[3]
    scale = Dh ** -0.5
    outs = []
    for blk in range(S // QUERY_BLOCK):
        q0 = blk * QUERY_BLOCK
        n_keys = q0 + QUERY_BLOCK
        q_blk = q[:, :, q0:n_keys]
        k_pre, v_pre = k[:, :, :n_keys], v[:, :, :n_keys]
        z = jnp.einsum('bhqd,bhkd->bhqk', q_blk, k_pre).astype(jnp.float32) * scale
        q_pos = q0 + jnp.arange(QUERY_BLOCK)
        k_pos = jnp.arange(n_keys)
        causal = k_pos[None, :] < q_pos[:, None]
        log_beta = jax.nn.log_sigmoid(z)
        log_keep = jnp.where(causal, jax.nn.log_sigmoid(-z), 0.0)
        log_remain = lax.cumsum(log_keep, axis=3, reverse=True) - log_keep
        a = jnp.where(causal, jnp.exp(log_beta + log_remain), 0.0)
        outs.append(jnp.einsum('bhqk,bhkd->bhqd', a.astype(v.dtype), v_pre))
    return jnp.concatenate(outs, axis=2)


def dilated_branch(q, k, v, window, dilation):
    B, H, S, Dh = q.shape
    n_back = window // dilation
    QB = QUERY_BLOCK
    L = S // dilation
    n_blocks = -(-L // QB)
    Lp = n_blocks * QB
    scale = Dh ** -0.5

    def to_sub(t):
        return t.reshape(B, H, L, dilation, Dh).transpose(0, 1, 3, 2, 4)

    qs = jnp.pad(to_sub(q), ((0, 0), (0, 0), (0, 0), (0, Lp - L), (0, 0)))
    pad_kv = ((0, 0), (0, 0), (0, 0), (QB, Lp - L), (0, 0))
    ks = jnp.pad(to_sub(k), pad_kv)
    vs = jnp.pad(to_sub(v), pad_kv)
    qb = qs.reshape(B, H, dilation, n_blocks, QB, Dh)

    def band(t):
        prev = t[:, :, :, :Lp].reshape(B, H, dilation, n_blocks, QB, Dh)
        cur = t[:, :, :, QB:QB + Lp].reshape(B, H, dilation, n_blocks, QB, Dh)
        return jnp.concatenate([prev, cur], axis=4)

    kb, vb = band(ks), band(vs)
    s = jnp.einsum('bhrnqd,bhrnkd->bhrnqk', qb, kb).astype(jnp.float32) * scale
    q_idx = jnp.arange(n_blocks)[:, None] * QB + jnp.arange(QB)[None, :]
    k_idx = jnp.arange(n_blocks)[:, None] * QB - QB + jnp.arange(2 * QB)[None, :]
    dist = q_idx[:, :, None] - k_idx[:, None, :]
    valid = (dist >= 0) & (dist <= n_back) & (k_idx[:, None, :] >= 0)
    s = jnp.where(valid, s, -jnp.inf)
    m = jnp.max(s, axis=-1, keepdims=True)
    p = jnp.exp(s - m)
    den = jnp.sum(p, axis=-1, keepdims=True)
    out = jnp.einsum('bhrnqk,bhrnkd->bhrnqd', p.astype(v.dtype), vb).astype(jnp.float32) / den
    lse = (m + jnp.log(den))[..., 0]
    out = out.reshape(B, H, dilation, Lp, Dh)[:, :, :, :L]
    out = out.transpose(0, 1, 3, 2, 4).reshape(B, H, S, Dh)
    lse = lse.reshape(B, H, dilation, Lp)[:, :, :, :L].transpose(0, 1, 3, 2).reshape(B, H, S)
    return out, lse


def dilated_attention(q, k, v):
    outs, lses = [], []
    for window, dilation in DILATED_BRANCHES:
        o, l = dilated_branch(q, k, v, window, dilation)
        outs.append(o)
        lses.append(l)
    w = jax.nn.softmax(jnp.stack(lses, axis=0), axis=0)
    out = jnp.sum(w[..., None] * jnp.stack(outs, axis=0), axis=0)
    return out.astype(q.dtype)


def conv_geglu_ffn(h, w_up, conv_w, conv_b, w_down):
    S = h.shape[1]
    u = jnp.einsum('bsd,df->bsf', h, w_up)
    up = jnp.pad(u, ((0, 0), (CONV_WIDTH - 1, 0), (0, 0)))
    u = sum(up[:, j:j + S] * conv_w[j] for j in range(CONV_WIDTH)) + conv_b
    gate, val = jnp.split(u, 2, axis=-1)
    y = jax.nn.gelu(gate, approximate=True) * val
    return jnp.einsum('bsf,fd->bsd', y, w_down)


def _fwd_setup_inputs(seed: int = 0) -> dict:
    key = jax.random.key(seed)
    ks = jax.random.split(key, 16)
    f32 = jnp.float32

    def gain(k, n):
        return 1.0 + 0.05 * jax.random.normal(k, (DEPTH, n), f32)

    return {
        'x': jax.random.normal(ks[0], (BATCH, SEQ, D_MODEL), f32),
        'pre_mix_gain': gain(ks[1], D_MODEL),
        'post_mix_gain': gain(ks[2], D_MODEL),
        'pre_ffn_gain': gain(ks[3], D_MODEL),
        'post_ffn_gain': gain(ks[4], D_MODEL),
        'w_in': jax.random.normal(ks[5], (DEPTH, D_MODEL, QKV_WIDTH), f32) * D_MODEL ** -0.5,
        'sb_out_gain': gain(ks[6], D_SB),
        'dil_out_gain': gain(ks[7], D_DIL),
        'w_out': jax.random.normal(ks[8], (DEPTH, D_SB + D_DIL, D_MODEL), f32) * (D_SB + D_DIL) ** -0.5,
        'w_up': jax.random.normal(ks[9], (DEPTH, D_MODEL, 2 * D_FF), f32) * D_MODEL ** -0.5,
        'conv_w': jax.random.normal(ks[10], (DEPTH, CONV_WIDTH, 2 * D_FF), f32) * CONV_WIDTH ** -0.5,
        'conv_b': 0.02 * jax.random.normal(ks[11], (DEPTH, 2 * D_FF), f32),
        'w_down': jax.random.normal(ks[12], (DEPTH, D_FF, D_MODEL), f32) * D_FF ** -0.5,
    }


def _fwd_reference(x, pre_mix_gain, post_mix_gain, pre_ffn_gain, post_ffn_gain, w_in,
              sb_out_gain, dil_out_gain, w_out, w_up, conv_w, conv_b, w_down):
    splits = [D_SB, 2 * D_SB, 3 * D_SB, 3 * D_SB + D_DIL, 3 * D_SB + 2 * D_DIL]
    for layer in range(DEPTH):
        h = rmsnorm(x, pre_mix_gain[layer])
        proj = jnp.einsum('bsd,de->bse', h, w_in[layer])
        q_sb, k_sb, v_sb, q_dl, k_dl, v_dl = jnp.split(proj, splits, axis=-1)
        o_sb = stick_breaking_attention(to_heads(q_sb, N_HEADS_SB), to_heads(k_sb, N_HEADS_SB),
                                        to_heads(v_sb, N_HEADS_SB))
        o_dl = dilated_attention(apply_rope(to_heads(q_dl, N_HEADS_DIL)),
                                 apply_rope(to_heads(k_dl, N_HEADS_DIL)),
                                 to_heads(v_dl, N_HEADS_DIL))
        o_sb = head_rmsnorm(o_sb, sb_out_gain[layer])
        o_dl = head_rmsnorm(o_dl, dil_out_gain[layer])
        B, _, S, _ = o_sb.shape
        mixed = jnp.concatenate([o_sb.transpose(0, 2, 1, 3).reshape(B, S, D_SB),
                                 o_dl.transpose(0, 2, 1, 3).reshape(B, S, D_DIL)], axis=-1)
        mix_out = jnp.einsum('bse,ed->bsd', mixed, w_out[layer])
        x = x + rmsnorm(mix_out, post_mix_gain[layer])
        h = rmsnorm(x, pre_ffn_gain[layer])
        f = conv_geglu_ffn(h, w_up[layer], conv_w[layer], conv_b[layer], w_down[layer])
        x = x + rmsnorm(f, post_ffn_gain[layer])
    return x


import jax as _jax
import jax.numpy as _jnp

TWIN_FORMAT = 'train_step'
FWD_PARAMS = ['x', 'pre_mix_gain', 'post_mix_gain', 'pre_ffn_gain', 'post_ffn_gain', 'w_in', 'sb_out_gain', 'dil_out_gain', 'w_out', 'w_up', 'conv_w', 'conv_b', 'w_down']
TWIN_WEIGHTS = ['pre_mix_gain', 'post_mix_gain', 'pre_ffn_gain', 'post_ffn_gain', 'w_in', 'sb_out_gain', 'dil_out_gain', 'w_out', 'w_up', 'conv_w', 'conv_b', 'w_down']
TWIN_DIFF_INPUT = 'x'
TWIN_INPUTS = ['x', 'pre_mix_gain', 'post_mix_gain', 'pre_ffn_gain', 'post_ffn_gain', 'w_in', 'sb_out_gain', 'dil_out_gain', 'w_out', 'w_up', 'conv_w', 'conv_b', 'w_down', 'loss_target', 'm_pre_mix_gain', 'm_post_mix_gain', 'm_pre_ffn_gain', 'm_post_ffn_gain', 'm_w_in', 'm_sb_out_gain', 'm_dil_out_gain', 'm_w_out', 'm_w_up', 'm_conv_w', 'm_conv_b', 'm_w_down', 'v_pre_mix_gain', 'v_post_mix_gain', 'v_pre_ffn_gain', 'v_post_ffn_gain', 'v_w_in', 'v_sb_out_gain', 'v_dil_out_gain', 'v_w_out', 'v_w_up', 'v_conv_w', 'v_conv_b', 'v_w_down']
TWIN_OUTPUTS = ['loss', 'grad_x', 'grad_pre_mix_gain', 'grad_post_mix_gain', 'grad_pre_ffn_gain', 'grad_post_ffn_gain', 'grad_w_in', 'grad_sb_out_gain', 'grad_dil_out_gain', 'grad_w_out', 'grad_w_up', 'grad_conv_w', 'grad_conv_b', 'grad_w_down', 'delta_pre_mix_gain', 'delta_post_mix_gain', 'delta_pre_ffn_gain', 'delta_post_ffn_gain', 'delta_w_in', 'delta_sb_out_gain', 'delta_dil_out_gain', 'delta_w_out', 'delta_w_up', 'delta_conv_w', 'delta_conv_b', 'delta_w_down', 'new_m_pre_mix_gain', 'new_m_post_mix_gain', 'new_m_pre_ffn_gain', 'new_m_post_ffn_gain', 'new_m_w_in', 'new_m_sb_out_gain', 'new_m_dil_out_gain', 'new_m_w_out', 'new_m_w_up', 'new_m_conv_w', 'new_m_conv_b', 'new_m_w_down', 'new_v_pre_mix_gain', 'new_v_post_mix_gain', 'new_v_pre_ffn_gain', 'new_v_post_ffn_gain', 'new_v_w_in', 'new_v_sb_out_gain', 'new_v_dil_out_gain', 'new_v_w_out', 'new_v_w_up', 'new_v_conv_w', 'new_v_conv_b', 'new_v_w_down']
TWIN_LEAF_KINDS = {'loss': 'loss', 'grad_x': 'grad_x', 'grad_pre_mix_gain': 'grad_w', 'grad_post_mix_gain': 'grad_w', 'grad_pre_ffn_gain': 'grad_w', 'grad_post_ffn_gain': 'grad_w', 'grad_w_in': 'grad_w', 'grad_sb_out_gain': 'grad_w', 'grad_dil_out_gain': 'grad_w', 'grad_w_out': 'grad_w', 'grad_w_up': 'grad_w', 'grad_conv_w': 'grad_w', 'grad_conv_b': 'grad_w', 'grad_w_down': 'grad_w', 'delta_pre_mix_gain': 'delta_w', 'delta_post_mix_gain': 'delta_w', 'delta_pre_ffn_gain': 'delta_w', 'delta_post_ffn_gain': 'delta_w', 'delta_w_in': 'delta_w', 'delta_sb_out_gain': 'delta_w', 'delta_dil_out_gain': 'delta_w', 'delta_w_out': 'delta_w', 'delta_w_up': 'delta_w', 'delta_conv_w': 'delta_w', 'delta_conv_b': 'delta_w', 'delta_w_down': 'delta_w', 'new_m_pre_mix_gain': 'new_m', 'new_m_post_mix_gain': 'new_m', 'new_m_pre_ffn_gain': 'new_m', 'new_m_post_ffn_gain': 'new_m', 'new_m_w_in': 'new_m', 'new_m_sb_out_gain': 'new_m', 'new_m_dil_out_gain': 'new_m', 'new_m_w_out': 'new_m', 'new_m_w_up': 'new_m', 'new_m_conv_w': 'new_m', 'new_m_conv_b': 'new_m', 'new_m_w_down': 'new_m', 'new_v_pre_mix_gain': 'new_v', 'new_v_post_mix_gain': 'new_v', 'new_v_pre_ffn_gain': 'new_v', 'new_v_post_ffn_gain': 'new_v', 'new_v_w_in': 'new_v', 'new_v_sb_out_gain': 'new_v', 'new_v_dil_out_gain': 'new_v', 'new_v_w_out': 'new_v', 'new_v_w_up': 'new_v', 'new_v_conv_w': 'new_v', 'new_v_conv_b': 'new_v', 'new_v_w_down': 'new_v'}


def _forward(args):
    return _fwd_reference(*[args[k] for k in FWD_PARAMS])


def _output_shape():
    out = _jax.eval_shape(lambda: _forward(_fwd_setup_inputs(0)))
    return out.shape, out.dtype

N_MICROBATCH = 1
ADAM_LR = 0.001
ADAM_B1 = 0.9
ADAM_B2 = 0.999
ADAM_EPS = 1e-08
ADAM_WD = 0.01
ADAM_STEP = 10
PER_EXAMPLE_BATCH_AXIS = {'x': 0, 'loss_target': 0}
SHARED_INPUTS = []
_WEIGHT_DTYPES = {'pre_mix_gain': _jnp.float32, 'post_mix_gain': _jnp.float32, 'pre_ffn_gain': _jnp.float32, 'post_ffn_gain': _jnp.float32, 'w_in': _jnp.float32, 'sb_out_gain': _jnp.float32, 'dil_out_gain': _jnp.float32, 'w_out': _jnp.float32, 'w_up': _jnp.float32, 'conv_w': _jnp.float32, 'conv_b': _jnp.float32, 'w_down': _jnp.float32}
MOMENT_SCALE = {'pre_mix_gain': 2.391201e-01, 'post_mix_gain': 7.984525e+00, 'pre_ffn_gain': 1.723368e-01, 'post_ffn_gain': 8.004640e+00, 'w_in': 1.378455e-01, 'sb_out_gain': 1.707373e-01, 'dil_out_gain': 1.837694e-01, 'w_out': 1.711865e-01, 'w_up': 7.350727e-02, 'conv_w': 7.599969e-02, 'conv_b': 1.300808e-01, 'w_down': 1.283045e-01}


def _to_microbatches(a, axis):
    t = _jnp.moveaxis(a, axis, 0)
    t = t.reshape((N_MICROBATCH, t.shape[0] // N_MICROBATCH) + t.shape[1:])
    return _jnp.moveaxis(t, 1, axis + 1)


def setup_inputs(seed: int = 0) -> dict:
    inp = _fwd_setup_inputs(seed)
    key = _jax.random.fold_in(_jax.random.key(seed), 7919)
    shape, _ = _output_shape()
    out = dict(inp)
    out["loss_target"] = _jax.random.normal(_jax.random.fold_in(key, 0), shape, _jnp.float32)
    for i, name in enumerate(TWIN_WEIGHTS):
        w = inp[name].astype(_jnp.float32)
        if MOMENT_SCALE is None:
            s = _jnp.sqrt(_jnp.mean(_jnp.square(w)) + 1e-30)
        else:
            s = MOMENT_SCALE[name]
        km, kv = _jax.random.split(_jax.random.fold_in(key, i + 1))
        out[name] = w
        out["m_" + name] = s * _jax.random.normal(km, w.shape, _jnp.float32)
        out["v_" + name] = (s * s) * _jax.random.uniform(kv, w.shape, _jnp.float32, 0.5, 1.5)
    if N_MICROBATCH > 1:
        for name, axis in PER_EXAMPLE_BATCH_AXIS.items():
            out[name] = _to_microbatches(out[name], axis)
    return {'x': out['x'], 'pre_mix_gain': out['pre_mix_gain'], 'post_mix_gain': out['post_mix_gain'], 'pre_ffn_gain': out['pre_ffn_gain'], 'post_ffn_gain': out['post_ffn_gain'], 'w_in': out['w_in'], 'sb_out_gain': out['sb_out_gain'], 'dil_out_gain': out['dil_out_gain'], 'w_out': out['w_out'], 'w_up': out['w_up'], 'conv_w': out['conv_w'], 'conv_b': out['conv_b'], 'w_down': out['w_down'], 'loss_target': out['loss_target'], 'm_pre_mix_gain': out['m_pre_mix_gain'], 'm_post_mix_gain': out['m_post_mix_gain'], 'm_pre_ffn_gain': out['m_pre_ffn_gain'], 'm_post_ffn_gain': out['m_post_ffn_gain'], 'm_w_in': out['m_w_in'], 'm_sb_out_gain': out['m_sb_out_gain'], 'm_dil_out_gain': out['m_dil_out_gain'], 'm_w_out': out['m_w_out'], 'm_w_up': out['m_w_up'], 'm_conv_w': out['m_conv_w'], 'm_conv_b': out['m_conv_b'], 'm_w_down': out['m_w_down'], 'v_pre_mix_gain': out['v_pre_mix_gain'], 'v_post_mix_gain': out['v_post_mix_gain'], 'v_pre_ffn_gain': out['v_pre_ffn_gain'], 'v_post_ffn_gain': out['v_post_ffn_gain'], 'v_w_in': out['v_w_in'], 'v_sb_out_gain': out['v_sb_out_gain'], 'v_dil_out_gain': out['v_dil_out_gain'], 'v_w_out': out['v_w_out'], 'v_w_up': out['v_w_up'], 'v_conv_w': out['v_conv_w'], 'v_conv_b': out['v_conv_b'], 'v_w_down': out['v_w_down']}


def _loss(weights, diff, rest, loss_target):
    with _jax.named_scope("forward"):
        args = {**rest, TWIN_DIFF_INPUT: diff, **{k: w.astype(_WEIGHT_DTYPES[k]) for k, w in weights.items()}}
        y = _forward(args)
    with _jax.named_scope("loss_head"):
        err = _jnp.square(y.astype(_jnp.float32) - loss_target)
        return 0.5 * _jnp.sum(_jnp.mean(err, axis=-1)) if err.ndim else 0.5 * err


def _adamw(w, g, m, v):
    m = ADAM_B1 * m + (1.0 - ADAM_B1) * g
    v = ADAM_B2 * v + (1.0 - ADAM_B2) * _jnp.square(g)
    m_hat = m / (1.0 - ADAM_B1 ** ADAM_STEP)
    v_hat = v / (1.0 - ADAM_B2 ** ADAM_STEP)
    delta = -ADAM_LR * (m_hat / (_jnp.sqrt(v_hat) + ADAM_EPS) + ADAM_WD * w)
    return delta, m, v


def reference(x, pre_mix_gain, post_mix_gain, pre_ffn_gain, post_ffn_gain, w_in, sb_out_gain, dil_out_gain, w_out, w_up, conv_w, conv_b, w_down, loss_target, m_pre_mix_gain, m_post_mix_gain, m_pre_ffn_gain, m_post_ffn_gain, m_w_in, m_sb_out_gain, m_dil_out_gain, m_w_out, m_w_up, m_conv_w, m_conv_b, m_w_down, v_pre_mix_gain, v_post_mix_gain, v_pre_ffn_gain, v_post_ffn_gain, v_w_in, v_sb_out_gain, v_dil_out_gain, v_w_out, v_w_up, v_conv_w, v_conv_b, v_w_down):
    given = dict(x=x, pre_mix_gain=pre_mix_gain, post_mix_gain=post_mix_gain, pre_ffn_gain=pre_ffn_gain, post_ffn_gain=post_ffn_gain, w_in=w_in, sb_out_gain=sb_out_gain, dil_out_gain=dil_out_gain, w_out=w_out, w_up=w_up, conv_w=conv_w, conv_b=conv_b, w_down=w_down, loss_target=loss_target, m_pre_mix_gain=m_pre_mix_gain, m_post_mix_gain=m_post_mix_gain, m_pre_ffn_gain=m_pre_ffn_gain, m_post_ffn_gain=m_post_ffn_gain, m_w_in=m_w_in, m_sb_out_gain=m_sb_out_gain, m_dil_out_gain=m_dil_out_gain, m_w_out=m_w_out, m_w_up=m_w_up, m_conv_w=m_conv_w, m_conv_b=m_conv_b, m_w_down=m_w_down, v_pre_mix_gain=v_pre_mix_gain, v_post_mix_gain=v_post_mix_gain, v_pre_ffn_gain=v_pre_ffn_gain, v_post_ffn_gain=v_post_ffn_gain, v_w_in=v_w_in, v_sb_out_gain=v_sb_out_gain, v_dil_out_gain=v_dil_out_gain, v_w_out=v_w_out, v_w_up=v_w_up, v_conv_w=v_conv_w, v_conv_b=v_conv_b, v_w_down=v_w_down)
    weights = {n: given[n] for n in TWIN_WEIGHTS}
    shared = {n: given[n] for n in SHARED_INPUTS}
    per_example = {n: given[n] for n in ['x']}
    grad_fn = _jax.value_and_grad(_loss, argnums=(0, 1))

    def one_microbatch(ex, loss_target):
        ex = dict(ex)
        diff = ex.pop(TWIN_DIFF_INPUT)
        return grad_fn(weights, diff, {**shared, **ex}, loss_target)

    if N_MICROBATCH == 1:
        loss, (grad_w, grad_x) = one_microbatch(per_example, given["loss_target"])
    else:
        def body(carry, xs):
            loss_sum, grad_sum = carry
            l_k, (gw_k, gx_k) = one_microbatch(xs[0], xs[1])
            with _jax.named_scope("update"):
                return (loss_sum + l_k, _jax.tree.map(_jnp.add, grad_sum, gw_k)), gx_k

        init = (_jnp.zeros((), _jnp.float32), _jax.tree.map(_jnp.zeros_like, weights))
        (loss, grad_w), grad_x = _jax.lax.scan(body, init, (per_example, given["loss_target"]))
    with _jax.named_scope("update"):
        delta_w, new_m, new_v = {}, {}, {}
        for n in TWIN_WEIGHTS:
            delta_w[n], new_m[n], new_v[n] = _adamw(weights[n], grad_w[n], given["m_" + n], given["v_" + n])
    return (loss, grad_x, *[grad_w[n] for n in TWIN_WEIGHTS], *[delta_w[n] for n in TWIN_WEIGHTS],
            *[new_m[n] for n in TWIN_WEIGHTS], *[new_v[n] for n in TWIN_WEIGHTS])
```

```python
import functools
import math

import jax
import jax.numpy as jnp
from jax import lax
from jax.experimental import pallas as pl
from jax.experimental.pallas import tpu as pltpu

F32 = jnp.float32
BF16 = jnp.bfloat16
HEAD_DIM = 128
LANES = 128
KEY_BLOCK = 128
DILATIONS = (1, 4, 16)
RMS_EPS = 1e-6
ROPE_THETA = 10000.0
NEG = -1e30
ADAM_LR, ADAM_B1, ADAM_B2, ADAM_EPS, ADAM_WD, ADAM_STEP = 0.001, 0.9, 0.999, 1e-08, 0.01, 10
MESH = pl.DeviceIdType.MESH
N_DEV = 8
N_CHIP = 4
HBM = pl.BlockSpec(memory_space=pl.ANY)
VMEM_LIMIT = 56 * 1024 * 1024

_pcall = pl.pallas_call


def _tile(n, pref, mult=LANES):
    best = None
    t = mult
    while t <= min(n, pref):
        if n % t == 0:
            best = t
        t += mult
    return n if best is None else best


def _params(*sem):
    return pltpu.CompilerParams(dimension_semantics=sem, vmem_limit_bytes=VMEM_LIMIT)


def _dot(a, b, dims):
    return lax.dot_general(a, b, (dims, ((), ())), preferred_element_type=F32)


NN = ((1,), (0,))
NT = ((1,), (1,))
TN = ((0,), (0,))


def _mm_nn(a, b3, out_dtype, name, tm=512, tn=1408, tk=512):
    M, K = a.shape
    C, _, n = b3.shape
    tm, tk, tn = _tile(M, tm, 8), _tile(K, tk), _tile(n, tn)
    npc, nk = n // tn, K // tk

    def body(a_ref, b_ref, o_ref, acc_ref):
        k = pl.program_id(2)

        @pl.when(k == 0)
        def _():
            acc_ref[...] = jnp.zeros_like(acc_ref)

        acc_ref[...] += _dot(a_ref[...].astype(BF16), b_ref[...].astype(BF16), NN)

        @pl.when(k == nk - 1)
        def _():
            o_ref[...] = acc_ref[...].astype(o_ref.dtype)

    return _pcall(
        body, grid=(M // tm, C * npc, nk),
        in_specs=[pl.BlockSpec((tm, tk), lambda i, j, k: (i, k)),
                  pl.BlockSpec((None, tk, tn), lambda i, j, k: (j // npc, k, j % npc))],
        out_specs=pl.BlockSpec((tm, tn), lambda i, j, k: (i, j)),
        out_shape=jax.ShapeDtypeStruct((M, C * n), out_dtype),
        scratch_shapes=[pltpu.VMEM((tm, tn), F32)],
        compiler_params=_params("parallel", "parallel", "arbitrary"), name=name)(a, b3)


def _mm_nt(a, b3, out_dtype, name, tm=512, tn=512, tk=1408):
    M, _ = a.shape
    C, N, n = b3.shape
    tm, tn, tk = _tile(M, tm, 8), _tile(N, tn), _tile(n, tk)
    kpc = n // tk
    nk = C * kpc

    def body(a_ref, b_ref, o_ref, acc_ref):
        k = pl.program_id(2)

        @pl.when(k == 0)
        def _():
            acc_ref[...] = jnp.zeros_like(acc_ref)

        acc_ref[...] += _dot(a_ref[...].astype(BF16), b_ref[...].astype(BF16), NT)

        @pl.when(k == nk - 1)
        def _():
            o_ref[...] = acc_ref[...].astype(o_ref.dtype)

    return _pcall(
        body, grid=(M // tm, N // tn, nk),
        in_specs=[pl.BlockSpec((tm, tk), lambda i, j, k: (i, k)),
                  pl.BlockSpec((None, tn, tk), lambda i, j, k: (k // kpc, j, k % kpc))],
        out_specs=pl.BlockSpec((tm, tn), lambda i, j, k: (i, j)),
        out_shape=jax.ShapeDtypeStruct((M, N), out_dtype),
        scratch_shapes=[pltpu.VMEM((tm, tn), F32)],
        compiler_params=_params("parallel", "parallel", "arbitrary"), name=name)(a, b3)


def _mm_tn(x, y, n, out_dtype, name, tm=512, tn=1408, tk=512):
    S, P = x.shape
    C = y.shape[1] // n
    tm, tn, tk = _tile(P, tm), _tile(n, tn), _tile(S, tk, 8)
    npc, nk = n // tn, S // tk

    def body(x_ref, y_ref, o_ref, acc_ref):
        k = pl.program_id(2)

        @pl.when(k == 0)
        def _():
            acc_ref[...] = jnp.zeros_like(acc_ref)

        acc_ref[...] += _dot(x_ref[...].astype(BF16), y_ref[...].astype(BF16), TN)

        @pl.when(k == nk - 1)
        def _():
            o_ref[...] = acc_ref[...].astype(o_ref.dtype)

    return _pcall(
        body, grid=(P // tm, C * npc, nk),
        in_specs=[pl.BlockSpec((tk, tm), lambda i, j, k: (k, i)),
                  pl.BlockSpec((tk, tn), lambda i, j, k: (k, j))],
        out_specs=pl.BlockSpec((None, tm, tn), lambda i, j, k: (j // npc, i, j % npc)),
        out_shape=jax.ShapeDtypeStruct((C, P, n), out_dtype),
        scratch_shapes=[pltpu.VMEM((tm, tn), F32)],
        compiler_params=_params("parallel", "parallel", "arbitrary"), name=name)(x, y)


def _rms_scale(v):
    return lax.rsqrt(jnp.mean(v * v, axis=-1, keepdims=True) + RMS_EPS)


def _rms_bwd(gy, v, r):
    return r * gy - v * (r * r * r * jnp.mean(gy * v, axis=-1, keepdims=True))


def _rows_spec(tm, d):
    return pl.BlockSpec((tm, d), lambda i: (i, 0))


def _vec_spec(d):
    return pl.BlockSpec((1, d), lambda i: (0, 0))


def _rms_fwd(x, g, name, tm=256):
    S, D = x.shape

    def body(x_ref, g_ref, h_ref):
        v = x_ref[...]
        h_ref[...] = (v * _rms_scale(v) * g_ref[...]).astype(BF16)

    return _pcall(body, grid=(S // tm,), in_specs=[_rows_spec(tm, D), _vec_spec(D)], out_specs=_rows_spec(tm, D),
                  out_shape=jax.ShapeDtypeStruct((S, D), BF16), compiler_params=_params("parallel"), name=name)(x, g)


def _mid_fwd(x, mix, g_post, g_pre, name, tm=256):
    S, D = x.shape

    def body(x_ref, m_ref, gp_ref, gn_ref, x2_ref, h_ref):
        m = m_ref[...]
        x2 = x_ref[...] + m * _rms_scale(m) * gp_ref[...]
        x2_ref[...] = x2
        h_ref[...] = (x2 * _rms_scale(x2) * gn_ref[...]).astype(BF16)

    return _pcall(body, grid=(S // tm,), in_specs=[_rows_spec(tm, D), _rows_spec(tm, D), _vec_spec(D), _vec_spec(D)],
                  out_specs=[_rows_spec(tm, D), _rows_spec(tm, D)],
                  out_shape=[jax.ShapeDtypeStruct((S, D), F32), jax.ShapeDtypeStruct((S, D), BF16)],
                  compiler_params=_params("parallel"), name=name)(x, mix, g_post, g_pre)


def _loss_bwd(x2, f, tgt, g_post, name, tm=256):
    S, D = x2.shape

    def body(x2_ref, f_ref, t_ref, g_ref, dy_ref, df_ref, dg_ref, ls_ref):
        i = pl.program_id(0)

        @pl.when(i == 0)
        def _():
            dg_ref[...] = jnp.zeros_like(dg_ref)
            ls_ref[...] = jnp.zeros_like(ls_ref)

        fv = f_ref[...]
        r = _rms_scale(fv)
        g = g_ref[...]
        err = x2_ref[...] + fv * r * g - t_ref[...]
        ls_ref[...] += jnp.broadcast_to(0.5 * jnp.sum(jnp.mean(err * err, axis=-1, keepdims=True), axis=0, keepdims=True), ls_ref.shape)
        dy = err * (1.0 / D)
        dy_ref[...] = dy
        df_ref[...] = _rms_bwd(dy * g, fv, r).astype(BF16)
        dg_ref[...] += jnp.sum(dy * fv * r, axis=0, keepdims=True)

    return _pcall(body, grid=(S // tm,),
                  in_specs=[_rows_spec(tm, D), _rows_spec(tm, D), _rows_spec(tm, D), _vec_spec(D)],
                  out_specs=[_rows_spec(tm, D), _rows_spec(tm, D), _vec_spec(D), _vec_spec(LANES)],
                  out_shape=[jax.ShapeDtypeStruct((S, D), F32), jax.ShapeDtypeStruct((S, D), BF16),
                             jax.ShapeDtypeStruct((1, D), F32), jax.ShapeDtypeStruct((1, LANES), F32)],
                  compiler_params=_params("arbitrary"), name=name)(x2, f, tgt, g_post)


def _mid_bwd(dy, dh2, x2, mix, g_pre, g_post, name, tm=256):
    S, D = dy.shape

    def body(dy_ref, dh_ref, x2_ref, m_ref, gn_ref, gp_ref, dx2_ref, dm_ref, dgn_ref, dgp_ref):
        i = pl.program_id(0)

        @pl.when(i == 0)
        def _():
            dgn_ref[...] = jnp.zeros_like(dgn_ref)
            dgp_ref[...] = jnp.zeros_like(dgp_ref)

        x2, dh = x2_ref[...], dh_ref[...]
        r = _rms_scale(x2)
        dx2 = dy_ref[...] + _rms_bwd(dh * gn_ref[...], x2, r)
        dgn_ref[...] += jnp.sum(dh * x2 * r, axis=0, keepdims=True)
        dx2_ref[...] = dx2
        m = m_ref[...]
        rm = _rms_scale(m)
        dm_ref[...] = _rms_bwd(dx2 * gp_ref[...], m, rm).astype(BF16)
        dgp_ref[...] += jnp.sum(dx2 * m * rm, axis=0, keepdims=True)

    return _pcall(body, grid=(S // tm,),
                  in_specs=[_rows_spec(tm, D)] * 4 + [_vec_spec(D)] * 2,
                  out_specs=[_rows_spec(tm, D), _rows_spec(tm, D), _vec_spec(D), _vec_spec(D)],
                  out_shape=[jax.ShapeDtypeStruct((S, D), F32), jax.ShapeDtypeStruct((S, D), BF16),
                             jax.ShapeDtypeStruct((1, D), F32), jax.ShapeDtypeStruct((1, D), F32)],
                  compiler_params=_params("arbitrary"), name=name)(dy, dh2, x2, mix, g_pre, g_post)


def _first_bwd(dx2, dh1, x, g_pre, name, tm=256):
    S, D = x.shape

    def body(dx2_ref, dh_ref, x_ref, g_ref, gx_ref, dg_ref):
        i = pl.program_id(0)

        @pl.when(i == 0)
        def _():
            dg_ref[...] = jnp.zeros_like(dg_ref)

        xv, dh = x_ref[...], dh_ref[...]
        r = _rms_scale(xv)
        gx_ref[...] = dx2_ref[...] + _rms_bwd(dh * g_ref[...], xv, r)
        dg_ref[...] += jnp.sum(dh * xv * r, axis=0, keepdims=True)

    return _pcall(body, grid=(S // tm,), in_specs=[_rows_spec(tm, D)] * 3 + [_vec_spec(D)],
                  out_specs=[_rows_spec(tm, D), _vec_spec(D)],
                  out_shape=[jax.ShapeDtypeStruct((S, D), F32), jax.ShapeDtypeStruct((1, D), F32)],
                  compiler_params=_params("arbitrary"), name=name)(dx2, dh1, x, g_pre)


def _logsig_pair(z):
    sp = jnp.log(1.0 + jnp.exp(-jnp.abs(z)))
    return jnp.minimum(z, 0.0) - sp, jnp.minimum(-z, 0.0) - sp


def _split_dot(v, u):
    hi = v.astype(BF16)
    lo = (v - hi.astype(F32)).astype(BF16)
    return _dot(hi, u, NN) + _dot(lo, u, NN)


def _head_out(o, g):
    return o * _rms_scale(o) * g


def _sb_fwd(proj, gain, n_heads, name, tq=512):
    S = proj.shape[0]
    H, tk = n_heads, KEY_BLOCK
    tq = _tile(S, tq, tk)
    scale = HEAD_DIM ** -0.5

    def body(q_ref, k_ref, v_ref, g_ref, o_ref, ct_ref, mx_ref, oacc, cacc):
        i = pl.program_id(1)
        q = q_ref[...].astype(BF16)
        oacc[...] = jnp.zeros_like(oacc)
        cacc[...] = jnp.zeros_like(cacc)
        row = i * tq + lax.broadcasted_iota(jnp.int32, (tq, tk), 0)
        col = lax.broadcasted_iota(jnp.int32, (tq, tk), 1)
        later = (lax.broadcasted_iota(jnp.int32, (tk, tk), 0) > lax.broadcasted_iota(jnp.int32, (tk, tk), 1)).astype(BF16)
        nkb = (i + 1) * (tq // tk)

        def step(it, carry):
            k0 = pl.multiple_of((nkb - 1 - it) * tk, tk)
            kj = k_ref[pl.ds(k0, tk), :].astype(BF16)
            vj = v_ref[pl.ds(k0, tk), :].astype(BF16)
            z = _dot(q, kj, NT) * scale
            causal = (col + k0) < row
            lb, lk = _logsig_pair(z)
            lk = jnp.where(causal, lk, 0.0)
            c = cacc[...]
            a = jnp.where(causal, jnp.exp(lb + _split_dot(lk, later) + c), 0.0)
            oacc[...] += _dot(a.astype(BF16), vj, NN)
            cacc[...] = c + jnp.sum(lk, axis=1, keepdims=True)
            return carry

        lax.fori_loop(0, nkb, step, 0)
        o = oacc[...]
        o_ref[...] = o
        ct_ref[...] = jnp.broadcast_to(cacc[...], (tq, LANES))
        mx_ref[...] = _head_out(o, g_ref[...]).astype(BF16)

    blk = pl.BlockSpec((tq, HEAD_DIM), lambda h, i: (i, h))
    return _pcall(
        body, grid=(H, S // tq),
        in_specs=[blk, pl.BlockSpec((S, HEAD_DIM), lambda h, i: (0, H + h)),
                  pl.BlockSpec((S, HEAD_DIM), lambda h, i: (0, 2 * H + h)), pl.BlockSpec((1, HEAD_DIM), lambda h, i: (0, h))],
        out_specs=[blk, blk, blk],
        out_shape=[jax.ShapeDtypeStruct((S, H * HEAD_DIM), F32), jax.ShapeDtypeStruct((S, H * HEAD_DIM), F32),
                   jax.ShapeDtypeStruct((S, H * HEAD_DIM), BF16)],
        scratch_shapes=[pltpu.VMEM((tq, HEAD_DIM), F32), pltpu.VMEM((tq, 1), F32)],
        compiler_params=_params("parallel", "arbitrary"), name=name)(proj, proj, proj, gain)


def _sb_bwd(proj, gain, o_raw, ctot, dmixed, dm_col0, n_heads, name, tq=512):
    S = proj.shape[0]
    H, tk = n_heads, KEY_BLOCK
    tq = _tile(S, tq, tk)
    nq = S // tq
    scale = HEAD_DIM ** -0.5

    def body(q_ref, k_ref, v_ref, g_ref, o_ref, ct_ref, dm_ref, dq_ref, dk_ref, dv_ref, dg_ref,
             dkacc, dvacc, dqacc, pfx, gcar):
        i = pl.program_id(1)

        @pl.when(i == 0)
        def _():
            dkacc[...] = jnp.zeros_like(dkacc)
            dvacc[...] = jnp.zeros_like(dvacc)
            dg_ref[...] = jnp.zeros_like(dg_ref)

        o, dm, g = o_ref[...], dm_ref[...], g_ref[...]
        r = _rms_scale(o)
        do = _rms_bwd(dm * g, o, r).astype(BF16)
        dg_ref[...] += jnp.broadcast_to(jnp.sum(dm * o * r, axis=0, keepdims=True), dg_ref.shape)
        q = q_ref[...].astype(BF16)
        ct = ct_ref[:, 0:1]
        dqacc[...] = jnp.zeros_like(dqacc)
        pfx[...] = jnp.zeros_like(pfx)
        gcar[...] = jnp.zeros_like(gcar)
        row = i * tq + lax.broadcasted_iota(jnp.int32, (tq, tk), 0)
        col = lax.broadcasted_iota(jnp.int32, (tq, tk), 1)
        ia, ib = lax.broadcasted_iota(jnp.int32, (tk, tk), 0), lax.broadcasted_iota(jnp.int32, (tk, tk), 1)
        later = (ia > ib).astype(BF16)
        earlier = (ia < ib).astype(BF16)
        nkb = (i + 1) * (tq // tk)

        def step(j, carry):
            k0 = pl.multiple_of(j * tk, tk)
            kj = k_ref[pl.ds(k0, tk), :].astype(BF16)
            vj = v_ref[pl.ds(k0, tk), :].astype(BF16)
            z = _dot(q, kj, NT) * scale
            causal = (col + k0) < row
            lb, lk = _logsig_pair(z)
            lk = jnp.where(causal, lk, 0.0)
            rs = jnp.sum(lk, axis=1, keepdims=True)
            p = pfx[...]
            a = jnp.where(causal, jnp.exp(lb + _split_dot(lk, later) + (ct - p - rs)), 0.0)
            dl = _dot(do, vj, NT) * a
            dvacc[pl.ds(k0, tk), :] += _dot(a.astype(BF16), do, TN)
            gc = gcar[...]
            gsum = _split_dot(dl, earlier) + gc
            sig = jnp.exp(lb)
            dz = ((dl * (1.0 - sig) - jnp.where(causal, gsum * sig, 0.0)) * scale).astype(BF16)
            dqacc[...] += _dot(dz, kj, NN)
            dkacc[pl.ds(k0, tk), :] += _dot(dz, q, TN)
            pfx[...] = p + rs
            gcar[...] = gc + jnp.sum(dl, axis=1, keepdims=True)
            return carry

        lax.fori_loop(0, nkb, step, 0)
        dq_ref[...] = dqacc[...].astype(BF16)

        @pl.when(i == nq - 1)
        def _():
            dk_ref[...] = dkacc[...].astype(BF16)
            dv_ref[...] = dvacc[...].astype(BF16)

    blk = pl.BlockSpec((tq, HEAD_DIM), lambda h, i: (i, h))
    full = pl.BlockSpec((S, HEAD_DIM), lambda h, i: (0, h))
    W = H * HEAD_DIM
    return _pcall(
        body, grid=(H, nq),
        in_specs=[blk, pl.BlockSpec((S, HEAD_DIM), lambda h, i: (0, H + h)),
                  pl.BlockSpec((S, HEAD_DIM), lambda h, i: (0, 2 * H + h)), pl.BlockSpec((1, HEAD_DIM), lambda h, i: (0, h)),
                  blk, blk, pl.BlockSpec((tq, HEAD_DIM), lambda h, i: (i, dm_col0 + h))],
        out_specs=[blk, full, full, pl.BlockSpec((8, HEAD_DIM), lambda h, i: (0, h))],
        out_shape=[jax.ShapeDtypeStruct((S, W), BF16), jax.ShapeDtypeStruct((S, W), BF16),
                   jax.ShapeDtypeStruct((S, W), BF16), jax.ShapeDtypeStruct((8, W), F32)],
        scratch_shapes=[pltpu.VMEM((S, HEAD_DIM), F32), pltpu.VMEM((S, HEAD_DIM), F32), pltpu.VMEM((tq, HEAD_DIM), F32),
                        pltpu.VMEM((tq, 1), F32), pltpu.VMEM((tq, 1), F32)],
        compiler_params=_params("arbitrary", "arbitrary"), name=name)(proj, proj, proj, gain, o_raw, ctot, dmixed)


def _rope_tables(S):
    inv_freq = ROPE_THETA ** (-jnp.arange(0, HEAD_DIM, 2, dtype=F32) / HEAD_DIM)
    ang = jnp.arange(S, dtype=F32)[:, None] * inv_freq[None, :]
    cos, sin = jnp.cos(ang), jnp.sin(ang)
    return jnp.concatenate([cos, cos], axis=1), jnp.concatenate([-sin, sin], axis=1)


def _rope(v, cos2, sin_signed):
    return v * cos2 + pltpu.roll(v, HEAD_DIM // 2, axis=1) * sin_signed


def _dil_rows(d, r, l0, n):
    if d == 1:
        return pl.ds(l0 if isinstance(l0, int) else pl.multiple_of(l0, KEY_BLOCK), n)
    return pl.ds(r + d * l0, n, stride=d)


def _dil_blocks(S, visit):
    B = KEY_BLOCK
    for b, d in enumerate(DILATIONS):
        nb = S // d // B

        def per_residue(r, carry, b=b, d=d, nb=nb):
            visit(b, d, r, 0, True)
            if nb > 1:
                def per_block(n, c2):
                    visit(b, d, r, n * B, False)
                    return c2
                lax.fori_loop(1, nb, per_block, 0)
            return carry

        if d == 1:
            per_residue(0, 0)
        else:
            lax.fori_loop(0, d, per_residue, 0)


def _dil_mask(first):
    B = KEY_BLOCK
    nk = B if first else 2 * B
    iq = lax.broadcasted_iota(jnp.int32, (B, nk), 0)
    ik = lax.broadcasted_iota(jnp.int32, (B, nk), 1)
    return (ik <= iq) if first else ((ik >= iq) & (ik <= iq + B))


def _dil_fwd(proj, cos2, sin_signed, gain, col0, n_heads, name):
    S = proj.shape[0]
    H, B = n_heads, KEY_BLOCK
    scale = HEAD_DIM ** -0.5
    rc = _tile(S, 256, 8)

    def body(q_ref, k_ref, v_ref, c_ref, s_ref, g_ref, o_ref, l_ref, mx_ref, qr, kr, *per_branch):
        ob, lb = per_branch[:len(DILATIONS)], per_branch[len(DILATIONS):]

        def rope_rows(t, carry):
            rows = pl.ds(pl.multiple_of(t * rc, rc), rc)
            qr[rows, :] = _rope(q_ref[rows, :], c_ref[rows, :], s_ref[rows, :])
            kr[rows, :] = _rope(k_ref[rows, :], c_ref[rows, :], s_ref[rows, :])
            return carry

        lax.fori_loop(0, S // rc, rope_rows, 0)

        def visit(b, d, r, l0, first):
            nk = B if first else 2 * B
            qrows = _dil_rows(d, r, l0, B)
            krows = qrows if first else _dil_rows(d, r, l0 - B, nk)
            s = _dot(qr[qrows, :].astype(BF16), kr[krows, :].astype(BF16), NT) * scale
            s = jnp.where(_dil_mask(first), s, NEG)
            m = jnp.max(s, axis=1, keepdims=True)
            p = jnp.exp(s - m)
            den = jnp.sum(p, axis=1, keepdims=True)
            ob[b][qrows, :] = _dot(p.astype(BF16), v_ref[krows, :].astype(BF16), NN) / den
            lb[b][qrows, :] = jnp.broadcast_to(m + jnp.log(den), (B, LANES))

        _dil_blocks(S, visit)

        def combine(t, carry):
            rows = pl.ds(pl.multiple_of(t * rc, rc), rc)
            l0, l1, l2 = lb[0][rows, :], lb[1][rows, :], lb[2][rows, :]
            m = jnp.maximum(jnp.maximum(l0, l1), l2)
            w0, w1, w2 = jnp.exp(l0 - m), jnp.exp(l1 - m), jnp.exp(l2 - m)
            den = w0 + w1 + w2
            o = (w0 * ob[0][rows, :] + w1 * ob[1][rows, :] + w2 * ob[2][rows, :]) / den
            o_ref[rows, :] = o
            l_ref[rows, :] = m + jnp.log(den)
            mx_ref[rows, :] = _head_out(o, g_ref[...]).astype(BF16)
            return carry

        lax.fori_loop(0, S // rc, combine, 0)

    def col(k):
        return pl.BlockSpec((S, HEAD_DIM), lambda h: (0, col0 + k * H + h))

    tab = pl.BlockSpec((S, HEAD_DIM), lambda h: (0, 0))
    out = pl.BlockSpec((S, HEAD_DIM), lambda h: (0, h))
    W = H * HEAD_DIM
    return _pcall(
        body, grid=(H,),
        in_specs=[col(0), col(1), col(2), tab, tab, pl.BlockSpec((1, HEAD_DIM), lambda h: (0, h))],
        out_specs=[out, out, out],
        out_shape=[jax.ShapeDtypeStruct((S, W), F32), jax.ShapeDtypeStruct((S, W), F32), jax.ShapeDtypeStruct((S, W), BF16)],
        scratch_shapes=[pltpu.VMEM((S, HEAD_DIM), F32)] * (2 + 2 * len(DILATIONS)),
        compiler_params=_params("parallel"), name=name)(proj, proj, proj, cos2, sin_signed, gain)


def _dil_bwd(proj, cos2, sin_signed, gain, o_raw, lse, dmixed, dm_col0, col0, n_heads, name):
    S = proj.shape[0]
    H, B = n_heads, KEY_BLOCK
    scale = HEAD_DIM ** -0.5
    rc = _tile(S, 256, 8)

    def body(q_ref, k_ref, v_ref, c_ref, s_ref, g_ref, o_ref, l_ref, dm_ref, dq_ref, dk_ref, dv_ref, dg_ref,
             qr, kr, dos, dsum, dqr, dkr, dvv):
        dg_ref[...] = jnp.zeros_like(dg_ref)

        def prep(t, carry):
            rows = pl.ds(pl.multiple_of(t * rc, rc), rc)
            qr[rows, :] = _rope(q_ref[rows, :], c_ref[rows, :], s_ref[rows, :])
            kr[rows, :] = _rope(k_ref[rows, :], c_ref[rows, :], s_ref[rows, :])
            o, dm = o_ref[rows, :], dm_ref[rows, :]
            r = _rms_scale(o)
            do = _rms_bwd(dm * g_ref[...], o, r)
            dg_ref[...] += jnp.broadcast_to(jnp.sum(dm * o * r, axis=0, keepdims=True), dg_ref.shape)
            dos[rows, :] = do
            dsum[rows, :] = jnp.broadcast_to(jnp.sum(do * o, axis=1, keepdims=True), (rc, LANES))
            dqr[rows, :] = jnp.zeros((rc, HEAD_DIM), F32)
            dkr[rows, :] = jnp.zeros((rc, HEAD_DIM), F32)
            dvv[rows, :] = jnp.zeros((rc, HEAD_DIM), F32)
            return carry

        lax.fori_loop(0, S // rc, prep, 0)

        def visit(b, d, r, l0, first):
            nk = B if first else 2 * B
            qrows = _dil_rows(d, r, l0, B)
            krows = qrows if first else _dil_rows(d, r, l0 - B, nk)
            qs, ks = qr[qrows, :].astype(BF16), kr[krows, :].astype(BF16)
            do = dos[qrows, :].astype(BF16)
            s = _dot(qs, ks, NT) * scale
            s = jnp.where(_dil_mask(first), s, NEG)
            p = jnp.exp(s - l_ref[qrows, :][:, 0:1])
            dp = _dot(do, v_ref[krows, :].astype(BF16), NT)
            ds = (p * (dp - dsum[qrows, :][:, 0:1]) * scale).astype(BF16)
            dqr[qrows, :] += _dot(ds, ks, NN)
            dkr[krows, :] += _dot(ds, qs, TN)
            dvv[krows, :] += _dot(p.astype(BF16), do, TN)

        _dil_blocks(S, visit)

        def finish(t, carry):
            rows = pl.ds(pl.multiple_of(t * rc, rc), rc)
            c, s = c_ref[rows, :], s_ref[rows, :]
            dq, dk = dqr[rows, :], dkr[rows, :]
            dq_ref[rows, :] = (dq * c + pltpu.roll(dq * s, HEAD_DIM // 2, axis=1)).astype(BF16)
            dk_ref[rows, :] = (dk * c + pltpu.roll(dk * s, HEAD_DIM // 2, axis=1)).astype(BF16)
            dv_ref[rows, :] = dvv[rows, :].astype(BF16)
            return carry

        lax.fori_loop(0, S // rc, finish, 0)

    def col(k):
        return pl.BlockSpec((S, HEAD_DIM), lambda h: (0, col0 + k * H + h))

    tab = pl.BlockSpec((S, HEAD_DIM), lambda h: (0, 0))
    out = pl.BlockSpec((S, HEAD_DIM), lambda h: (0, h))
    W = H * HEAD_DIM
    big = pltpu.VMEM((S, HEAD_DIM), F32)
    return _pcall(
        body, grid=(H,),
        in_specs=[col(0), col(1), col(2), tab, tab, pl.BlockSpec((1, HEAD_DIM), lambda h: (0, h)), out, out,
                  pl.BlockSpec((S, HEAD_DIM), lambda h: (0, dm_col0 + h))],
        out_specs=[out, out, out, pl.BlockSpec((8, HEAD_DIM), lambda h: (0, h))],
        out_shape=[jax.ShapeDtypeStruct((S, W), BF16), jax.ShapeDtypeStruct((S, W), BF16),
                   jax.ShapeDtypeStruct((S, W), BF16), jax.ShapeDtypeStruct((8, W), F32)],
        scratch_shapes=[big, big, big, pltpu.VMEM((S, LANES), F32), big, big, big],
        compiler_params=_params("parallel"), name=name)(proj, proj, proj, cos2, sin_signed, gain, o_raw, lse, dmixed)


GELU_C = math.sqrt(2.0 / math.pi)
GELU_A = 0.044715
HALO = 8


def _shift_down(cur, halo, k):
    n = cur.shape[0]
    out = pltpu.roll(cur, k, axis=0)
    row = lax.broadcasted_iota(jnp.int32, cur.shape, 0)
    for t in range(k):
        out = jnp.where(row == t, halo[HALO - k + t:HALO - k + t + 1, :], out)
    return out


def _shift_up(cur, halo, k):
    n = cur.shape[0]
    out = pltpu.roll(cur, n - k, axis=0)
    row = lax.broadcasted_iota(jnp.int32, cur.shape, 0)
    for t in range(k):
        out = jnp.where(row == n - k + t, halo[t:t + 1, :], out)
    return out


def _conv3(cur, halo, cw):
    return _shift_down(cur, halo, 2) * cw[0:1, :] + _shift_down(cur, halo, 1) * cw[1:2, :] + cur * cw[2:3, :] + cw[3:4, :]


def _gelu_parts(x):
    t = jnp.tanh(GELU_C * (x + GELU_A * x * x * x))
    return 0.5 * x * (1.0 + t), t


def _geglu_specs(tm, tn, ncb):
    hb = tm // HALO

    def cur(off):
        return pl.BlockSpec((tm, tn), lambda j, i: (i, off + j))

    def prev(off):
        return pl.BlockSpec((HALO, tn), lambda j, i: (jnp.maximum(i * hb - 1, 0), off + j))

    def taps(off):
        return pl.BlockSpec((8, tn), lambda j, i: (0, off + j))

    return [cur(0), prev(0), cur(ncb), prev(ncb), taps(0), taps(ncb)]


def _geglu_fwd(u, cwb, name, tm=256, tn=512):
    S, F2 = u.shape
    F = F2 // 2
    tm, tn = _tile(S, tm, 8), _tile(F, tn)
    ncb = F // tn

    def body(g_ref, gp_ref, v_ref, vp_ref, cg_ref, cv_ref, y_ref):
        top = pl.program_id(1) > 0
        gp = jnp.where(top, gp_ref[...], 0.0)
        vp = jnp.where(top, vp_ref[...], 0.0)
        gc = _conv3(g_ref[...], gp, cg_ref[...])
        vc = _conv3(v_ref[...], vp, cv_ref[...])
        y_ref[...] = (_gelu_parts(gc)[0] * vc).astype(BF16)

    return _pcall(body, grid=(ncb, S // tm), in_specs=_geglu_specs(tm, tn, ncb),
                  out_specs=pl.BlockSpec((tm, tn), lambda j, i: (i, j)),
                  out_shape=jax.ShapeDtypeStruct((S, F), BF16),
                  compiler_params=_params("parallel", "parallel"), name=name)(u, u, u, u, cwb, cwb)


def _geglu_bwd(u, dy, cwb, name, tm=256, tn=512):
    S, F2 = u.shape
    F = F2 // 2
    tm, tn = _tile(S, tm, 8), _tile(F, tn)
    ncb = F // tn

    def body(g_ref, gp_ref, v_ref, vp_ref, cg_ref, cv_ref, dy_ref, dc_ref, dwg_ref, dwv_ref):
        i = pl.program_id(1)

        @pl.when(i == 0)
        def _():
            dwg_ref[...] = jnp.zeros_like(dwg_ref)
            dwv_ref[...] = jnp.zeros_like(dwv_ref)

        top = i > 0
        g, v = g_ref[...], v_ref[...]
        gp = jnp.where(top, gp_ref[...], 0.0)
        vp = jnp.where(top, vp_ref[...], 0.0)
        gc = _conv3(g, gp, cg_ref[...])
        vc = _conv3(v, vp, cv_ref[...])
        act, t = _gelu_parts(gc)
        dact = 0.5 * (1.0 + t) + 0.5 * gc * (1.0 - t * t) * GELU_C * (1.0 + 3.0 * GELU_A * gc * gc)
        dyv = dy_ref[...].astype(F32)
        dgc = dyv * vc * dact
        dvc = dyv * act
        dc_ref[0] = dgc.astype(BF16)
        dc_ref[1] = dvc.astype(BF16)

        def taps(out_ref, dc, cur, halo):
            out_ref[0:1, :] += jnp.sum(dc * _shift_down(cur, halo, 2), axis=0, keepdims=True)
            out_ref[1:2, :] += jnp.sum(dc * _shift_down(cur, halo, 1), axis=0, keepdims=True)
            out_ref[2:3, :] += jnp.sum(dc * cur, axis=0, keepdims=True)
            out_ref[3:4, :] += jnp.sum(dc, axis=0, keepdims=True)

        taps(dwg_ref, dgc, g, gp)
        taps(dwv_ref, dvc, v, vp)

    return _pcall(body, grid=(ncb, S // tm),
                  in_specs=_geglu_specs(tm, tn, ncb) + [pl.BlockSpec((tm, tn), lambda j, i: (i, j))],
                  out_specs=[pl.BlockSpec((2, tm, tn), lambda j, i: (0, i, j)),
                             pl.BlockSpec((8, tn), lambda j, i: (0, j)), pl.BlockSpec((8, tn), lambda j, i: (0, j))],
                  out_shape=[jax.ShapeDtypeStruct((2, S, F), BF16), jax.ShapeDtypeStruct((8, F), F32),
                             jax.ShapeDtypeStruct((8, F), F32)],
                  compiler_params=_params("parallel", "arbitrary"), name=name)(u, u, u, u, cwb, cwb, dy)


CONV_BWD_HALO = 16


def _conv_bwd(dc, cwb, name, tm=256, tn=512):
    _, S, F = dc.shape
    tm, tn = _tile(S, tm, CONV_BWD_HALO), _tile(F, tn)
    ncb, nrb = F // tn, S // tm
    hb = tm // CONV_BWD_HALO

    def body(c_ref, n_ref, w_ref, du_ref):
        cur = c_ref[...].astype(F32)
        nxt = jnp.where(pl.program_id(2) < nrb - 1, n_ref[...].astype(F32), 0.0)
        w = w_ref[...]
        du = cur * w[2:3, :] + _shift_up(cur, nxt, 1) * w[1:2, :] + _shift_up(cur, nxt, 2) * w[0:1, :]
        du_ref[...] = du.astype(BF16)

    return _pcall(body, grid=(2, ncb, nrb),
                  in_specs=[pl.BlockSpec((None, tm, tn), lambda c, j, i: (c, i, j)),
                            pl.BlockSpec((None, CONV_BWD_HALO, tn), lambda c, j, i: (c, jnp.minimum((i + 1) * hb, S // CONV_BWD_HALO - 1), j)),
                            pl.BlockSpec((8, tn), lambda c, j, i: (0, c * ncb + j))],
                  out_specs=pl.BlockSpec((tm, tn), lambda c, j, i: (i, c * ncb + j)),
                  out_shape=jax.ShapeDtypeStruct((S, 2 * F), BF16),
                  compiler_params=_params("parallel", "parallel", "parallel"), name=name)(dc, dc, cwb)


def _adam_math(w, g, m, v):
    m = ADAM_B1 * m + (1.0 - ADAM_B1) * g
    v = ADAM_B2 * v + (1.0 - ADAM_B2) * (g * g)
    m_hat = m / (1.0 - ADAM_B1 ** ADAM_STEP)
    v_hat = v / (1.0 - ADAM_B2 ** ADAM_STEP)
    return -ADAM_LR * (m_hat / (jnp.sqrt(v_hat) + ADAM_EPS) + ADAM_WD * w), m, v


def _adamw(w, parts, m, v, name, tr=256):
    R, C = w.shape
    n, _, Cp = parts.shape
    tr = _tile(R, tr, 8)

    def body(w_ref, p_ref, m_ref, v_ref, g_out, d_out, m_out, v_out):
        g = p_ref[0, :, 0:C].astype(F32)
        for k in range(1, n):
            g = g + p_ref[k, :, 0:C].astype(F32)
        d, mn, vn = _adam_math(w_ref[...], g, m_ref[...], v_ref[...])
        g_out[...] = g
        d_out[...] = d
        m_out[...] = mn
        v_out[...] = vn

    spec = pl.BlockSpec((tr, C), lambda i: (i, 0))
    shape = jax.ShapeDtypeStruct((R, C), F32)
    return _pcall(body, grid=(R // tr,), in_specs=[spec, pl.BlockSpec((n, tr, Cp), lambda i: (0, i, 0)), spec, spec],
                  out_specs=[spec] * 4, out_shape=[shape] * 4, compiler_params=_params("parallel"), name=name)(w, parts, m, v)


def _place():
    return lax.axis_index("x"), lax.axis_index("y"), lax.axis_index("c")


def _other_chips(x, y):
    return [(1 - x, y), (x, 1 - y), (1 - x, 1 - y)]


def _gather_weights(shards, slots, out_shapes, zero_src, zero_views, name):
    na = len(shards)
    n_local = na + len(zero_views)

    def body(*refs):
        srcs, zsrc, dsts = refs[:na], refs[na], refs[na + 1:2 * na + 1]
        send_sems, recv_sems, local_sems = refs[2 * na + 1:]
        x, y, c = _place()
        me, sibling = (x, y, c), (x, y, 1 - c)
        chips = _other_chips(x, y)

        def copy(a, k, block, to, src=None):
            view = slots[a](dsts[a], *block)
            return pltpu.make_async_remote_copy(
                src_ref=view if src is None else src, dst_ref=view,
                send_sem=send_sems.at[a * 7 + k], recv_sem=recv_sems.at[a * 7 + k], device_id=to, device_id_type=MESH)

        local = [pltpu.make_async_copy(srcs[a], slots[a](dsts[a], *me), local_sems.at[a]) for a in range(na)]
        local += [pltpu.make_async_copy(zsrc, view(dsts[a]), local_sems.at[na + z]) for z, (a, view) in enumerate(zero_views)]
        for cp in local:
            cp.start()
        first = []
        for a in range(na):
            first.append(copy(a, 0, me, sibling, src=srcs[a]))
            first += [copy(a, 1 + j, me, (*chip, c), src=srcs[a]) for j, chip in enumerate(chips)]
        for cp in first:
            cp.start()
        passed = []
        for j, chip in enumerate(chips):
            for a in range(na):
                copy(a, 1 + j, (*chip, c), me).wait_recv()
                passed.append(copy(a, 4 + j, (*chip, c), sibling))
                passed[-1].start()
        for a in range(na):
            copy(a, 0, sibling, me).wait_recv()
            for j, chip in enumerate(chips):
                copy(a, 4 + j, (*chip, 1 - c), me).wait_recv()
        for cp in first + passed:
            cp.wait_send()
        for cp in local:
            cp.wait()

    return _pcall(
        body, in_specs=[HBM] * (na + 1), out_specs=[HBM] * na, out_shape=out_shapes,
        scratch_shapes=[pltpu.SemaphoreType.DMA((7 * na,)), pltpu.SemaphoreType.DMA((7 * na,)), pltpu.SemaphoreType.DMA((n_local,))],
        name=name)(*shards, zero_src)


def _pair_exchange(grads, views, recv_shapes, name):
    na = len(grads)

    def body(*refs):
        srcs, dsts = refs[:na], refs[na:2 * na]
        send_sems, recv_sems = refs[2 * na:]
        x, y, c = _place()
        copies = []
        for a in range(na):
            for chip in range(N_CHIP):
                copies.append(pltpu.make_async_remote_copy(
                    src_ref=views[a](srcs[a], chip, 1 - c), dst_ref=dsts[a].at[chip],
                    send_sem=send_sems.at[a * N_CHIP + chip], recv_sem=recv_sems.at[a * N_CHIP + chip],
                    device_id=(x, y, 1 - c), device_id_type=MESH))
        for cp in copies:
            cp.start()
        for cp in copies:
            cp.wait()

    return _pcall(body, in_specs=[HBM] * na, out_specs=[HBM] * na, out_shape=recv_shapes,
                  scratch_shapes=[pltpu.SemaphoreType.DMA((N_CHIP * na,)), pltpu.SemaphoreType.DMA((N_CHIP * na,))],
                  name=name)(*grads)


def _chip_exchange(pairs, name):
    na = len(pairs)

    def body(*refs):
        srcs, dsts = refs[:na], refs[na:2 * na]
        send_sems, recv_sems, local_sems = refs[2 * na:]
        x, y, c = _place()
        mine = 2 * x + y
        local, remote = [], []
        for a in range(na):
            local.append(pltpu.make_async_copy(srcs[a].at[mine], dsts[a].at[mine], local_sems.at[a]))
            for j, (px, py) in enumerate(_other_chips(x, y)):
                remote.append(pltpu.make_async_remote_copy(
                    src_ref=srcs[a].at[2 * px + py], dst_ref=dsts[a].at[mine],
                    send_sem=send_sems.at[a * 3 + j], recv_sem=recv_sems.at[a * 3 + j],
                    device_id=(px, py, c), device_id_type=MESH))
        for cp in local + remote:
            cp.start()
        for a in range(na):
            for j, (px, py) in enumerate(_other_chips(x, y)):
                pltpu.make_async_remote_copy(
                    src_ref=srcs[a].at[mine], dst_ref=dsts[a].at[2 * px + py],
                    send_sem=send_sems.at[a * 3 + j], recv_sem=recv_sems.at[a * 3 + j],
                    device_id=(px, py, c), device_id_type=MESH).wait_recv()
        for cp in remote:
            cp.wait_send()
        for cp in local:
            cp.wait()

    return _pcall(body, in_specs=[HBM] * na, out_specs=[HBM] * na,
                  out_shape=[jax.ShapeDtypeStruct(p.shape, p.dtype) for p in pairs],
                  scratch_shapes=[pltpu.SemaphoreType.DMA((3 * na,)), pltpu.SemaphoreType.DMA((3 * na,)),
                                  pltpu.SemaphoreType.DMA((na,))],
                  name=name)(*pairs)


def _pair_add(core, grad, recv, block, grad_spec, name):
    _, R, C = recv.shape
    tr = block

    def body(c_ref, g_ref, r_ref, o_ref):
        o_ref[...] = (g_ref[...].astype(F32) + r_ref[...].astype(F32)).astype(BF16)

    grid_spec = pltpu.PrefetchScalarGridSpec(
        num_scalar_prefetch=1, grid=(N_CHIP, R // tr),
        in_specs=[grad_spec, pl.BlockSpec((None, tr, C), lambda k, i, c: (k, i, 0))],
        out_specs=pl.BlockSpec((None, tr, C), lambda k, i, c: (k, i, 0)))
    return _pcall(body, grid_spec=grid_spec, out_shape=jax.ShapeDtypeStruct(recv.shape, BF16),
                  compiler_params=_params("parallel", "parallel"), name=name)(core, grad, recv)


def _all_reduce_small(part, name):
    R = part.shape[0]

    def body(p_ref, o_ref, all_ref, send_sems, recv_sems):
        x, y, c = _place()
        me = 4 * x + 2 * y + c
        all_ref[me] = p_ref[...]
        peers = [(x, y, 1 - c)] + [(px, py, pc) for px, py in _other_chips(x, y) for pc in (c, 1 - c)]
        copies = [pltpu.make_async_remote_copy(
            src_ref=p_ref, dst_ref=all_ref.at[me], send_sem=send_sems.at[k], recv_sem=recv_sems.at[k],
            device_id=peer, device_id_type=MESH) for k, peer in enumerate(peers)]
        for cp in copies:
            cp.start()
        for k, (px, py, pc) in enumerate(peers):
            pltpu.make_async_remote_copy(
                src_ref=p_ref, dst_ref=all_ref.at[4 * px + 2 * py + pc], send_sem=send_sems.at[k], recv_sem=recv_sems.at[k],
                device_id=peers[k], device_id_type=MESH).wait_recv()
        for cp in copies:
            cp.wait_send()
        acc = all_ref[0]
        for k in range(1, N_DEV):
            acc = acc + all_ref[k]
        o_ref[...] = acc

    vm = pl.BlockSpec(memory_space=pltpu.VMEM)
    return _pcall(body, in_specs=[vm], out_specs=vm, out_shape=jax.ShapeDtypeStruct((R, LANES), F32),
                  scratch_shapes=[pltpu.VMEM((N_DEV, R, LANES), F32), pltpu.SemaphoreType.DMA((7,)), pltpu.SemaphoreType.DMA((7,))],
                  name=name)(part)


def _local_step(x, tgt, gains, w_in_g, w_out_g, w_up_g, w_down_g, cwb):
    g_pre_mix, g_post_mix, g_pre_ffn, g_post_ffn, g_sb, g_dil = gains
    S, D = x.shape
    hs = g_sb.shape[1] // HEAD_DIM
    hd = g_dil.shape[1] // HEAD_DIM
    cos2, sin_signed = _rope_tables(S)

    h1 = _rms_fwd(x, g_pre_mix, "rms_in")
    proj = _mm_nn(h1, w_in_g, F32, "proj", tn=768)
    o_sb, ct_sb, mx_sb = _sb_fwd(proj, g_sb, hs, "sb_fwd")
    o_dl, lse_dl, mx_dl = _dil_fwd(proj, cos2, sin_signed, g_dil, 3 * hs, hd, "dil_fwd")
    mixed = jnp.concatenate([mx_sb, mx_dl], axis=1)
    mix = _mm_nn(mixed, w_out_g, F32, "mix_out", tn=1024)
    x2, h2 = _mid_fwd(x, mix, g_post_mix, g_pre_ffn, "mid_fwd")
    u = _mm_nn(h2, w_up_g, F32, "ffn_up")
    y = _geglu_fwd(u, cwb, "geglu_fwd")
    f = _mm_nn(y, w_down_g, F32, "ffn_down", tn=1024)

    dy, df, dg_post_ffn, loss = _loss_bwd(x2, f, tgt, g_post_ffn, "loss_bwd")
    dyv = _mm_nt(df, w_down_g, BF16, "d_y", tk=1024)
    dw_down = _mm_tn(y, df, D, BF16, "dw_down", tn=1024)
    dc, dcw_g, dcw_v = _geglu_bwd(u, dyv, cwb, "geglu_bwd")
    du = _conv_bwd(dc, cwb, "conv_bwd")
    dh2 = _mm_nt(du, w_up_g, F32, "d_h2")
    dw_up = _mm_tn(h2, du, w_up_g.shape[2], BF16, "dw_up")
    dx2, dmix, dg_pre_ffn, dg_post_mix = _mid_bwd(dy, dh2, x2, mix, g_pre_ffn, g_post_mix, "mid_bwd")
    dmixed = _mm_nt(dmix, w_out_g, F32, "d_mixed", tk=1024)
    dw_out = _mm_tn(mixed, dmix, D, BF16, "dw_out", tn=1024)
    dq_s, dk_s, dv_s, dg_sb = _sb_bwd(proj, g_sb, o_sb, ct_sb, dmixed, 0, hs, "sb_bwd")
    dq_d, dk_d, dv_d, dg_dil = _dil_bwd(proj, cos2, sin_signed, g_dil, o_dl, lse_dl, dmixed, hs, 3 * hs, hd, "dil_bwd")
    dproj = jnp.concatenate([dq_s, dk_s, dv_s, dq_d, dk_d, dv_d], axis=1)
    dh1 = _mm_nt(dproj, w_in_g, F32, "d_h1", tk=768)
    dw_in = _mm_tn(h1, dproj, w_in_g.shape[2], BF16, "dw_in", tn=768)
    grad_x, dg_pre_mix = _first_bwd(dx2, dh1, x, g_pre_mix, "first_bwd")
    small = (dg_pre_mix, dg_post_mix, dg_pre_ffn, dg_post_ffn, dg_sb[0:1], dg_dil[0:1], jnp.concatenate([dcw_g, dcw_v], axis=1))
    return loss, grad_x, small, (dw_in, dw_out, dw_up, dw_down)


def _pad_cols(a, to):
    return jnp.pad(a, ((0, 0), (0, to - a.shape[1])))


def kernel(x, pre_mix_gain, post_mix_gain, pre_ffn_gain, post_ffn_gain, w_in, sb_out_gain, dil_out_gain, w_out, w_up, conv_w, conv_b, w_down, loss_target, m_pre_mix_gain, m_post_mix_gain, m_pre_ffn_gain, m_post_ffn_gain, m_w_in, m_sb_out_gain, m_dil_out_gain, m_w_out, m_w_up, m_conv_w, m_conv_b, m_w_down, v_pre_mix_gain, v_post_mix_gain, v_pre_ffn_gain, v_post_ffn_gain, v_w_in, v_sb_out_gain, v_dil_out_gain, v_w_out, v_w_up, v_conv_w, v_conv_b, v_w_down):
    xb, tb = x[0], loss_target[0]
    S, D = xb.shape
    w_in, w_out, w_up, w_down, conv_w = w_in[0], w_out[0], w_up[0], w_down[0], conv_w[0]
    n_in, e_rows = w_in.shape[1], w_out.shape[0]
    cu, half = w_up.shape[1], w_down.shape[0]
    assert cu == 2 * half and half % 16 == 0
    cup = -(-cu // LANES) * LANES
    fp = N_CHIP * cup
    px, py, pc = _place()
    me = 4 * px + 2 * py + pc
    core = jnp.reshape(pc, (1,)).astype(jnp.int32)

    shards = [w_in.astype(BF16), w_out.astype(BF16), _pad_cols(w_up, cup).astype(BF16), w_down.astype(BF16),
              jnp.pad(_pad_cols(conv_w, cup), ((0, 8 - conv_w.shape[0]), (0, 0)))]

    def by_dev(ref, qx, qy, qc):
        return ref.at[4 * qx + 2 * qy + qc]

    def down_slot(ref, qx, qy, qc):
        return ref.at[2 * qx + qy, pl.ds(qc * half, half)]

    gathered = _gather_weights(
        shards, [by_dev, by_dev, by_dev, down_slot, by_dev],
        [jax.ShapeDtypeStruct((N_DEV, D, n_in), BF16), jax.ShapeDtypeStruct((N_DEV, e_rows, D), BF16),
         jax.ShapeDtypeStruct((N_DEV, D, cup), BF16), jax.ShapeDtypeStruct((N_CHIP, cup, D), BF16),
         jax.ShapeDtypeStruct((N_DEV, 8, cup), F32)],
        jnp.zeros((cup - cu, D), BF16),
        [(3, functools.partial(lambda ref, k: ref.at[k, pl.ds(cu, cup - cu)], k=k)) for k in range(N_CHIP)],
        "gather_weights")
    w_in_g, w_out_g, w_up_g, w_down_g, cw_g = gathered
    cb = _pad_cols(conv_b.reshape(N_DEV, cu), cup).reshape(1, 2 * fp)
    cw_full = jnp.transpose(cw_g[:, :3, :], (1, 0, 2)).reshape(3, 2 * fp)
    cwb = jnp.concatenate([cw_full, cb, jnp.zeros((4, 2 * fp), F32)], axis=0)

    gains = (pre_mix_gain, post_mix_gain, pre_ffn_gain, post_ffn_gain, sb_out_gain, dil_out_gain)
    loss, grad_x, small, (dw_in, dw_out, dw_up, dw_down) = _local_step(
        xb, tb, gains, w_in_g, w_out_g.reshape(1, N_DEV * e_rows, D), w_up_g, w_down_g.reshape(1, fp, D), cwb)

    flat = jnp.concatenate([s.reshape(-1) for s in small] + [loss.reshape(-1)])
    sizes = [s.size for s in small]
    rows = -(-flat.size // LANES)
    rows = -(-rows // 8) * 8
    packed = jnp.pad(flat, (0, rows * LANES - flat.size)).reshape(rows, LANES)
    total = _all_reduce_small(packed, "reduce_small").reshape(-1)
    offs = [0]
    for s in sizes:
        offs.append(offs[-1] + s)
    red = [total[offs[k]:offs[k + 1]].reshape(small[k].shape) for k in range(len(small))]
    loss_out = total[offs[-1]]
    g_pre_mix, g_post_mix, g_pre_ffn, g_post_ffn, g_sb, g_dil, g_conv = red
    g_conv_b = g_conv[3].reshape(N_DEV, cup)[:, :cu].reshape(1, N_DEV * cu)
    g_conv_w = lax.dynamic_index_in_dim(g_conv[0:3].reshape(3, N_DEV, cup), me, axis=1, keepdims=False)[:, :cu]

    dw_in = dw_in.reshape(N_CHIP, 2, D, n_in)
    dw_out = dw_out.reshape(N_CHIP, 2, e_rows, D)
    dw_up = dw_up.reshape(N_CHIP, 2, D, cup)
    dw_down = dw_down.reshape(N_CHIP, cup, D)

    def by_pair(ref, chip, k):
        return ref.at[chip, k]

    def down_pair(ref, chip, k):
        return ref.at[chip, pl.ds(k * half, half)]

    grads = [dw_in, dw_out, dw_up, dw_down]
    recv_shapes = [jax.ShapeDtypeStruct((N_CHIP, D, n_in), BF16), jax.ShapeDtypeStruct((N_CHIP, e_rows, D), BF16),
                   jax.ShapeDtypeStruct((N_CHIP, D, cup), BF16), jax.ShapeDtypeStruct((N_CHIP, half, D), BF16)]
    recv = _pair_exchange(grads, [by_pair, by_pair, by_pair, down_pair], recv_shapes, "pair_exchange")

    def pair_spec(tr, cols):
        return pl.BlockSpec((None, None, tr, cols), lambda k, i, c: (k, c[0], i, 0))

    tr_in, tr_up = _tile(D, 512, 16), _tile(D, 256, 16)
    pairs = [
        _pair_add(core, dw_in, recv[0], tr_in, pair_spec(tr_in, n_in), "pair_add_in"),
        _pair_add(core, dw_out, recv[1], e_rows, pair_spec(e_rows, D), "pair_add_out"),
        _pair_add(core, dw_up, recv[2], tr_up, pair_spec(tr_up, cup), "pair_add_up"),
        _pair_add(core, dw_down, recv[3], half, pl.BlockSpec((None, half, D), lambda k, i, c: (k, c[0], 0)), "pair_add_down"),
    ]
    parts = _chip_exchange(pairs, "chip_exchange")

    def small_adam(w, g, m, v, name):
        one = w.shape[0] == 1
        if one:
            w, g, m, v = (jnp.broadcast_to(t, (8, t.shape[1])) for t in (w, g, m, v))
        outs = _adamw(w, g[None], m, v, name)
        return [o[0:1] for o in outs] if one else outs

    out_w_in = _adamw(w_in, parts[0], m_w_in[0], v_w_in[0], "adam_w_in")
    out_w_out = _adamw(w_out, parts[1], m_w_out[0], v_w_out[0], "adam_w_out")
    out_w_up = _adamw(w_up, parts[2], m_w_up[0], v_w_up[0], "adam_w_up")
    out_w_down = _adamw(w_down, parts[3], m_w_down[0], v_w_down[0], "adam_w_down")
    out_pre_mix = small_adam(pre_mix_gain, g_pre_mix, m_pre_mix_gain, v_pre_mix_gain, "adam_pre_mix")
    out_post_mix = small_adam(post_mix_gain, g_post_mix, m_post_mix_gain, v_post_mix_gain, "adam_post_mix")
    out_pre_ffn = small_adam(pre_ffn_gain, g_pre_ffn, m_pre_ffn_gain, v_pre_ffn_gain, "adam_pre_ffn")
    out_post_ffn = small_adam(post_ffn_gain, g_post_ffn, m_post_ffn_gain, v_post_ffn_gain, "adam_post_ffn")
    out_sb = small_adam(sb_out_gain, g_sb, m_sb_out_gain, v_sb_out_gain, "adam_sb_gain")
    out_dil = small_adam(dil_out_gain, g_dil, m_dil_out_gain, v_dil_out_gain, "adam_dil_gain")
    out_conv_b = small_adam(conv_b, g_conv_b, m_conv_b, v_conv_b, "adam_conv_b")
    cw8 = [jnp.pad(t, ((0, 5), (0, 0))) for t in (conv_w, g_conv_w, m_conv_w[0], v_conv_w[0])]
    out_conv_w = [o[0:3] for o in _adamw(cw8[0], cw8[1][None], cw8[2], cw8[3], "adam_conv_w")]

    order = [out_pre_mix, out_post_mix, out_pre_ffn, out_post_ffn, [o[None] for o in out_w_in], out_sb, out_dil,
             [o[None] for o in out_w_out], [o[None] for o in out_w_up], [o[None] for o in out_conv_w], out_conv_b,
             [o[None] for o in out_w_down]]
    outs = [loss_out, grad_x[None]]
    for k in range(4):
        outs += [o[k] for o in order]
    return tuple(outs)
```

```python
import functools
import math

import jax
import jax.numpy as jnp
from jax import lax
from jax.experimental import pallas as pl
from jax.experimental.pallas import tpu as pltpu

F32 = jnp.float32
BF16 = jnp.bfloat16
HEAD_DIM = 128
LANES = 128
KEY_BLOCK = 128
DILATIONS = (1, 4, 16)
RMS_EPS = 1e-6
ROPE_THETA = 10000.0
NEG = -1e30
ADAM_LR, ADAM_B1, ADAM_B2, ADAM_EPS, ADAM_WD, ADAM_STEP = 0.001, 0.9, 0.999, 1e-08, 0.01, 10
MESH = pl.DeviceIdType.MESH
N_DEV = 8
N_CHIP = 4
HBM = pl.BlockSpec(memory_space=pl.ANY)
VMEM_LIMIT = 56 * 1024 * 1024

_pcall = pl.pallas_call


def _tile(n, pref, mult=LANES):
    best = None
    t = mult
    while t <= min(n, pref):
        if n % t == 0:
            best = t
        t += mult
    return n if best is None else best


def _params(*sem):
    return pltpu.CompilerParams(dimension_semantics=sem, vmem_limit_bytes=VMEM_LIMIT)


def _dot(a, b, dims):
    return lax.dot_general(a, b, (dims, ((), ())), preferred_element_type=F32)


NN = ((1,), (0,))
NT = ((1,), (1,))
TN = ((0,), (0,))


def _mm_body(dims, nk, tile):
    if nk == 1:
        def single(a_ref, b_ref, o_ref):
            o_ref[...] = _dot(a_ref[...].astype(BF16), b_ref[...].astype(BF16), dims).astype(o_ref.dtype)

        return single, []

    def body(a_ref, b_ref, o_ref, acc_ref):
        k = pl.program_id(2)

        @pl.when(k == 0)
        def _():
            acc_ref[...] = jnp.zeros_like(acc_ref)

        acc_ref[...] += _dot(a_ref[...].astype(BF16), b_ref[...].astype(BF16), dims)

        @pl.when(k == nk - 1)
        def _():
            o_ref[...] = acc_ref[...].astype(o_ref.dtype)

    return body, [pltpu.VMEM(tile, F32)]


def _mm_nn(a, b3, out_dtype, name, tm=1024, tn=1408, tk=2048):
    M, K = a.shape
    C, _, n = b3.shape
    tm, tk, tn = _tile(M, tm, 8), _tile(K, tk), _tile(n, tn)
    npc, nk = n // tn, K // tk
    body, scratch = _mm_body(NN, nk, (tm, tn))
    return _pcall(
        body, grid=(M // tm, C * npc, nk),
        in_specs=[pl.BlockSpec((tm, tk), lambda i, j, k: (i, k)),
                  pl.BlockSpec((None, tk, tn), lambda i, j, k: (j // npc, k, j % npc))],
        out_specs=pl.BlockSpec((tm, tn), lambda i, j, k: (i, j)),
        out_shape=jax.ShapeDtypeStruct((M, C * n), out_dtype), scratch_shapes=scratch,
        compiler_params=_params("parallel", "parallel", "arbitrary"), name=name)(a, b3)


def _mm_nt(a, b3, out_dtype, name, tm=1024, tn=1024, tk=2048):
    M, _ = a.shape
    C, N, n = b3.shape
    tm, tn, tk = _tile(M, tm, 8), _tile(N, tn), _tile(n, tk)
    kpc = n // tk
    nk = C * kpc
    body, scratch = _mm_body(NT, nk, (tm, tn))
    return _pcall(
        body, grid=(M // tm, N // tn, nk),
        in_specs=[pl.BlockSpec((tm, tk), lambda i, j, k: (i, k)),
                  pl.BlockSpec((None, tn, tk), lambda i, j, k: (k // kpc, j, k % kpc))],
        out_specs=pl.BlockSpec((tm, tn), lambda i, j, k: (i, j)),
        out_shape=jax.ShapeDtypeStruct((M, N), out_dtype), scratch_shapes=scratch,
        compiler_params=_params("parallel", "parallel", "arbitrary"), name=name)(a, b3)


def _mm_tn(x, y, n, out_dtype, name, tm=1024, tn=1408, tk=2048):
    S, P = x.shape
    C = y.shape[1] // n
    tm, tn, tk = _tile(P, tm), _tile(n, tn), _tile(S, tk, 8)
    npc, nk = n // tn, S // tk
    body, scratch = _mm_body(TN, nk, (tm, tn))
    return _pcall(
        body, grid=(P // tm, C * npc, nk),
        in_specs=[pl.BlockSpec((tk, tm), lambda i, j, k: (k, i)),
                  pl.BlockSpec((tk, tn), lambda i, j, k: (k, j))],
        out_specs=pl.BlockSpec((None, tm, tn), lambda i, j, k: (j // npc, i, j % npc)),
        out_shape=jax.ShapeDtypeStruct((C, P, n), out_dtype), scratch_shapes=scratch,
        compiler_params=_params("parallel", "parallel", "arbitrary"), name=name)(x, y)


def _rms_scale(v):
    return lax.rsqrt(jnp.mean(v * v, axis=-1, keepdims=True) + RMS_EPS)


def _rms_bwd(gy, v, r):
    return r * gy - v * (r * r * r * jnp.mean(gy * v, axis=-1, keepdims=True))


def _rows_spec(tm, d):
    return pl.BlockSpec((tm, d), lambda i: (i, 0))


def _vec_spec(d):
    return pl.BlockSpec((1, d), lambda i: (0, 0))


def _rms_fwd(x, g, name, tm=256):
    S, D = x.shape

    def body(x_ref, g_ref, h_ref):
        v = x_ref[...]
        h_ref[...] = (v * _rms_scale(v) * g_ref[...]).astype(BF16)

    return _pcall(body, grid=(S // tm,), in_specs=[_rows_spec(tm, D), _vec_spec(D)], out_specs=_rows_spec(tm, D),
                  out_shape=jax.ShapeDtypeStruct((S, D), BF16), compiler_params=_params("parallel"), name=name)(x, g)


def _mid_fwd(x, mix, g_post, g_pre, name, tm=256):
    S, D = x.shape

    def body(x_ref, m_ref, gp_ref, gn_ref, x2_ref, h_ref):
        m = m_ref[...]
        x2 = x_ref[...] + m * _rms_scale(m) * gp_ref[...]
        x2_ref[...] = x2
        h_ref[...] = (x2 * _rms_scale(x2) * gn_ref[...]).astype(BF16)

    return _pcall(body, grid=(S // tm,), in_specs=[_rows_spec(tm, D), _rows_spec(tm, D), _vec_spec(D), _vec_spec(D)],
                  out_specs=[_rows_spec(tm, D), _rows_spec(tm, D)],
                  out_shape=[jax.ShapeDtypeStruct((S, D), F32), jax.ShapeDtypeStruct((S, D), BF16)],
                  compiler_params=_params("parallel"), name=name)(x, mix, g_post, g_pre)


def _loss_bwd(x2, f, tgt, g_post, name, tm=256):
    S, D = x2.shape

    def body(x2_ref, f_ref, t_ref, g_ref, dy_ref, df_ref, dg_ref, ls_ref):
        i = pl.program_id(0)

        @pl.when(i == 0)
        def _():
            dg_ref[...] = jnp.zeros_like(dg_ref)
            ls_ref[...] = jnp.zeros_like(ls_ref)

        fv = f_ref[...]
        r = _rms_scale(fv)
        g = g_ref[...]
        err = x2_ref[...] + fv * r * g - t_ref[...]
        ls_ref[...] += jnp.broadcast_to(0.5 * jnp.sum(jnp.mean(err * err, axis=-1, keepdims=True), axis=0, keepdims=True), ls_ref.shape)
        dy = err * (1.0 / D)
        dy_ref[...] = dy
        df_ref[...] = _rms_bwd(dy * g, fv, r).astype(BF16)
        dg_ref[...] += jnp.sum(dy * fv * r, axis=0, keepdims=True)

    return _pcall(body, grid=(S // tm,),
                  in_specs=[_rows_spec(tm, D), _rows_spec(tm, D), _rows_spec(tm, D), _vec_spec(D)],
                  out_specs=[_rows_spec(tm, D), _rows_spec(tm, D), _vec_spec(D), _vec_spec(LANES)],
                  out_shape=[jax.ShapeDtypeStruct((S, D), F32), jax.ShapeDtypeStruct((S, D), BF16),
                             jax.ShapeDtypeStruct((1, D), F32), jax.ShapeDtypeStruct((1, LANES), F32)],
                  compiler_params=_params("arbitrary"), name=name)(x2, f, tgt, g_post)


def _mid_bwd(dy, dh2, x2, mix, g_pre, g_post, name, tm=256):
    S, D = dy.shape

    def body(dy_ref, dh_ref, x2_ref, m_ref, gn_ref, gp_ref, dx2_ref, dm_ref, dgn_ref, dgp_ref):
        i = pl.program_id(0)

        @pl.when(i == 0)
        def _():
            dgn_ref[...] = jnp.zeros_like(dgn_ref)
            dgp_ref[...] = jnp.zeros_like(dgp_ref)

        x2, dh = x2_ref[...], dh_ref[...]
        r = _rms_scale(x2)
        dx2 = dy_ref[...] + _rms_bwd(dh * gn_ref[...], x2, r)
        dgn_ref[...] += jnp.sum(dh * x2 * r, axis=0, keepdims=True)
        dx2_ref[...] = dx2
        m = m_ref[...]
        rm = _rms_scale(m)
        dm_ref[...] = _rms_bwd(dx2 * gp_ref[...], m, rm).astype(BF16)
        dgp_ref[...] += jnp.sum(dx2 * m * rm, axis=0, keepdims=True)

    return _pcall(body, grid=(S // tm,),
                  in_specs=[_rows_spec(tm, D)] * 4 + [_vec_spec(D)] * 2,
                  out_specs=[_rows_spec(tm, D), _rows_spec(tm, D), _vec_spec(D), _vec_spec(D)],
                  out_shape=[jax.ShapeDtypeStruct((S, D), F32), jax.ShapeDtypeStruct((S, D), BF16),
                             jax.ShapeDtypeStruct((1, D), F32), jax.ShapeDtypeStruct((1, D), F32)],
                  compiler_params=_params("arbitrary"), name=name)(dy, dh2, x2, mix, g_pre, g_post)


def _first_bwd(dx2, dh1, x, g_pre, name, tm=256):
    S, D = x.shape

    def body(dx2_ref, dh_ref, x_ref, g_ref, gx_ref, dg_ref):
        i = pl.program_id(0)

        @pl.when(i == 0)
        def _():
            dg_ref[...] = jnp.zeros_like(dg_ref)

        xv, dh = x_ref[...], dh_ref[...]
        r = _rms_scale(xv)
        gx_ref[...] = dx2_ref[...] + _rms_bwd(dh * g_ref[...], xv, r)
        dg_ref[...] += jnp.sum(dh * xv * r, axis=0, keepdims=True)

    return _pcall(body, grid=(S // tm,), in_specs=[_rows_spec(tm, D)] * 3 + [_vec_spec(D)],
                  out_specs=[_rows_spec(tm, D), _vec_spec(D)],
                  out_shape=[jax.ShapeDtypeStruct((S, D), F32), jax.ShapeDtypeStruct((1, D), F32)],
                  compiler_params=_params("arbitrary"), name=name)(dx2, dh1, x, g_pre)


def _logsig_pair(z):
    sp = jnp.log(1.0 + jnp.exp(-jnp.abs(z)))
    return jnp.minimum(z, 0.0) - sp, jnp.minimum(-z, 0.0) - sp


def _split_dot(v, u):
    hi = v.astype(BF16)
    lo = (v - hi.astype(F32)).astype(BF16)
    return _dot(hi, u, NN) + _dot(lo, u, NN)


def _head_out(o, g):
    return o * _rms_scale(o) * g


def _sb_fwd(proj, gain, n_heads, name, tq=512):
    S = proj.shape[0]
    H, tk = n_heads, KEY_BLOCK
    tq = _tile(S, tq, tk)
    scale = HEAD_DIM ** -0.5

    def body(q_ref, k_ref, v_ref, g_ref, o_ref, ct_ref, mx_ref, oacc, cacc):
        i = pl.program_id(1)
        q = q_ref[...].astype(BF16)
        oacc[...] = jnp.zeros_like(oacc)
        cacc[...] = jnp.zeros_like(cacc)
        row = i * tq + lax.broadcasted_iota(jnp.int32, (tq, tk), 0)
        col = lax.broadcasted_iota(jnp.int32, (tq, tk), 1)
        later = (lax.broadcasted_iota(jnp.int32, (tk, tk), 0) > lax.broadcasted_iota(jnp.int32, (tk, tk), 1)).astype(BF16)
        nkb = (i + 1) * (tq // tk)

        def step(it, carry):
            k0 = pl.multiple_of((nkb - 1 - it) * tk, tk)
            kj = k_ref[pl.ds(k0, tk), :].astype(BF16)
            vj = v_ref[pl.ds(k0, tk), :].astype(BF16)
            z = _dot(q, kj, NT) * scale
            causal = (col + k0) < row
            lb, lk = _logsig_pair(z)
            lk = jnp.where(causal, lk, 0.0)
            c = cacc[...]
            a = jnp.where(causal, jnp.exp(lb + _split_dot(lk, later) + c), 0.0)
            oacc[...] += _dot(a.astype(BF16), vj, NN)
            cacc[...] = c + jnp.sum(lk, axis=1, keepdims=True)
            return carry

        lax.fori_loop(0, nkb, step, 0)
        o = oacc[...]
        o_ref[...] = o
        ct_ref[...] = jnp.broadcast_to(cacc[...], (tq, LANES))
        mx_ref[...] = _head_out(o, g_ref[...]).astype(BF16)

    blk = pl.BlockSpec((tq, HEAD_DIM), lambda h, i: (i, h))
    return _pcall(
        body, grid=(H, S // tq),
        in_specs=[blk, pl.BlockSpec((S, HEAD_DIM), lambda h, i: (0, H + h)),
                  pl.BlockSpec((S, HEAD_DIM), lambda h, i: (0, 2 * H + h)), pl.BlockSpec((1, HEAD_DIM), lambda h, i: (0, h))],
        out_specs=[blk, blk, blk],
        out_shape=[jax.ShapeDtypeStruct((S, H * HEAD_DIM), F32), jax.ShapeDtypeStruct((S, H * HEAD_DIM), F32),
                   jax.ShapeDtypeStruct((S, H * HEAD_DIM), BF16)],
        scratch_shapes=[pltpu.VMEM((tq, HEAD_DIM), F32), pltpu.VMEM((tq, 1), F32)],
        compiler_params=_params("parallel", "arbitrary"), name=name)(proj, proj, proj, gain)


def _sb_bwd(proj, gain, o_raw, ctot, dmixed, dm_col0, n_heads, name, tq=512):
    S = proj.shape[0]
    H, tk = n_heads, KEY_BLOCK
    tq = _tile(S, tq, tk)
    nq = S // tq
    scale = HEAD_DIM ** -0.5

    def body(q_ref, k_ref, v_ref, g_ref, o_ref, ct_ref, dm_ref, dq_ref, dk_ref, dv_ref, dg_ref,
             dkacc, dvacc, dqacc, pfx, gcar):
        i = pl.program_id(1)

        @pl.when(i == 0)
        def _():
            dkacc[...] = jnp.zeros_like(dkacc)
            dvacc[...] = jnp.zeros_like(dvacc)
            dg_ref[...] = jnp.zeros_like(dg_ref)

        o, dm, g = o_ref[...], dm_ref[...], g_ref[...]
        r = _rms_scale(o)
        do = _rms_bwd(dm * g, o, r).astype(BF16)
        dg_ref[...] += jnp.broadcast_to(jnp.sum(dm * o * r, axis=0, keepdims=True), dg_ref.shape)
        q = q_ref[...].astype(BF16)
        ct = ct_ref[:, 0:1]
        dqacc[...] = jnp.zeros_like(dqacc)
        pfx[...] = jnp.zeros_like(pfx)
        gcar[...] = jnp.zeros_like(gcar)
        row = i * tq + lax.broadcasted_iota(jnp.int32, (tq, tk), 0)
        col = lax.broadcasted_iota(jnp.int32, (tq, tk), 1)
        ia, ib = lax.broadcasted_iota(jnp.int32, (tk, tk), 0), lax.broadcasted_iota(jnp.int32, (tk, tk), 1)
        later = (ia > ib).astype(BF16)
        earlier = (ia < ib).astype(BF16)
        nkb = (i + 1) * (tq // tk)

        def step(j, carry):
            k0 = pl.multiple_of(j * tk, tk)
            kj = k_ref[pl.ds(k0, tk), :].astype(BF16)
            vj = v_ref[pl.ds(k0, tk), :].astype(BF16)
            z = _dot(q, kj, NT) * scale
            causal = (col + k0) < row
            lb, lk = _logsig_pair(z)
            lk = jnp.where(causal, lk, 0.0)
            rs = jnp.sum(lk, axis=1, keepdims=True)
            p = pfx[...]
            a = jnp.where(causal, jnp.exp(lb + _split_dot(lk, later) + (ct - p - rs)), 0.0)
            dl = _dot(do, vj, NT) * a
            dvacc[pl.ds(k0, tk), :] += _dot(a.astype(BF16), do, TN)
            gc = gcar[...]
            gsum = _split_dot(dl, earlier) + gc
            sig = jnp.exp(lb)
            dz = ((dl * (1.0 - sig) - jnp.where(causal, gsum * sig, 0.0)) * scale).astype(BF16)
            dqacc[...] += _dot(dz, kj, NN)
            dkacc[pl.ds(k0, tk), :] += _dot(dz, q, TN)
            pfx[...] = p + rs
            gcar[...] = gc + jnp.sum(dl, axis=1, keepdims=True)
            return carry

        lax.fori_loop(0, nkb, step, 0)
        dq_ref[...] = dqacc[...].astype(BF16)

        @pl.when(i == nq - 1)
        def _():
            dk_ref[...] = dkacc[...].astype(BF16)
            dv_ref[...] = dvacc[...].astype(BF16)

    blk = pl.BlockSpec((tq, HEAD_DIM), lambda h, i: (i, h))
    full = pl.BlockSpec((S, HEAD_DIM), lambda h, i: (0, h))
    W = H * HEAD_DIM
    return _pcall(
        body, grid=(H, nq),
        in_specs=[blk, pl.BlockSpec((S, HEAD_DIM), lambda h, i: (0, H + h)),
                  pl.BlockSpec((S, HEAD_DIM), lambda h, i: (0, 2 * H + h)), pl.BlockSpec((1, HEAD_DIM), lambda h, i: (0, h)),
                  blk, blk, pl.BlockSpec((tq, HEAD_DIM), lambda h, i: (i, dm_col0 + h))],
        out_specs=[blk, full, full, pl.BlockSpec((8, HEAD_DIM), lambda h, i: (0, h))],
        out_shape=[jax.ShapeDtypeStruct((S, W), BF16), jax.ShapeDtypeStruct((S, W), BF16),
                   jax.ShapeDtypeStruct((S, W), BF16), jax.ShapeDtypeStruct((8, W), F32)],
        scratch_shapes=[pltpu.VMEM((S, HEAD_DIM), F32), pltpu.VMEM((S, HEAD_DIM), F32), pltpu.VMEM((tq, HEAD_DIM), F32),
                        pltpu.VMEM((tq, 1), F32), pltpu.VMEM((tq, 1), F32)],
        compiler_params=_params("arbitrary", "arbitrary"), name=name)(proj, proj, proj, gain, o_raw, ctot, dmixed)


def _rope_tables(S):
    inv_freq = ROPE_THETA ** (-jnp.arange(0, HEAD_DIM, 2, dtype=F32) / HEAD_DIM)
    ang = jnp.arange(S, dtype=F32)[:, None] * inv_freq[None, :]
    cos, sin = jnp.cos(ang), jnp.sin(ang)
    return jnp.concatenate([cos, cos], axis=1), jnp.concatenate([-sin, sin], axis=1)


def _rope(v, cos2, sin_signed):
    return v * cos2 + pltpu.roll(v, HEAD_DIM // 2, axis=1) * sin_signed


def _dil_rows(d, r, l0, n):
    if d == 1:
        return pl.ds(l0 if isinstance(l0, int) else pl.multiple_of(l0, KEY_BLOCK), n)
    return pl.ds(r + d * l0, n, stride=d)


def _dil_blocks(S, visit):
    B = KEY_BLOCK
    for b, d in enumerate(DILATIONS):
        nb = S // d // B

        def per_residue(r, carry, b=b, d=d, nb=nb):
            visit(b, d, r, 0, True)
            if nb > 1:
                def per_block(n, c2):
                    visit(b, d, r, n * B, False)
                    return c2
                lax.fori_loop(1, nb, per_block, 0)
            return carry

        if d == 1:
            per_residue(0, 0)
        else:
            lax.fori_loop(0, d, per_residue, 0)


def _dil_mask(first):
    B = KEY_BLOCK
    nk = B if first else 2 * B
    iq = lax.broadcasted_iota(jnp.int32, (B, nk), 0)
    ik = lax.broadcasted_iota(jnp.int32, (B, nk), 1)
    return (ik <= iq) if first else ((ik >= iq) & (ik <= iq + B))


def _dil_fwd(proj, cos2, sin_signed, gain, col0, n_heads, name):
    S = proj.shape[0]
    H, B = n_heads, KEY_BLOCK
    scale = HEAD_DIM ** -0.5
    rc = _tile(S, 256, 8)

    def body(q_ref, k_ref, v_ref, c_ref, s_ref, g_ref, o_ref, l_ref, mx_ref, qr, kr, *per_branch):
        ob, lb = per_branch[:len(DILATIONS)], per_branch[len(DILATIONS):]

        def rope_rows(t, carry):
            rows = pl.ds(pl.multiple_of(t * rc, rc), rc)
            qr[rows, :] = _rope(q_ref[rows, :], c_ref[rows, :], s_ref[rows, :])
            kr[rows, :] = _rope(k_ref[rows, :], c_ref[rows, :], s_ref[rows, :])
            return carry

        lax.fori_loop(0, S // rc, rope_rows, 0)

        def visit(b, d, r, l0, first):
            nk = B if first else 2 * B
            qrows = _dil_rows(d, r, l0, B)
            krows = qrows if first else _dil_rows(d, r, l0 - B, nk)
            s = _dot(qr[qrows, :].astype(BF16), kr[krows, :].astype(BF16), NT) * scale
            s = jnp.where(_dil_mask(first), s, NEG)
            m = jnp.max(s, axis=1, keepdims=True)
            p = jnp.exp(s - m)
            den = jnp.sum(p, axis=1, keepdims=True)
            ob[b][qrows, :] = _dot(p.astype(BF16), v_ref[krows, :].astype(BF16), NN) / den
            lb[b][qrows, :] = jnp.broadcast_to(m + jnp.log(den), (B, LANES))

        _dil_blocks(S, visit)

        def combine(t, carry):
            rows = pl.ds(pl.multiple_of(t * rc, rc), rc)
            l0, l1, l2 = lb[0][rows, :], lb[1][rows, :], lb[2][rows, :]
            m = jnp.maximum(jnp.maximum(l0, l1), l2)
            w0, w1, w2 = jnp.exp(l0 - m), jnp.exp(l1 - m), jnp.exp(l2 - m)
            den = w0 + w1 + w2
            o = (w0 * ob[0][rows, :] + w1 * ob[1][rows, :] + w2 * ob[2][rows, :]) / den
            o_ref[rows, :] = o
            l_ref[rows, :] = m + jnp.log(den)
            mx_ref[rows, :] = _head_out(o, g_ref[...]).astype(BF16)
            return carry

        lax.fori_loop(0, S // rc, combine, 0)

    def col(k):
        return pl.BlockSpec((S, HEAD_DIM), lambda h: (0, col0 + k * H + h))

    tab = pl.BlockSpec((S, HEAD_DIM), lambda h: (0, 0))
    out = pl.BlockSpec((S, HEAD_DIM), lambda h: (0, h))
    W = H * HEAD_DIM
    return _pcall(
        body, grid=(H,),
        in_specs=[col(0), col(1), col(2), tab, tab, pl.BlockSpec((1, HEAD_DIM), lambda h: (0, h))],
        out_specs=[out, out, out],
        out_shape=[jax.ShapeDtypeStruct((S, W), F32), jax.ShapeDtypeStruct((S, W), F32), jax.ShapeDtypeStruct((S, W), BF16)],
        scratch_shapes=[pltpu.VMEM((S, HEAD_DIM), F32)] * (2 + 2 * len(DILATIONS)),
        compiler_params=_params("parallel"), name=name)(proj, proj, proj, cos2, sin_signed, gain)


def _dil_bwd(proj, cos2, sin_signed, gain, o_raw, lse, dmixed, dm_col0, col0, n_heads, name):
    S = proj.shape[0]
    H, B = n_heads, KEY_BLOCK
    scale = HEAD_DIM ** -0.5
    rc = _tile(S, 256, 8)

    def body(q_ref, k_ref, v_ref, c_ref, s_ref, g_ref, o_ref, l_ref, dm_ref, dq_ref, dk_ref, dv_ref, dg_ref,
             qr, kr, dos, dsum, dqr, dkr, dvv):
        dg_ref[...] = jnp.zeros_like(dg_ref)

        def prep(t, carry):
            rows = pl.ds(pl.multiple_of(t * rc, rc), rc)
            qr[rows, :] = _rope(q_ref[rows, :], c_ref[rows, :], s_ref[rows, :])
            kr[rows, :] = _rope(k_ref[rows, :], c_ref[rows, :], s_ref[rows, :])
            o, dm = o_ref[rows, :], dm_ref[rows, :]
            r = _rms_scale(o)
            do = _rms_bwd(dm * g_ref[...], o, r)
            dg_ref[...] += jnp.broadcast_to(jnp.sum(dm * o * r, axis=0, keepdims=True), dg_ref.shape)
            dos[rows, :] = do
            dsum[rows, :] = jnp.broadcast_to(jnp.sum(do * o, axis=1, keepdims=True), (rc, LANES))
            dqr[rows, :] = jnp.zeros((rc, HEAD_DIM), F32)
            dkr[rows, :] = jnp.zeros((rc, HEAD_DIM), F32)
            dvv[rows, :] = jnp.zeros((rc, HEAD_DIM), F32)
            return carry

        lax.fori_loop(0, S // rc, prep, 0)

        def visit(b, d, r, l0, first):
            nk = B if first else 2 * B
            qrows = _dil_rows(d, r, l0, B)
            krows = qrows if first else _dil_rows(d, r, l0 - B, nk)
            qs, ks = qr[qrows, :].astype(BF16), kr[krows, :].astype(BF16)
            do = dos[qrows, :].astype(BF16)
            s = _dot(qs, ks, NT) * scale
            s = jnp.where(_dil_mask(first), s, NEG)
            p = jnp.exp(s - l_ref[qrows, :][:, 0:1])
            dp = _dot(do, v_ref[krows, :].astype(BF16), NT)
            ds = (p * (dp - dsum[qrows, :][:, 0:1]) * scale).astype(BF16)
            dqr[qrows, :] += _dot(ds, ks, NN)
            dkr[krows, :] += _dot(ds, qs, TN)
            dvv[krows, :] += _dot(p.astype(BF16), do, TN)

        _dil_blocks(S, visit)

        def finish(t, carry):
            rows = pl.ds(pl.multiple_of(t * rc, rc), rc)
            c, s = c_ref[rows, :], s_ref[rows, :]
            dq, dk = dqr[rows, :], dkr[rows, :]
            dq_ref[rows, :] = (dq * c + pltpu.roll(dq * s, HEAD_DIM // 2, axis=1)).astype(BF16)
            dk_ref[rows, :] = (dk * c + pltpu.roll(dk * s, HEAD_DIM // 2, axis=1)).astype(BF16)
            dv_ref[rows, :] = dvv[rows, :].astype(BF16)
            return carry

        lax.fori_loop(0, S // rc, finish, 0)

    def col(k):
        return pl.BlockSpec((S, HEAD_DIM), lambda h: (0, col0 + k * H + h))

    tab = pl.BlockSpec((S, HEAD_DIM), lambda h: (0, 0))
    out = pl.BlockSpec((S, HEAD_DIM), lambda h: (0, h))
    W = H * HEAD_DIM
    big = pltpu.VMEM((S, HEAD_DIM), F32)
    return _pcall(
        body, grid=(H,),
        in_specs=[col(0), col(1), col(2), tab, tab, pl.BlockSpec((1, HEAD_DIM), lambda h: (0, h)), out, out,
                  pl.BlockSpec((S, HEAD_DIM), lambda h: (0, dm_col0 + h))],
        out_specs=[out, out, out, pl.BlockSpec((8, HEAD_DIM), lambda h: (0, h))],
        out_shape=[jax.ShapeDtypeStruct((S, W), BF16), jax.ShapeDtypeStruct((S, W), BF16),
                   jax.ShapeDtypeStruct((S, W), BF16), jax.ShapeDtypeStruct((8, W), F32)],
        scratch_shapes=[big, big, big, pltpu.VMEM((S, LANES), F32), big, big, big],
        compiler_params=_params("parallel"), name=name)(proj, proj, proj, cos2, sin_signed, gain, o_raw, lse, dmixed)


GELU_C = math.sqrt(2.0 / math.pi)
GELU_A = 0.044715
HALO = 16


def _shift_down(cur, halo, k):
    out = pltpu.roll(cur, k, axis=0)
    row = lax.broadcasted_iota(jnp.int32, cur.shape, 0)
    for t in range(k):
        out = jnp.where(row == t, halo[HALO - k + t:HALO - k + t + 1, :], out)
    return out


def _shift_up(cur, halo, k):
    n = cur.shape[0]
    out = pltpu.roll(cur, n - k, axis=0)
    row = lax.broadcasted_iota(jnp.int32, cur.shape, 0)
    for t in range(k):
        out = jnp.where(row == n - k + t, halo[t:t + 1, :], out)
    return out


def _conv3(cur, halo, cw):
    return _shift_down(cur, halo, 2) * cw[0:1, :] + _shift_down(cur, halo, 1) * cw[1:2, :] + cur * cw[2:3, :] + cw[3:4, :]


def _gelu_parts(x):
    t = jnp.tanh(GELU_C * (x + GELU_A * x * x * x))
    return 0.5 * x * (1.0 + t), t


def _geglu_specs(tm, tn, ncb):
    hb = tm // HALO

    def cur(off):
        return pl.BlockSpec((tm, tn), lambda j, i: (i, off + j))

    def prev(off):
        return pl.BlockSpec((HALO, tn), lambda j, i: (jnp.maximum(i * hb - 1, 0), off + j))

    def taps(off):
        return pl.BlockSpec((8, tn), lambda j, i: (0, off + j))

    return [cur(0), prev(0), cur(ncb), prev(ncb), taps(0), taps(ncb)]


def _geglu_fwd(u, cwb, name, tm=256, tn=512):
    S, F2 = u.shape
    F = F2 // 2
    tm, tn = _tile(S, tm, HALO), _tile(F, tn)
    ncb = F // tn

    def body(g_ref, gp_ref, v_ref, vp_ref, cg_ref, cv_ref, y_ref):
        top = pl.program_id(1) > 0
        gp = jnp.where(top, gp_ref[...].astype(F32), 0.0)
        vp = jnp.where(top, vp_ref[...].astype(F32), 0.0)
        gc = _conv3(g_ref[...].astype(F32), gp, cg_ref[...])
        vc = _conv3(v_ref[...].astype(F32), vp, cv_ref[...])
        y_ref[...] = (_gelu_parts(gc)[0] * vc).astype(BF16)

    return _pcall(body, grid=(ncb, S // tm), in_specs=_geglu_specs(tm, tn, ncb),
                  out_specs=pl.BlockSpec((tm, tn), lambda j, i: (i, j)),
                  out_shape=jax.ShapeDtypeStruct((S, F), BF16),
                  compiler_params=_params("parallel", "parallel"), name=name)(u, u, u, u, cwb, cwb)


def _geglu_bwd(u, dy, cwb, name, tm=256, tn=512):
    S, F2 = u.shape
    F = F2 // 2
    tm, tn = _tile(S, tm, HALO), _tile(F, tn)
    ncb = F // tn

    def body(g_ref, gp_ref, v_ref, vp_ref, cg_ref, cv_ref, dy_ref, dc_ref, dwg_ref, dwv_ref):
        i = pl.program_id(1)

        @pl.when(i == 0)
        def _():
            dwg_ref[...] = jnp.zeros_like(dwg_ref)
            dwv_ref[...] = jnp.zeros_like(dwv_ref)

        top = i > 0
        g, v = g_ref[...].astype(F32), v_ref[...].astype(F32)
        gp = jnp.where(top, gp_ref[...].astype(F32), 0.0)
        vp = jnp.where(top, vp_ref[...].astype(F32), 0.0)
        gc = _conv3(g, gp, cg_ref[...])
        vc = _conv3(v, vp, cv_ref[...])
        act, t = _gelu_parts(gc)
        dact = 0.5 * (1.0 + t) + 0.5 * gc * (1.0 - t * t) * GELU_C * (1.0 + 3.0 * GELU_A * gc * gc)
        dyv = dy_ref[...].astype(F32)
        dgc = dyv * vc * dact
        dvc = dyv * act
        dc_ref[0] = dgc.astype(BF16)
        dc_ref[1] = dvc.astype(BF16)

        def taps(out_ref, dc, cur, halo):
            out_ref[0:1, :] += jnp.sum(dc * _shift_down(cur, halo, 2), axis=0, keepdims=True)
            out_ref[1:2, :] += jnp.sum(dc * _shift_down(cur, halo, 1), axis=0, keepdims=True)
            out_ref[2:3, :] += jnp.sum(dc * cur, axis=0, keepdims=True)
            out_ref[3:4, :] += jnp.sum(dc, axis=0, keepdims=True)

        taps(dwg_ref, dgc, g, gp)
        taps(dwv_ref, dvc, v, vp)

    return _pcall(body, grid=(ncb, S // tm),
                  in_specs=_geglu_specs(tm, tn, ncb) + [pl.BlockSpec((tm, tn), lambda j, i: (i, j))],
                  out_specs=[pl.BlockSpec((2, tm, tn), lambda j, i: (0, i, j)),
                             pl.BlockSpec((8, tn), lambda j, i: (0, j)), pl.BlockSpec((8, tn), lambda j, i: (0, j))],
                  out_shape=[jax.ShapeDtypeStruct((2, S, F), BF16), jax.ShapeDtypeStruct((8, F), F32),
                             jax.ShapeDtypeStruct((8, F), F32)],
                  compiler_params=_params("parallel", "arbitrary"), name=name)(u, u, u, u, cwb, cwb, dy)


def _conv_bwd(dc, cwb, name, tm=256, tn=512):
    _, S, F = dc.shape
    tm, tn = _tile(S, tm, HALO), _tile(F, tn)
    ncb, nrb = F // tn, S // tm
    hb = tm // HALO

    def body(c_ref, n_ref, w_ref, du_ref):
        cur = c_ref[...].astype(F32)
        nxt = jnp.where(pl.program_id(2) < nrb - 1, n_ref[...].astype(F32), 0.0)
        w = w_ref[...]
        du = cur * w[2:3, :] + _shift_up(cur, nxt, 1) * w[1:2, :] + _shift_up(cur, nxt, 2) * w[0:1, :]
        du_ref[...] = du.astype(BF16)

    return _pcall(body, grid=(2, ncb, nrb),
                  in_specs=[pl.BlockSpec((None, tm, tn), lambda c, j, i: (c, i, j)),
                            pl.BlockSpec((None, HALO, tn), lambda c, j, i: (c, jnp.minimum((i + 1) * hb, S // HALO - 1), j)),
                            pl.BlockSpec((8, tn), lambda c, j, i: (0, c * ncb + j))],
                  out_specs=pl.BlockSpec((tm, tn), lambda c, j, i: (i, c * ncb + j)),
                  out_shape=jax.ShapeDtypeStruct((S, 2 * F), BF16),
                  compiler_params=_params("parallel", "parallel", "parallel"), name=name)(dc, dc, cwb)


def _adam_math(w, g, m, v):
    m = ADAM_B1 * m + (1.0 - ADAM_B1) * g
    v = ADAM_B2 * v + (1.0 - ADAM_B2) * (g * g)
    m_hat = m / (1.0 - ADAM_B1 ** ADAM_STEP)
    v_hat = v / (1.0 - ADAM_B2 ** ADAM_STEP)
    return -ADAM_LR * (m_hat / (jnp.sqrt(v_hat) + ADAM_EPS) + ADAM_WD * w), m, v


def _adamw(w, parts, m, v, name, tr=256):
    R, C = w.shape
    n, _, Cp = parts.shape
    tr = _tile(R, tr, 8)

    def body(w_ref, p_ref, m_ref, v_ref, g_out, d_out, m_out, v_out):
        g = p_ref[0, :, 0:C].astype(F32)
        for k in range(1, n):
            g = g + p_ref[k, :, 0:C].astype(F32)
        d, mn, vn = _adam_math(w_ref[...], g, m_ref[...], v_ref[...])
        g_out[...] = g
        d_out[...] = d
        m_out[...] = mn
        v_out[...] = vn

    spec = pl.BlockSpec((tr, C), lambda i: (i, 0))
    shape = jax.ShapeDtypeStruct((R, C), F32)
    return _pcall(body, grid=(R // tr,), in_specs=[spec, pl.BlockSpec((n, tr, Cp), lambda i: (0, i, 0)), spec, spec],
                  out_specs=[spec] * 4, out_shape=[shape] * 4, compiler_params=_params("parallel"), name=name)(w, parts, m, v)


def _place():
    return lax.axis_index("x"), lax.axis_index("y"), lax.axis_index("c")


def _other_chips(x, y):
    return [(1 - x, y), (x, 1 - y), (1 - x, 1 - y)]


IN_HBM = pl.BlockSpec(memory_space=pltpu.HBM)
SEM = pl.BlockSpec(memory_space=pltpu.SEMAPHORE)
EFFECT = pltpu.SideEffectType.DATAFLOW_SIDE_EFFECTING
TOKEN = jax.ShapeDtypeStruct((8, LANES), F32)
TOKEN_SPEC = pl.BlockSpec(memory_space=pltpu.VMEM)


def _in_hbm(a):
    return pltpu.with_memory_space_constraint(a, pltpu.HBM)


def _landing(shape):
    return _in_hbm(lax.empty(shape.shape, shape.dtype))


def _hbm_like(a):
    return pltpu.HBM(a.shape, a.dtype)


def _gather_start(shards, slots, out_shapes, zero_src, zero_views, after, name):
    na = len(shards)

    def body(*refs):
        srcs, zsrc, land = refs[:na], refs[na], refs[na + 1:2 * na + 1]
        send_sems, recv_sems = refs[2 * na + 2], refs[2 * na + 3]
        token, local_sems = refs[-2], refs[-1]
        x, y, c = _place()
        me = (x, y, c)
        local = [pltpu.make_async_copy(srcs[a], slots[a](land[a], *me), local_sems.at[a]) for a in range(na)]
        local += [pltpu.make_async_copy(zsrc, view(land[a]), local_sems.at[na + z]) for z, (a, view) in enumerate(zero_views)]
        for cp in local:
            cp.start()
        for a in range(na):
            for k, to in enumerate([(x, y, 1 - c)] + [(*chip, c) for chip in _other_chips(x, y)]):
                pltpu.make_async_remote_copy(
                    src_ref=srcs[a], dst_ref=slots[a](land[a], *me), send_sem=send_sems.at[4 * a + k],
                    recv_sem=recv_sems.at[4 * a + k], device_id=to, device_id_type=MESH).start()
        for cp in local:
            cp.wait()
        token[...] = jnp.zeros_like(token)

    sems = pltpu.SemaphoreType.DMA((4 * na,))
    outs = _pcall(
        body, in_specs=[IN_HBM] * (2 * na + 1) + [HBM],
        out_specs=[SEM, SEM] + [IN_HBM] * (2 * na) + [TOKEN_SPEC],
        out_shape=[sems, sems] + [_hbm_like(s) for s in shards] + [_hbm_like(s) for s in out_shapes] + [TOKEN],
        input_output_aliases={**{a: 2 + a for a in range(na)}, **{na + 1 + a: 2 + na + a for a in range(na)}},
        scratch_shapes=[pltpu.SemaphoreType.DMA((na + len(zero_views),))],
        compiler_params=pltpu.CompilerParams(has_side_effects=EFFECT), name=name,
    )(*[_in_hbm(s) for s in shards], _in_hbm(zero_src), *[_landing(s) for s in out_shapes], after)
    return outs[0], outs[1], outs[2:2 + na], outs[2 + na:2 + 2 * na], outs[-1]


def _gather_forward(shards, gathered, send_sems, recv_sems, slots, after, name):
    na = len(shards)

    def body(*refs):
        srcs, gath = refs[:na], refs[na:2 * na]
        send1, recv1 = refs[2 * na], refs[2 * na + 1]
        fsend, frecv = refs[2 * na + 3], refs[2 * na + 4]
        token = refs[-1]
        x, y, c = _place()
        chips = _other_chips(x, y)
        for a in range(na):
            for k, peer in enumerate([(x, y, 1 - c)] + [(*chip, c) for chip in chips]):
                arrival = pltpu.make_async_remote_copy(
                    src_ref=srcs[a], dst_ref=slots[a](gath[a], *peer), send_sem=send1.at[4 * a + k],
                    recv_sem=recv1.at[4 * a + k], device_id=peer, device_id_type=MESH)
                arrival.wait_send()
                arrival.wait_recv()
        for a in range(na):
            for j, chip in enumerate(chips):
                view = slots[a](gath[a], *chip, c)
                pltpu.make_async_remote_copy(
                    src_ref=view, dst_ref=view, send_sem=fsend.at[3 * a + j], recv_sem=frecv.at[3 * a + j],
                    device_id=(x, y, 1 - c), device_id_type=MESH).start()
        token[...] = jnp.zeros_like(token)

    sems = pltpu.SemaphoreType.DMA((3 * na,))
    outs = _pcall(
        body, in_specs=[IN_HBM] * (2 * na) + [SEM, SEM, HBM],
        out_specs=[SEM, SEM] + [IN_HBM] * na + [TOKEN_SPEC],
        out_shape=[sems, sems] + [_hbm_like(g) for g in gathered] + [TOKEN],
        input_output_aliases={na + a: 2 + a for a in range(na)},
        compiler_params=pltpu.CompilerParams(has_side_effects=EFFECT), name=name,
    )(*shards, *gathered, send_sems, recv_sems, after)
    return outs[0], outs[1], outs[2:2 + na], outs[-1]


def _gather_finish(gathered, fsend, frecv, slots, after, name):
    na = len(gathered)

    def body(*refs):
        gath, fs, fr = refs[:na], refs[na], refs[na + 1]
        x, y, c = _place()
        for a in range(na):
            for j, chip in enumerate(_other_chips(x, y)):
                passed = pltpu.make_async_remote_copy(
                    src_ref=slots[a](gath[a], *chip, c), dst_ref=slots[a](gath[a], *chip, 1 - c),
                    send_sem=fs.at[3 * a + j], recv_sem=fr.at[3 * a + j], device_id=(x, y, 1 - c), device_id_type=MESH)
                passed.wait_send()
                passed.wait_recv()

    outs = _pcall(
        body, in_specs=[IN_HBM] * na + [SEM, SEM, HBM], out_specs=[IN_HBM] * na,
        out_shape=[_hbm_like(g) for g in gathered], input_output_aliases={a: a for a in range(na)},
        compiler_params=pltpu.CompilerParams(has_side_effects=EFFECT), name=name,
    )(*gathered, fsend, frecv, after)
    return list(outs)


def _pair_exchange(grads, views, recv_shapes, name):
    na = len(grads)

    def body(*refs):
        srcs, dsts = refs[:na], refs[na:2 * na]
        send_sems, recv_sems = refs[2 * na:]
        x, y, c = _place()
        copies = []
        for a in range(na):
            for chip in range(N_CHIP):
                copies.append(pltpu.make_async_remote_copy(
                    src_ref=views[a](srcs[a], chip, 1 - c), dst_ref=dsts[a].at[chip],
                    send_sem=send_sems.at[a * N_CHIP + chip], recv_sem=recv_sems.at[a * N_CHIP + chip],
                    device_id=(x, y, 1 - c), device_id_type=MESH))
        for cp in copies:
            cp.start()
        for cp in copies:
            cp.wait()

    return _pcall(body, in_specs=[HBM] * na, out_specs=[HBM] * na, out_shape=recv_shapes,
                  scratch_shapes=[pltpu.SemaphoreType.DMA((N_CHIP * na,)), pltpu.SemaphoreType.DMA((N_CHIP * na,))],
                  name=name)(*grads)


def _chip_start(pair, after, name):
    def body(src, land, after_ref, send_sems, recv_sems, src_thru, land_thru, token, local_sem):
        x, y, c = _place()
        mine = 2 * x + y
        own = pltpu.make_async_copy(src.at[mine], land.at[mine], local_sem)
        own.start()
        for j, (px, py) in enumerate(_other_chips(x, y)):
            pltpu.make_async_remote_copy(
                src_ref=src.at[2 * px + py], dst_ref=land.at[mine], send_sem=send_sems.at[j], recv_sem=recv_sems.at[j],
                device_id=(px, py, c), device_id_type=MESH).start()
        own.wait()
        token[...] = jnp.zeros_like(token)

    sems = pltpu.SemaphoreType.DMA((3,))
    return _pcall(
        body, in_specs=[IN_HBM, IN_HBM, HBM], out_specs=[SEM, SEM, IN_HBM, IN_HBM, TOKEN_SPEC],
        out_shape=[sems, sems, _hbm_like(pair), _hbm_like(pair), TOKEN], input_output_aliases={0: 2, 1: 3},
        scratch_shapes=[pltpu.SemaphoreType.DMA(())],
        compiler_params=pltpu.CompilerParams(has_side_effects=EFFECT), name=name,
    )(_in_hbm(pair), _landing(pair), after)


def _chip_wait(pair, parts, send_sems, recv_sems, after, name):
    def body(src, land, send, recv, after_ref, src_thru, land_thru):
        x, y, c = _place()
        mine = 2 * x + y
        for j, (px, py) in enumerate(_other_chips(x, y)):
            copy = pltpu.make_async_remote_copy(
                src_ref=src.at[2 * px + py], dst_ref=land.at[2 * px + py], send_sem=send.at[j], recv_sem=recv.at[j],
                device_id=(px, py, c), device_id_type=MESH)
            copy.wait_send()
            copy.wait_recv()

    return _pcall(
        body, in_specs=[IN_HBM, IN_HBM, SEM, SEM, HBM], out_specs=[IN_HBM, IN_HBM],
        out_shape=[_hbm_like(pair), _hbm_like(parts)], input_output_aliases={0: 0, 1: 1},
        compiler_params=pltpu.CompilerParams(has_side_effects=EFFECT), name=name,
    )(pair, parts, send_sems, recv_sems, after)[1]


def _pair_add(core, grad, recv, block, grad_spec, name):
    _, R, C = recv.shape
    tr = block

    def body(c_ref, g_ref, r_ref, o_ref):
        o_ref[...] = (g_ref[...].astype(F32) + r_ref[...].astype(F32)).astype(BF16)

    grid_spec = pltpu.PrefetchScalarGridSpec(
        num_scalar_prefetch=1, grid=(N_CHIP, R // tr),
        in_specs=[grad_spec, pl.BlockSpec((None, tr, C), lambda k, i, c: (k, i, 0))],
        out_specs=pl.BlockSpec((None, tr, C), lambda k, i, c: (k, i, 0)))
    return _pcall(body, grid_spec=grid_spec, out_shape=jax.ShapeDtypeStruct(recv.shape, BF16),
                  compiler_params=_params("parallel", "parallel"), name=name)(core, grad, recv)


def _all_reduce_small(part, name):
    R = part.shape[0]

    def body(p_ref, o_ref, all_ref, send_sems, recv_sems):
        x, y, c = _place()
        me = 4 * x + 2 * y + c
        all_ref[me] = p_ref[...]
        peers = [(x, y, 1 - c)] + [(px, py, pc) for px, py in _other_chips(x, y) for pc in (c, 1 - c)]
        copies = [pltpu.make_async_remote_copy(
            src_ref=p_ref, dst_ref=all_ref.at[me], send_sem=send_sems.at[k], recv_sem=recv_sems.at[k],
            device_id=peer, device_id_type=MESH) for k, peer in enumerate(peers)]
        for cp in copies:
            cp.start()
        for k, (px, py, pc) in enumerate(peers):
            pltpu.make_async_remote_copy(
                src_ref=p_ref, dst_ref=all_ref.at[4 * px + 2 * py + pc], send_sem=send_sems.at[k], recv_sem=recv_sems.at[k],
                device_id=peers[k], device_id_type=MESH).wait_recv()
        for cp in copies:
            cp.wait_send()
        acc = all_ref[0]
        for k in range(1, N_DEV):
            acc = acc + all_ref[k]
        o_ref[...] = acc

    vm = pl.BlockSpec(memory_space=pltpu.VMEM)
    return _pcall(body, in_specs=[vm], out_specs=vm, out_shape=jax.ShapeDtypeStruct((R, LANES), F32),
                  scratch_shapes=[pltpu.VMEM((N_DEV, R, LANES), F32), pltpu.SemaphoreType.DMA((7,)), pltpu.SemaphoreType.DMA((7,))],
                  name=name)(part)


def _local_step(x, tgt, gains, weights):
    g_pre_mix, g_post_mix, g_pre_ffn, g_post_ffn, g_sb, g_dil = gains
    S, D = x.shape
    hs = g_sb.shape[1] // HEAD_DIM
    hd = g_dil.shape[1] // HEAD_DIM
    cos2, sin_signed = _rope_tables(S)

    h1 = _rms_fwd(x, g_pre_mix + weights.start(), "rms_in")
    w_in_g = weights.w_in(h1)
    proj = _mm_nn(h1, w_in_g, F32, "proj", tn=768)
    o_sb, ct_sb, mx_sb = _sb_fwd(proj, g_sb, hs, "sb_fwd")
    o_dl, lse_dl, mx_dl = _dil_fwd(proj, cos2, sin_signed, g_dil + weights.forward(o_sb), 3 * hs, hd, "dil_fwd")
    w_out_g, w_up_g, w_down_g, cwb = weights.rest(o_dl)
    mixed = jnp.concatenate([mx_sb, mx_dl], axis=1)
    mix = _mm_nn(mixed, w_out_g, F32, "mix_out", tn=1024)
    x2, h2 = _mid_fwd(x, mix, g_post_mix, g_pre_ffn, "mid_fwd")
    u = _mm_nn(h2, w_up_g, BF16, "ffn_up")
    y = _geglu_fwd(u, cwb, "geglu_fwd")
    f = _mm_nn(y, w_down_g, F32, "ffn_down", tn=1024, tk=1408)

    dy, df, dg_post_ffn, loss = _loss_bwd(x2, f, tgt, g_post_ffn, "loss_bwd")
    dyv = _mm_nt(df, w_down_g, BF16, "d_y", tn=1408)
    dw_down = _mm_tn(y, df, D, BF16, "dw_down", tm=1408, tn=1024)
    dc, dcw_g, dcw_v = _geglu_bwd(u, dyv, cwb + weights.grad("w_down", dw_down), "geglu_bwd")
    du = _conv_bwd(dc, cwb, "conv_bwd")
    dh2 = _mm_nt(du, w_up_g, F32, "d_h2", tk=1408)
    dw_up = _mm_tn(h2, du, w_up_g.shape[2], BF16, "dw_up")
    dx2, dmix, dg_pre_ffn, dg_post_mix = _mid_bwd(
        dy, dh2, x2, mix, g_pre_ffn + weights.grad("w_up", dw_up), g_post_mix, "mid_bwd")
    dmixed = _mm_nt(dmix, w_out_g, F32, "d_mixed")
    dw_out = _mm_tn(mixed, dmix, D, BF16, "dw_out", tn=1024)
    dq_s, dk_s, dv_s, dg_sb = _sb_bwd(proj, g_sb + weights.grad("w_out", dw_out), o_sb, ct_sb, dmixed, 0, hs, "sb_bwd")
    dq_d, dk_d, dv_d, dg_dil = _dil_bwd(proj, cos2, sin_signed, g_dil, o_dl, lse_dl, dmixed, hs, 3 * hs, hd, "dil_bwd")
    dproj = jnp.concatenate([dq_s, dk_s, dv_s, dq_d, dk_d, dv_d], axis=1)
    dh1 = _mm_nt(dproj, w_in_g, F32, "d_h1", tk=768)
    dw_in = _mm_tn(h1, dproj, w_in_g.shape[2], BF16, "dw_in", tn=768)
    grad_x, dg_pre_mix = _first_bwd(dx2, dh1, x, g_pre_mix + weights.grad("w_in", dw_in), "first_bwd")
    small = (dg_pre_mix, dg_post_mix, dg_pre_ffn, dg_post_ffn, dg_sb[0:1], dg_dil[0:1], jnp.concatenate([dcw_g, dcw_v], axis=1))
    return loss, grad_x, small


def _pad_cols(a, to):
    return jnp.pad(a, ((0, 0), (0, to - a.shape[1])))


def kernel(x, pre_mix_gain, post_mix_gain, pre_ffn_gain, post_ffn_gain, w_in, sb_out_gain, dil_out_gain, w_out, w_up, conv_w, conv_b, w_down, loss_target, m_pre_mix_gain, m_post_mix_gain, m_pre_ffn_gain, m_post_ffn_gain, m_w_in, m_sb_out_gain, m_dil_out_gain, m_w_out, m_w_up, m_conv_w, m_conv_b, m_w_down, v_pre_mix_gain, v_post_mix_gain, v_pre_ffn_gain, v_post_ffn_gain, v_w_in, v_sb_out_gain, v_dil_out_gain, v_w_out, v_w_up, v_conv_w, v_conv_b, v_w_down):
    xb, tb = x[0], loss_target[0]
    S, D = xb.shape
    w_in, w_out, w_up, w_down, conv_w = w_in[0], w_out[0], w_up[0], w_down[0], conv_w[0]
    n_in, e_rows = w_in.shape[1], w_out.shape[0]
    cu, half = w_up.shape[1], w_down.shape[0]
    assert cu == 2 * half and half % 16 == 0
    cup = -(-cu // LANES) * LANES
    fp = N_CHIP * cup
    px, py, pc = _place()
    me = 4 * px + 2 * py + pc
    core = jnp.reshape(pc, (1,)).astype(jnp.int32)

    shards = [w_in.astype(BF16), w_out.astype(BF16), _pad_cols(w_up, cup).astype(BF16), w_down.astype(BF16),
              jnp.pad(_pad_cols(conv_w, cup), ((0, 8 - conv_w.shape[0]), (0, 0)))]

    def by_dev(ref, qx, qy, qc):
        return ref.at[4 * qx + 2 * qy + qc]

    def down_slot(ref, qx, qy, qc):
        return ref.at[2 * qx + qy, pl.ds(qc * half, half)]

    def by_pair(ref, chip, k):
        return ref.at[chip, k]

    def down_pair(ref, chip, k):
        return ref.at[chip, pl.ds(k * half, half)]

    def pair_spec(tr, cols):
        return pl.BlockSpec((None, None, tr, cols), lambda k, i, c: (k, c[0], i, 0))

    tr_in, tr_up = _tile(D, 512, 16), _tile(D, 256, 16)
    grad_plan = {
        "w_in": ((N_CHIP, 2, D, n_in), by_pair, (D, n_in), tr_in, pair_spec(tr_in, n_in)),
        "w_out": ((N_CHIP, 2, e_rows, D), by_pair, (e_rows, D), e_rows, pair_spec(e_rows, D)),
        "w_up": ((N_CHIP, 2, D, cup), by_pair, (D, cup), tr_up, pair_spec(tr_up, cup)),
        "w_down": ((N_CHIP, cup, D), down_pair, (half, D), half,
                   pl.BlockSpec((None, half, D), lambda k, i, c: (k, c[0], 0))),
    }

    class Exchanges:
        def __init__(self):
            self.in_flight = {}

        def start(self):
            shapes = [jax.ShapeDtypeStruct((N_DEV, D, n_in), BF16), jax.ShapeDtypeStruct((N_DEV, e_rows, D), BF16),
                      jax.ShapeDtypeStruct((N_DEV, D, cup), BF16), jax.ShapeDtypeStruct((N_CHIP, cup, D), BF16),
                      jax.ShapeDtypeStruct((N_DEV, 8, cup), F32)]
            zero_src = jnp.zeros((cup - cu, D), BF16)
            self.first = _gather_start(shards[:1], [by_dev], shapes[:1], zero_src, [], zero_src, "gather_in_start")
            zero_views = [(2, functools.partial(lambda ref, k: ref.at[k, pl.ds(cu, cup - cu)], k=k)) for k in range(N_CHIP)]
            self.rest_slots = [by_dev, by_dev, down_slot, by_dev]
            self.second = _gather_start(shards[1:], self.rest_slots, shapes[1:], zero_src, zero_views, self.first[4],
                                        "gather_rest_start")
            return self.second[4][0, 0]

        def w_in(self, after):
            send, recv, sh, gath, _ = self.first
            fsend, frecv, gath, token = _gather_forward(sh, gath, send, recv, [by_dev], after, "gather_in_forward")
            return _gather_finish(gath, fsend, frecv, [by_dev], token, "gather_in_finish")[0]

        def forward(self, after):
            send, recv, sh, gath, _ = self.second
            self.passed = _gather_forward(sh, gath, send, recv, self.rest_slots, after, "gather_rest_forward")
            return self.passed[3][0, 0]

        def rest(self, after):
            fsend, frecv, gath, _ = self.passed
            w_out_g, w_up_g, w_down_g, cw_g = _gather_finish(gath, fsend, frecv, self.rest_slots, after, "gather_rest_finish")
            cb = _pad_cols(conv_b.reshape(N_DEV, cu), cup).reshape(1, 2 * fp)
            cw_full = jnp.transpose(cw_g[:, :3, :], (1, 0, 2)).reshape(3, 2 * fp)
            cwb = jnp.concatenate([cw_full, cb, jnp.zeros((4, 2 * fp), F32)], axis=0)
            return w_out_g.reshape(1, N_DEV * e_rows, D), w_up_g, w_down_g.reshape(1, fp, D), cwb

        def grad(self, name, dw):
            view_shape, view, block, tr, spec = grad_plan[name]
            dw = dw.reshape(view_shape)
            recv = _pair_exchange([dw], [view], [jax.ShapeDtypeStruct((N_CHIP, *block), BF16)], "pair_exchange_" + name)[0]
            pair = _pair_add(core, dw, recv, tr, spec, "pair_add_" + name)
            send, recv_sems, pair, parts, token = _chip_start(pair, pair, "chip_start_" + name)
            self.in_flight[name] = (pair, parts, send, recv_sems)
            return token[0, 0]

        def grad_parts(self, name, after):
            return _chip_wait(*self.in_flight[name], after, "chip_wait_" + name)

    exchanges = Exchanges()
    gains = (pre_mix_gain, post_mix_gain, pre_ffn_gain, post_ffn_gain, sb_out_gain, dil_out_gain)
    loss, grad_x, small = _local_step(xb, tb, gains, exchanges)

    flat = jnp.concatenate([s.reshape(-1) for s in small] + [loss.reshape(-1)])
    sizes = [s.size for s in small]
    rows = -(-flat.size // LANES)
    rows = -(-rows // 8) * 8
    packed = jnp.pad(flat, (0, rows * LANES - flat.size)).reshape(rows, LANES)
    total = _all_reduce_small(packed, "reduce_small").reshape(-1)
    offs = [0]
    for s in sizes:
        offs.append(offs[-1] + s)
    red = [total[offs[k]:offs[k + 1]].reshape(small[k].shape) for k in range(len(small))]
    loss_out = total[offs[-1]]
    g_pre_mix, g_post_mix, g_pre_ffn, g_post_ffn, g_sb, g_dil, g_conv = red
    g_conv_b = g_conv[3].reshape(N_DEV, cup)[:, :cu].reshape(1, N_DEV * cu)
    g_conv_w = lax.dynamic_index_in_dim(g_conv[0:3].reshape(3, N_DEV, cup), me, axis=1, keepdims=False)[:, :cu]

    def small_adam(w, g, m, v, name):
        one = w.shape[0] == 1
        if one:
            w, g, m, v = (jnp.broadcast_to(t, (8, t.shape[1])) for t in (w, g, m, v))
        outs = _adamw(w, g[None], m, v, name)
        return [o[0:1] for o in outs] if one else outs

    out_w_down = _adamw(w_down, exchanges.grad_parts("w_down", total), m_w_down[0], v_w_down[0], "adam_w_down")
    out_w_up = _adamw(w_up, exchanges.grad_parts("w_up", out_w_down[1]), m_w_up[0], v_w_up[0], "adam_w_up")
    out_w_out = _adamw(w_out, exchanges.grad_parts("w_out", out_w_up[1]), m_w_out[0], v_w_out[0], "adam_w_out")
    out_w_in = _adamw(w_in, exchanges.grad_parts("w_in", out_w_out[1]), m_w_in[0], v_w_in[0], "adam_w_in")
    out_pre_mix = small_adam(pre_mix_gain, g_pre_mix, m_pre_mix_gain, v_pre_mix_gain, "adam_pre_mix")
    out_post_mix = small_adam(post_mix_gain, g_post_mix, m_post_mix_gain, v_post_mix_gain, "adam_post_mix")
    out_pre_ffn = small_adam(pre_ffn_gain, g_pre_ffn, m_pre_ffn_gain, v_pre_ffn_gain, "adam_pre_ffn")
    out_post_ffn = small_adam(post_ffn_gain, g_post_ffn, m_post_ffn_gain, v_post_ffn_gain, "adam_post_ffn")
    out_sb = small_adam(sb_out_gain, g_sb, m_sb_out_gain, v_sb_out_gain, "adam_sb_gain")
    out_dil = small_adam(dil_out_gain, g_dil, m_dil_out_gain, v_dil_out_gain, "adam_dil_gain")
    out_conv_b = small_adam(conv_b, g_conv_b, m_conv_b, v_conv_b, "adam_conv_b")
    cw8 = [jnp.pad(t, ((0, 5), (0, 0))) for t in (conv_w, g_conv_w, m_conv_w[0], v_conv_w[0])]
    out_conv_w = [o[0:3] for o in _adamw(cw8[0], cw8[1][None], cw8[2], cw8[3], "adam_conv_w")]

    order = [out_pre_mix, out_post_mix, out_pre_ffn, out_post_ffn, [o[None] for o in out_w_in], out_sb, out_dil,
             [o[None] for o in out_w_out], [o[None] for o in out_w_up], [o[None] for o in out_conv_w], out_conv_b,
             [o[None] for o in out_w_down]]
    outs = [loss_out, grad_x[None]]
    for k in range(4):
        outs += [o[k] for o in order]
    return tuple(outs)
```

```python
import functools
import math

import jax
import jax.numpy as jnp
from jax import lax
from jax.experimental import pallas as pl
from jax.experimental.pallas import tpu as pltpu

F32 = jnp.float32
BF16 = jnp.bfloat16
HEAD_DIM = 128
LANES = 128
KEY_BLOCK = 128
DILATIONS = (1, 4, 16)
RMS_EPS = 1e-6
ROPE_THETA = 10000.0
NEG = -1e30
ADAM_LR, ADAM_B1, ADAM_B2, ADAM_EPS, ADAM_WD, ADAM_STEP = 0.001, 0.9, 0.999, 1e-08, 0.01, 10
MESH = pl.DeviceIdType.MESH
N_DEV = 8
N_CHIP = 4
HBM = pl.BlockSpec(memory_space=pl.ANY)
VMEM_LIMIT = 56 * 1024 * 1024

_pcall = pl.pallas_call


def _tile(n, pref, mult=LANES):
    best = None
    t = mult
    while t <= min(n, pref):
        if n % t == 0:
            best = t
        t += mult
    return n if best is None else best


def _params(*sem):
    return pltpu.CompilerParams(dimension_semantics=sem, vmem_limit_bytes=VMEM_LIMIT)


def _dot(a, b, dims):
    return lax.dot_general(a, b, (dims, ((), ())), preferred_element_type=F32)


NN = ((1,), (0,))
NT = ((1,), (1,))
TN = ((0,), (0,))


def _mm_body(dims, nk, tile):
    if nk == 1:
        def single(a_ref, b_ref, o_ref):
            o_ref[...] = _dot(a_ref[...].astype(BF16), b_ref[...].astype(BF16), dims).astype(o_ref.dtype)

        return single, []

    def body(a_ref, b_ref, o_ref, acc_ref):
        k = pl.program_id(2)

        @pl.when(k == 0)
        def _():
            acc_ref[...] = jnp.zeros_like(acc_ref)

        acc_ref[...] += _dot(a_ref[...].astype(BF16), b_ref[...].astype(BF16), dims)

        @pl.when(k == nk - 1)
        def _():
            o_ref[...] = acc_ref[...].astype(o_ref.dtype)

    return body, [pltpu.VMEM(tile, F32)]


def _mm_nn(a, b3, out_dtype, name, tm=1024, tn=1408, tk=2048):
    M, K = a.shape
    C, _, n = b3.shape
    tm, tk, tn = _tile(M, tm, 8), _tile(K, tk), _tile(n, tn)
    npc, nk = n // tn, K // tk
    body, scratch = _mm_body(NN, nk, (tm, tn))
    return _pcall(
        body, grid=(M // tm, C * npc, nk),
        in_specs=[pl.BlockSpec((tm, tk), lambda i, j, k: (i, k)),
                  pl.BlockSpec((None, tk, tn), lambda i, j, k: (j // npc, k, j % npc))],
        out_specs=pl.BlockSpec((tm, tn), lambda i, j, k: (i, j)),
        out_shape=jax.ShapeDtypeStruct((M, C * n), out_dtype), scratch_shapes=scratch,
        compiler_params=_params("parallel", "parallel", "arbitrary"), name=name)(a, b3)


def _mm_nt(a, b3, out_dtype, name, tm=1024, tn=1024, tk=2048):
    M, _ = a.shape
    C, N, n = b3.shape
    tm, tn, tk = _tile(M, tm, 8), _tile(N, tn), _tile(n, tk)
    kpc = n // tk
    nk = C * kpc
    body, scratch = _mm_body(NT, nk, (tm, tn))
    return _pcall(
        body, grid=(M // tm, N // tn, nk),
        in_specs=[pl.BlockSpec((tm, tk), lambda i, j, k: (i, k)),
                  pl.BlockSpec((None, tn, tk), lambda i, j, k: (k // kpc, j, k % kpc))],
        out_specs=pl.BlockSpec((tm, tn), lambda i, j, k: (i, j)),
        out_shape=jax.ShapeDtypeStruct((M, N), out_dtype), scratch_shapes=scratch,
        compiler_params=_params("parallel", "parallel", "arbitrary"), name=name)(a, b3)


def _mm_tn(x, y, n, out_dtype, name, tm=1024, tn=1408, tk=2048):
    S, P = x.shape
    C = y.shape[1] // n
    tm, tn, tk = _tile(P, tm), _tile(n, tn), _tile(S, tk, 8)
    npc, nk = n // tn, S // tk
    body, scratch = _mm_body(TN, nk, (tm, tn))
    return _pcall(
        body, grid=(P // tm, C * npc, nk),
        in_specs=[pl.BlockSpec((tk, tm), lambda i, j, k: (k, i)),
                  pl.BlockSpec((tk, tn), lambda i, j, k: (k, j))],
        out_specs=pl.BlockSpec((None, tm, tn), lambda i, j, k: (j // npc, i, j % npc)),
        out_shape=jax.ShapeDtypeStruct((C, P, n), out_dtype), scratch_shapes=scratch,
        compiler_params=_params("parallel", "parallel", "arbitrary"), name=name)(x, y)


def _rms_scale(v):
    return lax.rsqrt(jnp.mean(v * v, axis=-1, keepdims=True) + RMS_EPS)


def _rms_bwd(gy, v, r):
    return r * gy - v * (r * r * r * jnp.mean(gy * v, axis=-1, keepdims=True))


def _rows_spec(tm, d):
    return pl.BlockSpec((tm, d), lambda i: (i, 0))


def _vec_spec(d):
    return pl.BlockSpec((1, d), lambda i: (0, 0))


def _rms_fwd(x, g, name, tm=256):
    S, D = x.shape

    def body(x_ref, g_ref, h_ref):
        v = x_ref[...]
        h_ref[...] = (v * _rms_scale(v) * g_ref[...]).astype(BF16)

    return _pcall(body, grid=(S // tm,), in_specs=[_rows_spec(tm, D), _vec_spec(D)], out_specs=_rows_spec(tm, D),
                  out_shape=jax.ShapeDtypeStruct((S, D), BF16), compiler_params=_params("parallel"), name=name)(x, g)


def _mid_fwd(x, mix, g_post, g_pre, name, tm=256):
    S, D = x.shape

    def body(x_ref, m_ref, gp_ref, gn_ref, x2_ref, h_ref):
        m = m_ref[...]
        x2 = x_ref[...] + m * _rms_scale(m) * gp_ref[...]
        x2_ref[...] = x2
        h_ref[...] = (x2 * _rms_scale(x2) * gn_ref[...]).astype(BF16)

    return _pcall(body, grid=(S // tm,), in_specs=[_rows_spec(tm, D), _rows_spec(tm, D), _vec_spec(D), _vec_spec(D)],
                  out_specs=[_rows_spec(tm, D), _rows_spec(tm, D)],
                  out_shape=[jax.ShapeDtypeStruct((S, D), F32), jax.ShapeDtypeStruct((S, D), BF16)],
                  compiler_params=_params("parallel"), name=name)(x, mix, g_post, g_pre)


def _loss_bwd(x2, f, tgt, g_post, name, tm=256):
    S, D = x2.shape

    def body(x2_ref, f_ref, t_ref, g_ref, dy_ref, df_ref, dg_ref, ls_ref):
        i = pl.program_id(0)

        @pl.when(i == 0)
        def _():
            dg_ref[...] = jnp.zeros_like(dg_ref)
            ls_ref[...] = jnp.zeros_like(ls_ref)

        fv = f_ref[...]
        r = _rms_scale(fv)
        g = g_ref[...]
        err = x2_ref[...] + fv * r * g - t_ref[...]
        ls_ref[...] += jnp.broadcast_to(0.5 * jnp.sum(jnp.mean(err * err, axis=-1, keepdims=True), axis=0, keepdims=True), ls_ref.shape)
        dy = err * (1.0 / D)
        dy_ref[...] = dy
        df_ref[...] = _rms_bwd(dy * g, fv, r).astype(BF16)
        dg_ref[...] += jnp.sum(dy * fv * r, axis=0, keepdims=True)

    return _pcall(body, grid=(S // tm,),
                  in_specs=[_rows_spec(tm, D), _rows_spec(tm, D), _rows_spec(tm, D), _vec_spec(D)],
                  out_specs=[_rows_spec(tm, D), _rows_spec(tm, D), _vec_spec(D), _vec_spec(LANES)],
                  out_shape=[jax.ShapeDtypeStruct((S, D), F32), jax.ShapeDtypeStruct((S, D), BF16),
                             jax.ShapeDtypeStruct((1, D), F32), jax.ShapeDtypeStruct((1, LANES), F32)],
                  compiler_params=_params("arbitrary"), name=name)(x2, f, tgt, g_post)


def _mid_bwd(dy, dh2, x2, mix, g_pre, g_post, name, tm=256):
    S, D = dy.shape

    def body(dy_ref, dh_ref, x2_ref, m_ref, gn_ref, gp_ref, dx2_ref, dm_ref, dgn_ref, dgp_ref):
        i = pl.program_id(0)

        @pl.when(i == 0)
        def _():
            dgn_ref[...] = jnp.zeros_like(dgn_ref)
            dgp_ref[...] = jnp.zeros_like(dgp_ref)

        x2, dh = x2_ref[...], dh_ref[...]
        r = _rms_scale(x2)
        dx2 = dy_ref[...] + _rms_bwd(dh * gn_ref[...], x2, r)
        dgn_ref[...] += jnp.sum(dh * x2 * r, axis=0, keepdims=True)
        dx2_ref[...] = dx2
        m = m_ref[...]
        rm = _rms_scale(m)
        dm_ref[...] = _rms_bwd(dx2 * gp_ref[...], m, rm).astype(BF16)
        dgp_ref[...] += jnp.sum(dx2 * m * rm, axis=0, keepdims=True)

    return _pcall(body, grid=(S // tm,),
                  in_specs=[_rows_spec(tm, D)] * 4 + [_vec_spec(D)] * 2,
                  out_specs=[_rows_spec(tm, D), _rows_spec(tm, D), _vec_spec(D), _vec_spec(D)],
                  out_shape=[jax.ShapeDtypeStruct((S, D), F32), jax.ShapeDtypeStruct((S, D), BF16),
                             jax.ShapeDtypeStruct((1, D), F32), jax.ShapeDtypeStruct((1, D), F32)],
                  compiler_params=_params("arbitrary"), name=name)(dy, dh2, x2, mix, g_pre, g_post)


def _first_bwd(dx2, dh1, x, g_pre, name, tm=256):
    S, D = x.shape

    def body(dx2_ref, dh_ref, x_ref, g_ref, gx_ref, dg_ref):
        i = pl.program_id(0)

        @pl.when(i == 0)
        def _():
            dg_ref[...] = jnp.zeros_like(dg_ref)

        xv, dh = x_ref[...], dh_ref[...]
        r = _rms_scale(xv)
        gx_ref[...] = dx2_ref[...] + _rms_bwd(dh * g_ref[...], xv, r)
        dg_ref[...] += jnp.sum(dh * xv * r, axis=0, keepdims=True)

    return _pcall(body, grid=(S // tm,), in_specs=[_rows_spec(tm, D)] * 3 + [_vec_spec(D)],
                  out_specs=[_rows_spec(tm, D), _vec_spec(D)],
                  out_shape=[jax.ShapeDtypeStruct((S, D), F32), jax.ShapeDtypeStruct((1, D), F32)],
                  compiler_params=_params("arbitrary"), name=name)(dx2, dh1, x, g_pre)


def _logsig_pair(z):
    sp = jnp.log(1.0 + jnp.exp(-jnp.abs(z)))
    return jnp.minimum(z, 0.0) - sp, jnp.minimum(-z, 0.0) - sp


def _split_dot(v, u):
    hi = v.astype(BF16)
    lo = (v - hi.astype(F32)).astype(BF16)
    return _dot(hi, u, NN) + _dot(lo, u, NN)


def _head_out(o, g):
    return o * _rms_scale(o) * g


def _sb_fwd(proj, gain, n_heads, name, tq=512):
    S = proj.shape[0]
    H, tk = n_heads, KEY_BLOCK
    tq = _tile(S, tq, tk)
    scale = HEAD_DIM ** -0.5

    def body(q_ref, k_ref, v_ref, g_ref, o_ref, ct_ref, mx_ref, oacc, cacc):
        i = pl.program_id(1)
        q = q_ref[...].astype(BF16)
        oacc[...] = jnp.zeros_like(oacc)
        cacc[...] = jnp.zeros_like(cacc)
        row = i * tq + lax.broadcasted_iota(jnp.int32, (tq, tk), 0)
        col = lax.broadcasted_iota(jnp.int32, (tq, tk), 1)
        later = (lax.broadcasted_iota(jnp.int32, (tk, tk), 0) > lax.broadcasted_iota(jnp.int32, (tk, tk), 1)).astype(BF16)
        nkb = (i + 1) * (tq // tk)

        def step(it, carry):
            k0 = pl.multiple_of((nkb - 1 - it) * tk, tk)
            kj = k_ref[pl.ds(k0, tk), :].astype(BF16)
            vj = v_ref[pl.ds(k0, tk), :].astype(BF16)
            z = _dot(q, kj, NT) * scale
            causal = (col + k0) < row
            lb, lk = _logsig_pair(z)
            lk = jnp.where(causal, lk, 0.0)
            c = cacc[...]
            a = jnp.where(causal, jnp.exp(lb + _split_dot(lk, later) + c), 0.0)
            oacc[...] += _dot(a.astype(BF16), vj, NN)
            cacc[...] = c + jnp.sum(lk, axis=1, keepdims=True)
            return carry

        lax.fori_loop(0, nkb, step, 0)
        o = oacc[...]
        o_ref[...] = o
        ct_ref[...] = jnp.broadcast_to(cacc[...], (tq, LANES))
        mx_ref[...] = _head_out(o, g_ref[...]).astype(BF16)

    blk = pl.BlockSpec((tq, HEAD_DIM), lambda h, i: (i, h))
    return _pcall(
        body, grid=(H, S // tq),
        in_specs=[blk, pl.BlockSpec((S, HEAD_DIM), lambda h, i: (0, H + h)),
                  pl.BlockSpec((S, HEAD_DIM), lambda h, i: (0, 2 * H + h)), pl.BlockSpec((1, HEAD_DIM), lambda h, i: (0, h))],
        out_specs=[blk, blk, blk],
        out_shape=[jax.ShapeDtypeStruct((S, H * HEAD_DIM), F32), jax.ShapeDtypeStruct((S, H * HEAD_DIM), F32),
                   jax.ShapeDtypeStruct((S, H * HEAD_DIM), BF16)],
        scratch_shapes=[pltpu.VMEM((tq, HEAD_DIM), F32), pltpu.VMEM((tq, 1), F32)],
        compiler_params=_params("parallel", "arbitrary"), name=name)(proj, proj, proj, gain)


def _sb_bwd(proj, gain, o_raw, ctot, dmixed, dm_col0, n_heads, name, tq=512):
    S = proj.shape[0]
    H, tk = n_heads, KEY_BLOCK
    tq = _tile(S, tq, tk)
    nq = S // tq
    scale = HEAD_DIM ** -0.5

    def body(q_ref, k_ref, v_ref, g_ref, o_ref, ct_ref, dm_ref, dq_ref, dk_ref, dv_ref, dg_ref,
             dkacc, dvacc, dqacc, pfx, gcar):
        i = pl.program_id(1)

        @pl.when(i == 0)
        def _():
            dkacc[...] = jnp.zeros_like(dkacc)
            dvacc[...] = jnp.zeros_like(dvacc)
            dg_ref[...] = jnp.zeros_like(dg_ref)

        o, dm, g = o_ref[...], dm_ref[...], g_ref[...]
        r = _rms_scale(o)
        do = _rms_bwd(dm * g, o, r).astype(BF16)
        dg_ref[...] += jnp.broadcast_to(jnp.sum(dm * o * r, axis=0, keepdims=True), dg_ref.shape)
        q = q_ref[...].astype(BF16)
        ct = ct_ref[:, 0:1]
        dqacc[...] = jnp.zeros_like(dqacc)
        pfx[...] = jnp.zeros_like(pfx)
        gcar[...] = jnp.zeros_like(gcar)
        row = i * tq + lax.broadcasted_iota(jnp.int32, (tq, tk), 0)
        col = lax.broadcasted_iota(jnp.int32, (tq, tk), 1)
        ia, ib = lax.broadcasted_iota(jnp.int32, (tk, tk), 0), lax.broadcasted_iota(jnp.int32, (tk, tk), 1)
        later = (ia > ib).astype(BF16)
        earlier = (ia < ib).astype(BF16)
        nkb = (i + 1) * (tq // tk)

        def step(j, carry):
            k0 = pl.multiple_of(j * tk, tk)
            kj = k_ref[pl.ds(k0, tk), :].astype(BF16)
            vj = v_ref[pl.ds(k0, tk), :].astype(BF16)
            z = _dot(q, kj, NT) * scale
            causal = (col + k0) < row
            lb, lk = _logsig_pair(z)
            lk = jnp.where(causal, lk, 0.0)
            rs = jnp.sum(lk, axis=1, keepdims=True)
            p = pfx[...]
            a = jnp.where(causal, jnp.exp(lb + _split_dot(lk, later) + (ct - p - rs)), 0.0)
            dl = _dot(do, vj, NT) * a
            dvacc[pl.ds(k0, tk), :] += _dot(a.astype(BF16), do, TN)
            gc = gcar[...]
            gsum = _split_dot(dl, earlier) + gc
            sig = jnp.exp(lb)
            dz = ((dl * (1.0 - sig) - jnp.where(causal, gsum * sig, 0.0)) * scale).astype(BF16)
            dqacc[...] += _dot(dz, kj, NN)
            dkacc[pl.ds(k0, tk), :] += _dot(dz, q, TN)
            pfx[...] = p + rs
            gcar[...] = gc + jnp.sum(dl, axis=1, keepdims=True)
            return carry

        lax.fori_loop(0, nkb, step, 0)
        dq_ref[...] = dqacc[...].astype(BF16)

        @pl.when(i == nq - 1)
        def _():
            dk_ref[...] = dkacc[...].astype(BF16)
            dv_ref[...] = dvacc[...].astype(BF16)

    blk = pl.BlockSpec((tq, HEAD_DIM), lambda h, i: (i, h))
    full = pl.BlockSpec((S, HEAD_DIM), lambda h, i: (0, h))
    W = H * HEAD_DIM
    return _pcall(
        body, grid=(H, nq),
        in_specs=[blk, pl.BlockSpec((S, HEAD_DIM), lambda h, i: (0, H + h)),
                  pl.BlockSpec((S, HEAD_DIM), lambda h, i: (0, 2 * H + h)), pl.BlockSpec((1, HEAD_DIM), lambda h, i: (0, h)),
                  blk, blk, pl.BlockSpec((tq, HEAD_DIM), lambda h, i: (i, dm_col0 + h))],
        out_specs=[blk, full, full, pl.BlockSpec((8, HEAD_DIM), lambda h, i: (0, h))],
        out_shape=[jax.ShapeDtypeStruct((S, W), BF16), jax.ShapeDtypeStruct((S, W), BF16),
                   jax.ShapeDtypeStruct((S, W), BF16), jax.ShapeDtypeStruct((8, W), F32)],
        scratch_shapes=[pltpu.VMEM((S, HEAD_DIM), F32), pltpu.VMEM((S, HEAD_DIM), F32), pltpu.VMEM((tq, HEAD_DIM), F32),
                        pltpu.VMEM((tq, 1), F32), pltpu.VMEM((tq, 1), F32)],
        compiler_params=_params("arbitrary", "arbitrary"), name=name)(proj, proj, proj, gain, o_raw, ctot, dmixed)


def _rope_tables(S):
    inv_freq = ROPE_THETA ** (-jnp.arange(0, HEAD_DIM, 2, dtype=F32) / HEAD_DIM)
    ang = jnp.arange(S, dtype=F32)[:, None] * inv_freq[None, :]
    cos, sin = jnp.cos(ang), jnp.sin(ang)
    return jnp.concatenate([cos, cos], axis=1), jnp.concatenate([-sin, sin], axis=1)


def _rope(v, cos2, sin_signed):
    return v * cos2 + pltpu.roll(v, HEAD_DIM // 2, axis=1) * sin_signed


def _dil_rows(d, r, l0, n):
    if d == 1:
        return pl.ds(l0 if isinstance(l0, int) else pl.multiple_of(l0, KEY_BLOCK), n)
    return pl.ds(r + d * l0, n, stride=d)


def _dil_blocks(S, visit):
    B = KEY_BLOCK
    for b, d in enumerate(DILATIONS):
        nb = S // d // B

        def per_residue(r, carry, b=b, d=d, nb=nb):
            visit(b, d, r, 0, True)
            if nb > 1:
                def per_block(n, c2):
                    visit(b, d, r, n * B, False)
                    return c2
                lax.fori_loop(1, nb, per_block, 0)
            return carry

        if d == 1:
            per_residue(0, 0)
        else:
            lax.fori_loop(0, d, per_residue, 0)


def _dil_mask(first):
    B = KEY_BLOCK
    nk = B if first else 2 * B
    iq = lax.broadcasted_iota(jnp.int32, (B, nk), 0)
    ik = lax.broadcasted_iota(jnp.int32, (B, nk), 1)
    return (ik <= iq) if first else ((ik >= iq) & (ik <= iq + B))


def _dil_fwd(proj, cos2, sin_signed, gain, col0, n_heads, name):
    S = proj.shape[0]
    H, B = n_heads, KEY_BLOCK
    scale = HEAD_DIM ** -0.5
    rc = _tile(S, 256, 8)

    def body(q_ref, k_ref, v_ref, c_ref, s_ref, g_ref, o_ref, l_ref, mx_ref, qr, kr, *per_branch):
        ob, lb = per_branch[:len(DILATIONS)], per_branch[len(DILATIONS):]

        def rope_rows(t, carry):
            rows = pl.ds(pl.multiple_of(t * rc, rc), rc)
            qr[rows, :] = _rope(q_ref[rows, :], c_ref[rows, :], s_ref[rows, :])
            kr[rows, :] = _rope(k_ref[rows, :], c_ref[rows, :], s_ref[rows, :])
            return carry

        lax.fori_loop(0, S // rc, rope_rows, 0)

        def visit(b, d, r, l0, first):
            nk = B if first else 2 * B
            qrows = _dil_rows(d, r, l0, B)
            krows = qrows if first else _dil_rows(d, r, l0 - B, nk)
            s = _dot(qr[qrows, :].astype(BF16), kr[krows, :].astype(BF16), NT) * scale
            s = jnp.where(_dil_mask(first), s, NEG)
            m = jnp.max(s, axis=1, keepdims=True)
            p = jnp.exp(s - m)
            den = jnp.sum(p, axis=1, keepdims=True)
            ob[b][qrows, :] = _dot(p.astype(BF16), v_ref[krows, :].astype(BF16), NN) / den
            lb[b][qrows, :] = jnp.broadcast_to(m + jnp.log(den), (B, LANES))

        _dil_blocks(S, visit)

        def combine(t, carry):
            rows = pl.ds(pl.multiple_of(t * rc, rc), rc)
            l0, l1, l2 = lb[0][rows, :], lb[1][rows, :], lb[2][rows, :]
            m = jnp.maximum(jnp.maximum(l0, l1), l2)
            w0, w1, w2 = jnp.exp(l0 - m), jnp.exp(l1 - m), jnp.exp(l2 - m)
            den = w0 + w1 + w2
            o = (w0 * ob[0][rows, :] + w1 * ob[1][rows, :] + w2 * ob[2][rows, :]) / den
            o_ref[rows, :] = o
            l_ref[rows, :] = m + jnp.log(den)
            mx_ref[rows, :] = _head_out(o, g_ref[...]).astype(BF16)
            return carry

        lax.fori_loop(0, S // rc, combine, 0)

    def col(k):
        return pl.BlockSpec((S, HEAD_DIM), lambda h: (0, col0 + k * H + h))

    tab = pl.BlockSpec((S, HEAD_DIM), lambda h: (0, 0))
    out = pl.BlockSpec((S, HEAD_DIM), lambda h: (0, h))
    W = H * HEAD_DIM
    return _pcall(
        body, grid=(H,),
        in_specs=[col(0), col(1), col(2), tab, tab, pl.BlockSpec((1, HEAD_DIM), lambda h: (0, h))],
        out_specs=[out, out, out],
        out_shape=[jax.ShapeDtypeStruct((S, W), F32), jax.ShapeDtypeStruct((S, W), F32), jax.ShapeDtypeStruct((S, W), BF16)],
        scratch_shapes=[pltpu.VMEM((S, HEAD_DIM), F32)] * (2 + 2 * len(DILATIONS)),
        compiler_params=_params("parallel"), name=name)(proj, proj, proj, cos2, sin_signed, gain)


def _dil_bwd(proj, cos2, sin_signed, gain, o_raw, lse, dmixed, dm_col0, col0, n_heads, name):
    S = proj.shape[0]
    H, B = n_heads, KEY_BLOCK
    scale = HEAD_DIM ** -0.5
    rc = _tile(S, 256, 8)

    def body(q_ref, k_ref, v_ref, c_ref, s_ref, g_ref, o_ref, l_ref, dm_ref, dq_ref, dk_ref, dv_ref, dg_ref,
             qr, kr, dos, dsum, dqr, dkr, dvv):
        dg_ref[...] = jnp.zeros_like(dg_ref)

        def prep(t, carry):
            rows = pl.ds(pl.multiple_of(t * rc, rc), rc)
            qr[rows, :] = _rope(q_ref[rows, :], c_ref[rows, :], s_ref[rows, :])
            kr[rows, :] = _rope(k_ref[rows, :], c_ref[rows, :], s_ref[rows, :])
            o, dm = o_ref[rows, :], dm_ref[rows, :]
            r = _rms_scale(o)
            do = _rms_bwd(dm * g_ref[...], o, r)
            dg_ref[...] += jnp.broadcast_to(jnp.sum(dm * o * r, axis=0, keepdims=True), dg_ref.shape)
            dos[rows, :] = do
            dsum[rows, :] = jnp.broadcast_to(jnp.sum(do * o, axis=1, keepdims=True), (rc, LANES))
            dqr[rows, :] = jnp.zeros((rc, HEAD_DIM), F32)
            dkr[rows, :] = jnp.zeros((rc, HEAD_DIM), F32)
            dvv[rows, :] = jnp.zeros((rc, HEAD_DIM), F32)
            return carry

        lax.fori_loop(0, S // rc, prep, 0)

        def visit(b, d, r, l0, first):
            nk = B if first else 2 * B
            qrows = _dil_rows(d, r, l0, B)
            krows = qrows if first else _dil_rows(d, r, l0 - B, nk)
            qs, ks = qr[qrows, :].astype(BF16), kr[krows, :].astype(BF16)
            do = dos[qrows, :].astype(BF16)
            s = _dot(qs, ks, NT) * scale
            s = jnp.where(_dil_mask(first), s, NEG)
            p = jnp.exp(s - l_ref[qrows, :][:, 0:1])
            dp = _dot(do, v_ref[krows, :].astype(BF16), NT)
            ds = (p * (dp - dsum[qrows, :][:, 0:1]) * scale).astype(BF16)
            dqr[qrows, :] += _dot(ds, ks, NN)
            dkr[krows, :] += _dot(ds, qs, TN)
            dvv[krows, :] += _dot(p.astype(BF16), do, TN)

        _dil_blocks(S, visit)

        def finish(t, carry):
            rows = pl.ds(pl.multiple_of(t * rc, rc), rc)
            c, s = c_ref[rows, :], s_ref[rows, :]
            dq, dk = dqr[rows, :], dkr[rows, :]
            dq_ref[rows, :] = (dq * c + pltpu.roll(dq * s, HEAD_DIM // 2, axis=1)).astype(BF16)
            dk_ref[rows, :] = (dk * c + pltpu.roll(dk * s, HEAD_DIM // 2, axis=1)).astype(BF16)
            dv_ref[rows, :] = dvv[rows, :].astype(BF16)
            return carry

        lax.fori_loop(0, S // rc, finish, 0)

    def col(k):
        return pl.BlockSpec((S, HEAD_DIM), lambda h: (0, col0 + k * H + h))

    tab = pl.BlockSpec((S, HEAD_DIM), lambda h: (0, 0))
    out = pl.BlockSpec((S, HEAD_DIM), lambda h: (0, h))
    W = H * HEAD_DIM
    big = pltpu.VMEM((S, HEAD_DIM), F32)
    return _pcall(
        body, grid=(H,),
        in_specs=[col(0), col(1), col(2), tab, tab, pl.BlockSpec((1, HEAD_DIM), lambda h: (0, h)), out, out,
                  pl.BlockSpec((S, HEAD_DIM), lambda h: (0, dm_col0 + h))],
        out_specs=[out, out, out, pl.BlockSpec((8, HEAD_DIM), lambda h: (0, h))],
        out_shape=[jax.ShapeDtypeStruct((S, W), BF16), jax.ShapeDtypeStruct((S, W), BF16),
                   jax.ShapeDtypeStruct((S, W), BF16), jax.ShapeDtypeStruct((8, W), F32)],
        scratch_shapes=[big, big, big, pltpu.VMEM((S, LANES), F32), big, big, big],
        compiler_params=_params("parallel"), name=name)(proj, proj, proj, cos2, sin_signed, gain, o_raw, lse, dmixed)


GELU_C = math.sqrt(2.0 / math.pi)
GELU_A = 0.044715
HALO = 16


def _shift_down(cur, halo, k):
    out = pltpu.roll(cur, k, axis=0)
    row = lax.broadcasted_iota(jnp.int32, cur.shape, 0)
    for t in range(k):
        out = jnp.where(row == t, halo[HALO - k + t:HALO - k + t + 1, :], out)
    return out


def _shift_up(cur, halo, k):
    n = cur.shape[0]
    out = pltpu.roll(cur, n - k, axis=0)
    row = lax.broadcasted_iota(jnp.int32, cur.shape, 0)
    for t in range(k):
        out = jnp.where(row == n - k + t, halo[t:t + 1, :], out)
    return out


def _conv3(cur, halo, cw):
    return _shift_down(cur, halo, 2) * cw[0:1, :] + _shift_down(cur, halo, 1) * cw[1:2, :] + cur * cw[2:3, :] + cw[3:4, :]


def _gelu_parts(x):
    t = jnp.tanh(GELU_C * (x + GELU_A * x * x * x))
    return 0.5 * x * (1.0 + t), t


def _geglu_specs(tm, tn, ncb):
    hb = tm // HALO

    def cur(off):
        return pl.BlockSpec((tm, tn), lambda j, i: (i, off + j))

    def prev(off):
        return pl.BlockSpec((HALO, tn), lambda j, i: (jnp.maximum(i * hb - 1, 0), off + j))

    def taps(off):
        return pl.BlockSpec((8, tn), lambda j, i: (0, off + j))

    return [cur(0), prev(0), cur(ncb), prev(ncb), taps(0), taps(ncb)]


def _geglu_fwd(u, cwb, name, tm=256, tn=512):
    S, F2 = u.shape
    F = F2 // 2
    tm, tn = _tile(S, tm, HALO), _tile(F, tn)
    ncb = F // tn

    def body(g_ref, gp_ref, v_ref, vp_ref, cg_ref, cv_ref, y_ref):
        top = pl.program_id(1) > 0
        gp = jnp.where(top, gp_ref[...].astype(F32), 0.0)
        vp = jnp.where(top, vp_ref[...].astype(F32), 0.0)
        gc = _conv3(g_ref[...].astype(F32), gp, cg_ref[...])
        vc = _conv3(v_ref[...].astype(F32), vp, cv_ref[...])
        y_ref[...] = (_gelu_parts(gc)[0] * vc).astype(BF16)

    return _pcall(body, grid=(ncb, S // tm), in_specs=_geglu_specs(tm, tn, ncb),
                  out_specs=pl.BlockSpec((tm, tn), lambda j, i: (i, j)),
                  out_shape=jax.ShapeDtypeStruct((S, F), BF16),
                  compiler_params=_params("parallel", "parallel"), name=name)(u, u, u, u, cwb, cwb)


def _geglu_bwd(u, dy, cwb, name, tm=256, tn=512):
    S, F2 = u.shape
    F = F2 // 2
    tm, tn = _tile(S, tm, HALO), _tile(F, tn)
    ncb = F // tn

    def body(g_ref, gp_ref, v_ref, vp_ref, cg_ref, cv_ref, dy_ref, dc_ref, dwg_ref, dwv_ref):
        i = pl.program_id(1)

        @pl.when(i == 0)
        def _():
            dwg_ref[...] = jnp.zeros_like(dwg_ref)
            dwv_ref[...] = jnp.zeros_like(dwv_ref)

        top = i > 0
        g, v = g_ref[...].astype(F32), v_ref[...].astype(F32)
        gp = jnp.where(top, gp_ref[...].astype(F32), 0.0)
        vp = jnp.where(top, vp_ref[...].astype(F32), 0.0)
        gc = _conv3(g, gp, cg_ref[...])
        vc = _conv3(v, vp, cv_ref[...])
        act, t = _gelu_parts(gc)
        dact = 0.5 * (1.0 + t) + 0.5 * gc * (1.0 - t * t) * GELU_C * (1.0 + 3.0 * GELU_A * gc * gc)
        dyv = dy_ref[...].astype(F32)
        dgc = dyv * vc * dact
        dvc = dyv * act
        dc_ref[0] = dgc.astype(BF16)
        dc_ref[1] = dvc.astype(BF16)

        def taps(out_ref, dc, cur, halo):
            out_ref[0:1, :] += jnp.sum(dc * _shift_down(cur, halo, 2), axis=0, keepdims=True)
            out_ref[1:2, :] += jnp.sum(dc * _shift_down(cur, halo, 1), axis=0, keepdims=True)
            out_ref[2:3, :] += jnp.sum(dc * cur, axis=0, keepdims=True)
            out_ref[3:4, :] += jnp.sum(dc, axis=0, keepdims=True)

        taps(dwg_ref, dgc, g, gp)
        taps(dwv_ref, dvc, v, vp)

    return _pcall(body, grid=(ncb, S // tm),
                  in_specs=_geglu_specs(tm, tn, ncb) + [pl.BlockSpec((tm, tn), lambda j, i: (i, j))],
                  out_specs=[pl.BlockSpec((2, tm, tn), lambda j, i: (0, i, j)),
                             pl.BlockSpec((8, tn), lambda j, i: (0, j)), pl.BlockSpec((8, tn), lambda j, i: (0, j))],
                  out_shape=[jax.ShapeDtypeStruct((2, S, F), BF16), jax.ShapeDtypeStruct((8, F), F32),
                             jax.ShapeDtypeStruct((8, F), F32)],
                  compiler_params=_params("parallel", "arbitrary"), name=name)(u, u, u, u, cwb, cwb, dy)


def _conv_bwd(dc, cwb, name, tm=256, tn=512):
    _, S, F = dc.shape
    tm, tn = _tile(S, tm, HALO), _tile(F, tn)
    ncb, nrb = F // tn, S // tm
    hb = tm // HALO

    def body(c_ref, n_ref, w_ref, du_ref):
        cur = c_ref[...].astype(F32)
        nxt = jnp.where(pl.program_id(2) < nrb - 1, n_ref[...].astype(F32), 0.0)
        w = w_ref[...]
        du = cur * w[2:3, :] + _shift_up(cur, nxt, 1) * w[1:2, :] + _shift_up(cur, nxt, 2) * w[0:1, :]
        du_ref[...] = du.astype(BF16)

    return _pcall(body, grid=(2, ncb, nrb),
                  in_specs=[pl.BlockSpec((None, tm, tn), lambda c, j, i: (c, i, j)),
                            pl.BlockSpec((None, HALO, tn), lambda c, j, i: (c, jnp.minimum((i + 1) * hb, S // HALO - 1), j)),
                            pl.BlockSpec((8, tn), lambda c, j, i: (0, c * ncb + j))],
                  out_specs=pl.BlockSpec((tm, tn), lambda c, j, i: (i, c * ncb + j)),
                  out_shape=jax.ShapeDtypeStruct((S, 2 * F), BF16),
                  compiler_params=_params("parallel", "parallel", "parallel"), name=name)(dc, dc, cwb)


def _adam_math(w, g, m, v):
    m = ADAM_B1 * m + (1.0 - ADAM_B1) * g
    v = ADAM_B2 * v + (1.0 - ADAM_B2) * (g * g)
    m_hat = m / (1.0 - ADAM_B1 ** ADAM_STEP)
    v_hat = v / (1.0 - ADAM_B2 ** ADAM_STEP)
    return -ADAM_LR * (m_hat / (jnp.sqrt(v_hat) + ADAM_EPS) + ADAM_WD * w), m, v


def _adamw(w, parts, m, v, name, tr=256):
    R, C = w.shape
    n, _, Cp = parts.shape
    tr = _tile(R, tr, 8)

    def body(w_ref, p_ref, m_ref, v_ref, g_out, d_out, m_out, v_out):
        g = p_ref[0, :, 0:C].astype(F32)
        for k in range(1, n):
            g = g + p_ref[k, :, 0:C].astype(F32)
        d, mn, vn = _adam_math(w_ref[...], g, m_ref[...], v_ref[...])
        g_out[...] = g
        d_out[...] = d
        m_out[...] = mn
        v_out[...] = vn

    spec = pl.BlockSpec((tr, C), lambda i: (i, 0))
    shape = jax.ShapeDtypeStruct((R, C), F32)
    return _pcall(body, grid=(R // tr,), in_specs=[spec, pl.BlockSpec((n, tr, Cp), lambda i: (0, i, 0)), spec, spec],
                  out_specs=[spec] * 4, out_shape=[shape] * 4, compiler_params=_params("parallel"), name=name)(w, parts, m, v)


def _adamw_chips(w, pair, parts, chip_ids, m, v, name, tr=256):
    R, C = w.shape
    Cp = pair.shape[2]
    tr = _tile(R, tr, 16)

    def body(ids_ref, w_ref, own_ref, p1_ref, p2_ref, p3_ref, m_ref, v_ref, g_out, d_out, m_out, v_out):
        g = own_ref[:, 0:C].astype(F32)
        for ref in (p1_ref, p2_ref, p3_ref):
            g = g + ref[:, 0:C].astype(F32)
        d, mn, vn = _adam_math(w_ref[...], g, m_ref[...], v_ref[...])
        g_out[...] = g
        d_out[...] = d
        m_out[...] = mn
        v_out[...] = vn

    spec = pl.BlockSpec((tr, C), lambda i, ids: (i, 0))

    def chip(k):
        return pl.BlockSpec((None, tr, Cp), lambda i, ids: (ids[k], i, 0))

    shape = jax.ShapeDtypeStruct((R, C), F32)
    grid_spec = pltpu.PrefetchScalarGridSpec(
        num_scalar_prefetch=1, grid=(R // tr,), in_specs=[spec, chip(0), chip(1), chip(2), chip(3), spec, spec],
        out_specs=[spec] * 4)
    return _pcall(body, grid_spec=grid_spec, out_shape=[shape] * 4, compiler_params=_params("parallel"),
                  name=name)(chip_ids, w, pair, parts, parts, parts, m, v)


def _place():
    return lax.axis_index("x"), lax.axis_index("y"), lax.axis_index("c")


def _other_chips(x, y):
    return [(1 - x, y), (x, 1 - y), (1 - x, 1 - y)]


IN_HBM = pl.BlockSpec(memory_space=pltpu.HBM)
SEM = pl.BlockSpec(memory_space=pltpu.SEMAPHORE)
EFFECT = pltpu.SideEffectType.DATAFLOW_SIDE_EFFECTING
TOKEN = jax.ShapeDtypeStruct((8, LANES), F32)
TOKEN_SPEC = pl.BlockSpec(memory_space=pltpu.VMEM)


def _in_hbm(a):
    return pltpu.with_memory_space_constraint(a, pltpu.HBM)


def _landing(shape):
    return _in_hbm(lax.empty(shape.shape, shape.dtype))


def _hbm_like(a):
    return pltpu.HBM(a.shape, a.dtype)


def _gather_start(landing, slots, after, name):
    na = len(landing)

    def body(*refs):
        land = refs[:na]
        send_sems, recv_sems = refs[na + 1], refs[na + 2]
        token = refs[-1]
        x, y, c = _place()
        for a in range(na):
            own = slots[a](land[a], x, y, c)
            for k, to in enumerate([(x, y, 1 - c)] + [(*chip, c) for chip in _other_chips(x, y)]):
                pltpu.make_async_remote_copy(
                    src_ref=own, dst_ref=own, send_sem=send_sems.at[4 * a + k],
                    recv_sem=recv_sems.at[4 * a + k], device_id=to, device_id_type=MESH).start()
        token[...] = jnp.zeros_like(token)

    sems = pltpu.SemaphoreType.DMA((4 * na,))
    outs = _pcall(
        body, in_specs=[IN_HBM] * na + [HBM],
        out_specs=[SEM, SEM] + [IN_HBM] * na + [TOKEN_SPEC],
        out_shape=[sems, sems] + [_hbm_like(s) for s in landing] + [TOKEN],
        input_output_aliases={a: 2 + a for a in range(na)},
        compiler_params=pltpu.CompilerParams(has_side_effects=EFFECT), name=name,
    )(*[_in_hbm(s) for s in landing], after)
    return outs[0], outs[1], outs[2:2 + na], outs[-1]


def _gather_forward(gathered, send_sems, recv_sems, slots, after, name):
    na = len(gathered)

    def body(*refs):
        gath = refs[:na]
        send1, recv1 = refs[na], refs[na + 1]
        fsend, frecv = refs[na + 3], refs[na + 4]
        token = refs[-1]
        x, y, c = _place()
        chips = _other_chips(x, y)
        for a in range(na):
            for k, peer in enumerate([(x, y, 1 - c)] + [(*chip, c) for chip in chips]):
                arrival = pltpu.make_async_remote_copy(
                    src_ref=slots[a](gath[a], x, y, c), dst_ref=slots[a](gath[a], *peer), send_sem=send1.at[4 * a + k],
                    recv_sem=recv1.at[4 * a + k], device_id=peer, device_id_type=MESH)
                arrival.wait_send()
                arrival.wait_recv()
        for a in range(na):
            for j, chip in enumerate(chips):
                view = slots[a](gath[a], *chip, c)
                pltpu.make_async_remote_copy(
                    src_ref=view, dst_ref=view, send_sem=fsend.at[3 * a + j], recv_sem=frecv.at[3 * a + j],
                    device_id=(x, y, 1 - c), device_id_type=MESH).start()
        token[...] = jnp.zeros_like(token)

    sems = pltpu.SemaphoreType.DMA((3 * na,))
    outs = _pcall(
        body, in_specs=[IN_HBM] * na + [SEM, SEM, HBM],
        out_specs=[SEM, SEM] + [IN_HBM] * na + [TOKEN_SPEC],
        out_shape=[sems, sems] + [_hbm_like(g) for g in gathered] + [TOKEN],
        input_output_aliases={a: 2 + a for a in range(na)},
        compiler_params=pltpu.CompilerParams(has_side_effects=EFFECT), name=name,
    )(*gathered, send_sems, recv_sems, after)
    return outs[0], outs[1], outs[2:2 + na], outs[-1]


def _gather_finish(gathered, fsend, frecv, slots, after, name):
    na = len(gathered)

    def body(*refs):
        gath, fs, fr = refs[:na], refs[na], refs[na + 1]
        x, y, c = _place()
        for a in range(na):
            for j, chip in enumerate(_other_chips(x, y)):
                passed = pltpu.make_async_remote_copy(
                    src_ref=slots[a](gath[a], *chip, c), dst_ref=slots[a](gath[a], *chip, 1 - c),
                    send_sem=fs.at[3 * a + j], recv_sem=fr.at[3 * a + j], device_id=(x, y, 1 - c), device_id_type=MESH)
                passed.wait_send()
                passed.wait_recv()

    outs = _pcall(
        body, in_specs=[IN_HBM] * na + [SEM, SEM, HBM], out_specs=[IN_HBM] * na,
        out_shape=[_hbm_like(g) for g in gathered], input_output_aliases={a: a for a in range(na)},
        compiler_params=pltpu.CompilerParams(has_side_effects=EFFECT), name=name,
    )(*gathered, fsend, frecv, after)
    return list(outs)


def _pair_exchange(grads, views, recv_shapes, name):
    na = len(grads)

    def body(*refs):
        srcs, dsts = refs[:na], refs[na:2 * na]
        send_sems, recv_sems = refs[2 * na:]
        x, y, c = _place()
        copies = []
        for a in range(na):
            for chip in range(N_CHIP):
                copies.append(pltpu.make_async_remote_copy(
                    src_ref=views[a](srcs[a], chip, 1 - c), dst_ref=dsts[a].at[chip],
                    send_sem=send_sems.at[a * N_CHIP + chip], recv_sem=recv_sems.at[a * N_CHIP + chip],
                    device_id=(x, y, 1 - c), device_id_type=MESH))
        for cp in copies:
            cp.start()
        for cp in copies:
            cp.wait()

    return _pcall(body, in_specs=[HBM] * na, out_specs=[HBM] * na, out_shape=recv_shapes,
                  scratch_shapes=[pltpu.SemaphoreType.DMA((N_CHIP * na,)), pltpu.SemaphoreType.DMA((N_CHIP * na,))],
                  name=name)(*grads)


def _chip_start(pair, after, name):
    def body(src, land, after_ref, send_sems, recv_sems, src_thru, land_thru, token):
        x, y, c = _place()
        for j, (px, py) in enumerate(_other_chips(x, y)):
            pltpu.make_async_remote_copy(
                src_ref=src.at[2 * px + py], dst_ref=land.at[2 * x + y], send_sem=send_sems.at[j], recv_sem=recv_sems.at[j],
                device_id=(px, py, c), device_id_type=MESH).start()
        token[...] = jnp.zeros_like(token)

    sems = pltpu.SemaphoreType.DMA((3,))
    return _pcall(
        body, in_specs=[IN_HBM, IN_HBM, HBM], out_specs=[SEM, SEM, IN_HBM, IN_HBM, TOKEN_SPEC],
        out_shape=[sems, sems, _hbm_like(pair), _hbm_like(pair), TOKEN], input_output_aliases={0: 2, 1: 3},
        compiler_params=pltpu.CompilerParams(has_side_effects=EFFECT), name=name,
    )(_in_hbm(pair), _landing(pair), after)


def _chip_wait(pair, parts, send_sems, recv_sems, after, name):
    def body(src, land, send, recv, after_ref, src_thru, land_thru):
        x, y, c = _place()
        for j, (px, py) in enumerate(_other_chips(x, y)):
            copy = pltpu.make_async_remote_copy(
                src_ref=src.at[2 * px + py], dst_ref=land.at[2 * px + py], send_sem=send.at[j], recv_sem=recv.at[j],
                device_id=(px, py, c), device_id_type=MESH)
            copy.wait_send()
            copy.wait_recv()

    return _pcall(
        body, in_specs=[IN_HBM, IN_HBM, SEM, SEM, HBM], out_specs=[IN_HBM, IN_HBM],
        out_shape=[_hbm_like(pair), _hbm_like(parts)], input_output_aliases={0: 0, 1: 1},
        compiler_params=pltpu.CompilerParams(has_side_effects=EFFECT), name=name,
    )(pair, parts, send_sems, recv_sems, after)


def _pair_add(core, grad, recv, block, grad_spec, name):
    _, R, C = recv.shape
    tr = block

    def body(c_ref, g_ref, r_ref, o_ref):
        o_ref[...] = (g_ref[...].astype(F32) + r_ref[...].astype(F32)).astype(BF16)

    grid_spec = pltpu.PrefetchScalarGridSpec(
        num_scalar_prefetch=1, grid=(N_CHIP, R // tr),
        in_specs=[grad_spec, pl.BlockSpec((None, tr, C), lambda k, i, c: (k, i, 0))],
        out_specs=pl.BlockSpec((None, tr, C), lambda k, i, c: (k, i, 0)))
    return _pcall(body, grid_spec=grid_spec, out_shape=jax.ShapeDtypeStruct(recv.shape, BF16),
                  compiler_params=_params("parallel", "parallel"), name=name)(core, grad, recv)


def _all_reduce_small(part, name):
    R = part.shape[0]

    def body(p_ref, o_ref, all_ref, send_sems, recv_sems):
        x, y, c = _place()
        me = 4 * x + 2 * y + c
        all_ref[me] = p_ref[...]
        peers = [(x, y, 1 - c)] + [(px, py, pc) for px, py in _other_chips(x, y) for pc in (c, 1 - c)]
        copies = [pltpu.make_async_remote_copy(
            src_ref=p_ref, dst_ref=all_ref.at[me], send_sem=send_sems.at[k], recv_sem=recv_sems.at[k],
            device_id=peer, device_id_type=MESH) for k, peer in enumerate(peers)]
        for cp in copies:
            cp.start()
        for k, (px, py, pc) in enumerate(peers):
            pltpu.make_async_remote_copy(
                src_ref=p_ref, dst_ref=all_ref.at[4 * px + 2 * py + pc], send_sem=send_sems.at[k], recv_sem=recv_sems.at[k],
                device_id=peers[k], device_id_type=MESH).wait_recv()
        for cp in copies:
            cp.wait_send()
        acc = all_ref[0]
        for k in range(1, N_DEV):
            acc = acc + all_ref[k]
        o_ref[...] = acc

    vm = pl.BlockSpec(memory_space=pltpu.VMEM)
    return _pcall(body, in_specs=[vm], out_specs=vm, out_shape=jax.ShapeDtypeStruct((R, LANES), F32),
                  scratch_shapes=[pltpu.VMEM((N_DEV, R, LANES), F32), pltpu.SemaphoreType.DMA((7,)), pltpu.SemaphoreType.DMA((7,))],
                  name=name)(part)


def _local_step(x, tgt, gains, weights):
    g_pre_mix, g_post_mix, g_pre_ffn, g_post_ffn, g_sb, g_dil = gains
    S, D = x.shape
    hs = g_sb.shape[1] // HEAD_DIM
    hd = g_dil.shape[1] // HEAD_DIM
    cos2, sin_signed = _rope_tables(S)

    h1 = _rms_fwd(x, g_pre_mix + weights.start(), "rms_in")
    w_in_g = weights.w_in(h1)
    proj = _mm_nn(h1, w_in_g, F32, "proj", tn=768)
    o_sb, ct_sb, mx_sb = _sb_fwd(proj, g_sb, hs, "sb_fwd")
    o_dl, lse_dl, mx_dl = _dil_fwd(proj, cos2, sin_signed, g_dil + weights.forward(o_sb), 3 * hs, hd, "dil_fwd")
    w_out_g, w_up_g, w_down_g, cwb = weights.rest(o_dl)
    mixed = jnp.concatenate([mx_sb, mx_dl], axis=1)
    mix = _mm_nn(mixed, w_out_g, F32, "mix_out", tn=1024)
    x2, h2 = _mid_fwd(x, mix, g_post_mix, g_pre_ffn, "mid_fwd")
    u = _mm_nn(h2, w_up_g, BF16, "ffn_up")
    y = _geglu_fwd(u, cwb, "geglu_fwd")
    f = _mm_nn(y, w_down_g, F32, "ffn_down", tn=1024, tk=1408)

    dy, df, dg_post_ffn, loss = _loss_bwd(x2, f, tgt, g_post_ffn, "loss_bwd")
    dyv = _mm_nt(df, w_down_g, BF16, "d_y", tn=1408)
    dw_down = _mm_tn(y, df, D, BF16, "dw_down", tm=1408, tn=1024)
    dc, dcw_g, dcw_v = _geglu_bwd(u, dyv, cwb + weights.grad("w_down", dw_down), "geglu_bwd")
    du = _conv_bwd(dc, cwb, "conv_bwd")
    dh2 = _mm_nt(du, w_up_g, F32, "d_h2", tk=1408)
    dw_up = _mm_tn(h2, du, w_up_g.shape[2], BF16, "dw_up")
    dx2, dmix, dg_pre_ffn, dg_post_mix = _mid_bwd(
        dy, dh2, x2, mix, g_pre_ffn + weights.grad("w_up", dw_up), g_post_mix, "mid_bwd")
    dmixed = _mm_nt(dmix, w_out_g, F32, "d_mixed")
    dw_out = _mm_tn(mixed, dmix, D, BF16, "dw_out", tn=1024)
    dq_s, dk_s, dv_s, dg_sb = _sb_bwd(proj, g_sb + weights.grad("w_out", dw_out), o_sb, ct_sb, dmixed, 0, hs, "sb_bwd")
    dq_d, dk_d, dv_d, dg_dil = _dil_bwd(proj, cos2, sin_signed, g_dil, o_dl, lse_dl, dmixed, hs, 3 * hs, hd, "dil_bwd")
    dproj = jnp.concatenate([dq_s, dk_s, dv_s, dq_d, dk_d, dv_d], axis=1)
    dh1 = _mm_nt(dproj, w_in_g, F32, "d_h1", tk=768)
    dw_in = _mm_tn(h1, dproj, w_in_g.shape[2], BF16, "dw_in", tn=768)
    grad_x, dg_pre_mix = _first_bwd(dx2, dh1, x, g_pre_mix + weights.grad("w_in", dw_in), "first_bwd")
    small = (dg_pre_mix, dg_post_mix, dg_pre_ffn, dg_post_ffn, dg_sb[0:1], dg_dil[0:1], jnp.concatenate([dcw_g, dcw_v], axis=1))
    return loss, grad_x, small


def _pad_cols(a, to):
    return jnp.pad(a, ((0, 0), (0, to - a.shape[1])))


def kernel(x, pre_mix_gain, post_mix_gain, pre_ffn_gain, post_ffn_gain, w_in, sb_out_gain, dil_out_gain, w_out, w_up, conv_w, conv_b, w_down, loss_target, m_pre_mix_gain, m_post_mix_gain, m_pre_ffn_gain, m_post_ffn_gain, m_w_in, m_sb_out_gain, m_dil_out_gain, m_w_out, m_w_up, m_conv_w, m_conv_b, m_w_down, v_pre_mix_gain, v_post_mix_gain, v_pre_ffn_gain, v_post_ffn_gain, v_w_in, v_sb_out_gain, v_dil_out_gain, v_w_out, v_w_up, v_conv_w, v_conv_b, v_w_down):
    xb, tb = x[0], loss_target[0]
    S, D = xb.shape
    w_in, w_out, w_up, w_down, conv_w = w_in[0], w_out[0], w_up[0], w_down[0], conv_w[0]
    n_in, e_rows = w_in.shape[1], w_out.shape[0]
    cu, half = w_up.shape[1], w_down.shape[0]
    assert cu == 2 * half and half % 16 == 0
    cup = -(-cu // LANES) * LANES
    fp = N_CHIP * cup
    px, py, pc = _place()
    me = 4 * px + 2 * py + pc
    core = jnp.reshape(pc, (1,)).astype(jnp.int32)

    shards = [w_in.astype(BF16), w_out.astype(BF16), _pad_cols(w_up, cup).astype(BF16), w_down.astype(BF16),
              jnp.pad(_pad_cols(conv_w, cup), ((0, 8 - conv_w.shape[0]), (0, 0)))]

    def by_dev(ref, qx, qy, qc):
        return ref.at[4 * qx + 2 * qy + qc]

    def down_slot(ref, qx, qy, qc):
        return ref.at[2 * qx + qy, pl.ds(qc * half, half)]

    def by_pair(ref, chip, k):
        return ref.at[chip, k]

    def down_pair(ref, chip, k):
        return ref.at[chip, pl.ds(k * half, half)]

    def pair_spec(tr, cols):
        return pl.BlockSpec((None, None, tr, cols), lambda k, i, c: (k, c[0], i, 0))

    tr_in, tr_up = _tile(D, 512, 16), _tile(D, 256, 16)
    grad_plan = {
        "w_in": ((N_CHIP, 2, D, n_in), by_pair, (D, n_in), tr_in, pair_spec(tr_in, n_in)),
        "w_out": ((N_CHIP, 2, e_rows, D), by_pair, (e_rows, D), e_rows, pair_spec(e_rows, D)),
        "w_up": ((N_CHIP, 2, D, cup), by_pair, (D, cup), tr_up, pair_spec(tr_up, cup)),
        "w_down": ((N_CHIP, cup, D), down_pair, (half, D), half,
                   pl.BlockSpec((None, half, D), lambda k, i, c: (k, c[0], 0))),
    }

    class Exchanges:
        def __init__(self):
            self.in_flight = {}

        def start(self):
            def own_slot(shard):
                return lax.dynamic_update_index_in_dim(lax.empty((N_DEV, *shard.shape), shard.dtype), shard, me, 0)

            down = lax.dynamic_update_slice(jnp.zeros((N_CHIP, cup, D), BF16), shards[3][None], (2 * px + py, pc * half, 0))
            landing = [own_slot(shards[0]), own_slot(shards[1]), own_slot(shards[2]), down, own_slot(shards[4])]
            self.first = _gather_start(landing[:1], [by_dev], core, "gather_in_start")
            self.rest_slots = [by_dev, by_dev, down_slot, by_dev]
            self.second = _gather_start(landing[1:], self.rest_slots, self.first[3], "gather_rest_start")
            return self.second[3][0, 0]

        def w_in(self, after):
            send, recv, gath, _ = self.first
            fsend, frecv, gath, token = _gather_forward(gath, send, recv, [by_dev], after, "gather_in_forward")
            return _gather_finish(gath, fsend, frecv, [by_dev], token, "gather_in_finish")[0]

        def forward(self, after):
            send, recv, gath, _ = self.second
            self.passed = _gather_forward(gath, send, recv, self.rest_slots, after, "gather_rest_forward")
            return self.passed[3][0, 0]

        def rest(self, after):
            fsend, frecv, gath, _ = self.passed
            w_out_g, w_up_g, w_down_g, cw_g = _gather_finish(gath, fsend, frecv, self.rest_slots, after, "gather_rest_finish")
            cb = _pad_cols(conv_b.reshape(N_DEV, cu), cup).reshape(1, 2 * fp)
            cw_full = jnp.transpose(cw_g[:, :3, :], (1, 0, 2)).reshape(3, 2 * fp)
            cwb = jnp.concatenate([cw_full, cb, jnp.zeros((4, 2 * fp), F32)], axis=0)
            return w_out_g.reshape(1, N_DEV * e_rows, D), w_up_g, w_down_g.reshape(1, fp, D), cwb

        def grad(self, name, dw):
            view_shape, view, block, tr, spec = grad_plan[name]
            dw = dw.reshape(view_shape)
            recv = _pair_exchange([dw], [view], [jax.ShapeDtypeStruct((N_CHIP, *block), BF16)], "pair_exchange_" + name)[0]
            pair = _pair_add(core, dw, recv, tr, spec, "pair_add_" + name)
            send, recv_sems, pair, parts, token = _chip_start(pair, recv, "chip_start_" + name)
            self.in_flight[name] = (pair, parts, send, recv_sems)
            return token[0, 0]

        def grad_parts(self, name, after):
            return _chip_wait(*self.in_flight[name], after, "chip_wait_" + name)

    exchanges = Exchanges()
    gains = (pre_mix_gain, post_mix_gain, pre_ffn_gain, post_ffn_gain, sb_out_gain, dil_out_gain)
    loss, grad_x, small = _local_step(xb, tb, gains, exchanges)

    flat = jnp.concatenate([s.reshape(-1) for s in small] + [loss.reshape(-1)])
    sizes = [s.size for s in small]
    rows = -(-flat.size // LANES)
    rows = -(-rows // 8) * 8
    packed = jnp.pad(flat, (0, rows * LANES - flat.size)).reshape(rows, LANES)
    total = _all_reduce_small(packed, "reduce_small").reshape(-1)
    offs = [0]
    for s in sizes:
        offs.append(offs[-1] + s)
    red = [total[offs[k]:offs[k + 1]].reshape(small[k].shape) for k in range(len(small))]
    loss_out = total[offs[-1]]
    g_pre_mix, g_post_mix, g_pre_ffn, g_post_ffn, g_sb, g_dil, g_conv = red
    g_conv_b = g_conv[3].reshape(N_DEV, cup)[:, :cu].reshape(1, N_DEV * cu)
    g_conv_w = lax.dynamic_index_in_dim(g_conv[0:3].reshape(3, N_DEV, cup), me, axis=1, keepdims=False)[:, :cu]

    def small_adam(w, g, m, v, name):
        one = w.shape[0] == 1
        if one:
            w, g, m, v = (jnp.broadcast_to(t, (8, t.shape[1])) for t in (w, g, m, v))
        outs = _adamw(w, g[None], m, v, name)
        return [o[0:1] for o in outs] if one else outs

    chip_ids = jnp.stack([2 * px + py, 2 * (1 - px) + py, 2 * px + 1 - py, 2 * (1 - px) + 1 - py]).astype(jnp.int32)
    out_w_down = _adamw_chips(w_down, *exchanges.grad_parts("w_down", total), chip_ids, m_w_down[0], v_w_down[0], "adam_w_down")
    out_w_up = _adamw_chips(w_up, *exchanges.grad_parts("w_up", out_w_down[1]), chip_ids, m_w_up[0], v_w_up[0], "adam_w_up")
    out_w_out = _adamw_chips(w_out, *exchanges.grad_parts("w_out", out_w_up[1]), chip_ids, m_w_out[0], v_w_out[0], "adam_w_out")
    out_w_in = _adamw_chips(w_in, *exchanges.grad_parts("w_in", out_w_out[1]), chip_ids, m_w_in[0], v_w_in[0], "adam_w_in")
    out_pre_mix = small_adam(pre_mix_gain, g_pre_mix, m_pre_mix_gain, v_pre_mix_gain, "adam_pre_mix")
    out_post_mix = small_adam(post_mix_gain, g_post_mix, m_post_mix_gain, v_post_mix_gain, "adam_post_mix")
    out_pre_ffn = small_adam(pre_ffn_gain, g_pre_ffn, m_pre_ffn_gain, v_pre_ffn_gain, "adam_pre_ffn")
    out_post_ffn = small_adam(post_ffn_gain, g_post_ffn, m_post_ffn_gain, v_post_ffn_gain, "adam_post_ffn")
    out_sb = small_adam(sb_out_gain, g_sb, m_sb_out_gain, v_sb_out_gain, "adam_sb_gain")
    out_dil = small_adam(dil_out_gain, g_dil, m_dil_out_gain, v_dil_out_gain, "adam_dil_gain")
    out_conv_b = small_adam(conv_b, g_conv_b, m_conv_b, v_conv_b, "adam_conv_b")
    cw8 = [jnp.pad(t, ((0, 5), (0, 0))) for t in (conv_w, g_conv_w, m_conv_w[0], v_conv_w[0])]
    out_conv_w = [o[0:3] for o in _adamw(cw8[0], cw8[1][None], cw8[2], cw8[3], "adam_conv_w")]

    order = [out_pre_mix, out_post_mix, out_pre_ffn, out_post_ffn, [o[None] for o in out_w_in], out_sb, out_dil,
             [o[None] for o in out_w_out], [o[None] for o in out_w_up], [o[None] for o in out_conv_w], out_conv_b,
             [o[None] for o in out_w_down]]
    outs = [loss_out, grad_x[None]]
    for k in range(4):
        outs += [o[k] for o in order]
    return tuple(outs)
```

```python
import functools
import math

import jax
import jax.numpy as jnp
from jax import lax
from jax.experimental import pallas as pl
from jax.experimental.pallas import tpu as pltpu

F32 = jnp.float32
BF16 = jnp.bfloat16
HEAD_DIM = 128
LANES = 128
KEY_BLOCK = 128
DILATIONS = (1, 4, 16)
RMS_EPS = 1e-6
ROPE_THETA = 10000.0
NEG = -1e30
ADAM_LR, ADAM_B1, ADAM_B2, ADAM_EPS, ADAM_WD, ADAM_STEP = 0.001, 0.9, 0.999, 1e-08, 0.01, 10
MESH = pl.DeviceIdType.MESH
N_DEV = 8
N_CHIP = 4
HBM = pl.BlockSpec(memory_space=pl.ANY)
VMEM_LIMIT = 56 * 1024 * 1024

_pcall = pl.pallas_call


def _tile(n, pref, mult=LANES):
    best = None
    t = mult
    while t <= min(n, pref):
        if n % t == 0:
            best = t
        t += mult
    return n if best is None else best


def _params(*sem):
    return pltpu.CompilerParams(dimension_semantics=sem, vmem_limit_bytes=VMEM_LIMIT)


def _dot(a, b, dims):
    return lax.dot_general(a, b, (dims, ((), ())), preferred_element_type=F32)


NN = ((1,), (0,))
NT = ((1,), (1,))
TN = ((0,), (0,))


def _mm_body(dims, nk, tile):
    if nk == 1:
        def single(a_ref, b_ref, o_ref):
            o_ref[...] = _dot(a_ref[...].astype(BF16), b_ref[...].astype(BF16), dims).astype(o_ref.dtype)

        return single, []

    def body(a_ref, b_ref, o_ref, acc_ref):
        k = pl.program_id(2)

        @pl.when(k == 0)
        def _():
            acc_ref[...] = jnp.zeros_like(acc_ref)

        acc_ref[...] += _dot(a_ref[...].astype(BF16), b_ref[...].astype(BF16), dims)

        @pl.when(k == nk - 1)
        def _():
            o_ref[...] = acc_ref[...].astype(o_ref.dtype)

    return body, [pltpu.VMEM(tile, F32)]


def _mm_nn(a, b3, out_dtype, name, tm=1024, tn=1408, tk=2048):
    M, K = a.shape
    C, _, n = b3.shape
    tm, tk, tn = _tile(M, tm, 8), _tile(K, tk), _tile(n, tn)
    npc, nk = n // tn, K // tk
    body, scratch = _mm_body(NN, nk, (tm, tn))
    return _pcall(
        body, grid=(M // tm, C * npc, nk),
        in_specs=[pl.BlockSpec((tm, tk), lambda i, j, k: (i, k)),
                  pl.BlockSpec((None, tk, tn), lambda i, j, k: (j // npc, k, j % npc))],
        out_specs=pl.BlockSpec((tm, tn), lambda i, j, k: (i, j)),
        out_shape=jax.ShapeDtypeStruct((M, C * n), out_dtype), scratch_shapes=scratch,
        compiler_params=_params("parallel", "parallel", "arbitrary"), name=name)(a, b3)


def _mm_nt(a, b3, out_dtype, name, tm=1024, tn=1024, tk=2048):
    M, _ = a.shape
    C, N, n = b3.shape
    tm, tn, tk = _tile(M, tm, 8), _tile(N, tn), _tile(n, tk)
    kpc = n // tk
    nk = C * kpc
    body, scratch = _mm_body(NT, nk, (tm, tn))
    return _pcall(
        body, grid=(M // tm, N // tn, nk),
        in_specs=[pl.BlockSpec((tm, tk), lambda i, j, k: (i, k)),
                  pl.BlockSpec((None, tn, tk), lambda i, j, k: (k // kpc, j, k % kpc))],
        out_specs=pl.BlockSpec((tm, tn), lambda i, j, k: (i, j)),
        out_shape=jax.ShapeDtypeStruct((M, N), out_dtype), scratch_shapes=scratch,
        compiler_params=_params("parallel", "parallel", "arbitrary"), name=name)(a, b3)


def _mm_tn(x, y, n, out_dtype, name, tm=1024, tn=1408, tk=2048, after=None):
    S, P = x.shape
    C = y.shape[1] // n
    tm, tn, tk = _tile(P, tm), _tile(n, tn), _tile(S, tk, 8)
    npc, nk = n // tn, S // tk
    inner, scratch = _mm_body(TN, nk, (tm, tn))
    extra = [] if after is None else [after]

    def body(x_ref, y_ref, *rest):
        inner(x_ref, y_ref, *rest[len(extra):])

    return _pcall(
        body, grid=(P // tm, C * npc, nk),
        in_specs=[pl.BlockSpec((tk, tm), lambda i, j, k: (k, i)),
                  pl.BlockSpec((tk, tn), lambda i, j, k: (k, j))] + [HBM] * len(extra),
        out_specs=pl.BlockSpec((None, tm, tn), lambda i, j, k: (j // npc, i, j % npc)),
        out_shape=jax.ShapeDtypeStruct((C, P, n), out_dtype), scratch_shapes=scratch,
        compiler_params=_params("parallel", "parallel", "arbitrary"), name=name)(x, y, *extra)


def _rms_scale(v):
    return lax.rsqrt(jnp.mean(v * v, axis=-1, keepdims=True) + RMS_EPS)


def _rms_bwd(gy, v, r):
    return r * gy - v * (r * r * r * jnp.mean(gy * v, axis=-1, keepdims=True))


def _rows_spec(tm, d):
    return pl.BlockSpec((tm, d), lambda i: (i, 0))


def _vec_spec(d):
    return pl.BlockSpec((1, d), lambda i: (0, 0))


def _rms_fwd(x, g, name, tm=256):
    S, D = x.shape

    def body(x_ref, g_ref, h_ref):
        v = x_ref[...]
        h_ref[...] = (v * _rms_scale(v) * g_ref[...]).astype(BF16)

    return _pcall(body, grid=(S // tm,), in_specs=[_rows_spec(tm, D), _vec_spec(D)], out_specs=_rows_spec(tm, D),
                  out_shape=jax.ShapeDtypeStruct((S, D), BF16), compiler_params=_params("parallel"), name=name)(x, g)


def _mid_fwd(x, mix, g_post, g_pre, name, tm=256):
    S, D = x.shape

    def body(x_ref, m_ref, gp_ref, gn_ref, x2_ref, h_ref):
        m = m_ref[...]
        x2 = x_ref[...] + m * _rms_scale(m) * gp_ref[...]
        x2_ref[...] = x2
        h_ref[...] = (x2 * _rms_scale(x2) * gn_ref[...]).astype(BF16)

    return _pcall(body, grid=(S // tm,), in_specs=[_rows_spec(tm, D), _rows_spec(tm, D), _vec_spec(D), _vec_spec(D)],
                  out_specs=[_rows_spec(tm, D), _rows_spec(tm, D)],
                  out_shape=[jax.ShapeDtypeStruct((S, D), F32), jax.ShapeDtypeStruct((S, D), BF16)],
                  compiler_params=_params("parallel"), name=name)(x, mix, g_post, g_pre)


def _loss_bwd(x2, f, tgt, g_post, name, tm=256):
    S, D = x2.shape

    def body(x2_ref, f_ref, t_ref, g_ref, dy_ref, df_ref, dg_ref, ls_ref):
        i = pl.program_id(0)

        @pl.when(i == 0)
        def _():
            dg_ref[...] = jnp.zeros_like(dg_ref)
            ls_ref[...] = jnp.zeros_like(ls_ref)

        fv = f_ref[...]
        r = _rms_scale(fv)
        g = g_ref[...]
        err = x2_ref[...] + fv * r * g - t_ref[...]
        ls_ref[...] += jnp.broadcast_to(0.5 * jnp.sum(jnp.mean(err * err, axis=-1, keepdims=True), axis=0, keepdims=True), ls_ref.shape)
        dy = err * (1.0 / D)
        dy_ref[...] = dy
        df_ref[...] = _rms_bwd(dy * g, fv, r).astype(BF16)
        dg_ref[...] += jnp.sum(dy * fv * r, axis=0, keepdims=True)

    return _pcall(body, grid=(S // tm,),
                  in_specs=[_rows_spec(tm, D), _rows_spec(tm, D), _rows_spec(tm, D), _vec_spec(D)],
                  out_specs=[_rows_spec(tm, D), _rows_spec(tm, D), _vec_spec(D), _vec_spec(LANES)],
                  out_shape=[jax.ShapeDtypeStruct((S, D), F32), jax.ShapeDtypeStruct((S, D), BF16),
                             jax.ShapeDtypeStruct((1, D), F32), jax.ShapeDtypeStruct((1, LANES), F32)],
                  compiler_params=_params("arbitrary"), name=name)(x2, f, tgt, g_post)


def _mid_bwd(dy, dh2, x2, mix, g_pre, g_post, name, tm=256):
    S, D = dy.shape

    def body(dy_ref, dh_ref, x2_ref, m_ref, gn_ref, gp_ref, dx2_ref, dm_ref, dgn_ref, dgp_ref):
        i = pl.program_id(0)

        @pl.when(i == 0)
        def _():
            dgn_ref[...] = jnp.zeros_like(dgn_ref)
            dgp_ref[...] = jnp.zeros_like(dgp_ref)

        x2, dh = x2_ref[...], dh_ref[...]
        r = _rms_scale(x2)
        dx2 = dy_ref[...] + _rms_bwd(dh * gn_ref[...], x2, r)
        dgn_ref[...] += jnp.sum(dh * x2 * r, axis=0, keepdims=True)
        dx2_ref[...] = dx2
        m = m_ref[...]
        rm = _rms_scale(m)
        dm_ref[...] = _rms_bwd(dx2 * gp_ref[...], m, rm).astype(BF16)
        dgp_ref[...] += jnp.sum(dx2 * m * rm, axis=0, keepdims=True)

    return _pcall(body, grid=(S // tm,),
                  in_specs=[_rows_spec(tm, D)] * 4 + [_vec_spec(D)] * 2,
                  out_specs=[_rows_spec(tm, D), _rows_spec(tm, D), _vec_spec(D), _vec_spec(D)],
                  out_shape=[jax.ShapeDtypeStruct((S, D), F32), jax.ShapeDtypeStruct((S, D), BF16),
                             jax.ShapeDtypeStruct((1, D), F32), jax.ShapeDtypeStruct((1, D), F32)],
                  compiler_params=_params("arbitrary"), name=name)(dy, dh2, x2, mix, g_pre, g_post)


def _first_bwd(dx2, dh1, x, g_pre, name, tm=256):
    S, D = x.shape

    def body(dx2_ref, dh_ref, x_ref, g_ref, gx_ref, dg_ref):
        i = pl.program_id(0)

        @pl.when(i == 0)
        def _():
            dg_ref[...] = jnp.zeros_like(dg_ref)

        xv, dh = x_ref[...], dh_ref[...]
        r = _rms_scale(xv)
        gx_ref[...] = dx2_ref[...] + _rms_bwd(dh * g_ref[...], xv, r)
        dg_ref[...] += jnp.sum(dh * xv * r, axis=0, keepdims=True)

    return _pcall(body, grid=(S // tm,), in_specs=[_rows_spec(tm, D)] * 3 + [_vec_spec(D)],
                  out_specs=[_rows_spec(tm, D), _vec_spec(D)],
                  out_shape=[jax.ShapeDtypeStruct((S, D), F32), jax.ShapeDtypeStruct((1, D), F32)],
                  compiler_params=_params("arbitrary"), name=name)(dx2, dh1, x, g_pre)


def _logsig_pair(z):
    lb = jnp.minimum(z, 0.0) - jnp.log(1.0 + jnp.exp(-jnp.abs(z)))
    return lb, lb - z


def _sum_matrix(strict):
    ia = lax.broadcasted_iota(jnp.int32, (KEY_BLOCK, KEY_BLOCK), 0)
    ib = lax.broadcasted_iota(jnp.int32, (KEY_BLOCK, KEY_BLOCK), 1)
    tri = (ia > ib) if strict == ">" else (ia < ib)
    return jnp.concatenate([tri.astype(BF16), jnp.ones((KEY_BLOCK, KEY_BLOCK), BF16)], axis=1)


def _split_dot(v, u):
    hi = v.astype(BF16)
    lo = (v - hi.astype(F32)).astype(BF16)
    return _dot(hi, u, NN) + _dot(lo, u, NN)


def _head_out(o, g):
    return o * _rms_scale(o) * g


def _sb_fwd(proj, gain, n_heads, name, tq=512):
    S = proj.shape[0]
    H, tk = n_heads, KEY_BLOCK
    tq = _tile(S, tq, tk)
    scale = HEAD_DIM ** -0.5

    def body(q_ref, k_ref, v_ref, g_ref, o_ref, ct_ref, mx_ref, oacc, cacc):
        i = pl.program_id(1)
        oacc[...] = jnp.zeros_like(oacc)
        cacc[...] = jnp.zeros_like(cacc)
        sums = _sum_matrix(">")

        def block(k0, r0, diagonal):
            rows = pl.ds(r0, tq - r0)
            q = q_ref[rows, :].astype(BF16)
            kj = k_ref[pl.ds(k0, tk), :].astype(BF16)
            vj = v_ref[pl.ds(k0, tk), :].astype(BF16)
            lb, lk = _logsig_pair(_dot(q, kj, NT) * scale)
            if diagonal:
                causal = (lax.broadcasted_iota(jnp.int32, (tq - r0, tk), 1) < lax.broadcasted_iota(jnp.int32, (tq - r0, tk), 0))
                lk = jnp.where(causal, lk, 0.0)
            both = _split_dot(lk, sums)
            c = cacc[rows, :]
            a = jnp.exp(lb + both[:, :tk] + c)
            if diagonal:
                a = jnp.where(causal, a, 0.0)
            oacc[rows, :] += _dot(a.astype(BF16), vj, NN)
            cacc[rows, :] = c + both[:, tk:]

        for d in reversed(range(tq // tk)):
            block(pl.multiple_of(i * tq + d * tk, tk), d * tk, True)
        n_whole = i * (tq // tk)

        def step(it, carry):
            block(pl.multiple_of((n_whole - 1 - it) * tk, tk), 0, False)
            return carry

        lax.fori_loop(0, n_whole, step, 0)
        o = oacc[...]
        o_ref[...] = o
        ct_ref[...] = cacc[...]
        mx_ref[...] = _head_out(o, g_ref[...]).astype(BF16)

    blk = pl.BlockSpec((tq, HEAD_DIM), lambda h, i: (i, h))
    return _pcall(
        body, grid=(H, S // tq),
        in_specs=[blk, pl.BlockSpec((S, HEAD_DIM), lambda h, i: (0, H + h)),
                  pl.BlockSpec((S, HEAD_DIM), lambda h, i: (0, 2 * H + h)), pl.BlockSpec((1, HEAD_DIM), lambda h, i: (0, h))],
        out_specs=[blk, blk, blk],
        out_shape=[jax.ShapeDtypeStruct((S, H * HEAD_DIM), F32), jax.ShapeDtypeStruct((S, H * HEAD_DIM), F32),
                   jax.ShapeDtypeStruct((S, H * HEAD_DIM), BF16)],
        scratch_shapes=[pltpu.VMEM((tq, HEAD_DIM), F32), pltpu.VMEM((tq, LANES), F32)],
        compiler_params=_params("parallel", "arbitrary"), name=name)(proj, proj, proj, gain)


def _sb_bwd(proj, gain, o_raw, ctot, dmixed, dm_col0, n_heads, name, tq=512):
    S = proj.shape[0]
    H, tk = n_heads, KEY_BLOCK
    tq = _tile(S, tq, tk)
    nq = S // tq
    scale = HEAD_DIM ** -0.5

    def body(q_ref, k_ref, v_ref, g_ref, o_ref, ct_ref, dm_ref, dq_ref, dk_ref, dv_ref, dg_ref,
             dkacc, dvacc, dqacc, pfx, gcar, dos):
        i = pl.program_id(1)

        @pl.when(i == 0)
        def _():
            dkacc[...] = jnp.zeros_like(dkacc)
            dvacc[...] = jnp.zeros_like(dvacc)
            dg_ref[...] = jnp.zeros_like(dg_ref)

        o, dm, g = o_ref[...], dm_ref[...], g_ref[...]
        r = _rms_scale(o)
        dos[...] = _rms_bwd(dm * g, o, r).astype(BF16)
        dg_ref[...] += jnp.broadcast_to(jnp.sum(dm * o * r, axis=0, keepdims=True), dg_ref.shape)
        dqacc[...] = jnp.zeros_like(dqacc)
        pfx[...] = jnp.zeros_like(pfx)
        gcar[...] = jnp.zeros_like(gcar)
        later, earlier = _sum_matrix(">"), _sum_matrix("<")

        def block(k0, r0, diagonal):
            rows = pl.ds(r0, tq - r0)
            keys = pl.ds(k0, tk)
            q, do = q_ref[rows, :].astype(BF16), dos[rows, :]
            kj, vj = k_ref[keys, :].astype(BF16), v_ref[keys, :].astype(BF16)
            lb, lk = _logsig_pair(_dot(q, kj, NT) * scale)
            if diagonal:
                causal = (lax.broadcasted_iota(jnp.int32, (tq - r0, tk), 1) < lax.broadcasted_iota(jnp.int32, (tq - r0, tk), 0))
                lk = jnp.where(causal, lk, 0.0)
            both = _split_dot(lk, later)
            p = pfx[rows, :] + both[:, tk:]
            a = jnp.exp(lb + both[:, :tk] + (ct_ref[rows, :] - p))
            if diagonal:
                a = jnp.where(causal, a, 0.0)
            dl = _dot(do, vj, NT) * a
            dvacc[keys, :] += _dot(a.astype(BF16), do, TN)
            both = _split_dot(dl, earlier)
            gc = gcar[rows, :]
            sig = jnp.exp(lb)
            gsum = (both[:, :tk] + gc) * sig
            if diagonal:
                gsum = jnp.where(causal, gsum, 0.0)
            dz = ((dl * (1.0 - sig) - gsum) * scale).astype(BF16)
            dqacc[rows, :] += _dot(dz, kj, NN)
            dkacc[keys, :] += _dot(dz, q, TN)
            pfx[rows, :] = p
            gcar[rows, :] = gc + both[:, tk:]

        def step(j, carry):
            block(pl.multiple_of(j * tk, tk), 0, False)
            return carry

        lax.fori_loop(0, i * (tq // tk), step, 0)
        for d in range(tq // tk):
            block(pl.multiple_of(i * tq + d * tk, tk), d * tk, True)
        dq_ref[...] = dqacc[...].astype(BF16)

        @pl.when(i == nq - 1)
        def _():
            dk_ref[...] = dkacc[...].astype(BF16)
            dv_ref[...] = dvacc[...].astype(BF16)

    blk = pl.BlockSpec((tq, HEAD_DIM), lambda h, i: (i, h))
    full = pl.BlockSpec((S, HEAD_DIM), lambda h, i: (0, h))
    W = H * HEAD_DIM
    return _pcall(
        body, grid=(H, nq),
        in_specs=[blk, pl.BlockSpec((S, HEAD_DIM), lambda h, i: (0, H + h)),
                  pl.BlockSpec((S, HEAD_DIM), lambda h, i: (0, 2 * H + h)), pl.BlockSpec((1, HEAD_DIM), lambda h, i: (0, h)),
                  blk, blk, pl.BlockSpec((tq, HEAD_DIM), lambda h, i: (i, dm_col0 + h))],
        out_specs=[blk, full, full, pl.BlockSpec((8, HEAD_DIM), lambda h, i: (0, h))],
        out_shape=[jax.ShapeDtypeStruct((S, W), BF16), jax.ShapeDtypeStruct((S, W), BF16),
                   jax.ShapeDtypeStruct((S, W), BF16), jax.ShapeDtypeStruct((8, W), F32)],
        scratch_shapes=[pltpu.VMEM((S, HEAD_DIM), F32), pltpu.VMEM((S, HEAD_DIM), F32), pltpu.VMEM((tq, HEAD_DIM), F32),
                        pltpu.VMEM((tq, LANES), F32), pltpu.VMEM((tq, LANES), F32), pltpu.VMEM((tq, HEAD_DIM), BF16)],
        compiler_params=_params("arbitrary", "arbitrary"), name=name)(proj, proj, proj, gain, o_raw, ctot, dmixed)


def _rope_tables(S):
    inv_freq = ROPE_THETA ** (-jnp.arange(0, HEAD_DIM, 2, dtype=F32) / HEAD_DIM)
    ang = jnp.arange(S, dtype=F32)[:, None] * inv_freq[None, :]
    cos, sin = jnp.cos(ang), jnp.sin(ang)
    return jnp.concatenate([cos, cos], axis=1), jnp.concatenate([-sin, sin], axis=1)


def _rope(v, cos2, sin_signed):
    return v * cos2 + pltpu.roll(v, HEAD_DIM // 2, axis=1) * sin_signed


def _dil_rows(d, r, l0, n):
    if d == 1:
        return pl.ds(l0 if isinstance(l0, int) else pl.multiple_of(l0, KEY_BLOCK), n)
    return pl.ds(r + d * l0, n, stride=d)


def _dil_blocks(S, visit):
    B = KEY_BLOCK
    for b, d in enumerate(DILATIONS):
        nb = S // d // B

        def per_residue(r, carry, b=b, d=d, nb=nb):
            visit(b, d, r, 0, True)
            if nb > 1:
                def per_block(n, c2):
                    visit(b, d, r, n * B, False)
                    return c2
                lax.fori_loop(1, nb, per_block, 0)
            return carry

        if d == 1:
            per_residue(0, 0)
        else:
            lax.fori_loop(0, d, per_residue, 0)


def _dil_mask(first):
    B = KEY_BLOCK
    nk = B if first else 2 * B
    iq = lax.broadcasted_iota(jnp.int32, (B, nk), 0)
    ik = lax.broadcasted_iota(jnp.int32, (B, nk), 1)
    return (ik <= iq) if first else ((ik >= iq) & (ik <= iq + B))


def _dil_fwd(proj, cos2, sin_signed, gain, col0, n_heads, name):
    S = proj.shape[0]
    H, B = n_heads, KEY_BLOCK
    scale = HEAD_DIM ** -0.5
    rc = _tile(S, 256, 8)

    def body(q_ref, k_ref, v_ref, c_ref, s_ref, g_ref, o_ref, l_ref, mx_ref, qr, kr, *per_branch):
        ob, lb = per_branch[:len(DILATIONS)], per_branch[len(DILATIONS):]

        def rope_rows(t, carry):
            rows = pl.ds(pl.multiple_of(t * rc, rc), rc)
            qr[rows, :] = _rope(q_ref[rows, :], c_ref[rows, :], s_ref[rows, :])
            kr[rows, :] = _rope(k_ref[rows, :], c_ref[rows, :], s_ref[rows, :])
            return carry

        lax.fori_loop(0, S // rc, rope_rows, 0)

        def visit(b, d, r, l0, first):
            nk = B if first else 2 * B
            qrows = _dil_rows(d, r, l0, B)
            krows = qrows if first else _dil_rows(d, r, l0 - B, nk)
            s = _dot(qr[qrows, :].astype(BF16), kr[krows, :].astype(BF16), NT) * scale
            s = jnp.where(_dil_mask(first), s, NEG)
            m = jnp.max(s, axis=1, keepdims=True)
            p = jnp.exp(s - m)
            den = jnp.sum(p, axis=1, keepdims=True)
            ob[b][qrows, :] = _dot(p.astype(BF16), v_ref[krows, :].astype(BF16), NN) / den
            lb[b][qrows, :] = jnp.broadcast_to(m + jnp.log(den), (B, LANES))

        _dil_blocks(S, visit)

        def combine(t, carry):
            rows = pl.ds(pl.multiple_of(t * rc, rc), rc)
            l0, l1, l2 = lb[0][rows, :], lb[1][rows, :], lb[2][rows, :]
            m = jnp.maximum(jnp.maximum(l0, l1), l2)
            w0, w1, w2 = jnp.exp(l0 - m), jnp.exp(l1 - m), jnp.exp(l2 - m)
            den = w0 + w1 + w2
            o = (w0 * ob[0][rows, :] + w1 * ob[1][rows, :] + w2 * ob[2][rows, :]) / den
            o_ref[rows, :] = o
            l_ref[rows, :] = m + jnp.log(den)
            mx_ref[rows, :] = _head_out(o, g_ref[...]).astype(BF16)
            return carry

        lax.fori_loop(0, S // rc, combine, 0)

    def col(k):
        return pl.BlockSpec((S, HEAD_DIM), lambda h: (0, col0 + k * H + h))

    tab = pl.BlockSpec((S, HEAD_DIM), lambda h: (0, 0))
    out = pl.BlockSpec((S, HEAD_DIM), lambda h: (0, h))
    W = H * HEAD_DIM
    return _pcall(
        body, grid=(H,),
        in_specs=[col(0), col(1), col(2), tab, tab, pl.BlockSpec((1, HEAD_DIM), lambda h: (0, h))],
        out_specs=[out, out, out],
        out_shape=[jax.ShapeDtypeStruct((S, W), F32), jax.ShapeDtypeStruct((S, W), F32), jax.ShapeDtypeStruct((S, W), BF16)],
        scratch_shapes=[pltpu.VMEM((S, HEAD_DIM), F32)] * (2 + 2 * len(DILATIONS)),
        compiler_params=_params("parallel"), name=name)(proj, proj, proj, cos2, sin_signed, gain)


def _dil_bwd(proj, cos2, sin_signed, gain, o_raw, lse, dmixed, dm_col0, col0, n_heads, name):
    S = proj.shape[0]
    H, B = n_heads, KEY_BLOCK
    scale = HEAD_DIM ** -0.5
    rc = _tile(S, 256, 8)

    def body(q_ref, k_ref, v_ref, c_ref, s_ref, g_ref, o_ref, l_ref, dm_ref, dq_ref, dk_ref, dv_ref, dg_ref,
             qr, kr, dos, dsum, dqr, dkr, dvv):
        dg_ref[...] = jnp.zeros_like(dg_ref)

        def prep(t, carry):
            rows = pl.ds(pl.multiple_of(t * rc, rc), rc)
            qr[rows, :] = _rope(q_ref[rows, :], c_ref[rows, :], s_ref[rows, :])
            kr[rows, :] = _rope(k_ref[rows, :], c_ref[rows, :], s_ref[rows, :])
            o, dm = o_ref[rows, :], dm_ref[rows, :]
            r = _rms_scale(o)
            do = _rms_bwd(dm * g_ref[...], o, r)
            dg_ref[...] += jnp.broadcast_to(jnp.sum(dm * o * r, axis=0, keepdims=True), dg_ref.shape)
            dos[rows, :] = do
            dsum[rows, :] = jnp.broadcast_to(jnp.sum(do * o, axis=1, keepdims=True), (rc, LANES))
            dqr[rows, :] = jnp.zeros((rc, HEAD_DIM), F32)
            dkr[rows, :] = jnp.zeros((rc, HEAD_DIM), F32)
            dvv[rows, :] = jnp.zeros((rc, HEAD_DIM), F32)
            return carry

        lax.fori_loop(0, S // rc, prep, 0)

        def visit(b, d, r, l0, first):
            nk = B if first else 2 * B
            qrows = _dil_rows(d, r, l0, B)
            krows = qrows if first else _dil_rows(d, r, l0 - B, nk)
            qs, ks = qr[qrows, :].astype(BF16), kr[krows, :].astype(BF16)
            do = dos[qrows, :].astype(BF16)
            s = _dot(qs, ks, NT) * scale
            s = jnp.where(_dil_mask(first), s, NEG)
            p = jnp.exp(s - l_ref[qrows, :][:, 0:1])
            dp = _dot(do, v_ref[krows, :].astype(BF16), NT)
            ds = (p * (dp - dsum[qrows, :][:, 0:1]) * scale).astype(BF16)
            dqr[qrows, :] += _dot(ds, ks, NN)
            dkr[krows, :] += _dot(ds, qs, TN)
            dvv[krows, :] += _dot(p.astype(BF16), do, TN)

        _dil_blocks(S, visit)

        def finish(t, carry):
            rows = pl.ds(pl.multiple_of(t * rc, rc), rc)
            c, s = c_ref[rows, :], s_ref[rows, :]
            dq, dk = dqr[rows, :], dkr[rows, :]
            dq_ref[rows, :] = (dq * c + pltpu.roll(dq * s, HEAD_DIM // 2, axis=1)).astype(BF16)
            dk_ref[rows, :] = (dk * c + pltpu.roll(dk * s, HEAD_DIM // 2, axis=1)).astype(BF16)
            dv_ref[rows, :] = dvv[rows, :].astype(BF16)
            return carry

        lax.fori_loop(0, S // rc, finish, 0)

    def col(k):
        return pl.BlockSpec((S, HEAD_DIM), lambda h: (0, col0 + k * H + h))

    tab = pl.BlockSpec((S, HEAD_DIM), lambda h: (0, 0))
    out = pl.BlockSpec((S, HEAD_DIM), lambda h: (0, h))
    W = H * HEAD_DIM
    big = pltpu.VMEM((S, HEAD_DIM), F32)
    return _pcall(
        body, grid=(H,),
        in_specs=[col(0), col(1), col(2), tab, tab, pl.BlockSpec((1, HEAD_DIM), lambda h: (0, h)), out, out,
                  pl.BlockSpec((S, HEAD_DIM), lambda h: (0, dm_col0 + h))],
        out_specs=[out, out, out, pl.BlockSpec((8, HEAD_DIM), lambda h: (0, h))],
        out_shape=[jax.ShapeDtypeStruct((S, W), BF16), jax.ShapeDtypeStruct((S, W), BF16),
                   jax.ShapeDtypeStruct((S, W), BF16), jax.ShapeDtypeStruct((8, W), F32)],
        scratch_shapes=[big, big, big, pltpu.VMEM((S, LANES), F32), big, big, big],
        compiler_params=_params("parallel"), name=name)(proj, proj, proj, cos2, sin_signed, gain, o_raw, lse, dmixed)


GELU_C = math.sqrt(2.0 / math.pi)
GELU_A = 0.044715
HALO = 16


def _shift_down(cur, halo, k):
    out = pltpu.roll(cur, k, axis=0)
    row = lax.broadcasted_iota(jnp.int32, cur.shape, 0)
    for t in range(k):
        out = jnp.where(row == t, halo[HALO - k + t:HALO - k + t + 1, :], out)
    return out


def _shift_up(cur, halo, k):
    n = cur.shape[0]
    out = pltpu.roll(cur, n - k, axis=0)
    row = lax.broadcasted_iota(jnp.int32, cur.shape, 0)
    for t in range(k):
        out = jnp.where(row == n - k + t, halo[t:t + 1, :], out)
    return out


def _conv3(cur, halo, cw):
    return _shift_down(cur, halo, 2) * cw[0:1, :] + _shift_down(cur, halo, 1) * cw[1:2, :] + cur * cw[2:3, :] + cw[3:4, :]


def _gelu_parts(x):
    t = jnp.tanh(GELU_C * (x + GELU_A * x * x * x))
    return 0.5 * x * (1.0 + t), t


def _geglu_specs(tm, tn, ncb):
    hb = tm // HALO

    def cur(off):
        return pl.BlockSpec((tm, tn), lambda j, i: (i, off + j))

    def prev(off):
        return pl.BlockSpec((HALO, tn), lambda j, i: (jnp.maximum(i * hb - 1, 0), off + j))

    def taps(off):
        return pl.BlockSpec((8, tn), lambda j, i: (0, off + j))

    return [cur(0), prev(0), cur(ncb), prev(ncb), taps(0), taps(ncb)]


def _geglu_fwd(u, cwb, name, tm=256, tn=512):
    S, F2 = u.shape
    F = F2 // 2
    tm, tn = _tile(S, tm, HALO), _tile(F, tn)
    ncb = F // tn

    def body(g_ref, gp_ref, v_ref, vp_ref, cg_ref, cv_ref, y_ref):
        top = pl.program_id(1) > 0
        gp = jnp.where(top, gp_ref[...].astype(F32), 0.0)
        vp = jnp.where(top, vp_ref[...].astype(F32), 0.0)
        gc = _conv3(g_ref[...].astype(F32), gp, cg_ref[...])
        vc = _conv3(v_ref[...].astype(F32), vp, cv_ref[...])
        y_ref[...] = (_gelu_parts(gc)[0] * vc).astype(BF16)

    return _pcall(body, grid=(ncb, S // tm), in_specs=_geglu_specs(tm, tn, ncb),
                  out_specs=pl.BlockSpec((tm, tn), lambda j, i: (i, j)),
                  out_shape=jax.ShapeDtypeStruct((S, F), BF16),
                  compiler_params=_params("parallel", "parallel"), name=name)(u, u, u, u, cwb, cwb)


def _geglu_bwd(u, dy, cwb, name, tm=256, tn=512):
    S, F2 = u.shape
    F = F2 // 2
    tm, tn = _tile(S, tm, HALO), _tile(F, tn)
    ncb = F // tn

    def body(g_ref, gp_ref, v_ref, vp_ref, cg_ref, cv_ref, dy_ref, dc_ref, dwg_ref, dwv_ref):
        i = pl.program_id(1)

        @pl.when(i == 0)
        def _():
            dwg_ref[...] = jnp.zeros_like(dwg_ref)
            dwv_ref[...] = jnp.zeros_like(dwv_ref)

        top = i > 0
        g, v = g_ref[...].astype(F32), v_ref[...].astype(F32)
        gp = jnp.where(top, gp_ref[...].astype(F32), 0.0)
        vp = jnp.where(top, vp_ref[...].astype(F32), 0.0)
        gc = _conv3(g, gp, cg_ref[...])
        vc = _conv3(v, vp, cv_ref[...])
        act, t = _gelu_parts(gc)
        dact = 0.5 * (1.0 + t) + 0.5 * gc * (1.0 - t * t) * GELU_C * (1.0 + 3.0 * GELU_A * gc * gc)
        dyv = dy_ref[...].astype(F32)
        dgc = dyv * vc * dact
        dvc = dyv * act
        dc_ref[0] = dgc.astype(BF16)
        dc_ref[1] = dvc.astype(BF16)

        def taps(out_ref, dc, cur, halo):
            out_ref[0:1, :] += jnp.sum(dc * _shift_down(cur, halo, 2), axis=0, keepdims=True)
            out_ref[1:2, :] += jnp.sum(dc * _shift_down(cur, halo, 1), axis=0, keepdims=True)
            out_ref[2:3, :] += jnp.sum(dc * cur, axis=0, keepdims=True)
            out_ref[3:4, :] += jnp.sum(dc, axis=0, keepdims=True)

        taps(dwg_ref, dgc, g, gp)
        taps(dwv_ref, dvc, v, vp)

    return _pcall(body, grid=(ncb, S // tm),
                  in_specs=_geglu_specs(tm, tn, ncb) + [pl.BlockSpec((tm, tn), lambda j, i: (i, j))],
                  out_specs=[pl.BlockSpec((2, tm, tn), lambda j, i: (0, i, j)),
                             pl.BlockSpec((8, tn), lambda j, i: (0, j)), pl.BlockSpec((8, tn), lambda j, i: (0, j))],
                  out_shape=[jax.ShapeDtypeStruct((2, S, F), BF16), jax.ShapeDtypeStruct((8, F), F32),
                             jax.ShapeDtypeStruct((8, F), F32)],
                  compiler_params=_params("parallel", "arbitrary"), name=name)(u, u, u, u, cwb, cwb, dy)


def _conv_bwd(dc, cwb, name, tm=256, tn=512):
    _, S, F = dc.shape
    tm, tn = _tile(S, tm, HALO), _tile(F, tn)
    ncb, nrb = F // tn, S // tm
    hb = tm // HALO

    def body(c_ref, n_ref, w_ref, du_ref):
        cur = c_ref[...].astype(F32)
        nxt = jnp.where(pl.program_id(2) < nrb - 1, n_ref[...].astype(F32), 0.0)
        w = w_ref[...]
        du = cur * w[2:3, :] + _shift_up(cur, nxt, 1) * w[1:2, :] + _shift_up(cur, nxt, 2) * w[0:1, :]
        du_ref[...] = du.astype(BF16)

    return _pcall(body, grid=(2, ncb, nrb),
                  in_specs=[pl.BlockSpec((None, tm, tn), lambda c, j, i: (c, i, j)),
                            pl.BlockSpec((None, HALO, tn), lambda c, j, i: (c, jnp.minimum((i + 1) * hb, S // HALO - 1), j)),
                            pl.BlockSpec((8, tn), lambda c, j, i: (0, c * ncb + j))],
                  out_specs=pl.BlockSpec((tm, tn), lambda c, j, i: (i, c * ncb + j)),
                  out_shape=jax.ShapeDtypeStruct((S, 2 * F), BF16),
                  compiler_params=_params("parallel", "parallel", "parallel"), name=name)(dc, dc, cwb)


def _adam_math(w, g, m, v):
    m = ADAM_B1 * m + (1.0 - ADAM_B1) * g
    v = ADAM_B2 * v + (1.0 - ADAM_B2) * (g * g)
    m_hat = m / (1.0 - ADAM_B1 ** ADAM_STEP)
    v_hat = v / (1.0 - ADAM_B2 ** ADAM_STEP)
    return -ADAM_LR * (m_hat / (jnp.sqrt(v_hat) + ADAM_EPS) + ADAM_WD * w), m, v


def _adamw(w, parts, m, v, name, tr=256):
    R, C = w.shape
    n, _, Cp = parts.shape
    tr = _tile(R, tr, 8)

    def body(w_ref, p_ref, m_ref, v_ref, g_out, d_out, m_out, v_out):
        g = p_ref[0, :, 0:C].astype(F32)
        for k in range(1, n):
            g = g + p_ref[k, :, 0:C].astype(F32)
        d, mn, vn = _adam_math(w_ref[...], g, m_ref[...], v_ref[...])
        g_out[...] = g
        d_out[...] = d
        m_out[...] = mn
        v_out[...] = vn

    spec = pl.BlockSpec((tr, C), lambda i: (i, 0))
    shape = jax.ShapeDtypeStruct((R, C), F32)
    return _pcall(body, grid=(R // tr,), in_specs=[spec, pl.BlockSpec((n, tr, Cp), lambda i: (0, i, 0)), spec, spec],
                  out_specs=[spec] * 4, out_shape=[shape] * 4, compiler_params=_params("parallel"), name=name)(w, parts, m, v)


def _adamw_chips(w, pair, parts, chip_ids, m, v, name, tr=256):
    R, C = w.shape
    Cp = pair.shape[2]
    tr = _tile(R, tr, 16)

    def body(ids_ref, w_ref, own_ref, p1_ref, p2_ref, p3_ref, m_ref, v_ref, g_out, d_out, m_out, v_out):
        g = own_ref[:, 0:C].astype(F32)
        for ref in (p1_ref, p2_ref, p3_ref):
            g = g + ref[:, 0:C].astype(F32)
        d, mn, vn = _adam_math(w_ref[...], g, m_ref[...], v_ref[...])
        g_out[...] = g
        d_out[...] = d
        m_out[...] = mn
        v_out[...] = vn

    spec = pl.BlockSpec((tr, C), lambda i, ids: (i, 0))

    def chip(k):
        return pl.BlockSpec((None, tr, Cp), lambda i, ids: (ids[k], i, 0))

    shape = jax.ShapeDtypeStruct((R, C), F32)
    grid_spec = pltpu.PrefetchScalarGridSpec(
        num_scalar_prefetch=1, grid=(R // tr,), in_specs=[spec, chip(0), chip(1), chip(2), chip(3), spec, spec],
        out_specs=[spec] * 4)
    return _pcall(body, grid_spec=grid_spec, out_shape=[shape] * 4, compiler_params=_params("parallel"),
                  name=name)(chip_ids, w, pair, parts, parts, parts, m, v)


def _place():
    return lax.axis_index("x"), lax.axis_index("y"), lax.axis_index("c")


def _other_chips(x, y):
    return [(1 - x, y), (x, 1 - y), (1 - x, 1 - y)]


IN_HBM = pl.BlockSpec(memory_space=pltpu.HBM)
SEM = pl.BlockSpec(memory_space=pltpu.SEMAPHORE)
EFFECT = pltpu.SideEffectType.DATAFLOW_SIDE_EFFECTING
TOKEN = jax.ShapeDtypeStruct((8, LANES), F32)
TOKEN_SPEC = pl.BlockSpec(memory_space=pltpu.VMEM)


def _in_hbm(a):
    return pltpu.with_memory_space_constraint(a, pltpu.HBM)


def _landing(shape):
    return _in_hbm(lax.empty(shape.shape, shape.dtype))


def _hbm_like(a):
    return pltpu.HBM(a.shape, a.dtype)


def _gather_start(landing, slots, after, name):
    na = len(landing)

    def body(*refs):
        land = refs[:na]
        send_sems, recv_sems = refs[na + 1], refs[na + 2]
        token = refs[-1]
        x, y, c = _place()
        for a in range(na):
            own = slots[a](land[a], x, y, c)
            for k, to in enumerate([(x, y, 1 - c)] + [(*chip, c) for chip in _other_chips(x, y)]):
                pltpu.make_async_remote_copy(
                    src_ref=own, dst_ref=own, send_sem=send_sems.at[4 * a + k],
                    recv_sem=recv_sems.at[4 * a + k], device_id=to, device_id_type=MESH).start()
        token[...] = jnp.zeros_like(token)

    sems = pltpu.SemaphoreType.DMA((4 * na,))
    outs = _pcall(
        body, in_specs=[IN_HBM] * na + [HBM],
        out_specs=[SEM, SEM] + [IN_HBM] * na + [TOKEN_SPEC],
        out_shape=[sems, sems] + [_hbm_like(s) for s in landing] + [TOKEN],
        input_output_aliases={a: 2 + a for a in range(na)},
        compiler_params=pltpu.CompilerParams(has_side_effects=EFFECT), name=name,
    )(*[_in_hbm(s) for s in landing], after)
    return outs[0], outs[1], outs[2:2 + na], outs[-1]


def _gather_forward(gathered, send_sems, recv_sems, slots, after, name):
    na = len(gathered)

    def body(*refs):
        gath = refs[:na]
        send1, recv1 = refs[na], refs[na + 1]
        fsend, frecv = refs[na + 3], refs[na + 4]
        token = refs[-1]
        x, y, c = _place()
        chips = _other_chips(x, y)
        for a in range(na):
            for k, peer in enumerate([(x, y, 1 - c)] + [(*chip, c) for chip in chips]):
                arrival = pltpu.make_async_remote_copy(
                    src_ref=slots[a](gath[a], x, y, c), dst_ref=slots[a](gath[a], *peer), send_sem=send1.at[4 * a + k],
                    recv_sem=recv1.at[4 * a + k], device_id=peer, device_id_type=MESH)
                arrival.wait_send()
                arrival.wait_recv()
        for a in range(na):
            for j, chip in enumerate(chips):
                view = slots[a](gath[a], *chip, c)
                pltpu.make_async_remote_copy(
                    src_ref=view, dst_ref=view, send_sem=fsend.at[3 * a + j], recv_sem=frecv.at[3 * a + j],
                    device_id=(x, y, 1 - c), device_id_type=MESH).start()
        token[...] = jnp.zeros_like(token)

    sems = pltpu.SemaphoreType.DMA((3 * na,))
    outs = _pcall(
        body, in_specs=[IN_HBM] * na + [SEM, SEM, HBM],
        out_specs=[SEM, SEM] + [IN_HBM] * na + [TOKEN_SPEC],
        out_shape=[sems, sems] + [_hbm_like(g) for g in gathered] + [TOKEN],
        input_output_aliases={a: 2 + a for a in range(na)},
        compiler_params=pltpu.CompilerParams(has_side_effects=EFFECT), name=name,
    )(*gathered, send_sems, recv_sems, after)
    return outs[0], outs[1], outs[2:2 + na], outs[-1]


def _gather_finish(gathered, fsend, frecv, slots, after, name):
    na = len(gathered)

    def body(*refs):
        gath, fs, fr = refs[:na], refs[na], refs[na + 1]
        x, y, c = _place()
        for a in range(na):
            for j, chip in enumerate(_other_chips(x, y)):
                passed = pltpu.make_async_remote_copy(
                    src_ref=slots[a](gath[a], *chip, c), dst_ref=slots[a](gath[a], *chip, 1 - c),
                    send_sem=fs.at[3 * a + j], recv_sem=fr.at[3 * a + j], device_id=(x, y, 1 - c), device_id_type=MESH)
                passed.wait_send()
                passed.wait_recv()

    outs = _pcall(
        body, in_specs=[IN_HBM] * na + [SEM, SEM, HBM], out_specs=[IN_HBM] * na,
        out_shape=[_hbm_like(g) for g in gathered], input_output_aliases={a: a for a in range(na)},
        compiler_params=pltpu.CompilerParams(has_side_effects=EFFECT), name=name,
    )(*gathered, fsend, frecv, after)
    return list(outs)


def _pair_exchange(grads, views, recv_shapes, name):
    na = len(grads)

    def body(*refs):
        srcs, dsts = refs[:na], refs[na:2 * na]
        send_sems, recv_sems = refs[2 * na:]
        x, y, c = _place()
        copies = []
        for a in range(na):
            for chip in range(N_CHIP):
                copies.append(pltpu.make_async_remote_copy(
                    src_ref=views[a](srcs[a], chip, 1 - c), dst_ref=dsts[a].at[chip],
                    send_sem=send_sems.at[a * N_CHIP + chip], recv_sem=recv_sems.at[a * N_CHIP + chip],
                    device_id=(x, y, 1 - c), device_id_type=MESH))
        for cp in copies:
            cp.start()
        for cp in copies:
            cp.wait()

    return _pcall(body, in_specs=[HBM] * na, out_specs=[HBM] * na, out_shape=recv_shapes,
                  scratch_shapes=[pltpu.SemaphoreType.DMA((N_CHIP * na,)), pltpu.SemaphoreType.DMA((N_CHIP * na,))],
                  name=name)(*grads)


def _chip_start(pair, after, name):
    def body(src, land, after_ref, send_sems, recv_sems, src_thru, land_thru, token):
        x, y, c = _place()
        for j, (px, py) in enumerate(_other_chips(x, y)):
            pltpu.make_async_remote_copy(
                src_ref=src.at[2 * px + py], dst_ref=land.at[2 * x + y], send_sem=send_sems.at[j], recv_sem=recv_sems.at[j],
                device_id=(px, py, c), device_id_type=MESH).start()
        token[...] = jnp.zeros_like(token)

    sems = pltpu.SemaphoreType.DMA((3,))
    return _pcall(
        body, in_specs=[IN_HBM, IN_HBM, HBM], out_specs=[SEM, SEM, IN_HBM, IN_HBM, TOKEN_SPEC],
        out_shape=[sems, sems, _hbm_like(pair), _hbm_like(pair), TOKEN], input_output_aliases={0: 2, 1: 3},
        compiler_params=pltpu.CompilerParams(has_side_effects=EFFECT), name=name,
    )(_in_hbm(pair), _landing(pair), after)


def _chip_wait(pair, parts, send_sems, recv_sems, after, name):
    def body(src, land, send, recv, after_ref, src_thru, land_thru):
        x, y, c = _place()
        for j, (px, py) in enumerate(_other_chips(x, y)):
            copy = pltpu.make_async_remote_copy(
                src_ref=src.at[2 * px + py], dst_ref=land.at[2 * px + py], send_sem=send.at[j], recv_sem=recv.at[j],
                device_id=(px, py, c), device_id_type=MESH)
            copy.wait_send()
            copy.wait_recv()

    return _pcall(
        body, in_specs=[IN_HBM, IN_HBM, SEM, SEM, HBM], out_specs=[IN_HBM, IN_HBM],
        out_shape=[_hbm_like(pair), _hbm_like(parts)], input_output_aliases={0: 0, 1: 1},
        compiler_params=pltpu.CompilerParams(has_side_effects=EFFECT), name=name,
    )(pair, parts, send_sems, recv_sems, after)


def _pair_add(core, grad, recv, block, grad_spec, name):
    _, R, C = recv.shape
    tr = block

    def body(c_ref, g_ref, r_ref, o_ref):
        o_ref[...] = (g_ref[...].astype(F32) + r_ref[...].astype(F32)).astype(BF16)

    grid_spec = pltpu.PrefetchScalarGridSpec(
        num_scalar_prefetch=1, grid=(N_CHIP, R // tr),
        in_specs=[grad_spec, pl.BlockSpec((None, tr, C), lambda k, i, c: (k, i, 0))],
        out_specs=pl.BlockSpec((None, tr, C), lambda k, i, c: (k, i, 0)))
    return _pcall(body, grid_spec=grid_spec, out_shape=jax.ShapeDtypeStruct(recv.shape, BF16),
                  compiler_params=_params("parallel", "parallel"), name=name)(core, grad, recv)


def _all_reduce_small(part, name):
    R = part.shape[0]

    def body(p_ref, o_ref, all_ref, send_sems, recv_sems):
        x, y, c = _place()
        me = 4 * x + 2 * y + c
        all_ref[me] = p_ref[...]
        peers = [(x, y, 1 - c)] + [(px, py, pc) for px, py in _other_chips(x, y) for pc in (c, 1 - c)]
        copies = [pltpu.make_async_remote_copy(
            src_ref=p_ref, dst_ref=all_ref.at[me], send_sem=send_sems.at[k], recv_sem=recv_sems.at[k],
            device_id=peer, device_id_type=MESH) for k, peer in enumerate(peers)]
        for cp in copies:
            cp.start()
        for k, (px, py, pc) in enumerate(peers):
            pltpu.make_async_remote_copy(
                src_ref=p_ref, dst_ref=all_ref.at[4 * px + 2 * py + pc], send_sem=send_sems.at[k], recv_sem=recv_sems.at[k],
                device_id=peers[k], device_id_type=MESH).wait_recv()
        for cp in copies:
            cp.wait_send()
        acc = all_ref[0]
        for k in range(1, N_DEV):
            acc = acc + all_ref[k]
        o_ref[...] = acc

    vm = pl.BlockSpec(memory_space=pltpu.VMEM)
    return _pcall(body, in_specs=[vm], out_specs=vm, out_shape=jax.ShapeDtypeStruct((R, LANES), F32),
                  scratch_shapes=[pltpu.VMEM((N_DEV, R, LANES), F32), pltpu.SemaphoreType.DMA((7,)), pltpu.SemaphoreType.DMA((7,))],
                  name=name)(part)


def _local_step(x, tgt, gains, weights):
    g_pre_mix, g_post_mix, g_pre_ffn, g_post_ffn, g_sb, g_dil = gains
    S, D = x.shape
    hs = g_sb.shape[1] // HEAD_DIM
    hd = g_dil.shape[1] // HEAD_DIM
    cos2, sin_signed = _rope_tables(S)

    h1 = _rms_fwd(x, g_pre_mix + weights.start(), "rms_in")
    w_in_g = weights.w_in(h1)
    proj = _mm_nn(h1, w_in_g, F32, "proj", tn=768)
    o_sb, ct_sb, mx_sb = _sb_fwd(proj, g_sb, hs, "sb_fwd")
    o_dl, lse_dl, mx_dl = _dil_fwd(proj, cos2, sin_signed, g_dil + weights.forward_out(o_sb), 3 * hs, hd, "dil_fwd")
    w_out_g, dep = weights.w_out(o_dl)
    mixed = jnp.concatenate([mx_sb, mx_dl], axis=1)
    mix = _mm_nn(mixed, w_out_g, F32, "mix_out", tn=1024)
    x2, h2 = _mid_fwd(x, mix, g_post_mix + dep, g_pre_ffn, "mid_fwd")
    w_up_g, w_down_g, cwb = weights.ffn(h2)
    u = _mm_nn(h2, w_up_g, BF16, "ffn_up")
    y = _geglu_fwd(u, cwb, "geglu_fwd")
    f = _mm_nn(y, w_down_g, F32, "ffn_down", tn=1024, tk=1408)

    dy, df, dg_post_ffn, loss = _loss_bwd(x2, f, tgt, g_post_ffn, "loss_bwd")
    dyv = _mm_nt(df, w_down_g, BF16, "d_y", tn=1408)
    dw_down = _mm_tn(y, df, D, BF16, "dw_down", tm=1408, tn=1024)
    dc, dcw_g, dcw_v = _geglu_bwd(u, dyv, cwb + weights.grad("w_down", dw_down), "geglu_bwd")
    du = _conv_bwd(dc, cwb, "conv_bwd")
    dh2 = _mm_nt(du, w_up_g, F32, "d_h2", tk=1408)
    dw_up = _mm_tn(h2, du, w_up_g.shape[2], BF16, "dw_up")
    dx2, dmix, dg_pre_ffn, dg_post_mix = _mid_bwd(
        dy, dh2, x2, mix, g_pre_ffn + weights.grad("w_up", dw_up), g_post_mix, "mid_bwd")
    dmixed = _mm_nt(dmix, w_out_g, F32, "d_mixed")
    dw_out = _mm_tn(mixed, dmix, D, BF16, "dw_out", tn=1024)
    dq_s, dk_s, dv_s, dg_sb = _sb_bwd(proj, g_sb + weights.grad("w_out", dw_out), o_sb, ct_sb, dmixed, 0, hs, "sb_bwd")
    dq_d, dk_d, dv_d, dg_dil = _dil_bwd(proj, cos2, sin_signed, g_dil, o_dl, lse_dl, dmixed, hs, 3 * hs, hd, "dil_bwd")
    dproj = jnp.concatenate([dq_s, dk_s, dv_s, dq_d, dk_d, dv_d], axis=1)
    dh1 = _mm_nt(dproj, w_in_g, F32, "d_h1", tk=768)
    grad_x, dg_pre_mix = _first_bwd(dx2, dh1, x, g_pre_mix, "first_bwd")
    small = (dg_pre_mix, dg_post_mix, dg_pre_ffn, dg_post_ffn, dg_sb[0:1], dg_dil[0:1], jnp.concatenate([dcw_g, dcw_v], axis=1))
    dw_in = _mm_tn(h1, dproj, w_in_g.shape[2], BF16, "dw_in", tn=768, after=weights.small(small, loss))
    weights.grad("w_in", dw_in)
    return loss, grad_x, small


def _pad_cols(a, to):
    return jnp.pad(a, ((0, 0), (0, to - a.shape[1])))


def kernel(x, pre_mix_gain, post_mix_gain, pre_ffn_gain, post_ffn_gain, w_in, sb_out_gain, dil_out_gain, w_out, w_up, conv_w, conv_b, w_down, loss_target, m_pre_mix_gain, m_post_mix_gain, m_pre_ffn_gain, m_post_ffn_gain, m_w_in, m_sb_out_gain, m_dil_out_gain, m_w_out, m_w_up, m_conv_w, m_conv_b, m_w_down, v_pre_mix_gain, v_post_mix_gain, v_pre_ffn_gain, v_post_ffn_gain, v_w_in, v_sb_out_gain, v_dil_out_gain, v_w_out, v_w_up, v_conv_w, v_conv_b, v_w_down):
    xb, tb = x[0], loss_target[0]
    S, D = xb.shape
    w_in, w_out, w_up, w_down, conv_w = w_in[0], w_out[0], w_up[0], w_down[0], conv_w[0]
    n_in, e_rows = w_in.shape[1], w_out.shape[0]
    cu, half = w_up.shape[1], w_down.shape[0]
    assert cu == 2 * half and half % 16 == 0
    cup = -(-cu // LANES) * LANES
    fp = N_CHIP * cup
    px, py, pc = _place()
    me = 4 * px + 2 * py + pc
    core = jnp.reshape(pc, (1,)).astype(jnp.int32)

    shards = [w_in.astype(BF16), w_out.astype(BF16), _pad_cols(w_up, cup).astype(BF16), w_down.astype(BF16),
              jnp.pad(_pad_cols(conv_w, cup), ((0, 8 - conv_w.shape[0]), (0, 0)))]

    def by_dev(ref, qx, qy, qc):
        return ref.at[4 * qx + 2 * qy + qc]

    def down_slot(ref, qx, qy, qc):
        return ref.at[2 * qx + qy, pl.ds(qc * half, half)]

    def by_pair(ref, chip, k):
        return ref.at[chip, k]

    def down_pair(ref, chip, k):
        return ref.at[chip, pl.ds(k * half, half)]

    def pair_spec(tr, cols):
        return pl.BlockSpec((None, None, tr, cols), lambda k, i, c: (k, c[0], i, 0))

    tr_in, tr_up = _tile(D, 512, 16), _tile(D, 256, 16)
    grad_plan = {
        "w_in": ((N_CHIP, 2, D, n_in), by_pair, (D, n_in), tr_in, pair_spec(tr_in, n_in)),
        "w_out": ((N_CHIP, 2, e_rows, D), by_pair, (e_rows, D), e_rows, pair_spec(e_rows, D)),
        "w_up": ((N_CHIP, 2, D, cup), by_pair, (D, cup), tr_up, pair_spec(tr_up, cup)),
        "w_down": ((N_CHIP, cup, D), down_pair, (half, D), half,
                   pl.BlockSpec((None, half, D), lambda k, i, c: (k, c[0], 0))),
    }

    class Exchanges:
        def __init__(self):
            self.in_flight = {}

        def start(self):
            def own_slot(shard):
                return lax.dynamic_update_index_in_dim(lax.empty((N_DEV, *shard.shape), shard.dtype), shard, me, 0)

            down = lax.dynamic_update_slice(jnp.zeros((N_CHIP, cup, D), BF16), shards[3][None], (2 * px + py, pc * half, 0))
            landing = [own_slot(shards[0]), own_slot(shards[1]), own_slot(shards[2]), down, own_slot(shards[4])]
            self.ffn_slots = [by_dev, down_slot, by_dev]
            self.g_in = _gather_start(landing[:1], [by_dev], core, "gather_in_start")
            self.g_out = _gather_start(landing[1:2], [by_dev], self.g_in[3], "gather_out_start")
            self.g_ffn = _gather_start(landing[2:], self.ffn_slots, self.g_out[3], "gather_ffn_start")
            return self.g_ffn[3][0, 0]

        def w_in(self, after):
            send, recv, gath, _ = self.g_in
            fsend, frecv, gath, token = _gather_forward(gath, send, recv, [by_dev], after, "gather_in_forward")
            return _gather_finish(gath, fsend, frecv, [by_dev], token, "gather_in_finish")[0]

        def forward_out(self, after):
            send, recv, gath, _ = self.g_out
            self.p_out = _gather_forward(gath, send, recv, [by_dev], after, "gather_out_forward")
            return self.p_out[3][0, 0]

        def w_out(self, after):
            fsend, frecv, gath, _ = self.p_out
            w_out_g = _gather_finish(gath, fsend, frecv, [by_dev], after, "gather_out_finish")[0]
            send, recv, gath, _ = self.g_ffn
            self.p_ffn = _gather_forward(gath, send, recv, self.ffn_slots, w_out_g, "gather_ffn_forward")
            return w_out_g.reshape(1, N_DEV * e_rows, D), self.p_ffn[3][0, 0]

        def ffn(self, after):
            fsend, frecv, gath, _ = self.p_ffn
            w_up_g, w_down_g, cw_g = _gather_finish(gath, fsend, frecv, self.ffn_slots, after, "gather_ffn_finish")
            cb = _pad_cols(conv_b.reshape(N_DEV, cu), cup).reshape(1, 2 * fp)
            cw_full = jnp.transpose(cw_g[:, :3, :], (1, 0, 2)).reshape(3, 2 * fp)
            cwb = jnp.concatenate([cw_full, cb, jnp.zeros((4, 2 * fp), F32)], axis=0)
            return w_up_g, w_down_g.reshape(1, fp, D), cwb

        def small(self, small, loss):
            flat = jnp.concatenate([s.reshape(-1) for s in small] + [loss.reshape(-1)])
            rows = -(-flat.size // (8 * LANES)) * 8
            packed = jnp.pad(flat, (0, rows * LANES - flat.size)).reshape(rows, LANES)
            self.total = _all_reduce_small(packed, "reduce_small")
            return self.total

        def grad(self, name, dw):
            view_shape, view, block, tr, spec = grad_plan[name]
            dw = dw.reshape(view_shape)
            recv = _pair_exchange([dw], [view], [jax.ShapeDtypeStruct((N_CHIP, *block), BF16)], "pair_exchange_" + name)[0]
            pair = _pair_add(core, dw, recv, tr, spec, "pair_add_" + name)
            send, recv_sems, pair, parts, token = _chip_start(pair, recv, "chip_start_" + name)
            self.in_flight[name] = (pair, parts, send, recv_sems)
            self.last_token = token
            return token[0, 0]

        def grad_parts(self, name, after):
            return _chip_wait(*self.in_flight[name], after, "chip_wait_" + name)

    exchanges = Exchanges()
    gains = (pre_mix_gain, post_mix_gain, pre_ffn_gain, post_ffn_gain, sb_out_gain, dil_out_gain)
    loss, grad_x, small = _local_step(xb, tb, gains, exchanges)

    sizes = [s.size for s in small]
    total = exchanges.total.reshape(-1)
    offs = [0]
    for s in sizes:
        offs.append(offs[-1] + s)
    red = [total[offs[k]:offs[k + 1]].reshape(small[k].shape) for k in range(len(small))]
    loss_out = total[offs[-1]]
    g_pre_mix, g_post_mix, g_pre_ffn, g_post_ffn, g_sb, g_dil, g_conv = red
    g_conv_b = g_conv[3].reshape(N_DEV, cup)[:, :cu].reshape(1, N_DEV * cu)
    g_conv_w = lax.dynamic_index_in_dim(g_conv[0:3].reshape(3, N_DEV, cup), me, axis=1, keepdims=False)[:, :cu]

    def small_adam(w, g, m, v, name):
        one = w.shape[0] == 1
        if one:
            w, g, m, v = (jnp.broadcast_to(t, (8, t.shape[1])) for t in (w, g, m, v))
        outs = _adamw(w, g[None], m, v, name)
        return [o[0:1] for o in outs] if one else outs

    chip_ids = jnp.stack([2 * px + py, 2 * (1 - px) + py, 2 * px + 1 - py, 2 * (1 - px) + 1 - py]).astype(jnp.int32)
    out_w_down = _adamw_chips(w_down, *exchanges.grad_parts("w_down", exchanges.last_token), chip_ids, m_w_down[0], v_w_down[0], "adam_w_down")
    out_w_up = _adamw_chips(w_up, *exchanges.grad_parts("w_up", out_w_down[1]), chip_ids, m_w_up[0], v_w_up[0], "adam_w_up")
    out_w_out = _adamw_chips(w_out, *exchanges.grad_parts("w_out", out_w_up[1]), chip_ids, m_w_out[0], v_w_out[0], "adam_w_out")
    out_w_in = _adamw_chips(w_in, *exchanges.grad_parts("w_in", out_w_out[1]), chip_ids, m_w_in[0], v_w_in[0], "adam_w_in")
    out_pre_mix = small_adam(pre_mix_gain, g_pre_mix, m_pre_mix_gain, v_pre_mix_gain, "adam_pre_mix")
    out_post_mix = small_adam(post_mix_gain, g_post_mix, m_post_mix_gain, v_post_mix_gain, "adam_post_mix")
    out_pre_ffn = small_adam(pre_ffn_gain, g_pre_ffn, m_pre_ffn_gain, v_pre_ffn_gain, "adam_pre_ffn")
    out_post_ffn = small_adam(post_ffn_gain, g_post_ffn, m_post_ffn_gain, v_post_ffn_gain, "adam_post_ffn")
    out_sb = small_adam(sb_out_gain, g_sb, m_sb_out_gain, v_sb_out_gain, "adam_sb_gain")
    out_dil = small_adam(dil_out_gain, g_dil, m_dil_out_gain, v_dil_out_gain, "adam_dil_gain")
    out_conv_b = small_adam(conv_b, g_conv_b, m_conv_b, v_conv_b, "adam_conv_b")
    cw8 = [jnp.pad(t, ((0, 5), (0, 0))) for t in (conv_w, g_conv_w, m_conv_w[0], v_conv_w[0])]
    out_conv_w = [o[0:3] for o in _adamw(cw8[0], cw8[1][None], cw8[2], cw8[3], "adam_conv_w")]

    order = [out_pre_mix, out_post_mix, out_pre_ffn, out_post_ffn, [o[None] for o in out_w_in], out_sb, out_dil,
             [o[None] for o in out_w_out], [o[None] for o in out_w_up], [o[None] for o in out_conv_w], out_conv_b,
             [o[None] for o in out_w_down]]
    outs = [loss_out, grad_x[None]]
    for k in range(4):
        outs += [o[k] for o in order]
    return tuple(outs)
```

```python
import functools
import math

import jax
import jax.numpy as jnp
from jax import lax
from jax.experimental import pallas as pl
from jax.experimental.pallas import tpu as pltpu

F32 = jnp.float32
BF16 = jnp.bfloat16
HEAD_DIM = 128
LANES = 128
KEY_BLOCK = 128
DILATIONS = (1, 4, 16)
RMS_EPS = 1e-6
ROPE_THETA = 10000.0
NEG = -1e30
ADAM_LR, ADAM_B1, ADAM_B2, ADAM_EPS, ADAM_WD, ADAM_STEP = 0.001, 0.9, 0.999, 1e-08, 0.01, 10
MESH = pl.DeviceIdType.MESH
N_DEV = 8
N_CHIP = 4
HBM = pl.BlockSpec(memory_space=pl.ANY)
VMEM_LIMIT = 56 * 1024 * 1024

_pcall = pl.pallas_call


def _tile(n, pref, mult=LANES):
    best = None
    t = mult
    while t <= min(n, pref):
        if n % t == 0:
            best = t
        t += mult
    return n if best is None else best


def _params(*sem):
    return pltpu.CompilerParams(dimension_semantics=sem, vmem_limit_bytes=VMEM_LIMIT)


def _dot(a, b, dims):
    return lax.dot_general(a, b, (dims, ((), ())), preferred_element_type=F32)


NN = ((1,), (0,))
NT = ((1,), (1,))
TN = ((0,), (0,))


def _mm_body(dims, nk, tile):
    if nk == 1:
        def single(a_ref, b_ref, o_ref):
            o_ref[...] = _dot(a_ref[...].astype(BF16), b_ref[...].astype(BF16), dims).astype(o_ref.dtype)

        return single, []

    def body(a_ref, b_ref, o_ref, acc_ref):
        k = pl.program_id(2)

        @pl.when(k == 0)
        def _():
            acc_ref[...] = jnp.zeros_like(acc_ref)

        acc_ref[...] += _dot(a_ref[...].astype(BF16), b_ref[...].astype(BF16), dims)

        @pl.when(k == nk - 1)
        def _():
            o_ref[...] = acc_ref[...].astype(o_ref.dtype)

    return body, [pltpu.VMEM(tile, F32)]


def _mm_nn(a, b3, out_dtype, name, tm=1024, tn=1408, tk=2048):
    M, K = a.shape
    C, _, n = b3.shape
    tm, tk, tn = _tile(M, tm, 8), _tile(K, tk), _tile(n, tn)
    npc, nk = n // tn, K // tk
    body, scratch = _mm_body(NN, nk, (tm, tn))
    return _pcall(
        body, grid=(M // tm, C * npc, nk),
        in_specs=[pl.BlockSpec((tm, tk), lambda i, j, k: (i, k)),
                  pl.BlockSpec((None, tk, tn), lambda i, j, k: (j // npc, k, j % npc))],
        out_specs=pl.BlockSpec((tm, tn), lambda i, j, k: (i, j)),
        out_shape=jax.ShapeDtypeStruct((M, C * n), out_dtype), scratch_shapes=scratch,
        compiler_params=_params("parallel", "parallel", "arbitrary"), name=name)(a, b3)


def _mm_nt(a, b3, out_dtype, name, tm=1024, tn=1024, tk=2048):
    M, _ = a.shape
    C, N, n = b3.shape
    tm, tn, tk = _tile(M, tm, 8), _tile(N, tn), _tile(n, tk)
    kpc = n // tk
    nk = C * kpc
    body, scratch = _mm_body(NT, nk, (tm, tn))
    return _pcall(
        body, grid=(M // tm, N // tn, nk),
        in_specs=[pl.BlockSpec((tm, tk), lambda i, j, k: (i, k)),
                  pl.BlockSpec((None, tn, tk), lambda i, j, k: (k // kpc, j, k % kpc))],
        out_specs=pl.BlockSpec((tm, tn), lambda i, j, k: (i, j)),
        out_shape=jax.ShapeDtypeStruct((M, N), out_dtype), scratch_shapes=scratch,
        compiler_params=_params("parallel", "parallel", "arbitrary"), name=name)(a, b3)


def _mm_tn(x, y, n, out_dtype, name, tm=1024, tn=1408, tk=2048, after=None):
    S, P = x.shape
    C = y.shape[1] // n
    tm, tn, tk = _tile(P, tm), _tile(n, tn), _tile(S, tk, 8)
    npc, nk = n // tn, S // tk
    inner, scratch = _mm_body(TN, nk, (tm, tn))
    extra = [] if after is None else [after]

    def body(x_ref, y_ref, *rest):
        inner(x_ref, y_ref, *rest[len(extra):])

    return _pcall(
        body, grid=(P // tm, C * npc, nk),
        in_specs=[pl.BlockSpec((tk, tm), lambda i, j, k: (k, i)),
                  pl.BlockSpec((tk, tn), lambda i, j, k: (k, j))] + [HBM] * len(extra),
        out_specs=pl.BlockSpec((None, tm, tn), lambda i, j, k: (j // npc, i, j % npc)),
        out_shape=jax.ShapeDtypeStruct((C, P, n), out_dtype), scratch_shapes=scratch,
        compiler_params=_params("parallel", "parallel", "arbitrary"), name=name)(x, y, *extra)


def _rms_scale(v):
    return lax.rsqrt(jnp.mean(v * v, axis=-1, keepdims=True) + RMS_EPS)


def _rms_bwd(gy, v, r):
    return r * gy - v * (r * r * r * jnp.mean(gy * v, axis=-1, keepdims=True))


def _rows_spec(tm, d):
    return pl.BlockSpec((tm, d), lambda i: (i, 0))


def _vec_spec(d):
    return pl.BlockSpec((1, d), lambda i: (0, 0))


def _rms_fwd(x, g, name, tm=256):
    S, D = x.shape

    def body(x_ref, g_ref, h_ref):
        v = x_ref[...]
        h_ref[...] = (v * _rms_scale(v) * g_ref[...]).astype(BF16)

    return _pcall(body, grid=(S // tm,), in_specs=[_rows_spec(tm, D), _vec_spec(D)], out_specs=_rows_spec(tm, D),
                  out_shape=jax.ShapeDtypeStruct((S, D), BF16), compiler_params=_params("parallel"), name=name)(x, g)


def _mid_fwd(x, mix, g_post, g_pre, name, tm=256):
    S, D = x.shape

    def body(x_ref, m_ref, gp_ref, gn_ref, x2_ref, h_ref):
        m = m_ref[...]
        x2 = x_ref[...] + m * _rms_scale(m) * gp_ref[...]
        x2_ref[...] = x2
        h_ref[...] = (x2 * _rms_scale(x2) * gn_ref[...]).astype(BF16)

    return _pcall(body, grid=(S // tm,), in_specs=[_rows_spec(tm, D), _rows_spec(tm, D), _vec_spec(D), _vec_spec(D)],
                  out_specs=[_rows_spec(tm, D), _rows_spec(tm, D)],
                  out_shape=[jax.ShapeDtypeStruct((S, D), F32), jax.ShapeDtypeStruct((S, D), BF16)],
                  compiler_params=_params("parallel"), name=name)(x, mix, g_post, g_pre)


def _loss_bwd(x2, f, tgt, g_post, name, tm=256):
    S, D = x2.shape

    def body(x2_ref, f_ref, t_ref, g_ref, dy_ref, df_ref, dg_ref, ls_ref):
        i = pl.program_id(0)

        @pl.when(i == 0)
        def _():
            dg_ref[...] = jnp.zeros_like(dg_ref)
            ls_ref[...] = jnp.zeros_like(ls_ref)

        fv = f_ref[...]
        r = _rms_scale(fv)
        g = g_ref[...]
        err = x2_ref[...] + fv * r * g - t_ref[...]
        ls_ref[...] += jnp.broadcast_to(0.5 * jnp.sum(jnp.mean(err * err, axis=-1, keepdims=True), axis=0, keepdims=True), ls_ref.shape)
        dy = err * (1.0 / D)
        dy_ref[...] = dy
        df_ref[...] = _rms_bwd(dy * g, fv, r).astype(BF16)
        dg_ref[...] += jnp.sum(dy * fv * r, axis=0, keepdims=True)

    return _pcall(body, grid=(S // tm,),
                  in_specs=[_rows_spec(tm, D), _rows_spec(tm, D), _rows_spec(tm, D), _vec_spec(D)],
                  out_specs=[_rows_spec(tm, D), _rows_spec(tm, D), _vec_spec(D), _vec_spec(LANES)],
                  out_shape=[jax.ShapeDtypeStruct((S, D), F32), jax.ShapeDtypeStruct((S, D), BF16),
                             jax.ShapeDtypeStruct((1, D), F32), jax.ShapeDtypeStruct((1, LANES), F32)],
                  compiler_params=_params("arbitrary"), name=name)(x2, f, tgt, g_post)


def _mid_bwd(dy, dh2, x2, mix, g_pre, g_post, name, tm=256):
    S, D = dy.shape

    def body(dy_ref, dh_ref, x2_ref, m_ref, gn_ref, gp_ref, dx2_ref, dm_ref, dgn_ref, dgp_ref):
        i = pl.program_id(0)

        @pl.when(i == 0)
        def _():
            dgn_ref[...] = jnp.zeros_like(dgn_ref)
            dgp_ref[...] = jnp.zeros_like(dgp_ref)

        x2, dh = x2_ref[...], dh_ref[...]
        r = _rms_scale(x2)
        dx2 = dy_ref[...] + _rms_bwd(dh * gn_ref[...], x2, r)
        dgn_ref[...] += jnp.sum(dh * x2 * r, axis=0, keepdims=True)
        dx2_ref[...] = dx2
        m = m_ref[...]
        rm = _rms_scale(m)
        dm_ref[...] = _rms_bwd(dx2 * gp_ref[...], m, rm).astype(BF16)
        dgp_ref[...] += jnp.sum(dx2 * m * rm, axis=0, keepdims=True)

    return _pcall(body, grid=(S // tm,),
                  in_specs=[_rows_spec(tm, D)] * 4 + [_vec_spec(D)] * 2,
                  out_specs=[_rows_spec(tm, D), _rows_spec(tm, D), _vec_spec(D), _vec_spec(D)],
                  out_shape=[jax.ShapeDtypeStruct((S, D), F32), jax.ShapeDtypeStruct((S, D), BF16),
                             jax.ShapeDtypeStruct((1, D), F32), jax.ShapeDtypeStruct((1, D), F32)],
                  compiler_params=_params("arbitrary"), name=name)(dy, dh2, x2, mix, g_pre, g_post)


def _first_bwd(dx2, dh1, x, g_pre, name, tm=256):
    S, D = x.shape

    def body(dx2_ref, dh_ref, x_ref, g_ref, gx_ref, dg_ref):
        i = pl.program_id(0)

        @pl.when(i == 0)
        def _():
            dg_ref[...] = jnp.zeros_like(dg_ref)

        xv, dh = x_ref[...], dh_ref[...]
        r = _rms_scale(xv)
        gx_ref[...] = dx2_ref[...] + _rms_bwd(dh * g_ref[...], xv, r)
        dg_ref[...] += jnp.sum(dh * xv * r, axis=0, keepdims=True)

    return _pcall(body, grid=(S // tm,), in_specs=[_rows_spec(tm, D)] * 3 + [_vec_spec(D)],
                  out_specs=[_rows_spec(tm, D), _vec_spec(D)],
                  out_shape=[jax.ShapeDtypeStruct((S, D), F32), jax.ShapeDtypeStruct((1, D), F32)],
                  compiler_params=_params("arbitrary"), name=name)(dx2, dh1, x, g_pre)


def _logsig_pair(z):
    lb = jnp.minimum(z, 0.0) - jnp.log(1.0 + jnp.exp(-jnp.abs(z)))
    return lb, lb - z


SB_KEY_BLOCK = 256


def _sum_matrix(strict):
    ia = lax.broadcasted_iota(jnp.int32, (SB_KEY_BLOCK, SB_KEY_BLOCK), 0)
    ib = lax.broadcasted_iota(jnp.int32, (SB_KEY_BLOCK, SB_KEY_BLOCK), 1)
    tri = (ia > ib) if strict == ">" else (ia < ib)
    return jnp.concatenate([tri.astype(BF16), jnp.ones((SB_KEY_BLOCK, LANES), BF16)], axis=1)


def _lanes(c, width):
    return jnp.tile(c, (1, width // LANES))


def _split_dot(v, u):
    hi = v.astype(BF16)
    lo = (v - hi.astype(F32)).astype(BF16)
    return _dot(hi, u, NN) + _dot(lo, u, NN)


def _head_out(o, g):
    return o * _rms_scale(o) * g


def _sb_fwd(proj, gain, n_heads, name, tq=512):
    S = proj.shape[0]
    H, tk = n_heads, SB_KEY_BLOCK
    tq = _tile(S, tq, 2 * tk)
    scale = HEAD_DIM ** -0.5

    def body(q_ref, k_ref, v_ref, g_ref, o_ref, ct_ref, mx_ref, oacc, cacc):
        i = pl.program_id(1)
        oacc[...] = jnp.zeros_like(oacc)
        cacc[...] = jnp.zeros_like(cacc)
        sums = _sum_matrix(">")

        def block(k0, r0, diagonal):
            rows = pl.ds(r0, tq - r0)
            q = q_ref[rows, :].astype(BF16)
            kj = k_ref[pl.ds(k0, tk), :].astype(BF16)
            vj = v_ref[pl.ds(k0, tk), :].astype(BF16)
            lb, lk = _logsig_pair(_dot(q, kj, NT) * scale)
            if diagonal:
                causal = (lax.broadcasted_iota(jnp.int32, (tq - r0, tk), 1) < lax.broadcasted_iota(jnp.int32, (tq - r0, tk), 0))
                lk = jnp.where(causal, lk, 0.0)
            both = _split_dot(lk, sums)
            c = cacc[rows, :]
            a = jnp.exp(lb + both[:, :tk] + _lanes(c, tk))
            if diagonal:
                a = jnp.where(causal, a, 0.0)
            oacc[rows, :] += _dot(a.astype(BF16), vj, NN)
            cacc[rows, :] = c + both[:, tk:]

        for d in reversed(range(tq // tk)):
            block(pl.multiple_of(i * tq + d * tk, tk), d * tk, True)
        n_pairs = i * (tq // tk // 2)

        def step(it, carry):
            k0 = pl.multiple_of((n_pairs - 1 - it) * 2 * tk, 2 * tk)
            block(pl.multiple_of(k0 + tk, tk), 0, False)
            block(k0, 0, False)
            return carry

        lax.fori_loop(0, n_pairs, step, 0)
        o = oacc[...]
        o_ref[...] = o
        ct_ref[...] = cacc[...]
        mx_ref[...] = _head_out(o, g_ref[...]).astype(BF16)

    blk = pl.BlockSpec((tq, HEAD_DIM), lambda h, i: (i, h))
    return _pcall(
        body, grid=(H, S // tq),
        in_specs=[blk, pl.BlockSpec((S, HEAD_DIM), lambda h, i: (0, H + h)),
                  pl.BlockSpec((S, HEAD_DIM), lambda h, i: (0, 2 * H + h)), pl.BlockSpec((1, HEAD_DIM), lambda h, i: (0, h))],
        out_specs=[blk, blk, blk],
        out_shape=[jax.ShapeDtypeStruct((S, H * HEAD_DIM), F32), jax.ShapeDtypeStruct((S, H * HEAD_DIM), F32),
                   jax.ShapeDtypeStruct((S, H * HEAD_DIM), BF16)],
        scratch_shapes=[pltpu.VMEM((tq, HEAD_DIM), F32), pltpu.VMEM((tq, LANES), F32)],
        compiler_params=_params("parallel", "arbitrary"), name=name)(proj, proj, proj, gain)


def _sb_bwd(proj, gain, o_raw, ctot, dmixed, dm_col0, n_heads, name, tq=512):
    S = proj.shape[0]
    H, tk = n_heads, SB_KEY_BLOCK
    tq = _tile(S, tq, 2 * tk)
    nq = S // tq
    scale = HEAD_DIM ** -0.5

    def body(q_ref, k_ref, v_ref, g_ref, o_ref, ct_ref, dm_ref, dq_ref, dk_ref, dv_ref, dg_ref,
             dkacc, dvacc, dqacc, pfx, gcar, dos):
        i = pl.program_id(1)

        @pl.when(i == 0)
        def _():
            dkacc[...] = jnp.zeros_like(dkacc)
            dvacc[...] = jnp.zeros_like(dvacc)
            dg_ref[...] = jnp.zeros_like(dg_ref)

        o, dm, g = o_ref[...], dm_ref[...], g_ref[...]
        r = _rms_scale(o)
        dos[...] = _rms_bwd(dm * g, o, r).astype(BF16)
        dg_ref[...] += jnp.broadcast_to(jnp.sum(dm * o * r, axis=0, keepdims=True), dg_ref.shape)
        dqacc[...] = jnp.zeros_like(dqacc)
        pfx[...] = jnp.zeros_like(pfx)
        gcar[...] = jnp.zeros_like(gcar)
        later, earlier = _sum_matrix(">"), _sum_matrix("<")

        def block(k0, r0, diagonal):
            rows = pl.ds(r0, tq - r0)
            keys = pl.ds(k0, tk)
            q, do = q_ref[rows, :].astype(BF16), dos[rows, :]
            kj, vj = k_ref[keys, :].astype(BF16), v_ref[keys, :].astype(BF16)
            lb, lk = _logsig_pair(_dot(q, kj, NT) * scale)
            if diagonal:
                causal = (lax.broadcasted_iota(jnp.int32, (tq - r0, tk), 1) < lax.broadcasted_iota(jnp.int32, (tq - r0, tk), 0))
                lk = jnp.where(causal, lk, 0.0)
            both = _split_dot(lk, later)
            p = pfx[rows, :] + both[:, tk:]
            a = jnp.exp(lb + both[:, :tk] + _lanes(ct_ref[rows, :] - p, tk))
            if diagonal:
                a = jnp.where(causal, a, 0.0)
            dl = _dot(do, vj, NT) * a
            dvacc[keys, :] += _dot(a.astype(BF16), do, TN)
            both = _split_dot(dl, earlier)
            gc = gcar[rows, :]
            sig = jnp.exp(lb)
            gsum = (both[:, :tk] + _lanes(gc, tk)) * sig
            if diagonal:
                gsum = jnp.where(causal, gsum, 0.0)
            dz = ((dl * (1.0 - sig) - gsum) * scale).astype(BF16)
            dqacc[rows, :] += _dot(dz, kj, NN)
            dkacc[keys, :] += _dot(dz, q, TN)
            pfx[rows, :] = p
            gcar[rows, :] = gc + both[:, tk:]

        def step(j, carry):
            k0 = pl.multiple_of(j * 2 * tk, 2 * tk)
            block(k0, 0, False)
            block(pl.multiple_of(k0 + tk, tk), 0, False)
            return carry

        lax.fori_loop(0, i * (tq // tk // 2), step, 0)
        for d in range(tq // tk):
            block(pl.multiple_of(i * tq + d * tk, tk), d * tk, True)
        dq_ref[...] = dqacc[...].astype(BF16)

        @pl.when(i == nq - 1)
        def _():
            dk_ref[...] = dkacc[...].astype(BF16)
            dv_ref[...] = dvacc[...].astype(BF16)

    blk = pl.BlockSpec((tq, HEAD_DIM), lambda h, i: (i, h))
    full = pl.BlockSpec((S, HEAD_DIM), lambda h, i: (0, h))
    W = H * HEAD_DIM
    return _pcall(
        body, grid=(H, nq),
        in_specs=[blk, pl.BlockSpec((S, HEAD_DIM), lambda h, i: (0, H + h)),
                  pl.BlockSpec((S, HEAD_DIM), lambda h, i: (0, 2 * H + h)), pl.BlockSpec((1, HEAD_DIM), lambda h, i: (0, h)),
                  blk, blk, pl.BlockSpec((tq, HEAD_DIM), lambda h, i: (i, dm_col0 + h))],
        out_specs=[blk, full, full, pl.BlockSpec((8, HEAD_DIM), lambda h, i: (0, h))],
        out_shape=[jax.ShapeDtypeStruct((S, W), BF16), jax.ShapeDtypeStruct((S, W), BF16),
                   jax.ShapeDtypeStruct((S, W), BF16), jax.ShapeDtypeStruct((8, W), F32)],
        scratch_shapes=[pltpu.VMEM((S, HEAD_DIM), F32), pltpu.VMEM((S, HEAD_DIM), F32), pltpu.VMEM((tq, HEAD_DIM), F32),
                        pltpu.VMEM((tq, LANES), F32), pltpu.VMEM((tq, LANES), F32), pltpu.VMEM((tq, HEAD_DIM), BF16)],
        compiler_params=_params("arbitrary", "arbitrary"), name=name)(proj, proj, proj, gain, o_raw, ctot, dmixed)


def _rope_tables(S):
    inv_freq = ROPE_THETA ** (-jnp.arange(0, HEAD_DIM, 2, dtype=F32) / HEAD_DIM)
    ang = jnp.arange(S, dtype=F32)[:, None] * inv_freq[None, :]
    cos, sin = jnp.cos(ang), jnp.sin(ang)
    return jnp.concatenate([cos, cos], axis=1), jnp.concatenate([-sin, sin], axis=1)


def _rope(v, cos2, sin_signed):
    return v * cos2 + pltpu.roll(v, HEAD_DIM // 2, axis=1) * sin_signed


def _dil_rows(d, r, l0, n):
    if d == 1:
        return pl.ds(l0 if isinstance(l0, int) else pl.multiple_of(l0, KEY_BLOCK), n)
    return pl.ds(r + d * l0, n, stride=d)


def _dil_blocks(S, visit):
    B = KEY_BLOCK
    group = 4
    for b, d in enumerate(DILATIONS):
        nb = S // d // B
        if nb == 1:
            g = math.gcd(d, group)

            def trip(t, carry, b=b, d=d, g=g):
                for u in range(g):
                    visit(b, d, t * g + u, 0, True)
                return carry

            lax.fori_loop(0, d // g, trip, 0)
        elif d == 1:
            visit(b, d, 0, 0, True)
            g = max(k for k in range(1, group + 2) if (nb - 1) % k == 0)

            def trip(t, carry, b=b, d=d, g=g):
                for u in range(g):
                    visit(b, d, 0, (1 + t * g + u) * B, False)
                return carry

            lax.fori_loop(0, (nb - 1) // g, trip, 0)
        else:
            def trip(r, carry, b=b, d=d, nb=nb):
                visit(b, d, r, 0, True)
                for n in range(1, nb):
                    visit(b, d, r, n * B, False)
                return carry

            lax.fori_loop(0, d, trip, 0)


def _dil_mask(first):
    B = KEY_BLOCK
    nk = B if first else 2 * B
    iq = lax.broadcasted_iota(jnp.int32, (B, nk), 0)
    ik = lax.broadcasted_iota(jnp.int32, (B, nk), 1)
    return (ik <= iq) if first else ((ik >= iq) & (ik <= iq + B))


def _dil_fwd(proj, cos2, sin_signed, gain, col0, n_heads, name):
    S = proj.shape[0]
    H, B = n_heads, KEY_BLOCK
    scale = HEAD_DIM ** -0.5
    rc = _tile(S, 256, 8)

    def body(q_ref, k_ref, v_ref, c_ref, s_ref, g_ref, o_ref, l_ref, mx_ref, qr, kr, *per_branch):
        ob, lb = per_branch[:len(DILATIONS)], per_branch[len(DILATIONS):]

        def rope_rows(t, carry):
            rows = pl.ds(pl.multiple_of(t * rc, rc), rc)
            qr[rows, :] = _rope(q_ref[rows, :], c_ref[rows, :], s_ref[rows, :])
            kr[rows, :] = _rope(k_ref[rows, :], c_ref[rows, :], s_ref[rows, :])
            return carry

        lax.fori_loop(0, S // rc, rope_rows, 0)

        def visit(b, d, r, l0, first):
            nk = B if first else 2 * B
            qrows = _dil_rows(d, r, l0, B)
            krows = qrows if first else _dil_rows(d, r, l0 - B, nk)
            s = _dot(qr[qrows, :].astype(BF16), kr[krows, :].astype(BF16), NT) * scale
            s = jnp.where(_dil_mask(first), s, NEG)
            m = jnp.max(s, axis=1, keepdims=True)
            p = jnp.exp(s - m)
            den = jnp.sum(p, axis=1, keepdims=True)
            ob[b][qrows, :] = _dot(p.astype(BF16), v_ref[krows, :].astype(BF16), NN) / den
            lb[b][qrows, :] = jnp.broadcast_to(m + jnp.log(den), (B, LANES))

        _dil_blocks(S, visit)

        def combine(t, carry):
            rows = pl.ds(pl.multiple_of(t * rc, rc), rc)
            l0, l1, l2 = lb[0][rows, :], lb[1][rows, :], lb[2][rows, :]
            m = jnp.maximum(jnp.maximum(l0, l1), l2)
            w0, w1, w2 = jnp.exp(l0 - m), jnp.exp(l1 - m), jnp.exp(l2 - m)
            den = w0 + w1 + w2
            o = (w0 * ob[0][rows, :] + w1 * ob[1][rows, :] + w2 * ob[2][rows, :]) / den
            o_ref[rows, :] = o
            l_ref[rows, :] = m + jnp.log(den)
            mx_ref[rows, :] = _head_out(o, g_ref[...]).astype(BF16)
            return carry

        lax.fori_loop(0, S // rc, combine, 0)

    def col(k):
        return pl.BlockSpec((S, HEAD_DIM), lambda h: (0, col0 + k * H + h))

    tab = pl.BlockSpec((S, HEAD_DIM), lambda h: (0, 0))
    out = pl.BlockSpec((S, HEAD_DIM), lambda h: (0, h))
    W = H * HEAD_DIM
    return _pcall(
        body, grid=(H,),
        in_specs=[col(0), col(1), col(2), tab, tab, pl.BlockSpec((1, HEAD_DIM), lambda h: (0, h))],
        out_specs=[out, out, out],
        out_shape=[jax.ShapeDtypeStruct((S, W), F32), jax.ShapeDtypeStruct((S, W), F32), jax.ShapeDtypeStruct((S, W), BF16)],
        scratch_shapes=[pltpu.VMEM((S, HEAD_DIM), F32)] * (2 + 2 * len(DILATIONS)),
        compiler_params=_params("parallel"), name=name)(proj, proj, proj, cos2, sin_signed, gain)


def _dil_bwd(proj, cos2, sin_signed, gain, o_raw, lse, dmixed, dm_col0, col0, n_heads, name):
    S = proj.shape[0]
    H, B = n_heads, KEY_BLOCK
    scale = HEAD_DIM ** -0.5
    rc = _tile(S, 256, 8)

    def body(q_ref, k_ref, v_ref, c_ref, s_ref, g_ref, o_ref, l_ref, dm_ref, dq_ref, dk_ref, dv_ref, dg_ref,
             qr, kr, dos, dsum, dqr, dkr, dvv):
        dg_ref[...] = jnp.zeros_like(dg_ref)

        def prep(t, carry):
            rows = pl.ds(pl.multiple_of(t * rc, rc), rc)
            qr[rows, :] = _rope(q_ref[rows, :], c_ref[rows, :], s_ref[rows, :])
            kr[rows, :] = _rope(k_ref[rows, :], c_ref[rows, :], s_ref[rows, :])
            o, dm = o_ref[rows, :], dm_ref[rows, :]
            r = _rms_scale(o)
            do = _rms_bwd(dm * g_ref[...], o, r)
            dg_ref[...] += jnp.broadcast_to(jnp.sum(dm * o * r, axis=0, keepdims=True), dg_ref.shape)
            dos[rows, :] = do
            dsum[rows, :] = jnp.broadcast_to(jnp.sum(do * o, axis=1, keepdims=True), (rc, LANES))
            dqr[rows, :] = jnp.zeros((rc, HEAD_DIM), F32)
            dkr[rows, :] = jnp.zeros((rc, HEAD_DIM), F32)
            dvv[rows, :] = jnp.zeros((rc, HEAD_DIM), F32)
            return carry

        lax.fori_loop(0, S // rc, prep, 0)

        def visit(b, d, r, l0, first):
            nk = B if first else 2 * B
            qrows = _dil_rows(d, r, l0, B)
            krows = qrows if first else _dil_rows(d, r, l0 - B, nk)
            qs, ks = qr[qrows, :].astype(BF16), kr[krows, :].astype(BF16)
            do = dos[qrows, :].astype(BF16)
            s = _dot(qs, ks, NT) * scale
            s = jnp.where(_dil_mask(first), s, NEG)
            p = jnp.exp(s - l_ref[qrows, :][:, 0:1])
            dp = _dot(do, v_ref[krows, :].astype(BF16), NT)
            ds = (p * (dp - dsum[qrows, :][:, 0:1]) * scale).astype(BF16)
            dqr[qrows, :] += _dot(ds, ks, NN)
            dkr[krows, :] += _dot(ds, qs, TN)
            dvv[krows, :] += _dot(p.astype(BF16), do, TN)

        _dil_blocks(S, visit)

        def finish(t, carry):
            rows = pl.ds(pl.multiple_of(t * rc, rc), rc)
            c, s = c_ref[rows, :], s_ref[rows, :]
            dq, dk = dqr[rows, :], dkr[rows, :]
            dq_ref[rows, :] = (dq * c + pltpu.roll(dq * s, HEAD_DIM // 2, axis=1)).astype(BF16)
            dk_ref[rows, :] = (dk * c + pltpu.roll(dk * s, HEAD_DIM // 2, axis=1)).astype(BF16)
            dv_ref[rows, :] = dvv[rows, :].astype(BF16)
            return carry

        lax.fori_loop(0, S // rc, finish, 0)

    def col(k):
        return pl.BlockSpec((S, HEAD_DIM), lambda h: (0, col0 + k * H + h))

    tab = pl.BlockSpec((S, HEAD_DIM), lambda h: (0, 0))
    out = pl.BlockSpec((S, HEAD_DIM), lambda h: (0, h))
    W = H * HEAD_DIM
    big = pltpu.VMEM((S, HEAD_DIM), F32)
    return _pcall(
        body, grid=(H,),
        in_specs=[col(0), col(1), col(2), tab, tab, pl.BlockSpec((1, HEAD_DIM), lambda h: (0, h)), out, out,
                  pl.BlockSpec((S, HEAD_DIM), lambda h: (0, dm_col0 + h))],
        out_specs=[out, out, out, pl.BlockSpec((8, HEAD_DIM), lambda h: (0, h))],
        out_shape=[jax.ShapeDtypeStruct((S, W), BF16), jax.ShapeDtypeStruct((S, W), BF16),
                   jax.ShapeDtypeStruct((S, W), BF16), jax.ShapeDtypeStruct((8, W), F32)],
        scratch_shapes=[big, big, big, pltpu.VMEM((S, LANES), F32), big, big, big],
        compiler_params=_params("parallel"), name=name)(proj, proj, proj, cos2, sin_signed, gain, o_raw, lse, dmixed)


GELU_C = math.sqrt(2.0 / math.pi)
GELU_A = 0.044715
HALO = 16


def _shift_down(cur, halo, k):
    out = pltpu.roll(cur, k, axis=0)
    row = lax.broadcasted_iota(jnp.int32, cur.shape, 0)
    for t in range(k):
        out = jnp.where(row == t, halo[HALO - k + t:HALO - k + t + 1, :], out)
    return out


def _shift_up(cur, halo, k):
    n = cur.shape[0]
    out = pltpu.roll(cur, n - k, axis=0)
    row = lax.broadcasted_iota(jnp.int32, cur.shape, 0)
    for t in range(k):
        out = jnp.where(row == n - k + t, halo[t:t + 1, :], out)
    return out


def _conv3(cur, halo, cw):
    return _shift_down(cur, halo, 2) * cw[0:1, :] + _shift_down(cur, halo, 1) * cw[1:2, :] + cur * cw[2:3, :] + cw[3:4, :]


def _gelu_parts(x):
    t = jnp.tanh(GELU_C * (x + GELU_A * x * x * x))
    return 0.5 * x * (1.0 + t), t


def _geglu_specs(tm, tn, ncb):
    hb = tm // HALO

    def cur(off):
        return pl.BlockSpec((tm, tn), lambda j, i: (i, off + j))

    def prev(off):
        return pl.BlockSpec((HALO, tn), lambda j, i: (jnp.maximum(i * hb - 1, 0), off + j))

    def taps(off):
        return pl.BlockSpec((8, tn), lambda j, i: (0, off + j))

    return [cur(0), prev(0), cur(ncb), prev(ncb), taps(0), taps(ncb)]


def _geglu_fwd(u, cwb, name, tm=256, tn=512):
    S, F2 = u.shape
    F = F2 // 2
    tm, tn = _tile(S, tm, HALO), _tile(F, tn)
    ncb = F // tn

    def body(g_ref, gp_ref, v_ref, vp_ref, cg_ref, cv_ref, y_ref):
        top = pl.program_id(1) > 0
        gp = jnp.where(top, gp_ref[...].astype(F32), 0.0)
        vp = jnp.where(top, vp_ref[...].astype(F32), 0.0)
        gc = _conv3(g_ref[...].astype(F32), gp, cg_ref[...])
        vc = _conv3(v_ref[...].astype(F32), vp, cv_ref[...])
        y_ref[...] = (_gelu_parts(gc)[0] * vc).astype(BF16)

    return _pcall(body, grid=(ncb, S // tm), in_specs=_geglu_specs(tm, tn, ncb),
                  out_specs=pl.BlockSpec((tm, tn), lambda j, i: (i, j)),
                  out_shape=jax.ShapeDtypeStruct((S, F), BF16),
                  compiler_params=_params("parallel", "parallel"), name=name)(u, u, u, u, cwb, cwb)


def _geglu_bwd(u, dy, cwb, name, tm=256, tn=512):
    S, F2 = u.shape
    F = F2 // 2
    tm, tn = _tile(S, tm, HALO), _tile(F, tn)
    ncb = F // tn

    def body(g_ref, gp_ref, v_ref, vp_ref, cg_ref, cv_ref, dy_ref, dc_ref, dwg_ref, dwv_ref):
        i = pl.program_id(1)

        @pl.when(i == 0)
        def _():
            dwg_ref[...] = jnp.zeros_like(dwg_ref)
            dwv_ref[...] = jnp.zeros_like(dwv_ref)

        top = i > 0
        g, v = g_ref[...].astype(F32), v_ref[...].astype(F32)
        gp = jnp.where(top, gp_ref[...].astype(F32), 0.0)
        vp = jnp.where(top, vp_ref[...].astype(F32), 0.0)
        gc = _conv3(g, gp, cg_ref[...])
        vc = _conv3(v, vp, cv_ref[...])
        act, t = _gelu_parts(gc)
        dact = 0.5 * (1.0 + t) + 0.5 * gc * (1.0 - t * t) * GELU_C * (1.0 + 3.0 * GELU_A * gc * gc)
        dyv = dy_ref[...].astype(F32)
        dgc = dyv * vc * dact
        dvc = dyv * act
        dc_ref[0] = dgc.astype(BF16)
        dc_ref[1] = dvc.astype(BF16)

        def taps(out_ref, dc, cur, halo):
            out_ref[0:1, :] += jnp.sum(dc * _shift_down(cur, halo, 2), axis=0, keepdims=True)
            out_ref[1:2, :] += jnp.sum(dc * _shift_down(cur, halo, 1), axis=0, keepdims=True)
            out_ref[2:3, :] += jnp.sum(dc * cur, axis=0, keepdims=True)
            out_ref[3:4, :] += jnp.sum(dc, axis=0, keepdims=True)

        taps(dwg_ref, dgc, g, gp)
        taps(dwv_ref, dvc, v, vp)

    return _pcall(body, grid=(ncb, S // tm),
                  in_specs=_geglu_specs(tm, tn, ncb) + [pl.BlockSpec((tm, tn), lambda j, i: (i, j))],
                  out_specs=[pl.BlockSpec((2, tm, tn), lambda j, i: (0, i, j)),
                             pl.BlockSpec((8, tn), lambda j, i: (0, j)), pl.BlockSpec((8, tn), lambda j, i: (0, j))],
                  out_shape=[jax.ShapeDtypeStruct((2, S, F), BF16), jax.ShapeDtypeStruct((8, F), F32),
                             jax.ShapeDtypeStruct((8, F), F32)],
                  compiler_params=_params("parallel", "arbitrary"), name=name)(u, u, u, u, cwb, cwb, dy)


def _conv_bwd(dc, cwb, name, tm=256, tn=512):
    _, S, F = dc.shape
    tm, tn = _tile(S, tm, HALO), _tile(F, tn)
    ncb, nrb = F // tn, S // tm
    hb = tm // HALO

    def body(c_ref, n_ref, w_ref, du_ref):
        cur = c_ref[...].astype(F32)
        nxt = jnp.where(pl.program_id(2) < nrb - 1, n_ref[...].astype(F32), 0.0)
        w = w_ref[...]
        du = cur * w[2:3, :] + _shift_up(cur, nxt, 1) * w[1:2, :] + _shift_up(cur, nxt, 2) * w[0:1, :]
        du_ref[...] = du.astype(BF16)

    return _pcall(body, grid=(2, ncb, nrb),
                  in_specs=[pl.BlockSpec((None, tm, tn), lambda c, j, i: (c, i, j)),
                            pl.BlockSpec((None, HALO, tn), lambda c, j, i: (c, jnp.minimum((i + 1) * hb, S // HALO - 1), j)),
                            pl.BlockSpec((8, tn), lambda c, j, i: (0, c * ncb + j))],
                  out_specs=pl.BlockSpec((tm, tn), lambda c, j, i: (i, c * ncb + j)),
                  out_shape=jax.ShapeDtypeStruct((S, 2 * F), BF16),
                  compiler_params=_params("parallel", "parallel", "parallel"), name=name)(dc, dc, cwb)


def _adam_math(w, g, m, v):
    m = ADAM_B1 * m + (1.0 - ADAM_B1) * g
    v = ADAM_B2 * v + (1.0 - ADAM_B2) * (g * g)
    m_hat = m / (1.0 - ADAM_B1 ** ADAM_STEP)
    v_hat = v / (1.0 - ADAM_B2 ** ADAM_STEP)
    return -ADAM_LR * (m_hat / (jnp.sqrt(v_hat) + ADAM_EPS) + ADAM_WD * w), m, v


def _adamw(w, parts, m, v, name, tr=256):
    R, C = w.shape
    n, _, Cp = parts.shape
    tr = _tile(R, tr, 8)

    def body(w_ref, p_ref, m_ref, v_ref, g_out, d_out, m_out, v_out):
        g = p_ref[0, :, 0:C].astype(F32)
        for k in range(1, n):
            g = g + p_ref[k, :, 0:C].astype(F32)
        d, mn, vn = _adam_math(w_ref[...], g, m_ref[...], v_ref[...])
        g_out[...] = g
        d_out[...] = d
        m_out[...] = mn
        v_out[...] = vn

    spec = pl.BlockSpec((tr, C), lambda i: (i, 0))
    shape = jax.ShapeDtypeStruct((R, C), F32)
    return _pcall(body, grid=(R // tr,), in_specs=[spec, pl.BlockSpec((n, tr, Cp), lambda i: (0, i, 0)), spec, spec],
                  out_specs=[spec] * 4, out_shape=[shape] * 4, compiler_params=_params("parallel"), name=name)(w, parts, m, v)


def _adamw_chips(w, pair, parts, chip_ids, m, v, name, tr=256):
    R, C = w.shape
    Cp = pair.shape[2]
    tr = _tile(R, tr, 16)

    def body(ids_ref, w_ref, own_ref, p1_ref, p2_ref, p3_ref, m_ref, v_ref, g_out, d_out, m_out, v_out):
        g = own_ref[:, 0:C].astype(F32)
        for ref in (p1_ref, p2_ref, p3_ref):
            g = g + ref[:, 0:C].astype(F32)
        d, mn, vn = _adam_math(w_ref[...], g, m_ref[...], v_ref[...])
        g_out[...] = g
        d_out[...] = d
        m_out[...] = mn
        v_out[...] = vn

    spec = pl.BlockSpec((tr, C), lambda i, ids: (i, 0))

    def chip(k):
        return pl.BlockSpec((None, tr, Cp), lambda i, ids: (ids[k], i, 0))

    shape = jax.ShapeDtypeStruct((R, C), F32)
    grid_spec = pltpu.PrefetchScalarGridSpec(
        num_scalar_prefetch=1, grid=(R // tr,), in_specs=[spec, chip(0), chip(1), chip(2), chip(3), spec, spec],
        out_specs=[spec] * 4)
    return _pcall(body, grid_spec=grid_spec, out_shape=[shape] * 4, compiler_params=_params("parallel"),
                  name=name)(chip_ids, w, pair, parts, parts, parts, m, v)


def _place():
    return lax.axis_index("x"), lax.axis_index("y"), lax.axis_index("c")


def _other_chips(x, y):
    return [(1 - x, y), (x, 1 - y), (1 - x, 1 - y)]


IN_HBM = pl.BlockSpec(memory_space=pltpu.HBM)
SEM = pl.BlockSpec(memory_space=pltpu.SEMAPHORE)
EFFECT = pltpu.SideEffectType.DATAFLOW_SIDE_EFFECTING
TOKEN = jax.ShapeDtypeStruct((8, LANES), F32)
TOKEN_SPEC = pl.BlockSpec(memory_space=pltpu.VMEM)


def _in_hbm(a):
    return pltpu.with_memory_space_constraint(a, pltpu.HBM)


def _landing(shape):
    return _in_hbm(lax.empty(shape.shape, shape.dtype))


def _hbm_like(a):
    return pltpu.HBM(a.shape, a.dtype)


def _gather_start(landing, slots, after, name):
    na = len(landing)

    def body(*refs):
        land = refs[:na]
        send_sems, recv_sems = refs[na + 1], refs[na + 2]
        token = refs[-1]
        x, y, c = _place()
        for a in range(na):
            own = slots[a](land[a], x, y, c)
            for k, to in enumerate([(x, y, 1 - c)] + [(*chip, c) for chip in _other_chips(x, y)]):
                pltpu.make_async_remote_copy(
                    src_ref=own, dst_ref=own, send_sem=send_sems.at[4 * a + k],
                    recv_sem=recv_sems.at[4 * a + k], device_id=to, device_id_type=MESH).start()
        token[...] = jnp.zeros_like(token)

    sems = pltpu.SemaphoreType.DMA((4 * na,))
    outs = _pcall(
        body, in_specs=[IN_HBM] * na + [HBM],
        out_specs=[SEM, SEM] + [IN_HBM] * na + [TOKEN_SPEC],
        out_shape=[sems, sems] + [_hbm_like(s) for s in landing] + [TOKEN],
        input_output_aliases={a: 2 + a for a in range(na)},
        compiler_params=pltpu.CompilerParams(has_side_effects=EFFECT), name=name,
    )(*[_in_hbm(s) for s in landing], after)
    return outs[0], outs[1], outs[2:2 + na], outs[-1]


def _gather_forward(gathered, send_sems, recv_sems, slots, after, name):
    na = len(gathered)

    def body(*refs):
        gath = refs[:na]
        send1, recv1 = refs[na], refs[na + 1]
        fsend, frecv = refs[na + 3], refs[na + 4]
        token = refs[-1]
        x, y, c = _place()
        chips = _other_chips(x, y)
        for a in range(na):
            for k, peer in enumerate([(x, y, 1 - c)] + [(*chip, c) for chip in chips]):
                arrival = pltpu.make_async_remote_copy(
                    src_ref=slots[a](gath[a], x, y, c), dst_ref=slots[a](gath[a], *peer), send_sem=send1.at[4 * a + k],
                    recv_sem=recv1.at[4 * a + k], device_id=peer, device_id_type=MESH)
                arrival.wait_send()
                arrival.wait_recv()
        for a in range(na):
            for j, chip in enumerate(chips):
                view = slots[a](gath[a], *chip, c)
                pltpu.make_async_remote_copy(
                    src_ref=view, dst_ref=view, send_sem=fsend.at[3 * a + j], recv_sem=frecv.at[3 * a + j],
                    device_id=(x, y, 1 - c), device_id_type=MESH).start()
        token[...] = jnp.zeros_like(token)

    sems = pltpu.SemaphoreType.DMA((3 * na,))
    outs = _pcall(
        body, in_specs=[IN_HBM] * na + [SEM, SEM, HBM],
        out_specs=[SEM, SEM] + [IN_HBM] * na + [TOKEN_SPEC],
        out_shape=[sems, sems] + [_hbm_like(g) for g in gathered] + [TOKEN],
        input_output_aliases={a: 2 + a for a in range(na)},
        compiler_params=pltpu.CompilerParams(has_side_effects=EFFECT), name=name,
    )(*gathered, send_sems, recv_sems, after)
    return outs[0], outs[1], outs[2:2 + na], outs[-1]


def _gather_finish(gathered, fsend, frecv, slots, after, name):
    na = len(gathered)

    def body(*refs):
        gath, fs, fr = refs[:na], refs[na], refs[na + 1]
        x, y, c = _place()
        for a in range(na):
            for j, chip in enumerate(_other_chips(x, y)):
                passed = pltpu.make_async_remote_copy(
                    src_ref=slots[a](gath[a], *chip, c), dst_ref=slots[a](gath[a], *chip, 1 - c),
                    send_sem=fs.at[3 * a + j], recv_sem=fr.at[3 * a + j], device_id=(x, y, 1 - c), device_id_type=MESH)
                passed.wait_send()
                passed.wait_recv()

    outs = _pcall(
        body, in_specs=[IN_HBM] * na + [SEM, SEM, HBM], out_specs=[IN_HBM] * na,
        out_shape=[_hbm_like(g) for g in gathered], input_output_aliases={a: a for a in range(na)},
        compiler_params=pltpu.CompilerParams(has_side_effects=EFFECT), name=name,
    )(*gathered, fsend, frecv, after)
    return list(outs)


def _pair_exchange(grads, views, recv_shapes, name):
    na = len(grads)

    def body(*refs):
        srcs, dsts = refs[:na], refs[na:2 * na]
        send_sems, recv_sems = refs[2 * na:]
        x, y, c = _place()
        copies = []
        for a in range(na):
            for chip in range(N_CHIP):
                copies.append(pltpu.make_async_remote_copy(
                    src_ref=views[a](srcs[a], chip, 1 - c), dst_ref=dsts[a].at[chip],
                    send_sem=send_sems.at[a * N_CHIP + chip], recv_sem=recv_sems.at[a * N_CHIP + chip],
                    device_id=(x, y, 1 - c), device_id_type=MESH))
        for cp in copies:
            cp.start()
        for cp in copies:
            cp.wait()

    return _pcall(body, in_specs=[HBM] * na, out_specs=[HBM] * na, out_shape=recv_shapes,
                  scratch_shapes=[pltpu.SemaphoreType.DMA((N_CHIP * na,)), pltpu.SemaphoreType.DMA((N_CHIP * na,))],
                  name=name)(*grads)


def _chip_start(pair, after, name):
    def body(src, land, after_ref, send_sems, recv_sems, src_thru, land_thru, token):
        x, y, c = _place()
        for j, (px, py) in enumerate(_other_chips(x, y)):
            pltpu.make_async_remote_copy(
                src_ref=src.at[2 * px + py], dst_ref=land.at[2 * x + y], send_sem=send_sems.at[j], recv_sem=recv_sems.at[j],
                device_id=(px, py, c), device_id_type=MESH).start()
        token[...] = jnp.zeros_like(token)

    sems = pltpu.SemaphoreType.DMA((3,))
    return _pcall(
        body, in_specs=[IN_HBM, IN_HBM, HBM], out_specs=[SEM, SEM, IN_HBM, IN_HBM, TOKEN_SPEC],
        out_shape=[sems, sems, _hbm_like(pair), _hbm_like(pair), TOKEN], input_output_aliases={0: 2, 1: 3},
        compiler_params=pltpu.CompilerParams(has_side_effects=EFFECT), name=name,
    )(_in_hbm(pair), _landing(pair), after)


def _chip_wait(pair, parts, send_sems, recv_sems, after, name):
    def body(src, land, send, recv, after_ref, src_thru, land_thru):
        x, y, c = _place()
        for j, (px, py) in enumerate(_other_chips(x, y)):
            copy = pltpu.make_async_remote_copy(
                src_ref=src.at[2 * px + py], dst_ref=land.at[2 * px + py], send_sem=send.at[j], recv_sem=recv.at[j],
                device_id=(px, py, c), device_id_type=MESH)
            copy.wait_send()
            copy.wait_recv()

    return _pcall(
        body, in_specs=[IN_HBM, IN_HBM, SEM, SEM, HBM], out_specs=[IN_HBM, IN_HBM],
        out_shape=[_hbm_like(pair), _hbm_like(parts)], input_output_aliases={0: 0, 1: 1},
        compiler_params=pltpu.CompilerParams(has_side_effects=EFFECT), name=name,
    )(pair, parts, send_sems, recv_sems, after)


def _pair_add(core, grad, recv, block, grad_spec, name):
    _, R, C = recv.shape
    tr = block

    def body(c_ref, g_ref, r_ref, o_ref):
        o_ref[...] = (g_ref[...].astype(F32) + r_ref[...].astype(F32)).astype(BF16)

    grid_spec = pltpu.PrefetchScalarGridSpec(
        num_scalar_prefetch=1, grid=(N_CHIP, R // tr),
        in_specs=[grad_spec, pl.BlockSpec((None, tr, C), lambda k, i, c: (k, i, 0))],
        out_specs=pl.BlockSpec((None, tr, C), lambda k, i, c: (k, i, 0)))
    return _pcall(body, grid_spec=grid_spec, out_shape=jax.ShapeDtypeStruct(recv.shape, BF16),
                  compiler_params=_params("parallel", "parallel"), name=name)(core, grad, recv)


def _all_reduce_small(part, name):
    R = part.shape[0]

    def body(p_ref, o_ref, all_ref, send_sems, recv_sems):
        x, y, c = _place()
        me = 4 * x + 2 * y + c
        all_ref[me] = p_ref[...]
        peers = [(x, y, 1 - c)] + [(px, py, pc) for px, py in _other_chips(x, y) for pc in (c, 1 - c)]
        copies = [pltpu.make_async_remote_copy(
            src_ref=p_ref, dst_ref=all_ref.at[me], send_sem=send_sems.at[k], recv_sem=recv_sems.at[k],
            device_id=peer, device_id_type=MESH) for k, peer in enumerate(peers)]
        for cp in copies:
            cp.start()
        for k, (px, py, pc) in enumerate(peers):
            pltpu.make_async_remote_copy(
                src_ref=p_ref, dst_ref=all_ref.at[4 * px + 2 * py + pc], send_sem=send_sems.at[k], recv_sem=recv_sems.at[k],
                device_id=peers[k], device_id_type=MESH).wait_recv()
        for cp in copies:
            cp.wait_send()
        acc = all_ref[0]
        for k in range(1, N_DEV):
            acc = acc + all_ref[k]
        o_ref[...] = acc

    vm = pl.BlockSpec(memory_space=pltpu.VMEM)
    return _pcall(body, in_specs=[vm], out_specs=vm, out_shape=jax.ShapeDtypeStruct((R, LANES), F32),
                  scratch_shapes=[pltpu.VMEM((N_DEV, R, LANES), F32), pltpu.SemaphoreType.DMA((7,)), pltpu.SemaphoreType.DMA((7,))],
                  name=name)(part)


def _local_step(x, tgt, gains, weights):
    g_pre_mix, g_post_mix, g_pre_ffn, g_post_ffn, g_sb, g_dil = gains
    S, D = x.shape
    hs = g_sb.shape[1] // HEAD_DIM
    hd = g_dil.shape[1] // HEAD_DIM
    cos2, sin_signed = _rope_tables(S)

    h1 = _rms_fwd(x, g_pre_mix + weights.start(), "rms_in")
    w_in_g = weights.w_in(h1)
    proj = _mm_nn(h1, w_in_g, F32, "proj", tn=768)
    o_sb, ct_sb, mx_sb = _sb_fwd(proj, g_sb, hs, "sb_fwd")
    o_dl, lse_dl, mx_dl = _dil_fwd(proj, cos2, sin_signed, g_dil + weights.forward_out(o_sb), 3 * hs, hd, "dil_fwd")
    w_out_g, dep = weights.w_out(o_dl)
    mixed = jnp.concatenate([mx_sb, mx_dl], axis=1)
    mix = _mm_nn(mixed, w_out_g, F32, "mix_out", tn=1024)
    x2, h2 = _mid_fwd(x, mix, g_post_mix + dep, g_pre_ffn, "mid_fwd")
    w_up_g, w_down_g, cwb = weights.ffn(h2)
    u = _mm_nn(h2, w_up_g, BF16, "ffn_up")
    y = _geglu_fwd(u, cwb, "geglu_fwd")
    f = _mm_nn(y, w_down_g, F32, "ffn_down", tn=1024, tk=1408)

    dy, df, dg_post_ffn, loss = _loss_bwd(x2, f, tgt, g_post_ffn, "loss_bwd")
    dyv = _mm_nt(df, w_down_g, BF16, "d_y", tn=1408)
    dw_down = _mm_tn(y, df, D, BF16, "dw_down", tm=1408, tn=1024)
    dc, dcw_g, dcw_v = _geglu_bwd(u, dyv, cwb + weights.grad("w_down", dw_down), "geglu_bwd")
    du = _conv_bwd(dc, cwb, "conv_bwd")
    dh2 = _mm_nt(du, w_up_g, F32, "d_h2", tk=1408)
    dw_up = _mm_tn(h2, du, w_up_g.shape[2], BF16, "dw_up")
    dx2, dmix, dg_pre_ffn, dg_post_mix = _mid_bwd(
        dy, dh2, x2, mix, g_pre_ffn + weights.grad("w_up", dw_up), g_post_mix, "mid_bwd")
    dmixed = _mm_nt(dmix, w_out_g, F32, "d_mixed")
    dw_out = _mm_tn(mixed, dmix, D, BF16, "dw_out", tn=1024)
    dq_s, dk_s, dv_s, dg_sb = _sb_bwd(proj, g_sb + weights.grad("w_out", dw_out), o_sb, ct_sb, dmixed, 0, hs, "sb_bwd")
    dq_d, dk_d, dv_d, dg_dil = _dil_bwd(proj, cos2, sin_signed, g_dil, o_dl, lse_dl, dmixed, hs, 3 * hs, hd, "dil_bwd")
    dproj = jnp.concatenate([dq_s, dk_s, dv_s, dq_d, dk_d, dv_d], axis=1)
    dh1 = _mm_nt(dproj, w_in_g, F32, "d_h1", tk=768)
    grad_x, dg_pre_mix = _first_bwd(dx2, dh1, x, g_pre_mix, "first_bwd")
    small = (dg_pre_mix, dg_post_mix, dg_pre_ffn, dg_post_ffn, dg_sb[0:1], dg_dil[0:1], jnp.concatenate([dcw_g, dcw_v], axis=1))
    dw_in = _mm_tn(h1, dproj, w_in_g.shape[2], BF16, "dw_in", tn=768, after=weights.small(small, loss))
    weights.grad("w_in", dw_in)
    return loss, grad_x, small


def _pad_cols(a, to):
    return jnp.pad(a, ((0, 0), (0, to - a.shape[1])))


def kernel(x, pre_mix_gain, post_mix_gain, pre_ffn_gain, post_ffn_gain, w_in, sb_out_gain, dil_out_gain, w_out, w_up, conv_w, conv_b, w_down, loss_target, m_pre_mix_gain, m_post_mix_gain, m_pre_ffn_gain, m_post_ffn_gain, m_w_in, m_sb_out_gain, m_dil_out_gain, m_w_out, m_w_up, m_conv_w, m_conv_b, m_w_down, v_pre_mix_gain, v_post_mix_gain, v_pre_ffn_gain, v_post_ffn_gain, v_w_in, v_sb_out_gain, v_dil_out_gain, v_w_out, v_w_up, v_conv_w, v_conv_b, v_w_down):
    xb, tb = x[0], loss_target[0]
    S, D = xb.shape
    w_in, w_out, w_up, w_down, conv_w = w_in[0], w_out[0], w_up[0], w_down[0], conv_w[0]
    n_in, e_rows = w_in.shape[1], w_out.shape[0]
    cu, half = w_up.shape[1], w_down.shape[0]
    assert cu == 2 * half and half % 16 == 0
    cup = -(-cu // LANES) * LANES
    fp = N_CHIP * cup
    px, py, pc = _place()
    me = 4 * px + 2 * py + pc
    core = jnp.reshape(pc, (1,)).astype(jnp.int32)

    shards = [w_in.astype(BF16), w_out.astype(BF16), _pad_cols(w_up, cup).astype(BF16), w_down.astype(BF16),
              jnp.pad(_pad_cols(conv_w, cup), ((0, 8 - conv_w.shape[0]), (0, 0)))]

    def by_dev(ref, qx, qy, qc):
        return ref.at[4 * qx + 2 * qy + qc]

    def down_slot(ref, qx, qy, qc):
        return ref.at[2 * qx + qy, pl.ds(qc * half, half)]

    def by_pair(ref, chip, k):
        return ref.at[chip, k]

    def down_pair(ref, chip, k):
        return ref.at[chip, pl.ds(k * half, half)]

    def pair_spec(tr, cols):
        return pl.BlockSpec((None, None, tr, cols), lambda k, i, c: (k, c[0], i, 0))

    tr_in, tr_up = _tile(D, 512, 16), _tile(D, 256, 16)
    grad_plan = {
        "w_in": ((N_CHIP, 2, D, n_in), by_pair, (D, n_in), tr_in, pair_spec(tr_in, n_in)),
        "w_out": ((N_CHIP, 2, e_rows, D), by_pair, (e_rows, D), e_rows, pair_spec(e_rows, D)),
        "w_up": ((N_CHIP, 2, D, cup), by_pair, (D, cup), tr_up, pair_spec(tr_up, cup)),
        "w_down": ((N_CHIP, cup, D), down_pair, (half, D), half,
                   pl.BlockSpec((None, half, D), lambda k, i, c: (k, c[0], 0))),
    }

    class Exchanges:
        def __init__(self):
            self.in_flight = {}

        def start(self):
            def own_slot(shard):
                return lax.dynamic_update_index_in_dim(lax.empty((N_DEV, *shard.shape), shard.dtype), shard, me, 0)

            down = lax.dynamic_update_slice(jnp.zeros((N_CHIP, cup, D), BF16), shards[3][None], (2 * px + py, pc * half, 0))
            landing = [own_slot(shards[0]), own_slot(shards[1]), own_slot(shards[2]), down, own_slot(shards[4])]
            self.ffn_slots = [by_dev, down_slot, by_dev]
            self.g_in = _gather_start(landing[:1], [by_dev], core, "gather_in_start")
            self.g_out = _gather_start(landing[1:2], [by_dev], self.g_in[3], "gather_out_start")
            self.g_ffn = _gather_start(landing[2:], self.ffn_slots, self.g_out[3], "gather_ffn_start")
            return self.g_ffn[3][0, 0]

        def w_in(self, after):
            send, recv, gath, _ = self.g_in
            fsend, frecv, gath, token = _gather_forward(gath, send, recv, [by_dev], after, "gather_in_forward")
            return _gather_finish(gath, fsend, frecv, [by_dev], token, "gather_in_finish")[0]

        def forward_out(self, after):
            send, recv, gath, _ = self.g_out
            self.p_out = _gather_forward(gath, send, recv, [by_dev], after, "gather_out_forward")
            return self.p_out[3][0, 0]

        def w_out(self, after):
            fsend, frecv, gath, _ = self.p_out
            w_out_g = _gather_finish(gath, fsend, frecv, [by_dev], after, "gather_out_finish")[0]
            send, recv, gath, _ = self.g_ffn
            self.p_ffn = _gather_forward(gath, send, recv, self.ffn_slots, w_out_g, "gather_ffn_forward")
            return w_out_g.reshape(1, N_DEV * e_rows, D), self.p_ffn[3][0, 0]

        def ffn(self, after):
            fsend, frecv, gath, _ = self.p_ffn
            w_up_g, w_down_g, cw_g = _gather_finish(gath, fsend, frecv, self.ffn_slots, after, "gather_ffn_finish")
            cb = _pad_cols(conv_b.reshape(N_DEV, cu), cup).reshape(1, 2 * fp)
            cw_full = jnp.transpose(cw_g[:, :3, :], (1, 0, 2)).reshape(3, 2 * fp)
            cwb = jnp.concatenate([cw_full, cb, jnp.zeros((4, 2 * fp), F32)], axis=0)
            return w_up_g, w_down_g.reshape(1, fp, D), cwb

        def small(self, small, loss):
            flat = jnp.concatenate([s.reshape(-1) for s in small] + [loss.reshape(-1)])
            rows = -(-flat.size // (8 * LANES)) * 8
            packed = jnp.pad(flat, (0, rows * LANES - flat.size)).reshape(rows, LANES)
            self.total = _all_reduce_small(packed, "reduce_small")
            return self.total

        def grad(self, name, dw):
            view_shape, view, block, tr, spec = grad_plan[name]
            dw = dw.reshape(view_shape)
            recv = _pair_exchange([dw], [view], [jax.ShapeDtypeStruct((N_CHIP, *block), BF16)], "pair_exchange_" + name)[0]
            pair = _pair_add(core, dw, recv, tr, spec, "pair_add_" + name)
            send, recv_sems, pair, parts, token = _chip_start(pair, recv, "chip_start_" + name)
            self.in_flight[name] = (pair, parts, send, recv_sems)
            self.last_token = token
            return token[0, 0]

        def grad_parts(self, name, after):
            return _chip_wait(*self.in_flight[name], after, "chip_wait_" + name)

    exchanges = Exchanges()
    gains = (pre_mix_gain, post_mix_gain, pre_ffn_gain, post_ffn_gain, sb_out_gain, dil_out_gain)
    loss, grad_x, small = _local_step(xb, tb, gains, exchanges)

    sizes = [s.size for s in small]
    total = exchanges.total.reshape(-1)
    offs = [0]
    for s in sizes:
        offs.append(offs[-1] + s)
    red = [total[offs[k]:offs[k + 1]].reshape(small[k].shape) for k in range(len(small))]
    loss_out = total[offs[-1]]
    g_pre_mix, g_post_mix, g_pre_ffn, g_post_ffn, g_sb, g_dil, g_conv = red
    g_conv_b = g_conv[3].reshape(N_DEV, cup)[:, :cu].reshape(1, N_DEV * cu)
    g_conv_w = lax.dynamic_index_in_dim(g_conv[0:3].reshape(3, N_DEV, cup), me, axis=1, keepdims=False)[:, :cu]

    def small_adam(w, g, m, v, name):
        one = w.shape[0] == 1
        if one:
            w, g, m, v = (jnp.broadcast_to(t, (8, t.shape[1])) for t in (w, g, m, v))
        outs = _adamw(w, g[None], m, v, name)
        return [o[0:1] for o in outs] if one else outs

    chip_ids = jnp.stack([2 * px + py, 2 * (1 - px) + py, 2 * px + 1 - py, 2 * (1 - px) + 1 - py]).astype(jnp.int32)
    out_w_down = _adamw_chips(w_down, *exchanges.grad_parts("w_down", exchanges.last_token), chip_ids, m_w_down[0], v_w_down[0], "adam_w_down")
    out_w_up = _adamw_chips(w_up, *exchanges.grad_parts("w_up", out_w_down[1]), chip_ids, m_w_up[0], v_w_up[0], "adam_w_up")
    out_w_out = _adamw_chips(w_out, *exchanges.grad_parts("w_out", out_w_up[1]), chip_ids, m_w_out[0], v_w_out[0], "adam_w_out")
    out_w_in = _adamw_chips(w_in, *exchanges.grad_parts("w_in", out_w_out[1]), chip_ids, m_w_in[0], v_w_in[0], "adam_w_in")
    out_pre_mix = small_adam(pre_mix_gain, g_pre_mix, m_pre_mix_gain, v_pre_mix_gain, "adam_pre_mix")
    out_post_mix = small_adam(post_mix_gain, g_post_mix, m_post_mix_gain, v_post_mix_gain, "adam_post_mix")
    out_pre_ffn = small_adam(pre_ffn_gain, g_pre_ffn, m_pre_ffn_gain, v_pre_ffn_gain, "adam_pre_ffn")
    out_post_ffn = small_adam(post_ffn_gain, g_post_ffn, m_post_ffn_gain, v_post_ffn_gain, "adam_post_ffn")
    out_sb = small_adam(sb_out_gain, g_sb, m_sb_out_gain, v_sb_out_gain, "adam_sb_gain")
    out_dil = small_adam(dil_out_gain, g_dil, m_dil_out_gain, v_dil_out_gain, "adam_dil_gain")
    out_conv_b = small_adam(conv_b, g_conv_b, m_conv_b, v_conv_b, "adam_conv_b")
    cw8 = [jnp.pad(t, ((0, 5), (0, 0))) for t in (conv_w, g_conv_w, m_conv_w[0], v_conv_w[0])]
    out_conv_w = [o[0:3] for o in _adamw(cw8[0], cw8[1][None], cw8[2], cw8[3], "adam_conv_w")]

    order = [out_pre_mix, out_post_mix, out_pre_ffn, out_post_ffn, [o[None] for o in out_w_in], out_sb, out_dil,
             [o[None] for o in out_w_out], [o[None] for o in out_w_up], [o[None] for o in out_conv_w], out_conv_b,
             [o[None] for o in out_w_down]]
    outs = [loss_out, grad_x[None]]
    for k in range(4):
        outs += [o[k] for o in order]
    return tuple(outs)
```

```python
import functools
import math

import jax
import jax.numpy as jnp
from jax import lax
from jax.experimental import pallas as pl
from jax.experimental.pallas import tpu as pltpu

F32 = jnp.float32
BF16 = jnp.bfloat16
HEAD_DIM = 128
LANES = 128
KEY_BLOCK = 128
DILATIONS = (1, 4, 16)
RMS_EPS = 1e-6
ROPE_THETA = 10000.0
NEG = -1e30
ADAM_LR, ADAM_B1, ADAM_B2, ADAM_EPS, ADAM_WD, ADAM_STEP = 0.001, 0.9, 0.999, 1e-08, 0.01, 10
MESH = pl.DeviceIdType.MESH
N_DEV = 8
N_CHIP = 4
HBM = pl.BlockSpec(memory_space=pl.ANY)
VMEM_LIMIT = 56 * 1024 * 1024

_pcall = pl.pallas_call


def _tile(n, pref, mult=LANES):
    best = None
    t = mult
    while t <= min(n, pref):
        if n % t == 0:
            best = t
        t += mult
    return n if best is None else best


def _params(*sem):
    return pltpu.CompilerParams(dimension_semantics=sem, vmem_limit_bytes=VMEM_LIMIT)


def _dot(a, b, dims):
    return lax.dot_general(a, b, (dims, ((), ())), preferred_element_type=F32)


NN = ((1,), (0,))
NT = ((1,), (1,))
TN = ((0,), (0,))


def _mm_body(dims, nk, tile):
    if nk == 1:
        def single(a_ref, b_ref, o_ref):
            o_ref[...] = _dot(a_ref[...].astype(BF16), b_ref[...].astype(BF16), dims).astype(o_ref.dtype)

        return single, []

    def body(a_ref, b_ref, o_ref, acc_ref):
        k = pl.program_id(2)

        @pl.when(k == 0)
        def _():
            acc_ref[...] = jnp.zeros_like(acc_ref)

        acc_ref[...] += _dot(a_ref[...].astype(BF16), b_ref[...].astype(BF16), dims)

        @pl.when(k == nk - 1)
        def _():
            o_ref[...] = acc_ref[...].astype(o_ref.dtype)

    return body, [pltpu.VMEM(tile, F32)]


def _mm_nn(a, b3, out_dtype, name, tm=1024, tn=1408, tk=2048):
    M, K = a.shape
    C, _, n = b3.shape
    tm, tk, tn = _tile(M, tm, 8), _tile(K, tk), _tile(n, tn)
    npc, nk = n // tn, K // tk
    body, scratch = _mm_body(NN, nk, (tm, tn))
    return _pcall(
        body, grid=(M // tm, C * npc, nk),
        in_specs=[pl.BlockSpec((tm, tk), lambda i, j, k: (i, k)),
                  pl.BlockSpec((None, tk, tn), lambda i, j, k: (j // npc, k, j % npc))],
        out_specs=pl.BlockSpec((tm, tn), lambda i, j, k: (i, j)),
        out_shape=jax.ShapeDtypeStruct((M, C * n), out_dtype), scratch_shapes=scratch,
        compiler_params=_params("parallel", "parallel", "arbitrary"), name=name)(a, b3)


def _mm_nt(a, b3, out_dtype, name, tm=1024, tn=1024, tk=2048, after=None):
    M, _ = a.shape
    C, N, n = b3.shape
    tm, tn, tk = _tile(M, tm, 8), _tile(N, tn), _tile(n, tk)
    kpc = n // tk
    nk = C * kpc
    inner, scratch = _mm_body(NT, nk, (tm, tn))
    extra = [] if after is None else [after]

    def body(a_ref, b_ref, *rest):
        inner(a_ref, b_ref, *rest[len(extra):])

    return _pcall(
        body, grid=(M // tm, N // tn, nk),
        in_specs=[pl.BlockSpec((tm, tk), lambda i, j, k: (i, k)),
                  pl.BlockSpec((None, tn, tk), lambda i, j, k: (k // kpc, j, k % kpc))] + [HBM] * len(extra),
        out_specs=pl.BlockSpec((tm, tn), lambda i, j, k: (i, j)),
        out_shape=jax.ShapeDtypeStruct((M, N), out_dtype), scratch_shapes=scratch,
        compiler_params=_params("parallel", "parallel", "arbitrary"), name=name)(a, b3, *extra)


def _mm_tn(x, y, n, out_dtype, name, tm=1024, tn=1408, tk=2048, after=None):
    S, P = x.shape
    C = y.shape[1] // n
    tm, tn, tk = _tile(P, tm), _tile(n, tn), _tile(S, tk, 8)
    npc, nk = n // tn, S // tk
    inner, scratch = _mm_body(TN, nk, (tm, tn))
    extra = [] if after is None else [after]

    def body(x_ref, y_ref, *rest):
        inner(x_ref, y_ref, *rest[len(extra):])

    return _pcall(
        body, grid=(P // tm, C * npc, nk),
        in_specs=[pl.BlockSpec((tk, tm), lambda i, j, k: (k, i)),
                  pl.BlockSpec((tk, tn), lambda i, j, k: (k, j))] + [HBM] * len(extra),
        out_specs=pl.BlockSpec((None, tm, tn), lambda i, j, k: (j // npc, i, j % npc)),
        out_shape=jax.ShapeDtypeStruct((C, P, n), out_dtype), scratch_shapes=scratch,
        compiler_params=_params("parallel", "parallel", "arbitrary"), name=name)(x, y, *extra)


def _rms_scale(v):
    return lax.rsqrt(jnp.mean(v * v, axis=-1, keepdims=True) + RMS_EPS)


def _rms_bwd(gy, v, r):
    return r * gy - v * (r * r * r * jnp.mean(gy * v, axis=-1, keepdims=True))


def _rows_spec(tm, d):
    return pl.BlockSpec((tm, d), lambda i: (i, 0))


def _vec_spec(d):
    return pl.BlockSpec((1, d), lambda i: (0, 0))


def _rms_fwd(x, g, name, tm=256):
    S, D = x.shape

    def body(x_ref, g_ref, h_ref):
        v = x_ref[...]
        h_ref[...] = (v * _rms_scale(v) * g_ref[...]).astype(BF16)

    return _pcall(body, grid=(S // tm,), in_specs=[_rows_spec(tm, D), _vec_spec(D)], out_specs=_rows_spec(tm, D),
                  out_shape=jax.ShapeDtypeStruct((S, D), BF16), compiler_params=_params("parallel"), name=name)(x, g)


def _mid_fwd(x, mix, g_post, g_pre, name, tm=256):
    S, D = x.shape

    def body(x_ref, m_ref, gp_ref, gn_ref, x2_ref, h_ref):
        m = m_ref[...]
        x2 = x_ref[...] + m * _rms_scale(m) * gp_ref[...]
        x2_ref[...] = x2
        h_ref[...] = (x2 * _rms_scale(x2) * gn_ref[...]).astype(BF16)

    return _pcall(body, grid=(S // tm,), in_specs=[_rows_spec(tm, D), _rows_spec(tm, D), _vec_spec(D), _vec_spec(D)],
                  out_specs=[_rows_spec(tm, D), _rows_spec(tm, D)],
                  out_shape=[jax.ShapeDtypeStruct((S, D), F32), jax.ShapeDtypeStruct((S, D), BF16)],
                  compiler_params=_params("parallel"), name=name)(x, mix, g_post, g_pre)


def _loss_bwd(x2, f, tgt, g_post, name, tm=256):
    S, D = x2.shape

    def body(x2_ref, f_ref, t_ref, g_ref, dy_ref, df_ref, dg_ref, ls_ref):
        i = pl.program_id(0)

        @pl.when(i == 0)
        def _():
            dg_ref[...] = jnp.zeros_like(dg_ref)
            ls_ref[...] = jnp.zeros_like(ls_ref)

        fv = f_ref[...]
        r = _rms_scale(fv)
        g = g_ref[...]
        err = x2_ref[...] + fv * r * g - t_ref[...]
        ls_ref[...] += jnp.broadcast_to(0.5 * jnp.sum(jnp.mean(err * err, axis=-1, keepdims=True), axis=0, keepdims=True), ls_ref.shape)
        dy = err * (1.0 / D)
        dy_ref[...] = dy
        df_ref[...] = _rms_bwd(dy * g, fv, r).astype(BF16)
        dg_ref[...] += jnp.sum(dy * fv * r, axis=0, keepdims=True)

    return _pcall(body, grid=(S // tm,),
                  in_specs=[_rows_spec(tm, D), _rows_spec(tm, D), _rows_spec(tm, D), _vec_spec(D)],
                  out_specs=[_rows_spec(tm, D), _rows_spec(tm, D), _vec_spec(D), _vec_spec(LANES)],
                  out_shape=[jax.ShapeDtypeStruct((S, D), F32), jax.ShapeDtypeStruct((S, D), BF16),
                             jax.ShapeDtypeStruct((1, D), F32), jax.ShapeDtypeStruct((1, LANES), F32)],
                  compiler_params=_params("arbitrary"), name=name)(x2, f, tgt, g_post)


def _mid_bwd(dy, dh2, x2, mix, g_pre, g_post, name, tm=256):
    S, D = dy.shape

    def body(dy_ref, dh_ref, x2_ref, m_ref, gn_ref, gp_ref, dx2_ref, dm_ref, dgn_ref, dgp_ref):
        i = pl.program_id(0)

        @pl.when(i == 0)
        def _():
            dgn_ref[...] = jnp.zeros_like(dgn_ref)
            dgp_ref[...] = jnp.zeros_like(dgp_ref)

        x2, dh = x2_ref[...], dh_ref[...]
        r = _rms_scale(x2)
        dx2 = dy_ref[...] + _rms_bwd(dh * gn_ref[...], x2, r)
        dgn_ref[...] += jnp.sum(dh * x2 * r, axis=0, keepdims=True)
        dx2_ref[...] = dx2
        m = m_ref[...]
        rm = _rms_scale(m)
        dm_ref[...] = _rms_bwd(dx2 * gp_ref[...], m, rm).astype(BF16)
        dgp_ref[...] += jnp.sum(dx2 * m * rm, axis=0, keepdims=True)

    return _pcall(body, grid=(S // tm,),
                  in_specs=[_rows_spec(tm, D)] * 4 + [_vec_spec(D)] * 2,
                  out_specs=[_rows_spec(tm, D), _rows_spec(tm, D), _vec_spec(D), _vec_spec(D)],
                  out_shape=[jax.ShapeDtypeStruct((S, D), F32), jax.ShapeDtypeStruct((S, D), BF16),
                             jax.ShapeDtypeStruct((1, D), F32), jax.ShapeDtypeStruct((1, D), F32)],
                  compiler_params=_params("arbitrary"), name=name)(dy, dh2, x2, mix, g_pre, g_post)


def _first_bwd(dx2, dh1, x, g_pre, name, tm=256):
    S, D = x.shape

    def body(dx2_ref, dh_ref, x_ref, g_ref, gx_ref, dg_ref):
        i = pl.program_id(0)

        @pl.when(i == 0)
        def _():
            dg_ref[...] = jnp.zeros_like(dg_ref)

        xv, dh = x_ref[...], dh_ref[...]
        r = _rms_scale(xv)
        gx_ref[...] = dx2_ref[...] + _rms_bwd(dh * g_ref[...], xv, r)
        dg_ref[...] += jnp.sum(dh * xv * r, axis=0, keepdims=True)

    return _pcall(body, grid=(S // tm,), in_specs=[_rows_spec(tm, D)] * 3 + [_vec_spec(D)],
                  out_specs=[_rows_spec(tm, D), _vec_spec(D)],
                  out_shape=[jax.ShapeDtypeStruct((S, D), F32), jax.ShapeDtypeStruct((1, D), F32)],
                  compiler_params=_params("arbitrary"), name=name)(dx2, dh1, x, g_pre)


def _logsig_pair(z):
    lb = jnp.minimum(z, 0.0) - jnp.log(1.0 + jnp.exp(-jnp.abs(z)))
    return lb, lb - z


SB_KEY_BLOCK = 256


def _sum_matrix(strict):
    ia = lax.broadcasted_iota(jnp.int32, (SB_KEY_BLOCK, SB_KEY_BLOCK), 0)
    ib = lax.broadcasted_iota(jnp.int32, (SB_KEY_BLOCK, SB_KEY_BLOCK), 1)
    tri = (ia > ib) if strict == ">" else (ia < ib)
    return jnp.concatenate([tri.astype(BF16), jnp.ones((SB_KEY_BLOCK, LANES), BF16)], axis=1)


def _lanes(c, width):
    return jnp.tile(c, (1, width // LANES))


def _split_dot(v, u):
    hi = v.astype(BF16)
    lo = (v - hi.astype(F32)).astype(BF16)
    return _dot(hi, u, NN) + _dot(lo, u, NN)


def _head_out(o, g):
    return o * _rms_scale(o) * g


def _sb_fwd(proj, gain, n_heads, name, tq=512):
    S = proj.shape[0]
    H, tk = n_heads, SB_KEY_BLOCK
    tq = _tile(S, tq, 2 * tk)
    scale = HEAD_DIM ** -0.5

    def body(q_ref, k_ref, v_ref, g_ref, o_ref, ct_ref, mx_ref, oacc, cacc):
        i = pl.program_id(1)
        oacc[...] = jnp.zeros_like(oacc)
        cacc[...] = jnp.zeros_like(cacc)
        sums = _sum_matrix(">")

        def block(k0, r0, diagonal):
            rows = pl.ds(r0, tq - r0)
            q = q_ref[rows, :].astype(BF16)
            kj = k_ref[pl.ds(k0, tk), :].astype(BF16)
            vj = v_ref[pl.ds(k0, tk), :].astype(BF16)
            lb, lk = _logsig_pair(_dot(q, kj, NT) * scale)
            if diagonal:
                causal = (lax.broadcasted_iota(jnp.int32, (tq - r0, tk), 1) < lax.broadcasted_iota(jnp.int32, (tq - r0, tk), 0))
                lk = jnp.where(causal, lk, 0.0)
            both = _split_dot(lk, sums)
            c = cacc[rows, :]
            a = jnp.exp(lb + both[:, :tk] + _lanes(c, tk))
            if diagonal:
                a = jnp.where(causal, a, 0.0)
            oacc[rows, :] += _dot(a.astype(BF16), vj, NN)
            cacc[rows, :] = c + both[:, tk:]

        for d in reversed(range(tq // tk)):
            block(pl.multiple_of(i * tq + d * tk, tk), d * tk, True)
        n_pairs = i * (tq // tk // 2)

        def step(it, carry):
            k0 = pl.multiple_of((n_pairs - 1 - it) * 2 * tk, 2 * tk)
            block(pl.multiple_of(k0 + tk, tk), 0, False)
            block(k0, 0, False)
            return carry

        lax.fori_loop(0, n_pairs, step, 0)
        o = oacc[...]
        o_ref[...] = o
        ct_ref[...] = cacc[...]
        mx_ref[...] = _head_out(o, g_ref[...]).astype(BF16)

    blk = pl.BlockSpec((tq, HEAD_DIM), lambda h, i: (i, h))
    return _pcall(
        body, grid=(H, S // tq),
        in_specs=[blk, pl.BlockSpec((S, HEAD_DIM), lambda h, i: (0, H + h)),
                  pl.BlockSpec((S, HEAD_DIM), lambda h, i: (0, 2 * H + h)), pl.BlockSpec((1, HEAD_DIM), lambda h, i: (0, h))],
        out_specs=[blk, blk, blk],
        out_shape=[jax.ShapeDtypeStruct((S, H * HEAD_DIM), F32), jax.ShapeDtypeStruct((S, H * HEAD_DIM), F32),
                   jax.ShapeDtypeStruct((S, H * HEAD_DIM), BF16)],
        scratch_shapes=[pltpu.VMEM((tq, HEAD_DIM), F32), pltpu.VMEM((tq, LANES), F32)],
        compiler_params=_params("parallel", "arbitrary"), name=name)(proj, proj, proj, gain)


def _sb_bwd(proj, gain, o_raw, ctot, dmixed, dm_col0, n_heads, name, tq=512):
    S = proj.shape[0]
    H, tk = n_heads, SB_KEY_BLOCK
    tq = _tile(S, tq, 2 * tk)
    nq = S // tq
    scale = HEAD_DIM ** -0.5

    def body(q_ref, k_ref, v_ref, g_ref, o_ref, ct_ref, dm_ref, dq_ref, dk_ref, dv_ref, dg_ref,
             dkacc, dvacc, dqacc, pfx, gcar, dos):
        i = pl.program_id(1)

        @pl.when(i == 0)
        def _():
            dkacc[...] = jnp.zeros_like(dkacc)
            dvacc[...] = jnp.zeros_like(dvacc)
            dg_ref[...] = jnp.zeros_like(dg_ref)

        o, dm, g = o_ref[...], dm_ref[...], g_ref[...]
        r = _rms_scale(o)
        dos[...] = _rms_bwd(dm * g, o, r).astype(BF16)
        dg_ref[...] += jnp.broadcast_to(jnp.sum(dm * o * r, axis=0, keepdims=True), dg_ref.shape)
        dqacc[...] = jnp.zeros_like(dqacc)
        pfx[...] = jnp.zeros_like(pfx)
        gcar[...] = jnp.zeros_like(gcar)
        later, earlier = _sum_matrix(">"), _sum_matrix("<")

        def block(k0, r0, diagonal):
            rows = pl.ds(r0, tq - r0)
            keys = pl.ds(k0, tk)
            q, do = q_ref[rows, :].astype(BF16), dos[rows, :]
            kj, vj = k_ref[keys, :].astype(BF16), v_ref[keys, :].astype(BF16)
            lb, lk = _logsig_pair(_dot(q, kj, NT) * scale)
            if diagonal:
                causal = (lax.broadcasted_iota(jnp.int32, (tq - r0, tk), 1) < lax.broadcasted_iota(jnp.int32, (tq - r0, tk), 0))
                lk = jnp.where(causal, lk, 0.0)
            both = _split_dot(lk, later)
            p = pfx[rows, :] + both[:, tk:]
            a = jnp.exp(lb + both[:, :tk] + _lanes(ct_ref[rows, :] - p, tk))
            if diagonal:
                a = jnp.where(causal, a, 0.0)
            dl = _dot(do, vj, NT) * a
            dvacc[keys, :] += _dot(a.astype(BF16), do, TN)
            both = _split_dot(dl, earlier)
            gc = gcar[rows, :]
            sig = jnp.exp(lb)
            gsum = (both[:, :tk] + _lanes(gc, tk)) * sig
            if diagonal:
                gsum = jnp.where(causal, gsum, 0.0)
            dz = ((dl * (1.0 - sig) - gsum) * scale).astype(BF16)
            dqacc[rows, :] += _dot(dz, kj, NN)
            dkacc[keys, :] += _dot(dz, q, TN)
            pfx[rows, :] = p
            gcar[rows, :] = gc + both[:, tk:]

        def step(j, carry):
            k0 = pl.multiple_of(j * 2 * tk, 2 * tk)
            block(k0, 0, False)
            block(pl.multiple_of(k0 + tk, tk), 0, False)
            return carry

        lax.fori_loop(0, i * (tq // tk // 2), step, 0)
        for d in range(tq // tk):
            block(pl.multiple_of(i * tq + d * tk, tk), d * tk, True)
        dq_ref[...] = dqacc[...].astype(BF16)

        @pl.when(i == nq - 1)
        def _():
            dk_ref[...] = dkacc[...].astype(BF16)
            dv_ref[...] = dvacc[...].astype(BF16)

    blk = pl.BlockSpec((tq, HEAD_DIM), lambda h, i: (i, h))
    full = pl.BlockSpec((S, HEAD_DIM), lambda h, i: (0, h))
    W = H * HEAD_DIM
    return _pcall(
        body, grid=(H, nq),
        in_specs=[blk, pl.BlockSpec((S, HEAD_DIM), lambda h, i: (0, H + h)),
                  pl.BlockSpec((S, HEAD_DIM), lambda h, i: (0, 2 * H + h)), pl.BlockSpec((1, HEAD_DIM), lambda h, i: (0, h)),
                  blk, blk, pl.BlockSpec((tq, HEAD_DIM), lambda h, i: (i, dm_col0 + h))],
        out_specs=[blk, full, full, pl.BlockSpec((8, HEAD_DIM), lambda h, i: (0, h))],
        out_shape=[jax.ShapeDtypeStruct((S, W), BF16), jax.ShapeDtypeStruct((S, W), BF16),
                   jax.ShapeDtypeStruct((S, W), BF16), jax.ShapeDtypeStruct((8, W), F32)],
        scratch_shapes=[pltpu.VMEM((S, HEAD_DIM), F32), pltpu.VMEM((S, HEAD_DIM), F32), pltpu.VMEM((tq, HEAD_DIM), F32),
                        pltpu.VMEM((tq, LANES), F32), pltpu.VMEM((tq, LANES), F32), pltpu.VMEM((tq, HEAD_DIM), BF16)],
        compiler_params=_params("arbitrary", "arbitrary"), name=name)(proj, proj, proj, gain, o_raw, ctot, dmixed)


def _rope_tables(S):
    inv_freq = ROPE_THETA ** (-jnp.arange(0, HEAD_DIM, 2, dtype=F32) / HEAD_DIM)
    ang = jnp.arange(S, dtype=F32)[:, None] * inv_freq[None, :]
    cos, sin = jnp.cos(ang), jnp.sin(ang)
    return jnp.concatenate([cos, cos], axis=1), jnp.concatenate([-sin, sin], axis=1)


def _rope(v, cos2, sin_signed):
    return v * cos2 + pltpu.roll(v, HEAD_DIM // 2, axis=1) * sin_signed


def _dil_rows(d, r, l0, n):
    if d == 1:
        return pl.ds(l0 if isinstance(l0, int) else pl.multiple_of(l0, KEY_BLOCK), n)
    return pl.ds(r + d * l0, n, stride=d)


def _dil_blocks(S, visit):
    B = KEY_BLOCK
    group = 4
    for b, d in enumerate(DILATIONS):
        nb = S // d // B
        if nb == 1:
            g = math.gcd(d, group)

            def trip(t, carry, b=b, d=d, g=g):
                for u in range(g):
                    visit(b, d, t * g + u, 0, True)
                return carry

            lax.fori_loop(0, d // g, trip, 0)
        elif d == 1:
            visit(b, d, 0, 0, True)
            g = max(k for k in range(1, group + 2) if (nb - 1) % k == 0)

            def trip(t, carry, b=b, d=d, g=g):
                for u in range(g):
                    visit(b, d, 0, (1 + t * g + u) * B, False)
                return carry

            lax.fori_loop(0, (nb - 1) // g, trip, 0)
        else:
            def trip(r, carry, b=b, d=d, nb=nb):
                visit(b, d, r, 0, True)
                for n in range(1, nb):
                    visit(b, d, r, n * B, False)
                return carry

            lax.fori_loop(0, d, trip, 0)


def _dil_mask(first):
    B = KEY_BLOCK
    nk = B if first else 2 * B
    iq = lax.broadcasted_iota(jnp.int32, (B, nk), 0)
    ik = lax.broadcasted_iota(jnp.int32, (B, nk), 1)
    return (ik <= iq) if first else ((ik >= iq) & (ik <= iq + B))


def _dil_fwd(proj, cos2, sin_signed, gain, col0, n_heads, name):
    S = proj.shape[0]
    H, B = n_heads, KEY_BLOCK
    scale = HEAD_DIM ** -0.5
    rc = _tile(S, 256, 8)

    def body(q_ref, k_ref, v_ref, c_ref, s_ref, g_ref, o_ref, l_ref, mx_ref, qr, kr, *per_branch):
        ob, lb = per_branch[:len(DILATIONS)], per_branch[len(DILATIONS):]

        def rope_rows(t, carry):
            rows = pl.ds(pl.multiple_of(t * rc, rc), rc)
            qr[rows, :] = _rope(q_ref[rows, :], c_ref[rows, :], s_ref[rows, :])
            kr[rows, :] = _rope(k_ref[rows, :], c_ref[rows, :], s_ref[rows, :])
            return carry

        lax.fori_loop(0, S // rc, rope_rows, 0)

        def visit(b, d, r, l0, first):
            nk = B if first else 2 * B
            qrows = _dil_rows(d, r, l0, B)
            krows = qrows if first else _dil_rows(d, r, l0 - B, nk)
            s = _dot(qr[qrows, :].astype(BF16), kr[krows, :].astype(BF16), NT) * scale
            s = jnp.where(_dil_mask(first), s, NEG)
            m = jnp.max(s, axis=1, keepdims=True)
            p = jnp.exp(s - m)
            den = jnp.sum(p, axis=1, keepdims=True)
            ob[b][qrows, :] = _dot(p.astype(BF16), v_ref[krows, :].astype(BF16), NN) / den
            lb[b][qrows, :] = jnp.broadcast_to(m + jnp.log(den), (B, LANES))

        _dil_blocks(S, visit)

        def combine(t, carry):
            rows = pl.ds(pl.multiple_of(t * rc, rc), rc)
            l0, l1, l2 = lb[0][rows, :], lb[1][rows, :], lb[2][rows, :]
            m = jnp.maximum(jnp.maximum(l0, l1), l2)
            w0, w1, w2 = jnp.exp(l0 - m), jnp.exp(l1 - m), jnp.exp(l2 - m)
            den = w0 + w1 + w2
            o = (w0 * ob[0][rows, :] + w1 * ob[1][rows, :] + w2 * ob[2][rows, :]) / den
            o_ref[rows, :] = o
            l_ref[rows, :] = m + jnp.log(den)
            mx_ref[rows, :] = _head_out(o, g_ref[...]).astype(BF16)
            return carry

        lax.fori_loop(0, S // rc, combine, 0)

    def col(k):
        return pl.BlockSpec((S, HEAD_DIM), lambda h: (0, col0 + k * H + h))

    tab = pl.BlockSpec((S, HEAD_DIM), lambda h: (0, 0))
    out = pl.BlockSpec((S, HEAD_DIM), lambda h: (0, h))
    W = H * HEAD_DIM
    return _pcall(
        body, grid=(H,),
        in_specs=[col(0), col(1), col(2), tab, tab, pl.BlockSpec((1, HEAD_DIM), lambda h: (0, h))],
        out_specs=[out, out, out],
        out_shape=[jax.ShapeDtypeStruct((S, W), F32), jax.ShapeDtypeStruct((S, W), F32), jax.ShapeDtypeStruct((S, W), BF16)],
        scratch_shapes=[pltpu.VMEM((S, HEAD_DIM), F32)] * (2 + 2 * len(DILATIONS)),
        compiler_params=_params("parallel"), name=name)(proj, proj, proj, cos2, sin_signed, gain)


def _dil_bwd(proj, cos2, sin_signed, gain, o_raw, lse, dmixed, dm_col0, col0, n_heads, name):
    S = proj.shape[0]
    H, B = n_heads, KEY_BLOCK
    scale = HEAD_DIM ** -0.5
    rc = _tile(S, 256, 8)

    def body(q_ref, k_ref, v_ref, c_ref, s_ref, g_ref, o_ref, l_ref, dm_ref, dq_ref, dk_ref, dv_ref, dg_ref,
             qr, kr, dos, dsum, dqr, dkr, dvv):
        dg_ref[...] = jnp.zeros_like(dg_ref)

        def prep(t, carry):
            rows = pl.ds(pl.multiple_of(t * rc, rc), rc)
            qr[rows, :] = _rope(q_ref[rows, :], c_ref[rows, :], s_ref[rows, :])
            kr[rows, :] = _rope(k_ref[rows, :], c_ref[rows, :], s_ref[rows, :])
            o, dm = o_ref[rows, :], dm_ref[rows, :]
            r = _rms_scale(o)
            do = _rms_bwd(dm * g_ref[...], o, r)
            dg_ref[...] += jnp.broadcast_to(jnp.sum(dm * o * r, axis=0, keepdims=True), dg_ref.shape)
            dos[rows, :] = do
            dsum[rows, :] = jnp.broadcast_to(jnp.sum(do * o, axis=1, keepdims=True), (rc, LANES))
            dqr[rows, :] = jnp.zeros((rc, HEAD_DIM), F32)
            dkr[rows, :] = jnp.zeros((rc, HEAD_DIM), F32)
            dvv[rows, :] = jnp.zeros((rc, HEAD_DIM), F32)
            return carry

        lax.fori_loop(0, S // rc, prep, 0)

        def visit(b, d, r, l0, first):
            nk = B if first else 2 * B
            qrows = _dil_rows(d, r, l0, B)
            krows = qrows if first else _dil_rows(d, r, l0 - B, nk)
            qs, ks = qr[qrows, :].astype(BF16), kr[krows, :].astype(BF16)
            do = dos[qrows, :].astype(BF16)
            s = _dot(qs, ks, NT) * scale
            s = jnp.where(_dil_mask(first), s, NEG)
            p = jnp.exp(s - l_ref[qrows, :][:, 0:1])
            dp = _dot(do, v_ref[krows, :].astype(BF16), NT)
            ds = (p * (dp - dsum[qrows, :][:, 0:1]) * scale).astype(BF16)
            dqr[qrows, :] += _dot(ds, ks, NN)
            dkr[krows, :] += _dot(ds, qs, TN)
            dvv[krows, :] += _dot(p.astype(BF16), do, TN)

        _dil_blocks(S, visit)

        def finish(t, carry):
            rows = pl.ds(pl.multiple_of(t * rc, rc), rc)
            c, s = c_ref[rows, :], s_ref[rows, :]
            dq, dk = dqr[rows, :], dkr[rows, :]
            dq_ref[rows, :] = (dq * c + pltpu.roll(dq * s, HEAD_DIM // 2, axis=1)).astype(BF16)
            dk_ref[rows, :] = (dk * c + pltpu.roll(dk * s, HEAD_DIM // 2, axis=1)).astype(BF16)
            dv_ref[rows, :] = dvv[rows, :].astype(BF16)
            return carry

        lax.fori_loop(0, S // rc, finish, 0)

    def col(k):
        return pl.BlockSpec((S, HEAD_DIM), lambda h: (0, col0 + k * H + h))

    tab = pl.BlockSpec((S, HEAD_DIM), lambda h: (0, 0))
    out = pl.BlockSpec((S, HEAD_DIM), lambda h: (0, h))
    W = H * HEAD_DIM
    big = pltpu.VMEM((S, HEAD_DIM), F32)
    return _pcall(
        body, grid=(H,),
        in_specs=[col(0), col(1), col(2), tab, tab, pl.BlockSpec((1, HEAD_DIM), lambda h: (0, h)), out, out,
                  pl.BlockSpec((S, HEAD_DIM), lambda h: (0, dm_col0 + h))],
        out_specs=[out, out, out, pl.BlockSpec((8, HEAD_DIM), lambda h: (0, h))],
        out_shape=[jax.ShapeDtypeStruct((S, W), BF16), jax.ShapeDtypeStruct((S, W), BF16),
                   jax.ShapeDtypeStruct((S, W), BF16), jax.ShapeDtypeStruct((8, W), F32)],
        scratch_shapes=[big, big, big, pltpu.VMEM((S, LANES), F32), big, big, big],
        compiler_params=_params("parallel"), name=name)(proj, proj, proj, cos2, sin_signed, gain, o_raw, lse, dmixed)


GELU_C = math.sqrt(2.0 / math.pi)
GELU_A = 0.044715
HALO = 16


def _shift_down(cur, halo, k):
    out = pltpu.roll(cur, k, axis=0)
    row = lax.broadcasted_iota(jnp.int32, cur.shape, 0)
    for t in range(k):
        out = jnp.where(row == t, halo[HALO - k + t:HALO - k + t + 1, :], out)
    return out


def _shift_up(cur, halo, k):
    n = cur.shape[0]
    out = pltpu.roll(cur, n - k, axis=0)
    row = lax.broadcasted_iota(jnp.int32, cur.shape, 0)
    for t in range(k):
        out = jnp.where(row == n - k + t, halo[t:t + 1, :], out)
    return out


def _conv3(cur, halo, cw):
    return _shift_down(cur, halo, 2) * cw[0:1, :] + _shift_down(cur, halo, 1) * cw[1:2, :] + cur * cw[2:3, :] + cw[3:4, :]


def _gelu_parts(x):
    t = jnp.tanh(GELU_C * (x + GELU_A * x * x * x))
    return 0.5 * x * (1.0 + t), t


def _geglu_specs(tm, tn, ncb):
    hb = tm // HALO

    def cur(off):
        return pl.BlockSpec((tm, tn), lambda j, i: (i, off + j))

    def prev(off):
        return pl.BlockSpec((HALO, tn), lambda j, i: (jnp.maximum(i * hb - 1, 0), off + j))

    def taps(off):
        return pl.BlockSpec((8, tn), lambda j, i: (0, off + j))

    return [cur(0), prev(0), cur(ncb), prev(ncb), taps(0), taps(ncb)]


def _geglu_fwd(u, cwb, name, tm=256, tn=1408):
    S, F2 = u.shape
    F = F2 // 2
    tm, tn = _tile(S, tm, HALO), _tile(F, tn)
    ncb = F // tn

    def body(g_ref, gp_ref, v_ref, vp_ref, cg_ref, cv_ref, y_ref):
        top = pl.program_id(1) > 0
        gp = jnp.where(top, gp_ref[...].astype(F32), 0.0)
        vp = jnp.where(top, vp_ref[...].astype(F32), 0.0)
        gc = _conv3(g_ref[...].astype(F32), gp, cg_ref[...])
        vc = _conv3(v_ref[...].astype(F32), vp, cv_ref[...])
        y_ref[...] = (_gelu_parts(gc)[0] * vc).astype(BF16)

    return _pcall(body, grid=(ncb, S // tm), in_specs=_geglu_specs(tm, tn, ncb),
                  out_specs=pl.BlockSpec((tm, tn), lambda j, i: (i, j)),
                  out_shape=jax.ShapeDtypeStruct((S, F), BF16),
                  compiler_params=_params("parallel", "parallel"), name=name)(u, u, u, u, cwb, cwb)


def _geglu_bwd(u, dy, cwb, name, tm=256, tn=512):
    S, F2 = u.shape
    F = F2 // 2
    tm, tn = _tile(S, tm, HALO), _tile(F, tn)
    ncb = F // tn

    def body(g_ref, gp_ref, v_ref, vp_ref, cg_ref, cv_ref, dy_ref, dc_ref, dwg_ref, dwv_ref):
        i = pl.program_id(1)

        @pl.when(i == 0)
        def _():
            dwg_ref[...] = jnp.zeros_like(dwg_ref)
            dwv_ref[...] = jnp.zeros_like(dwv_ref)

        top = i > 0
        g, v = g_ref[...].astype(F32), v_ref[...].astype(F32)
        gp = jnp.where(top, gp_ref[...].astype(F32), 0.0)
        vp = jnp.where(top, vp_ref[...].astype(F32), 0.0)
        gc = _conv3(g, gp, cg_ref[...])
        vc = _conv3(v, vp, cv_ref[...])
        act, t = _gelu_parts(gc)
        dact = 0.5 * (1.0 + t) + 0.5 * gc * (1.0 - t * t) * GELU_C * (1.0 + 3.0 * GELU_A * gc * gc)
        dyv = dy_ref[...].astype(F32)
        dgc = dyv * vc * dact
        dvc = dyv * act
        dc_ref[0] = dgc.astype(BF16)
        dc_ref[1] = dvc.astype(BF16)

        def taps(out_ref, dc, cur, halo):
            out_ref[0:1, :] += jnp.sum(dc * _shift_down(cur, halo, 2), axis=0, keepdims=True)
            out_ref[1:2, :] += jnp.sum(dc * _shift_down(cur, halo, 1), axis=0, keepdims=True)
            out_ref[2:3, :] += jnp.sum(dc * cur, axis=0, keepdims=True)
            out_ref[3:4, :] += jnp.sum(dc, axis=0, keepdims=True)

        taps(dwg_ref, dgc, g, gp)
        taps(dwv_ref, dvc, v, vp)

    return _pcall(body, grid=(ncb, S // tm),
                  in_specs=_geglu_specs(tm, tn, ncb) + [pl.BlockSpec((tm, tn), lambda j, i: (i, j))],
                  out_specs=[pl.BlockSpec((2, tm, tn), lambda j, i: (0, i, j)),
                             pl.BlockSpec((8, tn), lambda j, i: (0, j)), pl.BlockSpec((8, tn), lambda j, i: (0, j))],
                  out_shape=[jax.ShapeDtypeStruct((2, S, F), BF16), jax.ShapeDtypeStruct((8, F), F32),
                             jax.ShapeDtypeStruct((8, F), F32)],
                  compiler_params=_params("parallel", "arbitrary"), name=name)(u, u, u, u, cwb, cwb, dy)


def _conv_bwd(dc, cwb, name, tm=512, tn=1408):
    _, S, F = dc.shape
    tm, tn = _tile(S, tm, HALO), _tile(F, tn)
    ncb, nrb = F // tn, S // tm
    hb = tm // HALO

    def body(c_ref, n_ref, w_ref, du_ref):
        cur = c_ref[...].astype(F32)
        nxt = jnp.where(pl.program_id(2) < nrb - 1, n_ref[...].astype(F32), 0.0)
        w = w_ref[...]
        du = cur * w[2:3, :] + _shift_up(cur, nxt, 1) * w[1:2, :] + _shift_up(cur, nxt, 2) * w[0:1, :]
        du_ref[...] = du.astype(BF16)

    return _pcall(body, grid=(2, ncb, nrb),
                  in_specs=[pl.BlockSpec((None, tm, tn), lambda c, j, i: (c, i, j)),
                            pl.BlockSpec((None, HALO, tn), lambda c, j, i: (c, jnp.minimum((i + 1) * hb, S // HALO - 1), j)),
                            pl.BlockSpec((8, tn), lambda c, j, i: (0, c * ncb + j))],
                  out_specs=pl.BlockSpec((tm, tn), lambda c, j, i: (i, c * ncb + j)),
                  out_shape=jax.ShapeDtypeStruct((S, 2 * F), BF16),
                  compiler_params=_params("parallel", "parallel", "parallel"), name=name)(dc, dc, cwb)


def _adam_math(w, g, m, v):
    m = ADAM_B1 * m + (1.0 - ADAM_B1) * g
    v = ADAM_B2 * v + (1.0 - ADAM_B2) * (g * g)
    m_hat = m / (1.0 - ADAM_B1 ** ADAM_STEP)
    v_hat = v / (1.0 - ADAM_B2 ** ADAM_STEP)
    return -ADAM_LR * (m_hat / (jnp.sqrt(v_hat) + ADAM_EPS) + ADAM_WD * w), m, v


def _adamw(w, parts, m, v, name, tr=256):
    R, C = w.shape
    n, _, Cp = parts.shape
    tr = _tile(R, tr, 8)

    def body(w_ref, p_ref, m_ref, v_ref, g_out, d_out, m_out, v_out):
        g = p_ref[0, :, 0:C].astype(F32)
        for k in range(1, n):
            g = g + p_ref[k, :, 0:C].astype(F32)
        d, mn, vn = _adam_math(w_ref[...], g, m_ref[...], v_ref[...])
        g_out[...] = g
        d_out[...] = d
        m_out[...] = mn
        v_out[...] = vn

    spec = pl.BlockSpec((tr, C), lambda i: (i, 0))
    shape = jax.ShapeDtypeStruct((R, C), F32)
    return _pcall(body, grid=(R // tr,), in_specs=[spec, pl.BlockSpec((n, tr, Cp), lambda i: (0, i, 0)), spec, spec],
                  out_specs=[spec] * 4, out_shape=[shape] * 4, compiler_params=_params("parallel"), name=name)(w, parts, m, v)


def _adamw_chips(w, pair, parts, chip_ids, m, v, name, tr=256):
    R, C = w.shape
    Cp = pair.shape[2]
    tr = _tile(R, tr, 16)

    def body(ids_ref, w_ref, own_ref, p1_ref, p2_ref, p3_ref, m_ref, v_ref, g_out, d_out, m_out, v_out):
        g = own_ref[:, 0:C].astype(F32)
        for ref in (p1_ref, p2_ref, p3_ref):
            g = g + ref[:, 0:C].astype(F32)
        d, mn, vn = _adam_math(w_ref[...], g, m_ref[...], v_ref[...])
        g_out[...] = g
        d_out[...] = d
        m_out[...] = mn
        v_out[...] = vn

    spec = pl.BlockSpec((tr, C), lambda i, ids: (i, 0))

    def chip(k):
        return pl.BlockSpec((None, tr, Cp), lambda i, ids: (ids[k], i, 0))

    shape = jax.ShapeDtypeStruct((R, C), F32)
    grid_spec = pltpu.PrefetchScalarGridSpec(
        num_scalar_prefetch=1, grid=(R // tr,), in_specs=[spec, chip(0), chip(1), chip(2), chip(3), spec, spec],
        out_specs=[spec] * 4)
    return _pcall(body, grid_spec=grid_spec, out_shape=[shape] * 4, compiler_params=_params("parallel"),
                  name=name)(chip_ids, w, pair, parts, parts, parts, m, v)


def _place():
    return lax.axis_index("x"), lax.axis_index("y"), lax.axis_index("c")


def _other_chips(x, y):
    return [(1 - x, y), (x, 1 - y), (1 - x, 1 - y)]


IN_HBM = pl.BlockSpec(memory_space=pltpu.HBM)
SEM = pl.BlockSpec(memory_space=pltpu.SEMAPHORE)
EFFECT = pltpu.SideEffectType.DATAFLOW_SIDE_EFFECTING
TOKEN = jax.ShapeDtypeStruct((8, LANES), F32)
TOKEN_SPEC = pl.BlockSpec(memory_space=pltpu.VMEM)


def _in_hbm(a):
    return pltpu.with_memory_space_constraint(a, pltpu.HBM)


def _landing(shape):
    return _in_hbm(lax.empty(shape.shape, shape.dtype))


def _hbm_like(a):
    return pltpu.HBM(a.shape, a.dtype)


def _gather_start(landing, slots, after, name):
    na = len(landing)

    def body(*refs):
        land = refs[:na]
        send_sems, recv_sems = refs[na + 1], refs[na + 2]
        token = refs[-1]
        x, y, c = _place()
        for a in range(na):
            own = slots[a](land[a], x, y, c)
            for k, to in enumerate([(x, y, 1 - c)] + [(*chip, c) for chip in _other_chips(x, y)]):
                pltpu.make_async_remote_copy(
                    src_ref=own, dst_ref=own, send_sem=send_sems.at[4 * a + k],
                    recv_sem=recv_sems.at[4 * a + k], device_id=to, device_id_type=MESH).start()
        token[...] = jnp.zeros_like(token)

    sems = pltpu.SemaphoreType.DMA((4 * na,))
    outs = _pcall(
        body, in_specs=[IN_HBM] * na + [HBM],
        out_specs=[SEM, SEM] + [IN_HBM] * na + [TOKEN_SPEC],
        out_shape=[sems, sems] + [_hbm_like(s) for s in landing] + [TOKEN],
        input_output_aliases={a: 2 + a for a in range(na)},
        compiler_params=pltpu.CompilerParams(has_side_effects=EFFECT), name=name,
    )(*[_in_hbm(s) for s in landing], after)
    return outs[0], outs[1], outs[2:2 + na], outs[-1]


def _gather_forward(gathered, send_sems, recv_sems, slots, after, name):
    na = len(gathered)

    def body(*refs):
        gath = refs[:na]
        send1, recv1 = refs[na], refs[na + 1]
        fsend, frecv = refs[na + 3], refs[na + 4]
        token = refs[-1]
        x, y, c = _place()
        chips = _other_chips(x, y)
        for a in range(na):
            for k, peer in enumerate([(x, y, 1 - c)] + [(*chip, c) for chip in chips]):
                arrival = pltpu.make_async_remote_copy(
                    src_ref=slots[a](gath[a], x, y, c), dst_ref=slots[a](gath[a], *peer), send_sem=send1.at[4 * a + k],
                    recv_sem=recv1.at[4 * a + k], device_id=peer, device_id_type=MESH)
                arrival.wait_send()
                arrival.wait_recv()
        for a in range(na):
            for j, chip in enumerate(chips):
                view = slots[a](gath[a], *chip, c)
                pltpu.make_async_remote_copy(
                    src_ref=view, dst_ref=view, send_sem=fsend.at[3 * a + j], recv_sem=frecv.at[3 * a + j],
                    device_id=(x, y, 1 - c), device_id_type=MESH).start()
        token[...] = jnp.zeros_like(token)

    sems = pltpu.SemaphoreType.DMA((3 * na,))
    outs = _pcall(
        body, in_specs=[IN_HBM] * na + [SEM, SEM, HBM],
        out_specs=[SEM, SEM] + [IN_HBM] * na + [TOKEN_SPEC],
        out_shape=[sems, sems] + [_hbm_like(g) for g in gathered] + [TOKEN],
        input_output_aliases={a: 2 + a for a in range(na)},
        compiler_params=pltpu.CompilerParams(has_side_effects=EFFECT), name=name,
    )(*gathered, send_sems, recv_sems, after)
    return outs[0], outs[1], outs[2:2 + na], outs[-1]


def _gather_finish(gathered, fsend, frecv, slots, after, name):
    na = len(gathered)

    def body(*refs):
        gath, fs, fr = refs[:na], refs[na], refs[na + 1]
        x, y, c = _place()
        for a in range(na):
            for j, chip in enumerate(_other_chips(x, y)):
                passed = pltpu.make_async_remote_copy(
                    src_ref=slots[a](gath[a], *chip, c), dst_ref=slots[a](gath[a], *chip, 1 - c),
                    send_sem=fs.at[3 * a + j], recv_sem=fr.at[3 * a + j], device_id=(x, y, 1 - c), device_id_type=MESH)
                passed.wait_send()
                passed.wait_recv()

    outs = _pcall(
        body, in_specs=[IN_HBM] * na + [SEM, SEM, HBM], out_specs=[IN_HBM] * na,
        out_shape=[_hbm_like(g) for g in gathered], input_output_aliases={a: a for a in range(na)},
        compiler_params=pltpu.CompilerParams(has_side_effects=EFFECT), name=name,
    )(*gathered, fsend, frecv, after)
    return list(outs)


def _pair_copy(view, src, land, send_sems, recv_sems, chip):
    x, y, c = _place()
    return pltpu.make_async_remote_copy(
        src_ref=view(src, chip, 1 - c), dst_ref=land.at[chip], send_sem=send_sems.at[chip], recv_sem=recv_sems.at[chip],
        device_id=(x, y, 1 - c), device_id_type=MESH)


def _pair_start(grad, view, block, after, name):
    def body(src, land, after_ref, send_sems, recv_sems, src_thru, land_thru, token):
        for chip in range(N_CHIP):
            _pair_copy(view, src, land, send_sems, recv_sems, chip).start()
        token[...] = jnp.zeros_like(token)

    sems = pltpu.SemaphoreType.DMA((N_CHIP,))
    land = jax.ShapeDtypeStruct((N_CHIP, *block), BF16)
    return _pcall(
        body, in_specs=[IN_HBM, IN_HBM, HBM], out_specs=[SEM, SEM, IN_HBM, IN_HBM, TOKEN_SPEC],
        out_shape=[sems, sems, _hbm_like(grad), _hbm_like(land), TOKEN], input_output_aliases={0: 2, 1: 3},
        compiler_params=pltpu.CompilerParams(has_side_effects=EFFECT), name=name,
    )(_in_hbm(grad), _landing(land), after)


def _pair_wait(grad, recv, send_sems, recv_sems, view, after, name):
    def body(src, land, send, recv_s, after_ref, src_thru, land_thru):
        for chip in range(N_CHIP):
            copy = _pair_copy(view, src, land, send, recv_s, chip)
            copy.wait_send()
            copy.wait_recv()

    return _pcall(
        body, in_specs=[IN_HBM, IN_HBM, SEM, SEM, HBM], out_specs=[IN_HBM, IN_HBM],
        out_shape=[_hbm_like(grad), _hbm_like(recv)], input_output_aliases={0: 0, 1: 1},
        compiler_params=pltpu.CompilerParams(has_side_effects=EFFECT), name=name,
    )(grad, recv, send_sems, recv_sems, after)


def _chip_start(pair, after, name):
    def body(src, land, after_ref, send_sems, recv_sems, src_thru, land_thru, token):
        x, y, c = _place()
        for j, (px, py) in enumerate(_other_chips(x, y)):
            pltpu.make_async_remote_copy(
                src_ref=src.at[2 * px + py], dst_ref=land.at[2 * x + y], send_sem=send_sems.at[j], recv_sem=recv_sems.at[j],
                device_id=(px, py, c), device_id_type=MESH).start()
        token[...] = jnp.zeros_like(token)

    sems = pltpu.SemaphoreType.DMA((3,))
    return _pcall(
        body, in_specs=[IN_HBM, IN_HBM, HBM], out_specs=[SEM, SEM, IN_HBM, IN_HBM, TOKEN_SPEC],
        out_shape=[sems, sems, _hbm_like(pair), _hbm_like(pair), TOKEN], input_output_aliases={0: 2, 1: 3},
        compiler_params=pltpu.CompilerParams(has_side_effects=EFFECT), name=name,
    )(_in_hbm(pair), _landing(pair), after)


def _chip_wait(pair, parts, send_sems, recv_sems, after, name):
    def body(src, land, send, recv, after_ref, src_thru, land_thru):
        x, y, c = _place()
        for j, (px, py) in enumerate(_other_chips(x, y)):
            copy = pltpu.make_async_remote_copy(
                src_ref=src.at[2 * px + py], dst_ref=land.at[2 * px + py], send_sem=send.at[j], recv_sem=recv.at[j],
                device_id=(px, py, c), device_id_type=MESH)
            copy.wait_send()
            copy.wait_recv()

    return _pcall(
        body, in_specs=[IN_HBM, IN_HBM, SEM, SEM, HBM], out_specs=[IN_HBM, IN_HBM],
        out_shape=[_hbm_like(pair), _hbm_like(parts)], input_output_aliases={0: 0, 1: 1},
        compiler_params=pltpu.CompilerParams(has_side_effects=EFFECT), name=name,
    )(pair, parts, send_sems, recv_sems, after)


def _pair_add(core, grad, recv, block, grad_spec, name):
    _, R, C = recv.shape
    tr = block

    def body(c_ref, g_ref, r_ref, o_ref):
        o_ref[...] = (g_ref[...].astype(F32) + r_ref[...].astype(F32)).astype(BF16)

    grid_spec = pltpu.PrefetchScalarGridSpec(
        num_scalar_prefetch=1, grid=(N_CHIP, R // tr),
        in_specs=[grad_spec, pl.BlockSpec((None, tr, C), lambda k, i, c: (k, i, 0))],
        out_specs=pl.BlockSpec((None, tr, C), lambda k, i, c: (k, i, 0)))
    return _pcall(body, grid_spec=grid_spec, out_shape=jax.ShapeDtypeStruct(recv.shape, BF16),
                  compiler_params=_params("parallel", "parallel"), name=name)(core, grad, recv)


def _all_reduce_small(part, name):
    R = part.shape[0]

    def body(p_ref, o_ref, all_ref, send_sems, recv_sems):
        x, y, c = _place()
        me = 4 * x + 2 * y + c
        all_ref[me] = p_ref[...]
        peers = [(x, y, 1 - c)] + [(px, py, pc) for px, py in _other_chips(x, y) for pc in (c, 1 - c)]
        copies = [pltpu.make_async_remote_copy(
            src_ref=p_ref, dst_ref=all_ref.at[me], send_sem=send_sems.at[k], recv_sem=recv_sems.at[k],
            device_id=peer, device_id_type=MESH) for k, peer in enumerate(peers)]
        for cp in copies:
            cp.start()
        for k, (px, py, pc) in enumerate(peers):
            pltpu.make_async_remote_copy(
                src_ref=p_ref, dst_ref=all_ref.at[4 * px + 2 * py + pc], send_sem=send_sems.at[k], recv_sem=recv_sems.at[k],
                device_id=peers[k], device_id_type=MESH).wait_recv()
        for cp in copies:
            cp.wait_send()
        acc = all_ref[0]
        for k in range(1, N_DEV):
            acc = acc + all_ref[k]
        o_ref[...] = acc

    vm = pl.BlockSpec(memory_space=pltpu.VMEM)
    return _pcall(body, in_specs=[vm], out_specs=vm, out_shape=jax.ShapeDtypeStruct((R, LANES), F32),
                  scratch_shapes=[pltpu.VMEM((N_DEV, R, LANES), F32), pltpu.SemaphoreType.DMA((7,)), pltpu.SemaphoreType.DMA((7,))],
                  name=name)(part)


def _local_step(x, tgt, gains, weights):
    g_pre_mix, g_post_mix, g_pre_ffn, g_post_ffn, g_sb, g_dil = gains
    S, D = x.shape
    hs = g_sb.shape[1] // HEAD_DIM
    hd = g_dil.shape[1] // HEAD_DIM
    cos2, sin_signed = _rope_tables(S)

    h1 = _rms_fwd(x, g_pre_mix + weights.start(), "rms_in")
    w_in_g = weights.w_in(h1)
    proj = _mm_nn(h1, w_in_g, F32, "proj", tn=768)
    o_sb, ct_sb, mx_sb = _sb_fwd(proj, g_sb, hs, "sb_fwd")
    o_dl, lse_dl, mx_dl = _dil_fwd(proj, cos2, sin_signed, g_dil + weights.forward_out(o_sb), 3 * hs, hd, "dil_fwd")
    w_out_g, dep = weights.w_out(o_dl)
    mixed = jnp.concatenate([mx_sb, mx_dl], axis=1)
    mix = _mm_nn(mixed, w_out_g, F32, "mix_out", tn=1024)
    x2, h2 = _mid_fwd(x, mix, g_post_mix + dep, g_pre_ffn, "mid_fwd")
    w_up_g, cwb = weights.w_up(h2)
    u = _mm_nn(h2, w_up_g, BF16, "ffn_up")
    y = _geglu_fwd(u, cwb + weights.forward_down(u), "geglu_fwd")
    w_down_g = weights.w_down(y)
    f = _mm_nn(y, w_down_g, F32, "ffn_down", tn=1024, tk=1408)

    dy, df, dg_post_ffn, loss = _loss_bwd(x2, f, tgt, g_post_ffn, "loss_bwd")
    dyv = _mm_nt(df, w_down_g, BF16, "d_y", tn=1408)
    dw_down = _mm_tn(y, df, D, BF16, "dw_down", tm=1408, tn=1024)
    dc, dcw_g, dcw_v = _geglu_bwd(u, dyv, cwb + weights.grad("w_down", dw_down), "geglu_bwd")
    du = _conv_bwd(dc, cwb + weights.grad_reduce("w_down", dc), "conv_bwd")
    dh2 = _mm_nt(du, w_up_g, F32, "d_h2", tk=1408)
    dw_up = _mm_tn(h2, du, w_up_g.shape[2], BF16, "dw_up")
    dx2, dmix, dg_pre_ffn, dg_post_mix = _mid_bwd(
        dy, dh2, x2, mix, g_pre_ffn + weights.grad("w_up", dw_up), g_post_mix, "mid_bwd")
    dmixed = _mm_nt(dmix, w_out_g, F32, "d_mixed", after=jnp.reshape(weights.grad_reduce("w_up", dmix), (1, 1)))
    dw_out = _mm_tn(mixed, dmix, D, BF16, "dw_out", tn=1024)
    dq_s, dk_s, dv_s, dg_sb = _sb_bwd(proj, g_sb + weights.grad("w_out", dw_out), o_sb, ct_sb, dmixed, 0, hs, "sb_bwd")
    dq_d, dk_d, dv_d, dg_dil = _dil_bwd(proj, cos2, sin_signed, g_dil + weights.grad_reduce("w_out", dq_s), o_dl, lse_dl,
                                        dmixed, hs, 3 * hs, hd, "dil_bwd")
    dproj = jnp.concatenate([dq_s, dk_s, dv_s, dq_d, dk_d, dv_d], axis=1)
    dh1 = _mm_nt(dproj, w_in_g, F32, "d_h1", tk=768)
    grad_x, dg_pre_mix = _first_bwd(dx2, dh1, x, g_pre_mix, "first_bwd")
    small = (dg_pre_mix, dg_post_mix, dg_pre_ffn, dg_post_ffn, dg_sb[0:1], dg_dil[0:1], jnp.concatenate([dcw_g, dcw_v], axis=1))
    dw_in = _mm_tn(h1, dproj, w_in_g.shape[2], BF16, "dw_in", tn=768, after=weights.small(small, loss))
    weights.grad("w_in", dw_in)
    weights.grad_reduce("w_in", grad_x)
    return loss, grad_x, small


def _pad_cols(a, to):
    return jnp.pad(a, ((0, 0), (0, to - a.shape[1])))


def kernel(x, pre_mix_gain, post_mix_gain, pre_ffn_gain, post_ffn_gain, w_in, sb_out_gain, dil_out_gain, w_out, w_up, conv_w, conv_b, w_down, loss_target, m_pre_mix_gain, m_post_mix_gain, m_pre_ffn_gain, m_post_ffn_gain, m_w_in, m_sb_out_gain, m_dil_out_gain, m_w_out, m_w_up, m_conv_w, m_conv_b, m_w_down, v_pre_mix_gain, v_post_mix_gain, v_pre_ffn_gain, v_post_ffn_gain, v_w_in, v_sb_out_gain, v_dil_out_gain, v_w_out, v_w_up, v_conv_w, v_conv_b, v_w_down):
    xb, tb = x[0], loss_target[0]
    S, D = xb.shape
    w_in, w_out, w_up, w_down, conv_w = w_in[0], w_out[0], w_up[0], w_down[0], conv_w[0]
    n_in, e_rows = w_in.shape[1], w_out.shape[0]
    cu, half = w_up.shape[1], w_down.shape[0]
    assert cu == 2 * half and half % 16 == 0
    cup = -(-cu // LANES) * LANES
    fp = N_CHIP * cup
    px, py, pc = _place()
    me = 4 * px + 2 * py + pc
    core = jnp.reshape(pc, (1,)).astype(jnp.int32)

    shards = [w_in.astype(BF16), w_out.astype(BF16), _pad_cols(w_up, cup).astype(BF16), w_down.astype(BF16),
              jnp.pad(_pad_cols(conv_w, cup), ((0, 8 - conv_w.shape[0]), (0, 0)))]

    def by_dev(ref, qx, qy, qc):
        return ref.at[4 * qx + 2 * qy + qc]

    def down_slot(ref, qx, qy, qc):
        return ref.at[2 * qx + qy, pl.ds(qc * half, half)]

    def by_pair(ref, chip, k):
        return ref.at[chip, k]

    def down_pair(ref, chip, k):
        return ref.at[chip, pl.ds(k * half, half)]

    def pair_spec(tr, cols):
        return pl.BlockSpec((None, None, tr, cols), lambda k, i, c: (k, c[0], i, 0))

    tr_in, tr_up = _tile(D, 512, 16), _tile(D, 256, 16)
    grad_plan = {
        "w_in": ((N_CHIP, 2, D, n_in), by_pair, (D, n_in), tr_in, pair_spec(tr_in, n_in)),
        "w_out": ((N_CHIP, 2, e_rows, D), by_pair, (e_rows, D), e_rows, pair_spec(e_rows, D)),
        "w_up": ((N_CHIP, 2, D, cup), by_pair, (D, cup), tr_up, pair_spec(tr_up, cup)),
        "w_down": ((N_CHIP, cup, D), down_pair, (half, D), half,
                   pl.BlockSpec((None, half, D), lambda k, i, c: (k, c[0], 0))),
    }

    class Exchanges:
        def __init__(self):
            self.in_flight = {}

        def start(self):
            def own_slot(shard):
                return lax.dynamic_update_index_in_dim(lax.empty((N_DEV, *shard.shape), shard.dtype), shard, me, 0)

            down = lax.dynamic_update_slice(jnp.zeros((N_CHIP, cup, D), BF16), shards[3][None], (2 * px + py, pc * half, 0))
            landing = [own_slot(shards[0]), own_slot(shards[1]), own_slot(shards[2]), down, own_slot(shards[4])]
            self.g_in = _gather_start(landing[:1], [by_dev], core, "gather_in_start")
            self.g_out = _gather_start(landing[1:2], [by_dev], self.g_in[3], "gather_out_start")
            self.g_up = _gather_start([landing[2], landing[4]], [by_dev, by_dev], self.g_out[3], "gather_up_start")
            self.g_down = _gather_start(landing[3:4], [down_slot], self.g_up[3], "gather_down_start")
            return self.g_down[3][0, 0]

        def w_in(self, after):
            send, recv, gath, _ = self.g_in
            fsend, frecv, gath, token = _gather_forward(gath, send, recv, [by_dev], after, "gather_in_forward")
            return _gather_finish(gath, fsend, frecv, [by_dev], token, "gather_in_finish")[0]

        def forward_out(self, after):
            send, recv, gath, _ = self.g_out
            self.p_out = _gather_forward(gath, send, recv, [by_dev], after, "gather_out_forward")
            return self.p_out[3][0, 0]

        def w_out(self, after):
            fsend, frecv, gath, _ = self.p_out
            w_out_g = _gather_finish(gath, fsend, frecv, [by_dev], after, "gather_out_finish")[0]
            send, recv, gath, _ = self.g_up
            self.p_up = _gather_forward(gath, send, recv, [by_dev, by_dev], w_out_g, "gather_up_forward")
            return w_out_g.reshape(1, N_DEV * e_rows, D), self.p_up[3][0, 0]

        def w_up(self, after):
            fsend, frecv, gath, _ = self.p_up
            w_up_g, cw_g = _gather_finish(gath, fsend, frecv, [by_dev, by_dev], after, "gather_up_finish")
            cb = _pad_cols(conv_b.reshape(N_DEV, cu), cup).reshape(1, 2 * fp)
            cw_full = jnp.transpose(cw_g[:, :3, :], (1, 0, 2)).reshape(3, 2 * fp)
            cwb = jnp.concatenate([cw_full, cb, jnp.zeros((4, 2 * fp), F32)], axis=0)
            return w_up_g, cwb

        def forward_down(self, after):
            send, recv, gath, _ = self.g_down
            self.p_down = _gather_forward(gath, send, recv, [down_slot], after, "gather_down_forward")
            return self.p_down[3][0, 0]

        def w_down(self, after):
            fsend, frecv, gath, _ = self.p_down
            return _gather_finish(gath, fsend, frecv, [down_slot], after, "gather_down_finish")[0].reshape(1, fp, D)

        def small(self, small, loss):
            flat = jnp.concatenate([s.reshape(-1) for s in small] + [loss.reshape(-1)])
            rows = -(-flat.size // (8 * LANES)) * 8
            packed = jnp.pad(flat, (0, rows * LANES - flat.size)).reshape(rows, LANES)
            self.total = _all_reduce_small(packed, "reduce_small")
            return self.total

        def grad(self, name, dw):
            view_shape, view, block, tr, spec = grad_plan[name]
            send, recv_sems, dw, recv, token = _pair_start(dw.reshape(view_shape), view, block, core, "pair_start_" + name)
            self.in_flight[name] = (dw, recv, send, recv_sems)
            return token[0, 0]

        def grad_reduce(self, name, after):
            _, view, _, tr, spec = grad_plan[name]
            dw, recv = _pair_wait(*self.in_flight[name], view, after, "pair_wait_" + name)
            pair = _pair_add(core, dw, recv, tr, spec, "pair_add_" + name)
            send, recv_sems, pair, parts, token = _chip_start(pair, recv, "chip_start_" + name)
            self.in_flight[name] = (pair, parts, send, recv_sems)
            self.last_token = token
            return token[0, 0]

        def grad_parts(self, name, after):
            return _chip_wait(*self.in_flight[name], after, "chip_wait_" + name)

    exchanges = Exchanges()
    gains = (pre_mix_gain, post_mix_gain, pre_ffn_gain, post_ffn_gain, sb_out_gain, dil_out_gain)
    loss, grad_x, small = _local_step(xb, tb, gains, exchanges)

    sizes = [s.size for s in small]
    total = exchanges.total.reshape(-1)
    offs = [0]
    for s in sizes:
        offs.append(offs[-1] + s)
    red = [total[offs[k]:offs[k + 1]].reshape(small[k].shape) for k in range(len(small))]
    loss_out = total[offs[-1]]
    g_pre_mix, g_post_mix, g_pre_ffn, g_post_ffn, g_sb, g_dil, g_conv = red
    g_conv_b = g_conv[3].reshape(N_DEV, cup)[:, :cu].reshape(1, N_DEV * cu)
    g_conv_w = lax.dynamic_index_in_dim(g_conv[0:3].reshape(3, N_DEV, cup), me, axis=1, keepdims=False)[:, :cu]

    def small_adam(w, g, m, v, name):
        one = w.shape[0] == 1
        if one:
            w, g, m, v = (jnp.broadcast_to(t, (8, t.shape[1])) for t in (w, g, m, v))
        outs = _adamw(w, g[None], m, v, name)
        return [o[0:1] for o in outs] if one else outs

    chip_ids = jnp.stack([2 * px + py, 2 * (1 - px) + py, 2 * px + 1 - py, 2 * (1 - px) + 1 - py]).astype(jnp.int32)
    out_w_down = _adamw_chips(w_down, *exchanges.grad_parts("w_down", exchanges.last_token), chip_ids, m_w_down[0], v_w_down[0], "adam_w_down")
    out_w_up = _adamw_chips(w_up, *exchanges.grad_parts("w_up", out_w_down[1]), chip_ids, m_w_up[0], v_w_up[0], "adam_w_up")
    out_w_out = _adamw_chips(w_out, *exchanges.grad_parts("w_out", out_w_up[1]), chip_ids, m_w_out[0], v_w_out[0], "adam_w_out")
    out_w_in = _adamw_chips(w_in, *exchanges.grad_parts("w_in", out_w_out[1]), chip_ids, m_w_in[0], v_w_in[0], "adam_w_in")
    out_pre_mix = small_adam(pre_mix_gain, g_pre_mix, m_pre_mix_gain, v_pre_mix_gain, "adam_pre_mix")
    out_post_mix = small_adam(post_mix_gain, g_post_mix, m_post_mix_gain, v_post_mix_gain, "adam_post_mix")
    out_pre_ffn = small_adam(pre_ffn_gain, g_pre_ffn, m_pre_ffn_gain, v_pre_ffn_gain, "adam_pre_ffn")
    out_post_ffn = small_adam(post_ffn_gain, g_post_ffn, m_post_ffn_gain, v_post_ffn_gain, "adam_post_ffn")
    out_sb = small_adam(sb_out_gain, g_sb, m_sb_out_gain, v_sb_out_gain, "adam_sb_gain")
    out_dil = small_adam(dil_out_gain, g_dil, m_dil_out_gain, v_dil_out_gain, "adam_dil_gain")
    out_conv_b = small_adam(conv_b, g_conv_b, m_conv_b, v_conv_b, "adam_conv_b")
    cw8 = [jnp.pad(t, ((0, 5), (0, 0))) for t in (conv_w, g_conv_w, m_conv_w[0], v_conv_w[0])]
    out_conv_w = [o[0:3] for o in _adamw(cw8[0], cw8[1][None], cw8[2], cw8[3], "adam_conv_w")]

    order = [out_pre_mix, out_post_mix, out_pre_ffn, out_post_ffn, [o[None] for o in out_w_in], out_sb, out_dil,
             [o[None] for o in out_w_out], [o[None] for o in out_w_up], [o[None] for o in out_conv_w], out_conv_b,
             [o[None] for o in out_w_down]]
    outs = [loss_out, grad_x[None]]
    for k in range(4):
        outs += [o[k] for o in order]
    return tuple(outs)
```

```python
import functools
import math

import jax
import jax.numpy as jnp
from jax import lax
from jax.experimental import pallas as pl
from jax.experimental.pallas import tpu as pltpu

F32 = jnp.float32
BF16 = jnp.bfloat16
HEAD_DIM = 128
LANES = 128
KEY_BLOCK = 128
DILATIONS = (1, 4, 16)
RMS_EPS = 1e-6
ROPE_THETA = 10000.0
NEG = -1e30
ADAM_LR, ADAM_B1, ADAM_B2, ADAM_EPS, ADAM_WD, ADAM_STEP = 0.001, 0.9, 0.999, 1e-08, 0.01, 10
MESH = pl.DeviceIdType.MESH
N_DEV = 8
N_CHIP = 4
HBM = pl.BlockSpec(memory_space=pl.ANY)
VMEM_LIMIT = 56 * 1024 * 1024

_pcall = pl.pallas_call


def _tile(n, pref, mult=LANES):
    best = None
    t = mult
    while t <= min(n, pref):
        if n % t == 0:
            best = t
        t += mult
    return n if best is None else best


def _params(*sem):
    return pltpu.CompilerParams(dimension_semantics=sem, vmem_limit_bytes=VMEM_LIMIT)


def _dot(a, b, dims):
    return lax.dot_general(a, b, (dims, ((), ())), preferred_element_type=F32)


NN = ((1,), (0,))
NT = ((1,), (1,))
TN = ((0,), (0,))


def _mm_body(dims, nk, tile):
    if nk == 1:
        def single(a_ref, b_ref, o_ref):
            o_ref[...] = _dot(a_ref[...].astype(BF16), b_ref[...].astype(BF16), dims).astype(o_ref.dtype)

        return single, []

    def body(a_ref, b_ref, o_ref, acc_ref):
        k = pl.program_id(2)

        @pl.when(k == 0)
        def _():
            acc_ref[...] = jnp.zeros_like(acc_ref)

        acc_ref[...] += _dot(a_ref[...].astype(BF16), b_ref[...].astype(BF16), dims)

        @pl.when(k == nk - 1)
        def _():
            o_ref[...] = acc_ref[...].astype(o_ref.dtype)

    return body, [pltpu.VMEM(tile, F32)]


def _mm_nn(a, b3, out_dtype, name, tm=1024, tn=1408, tk=2048):
    M, K = a.shape
    C, _, n = b3.shape
    tm, tk, tn = _tile(M, tm, 8), _tile(K, tk), _tile(n, tn)
    npc, nk = n // tn, K // tk
    body, scratch = _mm_body(NN, nk, (tm, tn))
    return _pcall(
        body, grid=(M // tm, C * npc, nk),
        in_specs=[pl.BlockSpec((tm, tk), lambda i, j, k: (i, k)),
                  pl.BlockSpec((None, tk, tn), lambda i, j, k: (j // npc, k, j % npc))],
        out_specs=pl.BlockSpec((tm, tn), lambda i, j, k: (i, j)),
        out_shape=jax.ShapeDtypeStruct((M, C * n), out_dtype), scratch_shapes=scratch,
        compiler_params=_params("parallel", "parallel", "arbitrary"), name=name)(a, b3)


def _mm_nt(a, b3, out_dtype, name, tm=1024, tn=1024, tk=2048, after=None):
    M, _ = a.shape
    C, N, n = b3.shape
    tm, tn, tk = _tile(M, tm, 8), _tile(N, tn), _tile(n, tk)
    kpc = n // tk
    nk = C * kpc
    inner, scratch = _mm_body(NT, nk, (tm, tn))
    extra = [] if after is None else [after]

    def body(a_ref, b_ref, *rest):
        inner(a_ref, b_ref, *rest[len(extra):])

    return _pcall(
        body, grid=(M // tm, N // tn, nk),
        in_specs=[pl.BlockSpec((tm, tk), lambda i, j, k: (i, k)),
                  pl.BlockSpec((None, tn, tk), lambda i, j, k: (k // kpc, j, k % kpc))] + [HBM] * len(extra),
        out_specs=pl.BlockSpec((tm, tn), lambda i, j, k: (i, j)),
        out_shape=jax.ShapeDtypeStruct((M, N), out_dtype), scratch_shapes=scratch,
        compiler_params=_params("parallel", "parallel", "arbitrary"), name=name)(a, b3, *extra)


def _mm_tn(x, y, n, out_dtype, name, tm=1024, tn=1408, tk=2048, after=None):
    S, P = x.shape
    C = y.shape[1] // n
    tm, tn, tk = _tile(P, tm), _tile(n, tn), _tile(S, tk, 8)
    npc, nk = n // tn, S // tk
    inner, scratch = _mm_body(TN, nk, (tm, tn))
    extra = [] if after is None else [after]

    def body(x_ref, y_ref, *rest):
        inner(x_ref, y_ref, *rest[len(extra):])

    return _pcall(
        body, grid=(P // tm, C * npc, nk),
        in_specs=[pl.BlockSpec((tk, tm), lambda i, j, k: (k, i)),
                  pl.BlockSpec((tk, tn), lambda i, j, k: (k, j))] + [HBM] * len(extra),
        out_specs=pl.BlockSpec((None, tm, tn), lambda i, j, k: (j // npc, i, j % npc)),
        out_shape=jax.ShapeDtypeStruct((C, P, n), out_dtype), scratch_shapes=scratch,
        compiler_params=_params("parallel", "parallel", "arbitrary"), name=name)(x, y, *extra)


def _rms_scale(v):
    return lax.rsqrt(jnp.mean(v * v, axis=-1, keepdims=True) + RMS_EPS)


def _rms_bwd(gy, v, r):
    return r * gy - v * (r * r * r * jnp.mean(gy * v, axis=-1, keepdims=True))


def _rows_spec(tm, d):
    return pl.BlockSpec((tm, d), lambda i: (i, 0))


def _vec_spec(d):
    return pl.BlockSpec((1, d), lambda i: (0, 0))


def _rms_fwd(x, g, name, tm=256):
    S, D = x.shape

    def body(x_ref, g_ref, h_ref):
        v = x_ref[...]
        h_ref[...] = (v * _rms_scale(v) * g_ref[...]).astype(BF16)

    return _pcall(body, grid=(S // tm,), in_specs=[_rows_spec(tm, D), _vec_spec(D)], out_specs=_rows_spec(tm, D),
                  out_shape=jax.ShapeDtypeStruct((S, D), BF16), compiler_params=_params("parallel"), name=name)(x, g)


def _mid_fwd(x, mix, g_post, g_pre, name, tm=256):
    S, D = x.shape

    def body(x_ref, m_ref, gp_ref, gn_ref, x2_ref, h_ref):
        m = m_ref[...]
        x2 = x_ref[...] + m * _rms_scale(m) * gp_ref[...]
        x2_ref[...] = x2
        h_ref[...] = (x2 * _rms_scale(x2) * gn_ref[...]).astype(BF16)

    return _pcall(body, grid=(S // tm,), in_specs=[_rows_spec(tm, D), _rows_spec(tm, D), _vec_spec(D), _vec_spec(D)],
                  out_specs=[_rows_spec(tm, D), _rows_spec(tm, D)],
                  out_shape=[jax.ShapeDtypeStruct((S, D), F32), jax.ShapeDtypeStruct((S, D), BF16)],
                  compiler_params=_params("parallel"), name=name)(x, mix, g_post, g_pre)


def _loss_bwd(x2, f, tgt, g_post, name, tm=256):
    S, D = x2.shape

    def body(x2_ref, f_ref, t_ref, g_ref, dy_ref, df_ref, dg_ref, ls_ref):
        i = pl.program_id(0)

        @pl.when(i == 0)
        def _():
            dg_ref[...] = jnp.zeros_like(dg_ref)
            ls_ref[...] = jnp.zeros_like(ls_ref)

        fv = f_ref[...]
        r = _rms_scale(fv)
        g = g_ref[...]
        err = x2_ref[...] + fv * r * g - t_ref[...]
        ls_ref[...] += jnp.broadcast_to(0.5 * jnp.sum(jnp.mean(err * err, axis=-1, keepdims=True), axis=0, keepdims=True), ls_ref.shape)
        dy = err * (1.0 / D)
        dy_ref[...] = dy
        df_ref[...] = _rms_bwd(dy * g, fv, r).astype(BF16)
        dg_ref[...] += jnp.sum(dy * fv * r, axis=0, keepdims=True)

    return _pcall(body, grid=(S // tm,),
                  in_specs=[_rows_spec(tm, D), _rows_spec(tm, D), _rows_spec(tm, D), _vec_spec(D)],
                  out_specs=[_rows_spec(tm, D), _rows_spec(tm, D), _vec_spec(D), _vec_spec(LANES)],
                  out_shape=[jax.ShapeDtypeStruct((S, D), F32), jax.ShapeDtypeStruct((S, D), BF16),
                             jax.ShapeDtypeStruct((1, D), F32), jax.ShapeDtypeStruct((1, LANES), F32)],
                  compiler_params=_params("arbitrary"), name=name)(x2, f, tgt, g_post)


def _mid_bwd(dy, dh2, x2, mix, g_pre, g_post, name, tm=256):
    S, D = dy.shape

    def body(dy_ref, dh_ref, x2_ref, m_ref, gn_ref, gp_ref, dx2_ref, dm_ref, dgn_ref, dgp_ref):
        i = pl.program_id(0)

        @pl.when(i == 0)
        def _():
            dgn_ref[...] = jnp.zeros_like(dgn_ref)
            dgp_ref[...] = jnp.zeros_like(dgp_ref)

        x2, dh = x2_ref[...], dh_ref[...]
        r = _rms_scale(x2)
        dx2 = dy_ref[...] + _rms_bwd(dh * gn_ref[...], x2, r)
        dgn_ref[...] += jnp.sum(dh * x2 * r, axis=0, keepdims=True)
        dx2_ref[...] = dx2
        m = m_ref[...]
        rm = _rms_scale(m)
        dm_ref[...] = _rms_bwd(dx2 * gp_ref[...], m, rm).astype(BF16)
        dgp_ref[...] += jnp.sum(dx2 * m * rm, axis=0, keepdims=True)

    return _pcall(body, grid=(S // tm,),
                  in_specs=[_rows_spec(tm, D)] * 4 + [_vec_spec(D)] * 2,
                  out_specs=[_rows_spec(tm, D), _rows_spec(tm, D), _vec_spec(D), _vec_spec(D)],
                  out_shape=[jax.ShapeDtypeStruct((S, D), F32), jax.ShapeDtypeStruct((S, D), BF16),
                             jax.ShapeDtypeStruct((1, D), F32), jax.ShapeDtypeStruct((1, D), F32)],
                  compiler_params=_params("arbitrary"), name=name)(dy, dh2, x2, mix, g_pre, g_post)


def _first_bwd(dx2, dh1, x, g_pre, name, tm=256):
    S, D = x.shape

    def body(dx2_ref, dh_ref, x_ref, g_ref, gx_ref, dg_ref):
        i = pl.program_id(0)

        @pl.when(i == 0)
        def _():
            dg_ref[...] = jnp.zeros_like(dg_ref)

        xv, dh = x_ref[...], dh_ref[...]
        r = _rms_scale(xv)
        gx_ref[...] = dx2_ref[...] + _rms_bwd(dh * g_ref[...], xv, r)
        dg_ref[...] += jnp.sum(dh * xv * r, axis=0, keepdims=True)

    return _pcall(body, grid=(S // tm,), in_specs=[_rows_spec(tm, D)] * 3 + [_vec_spec(D)],
                  out_specs=[_rows_spec(tm, D), _vec_spec(D)],
                  out_shape=[jax.ShapeDtypeStruct((S, D), F32), jax.ShapeDtypeStruct((1, D), F32)],
                  compiler_params=_params("arbitrary"), name=name)(dx2, dh1, x, g_pre)


def _logsig_pair(z):
    lb = jnp.minimum(z, 0.0) - jnp.log(1.0 + jnp.exp(-jnp.abs(z)))
    return lb, lb - z


SB_KEY_BLOCK = 256


def _sum_matrix(strict):
    ia = lax.broadcasted_iota(jnp.int32, (SB_KEY_BLOCK, SB_KEY_BLOCK), 0)
    ib = lax.broadcasted_iota(jnp.int32, (SB_KEY_BLOCK, SB_KEY_BLOCK), 1)
    tri = (ia > ib) if strict == ">" else (ia < ib)
    return jnp.concatenate([tri.astype(BF16), jnp.ones((SB_KEY_BLOCK, LANES), BF16)], axis=1)


def _lanes(c, width):
    return jnp.tile(c, (1, width // LANES))


def _split_dot(v, u):
    hi = v.astype(BF16)
    lo = (v - hi.astype(F32)).astype(BF16)
    return _dot(hi, u, NN) + _dot(lo, u, NN)


def _head_out(o, g):
    return o * _rms_scale(o) * g


def _sb_fwd(proj, gain, n_heads, name, tq=1024):
    S = proj.shape[0]
    H, tk = n_heads, SB_KEY_BLOCK
    tq = _tile(S, tq, 2 * tk)
    scale = HEAD_DIM ** -0.5

    def body(q_ref, k_ref, v_ref, g_ref, o_ref, ct_ref, mx_ref, oacc, cacc):
        i = pl.program_id(1)
        oacc[...] = jnp.zeros_like(oacc)
        cacc[...] = jnp.zeros_like(cacc)
        sums = _sum_matrix(">")

        def block(k0, r0, diagonal):
            rows = pl.ds(r0, tq - r0)
            q = q_ref[rows, :].astype(BF16)
            kj = k_ref[pl.ds(k0, tk), :].astype(BF16)
            vj = v_ref[pl.ds(k0, tk), :].astype(BF16)
            lb, lk = _logsig_pair(_dot(q, kj, NT) * scale)
            if diagonal:
                causal = (lax.broadcasted_iota(jnp.int32, (tq - r0, tk), 1) < lax.broadcasted_iota(jnp.int32, (tq - r0, tk), 0))
                lk = jnp.where(causal, lk, 0.0)
            both = _split_dot(lk, sums)
            c = cacc[rows, :]
            a = jnp.exp(lb + both[:, :tk] + _lanes(c, tk))
            if diagonal:
                a = jnp.where(causal, a, 0.0)
            oacc[rows, :] += _dot(a.astype(BF16), vj, NN)
            cacc[rows, :] = c + both[:, tk:]

        for d in reversed(range(tq // tk)):
            block(pl.multiple_of(i * tq + d * tk, tk), d * tk, True)
        n_pairs = i * (tq // tk // 2)

        def step(it, carry):
            k0 = pl.multiple_of((n_pairs - 1 - it) * 2 * tk, 2 * tk)
            block(pl.multiple_of(k0 + tk, tk), 0, False)
            block(k0, 0, False)
            return carry

        lax.fori_loop(0, n_pairs, step, 0)
        o = oacc[...]
        o_ref[...] = o
        ct_ref[...] = cacc[...]
        mx_ref[...] = _head_out(o, g_ref[...]).astype(BF16)

    blk = pl.BlockSpec((tq, HEAD_DIM), lambda h, i: (i, h))
    return _pcall(
        body, grid=(H, S // tq),
        in_specs=[blk, pl.BlockSpec((S, HEAD_DIM), lambda h, i: (0, H + h)),
                  pl.BlockSpec((S, HEAD_DIM), lambda h, i: (0, 2 * H + h)), pl.BlockSpec((1, HEAD_DIM), lambda h, i: (0, h))],
        out_specs=[blk, blk, blk],
        out_shape=[jax.ShapeDtypeStruct((S, H * HEAD_DIM), F32), jax.ShapeDtypeStruct((S, H * HEAD_DIM), F32),
                   jax.ShapeDtypeStruct((S, H * HEAD_DIM), BF16)],
        scratch_shapes=[pltpu.VMEM((tq, HEAD_DIM), F32), pltpu.VMEM((tq, LANES), F32)],
        compiler_params=_params("parallel", "arbitrary"), name=name)(proj, proj, proj, gain)


def _sb_bwd(proj, gain, o_raw, ctot, dmixed, dm_col0, n_heads, name, tq=1024):
    S = proj.shape[0]
    H, tk = n_heads, SB_KEY_BLOCK
    tq = _tile(S, tq, 2 * tk)
    nq = S // tq
    scale = HEAD_DIM ** -0.5

    def body(q_ref, k_ref, v_ref, g_ref, o_ref, ct_ref, dm_ref, dq_ref, dk_ref, dv_ref, dg_ref,
             dkacc, dvacc, dqacc, pfx, gcar, dos):
        i = pl.program_id(1)

        @pl.when(i == 0)
        def _():
            dkacc[...] = jnp.zeros_like(dkacc)
            dvacc[...] = jnp.zeros_like(dvacc)
            dg_ref[...] = jnp.zeros_like(dg_ref)

        o, dm, g = o_ref[...], dm_ref[...], g_ref[...]
        r = _rms_scale(o)
        dos[...] = _rms_bwd(dm * g, o, r).astype(BF16)
        dg_ref[...] += jnp.broadcast_to(jnp.sum(dm * o * r, axis=0, keepdims=True), dg_ref.shape)
        dqacc[...] = jnp.zeros_like(dqacc)
        pfx[...] = jnp.zeros_like(pfx)
        gcar[...] = jnp.zeros_like(gcar)
        later, earlier = _sum_matrix(">"), _sum_matrix("<")

        def block(k0, r0, diagonal):
            rows = pl.ds(r0, tq - r0)
            keys = pl.ds(k0, tk)
            q, do = q_ref[rows, :].astype(BF16), dos[rows, :]
            kj, vj = k_ref[keys, :].astype(BF16), v_ref[keys, :].astype(BF16)
            lb, lk = _logsig_pair(_dot(q, kj, NT) * scale)
            if diagonal:
                causal = (lax.broadcasted_iota(jnp.int32, (tq - r0, tk), 1) < lax.broadcasted_iota(jnp.int32, (tq - r0, tk), 0))
                lk = jnp.where(causal, lk, 0.0)
            both = _split_dot(lk, later)
            p = pfx[rows, :] + both[:, tk:]
            a = jnp.exp(lb + both[:, :tk] + _lanes(ct_ref[rows, :] - p, tk))
            if diagonal:
                a = jnp.where(causal, a, 0.0)
            dl = _dot(do, vj, NT) * a
            dvacc[keys, :] += _dot(a.astype(BF16), do, TN)
            both = _dot(dl.astype(BF16), earlier, NN)
            gc = gcar[rows, :]
            sig = jnp.exp(lb)
            gsum = (both[:, :tk] + _lanes(gc, tk)) * sig
            if diagonal:
                gsum = jnp.where(causal, gsum, 0.0)
            dz = ((dl * (1.0 - sig) - gsum) * scale).astype(BF16)
            dqacc[rows, :] += _dot(dz, kj, NN)
            dkacc[keys, :] += _dot(dz, q, TN)
            pfx[rows, :] = p
            gcar[rows, :] = gc + both[:, tk:]

        def step(j, carry):
            k0 = pl.multiple_of(j * 2 * tk, 2 * tk)
            block(k0, 0, False)
            block(pl.multiple_of(k0 + tk, tk), 0, False)
            return carry

        lax.fori_loop(0, i * (tq // tk // 2), step, 0)
        for d in range(tq // tk):
            block(pl.multiple_of(i * tq + d * tk, tk), d * tk, True)
        dq_ref[...] = dqacc[...].astype(BF16)

        @pl.when(i == nq - 1)
        def _():
            dk_ref[...] = dkacc[...].astype(BF16)
            dv_ref[...] = dvacc[...].astype(BF16)

    blk = pl.BlockSpec((tq, HEAD_DIM), lambda h, i: (i, h))
    full = pl.BlockSpec((S, HEAD_DIM), lambda h, i: (0, h))
    W = H * HEAD_DIM
    return _pcall(
        body, grid=(H, nq),
        in_specs=[blk, pl.BlockSpec((S, HEAD_DIM), lambda h, i: (0, H + h)),
                  pl.BlockSpec((S, HEAD_DIM), lambda h, i: (0, 2 * H + h)), pl.BlockSpec((1, HEAD_DIM), lambda h, i: (0, h)),
                  blk, blk, pl.BlockSpec((tq, HEAD_DIM), lambda h, i: (i, dm_col0 + h))],
        out_specs=[blk, full, full, pl.BlockSpec((8, HEAD_DIM), lambda h, i: (0, h))],
        out_shape=[jax.ShapeDtypeStruct((S, W), BF16), jax.ShapeDtypeStruct((S, W), BF16),
                   jax.ShapeDtypeStruct((S, W), BF16), jax.ShapeDtypeStruct((8, W), F32)],
        scratch_shapes=[pltpu.VMEM((S, HEAD_DIM), F32), pltpu.VMEM((S, HEAD_DIM), F32), pltpu.VMEM((tq, HEAD_DIM), F32),
                        pltpu.VMEM((tq, LANES), F32), pltpu.VMEM((tq, LANES), F32), pltpu.VMEM((tq, HEAD_DIM), BF16)],
        compiler_params=_params("arbitrary", "arbitrary"), name=name)(proj, proj, proj, gain, o_raw, ctot, dmixed)


def _rope_tables(S):
    inv_freq = ROPE_THETA ** (-jnp.arange(0, HEAD_DIM, 2, dtype=F32) / HEAD_DIM)
    ang = jnp.arange(S, dtype=F32)[:, None] * inv_freq[None, :]
    cos, sin = jnp.cos(ang), jnp.sin(ang)
    return jnp.concatenate([cos, cos], axis=1), jnp.concatenate([-sin, sin], axis=1)


def _rope(v, cos2, sin_signed):
    return v * cos2 + pltpu.roll(v, HEAD_DIM // 2, axis=1) * sin_signed


def _dil_rows(d, r, l0, n):
    if d == 1:
        return pl.ds(l0 if isinstance(l0, int) else pl.multiple_of(l0, KEY_BLOCK), n)
    return pl.ds(r + d * l0, n, stride=d)


def _dil_blocks(S, visit):
    B = KEY_BLOCK
    group = 4
    for b, d in enumerate(DILATIONS):
        nb = S // d // B
        if nb == 1:
            g = math.gcd(d, group)

            def trip(t, carry, b=b, d=d, g=g):
                for u in range(g):
                    visit(b, d, t * g + u, 0, True)
                return carry

            lax.fori_loop(0, d // g, trip, 0)
        elif d == 1:
            visit(b, d, 0, 0, True)
            g = max(k for k in range(1, group + 2) if (nb - 1) % k == 0)

            def trip(t, carry, b=b, d=d, g=g):
                for u in range(g):
                    visit(b, d, 0, (1 + t * g + u) * B, False)
                return carry

            lax.fori_loop(0, (nb - 1) // g, trip, 0)
        else:
            def trip(r, carry, b=b, d=d, nb=nb):
                visit(b, d, r, 0, True)
                for n in range(1, nb):
                    visit(b, d, r, n * B, False)
                return carry

            lax.fori_loop(0, d, trip, 0)


def _dil_mask(first):
    B = KEY_BLOCK
    nk = B if first else 2 * B
    iq = lax.broadcasted_iota(jnp.int32, (B, nk), 0)
    ik = lax.broadcasted_iota(jnp.int32, (B, nk), 1)
    return (ik <= iq) if first else ((ik >= iq) & (ik <= iq + B))


def _dil_fwd(proj, cos2, sin_signed, gain, col0, n_heads, name):
    S = proj.shape[0]
    H, B = n_heads, KEY_BLOCK
    scale = HEAD_DIM ** -0.5
    rc = _tile(S, 256, 8)

    def body(q_ref, k_ref, v_ref, c_ref, s_ref, g_ref, o_ref, l_ref, mx_ref, qr, kr, *per_branch):
        ob, lb = per_branch[:len(DILATIONS)], per_branch[len(DILATIONS):]

        def rope_rows(t, carry):
            rows = pl.ds(pl.multiple_of(t * rc, rc), rc)
            qr[rows, :] = _rope(q_ref[rows, :], c_ref[rows, :], s_ref[rows, :])
            kr[rows, :] = _rope(k_ref[rows, :], c_ref[rows, :], s_ref[rows, :])
            return carry

        lax.fori_loop(0, S // rc, rope_rows, 0)

        def visit(b, d, r, l0, first):
            nk = B if first else 2 * B
            qrows = _dil_rows(d, r, l0, B)
            krows = qrows if first else _dil_rows(d, r, l0 - B, nk)
            s = _dot(qr[qrows, :].astype(BF16), kr[krows, :].astype(BF16), NT) * scale
            s = jnp.where(_dil_mask(first), s, NEG)
            m = jnp.max(s, axis=1, keepdims=True)
            p = jnp.exp(s - m)
            den = jnp.sum(p, axis=1, keepdims=True)
            ob[b][qrows, :] = _dot(p.astype(BF16), v_ref[krows, :].astype(BF16), NN) / den
            lb[b][qrows, :] = jnp.broadcast_to(m + jnp.log(den), (B, LANES))

        _dil_blocks(S, visit)

        def combine(t, carry):
            rows = pl.ds(pl.multiple_of(t * rc, rc), rc)
            l0, l1, l2 = lb[0][rows, :], lb[1][rows, :], lb[2][rows, :]
            m = jnp.maximum(jnp.maximum(l0, l1), l2)
            w0, w1, w2 = jnp.exp(l0 - m), jnp.exp(l1 - m), jnp.exp(l2 - m)
            den = w0 + w1 + w2
            o = (w0 * ob[0][rows, :] + w1 * ob[1][rows, :] + w2 * ob[2][rows, :]) / den
            o_ref[rows, :] = o
            l_ref[rows, :] = m + jnp.log(den)
            mx_ref[rows, :] = _head_out(o, g_ref[...]).astype(BF16)
            return carry

        lax.fori_loop(0, S // rc, combine, 0)

    def col(k):
        return pl.BlockSpec((S, HEAD_DIM), lambda h: (0, col0 + k * H + h))

    tab = pl.BlockSpec((S, HEAD_DIM), lambda h: (0, 0))
    out = pl.BlockSpec((S, HEAD_DIM), lambda h: (0, h))
    W = H * HEAD_DIM
    return _pcall(
        body, grid=(H,),
        in_specs=[col(0), col(1), col(2), tab, tab, pl.BlockSpec((1, HEAD_DIM), lambda h: (0, h))],
        out_specs=[out, out, out],
        out_shape=[jax.ShapeDtypeStruct((S, W), F32), jax.ShapeDtypeStruct((S, W), F32), jax.ShapeDtypeStruct((S, W), BF16)],
        scratch_shapes=[pltpu.VMEM((S, HEAD_DIM), F32)] * (2 + 2 * len(DILATIONS)),
        compiler_params=_params("parallel"), name=name)(proj, proj, proj, cos2, sin_signed, gain)


def _dil_bwd(proj, cos2, sin_signed, gain, o_raw, lse, dmixed, dm_col0, col0, n_heads, name):
    S = proj.shape[0]
    H, B = n_heads, KEY_BLOCK
    scale = HEAD_DIM ** -0.5
    rc = _tile(S, 256, 8)

    def body(q_ref, k_ref, v_ref, c_ref, s_ref, g_ref, o_ref, l_ref, dm_ref, dq_ref, dk_ref, dv_ref, dg_ref,
             qr, kr, dos, dsum, dqr, dkr, dvv):
        dg_ref[...] = jnp.zeros_like(dg_ref)

        def prep(t, carry):
            rows = pl.ds(pl.multiple_of(t * rc, rc), rc)
            qr[rows, :] = _rope(q_ref[rows, :], c_ref[rows, :], s_ref[rows, :])
            kr[rows, :] = _rope(k_ref[rows, :], c_ref[rows, :], s_ref[rows, :])
            o, dm = o_ref[rows, :], dm_ref[rows, :]
            r = _rms_scale(o)
            do = _rms_bwd(dm * g_ref[...], o, r)
            dg_ref[...] += jnp.broadcast_to(jnp.sum(dm * o * r, axis=0, keepdims=True), dg_ref.shape)
            dos[rows, :] = do
            dsum[rows, :] = jnp.broadcast_to(jnp.sum(do * o, axis=1, keepdims=True), (rc, LANES))
            dqr[rows, :] = jnp.zeros((rc, HEAD_DIM), F32)
            dkr[rows, :] = jnp.zeros((rc, HEAD_DIM), F32)
            dvv[rows, :] = jnp.zeros((rc, HEAD_DIM), F32)
            return carry

        lax.fori_loop(0, S // rc, prep, 0)

        def visit(b, d, r, l0, first):
            nk = B if first else 2 * B
            qrows = _dil_rows(d, r, l0, B)
            krows = qrows if first else _dil_rows(d, r, l0 - B, nk)
            qs, ks = qr[qrows, :].astype(BF16), kr[krows, :].astype(BF16)
            do = dos[qrows, :].astype(BF16)
            s = _dot(qs, ks, NT) * scale
            s = jnp.where(_dil_mask(first), s, NEG)
            p = jnp.exp(s - l_ref[qrows, :][:, 0:1])
            dp = _dot(do, v_ref[krows, :].astype(BF16), NT)
            ds = (p * (dp - dsum[qrows, :][:, 0:1]) * scale).astype(BF16)
            dqr[qrows, :] += _dot(ds, ks, NN)
            dkr[krows, :] += _dot(ds, qs, TN)
            dvv[krows, :] += _dot(p.astype(BF16), do, TN)

        _dil_blocks(S, visit)

        def finish(t, carry):
            rows = pl.ds(pl.multiple_of(t * rc, rc), rc)
            c, s = c_ref[rows, :], s_ref[rows, :]
            dq, dk = dqr[rows, :], dkr[rows, :]
            dq_ref[rows, :] = (dq * c + pltpu.roll(dq * s, HEAD_DIM // 2, axis=1)).astype(BF16)
            dk_ref[rows, :] = (dk * c + pltpu.roll(dk * s, HEAD_DIM // 2, axis=1)).astype(BF16)
            dv_ref[rows, :] = dvv[rows, :].astype(BF16)
            return carry

        lax.fori_loop(0, S // rc, finish, 0)

    def col(k):
        return pl.BlockSpec((S, HEAD_DIM), lambda h: (0, col0 + k * H + h))

    tab = pl.BlockSpec((S, HEAD_DIM), lambda h: (0, 0))
    out = pl.BlockSpec((S, HEAD_DIM), lambda h: (0, h))
    W = H * HEAD_DIM
    big = pltpu.VMEM((S, HEAD_DIM), F32)
    return _pcall(
        body, grid=(H,),
        in_specs=[col(0), col(1), col(2), tab, tab, pl.BlockSpec((1, HEAD_DIM), lambda h: (0, h)), out, out,
                  pl.BlockSpec((S, HEAD_DIM), lambda h: (0, dm_col0 + h))],
        out_specs=[out, out, out, pl.BlockSpec((8, HEAD_DIM), lambda h: (0, h))],
        out_shape=[jax.ShapeDtypeStruct((S, W), BF16), jax.ShapeDtypeStruct((S, W), BF16),
                   jax.ShapeDtypeStruct((S, W), BF16), jax.ShapeDtypeStruct((8, W), F32)],
        scratch_shapes=[big, big, big, pltpu.VMEM((S, LANES), F32), big, big, big],
        compiler_params=_params("parallel"), name=name)(proj, proj, proj, cos2, sin_signed, gain, o_raw, lse, dmixed)


GELU_C = math.sqrt(2.0 / math.pi)
GELU_A = 0.044715
HALO = 16


def _shift_down(cur, halo, k):
    out = pltpu.roll(cur, k, axis=0)
    row = lax.broadcasted_iota(jnp.int32, cur.shape, 0)
    for t in range(k):
        out = jnp.where(row == t, halo[HALO - k + t:HALO - k + t + 1, :], out)
    return out


def _shift_up(cur, halo, k):
    n = cur.shape[0]
    out = pltpu.roll(cur, n - k, axis=0)
    row = lax.broadcasted_iota(jnp.int32, cur.shape, 0)
    for t in range(k):
        out = jnp.where(row == n - k + t, halo[t:t + 1, :], out)
    return out


def _conv3(cur, halo, cw):
    return _shift_down(cur, halo, 2) * cw[0:1, :] + _shift_down(cur, halo, 1) * cw[1:2, :] + cur * cw[2:3, :] + cw[3:4, :]


def _gelu_parts(x):
    t = jnp.tanh(GELU_C * (x + GELU_A * x * x * x))
    return 0.5 * x * (1.0 + t), t


def _geglu_specs(tm, tn, ncb):
    hb = tm // HALO

    def cur(off):
        return pl.BlockSpec((tm, tn), lambda j, i: (i, off + j))

    def prev(off):
        return pl.BlockSpec((HALO, tn), lambda j, i: (jnp.maximum(i * hb - 1, 0), off + j))

    def taps(off):
        return pl.BlockSpec((8, tn), lambda j, i: (0, off + j))

    return [cur(0), prev(0), cur(ncb), prev(ncb), taps(0), taps(ncb)]


def _geglu_fwd(u, cwb, name, tm=256, tn=1408):
    S, F2 = u.shape
    F = F2 // 2
    tm, tn = _tile(S, tm, HALO), _tile(F, tn)
    ncb = F // tn

    def body(g_ref, gp_ref, v_ref, vp_ref, cg_ref, cv_ref, y_ref):
        top = pl.program_id(1) > 0
        gp = jnp.where(top, gp_ref[...].astype(F32), 0.0)
        vp = jnp.where(top, vp_ref[...].astype(F32), 0.0)
        gc = _conv3(g_ref[...].astype(F32), gp, cg_ref[...])
        vc = _conv3(v_ref[...].astype(F32), vp, cv_ref[...])
        y_ref[...] = (_gelu_parts(gc)[0] * vc).astype(BF16)

    return _pcall(body, grid=(ncb, S // tm), in_specs=_geglu_specs(tm, tn, ncb),
                  out_specs=pl.BlockSpec((tm, tn), lambda j, i: (i, j)),
                  out_shape=jax.ShapeDtypeStruct((S, F), BF16),
                  compiler_params=_params("parallel", "parallel"), name=name)(u, u, u, u, cwb, cwb)


def _geglu_bwd(u, dy, cwb, name, tm=256, tn=512):
    S, F2 = u.shape
    F = F2 // 2
    tm, tn = _tile(S, tm, HALO), _tile(F, tn)
    ncb = F // tn

    def body(g_ref, gp_ref, v_ref, vp_ref, cg_ref, cv_ref, dy_ref, dc_ref, dwg_ref, dwv_ref):
        i = pl.program_id(1)

        @pl.when(i == 0)
        def _():
            dwg_ref[...] = jnp.zeros_like(dwg_ref)
            dwv_ref[...] = jnp.zeros_like(dwv_ref)

        top = i > 0
        g, v = g_ref[...].astype(F32), v_ref[...].astype(F32)
        gp = jnp.where(top, gp_ref[...].astype(F32), 0.0)
        vp = jnp.where(top, vp_ref[...].astype(F32), 0.0)
        gc = _conv3(g, gp, cg_ref[...])
        vc = _conv3(v, vp, cv_ref[...])
        act, t = _gelu_parts(gc)
        dact = 0.5 * (1.0 + t) + 0.5 * gc * (1.0 - t * t) * GELU_C * (1.0 + 3.0 * GELU_A * gc * gc)
        dyv = dy_ref[...].astype(F32)
        dgc = dyv * vc * dact
        dvc = dyv * act
        dc_ref[0] = dgc.astype(BF16)
        dc_ref[1] = dvc.astype(BF16)

        def taps(out_ref, dc, cur, halo):
            out_ref[0:1, :] += jnp.sum(dc * _shift_down(cur, halo, 2), axis=0, keepdims=True)
            out_ref[1:2, :] += jnp.sum(dc * _shift_down(cur, halo, 1), axis=0, keepdims=True)
            out_ref[2:3, :] += jnp.sum(dc * cur, axis=0, keepdims=True)
            out_ref[3:4, :] += jnp.sum(dc, axis=0, keepdims=True)

        taps(dwg_ref, dgc, g, gp)
        taps(dwv_ref, dvc, v, vp)

    return _pcall(body, grid=(ncb, S // tm),
                  in_specs=_geglu_specs(tm, tn, ncb) + [pl.BlockSpec((tm, tn), lambda j, i: (i, j))],
                  out_specs=[pl.BlockSpec((2, tm, tn), lambda j, i: (0, i, j)),
                             pl.BlockSpec((8, tn), lambda j, i: (0, j)), pl.BlockSpec((8, tn), lambda j, i: (0, j))],
                  out_shape=[jax.ShapeDtypeStruct((2, S, F), BF16), jax.ShapeDtypeStruct((8, F), F32),
                             jax.ShapeDtypeStruct((8, F), F32)],
                  compiler_params=_params("parallel", "arbitrary"), name=name)(u, u, u, u, cwb, cwb, dy)


def _conv_bwd(dc, cwb, name, tm=512, tn=1408):
    _, S, F = dc.shape
    tm, tn = _tile(S, tm, HALO), _tile(F, tn)
    ncb, nrb = F // tn, S // tm
    hb = tm // HALO

    def body(c_ref, n_ref, w_ref, du_ref):
        cur = c_ref[...].astype(F32)
        nxt = jnp.where(pl.program_id(2) < nrb - 1, n_ref[...].astype(F32), 0.0)
        w = w_ref[...]
        du = cur * w[2:3, :] + _shift_up(cur, nxt, 1) * w[1:2, :] + _shift_up(cur, nxt, 2) * w[0:1, :]
        du_ref[...] = du.astype(BF16)

    return _pcall(body, grid=(2, ncb, nrb),
                  in_specs=[pl.BlockSpec((None, tm, tn), lambda c, j, i: (c, i, j)),
                            pl.BlockSpec((None, HALO, tn), lambda c, j, i: (c, jnp.minimum((i + 1) * hb, S // HALO - 1), j)),
                            pl.BlockSpec((8, tn), lambda c, j, i: (0, c * ncb + j))],
                  out_specs=pl.BlockSpec((tm, tn), lambda c, j, i: (i, c * ncb + j)),
                  out_shape=jax.ShapeDtypeStruct((S, 2 * F), BF16),
                  compiler_params=_params("parallel", "parallel", "parallel"), name=name)(dc, dc, cwb)


def _adam_math(w, g, m, v):
    m = ADAM_B1 * m + (1.0 - ADAM_B1) * g
    v = ADAM_B2 * v + (1.0 - ADAM_B2) * (g * g)
    m_hat = m / (1.0 - ADAM_B1 ** ADAM_STEP)
    v_hat = v / (1.0 - ADAM_B2 ** ADAM_STEP)
    return -ADAM_LR * (m_hat / (jnp.sqrt(v_hat) + ADAM_EPS) + ADAM_WD * w), m, v


def _adamw(w, parts, m, v, name, tr=256):
    R, C = w.shape
    n, _, Cp = parts.shape
    tr = _tile(R, tr, 8)

    def body(w_ref, p_ref, m_ref, v_ref, g_out, d_out, m_out, v_out):
        g = p_ref[0, :, 0:C].astype(F32)
        for k in range(1, n):
            g = g + p_ref[k, :, 0:C].astype(F32)
        d, mn, vn = _adam_math(w_ref[...], g, m_ref[...], v_ref[...])
        g_out[...] = g
        d_out[...] = d
        m_out[...] = mn
        v_out[...] = vn

    spec = pl.BlockSpec((tr, C), lambda i: (i, 0))
    shape = jax.ShapeDtypeStruct((R, C), F32)
    return _pcall(body, grid=(R // tr,), in_specs=[spec, pl.BlockSpec((n, tr, Cp), lambda i: (0, i, 0)), spec, spec],
                  out_specs=[spec] * 4, out_shape=[shape] * 4, compiler_params=_params("parallel"), name=name)(w, parts, m, v)


def _adamw_chips(w, pair, parts, chip_ids, m, v, name, tr=256):
    R, C = w.shape
    Cp = pair.shape[2]
    tr = _tile(R, tr, 16)

    def body(ids_ref, w_ref, own_ref, p1_ref, p2_ref, p3_ref, m_ref, v_ref, g_out, d_out, m_out, v_out):
        g = own_ref[:, 0:C].astype(F32)
        for ref in (p1_ref, p2_ref, p3_ref):
            g = g + ref[:, 0:C].astype(F32)
        d, mn, vn = _adam_math(w_ref[...], g, m_ref[...], v_ref[...])
        g_out[...] = g
        d_out[...] = d
        m_out[...] = mn
        v_out[...] = vn

    spec = pl.BlockSpec((tr, C), lambda i, ids: (i, 0))

    def chip(k):
        return pl.BlockSpec((None, tr, Cp), lambda i, ids: (ids[k], i, 0))

    shape = jax.ShapeDtypeStruct((R, C), F32)
    grid_spec = pltpu.PrefetchScalarGridSpec(
        num_scalar_prefetch=1, grid=(R // tr,), in_specs=[spec, chip(0), chip(1), chip(2), chip(3), spec, spec],
        out_specs=[spec] * 4)
    return _pcall(body, grid_spec=grid_spec, out_shape=[shape] * 4, compiler_params=_params("parallel"),
                  name=name)(chip_ids, w, pair, parts, parts, parts, m, v)


def _place():
    return lax.axis_index("x"), lax.axis_index("y"), lax.axis_index("c")


def _other_chips(x, y):
    return [(1 - x, y), (x, 1 - y), (1 - x, 1 - y)]


IN_HBM = pl.BlockSpec(memory_space=pltpu.HBM)
SEM = pl.BlockSpec(memory_space=pltpu.SEMAPHORE)
EFFECT = pltpu.SideEffectType.DATAFLOW_SIDE_EFFECTING
TOKEN = jax.ShapeDtypeStruct((8, LANES), F32)
TOKEN_SPEC = pl.BlockSpec(memory_space=pltpu.VMEM)


def _in_hbm(a):
    return pltpu.with_memory_space_constraint(a, pltpu.HBM)


def _landing(shape):
    return _in_hbm(lax.empty(shape.shape, shape.dtype))


def _hbm_like(a):
    return pltpu.HBM(a.shape, a.dtype)


def _gather_start(landing, slots, after, name):
    na = len(landing)

    def body(*refs):
        land = refs[:na]
        send_sems, recv_sems = refs[na + 1], refs[na + 2]
        token = refs[-1]
        x, y, c = _place()
        for a in range(na):
            own = slots[a](land[a], x, y, c)
            for k, to in enumerate([(x, y, 1 - c)] + [(*chip, c) for chip in _other_chips(x, y)]):
                pltpu.make_async_remote_copy(
                    src_ref=own, dst_ref=own, send_sem=send_sems.at[4 * a + k],
                    recv_sem=recv_sems.at[4 * a + k], device_id=to, device_id_type=MESH).start()
        token[...] = jnp.zeros_like(token)

    sems = pltpu.SemaphoreType.DMA((4 * na,))
    outs = _pcall(
        body, in_specs=[IN_HBM] * na + [HBM],
        out_specs=[SEM, SEM] + [IN_HBM] * na + [TOKEN_SPEC],
        out_shape=[sems, sems] + [_hbm_like(s) for s in landing] + [TOKEN],
        input_output_aliases={a: 2 + a for a in range(na)},
        compiler_params=pltpu.CompilerParams(has_side_effects=EFFECT), name=name,
    )(*[_in_hbm(s) for s in landing], after)
    return outs[0], outs[1], outs[2:2 + na], outs[-1]


def _gather_forward(gathered, send_sems, recv_sems, slots, after, name):
    na = len(gathered)

    def body(*refs):
        gath = refs[:na]
        send1, recv1 = refs[na], refs[na + 1]
        fsend, frecv = refs[na + 3], refs[na + 4]
        token = refs[-1]
        x, y, c = _place()
        chips = _other_chips(x, y)
        for a in range(na):
            for k, peer in enumerate([(x, y, 1 - c)] + [(*chip, c) for chip in chips]):
                arrival = pltpu.make_async_remote_copy(
                    src_ref=slots[a](gath[a], x, y, c), dst_ref=slots[a](gath[a], *peer), send_sem=send1.at[4 * a + k],
                    recv_sem=recv1.at[4 * a + k], device_id=peer, device_id_type=MESH)
                arrival.wait_send()
                arrival.wait_recv()
        for a in range(na):
            for j, chip in enumerate(chips):
                view = slots[a](gath[a], *chip, c)
                pltpu.make_async_remote_copy(
                    src_ref=view, dst_ref=view, send_sem=fsend.at[3 * a + j], recv_sem=frecv.at[3 * a + j],
                    device_id=(x, y, 1 - c), device_id_type=MESH).start()
        token[...] = jnp.zeros_like(token)

    sems = pltpu.SemaphoreType.DMA((3 * na,))
    outs = _pcall(
        body, in_specs=[IN_HBM] * na + [SEM, SEM, HBM],
        out_specs=[SEM, SEM] + [IN_HBM] * na + [TOKEN_SPEC],
        out_shape=[sems, sems] + [_hbm_like(g) for g in gathered] + [TOKEN],
        input_output_aliases={a: 2 + a for a in range(na)},
        compiler_params=pltpu.CompilerParams(has_side_effects=EFFECT), name=name,
    )(*gathered, send_sems, recv_sems, after)
    return outs[0], outs[1], outs[2:2 + na], outs[-1]


def _gather_finish(gathered, fsend, frecv, slots, after, name):
    na = len(gathered)

    def body(*refs):
        gath, fs, fr = refs[:na], refs[na], refs[na + 1]
        x, y, c = _place()
        for a in range(na):
            for j, chip in enumerate(_other_chips(x, y)):
                passed = pltpu.make_async_remote_copy(
                    src_ref=slots[a](gath[a], *chip, c), dst_ref=slots[a](gath[a], *chip, 1 - c),
                    send_sem=fs.at[3 * a + j], recv_sem=fr.at[3 * a + j], device_id=(x, y, 1 - c), device_id_type=MESH)
                passed.wait_send()
                passed.wait_recv()

    outs = _pcall(
        body, in_specs=[IN_HBM] * na + [SEM, SEM, HBM], out_specs=[IN_HBM] * na,
        out_shape=[_hbm_like(g) for g in gathered], input_output_aliases={a: a for a in range(na)},
        compiler_params=pltpu.CompilerParams(has_side_effects=EFFECT), name=name,
    )(*gathered, fsend, frecv, after)
    return list(outs)


def _pair_copy(view, src, land, send_sems, recv_sems, chip):
    x, y, c = _place()
    return pltpu.make_async_remote_copy(
        src_ref=view(src, chip, 1 - c), dst_ref=land.at[chip], send_sem=send_sems.at[chip], recv_sem=recv_sems.at[chip],
        device_id=(x, y, 1 - c), device_id_type=MESH)


def _pair_start(grad, view, block, after, name):
    def body(src, land, after_ref, send_sems, recv_sems, src_thru, land_thru, token):
        for chip in range(N_CHIP):
            _pair_copy(view, src, land, send_sems, recv_sems, chip).start()
        token[...] = jnp.zeros_like(token)

    sems = pltpu.SemaphoreType.DMA((N_CHIP,))
    land = jax.ShapeDtypeStruct((N_CHIP, *block), BF16)
    return _pcall(
        body, in_specs=[IN_HBM, IN_HBM, HBM], out_specs=[SEM, SEM, IN_HBM, IN_HBM, TOKEN_SPEC],
        out_shape=[sems, sems, _hbm_like(grad), _hbm_like(land), TOKEN], input_output_aliases={0: 2, 1: 3},
        compiler_params=pltpu.CompilerParams(has_side_effects=EFFECT), name=name,
    )(_in_hbm(grad), _landing(land), after)


def _pair_wait(grad, recv, send_sems, recv_sems, view, after, name):
    def body(src, land, send, recv_s, after_ref, src_thru, land_thru):
        for chip in range(N_CHIP):
            copy = _pair_copy(view, src, land, send, recv_s, chip)
            copy.wait_send()
            copy.wait_recv()

    return _pcall(
        body, in_specs=[IN_HBM, IN_HBM, SEM, SEM, HBM], out_specs=[IN_HBM, IN_HBM],
        out_shape=[_hbm_like(grad), _hbm_like(recv)], input_output_aliases={0: 0, 1: 1},
        compiler_params=pltpu.CompilerParams(has_side_effects=EFFECT), name=name,
    )(grad, recv, send_sems, recv_sems, after)


def _chip_start(pair, after, name):
    def body(src, land, after_ref, send_sems, recv_sems, src_thru, land_thru, token):
        x, y, c = _place()
        for j, (px, py) in enumerate(_other_chips(x, y)):
            pltpu.make_async_remote_copy(
                src_ref=src.at[2 * px + py], dst_ref=land.at[2 * x + y], send_sem=send_sems.at[j], recv_sem=recv_sems.at[j],
                device_id=(px, py, c), device_id_type=MESH).start()
        token[...] = jnp.zeros_like(token)

    sems = pltpu.SemaphoreType.DMA((3,))
    return _pcall(
        body, in_specs=[IN_HBM, IN_HBM, HBM], out_specs=[SEM, SEM, IN_HBM, IN_HBM, TOKEN_SPEC],
        out_shape=[sems, sems, _hbm_like(pair), _hbm_like(pair), TOKEN], input_output_aliases={0: 2, 1: 3},
        compiler_params=pltpu.CompilerParams(has_side_effects=EFFECT), name=name,
    )(_in_hbm(pair), _landing(pair), after)


def _chip_wait(pair, parts, send_sems, recv_sems, after, name):
    def body(src, land, send, recv, after_ref, src_thru, land_thru):
        x, y, c = _place()
        for j, (px, py) in enumerate(_other_chips(x, y)):
            copy = pltpu.make_async_remote_copy(
                src_ref=src.at[2 * px + py], dst_ref=land.at[2 * px + py], send_sem=send.at[j], recv_sem=recv.at[j],
                device_id=(px, py, c), device_id_type=MESH)
            copy.wait_send()
            copy.wait_recv()

    return _pcall(
        body, in_specs=[IN_HBM, IN_HBM, SEM, SEM, HBM], out_specs=[IN_HBM, IN_HBM],
        out_shape=[_hbm_like(pair), _hbm_like(parts)], input_output_aliases={0: 0, 1: 1},
        compiler_params=pltpu.CompilerParams(has_side_effects=EFFECT), name=name,
    )(pair, parts, send_sems, recv_sems, after)


def _pair_add(core, grad, recv, block, grad_spec, name):
    _, R, C = recv.shape
    tr = block

    def body(c_ref, g_ref, r_ref, o_ref):
        o_ref[...] = (g_ref[...].astype(F32) + r_ref[...].astype(F32)).astype(BF16)

    grid_spec = pltpu.PrefetchScalarGridSpec(
        num_scalar_prefetch=1, grid=(N_CHIP, R // tr),
        in_specs=[grad_spec, pl.BlockSpec((None, tr, C), lambda k, i, c: (k, i, 0))],
        out_specs=pl.BlockSpec((None, tr, C), lambda k, i, c: (k, i, 0)))
    return _pcall(body, grid_spec=grid_spec, out_shape=jax.ShapeDtypeStruct(recv.shape, BF16),
                  compiler_params=_params("parallel", "parallel"), name=name)(core, grad, recv)


def _small_step(parts, params, name):
    na, npar = len(parts), len(params)

    def body(*refs):
        p_refs, wmv = refs[:na], refs[na:na + 3 * npar]
        o_parts = refs[na + 3 * npar:2 * na + 3 * npar]
        o_params = refs[2 * na + 3 * npar:2 * na + 7 * npar]
        alls, (send_sems, recv_sems) = refs[2 * na + 7 * npar:3 * na + 7 * npar], refs[3 * na + 7 * npar:]
        x, y, c = _place()
        me = 4 * x + 2 * y + c
        peers = [(x, y, 1 - c)] + [(px, py, pc) for px, py in _other_chips(x, y) for pc in (c, 1 - c)]
        copies = []
        for a in range(na):
            alls[a][me] = p_refs[a][...]
            copies += [pltpu.make_async_remote_copy(
                src_ref=p_refs[a], dst_ref=alls[a].at[me], send_sem=send_sems.at[7 * a + k], recv_sem=recv_sems.at[7 * a + k],
                device_id=peer, device_id_type=MESH) for k, peer in enumerate(peers)]
        for cp in copies:
            cp.start()
        for a in range(na):
            for k, (px, py, pc) in enumerate(peers):
                pltpu.make_async_remote_copy(
                    src_ref=p_refs[a], dst_ref=alls[a].at[4 * px + 2 * py + pc], send_sem=send_sems.at[7 * a + k],
                    recv_sem=recv_sems.at[7 * a + k], device_id=peers[k], device_id_type=MESH).wait_recv()
        for cp in copies:
            cp.wait_send()
        sums = []
        for a in range(na):
            acc = alls[a][0]
            for k in range(1, N_DEV):
                acc = acc + alls[a][k]
            o_parts[a][...] = acc
            sums.append(acc)
        for j, (a, row, _, _, _) in enumerate(params):
            g = sums[a][row:row + 1, :]
            d, mn, vn = _adam_math(wmv[3 * j][...], g, wmv[3 * j + 1][...], wmv[3 * j + 2][...])
            for out, val in zip(o_params[4 * j:4 * j + 4], (g, d, mn, vn)):
                out[...] = val

    vm = pl.BlockSpec(memory_space=pltpu.VMEM)
    flat = [t for p in params for t in p[2:]]
    out_shape = [jax.ShapeDtypeStruct(p.shape, F32) for p in parts]
    out_shape += [jax.ShapeDtypeStruct(p[2].shape, F32) for p in params for _ in range(4)]
    outs = _pcall(body, in_specs=[vm] * (na + 3 * npar), out_specs=[vm] * len(out_shape), out_shape=out_shape,
                  scratch_shapes=[pltpu.VMEM((N_DEV, *p.shape), F32) for p in parts]
                  + [pltpu.SemaphoreType.DMA((7 * na,)), pltpu.SemaphoreType.DMA((7 * na,))],
                  name=name)(*parts, *flat)
    return outs[:na], [outs[na + 4 * j:na + 4 * j + 4] for j in range(npar)]


def _local_step(x, tgt, gains, weights):
    g_pre_mix, g_post_mix, g_pre_ffn, g_post_ffn, g_sb, g_dil = gains
    S, D = x.shape
    hs = g_sb.shape[1] // HEAD_DIM
    hd = g_dil.shape[1] // HEAD_DIM
    cos2, sin_signed = _rope_tables(S)

    h1 = _rms_fwd(x, g_pre_mix + weights.start(), "rms_in")
    w_in_g = weights.w_in(h1)
    proj = _mm_nn(h1, w_in_g, F32, "proj", tn=768)
    o_sb, ct_sb, mx_sb = _sb_fwd(proj, g_sb, hs, "sb_fwd")
    o_dl, lse_dl, mx_dl = _dil_fwd(proj, cos2, sin_signed, g_dil + weights.forward_out(o_sb), 3 * hs, hd, "dil_fwd")
    w_out_g, dep = weights.w_out(o_dl)
    mixed = jnp.concatenate([mx_sb, mx_dl], axis=1)
    mix = _mm_nn(mixed, w_out_g, F32, "mix_out", tn=1024)
    x2, h2 = _mid_fwd(x, mix, g_post_mix + dep, g_pre_ffn, "mid_fwd")
    w_up_g, cwb = weights.w_up(h2)
    u = _mm_nn(h2, w_up_g, BF16, "ffn_up")
    y = _geglu_fwd(u, cwb + weights.forward_down(u), "geglu_fwd")
    w_down_g = weights.w_down(y)
    f = _mm_nn(y, w_down_g, F32, "ffn_down", tn=1024, tk=1408)

    dy, df, dg_post_ffn, loss = _loss_bwd(x2, f, tgt, g_post_ffn, "loss_bwd")
    dyv = _mm_nt(df, w_down_g, BF16, "d_y", tn=1408)
    dw_down = _mm_tn(y, df, D, BF16, "dw_down", tm=1408, tn=1024)
    dc, dcw_g, dcw_v = _geglu_bwd(u, dyv, cwb + weights.grad("w_down", dw_down), "geglu_bwd")
    du = _conv_bwd(dc, cwb + weights.grad_reduce("w_down", dc), "conv_bwd")
    dh2 = _mm_nt(du, w_up_g, F32, "d_h2", tk=1408)
    dw_up = _mm_tn(h2, du, w_up_g.shape[2], BF16, "dw_up")
    dx2, dmix, dg_pre_ffn, dg_post_mix = _mid_bwd(
        dy, dh2, x2, mix, g_pre_ffn + weights.grad("w_up", dw_up), g_post_mix, "mid_bwd")
    dmixed = _mm_nt(dmix, w_out_g, F32, "d_mixed", after=jnp.reshape(weights.grad_reduce("w_up", dmix), (1, 1)))
    dw_out = _mm_tn(mixed, dmix, D, BF16, "dw_out", tn=1024)
    dq_s, dk_s, dv_s, dg_sb = _sb_bwd(proj, g_sb + weights.grad("w_out", dw_out), o_sb, ct_sb, dmixed, 0, hs, "sb_bwd")
    dq_d, dk_d, dv_d, dg_dil = _dil_bwd(proj, cos2, sin_signed, g_dil + weights.grad_reduce("w_out", dq_s), o_dl, lse_dl,
                                        dmixed, hs, 3 * hs, hd, "dil_bwd")
    dproj = jnp.concatenate([dq_s, dk_s, dv_s, dq_d, dk_d, dv_d], axis=1)
    dh1 = _mm_nt(dproj, w_in_g, F32, "d_h1", tk=768)
    grad_x, dg_pre_mix = _first_bwd(dx2, dh1, x, g_pre_mix, "first_bwd")
    small = (dg_pre_mix, dg_post_mix, dg_pre_ffn, dg_post_ffn, dg_sb[0:1], dg_dil[0:1], jnp.concatenate([dcw_g, dcw_v], axis=1))
    dw_in = _mm_tn(h1, dproj, w_in_g.shape[2], BF16, "dw_in", tn=768, after=weights.small(small, loss))
    weights.grad("w_in", dw_in)
    weights.grad_reduce("w_in", grad_x)
    return loss, grad_x, small


def _pad_cols(a, to):
    return jnp.pad(a, ((0, 0), (0, to - a.shape[1])))


def kernel(x, pre_mix_gain, post_mix_gain, pre_ffn_gain, post_ffn_gain, w_in, sb_out_gain, dil_out_gain, w_out, w_up, conv_w, conv_b, w_down, loss_target, m_pre_mix_gain, m_post_mix_gain, m_pre_ffn_gain, m_post_ffn_gain, m_w_in, m_sb_out_gain, m_dil_out_gain, m_w_out, m_w_up, m_conv_w, m_conv_b, m_w_down, v_pre_mix_gain, v_post_mix_gain, v_pre_ffn_gain, v_post_ffn_gain, v_w_in, v_sb_out_gain, v_dil_out_gain, v_w_out, v_w_up, v_conv_w, v_conv_b, v_w_down):
    xb, tb = x[0], loss_target[0]
    S, D = xb.shape
    w_in, w_out, w_up, w_down, conv_w = w_in[0], w_out[0], w_up[0], w_down[0], conv_w[0]
    n_in, e_rows = w_in.shape[1], w_out.shape[0]
    cu, half = w_up.shape[1], w_down.shape[0]
    assert cu == 2 * half and half % 16 == 0
    cup = -(-cu // LANES) * LANES
    fp = N_CHIP * cup
    px, py, pc = _place()
    me = 4 * px + 2 * py + pc
    core = jnp.reshape(pc, (1,)).astype(jnp.int32)

    shards = [w_in.astype(BF16), w_out.astype(BF16), _pad_cols(w_up, cup).astype(BF16), w_down.astype(BF16),
              jnp.pad(_pad_cols(conv_w, cup), ((0, 8 - conv_w.shape[0]), (0, 0)))]

    def by_dev(ref, qx, qy, qc):
        return ref.at[4 * qx + 2 * qy + qc]

    def down_slot(ref, qx, qy, qc):
        return ref.at[2 * qx + qy, pl.ds(qc * half, half)]

    def by_pair(ref, chip, k):
        return ref.at[chip, k]

    def down_pair(ref, chip, k):
        return ref.at[chip, pl.ds(k * half, half)]

    def pair_spec(tr, cols):
        return pl.BlockSpec((None, None, tr, cols), lambda k, i, c: (k, c[0], i, 0))

    tr_in, tr_up = _tile(D, 512, 16), _tile(D, 256, 16)
    grad_plan = {
        "w_in": ((N_CHIP, 2, D, n_in), by_pair, (D, n_in), tr_in, pair_spec(tr_in, n_in)),
        "w_out": ((N_CHIP, 2, e_rows, D), by_pair, (e_rows, D), e_rows, pair_spec(e_rows, D)),
        "w_up": ((N_CHIP, 2, D, cup), by_pair, (D, cup), tr_up, pair_spec(tr_up, cup)),
        "w_down": ((N_CHIP, cup, D), down_pair, (half, D), half,
                   pl.BlockSpec((None, half, D), lambda k, i, c: (k, c[0], 0))),
    }

    class Exchanges:
        def __init__(self):
            self.in_flight = {}

        def start(self):
            def own_slot(shard):
                return lax.dynamic_update_index_in_dim(lax.empty((N_DEV, *shard.shape), shard.dtype), shard, me, 0)

            down = lax.dynamic_update_slice(jnp.zeros((N_CHIP, cup, D), BF16), shards[3][None], (2 * px + py, pc * half, 0))
            landing = [own_slot(shards[0]), own_slot(shards[1]), own_slot(shards[2]), down, own_slot(shards[4])]
            self.g_in = _gather_start(landing[:1], [by_dev], core, "gather_in_start")
            self.g_out = _gather_start(landing[1:2], [by_dev], self.g_in[3], "gather_out_start")
            self.g_up = _gather_start([landing[2], landing[4]], [by_dev, by_dev], self.g_out[3], "gather_up_start")
            self.g_down = _gather_start(landing[3:4], [down_slot], self.g_up[3], "gather_down_start")
            return self.g_down[3][0, 0]

        def w_in(self, after):
            send, recv, gath, _ = self.g_in
            fsend, frecv, gath, token = _gather_forward(gath, send, recv, [by_dev], after, "gather_in_forward")
            return _gather_finish(gath, fsend, frecv, [by_dev], token, "gather_in_finish")[0]

        def forward_out(self, after):
            send, recv, gath, _ = self.g_out
            self.p_out = _gather_forward(gath, send, recv, [by_dev], after, "gather_out_forward")
            return self.p_out[3][0, 0]

        def w_out(self, after):
            fsend, frecv, gath, _ = self.p_out
            w_out_g = _gather_finish(gath, fsend, frecv, [by_dev], after, "gather_out_finish")[0]
            send, recv, gath, _ = self.g_up
            self.p_up = _gather_forward(gath, send, recv, [by_dev, by_dev], w_out_g, "gather_up_forward")
            return w_out_g.reshape(1, N_DEV * e_rows, D), self.p_up[3][0, 0]

        def w_up(self, after):
            fsend, frecv, gath, _ = self.p_up
            w_up_g, cw_g = _gather_finish(gath, fsend, frecv, [by_dev, by_dev], after, "gather_up_finish")
            cb = _pad_cols(conv_b.reshape(N_DEV, cu), cup).reshape(1, 2 * fp)
            cw_full = jnp.transpose(cw_g[:, :3, :], (1, 0, 2)).reshape(3, 2 * fp)
            cwb = jnp.concatenate([cw_full, cb, jnp.zeros((4, 2 * fp), F32)], axis=0)
            return w_up_g, cwb

        def forward_down(self, after):
            send, recv, gath, _ = self.g_down
            self.p_down = _gather_forward(gath, send, recv, [down_slot], after, "gather_down_forward")
            return self.p_down[3][0, 0]

        def w_down(self, after):
            fsend, frecv, gath, _ = self.p_down
            return _gather_finish(gath, fsend, frecv, [down_slot], after, "gather_down_finish")[0].reshape(1, fp, D)

        def small(self, small, loss):
            d_pre_mix, d_post_mix, d_pre_ffn, d_post_ffn, d_sb, d_dil, d_conv = small

            def rows_of(*vectors):
                n = vectors[0].shape[1]
                row = lax.broadcasted_iota(jnp.int32, (8, n), 0)
                out = jnp.zeros((8, n), F32)
                for k, vec in enumerate(vectors):
                    out = jnp.where(row == k, vec, out)
                return out

            parts = [rows_of(d_pre_mix, d_post_mix, d_pre_ffn, d_post_ffn, jnp.broadcast_to(loss[:, :1], (1, D))),
                     rows_of(d_sb, d_dil), d_conv]
            params = [(0, 0, pre_mix_gain, m_pre_mix_gain, v_pre_mix_gain), (0, 1, post_mix_gain, m_post_mix_gain, v_post_mix_gain),
                      (0, 2, pre_ffn_gain, m_pre_ffn_gain, v_pre_ffn_gain), (0, 3, post_ffn_gain, m_post_ffn_gain, v_post_ffn_gain),
                      (1, 0, sb_out_gain, m_sb_out_gain, v_sb_out_gain), (1, 1, dil_out_gain, m_dil_out_gain, v_dil_out_gain)]
            (gains_sum, _, self.conv_sum), self.gain_steps = _small_step(parts, params, "small_step")
            self.loss_sum = gains_sum[4, 0]
            return self.conv_sum

        def grad(self, name, dw):
            view_shape, view, block, tr, spec = grad_plan[name]
            send, recv_sems, dw, recv, token = _pair_start(dw.reshape(view_shape), view, block, core, "pair_start_" + name)
            self.in_flight[name] = (dw, recv, send, recv_sems)
            return token[0, 0]

        def grad_reduce(self, name, after):
            _, view, _, tr, spec = grad_plan[name]
            dw, recv = _pair_wait(*self.in_flight[name], view, after, "pair_wait_" + name)
            pair = _pair_add(core, dw, recv, tr, spec, "pair_add_" + name)
            send, recv_sems, pair, parts, token = _chip_start(pair, recv, "chip_start_" + name)
            self.in_flight[name] = (pair, parts, send, recv_sems)
            self.last_token = token
            return token[0, 0]

        def grad_parts(self, name, after):
            return _chip_wait(*self.in_flight[name], after, "chip_wait_" + name)

    exchanges = Exchanges()
    gains = (pre_mix_gain, post_mix_gain, pre_ffn_gain, post_ffn_gain, sb_out_gain, dil_out_gain)
    loss, grad_x, small = _local_step(xb, tb, gains, exchanges)

    loss_out, g_conv = exchanges.loss_sum, exchanges.conv_sum
    g_conv_b = g_conv[3].reshape(N_DEV, cup)[:, :cu].reshape(1, N_DEV * cu)
    g_conv_w = lax.dynamic_index_in_dim(g_conv[0:3].reshape(3, N_DEV, cup), me, axis=1, keepdims=False)[:, :cu]

    def small_adam(w, g, m, v, name):
        one = w.shape[0] == 1
        if one:
            w, g, m, v = (jnp.broadcast_to(t, (8, t.shape[1])) for t in (w, g, m, v))
        outs = _adamw(w, g[None], m, v, name)
        return [o[0:1] for o in outs] if one else outs

    chip_ids = jnp.stack([2 * px + py, 2 * (1 - px) + py, 2 * px + 1 - py, 2 * (1 - px) + 1 - py]).astype(jnp.int32)
    out_w_down = _adamw_chips(w_down, *exchanges.grad_parts("w_down", exchanges.last_token), chip_ids, m_w_down[0], v_w_down[0], "adam_w_down")
    out_w_up = _adamw_chips(w_up, *exchanges.grad_parts("w_up", out_w_down[1]), chip_ids, m_w_up[0], v_w_up[0], "adam_w_up")
    out_w_out = _adamw_chips(w_out, *exchanges.grad_parts("w_out", out_w_up[1]), chip_ids, m_w_out[0], v_w_out[0], "adam_w_out")
    out_w_in = _adamw_chips(w_in, *exchanges.grad_parts("w_in", out_w_out[1]), chip_ids, m_w_in[0], v_w_in[0], "adam_w_in")
    out_pre_mix, out_post_mix, out_pre_ffn, out_post_ffn, out_sb, out_dil = exchanges.gain_steps
    out_conv_b = small_adam(conv_b, g_conv_b, m_conv_b, v_conv_b, "adam_conv_b")
    cw8 = [jnp.pad(t, ((0, 5), (0, 0))) for t in (conv_w, g_conv_w, m_conv_w[0], v_conv_w[0])]
    out_conv_w = [o[0:3] for o in _adamw(cw8[0], cw8[1][None], cw8[2], cw8[3], "adam_conv_w")]

    order = [out_pre_mix, out_post_mix, out_pre_ffn, out_post_ffn, [o[None] for o in out_w_in], out_sb, out_dil,
             [o[None] for o in out_w_out], [o[None] for o in out_w_up], [o[None] for o in out_conv_w], out_conv_b,
             [o[None] for o in out_w_down]]
    outs = [loss_out, grad_x[None]]
    for k in range(4):
        outs += [o[k] for o in order]
    return tuple(outs)
```

```python
import functools
import math

import jax
import jax.numpy as jnp
from jax import lax
from jax.experimental import pallas as pl
from jax.experimental.pallas import tpu as pltpu

F32 = jnp.float32
BF16 = jnp.bfloat16
HEAD_DIM = 128
LANES = 128
KEY_BLOCK = 128
DILATIONS = (1, 4, 16)
RMS_EPS = 1e-6
ROPE_THETA = 10000.0
NEG = -1e30
ADAM_LR, ADAM_B1, ADAM_B2, ADAM_EPS, ADAM_WD, ADAM_STEP = 0.001, 0.9, 0.999, 1e-08, 0.01, 10
MESH = pl.DeviceIdType.MESH
N_DEV = 8
N_CHIP = 4
HBM = pl.BlockSpec(memory_space=pl.ANY)
VMEM_LIMIT = 56 * 1024 * 1024

_pcall = pl.pallas_call


def _tile(n, pref, mult=LANES):
    best = None
    t = mult
    while t <= min(n, pref):
        if n % t == 0:
            best = t
        t += mult
    return n if best is None else best


def _params(*sem):
    return pltpu.CompilerParams(dimension_semantics=sem, vmem_limit_bytes=VMEM_LIMIT)


def _dot(a, b, dims):
    return lax.dot_general(a, b, (dims, ((), ())), preferred_element_type=F32)


NN = ((1,), (0,))
NT = ((1,), (1,))
TN = ((0,), (0,))


def _mm_body(dims, nk, tile):
    if nk == 1:
        def single(a_ref, b_ref, o_ref):
            o_ref[...] = _dot(a_ref[...].astype(BF16), b_ref[...].astype(BF16), dims).astype(o_ref.dtype)

        return single, []

    def body(a_ref, b_ref, o_ref, acc_ref):
        k = pl.program_id(2)

        @pl.when(k == 0)
        def _():
            acc_ref[...] = jnp.zeros_like(acc_ref)

        acc_ref[...] += _dot(a_ref[...].astype(BF16), b_ref[...].astype(BF16), dims)

        @pl.when(k == nk - 1)
        def _():
            o_ref[...] = acc_ref[...].astype(o_ref.dtype)

    return body, [pltpu.VMEM(tile, F32)]


def _mm_nn(a, b3, out_dtype, name, tm=1024, tn=1408, tk=2048, b_transposed=False):
    M, K = a.shape
    C, n = b3.shape[0], b3.shape[1 if b_transposed else 2]
    tm, tk, tn = _tile(M, tm, 8), _tile(K, tk), _tile(n, tn)
    npc, nk = n // tn, K // tk
    body, scratch = _mm_body(NT if b_transposed else NN, nk, (tm, tn))
    b_spec = (pl.BlockSpec((None, tn, tk), lambda i, j, k: (j // npc, j % npc, k)) if b_transposed
              else pl.BlockSpec((None, tk, tn), lambda i, j, k: (j // npc, k, j % npc)))
    return _pcall(
        body, grid=(M // tm, C * npc, nk),
        in_specs=[pl.BlockSpec((tm, tk), lambda i, j, k: (i, k)), b_spec],
        out_specs=pl.BlockSpec((tm, tn), lambda i, j, k: (i, j)),
        out_shape=jax.ShapeDtypeStruct((M, C * n), out_dtype), scratch_shapes=scratch,
        compiler_params=_params("parallel", "parallel", "arbitrary"), name=name)(a, b3)


def _mm_nt(a, b3, out_dtype, name, tm=1024, tn=1024, tk=2048, after=None, b_transposed=False):
    M, _ = a.shape
    C, N, n = (b3.shape[0], b3.shape[2], b3.shape[1]) if b_transposed else b3.shape
    tm, tn, tk = _tile(M, tm, 8), _tile(N, tn), _tile(n, tk)
    kpc = n // tk
    nk = C * kpc
    inner, scratch = _mm_body(NN if b_transposed else NT, nk, (tm, tn))
    extra = [] if after is None else [after]

    def body(a_ref, b_ref, *rest):
        inner(a_ref, b_ref, *rest[len(extra):])

    b_spec = (pl.BlockSpec((None, tk, tn), lambda i, j, k: (k // kpc, k % kpc, j)) if b_transposed
              else pl.BlockSpec((None, tn, tk), lambda i, j, k: (k // kpc, j, k % kpc)))
    return _pcall(
        body, grid=(M // tm, N // tn, nk),
        in_specs=[pl.BlockSpec((tm, tk), lambda i, j, k: (i, k)), b_spec] + [HBM] * len(extra),
        out_specs=pl.BlockSpec((tm, tn), lambda i, j, k: (i, j)),
        out_shape=jax.ShapeDtypeStruct((M, N), out_dtype), scratch_shapes=scratch,
        compiler_params=_params("parallel", "parallel", "arbitrary"), name=name)(a, b3, *extra)


def _mm_tn(x, y, n, out_dtype, name, tm=1024, tn=1408, tk=2048, after=None):
    S, P = x.shape
    C = y.shape[1] // n
    tm, tn, tk = _tile(P, tm), _tile(n, tn), _tile(S, tk, 8)
    npc, nk = n // tn, S // tk
    inner, scratch = _mm_body(TN, nk, (tm, tn))
    extra = [] if after is None else [after]

    def body(x_ref, y_ref, *rest):
        inner(x_ref, y_ref, *rest[len(extra):])

    return _pcall(
        body, grid=(P // tm, C * npc, nk),
        in_specs=[pl.BlockSpec((tk, tm), lambda i, j, k: (k, i)),
                  pl.BlockSpec((tk, tn), lambda i, j, k: (k, j))] + [HBM] * len(extra),
        out_specs=pl.BlockSpec((None, tm, tn), lambda i, j, k: (j // npc, i, j % npc)),
        out_shape=jax.ShapeDtypeStruct((C, P, n), out_dtype), scratch_shapes=scratch,
        compiler_params=_params("parallel", "parallel", "arbitrary"), name=name)(x, y, *extra)


def _rms_scale(v):
    return lax.rsqrt(jnp.mean(v * v, axis=-1, keepdims=True) + RMS_EPS)


def _rms_bwd(gy, v, r):
    return r * gy - v * (r * r * r * jnp.mean(gy * v, axis=-1, keepdims=True))


def _rows_spec(tm, d):
    return pl.BlockSpec((tm, d), lambda i: (i, 0))


def _vec_spec(d):
    return pl.BlockSpec((1, d), lambda i: (0, 0))


def _rms_fwd(x, g, name, tm=256):
    S, D = x.shape

    def body(x_ref, g_ref, h_ref):
        v = x_ref[...]
        h_ref[...] = (v * _rms_scale(v) * g_ref[...]).astype(BF16)

    return _pcall(body, grid=(S // tm,), in_specs=[_rows_spec(tm, D), _vec_spec(D)], out_specs=_rows_spec(tm, D),
                  out_shape=jax.ShapeDtypeStruct((S, D), BF16), compiler_params=_params("parallel"), name=name)(x, g)


def _mid_fwd(x, mix, g_post, g_pre, name, tm=256):
    S, D = x.shape

    def body(x_ref, m_ref, gp_ref, gn_ref, x2_ref, h_ref):
        m = m_ref[...]
        x2 = x_ref[...] + m * _rms_scale(m) * gp_ref[...]
        x2_ref[...] = x2
        h_ref[...] = (x2 * _rms_scale(x2) * gn_ref[...]).astype(BF16)

    return _pcall(body, grid=(S // tm,), in_specs=[_rows_spec(tm, D), _rows_spec(tm, D), _vec_spec(D), _vec_spec(D)],
                  out_specs=[_rows_spec(tm, D), _rows_spec(tm, D)],
                  out_shape=[jax.ShapeDtypeStruct((S, D), F32), jax.ShapeDtypeStruct((S, D), BF16)],
                  compiler_params=_params("parallel"), name=name)(x, mix, g_post, g_pre)


def _loss_bwd(x2, f, tgt, g_post, name, tm=256):
    S, D = x2.shape

    def body(x2_ref, f_ref, t_ref, g_ref, dy_ref, df_ref, dg_ref, ls_ref):
        i = pl.program_id(0)

        @pl.when(i == 0)
        def _():
            dg_ref[...] = jnp.zeros_like(dg_ref)
            ls_ref[...] = jnp.zeros_like(ls_ref)

        fv = f_ref[...]
        r = _rms_scale(fv)
        g = g_ref[...]
        err = x2_ref[...] + fv * r * g - t_ref[...]
        ls_ref[...] += jnp.broadcast_to(0.5 * jnp.sum(jnp.mean(err * err, axis=-1, keepdims=True), axis=0, keepdims=True), ls_ref.shape)
        dy = err * (1.0 / D)
        dy_ref[...] = dy
        df_ref[...] = _rms_bwd(dy * g, fv, r).astype(BF16)
        dg_ref[...] += jnp.sum(dy * fv * r, axis=0, keepdims=True)

    return _pcall(body, grid=(S // tm,),
                  in_specs=[_rows_spec(tm, D), _rows_spec(tm, D), _rows_spec(tm, D), _vec_spec(D)],
                  out_specs=[_rows_spec(tm, D), _rows_spec(tm, D), _vec_spec(D), _vec_spec(LANES)],
                  out_shape=[jax.ShapeDtypeStruct((S, D), F32), jax.ShapeDtypeStruct((S, D), BF16),
                             jax.ShapeDtypeStruct((1, D), F32), jax.ShapeDtypeStruct((1, LANES), F32)],
                  compiler_params=_params("arbitrary"), name=name)(x2, f, tgt, g_post)


def _mid_bwd(dy, dh2, x2, mix, g_pre, g_post, name, tm=256):
    S, D = dy.shape

    def body(dy_ref, dh_ref, x2_ref, m_ref, gn_ref, gp_ref, dx2_ref, dm_ref, dgn_ref, dgp_ref):
        i = pl.program_id(0)

        @pl.when(i == 0)
        def _():
            dgn_ref[...] = jnp.zeros_like(dgn_ref)
            dgp_ref[...] = jnp.zeros_like(dgp_ref)

        x2, dh = x2_ref[...], dh_ref[...]
        r = _rms_scale(x2)
        dx2 = dy_ref[...] + _rms_bwd(dh * gn_ref[...], x2, r)
        dgn_ref[...] += jnp.sum(dh * x2 * r, axis=0, keepdims=True)
        dx2_ref[...] = dx2
        m = m_ref[...]
        rm = _rms_scale(m)
        dm_ref[...] = _rms_bwd(dx2 * gp_ref[...], m, rm).astype(BF16)
        dgp_ref[...] += jnp.sum(dx2 * m * rm, axis=0, keepdims=True)

    return _pcall(body, grid=(S // tm,),
                  in_specs=[_rows_spec(tm, D)] * 4 + [_vec_spec(D)] * 2,
                  out_specs=[_rows_spec(tm, D), _rows_spec(tm, D), _vec_spec(D), _vec_spec(D)],
                  out_shape=[jax.ShapeDtypeStruct((S, D), F32), jax.ShapeDtypeStruct((S, D), BF16),
                             jax.ShapeDtypeStruct((1, D), F32), jax.ShapeDtypeStruct((1, D), F32)],
                  compiler_params=_params("arbitrary"), name=name)(dy, dh2, x2, mix, g_pre, g_post)


def _first_bwd(dx2, dh1, x, g_pre, name, tm=256):
    S, D = x.shape

    def body(dx2_ref, dh_ref, x_ref, g_ref, gx_ref, dg_ref):
        i = pl.program_id(0)

        @pl.when(i == 0)
        def _():
            dg_ref[...] = jnp.zeros_like(dg_ref)

        xv, dh = x_ref[...], dh_ref[...]
        r = _rms_scale(xv)
        gx_ref[...] = dx2_ref[...] + _rms_bwd(dh * g_ref[...], xv, r)
        dg_ref[...] += jnp.sum(dh * xv * r, axis=0, keepdims=True)

    return _pcall(body, grid=(S // tm,), in_specs=[_rows_spec(tm, D)] * 3 + [_vec_spec(D)],
                  out_specs=[_rows_spec(tm, D), _vec_spec(D)],
                  out_shape=[jax.ShapeDtypeStruct((S, D), F32), jax.ShapeDtypeStruct((1, D), F32)],
                  compiler_params=_params("arbitrary"), name=name)(dx2, dh1, x, g_pre)


def _logsig_pair(z):
    lb = jnp.minimum(z, 0.0) - jnp.log(1.0 + jnp.exp(-jnp.abs(z)))
    return lb, lb - z


SB_KEY_BLOCK = 256


def _sum_matrix(strict):
    ia = lax.broadcasted_iota(jnp.int32, (SB_KEY_BLOCK, SB_KEY_BLOCK), 0)
    ib = lax.broadcasted_iota(jnp.int32, (SB_KEY_BLOCK, SB_KEY_BLOCK), 1)
    tri = (ia > ib) if strict == ">" else (ia < ib)
    return jnp.concatenate([tri.astype(BF16), jnp.ones((SB_KEY_BLOCK, LANES), BF16)], axis=1)


def _lanes(c, width):
    return jnp.tile(c, (1, width // LANES))


def _split_dot(v, u):
    hi = v.astype(BF16)
    lo = (v - hi.astype(F32)).astype(BF16)
    return _dot(hi, u, NN) + _dot(lo, u, NN)


def _head_out(o, g):
    return o * _rms_scale(o) * g


def _sb_fwd(proj, gain, n_heads, name, tq=1024):
    S = proj.shape[0]
    H, tk = n_heads, SB_KEY_BLOCK
    tq = _tile(S, tq, 2 * tk)
    scale = HEAD_DIM ** -0.5

    def body(q_ref, k_ref, v_ref, g_ref, o_ref, ct_ref, mx_ref, oacc, cacc):
        i = pl.program_id(1)
        oacc[...] = jnp.zeros_like(oacc)
        cacc[...] = jnp.zeros_like(cacc)
        sums = _sum_matrix(">")

        def block(k0, r0, diagonal):
            rows = pl.ds(r0, tq - r0)
            q = q_ref[rows, :].astype(BF16)
            kj = k_ref[pl.ds(k0, tk), :].astype(BF16)
            vj = v_ref[pl.ds(k0, tk), :].astype(BF16)
            lb, lk = _logsig_pair(_dot(q, kj, NT) * scale)
            if diagonal:
                causal = (lax.broadcasted_iota(jnp.int32, (tq - r0, tk), 1) < lax.broadcasted_iota(jnp.int32, (tq - r0, tk), 0))
                lk = jnp.where(causal, lk, 0.0)
            both = _split_dot(lk, sums)
            c = cacc[rows, :]
            a = jnp.exp(lb + both[:, :tk] + _lanes(c, tk))
            if diagonal:
                a = jnp.where(causal, a, 0.0)
            oacc[rows, :] += _dot(a.astype(BF16), vj, NN)
            cacc[rows, :] = c + both[:, tk:]

        for d in reversed(range(tq // tk)):
            block(pl.multiple_of(i * tq + d * tk, tk), d * tk, True)
        n_pairs = i * (tq // tk // 2)

        def step(it, carry):
            k0 = pl.multiple_of((n_pairs - 1 - it) * 2 * tk, 2 * tk)
            block(pl.multiple_of(k0 + tk, tk), 0, False)
            block(k0, 0, False)
            return carry

        lax.fori_loop(0, n_pairs, step, 0)
        o = oacc[...]
        o_ref[...] = o
        ct_ref[...] = cacc[...]
        mx_ref[...] = _head_out(o, g_ref[...]).astype(BF16)

    blk = pl.BlockSpec((tq, HEAD_DIM), lambda h, i: (i, h))
    return _pcall(
        body, grid=(H, S // tq),
        in_specs=[blk, pl.BlockSpec((S, HEAD_DIM), lambda h, i: (0, H + h)),
                  pl.BlockSpec((S, HEAD_DIM), lambda h, i: (0, 2 * H + h)), pl.BlockSpec((1, HEAD_DIM), lambda h, i: (0, h))],
        out_specs=[blk, blk, blk],
        out_shape=[jax.ShapeDtypeStruct((S, H * HEAD_DIM), F32), jax.ShapeDtypeStruct((S, H * HEAD_DIM), F32),
                   jax.ShapeDtypeStruct((S, H * HEAD_DIM), BF16)],
        scratch_shapes=[pltpu.VMEM((tq, HEAD_DIM), F32), pltpu.VMEM((tq, LANES), F32)],
        compiler_params=_params("parallel", "arbitrary"), name=name)(proj, proj, proj, gain)


def _sb_bwd(proj, gain, o_raw, ctot, dmixed, dm_col0, n_heads, name, tq=1024):
    S = proj.shape[0]
    H, tk = n_heads, SB_KEY_BLOCK
    tq = _tile(S, tq, 2 * tk)
    nq = S // tq
    scale = HEAD_DIM ** -0.5

    def body(q_ref, k_ref, v_ref, g_ref, o_ref, ct_ref, dm_ref, dq_ref, dk_ref, dv_ref, dg_ref,
             dkacc, dvacc, dqacc, pfx, gcar, dos):
        i = pl.program_id(1)

        @pl.when(i == 0)
        def _():
            dkacc[...] = jnp.zeros_like(dkacc)
            dvacc[...] = jnp.zeros_like(dvacc)
            dg_ref[...] = jnp.zeros_like(dg_ref)

        o, dm, g = o_ref[...], dm_ref[...], g_ref[...]
        r = _rms_scale(o)
        dos[...] = _rms_bwd(dm * g, o, r).astype(BF16)
        dg_ref[...] += jnp.broadcast_to(jnp.sum(dm * o * r, axis=0, keepdims=True), dg_ref.shape)
        dqacc[...] = jnp.zeros_like(dqacc)
        pfx[...] = jnp.zeros_like(pfx)
        gcar[...] = jnp.zeros_like(gcar)
        later, earlier = _sum_matrix(">"), _sum_matrix("<")

        def block(k0, r0, diagonal):
            rows = pl.ds(r0, tq - r0)
            keys = pl.ds(k0, tk)
            q, do = q_ref[rows, :].astype(BF16), dos[rows, :]
            kj, vj = k_ref[keys, :].astype(BF16), v_ref[keys, :].astype(BF16)
            lb, lk = _logsig_pair(_dot(q, kj, NT) * scale)
            if diagonal:
                causal = (lax.broadcasted_iota(jnp.int32, (tq - r0, tk), 1) < lax.broadcasted_iota(jnp.int32, (tq - r0, tk), 0))
                lk = jnp.where(causal, lk, 0.0)
            both = _split_dot(lk, later)
            p = pfx[rows, :] + both[:, tk:]
            a = jnp.exp(lb + both[:, :tk] + _lanes(ct_ref[rows, :] - p, tk))
            if diagonal:
                a = jnp.where(causal, a, 0.0)
            dl = _dot(do, vj, NT) * a
            dvacc[keys, :] += _dot(a.astype(BF16), do, TN)
            both = _dot(dl.astype(BF16), earlier, NN)
            gc = gcar[rows, :]
            sig = jnp.exp(lb)
            gsum = (both[:, :tk] + _lanes(gc, tk)) * sig
            if diagonal:
                gsum = jnp.where(causal, gsum, 0.0)
            dz = ((dl * (1.0 - sig) - gsum) * scale).astype(BF16)
            dqacc[rows, :] += _dot(dz, kj, NN)
            dkacc[keys, :] += _dot(dz, q, TN)
            pfx[rows, :] = p
            gcar[rows, :] = gc + both[:, tk:]

        def step(j, carry):
            k0 = pl.multiple_of(j * 2 * tk, 2 * tk)
            block(k0, 0, False)
            block(pl.multiple_of(k0 + tk, tk), 0, False)
            return carry

        lax.fori_loop(0, i * (tq // tk // 2), step, 0)
        for d in range(tq // tk):
            block(pl.multiple_of(i * tq + d * tk, tk), d * tk, True)
        dq_ref[...] = dqacc[...].astype(BF16)

        @pl.when(i == nq - 1)
        def _():
            dk_ref[...] = dkacc[...].astype(BF16)
            dv_ref[...] = dvacc[...].astype(BF16)

    blk = pl.BlockSpec((tq, HEAD_DIM), lambda h, i: (i, h))
    full = pl.BlockSpec((S, HEAD_DIM), lambda h, i: (0, h))
    W = H * HEAD_DIM
    return _pcall(
        body, grid=(H, nq),
        in_specs=[blk, pl.BlockSpec((S, HEAD_DIM), lambda h, i: (0, H + h)),
                  pl.BlockSpec((S, HEAD_DIM), lambda h, i: (0, 2 * H + h)), pl.BlockSpec((1, HEAD_DIM), lambda h, i: (0, h)),
                  blk, blk, pl.BlockSpec((tq, HEAD_DIM), lambda h, i: (i, dm_col0 + h))],
        out_specs=[blk, full, full, pl.BlockSpec((8, HEAD_DIM), lambda h, i: (0, h))],
        out_shape=[jax.ShapeDtypeStruct((S, W), BF16), jax.ShapeDtypeStruct((S, W), BF16),
                   jax.ShapeDtypeStruct((S, W), BF16), jax.ShapeDtypeStruct((8, W), F32)],
        scratch_shapes=[pltpu.VMEM((S, HEAD_DIM), F32), pltpu.VMEM((S, HEAD_DIM), F32), pltpu.VMEM((tq, HEAD_DIM), F32),
                        pltpu.VMEM((tq, LANES), F32), pltpu.VMEM((tq, LANES), F32), pltpu.VMEM((tq, HEAD_DIM), BF16)],
        compiler_params=_params("arbitrary", "arbitrary"), name=name)(proj, proj, proj, gain, o_raw, ctot, dmixed)


def _rope_tables(S):
    inv_freq = ROPE_THETA ** (-jnp.arange(0, HEAD_DIM, 2, dtype=F32) / HEAD_DIM)
    ang = jnp.arange(S, dtype=F32)[:, None] * inv_freq[None, :]
    cos, sin = jnp.cos(ang), jnp.sin(ang)
    return jnp.concatenate([cos, cos], axis=1), jnp.concatenate([-sin, sin], axis=1)


def _rope(v, cos2, sin_signed):
    return v * cos2 + pltpu.roll(v, HEAD_DIM // 2, axis=1) * sin_signed


def _dil_rows(d, r, l0, n):
    if d == 1:
        return pl.ds(l0 if isinstance(l0, int) else pl.multiple_of(l0, KEY_BLOCK), n)
    return pl.ds(r + d * l0, n, stride=d)


def _dil_blocks(S, visit):
    B = KEY_BLOCK
    group = 4
    for b, d in enumerate(DILATIONS):
        nb = S // d // B
        if nb == 1:
            g = math.gcd(d, group)

            def trip(t, carry, b=b, d=d, g=g):
                for u in range(g):
                    visit(b, d, t * g + u, 0, True)
                return carry

            lax.fori_loop(0, d // g, trip, 0)
        elif d == 1:
            visit(b, d, 0, 0, True)
            g = max(k for k in range(1, group + 2) if (nb - 1) % k == 0)

            def trip(t, carry, b=b, d=d, g=g):
                for u in range(g):
                    visit(b, d, 0, (1 + t * g + u) * B, False)
                return carry

            lax.fori_loop(0, (nb - 1) // g, trip, 0)
        else:
            def trip(r, carry, b=b, d=d, nb=nb):
                visit(b, d, r, 0, True)
                for n in range(1, nb):
                    visit(b, d, r, n * B, False)
                return carry

            lax.fori_loop(0, d, trip, 0)


def _dil_mask(first):
    B = KEY_BLOCK
    nk = B if first else 2 * B
    iq = lax.broadcasted_iota(jnp.int32, (B, nk), 0)
    ik = lax.broadcasted_iota(jnp.int32, (B, nk), 1)
    return (ik <= iq) if first else ((ik >= iq) & (ik <= iq + B))


def _dil_fwd(proj, cos2, sin_signed, gain, col0, n_heads, name):
    S = proj.shape[0]
    H, B = n_heads, KEY_BLOCK
    scale = HEAD_DIM ** -0.5
    rc = _tile(S, 256, 8)

    def body(q_ref, k_ref, v_ref, c_ref, s_ref, g_ref, o_ref, l_ref, mx_ref, qr, kr, *per_branch):
        ob, lb = per_branch[:len(DILATIONS)], per_branch[len(DILATIONS):]

        def rope_rows(t, carry):
            rows = pl.ds(pl.multiple_of(t * rc, rc), rc)
            qr[rows, :] = _rope(q_ref[rows, :], c_ref[rows, :], s_ref[rows, :])
            kr[rows, :] = _rope(k_ref[rows, :], c_ref[rows, :], s_ref[rows, :])
            return carry

        lax.fori_loop(0, S // rc, rope_rows, 0)

        def visit(b, d, r, l0, first):
            nk = B if first else 2 * B
            qrows = _dil_rows(d, r, l0, B)
            krows = qrows if first else _dil_rows(d, r, l0 - B, nk)
            s = _dot(qr[qrows, :].astype(BF16), kr[krows, :].astype(BF16), NT) * scale
            s = jnp.where(_dil_mask(first), s, NEG)
            m = jnp.max(s, axis=1, keepdims=True)
            p = jnp.exp(s - m)
            den = jnp.sum(p, axis=1, keepdims=True)
            ob[b][qrows, :] = _dot(p.astype(BF16), v_ref[krows, :].astype(BF16), NN) / den
            lb[b][qrows, :] = jnp.broadcast_to(m + jnp.log(den), (B, LANES))

        _dil_blocks(S, visit)

        def combine(t, carry):
            rows = pl.ds(pl.multiple_of(t * rc, rc), rc)
            l0, l1, l2 = lb[0][rows, :], lb[1][rows, :], lb[2][rows, :]
            m = jnp.maximum(jnp.maximum(l0, l1), l2)
            w0, w1, w2 = jnp.exp(l0 - m), jnp.exp(l1 - m), jnp.exp(l2 - m)
            den = w0 + w1 + w2
            o = (w0 * ob[0][rows, :] + w1 * ob[1][rows, :] + w2 * ob[2][rows, :]) / den
            o_ref[rows, :] = o
            l_ref[rows, :] = m + jnp.log(den)
            mx_ref[rows, :] = _head_out(o, g_ref[...]).astype(BF16)
            return carry

        lax.fori_loop(0, S // rc, combine, 0)

    def col(k):
        return pl.BlockSpec((S, HEAD_DIM), lambda h: (0, col0 + k * H + h))

    tab = pl.BlockSpec((S, HEAD_DIM), lambda h: (0, 0))
    out = pl.BlockSpec((S, HEAD_DIM), lambda h: (0, h))
    W = H * HEAD_DIM
    return _pcall(
        body, grid=(H,),
        in_specs=[col(0), col(1), col(2), tab, tab, pl.BlockSpec((1, HEAD_DIM), lambda h: (0, h))],
        out_specs=[out, out, out],
        out_shape=[jax.ShapeDtypeStruct((S, W), F32), jax.ShapeDtypeStruct((S, W), F32), jax.ShapeDtypeStruct((S, W), BF16)],
        scratch_shapes=[pltpu.VMEM((S, HEAD_DIM), F32)] * (2 + 2 * len(DILATIONS)),
        compiler_params=_params("parallel"), name=name)(proj, proj, proj, cos2, sin_signed, gain)


def _dil_bwd(proj, cos2, sin_signed, gain, o_raw, lse, dmixed, dm_col0, col0, n_heads, name):
    S = proj.shape[0]
    H, B = n_heads, KEY_BLOCK
    scale = HEAD_DIM ** -0.5
    rc = _tile(S, 256, 8)

    def body(q_ref, k_ref, v_ref, c_ref, s_ref, g_ref, o_ref, l_ref, dm_ref, dq_ref, dk_ref, dv_ref, dg_ref,
             qr, kr, dos, dsum, dqr, dkr, dvv):
        dg_ref[...] = jnp.zeros_like(dg_ref)

        def prep(t, carry):
            rows = pl.ds(pl.multiple_of(t * rc, rc), rc)
            qr[rows, :] = _rope(q_ref[rows, :], c_ref[rows, :], s_ref[rows, :])
            kr[rows, :] = _rope(k_ref[rows, :], c_ref[rows, :], s_ref[rows, :])
            o, dm = o_ref[rows, :], dm_ref[rows, :]
            r = _rms_scale(o)
            do = _rms_bwd(dm * g_ref[...], o, r)
            dg_ref[...] += jnp.broadcast_to(jnp.sum(dm * o * r, axis=0, keepdims=True), dg_ref.shape)
            dos[rows, :] = do
            dsum[rows, :] = jnp.broadcast_to(jnp.sum(do * o, axis=1, keepdims=True), (rc, LANES))
            dqr[rows, :] = jnp.zeros((rc, HEAD_DIM), F32)
            dkr[rows, :] = jnp.zeros((rc, HEAD_DIM), F32)
            dvv[rows, :] = jnp.zeros((rc, HEAD_DIM), F32)
            return carry

        lax.fori_loop(0, S // rc, prep, 0)

        def visit(b, d, r, l0, first):
            nk = B if first else 2 * B
            qrows = _dil_rows(d, r, l0, B)
            krows = qrows if first else _dil_rows(d, r, l0 - B, nk)
            qs, ks = qr[qrows, :].astype(BF16), kr[krows, :].astype(BF16)
            do = dos[qrows, :].astype(BF16)
            s = _dot(qs, ks, NT) * scale
            s = jnp.where(_dil_mask(first), s, NEG)
            p = jnp.exp(s - l_ref[qrows, :][:, 0:1])
            dp = _dot(do, v_ref[krows, :].astype(BF16), NT)
            ds = (p * (dp - dsum[qrows, :][:, 0:1]) * scale).astype(BF16)
            dqr[qrows, :] += _dot(ds, ks, NN)
            dkr[krows, :] += _dot(ds, qs, TN)
            dvv[krows, :] += _dot(p.astype(BF16), do, TN)

        _dil_blocks(S, visit)

        def finish(t, carry):
            rows = pl.ds(pl.multiple_of(t * rc, rc), rc)
            c, s = c_ref[rows, :], s_ref[rows, :]
            dq, dk = dqr[rows, :], dkr[rows, :]
            dq_ref[rows, :] = (dq * c + pltpu.roll(dq * s, HEAD_DIM // 2, axis=1)).astype(BF16)
            dk_ref[rows, :] = (dk * c + pltpu.roll(dk * s, HEAD_DIM // 2, axis=1)).astype(BF16)
            dv_ref[rows, :] = dvv[rows, :].astype(BF16)
            return carry

        lax.fori_loop(0, S // rc, finish, 0)

    def col(k):
        return pl.BlockSpec((S, HEAD_DIM), lambda h: (0, col0 + k * H + h))

    tab = pl.BlockSpec((S, HEAD_DIM), lambda h: (0, 0))
    out = pl.BlockSpec((S, HEAD_DIM), lambda h: (0, h))
    W = H * HEAD_DIM
    big = pltpu.VMEM((S, HEAD_DIM), F32)
    return _pcall(
        body, grid=(H,),
        in_specs=[col(0), col(1), col(2), tab, tab, pl.BlockSpec((1, HEAD_DIM), lambda h: (0, h)), out, out,
                  pl.BlockSpec((S, HEAD_DIM), lambda h: (0, dm_col0 + h))],
        out_specs=[out, out, out, pl.BlockSpec((8, HEAD_DIM), lambda h: (0, h))],
        out_shape=[jax.ShapeDtypeStruct((S, W), BF16), jax.ShapeDtypeStruct((S, W), BF16),
                   jax.ShapeDtypeStruct((S, W), BF16), jax.ShapeDtypeStruct((8, W), F32)],
        scratch_shapes=[big, big, big, pltpu.VMEM((S, LANES), F32), big, big, big],
        compiler_params=_params("parallel"), name=name)(proj, proj, proj, cos2, sin_signed, gain, o_raw, lse, dmixed)


GELU_C = math.sqrt(2.0 / math.pi)
GELU_A = 0.044715
HALO = 16


def _shift_down(cur, halo, k):
    out = pltpu.roll(cur, k, axis=0)
    row = lax.broadcasted_iota(jnp.int32, cur.shape, 0)
    for t in range(k):
        out = jnp.where(row == t, halo[HALO - k + t:HALO - k + t + 1, :], out)
    return out


def _shift_up(cur, halo, k):
    n = cur.shape[0]
    out = pltpu.roll(cur, n - k, axis=0)
    row = lax.broadcasted_iota(jnp.int32, cur.shape, 0)
    for t in range(k):
        out = jnp.where(row == n - k + t, halo[t:t + 1, :], out)
    return out


def _conv3(cur, halo, cw):
    return _shift_down(cur, halo, 2) * cw[0:1, :] + _shift_down(cur, halo, 1) * cw[1:2, :] + cur * cw[2:3, :] + cw[3:4, :]


def _gelu_parts(x):
    t = jnp.tanh(GELU_C * (x + GELU_A * x * x * x))
    return 0.5 * x * (1.0 + t), t


def _geglu_specs(tm, tn, ncb):
    hb = tm // HALO

    def cur(off):
        return pl.BlockSpec((tm, tn), lambda j, i: (i, off + j))

    def prev(off):
        return pl.BlockSpec((HALO, tn), lambda j, i: (jnp.maximum(i * hb - 1, 0), off + j))

    def taps(off):
        return pl.BlockSpec((8, tn), lambda j, i: (0, off + j))

    return [cur(0), prev(0), cur(ncb), prev(ncb), taps(0), taps(ncb)]


def _geglu_fwd(u, cwb, name, tm=256, tn=1408):
    S, F2 = u.shape
    F = F2 // 2
    tm, tn = _tile(S, tm, HALO), _tile(F, tn)
    ncb = F // tn

    def body(g_ref, gp_ref, v_ref, vp_ref, cg_ref, cv_ref, y_ref):
        top = pl.program_id(1) > 0
        gp = jnp.where(top, gp_ref[...].astype(F32), 0.0)
        vp = jnp.where(top, vp_ref[...].astype(F32), 0.0)
        gc = _conv3(g_ref[...].astype(F32), gp, cg_ref[...])
        vc = _conv3(v_ref[...].astype(F32), vp, cv_ref[...])
        y_ref[...] = (_gelu_parts(gc)[0] * vc).astype(BF16)

    return _pcall(body, grid=(ncb, S // tm), in_specs=_geglu_specs(tm, tn, ncb),
                  out_specs=pl.BlockSpec((tm, tn), lambda j, i: (i, j)),
                  out_shape=jax.ShapeDtypeStruct((S, F), BF16),
                  compiler_params=_params("parallel", "parallel"), name=name)(u, u, u, u, cwb, cwb)


def _geglu_bwd(u, dy, cwb, name, tm=256, tn=512):
    S, F2 = u.shape
    F = F2 // 2
    tm, tn = _tile(S, tm, HALO), _tile(F, tn)
    ncb = F // tn

    def body(g_ref, gp_ref, v_ref, vp_ref, cg_ref, cv_ref, dy_ref, dc_ref, dwg_ref, dwv_ref):
        i = pl.program_id(1)

        @pl.when(i == 0)
        def _():
            dwg_ref[...] = jnp.zeros_like(dwg_ref)
            dwv_ref[...] = jnp.zeros_like(dwv_ref)

        top = i > 0
        g, v = g_ref[...].astype(F32), v_ref[...].astype(F32)
        gp = jnp.where(top, gp_ref[...].astype(F32), 0.0)
        vp = jnp.where(top, vp_ref[...].astype(F32), 0.0)
        gc = _conv3(g, gp, cg_ref[...])
        vc = _conv3(v, vp, cv_ref[...])
        act, t = _gelu_parts(gc)
        dact = 0.5 * (1.0 + t) + 0.5 * gc * (1.0 - t * t) * GELU_C * (1.0 + 3.0 * GELU_A * gc * gc)
        dyv = dy_ref[...].astype(F32)
        dgc = dyv * vc * dact
        dvc = dyv * act
        dc_ref[0] = dgc.astype(BF16)
        dc_ref[1] = dvc.astype(BF16)

        def taps(out_ref, dc, cur, halo):
            out_ref[0:1, :] += jnp.sum(dc * _shift_down(cur, halo, 2), axis=0, keepdims=True)
            out_ref[1:2, :] += jnp.sum(dc * _shift_down(cur, halo, 1), axis=0, keepdims=True)
            out_ref[2:3, :] += jnp.sum(dc * cur, axis=0, keepdims=True)
            out_ref[3:4, :] += jnp.sum(dc, axis=0, keepdims=True)

        taps(dwg_ref, dgc, g, gp)
        taps(dwv_ref, dvc, v, vp)

    return _pcall(body, grid=(ncb, S // tm),
                  in_specs=_geglu_specs(tm, tn, ncb) + [pl.BlockSpec((tm, tn), lambda j, i: (i, j))],
                  out_specs=[pl.BlockSpec((2, tm, tn), lambda j, i: (0, i, j)),
                             pl.BlockSpec((8, tn), lambda j, i: (0, j)), pl.BlockSpec((8, tn), lambda j, i: (0, j))],
                  out_shape=[jax.ShapeDtypeStruct((2, S, F), BF16), jax.ShapeDtypeStruct((8, F), F32),
                             jax.ShapeDtypeStruct((8, F), F32)],
                  compiler_params=_params("parallel", "arbitrary"), name=name)(u, u, u, u, cwb, cwb, dy)


def _conv_bwd(dc, cwb, name, tm=512, tn=1408):
    _, S, F = dc.shape
    tm, tn = _tile(S, tm, HALO), _tile(F, tn)
    ncb, nrb = F // tn, S // tm
    hb = tm // HALO

    def body(c_ref, n_ref, w_ref, du_ref):
        cur = c_ref[...].astype(F32)
        nxt = jnp.where(pl.program_id(2) < nrb - 1, n_ref[...].astype(F32), 0.0)
        w = w_ref[...]
        du = cur * w[2:3, :] + _shift_up(cur, nxt, 1) * w[1:2, :] + _shift_up(cur, nxt, 2) * w[0:1, :]
        du_ref[...] = du.astype(BF16)

    return _pcall(body, grid=(2, ncb, nrb),
                  in_specs=[pl.BlockSpec((None, tm, tn), lambda c, j, i: (c, i, j)),
                            pl.BlockSpec((None, HALO, tn), lambda c, j, i: (c, jnp.minimum((i + 1) * hb, S // HALO - 1), j)),
                            pl.BlockSpec((8, tn), lambda c, j, i: (0, c * ncb + j))],
                  out_specs=pl.BlockSpec((tm, tn), lambda c, j, i: (i, c * ncb + j)),
                  out_shape=jax.ShapeDtypeStruct((S, 2 * F), BF16),
                  compiler_params=_params("parallel", "parallel", "parallel"), name=name)(dc, dc, cwb)


def _adam_math(w, g, m, v):
    m = ADAM_B1 * m + (1.0 - ADAM_B1) * g
    v = ADAM_B2 * v + (1.0 - ADAM_B2) * (g * g)
    m_hat = m / (1.0 - ADAM_B1 ** ADAM_STEP)
    v_hat = v / (1.0 - ADAM_B2 ** ADAM_STEP)
    return -ADAM_LR * (m_hat / (jnp.sqrt(v_hat) + ADAM_EPS) + ADAM_WD * w), m, v


def _adamw(w, parts, m, v, name, tr=256):
    R, C = w.shape
    n, _, Cp = parts.shape
    tr = _tile(R, tr, 8)

    def body(w_ref, p_ref, m_ref, v_ref, g_out, d_out, m_out, v_out):
        g = p_ref[0, :, 0:C].astype(F32)
        for k in range(1, n):
            g = g + p_ref[k, :, 0:C].astype(F32)
        d, mn, vn = _adam_math(w_ref[...], g, m_ref[...], v_ref[...])
        g_out[...] = g
        d_out[...] = d
        m_out[...] = mn
        v_out[...] = vn

    spec = pl.BlockSpec((tr, C), lambda i: (i, 0))
    shape = jax.ShapeDtypeStruct((R, C), F32)
    return _pcall(body, grid=(R // tr,), in_specs=[spec, pl.BlockSpec((n, tr, Cp), lambda i: (0, i, 0)), spec, spec],
                  out_specs=[spec] * 4, out_shape=[shape] * 4, compiler_params=_params("parallel"), name=name)(w, parts, m, v)


def _adamw_chips(w, pair, parts, chip_ids, m, v, name, tr=256):
    R, C = w.shape
    Cp = pair.shape[2]
    tr = _tile(R, tr, 16)

    def body(ids_ref, w_ref, own_ref, p1_ref, p2_ref, p3_ref, m_ref, v_ref, g_out, d_out, m_out, v_out):
        g = own_ref[:, 0:C].astype(F32)
        for ref in (p1_ref, p2_ref, p3_ref):
            g = g + ref[:, 0:C].astype(F32)
        d, mn, vn = _adam_math(w_ref[...], g, m_ref[...], v_ref[...])
        g_out[...] = g
        d_out[...] = d
        m_out[...] = mn
        v_out[...] = vn

    spec = pl.BlockSpec((tr, C), lambda i, ids: (i, 0))

    def chip(k):
        return pl.BlockSpec((None, tr, Cp), lambda i, ids: (ids[k], i, 0))

    shape = jax.ShapeDtypeStruct((R, C), F32)
    grid_spec = pltpu.PrefetchScalarGridSpec(
        num_scalar_prefetch=1, grid=(R // tr,), in_specs=[spec, chip(0), chip(1), chip(2), chip(3), spec, spec],
        out_specs=[spec] * 4)
    return _pcall(body, grid_spec=grid_spec, out_shape=[shape] * 4, compiler_params=_params("parallel"),
                  name=name)(chip_ids, w, pair, parts, parts, parts, m, v)


def _place():
    return lax.axis_index("x"), lax.axis_index("y"), lax.axis_index("c")


def _other_chips(x, y):
    return [(1 - x, y), (x, 1 - y), (1 - x, 1 - y)]


IN_HBM = pl.BlockSpec(memory_space=pltpu.HBM)
SEM = pl.BlockSpec(memory_space=pltpu.SEMAPHORE)
EFFECT = pltpu.SideEffectType.DATAFLOW_SIDE_EFFECTING
TOKEN = jax.ShapeDtypeStruct((8, LANES), F32)
TOKEN_SPEC = pl.BlockSpec(memory_space=pltpu.VMEM)


def _in_hbm(a):
    return pltpu.with_memory_space_constraint(a, pltpu.HBM)


def _landing(shape):
    return _in_hbm(lax.empty(shape.shape, shape.dtype))


def _hbm_like(a):
    return pltpu.HBM(a.shape, a.dtype)


def _gather_start(landing, slots, after, name):
    na = len(landing)

    def body(*refs):
        land = refs[:na]
        send_sems, recv_sems = refs[na + 1], refs[na + 2]
        token = refs[-1]
        x, y, c = _place()
        for a in range(na):
            own = slots[a](land[a], x, y, c)
            for k, to in enumerate([(x, y, 1 - c)] + [(*chip, c) for chip in _other_chips(x, y)]):
                pltpu.make_async_remote_copy(
                    src_ref=own, dst_ref=own, send_sem=send_sems.at[4 * a + k],
                    recv_sem=recv_sems.at[4 * a + k], device_id=to, device_id_type=MESH).start()
        token[...] = jnp.zeros_like(token)

    sems = pltpu.SemaphoreType.DMA((4 * na,))
    outs = _pcall(
        body, in_specs=[IN_HBM] * na + [HBM],
        out_specs=[SEM, SEM] + [IN_HBM] * na + [TOKEN_SPEC],
        out_shape=[sems, sems] + [_hbm_like(s) for s in landing] + [TOKEN],
        input_output_aliases={a: 2 + a for a in range(na)},
        compiler_params=pltpu.CompilerParams(has_side_effects=EFFECT), name=name,
    )(*[_in_hbm(s) for s in landing], after)
    return outs[0], outs[1], outs[2:2 + na], outs[-1]


def _gather_forward(gathered, send_sems, recv_sems, slots, after, name):
    na = len(gathered)

    def body(*refs):
        gath = refs[:na]
        send1, recv1 = refs[na], refs[na + 1]
        fsend, frecv = refs[na + 3], refs[na + 4]
        token = refs[-1]
        x, y, c = _place()
        chips = _other_chips(x, y)
        for a in range(na):
            for k, peer in enumerate([(x, y, 1 - c)] + [(*chip, c) for chip in chips]):
                arrival = pltpu.make_async_remote_copy(
                    src_ref=slots[a](gath[a], x, y, c), dst_ref=slots[a](gath[a], *peer), send_sem=send1.at[4 * a + k],
                    recv_sem=recv1.at[4 * a + k], device_id=peer, device_id_type=MESH)
                arrival.wait_send()
                arrival.wait_recv()
        for a in range(na):
            for j, chip in enumerate(chips):
                view = slots[a](gath[a], *chip, c)
                pltpu.make_async_remote_copy(
                    src_ref=view, dst_ref=view, send_sem=fsend.at[3 * a + j], recv_sem=frecv.at[3 * a + j],
                    device_id=(x, y, 1 - c), device_id_type=MESH).start()
        token[...] = jnp.zeros_like(token)

    sems = pltpu.SemaphoreType.DMA((3 * na,))
    outs = _pcall(
        body, in_specs=[IN_HBM] * na + [SEM, SEM, HBM],
        out_specs=[SEM, SEM] + [IN_HBM] * na + [TOKEN_SPEC],
        out_shape=[sems, sems] + [_hbm_like(g) for g in gathered] + [TOKEN],
        input_output_aliases={a: 2 + a for a in range(na)},
        compiler_params=pltpu.CompilerParams(has_side_effects=EFFECT), name=name,
    )(*gathered, send_sems, recv_sems, after)
    return outs[0], outs[1], outs[2:2 + na], outs[-1]


def _gather_finish(gathered, fsend, frecv, slots, after, name):
    na = len(gathered)

    def body(*refs):
        gath, fs, fr = refs[:na], refs[na], refs[na + 1]
        x, y, c = _place()
        for a in range(na):
            for j, chip in enumerate(_other_chips(x, y)):
                passed = pltpu.make_async_remote_copy(
                    src_ref=slots[a](gath[a], *chip, c), dst_ref=slots[a](gath[a], *chip, 1 - c),
                    send_sem=fs.at[3 * a + j], recv_sem=fr.at[3 * a + j], device_id=(x, y, 1 - c), device_id_type=MESH)
                passed.wait_send()
                passed.wait_recv()

    outs = _pcall(
        body, in_specs=[IN_HBM] * na + [SEM, SEM, HBM], out_specs=[IN_HBM] * na,
        out_shape=[_hbm_like(g) for g in gathered], input_output_aliases={a: a for a in range(na)},
        compiler_params=pltpu.CompilerParams(has_side_effects=EFFECT), name=name,
    )(*gathered, fsend, frecv, after)
    return list(outs)


def _pair_copy(view, src, land, send_sems, recv_sems, chip):
    x, y, c = _place()
    return pltpu.make_async_remote_copy(
        src_ref=view(src, chip, 1 - c), dst_ref=land.at[chip], send_sem=send_sems.at[chip], recv_sem=recv_sems.at[chip],
        device_id=(x, y, 1 - c), device_id_type=MESH)


def _pair_start(grad, view, block, after, name):
    def body(src, land, after_ref, send_sems, recv_sems, src_thru, land_thru, token):
        for chip in range(N_CHIP):
            _pair_copy(view, src, land, send_sems, recv_sems, chip).start()
        token[...] = jnp.zeros_like(token)

    sems = pltpu.SemaphoreType.DMA((N_CHIP,))
    land = jax.ShapeDtypeStruct((N_CHIP, *block), BF16)
    return _pcall(
        body, in_specs=[IN_HBM, IN_HBM, HBM], out_specs=[SEM, SEM, IN_HBM, IN_HBM, TOKEN_SPEC],
        out_shape=[sems, sems, _hbm_like(grad), _hbm_like(land), TOKEN], input_output_aliases={0: 2, 1: 3},
        compiler_params=pltpu.CompilerParams(has_side_effects=EFFECT), name=name,
    )(_in_hbm(grad), _landing(land), after)


def _pair_wait(grad, recv, send_sems, recv_sems, view, after, name):
    def body(src, land, send, recv_s, after_ref, src_thru, land_thru):
        for chip in range(N_CHIP):
            copy = _pair_copy(view, src, land, send, recv_s, chip)
            copy.wait_send()
            copy.wait_recv()

    return _pcall(
        body, in_specs=[IN_HBM, IN_HBM, SEM, SEM, HBM], out_specs=[IN_HBM, IN_HBM],
        out_shape=[_hbm_like(grad), _hbm_like(recv)], input_output_aliases={0: 0, 1: 1},
        compiler_params=pltpu.CompilerParams(has_side_effects=EFFECT), name=name,
    )(grad, recv, send_sems, recv_sems, after)


def _chip_start(pair, after, name):
    def body(src, land, after_ref, send_sems, recv_sems, src_thru, land_thru, token):
        x, y, c = _place()
        for j, (px, py) in enumerate(_other_chips(x, y)):
            pltpu.make_async_remote_copy(
                src_ref=src.at[2 * px + py], dst_ref=land.at[2 * x + y], send_sem=send_sems.at[j], recv_sem=recv_sems.at[j],
                device_id=(px, py, c), device_id_type=MESH).start()
        token[...] = jnp.zeros_like(token)

    sems = pltpu.SemaphoreType.DMA((3,))
    return _pcall(
        body, in_specs=[IN_HBM, IN_HBM, HBM], out_specs=[SEM, SEM, IN_HBM, IN_HBM, TOKEN_SPEC],
        out_shape=[sems, sems, _hbm_like(pair), _hbm_like(pair), TOKEN], input_output_aliases={0: 2, 1: 3},
        compiler_params=pltpu.CompilerParams(has_side_effects=EFFECT), name=name,
    )(_in_hbm(pair), _landing(pair), after)


def _chip_wait(pair, parts, send_sems, recv_sems, after, name):
    def body(src, land, send, recv, after_ref, src_thru, land_thru):
        x, y, c = _place()
        for j, (px, py) in enumerate(_other_chips(x, y)):
            copy = pltpu.make_async_remote_copy(
                src_ref=src.at[2 * px + py], dst_ref=land.at[2 * px + py], send_sem=send.at[j], recv_sem=recv.at[j],
                device_id=(px, py, c), device_id_type=MESH)
            copy.wait_send()
            copy.wait_recv()

    return _pcall(
        body, in_specs=[IN_HBM, IN_HBM, SEM, SEM, HBM], out_specs=[IN_HBM, IN_HBM],
        out_shape=[_hbm_like(pair), _hbm_like(parts)], input_output_aliases={0: 0, 1: 1},
        compiler_params=pltpu.CompilerParams(has_side_effects=EFFECT), name=name,
    )(pair, parts, send_sems, recv_sems, after)


def _pair_add(core, grad, recv, block, grad_spec, name):
    _, R, C = recv.shape
    tr = block

    def body(c_ref, g_ref, r_ref, o_ref):
        o_ref[...] = (g_ref[...].astype(F32) + r_ref[...].astype(F32)).astype(BF16)

    grid_spec = pltpu.PrefetchScalarGridSpec(
        num_scalar_prefetch=1, grid=(N_CHIP, R // tr),
        in_specs=[grad_spec, pl.BlockSpec((None, tr, C), lambda k, i, c: (k, i, 0))],
        out_specs=pl.BlockSpec((None, tr, C), lambda k, i, c: (k, i, 0)))
    return _pcall(body, grid_spec=grid_spec, out_shape=jax.ShapeDtypeStruct(recv.shape, BF16),
                  compiler_params=_params("parallel", "parallel"), name=name)(core, grad, recv)


def _small_step(parts, params, name):
    na, npar = len(parts), len(params)

    def body(*refs):
        p_refs, wmv = refs[:na], refs[na:na + 3 * npar]
        o_parts = refs[na + 3 * npar:2 * na + 3 * npar]
        o_params = refs[2 * na + 3 * npar:2 * na + 7 * npar]
        alls, (send_sems, recv_sems) = refs[2 * na + 7 * npar:3 * na + 7 * npar], refs[3 * na + 7 * npar:]
        x, y, c = _place()
        me = 4 * x + 2 * y + c
        peers = [(x, y, 1 - c)] + [(px, py, pc) for px, py in _other_chips(x, y) for pc in (c, 1 - c)]
        copies = []
        for a in range(na):
            alls[a][me] = p_refs[a][...]
            copies += [pltpu.make_async_remote_copy(
                src_ref=p_refs[a], dst_ref=alls[a].at[me], send_sem=send_sems.at[7 * a + k], recv_sem=recv_sems.at[7 * a + k],
                device_id=peer, device_id_type=MESH) for k, peer in enumerate(peers)]
        for cp in copies:
            cp.start()
        for a in range(na):
            for k, (px, py, pc) in enumerate(peers):
                pltpu.make_async_remote_copy(
                    src_ref=p_refs[a], dst_ref=alls[a].at[4 * px + 2 * py + pc], send_sem=send_sems.at[7 * a + k],
                    recv_sem=recv_sems.at[7 * a + k], device_id=peers[k], device_id_type=MESH).wait_recv()
        for cp in copies:
            cp.wait_send()
        sums = []
        for a in range(na):
            acc = alls[a][0]
            for k in range(1, N_DEV):
                acc = acc + alls[a][k]
            o_parts[a][...] = acc
            sums.append(acc)
        for j, (a, row, _, _, _) in enumerate(params):
            g = sums[a][row:row + 1, :]
            d, mn, vn = _adam_math(wmv[3 * j][...], g, wmv[3 * j + 1][...], wmv[3 * j + 2][...])
            for out, val in zip(o_params[4 * j:4 * j + 4], (g, d, mn, vn)):
                out[...] = val

    vm = pl.BlockSpec(memory_space=pltpu.VMEM)
    flat = [t for p in params for t in p[2:]]
    out_shape = [jax.ShapeDtypeStruct(p.shape, F32) for p in parts]
    out_shape += [jax.ShapeDtypeStruct(p[2].shape, F32) for p in params for _ in range(4)]
    outs = _pcall(body, in_specs=[vm] * (na + 3 * npar), out_specs=[vm] * len(out_shape), out_shape=out_shape,
                  scratch_shapes=[pltpu.VMEM((N_DEV, *p.shape), F32) for p in parts]
                  + [pltpu.SemaphoreType.DMA((7 * na,)), pltpu.SemaphoreType.DMA((7 * na,))],
                  name=name)(*parts, *flat)
    return outs[:na], [outs[na + 4 * j:na + 4 * j + 4] for j in range(npar)]


def _local_step(x, tgt, gains, weights):
    g_pre_mix, g_post_mix, g_pre_ffn, g_post_ffn, g_sb, g_dil = gains
    S, D = x.shape
    hs = g_sb.shape[1] // HEAD_DIM
    hd = g_dil.shape[1] // HEAD_DIM
    cos2, sin_signed = _rope_tables(S)

    h1 = _rms_fwd(x, g_pre_mix + weights.start(), "rms_in")
    w_in_g = weights.w_in(h1)
    proj = _mm_nn(h1, w_in_g, F32, "proj", tn=768)
    o_sb, ct_sb, mx_sb = _sb_fwd(proj, g_sb, hs, "sb_fwd")
    o_dl, lse_dl, mx_dl = _dil_fwd(proj, cos2, sin_signed, g_dil + weights.forward_out(o_sb), 3 * hs, hd, "dil_fwd")
    w_out_g, dep = weights.w_out(o_dl)
    mixed = jnp.concatenate([mx_sb, mx_dl], axis=1)
    mix = _mm_nn(mixed, w_out_g, F32, "mix_out", tn=1024)
    x2, h2 = _mid_fwd(x, mix, g_post_mix + dep, g_pre_ffn, "mid_fwd")
    w_up_g, cwb = weights.w_up(h2)
    u = _mm_nn(h2, w_up_g, BF16, "ffn_up", b_transposed=True)
    y = _geglu_fwd(u, cwb + weights.forward_down(u), "geglu_fwd")
    w_down_g = weights.w_down(y)
    f = _mm_nn(y, w_down_g, F32, "ffn_down", tn=1024, tk=1408)

    dy, df, dg_post_ffn, loss = _loss_bwd(x2, f, tgt, g_post_ffn, "loss_bwd")
    dyv = _mm_nt(df, w_down_g, BF16, "d_y", tn=1408)
    dw_down = _mm_tn(y, df, D, BF16, "dw_down", tm=1408, tn=1024)
    dc, dcw_g, dcw_v = _geglu_bwd(u, dyv, cwb + weights.grad("w_down", dw_down), "geglu_bwd")
    du = _conv_bwd(dc, cwb + weights.grad_reduce("w_down", dc), "conv_bwd")
    dh2 = _mm_nt(du, w_up_g, F32, "d_h2", tk=1408, b_transposed=True)
    dw_up = _mm_tn(du, h2, D, BF16, "dw_up", tm=1408, tn=1024)
    dx2, dmix, dg_pre_ffn, dg_post_mix = _mid_bwd(
        dy, dh2, x2, mix, g_pre_ffn + weights.grad("w_up", dw_up), g_post_mix, "mid_bwd")
    dmixed = _mm_nt(dmix, w_out_g, F32, "d_mixed", after=jnp.reshape(weights.grad_reduce("w_up", dmix), (1, 1)))
    dw_out = _mm_tn(mixed, dmix, D, BF16, "dw_out", tn=1024)
    dq_s, dk_s, dv_s, dg_sb = _sb_bwd(proj, g_sb + weights.grad("w_out", dw_out), o_sb, ct_sb, dmixed, 0, hs, "sb_bwd")
    dq_d, dk_d, dv_d, dg_dil = _dil_bwd(proj, cos2, sin_signed, g_dil + weights.grad_reduce("w_out", dq_s), o_dl, lse_dl,
                                        dmixed, hs, 3 * hs, hd, "dil_bwd")
    dproj = jnp.concatenate([dq_s, dk_s, dv_s, dq_d, dk_d, dv_d], axis=1)
    dh1 = _mm_nt(dproj, w_in_g, F32, "d_h1", tk=768)
    grad_x, dg_pre_mix = _first_bwd(dx2, dh1, x, g_pre_mix, "first_bwd")
    small = (dg_pre_mix, dg_post_mix, dg_pre_ffn, dg_post_ffn, dg_sb[0:1], dg_dil[0:1], jnp.concatenate([dcw_g, dcw_v], axis=1))
    dw_in = _mm_tn(h1, dproj, w_in_g.shape[2], BF16, "dw_in", tn=768, after=weights.small(small, loss))
    weights.grad("w_in", dw_in)
    weights.grad_reduce("w_in", grad_x)
    return loss, grad_x, small


def _pad_cols(a, to):
    return jnp.pad(a, ((0, 0), (0, to - a.shape[1])))


def kernel(x, pre_mix_gain, post_mix_gain, pre_ffn_gain, post_ffn_gain, w_in, sb_out_gain, dil_out_gain, w_out, w_up, conv_w, conv_b, w_down, loss_target, m_pre_mix_gain, m_post_mix_gain, m_pre_ffn_gain, m_post_ffn_gain, m_w_in, m_sb_out_gain, m_dil_out_gain, m_w_out, m_w_up, m_conv_w, m_conv_b, m_w_down, v_pre_mix_gain, v_post_mix_gain, v_pre_ffn_gain, v_post_ffn_gain, v_w_in, v_sb_out_gain, v_dil_out_gain, v_w_out, v_w_up, v_conv_w, v_conv_b, v_w_down):
    xb, tb = x[0], loss_target[0]
    S, D = xb.shape
    w_in, w_out, w_up, w_down, conv_w = w_in[0], w_out[0], w_up[0], w_down[0], conv_w[0]
    n_in, e_rows = w_in.shape[1], w_out.shape[0]
    cu, half = w_up.shape[1], w_down.shape[0]
    assert cu == 2 * half and half % 16 == 0
    cup = -(-cu // LANES) * LANES
    fp = N_CHIP * cup
    px, py, pc = _place()
    me = 4 * px + 2 * py + pc
    core = jnp.reshape(pc, (1,)).astype(jnp.int32)

    w_up_t, m_up_t, v_up_t = (jnp.swapaxes(t, 0, 1) for t in (w_up, m_w_up[0], v_w_up[0]))

    def by_dev(ref, qx, qy, qc):
        return ref.at[4 * qx + 2 * qy + qc]

    def down_slot(ref, qx, qy, qc):
        return ref.at[2 * qx + qy, pl.ds(qc * half, half)]

    def by_pair(ref, chip, k):
        return ref.at[chip, k]

    def down_pair(ref, chip, k):
        return ref.at[chip, pl.ds(k * half, half)]

    def pair_spec(tr, cols):
        return pl.BlockSpec((None, None, tr, cols), lambda k, i, c: (k, c[0], i, 0))

    tr_in, tr_up = _tile(D, 512, 16), _tile(cup, 256, 16)
    grad_plan = {
        "w_in": ((N_CHIP, 2, D, n_in), by_pair, (D, n_in), tr_in, pair_spec(tr_in, n_in)),
        "w_out": ((N_CHIP, 2, e_rows, D), by_pair, (e_rows, D), e_rows, pair_spec(e_rows, D)),
        "w_up": ((N_CHIP, 2, cup, D), by_pair, (cup, D), tr_up, pair_spec(tr_up, D)),
        "w_down": ((N_CHIP, cup, D), down_pair, (half, D), half,
                   pl.BlockSpec((None, half, D), lambda k, i, c: (k, c[0], 0))),
    }

    class Exchanges:
        def __init__(self):
            self.in_flight = {}

        def start(self):
            def own_slot(shard):
                return lax.dynamic_update_index_in_dim(lax.empty((N_DEV, *shard.shape), shard.dtype), shard, me, 0)

            self.g_in = _gather_start([own_slot(w_in.astype(BF16))], [by_dev], core, "gather_in_start")
            zero = self.g_in[3][0, 0]
            self.g_out = _gather_start([own_slot((w_out + zero).astype(BF16))], [by_dev], self.g_in[3], "gather_out_start")
            up = jnp.pad(w_up_t + zero, ((0, cup - cu), (0, 0))).astype(BF16)
            taps = jnp.pad(conv_w + zero, ((0, 8 - conv_w.shape[0]), (0, cup - cu)))
            self.g_up = _gather_start([own_slot(up), own_slot(taps)], [by_dev, by_dev], self.g_out[3], "gather_up_start")
            down = lax.dynamic_update_slice(jnp.zeros((N_CHIP, cup, D), BF16), (w_down + zero).astype(BF16)[None],
                                            (2 * px + py, pc * half, 0))
            self.g_down = _gather_start([down], [down_slot], self.g_up[3], "gather_down_start")
            return self.g_down[3][0, 0]

        def w_in(self, after):
            send, recv, gath, _ = self.g_in
            fsend, frecv, gath, token = _gather_forward(gath, send, recv, [by_dev], after, "gather_in_forward")
            return _gather_finish(gath, fsend, frecv, [by_dev], token, "gather_in_finish")[0]

        def forward_out(self, after):
            send, recv, gath, _ = self.g_out
            self.p_out = _gather_forward(gath, send, recv, [by_dev], after, "gather_out_forward")
            return self.p_out[3][0, 0]

        def w_out(self, after):
            fsend, frecv, gath, _ = self.p_out
            w_out_g = _gather_finish(gath, fsend, frecv, [by_dev], after, "gather_out_finish")[0]
            send, recv, gath, _ = self.g_up
            self.p_up = _gather_forward(gath, send, recv, [by_dev, by_dev], w_out_g, "gather_up_forward")
            return w_out_g.reshape(1, N_DEV * e_rows, D), self.p_up[3][0, 0]

        def w_up(self, after):
            fsend, frecv, gath, _ = self.p_up
            w_up_g, cw_g = _gather_finish(gath, fsend, frecv, [by_dev, by_dev], after, "gather_up_finish")
            cb = _pad_cols(conv_b.reshape(N_DEV, cu), cup).reshape(1, 2 * fp)
            cw_full = jnp.transpose(cw_g[:, :3, :], (1, 0, 2)).reshape(3, 2 * fp)
            cwb = jnp.concatenate([cw_full, cb, jnp.zeros((4, 2 * fp), F32)], axis=0)
            return w_up_g, cwb

        def forward_down(self, after):
            send, recv, gath, _ = self.g_down
            self.p_down = _gather_forward(gath, send, recv, [down_slot], after, "gather_down_forward")
            return self.p_down[3][0, 0]

        def w_down(self, after):
            fsend, frecv, gath, _ = self.p_down
            return _gather_finish(gath, fsend, frecv, [down_slot], after, "gather_down_finish")[0].reshape(1, fp, D)

        def small(self, small, loss):
            d_pre_mix, d_post_mix, d_pre_ffn, d_post_ffn, d_sb, d_dil, d_conv = small

            def rows_of(*vectors):
                n = vectors[0].shape[1]
                row = lax.broadcasted_iota(jnp.int32, (8, n), 0)
                out = jnp.zeros((8, n), F32)
                for k, vec in enumerate(vectors):
                    out = jnp.where(row == k, vec, out)
                return out

            parts = [rows_of(d_pre_mix, d_post_mix, d_pre_ffn, d_post_ffn, jnp.broadcast_to(loss[:, :1], (1, D))),
                     rows_of(d_sb, d_dil), d_conv]
            params = [(0, 0, pre_mix_gain, m_pre_mix_gain, v_pre_mix_gain), (0, 1, post_mix_gain, m_post_mix_gain, v_post_mix_gain),
                      (0, 2, pre_ffn_gain, m_pre_ffn_gain, v_pre_ffn_gain), (0, 3, post_ffn_gain, m_post_ffn_gain, v_post_ffn_gain),
                      (1, 0, sb_out_gain, m_sb_out_gain, v_sb_out_gain), (1, 1, dil_out_gain, m_dil_out_gain, v_dil_out_gain)]
            (gains_sum, _, self.conv_sum), self.gain_steps = _small_step(parts, params, "small_step")
            self.loss_sum = gains_sum[4, 0]
            return self.conv_sum

        def grad(self, name, dw):
            view_shape, view, block, tr, spec = grad_plan[name]
            send, recv_sems, dw, recv, token = _pair_start(dw.reshape(view_shape), view, block, core, "pair_start_" + name)
            self.in_flight[name] = (dw, recv, send, recv_sems)
            return token[0, 0]

        def grad_reduce(self, name, after):
            _, view, _, tr, spec = grad_plan[name]
            dw, recv = _pair_wait(*self.in_flight[name], view, after, "pair_wait_" + name)
            pair = _pair_add(core, dw, recv, tr, spec, "pair_add_" + name)
            send, recv_sems, pair, parts, token = _chip_start(pair, recv, "chip_start_" + name)
            self.in_flight[name] = (pair, parts, send, recv_sems)
            self.last_token = token
            return token[0, 0]

        def grad_parts(self, name, after):
            return _chip_wait(*self.in_flight[name], after, "chip_wait_" + name)

    exchanges = Exchanges()
    gains = (pre_mix_gain, post_mix_gain, pre_ffn_gain, post_ffn_gain, sb_out_gain, dil_out_gain)
    loss, grad_x, small = _local_step(xb, tb, gains, exchanges)

    loss_out, g_conv = exchanges.loss_sum, exchanges.conv_sum
    g_conv_b = g_conv[3].reshape(N_DEV, cup)[:, :cu].reshape(1, N_DEV * cu)
    g_conv_w = lax.dynamic_index_in_dim(g_conv[0:3].reshape(3, N_DEV, cup), me, axis=1, keepdims=False)[:, :cu]

    def small_adam(w, g, m, v, name):
        one = w.shape[0] == 1
        if one:
            w, g, m, v = (jnp.broadcast_to(t, (8, t.shape[1])) for t in (w, g, m, v))
        outs = _adamw(w, g[None], m, v, name)
        return [o[0:1] for o in outs] if one else outs

    chip_ids = jnp.stack([2 * px + py, 2 * (1 - px) + py, 2 * px + 1 - py, 2 * (1 - px) + 1 - py]).astype(jnp.int32)
    out_w_down = _adamw_chips(w_down, *exchanges.grad_parts("w_down", exchanges.last_token), chip_ids, m_w_down[0], v_w_down[0], "adam_w_down")
    out_up_t = _adamw_chips(w_up_t, *exchanges.grad_parts("w_up", out_w_down[1]), chip_ids, m_up_t, v_up_t, "adam_w_up")
    out_w_up = [jnp.swapaxes(o, 0, 1) for o in out_up_t]
    out_w_out = _adamw_chips(w_out, *exchanges.grad_parts("w_out", out_up_t[1]), chip_ids, m_w_out[0], v_w_out[0], "adam_w_out")
    out_w_in = _adamw_chips(w_in, *exchanges.grad_parts("w_in", out_w_out[1]), chip_ids, m_w_in[0], v_w_in[0], "adam_w_in")
    out_pre_mix, out_post_mix, out_pre_ffn, out_post_ffn, out_sb, out_dil = exchanges.gain_steps
    out_conv_b = small_adam(conv_b, g_conv_b, m_conv_b, v_conv_b, "adam_conv_b")
    cw8 = [jnp.pad(t, ((0, 5), (0, 0))) for t in (conv_w, g_conv_w, m_conv_w[0], v_conv_w[0])]
    out_conv_w = [o[0:3] for o in _adamw(cw8[0], cw8[1][None], cw8[2], cw8[3], "adam_conv_w")]

    order = [out_pre_mix, out_post_mix, out_pre_ffn, out_post_ffn, [o[None] for o in out_w_in], out_sb, out_dil,
             [o[None] for o in out_w_out], [o[None] for o in out_w_up], [o[None] for o in out_conv_w], out_conv_b,
             [o[None] for o in out_w_down]]
    outs = [loss_out, grad_x[None]]
    for k in range(4):
        outs += [o[k] for o in order]
    return tuple(outs)
```

```python
import functools
import math

import jax
import jax.numpy as jnp
from jax import lax
from jax.experimental import pallas as pl
from jax.experimental.pallas import tpu as pltpu

F32 = jnp.float32
BF16 = jnp.bfloat16
HEAD_DIM = 128
LANES = 128
KEY_BLOCK = 128
DILATIONS = (1, 4, 16)
RMS_EPS = 1e-6
ROPE_THETA = 10000.0
NEG = -1e30
ADAM_LR, ADAM_B1, ADAM_B2, ADAM_EPS, ADAM_WD, ADAM_STEP = 0.001, 0.9, 0.999, 1e-08, 0.01, 10
MESH = pl.DeviceIdType.MESH
N_DEV = 8
N_CHIP = 4
HBM = pl.BlockSpec(memory_space=pl.ANY)
VMEM_LIMIT = 56 * 1024 * 1024

_pcall = pl.pallas_call


def _tile(n, pref, mult=LANES):
    best = None
    t = mult
    while t <= min(n, pref):
        if n % t == 0:
            best = t
        t += mult
    return n if best is None else best


def _params(*sem):
    return pltpu.CompilerParams(dimension_semantics=sem, vmem_limit_bytes=VMEM_LIMIT)


def _dot(a, b, dims):
    return lax.dot_general(a, b, (dims, ((), ())), preferred_element_type=F32)


NN = ((1,), (0,))
NT = ((1,), (1,))
TN = ((0,), (0,))


def _mm_body(dims, nk, tile):
    if nk == 1:
        def single(a_ref, b_ref, o_ref):
            o_ref[...] = _dot(a_ref[...].astype(BF16), b_ref[...].astype(BF16), dims).astype(o_ref.dtype)

        return single, []

    def body(a_ref, b_ref, o_ref, acc_ref):
        k = pl.program_id(2)

        @pl.when(k == 0)
        def _():
            acc_ref[...] = jnp.zeros_like(acc_ref)

        acc_ref[...] += _dot(a_ref[...].astype(BF16), b_ref[...].astype(BF16), dims)

        @pl.when(k == nk - 1)
        def _():
            o_ref[...] = acc_ref[...].astype(o_ref.dtype)

    return body, [pltpu.VMEM(tile, F32)]


def _mm_nn(a, b3, out_dtype, name, tm=1024, tn=1408, tk=2048, b_transposed=False):
    M, K = a.shape
    C, n = b3.shape[0], b3.shape[1 if b_transposed else 2]
    tm, tk, tn = _tile(M, tm, 8), _tile(K, tk), _tile(n, tn)
    npc, nk = n // tn, K // tk
    body, scratch = _mm_body(NT if b_transposed else NN, nk, (tm, tn))
    b_spec = (pl.BlockSpec((None, tn, tk), lambda i, j, k: (j // npc, j % npc, k)) if b_transposed
              else pl.BlockSpec((None, tk, tn), lambda i, j, k: (j // npc, k, j % npc)))
    return _pcall(
        body, grid=(M // tm, C * npc, nk),
        in_specs=[pl.BlockSpec((tm, tk), lambda i, j, k: (i, k)), b_spec],
        out_specs=pl.BlockSpec((tm, tn), lambda i, j, k: (i, j)),
        out_shape=jax.ShapeDtypeStruct((M, C * n), out_dtype), scratch_shapes=scratch,
        compiler_params=_params("parallel", "parallel", "arbitrary"), name=name)(a, b3)


def _mm_nt(a, b3, out_dtype, name, tm=1024, tn=1024, tk=2048, after=None, b_transposed=False):
    M, _ = a.shape
    C, N, n = (b3.shape[0], b3.shape[2], b3.shape[1]) if b_transposed else b3.shape
    tm, tn, tk = _tile(M, tm, 8), _tile(N, tn), _tile(n, tk)
    kpc = n // tk
    nk = C * kpc
    inner, scratch = _mm_body(NN if b_transposed else NT, nk, (tm, tn))
    extra = [] if after is None else [after]

    def body(a_ref, b_ref, *rest):
        inner(a_ref, b_ref, *rest[len(extra):])

    b_spec = (pl.BlockSpec((None, tk, tn), lambda i, j, k: (k // kpc, k % kpc, j)) if b_transposed
              else pl.BlockSpec((None, tn, tk), lambda i, j, k: (k // kpc, j, k % kpc)))
    return _pcall(
        body, grid=(M // tm, N // tn, nk),
        in_specs=[pl.BlockSpec((tm, tk), lambda i, j, k: (i, k)), b_spec] + [HBM] * len(extra),
        out_specs=pl.BlockSpec((tm, tn), lambda i, j, k: (i, j)),
        out_shape=jax.ShapeDtypeStruct((M, N), out_dtype), scratch_shapes=scratch,
        compiler_params=_params("parallel", "parallel", "arbitrary"), name=name)(a, b3, *extra)


def _mm_tn(x, y, n, out_dtype, name, tm=1024, tn=1408, tk=2048, after=None):
    S, P = x.shape
    C = y.shape[1] // n
    tm, tn, tk = _tile(P, tm), _tile(n, tn), _tile(S, tk, 8)
    npc, nk = n // tn, S // tk
    inner, scratch = _mm_body(TN, nk, (tm, tn))
    extra = [] if after is None else [after]

    def body(x_ref, y_ref, *rest):
        inner(x_ref, y_ref, *rest[len(extra):])

    return _pcall(
        body, grid=(P // tm, C * npc, nk),
        in_specs=[pl.BlockSpec((tk, tm), lambda i, j, k: (k, i)),
                  pl.BlockSpec((tk, tn), lambda i, j, k: (k, j))] + [HBM] * len(extra),
        out_specs=pl.BlockSpec((None, tm, tn), lambda i, j, k: (j // npc, i, j % npc)),
        out_shape=jax.ShapeDtypeStruct((C, P, n), out_dtype), scratch_shapes=scratch,
        compiler_params=_params("parallel", "parallel", "arbitrary"), name=name)(x, y, *extra)


def _rms_scale(v):
    return lax.rsqrt(jnp.mean(v * v, axis=-1, keepdims=True) + RMS_EPS)


def _rms_bwd(gy, v, r):
    return r * gy - v * (r * r * r * jnp.mean(gy * v, axis=-1, keepdims=True))


def _rows_spec(tm, d):
    return pl.BlockSpec((tm, d), lambda i: (i, 0))


def _vec_spec(d):
    return pl.BlockSpec((1, d), lambda i: (0, 0))


def _rms_fwd(x, g, name, tm=256):
    S, D = x.shape

    def body(x_ref, g_ref, h_ref):
        v = x_ref[...]
        h_ref[...] = (v * _rms_scale(v) * g_ref[...]).astype(BF16)

    return _pcall(body, grid=(S // tm,), in_specs=[_rows_spec(tm, D), _vec_spec(D)], out_specs=_rows_spec(tm, D),
                  out_shape=jax.ShapeDtypeStruct((S, D), BF16), compiler_params=_params("parallel"), name=name)(x, g)


def _mid_fwd(x, mix, g_post, g_pre, name, tm=256):
    S, D = x.shape

    def body(x_ref, m_ref, gp_ref, gn_ref, x2_ref, h_ref):
        m = m_ref[...]
        x2 = x_ref[...] + m * _rms_scale(m) * gp_ref[...]
        x2_ref[...] = x2
        h_ref[...] = (x2 * _rms_scale(x2) * gn_ref[...]).astype(BF16)

    return _pcall(body, grid=(S // tm,), in_specs=[_rows_spec(tm, D), _rows_spec(tm, D), _vec_spec(D), _vec_spec(D)],
                  out_specs=[_rows_spec(tm, D), _rows_spec(tm, D)],
                  out_shape=[jax.ShapeDtypeStruct((S, D), F32), jax.ShapeDtypeStruct((S, D), BF16)],
                  compiler_params=_params("parallel"), name=name)(x, mix, g_post, g_pre)


def _loss_bwd(x2, f, tgt, g_post, name, tm=256):
    S, D = x2.shape

    def body(x2_ref, f_ref, t_ref, g_ref, dy_ref, df_ref, dg_ref, ls_ref):
        i = pl.program_id(0)

        @pl.when(i == 0)
        def _():
            dg_ref[...] = jnp.zeros_like(dg_ref)
            ls_ref[...] = jnp.zeros_like(ls_ref)

        fv = f_ref[...]
        r = _rms_scale(fv)
        g = g_ref[...]
        err = x2_ref[...] + fv * r * g - t_ref[...]
        ls_ref[...] += jnp.broadcast_to(0.5 * jnp.sum(jnp.mean(err * err, axis=-1, keepdims=True), axis=0, keepdims=True), ls_ref.shape)
        dy = err * (1.0 / D)
        dy_ref[...] = dy
        df_ref[...] = _rms_bwd(dy * g, fv, r).astype(BF16)
        dg_ref[...] += jnp.sum(dy * fv * r, axis=0, keepdims=True)

    return _pcall(body, grid=(S // tm,),
                  in_specs=[_rows_spec(tm, D), _rows_spec(tm, D), _rows_spec(tm, D), _vec_spec(D)],
                  out_specs=[_rows_spec(tm, D), _rows_spec(tm, D), _vec_spec(D), _vec_spec(LANES)],
                  out_shape=[jax.ShapeDtypeStruct((S, D), F32), jax.ShapeDtypeStruct((S, D), BF16),
                             jax.ShapeDtypeStruct((1, D), F32), jax.ShapeDtypeStruct((1, LANES), F32)],
                  compiler_params=_params("arbitrary"), name=name)(x2, f, tgt, g_post)


def _mid_bwd(dy, dh2, x2, mix, g_pre, g_post, name, tm=256):
    S, D = dy.shape

    def body(dy_ref, dh_ref, x2_ref, m_ref, gn_ref, gp_ref, dx2_ref, dm_ref, dgn_ref, dgp_ref):
        i = pl.program_id(0)

        @pl.when(i == 0)
        def _():
            dgn_ref[...] = jnp.zeros_like(dgn_ref)
            dgp_ref[...] = jnp.zeros_like(dgp_ref)

        x2, dh = x2_ref[...], dh_ref[...]
        r = _rms_scale(x2)
        dx2 = dy_ref[...] + _rms_bwd(dh * gn_ref[...], x2, r)
        dgn_ref[...] += jnp.sum(dh * x2 * r, axis=0, keepdims=True)
        dx2_ref[...] = dx2
        m = m_ref[...]
        rm = _rms_scale(m)
        dm_ref[...] = _rms_bwd(dx2 * gp_ref[...], m, rm).astype(BF16)
        dgp_ref[...] += jnp.sum(dx2 * m * rm, axis=0, keepdims=True)

    return _pcall(body, grid=(S // tm,),
                  in_specs=[_rows_spec(tm, D)] * 4 + [_vec_spec(D)] * 2,
                  out_specs=[_rows_spec(tm, D), _rows_spec(tm, D), _vec_spec(D), _vec_spec(D)],
                  out_shape=[jax.ShapeDtypeStruct((S, D), F32), jax.ShapeDtypeStruct((S, D), BF16),
                             jax.ShapeDtypeStruct((1, D), F32), jax.ShapeDtypeStruct((1, D), F32)],
                  compiler_params=_params("arbitrary"), name=name)(dy, dh2, x2, mix, g_pre, g_post)


def _first_bwd(dx2, dh1, x, g_pre, name, tm=256):
    S, D = x.shape

    def body(dx2_ref, dh_ref, x_ref, g_ref, gx_ref, dg_ref):
        i = pl.program_id(0)

        @pl.when(i == 0)
        def _():
            dg_ref[...] = jnp.zeros_like(dg_ref)

        xv, dh = x_ref[...], dh_ref[...]
        r = _rms_scale(xv)
        gx_ref[...] = dx2_ref[...] + _rms_bwd(dh * g_ref[...], xv, r)
        dg_ref[...] += jnp.sum(dh * xv * r, axis=0, keepdims=True)

    return _pcall(body, grid=(S // tm,), in_specs=[_rows_spec(tm, D)] * 3 + [_vec_spec(D)],
                  out_specs=[_rows_spec(tm, D), _vec_spec(D)],
                  out_shape=[jax.ShapeDtypeStruct((S, D), F32), jax.ShapeDtypeStruct((1, D), F32)],
                  compiler_params=_params("arbitrary"), name=name)(dx2, dh1, x, g_pre)


def _logsig_pair(z):
    lb = jnp.minimum(z, 0.0) - jnp.log(1.0 + jnp.exp(-jnp.abs(z)))
    return lb, lb - z


SB_KEY_BLOCK = 256


def _sum_matrix(strict):
    ia = lax.broadcasted_iota(jnp.int32, (SB_KEY_BLOCK, SB_KEY_BLOCK), 0)
    ib = lax.broadcasted_iota(jnp.int32, (SB_KEY_BLOCK, SB_KEY_BLOCK), 1)
    return ((ia > ib) if strict == ">" else (ia < ib)).astype(BF16)


def _row_total(sums, v, col):
    return jnp.broadcast_to(sums[:, col:col + 1] + v[:, col:col + 1], (v.shape[0], LANES))


def _lanes(c, width):
    return jnp.tile(c, (1, width // LANES))


def _split_dot(v, u):
    hi = v.astype(BF16)
    lo = (v - hi.astype(F32)).astype(BF16)
    return _dot(hi, u, NN) + _dot(lo, u, NN)


def _head_out(o, g):
    return o * _rms_scale(o) * g


def _sb_fwd(proj, gain, n_heads, name, tq=1024):
    S = proj.shape[0]
    H, tk = n_heads, SB_KEY_BLOCK
    tq = _tile(S, tq, 2 * tk)
    scale = HEAD_DIM ** -0.5

    def body(q_ref, k_ref, v_ref, g_ref, o_ref, ct_ref, mx_ref, oacc, cacc):
        i = pl.program_id(1)
        oacc[...] = jnp.zeros_like(oacc)
        cacc[...] = jnp.zeros_like(cacc)
        sums = _sum_matrix(">")

        def block(k0, r0, diagonal):
            rows = pl.ds(r0, tq - r0)
            q = q_ref[rows, :].astype(BF16)
            kj = k_ref[pl.ds(k0, tk), :].astype(BF16)
            vj = v_ref[pl.ds(k0, tk), :].astype(BF16)
            lb, lk = _logsig_pair(_dot(q, kj, NT) * scale)
            if diagonal:
                causal = (lax.broadcasted_iota(jnp.int32, (tq - r0, tk), 1) < lax.broadcasted_iota(jnp.int32, (tq - r0, tk), 0))
                lk = jnp.where(causal, lk, 0.0)
            after = _split_dot(lk, sums)
            c = cacc[rows, :]
            a = jnp.exp(lb + after + _lanes(c, tk))
            if diagonal:
                a = jnp.where(causal, a, 0.0)
            oacc[rows, :] += _dot(a.astype(BF16), vj, NN)
            cacc[rows, :] = c + _row_total(after, lk, 0)

        for d in reversed(range(tq // tk)):
            block(pl.multiple_of(i * tq + d * tk, tk), d * tk, True)
        n_pairs = i * (tq // tk // 2)

        def step(it, carry):
            k0 = pl.multiple_of((n_pairs - 1 - it) * 2 * tk, 2 * tk)
            block(pl.multiple_of(k0 + tk, tk), 0, False)
            block(k0, 0, False)
            return carry

        lax.fori_loop(0, n_pairs, step, 0)
        o = oacc[...]
        o_ref[...] = o
        ct_ref[...] = cacc[...]
        mx_ref[...] = _head_out(o, g_ref[...]).astype(BF16)

    blk = pl.BlockSpec((tq, HEAD_DIM), lambda h, i: (i, h))
    return _pcall(
        body, grid=(H, S // tq),
        in_specs=[blk, pl.BlockSpec((S, HEAD_DIM), lambda h, i: (0, H + h)),
                  pl.BlockSpec((S, HEAD_DIM), lambda h, i: (0, 2 * H + h)), pl.BlockSpec((1, HEAD_DIM), lambda h, i: (0, h))],
        out_specs=[blk, blk, blk],
        out_shape=[jax.ShapeDtypeStruct((S, H * HEAD_DIM), F32), jax.ShapeDtypeStruct((S, H * HEAD_DIM), F32),
                   jax.ShapeDtypeStruct((S, H * HEAD_DIM), BF16)],
        scratch_shapes=[pltpu.VMEM((tq, HEAD_DIM), F32), pltpu.VMEM((tq, LANES), F32)],
        compiler_params=_params("parallel", "arbitrary"), name=name)(proj, proj, proj, gain)


def _sb_bwd(proj, gain, o_raw, ctot, dmixed, dm_col0, n_heads, name, tq=1024):
    S = proj.shape[0]
    H, tk = n_heads, SB_KEY_BLOCK
    tq = _tile(S, tq, 2 * tk)
    nq = S // tq
    scale = HEAD_DIM ** -0.5

    def body(q_ref, k_ref, v_ref, g_ref, o_ref, ct_ref, dm_ref, dq_ref, dk_ref, dv_ref, dg_ref,
             dkacc, dvacc, dqacc, pfx, gcar, dos):
        i = pl.program_id(1)

        @pl.when(i == 0)
        def _():
            dkacc[...] = jnp.zeros_like(dkacc)
            dvacc[...] = jnp.zeros_like(dvacc)
            dg_ref[...] = jnp.zeros_like(dg_ref)

        o, dm, g = o_ref[...], dm_ref[...], g_ref[...]
        r = _rms_scale(o)
        dos[...] = _rms_bwd(dm * g, o, r).astype(BF16)
        dg_ref[...] += jnp.broadcast_to(jnp.sum(dm * o * r, axis=0, keepdims=True), dg_ref.shape)
        dqacc[...] = jnp.zeros_like(dqacc)
        pfx[...] = jnp.zeros_like(pfx)
        gcar[...] = jnp.zeros_like(gcar)
        later, earlier = _sum_matrix(">"), _sum_matrix("<")

        def block(k0, r0, diagonal):
            rows = pl.ds(r0, tq - r0)
            keys = pl.ds(k0, tk)
            q, do = q_ref[rows, :].astype(BF16), dos[rows, :]
            kj, vj = k_ref[keys, :].astype(BF16), v_ref[keys, :].astype(BF16)
            lb, lk = _logsig_pair(_dot(q, kj, NT) * scale)
            if diagonal:
                causal = (lax.broadcasted_iota(jnp.int32, (tq - r0, tk), 1) < lax.broadcasted_iota(jnp.int32, (tq - r0, tk), 0))
                lk = jnp.where(causal, lk, 0.0)
            after = _split_dot(lk, later)
            p = pfx[rows, :] + _row_total(after, lk, 0)
            a = jnp.exp(lb + after + _lanes(ct_ref[rows, :] - p, tk))
            if diagonal:
                a = jnp.where(causal, a, 0.0)
            dl = _dot(do, vj, NT) * a
            dvacc[keys, :] += _dot(a.astype(BF16), do, TN)
            before = _dot(dl.astype(BF16), earlier, NN)
            gc = gcar[rows, :]
            sig = jnp.exp(lb)
            gsum = (before + _lanes(gc, tk)) * sig
            if diagonal:
                gsum = jnp.where(causal, gsum, 0.0)
            dz = ((dl * (1.0 - sig) - gsum) * scale).astype(BF16)
            dqacc[rows, :] += _dot(dz, kj, NN)
            dkacc[keys, :] += _dot(dz, q, TN)
            pfx[rows, :] = p
            gcar[rows, :] = gc + _row_total(before, dl, tk - 1)

        def step(j, carry):
            k0 = pl.multiple_of(j * 2 * tk, 2 * tk)
            block(k0, 0, False)
            block(pl.multiple_of(k0 + tk, tk), 0, False)
            return carry

        lax.fori_loop(0, i * (tq // tk // 2), step, 0)
        for d in range(tq // tk):
            block(pl.multiple_of(i * tq + d * tk, tk), d * tk, True)
        dq_ref[...] = dqacc[...].astype(BF16)

        @pl.when(i == nq - 1)
        def _():
            dk_ref[...] = dkacc[...].astype(BF16)
            dv_ref[...] = dvacc[...].astype(BF16)

    blk = pl.BlockSpec((tq, HEAD_DIM), lambda h, i: (i, h))
    full = pl.BlockSpec((S, HEAD_DIM), lambda h, i: (0, h))
    W = H * HEAD_DIM
    return _pcall(
        body, grid=(H, nq),
        in_specs=[blk, pl.BlockSpec((S, HEAD_DIM), lambda h, i: (0, H + h)),
                  pl.BlockSpec((S, HEAD_DIM), lambda h, i: (0, 2 * H + h)), pl.BlockSpec((1, HEAD_DIM), lambda h, i: (0, h)),
                  blk, blk, pl.BlockSpec((tq, HEAD_DIM), lambda h, i: (i, dm_col0 + h))],
        out_specs=[blk, full, full, pl.BlockSpec((8, HEAD_DIM), lambda h, i: (0, h))],
        out_shape=[jax.ShapeDtypeStruct((S, W), BF16), jax.ShapeDtypeStruct((S, W), BF16),
                   jax.ShapeDtypeStruct((S, W), BF16), jax.ShapeDtypeStruct((8, W), F32)],
        scratch_shapes=[pltpu.VMEM((S, HEAD_DIM), F32), pltpu.VMEM((S, HEAD_DIM), F32), pltpu.VMEM((tq, HEAD_DIM), F32),
                        pltpu.VMEM((tq, LANES), F32), pltpu.VMEM((tq, LANES), F32), pltpu.VMEM((tq, HEAD_DIM), BF16)],
        compiler_params=_params("arbitrary", "arbitrary"), name=name)(proj, proj, proj, gain, o_raw, ctot, dmixed)


def _rope_tables(S):
    inv_freq = ROPE_THETA ** (-jnp.arange(0, HEAD_DIM, 2, dtype=F32) / HEAD_DIM)
    ang = jnp.arange(S, dtype=F32)[:, None] * inv_freq[None, :]
    cos, sin = jnp.cos(ang), jnp.sin(ang)
    return jnp.concatenate([cos, cos], axis=1), jnp.concatenate([-sin, sin], axis=1)


def _rope(v, cos2, sin_signed):
    return v * cos2 + pltpu.roll(v, HEAD_DIM // 2, axis=1) * sin_signed


def _dil_rows(d, r, l0, n):
    if d == 1:
        return pl.ds(l0 if isinstance(l0, int) else pl.multiple_of(l0, KEY_BLOCK), n)
    return pl.ds(r + d * l0, n, stride=d)


def _dil_blocks(S, visit):
    B = KEY_BLOCK
    group = 4
    for b, d in enumerate(DILATIONS):
        nb = S // d // B
        if nb == 1:
            g = math.gcd(d, group)

            def trip(t, carry, b=b, d=d, g=g):
                for u in range(g):
                    visit(b, d, t * g + u, 0, True)
                return carry

            lax.fori_loop(0, d // g, trip, 0)
        elif d == 1:
            visit(b, d, 0, 0, True)
            g = max(k for k in range(1, group + 2) if (nb - 1) % k == 0)

            def trip(t, carry, b=b, d=d, g=g):
                for u in range(g):
                    visit(b, d, 0, (1 + t * g + u) * B, False)
                return carry

            lax.fori_loop(0, (nb - 1) // g, trip, 0)
        else:
            def trip(r, carry, b=b, d=d, nb=nb):
                visit(b, d, r, 0, True)
                for n in range(1, nb):
                    visit(b, d, r, n * B, False)
                return carry

            lax.fori_loop(0, d, trip, 0)


def _dil_mask(first):
    B = KEY_BLOCK
    nk = B if first else 2 * B
    iq = lax.broadcasted_iota(jnp.int32, (B, nk), 0)
    ik = lax.broadcasted_iota(jnp.int32, (B, nk), 1)
    return (ik <= iq) if first else ((ik >= iq) & (ik <= iq + B))


def _dil_fwd(proj, cos2, sin_signed, gain, col0, n_heads, name):
    S = proj.shape[0]
    H, B = n_heads, KEY_BLOCK
    scale = HEAD_DIM ** -0.5
    rc = _tile(S, 256, 8)

    def body(q_ref, k_ref, v_ref, c_ref, s_ref, g_ref, o_ref, l_ref, mx_ref, qr, kr, *per_branch):
        ob, lb = per_branch[:len(DILATIONS)], per_branch[len(DILATIONS):]

        def rope_rows(t, carry):
            rows = pl.ds(pl.multiple_of(t * rc, rc), rc)
            qr[rows, :] = _rope(q_ref[rows, :], c_ref[rows, :], s_ref[rows, :])
            kr[rows, :] = _rope(k_ref[rows, :], c_ref[rows, :], s_ref[rows, :])
            return carry

        lax.fori_loop(0, S // rc, rope_rows, 0)

        def visit(b, d, r, l0, first):
            nk = B if first else 2 * B
            qrows = _dil_rows(d, r, l0, B)
            krows = qrows if first else _dil_rows(d, r, l0 - B, nk)
            s = _dot(qr[qrows, :].astype(BF16), kr[krows, :].astype(BF16), NT) * scale
            s = jnp.where(_dil_mask(first), s, NEG)
            m = jnp.max(s, axis=1, keepdims=True)
            p = jnp.exp(s - m)
            den = jnp.sum(p, axis=1, keepdims=True)
            ob[b][qrows, :] = _dot(p.astype(BF16), v_ref[krows, :].astype(BF16), NN) / den
            lb[b][qrows, :] = jnp.broadcast_to(m + jnp.log(den), (B, LANES))

        _dil_blocks(S, visit)

        def combine(t, carry):
            rows = pl.ds(pl.multiple_of(t * rc, rc), rc)
            l0, l1, l2 = lb[0][rows, :], lb[1][rows, :], lb[2][rows, :]
            m = jnp.maximum(jnp.maximum(l0, l1), l2)
            w0, w1, w2 = jnp.exp(l0 - m), jnp.exp(l1 - m), jnp.exp(l2 - m)
            den = w0 + w1 + w2
            o = (w0 * ob[0][rows, :] + w1 * ob[1][rows, :] + w2 * ob[2][rows, :]) / den
            o_ref[rows, :] = o
            l_ref[rows, :] = m + jnp.log(den)
            mx_ref[rows, :] = _head_out(o, g_ref[...]).astype(BF16)
            return carry

        lax.fori_loop(0, S // rc, combine, 0)

    def col(k):
        return pl.BlockSpec((S, HEAD_DIM), lambda h: (0, col0 + k * H + h))

    tab = pl.BlockSpec((S, HEAD_DIM), lambda h: (0, 0))
    out = pl.BlockSpec((S, HEAD_DIM), lambda h: (0, h))
    W = H * HEAD_DIM
    return _pcall(
        body, grid=(H,),
        in_specs=[col(0), col(1), col(2), tab, tab, pl.BlockSpec((1, HEAD_DIM), lambda h: (0, h))],
        out_specs=[out, out, out],
        out_shape=[jax.ShapeDtypeStruct((S, W), F32), jax.ShapeDtypeStruct((S, W), F32), jax.ShapeDtypeStruct((S, W), BF16)],
        scratch_shapes=[pltpu.VMEM((S, HEAD_DIM), F32)] * (2 + 2 * len(DILATIONS)),
        compiler_params=_params("parallel"), name=name)(proj, proj, proj, cos2, sin_signed, gain)


def _dil_bwd(proj, cos2, sin_signed, gain, o_raw, lse, dmixed, dm_col0, col0, n_heads, name):
    S = proj.shape[0]
    H, B = n_heads, KEY_BLOCK
    scale = HEAD_DIM ** -0.5
    rc = _tile(S, 256, 8)

    def body(q_ref, k_ref, v_ref, c_ref, s_ref, g_ref, o_ref, l_ref, dm_ref, dq_ref, dk_ref, dv_ref, dg_ref,
             qr, kr, dos, dsum, dqr, dkr, dvv):
        dg_ref[...] = jnp.zeros_like(dg_ref)

        def prep(t, carry):
            rows = pl.ds(pl.multiple_of(t * rc, rc), rc)
            qr[rows, :] = _rope(q_ref[rows, :], c_ref[rows, :], s_ref[rows, :])
            kr[rows, :] = _rope(k_ref[rows, :], c_ref[rows, :], s_ref[rows, :])
            o, dm = o_ref[rows, :], dm_ref[rows, :]
            r = _rms_scale(o)
            do = _rms_bwd(dm * g_ref[...], o, r)
            dg_ref[...] += jnp.broadcast_to(jnp.sum(dm * o * r, axis=0, keepdims=True), dg_ref.shape)
            dos[rows, :] = do
            dsum[rows, :] = jnp.broadcast_to(jnp.sum(do * o, axis=1, keepdims=True), (rc, LANES))
            dqr[rows, :] = jnp.zeros((rc, HEAD_DIM), F32)
            dkr[rows, :] = jnp.zeros((rc, HEAD_DIM), F32)
            dvv[rows, :] = jnp.zeros((rc, HEAD_DIM), F32)
            return carry

        lax.fori_loop(0, S // rc, prep, 0)

        def visit(b, d, r, l0, first):
            nk = B if first else 2 * B
            qrows = _dil_rows(d, r, l0, B)
            krows = qrows if first else _dil_rows(d, r, l0 - B, nk)
            qs, ks = qr[qrows, :].astype(BF16), kr[krows, :].astype(BF16)
            do = dos[qrows, :].astype(BF16)
            s = _dot(qs, ks, NT) * scale
            s = jnp.where(_dil_mask(first), s, NEG)
            p = jnp.exp(s - l_ref[qrows, :][:, 0:1])
            dp = _dot(do, v_ref[krows, :].astype(BF16), NT)
            ds = (p * (dp - dsum[qrows, :][:, 0:1]) * scale).astype(BF16)
            dqr[qrows, :] += _dot(ds, ks, NN)
            dkr[krows, :] += _dot(ds, qs, TN)
            dvv[krows, :] += _dot(p.astype(BF16), do, TN)

        _dil_blocks(S, visit)

        def finish(t, carry):
            rows = pl.ds(pl.multiple_of(t * rc, rc), rc)
            c, s = c_ref[rows, :], s_ref[rows, :]
            dq, dk = dqr[rows, :], dkr[rows, :]
            dq_ref[rows, :] = (dq * c + pltpu.roll(dq * s, HEAD_DIM // 2, axis=1)).astype(BF16)
            dk_ref[rows, :] = (dk * c + pltpu.roll(dk * s, HEAD_DIM // 2, axis=1)).astype(BF16)
            dv_ref[rows, :] = dvv[rows, :].astype(BF16)
            return carry

        lax.fori_loop(0, S // rc, finish, 0)

    def col(k):
        return pl.BlockSpec((S, HEAD_DIM), lambda h: (0, col0 + k * H + h))

    tab = pl.BlockSpec((S, HEAD_DIM), lambda h: (0, 0))
    out = pl.BlockSpec((S, HEAD_DIM), lambda h: (0, h))
    W = H * HEAD_DIM
    big = pltpu.VMEM((S, HEAD_DIM), F32)
    return _pcall(
        body, grid=(H,),
        in_specs=[col(0), col(1), col(2), tab, tab, pl.BlockSpec((1, HEAD_DIM), lambda h: (0, h)), out, out,
                  pl.BlockSpec((S, HEAD_DIM), lambda h: (0, dm_col0 + h))],
        out_specs=[out, out, out, pl.BlockSpec((8, HEAD_DIM), lambda h: (0, h))],
        out_shape=[jax.ShapeDtypeStruct((S, W), BF16), jax.ShapeDtypeStruct((S, W), BF16),
                   jax.ShapeDtypeStruct((S, W), BF16), jax.ShapeDtypeStruct((8, W), F32)],
        scratch_shapes=[big, big, big, pltpu.VMEM((S, LANES), F32), big, big, big],
        compiler_params=_params("parallel"), name=name)(proj, proj, proj, cos2, sin_signed, gain, o_raw, lse, dmixed)


GELU_C = math.sqrt(2.0 / math.pi)
GELU_A = 0.044715
HALO = 16


def _shift_down(cur, halo, k):
    out = pltpu.roll(cur, k, axis=0)
    row = lax.broadcasted_iota(jnp.int32, cur.shape, 0)
    for t in range(k):
        out = jnp.where(row == t, halo[HALO - k + t:HALO - k + t + 1, :], out)
    return out


def _shift_up(cur, halo, k):
    n = cur.shape[0]
    out = pltpu.roll(cur, n - k, axis=0)
    row = lax.broadcasted_iota(jnp.int32, cur.shape, 0)
    for t in range(k):
        out = jnp.where(row == n - k + t, halo[t:t + 1, :], out)
    return out


def _conv3(cur, halo, cw):
    return _shift_down(cur, halo, 2) * cw[0:1, :] + _shift_down(cur, halo, 1) * cw[1:2, :] + cur * cw[2:3, :] + cw[3:4, :]


def _gelu_parts(x):
    t = jnp.tanh(GELU_C * (x + GELU_A * x * x * x))
    return 0.5 * x * (1.0 + t), t


def _geglu_specs(tm, tn, ncb):
    hb = tm // HALO

    def cur(off):
        return pl.BlockSpec((tm, tn), lambda j, i: (i, off + j))

    def prev(off):
        return pl.BlockSpec((HALO, tn), lambda j, i: (jnp.maximum(i * hb - 1, 0), off + j))

    def taps(off):
        return pl.BlockSpec((8, tn), lambda j, i: (0, off + j))

    return [cur(0), prev(0), cur(ncb), prev(ncb), taps(0), taps(ncb)]


def _geglu_fwd(u, cwb, name, tm=256, tn=1408):
    S, F2 = u.shape
    F = F2 // 2
    tm, tn = _tile(S, tm, HALO), _tile(F, tn)
    ncb = F // tn

    def body(g_ref, gp_ref, v_ref, vp_ref, cg_ref, cv_ref, y_ref):
        top = pl.program_id(1) > 0
        gp = jnp.where(top, gp_ref[...].astype(F32), 0.0)
        vp = jnp.where(top, vp_ref[...].astype(F32), 0.0)
        gc = _conv3(g_ref[...].astype(F32), gp, cg_ref[...])
        vc = _conv3(v_ref[...].astype(F32), vp, cv_ref[...])
        y_ref[...] = (_gelu_parts(gc)[0] * vc).astype(BF16)

    return _pcall(body, grid=(ncb, S // tm), in_specs=_geglu_specs(tm, tn, ncb),
                  out_specs=pl.BlockSpec((tm, tn), lambda j, i: (i, j)),
                  out_shape=jax.ShapeDtypeStruct((S, F), BF16),
                  compiler_params=_params("parallel", "parallel"), name=name)(u, u, u, u, cwb, cwb)


def _geglu_bwd(u, dy, cwb, name, tm=256, tn=512):
    S, F2 = u.shape
    F = F2 // 2
    tm, tn = _tile(S, tm, HALO), _tile(F, tn)
    ncb = F // tn

    def body(g_ref, gp_ref, v_ref, vp_ref, cg_ref, cv_ref, dy_ref, dc_ref, dwg_ref, dwv_ref):
        i = pl.program_id(1)

        @pl.when(i == 0)
        def _():
            dwg_ref[...] = jnp.zeros_like(dwg_ref)
            dwv_ref[...] = jnp.zeros_like(dwv_ref)

        top = i > 0
        g, v = g_ref[...].astype(F32), v_ref[...].astype(F32)
        gp = jnp.where(top, gp_ref[...].astype(F32), 0.0)
        vp = jnp.where(top, vp_ref[...].astype(F32), 0.0)
        gc = _conv3(g, gp, cg_ref[...])
        vc = _conv3(v, vp, cv_ref[...])
        act, t = _gelu_parts(gc)
        dact = 0.5 * (1.0 + t) + 0.5 * gc * (1.0 - t * t) * GELU_C * (1.0 + 3.0 * GELU_A * gc * gc)
        dyv = dy_ref[...].astype(F32)
        dgc = dyv * vc * dact
        dvc = dyv * act
        dc_ref[0] = dgc.astype(BF16)
        dc_ref[1] = dvc.astype(BF16)

        def taps(out_ref, dc, cur, halo):
            out_ref[0:1, :] += jnp.sum(dc * _shift_down(cur, halo, 2), axis=0, keepdims=True)
            out_ref[1:2, :] += jnp.sum(dc * _shift_down(cur, halo, 1), axis=0, keepdims=True)
            out_ref[2:3, :] += jnp.sum(dc * cur, axis=0, keepdims=True)
            out_ref[3:4, :] += jnp.sum(dc, axis=0, keepdims=True)

        taps(dwg_ref, dgc, g, gp)
        taps(dwv_ref, dvc, v, vp)

    return _pcall(body, grid=(ncb, S // tm),
                  in_specs=_geglu_specs(tm, tn, ncb) + [pl.BlockSpec((tm, tn), lambda j, i: (i, j))],
                  out_specs=[pl.BlockSpec((2, tm, tn), lambda j, i: (0, i, j)),
                             pl.BlockSpec((8, tn), lambda j, i: (0, j)), pl.BlockSpec((8, tn), lambda j, i: (0, j))],
                  out_shape=[jax.ShapeDtypeStruct((2, S, F), BF16), jax.ShapeDtypeStruct((8, F), F32),
                             jax.ShapeDtypeStruct((8, F), F32)],
                  compiler_params=_params("parallel", "arbitrary"), name=name)(u, u, u, u, cwb, cwb, dy)


def _conv_bwd(dc, cwb, name, tm=512, tn=1408):
    _, S, F = dc.shape
    tm, tn = _tile(S, tm, HALO), _tile(F, tn)
    ncb, nrb = F // tn, S // tm
    hb = tm // HALO

    def body(c_ref, n_ref, w_ref, du_ref):
        cur = c_ref[...].astype(F32)
        nxt = jnp.where(pl.program_id(2) < nrb - 1, n_ref[...].astype(F32), 0.0)
        w = w_ref[...]
        du = cur * w[2:3, :] + _shift_up(cur, nxt, 1) * w[1:2, :] + _shift_up(cur, nxt, 2) * w[0:1, :]
        du_ref[...] = du.astype(BF16)

    return _pcall(body, grid=(2, ncb, nrb),
                  in_specs=[pl.BlockSpec((None, tm, tn), lambda c, j, i: (c, i, j)),
                            pl.BlockSpec((None, HALO, tn), lambda c, j, i: (c, jnp.minimum((i + 1) * hb, S // HALO - 1), j)),
                            pl.BlockSpec((8, tn), lambda c, j, i: (0, c * ncb + j))],
                  out_specs=pl.BlockSpec((tm, tn), lambda c, j, i: (i, c * ncb + j)),
                  out_shape=jax.ShapeDtypeStruct((S, 2 * F), BF16),
                  compiler_params=_params("parallel", "parallel", "parallel"), name=name)(dc, dc, cwb)


def _adam_math(w, g, m, v):
    m = ADAM_B1 * m + (1.0 - ADAM_B1) * g
    v = ADAM_B2 * v + (1.0 - ADAM_B2) * (g * g)
    m_hat = m / (1.0 - ADAM_B1 ** ADAM_STEP)
    v_hat = v / (1.0 - ADAM_B2 ** ADAM_STEP)
    return -ADAM_LR * (m_hat / (jnp.sqrt(v_hat) + ADAM_EPS) + ADAM_WD * w), m, v


def _adamw(w, parts, m, v, name, tr=256):
    R, C = w.shape
    n, _, Cp = parts.shape
    tr = _tile(R, tr, 8)

    def body(w_ref, p_ref, m_ref, v_ref, g_out, d_out, m_out, v_out):
        g = p_ref[0, :, 0:C].astype(F32)
        for k in range(1, n):
            g = g + p_ref[k, :, 0:C].astype(F32)
        d, mn, vn = _adam_math(w_ref[...], g, m_ref[...], v_ref[...])
        g_out[...] = g
        d_out[...] = d
        m_out[...] = mn
        v_out[...] = vn

    spec = pl.BlockSpec((tr, C), lambda i: (i, 0))
    shape = jax.ShapeDtypeStruct((R, C), F32)
    return _pcall(body, grid=(R // tr,), in_specs=[spec, pl.BlockSpec((n, tr, Cp), lambda i: (0, i, 0)), spec, spec],
                  out_specs=[spec] * 4, out_shape=[shape] * 4, compiler_params=_params("parallel"), name=name)(w, parts, m, v)


def _adamw_chips(w, pair, parts, chip_ids, m, v, name, tr=256):
    R, C = w.shape
    Cp = pair.shape[2]
    by_columns = C == Cp and _tile(R, tr, 16) < 64
    tr, tc = (R, _tile(C, 256)) if by_columns else (_tile(R, tr, 16), C)

    def body(ids_ref, w_ref, own_ref, p1_ref, p2_ref, p3_ref, m_ref, v_ref, g_out, d_out, m_out, v_out):
        g = own_ref[:, 0:tc].astype(F32)
        for ref in (p1_ref, p2_ref, p3_ref):
            g = g + ref[:, 0:tc].astype(F32)
        d, mn, vn = _adam_math(w_ref[...], g, m_ref[...], v_ref[...])
        g_out[...] = g
        d_out[...] = d
        m_out[...] = mn
        v_out[...] = vn

    if by_columns:
        spec = pl.BlockSpec((tr, tc), lambda j, ids: (0, j))
    else:
        spec = pl.BlockSpec((tr, tc), lambda i, ids: (i, 0))

    def chip(k):
        if by_columns:
            return pl.BlockSpec((None, tr, tc), lambda j, ids: (ids[k], 0, j))
        return pl.BlockSpec((None, tr, Cp), lambda i, ids: (ids[k], i, 0))

    shape = jax.ShapeDtypeStruct((R, C), F32)
    grid_spec = pltpu.PrefetchScalarGridSpec(
        num_scalar_prefetch=1, grid=(C // tc if by_columns else R // tr,),
        in_specs=[spec, chip(0), chip(1), chip(2), chip(3), spec, spec], out_specs=[spec] * 4)
    return _pcall(body, grid_spec=grid_spec, out_shape=[shape] * 4, compiler_params=_params("parallel"),
                  name=name)(chip_ids, w, pair, parts, parts, parts, m, v)


def _place():
    return lax.axis_index("x"), lax.axis_index("y"), lax.axis_index("c")


def _other_chips(x, y):
    return [(1 - x, y), (x, 1 - y), (1 - x, 1 - y)]


IN_HBM = pl.BlockSpec(memory_space=pltpu.HBM)
SEM = pl.BlockSpec(memory_space=pltpu.SEMAPHORE)
EFFECT = pltpu.SideEffectType.DATAFLOW_SIDE_EFFECTING
TOKEN = jax.ShapeDtypeStruct((8, LANES), F32)
TOKEN_SPEC = pl.BlockSpec(memory_space=pltpu.VMEM)


def _in_hbm(a):
    return pltpu.with_memory_space_constraint(a, pltpu.HBM)


def _landing(shape):
    return _in_hbm(lax.empty(shape.shape, shape.dtype))


def _hbm_like(a):
    return pltpu.HBM(a.shape, a.dtype)


def _gather_start(landing, slots, after, name):
    na = len(landing)

    def body(*refs):
        land = refs[:na]
        send_sems, recv_sems = refs[na + 1], refs[na + 2]
        token = refs[-1]
        x, y, c = _place()
        for a in range(na):
            own = slots[a](land[a], x, y, c)
            for k, to in enumerate([(x, y, 1 - c)] + [(*chip, c) for chip in _other_chips(x, y)]):
                pltpu.make_async_remote_copy(
                    src_ref=own, dst_ref=own, send_sem=send_sems.at[4 * a + k],
                    recv_sem=recv_sems.at[4 * a + k], device_id=to, device_id_type=MESH).start()
        token[...] = jnp.zeros_like(token)

    sems = pltpu.SemaphoreType.DMA((4 * na,))
    outs = _pcall(
        body, in_specs=[IN_HBM] * na + [HBM],
        out_specs=[SEM, SEM] + [IN_HBM] * na + [TOKEN_SPEC],
        out_shape=[sems, sems] + [_hbm_like(s) for s in landing] + [TOKEN],
        input_output_aliases={a: 2 + a for a in range(na)},
        compiler_params=pltpu.CompilerParams(has_side_effects=EFFECT), name=name,
    )(*[_in_hbm(s) for s in landing], after)
    return outs[0], outs[1], outs[2:2 + na], outs[-1]


def _gather_forward(gathered, send_sems, recv_sems, slots, after, name):
    na = len(gathered)

    def body(*refs):
        gath = refs[:na]
        send1, recv1 = refs[na], refs[na + 1]
        fsend, frecv = refs[na + 3], refs[na + 4]
        token = refs[-1]
        x, y, c = _place()
        chips = _other_chips(x, y)
        for a in range(na):
            for k, peer in enumerate([(x, y, 1 - c)] + [(*chip, c) for chip in chips]):
                arrival = pltpu.make_async_remote_copy(
                    src_ref=slots[a](gath[a], x, y, c), dst_ref=slots[a](gath[a], *peer), send_sem=send1.at[4 * a + k],
                    recv_sem=recv1.at[4 * a + k], device_id=peer, device_id_type=MESH)
                arrival.wait_send()
                arrival.wait_recv()
        for a in range(na):
            for j, chip in enumerate(chips):
                view = slots[a](gath[a], *chip, c)
                pltpu.make_async_remote_copy(
                    src_ref=view, dst_ref=view, send_sem=fsend.at[3 * a + j], recv_sem=frecv.at[3 * a + j],
                    device_id=(x, y, 1 - c), device_id_type=MESH).start()
        token[...] = jnp.zeros_like(token)

    sems = pltpu.SemaphoreType.DMA((3 * na,))
    outs = _pcall(
        body, in_specs=[IN_HBM] * na + [SEM, SEM, HBM],
        out_specs=[SEM, SEM] + [IN_HBM] * na + [TOKEN_SPEC],
        out_shape=[sems, sems] + [_hbm_like(g) for g in gathered] + [TOKEN],
        input_output_aliases={a: 2 + a for a in range(na)},
        compiler_params=pltpu.CompilerParams(has_side_effects=EFFECT), name=name,
    )(*gathered, send_sems, recv_sems, after)
    return outs[0], outs[1], outs[2:2 + na], outs[-1]


def _gather_finish(gathered, fsend, frecv, slots, after, name):
    na = len(gathered)

    def body(*refs):
        gath, fs, fr = refs[:na], refs[na], refs[na + 1]
        x, y, c = _place()
        for a in range(na):
            for j, chip in enumerate(_other_chips(x, y)):
                passed = pltpu.make_async_remote_copy(
                    src_ref=slots[a](gath[a], *chip, c), dst_ref=slots[a](gath[a], *chip, 1 - c),
                    send_sem=fs.at[3 * a + j], recv_sem=fr.at[3 * a + j], device_id=(x, y, 1 - c), device_id_type=MESH)
                passed.wait_send()
                passed.wait_recv()

    outs = _pcall(
        body, in_specs=[IN_HBM] * na + [SEM, SEM, HBM], out_specs=[IN_HBM] * na,
        out_shape=[_hbm_like(g) for g in gathered], input_output_aliases={a: a for a in range(na)},
        compiler_params=pltpu.CompilerParams(has_side_effects=EFFECT), name=name,
    )(*gathered, fsend, frecv, after)
    return list(outs)


def _pair_copy(view, src, land, send_sems, recv_sems, chip):
    x, y, c = _place()
    return pltpu.make_async_remote_copy(
        src_ref=view(src, chip, 1 - c), dst_ref=land.at[chip], send_sem=send_sems.at[chip], recv_sem=recv_sems.at[chip],
        device_id=(x, y, 1 - c), device_id_type=MESH)


def _pair_start(grad, view, block, after, name):
    def body(src, land, after_ref, send_sems, recv_sems, src_thru, land_thru, token):
        for chip in range(N_CHIP):
            _pair_copy(view, src, land, send_sems, recv_sems, chip).start()
        token[...] = jnp.zeros_like(token)

    sems = pltpu.SemaphoreType.DMA((N_CHIP,))
    land = jax.ShapeDtypeStruct((N_CHIP, *block), BF16)
    return _pcall(
        body, in_specs=[IN_HBM, IN_HBM, HBM], out_specs=[SEM, SEM, IN_HBM, IN_HBM, TOKEN_SPEC],
        out_shape=[sems, sems, _hbm_like(grad), _hbm_like(land), TOKEN], input_output_aliases={0: 2, 1: 3},
        compiler_params=pltpu.CompilerParams(has_side_effects=EFFECT), name=name,
    )(_in_hbm(grad), _landing(land), after)


def _pair_wait(grad, recv, send_sems, recv_sems, view, after, name):
    def body(src, land, send, recv_s, after_ref, src_thru, land_thru):
        for chip in range(N_CHIP):
            copy = _pair_copy(view, src, land, send, recv_s, chip)
            copy.wait_send()
            copy.wait_recv()

    return _pcall(
        body, in_specs=[IN_HBM, IN_HBM, SEM, SEM, HBM], out_specs=[IN_HBM, IN_HBM],
        out_shape=[_hbm_like(grad), _hbm_like(recv)], input_output_aliases={0: 0, 1: 1},
        compiler_params=pltpu.CompilerParams(has_side_effects=EFFECT), name=name,
    )(grad, recv, send_sems, recv_sems, after)


def _chip_start(pair, after, name):
    def body(src, land, after_ref, send_sems, recv_sems, src_thru, land_thru, token):
        x, y, c = _place()
        for j, (px, py) in enumerate(_other_chips(x, y)):
            pltpu.make_async_remote_copy(
                src_ref=src.at[2 * px + py], dst_ref=land.at[2 * x + y], send_sem=send_sems.at[j], recv_sem=recv_sems.at[j],
                device_id=(px, py, c), device_id_type=MESH).start()
        token[...] = jnp.zeros_like(token)

    sems = pltpu.SemaphoreType.DMA((3,))
    return _pcall(
        body, in_specs=[IN_HBM, IN_HBM, HBM], out_specs=[SEM, SEM, IN_HBM, IN_HBM, TOKEN_SPEC],
        out_shape=[sems, sems, _hbm_like(pair), _hbm_like(pair), TOKEN], input_output_aliases={0: 2, 1: 3},
        compiler_params=pltpu.CompilerParams(has_side_effects=EFFECT), name=name,
    )(_in_hbm(pair), _landing(pair), after)


def _chip_wait(pair, parts, send_sems, recv_sems, after, name):
    def body(src, land, send, recv, after_ref, src_thru, land_thru):
        x, y, c = _place()
        for j, (px, py) in enumerate(_other_chips(x, y)):
            copy = pltpu.make_async_remote_copy(
                src_ref=src.at[2 * px + py], dst_ref=land.at[2 * px + py], send_sem=send.at[j], recv_sem=recv.at[j],
                device_id=(px, py, c), device_id_type=MESH)
            copy.wait_send()
            copy.wait_recv()

    return _pcall(
        body, in_specs=[IN_HBM, IN_HBM, SEM, SEM, HBM], out_specs=[IN_HBM, IN_HBM],
        out_shape=[_hbm_like(pair), _hbm_like(parts)], input_output_aliases={0: 0, 1: 1},
        compiler_params=pltpu.CompilerParams(has_side_effects=EFFECT), name=name,
    )(pair, parts, send_sems, recv_sems, after)


def _pair_add(core, grad, recv, block, grad_spec, name):
    _, R, C = recv.shape
    tr = block

    def body(c_ref, g_ref, r_ref, o_ref):
        o_ref[...] = (g_ref[...].astype(F32) + r_ref[...].astype(F32)).astype(BF16)

    grid_spec = pltpu.PrefetchScalarGridSpec(
        num_scalar_prefetch=1, grid=(N_CHIP, R // tr),
        in_specs=[grad_spec, pl.BlockSpec((None, tr, C), lambda k, i, c: (k, i, 0))],
        out_specs=pl.BlockSpec((None, tr, C), lambda k, i, c: (k, i, 0)))
    return _pcall(body, grid_spec=grid_spec, out_shape=jax.ShapeDtypeStruct(recv.shape, BF16),
                  compiler_params=_params("parallel", "parallel"), name=name)(core, grad, recv)


def _small_step(parts, params, name):
    na, npar = len(parts), len(params)

    def body(*refs):
        p_refs, wmv = refs[:na], refs[na:na + 3 * npar]
        o_parts = refs[na + 3 * npar:2 * na + 3 * npar]
        o_params = refs[2 * na + 3 * npar:2 * na + 7 * npar]
        alls, (send_sems, recv_sems) = refs[2 * na + 7 * npar:3 * na + 7 * npar], refs[3 * na + 7 * npar:]
        x, y, c = _place()
        me = 4 * x + 2 * y + c
        peers = [(x, y, 1 - c)] + [(px, py, pc) for px, py in _other_chips(x, y) for pc in (c, 1 - c)]
        copies = []
        for a in range(na):
            alls[a][me] = p_refs[a][...]
            copies += [pltpu.make_async_remote_copy(
                src_ref=p_refs[a], dst_ref=alls[a].at[me], send_sem=send_sems.at[7 * a + k], recv_sem=recv_sems.at[7 * a + k],
                device_id=peer, device_id_type=MESH) for k, peer in enumerate(peers)]
        for cp in copies:
            cp.start()
        for a in range(na):
            for k, (px, py, pc) in enumerate(peers):
                pltpu.make_async_remote_copy(
                    src_ref=p_refs[a], dst_ref=alls[a].at[4 * px + 2 * py + pc], send_sem=send_sems.at[7 * a + k],
                    recv_sem=recv_sems.at[7 * a + k], device_id=peers[k], device_id_type=MESH).wait_recv()
        for cp in copies:
            cp.wait_send()
        sums = []
        for a in range(na):
            acc = alls[a][0]
            for k in range(1, N_DEV):
                acc = acc + alls[a][k]
            o_parts[a][...] = acc
            sums.append(acc)
        for j, (a, row, _, _, _) in enumerate(params):
            g = sums[a][row:row + 1, :]
            d, mn, vn = _adam_math(wmv[3 * j][...], g, wmv[3 * j + 1][...], wmv[3 * j + 2][...])
            for out, val in zip(o_params[4 * j:4 * j + 4], (g, d, mn, vn)):
                out[...] = val

    vm = pl.BlockSpec(memory_space=pltpu.VMEM)
    flat = [t for p in params for t in p[2:]]
    out_shape = [jax.ShapeDtypeStruct(p.shape, F32) for p in parts]
    out_shape += [jax.ShapeDtypeStruct(p[2].shape, F32) for p in params for _ in range(4)]
    outs = _pcall(body, in_specs=[vm] * (na + 3 * npar), out_specs=[vm] * len(out_shape), out_shape=out_shape,
                  scratch_shapes=[pltpu.VMEM((N_DEV, *p.shape), F32) for p in parts]
                  + [pltpu.SemaphoreType.DMA((7 * na,)), pltpu.SemaphoreType.DMA((7 * na,))],
                  name=name)(*parts, *flat)
    return outs[:na], [outs[na + 4 * j:na + 4 * j + 4] for j in range(npar)]


def _local_step(x, tgt, gains, weights):
    g_pre_mix, g_post_mix, g_pre_ffn, g_post_ffn, g_sb, g_dil = gains
    S, D = x.shape
    hs = g_sb.shape[1] // HEAD_DIM
    hd = g_dil.shape[1] // HEAD_DIM
    cos2, sin_signed = _rope_tables(S)

    h1 = _rms_fwd(x, g_pre_mix + weights.start(), "rms_in")
    w_in_g = weights.w_in(h1)
    proj = _mm_nn(h1, w_in_g, F32, "proj", tn=768)
    o_sb, ct_sb, mx_sb = _sb_fwd(proj, g_sb, hs, "sb_fwd")
    o_dl, lse_dl, mx_dl = _dil_fwd(proj, cos2, sin_signed, g_dil + weights.forward_out(o_sb), 3 * hs, hd, "dil_fwd")
    w_out_g, dep = weights.w_out(o_dl)
    mixed = jnp.concatenate([mx_sb, mx_dl], axis=1)
    mix = _mm_nn(mixed, w_out_g, F32, "mix_out", tn=1024)
    x2, h2 = _mid_fwd(x, mix, g_post_mix + dep, g_pre_ffn, "mid_fwd")
    w_up_g, cwb = weights.w_up(h2)
    u = _mm_nn(h2, w_up_g, BF16, "ffn_up", b_transposed=True)
    y = _geglu_fwd(u, cwb + weights.forward_down(u), "geglu_fwd")
    w_down_g = weights.w_down(y)
    f = _mm_nn(y, w_down_g, F32, "ffn_down", tn=1024, tk=1408)

    dy, df, dg_post_ffn, loss = _loss_bwd(x2, f, tgt, g_post_ffn, "loss_bwd")
    dyv = _mm_nt(df, w_down_g, BF16, "d_y", tn=1408)
    dw_down = _mm_tn(y, df, D, BF16, "dw_down", tm=1408, tn=1024)
    dc, dcw_g, dcw_v = _geglu_bwd(u, dyv, cwb + weights.grad("w_down", dw_down), "geglu_bwd")
    du = _conv_bwd(dc, cwb + weights.grad_reduce("w_down", dc), "conv_bwd")
    dh2 = _mm_nt(du, w_up_g, F32, "d_h2", tk=1408, b_transposed=True)
    dw_up = _mm_tn(du, h2, D, BF16, "dw_up", tm=1408, tn=1024)
    dx2, dmix, dg_pre_ffn, dg_post_mix = _mid_bwd(
        dy, dh2, x2, mix, g_pre_ffn + weights.grad("w_up", dw_up), g_post_mix, "mid_bwd")
    dmixed = _mm_nt(dmix, w_out_g, F32, "d_mixed", after=jnp.reshape(weights.grad_reduce("w_up", dmix), (1, 1)))
    dw_out = _mm_tn(mixed, dmix, D, BF16, "dw_out", tn=1024)
    dq_s, dk_s, dv_s, dg_sb = _sb_bwd(proj, g_sb + weights.grad("w_out", dw_out), o_sb, ct_sb, dmixed, 0, hs, "sb_bwd")
    dq_d, dk_d, dv_d, dg_dil = _dil_bwd(proj, cos2, sin_signed, g_dil + weights.grad_reduce("w_out", dq_s), o_dl, lse_dl,
                                        dmixed, hs, 3 * hs, hd, "dil_bwd")
    dproj = jnp.concatenate([dq_s, dk_s, dv_s, dq_d, dk_d, dv_d], axis=1)
    dh1 = _mm_nt(dproj, w_in_g, F32, "d_h1", tk=768)
    grad_x, dg_pre_mix = _first_bwd(dx2, dh1, x, g_pre_mix, "first_bwd")
    small = (dg_pre_mix, dg_post_mix, dg_pre_ffn, dg_post_ffn, dg_sb[0:1], dg_dil[0:1], jnp.concatenate([dcw_g, dcw_v], axis=1))
    dw_in = _mm_tn(h1, dproj, w_in_g.shape[2], BF16, "dw_in", tn=768, after=weights.small(small, loss))
    weights.grad("w_in", dw_in)
    weights.grad_reduce("w_in", grad_x)
    return loss, grad_x, small


def _pad_cols(a, to):
    return jnp.pad(a, ((0, 0), (0, to - a.shape[1])))


def kernel(x, pre_mix_gain, post_mix_gain, pre_ffn_gain, post_ffn_gain, w_in, sb_out_gain, dil_out_gain, w_out, w_up, conv_w, conv_b, w_down, loss_target, m_pre_mix_gain, m_post_mix_gain, m_pre_ffn_gain, m_post_ffn_gain, m_w_in, m_sb_out_gain, m_dil_out_gain, m_w_out, m_w_up, m_conv_w, m_conv_b, m_w_down, v_pre_mix_gain, v_post_mix_gain, v_pre_ffn_gain, v_post_ffn_gain, v_w_in, v_sb_out_gain, v_dil_out_gain, v_w_out, v_w_up, v_conv_w, v_conv_b, v_w_down):
    xb, tb = x[0], loss_target[0]
    S, D = xb.shape
    w_in, w_out, w_up, w_down, conv_w = w_in[0], w_out[0], w_up[0], w_down[0], conv_w[0]
    n_in, e_rows = w_in.shape[1], w_out.shape[0]
    cu, half = w_up.shape[1], w_down.shape[0]
    assert cu == 2 * half and half % 16 == 0
    cup = -(-cu // LANES) * LANES
    fp = N_CHIP * cup
    px, py, pc = _place()
    me = 4 * px + 2 * py + pc
    core = jnp.reshape(pc, (1,)).astype(jnp.int32)

    w_up_t, m_up_t, v_up_t = (jnp.swapaxes(t, 0, 1) for t in (w_up, m_w_up[0], v_w_up[0]))

    def by_dev(ref, qx, qy, qc):
        return ref.at[4 * qx + 2 * qy + qc]

    def down_slot(ref, qx, qy, qc):
        return ref.at[2 * qx + qy, pl.ds(qc * half, half)]

    def by_pair(ref, chip, k):
        return ref.at[chip, k]

    def down_pair(ref, chip, k):
        return ref.at[chip, pl.ds(k * half, half)]

    def pair_spec(tr, cols):
        return pl.BlockSpec((None, None, tr, cols), lambda k, i, c: (k, c[0], i, 0))

    tr_in, tr_up = _tile(D, 512, 16), _tile(cup, 256, 16)
    grad_plan = {
        "w_in": ((N_CHIP, 2, D, n_in), by_pair, (D, n_in), tr_in, pair_spec(tr_in, n_in)),
        "w_out": ((N_CHIP, 2, e_rows, D), by_pair, (e_rows, D), e_rows, pair_spec(e_rows, D)),
        "w_up": ((N_CHIP, 2, cup, D), by_pair, (cup, D), tr_up, pair_spec(tr_up, D)),
        "w_down": ((N_CHIP, cup, D), down_pair, (half, D), half,
                   pl.BlockSpec((None, half, D), lambda k, i, c: (k, c[0], 0))),
    }

    class Exchanges:
        def __init__(self):
            self.in_flight = {}

        def start(self):
            def own_slot(shard):
                return lax.dynamic_update_index_in_dim(lax.empty((N_DEV, *shard.shape), shard.dtype), shard, me, 0)

            self.g_in = _gather_start([own_slot(w_in.astype(BF16))], [by_dev], core, "gather_in_start")
            zero = self.g_in[3][0, 0]
            self.g_out = _gather_start([own_slot((w_out + zero).astype(BF16))], [by_dev], self.g_in[3], "gather_out_start")
            up = jnp.pad(w_up_t + zero, ((0, cup - cu), (0, 0))).astype(BF16)
            taps = jnp.pad(conv_w + zero, ((0, 8 - conv_w.shape[0]), (0, cup - cu)))
            self.g_up = _gather_start([own_slot(up), own_slot(taps)], [by_dev, by_dev], self.g_out[3], "gather_up_start")
            down = lax.dynamic_update_slice(jnp.zeros((N_CHIP, cup, D), BF16), (w_down + zero).astype(BF16)[None],
                                            (2 * px + py, pc * half, 0))
            self.g_down = _gather_start([down], [down_slot], self.g_up[3], "gather_down_start")
            return self.g_down[3][0, 0]

        def w_in(self, after):
            send, recv, gath, _ = self.g_in
            fsend, frecv, gath, token = _gather_forward(gath, send, recv, [by_dev], after, "gather_in_forward")
            return _gather_finish(gath, fsend, frecv, [by_dev], token, "gather_in_finish")[0]

        def forward_out(self, after):
            send, recv, gath, _ = self.g_out
            self.p_out = _gather_forward(gath, send, recv, [by_dev], after, "gather_out_forward")
            return self.p_out[3][0, 0]

        def w_out(self, after):
            fsend, frecv, gath, _ = self.p_out
            w_out_g = _gather_finish(gath, fsend, frecv, [by_dev], after, "gather_out_finish")[0]
            send, recv, gath, _ = self.g_up
            self.p_up = _gather_forward(gath, send, recv, [by_dev, by_dev], w_out_g, "gather_up_forward")
            return w_out_g.reshape(1, N_DEV * e_rows, D), self.p_up[3][0, 0]

        def w_up(self, after):
            fsend, frecv, gath, _ = self.p_up
            w_up_g, cw_g = _gather_finish(gath, fsend, frecv, [by_dev, by_dev], after, "gather_up_finish")
            cb = _pad_cols(conv_b.reshape(N_DEV, cu), cup).reshape(1, 2 * fp)
            cw_full = jnp.transpose(cw_g[:, :3, :], (1, 0, 2)).reshape(3, 2 * fp)
            cwb = jnp.concatenate([cw_full, cb, jnp.zeros((4, 2 * fp), F32)], axis=0)
            return w_up_g, cwb

        def forward_down(self, after):
            send, recv, gath, _ = self.g_down
            self.p_down = _gather_forward(gath, send, recv, [down_slot], after, "gather_down_forward")
            return self.p_down[3][0, 0]

        def w_down(self, after):
            fsend, frecv, gath, _ = self.p_down
            return _gather_finish(gath, fsend, frecv, [down_slot], after, "gather_down_finish")[0].reshape(1, fp, D)

        def small(self, small, loss):
            d_pre_mix, d_post_mix, d_pre_ffn, d_post_ffn, d_sb, d_dil, d_conv = small

            def rows_of(*vectors):
                n = vectors[0].shape[1]
                row = lax.broadcasted_iota(jnp.int32, (8, n), 0)
                out = jnp.zeros((8, n), F32)
                for k, vec in enumerate(vectors):
                    out = jnp.where(row == k, vec, out)
                return out

            parts = [rows_of(d_pre_mix, d_post_mix, d_pre_ffn, d_post_ffn, jnp.broadcast_to(loss[:, :1], (1, D))),
                     rows_of(d_sb, d_dil), d_conv]
            params = [(0, 0, pre_mix_gain, m_pre_mix_gain, v_pre_mix_gain), (0, 1, post_mix_gain, m_post_mix_gain, v_post_mix_gain),
                      (0, 2, pre_ffn_gain, m_pre_ffn_gain, v_pre_ffn_gain), (0, 3, post_ffn_gain, m_post_ffn_gain, v_post_ffn_gain),
                      (1, 0, sb_out_gain, m_sb_out_gain, v_sb_out_gain), (1, 1, dil_out_gain, m_dil_out_gain, v_dil_out_gain)]
            (gains_sum, _, self.conv_sum), self.gain_steps = _small_step(parts, params, "small_step")
            self.loss_sum = gains_sum[4, 0]
            return self.conv_sum

        def grad(self, name, dw):
            view_shape, view, block, tr, spec = grad_plan[name]
            send, recv_sems, dw, recv, token = _pair_start(dw.reshape(view_shape), view, block, core, "pair_start_" + name)
            self.in_flight[name] = (dw, recv, send, recv_sems)
            return token[0, 0]

        def grad_reduce(self, name, after):
            _, view, _, tr, spec = grad_plan[name]
            dw, recv = _pair_wait(*self.in_flight[name], view, after, "pair_wait_" + name)
            pair = _pair_add(core, dw, recv, tr, spec, "pair_add_" + name)
            send, recv_sems, pair, parts, token = _chip_start(pair, recv, "chip_start_" + name)
            self.in_flight[name] = (pair, parts, send, recv_sems)
            self.last_token = token
            return token[0, 0]

        def grad_parts(self, name, after):
            return _chip_wait(*self.in_flight[name], after, "chip_wait_" + name)

    exchanges = Exchanges()
    gains = (pre_mix_gain, post_mix_gain, pre_ffn_gain, post_ffn_gain, sb_out_gain, dil_out_gain)
    loss, grad_x, small = _local_step(xb, tb, gains, exchanges)

    loss_out, g_conv = exchanges.loss_sum, exchanges.conv_sum
    g_conv_b = g_conv[3].reshape(N_DEV, cup)[:, :cu].reshape(1, N_DEV * cu)
    g_conv_w = lax.dynamic_index_in_dim(g_conv[0:3].reshape(3, N_DEV, cup), me, axis=1, keepdims=False)[:, :cu]

    def small_adam(w, g, m, v, name):
        one = w.shape[0] == 1
        if one:
            w, g, m, v = (jnp.broadcast_to(t, (8, t.shape[1])) for t in (w, g, m, v))
        outs = _adamw(w, g[None], m, v, name)
        return [o[0:1] for o in outs] if one else outs

    chip_ids = jnp.stack([2 * px + py, 2 * (1 - px) + py, 2 * px + 1 - py, 2 * (1 - px) + 1 - py]).astype(jnp.int32)
    out_w_down = _adamw_chips(w_down, *exchanges.grad_parts("w_down", exchanges.last_token), chip_ids, m_w_down[0], v_w_down[0], "adam_w_down")
    out_up_t = _adamw_chips(w_up_t, *exchanges.grad_parts("w_up", out_w_down[1]), chip_ids, m_up_t, v_up_t, "adam_w_up")
    out_w_up = [jnp.swapaxes(o, 0, 1) for o in out_up_t]
    out_w_out = _adamw_chips(w_out, *exchanges.grad_parts("w_out", out_up_t[1]), chip_ids, m_w_out[0], v_w_out[0], "adam_w_out")
    out_w_in = _adamw_chips(w_in, *exchanges.grad_parts("w_in", out_w_out[1]), chip_ids, m_w_in[0], v_w_in[0], "adam_w_in")
    out_pre_mix, out_post_mix, out_pre_ffn, out_post_ffn, out_sb, out_dil = exchanges.gain_steps
    out_conv_b = small_adam(conv_b, g_conv_b, m_conv_b, v_conv_b, "adam_conv_b")
    cw8 = [jnp.pad(t, ((0, 5), (0, 0))) for t in (conv_w, g_conv_w, m_conv_w[0], v_conv_w[0])]
    out_conv_w = [o[0:3] for o in _adamw(cw8[0], cw8[1][None], cw8[2], cw8[3], "adam_conv_w")]

    order = [out_pre_mix, out_post_mix, out_pre_ffn, out_post_ffn, [o[None] for o in out_w_in], out_sb, out_dil,
             [o[None] for o in out_w_out], [o[None] for o in out_w_up], [o[None] for o in out_conv_w], out_conv_b,
             [o[None] for o in out_w_down]]
    outs = [loss_out, grad_x[None]]
    for k in range(4):
        outs += [o[k] for o in order]
    return tuple(outs)
```

```python
import functools
import math

import jax
import jax.numpy as jnp
from jax import lax
from jax.experimental import pallas as pl
from jax.experimental.pallas import tpu as pltpu

F32 = jnp.float32
BF16 = jnp.bfloat16
HEAD_DIM = 128
LANES = 128
KEY_BLOCK = 128
DILATIONS = (1, 4, 16)
RMS_EPS = 1e-6
ROPE_THETA = 10000.0
NEG = -1e30
ADAM_LR, ADAM_B1, ADAM_B2, ADAM_EPS, ADAM_WD, ADAM_STEP = 0.001, 0.9, 0.999, 1e-08, 0.01, 10
MESH = pl.DeviceIdType.MESH
N_DEV = 8
N_CHIP = 4
HBM = pl.BlockSpec(memory_space=pl.ANY)
VMEM_LIMIT = 56 * 1024 * 1024

_pcall = pl.pallas_call


def _tile(n, pref, mult=LANES):
    best = None
    t = mult
    while t <= min(n, pref):
        if n % t == 0:
            best = t
        t += mult
    return n if best is None else best


def _params(*sem):
    return pltpu.CompilerParams(dimension_semantics=sem, vmem_limit_bytes=VMEM_LIMIT)


def _dot(a, b, dims):
    return lax.dot_general(a, b, (dims, ((), ())), preferred_element_type=F32)


NN = ((1,), (0,))
NT = ((1,), (1,))
TN = ((0,), (0,))


def _mm_body(dims, nk, tile):
    if nk == 1:
        def single(a_ref, b_ref, o_ref):
            o_ref[...] = _dot(a_ref[...].astype(BF16), b_ref[...].astype(BF16), dims).astype(o_ref.dtype)

        return single, []

    def body(a_ref, b_ref, o_ref, acc_ref):
        k = pl.program_id(2)

        @pl.when(k == 0)
        def _():
            acc_ref[...] = jnp.zeros_like(acc_ref)

        acc_ref[...] += _dot(a_ref[...].astype(BF16), b_ref[...].astype(BF16), dims)

        @pl.when(k == nk - 1)
        def _():
            o_ref[...] = acc_ref[...].astype(o_ref.dtype)

    return body, [pltpu.VMEM(tile, F32)]


def _mm_nn(a, b3, out_dtype, name, tm=1024, tn=1408, tk=2048, b_transposed=False):
    M, K = a.shape
    C, n = b3.shape[0], b3.shape[1 if b_transposed else 2]
    tm, tk, tn = _tile(M, tm, 8), _tile(K, tk), _tile(n, tn)
    npc, nk = n // tn, K // tk
    body, scratch = _mm_body(NT if b_transposed else NN, nk, (tm, tn))
    b_spec = (pl.BlockSpec((None, tn, tk), lambda i, j, k: (j // npc, j % npc, k)) if b_transposed
              else pl.BlockSpec((None, tk, tn), lambda i, j, k: (j // npc, k, j % npc)))
    return _pcall(
        body, grid=(M // tm, C * npc, nk),
        in_specs=[pl.BlockSpec((tm, tk), lambda i, j, k: (i, k)), b_spec],
        out_specs=pl.BlockSpec((tm, tn), lambda i, j, k: (i, j)),
        out_shape=jax.ShapeDtypeStruct((M, C * n), out_dtype), scratch_shapes=scratch,
        compiler_params=_params("parallel", "parallel", "arbitrary"), name=name)(a, b3)


def _mm_nt(a, b3, out_dtype, name, tm=1024, tn=1024, tk=2048, after=None, b_transposed=False):
    M, _ = a.shape
    C, N, n = (b3.shape[0], b3.shape[2], b3.shape[1]) if b_transposed else b3.shape
    tm, tn, tk = _tile(M, tm, 8), _tile(N, tn), _tile(n, tk)
    kpc = n // tk
    nk = C * kpc
    inner, scratch = _mm_body(NN if b_transposed else NT, nk, (tm, tn))
    extra = [] if after is None else [after]

    def body(a_ref, b_ref, *rest):
        inner(a_ref, b_ref, *rest[len(extra):])

    b_spec = (pl.BlockSpec((None, tk, tn), lambda i, j, k: (k // kpc, k % kpc, j)) if b_transposed
              else pl.BlockSpec((None, tn, tk), lambda i, j, k: (k // kpc, j, k % kpc)))
    return _pcall(
        body, grid=(M // tm, N // tn, nk),
        in_specs=[pl.BlockSpec((tm, tk), lambda i, j, k: (i, k)), b_spec] + [HBM] * len(extra),
        out_specs=pl.BlockSpec((tm, tn), lambda i, j, k: (i, j)),
        out_shape=jax.ShapeDtypeStruct((M, N), out_dtype), scratch_shapes=scratch,
        compiler_params=_params("parallel", "parallel", "arbitrary"), name=name)(a, b3, *extra)


def _mm_tn(x, y, n, out_dtype, name, tm=1024, tn=1408, tk=2048, after=None):
    S, P = x.shape
    C = y.shape[1] // n
    tm, tn, tk = _tile(P, tm), _tile(n, tn), _tile(S, tk, 8)
    npc, nk = n // tn, S // tk
    inner, scratch = _mm_body(TN, nk, (tm, tn))
    extra = [] if after is None else [after]

    def body(x_ref, y_ref, *rest):
        inner(x_ref, y_ref, *rest[len(extra):])

    return _pcall(
        body, grid=(P // tm, C * npc, nk),
        in_specs=[pl.BlockSpec((tk, tm), lambda i, j, k: (k, i)),
                  pl.BlockSpec((tk, tn), lambda i, j, k: (k, j))] + [HBM] * len(extra),
        out_specs=pl.BlockSpec((None, tm, tn), lambda i, j, k: (j // npc, i, j % npc)),
        out_shape=jax.ShapeDtypeStruct((C, P, n), out_dtype), scratch_shapes=scratch,
        compiler_params=_params("parallel", "parallel", "arbitrary"), name=name)(x, y, *extra)


def _rms_scale(v):
    return lax.rsqrt(jnp.mean(v * v, axis=-1, keepdims=True) + RMS_EPS)


def _rms_bwd(gy, v, r):
    return r * gy - v * (r * r * r * jnp.mean(gy * v, axis=-1, keepdims=True))


def _rows_spec(tm, d):
    return pl.BlockSpec((tm, d), lambda i: (i, 0))


def _vec_spec(d):
    return pl.BlockSpec((1, d), lambda i: (0, 0))


def _rms_fwd(x, g, name, tm=256):
    S, D = x.shape

    def body(x_ref, g_ref, h_ref):
        v = x_ref[...]
        h_ref[...] = (v * _rms_scale(v) * g_ref[...]).astype(BF16)

    return _pcall(body, grid=(S // tm,), in_specs=[_rows_spec(tm, D), _vec_spec(D)], out_specs=_rows_spec(tm, D),
                  out_shape=jax.ShapeDtypeStruct((S, D), BF16), compiler_params=_params("parallel"), name=name)(x, g)


def _mid_fwd(x, mix, g_post, g_pre, name, tm=256):
    S, D = x.shape

    def body(x_ref, m_ref, gp_ref, gn_ref, x2_ref, h_ref):
        m = m_ref[...]
        x2 = x_ref[...] + m * _rms_scale(m) * gp_ref[...]
        x2_ref[...] = x2
        h_ref[...] = (x2 * _rms_scale(x2) * gn_ref[...]).astype(BF16)

    return _pcall(body, grid=(S // tm,), in_specs=[_rows_spec(tm, D), _rows_spec(tm, D), _vec_spec(D), _vec_spec(D)],
                  out_specs=[_rows_spec(tm, D), _rows_spec(tm, D)],
                  out_shape=[jax.ShapeDtypeStruct((S, D), F32), jax.ShapeDtypeStruct((S, D), BF16)],
                  compiler_params=_params("parallel"), name=name)(x, mix, g_post, g_pre)


def _loss_bwd(x2, f, tgt, g_post, name, tm=256):
    S, D = x2.shape

    def body(x2_ref, f_ref, t_ref, g_ref, dy_ref, df_ref, dg_ref, ls_ref):
        i = pl.program_id(0)

        @pl.when(i == 0)
        def _():
            dg_ref[...] = jnp.zeros_like(dg_ref)
            ls_ref[...] = jnp.zeros_like(ls_ref)

        fv = f_ref[...]
        r = _rms_scale(fv)
        g = g_ref[...]
        err = x2_ref[...] + fv * r * g - t_ref[...]
        ls_ref[...] += jnp.broadcast_to(0.5 * jnp.sum(jnp.mean(err * err, axis=-1, keepdims=True), axis=0, keepdims=True), ls_ref.shape)
        dy = err * (1.0 / D)
        dy_ref[...] = dy
        df_ref[...] = _rms_bwd(dy * g, fv, r).astype(BF16)
        dg_ref[...] += jnp.sum(dy * fv * r, axis=0, keepdims=True)

    return _pcall(body, grid=(S // tm,),
                  in_specs=[_rows_spec(tm, D), _rows_spec(tm, D), _rows_spec(tm, D), _vec_spec(D)],
                  out_specs=[_rows_spec(tm, D), _rows_spec(tm, D), _vec_spec(D), _vec_spec(LANES)],
                  out_shape=[jax.ShapeDtypeStruct((S, D), F32), jax.ShapeDtypeStruct((S, D), BF16),
                             jax.ShapeDtypeStruct((1, D), F32), jax.ShapeDtypeStruct((1, LANES), F32)],
                  compiler_params=_params("arbitrary"), name=name)(x2, f, tgt, g_post)


def _mid_bwd(dy, dh2, x2, mix, g_pre, g_post, name, tm=256):
    S, D = dy.shape

    def body(dy_ref, dh_ref, x2_ref, m_ref, gn_ref, gp_ref, dx2_ref, dm_ref, dgn_ref, dgp_ref):
        i = pl.program_id(0)

        @pl.when(i == 0)
        def _():
            dgn_ref[...] = jnp.zeros_like(dgn_ref)
            dgp_ref[...] = jnp.zeros_like(dgp_ref)

        x2, dh = x2_ref[...], dh_ref[...]
        r = _rms_scale(x2)
        dx2 = dy_ref[...] + _rms_bwd(dh * gn_ref[...], x2, r)
        dgn_ref[...] += jnp.sum(dh * x2 * r, axis=0, keepdims=True)
        dx2_ref[...] = dx2
        m = m_ref[...]
        rm = _rms_scale(m)
        dm_ref[...] = _rms_bwd(dx2 * gp_ref[...], m, rm).astype(BF16)
        dgp_ref[...] += jnp.sum(dx2 * m * rm, axis=0, keepdims=True)

    return _pcall(body, grid=(S // tm,),
                  in_specs=[_rows_spec(tm, D)] * 4 + [_vec_spec(D)] * 2,
                  out_specs=[_rows_spec(tm, D), _rows_spec(tm, D), _vec_spec(D), _vec_spec(D)],
                  out_shape=[jax.ShapeDtypeStruct((S, D), F32), jax.ShapeDtypeStruct((S, D), BF16),
                             jax.ShapeDtypeStruct((1, D), F32), jax.ShapeDtypeStruct((1, D), F32)],
                  compiler_params=_params("arbitrary"), name=name)(dy, dh2, x2, mix, g_pre, g_post)


def _first_bwd(dx2, dh1, x, g_pre, name, tm=256):
    S, D = x.shape

    def body(dx2_ref, dh_ref, x_ref, g_ref, gx_ref, dg_ref):
        i = pl.program_id(0)

        @pl.when(i == 0)
        def _():
            dg_ref[...] = jnp.zeros_like(dg_ref)

        xv, dh = x_ref[...], dh_ref[...]
        r = _rms_scale(xv)
        gx_ref[...] = dx2_ref[...] + _rms_bwd(dh * g_ref[...], xv, r)
        dg_ref[...] += jnp.sum(dh * xv * r, axis=0, keepdims=True)

    return _pcall(body, grid=(S // tm,), in_specs=[_rows_spec(tm, D)] * 3 + [_vec_spec(D)],
                  out_specs=[_rows_spec(tm, D), _vec_spec(D)],
                  out_shape=[jax.ShapeDtypeStruct((S, D), F32), jax.ShapeDtypeStruct((1, D), F32)],
                  compiler_params=_params("arbitrary"), name=name)(dx2, dh1, x, g_pre)


def _logsig_pair(z):
    lb = jnp.minimum(z, 0.0) - jnp.log(1.0 + jnp.exp(-jnp.abs(z)))
    return lb, lb - z


SB_KEY_BLOCK = 256


def _sum_matrix(strict):
    ia = lax.broadcasted_iota(jnp.int32, (SB_KEY_BLOCK, SB_KEY_BLOCK), 0)
    ib = lax.broadcasted_iota(jnp.int32, (SB_KEY_BLOCK, SB_KEY_BLOCK), 1)
    return ((ia > ib) if strict == ">" else (ia < ib)).astype(BF16)


def _row_total(sums, v, col):
    return jnp.broadcast_to(sums[:, col:col + 1] + v[:, col:col + 1], (v.shape[0], LANES))


def _lanes(c, width):
    return jnp.tile(c, (1, width // LANES))


def _split_dot(v, u):
    hi = v.astype(BF16)
    lo = (v - hi.astype(F32)).astype(BF16)
    return _dot(hi, u, NN) + _dot(lo, u, NN)


def _head_out(o, g):
    return o * _rms_scale(o) * g


def _sb_fwd(proj, gain, n_heads, name, tq=1024):
    S = proj.shape[0]
    H, tk = n_heads, SB_KEY_BLOCK
    tq = _tile(S, tq, 2 * tk)
    scale = HEAD_DIM ** -0.5

    def body(q_ref, k_ref, v_ref, g_ref, o_ref, ct_ref, mx_ref, oacc, cacc):
        i = pl.program_id(1)
        oacc[...] = jnp.zeros_like(oacc)
        cacc[...] = jnp.zeros_like(cacc)
        sums = _sum_matrix(">")

        def block(k0, r0, diagonal):
            rows = pl.ds(r0, tq - r0)
            q = q_ref[rows, :].astype(BF16)
            kj = k_ref[pl.ds(k0, tk), :].astype(BF16)
            vj = v_ref[pl.ds(k0, tk), :].astype(BF16)
            lb, lk = _logsig_pair(_dot(q, kj, NT) * scale)
            if diagonal:
                causal = (lax.broadcasted_iota(jnp.int32, (tq - r0, tk), 1) < lax.broadcasted_iota(jnp.int32, (tq - r0, tk), 0))
                lk = jnp.where(causal, lk, 0.0)
            after = _split_dot(lk, sums)
            c = cacc[rows, :]
            a = jnp.exp(lb + after + _lanes(c, tk))
            if diagonal:
                a = jnp.where(causal, a, 0.0)
            oacc[rows, :] += _dot(a.astype(BF16), vj, NN)
            cacc[rows, :] = c + _row_total(after, lk, 0)

        for d in reversed(range(tq // tk)):
            block(pl.multiple_of(i * tq + d * tk, tk), d * tk, True)
        n_pairs = i * (tq // tk // 2)

        def step(it, carry):
            k0 = pl.multiple_of((n_pairs - 1 - it) * 2 * tk, 2 * tk)
            block(pl.multiple_of(k0 + tk, tk), 0, False)
            block(k0, 0, False)
            return carry

        lax.fori_loop(0, n_pairs, step, 0)
        o = oacc[...]
        o_ref[...] = o
        ct_ref[...] = cacc[...]
        mx_ref[...] = _head_out(o, g_ref[...]).astype(BF16)

    blk = pl.BlockSpec((tq, HEAD_DIM), lambda h, i: (i, h))
    return _pcall(
        body, grid=(H, S // tq),
        in_specs=[blk, pl.BlockSpec((S, HEAD_DIM), lambda h, i: (0, H + h)),
                  pl.BlockSpec((S, HEAD_DIM), lambda h, i: (0, 2 * H + h)), pl.BlockSpec((1, HEAD_DIM), lambda h, i: (0, h))],
        out_specs=[blk, blk, blk],
        out_shape=[jax.ShapeDtypeStruct((S, H * HEAD_DIM), F32), jax.ShapeDtypeStruct((S, H * HEAD_DIM), F32),
                   jax.ShapeDtypeStruct((S, H * HEAD_DIM), BF16)],
        scratch_shapes=[pltpu.VMEM((tq, HEAD_DIM), F32), pltpu.VMEM((tq, LANES), F32)],
        compiler_params=_params("parallel", "arbitrary"), name=name)(proj, proj, proj, gain)


def _sb_bwd(proj, gain, o_raw, ctot, dmixed, dm_col0, n_heads, name, tq=1024):
    S = proj.shape[0]
    H, tk = n_heads, SB_KEY_BLOCK
    tq = _tile(S, tq, 2 * tk)
    nq = S // tq
    scale = HEAD_DIM ** -0.5

    def body(q_ref, k_ref, v_ref, g_ref, o_ref, ct_ref, dm_ref, dq_ref, dk_ref, dv_ref, dg_ref,
             dkacc, dvacc, dqacc, pfx, gcar, dos):
        i = pl.program_id(1)

        @pl.when(i == 0)
        def _():
            dkacc[...] = jnp.zeros_like(dkacc)
            dvacc[...] = jnp.zeros_like(dvacc)
            dg_ref[...] = jnp.zeros_like(dg_ref)

        o, dm, g = o_ref[...], dm_ref[...], g_ref[...]
        r = _rms_scale(o)
        dos[...] = _rms_bwd(dm * g, o, r).astype(BF16)
        dg_ref[...] += jnp.broadcast_to(jnp.sum(dm * o * r, axis=0, keepdims=True), dg_ref.shape)
        dqacc[...] = jnp.zeros_like(dqacc)
        pfx[...] = jnp.zeros_like(pfx)
        gcar[...] = jnp.zeros_like(gcar)
        later, earlier = _sum_matrix(">"), _sum_matrix("<")

        def block(k0, r0, diagonal):
            rows = pl.ds(r0, tq - r0)
            keys = pl.ds(k0, tk)
            q, do = q_ref[rows, :].astype(BF16), dos[rows, :]
            kj, vj = k_ref[keys, :].astype(BF16), v_ref[keys, :].astype(BF16)
            lb, lk = _logsig_pair(_dot(q, kj, NT) * scale)
            if diagonal:
                causal = (lax.broadcasted_iota(jnp.int32, (tq - r0, tk), 1) < lax.broadcasted_iota(jnp.int32, (tq - r0, tk), 0))
                lk = jnp.where(causal, lk, 0.0)
            after = _split_dot(lk, later)
            p = pfx[rows, :] + _row_total(after, lk, 0)
            a = jnp.exp(lb + after + _lanes(ct_ref[rows, :] - p, tk))
            if diagonal:
                a = jnp.where(causal, a, 0.0)
            dl = _dot(do, vj, NT) * a
            dvacc[keys, :] += _dot(a.astype(BF16), do, TN)
            before = _dot(dl.astype(BF16), earlier, NN)
            gc = gcar[rows, :]
            sig = jnp.exp(lb)
            gsum = (before + _lanes(gc, tk)) * sig
            if diagonal:
                gsum = jnp.where(causal, gsum, 0.0)
            dz = ((dl * (1.0 - sig) - gsum) * scale).astype(BF16)
            dqacc[rows, :] += _dot(dz, kj, NN)
            dkacc[keys, :] += _dot(dz, q, TN)
            pfx[rows, :] = p
            gcar[rows, :] = gc + _row_total(before, dl, tk - 1)

        def step(j, carry):
            k0 = pl.multiple_of(j * 2 * tk, 2 * tk)
            block(k0, 0, False)
            block(pl.multiple_of(k0 + tk, tk), 0, False)
            return carry

        lax.fori_loop(0, i * (tq // tk // 2), step, 0)
        for d in range(tq // tk):
            block(pl.multiple_of(i * tq + d * tk, tk), d * tk, True)
        dq_ref[...] = dqacc[...].astype(BF16)

        @pl.when(i == nq - 1)
        def _():
            dk_ref[...] = dkacc[...].astype(BF16)
            dv_ref[...] = dvacc[...].astype(BF16)

    blk = pl.BlockSpec((tq, HEAD_DIM), lambda h, i: (i, h))
    full = pl.BlockSpec((S, HEAD_DIM), lambda h, i: (0, h))
    W = H * HEAD_DIM
    return _pcall(
        body, grid=(H, nq),
        in_specs=[blk, pl.BlockSpec((S, HEAD_DIM), lambda h, i: (0, H + h)),
                  pl.BlockSpec((S, HEAD_DIM), lambda h, i: (0, 2 * H + h)), pl.BlockSpec((1, HEAD_DIM), lambda h, i: (0, h)),
                  blk, blk, pl.BlockSpec((tq, HEAD_DIM), lambda h, i: (i, dm_col0 + h))],
        out_specs=[blk, full, full, pl.BlockSpec((8, HEAD_DIM), lambda h, i: (0, h))],
        out_shape=[jax.ShapeDtypeStruct((S, W), BF16), jax.ShapeDtypeStruct((S, W), BF16),
                   jax.ShapeDtypeStruct((S, W), BF16), jax.ShapeDtypeStruct((8, W), F32)],
        scratch_shapes=[pltpu.VMEM((S, HEAD_DIM), F32), pltpu.VMEM((S, HEAD_DIM), F32), pltpu.VMEM((tq, HEAD_DIM), F32),
                        pltpu.VMEM((tq, LANES), F32), pltpu.VMEM((tq, LANES), F32), pltpu.VMEM((tq, HEAD_DIM), BF16)],
        compiler_params=_params("arbitrary", "arbitrary"), name=name)(proj, proj, proj, gain, o_raw, ctot, dmixed)


def _rope_tables(S):
    inv_freq = ROPE_THETA ** (-jnp.arange(0, HEAD_DIM, 2, dtype=F32) / HEAD_DIM)
    ang = jnp.arange(S, dtype=F32)[:, None] * inv_freq[None, :]
    cos, sin = jnp.cos(ang), jnp.sin(ang)
    return jnp.concatenate([cos, cos], axis=1), jnp.concatenate([-sin, sin], axis=1)


def _rope(v, cos2, sin_signed):
    return v * cos2 + pltpu.roll(v, HEAD_DIM // 2, axis=1) * sin_signed


def _dil_rows(d, r, l0, n):
    if d == 1:
        return pl.ds(l0 if isinstance(l0, int) else pl.multiple_of(l0, KEY_BLOCK), n)
    return pl.ds(r + d * l0, n, stride=d)


def _dil_blocks(S, visit):
    B = KEY_BLOCK
    group = 8
    for b, d in enumerate(DILATIONS):
        nb = S // d // B
        if nb == 1:
            g = math.gcd(d, group)

            def trip(t, carry, b=b, d=d, g=g):
                for u in range(g):
                    visit(b, d, t * g + u, 0, True)
                return carry

            lax.fori_loop(0, d // g, trip, 0)
        elif d == 1:
            visit(b, d, 0, 0, True)
            g = max(k for k in range(1, group + 2) if (nb - 1) % k == 0)

            def trip(t, carry, b=b, d=d, g=g):
                for u in range(g):
                    visit(b, d, 0, (1 + t * g + u) * B, False)
                return carry

            lax.fori_loop(0, (nb - 1) // g, trip, 0)
        else:
            g = math.gcd(d, max(group // nb, 1))

            def trip(t, carry, b=b, d=d, nb=nb, g=g):
                for u in range(g):
                    visit(b, d, t * g + u, 0, True)
                    for n in range(1, nb):
                        visit(b, d, t * g + u, n * B, False)
                return carry

            lax.fori_loop(0, d // g, trip, 0)


def _dil_mask(first):
    B = KEY_BLOCK
    nk = B if first else 2 * B
    iq = lax.broadcasted_iota(jnp.int32, (B, nk), 0)
    ik = lax.broadcasted_iota(jnp.int32, (B, nk), 1)
    return (ik <= iq) if first else ((ik >= iq) & (ik <= iq + B))


def _dil_fwd(proj, cos2, sin_signed, gain, col0, n_heads, name):
    S = proj.shape[0]
    H, B = n_heads, KEY_BLOCK
    scale = HEAD_DIM ** -0.5
    rc = _tile(S, 256, 8)

    def body(q_ref, k_ref, v_ref, c_ref, s_ref, g_ref, o_ref, l_ref, mx_ref, qr, kr, *per_branch):
        ob, lb = per_branch[:len(DILATIONS)], per_branch[len(DILATIONS):]

        def rope_rows(t, carry):
            rows = pl.ds(pl.multiple_of(t * rc, rc), rc)
            qr[rows, :] = _rope(q_ref[rows, :], c_ref[rows, :], s_ref[rows, :])
            kr[rows, :] = _rope(k_ref[rows, :], c_ref[rows, :], s_ref[rows, :])
            return carry

        lax.fori_loop(0, S // rc, rope_rows, 0)

        def visit(b, d, r, l0, first):
            nk = B if first else 2 * B
            qrows = _dil_rows(d, r, l0, B)
            krows = qrows if first else _dil_rows(d, r, l0 - B, nk)
            s = _dot(qr[qrows, :].astype(BF16), kr[krows, :].astype(BF16), NT) * scale
            s = jnp.where(_dil_mask(first), s, NEG)
            m = jnp.max(s, axis=1, keepdims=True)
            p = jnp.exp(s - m)
            den = jnp.sum(p, axis=1, keepdims=True)
            ob[b][qrows, :] = _dot(p.astype(BF16), v_ref[krows, :].astype(BF16), NN) / den
            lb[b][qrows, :] = jnp.broadcast_to(m + jnp.log(den), (B, LANES))

        _dil_blocks(S, visit)

        def combine(t, carry):
            rows = pl.ds(pl.multiple_of(t * rc, rc), rc)
            l0, l1, l2 = lb[0][rows, :], lb[1][rows, :], lb[2][rows, :]
            m = jnp.maximum(jnp.maximum(l0, l1), l2)
            w0, w1, w2 = jnp.exp(l0 - m), jnp.exp(l1 - m), jnp.exp(l2 - m)
            den = w0 + w1 + w2
            o = (w0 * ob[0][rows, :] + w1 * ob[1][rows, :] + w2 * ob[2][rows, :]) / den
            o_ref[rows, :] = o
            l_ref[rows, :] = m + jnp.log(den)
            mx_ref[rows, :] = _head_out(o, g_ref[...]).astype(BF16)
            return carry

        lax.fori_loop(0, S // rc, combine, 0)

    def col(k):
        return pl.BlockSpec((S, HEAD_DIM), lambda h: (0, col0 + k * H + h))

    tab = pl.BlockSpec((S, HEAD_DIM), lambda h: (0, 0))
    out = pl.BlockSpec((S, HEAD_DIM), lambda h: (0, h))
    W = H * HEAD_DIM
    return _pcall(
        body, grid=(H,),
        in_specs=[col(0), col(1), col(2), tab, tab, pl.BlockSpec((1, HEAD_DIM), lambda h: (0, h))],
        out_specs=[out, out, out],
        out_shape=[jax.ShapeDtypeStruct((S, W), F32), jax.ShapeDtypeStruct((S, W), F32), jax.ShapeDtypeStruct((S, W), BF16)],
        scratch_shapes=[pltpu.VMEM((S, HEAD_DIM), F32)] * (2 + 2 * len(DILATIONS)),
        compiler_params=_params("parallel"), name=name)(proj, proj, proj, cos2, sin_signed, gain)


def _dil_bwd(proj, cos2, sin_signed, gain, o_raw, lse, dmixed, dm_col0, col0, n_heads, name):
    S = proj.shape[0]
    H, B = n_heads, KEY_BLOCK
    scale = HEAD_DIM ** -0.5
    rc = _tile(S, 256, 8)

    def body(q_ref, k_ref, v_ref, c_ref, s_ref, g_ref, o_ref, l_ref, dm_ref, dq_ref, dk_ref, dv_ref, dg_ref,
             qr, kr, dos, dsum, dqr, dkr, dvv):
        dg_ref[...] = jnp.zeros_like(dg_ref)

        def prep(t, carry):
            rows = pl.ds(pl.multiple_of(t * rc, rc), rc)
            qr[rows, :] = _rope(q_ref[rows, :], c_ref[rows, :], s_ref[rows, :])
            kr[rows, :] = _rope(k_ref[rows, :], c_ref[rows, :], s_ref[rows, :])
            o, dm = o_ref[rows, :], dm_ref[rows, :]
            r = _rms_scale(o)
            do = _rms_bwd(dm * g_ref[...], o, r)
            dg_ref[...] += jnp.broadcast_to(jnp.sum(dm * o * r, axis=0, keepdims=True), dg_ref.shape)
            dos[rows, :] = do
            dsum[rows, :] = jnp.broadcast_to(jnp.sum(do * o, axis=1, keepdims=True), (rc, LANES))
            dqr[rows, :] = jnp.zeros((rc, HEAD_DIM), F32)
            dkr[rows, :] = jnp.zeros((rc, HEAD_DIM), F32)
            dvv[rows, :] = jnp.zeros((rc, HEAD_DIM), F32)
            return carry

        lax.fori_loop(0, S // rc, prep, 0)

        def visit(b, d, r, l0, first):
            nk = B if first else 2 * B
            qrows = _dil_rows(d, r, l0, B)
            krows = qrows if first else _dil_rows(d, r, l0 - B, nk)
            qs, ks = qr[qrows, :].astype(BF16), kr[krows, :].astype(BF16)
            do = dos[qrows, :].astype(BF16)
            s = _dot(qs, ks, NT) * scale
            s = jnp.where(_dil_mask(first), s, NEG)
            p = jnp.exp(s - l_ref[qrows, :][:, 0:1])
            dp = _dot(do, v_ref[krows, :].astype(BF16), NT)
            ds = (p * (dp - dsum[qrows, :][:, 0:1]) * scale).astype(BF16)
            dqr[qrows, :] += _dot(ds, ks, NN)
            dkr[krows, :] += _dot(ds, qs, TN)
            dvv[krows, :] += _dot(p.astype(BF16), do, TN)

        _dil_blocks(S, visit)

        def finish(t, carry):
            rows = pl.ds(pl.multiple_of(t * rc, rc), rc)
            c, s = c_ref[rows, :], s_ref[rows, :]
            dq, dk = dqr[rows, :], dkr[rows, :]
            dq_ref[rows, :] = (dq * c + pltpu.roll(dq * s, HEAD_DIM // 2, axis=1)).astype(BF16)
            dk_ref[rows, :] = (dk * c + pltpu.roll(dk * s, HEAD_DIM // 2, axis=1)).astype(BF16)
            dv_ref[rows, :] = dvv[rows, :].astype(BF16)
            return carry

        lax.fori_loop(0, S // rc, finish, 0)

    def col(k):
        return pl.BlockSpec((S, HEAD_DIM), lambda h: (0, col0 + k * H + h))

    tab = pl.BlockSpec((S, HEAD_DIM), lambda h: (0, 0))
    out = pl.BlockSpec((S, HEAD_DIM), lambda h: (0, h))
    W = H * HEAD_DIM
    big = pltpu.VMEM((S, HEAD_DIM), F32)
    return _pcall(
        body, grid=(H,),
        in_specs=[col(0), col(1), col(2), tab, tab, pl.BlockSpec((1, HEAD_DIM), lambda h: (0, h)), out, out,
                  pl.BlockSpec((S, HEAD_DIM), lambda h: (0, dm_col0 + h))],
        out_specs=[out, out, out, pl.BlockSpec((8, HEAD_DIM), lambda h: (0, h))],
        out_shape=[jax.ShapeDtypeStruct((S, W), BF16), jax.ShapeDtypeStruct((S, W), BF16),
                   jax.ShapeDtypeStruct((S, W), BF16), jax.ShapeDtypeStruct((8, W), F32)],
        scratch_shapes=[big, big, big, pltpu.VMEM((S, LANES), F32), big, big, big],
        compiler_params=_params("parallel"), name=name)(proj, proj, proj, cos2, sin_signed, gain, o_raw, lse, dmixed)


GELU_C = math.sqrt(2.0 / math.pi)
GELU_A = 0.044715
HALO = 16


def _shift_down(cur, halo, k):
    out = pltpu.roll(cur, k, axis=0)
    row = lax.broadcasted_iota(jnp.int32, cur.shape, 0)
    for t in range(k):
        out = jnp.where(row == t, halo[HALO - k + t:HALO - k + t + 1, :], out)
    return out


def _shift_up(cur, halo, k):
    n = cur.shape[0]
    out = pltpu.roll(cur, n - k, axis=0)
    row = lax.broadcasted_iota(jnp.int32, cur.shape, 0)
    for t in range(k):
        out = jnp.where(row == n - k + t, halo[t:t + 1, :], out)
    return out


def _conv3(cur, halo, cw):
    return _shift_down(cur, halo, 2) * cw[0:1, :] + _shift_down(cur, halo, 1) * cw[1:2, :] + cur * cw[2:3, :] + cw[3:4, :]


def _gelu_parts(x):
    t = jnp.tanh(GELU_C * (x + GELU_A * x * x * x))
    return 0.5 * x * (1.0 + t), t


def _geglu_specs(tm, tn, ncb):
    hb = tm // HALO

    def cur(off):
        return pl.BlockSpec((tm, tn), lambda j, i: (i, off + j))

    def prev(off):
        return pl.BlockSpec((HALO, tn), lambda j, i: (jnp.maximum(i * hb - 1, 0), off + j))

    def taps(off):
        return pl.BlockSpec((8, tn), lambda j, i: (0, off + j))

    return [cur(0), prev(0), cur(ncb), prev(ncb), taps(0), taps(ncb)]


def _geglu_fwd(u, cwb, name, tm=256, tn=1408):
    S, F2 = u.shape
    F = F2 // 2
    tm, tn = _tile(S, tm, HALO), _tile(F, tn)
    ncb = F // tn

    def body(g_ref, gp_ref, v_ref, vp_ref, cg_ref, cv_ref, y_ref):
        top = pl.program_id(1) > 0
        gp = jnp.where(top, gp_ref[...].astype(F32), 0.0)
        vp = jnp.where(top, vp_ref[...].astype(F32), 0.0)
        gc = _conv3(g_ref[...].astype(F32), gp, cg_ref[...])
        vc = _conv3(v_ref[...].astype(F32), vp, cv_ref[...])
        y_ref[...] = (_gelu_parts(gc)[0] * vc).astype(BF16)

    return _pcall(body, grid=(ncb, S // tm), in_specs=_geglu_specs(tm, tn, ncb),
                  out_specs=pl.BlockSpec((tm, tn), lambda j, i: (i, j)),
                  out_shape=jax.ShapeDtypeStruct((S, F), BF16),
                  compiler_params=_params("parallel", "parallel"), name=name)(u, u, u, u, cwb, cwb)


def _geglu_bwd(u, dy, cwb, name, tm=256, tn=512):
    S, F2 = u.shape
    F = F2 // 2
    tm, tn = _tile(S, tm, HALO), _tile(F, tn)
    ncb = F // tn

    def body(g_ref, gp_ref, v_ref, vp_ref, cg_ref, cv_ref, dy_ref, dc_ref, dwg_ref, dwv_ref):
        i = pl.program_id(1)

        @pl.when(i == 0)
        def _():
            dwg_ref[...] = jnp.zeros_like(dwg_ref)
            dwv_ref[...] = jnp.zeros_like(dwv_ref)

        top = i > 0
        g, v = g_ref[...].astype(F32), v_ref[...].astype(F32)
        gp = jnp.where(top, gp_ref[...].astype(F32), 0.0)
        vp = jnp.where(top, vp_ref[...].astype(F32), 0.0)
        gc = _conv3(g, gp, cg_ref[...])
        vc = _conv3(v, vp, cv_ref[...])
        act, t = _gelu_parts(gc)
        dact = 0.5 * (1.0 + t) + 0.5 * gc * (1.0 - t * t) * GELU_C * (1.0 + 3.0 * GELU_A * gc * gc)
        dyv = dy_ref[...].astype(F32)
        dgc = dyv * vc * dact
        dvc = dyv * act
        dc_ref[0] = dgc.astype(BF16)
        dc_ref[1] = dvc.astype(BF16)

        def taps(out_ref, dc, cur, halo):
            out_ref[0:1, :] += jnp.sum(dc * _shift_down(cur, halo, 2), axis=0, keepdims=True)
            out_ref[1:2, :] += jnp.sum(dc * _shift_down(cur, halo, 1), axis=0, keepdims=True)
            out_ref[2:3, :] += jnp.sum(dc * cur, axis=0, keepdims=True)
            out_ref[3:4, :] += jnp.sum(dc, axis=0, keepdims=True)

        taps(dwg_ref, dgc, g, gp)
        taps(dwv_ref, dvc, v, vp)

    return _pcall(body, grid=(ncb, S // tm),
                  in_specs=_geglu_specs(tm, tn, ncb) + [pl.BlockSpec((tm, tn), lambda j, i: (i, j))],
                  out_specs=[pl.BlockSpec((2, tm, tn), lambda j, i: (0, i, j)),
                             pl.BlockSpec((8, tn), lambda j, i: (0, j)), pl.BlockSpec((8, tn), lambda j, i: (0, j))],
                  out_shape=[jax.ShapeDtypeStruct((2, S, F), BF16), jax.ShapeDtypeStruct((8, F), F32),
                             jax.ShapeDtypeStruct((8, F), F32)],
                  compiler_params=_params("parallel", "arbitrary"), name=name)(u, u, u, u, cwb, cwb, dy)


def _conv_bwd(dc, cwb, name, tm=512, tn=1408):
    _, S, F = dc.shape
    tm, tn = _tile(S, tm, HALO), _tile(F, tn)
    ncb, nrb = F // tn, S // tm
    hb = tm // HALO

    def body(c_ref, n_ref, w_ref, du_ref):
        cur = c_ref[...].astype(F32)
        nxt = jnp.where(pl.program_id(2) < nrb - 1, n_ref[...].astype(F32), 0.0)
        w = w_ref[...]
        du = cur * w[2:3, :] + _shift_up(cur, nxt, 1) * w[1:2, :] + _shift_up(cur, nxt, 2) * w[0:1, :]
        du_ref[...] = du.astype(BF16)

    return _pcall(body, grid=(2, ncb, nrb),
                  in_specs=[pl.BlockSpec((None, tm, tn), lambda c, j, i: (c, i, j)),
                            pl.BlockSpec((None, HALO, tn), lambda c, j, i: (c, jnp.minimum((i + 1) * hb, S // HALO - 1), j)),
                            pl.BlockSpec((8, tn), lambda c, j, i: (0, c * ncb + j))],
                  out_specs=pl.BlockSpec((tm, tn), lambda c, j, i: (i, c * ncb + j)),
                  out_shape=jax.ShapeDtypeStruct((S, 2 * F), BF16),
                  compiler_params=_params("parallel", "parallel", "parallel"), name=name)(dc, dc, cwb)


def _adam_math(w, g, m, v):
    m = ADAM_B1 * m + (1.0 - ADAM_B1) * g
    v = ADAM_B2 * v + (1.0 - ADAM_B2) * (g * g)
    m_hat = m / (1.0 - ADAM_B1 ** ADAM_STEP)
    v_hat = v / (1.0 - ADAM_B2 ** ADAM_STEP)
    return -ADAM_LR * (m_hat / (jnp.sqrt(v_hat) + ADAM_EPS) + ADAM_WD * w), m, v


def _adamw(w, parts, m, v, name, tr=256):
    R, C = w.shape
    n, _, Cp = parts.shape
    tr = _tile(R, tr, 8)

    def body(w_ref, p_ref, m_ref, v_ref, g_out, d_out, m_out, v_out):
        g = p_ref[0, :, 0:C].astype(F32)
        for k in range(1, n):
            g = g + p_ref[k, :, 0:C].astype(F32)
        d, mn, vn = _adam_math(w_ref[...], g, m_ref[...], v_ref[...])
        g_out[...] = g
        d_out[...] = d
        m_out[...] = mn
        v_out[...] = vn

    spec = pl.BlockSpec((tr, C), lambda i: (i, 0))
    shape = jax.ShapeDtypeStruct((R, C), F32)
    return _pcall(body, grid=(R // tr,), in_specs=[spec, pl.BlockSpec((n, tr, Cp), lambda i: (0, i, 0)), spec, spec],
                  out_specs=[spec] * 4, out_shape=[shape] * 4, compiler_params=_params("parallel"), name=name)(w, parts, m, v)


def _adamw_chips(w, pair, parts, chip_ids, m, v, name, tr=256):
    R, C = w.shape
    Cp = pair.shape[2]
    by_columns = C == Cp and _tile(R, tr, 16) < 64
    tr, tc = (R, _tile(C, 256)) if by_columns else (_tile(R, tr, 16), C)

    def body(ids_ref, w_ref, own_ref, p1_ref, p2_ref, p3_ref, m_ref, v_ref, g_out, d_out, m_out, v_out):
        g = own_ref[:, 0:tc].astype(F32)
        for ref in (p1_ref, p2_ref, p3_ref):
            g = g + ref[:, 0:tc].astype(F32)
        d, mn, vn = _adam_math(w_ref[...], g, m_ref[...], v_ref[...])
        g_out[...] = g
        d_out[...] = d
        m_out[...] = mn
        v_out[...] = vn

    if by_columns:
        spec = pl.BlockSpec((tr, tc), lambda j, ids: (0, j))
    else:
        spec = pl.BlockSpec((tr, tc), lambda i, ids: (i, 0))

    def chip(k):
        if by_columns:
            return pl.BlockSpec((None, tr, tc), lambda j, ids: (ids[k], 0, j))
        return pl.BlockSpec((None, tr, Cp), lambda i, ids: (ids[k], i, 0))

    shape = jax.ShapeDtypeStruct((R, C), F32)
    grid_spec = pltpu.PrefetchScalarGridSpec(
        num_scalar_prefetch=1, grid=(C // tc if by_columns else R // tr,),
        in_specs=[spec, chip(0), chip(1), chip(2), chip(3), spec, spec], out_specs=[spec] * 4)
    return _pcall(body, grid_spec=grid_spec, out_shape=[shape] * 4, compiler_params=_params("parallel"),
                  name=name)(chip_ids, w, pair, parts, parts, parts, m, v)


def _place():
    return lax.axis_index("x"), lax.axis_index("y"), lax.axis_index("c")


def _other_chips(x, y):
    return [(1 - x, y), (x, 1 - y), (1 - x, 1 - y)]


IN_HBM = pl.BlockSpec(memory_space=pltpu.HBM)
SEM = pl.BlockSpec(memory_space=pltpu.SEMAPHORE)
EFFECT = pltpu.SideEffectType.DATAFLOW_SIDE_EFFECTING
TOKEN = jax.ShapeDtypeStruct((8, LANES), F32)
TOKEN_SPEC = pl.BlockSpec(memory_space=pltpu.VMEM)


def _in_hbm(a):
    return pltpu.with_memory_space_constraint(a, pltpu.HBM)


def _landing(shape):
    return _in_hbm(lax.empty(shape.shape, shape.dtype))


def _hbm_like(a):
    return pltpu.HBM(a.shape, a.dtype)


def _gather_start(landing, slots, after, name):
    na = len(landing)

    def body(*refs):
        land = refs[:na]
        send_sems, recv_sems = refs[na + 1], refs[na + 2]
        token = refs[-1]
        x, y, c = _place()
        for a in range(na):
            own = slots[a](land[a], x, y, c)
            for k, to in enumerate([(x, y, 1 - c)] + [(*chip, c) for chip in _other_chips(x, y)]):
                pltpu.make_async_remote_copy(
                    src_ref=own, dst_ref=own, send_sem=send_sems.at[4 * a + k],
                    recv_sem=recv_sems.at[4 * a + k], device_id=to, device_id_type=MESH).start()
        token[...] = jnp.zeros_like(token)

    sems = pltpu.SemaphoreType.DMA((4 * na,))
    outs = _pcall(
        body, in_specs=[IN_HBM] * na + [HBM],
        out_specs=[SEM, SEM] + [IN_HBM] * na + [TOKEN_SPEC],
        out_shape=[sems, sems] + [_hbm_like(s) for s in landing] + [TOKEN],
        input_output_aliases={a: 2 + a for a in range(na)},
        compiler_params=pltpu.CompilerParams(has_side_effects=EFFECT), name=name,
    )(*[_in_hbm(s) for s in landing], after)
    return outs[0], outs[1], outs[2:2 + na], outs[-1]


def _gather_forward(gathered, send_sems, recv_sems, slots, after, name):
    na = len(gathered)

    def body(*refs):
        gath = refs[:na]
        send1, recv1 = refs[na], refs[na + 1]
        fsend, frecv = refs[na + 3], refs[na + 4]
        token = refs[-1]
        x, y, c = _place()
        chips = _other_chips(x, y)
        for a in range(na):
            for k, peer in enumerate([(x, y, 1 - c)] + [(*chip, c) for chip in chips]):
                arrival = pltpu.make_async_remote_copy(
                    src_ref=slots[a](gath[a], x, y, c), dst_ref=slots[a](gath[a], *peer), send_sem=send1.at[4 * a + k],
                    recv_sem=recv1.at[4 * a + k], device_id=peer, device_id_type=MESH)
                arrival.wait_send()
                arrival.wait_recv()
        for a in range(na):
            for j, chip in enumerate(chips):
                view = slots[a](gath[a], *chip, c)
                pltpu.make_async_remote_copy(
                    src_ref=view, dst_ref=view, send_sem=fsend.at[3 * a + j], recv_sem=frecv.at[3 * a + j],
                    device_id=(x, y, 1 - c), device_id_type=MESH).start()
        token[...] = jnp.zeros_like(token)

    sems = pltpu.SemaphoreType.DMA((3 * na,))
    outs = _pcall(
        body, in_specs=[IN_HBM] * na + [SEM, SEM, HBM],
        out_specs=[SEM, SEM] + [IN_HBM] * na + [TOKEN_SPEC],
        out_shape=[sems, sems] + [_hbm_like(g) for g in gathered] + [TOKEN],
        input_output_aliases={a: 2 + a for a in range(na)},
        compiler_params=pltpu.CompilerParams(has_side_effects=EFFECT), name=name,
    )(*gathered, send_sems, recv_sems, after)
    return outs[0], outs[1], outs[2:2 + na], outs[-1]


def _gather_finish(gathered, fsend, frecv, slots, after, name):
    na = len(gathered)

    def body(*refs):
        gath, fs, fr = refs[:na], refs[na], refs[na + 1]
        x, y, c = _place()
        for a in range(na):
            for j, chip in enumerate(_other_chips(x, y)):
                passed = pltpu.make_async_remote_copy(
                    src_ref=slots[a](gath[a], *chip, c), dst_ref=slots[a](gath[a], *chip, 1 - c),
                    send_sem=fs.at[3 * a + j], recv_sem=fr.at[3 * a + j], device_id=(x, y, 1 - c), device_id_type=MESH)
                passed.wait_send()
                passed.wait_recv()

    outs = _pcall(
        body, in_specs=[IN_HBM] * na + [SEM, SEM, HBM], out_specs=[IN_HBM] * na,
        out_shape=[_hbm_like(g) for g in gathered], input_output_aliases={a: a for a in range(na)},
        compiler_params=pltpu.CompilerParams(has_side_effects=EFFECT), name=name,
    )(*gathered, fsend, frecv, after)
    return list(outs)


def _pair_copy(view, src, land, send_sems, recv_sems, chip):
    x, y, c = _place()
    return pltpu.make_async_remote_copy(
        src_ref=view(src, chip, 1 - c), dst_ref=land.at[chip], send_sem=send_sems.at[chip], recv_sem=recv_sems.at[chip],
        device_id=(x, y, 1 - c), device_id_type=MESH)


def _pair_start(grad, view, block, after, name):
    def body(src, land, after_ref, send_sems, recv_sems, src_thru, land_thru, token):
        for chip in range(N_CHIP):
            _pair_copy(view, src, land, send_sems, recv_sems, chip).start()
        token[...] = jnp.zeros_like(token)

    sems = pltpu.SemaphoreType.DMA((N_CHIP,))
    land = jax.ShapeDtypeStruct((N_CHIP, *block), BF16)
    return _pcall(
        body, in_specs=[IN_HBM, IN_HBM, HBM], out_specs=[SEM, SEM, IN_HBM, IN_HBM, TOKEN_SPEC],
        out_shape=[sems, sems, _hbm_like(grad), _hbm_like(land), TOKEN], input_output_aliases={0: 2, 1: 3},
        compiler_params=pltpu.CompilerParams(has_side_effects=EFFECT), name=name,
    )(_in_hbm(grad), _landing(land), after)


def _pair_wait(grad, recv, send_sems, recv_sems, view, after, name):
    def body(src, land, send, recv_s, after_ref, src_thru, land_thru):
        for chip in range(N_CHIP):
            copy = _pair_copy(view, src, land, send, recv_s, chip)
            copy.wait_send()
            copy.wait_recv()

    return _pcall(
        body, in_specs=[IN_HBM, IN_HBM, SEM, SEM, HBM], out_specs=[IN_HBM, IN_HBM],
        out_shape=[_hbm_like(grad), _hbm_like(recv)], input_output_aliases={0: 0, 1: 1},
        compiler_params=pltpu.CompilerParams(has_side_effects=EFFECT), name=name,
    )(grad, recv, send_sems, recv_sems, after)


def _chip_start(pair, after, name):
    def body(src, land, after_ref, send_sems, recv_sems, src_thru, land_thru, token):
        x, y, c = _place()
        for j, (px, py) in enumerate(_other_chips(x, y)):
            pltpu.make_async_remote_copy(
                src_ref=src.at[2 * px + py], dst_ref=land.at[2 * x + y], send_sem=send_sems.at[j], recv_sem=recv_sems.at[j],
                device_id=(px, py, c), device_id_type=MESH).start()
        token[...] = jnp.zeros_like(token)

    sems = pltpu.SemaphoreType.DMA((3,))
    return _pcall(
        body, in_specs=[IN_HBM, IN_HBM, HBM], out_specs=[SEM, SEM, IN_HBM, IN_HBM, TOKEN_SPEC],
        out_shape=[sems, sems, _hbm_like(pair), _hbm_like(pair), TOKEN], input_output_aliases={0: 2, 1: 3},
        compiler_params=pltpu.CompilerParams(has_side_effects=EFFECT), name=name,
    )(_in_hbm(pair), _landing(pair), after)


def _chip_wait(pair, parts, send_sems, recv_sems, after, name):
    def body(src, land, send, recv, after_ref, src_thru, land_thru):
        x, y, c = _place()
        for j, (px, py) in enumerate(_other_chips(x, y)):
            copy = pltpu.make_async_remote_copy(
                src_ref=src.at[2 * px + py], dst_ref=land.at[2 * px + py], send_sem=send.at[j], recv_sem=recv.at[j],
                device_id=(px, py, c), device_id_type=MESH)
            copy.wait_send()
            copy.wait_recv()

    return _pcall(
        body, in_specs=[IN_HBM, IN_HBM, SEM, SEM, HBM], out_specs=[IN_HBM, IN_HBM],
        out_shape=[_hbm_like(pair), _hbm_like(parts)], input_output_aliases={0: 0, 1: 1},
        compiler_params=pltpu.CompilerParams(has_side_effects=EFFECT), name=name,
    )(pair, parts, send_sems, recv_sems, after)


def _pair_add(core, grad, recv, block, grad_spec, name):
    _, R, C = recv.shape
    tr = block

    def body(c_ref, g_ref, r_ref, o_ref):
        o_ref[...] = (g_ref[...].astype(F32) + r_ref[...].astype(F32)).astype(BF16)

    grid_spec = pltpu.PrefetchScalarGridSpec(
        num_scalar_prefetch=1, grid=(N_CHIP, R // tr),
        in_specs=[grad_spec, pl.BlockSpec((None, tr, C), lambda k, i, c: (k, i, 0))],
        out_specs=pl.BlockSpec((None, tr, C), lambda k, i, c: (k, i, 0)))
    return _pcall(body, grid_spec=grid_spec, out_shape=jax.ShapeDtypeStruct(recv.shape, BF16),
                  compiler_params=_params("parallel", "parallel"), name=name)(core, grad, recv)


def _small_step(parts, params, name):
    na, npar = len(parts), len(params)

    def body(*refs):
        p_refs, wmv = refs[:na], refs[na:na + 3 * npar]
        o_parts = refs[na + 3 * npar:2 * na + 3 * npar]
        o_params = refs[2 * na + 3 * npar:2 * na + 7 * npar]
        alls, (send_sems, recv_sems) = refs[2 * na + 7 * npar:3 * na + 7 * npar], refs[3 * na + 7 * npar:]
        x, y, c = _place()
        me = 4 * x + 2 * y + c
        peers = [(x, y, 1 - c)] + [(px, py, pc) for px, py in _other_chips(x, y) for pc in (c, 1 - c)]
        copies = []
        for a in range(na):
            alls[a][me] = p_refs[a][...]
            copies += [pltpu.make_async_remote_copy(
                src_ref=p_refs[a], dst_ref=alls[a].at[me], send_sem=send_sems.at[7 * a + k], recv_sem=recv_sems.at[7 * a + k],
                device_id=peer, device_id_type=MESH) for k, peer in enumerate(peers)]
        for cp in copies:
            cp.start()
        for a in range(na):
            for k, (px, py, pc) in enumerate(peers):
                pltpu.make_async_remote_copy(
                    src_ref=p_refs[a], dst_ref=alls[a].at[4 * px + 2 * py + pc], send_sem=send_sems.at[7 * a + k],
                    recv_sem=recv_sems.at[7 * a + k], device_id=peers[k], device_id_type=MESH).wait_recv()
        for cp in copies:
            cp.wait_send()
        sums = []
        for a in range(na):
            acc = alls[a][0]
            for k in range(1, N_DEV):
                acc = acc + alls[a][k]
            o_parts[a][...] = acc
            sums.append(acc)
        for j, (a, row, _, _, _) in enumerate(params):
            g = sums[a][row:row + 1, :]
            d, mn, vn = _adam_math(wmv[3 * j][...], g, wmv[3 * j + 1][...], wmv[3 * j + 2][...])
            for out, val in zip(o_params[4 * j:4 * j + 4], (g, d, mn, vn)):
                out[...] = val

    vm = pl.BlockSpec(memory_space=pltpu.VMEM)
    flat = [t for p in params for t in p[2:]]
    out_shape = [jax.ShapeDtypeStruct(p.shape, F32) for p in parts]
    out_shape += [jax.ShapeDtypeStruct(p[2].shape, F32) for p in params for _ in range(4)]
    outs = _pcall(body, in_specs=[vm] * (na + 3 * npar), out_specs=[vm] * len(out_shape), out_shape=out_shape,
                  scratch_shapes=[pltpu.VMEM((N_DEV, *p.shape), F32) for p in parts]
                  + [pltpu.SemaphoreType.DMA((7 * na,)), pltpu.SemaphoreType.DMA((7 * na,))],
                  name=name)(*parts, *flat)
    return outs[:na], [outs[na + 4 * j:na + 4 * j + 4] for j in range(npar)]


def _local_step(x, tgt, gains, weights):
    g_pre_mix, g_post_mix, g_pre_ffn, g_post_ffn, g_sb, g_dil = gains
    S, D = x.shape
    hs = g_sb.shape[1] // HEAD_DIM
    hd = g_dil.shape[1] // HEAD_DIM
    cos2, sin_signed = _rope_tables(S)

    h1 = _rms_fwd(x, g_pre_mix + weights.start(), "rms_in")
    w_in_g = weights.w_in(h1)
    proj = _mm_nn(h1, w_in_g, F32, "proj", tn=768)
    o_sb, ct_sb, mx_sb = _sb_fwd(proj, g_sb, hs, "sb_fwd")
    o_dl, lse_dl, mx_dl = _dil_fwd(proj, cos2, sin_signed, g_dil + weights.forward_out(o_sb), 3 * hs, hd, "dil_fwd")
    w_out_g, dep = weights.w_out(o_dl)
    mixed = jnp.concatenate([mx_sb, mx_dl], axis=1)
    mix = _mm_nn(mixed, w_out_g, F32, "mix_out", tn=1024)
    x2, h2 = _mid_fwd(x, mix, g_post_mix + dep, g_pre_ffn, "mid_fwd")
    w_up_g, cwb = weights.w_up(h2)
    u = _mm_nn(h2, w_up_g, BF16, "ffn_up", b_transposed=True)
    y = _geglu_fwd(u, cwb + weights.forward_down(u), "geglu_fwd")
    w_down_g = weights.w_down(y)
    f = _mm_nn(y, w_down_g, F32, "ffn_down", tn=1024, tk=1408)

    dy, df, dg_post_ffn, loss = _loss_bwd(x2, f, tgt, g_post_ffn, "loss_bwd")
    dyv = _mm_nt(df, w_down_g, BF16, "d_y", tn=1408)
    dw_down = _mm_tn(y, df, D, BF16, "dw_down", tm=1408, tn=1024)
    dc, dcw_g, dcw_v = _geglu_bwd(u, dyv, cwb + weights.grad("w_down", dw_down), "geglu_bwd")
    du = _conv_bwd(dc, cwb + weights.grad_reduce("w_down", dc), "conv_bwd")
    dh2 = _mm_nt(du, w_up_g, F32, "d_h2", tk=1408, b_transposed=True)
    dw_up = _mm_tn(du, h2, D, BF16, "dw_up", tm=1408, tn=1024)
    dx2, dmix, dg_pre_ffn, dg_post_mix = _mid_bwd(
        dy, dh2, x2, mix, g_pre_ffn + weights.grad("w_up", dw_up), g_post_mix, "mid_bwd")
    dmixed = _mm_nt(dmix, w_out_g, F32, "d_mixed", after=jnp.reshape(weights.grad_reduce("w_up", dmix), (1, 1)))
    dw_out = _mm_tn(mixed, dmix, D, BF16, "dw_out", tn=1024)
    dq_s, dk_s, dv_s, dg_sb = _sb_bwd(proj, g_sb + weights.grad("w_out", dw_out), o_sb, ct_sb, dmixed, 0, hs, "sb_bwd")
    dq_d, dk_d, dv_d, dg_dil = _dil_bwd(proj, cos2, sin_signed, g_dil + weights.grad_reduce("w_out", dq_s), o_dl, lse_dl,
                                        dmixed, hs, 3 * hs, hd, "dil_bwd")
    dproj = jnp.concatenate([dq_s, dk_s, dv_s, dq_d, dk_d, dv_d], axis=1)
    dw_in = _mm_tn(h1, dproj, w_in_g.shape[2], BF16, "dw_in", tn=768)
    weights.grad("w_in", dw_in)
    dep = weights.grad_reduce("w_in", dproj)
    dh1 = _mm_nt(dproj, w_in_g, F32, "d_h1", tk=768, after=jnp.reshape(dep, (1, 1)))
    grad_x, dg_pre_mix = _first_bwd(dx2, dh1, x, g_pre_mix, "first_bwd")
    small = (dg_pre_mix, dg_post_mix, dg_pre_ffn, dg_post_ffn, dg_sb[0:1], dg_dil[0:1], jnp.concatenate([dcw_g, dcw_v], axis=1))
    weights.small(small, loss)
    return loss, grad_x, small


def _pad_cols(a, to):
    return jnp.pad(a, ((0, 0), (0, to - a.shape[1])))


def kernel(x, pre_mix_gain, post_mix_gain, pre_ffn_gain, post_ffn_gain, w_in, sb_out_gain, dil_out_gain, w_out, w_up, conv_w, conv_b, w_down, loss_target, m_pre_mix_gain, m_post_mix_gain, m_pre_ffn_gain, m_post_ffn_gain, m_w_in, m_sb_out_gain, m_dil_out_gain, m_w_out, m_w_up, m_conv_w, m_conv_b, m_w_down, v_pre_mix_gain, v_post_mix_gain, v_pre_ffn_gain, v_post_ffn_gain, v_w_in, v_sb_out_gain, v_dil_out_gain, v_w_out, v_w_up, v_conv_w, v_conv_b, v_w_down):
    xb, tb = x[0], loss_target[0]
    S, D = xb.shape
    w_in, w_out, w_up, w_down, conv_w = w_in[0], w_out[0], w_up[0], w_down[0], conv_w[0]
    n_in, e_rows = w_in.shape[1], w_out.shape[0]
    cu, half = w_up.shape[1], w_down.shape[0]
    assert cu == 2 * half and half % 16 == 0
    cup = -(-cu // LANES) * LANES
    fp = N_CHIP * cup
    px, py, pc = _place()
    me = 4 * px + 2 * py + pc
    core = jnp.reshape(pc, (1,)).astype(jnp.int32)

    w_up_t, m_up_t, v_up_t = (jnp.swapaxes(t, 0, 1) for t in (w_up, m_w_up[0], v_w_up[0]))

    def by_dev(ref, qx, qy, qc):
        return ref.at[4 * qx + 2 * qy + qc]

    def down_slot(ref, qx, qy, qc):
        return ref.at[2 * qx + qy, pl.ds(qc * half, half)]

    def by_pair(ref, chip, k):
        return ref.at[chip, k]

    def down_pair(ref, chip, k):
        return ref.at[chip, pl.ds(k * half, half)]

    def pair_spec(tr, cols):
        return pl.BlockSpec((None, None, tr, cols), lambda k, i, c: (k, c[0], i, 0))

    tr_in, tr_up = _tile(D, 512, 16), _tile(cup, 256, 16)
    grad_plan = {
        "w_in": ((N_CHIP, 2, D, n_in), by_pair, (D, n_in), tr_in, pair_spec(tr_in, n_in)),
        "w_out": ((N_CHIP, 2, e_rows, D), by_pair, (e_rows, D), e_rows, pair_spec(e_rows, D)),
        "w_up": ((N_CHIP, 2, cup, D), by_pair, (cup, D), tr_up, pair_spec(tr_up, D)),
        "w_down": ((N_CHIP, cup, D), down_pair, (half, D), half,
                   pl.BlockSpec((None, half, D), lambda k, i, c: (k, c[0], 0))),
    }

    class Exchanges:
        def __init__(self):
            self.in_flight = {}

        def start(self):
            def own_slot(shard):
                return lax.dynamic_update_index_in_dim(lax.empty((N_DEV, *shard.shape), shard.dtype), shard, me, 0)

            self.g_in = _gather_start([own_slot(w_in.astype(BF16))], [by_dev], core, "gather_in_start")
            zero = self.g_in[3][0, 0]
            self.g_out = _gather_start([own_slot((w_out + zero).astype(BF16))], [by_dev], self.g_in[3], "gather_out_start")
            up = jnp.pad(w_up_t + zero, ((0, cup - cu), (0, 0))).astype(BF16)
            taps = jnp.pad(conv_w + zero, ((0, 8 - conv_w.shape[0]), (0, cup - cu)))
            self.g_up = _gather_start([own_slot(up), own_slot(taps)], [by_dev, by_dev], self.g_out[3], "gather_up_start")
            down = lax.dynamic_update_slice(jnp.zeros((N_CHIP, cup, D), BF16), (w_down + zero).astype(BF16)[None],
                                            (2 * px + py, pc * half, 0))
            self.g_down = _gather_start([down], [down_slot], self.g_up[3], "gather_down_start")
            return self.g_down[3][0, 0]

        def w_in(self, after):
            send, recv, gath, _ = self.g_in
            fsend, frecv, gath, token = _gather_forward(gath, send, recv, [by_dev], after, "gather_in_forward")
            return _gather_finish(gath, fsend, frecv, [by_dev], token, "gather_in_finish")[0]

        def forward_out(self, after):
            send, recv, gath, _ = self.g_out
            self.p_out = _gather_forward(gath, send, recv, [by_dev], after, "gather_out_forward")
            return self.p_out[3][0, 0]

        def w_out(self, after):
            fsend, frecv, gath, _ = self.p_out
            w_out_g = _gather_finish(gath, fsend, frecv, [by_dev], after, "gather_out_finish")[0]
            send, recv, gath, _ = self.g_up
            self.p_up = _gather_forward(gath, send, recv, [by_dev, by_dev], w_out_g, "gather_up_forward")
            return w_out_g.reshape(1, N_DEV * e_rows, D), self.p_up[3][0, 0]

        def w_up(self, after):
            fsend, frecv, gath, _ = self.p_up
            w_up_g, cw_g = _gather_finish(gath, fsend, frecv, [by_dev, by_dev], after, "gather_up_finish")
            cb = _pad_cols(conv_b.reshape(N_DEV, cu), cup).reshape(1, 2 * fp)
            cw_full = jnp.transpose(cw_g[:, :3, :], (1, 0, 2)).reshape(3, 2 * fp)
            cwb = jnp.concatenate([cw_full, cb, jnp.zeros((4, 2 * fp), F32)], axis=0)
            return w_up_g, cwb

        def forward_down(self, after):
            send, recv, gath, _ = self.g_down
            self.p_down = _gather_forward(gath, send, recv, [down_slot], after, "gather_down_forward")
            return self.p_down[3][0, 0]

        def w_down(self, after):
            fsend, frecv, gath, _ = self.p_down
            return _gather_finish(gath, fsend, frecv, [down_slot], after, "gather_down_finish")[0].reshape(1, fp, D)

        def small(self, small, loss):
            d_pre_mix, d_post_mix, d_pre_ffn, d_post_ffn, d_sb, d_dil, d_conv = small

            def rows_of(*vectors):
                n = vectors[0].shape[1]
                row = lax.broadcasted_iota(jnp.int32, (8, n), 0)
                out = jnp.zeros((8, n), F32)
                for k, vec in enumerate(vectors):
                    out = jnp.where(row == k, vec, out)
                return out

            parts = [rows_of(d_pre_mix, d_post_mix, d_pre_ffn, d_post_ffn, jnp.broadcast_to(loss[:, :1], (1, D))),
                     rows_of(d_sb, d_dil), d_conv]
            params = [(0, 0, pre_mix_gain, m_pre_mix_gain, v_pre_mix_gain), (0, 1, post_mix_gain, m_post_mix_gain, v_post_mix_gain),
                      (0, 2, pre_ffn_gain, m_pre_ffn_gain, v_pre_ffn_gain), (0, 3, post_ffn_gain, m_post_ffn_gain, v_post_ffn_gain),
                      (1, 0, sb_out_gain, m_sb_out_gain, v_sb_out_gain), (1, 1, dil_out_gain, m_dil_out_gain, v_dil_out_gain)]
            (gains_sum, _, self.conv_sum), self.gain_steps = _small_step(parts, params, "small_step")
            self.loss_sum = gains_sum[4, 0]
            return self.conv_sum

        def grad(self, name, dw):
            view_shape, view, block, tr, spec = grad_plan[name]
            send, recv_sems, dw, recv, token = _pair_start(dw.reshape(view_shape), view, block, core, "pair_start_" + name)
            self.in_flight[name] = (dw, recv, send, recv_sems)
            return token[0, 0]

        def grad_reduce(self, name, after):
            _, view, _, tr, spec = grad_plan[name]
            dw, recv = _pair_wait(*self.in_flight[name], view, after, "pair_wait_" + name)
            pair = _pair_add(core, dw, recv, tr, spec, "pair_add_" + name)
            send, recv_sems, pair, parts, token = _chip_start(pair, recv, "chip_start_" + name)
            self.in_flight[name] = (pair, parts, send, recv_sems)
            self.last_token = token
            return token[0, 0]

        def grad_parts(self, name, after):
            return _chip_wait(*self.in_flight[name], after, "chip_wait_" + name)

    exchanges = Exchanges()
    gains = (pre_mix_gain, post_mix_gain, pre_ffn_gain, post_ffn_gain, sb_out_gain, dil_out_gain)
    loss, grad_x, small = _local_step(xb, tb, gains, exchanges)

    loss_out, g_conv = exchanges.loss_sum, exchanges.conv_sum
    g_conv_b = g_conv[3].reshape(N_DEV, cup)[:, :cu].reshape(1, N_DEV * cu)
    g_conv_w = lax.dynamic_index_in_dim(g_conv[0:3].reshape(3, N_DEV, cup), me, axis=1, keepdims=False)[:, :cu]

    def small_adam(w, g, m, v, name):
        one = w.shape[0] == 1
        if one:
            w, g, m, v = (jnp.broadcast_to(t, (8, t.shape[1])) for t in (w, g, m, v))
        outs = _adamw(w, g[None], m, v, name)
        return [o[0:1] for o in outs] if one else outs

    chip_ids = jnp.stack([2 * px + py, 2 * (1 - px) + py, 2 * px + 1 - py, 2 * (1 - px) + 1 - py]).astype(jnp.int32)
    out_w_down = _adamw_chips(w_down, *exchanges.grad_parts("w_down", exchanges.conv_sum), chip_ids, m_w_down[0], v_w_down[0], "adam_w_down")
    out_up_t = _adamw_chips(w_up_t, *exchanges.grad_parts("w_up", out_w_down[1]), chip_ids, m_up_t, v_up_t, "adam_w_up")
    out_w_up = [jnp.swapaxes(o, 0, 1) for o in out_up_t]
    out_w_out = _adamw_chips(w_out, *exchanges.grad_parts("w_out", out_up_t[1]), chip_ids, m_w_out[0], v_w_out[0], "adam_w_out")
    out_w_in = _adamw_chips(w_in, *exchanges.grad_parts("w_in", out_w_out[1]), chip_ids, m_w_in[0], v_w_in[0], "adam_w_in")
    out_pre_mix, out_post_mix, out_pre_ffn, out_post_ffn, out_sb, out_dil = exchanges.gain_steps
    out_conv_b = small_adam(conv_b, g_conv_b, m_conv_b, v_conv_b, "adam_conv_b")
    cw8 = [jnp.pad(t, ((0, 5), (0, 0))) for t in (conv_w, g_conv_w, m_conv_w[0], v_conv_w[0])]
    out_conv_w = [o[0:3] for o in _adamw(cw8[0], cw8[1][None], cw8[2], cw8[3], "adam_conv_w")]

    order = [out_pre_mix, out_post_mix, out_pre_ffn, out_post_ffn, [o[None] for o in out_w_in], out_sb, out_dil,
             [o[None] for o in out_w_out], [o[None] for o in out_w_up], [o[None] for o in out_conv_w], out_conv_b,
             [o[None] for o in out_w_down]]
    outs = [loss_out, grad_x[None]]
    for k in range(4):
        outs += [o[k] for o in order]
    return tuple(outs)
```

```python
import functools
import math

import jax
import jax.numpy as jnp
from jax import lax
from jax.experimental import pallas as pl
from jax.experimental.pallas import tpu as pltpu

F32 = jnp.float32
BF16 = jnp.bfloat16
HEAD_DIM = 128
LANES = 128
KEY_BLOCK = 128
DILATIONS = (1, 4, 16)
RMS_EPS = 1e-6
ROPE_THETA = 10000.0
NEG = -1e30
ADAM_LR, ADAM_B1, ADAM_B2, ADAM_EPS, ADAM_WD, ADAM_STEP = 0.001, 0.9, 0.999, 1e-08, 0.01, 10
MESH = pl.DeviceIdType.MESH
N_DEV = 8
N_CHIP = 4
HBM = pl.BlockSpec(memory_space=pl.ANY)
VMEM_LIMIT = 56 * 1024 * 1024

_pcall = pl.pallas_call


def _tile(n, pref, mult=LANES):
    best = None
    t = mult
    while t <= min(n, pref):
        if n % t == 0:
            best = t
        t += mult
    return n if best is None else best


def _params(*sem):
    return pltpu.CompilerParams(dimension_semantics=sem, vmem_limit_bytes=VMEM_LIMIT)


def _dot(a, b, dims):
    return lax.dot_general(a, b, (dims, ((), ())), preferred_element_type=F32)


NN = ((1,), (0,))
NT = ((1,), (1,))
TN = ((0,), (0,))


def _mm_body(dims, nk, tile):
    if nk == 1:
        def single(a_ref, b_ref, o_ref):
            o_ref[...] = _dot(a_ref[...].astype(BF16), b_ref[...].astype(BF16), dims).astype(o_ref.dtype)

        return single, []

    def body(a_ref, b_ref, o_ref, acc_ref):
        k = pl.program_id(2)

        @pl.when(k == 0)
        def _():
            acc_ref[...] = jnp.zeros_like(acc_ref)

        acc_ref[...] += _dot(a_ref[...].astype(BF16), b_ref[...].astype(BF16), dims)

        @pl.when(k == nk - 1)
        def _():
            o_ref[...] = acc_ref[...].astype(o_ref.dtype)

    return body, [pltpu.VMEM(tile, F32)]


def _mm_nn(a, b3, out_dtype, name, tm=1024, tn=1408, tk=2048, b_transposed=False):
    M, K = a.shape
    C, n = b3.shape[0], b3.shape[1 if b_transposed else 2]
    tm, tk, tn = _tile(M, tm, 8), _tile(K, tk), _tile(n, tn)
    npc, nk = n // tn, K // tk
    body, scratch = _mm_body(NT if b_transposed else NN, nk, (tm, tn))
    b_spec = (pl.BlockSpec((None, tn, tk), lambda i, j, k: (j // npc, j % npc, k)) if b_transposed
              else pl.BlockSpec((None, tk, tn), lambda i, j, k: (j // npc, k, j % npc)))
    return _pcall(
        body, grid=(M // tm, C * npc, nk),
        in_specs=[pl.BlockSpec((tm, tk), lambda i, j, k: (i, k)), b_spec],
        out_specs=pl.BlockSpec((tm, tn), lambda i, j, k: (i, j)),
        out_shape=jax.ShapeDtypeStruct((M, C * n), out_dtype), scratch_shapes=scratch,
        compiler_params=_params("parallel", "parallel", "arbitrary"), name=name)(a, b3)


def _mm_nt(a, b3, out_dtype, name, tm=1024, tn=1024, tk=2048, after=None, b_transposed=False):
    M, _ = a.shape
    C, N, n = (b3.shape[0], b3.shape[2], b3.shape[1]) if b_transposed else b3.shape
    tm, tn, tk = _tile(M, tm, 8), _tile(N, tn), _tile(n, tk)
    kpc = n // tk
    nk = C * kpc
    inner, scratch = _mm_body(NN if b_transposed else NT, nk, (tm, tn))
    extra = [] if after is None else [after]

    def body(a_ref, b_ref, *rest):
        inner(a_ref, b_ref, *rest[len(extra):])

    b_spec = (pl.BlockSpec((None, tk, tn), lambda i, j, k: (k // kpc, k % kpc, j)) if b_transposed
              else pl.BlockSpec((None, tn, tk), lambda i, j, k: (k // kpc, j, k % kpc)))
    return _pcall(
        body, grid=(M // tm, N // tn, nk),
        in_specs=[pl.BlockSpec((tm, tk), lambda i, j, k: (i, k)), b_spec] + [HBM] * len(extra),
        out_specs=pl.BlockSpec((tm, tn), lambda i, j, k: (i, j)),
        out_shape=jax.ShapeDtypeStruct((M, N), out_dtype), scratch_shapes=scratch,
        compiler_params=_params("parallel", "parallel", "arbitrary"), name=name)(a, b3, *extra)


def _mm_tn(x, y, n, out_dtype, name, tm=1024, tn=1408, tk=2048, after=None):
    S, P = x.shape
    C = y.shape[1] // n
    tm, tn, tk = _tile(P, tm), _tile(n, tn), _tile(S, tk, 8)
    npc, nk = n // tn, S // tk
    inner, scratch = _mm_body(TN, nk, (tm, tn))
    extra = [] if after is None else [after]

    def body(x_ref, y_ref, *rest):
        inner(x_ref, y_ref, *rest[len(extra):])

    return _pcall(
        body, grid=(P // tm, C * npc, nk),
        in_specs=[pl.BlockSpec((tk, tm), lambda i, j, k: (k, i)),
                  pl.BlockSpec((tk, tn), lambda i, j, k: (k, j))] + [HBM] * len(extra),
        out_specs=pl.BlockSpec((None, tm, tn), lambda i, j, k: (j // npc, i, j % npc)),
        out_shape=jax.ShapeDtypeStruct((C, P, n), out_dtype), scratch_shapes=scratch,
        compiler_params=_params("parallel", "parallel", "arbitrary"), name=name)(x, y, *extra)


def _rms_scale(v):
    return lax.rsqrt(jnp.mean(v * v, axis=-1, keepdims=True) + RMS_EPS)


def _rms_bwd(gy, v, r):
    return r * gy - v * (r * r * r * jnp.mean(gy * v, axis=-1, keepdims=True))


def _rows_spec(tm, d):
    return pl.BlockSpec((tm, d), lambda i: (i, 0))


def _vec_spec(d):
    return pl.BlockSpec((1, d), lambda i: (0, 0))


def _rms_fwd(x, g, name, tm=256):
    S, D = x.shape

    def body(x_ref, g_ref, h_ref):
        v = x_ref[...]
        h_ref[...] = (v * _rms_scale(v) * g_ref[...]).astype(BF16)

    return _pcall(body, grid=(S // tm,), in_specs=[_rows_spec(tm, D), _vec_spec(D)], out_specs=_rows_spec(tm, D),
                  out_shape=jax.ShapeDtypeStruct((S, D), BF16), compiler_params=_params("parallel"), name=name)(x, g)


def _mid_fwd(x, mix, g_post, g_pre, name, tm=256):
    S, D = x.shape

    def body(x_ref, m_ref, gp_ref, gn_ref, x2_ref, h_ref):
        m = m_ref[...]
        x2 = x_ref[...] + m * _rms_scale(m) * gp_ref[...]
        x2_ref[...] = x2
        h_ref[...] = (x2 * _rms_scale(x2) * gn_ref[...]).astype(BF16)

    return _pcall(body, grid=(S // tm,), in_specs=[_rows_spec(tm, D), _rows_spec(tm, D), _vec_spec(D), _vec_spec(D)],
                  out_specs=[_rows_spec(tm, D), _rows_spec(tm, D)],
                  out_shape=[jax.ShapeDtypeStruct((S, D), F32), jax.ShapeDtypeStruct((S, D), BF16)],
                  compiler_params=_params("parallel"), name=name)(x, mix, g_post, g_pre)


def _loss_bwd(x2, f, tgt, g_post, name, tm=256):
    S, D = x2.shape

    def body(x2_ref, f_ref, t_ref, g_ref, dy_ref, df_ref, dg_ref, ls_ref):
        i = pl.program_id(0)

        @pl.when(i == 0)
        def _():
            dg_ref[...] = jnp.zeros_like(dg_ref)
            ls_ref[...] = jnp.zeros_like(ls_ref)

        fv = f_ref[...]
        r = _rms_scale(fv)
        g = g_ref[...]
        err = x2_ref[...] + fv * r * g - t_ref[...]
        ls_ref[...] += jnp.broadcast_to(0.5 * jnp.sum(jnp.mean(err * err, axis=-1, keepdims=True), axis=0, keepdims=True), ls_ref.shape)
        dy = err * (1.0 / D)
        dy_ref[...] = dy
        df_ref[...] = _rms_bwd(dy * g, fv, r).astype(BF16)
        dg_ref[...] += jnp.sum(dy * fv * r, axis=0, keepdims=True)

    return _pcall(body, grid=(S // tm,),
                  in_specs=[_rows_spec(tm, D), _rows_spec(tm, D), _rows_spec(tm, D), _vec_spec(D)],
                  out_specs=[_rows_spec(tm, D), _rows_spec(tm, D), _vec_spec(D), _vec_spec(LANES)],
                  out_shape=[jax.ShapeDtypeStruct((S, D), F32), jax.ShapeDtypeStruct((S, D), BF16),
                             jax.ShapeDtypeStruct((1, D), F32), jax.ShapeDtypeStruct((1, LANES), F32)],
                  compiler_params=_params("arbitrary"), name=name)(x2, f, tgt, g_post)


def _mid_bwd(dy, dh2, x2, mix, g_pre, g_post, name, tm=256):
    S, D = dy.shape

    def body(dy_ref, dh_ref, x2_ref, m_ref, gn_ref, gp_ref, dx2_ref, dm_ref, dgn_ref, dgp_ref):
        i = pl.program_id(0)

        @pl.when(i == 0)
        def _():
            dgn_ref[...] = jnp.zeros_like(dgn_ref)
            dgp_ref[...] = jnp.zeros_like(dgp_ref)

        x2, dh = x2_ref[...], dh_ref[...]
        r = _rms_scale(x2)
        dx2 = dy_ref[...] + _rms_bwd(dh * gn_ref[...], x2, r)
        dgn_ref[...] += jnp.sum(dh * x2 * r, axis=0, keepdims=True)
        dx2_ref[...] = dx2
        m = m_ref[...]
        rm = _rms_scale(m)
        dm_ref[...] = _rms_bwd(dx2 * gp_ref[...], m, rm).astype(BF16)
        dgp_ref[...] += jnp.sum(dx2 * m * rm, axis=0, keepdims=True)

    return _pcall(body, grid=(S // tm,),
                  in_specs=[_rows_spec(tm, D)] * 4 + [_vec_spec(D)] * 2,
                  out_specs=[_rows_spec(tm, D), _rows_spec(tm, D), _vec_spec(D), _vec_spec(D)],
                  out_shape=[jax.ShapeDtypeStruct((S, D), F32), jax.ShapeDtypeStruct((S, D), BF16),
                             jax.ShapeDtypeStruct((1, D), F32), jax.ShapeDtypeStruct((1, D), F32)],
                  compiler_params=_params("arbitrary"), name=name)(dy, dh2, x2, mix, g_pre, g_post)


def _first_bwd(dx2, dh1, x, g_pre, name, tm=256):
    S, D = x.shape

    def body(dx2_ref, dh_ref, x_ref, g_ref, gx_ref, dg_ref):
        i = pl.program_id(0)

        @pl.when(i == 0)
        def _():
            dg_ref[...] = jnp.zeros_like(dg_ref)

        xv, dh = x_ref[...], dh_ref[...]
        r = _rms_scale(xv)
        gx_ref[...] = dx2_ref[...] + _rms_bwd(dh * g_ref[...], xv, r)
        dg_ref[...] += jnp.sum(dh * xv * r, axis=0, keepdims=True)

    return _pcall(body, grid=(S // tm,), in_specs=[_rows_spec(tm, D)] * 3 + [_vec_spec(D)],
                  out_specs=[_rows_spec(tm, D), _vec_spec(D)],
                  out_shape=[jax.ShapeDtypeStruct((S, D), F32), jax.ShapeDtypeStruct((1, D), F32)],
                  compiler_params=_params("arbitrary"), name=name)(dx2, dh1, x, g_pre)


def _logsig_pair(z):
    lb = jnp.minimum(z, 0.0) - jnp.log(1.0 + jnp.exp(-jnp.abs(z)))
    return lb, lb - z


SB_KEY_BLOCK = 256


def _sum_matrix(strict):
    ia = lax.broadcasted_iota(jnp.int32, (SB_KEY_BLOCK, SB_KEY_BLOCK), 0)
    ib = lax.broadcasted_iota(jnp.int32, (SB_KEY_BLOCK, SB_KEY_BLOCK), 1)
    return ((ia > ib) if strict == ">" else (ia < ib)).astype(BF16)


def _row_total(sums, v, col):
    return jnp.broadcast_to(sums[:, col:col + 1] + v[:, col:col + 1], (v.shape[0], LANES))


def _lanes(c, width):
    return jnp.tile(c, (1, width // LANES))


def _split_dot(v, u):
    hi = v.astype(BF16)
    lo = (v - hi.astype(F32)).astype(BF16)
    return _dot(hi, u, NN) + _dot(lo, u, NN)


def _head_out(o, g):
    return o * _rms_scale(o) * g


def _sb_fwd(proj, gain, n_heads, name, tq=1024):
    S = proj.shape[0]
    H, tk = n_heads, SB_KEY_BLOCK
    tq = _tile(S, tq, 2 * tk)
    scale = HEAD_DIM ** -0.5

    def body(q_ref, k_ref, v_ref, g_ref, o_ref, ct_ref, mx_ref, oacc, cacc):
        i = pl.program_id(1)
        oacc[...] = jnp.zeros_like(oacc)
        cacc[...] = jnp.zeros_like(cacc)
        sums = _sum_matrix(">")

        def run(blocks):
            scored = []
            for k0, r0, diagonal in blocks:
                rows = pl.ds(r0, tq - r0)
                lb, lk = _logsig_pair(_dot(q_ref[rows, :].astype(BF16), k_ref[pl.ds(k0, tk), :].astype(BF16), NT) * scale)
                causal = None
                if diagonal:
                    causal = (lax.broadcasted_iota(jnp.int32, (tq - r0, tk), 1)
                              < lax.broadcasted_iota(jnp.int32, (tq - r0, tk), 0))
                    lk = jnp.where(causal, lk, 0.0)
                scored.append((k0, rows, causal, lb, lk))
            summed = [(k0, rows, causal, lb, lk, _split_dot(lk, sums)) for k0, rows, causal, lb, lk in scored]
            weights = []
            for k0, rows, causal, lb, lk, after in summed:
                c = cacc[rows, :]
                a = jnp.exp(lb + after + _lanes(c, tk))
                if causal is not None:
                    a = jnp.where(causal, a, 0.0)
                cacc[rows, :] = c + _row_total(after, lk, 0)
                weights.append((k0, rows, a.astype(BF16)))
            for k0, rows, a in weights:
                oacc[rows, :] += _dot(a, v_ref[pl.ds(k0, tk), :].astype(BF16), NN)

        for d in reversed(range(0, tq // tk, 2)):
            run([(pl.multiple_of(i * tq + e * tk, tk), e * tk, True) for e in (d + 1, d)])
        n_pairs = i * (tq // tk // 2)

        def step(it, carry):
            k0 = pl.multiple_of((n_pairs - 1 - it) * 2 * tk, 2 * tk)
            run([(pl.multiple_of(k0 + tk, tk), 0, False), (k0, 0, False)])
            return carry

        lax.fori_loop(0, n_pairs, step, 0)
        o = oacc[...]
        o_ref[...] = o
        ct_ref[...] = cacc[...]
        mx_ref[...] = _head_out(o, g_ref[...]).astype(BF16)

    blk = pl.BlockSpec((tq, HEAD_DIM), lambda h, i: (i, h))
    return _pcall(
        body, grid=(H, S // tq),
        in_specs=[blk, pl.BlockSpec((S, HEAD_DIM), lambda h, i: (0, H + h)),
                  pl.BlockSpec((S, HEAD_DIM), lambda h, i: (0, 2 * H + h)), pl.BlockSpec((1, HEAD_DIM), lambda h, i: (0, h))],
        out_specs=[blk, blk, blk],
        out_shape=[jax.ShapeDtypeStruct((S, H * HEAD_DIM), F32), jax.ShapeDtypeStruct((S, H * HEAD_DIM), F32),
                   jax.ShapeDtypeStruct((S, H * HEAD_DIM), BF16)],
        scratch_shapes=[pltpu.VMEM((tq, HEAD_DIM), F32), pltpu.VMEM((tq, LANES), F32)],
        compiler_params=_params("parallel", "arbitrary"), name=name)(proj, proj, proj, gain)


def _sb_bwd(proj, gain, o_raw, ctot, dmixed, dm_col0, n_heads, name, tq=1024):
    S = proj.shape[0]
    H, tk = n_heads, SB_KEY_BLOCK
    tq = _tile(S, tq, 2 * tk)
    nq = S // tq
    scale = HEAD_DIM ** -0.5

    def body(q_ref, k_ref, v_ref, g_ref, o_ref, ct_ref, dm_ref, dq_ref, dk_ref, dv_ref, dg_ref,
             dkacc, dvacc, dqacc, pfx, gcar, dos):
        i = pl.program_id(1)

        @pl.when(i == 0)
        def _():
            dkacc[...] = jnp.zeros_like(dkacc)
            dvacc[...] = jnp.zeros_like(dvacc)
            dg_ref[...] = jnp.zeros_like(dg_ref)

        o, dm, g = o_ref[...], dm_ref[...], g_ref[...]
        r = _rms_scale(o)
        dos[...] = _rms_bwd(dm * g, o, r).astype(BF16)
        dg_ref[...] += jnp.broadcast_to(jnp.sum(dm * o * r, axis=0, keepdims=True), dg_ref.shape)
        dqacc[...] = jnp.zeros_like(dqacc)
        pfx[...] = jnp.zeros_like(pfx)
        gcar[...] = jnp.zeros_like(gcar)
        later, earlier = _sum_matrix(">"), _sum_matrix("<")

        def run(blocks):
            scored = []
            for k0, r0, diagonal in blocks:
                rows, keys = pl.ds(r0, tq - r0), pl.ds(k0, tk)
                lb, lk = _logsig_pair(_dot(q_ref[rows, :].astype(BF16), k_ref[keys, :].astype(BF16), NT) * scale)
                da = _dot(dos[rows, :], v_ref[keys, :].astype(BF16), NT)
                causal = None
                if diagonal:
                    causal = (lax.broadcasted_iota(jnp.int32, (tq - r0, tk), 1)
                              < lax.broadcasted_iota(jnp.int32, (tq - r0, tk), 0))
                    lk = jnp.where(causal, lk, 0.0)
                scored.append((rows, keys, causal, lb, lk, da))
            summed = [(*blk, _split_dot(blk[4], later)) for blk in scored]
            weighted = []
            for rows, keys, causal, lb, lk, da, after in summed:
                p = pfx[rows, :] + _row_total(after, lk, 0)
                pfx[rows, :] = p
                a = jnp.exp(lb + after + _lanes(ct_ref[rows, :] - p, tk))
                if causal is not None:
                    a = jnp.where(causal, a, 0.0)
                dl = da * a
                weighted.append((rows, keys, causal, lb, a.astype(BF16), dl, _dot(dl.astype(BF16), earlier, NN)))
            cotangents = []
            for rows, keys, causal, lb, a, dl, before in weighted:
                gc = gcar[rows, :]
                gcar[rows, :] = gc + _row_total(before, dl, tk - 1)
                sig = jnp.exp(lb)
                gsum = (before + _lanes(gc, tk)) * sig
                if causal is not None:
                    gsum = jnp.where(causal, gsum, 0.0)
                cotangents.append((rows, keys, a, ((dl * (1.0 - sig) - gsum) * scale).astype(BF16)))
            for rows, keys, a, dz in cotangents:
                q, do = q_ref[rows, :].astype(BF16), dos[rows, :]
                dvacc[keys, :] += _dot(a, do, TN)
                dqacc[rows, :] += _dot(dz, k_ref[keys, :].astype(BF16), NN)
                dkacc[keys, :] += _dot(dz, q, TN)

        def step(j, carry):
            k0 = pl.multiple_of(j * 2 * tk, 2 * tk)
            run([(k0, 0, False), (pl.multiple_of(k0 + tk, tk), 0, False)])
            return carry

        lax.fori_loop(0, i * (tq // tk // 2), step, 0)
        for d in range(0, tq // tk, 2):
            run([(pl.multiple_of(i * tq + e * tk, tk), e * tk, True) for e in (d, d + 1)])
        dq_ref[...] = dqacc[...].astype(BF16)

        @pl.when(i == nq - 1)
        def _():
            dk_ref[...] = dkacc[...].astype(BF16)
            dv_ref[...] = dvacc[...].astype(BF16)

    blk = pl.BlockSpec((tq, HEAD_DIM), lambda h, i: (i, h))
    full = pl.BlockSpec((S, HEAD_DIM), lambda h, i: (0, h))
    W = H * HEAD_DIM
    return _pcall(
        body, grid=(H, nq),
        in_specs=[blk, pl.BlockSpec((S, HEAD_DIM), lambda h, i: (0, H + h)),
                  pl.BlockSpec((S, HEAD_DIM), lambda h, i: (0, 2 * H + h)), pl.BlockSpec((1, HEAD_DIM), lambda h, i: (0, h)),
                  blk, blk, pl.BlockSpec((tq, HEAD_DIM), lambda h, i: (i, dm_col0 + h))],
        out_specs=[blk, full, full, pl.BlockSpec((8, HEAD_DIM), lambda h, i: (0, h))],
        out_shape=[jax.ShapeDtypeStruct((S, W), BF16), jax.ShapeDtypeStruct((S, W), BF16),
                   jax.ShapeDtypeStruct((S, W), BF16), jax.ShapeDtypeStruct((8, W), F32)],
        scratch_shapes=[pltpu.VMEM((S, HEAD_DIM), F32), pltpu.VMEM((S, HEAD_DIM), F32), pltpu.VMEM((tq, HEAD_DIM), F32),
                        pltpu.VMEM((tq, LANES), F32), pltpu.VMEM((tq, LANES), F32), pltpu.VMEM((tq, HEAD_DIM), BF16)],
        compiler_params=_params("arbitrary", "arbitrary"), name=name)(proj, proj, proj, gain, o_raw, ctot, dmixed)


def _rope_tables(S):
    inv_freq = ROPE_THETA ** (-jnp.arange(0, HEAD_DIM, 2, dtype=F32) / HEAD_DIM)
    ang = jnp.arange(S, dtype=F32)[:, None] * inv_freq[None, :]
    cos, sin = jnp.cos(ang), jnp.sin(ang)
    return jnp.concatenate([cos, cos], axis=1), jnp.concatenate([-sin, sin], axis=1)


def _rope(v, cos2, sin_signed):
    return v * cos2 + pltpu.roll(v, HEAD_DIM // 2, axis=1) * sin_signed


def _dil_rows(d, r, l0, n):
    if d == 1:
        return pl.ds(l0 if isinstance(l0, int) else pl.multiple_of(l0, KEY_BLOCK), n)
    return pl.ds(r + d * l0, n, stride=d)


def _dil_blocks(S, visit):
    B = KEY_BLOCK
    group = 8
    for b, d in enumerate(DILATIONS):
        nb = S // d // B
        if nb == 1:
            g = math.gcd(d, group)

            def trip(t, carry, b=b, d=d, g=g):
                visit([(b, d, t * g + u, 0, True) for u in range(g)])
                return carry

            lax.fori_loop(0, d // g, trip, 0)
        elif d == 1:
            visit([(b, d, 0, 0, True)])
            g = max(k for k in range(1, group + 2) if (nb - 1) % k == 0)

            def trip(t, carry, b=b, d=d, g=g):
                visit([(b, d, 0, (1 + t * g + u) * B, False) for u in range(g)])
                return carry

            lax.fori_loop(0, (nb - 1) // g, trip, 0)
        else:
            g = math.gcd(d, max(group // nb, 1))

            def trip(t, carry, b=b, d=d, nb=nb, g=g):
                visit([(b, d, t * g + u, n * B, n == 0) for u in range(g) for n in range(nb)])
                return carry

            lax.fori_loop(0, d // g, trip, 0)


def _dil_mask(first):
    B = KEY_BLOCK
    nk = B if first else 2 * B
    iq = lax.broadcasted_iota(jnp.int32, (B, nk), 0)
    ik = lax.broadcasted_iota(jnp.int32, (B, nk), 1)
    return (ik <= iq) if first else ((ik >= iq) & (ik <= iq + B))


def _dil_fwd(proj, cos2, sin_signed, gain, col0, n_heads, name):
    S = proj.shape[0]
    H, B = n_heads, KEY_BLOCK
    scale = HEAD_DIM ** -0.5
    rc = _tile(S, 256, 8)

    def body(q_ref, k_ref, v_ref, c_ref, s_ref, g_ref, o_ref, l_ref, mx_ref, qr, kr, *per_branch):
        ob, lb = per_branch[:len(DILATIONS)], per_branch[len(DILATIONS):]

        def rope_rows(t, carry):
            rows = pl.ds(pl.multiple_of(t * rc, rc), rc)
            qr[rows, :] = _rope(q_ref[rows, :], c_ref[rows, :], s_ref[rows, :])
            kr[rows, :] = _rope(k_ref[rows, :], c_ref[rows, :], s_ref[rows, :])
            return carry

        lax.fori_loop(0, S // rc, rope_rows, 0)

        def visit(blocks):
            scores = []
            for b, d, r, l0, first in blocks:
                qrows = _dil_rows(d, r, l0, B)
                krows = qrows if first else _dil_rows(d, r, l0 - B, 2 * B)
                s = _dot(qr[qrows, :].astype(BF16), kr[krows, :].astype(BF16), NT) * scale
                scores.append((b, qrows, krows, jnp.where(_dil_mask(first), s, NEG)))
            weights = []
            for b, qrows, krows, s in scores:
                m = jnp.max(s, axis=1, keepdims=True)
                p = jnp.exp(s - m)
                den = jnp.sum(p, axis=1, keepdims=True)
                lb[b][qrows, :] = jnp.broadcast_to(m + jnp.log(den), (B, LANES))
                weights.append((b, qrows, krows, p.astype(BF16), den))
            for b, qrows, krows, p, den in weights:
                ob[b][qrows, :] = _dot(p, v_ref[krows, :].astype(BF16), NN) / den

        _dil_blocks(S, visit)

        def combine(t, carry):
            rows = pl.ds(pl.multiple_of(t * rc, rc), rc)
            l0, l1, l2 = lb[0][rows, :], lb[1][rows, :], lb[2][rows, :]
            m = jnp.maximum(jnp.maximum(l0, l1), l2)
            w0, w1, w2 = jnp.exp(l0 - m), jnp.exp(l1 - m), jnp.exp(l2 - m)
            den = w0 + w1 + w2
            o = (w0 * ob[0][rows, :] + w1 * ob[1][rows, :] + w2 * ob[2][rows, :]) / den
            o_ref[rows, :] = o
            l_ref[rows, :] = m + jnp.log(den)
            mx_ref[rows, :] = _head_out(o, g_ref[...]).astype(BF16)
            return carry

        lax.fori_loop(0, S // rc, combine, 0)

    def col(k):
        return pl.BlockSpec((S, HEAD_DIM), lambda h: (0, col0 + k * H + h))

    tab = pl.BlockSpec((S, HEAD_DIM), lambda h: (0, 0))
    out = pl.BlockSpec((S, HEAD_DIM), lambda h: (0, h))
    W = H * HEAD_DIM
    return _pcall(
        body, grid=(H,),
        in_specs=[col(0), col(1), col(2), tab, tab, pl.BlockSpec((1, HEAD_DIM), lambda h: (0, h))],
        out_specs=[out, out, out],
        out_shape=[jax.ShapeDtypeStruct((S, W), F32), jax.ShapeDtypeStruct((S, W), F32), jax.ShapeDtypeStruct((S, W), BF16)],
        scratch_shapes=[pltpu.VMEM((S, HEAD_DIM), F32)] * (2 + 2 * len(DILATIONS)),
        compiler_params=_params("parallel"), name=name)(proj, proj, proj, cos2, sin_signed, gain)


def _dil_bwd(proj, cos2, sin_signed, gain, o_raw, lse, dmixed, dm_col0, col0, n_heads, name):
    S = proj.shape[0]
    H, B = n_heads, KEY_BLOCK
    scale = HEAD_DIM ** -0.5
    rc = _tile(S, 256, 8)

    def body(q_ref, k_ref, v_ref, c_ref, s_ref, g_ref, o_ref, l_ref, dm_ref, dq_ref, dk_ref, dv_ref, dg_ref,
             qr, kr, dos, dsum, dqr, dkr, dvv):
        dg_ref[...] = jnp.zeros_like(dg_ref)

        def prep(t, carry):
            rows = pl.ds(pl.multiple_of(t * rc, rc), rc)
            qr[rows, :] = _rope(q_ref[rows, :], c_ref[rows, :], s_ref[rows, :])
            kr[rows, :] = _rope(k_ref[rows, :], c_ref[rows, :], s_ref[rows, :])
            o, dm = o_ref[rows, :], dm_ref[rows, :]
            r = _rms_scale(o)
            do = _rms_bwd(dm * g_ref[...], o, r)
            dg_ref[...] += jnp.broadcast_to(jnp.sum(dm * o * r, axis=0, keepdims=True), dg_ref.shape)
            dos[rows, :] = do
            dsum[rows, :] = jnp.broadcast_to(jnp.sum(do * o, axis=1, keepdims=True), (rc, LANES))
            dqr[rows, :] = jnp.zeros((rc, HEAD_DIM), F32)
            dkr[rows, :] = jnp.zeros((rc, HEAD_DIM), F32)
            dvv[rows, :] = jnp.zeros((rc, HEAD_DIM), F32)
            return carry

        lax.fori_loop(0, S // rc, prep, 0)

        def visit(blocks):
            products = []
            for b, d, r, l0, first in blocks:
                qrows = _dil_rows(d, r, l0, B)
                krows = qrows if first else _dil_rows(d, r, l0 - B, 2 * B)
                qs, ks = qr[qrows, :].astype(BF16), kr[krows, :].astype(BF16)
                do = dos[qrows, :].astype(BF16)
                s = jnp.where(_dil_mask(first), _dot(qs, ks, NT) * scale, NEG)
                dp = _dot(do, v_ref[krows, :].astype(BF16), NT)
                products.append((qrows, krows, qs, ks, do, s, dp))
            cotangents = []
            for qrows, krows, qs, ks, do, s, dp in products:
                p = jnp.exp(s - l_ref[qrows, :][:, 0:1])
                ds = (p * (dp - dsum[qrows, :][:, 0:1]) * scale).astype(BF16)
                cotangents.append((qrows, krows, qs, ks, do, p.astype(BF16), ds))
            for qrows, krows, qs, ks, do, p, ds in cotangents:
                dqr[qrows, :] += _dot(ds, ks, NN)
                dkr[krows, :] += _dot(ds, qs, TN)
                dvv[krows, :] += _dot(p, do, TN)

        _dil_blocks(S, visit)

        def finish(t, carry):
            rows = pl.ds(pl.multiple_of(t * rc, rc), rc)
            c, s = c_ref[rows, :], s_ref[rows, :]
            dq, dk = dqr[rows, :], dkr[rows, :]
            dq_ref[rows, :] = (dq * c + pltpu.roll(dq * s, HEAD_DIM // 2, axis=1)).astype(BF16)
            dk_ref[rows, :] = (dk * c + pltpu.roll(dk * s, HEAD_DIM // 2, axis=1)).astype(BF16)
            dv_ref[rows, :] = dvv[rows, :].astype(BF16)
            return carry

        lax.fori_loop(0, S // rc, finish, 0)

    def col(k):
        return pl.BlockSpec((S, HEAD_DIM), lambda h: (0, col0 + k * H + h))

    tab = pl.BlockSpec((S, HEAD_DIM), lambda h: (0, 0))
    out = pl.BlockSpec((S, HEAD_DIM), lambda h: (0, h))
    W = H * HEAD_DIM
    big = pltpu.VMEM((S, HEAD_DIM), F32)
    return _pcall(
        body, grid=(H,),
        in_specs=[col(0), col(1), col(2), tab, tab, pl.BlockSpec((1, HEAD_DIM), lambda h: (0, h)), out, out,
                  pl.BlockSpec((S, HEAD_DIM), lambda h: (0, dm_col0 + h))],
        out_specs=[out, out, out, pl.BlockSpec((8, HEAD_DIM), lambda h: (0, h))],
        out_shape=[jax.ShapeDtypeStruct((S, W), BF16), jax.ShapeDtypeStruct((S, W), BF16),
                   jax.ShapeDtypeStruct((S, W), BF16), jax.ShapeDtypeStruct((8, W), F32)],
        scratch_shapes=[big, big, big, pltpu.VMEM((S, LANES), F32), big, big, big],
        compiler_params=_params("parallel"), name=name)(proj, proj, proj, cos2, sin_signed, gain, o_raw, lse, dmixed)


GELU_C = math.sqrt(2.0 / math.pi)
GELU_A = 0.044715
HALO = 16


def _shift_down(cur, halo, k):
    out = pltpu.roll(cur, k, axis=0)
    row = lax.broadcasted_iota(jnp.int32, cur.shape, 0)
    for t in range(k):
        out = jnp.where(row == t, halo[HALO - k + t:HALO - k + t + 1, :], out)
    return out


def _shift_up(cur, halo, k):
    n = cur.shape[0]
    out = pltpu.roll(cur, n - k, axis=0)
    row = lax.broadcasted_iota(jnp.int32, cur.shape, 0)
    for t in range(k):
        out = jnp.where(row == n - k + t, halo[t:t + 1, :], out)
    return out


def _conv3(cur, halo, cw):
    return _shift_down(cur, halo, 2) * cw[0:1, :] + _shift_down(cur, halo, 1) * cw[1:2, :] + cur * cw[2:3, :] + cw[3:4, :]


def _gelu_parts(x):
    t = jnp.tanh(GELU_C * (x + GELU_A * x * x * x))
    return 0.5 * x * (1.0 + t), t


def _geglu_specs(tm, tn, ncb):
    hb = tm // HALO

    def cur(off):
        return pl.BlockSpec((tm, tn), lambda j, i: (i, off + j))

    def prev(off):
        return pl.BlockSpec((HALO, tn), lambda j, i: (jnp.maximum(i * hb - 1, 0), off + j))

    def taps(off):
        return pl.BlockSpec((8, tn), lambda j, i: (0, off + j))

    return [cur(0), prev(0), cur(ncb), prev(ncb), taps(0), taps(ncb)]


def _geglu_fwd(u, cwb, name, tm=256, tn=1408):
    S, F2 = u.shape
    F = F2 // 2
    tm, tn = _tile(S, tm, HALO), _tile(F, tn)
    ncb = F // tn

    def body(g_ref, gp_ref, v_ref, vp_ref, cg_ref, cv_ref, y_ref):
        top = pl.program_id(1) > 0
        gp = jnp.where(top, gp_ref[...].astype(F32), 0.0)
        vp = jnp.where(top, vp_ref[...].astype(F32), 0.0)
        gc = _conv3(g_ref[...].astype(F32), gp, cg_ref[...])
        vc = _conv3(v_ref[...].astype(F32), vp, cv_ref[...])
        y_ref[...] = (_gelu_parts(gc)[0] * vc).astype(BF16)

    return _pcall(body, grid=(ncb, S // tm), in_specs=_geglu_specs(tm, tn, ncb),
                  out_specs=pl.BlockSpec((tm, tn), lambda j, i: (i, j)),
                  out_shape=jax.ShapeDtypeStruct((S, F), BF16),
                  compiler_params=_params("parallel", "parallel"), name=name)(u, u, u, u, cwb, cwb)


def _geglu_bwd(u, dy, cwb, name, tm=256, tn=512):
    S, F2 = u.shape
    F = F2 // 2
    tm, tn = _tile(S, tm, HALO), _tile(F, tn)
    ncb = F // tn

    def body(g_ref, gp_ref, v_ref, vp_ref, cg_ref, cv_ref, dy_ref, dc_ref, dwg_ref, dwv_ref):
        i = pl.program_id(1)

        @pl.when(i == 0)
        def _():
            dwg_ref[...] = jnp.zeros_like(dwg_ref)
            dwv_ref[...] = jnp.zeros_like(dwv_ref)

        top = i > 0
        g, v = g_ref[...].astype(F32), v_ref[...].astype(F32)
        gp = jnp.where(top, gp_ref[...].astype(F32), 0.0)
        vp = jnp.where(top, vp_ref[...].astype(F32), 0.0)
        gc = _conv3(g, gp, cg_ref[...])
        vc = _conv3(v, vp, cv_ref[...])
        act, t = _gelu_parts(gc)
        dact = 0.5 * (1.0 + t) + 0.5 * gc * (1.0 - t * t) * GELU_C * (1.0 + 3.0 * GELU_A * gc * gc)
        dyv = dy_ref[...].astype(F32)
        dgc = dyv * vc * dact
        dvc = dyv * act
        dc_ref[0] = dgc.astype(BF16)
        dc_ref[1] = dvc.astype(BF16)

        def taps(out_ref, dc, cur, halo):
            out_ref[0:1, :] += jnp.sum(dc * _shift_down(cur, halo, 2), axis=0, keepdims=True)
            out_ref[1:2, :] += jnp.sum(dc * _shift_down(cur, halo, 1), axis=0, keepdims=True)
            out_ref[2:3, :] += jnp.sum(dc * cur, axis=0, keepdims=True)
            out_ref[3:4, :] += jnp.sum(dc, axis=0, keepdims=True)

        taps(dwg_ref, dgc, g, gp)
        taps(dwv_ref, dvc, v, vp)

    return _pcall(body, grid=(ncb, S // tm),
                  in_specs=_geglu_specs(tm, tn, ncb) + [pl.BlockSpec((tm, tn), lambda j, i: (i, j))],
                  out_specs=[pl.BlockSpec((2, tm, tn), lambda j, i: (0, i, j)),
                             pl.BlockSpec((8, tn), lambda j, i: (0, j)), pl.BlockSpec((8, tn), lambda j, i: (0, j))],
                  out_shape=[jax.ShapeDtypeStruct((2, S, F), BF16), jax.ShapeDtypeStruct((8, F), F32),
                             jax.ShapeDtypeStruct((8, F), F32)],
                  compiler_params=_params("parallel", "arbitrary"), name=name)(u, u, u, u, cwb, cwb, dy)


def _conv_bwd(dc, cwb, name, tm=512, tn=1408):
    _, S, F = dc.shape
    tm, tn = _tile(S, tm, HALO), _tile(F, tn)
    ncb, nrb = F // tn, S // tm
    hb = tm // HALO

    def body(c_ref, n_ref, w_ref, du_ref):
        cur = c_ref[...].astype(F32)
        nxt = jnp.where(pl.program_id(2) < nrb - 1, n_ref[...].astype(F32), 0.0)
        w = w_ref[...]
        du = cur * w[2:3, :] + _shift_up(cur, nxt, 1) * w[1:2, :] + _shift_up(cur, nxt, 2) * w[0:1, :]
        du_ref[...] = du.astype(BF16)

    return _pcall(body, grid=(2, ncb, nrb),
                  in_specs=[pl.BlockSpec((None, tm, tn), lambda c, j, i: (c, i, j)),
                            pl.BlockSpec((None, HALO, tn), lambda c, j, i: (c, jnp.minimum((i + 1) * hb, S // HALO - 1), j)),
                            pl.BlockSpec((8, tn), lambda c, j, i: (0, c * ncb + j))],
                  out_specs=pl.BlockSpec((tm, tn), lambda c, j, i: (i, c * ncb + j)),
                  out_shape=jax.ShapeDtypeStruct((S, 2 * F), BF16),
                  compiler_params=_params("parallel", "parallel", "parallel"), name=name)(dc, dc, cwb)


def _adam_math(w, g, m, v):
    m = ADAM_B1 * m + (1.0 - ADAM_B1) * g
    v = ADAM_B2 * v + (1.0 - ADAM_B2) * (g * g)
    m_hat = m / (1.0 - ADAM_B1 ** ADAM_STEP)
    v_hat = v / (1.0 - ADAM_B2 ** ADAM_STEP)
    return -ADAM_LR * (m_hat / (jnp.sqrt(v_hat) + ADAM_EPS) + ADAM_WD * w), m, v


def _adamw(w, parts, m, v, name, tr=256):
    R, C = w.shape
    n, _, Cp = parts.shape
    tr = _tile(R, tr, 8)

    def body(w_ref, p_ref, m_ref, v_ref, g_out, d_out, m_out, v_out):
        g = p_ref[0, :, 0:C].astype(F32)
        for k in range(1, n):
            g = g + p_ref[k, :, 0:C].astype(F32)
        d, mn, vn = _adam_math(w_ref[...], g, m_ref[...], v_ref[...])
        g_out[...] = g
        d_out[...] = d
        m_out[...] = mn
        v_out[...] = vn

    spec = pl.BlockSpec((tr, C), lambda i: (i, 0))
    shape = jax.ShapeDtypeStruct((R, C), F32)
    return _pcall(body, grid=(R // tr,), in_specs=[spec, pl.BlockSpec((n, tr, Cp), lambda i: (0, i, 0)), spec, spec],
                  out_specs=[spec] * 4, out_shape=[shape] * 4, compiler_params=_params("parallel"), name=name)(w, parts, m, v)


def _adamw_chips(w, pair, parts, chip_ids, m, v, name, tr=256):
    R, C = w.shape
    Cp = pair.shape[2]
    by_columns = C == Cp and _tile(R, tr, 16) < 64
    tr, tc = (R, _tile(C, 256)) if by_columns else (_tile(R, tr, 16), C)

    def body(ids_ref, w_ref, own_ref, p1_ref, p2_ref, p3_ref, m_ref, v_ref, g_out, d_out, m_out, v_out):
        g = own_ref[:, 0:tc].astype(F32)
        for ref in (p1_ref, p2_ref, p3_ref):
            g = g + ref[:, 0:tc].astype(F32)
        d, mn, vn = _adam_math(w_ref[...], g, m_ref[...], v_ref[...])
        g_out[...] = g
        d_out[...] = d
        m_out[...] = mn
        v_out[...] = vn

    if by_columns:
        spec = pl.BlockSpec((tr, tc), lambda j, ids: (0, j))
    else:
        spec = pl.BlockSpec((tr, tc), lambda i, ids: (i, 0))

    def chip(k):
        if by_columns:
            return pl.BlockSpec((None, tr, tc), lambda j, ids: (ids[k], 0, j))
        return pl.BlockSpec((None, tr, Cp), lambda i, ids: (ids[k], i, 0))

    shape = jax.ShapeDtypeStruct((R, C), F32)
    grid_spec = pltpu.PrefetchScalarGridSpec(
        num_scalar_prefetch=1, grid=(C // tc if by_columns else R // tr,),
        in_specs=[spec, chip(0), chip(1), chip(2), chip(3), spec, spec], out_specs=[spec] * 4)
    return _pcall(body, grid_spec=grid_spec, out_shape=[shape] * 4, compiler_params=_params("parallel"),
                  name=name)(chip_ids, w, pair, parts, parts, parts, m, v)


def _place():
    return lax.axis_index("x"), lax.axis_index("y"), lax.axis_index("c")


def _other_chips(x, y):
    return [(1 - x, y), (x, 1 - y), (1 - x, 1 - y)]


IN_HBM = pl.BlockSpec(memory_space=pltpu.HBM)
SEM = pl.BlockSpec(memory_space=pltpu.SEMAPHORE)
EFFECT = pltpu.SideEffectType.DATAFLOW_SIDE_EFFECTING
TOKEN = jax.ShapeDtypeStruct((8, LANES), F32)
TOKEN_SPEC = pl.BlockSpec(memory_space=pltpu.VMEM)


def _in_hbm(a):
    return pltpu.with_memory_space_constraint(a, pltpu.HBM)


def _landing(shape):
    return _in_hbm(lax.empty(shape.shape, shape.dtype))


def _hbm_like(a):
    return pltpu.HBM(a.shape, a.dtype)


def _gather_start(landing, slots, after, name):
    na = len(landing)

    def body(*refs):
        land = refs[:na]
        send_sems, recv_sems = refs[na + 1], refs[na + 2]
        token = refs[-1]
        x, y, c = _place()
        for a in range(na):
            own = slots[a](land[a], x, y, c)
            for k, to in enumerate([(x, y, 1 - c)] + [(*chip, c) for chip in _other_chips(x, y)]):
                pltpu.make_async_remote_copy(
                    src_ref=own, dst_ref=own, send_sem=send_sems.at[4 * a + k],
                    recv_sem=recv_sems.at[4 * a + k], device_id=to, device_id_type=MESH).start()
        token[...] = jnp.zeros_like(token)

    sems = pltpu.SemaphoreType.DMA((4 * na,))
    outs = _pcall(
        body, in_specs=[IN_HBM] * na + [HBM],
        out_specs=[SEM, SEM] + [IN_HBM] * na + [TOKEN_SPEC],
        out_shape=[sems, sems] + [_hbm_like(s) for s in landing] + [TOKEN],
        input_output_aliases={a: 2 + a for a in range(na)},
        compiler_params=pltpu.CompilerParams(has_side_effects=EFFECT), name=name,
    )(*[_in_hbm(s) for s in landing], after)
    return outs[0], outs[1], outs[2:2 + na], outs[-1]


def _gather_forward(gathered, send_sems, recv_sems, slots, after, name):
    na = len(gathered)

    def body(*refs):
        gath = refs[:na]
        send1, recv1 = refs[na], refs[na + 1]
        fsend, frecv = refs[na + 3], refs[na + 4]
        token = refs[-1]
        x, y, c = _place()
        chips = _other_chips(x, y)
        for a in range(na):
            for k, peer in enumerate([(x, y, 1 - c)] + [(*chip, c) for chip in chips]):
                arrival = pltpu.make_async_remote_copy(
                    src_ref=slots[a](gath[a], x, y, c), dst_ref=slots[a](gath[a], *peer), send_sem=send1.at[4 * a + k],
                    recv_sem=recv1.at[4 * a + k], device_id=peer, device_id_type=MESH)
                arrival.wait_send()
                arrival.wait_recv()
        for a in range(na):
            for j, chip in enumerate(chips):
                view = slots[a](gath[a], *chip, c)
                pltpu.make_async_remote_copy(
                    src_ref=view, dst_ref=view, send_sem=fsend.at[3 * a + j], recv_sem=frecv.at[3 * a + j],
                    device_id=(x, y, 1 - c), device_id_type=MESH).start()
        token[...] = jnp.zeros_like(token)

    sems = pltpu.SemaphoreType.DMA((3 * na,))
    outs = _pcall(
        body, in_specs=[IN_HBM] * na + [SEM, SEM, HBM],
        out_specs=[SEM, SEM] + [IN_HBM] * na + [TOKEN_SPEC],
        out_shape=[sems, sems] + [_hbm_like(g) for g in gathered] + [TOKEN],
        input_output_aliases={a: 2 + a for a in range(na)},
        compiler_params=pltpu.CompilerParams(has_side_effects=EFFECT), name=name,
    )(*gathered, send_sems, recv_sems, after)
    return outs[0], outs[1], outs[2:2 + na], outs[-1]


def _gather_finish(gathered, fsend, frecv, slots, after, name):
    na = len(gathered)

    def body(*refs):
        gath, fs, fr = refs[:na], refs[na], refs[na + 1]
        x, y, c = _place()
        for a in range(na):
            for j, chip in enumerate(_other_chips(x, y)):
                passed = pltpu.make_async_remote_copy(
                    src_ref=slots[a](gath[a], *chip, c), dst_ref=slots[a](gath[a], *chip, 1 - c),
                    send_sem=fs.at[3 * a + j], recv_sem=fr.at[3 * a + j], device_id=(x, y, 1 - c), device_id_type=MESH)
                passed.wait_send()
                passed.wait_recv()

    outs = _pcall(
        body, in_specs=[IN_HBM] * na + [SEM, SEM, HBM], out_specs=[IN_HBM] * na,
        out_shape=[_hbm_like(g) for g in gathered], input_output_aliases={a: a for a in range(na)},
        compiler_params=pltpu.CompilerParams(has_side_effects=EFFECT), name=name,
    )(*gathered, fsend, frecv, after)
    return list(outs)


def _pair_copy(view, src, land, send_sems, recv_sems, chip):
    x, y, c = _place()
    return pltpu.make_async_remote_copy(
        src_ref=view(src, chip, 1 - c), dst_ref=land.at[chip], send_sem=send_sems.at[chip], recv_sem=recv_sems.at[chip],
        device_id=(x, y, 1 - c), device_id_type=MESH)


def _pair_start(grad, view, block, after, name):
    def body(src, land, after_ref, send_sems, recv_sems, src_thru, land_thru, token):
        for chip in range(N_CHIP):
            _pair_copy(view, src, land, send_sems, recv_sems, chip).start()
        token[...] = jnp.zeros_like(token)

    sems = pltpu.SemaphoreType.DMA((N_CHIP,))
    land = jax.ShapeDtypeStruct((N_CHIP, *block), BF16)
    return _pcall(
        body, in_specs=[IN_HBM, IN_HBM, HBM], out_specs=[SEM, SEM, IN_HBM, IN_HBM, TOKEN_SPEC],
        out_shape=[sems, sems, _hbm_like(grad), _hbm_like(land), TOKEN], input_output_aliases={0: 2, 1: 3},
        compiler_params=pltpu.CompilerParams(has_side_effects=EFFECT), name=name,
    )(_in_hbm(grad), _landing(land), after)


def _pair_wait(grad, recv, send_sems, recv_sems, view, after, name):
    def body(src, land, send, recv_s, after_ref, src_thru, land_thru):
        for chip in range(N_CHIP):
            copy = _pair_copy(view, src, land, send, recv_s, chip)
            copy.wait_send()
            copy.wait_recv()

    return _pcall(
        body, in_specs=[IN_HBM, IN_HBM, SEM, SEM, HBM], out_specs=[IN_HBM, IN_HBM],
        out_shape=[_hbm_like(grad), _hbm_like(recv)], input_output_aliases={0: 0, 1: 1},
        compiler_params=pltpu.CompilerParams(has_side_effects=EFFECT), name=name,
    )(grad, recv, send_sems, recv_sems, after)


def _chip_start(pair, after, name):
    def body(src, land, after_ref, send_sems, recv_sems, src_thru, land_thru, token):
        x, y, c = _place()
        for j, (px, py) in enumerate(_other_chips(x, y)):
            pltpu.make_async_remote_copy(
                src_ref=src.at[2 * px + py], dst_ref=land.at[2 * x + y], send_sem=send_sems.at[j], recv_sem=recv_sems.at[j],
                device_id=(px, py, c), device_id_type=MESH).start()
        token[...] = jnp.zeros_like(token)

    sems = pltpu.SemaphoreType.DMA((3,))
    return _pcall(
        body, in_specs=[IN_HBM, IN_HBM, HBM], out_specs=[SEM, SEM, IN_HBM, IN_HBM, TOKEN_SPEC],
        out_shape=[sems, sems, _hbm_like(pair), _hbm_like(pair), TOKEN], input_output_aliases={0: 2, 1: 3},
        compiler_params=pltpu.CompilerParams(has_side_effects=EFFECT), name=name,
    )(_in_hbm(pair), _landing(pair), after)


def _chip_wait(pair, parts, send_sems, recv_sems, after, name):
    def body(src, land, send, recv, after_ref, src_thru, land_thru):
        x, y, c = _place()
        for j, (px, py) in enumerate(_other_chips(x, y)):
            copy = pltpu.make_async_remote_copy(
                src_ref=src.at[2 * px + py], dst_ref=land.at[2 * px + py], send_sem=send.at[j], recv_sem=recv.at[j],
                device_id=(px, py, c), device_id_type=MESH)
            copy.wait_send()
            copy.wait_recv()

    return _pcall(
        body, in_specs=[IN_HBM, IN_HBM, SEM, SEM, HBM], out_specs=[IN_HBM, IN_HBM],
        out_shape=[_hbm_like(pair), _hbm_like(parts)], input_output_aliases={0: 0, 1: 1},
        compiler_params=pltpu.CompilerParams(has_side_effects=EFFECT), name=name,
    )(pair, parts, send_sems, recv_sems, after)


def _pair_add(core, grad, recv, block, grad_spec, name):
    _, R, C = recv.shape
    tr = block

    def body(c_ref, g_ref, r_ref, o_ref):
        o_ref[...] = (g_ref[...].astype(F32) + r_ref[...].astype(F32)).astype(BF16)

    grid_spec = pltpu.PrefetchScalarGridSpec(
        num_scalar_prefetch=1, grid=(N_CHIP, R // tr),
        in_specs=[grad_spec, pl.BlockSpec((None, tr, C), lambda k, i, c: (k, i, 0))],
        out_specs=pl.BlockSpec((None, tr, C), lambda k, i, c: (k, i, 0)))
    return _pcall(body, grid_spec=grid_spec, out_shape=jax.ShapeDtypeStruct(recv.shape, BF16),
                  compiler_params=_params("parallel", "parallel"), name=name)(core, grad, recv)


def _small_step(parts, params, name):
    na, npar = len(parts), len(params)

    def body(*refs):
        p_refs, wmv = refs[:na], refs[na:na + 3 * npar]
        o_parts = refs[na + 3 * npar:2 * na + 3 * npar]
        o_params = refs[2 * na + 3 * npar:2 * na + 7 * npar]
        alls, (send_sems, recv_sems) = refs[2 * na + 7 * npar:3 * na + 7 * npar], refs[3 * na + 7 * npar:]
        x, y, c = _place()
        me = 4 * x + 2 * y + c
        peers = [(x, y, 1 - c)] + [(px, py, pc) for px, py in _other_chips(x, y) for pc in (c, 1 - c)]
        copies = []
        for a in range(na):
            alls[a][me] = p_refs[a][...]
            copies += [pltpu.make_async_remote_copy(
                src_ref=p_refs[a], dst_ref=alls[a].at[me], send_sem=send_sems.at[7 * a + k], recv_sem=recv_sems.at[7 * a + k],
                device_id=peer, device_id_type=MESH) for k, peer in enumerate(peers)]
        for cp in copies:
            cp.start()
        for a in range(na):
            for k, (px, py, pc) in enumerate(peers):
                pltpu.make_async_remote_copy(
                    src_ref=p_refs[a], dst_ref=alls[a].at[4 * px + 2 * py + pc], send_sem=send_sems.at[7 * a + k],
                    recv_sem=recv_sems.at[7 * a + k], device_id=peers[k], device_id_type=MESH).wait_recv()
        for cp in copies:
            cp.wait_send()
        sums = []
        for a in range(na):
            acc = alls[a][0]
            for k in range(1, N_DEV):
                acc = acc + alls[a][k]
            o_parts[a][...] = acc
            sums.append(acc)
        for j, (a, row, _, _, _) in enumerate(params):
            g = sums[a][row:row + 1, :]
            d, mn, vn = _adam_math(wmv[3 * j][...], g, wmv[3 * j + 1][...], wmv[3 * j + 2][...])
            for out, val in zip(o_params[4 * j:4 * j + 4], (g, d, mn, vn)):
                out[...] = val

    vm = pl.BlockSpec(memory_space=pltpu.VMEM)
    flat = [t for p in params for t in p[2:]]
    out_shape = [jax.ShapeDtypeStruct(p.shape, F32) for p in parts]
    out_shape += [jax.ShapeDtypeStruct(p[2].shape, F32) for p in params for _ in range(4)]
    outs = _pcall(body, in_specs=[vm] * (na + 3 * npar), out_specs=[vm] * len(out_shape), out_shape=out_shape,
                  scratch_shapes=[pltpu.VMEM((N_DEV, *p.shape), F32) for p in parts]
                  + [pltpu.SemaphoreType.DMA((7 * na,)), pltpu.SemaphoreType.DMA((7 * na,))],
                  name=name)(*parts, *flat)
    return outs[:na], [outs[na + 4 * j:na + 4 * j + 4] for j in range(npar)]


def _local_step(x, tgt, gains, weights):
    g_pre_mix, g_post_mix, g_pre_ffn, g_post_ffn, g_sb, g_dil = gains
    S, D = x.shape
    hs = g_sb.shape[1] // HEAD_DIM
    hd = g_dil.shape[1] // HEAD_DIM
    cos2, sin_signed = _rope_tables(S)

    h1 = _rms_fwd(x, g_pre_mix + weights.start(), "rms_in")
    w_in_g = weights.w_in(h1)
    proj = _mm_nn(h1, w_in_g, F32, "proj", tn=768)
    o_sb, ct_sb, mx_sb = _sb_fwd(proj, g_sb, hs, "sb_fwd")
    o_dl, lse_dl, mx_dl = _dil_fwd(proj, cos2, sin_signed, g_dil + weights.forward_out(o_sb), 3 * hs, hd, "dil_fwd")
    w_out_g, dep = weights.w_out(o_dl)
    mixed = jnp.concatenate([mx_sb, mx_dl], axis=1)
    mix = _mm_nn(mixed, w_out_g, F32, "mix_out", tn=1024)
    x2, h2 = _mid_fwd(x, mix, g_post_mix + dep, g_pre_ffn, "mid_fwd")
    w_up_g, cwb = weights.w_up(h2)
    u = _mm_nn(h2, w_up_g, BF16, "ffn_up", b_transposed=True)
    y = _geglu_fwd(u, cwb + weights.forward_down(u), "geglu_fwd")
    w_down_g = weights.w_down(y)
    f = _mm_nn(y, w_down_g, F32, "ffn_down", tn=1024, tk=1408)

    dy, df, dg_post_ffn, loss = _loss_bwd(x2, f, tgt, g_post_ffn, "loss_bwd")
    dyv = _mm_nt(df, w_down_g, BF16, "d_y", tn=1408)
    dw_down = _mm_tn(y, df, D, BF16, "dw_down", tm=1408, tn=1024)
    dc, dcw_g, dcw_v = _geglu_bwd(u, dyv, cwb + weights.grad("w_down", dw_down), "geglu_bwd")
    du = _conv_bwd(dc, cwb + weights.grad_reduce("w_down", dc), "conv_bwd")
    dh2 = _mm_nt(du, w_up_g, F32, "d_h2", tk=1408, b_transposed=True)
    dw_up = _mm_tn(du, h2, D, BF16, "dw_up", tm=1408, tn=1024)
    dx2, dmix, dg_pre_ffn, dg_post_mix = _mid_bwd(
        dy, dh2, x2, mix, g_pre_ffn + weights.grad("w_up", dw_up), g_post_mix, "mid_bwd")
    dmixed = _mm_nt(dmix, w_out_g, F32, "d_mixed", after=jnp.reshape(weights.grad_reduce("w_up", dmix), (1, 1)))
    dw_out = _mm_tn(mixed, dmix, D, BF16, "dw_out", tn=1024)
    dq_s, dk_s, dv_s, dg_sb = _sb_bwd(proj, g_sb + weights.grad("w_out", dw_out), o_sb, ct_sb, dmixed, 0, hs, "sb_bwd")
    dq_d, dk_d, dv_d, dg_dil = _dil_bwd(proj, cos2, sin_signed, g_dil + weights.grad_reduce("w_out", dq_s), o_dl, lse_dl,
                                        dmixed, hs, 3 * hs, hd, "dil_bwd")
    dproj = jnp.concatenate([dq_s, dk_s, dv_s, dq_d, dk_d, dv_d], axis=1)
    dw_in = _mm_tn(h1, dproj, w_in_g.shape[2], BF16, "dw_in", tn=768)
    weights.grad("w_in", dw_in)
    dep = weights.grad_reduce("w_in", dproj)
    dh1 = _mm_nt(dproj, w_in_g, F32, "d_h1", tk=768, after=jnp.reshape(dep, (1, 1)))
    grad_x, dg_pre_mix = _first_bwd(dx2, dh1, x, g_pre_mix, "first_bwd")
    small = (dg_pre_mix, dg_post_mix, dg_pre_ffn, dg_post_ffn, dg_sb[0:1], dg_dil[0:1], jnp.concatenate([dcw_g, dcw_v], axis=1))
    weights.small(small, loss)
    return loss, grad_x, small


def _pad_cols(a, to):
    return jnp.pad(a, ((0, 0), (0, to - a.shape[1])))


def kernel(x, pre_mix_gain, post_mix_gain, pre_ffn_gain, post_ffn_gain, w_in, sb_out_gain, dil_out_gain, w_out, w_up, conv_w, conv_b, w_down, loss_target, m_pre_mix_gain, m_post_mix_gain, m_pre_ffn_gain, m_post_ffn_gain, m_w_in, m_sb_out_gain, m_dil_out_gain, m_w_out, m_w_up, m_conv_w, m_conv_b, m_w_down, v_pre_mix_gain, v_post_mix_gain, v_pre_ffn_gain, v_post_ffn_gain, v_w_in, v_sb_out_gain, v_dil_out_gain, v_w_out, v_w_up, v_conv_w, v_conv_b, v_w_down):
    xb, tb = x[0], loss_target[0]
    S, D = xb.shape
    w_in, w_out, w_up, w_down, conv_w = w_in[0], w_out[0], w_up[0], w_down[0], conv_w[0]
    n_in, e_rows = w_in.shape[1], w_out.shape[0]
    cu, half = w_up.shape[1], w_down.shape[0]
    assert cu == 2 * half and half % 16 == 0
    cup = -(-cu // LANES) * LANES
    fp = N_CHIP * cup
    px, py, pc = _place()
    me = 4 * px + 2 * py + pc
    core = jnp.reshape(pc, (1,)).astype(jnp.int32)

    w_up_t, m_up_t, v_up_t = (jnp.swapaxes(t, 0, 1) for t in (w_up, m_w_up[0], v_w_up[0]))

    def by_dev(ref, qx, qy, qc):
        return ref.at[4 * qx + 2 * qy + qc]

    def down_slot(ref, qx, qy, qc):
        return ref.at[2 * qx + qy, pl.ds(qc * half, half)]

    def by_pair(ref, chip, k):
        return ref.at[chip, k]

    def down_pair(ref, chip, k):
        return ref.at[chip, pl.ds(k * half, half)]

    def pair_spec(tr, cols):
        return pl.BlockSpec((None, None, tr, cols), lambda k, i, c: (k, c[0], i, 0))

    tr_in, tr_up = _tile(D, 512, 16), _tile(cup, 256, 16)
    grad_plan = {
        "w_in": ((N_CHIP, 2, D, n_in), by_pair, (D, n_in), tr_in, pair_spec(tr_in, n_in)),
        "w_out": ((N_CHIP, 2, e_rows, D), by_pair, (e_rows, D), e_rows, pair_spec(e_rows, D)),
        "w_up": ((N_CHIP, 2, cup, D), by_pair, (cup, D), tr_up, pair_spec(tr_up, D)),
        "w_down": ((N_CHIP, cup, D), down_pair, (half, D), half,
                   pl.BlockSpec((None, half, D), lambda k, i, c: (k, c[0], 0))),
    }

    class Exchanges:
        def __init__(self):
            self.in_flight = {}

        def start(self):
            def own_slot(shard):
                return lax.dynamic_update_index_in_dim(lax.empty((N_DEV, *shard.shape), shard.dtype), shard, me, 0)

            self.g_in = _gather_start([own_slot(w_in.astype(BF16))], [by_dev], core, "gather_in_start")
            zero = self.g_in[3][0, 0]
            self.g_out = _gather_start([own_slot((w_out + zero).astype(BF16))], [by_dev], self.g_in[3], "gather_out_start")
            up = jnp.pad(w_up_t + zero, ((0, cup - cu), (0, 0))).astype(BF16)
            taps = jnp.pad(conv_w + zero, ((0, 8 - conv_w.shape[0]), (0, cup - cu)))
            self.g_up = _gather_start([own_slot(up), own_slot(taps)], [by_dev, by_dev], self.g_out[3], "gather_up_start")
            down = lax.dynamic_update_slice(jnp.zeros((N_CHIP, cup, D), BF16), (w_down + zero).astype(BF16)[None],
                                            (2 * px + py, pc * half, 0))
            self.g_down = _gather_start([down], [down_slot], self.g_up[3], "gather_down_start")
            return self.g_down[3][0, 0]

        def w_in(self, after):
            send, recv, gath, _ = self.g_in
            fsend, frecv, gath, token = _gather_forward(gath, send, recv, [by_dev], after, "gather_in_forward")
            return _gather_finish(gath, fsend, frecv, [by_dev], token, "gather_in_finish")[0]

        def forward_out(self, after):
            send, recv, gath, _ = self.g_out
            self.p_out = _gather_forward(gath, send, recv, [by_dev], after, "gather_out_forward")
            return self.p_out[3][0, 0]

        def w_out(self, after):
            fsend, frecv, gath, _ = self.p_out
            w_out_g = _gather_finish(gath, fsend, frecv, [by_dev], after, "gather_out_finish")[0]
            send, recv, gath, _ = self.g_up
            self.p_up = _gather_forward(gath, send, recv, [by_dev, by_dev], w_out_g, "gather_up_forward")
            return w_out_g.reshape(1, N_DEV * e_rows, D), self.p_up[3][0, 0]

        def w_up(self, after):
            fsend, frecv, gath, _ = self.p_up
            w_up_g, cw_g = _gather_finish(gath, fsend, frecv, [by_dev, by_dev], after, "gather_up_finish")
            cb = _pad_cols(conv_b.reshape(N_DEV, cu), cup).reshape(1, 2 * fp)
            cw_full = jnp.transpose(cw_g[:, :3, :], (1, 0, 2)).reshape(3, 2 * fp)
            cwb = jnp.concatenate([cw_full, cb, jnp.zeros((4, 2 * fp), F32)], axis=0)
            return w_up_g, cwb

        def forward_down(self, after):
            send, recv, gath, _ = self.g_down
            self.p_down = _gather_forward(gath, send, recv, [down_slot], after, "gather_down_forward")
            return self.p_down[3][0, 0]

        def w_down(self, after):
            fsend, frecv, gath, _ = self.p_down
            return _gather_finish(gath, fsend, frecv, [down_slot], after, "gather_down_finish")[0].reshape(1, fp, D)

        def small(self, small, loss):
            d_pre_mix, d_post_mix, d_pre_ffn, d_post_ffn, d_sb, d_dil, d_conv = small

            def rows_of(*vectors):
                n = vectors[0].shape[1]
                row = lax.broadcasted_iota(jnp.int32, (8, n), 0)
                out = jnp.zeros((8, n), F32)
                for k, vec in enumerate(vectors):
                    out = jnp.where(row == k, vec, out)
                return out

            parts = [rows_of(d_pre_mix, d_post_mix, d_pre_ffn, d_post_ffn, jnp.broadcast_to(loss[:, :1], (1, D))),
                     rows_of(d_sb, d_dil), d_conv]
            params = [(0, 0, pre_mix_gain, m_pre_mix_gain, v_pre_mix_gain), (0, 1, post_mix_gain, m_post_mix_gain, v_post_mix_gain),
                      (0, 2, pre_ffn_gain, m_pre_ffn_gain, v_pre_ffn_gain), (0, 3, post_ffn_gain, m_post_ffn_gain, v_post_ffn_gain),
                      (1, 0, sb_out_gain, m_sb_out_gain, v_sb_out_gain), (1, 1, dil_out_gain, m_dil_out_gain, v_dil_out_gain)]
            (gains_sum, _, self.conv_sum), self.gain_steps = _small_step(parts, params, "small_step")
            self.loss_sum = gains_sum[4, 0]
            return self.conv_sum

        def grad(self, name, dw):
            view_shape, view, block, tr, spec = grad_plan[name]
            send, recv_sems, dw, recv, token = _pair_start(dw.reshape(view_shape), view, block, core, "pair_start_" + name)
            self.in_flight[name] = (dw, recv, send, recv_sems)
            return token[0, 0]

        def grad_reduce(self, name, after):
            _, view, _, tr, spec = grad_plan[name]
            dw, recv = _pair_wait(*self.in_flight[name], view, after, "pair_wait_" + name)
            pair = _pair_add(core, dw, recv, tr, spec, "pair_add_" + name)
            send, recv_sems, pair, parts, token = _chip_start(pair, recv, "chip_start_" + name)
            self.in_flight[name] = (pair, parts, send, recv_sems)
            self.last_token = token
            return token[0, 0]

        def grad_parts(self, name, after):
            return _chip_wait(*self.in_flight[name], after, "chip_wait_" + name)

    exchanges = Exchanges()
    gains = (pre_mix_gain, post_mix_gain, pre_ffn_gain, post_ffn_gain, sb_out_gain, dil_out_gain)
    loss, grad_x, small = _local_step(xb, tb, gains, exchanges)

    loss_out, g_conv = exchanges.loss_sum, exchanges.conv_sum
    g_conv_b = g_conv[3].reshape(N_DEV, cup)[:, :cu].reshape(1, N_DEV * cu)
    g_conv_w = lax.dynamic_index_in_dim(g_conv[0:3].reshape(3, N_DEV, cup), me, axis=1, keepdims=False)[:, :cu]

    def small_adam(w, g, m, v, name):
        one = w.shape[0] == 1
        if one:
            w, g, m, v = (jnp.broadcast_to(t, (8, t.shape[1])) for t in (w, g, m, v))
        outs = _adamw(w, g[None], m, v, name)
        return [o[0:1] for o in outs] if one else outs

    chip_ids = jnp.stack([2 * px + py, 2 * (1 - px) + py, 2 * px + 1 - py, 2 * (1 - px) + 1 - py]).astype(jnp.int32)
    out_w_down = _adamw_chips(w_down, *exchanges.grad_parts("w_down", exchanges.conv_sum), chip_ids, m_w_down[0], v_w_down[0], "adam_w_down")
    out_up_t = _adamw_chips(w_up_t, *exchanges.grad_parts("w_up", out_w_down[1]), chip_ids, m_up_t, v_up_t, "adam_w_up")
    out_w_up = [jnp.swapaxes(o, 0, 1) for o in out_up_t]
    out_w_out = _adamw_chips(w_out, *exchanges.grad_parts("w_out", out_up_t[1]), chip_ids, m_w_out[0], v_w_out[0], "adam_w_out")
    out_w_in = _adamw_chips(w_in, *exchanges.grad_parts("w_in", out_w_out[1]), chip_ids, m_w_in[0], v_w_in[0], "adam_w_in")
    out_pre_mix, out_post_mix, out_pre_ffn, out_post_ffn, out_sb, out_dil = exchanges.gain_steps
    out_conv_b = small_adam(conv_b, g_conv_b, m_conv_b, v_conv_b, "adam_conv_b")
    cw8 = [jnp.pad(t, ((0, 5), (0, 0))) for t in (conv_w, g_conv_w, m_conv_w[0], v_conv_w[0])]
    out_conv_w = [o[0:3] for o in _adamw(cw8[0], cw8[1][None], cw8[2], cw8[3], "adam_conv_w")]

    order = [out_pre_mix, out_post_mix, out_pre_ffn, out_post_ffn, [o[None] for o in out_w_in], out_sb, out_dil,
             [o[None] for o in out_w_out], [o[None] for o in out_w_up], [o[None] for o in out_conv_w], out_conv_b,
             [o[None] for o in out_w_down]]
    outs = [loss_out, grad_x[None]]
    for k in range(4):
        outs += [o[k] for o in order]
    return tuple(outs)
```

```python
import functools
import math

import jax
import jax.numpy as jnp
from jax import lax
from jax.experimental import pallas as pl
from jax.experimental.pallas import tpu as pltpu

F32 = jnp.float32
BF16 = jnp.bfloat16
HEAD_DIM = 128
LANES = 128
KEY_BLOCK = 128
DILATIONS = (1, 4, 16)
RMS_EPS = 1e-6
ROPE_THETA = 10000.0
NEG = -1e30
ADAM_LR, ADAM_B1, ADAM_B2, ADAM_EPS, ADAM_WD, ADAM_STEP = 0.001, 0.9, 0.999, 1e-08, 0.01, 10
MESH = pl.DeviceIdType.MESH
N_DEV = 8
N_CHIP = 4
HBM = pl.BlockSpec(memory_space=pl.ANY)
VMEM_LIMIT = 56 * 1024 * 1024

_pcall = pl.pallas_call


def _tile(n, pref, mult=LANES):
    best = None
    t = mult
    while t <= min(n, pref):
        if n % t == 0:
            best = t
        t += mult
    return n if best is None else best


def _params(*sem):
    return pltpu.CompilerParams(dimension_semantics=sem, vmem_limit_bytes=VMEM_LIMIT)


def _dot(a, b, dims):
    return lax.dot_general(a, b, (dims, ((), ())), preferred_element_type=F32)


NN = ((1,), (0,))
NT = ((1,), (1,))
TN = ((0,), (0,))


def _mm_body(dims, nk, tile):
    if nk == 1:
        def single(a_ref, b_ref, o_ref):
            o_ref[...] = _dot(a_ref[...].astype(BF16), b_ref[...].astype(BF16), dims).astype(o_ref.dtype)

        return single, []

    def body(a_ref, b_ref, o_ref, acc_ref):
        k = pl.program_id(2)

        @pl.when(k == 0)
        def _():
            acc_ref[...] = jnp.zeros_like(acc_ref)

        acc_ref[...] += _dot(a_ref[...].astype(BF16), b_ref[...].astype(BF16), dims)

        @pl.when(k == nk - 1)
        def _():
            o_ref[...] = acc_ref[...].astype(o_ref.dtype)

    return body, [pltpu.VMEM(tile, F32)]


def _mm_nn(a, b3, out_dtype, name, tm=1024, tn=1408, tk=2048, b_transposed=False):
    M, K = a.shape
    C, n = b3.shape[0], b3.shape[1 if b_transposed else 2]
    tm, tk, tn = _tile(M, tm, 8), _tile(K, tk), _tile(n, tn)
    npc, nk = n // tn, K // tk
    body, scratch = _mm_body(NT if b_transposed else NN, nk, (tm, tn))
    b_spec = (pl.BlockSpec((None, tn, tk), lambda i, j, k: (j // npc, j % npc, k)) if b_transposed
              else pl.BlockSpec((None, tk, tn), lambda i, j, k: (j // npc, k, j % npc)))
    return _pcall(
        body, grid=(M // tm, C * npc, nk),
        in_specs=[pl.BlockSpec((tm, tk), lambda i, j, k: (i, k)), b_spec],
        out_specs=pl.BlockSpec((tm, tn), lambda i, j, k: (i, j)),
        out_shape=jax.ShapeDtypeStruct((M, C * n), out_dtype), scratch_shapes=scratch,
        compiler_params=_params("parallel", "parallel", "arbitrary"), name=name)(a, b3)


def _mm_nt(a, b3, out_dtype, name, tm=1024, tn=1024, tk=2048, after=None, b_transposed=False):
    M, _ = a.shape
    C, N, n = (b3.shape[0], b3.shape[2], b3.shape[1]) if b_transposed else b3.shape
    tm, tn, tk = _tile(M, tm, 8), _tile(N, tn), _tile(n, tk)
    kpc = n // tk
    nk = C * kpc
    inner, scratch = _mm_body(NN if b_transposed else NT, nk, (tm, tn))
    extra = [] if after is None else [after]

    def body(a_ref, b_ref, *rest):
        inner(a_ref, b_ref, *rest[len(extra):])

    b_spec = (pl.BlockSpec((None, tk, tn), lambda i, j, k: (k // kpc, k % kpc, j)) if b_transposed
              else pl.BlockSpec((None, tn, tk), lambda i, j, k: (k // kpc, j, k % kpc)))
    return _pcall(
        body, grid=(M // tm, N // tn, nk),
        in_specs=[pl.BlockSpec((tm, tk), lambda i, j, k: (i, k)), b_spec] + [HBM] * len(extra),
        out_specs=pl.BlockSpec((tm, tn), lambda i, j, k: (i, j)),
        out_shape=jax.ShapeDtypeStruct((M, N), out_dtype), scratch_shapes=scratch,
        compiler_params=_params("parallel", "parallel", "arbitrary"), name=name)(a, b3, *extra)


def _mm_tn(x, y, n, out_dtype, name, tm=1024, tn=1408, tk=2048, after=None):
    S, P = x.shape
    C = y.shape[1] // n
    tm, tn, tk = _tile(P, tm), _tile(n, tn), _tile(S, tk, 8)
    npc, nk = n // tn, S // tk
    inner, scratch = _mm_body(TN, nk, (tm, tn))
    extra = [] if after is None else [after]

    def body(x_ref, y_ref, *rest):
        inner(x_ref, y_ref, *rest[len(extra):])

    return _pcall(
        body, grid=(P // tm, C * npc, nk),
        in_specs=[pl.BlockSpec((tk, tm), lambda i, j, k: (k, i)),
                  pl.BlockSpec((tk, tn), lambda i, j, k: (k, j))] + [HBM] * len(extra),
        out_specs=pl.BlockSpec((None, tm, tn), lambda i, j, k: (j // npc, i, j % npc)),
        out_shape=jax.ShapeDtypeStruct((C, P, n), out_dtype), scratch_shapes=scratch,
        compiler_params=_params("parallel", "parallel", "arbitrary"), name=name)(x, y, *extra)


def _rms_scale(v):
    return lax.rsqrt(jnp.mean(v * v, axis=-1, keepdims=True) + RMS_EPS)


def _rms_bwd(gy, v, r):
    return r * gy - v * (r * r * r * jnp.mean(gy * v, axis=-1, keepdims=True))


def _rows_spec(tm, d):
    return pl.BlockSpec((tm, d), lambda i: (i, 0))


def _vec_spec(d):
    return pl.BlockSpec((1, d), lambda i: (0, 0))


def _rms_fwd(x, g, name, tm=256):
    S, D = x.shape

    def body(x_ref, g_ref, h_ref):
        v = x_ref[...]
        h_ref[...] = (v * _rms_scale(v) * g_ref[...]).astype(BF16)

    return _pcall(body, grid=(S // tm,), in_specs=[_rows_spec(tm, D), _vec_spec(D)], out_specs=_rows_spec(tm, D),
                  out_shape=jax.ShapeDtypeStruct((S, D), BF16), compiler_params=_params("parallel"), name=name)(x, g)


def _mid_fwd(x, mix, g_post, g_pre, name, tm=256):
    S, D = x.shape

    def body(x_ref, m_ref, gp_ref, gn_ref, x2_ref, h_ref):
        m = m_ref[...]
        x2 = x_ref[...] + m * _rms_scale(m) * gp_ref[...]
        x2_ref[...] = x2
        h_ref[...] = (x2 * _rms_scale(x2) * gn_ref[...]).astype(BF16)

    return _pcall(body, grid=(S // tm,), in_specs=[_rows_spec(tm, D), _rows_spec(tm, D), _vec_spec(D), _vec_spec(D)],
                  out_specs=[_rows_spec(tm, D), _rows_spec(tm, D)],
                  out_shape=[jax.ShapeDtypeStruct((S, D), F32), jax.ShapeDtypeStruct((S, D), BF16)],
                  compiler_params=_params("parallel"), name=name)(x, mix, g_post, g_pre)


def _loss_bwd(x2, f, tgt, g_post, name, tm=256):
    S, D = x2.shape

    def body(x2_ref, f_ref, t_ref, g_ref, dy_ref, df_ref, dg_ref, ls_ref):
        i = pl.program_id(0)

        @pl.when(i == 0)
        def _():
            dg_ref[...] = jnp.zeros_like(dg_ref)
            ls_ref[...] = jnp.zeros_like(ls_ref)

        fv = f_ref[...]
        r = _rms_scale(fv)
        g = g_ref[...]
        err = x2_ref[...] + fv * r * g - t_ref[...]
        ls_ref[...] += jnp.broadcast_to(0.5 * jnp.sum(jnp.mean(err * err, axis=-1, keepdims=True), axis=0, keepdims=True), ls_ref.shape)
        dy = err * (1.0 / D)
        dy_ref[...] = dy
        df_ref[...] = _rms_bwd(dy * g, fv, r).astype(BF16)
        dg_ref[...] += jnp.sum(dy * fv * r, axis=0, keepdims=True)

    return _pcall(body, grid=(S // tm,),
                  in_specs=[_rows_spec(tm, D), _rows_spec(tm, D), _rows_spec(tm, D), _vec_spec(D)],
                  out_specs=[_rows_spec(tm, D), _rows_spec(tm, D), _vec_spec(D), _vec_spec(LANES)],
                  out_shape=[jax.ShapeDtypeStruct((S, D), F32), jax.ShapeDtypeStruct((S, D), BF16),
                             jax.ShapeDtypeStruct((1, D), F32), jax.ShapeDtypeStruct((1, LANES), F32)],
                  compiler_params=_params("arbitrary"), name=name)(x2, f, tgt, g_post)


def _mid_bwd(dy, dh2, x2, mix, g_pre, g_post, name, tm=256):
    S, D = dy.shape

    def body(dy_ref, dh_ref, x2_ref, m_ref, gn_ref, gp_ref, dx2_ref, dm_ref, dgn_ref, dgp_ref):
        i = pl.program_id(0)

        @pl.when(i == 0)
        def _():
            dgn_ref[...] = jnp.zeros_like(dgn_ref)
            dgp_ref[...] = jnp.zeros_like(dgp_ref)

        x2, dh = x2_ref[...], dh_ref[...]
        r = _rms_scale(x2)
        dx2 = dy_ref[...] + _rms_bwd(dh * gn_ref[...], x2, r)
        dgn_ref[...] += jnp.sum(dh * x2 * r, axis=0, keepdims=True)
        dx2_ref[...] = dx2
        m = m_ref[...]
        rm = _rms_scale(m)
        dm_ref[...] = _rms_bwd(dx2 * gp_ref[...], m, rm).astype(BF16)
        dgp_ref[...] += jnp.sum(dx2 * m * rm, axis=0, keepdims=True)

    return _pcall(body, grid=(S // tm,),
                  in_specs=[_rows_spec(tm, D)] * 4 + [_vec_spec(D)] * 2,
                  out_specs=[_rows_spec(tm, D), _rows_spec(tm, D), _vec_spec(D), _vec_spec(D)],
                  out_shape=[jax.ShapeDtypeStruct((S, D), F32), jax.ShapeDtypeStruct((S, D), BF16),
                             jax.ShapeDtypeStruct((1, D), F32), jax.ShapeDtypeStruct((1, D), F32)],
                  compiler_params=_params("arbitrary"), name=name)(dy, dh2, x2, mix, g_pre, g_post)


def _first_bwd(dx2, dh1, x, g_pre, name, tm=256):
    S, D = x.shape

    def body(dx2_ref, dh_ref, x_ref, g_ref, gx_ref, dg_ref):
        i = pl.program_id(0)

        @pl.when(i == 0)
        def _():
            dg_ref[...] = jnp.zeros_like(dg_ref)

        xv, dh = x_ref[...], dh_ref[...]
        r = _rms_scale(xv)
        gx_ref[...] = dx2_ref[...] + _rms_bwd(dh * g_ref[...], xv, r)
        dg_ref[...] += jnp.sum(dh * xv * r, axis=0, keepdims=True)

    return _pcall(body, grid=(S // tm,), in_specs=[_rows_spec(tm, D)] * 3 + [_vec_spec(D)],
                  out_specs=[_rows_spec(tm, D), _vec_spec(D)],
                  out_shape=[jax.ShapeDtypeStruct((S, D), F32), jax.ShapeDtypeStruct((1, D), F32)],
                  compiler_params=_params("arbitrary"), name=name)(dx2, dh1, x, g_pre)


def _logsig_pair(z):
    lb = jnp.minimum(z, 0.0) - jnp.log(1.0 + jnp.exp(-jnp.abs(z)))
    return lb, lb - z


SB_KEY_BLOCK = 256


def _sum_matrix(strict):
    ia = lax.broadcasted_iota(jnp.int32, (SB_KEY_BLOCK, SB_KEY_BLOCK), 0)
    ib = lax.broadcasted_iota(jnp.int32, (SB_KEY_BLOCK, SB_KEY_BLOCK), 1)
    return ((ia > ib) if strict == ">" else (ia < ib)).astype(BF16)


def _row_total(sums, v, col):
    return jnp.broadcast_to(sums[:, col:col + 1] + v[:, col:col + 1], (v.shape[0], LANES))


def _lanes(c, width):
    return jnp.tile(c, (1, width // LANES))


def _split_dot(v, u):
    hi = v.astype(BF16)
    lo = (v - hi.astype(F32)).astype(BF16)
    return _dot(hi, u, NN) + _dot(lo, u, NN)


def _head_out(o, g):
    return o * _rms_scale(o) * g


def _sb_fwd(proj, gain, n_heads, name, tq=1024):
    S = proj.shape[0]
    H, tk = n_heads, SB_KEY_BLOCK
    tq = _tile(S, tq, 2 * tk)
    scale = HEAD_DIM ** -0.5

    def body(q_ref, k_ref, v_ref, g_ref, o_ref, ct_ref, mx_ref, oacc, cacc):
        i = pl.program_id(1)
        oacc[...] = jnp.zeros_like(oacc)
        cacc[...] = jnp.zeros_like(cacc)
        sums = _sum_matrix(">")

        def run(blocks):
            scored = []
            for k0, r0, diagonal in blocks:
                rows = pl.ds(r0, tq - r0)
                lb, lk = _logsig_pair(_dot(q_ref[rows, :].astype(BF16), k_ref[pl.ds(k0, tk), :].astype(BF16), NT) * scale)
                causal = None
                if diagonal:
                    causal = (lax.broadcasted_iota(jnp.int32, (tq - r0, tk), 1)
                              < lax.broadcasted_iota(jnp.int32, (tq - r0, tk), 0))
                    lk = jnp.where(causal, lk, 0.0)
                scored.append((k0, rows, causal, lb, lk))
            summed = [(k0, rows, causal, lb, lk, _split_dot(lk, sums)) for k0, rows, causal, lb, lk in scored]
            weights = []
            for k0, rows, causal, lb, lk, after in summed:
                c = cacc[rows, :]
                a = jnp.exp(lb + after + _lanes(c, tk))
                if causal is not None:
                    a = jnp.where(causal, a, 0.0)
                cacc[rows, :] = c + _row_total(after, lk, 0)
                weights.append((k0, rows, a.astype(BF16)))
            for k0, rows, a in weights:
                oacc[rows, :] += _dot(a, v_ref[pl.ds(k0, tk), :].astype(BF16), NN)

        for d in reversed(range(0, tq // tk, 2)):
            run([(pl.multiple_of(i * tq + e * tk, tk), e * tk, True) for e in (d + 1, d)])
        per_trip = tq // tk

        def step(it, carry):
            k0 = pl.multiple_of((i - 1 - it) * tq, tq)
            run([(pl.multiple_of(k0 + e * tk, tk), 0, False) for e in reversed(range(per_trip))])
            return carry

        lax.fori_loop(0, i, step, 0)
        o = oacc[...]
        o_ref[...] = o
        ct_ref[...] = cacc[...]
        mx_ref[...] = _head_out(o, g_ref[...]).astype(BF16)

    blk = pl.BlockSpec((tq, HEAD_DIM), lambda h, i: (i, h))
    return _pcall(
        body, grid=(H, S // tq),
        in_specs=[blk, pl.BlockSpec((S, HEAD_DIM), lambda h, i: (0, H + h)),
                  pl.BlockSpec((S, HEAD_DIM), lambda h, i: (0, 2 * H + h)), pl.BlockSpec((1, HEAD_DIM), lambda h, i: (0, h))],
        out_specs=[blk, blk, blk],
        out_shape=[jax.ShapeDtypeStruct((S, H * HEAD_DIM), F32), jax.ShapeDtypeStruct((S, H * HEAD_DIM), F32),
                   jax.ShapeDtypeStruct((S, H * HEAD_DIM), BF16)],
        scratch_shapes=[pltpu.VMEM((tq, HEAD_DIM), F32), pltpu.VMEM((tq, LANES), F32)],
        compiler_params=_params("parallel", "arbitrary"), name=name)(proj, proj, proj, gain)


def _sb_bwd(proj, gain, o_raw, ctot, dmixed, dm_col0, n_heads, name, tq=1024):
    S = proj.shape[0]
    H, tk = n_heads, SB_KEY_BLOCK
    tq = _tile(S, tq, 2 * tk)
    nq = S // tq
    scale = HEAD_DIM ** -0.5

    def body(q_ref, k_ref, v_ref, g_ref, o_ref, ct_ref, dm_ref, dq_ref, dk_ref, dv_ref, dg_ref,
             dkacc, dvacc, dqacc, pfx, gcar, dos):
        i = pl.program_id(1)

        @pl.when(i == 0)
        def _():
            dkacc[...] = jnp.zeros_like(dkacc)
            dvacc[...] = jnp.zeros_like(dvacc)
            dg_ref[...] = jnp.zeros_like(dg_ref)

        o, dm, g = o_ref[...], dm_ref[...], g_ref[...]
        r = _rms_scale(o)
        dos[...] = _rms_bwd(dm * g, o, r).astype(BF16)
        dg_ref[...] += jnp.broadcast_to(jnp.sum(dm * o * r, axis=0, keepdims=True), dg_ref.shape)
        dqacc[...] = jnp.zeros_like(dqacc)
        pfx[...] = jnp.zeros_like(pfx)
        gcar[...] = jnp.zeros_like(gcar)
        later, earlier = _sum_matrix(">"), _sum_matrix("<")

        def run(blocks):
            scored = []
            for k0, r0, diagonal in blocks:
                rows, keys = pl.ds(r0, tq - r0), pl.ds(k0, tk)
                lb, lk = _logsig_pair(_dot(q_ref[rows, :].astype(BF16), k_ref[keys, :].astype(BF16), NT) * scale)
                da = _dot(dos[rows, :], v_ref[keys, :].astype(BF16), NT)
                causal = None
                if diagonal:
                    causal = (lax.broadcasted_iota(jnp.int32, (tq - r0, tk), 1)
                              < lax.broadcasted_iota(jnp.int32, (tq - r0, tk), 0))
                    lk = jnp.where(causal, lk, 0.0)
                scored.append((rows, keys, causal, lb, lk, da))
            summed = [(*blk, _split_dot(blk[4], later)) for blk in scored]
            weighted = []
            for rows, keys, causal, lb, lk, da, after in summed:
                p = pfx[rows, :] + _row_total(after, lk, 0)
                pfx[rows, :] = p
                a = jnp.exp(lb + after + _lanes(ct_ref[rows, :] - p, tk))
                if causal is not None:
                    a = jnp.where(causal, a, 0.0)
                dl = da * a
                weighted.append((rows, keys, causal, lb, a.astype(BF16), dl, _dot(dl.astype(BF16), earlier, NN)))
            cotangents = []
            for rows, keys, causal, lb, a, dl, before in weighted:
                gc = gcar[rows, :]
                gcar[rows, :] = gc + _row_total(before, dl, tk - 1)
                sig = jnp.exp(lb)
                gsum = (before + _lanes(gc, tk)) * sig
                if causal is not None:
                    gsum = jnp.where(causal, gsum, 0.0)
                cotangents.append((rows, keys, a, ((dl * (1.0 - sig) - gsum) * scale).astype(BF16)))
            for rows, keys, a, dz in cotangents:
                q, do = q_ref[rows, :].astype(BF16), dos[rows, :]
                dvacc[keys, :] += _dot(a, do, TN)
                dqacc[rows, :] += _dot(dz, k_ref[keys, :].astype(BF16), NN)
                dkacc[keys, :] += _dot(dz, q, TN)

        def step(j, carry):
            k0 = pl.multiple_of(j * 2 * tk, 2 * tk)
            run([(k0, 0, False), (pl.multiple_of(k0 + tk, tk), 0, False)])
            return carry

        lax.fori_loop(0, i * (tq // tk // 2), step, 0)
        for d in range(0, tq // tk, 2):
            run([(pl.multiple_of(i * tq + e * tk, tk), e * tk, True) for e in (d, d + 1)])
        dq_ref[...] = dqacc[...].astype(BF16)

        @pl.when(i == nq - 1)
        def _():
            dk_ref[...] = dkacc[...].astype(BF16)
            dv_ref[...] = dvacc[...].astype(BF16)

    blk = pl.BlockSpec((tq, HEAD_DIM), lambda h, i: (i, h))
    full = pl.BlockSpec((S, HEAD_DIM), lambda h, i: (0, h))
    W = H * HEAD_DIM
    return _pcall(
        body, grid=(H, nq),
        in_specs=[blk, pl.BlockSpec((S, HEAD_DIM), lambda h, i: (0, H + h)),
                  pl.BlockSpec((S, HEAD_DIM), lambda h, i: (0, 2 * H + h)), pl.BlockSpec((1, HEAD_DIM), lambda h, i: (0, h)),
                  blk, blk, pl.BlockSpec((tq, HEAD_DIM), lambda h, i: (i, dm_col0 + h))],
        out_specs=[blk, full, full, pl.BlockSpec((8, HEAD_DIM), lambda h, i: (0, h))],
        out_shape=[jax.ShapeDtypeStruct((S, W), BF16), jax.ShapeDtypeStruct((S, W), BF16),
                   jax.ShapeDtypeStruct((S, W), BF16), jax.ShapeDtypeStruct((8, W), F32)],
        scratch_shapes=[pltpu.VMEM((S, HEAD_DIM), F32), pltpu.VMEM((S, HEAD_DIM), F32), pltpu.VMEM((tq, HEAD_DIM), F32),
                        pltpu.VMEM((tq, LANES), F32), pltpu.VMEM((tq, LANES), F32), pltpu.VMEM((tq, HEAD_DIM), BF16)],
        compiler_params=_params("arbitrary", "arbitrary"), name=name)(proj, proj, proj, gain, o_raw, ctot, dmixed)


def _rope_tables(S):
    inv_freq = ROPE_THETA ** (-jnp.arange(0, HEAD_DIM, 2, dtype=F32) / HEAD_DIM)
    ang = jnp.arange(S, dtype=F32)[:, None] * inv_freq[None, :]
    cos, sin = jnp.cos(ang), jnp.sin(ang)
    return jnp.concatenate([cos, cos], axis=1), jnp.concatenate([-sin, sin], axis=1)


def _rope(v, cos2, sin_signed):
    return v * cos2 + pltpu.roll(v, HEAD_DIM // 2, axis=1) * sin_signed


def _dil_rows(d, r, l0, n):
    if d == 1:
        return pl.ds(l0 if isinstance(l0, int) else pl.multiple_of(l0, KEY_BLOCK), n)
    return pl.ds(r + d * l0, n, stride=d)


def _dil_blocks(S, visit):
    B = KEY_BLOCK
    group = 16
    for b, d in enumerate(DILATIONS):
        nb = S // d // B
        if nb == 1:
            g = math.gcd(d, group)

            def trip(t, carry, b=b, d=d, g=g):
                visit([(b, d, t * g + u, 0, True) for u in range(g)])
                return carry

            lax.fori_loop(0, d // g, trip, 0)
        elif d == 1:
            visit([(b, d, 0, 0, True)])
            g = max(k for k in range(1, group + 2) if (nb - 1) % k == 0)

            def trip(t, carry, b=b, d=d, g=g):
                visit([(b, d, 0, (1 + t * g + u) * B, False) for u in range(g)])
                return carry

            lax.fori_loop(0, (nb - 1) // g, trip, 0)
        else:
            g = math.gcd(d, max(group // nb, 1))

            def trip(t, carry, b=b, d=d, nb=nb, g=g):
                visit([(b, d, t * g + u, n * B, n == 0) for u in range(g) for n in range(nb)])
                return carry

            lax.fori_loop(0, d // g, trip, 0)


def _dil_mask(first):
    B = KEY_BLOCK
    nk = B if first else 2 * B
    iq = lax.broadcasted_iota(jnp.int32, (B, nk), 0)
    ik = lax.broadcasted_iota(jnp.int32, (B, nk), 1)
    return (ik <= iq) if first else ((ik >= iq) & (ik <= iq + B))


def _dil_fwd(proj, cos2, sin_signed, gain, col0, n_heads, name):
    S = proj.shape[0]
    H, B = n_heads, KEY_BLOCK
    scale = HEAD_DIM ** -0.5
    rc = _tile(S, 256, 8)

    def body(q_ref, k_ref, v_ref, c_ref, s_ref, g_ref, o_ref, l_ref, mx_ref, qr, kr, *per_branch):
        ob, lb = per_branch[:len(DILATIONS)], per_branch[len(DILATIONS):]

        def rope_rows(t, carry):
            rows = pl.ds(pl.multiple_of(t * rc, rc), rc)
            qr[rows, :] = _rope(q_ref[rows, :], c_ref[rows, :], s_ref[rows, :])
            kr[rows, :] = _rope(k_ref[rows, :], c_ref[rows, :], s_ref[rows, :])
            return carry

        lax.fori_loop(0, S // rc, rope_rows, 0)

        def visit(blocks):
            scores = []
            for b, d, r, l0, first in blocks:
                qrows = _dil_rows(d, r, l0, B)
                krows = qrows if first else _dil_rows(d, r, l0 - B, 2 * B)
                s = _dot(qr[qrows, :].astype(BF16), kr[krows, :].astype(BF16), NT) * scale
                scores.append((b, qrows, krows, jnp.where(_dil_mask(first), s, NEG)))
            weights = []
            for b, qrows, krows, s in scores:
                m = jnp.max(s, axis=1, keepdims=True)
                p = jnp.exp(s - m)
                den = jnp.sum(p, axis=1, keepdims=True)
                lb[b][qrows, :] = jnp.broadcast_to(m + jnp.log(den), (B, LANES))
                weights.append((b, qrows, krows, p.astype(BF16), den))
            for b, qrows, krows, p, den in weights:
                ob[b][qrows, :] = _dot(p, v_ref[krows, :].astype(BF16), NN) / den

        _dil_blocks(S, visit)

        def combine(t, carry):
            rows = pl.ds(pl.multiple_of(t * rc, rc), rc)
            l0, l1, l2 = lb[0][rows, :], lb[1][rows, :], lb[2][rows, :]
            m = jnp.maximum(jnp.maximum(l0, l1), l2)
            w0, w1, w2 = jnp.exp(l0 - m), jnp.exp(l1 - m), jnp.exp(l2 - m)
            den = w0 + w1 + w2
            o = (w0 * ob[0][rows, :] + w1 * ob[1][rows, :] + w2 * ob[2][rows, :]) / den
            o_ref[rows, :] = o
            l_ref[rows, :] = m + jnp.log(den)
            mx_ref[rows, :] = _head_out(o, g_ref[...]).astype(BF16)
            return carry

        lax.fori_loop(0, S // rc, combine, 0)

    def col(k):
        return pl.BlockSpec((S, HEAD_DIM), lambda h: (0, col0 + k * H + h))

    tab = pl.BlockSpec((S, HEAD_DIM), lambda h: (0, 0))
    out = pl.BlockSpec((S, HEAD_DIM), lambda h: (0, h))
    W = H * HEAD_DIM
    return _pcall(
        body, grid=(H,),
        in_specs=[col(0), col(1), col(2), tab, tab, pl.BlockSpec((1, HEAD_DIM), lambda h: (0, h))],
        out_specs=[out, out, out],
        out_shape=[jax.ShapeDtypeStruct((S, W), F32), jax.ShapeDtypeStruct((S, W), F32), jax.ShapeDtypeStruct((S, W), BF16)],
        scratch_shapes=[pltpu.VMEM((S, HEAD_DIM), F32)] * (2 + 2 * len(DILATIONS)),
        compiler_params=_params("parallel"), name=name)(proj, proj, proj, cos2, sin_signed, gain)


def _dil_bwd(proj, cos2, sin_signed, gain, o_raw, lse, dmixed, dm_col0, col0, n_heads, name):
    S = proj.shape[0]
    H, B = n_heads, KEY_BLOCK
    scale = HEAD_DIM ** -0.5
    rc = _tile(S, 256, 8)

    def body(q_ref, k_ref, v_ref, c_ref, s_ref, g_ref, o_ref, l_ref, dm_ref, dq_ref, dk_ref, dv_ref, dg_ref,
             qr, kr, dos, dsum, dqr, dkr, dvv):
        dg_ref[...] = jnp.zeros_like(dg_ref)

        def prep(t, carry):
            rows = pl.ds(pl.multiple_of(t * rc, rc), rc)
            qr[rows, :] = _rope(q_ref[rows, :], c_ref[rows, :], s_ref[rows, :])
            kr[rows, :] = _rope(k_ref[rows, :], c_ref[rows, :], s_ref[rows, :])
            o, dm = o_ref[rows, :], dm_ref[rows, :]
            r = _rms_scale(o)
            do = _rms_bwd(dm * g_ref[...], o, r)
            dg_ref[...] += jnp.broadcast_to(jnp.sum(dm * o * r, axis=0, keepdims=True), dg_ref.shape)
            dos[rows, :] = do
            dsum[rows, :] = jnp.broadcast_to(jnp.sum(do * o, axis=1, keepdims=True), (rc, LANES))
            dqr[rows, :] = jnp.zeros((rc, HEAD_DIM), F32)
            dkr[rows, :] = jnp.zeros((rc, HEAD_DIM), F32)
            dvv[rows, :] = jnp.zeros((rc, HEAD_DIM), F32)
            return carry

        lax.fori_loop(0, S // rc, prep, 0)

        def visit(blocks):
            products = []
            for b, d, r, l0, first in blocks:
                qrows = _dil_rows(d, r, l0, B)
                krows = qrows if first else _dil_rows(d, r, l0 - B, 2 * B)
                qs, ks = qr[qrows, :].astype(BF16), kr[krows, :].astype(BF16)
                do = dos[qrows, :].astype(BF16)
                s = jnp.where(_dil_mask(first), _dot(qs, ks, NT) * scale, NEG)
                dp = _dot(do, v_ref[krows, :].astype(BF16), NT)
                products.append((qrows, krows, qs, ks, do, s, dp))
            cotangents = []
            for qrows, krows, qs, ks, do, s, dp in products:
                p = jnp.exp(s - l_ref[qrows, :][:, 0:1])
                ds = (p * (dp - dsum[qrows, :][:, 0:1]) * scale).astype(BF16)
                cotangents.append((qrows, krows, qs, ks, do, p.astype(BF16), ds))
            for qrows, krows, qs, ks, do, p, ds in cotangents:
                dqr[qrows, :] += _dot(ds, ks, NN)
                dkr[krows, :] += _dot(ds, qs, TN)
                dvv[krows, :] += _dot(p, do, TN)

        _dil_blocks(S, visit)

        def finish(t, carry):
            rows = pl.ds(pl.multiple_of(t * rc, rc), rc)
            c, s = c_ref[rows, :], s_ref[rows, :]
            dq, dk = dqr[rows, :], dkr[rows, :]
            dq_ref[rows, :] = (dq * c + pltpu.roll(dq * s, HEAD_DIM // 2, axis=1)).astype(BF16)
            dk_ref[rows, :] = (dk * c + pltpu.roll(dk * s, HEAD_DIM // 2, axis=1)).astype(BF16)
            dv_ref[rows, :] = dvv[rows, :].astype(BF16)
            return carry

        lax.fori_loop(0, S // rc, finish, 0)

    def col(k):
        return pl.BlockSpec((S, HEAD_DIM), lambda h: (0, col0 + k * H + h))

    tab = pl.BlockSpec((S, HEAD_DIM), lambda h: (0, 0))
    out = pl.BlockSpec((S, HEAD_DIM), lambda h: (0, h))
    W = H * HEAD_DIM
    big = pltpu.VMEM((S, HEAD_DIM), F32)
    return _pcall(
        body, grid=(H,),
        in_specs=[col(0), col(1), col(2), tab, tab, pl.BlockSpec((1, HEAD_DIM), lambda h: (0, h)), out, out,
                  pl.BlockSpec((S, HEAD_DIM), lambda h: (0, dm_col0 + h))],
        out_specs=[out, out, out, pl.BlockSpec((8, HEAD_DIM), lambda h: (0, h))],
        out_shape=[jax.ShapeDtypeStruct((S, W), BF16), jax.ShapeDtypeStruct((S, W), BF16),
                   jax.ShapeDtypeStruct((S, W), BF16), jax.ShapeDtypeStruct((8, W), F32)],
        scratch_shapes=[big, big, big, pltpu.VMEM((S, LANES), F32), big, big, big],
        compiler_params=_params("parallel"), name=name)(proj, proj, proj, cos2, sin_signed, gain, o_raw, lse, dmixed)


GELU_C = math.sqrt(2.0 / math.pi)
GELU_A = 0.044715
HALO = 16


def _shift_down(cur, halo, k):
    out = pltpu.roll(cur, k, axis=0)
    row = lax.broadcasted_iota(jnp.int32, cur.shape, 0)
    for t in range(k):
        out = jnp.where(row == t, halo[HALO - k + t:HALO - k + t + 1, :], out)
    return out


def _shift_up(cur, halo, k):
    n = cur.shape[0]
    out = pltpu.roll(cur, n - k, axis=0)
    row = lax.broadcasted_iota(jnp.int32, cur.shape, 0)
    for t in range(k):
        out = jnp.where(row == n - k + t, halo[t:t + 1, :], out)
    return out


def _conv3(cur, halo, cw):
    return _shift_down(cur, halo, 2) * cw[0:1, :] + _shift_down(cur, halo, 1) * cw[1:2, :] + cur * cw[2:3, :] + cw[3:4, :]


def _gelu_parts(x):
    t = jnp.tanh(GELU_C * (x + GELU_A * x * x * x))
    return 0.5 * x * (1.0 + t), t


def _geglu_specs(tm, tn, ncb):
    hb = tm // HALO

    def cur(off):
        return pl.BlockSpec((tm, tn), lambda j, i: (i, off + j))

    def prev(off):
        return pl.BlockSpec((HALO, tn), lambda j, i: (jnp.maximum(i * hb - 1, 0), off + j))

    def taps(off):
        return pl.BlockSpec((8, tn), lambda j, i: (0, off + j))

    return [cur(0), prev(0), cur(ncb), prev(ncb), taps(0), taps(ncb)]


def _geglu_fwd(u, cwb, name, tm=256, tn=1408):
    S, F2 = u.shape
    F = F2 // 2
    tm, tn = _tile(S, tm, HALO), _tile(F, tn)
    ncb = F // tn

    def body(g_ref, gp_ref, v_ref, vp_ref, cg_ref, cv_ref, y_ref):
        top = pl.program_id(1) > 0
        gp = jnp.where(top, gp_ref[...].astype(F32), 0.0)
        vp = jnp.where(top, vp_ref[...].astype(F32), 0.0)
        gc = _conv3(g_ref[...].astype(F32), gp, cg_ref[...])
        vc = _conv3(v_ref[...].astype(F32), vp, cv_ref[...])
        y_ref[...] = (_gelu_parts(gc)[0] * vc).astype(BF16)

    return _pcall(body, grid=(ncb, S // tm), in_specs=_geglu_specs(tm, tn, ncb),
                  out_specs=pl.BlockSpec((tm, tn), lambda j, i: (i, j)),
                  out_shape=jax.ShapeDtypeStruct((S, F), BF16),
                  compiler_params=_params("parallel", "parallel"), name=name)(u, u, u, u, cwb, cwb)


def _geglu_bwd(u, dy, cwb, name, tm=256, tn=512):
    S, F2 = u.shape
    F = F2 // 2
    tm, tn = _tile(S, tm, HALO), _tile(F, tn)
    ncb = F // tn

    def body(g_ref, gp_ref, v_ref, vp_ref, cg_ref, cv_ref, dy_ref, dc_ref, dwg_ref, dwv_ref):
        i = pl.program_id(1)

        @pl.when(i == 0)
        def _():
            dwg_ref[...] = jnp.zeros_like(dwg_ref)
            dwv_ref[...] = jnp.zeros_like(dwv_ref)

        top = i > 0
        g, v = g_ref[...].astype(F32), v_ref[...].astype(F32)
        gp = jnp.where(top, gp_ref[...].astype(F32), 0.0)
        vp = jnp.where(top, vp_ref[...].astype(F32), 0.0)
        gc = _conv3(g, gp, cg_ref[...])
        vc = _conv3(v, vp, cv_ref[...])
        act, t = _gelu_parts(gc)
        dact = 0.5 * (1.0 + t) + 0.5 * gc * (1.0 - t * t) * GELU_C * (1.0 + 3.0 * GELU_A * gc * gc)
        dyv = dy_ref[...].astype(F32)
        dgc = dyv * vc * dact
        dvc = dyv * act
        dc_ref[0] = dgc.astype(BF16)
        dc_ref[1] = dvc.astype(BF16)

        def taps(out_ref, dc, cur, halo):
            out_ref[0:1, :] += jnp.sum(dc * _shift_down(cur, halo, 2), axis=0, keepdims=True)
            out_ref[1:2, :] += jnp.sum(dc * _shift_down(cur, halo, 1), axis=0, keepdims=True)
            out_ref[2:3, :] += jnp.sum(dc * cur, axis=0, keepdims=True)
            out_ref[3:4, :] += jnp.sum(dc, axis=0, keepdims=True)

        taps(dwg_ref, dgc, g, gp)
        taps(dwv_ref, dvc, v, vp)

    return _pcall(body, grid=(ncb, S // tm),
                  in_specs=_geglu_specs(tm, tn, ncb) + [pl.BlockSpec((tm, tn), lambda j, i: (i, j))],
                  out_specs=[pl.BlockSpec((2, tm, tn), lambda j, i: (0, i, j)),
                             pl.BlockSpec((8, tn), lambda j, i: (0, j)), pl.BlockSpec((8, tn), lambda j, i: (0, j))],
                  out_shape=[jax.ShapeDtypeStruct((2, S, F), BF16), jax.ShapeDtypeStruct((8, F), F32),
                             jax.ShapeDtypeStruct((8, F), F32)],
                  compiler_params=_params("parallel", "arbitrary"), name=name)(u, u, u, u, cwb, cwb, dy)


def _conv_bwd(dc, cwb, name, tm=512, tn=1408):
    _, S, F = dc.shape
    tm, tn = _tile(S, tm, HALO), _tile(F, tn)
    ncb, nrb = F // tn, S // tm
    hb = tm // HALO

    def body(c_ref, n_ref, w_ref, du_ref):
        cur = c_ref[...].astype(F32)
        nxt = jnp.where(pl.program_id(2) < nrb - 1, n_ref[...].astype(F32), 0.0)
        w = w_ref[...]
        du = cur * w[2:3, :] + _shift_up(cur, nxt, 1) * w[1:2, :] + _shift_up(cur, nxt, 2) * w[0:1, :]
        du_ref[...] = du.astype(BF16)

    return _pcall(body, grid=(2, ncb, nrb),
                  in_specs=[pl.BlockSpec((None, tm, tn), lambda c, j, i: (c, i, j)),
                            pl.BlockSpec((None, HALO, tn), lambda c, j, i: (c, jnp.minimum((i + 1) * hb, S // HALO - 1), j)),
                            pl.BlockSpec((8, tn), lambda c, j, i: (0, c * ncb + j))],
                  out_specs=pl.BlockSpec((tm, tn), lambda c, j, i: (i, c * ncb + j)),
                  out_shape=jax.ShapeDtypeStruct((S, 2 * F), BF16),
                  compiler_params=_params("parallel", "parallel", "parallel"), name=name)(dc, dc, cwb)


def _adam_math(w, g, m, v):
    m = ADAM_B1 * m + (1.0 - ADAM_B1) * g
    v = ADAM_B2 * v + (1.0 - ADAM_B2) * (g * g)
    m_hat = m / (1.0 - ADAM_B1 ** ADAM_STEP)
    v_hat = v / (1.0 - ADAM_B2 ** ADAM_STEP)
    return -ADAM_LR * (m_hat / (jnp.sqrt(v_hat) + ADAM_EPS) + ADAM_WD * w), m, v


def _adamw(w, parts, m, v, name, tr=256):
    R, C = w.shape
    n, _, Cp = parts.shape
    tr = _tile(R, tr, 8)

    def body(w_ref, p_ref, m_ref, v_ref, g_out, d_out, m_out, v_out):
        g = p_ref[0, :, 0:C].astype(F32)
        for k in range(1, n):
            g = g + p_ref[k, :, 0:C].astype(F32)
        d, mn, vn = _adam_math(w_ref[...], g, m_ref[...], v_ref[...])
        g_out[...] = g
        d_out[...] = d
        m_out[...] = mn
        v_out[...] = vn

    spec = pl.BlockSpec((tr, C), lambda i: (i, 0))
    shape = jax.ShapeDtypeStruct((R, C), F32)
    return _pcall(body, grid=(R // tr,), in_specs=[spec, pl.BlockSpec((n, tr, Cp), lambda i: (0, i, 0)), spec, spec],
                  out_specs=[spec] * 4, out_shape=[shape] * 4, compiler_params=_params("parallel"), name=name)(w, parts, m, v)


def _adamw_chips(w, pair, parts, chip_ids, m, v, name, tr=256):
    R, C = w.shape
    Cp = pair.shape[2]
    by_columns = C == Cp and _tile(R, tr, 16) < 64
    tr, tc = (R, _tile(C, 256)) if by_columns else (_tile(R, tr, 16), C)

    def body(ids_ref, w_ref, own_ref, p1_ref, p2_ref, p3_ref, m_ref, v_ref, g_out, d_out, m_out, v_out):
        g = own_ref[:, 0:tc].astype(F32)
        for ref in (p1_ref, p2_ref, p3_ref):
            g = g + ref[:, 0:tc].astype(F32)
        d, mn, vn = _adam_math(w_ref[...], g, m_ref[...], v_ref[...])
        g_out[...] = g
        d_out[...] = d
        m_out[...] = mn
        v_out[...] = vn

    if by_columns:
        spec = pl.BlockSpec((tr, tc), lambda j, ids: (0, j))
    else:
        spec = pl.BlockSpec((tr, tc), lambda i, ids: (i, 0))

    def chip(k):
        if by_columns:
            return pl.BlockSpec((None, tr, tc), lambda j, ids: (ids[k], 0, j))
        return pl.BlockSpec((None, tr, Cp), lambda i, ids: (ids[k], i, 0))

    shape = jax.ShapeDtypeStruct((R, C), F32)
    grid_spec = pltpu.PrefetchScalarGridSpec(
        num_scalar_prefetch=1, grid=(C // tc if by_columns else R // tr,),
        in_specs=[spec, chip(0), chip(1), chip(2), chip(3), spec, spec], out_specs=[spec] * 4)
    return _pcall(body, grid_spec=grid_spec, out_shape=[shape] * 4, compiler_params=_params("parallel"),
                  name=name)(chip_ids, w, pair, parts, parts, parts, m, v)


def _place():
    return lax.axis_index("x"), lax.axis_index("y"), lax.axis_index("c")


def _other_chips(x, y):
    return [(1 - x, y), (x, 1 - y), (1 - x, 1 - y)]


IN_HBM = pl.BlockSpec(memory_space=pltpu.HBM)
SEM = pl.BlockSpec(memory_space=pltpu.SEMAPHORE)
EFFECT = pltpu.SideEffectType.DATAFLOW_SIDE_EFFECTING
TOKEN = jax.ShapeDtypeStruct((8, LANES), F32)
TOKEN_SPEC = pl.BlockSpec(memory_space=pltpu.VMEM)


def _in_hbm(a):
    return pltpu.with_memory_space_constraint(a, pltpu.HBM)


def _landing(shape):
    return _in_hbm(lax.empty(shape.shape, shape.dtype))


def _hbm_like(a):
    return pltpu.HBM(a.shape, a.dtype)


def _gather_start(landing, slots, after, name):
    na = len(landing)

    def body(*refs):
        land = refs[:na]
        send_sems, recv_sems = refs[na + 1], refs[na + 2]
        token = refs[-1]
        x, y, c = _place()
        for a in range(na):
            own = slots[a](land[a], x, y, c)
            for k, to in enumerate([(x, y, 1 - c)] + [(*chip, c) for chip in _other_chips(x, y)]):
                pltpu.make_async_remote_copy(
                    src_ref=own, dst_ref=own, send_sem=send_sems.at[4 * a + k],
                    recv_sem=recv_sems.at[4 * a + k], device_id=to, device_id_type=MESH).start()
        token[...] = jnp.zeros_like(token)

    sems = pltpu.SemaphoreType.DMA((4 * na,))
    outs = _pcall(
        body, in_specs=[IN_HBM] * na + [HBM],
        out_specs=[SEM, SEM] + [IN_HBM] * na + [TOKEN_SPEC],
        out_shape=[sems, sems] + [_hbm_like(s) for s in landing] + [TOKEN],
        input_output_aliases={a: 2 + a for a in range(na)},
        compiler_params=pltpu.CompilerParams(has_side_effects=EFFECT), name=name,
    )(*[_in_hbm(s) for s in landing], after)
    return outs[0], outs[1], outs[2:2 + na], outs[-1]


def _gather_forward(gathered, send_sems, recv_sems, slots, after, name):
    na = len(gathered)

    def body(*refs):
        gath = refs[:na]
        send1, recv1 = refs[na], refs[na + 1]
        fsend, frecv = refs[na + 3], refs[na + 4]
        token = refs[-1]
        x, y, c = _place()
        chips = _other_chips(x, y)
        for a in range(na):
            for k, peer in enumerate([(x, y, 1 - c)] + [(*chip, c) for chip in chips]):
                arrival = pltpu.make_async_remote_copy(
                    src_ref=slots[a](gath[a], x, y, c), dst_ref=slots[a](gath[a], *peer), send_sem=send1.at[4 * a + k],
                    recv_sem=recv1.at[4 * a + k], device_id=peer, device_id_type=MESH)
                arrival.wait_send()
                arrival.wait_recv()
        for a in range(na):
            for j, chip in enumerate(chips):
                view = slots[a](gath[a], *chip, c)
                pltpu.make_async_remote_copy(
                    src_ref=view, dst_ref=view, send_sem=fsend.at[3 * a + j], recv_sem=frecv.at[3 * a + j],
                    device_id=(x, y, 1 - c), device_id_type=MESH).start()
        token[...] = jnp.zeros_like(token)

    sems = pltpu.SemaphoreType.DMA((3 * na,))
    outs = _pcall(
        body, in_specs=[IN_HBM] * na + [SEM, SEM, HBM],
        out_specs=[SEM, SEM] + [IN_HBM] * na + [TOKEN_SPEC],
        out_shape=[sems, sems] + [_hbm_like(g) for g in gathered] + [TOKEN],
        input_output_aliases={a: 2 + a for a in range(na)},
        compiler_params=pltpu.CompilerParams(has_side_effects=EFFECT), name=name,
    )(*gathered, send_sems, recv_sems, after)
    return outs[0], outs[1], outs[2:2 + na], outs[-1]


def _gather_finish(gathered, fsend, frecv, slots, after, name):
    na = len(gathered)

    def body(*refs):
        gath, fs, fr = refs[:na], refs[na], refs[na + 1]
        x, y, c = _place()
        for a in range(na):
            for j, chip in enumerate(_other_chips(x, y)):
                passed = pltpu.make_async_remote_copy(
                    src_ref=slots[a](gath[a], *chip, c), dst_ref=slots[a](gath[a], *chip, 1 - c),
                    send_sem=fs.at[3 * a + j], recv_sem=fr.at[3 * a + j], device_id=(x, y, 1 - c), device_id_type=MESH)
                passed.wait_send()
                passed.wait_recv()

    outs = _pcall(
        body, in_specs=[IN_HBM] * na + [SEM, SEM, HBM], out_specs=[IN_HBM] * na,
        out_shape=[_hbm_like(g) for g in gathered], input_output_aliases={a: a for a in range(na)},
        compiler_params=pltpu.CompilerParams(has_side_effects=EFFECT), name=name,
    )(*gathered, fsend, frecv, after)
    return list(outs)


def _pair_copy(view, src, land, send_sems, recv_sems, chip):
    x, y, c = _place()
    return pltpu.make_async_remote_copy(
        src_ref=view(src, chip, 1 - c), dst_ref=land.at[chip], send_sem=send_sems.at[chip], recv_sem=recv_sems.at[chip],
        device_id=(x, y, 1 - c), device_id_type=MESH)


def _pair_start(grad, view, block, after, name):
    def body(src, land, after_ref, send_sems, recv_sems, src_thru, land_thru, token):
        for chip in range(N_CHIP):
            _pair_copy(view, src, land, send_sems, recv_sems, chip).start()
        token[...] = jnp.zeros_like(token)

    sems = pltpu.SemaphoreType.DMA((N_CHIP,))
    land = jax.ShapeDtypeStruct((N_CHIP, *block), BF16)
    return _pcall(
        body, in_specs=[IN_HBM, IN_HBM, HBM], out_specs=[SEM, SEM, IN_HBM, IN_HBM, TOKEN_SPEC],
        out_shape=[sems, sems, _hbm_like(grad), _hbm_like(land), TOKEN], input_output_aliases={0: 2, 1: 3},
        compiler_params=pltpu.CompilerParams(has_side_effects=EFFECT), name=name,
    )(_in_hbm(grad), _landing(land), after)


def _pair_wait(grad, recv, send_sems, recv_sems, view, after, name):
    def body(src, land, send, recv_s, after_ref, src_thru, land_thru):
        for chip in range(N_CHIP):
            copy = _pair_copy(view, src, land, send, recv_s, chip)
            copy.wait_send()
            copy.wait_recv()

    return _pcall(
        body, in_specs=[IN_HBM, IN_HBM, SEM, SEM, HBM], out_specs=[IN_HBM, IN_HBM],
        out_shape=[_hbm_like(grad), _hbm_like(recv)], input_output_aliases={0: 0, 1: 1},
        compiler_params=pltpu.CompilerParams(has_side_effects=EFFECT), name=name,
    )(grad, recv, send_sems, recv_sems, after)


def _chip_start(pair, after, name):
    def body(src, land, after_ref, send_sems, recv_sems, src_thru, land_thru, token):
        x, y, c = _place()
        for j, (px, py) in enumerate(_other_chips(x, y)):
            pltpu.make_async_remote_copy(
                src_ref=src.at[2 * px + py], dst_ref=land.at[2 * x + y], send_sem=send_sems.at[j], recv_sem=recv_sems.at[j],
                device_id=(px, py, c), device_id_type=MESH).start()
        token[...] = jnp.zeros_like(token)

    sems = pltpu.SemaphoreType.DMA((3,))
    return _pcall(
        body, in_specs=[IN_HBM, IN_HBM, HBM], out_specs=[SEM, SEM, IN_HBM, IN_HBM, TOKEN_SPEC],
        out_shape=[sems, sems, _hbm_like(pair), _hbm_like(pair), TOKEN], input_output_aliases={0: 2, 1: 3},
        compiler_params=pltpu.CompilerParams(has_side_effects=EFFECT), name=name,
    )(_in_hbm(pair), _landing(pair), after)


def _chip_wait(pair, parts, send_sems, recv_sems, after, name):
    def body(src, land, send, recv, after_ref, src_thru, land_thru):
        x, y, c = _place()
        for j, (px, py) in enumerate(_other_chips(x, y)):
            copy = pltpu.make_async_remote_copy(
                src_ref=src.at[2 * px + py], dst_ref=land.at[2 * px + py], send_sem=send.at[j], recv_sem=recv.at[j],
                device_id=(px, py, c), device_id_type=MESH)
            copy.wait_send()
            copy.wait_recv()

    return _pcall(
        body, in_specs=[IN_HBM, IN_HBM, SEM, SEM, HBM], out_specs=[IN_HBM, IN_HBM],
        out_shape=[_hbm_like(pair), _hbm_like(parts)], input_output_aliases={0: 0, 1: 1},
        compiler_params=pltpu.CompilerParams(has_side_effects=EFFECT), name=name,
    )(pair, parts, send_sems, recv_sems, after)


def _pair_add(core, grad, recv, block, grad_spec, name):
    _, R, C = recv.shape
    tr = block

    def body(c_ref, g_ref, r_ref, o_ref):
        o_ref[...] = (g_ref[...].astype(F32) + r_ref[...].astype(F32)).astype(BF16)

    grid_spec = pltpu.PrefetchScalarGridSpec(
        num_scalar_prefetch=1, grid=(N_CHIP, R // tr),
        in_specs=[grad_spec, pl.BlockSpec((None, tr, C), lambda k, i, c: (k, i, 0))],
        out_specs=pl.BlockSpec((None, tr, C), lambda k, i, c: (k, i, 0)))
    return _pcall(body, grid_spec=grid_spec, out_shape=jax.ShapeDtypeStruct(recv.shape, BF16),
                  compiler_params=_params("parallel", "parallel"), name=name)(core, grad, recv)


def _small_step(parts, params, name):
    na, npar = len(parts), len(params)

    def body(*refs):
        p_refs, wmv = refs[:na], refs[na:na + 3 * npar]
        o_parts = refs[na + 3 * npar:2 * na + 3 * npar]
        o_params = refs[2 * na + 3 * npar:2 * na + 7 * npar]
        alls, (send_sems, recv_sems) = refs[2 * na + 7 * npar:3 * na + 7 * npar], refs[3 * na + 7 * npar:]
        x, y, c = _place()
        me = 4 * x + 2 * y + c
        peers = [(x, y, 1 - c)] + [(px, py, pc) for px, py in _other_chips(x, y) for pc in (c, 1 - c)]
        copies = []
        for a in range(na):
            alls[a][me] = p_refs[a][...]
            copies += [pltpu.make_async_remote_copy(
                src_ref=p_refs[a], dst_ref=alls[a].at[me], send_sem=send_sems.at[7 * a + k], recv_sem=recv_sems.at[7 * a + k],
                device_id=peer, device_id_type=MESH) for k, peer in enumerate(peers)]
        for cp in copies:
            cp.start()
        for a in range(na):
            for k, (px, py, pc) in enumerate(peers):
                pltpu.make_async_remote_copy(
                    src_ref=p_refs[a], dst_ref=alls[a].at[4 * px + 2 * py + pc], send_sem=send_sems.at[7 * a + k],
                    recv_sem=recv_sems.at[7 * a + k], device_id=peers[k], device_id_type=MESH).wait_recv()
        for cp in copies:
            cp.wait_send()
        sums = []
        for a in range(na):
            acc = alls[a][0]
            for k in range(1, N_DEV):
                acc = acc + alls[a][k]
            o_parts[a][...] = acc
            sums.append(acc)
        for j, (a, row, _, _, _) in enumerate(params):
            g = sums[a][row:row + 1, :]
            d, mn, vn = _adam_math(wmv[3 * j][...], g, wmv[3 * j + 1][...], wmv[3 * j + 2][...])
            for out, val in zip(o_params[4 * j:4 * j + 4], (g, d, mn, vn)):
                out[...] = val

    vm = pl.BlockSpec(memory_space=pltpu.VMEM)
    flat = [t for p in params for t in p[2:]]
    out_shape = [jax.ShapeDtypeStruct(p.shape, F32) for p in parts]
    out_shape += [jax.ShapeDtypeStruct(p[2].shape, F32) for p in params for _ in range(4)]
    outs = _pcall(body, in_specs=[vm] * (na + 3 * npar), out_specs=[vm] * len(out_shape), out_shape=out_shape,
                  scratch_shapes=[pltpu.VMEM((N_DEV, *p.shape), F32) for p in parts]
                  + [pltpu.SemaphoreType.DMA((7 * na,)), pltpu.SemaphoreType.DMA((7 * na,))],
                  name=name)(*parts, *flat)
    return outs[:na], [outs[na + 4 * j:na + 4 * j + 4] for j in range(npar)]


def _local_step(x, tgt, gains, weights):
    g_pre_mix, g_post_mix, g_pre_ffn, g_post_ffn, g_sb, g_dil = gains
    S, D = x.shape
    hs = g_sb.shape[1] // HEAD_DIM
    hd = g_dil.shape[1] // HEAD_DIM
    cos2, sin_signed = _rope_tables(S)

    h1 = _rms_fwd(x, g_pre_mix + weights.start(), "rms_in")
    w_in_g = weights.w_in(h1)
    proj = _mm_nn(h1, w_in_g, F32, "proj", tn=768)
    o_sb, ct_sb, mx_sb = _sb_fwd(proj, g_sb, hs, "sb_fwd")
    o_dl, lse_dl, mx_dl = _dil_fwd(proj, cos2, sin_signed, g_dil + weights.forward_out(o_sb), 3 * hs, hd, "dil_fwd")
    w_out_g, dep = weights.w_out(o_dl)
    mixed = jnp.concatenate([mx_sb, mx_dl], axis=1)
    mix = _mm_nn(mixed, w_out_g, F32, "mix_out", tn=1024)
    x2, h2 = _mid_fwd(x, mix, g_post_mix + dep, g_pre_ffn, "mid_fwd")
    w_up_g, cwb = weights.w_up(h2)
    u = _mm_nn(h2, w_up_g, BF16, "ffn_up", b_transposed=True)
    y = _geglu_fwd(u, cwb + weights.forward_down(u), "geglu_fwd")
    w_down_g = weights.w_down(y)
    f = _mm_nn(y, w_down_g, F32, "ffn_down", tn=1024, tk=1408)

    dy, df, dg_post_ffn, loss = _loss_bwd(x2, f, tgt, g_post_ffn, "loss_bwd")
    dyv = _mm_nt(df, w_down_g, BF16, "d_y", tn=1408)
    dw_down = _mm_tn(y, df, D, BF16, "dw_down", tm=1408, tn=1024)
    dc, dcw_g, dcw_v = _geglu_bwd(u, dyv, cwb + weights.grad("w_down", dw_down), "geglu_bwd")
    du = _conv_bwd(dc, cwb + weights.grad_reduce("w_down", dc), "conv_bwd")
    dh2 = _mm_nt(du, w_up_g, F32, "d_h2", tk=1408, b_transposed=True)
    dw_up = _mm_tn(du, h2, D, BF16, "dw_up", tm=1408, tn=1024)
    dx2, dmix, dg_pre_ffn, dg_post_mix = _mid_bwd(
        dy, dh2, x2, mix, g_pre_ffn + weights.grad("w_up", dw_up), g_post_mix, "mid_bwd")
    dmixed = _mm_nt(dmix, w_out_g, F32, "d_mixed", after=jnp.reshape(weights.grad_reduce("w_up", dmix), (1, 1)))
    dw_out = _mm_tn(mixed, dmix, D, BF16, "dw_out", tn=1024)
    dq_s, dk_s, dv_s, dg_sb = _sb_bwd(proj, g_sb + weights.grad("w_out", dw_out), o_sb, ct_sb, dmixed, 0, hs, "sb_bwd")
    dq_d, dk_d, dv_d, dg_dil = _dil_bwd(proj, cos2, sin_signed, g_dil + weights.grad_reduce("w_out", dq_s), o_dl, lse_dl,
                                        dmixed, hs, 3 * hs, hd, "dil_bwd")
    dproj = jnp.concatenate([dq_s, dk_s, dv_s, dq_d, dk_d, dv_d], axis=1)
    dw_in = _mm_tn(h1, dproj, w_in_g.shape[2], BF16, "dw_in", tn=768)
    weights.grad("w_in", dw_in)
    dep = weights.grad_reduce("w_in", dproj)
    dh1 = _mm_nt(dproj, w_in_g, F32, "d_h1", tk=768, after=jnp.reshape(dep, (1, 1)))
    grad_x, dg_pre_mix = _first_bwd(dx2, dh1, x, g_pre_mix, "first_bwd")
    small = (dg_pre_mix, dg_post_mix, dg_pre_ffn, dg_post_ffn, dg_sb[0:1], dg_dil[0:1], jnp.concatenate([dcw_g, dcw_v], axis=1))
    weights.small(small, loss)
    return loss, grad_x, small


def _pad_cols(a, to):
    return jnp.pad(a, ((0, 0), (0, to - a.shape[1])))


def kernel(x, pre_mix_gain, post_mix_gain, pre_ffn_gain, post_ffn_gain, w_in, sb_out_gain, dil_out_gain, w_out, w_up, conv_w, conv_b, w_down, loss_target, m_pre_mix_gain, m_post_mix_gain, m_pre_ffn_gain, m_post_ffn_gain, m_w_in, m_sb_out_gain, m_dil_out_gain, m_w_out, m_w_up, m_conv_w, m_conv_b, m_w_down, v_pre_mix_gain, v_post_mix_gain, v_pre_ffn_gain, v_post_ffn_gain, v_w_in, v_sb_out_gain, v_dil_out_gain, v_w_out, v_w_up, v_conv_w, v_conv_b, v_w_down):
    xb, tb = x[0], loss_target[0]
    S, D = xb.shape
    w_in, w_out, w_up, w_down, conv_w = w_in[0], w_out[0], w_up[0], w_down[0], conv_w[0]
    n_in, e_rows = w_in.shape[1], w_out.shape[0]
    cu, half = w_up.shape[1], w_down.shape[0]
    assert cu == 2 * half and half % 16 == 0
    cup = -(-cu // LANES) * LANES
    fp = N_CHIP * cup
    px, py, pc = _place()
    me = 4 * px + 2 * py + pc
    core = jnp.reshape(pc, (1,)).astype(jnp.int32)

    w_up_t, m_up_t, v_up_t = (jnp.swapaxes(t, 0, 1) for t in (w_up, m_w_up[0], v_w_up[0]))

    def by_dev(ref, qx, qy, qc):
        return ref.at[4 * qx + 2 * qy + qc]

    def down_slot(ref, qx, qy, qc):
        return ref.at[2 * qx + qy, pl.ds(qc * half, half)]

    def by_pair(ref, chip, k):
        return ref.at[chip, k]

    def down_pair(ref, chip, k):
        return ref.at[chip, pl.ds(k * half, half)]

    def pair_spec(tr, cols):
        return pl.BlockSpec((None, None, tr, cols), lambda k, i, c: (k, c[0], i, 0))

    tr_in, tr_up = _tile(D, 512, 16), _tile(cup, 256, 16)
    grad_plan = {
        "w_in": ((N_CHIP, 2, D, n_in), by_pair, (D, n_in), tr_in, pair_spec(tr_in, n_in)),
        "w_out": ((N_CHIP, 2, e_rows, D), by_pair, (e_rows, D), e_rows, pair_spec(e_rows, D)),
        "w_up": ((N_CHIP, 2, cup, D), by_pair, (cup, D), tr_up, pair_spec(tr_up, D)),
        "w_down": ((N_CHIP, cup, D), down_pair, (half, D), half,
                   pl.BlockSpec((None, half, D), lambda k, i, c: (k, c[0], 0))),
    }

    class Exchanges:
        def __init__(self):
            self.in_flight = {}

        def start(self):
            def own_slot(shard):
                return lax.dynamic_update_index_in_dim(lax.empty((N_DEV, *shard.shape), shard.dtype), shard, me, 0)

            self.g_in = _gather_start([own_slot(w_in.astype(BF16))], [by_dev], core, "gather_in_start")
            zero = self.g_in[3][0, 0]
            self.g_out = _gather_start([own_slot((w_out + zero).astype(BF16))], [by_dev], self.g_in[3], "gather_out_start")
            up = jnp.pad(w_up_t + zero, ((0, cup - cu), (0, 0))).astype(BF16)
            taps = jnp.pad(conv_w + zero, ((0, 8 - conv_w.shape[0]), (0, cup - cu)))
            self.g_up = _gather_start([own_slot(up), own_slot(taps)], [by_dev, by_dev], self.g_out[3], "gather_up_start")
            down = lax.dynamic_update_slice(jnp.zeros((N_CHIP, cup, D), BF16), (w_down + zero).astype(BF16)[None],
                                            (2 * px + py, pc * half, 0))
            self.g_down = _gather_start([down], [down_slot], self.g_up[3], "gather_down_start")
            return self.g_down[3][0, 0]

        def w_in(self, after):
            send, recv, gath, _ = self.g_in
            fsend, frecv, gath, token = _gather_forward(gath, send, recv, [by_dev], after, "gather_in_forward")
            return _gather_finish(gath, fsend, frecv, [by_dev], token, "gather_in_finish")[0]

        def forward_out(self, after):
            send, recv, gath, _ = self.g_out
            self.p_out = _gather_forward(gath, send, recv, [by_dev], after, "gather_out_forward")
            return self.p_out[3][0, 0]

        def w_out(self, after):
            fsend, frecv, gath, _ = self.p_out
            w_out_g = _gather_finish(gath, fsend, frecv, [by_dev], after, "gather_out_finish")[0]
            send, recv, gath, _ = self.g_up
            self.p_up = _gather_forward(gath, send, recv, [by_dev, by_dev], w_out_g, "gather_up_forward")
            return w_out_g.reshape(1, N_DEV * e_rows, D), self.p_up[3][0, 0]

        def w_up(self, after):
            fsend, frecv, gath, _ = self.p_up
            w_up_g, cw_g = _gather_finish(gath, fsend, frecv, [by_dev, by_dev], after, "gather_up_finish")
            cb = _pad_cols(conv_b.reshape(N_DEV, cu), cup).reshape(1, 2 * fp)
            cw_full = jnp.transpose(cw_g[:, :3, :], (1, 0, 2)).reshape(3, 2 * fp)
            cwb = jnp.concatenate([cw_full, cb, jnp.zeros((4, 2 * fp), F32)], axis=0)
            return w_up_g, cwb

        def forward_down(self, after):
            send, recv, gath, _ = self.g_down
            self.p_down = _gather_forward(gath, send, recv, [down_slot], after, "gather_down_forward")
            return self.p_down[3][0, 0]

        def w_down(self, after):
            fsend, frecv, gath, _ = self.p_down
            return _gather_finish(gath, fsend, frecv, [down_slot], after, "gather_down_finish")[0].reshape(1, fp, D)

        def small(self, small, loss):
            d_pre_mix, d_post_mix, d_pre_ffn, d_post_ffn, d_sb, d_dil, d_conv = small

            def rows_of(*vectors):
                n = vectors[0].shape[1]
                row = lax.broadcasted_iota(jnp.int32, (8, n), 0)
                out = jnp.zeros((8, n), F32)
                for k, vec in enumerate(vectors):
                    out = jnp.where(row == k, vec, out)
                return out

            parts = [rows_of(d_pre_mix, d_post_mix, d_pre_ffn, d_post_ffn, jnp.broadcast_to(loss[:, :1], (1, D))),
                     rows_of(d_sb, d_dil), d_conv]
            params = [(0, 0, pre_mix_gain, m_pre_mix_gain, v_pre_mix_gain), (0, 1, post_mix_gain, m_post_mix_gain, v_post_mix_gain),
                      (0, 2, pre_ffn_gain, m_pre_ffn_gain, v_pre_ffn_gain), (0, 3, post_ffn_gain, m_post_ffn_gain, v_post_ffn_gain),
                      (1, 0, sb_out_gain, m_sb_out_gain, v_sb_out_gain), (1, 1, dil_out_gain, m_dil_out_gain, v_dil_out_gain)]
            (gains_sum, _, self.conv_sum), self.gain_steps = _small_step(parts, params, "small_step")
            self.loss_sum = gains_sum[4, 0]
            return self.conv_sum

        def grad(self, name, dw):
            view_shape, view, block, tr, spec = grad_plan[name]
            send, recv_sems, dw, recv, token = _pair_start(dw.reshape(view_shape), view, block, core, "pair_start_" + name)
            self.in_flight[name] = (dw, recv, send, recv_sems)
            return token[0, 0]

        def grad_reduce(self, name, after):
            _, view, _, tr, spec = grad_plan[name]
            dw, recv = _pair_wait(*self.in_flight[name], view, after, "pair_wait_" + name)
            pair = _pair_add(core, dw, recv, tr, spec, "pair_add_" + name)
            send, recv_sems, pair, parts, token = _chip_start(pair, recv, "chip_start_" + name)
            self.in_flight[name] = (pair, parts, send, recv_sems)
            self.last_token = token
            return token[0, 0]

        def grad_parts(self, name, after):
            return _chip_wait(*self.in_flight[name], after, "chip_wait_" + name)

    exchanges = Exchanges()
    gains = (pre_mix_gain, post_mix_gain, pre_ffn_gain, post_ffn_gain, sb_out_gain, dil_out_gain)
    loss, grad_x, small = _local_step(xb, tb, gains, exchanges)

    loss_out, g_conv = exchanges.loss_sum, exchanges.conv_sum
    g_conv_b = g_conv[3].reshape(N_DEV, cup)[:, :cu].reshape(1, N_DEV * cu)
    g_conv_w = lax.dynamic_index_in_dim(g_conv[0:3].reshape(3, N_DEV, cup), me, axis=1, keepdims=False)[:, :cu]

    def small_adam(w, g, m, v, name):
        one = w.shape[0] == 1
        if one:
            w, g, m, v = (jnp.broadcast_to(t, (8, t.shape[1])) for t in (w, g, m, v))
        outs = _adamw(w, g[None], m, v, name)
        return [o[0:1] for o in outs] if one else outs

    chip_ids = jnp.stack([2 * px + py, 2 * (1 - px) + py, 2 * px + 1 - py, 2 * (1 - px) + 1 - py]).astype(jnp.int32)
    out_w_down = _adamw_chips(w_down, *exchanges.grad_parts("w_down", exchanges.conv_sum), chip_ids, m_w_down[0], v_w_down[0], "adam_w_down")
    out_up_t = _adamw_chips(w_up_t, *exchanges.grad_parts("w_up", out_w_down[1]), chip_ids, m_up_t, v_up_t, "adam_w_up")
    out_w_up = [jnp.swapaxes(o, 0, 1) for o in out_up_t]
    out_w_out = _adamw_chips(w_out, *exchanges.grad_parts("w_out", out_up_t[1]), chip_ids, m_w_out[0], v_w_out[0], "adam_w_out")
    out_w_in = _adamw_chips(w_in, *exchanges.grad_parts("w_in", out_w_out[1]), chip_ids, m_w_in[0], v_w_in[0], "adam_w_in")
    out_pre_mix, out_post_mix, out_pre_ffn, out_post_ffn, out_sb, out_dil = exchanges.gain_steps
    out_conv_b = small_adam(conv_b, g_conv_b, m_conv_b, v_conv_b, "adam_conv_b")
    cw8 = [jnp.pad(t, ((0, 5), (0, 0))) for t in (conv_w, g_conv_w, m_conv_w[0], v_conv_w[0])]
    out_conv_w = [o[0:3] for o in _adamw(cw8[0], cw8[1][None], cw8[2], cw8[3], "adam_conv_w")]

    order = [out_pre_mix, out_post_mix, out_pre_ffn, out_post_ffn, [o[None] for o in out_w_in], out_sb, out_dil,
             [o[None] for o in out_w_out], [o[None] for o in out_w_up], [o[None] for o in out_conv_w], out_conv_b,
             [o[None] for o in out_w_down]]
    outs = [loss_out, grad_x[None]]
    for k in range(4):
        outs += [o[k] for o in order]
    return tuple(outs)
```

```python
import functools
import math

import jax
import jax.numpy as jnp
from jax import lax
from jax.experimental import pallas as pl
from jax.experimental.pallas import tpu as pltpu

F32 = jnp.float32
BF16 = jnp.bfloat16
HEAD_DIM = 128
LANES = 128
KEY_BLOCK = 128
DILATIONS = (1, 4, 16)
RMS_EPS = 1e-6
ROPE_THETA = 10000.0
NEG = -1e30
ADAM_LR, ADAM_B1, ADAM_B2, ADAM_EPS, ADAM_WD, ADAM_STEP = 0.001, 0.9, 0.999, 1e-08, 0.01, 10
MESH = pl.DeviceIdType.MESH
N_DEV = 8
N_CHIP = 4
HBM = pl.BlockSpec(memory_space=pl.ANY)
VMEM_LIMIT = 56 * 1024 * 1024

_pcall = pl.pallas_call


def _tile(n, pref, mult=LANES):
    best = None
    t = mult
    while t <= min(n, pref):
        if n % t == 0:
            best = t
        t += mult
    return n if best is None else best


def _params(*sem):
    return pltpu.CompilerParams(dimension_semantics=sem, vmem_limit_bytes=VMEM_LIMIT)


def _dot(a, b, dims):
    return lax.dot_general(a, b, (dims, ((), ())), preferred_element_type=F32)


NN = ((1,), (0,))
NT = ((1,), (1,))
TN = ((0,), (0,))


def _mm_body(dims, nk, tile):
    if nk == 1:
        def single(a_ref, b_ref, o_ref):
            o_ref[...] = _dot(a_ref[...].astype(BF16), b_ref[...].astype(BF16), dims).astype(o_ref.dtype)

        return single, []

    def body(a_ref, b_ref, o_ref, acc_ref):
        k = pl.program_id(2)

        @pl.when(k == 0)
        def _():
            acc_ref[...] = jnp.zeros_like(acc_ref)

        acc_ref[...] += _dot(a_ref[...].astype(BF16), b_ref[...].astype(BF16), dims)

        @pl.when(k == nk - 1)
        def _():
            o_ref[...] = acc_ref[...].astype(o_ref.dtype)

    return body, [pltpu.VMEM(tile, F32)]


def _mm_nn(a, b3, out_dtype, name, tm=1024, tn=1408, tk=2048, b_transposed=False):
    M, K = a.shape
    C, n = b3.shape[0], b3.shape[1 if b_transposed else 2]
    tm, tk, tn = _tile(M, tm, 8), _tile(K, tk), _tile(n, tn)
    npc, nk = n // tn, K // tk
    body, scratch = _mm_body(NT if b_transposed else NN, nk, (tm, tn))
    b_spec = (pl.BlockSpec((None, tn, tk), lambda i, j, k: (j // npc, j % npc, k)) if b_transposed
              else pl.BlockSpec((None, tk, tn), lambda i, j, k: (j // npc, k, j % npc)))
    return _pcall(
        body, grid=(M // tm, C * npc, nk),
        in_specs=[pl.BlockSpec((tm, tk), lambda i, j, k: (i, k)), b_spec],
        out_specs=pl.BlockSpec((tm, tn), lambda i, j, k: (i, j)),
        out_shape=jax.ShapeDtypeStruct((M, C * n), out_dtype), scratch_shapes=scratch,
        compiler_params=_params("parallel", "parallel", "arbitrary"), name=name)(a, b3)


def _mm_nt(a, b3, out_dtype, name, tm=1024, tn=1024, tk=2048, after=None, b_transposed=False):
    M, _ = a.shape
    C, N, n = (b3.shape[0], b3.shape[2], b3.shape[1]) if b_transposed else b3.shape
    tm, tn, tk = _tile(M, tm, 8), _tile(N, tn), _tile(n, tk)
    kpc = n // tk
    nk = C * kpc
    inner, scratch = _mm_body(NN if b_transposed else NT, nk, (tm, tn))
    extra = [] if after is None else [after]

    def body(a_ref, b_ref, *rest):
        inner(a_ref, b_ref, *rest[len(extra):])

    b_spec = (pl.BlockSpec((None, tk, tn), lambda i, j, k: (k // kpc, k % kpc, j)) if b_transposed
              else pl.BlockSpec((None, tn, tk), lambda i, j, k: (k // kpc, j, k % kpc)))
    return _pcall(
        body, grid=(M // tm, N // tn, nk),
        in_specs=[pl.BlockSpec((tm, tk), lambda i, j, k: (i, k)), b_spec] + [HBM] * len(extra),
        out_specs=pl.BlockSpec((tm, tn), lambda i, j, k: (i, j)),
        out_shape=jax.ShapeDtypeStruct((M, N), out_dtype), scratch_shapes=scratch,
        compiler_params=_params("parallel", "parallel", "arbitrary"), name=name)(a, b3, *extra)


def _mm_tn(x, y, n, out_dtype, name, tm=1024, tn=1408, tk=2048, after=None):
    S, P = x.shape
    C = y.shape[1] // n
    tm, tn, tk = _tile(P, tm), _tile(n, tn), _tile(S, tk, 8)
    npc, nk = n // tn, S // tk
    inner, scratch = _mm_body(TN, nk, (tm, tn))
    extra = [] if after is None else [after]

    def body(x_ref, y_ref, *rest):
        inner(x_ref, y_ref, *rest[len(extra):])

    return _pcall(
        body, grid=(P // tm, C * npc, nk),
        in_specs=[pl.BlockSpec((tk, tm), lambda i, j, k: (k, i)),
                  pl.BlockSpec((tk, tn), lambda i, j, k: (k, j))] + [HBM] * len(extra),
        out_specs=pl.BlockSpec((None, tm, tn), lambda i, j, k: (j // npc, i, j % npc)),
        out_shape=jax.ShapeDtypeStruct((C, P, n), out_dtype), scratch_shapes=scratch,
        compiler_params=_params("parallel", "parallel", "arbitrary"), name=name)(x, y, *extra)


def _rms_scale(v):
    return lax.rsqrt(jnp.mean(v * v, axis=-1, keepdims=True) + RMS_EPS)


def _rms_bwd(gy, v, r):
    return r * gy - v * (r * r * r * jnp.mean(gy * v, axis=-1, keepdims=True))


def _rows_spec(tm, d):
    return pl.BlockSpec((tm, d), lambda i: (i, 0))


def _vec_spec(d):
    return pl.BlockSpec((1, d), lambda i: (0, 0))


def _rms_fwd(x, g, name, tm=256):
    S, D = x.shape

    def body(x_ref, g_ref, h_ref):
        v = x_ref[...]
        h_ref[...] = (v * _rms_scale(v) * g_ref[...]).astype(BF16)

    return _pcall(body, grid=(S // tm,), in_specs=[_rows_spec(tm, D), _vec_spec(D)], out_specs=_rows_spec(tm, D),
                  out_shape=jax.ShapeDtypeStruct((S, D), BF16), compiler_params=_params("parallel"), name=name)(x, g)


def _mid_fwd(x, mix, g_post, g_pre, name, tm=256):
    S, D = x.shape

    def body(x_ref, m_ref, gp_ref, gn_ref, x2_ref, h_ref):
        m = m_ref[...]
        x2 = x_ref[...] + m * _rms_scale(m) * gp_ref[...]
        x2_ref[...] = x2
        h_ref[...] = (x2 * _rms_scale(x2) * gn_ref[...]).astype(BF16)

    return _pcall(body, grid=(S // tm,), in_specs=[_rows_spec(tm, D), _rows_spec(tm, D), _vec_spec(D), _vec_spec(D)],
                  out_specs=[_rows_spec(tm, D), _rows_spec(tm, D)],
                  out_shape=[jax.ShapeDtypeStruct((S, D), F32), jax.ShapeDtypeStruct((S, D), BF16)],
                  compiler_params=_params("parallel"), name=name)(x, mix, g_post, g_pre)


def _loss_bwd(x2, f, tgt, g_post, name, tm=256):
    S, D = x2.shape

    def body(x2_ref, f_ref, t_ref, g_ref, dy_ref, df_ref, dg_ref, ls_ref):
        i = pl.program_id(0)

        @pl.when(i == 0)
        def _():
            dg_ref[...] = jnp.zeros_like(dg_ref)
            ls_ref[...] = jnp.zeros_like(ls_ref)

        fv = f_ref[...]
        r = _rms_scale(fv)
        g = g_ref[...]
        err = x2_ref[...] + fv * r * g - t_ref[...]
        ls_ref[...] += jnp.broadcast_to(0.5 * jnp.sum(jnp.mean(err * err, axis=-1, keepdims=True), axis=0, keepdims=True), ls_ref.shape)
        dy = err * (1.0 / D)
        dy_ref[...] = dy
        df_ref[...] = _rms_bwd(dy * g, fv, r).astype(BF16)
        dg_ref[...] += jnp.sum(dy * fv * r, axis=0, keepdims=True)

    return _pcall(body, grid=(S // tm,),
                  in_specs=[_rows_spec(tm, D), _rows_spec(tm, D), _rows_spec(tm, D), _vec_spec(D)],
                  out_specs=[_rows_spec(tm, D), _rows_spec(tm, D), _vec_spec(D), _vec_spec(LANES)],
                  out_shape=[jax.ShapeDtypeStruct((S, D), F32), jax.ShapeDtypeStruct((S, D), BF16),
                             jax.ShapeDtypeStruct((1, D), F32), jax.ShapeDtypeStruct((1, LANES), F32)],
                  compiler_params=_params("arbitrary"), name=name)(x2, f, tgt, g_post)


def _mid_bwd(dy, dh2, x2, mix, g_pre, g_post, name, tm=256):
    S, D = dy.shape

    def body(dy_ref, dh_ref, x2_ref, m_ref, gn_ref, gp_ref, dx2_ref, dm_ref, dgn_ref, dgp_ref):
        i = pl.program_id(0)

        @pl.when(i == 0)
        def _():
            dgn_ref[...] = jnp.zeros_like(dgn_ref)
            dgp_ref[...] = jnp.zeros_like(dgp_ref)

        x2, dh = x2_ref[...], dh_ref[...]
        r = _rms_scale(x2)
        dx2 = dy_ref[...] + _rms_bwd(dh * gn_ref[...], x2, r)
        dgn_ref[...] += jnp.sum(dh * x2 * r, axis=0, keepdims=True)
        dx2_ref[...] = dx2
        m = m_ref[...]
        rm = _rms_scale(m)
        dm_ref[...] = _rms_bwd(dx2 * gp_ref[...], m, rm).astype(BF16)
        dgp_ref[...] += jnp.sum(dx2 * m * rm, axis=0, keepdims=True)

    return _pcall(body, grid=(S // tm,),
                  in_specs=[_rows_spec(tm, D)] * 4 + [_vec_spec(D)] * 2,
                  out_specs=[_rows_spec(tm, D), _rows_spec(tm, D), _vec_spec(D), _vec_spec(D)],
                  out_shape=[jax.ShapeDtypeStruct((S, D), F32), jax.ShapeDtypeStruct((S, D), BF16),
                             jax.ShapeDtypeStruct((1, D), F32), jax.ShapeDtypeStruct((1, D), F32)],
                  compiler_params=_params("arbitrary"), name=name)(dy, dh2, x2, mix, g_pre, g_post)


def _first_bwd(dx2, dh1, x, g_pre, name, tm=256):
    S, D = x.shape

    def body(dx2_ref, dh_ref, x_ref, g_ref, gx_ref, dg_ref):
        i = pl.program_id(0)

        @pl.when(i == 0)
        def _():
            dg_ref[...] = jnp.zeros_like(dg_ref)

        xv, dh = x_ref[...], dh_ref[...]
        r = _rms_scale(xv)
        gx_ref[...] = dx2_ref[...] + _rms_bwd(dh * g_ref[...], xv, r)
        dg_ref[...] += jnp.sum(dh * xv * r, axis=0, keepdims=True)

    return _pcall(body, grid=(S // tm,), in_specs=[_rows_spec(tm, D)] * 3 + [_vec_spec(D)],
                  out_specs=[_rows_spec(tm, D), _vec_spec(D)],
                  out_shape=[jax.ShapeDtypeStruct((S, D), F32), jax.ShapeDtypeStruct((1, D), F32)],
                  compiler_params=_params("arbitrary"), name=name)(dx2, dh1, x, g_pre)


def _logsig_pair(z):
    lb = jnp.minimum(z, 0.0) - jnp.log(1.0 + jnp.exp(-jnp.abs(z)))
    return lb, lb - z


SB_KEY_BLOCK = 256


def _sum_matrix(strict):
    ia = lax.broadcasted_iota(jnp.int32, (SB_KEY_BLOCK, SB_KEY_BLOCK), 0)
    ib = lax.broadcasted_iota(jnp.int32, (SB_KEY_BLOCK, SB_KEY_BLOCK), 1)
    return ((ia > ib) if strict == ">" else (ia < ib)).astype(BF16)


def _row_total(sums, v, col):
    return jnp.broadcast_to(sums[:, col:col + 1] + v[:, col:col + 1], (v.shape[0], LANES))


def _lanes(c, width):
    return jnp.tile(c, (1, width // LANES))


def _split_dot(v, u):
    hi = v.astype(BF16)
    lo = (v - hi.astype(F32)).astype(BF16)
    return _dot(hi, u, NN) + _dot(lo, u, NN)


def _head_out(o, g):
    return o * _rms_scale(o) * g


def _sb_fwd(proj, gain, n_heads, name, tq=1024):
    S = proj.shape[0]
    H, tk = n_heads, SB_KEY_BLOCK
    tq = _tile(S, tq, 2 * tk)
    scale = HEAD_DIM ** -0.5

    def body(q_ref, k_ref, v_ref, g_ref, o_ref, ct_ref, mx_ref, oacc, cacc):
        i = pl.program_id(1)
        oacc[...] = jnp.zeros_like(oacc)
        cacc[...] = jnp.zeros_like(cacc)
        sums = _sum_matrix(">")

        def run(blocks):
            scored = []
            for k0, r0, diagonal in blocks:
                rows = pl.ds(r0, tq - r0)
                lb, lk = _logsig_pair(_dot(q_ref[rows, :].astype(BF16), k_ref[pl.ds(k0, tk), :].astype(BF16), NT) * scale)
                causal = None
                if diagonal:
                    causal = (lax.broadcasted_iota(jnp.int32, (tq - r0, tk), 1)
                              < lax.broadcasted_iota(jnp.int32, (tq - r0, tk), 0))
                    lk = jnp.where(causal, lk, 0.0)
                scored.append((k0, rows, causal, lb, lk))
            summed = [(k0, rows, causal, lb, lk, _split_dot(lk, sums)) for k0, rows, causal, lb, lk in scored]
            weights = []
            for k0, rows, causal, lb, lk, after in summed:
                c = cacc[rows, :]
                a = jnp.exp(lb + after + _lanes(c, tk))
                if causal is not None:
                    a = jnp.where(causal, a, 0.0)
                cacc[rows, :] = c + _row_total(after, lk, 0)
                weights.append((k0, rows, a.astype(BF16)))
            for k0, rows, a in weights:
                oacc[rows, :] += _dot(a, v_ref[pl.ds(k0, tk), :].astype(BF16), NN)

        for d in reversed(range(0, tq // tk, 2)):
            run([(pl.multiple_of(i * tq + e * tk, tk), e * tk, True) for e in (d + 1, d)])
        per_trip = tq // tk

        def step(it, carry):
            k0 = pl.multiple_of((i - 1 - it) * tq, tq)
            run([(pl.multiple_of(k0 + e * tk, tk), 0, False) for e in reversed(range(per_trip))])
            return carry

        lax.fori_loop(0, i, step, 0)
        o = oacc[...]
        o_ref[...] = o
        ct_ref[...] = cacc[...]
        mx_ref[...] = _head_out(o, g_ref[...]).astype(BF16)

    blk = pl.BlockSpec((tq, HEAD_DIM), lambda h, i: (i, h))
    return _pcall(
        body, grid=(H, S // tq),
        in_specs=[blk, pl.BlockSpec((S, HEAD_DIM), lambda h, i: (0, H + h)),
                  pl.BlockSpec((S, HEAD_DIM), lambda h, i: (0, 2 * H + h)), pl.BlockSpec((1, HEAD_DIM), lambda h, i: (0, h))],
        out_specs=[blk, blk, blk],
        out_shape=[jax.ShapeDtypeStruct((S, H * HEAD_DIM), F32), jax.ShapeDtypeStruct((S, H * HEAD_DIM), F32),
                   jax.ShapeDtypeStruct((S, H * HEAD_DIM), BF16)],
        scratch_shapes=[pltpu.VMEM((tq, HEAD_DIM), F32), pltpu.VMEM((tq, LANES), F32)],
        compiler_params=_params("parallel", "arbitrary"), name=name)(proj, proj, proj, gain)


def _sb_bwd(proj, gain, o_raw, ctot, dmixed, dm_col0, n_heads, name, tq=1024):
    S = proj.shape[0]
    H, tk = n_heads, SB_KEY_BLOCK
    tq = _tile(S, tq, 2 * tk)
    nq = S // tq
    scale = HEAD_DIM ** -0.5

    def body(q_ref, k_ref, v_ref, g_ref, o_ref, ct_ref, dm_ref, dq_ref, dk_ref, dv_ref, dg_ref,
             dkacc, dvacc, dqacc, pfx, gcar, dos):
        i = pl.program_id(1)

        @pl.when(i == 0)
        def _():
            dkacc[...] = jnp.zeros_like(dkacc)
            dvacc[...] = jnp.zeros_like(dvacc)
            dg_ref[...] = jnp.zeros_like(dg_ref)

        o, dm, g = o_ref[...], dm_ref[...], g_ref[...]
        r = _rms_scale(o)
        dos[...] = _rms_bwd(dm * g, o, r).astype(BF16)
        dg_ref[...] += jnp.broadcast_to(jnp.sum(dm * o * r, axis=0, keepdims=True), dg_ref.shape)
        dqacc[...] = jnp.zeros_like(dqacc)
        pfx[...] = jnp.zeros_like(pfx)
        gcar[...] = jnp.zeros_like(gcar)
        later, earlier = _sum_matrix(">"), _sum_matrix("<")

        def run(blocks):
            scored = []
            for k0, r0, diagonal in blocks:
                rows, keys = pl.ds(r0, tq - r0), pl.ds(k0, tk)
                lb, lk = _logsig_pair(_dot(q_ref[rows, :].astype(BF16), k_ref[keys, :].astype(BF16), NT) * scale)
                da = _dot(dos[rows, :], v_ref[keys, :].astype(BF16), NT)
                causal = None
                if diagonal:
                    causal = (lax.broadcasted_iota(jnp.int32, (tq - r0, tk), 1)
                              < lax.broadcasted_iota(jnp.int32, (tq - r0, tk), 0))
                    lk = jnp.where(causal, lk, 0.0)
                scored.append((rows, keys, causal, lb, lk, da))
            summed = [(*blk, _split_dot(blk[4], later)) for blk in scored]
            weighted = []
            for rows, keys, causal, lb, lk, da, after in summed:
                p = pfx[rows, :] + _row_total(after, lk, 0)
                pfx[rows, :] = p
                a = jnp.exp(lb + after + _lanes(ct_ref[rows, :] - p, tk))
                if causal is not None:
                    a = jnp.where(causal, a, 0.0)
                dl = da * a
                weighted.append((rows, keys, causal, lb, a.astype(BF16), dl, _dot(dl.astype(BF16), earlier, NN)))
            cotangents = []
            for rows, keys, causal, lb, a, dl, before in weighted:
                gc = gcar[rows, :]
                gcar[rows, :] = gc + _row_total(before, dl, tk - 1)
                sig = jnp.exp(lb)
                gsum = (before + _lanes(gc, tk)) * sig
                if causal is not None:
                    gsum = jnp.where(causal, gsum, 0.0)
                cotangents.append((rows, keys, a, ((dl * (1.0 - sig) - gsum) * scale).astype(BF16)))
            for rows, keys, a, dz in cotangents:
                q, do = q_ref[rows, :].astype(BF16), dos[rows, :]
                dvacc[keys, :] += _dot(a, do, TN)
                dqacc[rows, :] += _dot(dz, k_ref[keys, :].astype(BF16), NN)
                dkacc[keys, :] += _dot(dz, q, TN)

        def step(j, carry):
            k0 = pl.multiple_of(j * 2 * tk, 2 * tk)
            run([(k0, 0, False), (pl.multiple_of(k0 + tk, tk), 0, False)])
            return carry

        lax.fori_loop(0, i * (tq // tk // 2), step, 0)
        for d in range(0, tq // tk, 2):
            run([(pl.multiple_of(i * tq + e * tk, tk), e * tk, True) for e in (d, d + 1)])
        dq_ref[...] = dqacc[...].astype(BF16)

        @pl.when(i == nq - 1)
        def _():
            dk_ref[...] = dkacc[...].astype(BF16)
            dv_ref[...] = dvacc[...].astype(BF16)

    blk = pl.BlockSpec((tq, HEAD_DIM), lambda h, i: (i, h))
    full = pl.BlockSpec((S, HEAD_DIM), lambda h, i: (0, h))
    W = H * HEAD_DIM
    return _pcall(
        body, grid=(H, nq),
        in_specs=[blk, pl.BlockSpec((S, HEAD_DIM), lambda h, i: (0, H + h)),
                  pl.BlockSpec((S, HEAD_DIM), lambda h, i: (0, 2 * H + h)), pl.BlockSpec((1, HEAD_DIM), lambda h, i: (0, h)),
                  blk, blk, pl.BlockSpec((tq, HEAD_DIM), lambda h, i: (i, dm_col0 + h))],
        out_specs=[blk, full, full, pl.BlockSpec((8, HEAD_DIM), lambda h, i: (0, h))],
        out_shape=[jax.ShapeDtypeStruct((S, W), BF16), jax.ShapeDtypeStruct((S, W), BF16),
                   jax.ShapeDtypeStruct((S, W), BF16), jax.ShapeDtypeStruct((8, W), F32)],
        scratch_shapes=[pltpu.VMEM((S, HEAD_DIM), F32), pltpu.VMEM((S, HEAD_DIM), F32), pltpu.VMEM((tq, HEAD_DIM), F32),
                        pltpu.VMEM((tq, LANES), F32), pltpu.VMEM((tq, LANES), F32), pltpu.VMEM((tq, HEAD_DIM), BF16)],
        compiler_params=_params("arbitrary", "arbitrary"), name=name)(proj, proj, proj, gain, o_raw, ctot, dmixed)


def _rope_tables(S):
    inv_freq = ROPE_THETA ** (-jnp.arange(0, HEAD_DIM, 2, dtype=F32) / HEAD_DIM)
    ang = jnp.arange(S, dtype=F32)[:, None] * inv_freq[None, :]
    cos, sin = jnp.cos(ang), jnp.sin(ang)
    return jnp.concatenate([cos, cos], axis=1), jnp.concatenate([-sin, sin], axis=1)


def _rope(v, cos2, sin_signed):
    return v * cos2 + pltpu.roll(v, HEAD_DIM // 2, axis=1) * sin_signed


def _dil_rows(d, r, l0, n):
    if d == 1:
        return pl.ds(l0 if isinstance(l0, int) else pl.multiple_of(l0, KEY_BLOCK), n)
    return pl.ds(r + d * l0, n, stride=d)


def _dil_blocks(S, visit):
    B = KEY_BLOCK
    group = 16
    for b, d in enumerate(DILATIONS):
        nb = S // d // B
        if nb == 1:
            g = math.gcd(d, group)

            def trip(t, carry, b=b, d=d, g=g):
                visit([(b, d, t * g + u, 0, True) for u in range(g)])
                return carry

            lax.fori_loop(0, d // g, trip, 0)
        elif d == 1:
            visit([(b, d, 0, 0, True)])
            g = max(k for k in range(1, group + 2) if (nb - 1) % k == 0)

            def trip(t, carry, b=b, d=d, g=g):
                visit([(b, d, 0, (1 + t * g + u) * B, False) for u in range(g)])
                return carry

            lax.fori_loop(0, (nb - 1) // g, trip, 0)
        else:
            g = math.gcd(d, max(group // nb, 1))

            def trip(t, carry, b=b, d=d, nb=nb, g=g):
                visit([(b, d, t * g + u, n * B, n == 0) for u in range(g) for n in range(nb)])
                return carry

            lax.fori_loop(0, d // g, trip, 0)


def _dil_mask(first):
    B = KEY_BLOCK
    nk = B if first else 2 * B
    iq = lax.broadcasted_iota(jnp.int32, (B, nk), 0)
    ik = lax.broadcasted_iota(jnp.int32, (B, nk), 1)
    return (ik <= iq) if first else ((ik >= iq) & (ik <= iq + B))


def _dil_fwd(proj, cos2, sin_signed, gain, col0, n_heads, name):
    S = proj.shape[0]
    H, B = n_heads, KEY_BLOCK
    scale = HEAD_DIM ** -0.5
    rc = _tile(S, 256, 8)

    def body(q_ref, k_ref, v_ref, c_ref, s_ref, g_ref, o_ref, l_ref, mx_ref, qr, kr, *per_branch):
        ob, lb = per_branch[:len(DILATIONS)], per_branch[len(DILATIONS):]

        def rope_rows(t, carry):
            rows = pl.ds(pl.multiple_of(t * rc, rc), rc)
            qr[rows, :] = _rope(q_ref[rows, :], c_ref[rows, :], s_ref[rows, :])
            kr[rows, :] = _rope(k_ref[rows, :], c_ref[rows, :], s_ref[rows, :])
            return carry

        lax.fori_loop(0, S // rc, rope_rows, 0)

        def visit(blocks):
            scores = []
            for b, d, r, l0, first in blocks:
                qrows = _dil_rows(d, r, l0, B)
                krows = qrows if first else _dil_rows(d, r, l0 - B, 2 * B)
                s = _dot(qr[qrows, :].astype(BF16), kr[krows, :].astype(BF16), NT) * scale
                scores.append((b, qrows, krows, jnp.where(_dil_mask(first), s, NEG)))
            weights = []
            for b, qrows, krows, s in scores:
                m = jnp.max(s, axis=1, keepdims=True)
                p = jnp.exp(s - m)
                den = jnp.sum(p, axis=1, keepdims=True)
                lb[b][qrows, :] = jnp.broadcast_to(m + jnp.log(den), (B, LANES))
                weights.append((b, qrows, krows, p.astype(BF16), den))
            for b, qrows, krows, p, den in weights:
                ob[b][qrows, :] = _dot(p, v_ref[krows, :].astype(BF16), NN) / den

        _dil_blocks(S, visit)

        def combine(t, carry):
            rows = pl.ds(pl.multiple_of(t * rc, rc), rc)
            l0, l1, l2 = lb[0][rows, :], lb[1][rows, :], lb[2][rows, :]
            m = jnp.maximum(jnp.maximum(l0, l1), l2)
            w0, w1, w2 = jnp.exp(l0 - m), jnp.exp(l1 - m), jnp.exp(l2 - m)
            den = w0 + w1 + w2
            o = (w0 * ob[0][rows, :] + w1 * ob[1][rows, :] + w2 * ob[2][rows, :]) / den
            o_ref[rows, :] = o
            l_ref[rows, :] = m + jnp.log(den)
            mx_ref[rows, :] = _head_out(o, g_ref[...]).astype(BF16)
            return carry

        lax.fori_loop(0, S // rc, combine, 0)

    def col(k):
        return pl.BlockSpec((S, HEAD_DIM), lambda h: (0, col0 + k * H + h))

    tab = pl.BlockSpec((S, HEAD_DIM), lambda h: (0, 0))
    out = pl.BlockSpec((S, HEAD_DIM), lambda h: (0, h))
    W = H * HEAD_DIM
    return _pcall(
        body, grid=(H,),
        in_specs=[col(0), col(1), col(2), tab, tab, pl.BlockSpec((1, HEAD_DIM), lambda h: (0, h))],
        out_specs=[out, out, out],
        out_shape=[jax.ShapeDtypeStruct((S, W), F32), jax.ShapeDtypeStruct((S, W), F32), jax.ShapeDtypeStruct((S, W), BF16)],
        scratch_shapes=[pltpu.VMEM((S, HEAD_DIM), F32)] * (2 + 2 * len(DILATIONS)),
        compiler_params=_params("parallel"), name=name)(proj, proj, proj, cos2, sin_signed, gain)


def _dil_bwd(proj, cos2, sin_signed, gain, o_raw, lse, dmixed, dm_col0, col0, n_heads, name):
    S = proj.shape[0]
    H, B = n_heads, KEY_BLOCK
    scale = HEAD_DIM ** -0.5
    rc = _tile(S, 256, 8)

    def body(q_ref, k_ref, v_ref, c_ref, s_ref, g_ref, o_ref, l_ref, dm_ref, dq_ref, dk_ref, dv_ref, dg_ref,
             qr, kr, dos, dsum, dqr, dkr, dvv):
        dg_ref[...] = jnp.zeros_like(dg_ref)

        def prep(t, carry):
            rows = pl.ds(pl.multiple_of(t * rc, rc), rc)
            qr[rows, :] = _rope(q_ref[rows, :], c_ref[rows, :], s_ref[rows, :])
            kr[rows, :] = _rope(k_ref[rows, :], c_ref[rows, :], s_ref[rows, :])
            o, dm = o_ref[rows, :], dm_ref[rows, :]
            r = _rms_scale(o)
            do = _rms_bwd(dm * g_ref[...], o, r)
            dg_ref[...] += jnp.broadcast_to(jnp.sum(dm * o * r, axis=0, keepdims=True), dg_ref.shape)
            dos[rows, :] = do
            dsum[rows, :] = jnp.broadcast_to(jnp.sum(do * o, axis=1, keepdims=True), (rc, LANES))
            dqr[rows, :] = jnp.zeros((rc, HEAD_DIM), F32)
            dkr[rows, :] = jnp.zeros((rc, HEAD_DIM), F32)
            dvv[rows, :] = jnp.zeros((rc, HEAD_DIM), F32)
            return carry

        lax.fori_loop(0, S // rc, prep, 0)

        def visit(blocks):
            products = []
            for b, d, r, l0, first in blocks:
                qrows = _dil_rows(d, r, l0, B)
                krows = qrows if first else _dil_rows(d, r, l0 - B, 2 * B)
                qs, ks = qr[qrows, :].astype(BF16), kr[krows, :].astype(BF16)
                do = dos[qrows, :].astype(BF16)
                s = jnp.where(_dil_mask(first), _dot(qs, ks, NT) * scale, NEG)
                dp = _dot(do, v_ref[krows, :].astype(BF16), NT)
                products.append((qrows, krows, qs, ks, do, s, dp))
            cotangents = []
            for qrows, krows, qs, ks, do, s, dp in products:
                p = jnp.exp(s - l_ref[qrows, :][:, 0:1])
                ds = (p * (dp - dsum[qrows, :][:, 0:1]) * scale).astype(BF16)
                cotangents.append((qrows, krows, qs, ks, do, p.astype(BF16), ds))
            for qrows, krows, qs, ks, do, p, ds in cotangents:
                dqr[qrows, :] += _dot(ds, ks, NN)
                dkr[krows, :] += _dot(ds, qs, TN)
                dvv[krows, :] += _dot(p, do, TN)

        _dil_blocks(S, visit)

        def finish(t, carry):
            rows = pl.ds(pl.multiple_of(t * rc, rc), rc)
            c, s = c_ref[rows, :], s_ref[rows, :]
            dq, dk = dqr[rows, :], dkr[rows, :]
            dq_ref[rows, :] = (dq * c + pltpu.roll(dq * s, HEAD_DIM // 2, axis=1)).astype(BF16)
            dk_ref[rows, :] = (dk * c + pltpu.roll(dk * s, HEAD_DIM // 2, axis=1)).astype(BF16)
            dv_ref[rows, :] = dvv[rows, :].astype(BF16)
            return carry

        lax.fori_loop(0, S // rc, finish, 0)

    def col(k):
        return pl.BlockSpec((S, HEAD_DIM), lambda h: (0, col0 + k * H + h))

    tab = pl.BlockSpec((S, HEAD_DIM), lambda h: (0, 0))
    out = pl.BlockSpec((S, HEAD_DIM), lambda h: (0, h))
    W = H * HEAD_DIM
    big = pltpu.VMEM((S, HEAD_DIM), F32)
    return _pcall(
        body, grid=(H,),
        in_specs=[col(0), col(1), col(2), tab, tab, pl.BlockSpec((1, HEAD_DIM), lambda h: (0, h)), out, out,
                  pl.BlockSpec((S, HEAD_DIM), lambda h: (0, dm_col0 + h))],
        out_specs=[out, out, out, pl.BlockSpec((8, HEAD_DIM), lambda h: (0, h))],
        out_shape=[jax.ShapeDtypeStruct((S, W), BF16), jax.ShapeDtypeStruct((S, W), BF16),
                   jax.ShapeDtypeStruct((S, W), BF16), jax.ShapeDtypeStruct((8, W), F32)],
        scratch_shapes=[big, big, big, pltpu.VMEM((S, LANES), F32), big, big, big],
        compiler_params=_params("parallel"), name=name)(proj, proj, proj, cos2, sin_signed, gain, o_raw, lse, dmixed)


GELU_C = math.sqrt(2.0 / math.pi)
GELU_A = 0.044715
HALO = 16


def _shift_down(cur, halo, k):
    out = pltpu.roll(cur, k, axis=0)
    row = lax.broadcasted_iota(jnp.int32, cur.shape, 0)
    for t in range(k):
        out = jnp.where(row == t, halo[HALO - k + t:HALO - k + t + 1, :], out)
    return out


def _shift_up(cur, halo, k):
    n = cur.shape[0]
    out = pltpu.roll(cur, n - k, axis=0)
    row = lax.broadcasted_iota(jnp.int32, cur.shape, 0)
    for t in range(k):
        out = jnp.where(row == n - k + t, halo[t:t + 1, :], out)
    return out


def _conv3(cur, halo, cw):
    return _shift_down(cur, halo, 2) * cw[0:1, :] + _shift_down(cur, halo, 1) * cw[1:2, :] + cur * cw[2:3, :] + cw[3:4, :]


def _gelu_parts(x):
    t = jnp.tanh(GELU_C * (x + GELU_A * x * x * x))
    return 0.5 * x * (1.0 + t), t


def _geglu_specs(tm, tn, ncb):
    hb = tm // HALO

    def cur(off):
        return pl.BlockSpec((tm, tn), lambda j, i: (i, off + j))

    def prev(off):
        return pl.BlockSpec((HALO, tn), lambda j, i: (jnp.maximum(i * hb - 1, 0), off + j))

    def taps(off):
        return pl.BlockSpec((8, tn), lambda j, i: (0, off + j))

    return [cur(0), prev(0), cur(ncb), prev(ncb), taps(0), taps(ncb)]


def _geglu_fwd(u, cwb, name, tm=256, tn=1408):
    S, F2 = u.shape
    F = F2 // 2
    tm, tn = _tile(S, tm, HALO), _tile(F, tn)
    ncb = F // tn

    def body(g_ref, gp_ref, v_ref, vp_ref, cg_ref, cv_ref, y_ref):
        top = pl.program_id(1) > 0
        gp = jnp.where(top, gp_ref[...].astype(F32), 0.0)
        vp = jnp.where(top, vp_ref[...].astype(F32), 0.0)
        gc = _conv3(g_ref[...].astype(F32), gp, cg_ref[...])
        vc = _conv3(v_ref[...].astype(F32), vp, cv_ref[...])
        y_ref[...] = (_gelu_parts(gc)[0] * vc).astype(BF16)

    return _pcall(body, grid=(ncb, S // tm), in_specs=_geglu_specs(tm, tn, ncb),
                  out_specs=pl.BlockSpec((tm, tn), lambda j, i: (i, j)),
                  out_shape=jax.ShapeDtypeStruct((S, F), BF16),
                  compiler_params=_params("parallel", "parallel"), name=name)(u, u, u, u, cwb, cwb)


def _geglu_bwd(u, dy, cwb, name, tm=256, tn=512):
    S, F2 = u.shape
    F = F2 // 2
    tm, tn = _tile(S, tm, HALO), _tile(F, tn)
    ncb = F // tn

    def body(g_ref, gp_ref, v_ref, vp_ref, cg_ref, cv_ref, dy_ref, dc_ref, dwg_ref, dwv_ref):
        i = pl.program_id(1)

        @pl.when(i == 0)
        def _():
            dwg_ref[...] = jnp.zeros_like(dwg_ref)
            dwv_ref[...] = jnp.zeros_like(dwv_ref)

        top = i > 0
        g, v = g_ref[...].astype(F32), v_ref[...].astype(F32)
        gp = jnp.where(top, gp_ref[...].astype(F32), 0.0)
        vp = jnp.where(top, vp_ref[...].astype(F32), 0.0)
        gc = _conv3(g, gp, cg_ref[...])
        vc = _conv3(v, vp, cv_ref[...])
        act, t = _gelu_parts(gc)
        dact = 0.5 * (1.0 + t) + 0.5 * gc * (1.0 - t * t) * GELU_C * (1.0 + 3.0 * GELU_A * gc * gc)
        dyv = dy_ref[...].astype(F32)
        dgc = dyv * vc * dact
        dvc = dyv * act
        dc_ref[0] = dgc.astype(BF16)
        dc_ref[1] = dvc.astype(BF16)

        def taps(out_ref, dc, cur, halo):
            out_ref[0:1, :] += jnp.sum(dc * _shift_down(cur, halo, 2), axis=0, keepdims=True)
            out_ref[1:2, :] += jnp.sum(dc * _shift_down(cur, halo, 1), axis=0, keepdims=True)
            out_ref[2:3, :] += jnp.sum(dc * cur, axis=0, keepdims=True)
            out_ref[3:4, :] += jnp.sum(dc, axis=0, keepdims=True)

        taps(dwg_ref, dgc, g, gp)
        taps(dwv_ref, dvc, v, vp)

    return _pcall(body, grid=(ncb, S // tm),
                  in_specs=_geglu_specs(tm, tn, ncb) + [pl.BlockSpec((tm, tn), lambda j, i: (i, j))],
                  out_specs=[pl.BlockSpec((2, tm, tn), lambda j, i: (0, i, j)),
                             pl.BlockSpec((8, tn), lambda j, i: (0, j)), pl.BlockSpec((8, tn), lambda j, i: (0, j))],
                  out_shape=[jax.ShapeDtypeStruct((2, S, F), BF16), jax.ShapeDtypeStruct((8, F), F32),
                             jax.ShapeDtypeStruct((8, F), F32)],
                  compiler_params=_params("parallel", "arbitrary"), name=name)(u, u, u, u, cwb, cwb, dy)


def _conv_bwd(dc, cwb, name, tm=512, tn=1408):
    _, S, F = dc.shape
    tm, tn = _tile(S, tm, HALO), _tile(F, tn)
    ncb, nrb = F // tn, S // tm
    hb = tm // HALO

    def body(c_ref, n_ref, w_ref, du_ref):
        cur = c_ref[...].astype(F32)
        nxt = jnp.where(pl.program_id(2) < nrb - 1, n_ref[...].astype(F32), 0.0)
        w = w_ref[...]
        du = cur * w[2:3, :] + _shift_up(cur, nxt, 1) * w[1:2, :] + _shift_up(cur, nxt, 2) * w[0:1, :]
        du_ref[...] = du.astype(BF16)

    return _pcall(body, grid=(2, ncb, nrb),
                  in_specs=[pl.BlockSpec((None, tm, tn), lambda c, j, i: (c, i, j)),
                            pl.BlockSpec((None, HALO, tn), lambda c, j, i: (c, jnp.minimum((i + 1) * hb, S // HALO - 1), j)),
                            pl.BlockSpec((8, tn), lambda c, j, i: (0, c * ncb + j))],
                  out_specs=pl.BlockSpec((tm, tn), lambda c, j, i: (i, c * ncb + j)),
                  out_shape=jax.ShapeDtypeStruct((S, 2 * F), BF16),
                  compiler_params=_params("parallel", "parallel", "parallel"), name=name)(dc, dc, cwb)


def _adam_math(w, g, m, v):
    m = ADAM_B1 * m + (1.0 - ADAM_B1) * g
    v = ADAM_B2 * v + (1.0 - ADAM_B2) * (g * g)
    m_hat = m / (1.0 - ADAM_B1 ** ADAM_STEP)
    v_hat = v / (1.0 - ADAM_B2 ** ADAM_STEP)
    return -ADAM_LR * (m_hat / (jnp.sqrt(v_hat) + ADAM_EPS) + ADAM_WD * w), m, v


def _adamw(w, parts, m, v, name, tr=256):
    R, C = w.shape
    n, _, Cp = parts.shape
    tr = _tile(R, tr, 8)

    def body(w_ref, p_ref, m_ref, v_ref, g_out, d_out, m_out, v_out):
        g = p_ref[0, :, 0:C].astype(F32)
        for k in range(1, n):
            g = g + p_ref[k, :, 0:C].astype(F32)
        d, mn, vn = _adam_math(w_ref[...], g, m_ref[...], v_ref[...])
        g_out[...] = g
        d_out[...] = d
        m_out[...] = mn
        v_out[...] = vn

    spec = pl.BlockSpec((tr, C), lambda i: (i, 0))
    shape = jax.ShapeDtypeStruct((R, C), F32)
    return _pcall(body, grid=(R // tr,), in_specs=[spec, pl.BlockSpec((n, tr, Cp), lambda i: (0, i, 0)), spec, spec],
                  out_specs=[spec] * 4, out_shape=[shape] * 4, compiler_params=_params("parallel"), name=name)(w, parts, m, v)


def _adamw_chips(w, pair, parts, chip_ids, m, v, name, tr=256):
    R, C = w.shape
    Cp = pair.shape[2]
    by_columns = C == Cp and _tile(R, tr, 16) < 64
    tr, tc = (R, _tile(C, 256)) if by_columns else (_tile(R, tr, 16), C)

    def body(ids_ref, w_ref, own_ref, p1_ref, p2_ref, p3_ref, m_ref, v_ref, g_out, d_out, m_out, v_out):
        g = own_ref[:, 0:tc].astype(F32)
        for ref in (p1_ref, p2_ref, p3_ref):
            g = g + ref[:, 0:tc].astype(F32)
        d, mn, vn = _adam_math(w_ref[...], g, m_ref[...], v_ref[...])
        g_out[...] = g
        d_out[...] = d
        m_out[...] = mn
        v_out[...] = vn

    if by_columns:
        spec = pl.BlockSpec((tr, tc), lambda j, ids: (0, j))
    else:
        spec = pl.BlockSpec((tr, tc), lambda i, ids: (i, 0))

    def chip(k):
        if by_columns:
            return pl.BlockSpec((None, tr, tc), lambda j, ids: (ids[k], 0, j))
        return pl.BlockSpec((None, tr, Cp), lambda i, ids: (ids[k], i, 0))

    shape = jax.ShapeDtypeStruct((R, C), F32)
    grid_spec = pltpu.PrefetchScalarGridSpec(
        num_scalar_prefetch=1, grid=(C // tc if by_columns else R // tr,),
        in_specs=[spec, chip(0), chip(1), chip(2), chip(3), spec, spec], out_specs=[spec] * 4)
    return _pcall(body, grid_spec=grid_spec, out_shape=[shape] * 4, compiler_params=_params("parallel"),
                  name=name)(chip_ids, w, pair, parts, parts, parts, m, v)


def _place():
    return lax.axis_index("x"), lax.axis_index("y"), lax.axis_index("c")


def _other_chips(x, y):
    return [(1 - x, y), (x, 1 - y), (1 - x, 1 - y)]


IN_HBM = pl.BlockSpec(memory_space=pltpu.HBM)
SEM = pl.BlockSpec(memory_space=pltpu.SEMAPHORE)
EFFECT = pltpu.SideEffectType.DATAFLOW_SIDE_EFFECTING
TOKEN = jax.ShapeDtypeStruct((8, LANES), F32)
TOKEN_SPEC = pl.BlockSpec(memory_space=pltpu.VMEM)


def _in_hbm(a):
    return pltpu.with_memory_space_constraint(a, pltpu.HBM)


def _landing(shape):
    return _in_hbm(lax.empty(shape.shape, shape.dtype))


def _hbm_like(a):
    return pltpu.HBM(a.shape, a.dtype)


def _gather_start(landing, slots, after, name):
    na = len(landing)

    def body(*refs):
        land = refs[:na]
        send_sems, recv_sems = refs[na + 1], refs[na + 2]
        token = refs[-1]
        x, y, c = _place()
        for a in range(na):
            own = slots[a](land[a], x, y, c)
            for k, to in enumerate([(x, y, 1 - c)] + [(*chip, c) for chip in _other_chips(x, y)]):
                pltpu.make_async_remote_copy(
                    src_ref=own, dst_ref=own, send_sem=send_sems.at[4 * a + k],
                    recv_sem=recv_sems.at[4 * a + k], device_id=to, device_id_type=MESH).start()
        token[...] = jnp.zeros_like(token)

    sems = pltpu.SemaphoreType.DMA((4 * na,))
    outs = _pcall(
        body, in_specs=[IN_HBM] * na + [HBM],
        out_specs=[SEM, SEM] + [IN_HBM] * na + [TOKEN_SPEC],
        out_shape=[sems, sems] + [_hbm_like(s) for s in landing] + [TOKEN],
        input_output_aliases={a: 2 + a for a in range(na)},
        compiler_params=pltpu.CompilerParams(has_side_effects=EFFECT), name=name,
    )(*[_in_hbm(s) for s in landing], after)
    return outs[0], outs[1], outs[2:2 + na], outs[-1]


def _gather_forward(gathered, send_sems, recv_sems, slots, after, name):
    na = len(gathered)

    def body(*refs):
        gath = refs[:na]
        send1, recv1 = refs[na], refs[na + 1]
        fsend, frecv = refs[na + 3], refs[na + 4]
        token = refs[-1]
        x, y, c = _place()
        chips = _other_chips(x, y)
        for a in range(na):
            for k, peer in enumerate([(x, y, 1 - c)] + [(*chip, c) for chip in chips]):
                arrival = pltpu.make_async_remote_copy(
                    src_ref=slots[a](gath[a], x, y, c), dst_ref=slots[a](gath[a], *peer), send_sem=send1.at[4 * a + k],
                    recv_sem=recv1.at[4 * a + k], device_id=peer, device_id_type=MESH)
                arrival.wait_send()
                arrival.wait_recv()
        for a in range(na):
            for j, chip in enumerate(chips):
                view = slots[a](gath[a], *chip, c)
                pltpu.make_async_remote_copy(
                    src_ref=view, dst_ref=view, send_sem=fsend.at[3 * a + j], recv_sem=frecv.at[3 * a + j],
                    device_id=(x, y, 1 - c), device_id_type=MESH).start()
        token[...] = jnp.zeros_like(token)

    sems = pltpu.SemaphoreType.DMA((3 * na,))
    outs = _pcall(
        body, in_specs=[IN_HBM] * na + [SEM, SEM, HBM],
        out_specs=[SEM, SEM] + [IN_HBM] * na + [TOKEN_SPEC],
        out_shape=[sems, sems] + [_hbm_like(g) for g in gathered] + [TOKEN],
        input_output_aliases={a: 2 + a for a in range(na)},
        compiler_params=pltpu.CompilerParams(has_side_effects=EFFECT), name=name,
    )(*gathered, send_sems, recv_sems, after)
    return outs[0], outs[1], outs[2:2 + na], outs[-1]


def _gather_finish(gathered, fsend, frecv, slots, after, name):
    na = len(gathered)

    def body(*refs):
        gath, fs, fr = refs[:na], refs[na], refs[na + 1]
        x, y, c = _place()
        for a in range(na):
            for j, chip in enumerate(_other_chips(x, y)):
                passed = pltpu.make_async_remote_copy(
                    src_ref=slots[a](gath[a], *chip, c), dst_ref=slots[a](gath[a], *chip, 1 - c),
                    send_sem=fs.at[3 * a + j], recv_sem=fr.at[3 * a + j], device_id=(x, y, 1 - c), device_id_type=MESH)
                passed.wait_send()
                passed.wait_recv()

    outs = _pcall(
        body, in_specs=[IN_HBM] * na + [SEM, SEM, HBM], out_specs=[IN_HBM] * na,
        out_shape=[_hbm_like(g) for g in gathered], input_output_aliases={a: a for a in range(na)},
        compiler_params=pltpu.CompilerParams(has_side_effects=EFFECT), name=name,
    )(*gathered, fsend, frecv, after)
    return list(outs)


def _pair_copy(view, src, land, send_sems, recv_sems, chip):
    x, y, c = _place()
    return pltpu.make_async_remote_copy(
        src_ref=view(src, chip, 1 - c), dst_ref=land.at[chip], send_sem=send_sems.at[chip], recv_sem=recv_sems.at[chip],
        device_id=(x, y, 1 - c), device_id_type=MESH)


def _pair_start(grad, view, block, after, name):
    def body(src, land, after_ref, send_sems, recv_sems, src_thru, land_thru, token):
        for chip in range(N_CHIP):
            _pair_copy(view, src, land, send_sems, recv_sems, chip).start()
        token[...] = jnp.zeros_like(token)

    sems = pltpu.SemaphoreType.DMA((N_CHIP,))
    land = jax.ShapeDtypeStruct((N_CHIP, *block), BF16)
    return _pcall(
        body, in_specs=[IN_HBM, IN_HBM, HBM], out_specs=[SEM, SEM, IN_HBM, IN_HBM, TOKEN_SPEC],
        out_shape=[sems, sems, _hbm_like(grad), _hbm_like(land), TOKEN], input_output_aliases={0: 2, 1: 3},
        compiler_params=pltpu.CompilerParams(has_side_effects=EFFECT), name=name,
    )(_in_hbm(grad), _landing(land), after)


def _pair_wait(grad, recv, send_sems, recv_sems, view, after, name):
    def body(src, land, send, recv_s, after_ref, src_thru, land_thru):
        for chip in range(N_CHIP):
            copy = _pair_copy(view, src, land, send, recv_s, chip)
            copy.wait_send()
            copy.wait_recv()

    return _pcall(
        body, in_specs=[IN_HBM, IN_HBM, SEM, SEM, HBM], out_specs=[IN_HBM, IN_HBM],
        out_shape=[_hbm_like(grad), _hbm_like(recv)], input_output_aliases={0: 0, 1: 1},
        compiler_params=pltpu.CompilerParams(has_side_effects=EFFECT), name=name,
    )(grad, recv, send_sems, recv_sems, after)


def _chip_start(pair, after, name):
    def body(src, land, after_ref, send_sems, recv_sems, src_thru, land_thru, token):
        x, y, c = _place()
        for j, (px, py) in enumerate(_other_chips(x, y)):
            pltpu.make_async_remote_copy(
                src_ref=src.at[2 * px + py], dst_ref=land.at[2 * x + y], send_sem=send_sems.at[j], recv_sem=recv_sems.at[j],
                device_id=(px, py, c), device_id_type=MESH).start()
        token[...] = jnp.zeros_like(token)

    sems = pltpu.SemaphoreType.DMA((3,))
    return _pcall(
        body, in_specs=[IN_HBM, IN_HBM, HBM], out_specs=[SEM, SEM, IN_HBM, IN_HBM, TOKEN_SPEC],
        out_shape=[sems, sems, _hbm_like(pair), _hbm_like(pair), TOKEN], input_output_aliases={0: 2, 1: 3},
        compiler_params=pltpu.CompilerParams(has_side_effects=EFFECT), name=name,
    )(_in_hbm(pair), _landing(pair), after)


def _chip_wait(pair, parts, send_sems, recv_sems, after, name):
    def body(src, land, send, recv, after_ref, src_thru, land_thru):
        x, y, c = _place()
        for j, (px, py) in enumerate(_other_chips(x, y)):
            copy = pltpu.make_async_remote_copy(
                src_ref=src.at[2 * px + py], dst_ref=land.at[2 * px + py], send_sem=send.at[j], recv_sem=recv.at[j],
                device_id=(px, py, c), device_id_type=MESH)
            copy.wait_send()
            copy.wait_recv()

    return _pcall(
        body, in_specs=[IN_HBM, IN_HBM, SEM, SEM, HBM], out_specs=[IN_HBM, IN_HBM],
        out_shape=[_hbm_like(pair), _hbm_like(parts)], input_output_aliases={0: 0, 1: 1},
        compiler_params=pltpu.CompilerParams(has_side_effects=EFFECT), name=name,
    )(pair, parts, send_sems, recv_sems, after)


def _pair_add(core, grad, recv, block, grad_spec, name):
    _, R, C = recv.shape
    tr = block

    def body(c_ref, g_ref, r_ref, o_ref):
        o_ref[...] = (g_ref[...].astype(F32) + r_ref[...].astype(F32)).astype(BF16)

    grid_spec = pltpu.PrefetchScalarGridSpec(
        num_scalar_prefetch=1, grid=(N_CHIP, R // tr),
        in_specs=[grad_spec, pl.BlockSpec((None, tr, C), lambda k, i, c: (k, i, 0))],
        out_specs=pl.BlockSpec((None, tr, C), lambda k, i, c: (k, i, 0)))
    return _pcall(body, grid_spec=grid_spec, out_shape=jax.ShapeDtypeStruct(recv.shape, BF16),
                  compiler_params=_params("parallel", "parallel"), name=name)(core, grad, recv)


def _small_copies(gath, send_sems, recv_sems):
    x, y, c = _place()
    peers = [(x, y, 1 - c)] + [(px, py, pc) for px, py in _other_chips(x, y) for pc in (c, 1 - c)]
    pairs = []
    for a, ref in enumerate(gath):
        mine = ref.at[4 * x + 2 * y + c]
        for k, (px, py, pc) in enumerate(peers):
            sems = dict(send_sem=send_sems.at[7 * a + k], recv_sem=recv_sems.at[7 * a + k], device_id=(px, py, pc),
                        device_id_type=MESH)
            pairs.append((pltpu.make_async_remote_copy(src_ref=mine, dst_ref=mine, **sems),
                          pltpu.make_async_remote_copy(src_ref=mine, dst_ref=ref.at[4 * px + 2 * py + pc], **sems)))
    return pairs


def _small_start(landing, after, name):
    na = len(landing)

    def body(*refs):
        for send, _ in _small_copies(refs[:na], refs[na + 1], refs[na + 2]):
            send.start()
        refs[-1][...] = jnp.zeros_like(refs[-1])

    sems = pltpu.SemaphoreType.DMA((7 * na,))
    outs = _pcall(
        body, in_specs=[IN_HBM] * na + [HBM], out_specs=[SEM, SEM] + [IN_HBM] * na + [TOKEN_SPEC],
        out_shape=[sems, sems] + [_hbm_like(s) for s in landing] + [TOKEN],
        input_output_aliases={a: 2 + a for a in range(na)},
        compiler_params=pltpu.CompilerParams(has_side_effects=EFFECT), name=name,
    )(*[_in_hbm(s) for s in landing], after)
    return outs[0], outs[1], outs[2:2 + na], outs[-1]


def _small_wait(gathered, send_sems, recv_sems, after, name):
    na = len(gathered)

    def body(*refs):
        for send, arrival in _small_copies(refs[:na], refs[na], refs[na + 1]):
            send.wait_send()
            arrival.wait_recv()

    return list(_pcall(
        body, in_specs=[IN_HBM] * na + [SEM, SEM, HBM], out_specs=[IN_HBM] * na,
        out_shape=[_hbm_like(g) for g in gathered], input_output_aliases={a: a for a in range(na)},
        compiler_params=pltpu.CompilerParams(has_side_effects=EFFECT), name=name,
    )(*gathered, send_sems, recv_sems, after))


def _small_finish(gathered, params, name):
    na, npar = len(gathered), len(params)

    def body(*refs):
        g_refs, wmv = refs[:na], refs[na:na + 3 * npar]
        o_sums, o_params = refs[na + 3 * npar:2 * na + 3 * npar], refs[2 * na + 3 * npar:]
        sums = []
        for a in range(na):
            acc = g_refs[a][0]
            for k in range(1, N_DEV):
                acc = acc + g_refs[a][k]
            o_sums[a][...] = acc
            sums.append(acc)
        for j, (a, row, _, _, _) in enumerate(params):
            g = sums[a][row:row + 1, :]
            d, mn, vn = _adam_math(wmv[3 * j][...], g, wmv[3 * j + 1][...], wmv[3 * j + 2][...])
            for out, val in zip(o_params[4 * j:4 * j + 4], (g, d, mn, vn)):
                out[...] = val

    vm = pl.BlockSpec(memory_space=pltpu.VMEM)
    flat = [t for p in params for t in p[2:]]
    out_shape = [jax.ShapeDtypeStruct(g.shape[1:], F32) for g in gathered]
    out_shape += [jax.ShapeDtypeStruct(p[2].shape, F32) for p in params for _ in range(4)]
    outs = _pcall(body, in_specs=[vm] * (na + 3 * npar), out_specs=[vm] * len(out_shape), out_shape=out_shape,
                  name=name)(*gathered, *flat)
    return outs[:na], [outs[na + 4 * j:na + 4 * j + 4] for j in range(npar)]


def _local_step(x, tgt, gains, weights):
    g_pre_mix, g_post_mix, g_pre_ffn, g_post_ffn, g_sb, g_dil = gains
    S, D = x.shape
    hs = g_sb.shape[1] // HEAD_DIM
    hd = g_dil.shape[1] // HEAD_DIM
    cos2, sin_signed = _rope_tables(S)

    h1 = _rms_fwd(x, g_pre_mix + weights.start(), "rms_in")
    w_in_g = weights.w_in(h1)
    proj = _mm_nn(h1, w_in_g, F32, "proj", tn=768)
    o_sb, ct_sb, mx_sb = _sb_fwd(proj, g_sb, hs, "sb_fwd")
    o_dl, lse_dl, mx_dl = _dil_fwd(proj, cos2, sin_signed, g_dil + weights.forward_out(o_sb), 3 * hs, hd, "dil_fwd")
    w_out_g, dep = weights.w_out(o_dl)
    mixed = jnp.concatenate([mx_sb, mx_dl], axis=1)
    mix = _mm_nn(mixed, w_out_g, F32, "mix_out", tn=1024)
    x2, h2 = _mid_fwd(x, mix, g_post_mix + dep, g_pre_ffn, "mid_fwd")
    w_up_g, cwb = weights.w_up(h2)
    u = _mm_nn(h2, w_up_g, BF16, "ffn_up", b_transposed=True)
    y = _geglu_fwd(u, cwb + weights.forward_down(u), "geglu_fwd")
    w_down_g = weights.w_down(y)
    f = _mm_nn(y, w_down_g, F32, "ffn_down", tn=1024, tk=1408)

    dy, df, dg_post_ffn, loss = _loss_bwd(x2, f, tgt, g_post_ffn, "loss_bwd")
    dyv = _mm_nt(df, w_down_g, BF16, "d_y", tn=1408)
    dw_down = _mm_tn(y, df, D, BF16, "dw_down", tm=1408, tn=1024)
    dc, dcw_g, dcw_v = _geglu_bwd(u, dyv, cwb + weights.grad("w_down", dw_down), "geglu_bwd")
    du = _conv_bwd(dc, cwb + weights.grad_reduce("w_down", dc), "conv_bwd")
    dh2 = _mm_nt(du, w_up_g, F32, "d_h2", tk=1408, b_transposed=True)
    dw_up = _mm_tn(du, h2, D, BF16, "dw_up", tm=1408, tn=1024)
    dx2, dmix, dg_pre_ffn, dg_post_mix = _mid_bwd(
        dy, dh2, x2, mix, g_pre_ffn + weights.grad("w_up", dw_up), g_post_mix, "mid_bwd")
    dmixed = _mm_nt(dmix, w_out_g, F32, "d_mixed", after=jnp.reshape(weights.grad_reduce("w_up", dmix), (1, 1)))
    dw_out = _mm_tn(mixed, dmix, D, BF16, "dw_out", tn=1024)
    dq_s, dk_s, dv_s, dg_sb = _sb_bwd(proj, g_sb + weights.grad("w_out", dw_out), o_sb, ct_sb, dmixed, 0, hs, "sb_bwd")
    dq_d, dk_d, dv_d, dg_dil = _dil_bwd(proj, cos2, sin_signed, g_dil + weights.grad_reduce("w_out", dq_s), o_dl, lse_dl,
                                        dmixed, hs, 3 * hs, hd, "dil_bwd")
    dproj = jnp.concatenate([dq_s, dk_s, dv_s, dq_d, dk_d, dv_d], axis=1)
    dw_in = _mm_tn(h1, dproj, w_in_g.shape[2], BF16, "dw_in", tn=768)
    weights.grad("w_in", dw_in)
    dep = weights.grad_reduce("w_in", dproj)
    dh1 = _mm_nt(dproj, w_in_g, F32, "d_h1", tk=768, after=jnp.reshape(dep, (1, 1)))
    grad_x, dg_pre_mix = _first_bwd(dx2, dh1, x, g_pre_mix, "first_bwd")
    small = (dg_pre_mix, dg_post_mix, dg_pre_ffn, dg_post_ffn, dg_sb[0:1], dg_dil[0:1], jnp.concatenate([dcw_g, dcw_v], axis=1))
    weights.small(small, loss)
    return loss, grad_x, small


def _pad_cols(a, to):
    return jnp.pad(a, ((0, 0), (0, to - a.shape[1])))


def kernel(x, pre_mix_gain, post_mix_gain, pre_ffn_gain, post_ffn_gain, w_in, sb_out_gain, dil_out_gain, w_out, w_up, conv_w, conv_b, w_down, loss_target, m_pre_mix_gain, m_post_mix_gain, m_pre_ffn_gain, m_post_ffn_gain, m_w_in, m_sb_out_gain, m_dil_out_gain, m_w_out, m_w_up, m_conv_w, m_conv_b, m_w_down, v_pre_mix_gain, v_post_mix_gain, v_pre_ffn_gain, v_post_ffn_gain, v_w_in, v_sb_out_gain, v_dil_out_gain, v_w_out, v_w_up, v_conv_w, v_conv_b, v_w_down):
    xb, tb = x[0], loss_target[0]
    S, D = xb.shape
    w_in, w_out, w_up, w_down, conv_w = w_in[0], w_out[0], w_up[0], w_down[0], conv_w[0]
    n_in, e_rows = w_in.shape[1], w_out.shape[0]
    cu, half = w_up.shape[1], w_down.shape[0]
    assert cu == 2 * half and half % 16 == 0
    cup = -(-cu // LANES) * LANES
    fp = N_CHIP * cup
    px, py, pc = _place()
    me = 4 * px + 2 * py + pc
    core = jnp.reshape(pc, (1,)).astype(jnp.int32)

    w_up_t, m_up_t, v_up_t = (jnp.swapaxes(t, 0, 1) for t in (w_up, m_w_up[0], v_w_up[0]))

    def by_dev(ref, qx, qy, qc):
        return ref.at[4 * qx + 2 * qy + qc]

    def down_slot(ref, qx, qy, qc):
        return ref.at[2 * qx + qy, pl.ds(qc * half, half)]

    def by_pair(ref, chip, k):
        return ref.at[chip, k]

    def down_pair(ref, chip, k):
        return ref.at[chip, pl.ds(k * half, half)]

    def pair_spec(tr, cols):
        return pl.BlockSpec((None, None, tr, cols), lambda k, i, c: (k, c[0], i, 0))

    tr_in, tr_up = _tile(D, 512, 16), _tile(cup, 256, 16)
    grad_plan = {
        "w_in": ((N_CHIP, 2, D, n_in), by_pair, (D, n_in), tr_in, pair_spec(tr_in, n_in)),
        "w_out": ((N_CHIP, 2, e_rows, D), by_pair, (e_rows, D), e_rows, pair_spec(e_rows, D)),
        "w_up": ((N_CHIP, 2, cup, D), by_pair, (cup, D), tr_up, pair_spec(tr_up, D)),
        "w_down": ((N_CHIP, cup, D), down_pair, (half, D), half,
                   pl.BlockSpec((None, half, D), lambda k, i, c: (k, c[0], 0))),
    }

    class Exchanges:
        def __init__(self):
            self.in_flight = {}

        def start(self):
            def own_slot(shard):
                return lax.dynamic_update_index_in_dim(lax.empty((N_DEV, *shard.shape), shard.dtype), shard, me, 0)

            self.g_in = _gather_start([own_slot(w_in.astype(BF16))], [by_dev], core, "gather_in_start")
            zero = self.g_in[3][0, 0]
            self.g_out = _gather_start([own_slot((w_out + zero).astype(BF16))], [by_dev], self.g_in[3], "gather_out_start")
            up = jnp.pad(w_up_t + zero, ((0, cup - cu), (0, 0))).astype(BF16)
            taps = jnp.pad(conv_w + zero, ((0, 8 - conv_w.shape[0]), (0, cup - cu)))
            self.g_up = _gather_start([own_slot(up), own_slot(taps)], [by_dev, by_dev], self.g_out[3], "gather_up_start")
            down = lax.dynamic_update_slice(jnp.zeros((N_CHIP, cup, D), BF16), (w_down + zero).astype(BF16)[None],
                                            (2 * px + py, pc * half, 0))
            self.g_down = _gather_start([down], [down_slot], self.g_up[3], "gather_down_start")
            return self.g_down[3][0, 0]

        def w_in(self, after):
            send, recv, gath, _ = self.g_in
            fsend, frecv, gath, token = _gather_forward(gath, send, recv, [by_dev], after, "gather_in_forward")
            return _gather_finish(gath, fsend, frecv, [by_dev], token, "gather_in_finish")[0]

        def forward_out(self, after):
            send, recv, gath, _ = self.g_out
            self.p_out = _gather_forward(gath, send, recv, [by_dev], after, "gather_out_forward")
            return self.p_out[3][0, 0]

        def w_out(self, after):
            fsend, frecv, gath, _ = self.p_out
            w_out_g = _gather_finish(gath, fsend, frecv, [by_dev], after, "gather_out_finish")[0]
            send, recv, gath, _ = self.g_up
            self.p_up = _gather_forward(gath, send, recv, [by_dev, by_dev], w_out_g, "gather_up_forward")
            return w_out_g.reshape(1, N_DEV * e_rows, D), self.p_up[3][0, 0]

        def w_up(self, after):
            fsend, frecv, gath, _ = self.p_up
            w_up_g, cw_g = _gather_finish(gath, fsend, frecv, [by_dev, by_dev], after, "gather_up_finish")
            cb = _pad_cols(conv_b.reshape(N_DEV, cu), cup).reshape(1, 2 * fp)
            cw_full = jnp.transpose(cw_g[:, :3, :], (1, 0, 2)).reshape(3, 2 * fp)
            cwb = jnp.concatenate([cw_full, cb, jnp.zeros((4, 2 * fp), F32)], axis=0)
            return w_up_g, cwb

        def forward_down(self, after):
            send, recv, gath, _ = self.g_down
            self.p_down = _gather_forward(gath, send, recv, [down_slot], after, "gather_down_forward")
            return self.p_down[3][0, 0]

        def w_down(self, after):
            fsend, frecv, gath, _ = self.p_down
            return _gather_finish(gath, fsend, frecv, [down_slot], after, "gather_down_finish")[0].reshape(1, fp, D)

        def small(self, small, loss):
            d_pre_mix, d_post_mix, d_pre_ffn, d_post_ffn, d_sb, d_dil, d_conv = small

            def rows_of(*vectors):
                n = vectors[0].shape[1]
                row = lax.broadcasted_iota(jnp.int32, (8, n), 0)
                out = jnp.zeros((8, n), F32)
                for k, vec in enumerate(vectors):
                    out = jnp.where(row == k, vec, out)
                return out

            parts = [rows_of(d_pre_mix, d_post_mix, d_pre_ffn, d_post_ffn, jnp.broadcast_to(loss[:, :1], (1, D))),
                     rows_of(d_sb, d_dil), d_conv]
            landing = [lax.dynamic_update_index_in_dim(lax.empty((N_DEV, *p.shape), F32), p, me, 0) for p in parts]
            self.small_flight = _small_start(landing, parts[0], "small_start")

        def small_sums(self, after):
            send, recv, gath, _ = self.small_flight
            gath = _small_wait(gath, send, recv, after, "small_wait")
            params = [(0, 0, pre_mix_gain, m_pre_mix_gain, v_pre_mix_gain), (0, 1, post_mix_gain, m_post_mix_gain, v_post_mix_gain),
                      (0, 2, pre_ffn_gain, m_pre_ffn_gain, v_pre_ffn_gain), (0, 3, post_ffn_gain, m_post_ffn_gain, v_post_ffn_gain),
                      (1, 0, sb_out_gain, m_sb_out_gain, v_sb_out_gain), (1, 1, dil_out_gain, m_dil_out_gain, v_dil_out_gain)]
            (gains_sum, _, conv_sum), gain_steps = _small_finish(gath, params, "small_finish")
            return gains_sum[4, 0], conv_sum, gain_steps

        def grad(self, name, dw):
            view_shape, view, block, tr, spec = grad_plan[name]
            send, recv_sems, dw, recv, token = _pair_start(dw.reshape(view_shape), view, block, core, "pair_start_" + name)
            self.in_flight[name] = (dw, recv, send, recv_sems)
            return token[0, 0]

        def grad_reduce(self, name, after):
            _, view, _, tr, spec = grad_plan[name]
            dw, recv = _pair_wait(*self.in_flight[name], view, after, "pair_wait_" + name)
            pair = _pair_add(core, dw, recv, tr, spec, "pair_add_" + name)
            send, recv_sems, pair, parts, token = _chip_start(pair, recv, "chip_start_" + name)
            self.in_flight[name] = (pair, parts, send, recv_sems)
            self.last_token = token
            return token[0, 0]

        def grad_parts(self, name, after):
            return _chip_wait(*self.in_flight[name], after, "chip_wait_" + name)

    exchanges = Exchanges()
    gains = (pre_mix_gain, post_mix_gain, pre_ffn_gain, post_ffn_gain, sb_out_gain, dil_out_gain)
    loss, grad_x, small = _local_step(xb, tb, gains, exchanges)

    def small_adam(w, g, m, v, name):
        one = w.shape[0] == 1
        if one:
            w, g, m, v = (jnp.broadcast_to(t, (8, t.shape[1])) for t in (w, g, m, v))
        outs = _adamw(w, g[None], m, v, name)
        return [o[0:1] for o in outs] if one else outs

    chip_ids = jnp.stack([2 * px + py, 2 * (1 - px) + py, 2 * px + 1 - py, 2 * (1 - px) + 1 - py]).astype(jnp.int32)
    out_w_down = _adamw_chips(w_down, *exchanges.grad_parts("w_down", exchanges.small_flight[3]), chip_ids, m_w_down[0], v_w_down[0], "adam_w_down")
    out_up_t = _adamw_chips(w_up_t, *exchanges.grad_parts("w_up", out_w_down[1]), chip_ids, m_up_t, v_up_t, "adam_w_up")
    out_w_up = [jnp.swapaxes(o, 0, 1) for o in out_up_t]
    out_w_out = _adamw_chips(w_out, *exchanges.grad_parts("w_out", out_up_t[1]), chip_ids, m_w_out[0], v_w_out[0], "adam_w_out")
    loss_out, g_conv, gain_steps = exchanges.small_sums(out_w_out[1])
    out_pre_mix, out_post_mix, out_pre_ffn, out_post_ffn, out_sb, out_dil = gain_steps
    g_conv_b = g_conv[3].reshape(N_DEV, cup)[:, :cu].reshape(1, N_DEV * cu)
    g_conv_w = lax.dynamic_index_in_dim(g_conv[0:3].reshape(3, N_DEV, cup), me, axis=1, keepdims=False)[:, :cu]
    out_conv_b = small_adam(conv_b, g_conv_b, m_conv_b, v_conv_b, "adam_conv_b")
    cw8 = [jnp.pad(t, ((0, 5), (0, 0))) for t in (conv_w, g_conv_w, m_conv_w[0], v_conv_w[0])]
    out_conv_w = [o[0:3] for o in _adamw(cw8[0], cw8[1][None], cw8[2], cw8[3], "adam_conv_w")]
    out_w_in = _adamw_chips(w_in, *exchanges.grad_parts("w_in", out_conv_w[1]), chip_ids, m_w_in[0], v_w_in[0], "adam_w_in")

    order = [out_pre_mix, out_post_mix, out_pre_ffn, out_post_ffn, [o[None] for o in out_w_in], out_sb, out_dil,
             [o[None] for o in out_w_out], [o[None] for o in out_w_up], [o[None] for o in out_conv_w], out_conv_b,
             [o[None] for o in out_w_down]]
    outs = [loss_out, grad_x[None]]
    for k in range(4):
        outs += [o[k] for o in order]
    return tuple(outs)
```

```python
import functools
import math

import jax
import jax.numpy as jnp
from jax import lax
from jax.experimental import pallas as pl
from jax.experimental.pallas import tpu as pltpu

F32 = jnp.float32
BF16 = jnp.bfloat16
HEAD_DIM = 128
LANES = 128
KEY_BLOCK = 128
DILATIONS = (1, 4, 16)
RMS_EPS = 1e-6
ROPE_THETA = 10000.0
NEG = -1e30
ADAM_LR, ADAM_B1, ADAM_B2, ADAM_EPS, ADAM_WD, ADAM_STEP = 0.001, 0.9, 0.999, 1e-08, 0.01, 10
MESH = pl.DeviceIdType.MESH
N_DEV = 8
N_CHIP = 4
HBM = pl.BlockSpec(memory_space=pl.ANY)
VMEM_LIMIT = 56 * 1024 * 1024

_pcall = pl.pallas_call


def _tile(n, pref, mult=LANES):
    best = None
    t = mult
    while t <= min(n, pref):
        if n % t == 0:
            best = t
        t += mult
    return n if best is None else best


def _params(*sem):
    return pltpu.CompilerParams(dimension_semantics=sem, vmem_limit_bytes=VMEM_LIMIT)


def _dot(a, b, dims):
    return lax.dot_general(a, b, (dims, ((), ())), preferred_element_type=F32)


NN = ((1,), (0,))
NT = ((1,), (1,))
TN = ((0,), (0,))


def _mm_body(dims, nk, tile):
    if nk == 1:
        def single(a_ref, b_ref, o_ref):
            o_ref[...] = _dot(a_ref[...].astype(BF16), b_ref[...].astype(BF16), dims).astype(o_ref.dtype)

        return single, []

    def body(a_ref, b_ref, o_ref, acc_ref):
        k = pl.program_id(2)

        @pl.when(k == 0)
        def _():
            acc_ref[...] = jnp.zeros_like(acc_ref)

        acc_ref[...] += _dot(a_ref[...].astype(BF16), b_ref[...].astype(BF16), dims)

        @pl.when(k == nk - 1)
        def _():
            o_ref[...] = acc_ref[...].astype(o_ref.dtype)

    return body, [pltpu.VMEM(tile, F32)]


def _mm_nn(a, b3, out_dtype, name, tm=1024, tn=1408, tk=2048, b_transposed=False):
    M, K = a.shape
    C, n = b3.shape[0], b3.shape[1 if b_transposed else 2]
    tm, tk, tn = _tile(M, tm, 8), _tile(K, tk), _tile(n, tn)
    npc, nk = n // tn, K // tk
    body, scratch = _mm_body(NT if b_transposed else NN, nk, (tm, tn))
    b_spec = (pl.BlockSpec((None, tn, tk), lambda i, j, k: (j // npc, j % npc, k)) if b_transposed
              else pl.BlockSpec((None, tk, tn), lambda i, j, k: (j // npc, k, j % npc)))
    return _pcall(
        body, grid=(M // tm, C * npc, nk),
        in_specs=[pl.BlockSpec((tm, tk), lambda i, j, k: (i, k)), b_spec],
        out_specs=pl.BlockSpec((tm, tn), lambda i, j, k: (i, j)),
        out_shape=jax.ShapeDtypeStruct((M, C * n), out_dtype), scratch_shapes=scratch,
        compiler_params=_params("parallel", "parallel", "arbitrary"), name=name)(a, b3)


def _mm_nt(a, b3, out_dtype, name, tm=1024, tn=1024, tk=2048, after=None, b_transposed=False, per_step=1):
    M, _ = a.shape
    C, N, n = (b3.shape[0], b3.shape[2], b3.shape[1]) if b_transposed else b3.shape
    tm, tn, tk = _tile(M, tm, 8), _tile(N, tn), _tile(n, tk)
    dims = NN if b_transposed else NT
    extra = [] if after is None else [after]
    if per_step > 1 and tk == n and C % per_step == 0:
        nk, scratch = C // per_step, [pltpu.VMEM((tm, tn), F32)]
        b3 = b3.reshape(nk, per_step, *b3.shape[1:])
        a_spec = pl.BlockSpec((tm, per_step * n), lambda i, j, k: (i, k))
        if b_transposed:
            b_spec = pl.BlockSpec((None, per_step, n, tn), lambda i, j, k: (k, 0, 0, j))
        else:
            b_spec = pl.BlockSpec((None, per_step, tn, n), lambda i, j, k: (k, 0, j, 0))

        def body(a_ref, b_ref, *rest):
            o_ref, acc_ref = rest[len(extra):]
            k = pl.program_id(2)

            @pl.when(k == 0)
            def _():
                acc_ref[...] = jnp.zeros_like(acc_ref)

            b = b_ref[...].astype(BF16)
            b = b.reshape(per_step * n, tn) if b_transposed else jnp.concatenate([b[u] for u in range(per_step)], axis=1)
            acc_ref[...] += _dot(a_ref[...].astype(BF16), b, dims)

            @pl.when(k == nk - 1)
            def _():
                o_ref[...] = acc_ref[...].astype(o_ref.dtype)
    else:
        kpc = n // tk
        nk = C * kpc
        inner, scratch = _mm_body(dims, nk, (tm, tn))
        a_spec = pl.BlockSpec((tm, tk), lambda i, j, k: (i, k))
        b_spec = (pl.BlockSpec((None, tk, tn), lambda i, j, k: (k // kpc, k % kpc, j)) if b_transposed
                  else pl.BlockSpec((None, tn, tk), lambda i, j, k: (k // kpc, j, k % kpc)))

        def body(a_ref, b_ref, *rest):
            inner(a_ref, b_ref, *rest[len(extra):])

    return _pcall(
        body, grid=(M // tm, N // tn, nk), in_specs=[a_spec, b_spec] + [HBM] * len(extra),
        out_specs=pl.BlockSpec((tm, tn), lambda i, j, k: (i, j)),
        out_shape=jax.ShapeDtypeStruct((M, N), out_dtype), scratch_shapes=scratch,
        compiler_params=_params("parallel", "parallel", "arbitrary"), name=name)(a, b3, *extra)


def _mm_tn(x, y, n, out_dtype, name, tm=1024, tn=1408, tk=2048, after=None):
    S, P = x.shape
    C = y.shape[1] // n
    tm, tn, tk = _tile(P, tm), _tile(n, tn), _tile(S, tk, 8)
    npc, nk = n // tn, S // tk
    inner, scratch = _mm_body(TN, nk, (tm, tn))
    extra = [] if after is None else [after]

    def body(x_ref, y_ref, *rest):
        inner(x_ref, y_ref, *rest[len(extra):])

    return _pcall(
        body, grid=(P // tm, C * npc, nk),
        in_specs=[pl.BlockSpec((tk, tm), lambda i, j, k: (k, i)),
                  pl.BlockSpec((tk, tn), lambda i, j, k: (k, j))] + [HBM] * len(extra),
        out_specs=pl.BlockSpec((None, tm, tn), lambda i, j, k: (j // npc, i, j % npc)),
        out_shape=jax.ShapeDtypeStruct((C, P, n), out_dtype), scratch_shapes=scratch,
        compiler_params=_params("parallel", "parallel", "arbitrary"), name=name)(x, y, *extra)


def _rms_scale(v):
    return lax.rsqrt(jnp.mean(v * v, axis=-1, keepdims=True) + RMS_EPS)


def _rms_bwd(gy, v, r):
    return r * gy - v * (r * r * r * jnp.mean(gy * v, axis=-1, keepdims=True))


def _rows_spec(tm, d):
    return pl.BlockSpec((tm, d), lambda i: (i, 0))


def _vec_spec(d):
    return pl.BlockSpec((1, d), lambda i: (0, 0))


def _rms_fwd(x, g, name, tm=256):
    S, D = x.shape

    def body(x_ref, g_ref, h_ref):
        v = x_ref[...]
        h_ref[...] = (v * _rms_scale(v) * g_ref[...]).astype(BF16)

    return _pcall(body, grid=(S // tm,), in_specs=[_rows_spec(tm, D), _vec_spec(D)], out_specs=_rows_spec(tm, D),
                  out_shape=jax.ShapeDtypeStruct((S, D), BF16), compiler_params=_params("parallel"), name=name)(x, g)


def _mid_fwd(x, mix, g_post, g_pre, name, tm=256):
    S, D = x.shape

    def body(x_ref, m_ref, gp_ref, gn_ref, x2_ref, h_ref):
        m = m_ref[...]
        x2 = x_ref[...] + m * _rms_scale(m) * gp_ref[...]
        x2_ref[...] = x2
        h_ref[...] = (x2 * _rms_scale(x2) * gn_ref[...]).astype(BF16)

    return _pcall(body, grid=(S // tm,), in_specs=[_rows_spec(tm, D), _rows_spec(tm, D), _vec_spec(D), _vec_spec(D)],
                  out_specs=[_rows_spec(tm, D), _rows_spec(tm, D)],
                  out_shape=[jax.ShapeDtypeStruct((S, D), F32), jax.ShapeDtypeStruct((S, D), BF16)],
                  compiler_params=_params("parallel"), name=name)(x, mix, g_post, g_pre)


def _loss_bwd(x2, f, tgt, g_post, name, tm=256):
    S, D = x2.shape

    def body(x2_ref, f_ref, t_ref, g_ref, dy_ref, df_ref, dg_ref, ls_ref):
        i = pl.program_id(0)

        @pl.when(i == 0)
        def _():
            dg_ref[...] = jnp.zeros_like(dg_ref)
            ls_ref[...] = jnp.zeros_like(ls_ref)

        fv = f_ref[...]
        r = _rms_scale(fv)
        g = g_ref[...]
        err = x2_ref[...] + fv * r * g - t_ref[...]
        ls_ref[...] += jnp.broadcast_to(0.5 * jnp.sum(jnp.mean(err * err, axis=-1, keepdims=True), axis=0, keepdims=True), ls_ref.shape)
        dy = err * (1.0 / D)
        dy_ref[...] = dy
        df_ref[...] = _rms_bwd(dy * g, fv, r).astype(BF16)
        dg_ref[...] += jnp.sum(dy * fv * r, axis=0, keepdims=True)

    return _pcall(body, grid=(S // tm,),
                  in_specs=[_rows_spec(tm, D), _rows_spec(tm, D), _rows_spec(tm, D), _vec_spec(D)],
                  out_specs=[_rows_spec(tm, D), _rows_spec(tm, D), _vec_spec(D), _vec_spec(LANES)],
                  out_shape=[jax.ShapeDtypeStruct((S, D), F32), jax.ShapeDtypeStruct((S, D), BF16),
                             jax.ShapeDtypeStruct((1, D), F32), jax.ShapeDtypeStruct((1, LANES), F32)],
                  compiler_params=_params("arbitrary"), name=name)(x2, f, tgt, g_post)


def _mid_bwd(dy, dh2, x2, mix, g_pre, g_post, name, tm=256):
    S, D = dy.shape

    def body(dy_ref, dh_ref, x2_ref, m_ref, gn_ref, gp_ref, dx2_ref, dm_ref, dgn_ref, dgp_ref):
        i = pl.program_id(0)

        @pl.when(i == 0)
        def _():
            dgn_ref[...] = jnp.zeros_like(dgn_ref)
            dgp_ref[...] = jnp.zeros_like(dgp_ref)

        x2, dh = x2_ref[...], dh_ref[...]
        r = _rms_scale(x2)
        dx2 = dy_ref[...] + _rms_bwd(dh * gn_ref[...], x2, r)
        dgn_ref[...] += jnp.sum(dh * x2 * r, axis=0, keepdims=True)
        dx2_ref[...] = dx2
        m = m_ref[...]
        rm = _rms_scale(m)
        dm_ref[...] = _rms_bwd(dx2 * gp_ref[...], m, rm).astype(BF16)
        dgp_ref[...] += jnp.sum(dx2 * m * rm, axis=0, keepdims=True)

    return _pcall(body, grid=(S // tm,),
                  in_specs=[_rows_spec(tm, D)] * 4 + [_vec_spec(D)] * 2,
                  out_specs=[_rows_spec(tm, D), _rows_spec(tm, D), _vec_spec(D), _vec_spec(D)],
                  out_shape=[jax.ShapeDtypeStruct((S, D), F32), jax.ShapeDtypeStruct((S, D), BF16),
                             jax.ShapeDtypeStruct((1, D), F32), jax.ShapeDtypeStruct((1, D), F32)],
                  compiler_params=_params("arbitrary"), name=name)(dy, dh2, x2, mix, g_pre, g_post)


def _first_bwd(dx2, dh1, x, g_pre, name, tm=256):
    S, D = x.shape

    def body(dx2_ref, dh_ref, x_ref, g_ref, gx_ref, dg_ref):
        i = pl.program_id(0)

        @pl.when(i == 0)
        def _():
            dg_ref[...] = jnp.zeros_like(dg_ref)

        xv, dh = x_ref[...], dh_ref[...]
        r = _rms_scale(xv)
        gx_ref[...] = dx2_ref[...] + _rms_bwd(dh * g_ref[...], xv, r)
        dg_ref[...] += jnp.sum(dh * xv * r, axis=0, keepdims=True)

    return _pcall(body, grid=(S // tm,), in_specs=[_rows_spec(tm, D)] * 3 + [_vec_spec(D)],
                  out_specs=[_rows_spec(tm, D), _vec_spec(D)],
                  out_shape=[jax.ShapeDtypeStruct((S, D), F32), jax.ShapeDtypeStruct((1, D), F32)],
                  compiler_params=_params("arbitrary"), name=name)(dx2, dh1, x, g_pre)


def _logsig_pair(z):
    lb = jnp.minimum(z, 0.0) - jnp.log(1.0 + jnp.exp(-jnp.abs(z)))
    return lb, lb - z


SB_KEY_BLOCK = 256


def _sum_matrix(strict):
    ia = lax.broadcasted_iota(jnp.int32, (SB_KEY_BLOCK, SB_KEY_BLOCK), 0)
    ib = lax.broadcasted_iota(jnp.int32, (SB_KEY_BLOCK, SB_KEY_BLOCK), 1)
    return ((ia > ib) if strict == ">" else (ia < ib)).astype(BF16)


def _row_total(sums, v, col):
    return jnp.broadcast_to(sums[:, col:col + 1] + v[:, col:col + 1], (v.shape[0], LANES))


def _lanes(c, width):
    return jnp.tile(c, (1, width // LANES))


def _split_dot(v, u):
    hi = v.astype(BF16)
    lo = (v - hi.astype(F32)).astype(BF16)
    return _dot(hi, u, NN) + _dot(lo, u, NN)


def _head_out(o, g):
    return o * _rms_scale(o) * g


def _sb_fwd(proj, gain, n_heads, name, tq=1024):
    S = proj.shape[0]
    H, tk = n_heads, SB_KEY_BLOCK
    tq = _tile(S, tq, 2 * tk)
    scale = HEAD_DIM ** -0.5

    def body(q_ref, k_ref, v_ref, g_ref, o_ref, ct_ref, mx_ref, oacc, cacc):
        i = pl.program_id(1)
        oacc[...] = jnp.zeros_like(oacc)
        cacc[...] = jnp.zeros_like(cacc)
        sums = _sum_matrix(">")

        def run(blocks):
            scored = []
            for k0, r0, diagonal in blocks:
                rows = pl.ds(r0, tq - r0)
                lb, lk = _logsig_pair(_dot(q_ref[rows, :].astype(BF16), k_ref[pl.ds(k0, tk), :].astype(BF16), NT) * scale)
                causal = None
                if diagonal:
                    causal = (lax.broadcasted_iota(jnp.int32, (tq - r0, tk), 1)
                              < lax.broadcasted_iota(jnp.int32, (tq - r0, tk), 0))
                    lk = jnp.where(causal, lk, 0.0)
                scored.append((k0, rows, causal, lb, lk))
            summed = [(k0, rows, causal, lb, lk, _split_dot(lk, sums)) for k0, rows, causal, lb, lk in scored]
            weights = []
            for k0, rows, causal, lb, lk, after in summed:
                c = cacc[rows, :]
                a = jnp.exp(lb + after + _lanes(c, tk))
                if causal is not None:
                    a = jnp.where(causal, a, 0.0)
                cacc[rows, :] = c + _row_total(after, lk, 0)
                weights.append((k0, rows, a.astype(BF16)))
            for k0, rows, a in weights:
                oacc[rows, :] += _dot(a, v_ref[pl.ds(k0, tk), :].astype(BF16), NN)

        for d in reversed(range(0, tq // tk, 2)):
            run([(pl.multiple_of(i * tq + e * tk, tk), e * tk, True) for e in (d + 1, d)])
        per_trip = tq // tk

        def step(it, carry):
            k0 = pl.multiple_of((i - 1 - it) * tq, tq)
            run([(pl.multiple_of(k0 + e * tk, tk), 0, False) for e in reversed(range(per_trip))])
            return carry

        lax.fori_loop(0, i, step, 0)
        o = oacc[...]
        o_ref[...] = o
        ct_ref[...] = cacc[...]
        mx_ref[...] = _head_out(o, g_ref[...]).astype(BF16)

    blk = pl.BlockSpec((tq, HEAD_DIM), lambda h, i: (i, h))
    return _pcall(
        body, grid=(H, S // tq),
        in_specs=[blk, pl.BlockSpec((S, HEAD_DIM), lambda h, i: (0, H + h)),
                  pl.BlockSpec((S, HEAD_DIM), lambda h, i: (0, 2 * H + h)), pl.BlockSpec((1, HEAD_DIM), lambda h, i: (0, h))],
        out_specs=[blk, blk, blk],
        out_shape=[jax.ShapeDtypeStruct((S, H * HEAD_DIM), F32), jax.ShapeDtypeStruct((S, H * HEAD_DIM), F32),
                   jax.ShapeDtypeStruct((S, H * HEAD_DIM), BF16)],
        scratch_shapes=[pltpu.VMEM((tq, HEAD_DIM), F32), pltpu.VMEM((tq, LANES), F32)],
        compiler_params=_params("parallel", "arbitrary"), name=name)(proj, proj, proj, gain)


def _sb_bwd(proj, gain, o_raw, ctot, dmixed, dm_col0, n_heads, name, tq=1024):
    S = proj.shape[0]
    H, tk = n_heads, SB_KEY_BLOCK
    tq = _tile(S, tq, 2 * tk)
    nq = S // tq
    scale = HEAD_DIM ** -0.5

    def body(q_ref, k_ref, v_ref, g_ref, o_ref, ct_ref, dm_ref, dq_ref, dk_ref, dv_ref, dg_ref,
             dkacc, dvacc, dqacc, pfx, gcar, dos):
        i = pl.program_id(1)

        @pl.when(i == 0)
        def _():
            dkacc[...] = jnp.zeros_like(dkacc)
            dvacc[...] = jnp.zeros_like(dvacc)
            dg_ref[...] = jnp.zeros_like(dg_ref)

        o, dm, g = o_ref[...], dm_ref[...], g_ref[...]
        r = _rms_scale(o)
        dos[...] = _rms_bwd(dm * g, o, r).astype(BF16)
        dg_ref[...] += jnp.broadcast_to(jnp.sum(dm * o * r, axis=0, keepdims=True), dg_ref.shape)
        dqacc[...] = jnp.zeros_like(dqacc)
        pfx[...] = jnp.zeros_like(pfx)
        gcar[...] = jnp.zeros_like(gcar)
        later, earlier = _sum_matrix(">"), _sum_matrix("<")

        def run(blocks):
            scored = []
            for k0, r0, diagonal in blocks:
                rows, keys = pl.ds(r0, tq - r0), pl.ds(k0, tk)
                lb, lk = _logsig_pair(_dot(q_ref[rows, :].astype(BF16), k_ref[keys, :].astype(BF16), NT) * scale)
                da = _dot(dos[rows, :], v_ref[keys, :].astype(BF16), NT)
                causal = None
                if diagonal:
                    causal = (lax.broadcasted_iota(jnp.int32, (tq - r0, tk), 1)
                              < lax.broadcasted_iota(jnp.int32, (tq - r0, tk), 0))
                    lk = jnp.where(causal, lk, 0.0)
                scored.append((rows, keys, causal, lb, lk, da))
            summed = [(*blk, _split_dot(blk[4], later)) for blk in scored]
            weighted = []
            for rows, keys, causal, lb, lk, da, after in summed:
                p = pfx[rows, :] + _row_total(after, lk, 0)
                pfx[rows, :] = p
                a = jnp.exp(lb + after + _lanes(ct_ref[rows, :] - p, tk))
                if causal is not None:
                    a = jnp.where(causal, a, 0.0)
                dl = da * a
                weighted.append((rows, keys, causal, lb, a.astype(BF16), dl, _dot(dl.astype(BF16), earlier, NN)))
            cotangents = []
            for rows, keys, causal, lb, a, dl, before in weighted:
                gc = gcar[rows, :]
                gcar[rows, :] = gc + _row_total(before, dl, tk - 1)
                sig = jnp.exp(lb)
                gsum = (before + _lanes(gc, tk)) * sig
                if causal is not None:
                    gsum = jnp.where(causal, gsum, 0.0)
                cotangents.append((rows, keys, a, ((dl * (1.0 - sig) - gsum) * scale).astype(BF16)))
            for rows, keys, a, dz in cotangents:
                q, do = q_ref[rows, :].astype(BF16), dos[rows, :]
                dvacc[keys, :] += _dot(a, do, TN)
                dqacc[rows, :] += _dot(dz, k_ref[keys, :].astype(BF16), NN)
                dkacc[keys, :] += _dot(dz, q, TN)

        def step(j, carry):
            k0 = pl.multiple_of(j * 2 * tk, 2 * tk)
            run([(k0, 0, False), (pl.multiple_of(k0 + tk, tk), 0, False)])
            return carry

        lax.fori_loop(0, i * (tq // tk // 2), step, 0)
        for d in range(0, tq // tk, 2):
            run([(pl.multiple_of(i * tq + e * tk, tk), e * tk, True) for e in (d, d + 1)])
        dq_ref[...] = dqacc[...].astype(BF16)

        @pl.when(i == nq - 1)
        def _():
            dk_ref[...] = dkacc[...].astype(BF16)
            dv_ref[...] = dvacc[...].astype(BF16)

    blk = pl.BlockSpec((tq, HEAD_DIM), lambda h, i: (i, h))
    full = pl.BlockSpec((S, HEAD_DIM), lambda h, i: (0, h))
    W = H * HEAD_DIM
    return _pcall(
        body, grid=(H, nq),
        in_specs=[blk, pl.BlockSpec((S, HEAD_DIM), lambda h, i: (0, H + h)),
                  pl.BlockSpec((S, HEAD_DIM), lambda h, i: (0, 2 * H + h)), pl.BlockSpec((1, HEAD_DIM), lambda h, i: (0, h)),
                  blk, blk, pl.BlockSpec((tq, HEAD_DIM), lambda h, i: (i, dm_col0 + h))],
        out_specs=[blk, full, full, pl.BlockSpec((8, HEAD_DIM), lambda h, i: (0, h))],
        out_shape=[jax.ShapeDtypeStruct((S, W), BF16), jax.ShapeDtypeStruct((S, W), BF16),
                   jax.ShapeDtypeStruct((S, W), BF16), jax.ShapeDtypeStruct((8, W), F32)],
        scratch_shapes=[pltpu.VMEM((S, HEAD_DIM), F32), pltpu.VMEM((S, HEAD_DIM), F32), pltpu.VMEM((tq, HEAD_DIM), F32),
                        pltpu.VMEM((tq, LANES), F32), pltpu.VMEM((tq, LANES), F32), pltpu.VMEM((tq, HEAD_DIM), BF16)],
        compiler_params=_params("arbitrary", "arbitrary"), name=name)(proj, proj, proj, gain, o_raw, ctot, dmixed)


def _rope_tables(S):
    inv_freq = ROPE_THETA ** (-jnp.arange(0, HEAD_DIM, 2, dtype=F32) / HEAD_DIM)
    ang = jnp.arange(S, dtype=F32)[:, None] * inv_freq[None, :]
    cos, sin = jnp.cos(ang), jnp.sin(ang)
    return jnp.concatenate([cos, cos], axis=1), jnp.concatenate([-sin, sin], axis=1)


def _rope(v, cos2, sin_signed):
    return v * cos2 + pltpu.roll(v, HEAD_DIM // 2, axis=1) * sin_signed


def _dil_rows(d, r, l0, n):
    if d == 1:
        return pl.ds(l0 if isinstance(l0, int) else pl.multiple_of(l0, KEY_BLOCK), n)
    return pl.ds(r + d * l0, n, stride=d)


def _dil_blocks(S, visit):
    B = KEY_BLOCK
    group = 16
    for b, d in enumerate(DILATIONS):
        nb = S // d // B
        if nb == 1:
            g = math.gcd(d, group)

            def trip(t, carry, b=b, d=d, g=g):
                visit([(b, d, t * g + u, 0, True) for u in range(g)])
                return carry

            lax.fori_loop(0, d // g, trip, 0)
        elif d == 1:
            visit([(b, d, 0, 0, True)])
            g = max(k for k in range(1, group + 2) if (nb - 1) % k == 0)

            def trip(t, carry, b=b, d=d, g=g):
                visit([(b, d, 0, (1 + t * g + u) * B, False) for u in range(g)])
                return carry

            lax.fori_loop(0, (nb - 1) // g, trip, 0)
        else:
            g = math.gcd(d, max(group // nb, 1))

            def trip(t, carry, b=b, d=d, nb=nb, g=g):
                visit([(b, d, t * g + u, n * B, n == 0) for u in range(g) for n in range(nb)])
                return carry

            lax.fori_loop(0, d // g, trip, 0)


def _dil_mask(first):
    B = KEY_BLOCK
    nk = B if first else 2 * B
    iq = lax.broadcasted_iota(jnp.int32, (B, nk), 0)
    ik = lax.broadcasted_iota(jnp.int32, (B, nk), 1)
    return (ik <= iq) if first else ((ik >= iq) & (ik <= iq + B))


def _dil_fwd(proj, cos2, sin_signed, gain, col0, n_heads, name):
    S = proj.shape[0]
    H, B = n_heads, KEY_BLOCK
    scale = HEAD_DIM ** -0.5
    rc = _tile(S, 256, 8)

    def body(q_ref, k_ref, v_ref, c_ref, s_ref, g_ref, o_ref, l_ref, mx_ref, qr, kr, *per_branch):
        ob, lb = per_branch[:len(DILATIONS)], per_branch[len(DILATIONS):]

        def rope_rows(t, carry):
            rows = pl.ds(pl.multiple_of(t * rc, rc), rc)
            qr[rows, :] = _rope(q_ref[rows, :], c_ref[rows, :], s_ref[rows, :])
            kr[rows, :] = _rope(k_ref[rows, :], c_ref[rows, :], s_ref[rows, :])
            return carry

        lax.fori_loop(0, S // rc, rope_rows, 0)

        def visit(blocks):
            scores = []
            for b, d, r, l0, first in blocks:
                qrows = _dil_rows(d, r, l0, B)
                krows = qrows if first else _dil_rows(d, r, l0 - B, 2 * B)
                s = _dot(qr[qrows, :].astype(BF16), kr[krows, :].astype(BF16), NT) * scale
                scores.append((b, qrows, krows, jnp.where(_dil_mask(first), s, NEG)))
            weights = []
            for b, qrows, krows, s in scores:
                m = jnp.max(s, axis=1, keepdims=True)
                p = jnp.exp(s - m)
                den = jnp.sum(p, axis=1, keepdims=True)
                lb[b][qrows, :] = jnp.broadcast_to(m + jnp.log(den), (B, LANES))
                weights.append((b, qrows, krows, p.astype(BF16), den))
            for b, qrows, krows, p, den in weights:
                ob[b][qrows, :] = _dot(p, v_ref[krows, :].astype(BF16), NN) / den

        _dil_blocks(S, visit)

        def combine(t, carry):
            rows = pl.ds(pl.multiple_of(t * rc, rc), rc)
            l0, l1, l2 = lb[0][rows, :], lb[1][rows, :], lb[2][rows, :]
            m = jnp.maximum(jnp.maximum(l0, l1), l2)
            w0, w1, w2 = jnp.exp(l0 - m), jnp.exp(l1 - m), jnp.exp(l2 - m)
            den = w0 + w1 + w2
            o = (w0 * ob[0][rows, :] + w1 * ob[1][rows, :] + w2 * ob[2][rows, :]) / den
            o_ref[rows, :] = o
            l_ref[rows, :] = m + jnp.log(den)
            mx_ref[rows, :] = _head_out(o, g_ref[...]).astype(BF16)
            return carry

        lax.fori_loop(0, S // rc, combine, 0)

    def col(k):
        return pl.BlockSpec((S, HEAD_DIM), lambda h: (0, col0 + k * H + h))

    tab = pl.BlockSpec((S, HEAD_DIM), lambda h: (0, 0))
    out = pl.BlockSpec((S, HEAD_DIM), lambda h: (0, h))
    W = H * HEAD_DIM
    return _pcall(
        body, grid=(H,),
        in_specs=[col(0), col(1), col(2), tab, tab, pl.BlockSpec((1, HEAD_DIM), lambda h: (0, h))],
        out_specs=[out, out, out],
        out_shape=[jax.ShapeDtypeStruct((S, W), F32), jax.ShapeDtypeStruct((S, W), F32), jax.ShapeDtypeStruct((S, W), BF16)],
        scratch_shapes=[pltpu.VMEM((S, HEAD_DIM), F32)] * (2 + 2 * len(DILATIONS)),
        compiler_params=_params("parallel"), name=name)(proj, proj, proj, cos2, sin_signed, gain)


def _dil_bwd(proj, cos2, sin_signed, gain, o_raw, lse, dmixed, dm_col0, col0, n_heads, name):
    S = proj.shape[0]
    H, B = n_heads, KEY_BLOCK
    scale = HEAD_DIM ** -0.5
    rc = _tile(S, 256, 8)

    def body(q_ref, k_ref, v_ref, c_ref, s_ref, g_ref, o_ref, l_ref, dm_ref, dq_ref, dk_ref, dv_ref, dg_ref,
             qr, kr, dos, dsum, dqr, dkr, dvv):
        dg_ref[...] = jnp.zeros_like(dg_ref)

        def prep(t, carry):
            rows = pl.ds(pl.multiple_of(t * rc, rc), rc)
            qr[rows, :] = _rope(q_ref[rows, :], c_ref[rows, :], s_ref[rows, :])
            kr[rows, :] = _rope(k_ref[rows, :], c_ref[rows, :], s_ref[rows, :])
            o, dm = o_ref[rows, :], dm_ref[rows, :]
            r = _rms_scale(o)
            do = _rms_bwd(dm * g_ref[...], o, r)
            dg_ref[...] += jnp.broadcast_to(jnp.sum(dm * o * r, axis=0, keepdims=True), dg_ref.shape)
            dos[rows, :] = do
            dsum[rows, :] = jnp.broadcast_to(jnp.sum(do * o, axis=1, keepdims=True), (rc, LANES))
            dqr[rows, :] = jnp.zeros((rc, HEAD_DIM), F32)
            dkr[rows, :] = jnp.zeros((rc, HEAD_DIM), F32)
            dvv[rows, :] = jnp.zeros((rc, HEAD_DIM), F32)
            return carry

        lax.fori_loop(0, S // rc, prep, 0)

        def visit(blocks):
            products = []
            for b, d, r, l0, first in blocks:
                qrows = _dil_rows(d, r, l0, B)
                krows = qrows if first else _dil_rows(d, r, l0 - B, 2 * B)
                qs, ks = qr[qrows, :].astype(BF16), kr[krows, :].astype(BF16)
                do = dos[qrows, :].astype(BF16)
                s = jnp.where(_dil_mask(first), _dot(qs, ks, NT) * scale, NEG)
                dp = _dot(do, v_ref[krows, :].astype(BF16), NT)
                products.append((qrows, krows, qs, ks, do, s, dp))
            cotangents = []
            for qrows, krows, qs, ks, do, s, dp in products:
                p = jnp.exp(s - l_ref[qrows, :][:, 0:1])
                ds = (p * (dp - dsum[qrows, :][:, 0:1]) * scale).astype(BF16)
                cotangents.append((qrows, krows, qs, ks, do, p.astype(BF16), ds))
            for qrows, krows, qs, ks, do, p, ds in cotangents:
                dqr[qrows, :] += _dot(ds, ks, NN)
                dkr[krows, :] += _dot(ds, qs, TN)
                dvv[krows, :] += _dot(p, do, TN)

        _dil_blocks(S, visit)

        def finish(t, carry):
            rows = pl.ds(pl.multiple_of(t * rc, rc), rc)
            c, s = c_ref[rows, :], s_ref[rows, :]
            dq, dk = dqr[rows, :], dkr[rows, :]
            dq_ref[rows, :] = (dq * c + pltpu.roll(dq * s, HEAD_DIM // 2, axis=1)).astype(BF16)
            dk_ref[rows, :] = (dk * c + pltpu.roll(dk * s, HEAD_DIM // 2, axis=1)).astype(BF16)
            dv_ref[rows, :] = dvv[rows, :].astype(BF16)
            return carry

        lax.fori_loop(0, S // rc, finish, 0)

    def col(k):
        return pl.BlockSpec((S, HEAD_DIM), lambda h: (0, col0 + k * H + h))

    tab = pl.BlockSpec((S, HEAD_DIM), lambda h: (0, 0))
    out = pl.BlockSpec((S, HEAD_DIM), lambda h: (0, h))
    W = H * HEAD_DIM
    big = pltpu.VMEM((S, HEAD_DIM), F32)
    return _pcall(
        body, grid=(H,),
        in_specs=[col(0), col(1), col(2), tab, tab, pl.BlockSpec((1, HEAD_DIM), lambda h: (0, h)), out, out,
                  pl.BlockSpec((S, HEAD_DIM), lambda h: (0, dm_col0 + h))],
        out_specs=[out, out, out, pl.BlockSpec((8, HEAD_DIM), lambda h: (0, h))],
        out_shape=[jax.ShapeDtypeStruct((S, W), BF16), jax.ShapeDtypeStruct((S, W), BF16),
                   jax.ShapeDtypeStruct((S, W), BF16), jax.ShapeDtypeStruct((8, W), F32)],
        scratch_shapes=[big, big, big, pltpu.VMEM((S, LANES), F32), big, big, big],
        compiler_params=_params("parallel"), name=name)(proj, proj, proj, cos2, sin_signed, gain, o_raw, lse, dmixed)


GELU_C = math.sqrt(2.0 / math.pi)
GELU_A = 0.044715
HALO = 16


def _shift_down(cur, halo, k):
    out = pltpu.roll(cur, k, axis=0)
    row = lax.broadcasted_iota(jnp.int32, cur.shape, 0)
    for t in range(k):
        out = jnp.where(row == t, halo[HALO - k + t:HALO - k + t + 1, :], out)
    return out


def _shift_up(cur, halo, k):
    n = cur.shape[0]
    out = pltpu.roll(cur, n - k, axis=0)
    row = lax.broadcasted_iota(jnp.int32, cur.shape, 0)
    for t in range(k):
        out = jnp.where(row == n - k + t, halo[t:t + 1, :], out)
    return out


def _conv3(cur, halo, cw):
    return _shift_down(cur, halo, 2) * cw[0:1, :] + _shift_down(cur, halo, 1) * cw[1:2, :] + cur * cw[2:3, :] + cw[3:4, :]


def _gelu_parts(x):
    t = jnp.tanh(GELU_C * (x + GELU_A * x * x * x))
    return 0.5 * x * (1.0 + t), t


def _geglu_specs(tm, tn, ncb):
    hb = tm // HALO

    def cur(off):
        return pl.BlockSpec((tm, tn), lambda j, i: (i, off + j))

    def prev(off):
        return pl.BlockSpec((HALO, tn), lambda j, i: (jnp.maximum(i * hb - 1, 0), off + j))

    def taps(off):
        return pl.BlockSpec((8, tn), lambda j, i: (0, off + j))

    return [cur(0), prev(0), cur(ncb), prev(ncb), taps(0), taps(ncb)]


def _geglu_fwd(u, cwb, name, tm=256, tn=1408):
    S, F2 = u.shape
    F = F2 // 2
    tm, tn = _tile(S, tm, HALO), _tile(F, tn)
    ncb = F // tn

    def body(g_ref, gp_ref, v_ref, vp_ref, cg_ref, cv_ref, y_ref):
        top = pl.program_id(1) > 0
        gp = jnp.where(top, gp_ref[...].astype(F32), 0.0)
        vp = jnp.where(top, vp_ref[...].astype(F32), 0.0)
        gc = _conv3(g_ref[...].astype(F32), gp, cg_ref[...])
        vc = _conv3(v_ref[...].astype(F32), vp, cv_ref[...])
        y_ref[...] = (_gelu_parts(gc)[0] * vc).astype(BF16)

    return _pcall(body, grid=(ncb, S // tm), in_specs=_geglu_specs(tm, tn, ncb),
                  out_specs=pl.BlockSpec((tm, tn), lambda j, i: (i, j)),
                  out_shape=jax.ShapeDtypeStruct((S, F), BF16),
                  compiler_params=_params("parallel", "parallel"), name=name)(u, u, u, u, cwb, cwb)


def _geglu_bwd(u, dy, cwb, name, tm=256, tn=512):
    S, F2 = u.shape
    F = F2 // 2
    tm, tn = _tile(S, tm, HALO), _tile(F, tn)
    ncb = F // tn

    def body(g_ref, gp_ref, v_ref, vp_ref, cg_ref, cv_ref, dy_ref, dc_ref, dwg_ref, dwv_ref):
        i = pl.program_id(1)

        @pl.when(i == 0)
        def _():
            dwg_ref[...] = jnp.zeros_like(dwg_ref)
            dwv_ref[...] = jnp.zeros_like(dwv_ref)

        top = i > 0
        g, v = g_ref[...].astype(F32), v_ref[...].astype(F32)
        gp = jnp.where(top, gp_ref[...].astype(F32), 0.0)
        vp = jnp.where(top, vp_ref[...].astype(F32), 0.0)
        gc = _conv3(g, gp, cg_ref[...])
        vc = _conv3(v, vp, cv_ref[...])
        act, t = _gelu_parts(gc)
        dact = 0.5 * (1.0 + t) + 0.5 * gc * (1.0 - t * t) * GELU_C * (1.0 + 3.0 * GELU_A * gc * gc)
        dyv = dy_ref[...].astype(F32)
        dgc = dyv * vc * dact
        dvc = dyv * act
        dc_ref[0] = dgc.astype(BF16)
        dc_ref[1] = dvc.astype(BF16)

        def taps(out_ref, dc, cur, halo):
            out_ref[0:1, :] += jnp.sum(dc * _shift_down(cur, halo, 2), axis=0, keepdims=True)
            out_ref[1:2, :] += jnp.sum(dc * _shift_down(cur, halo, 1), axis=0, keepdims=True)
            out_ref[2:3, :] += jnp.sum(dc * cur, axis=0, keepdims=True)
            out_ref[3:4, :] += jnp.sum(dc, axis=0, keepdims=True)

        taps(dwg_ref, dgc, g, gp)
        taps(dwv_ref, dvc, v, vp)

    return _pcall(body, grid=(ncb, S // tm),
                  in_specs=_geglu_specs(tm, tn, ncb) + [pl.BlockSpec((tm, tn), lambda j, i: (i, j))],
                  out_specs=[pl.BlockSpec((2, tm, tn), lambda j, i: (0, i, j)),
                             pl.BlockSpec((8, tn), lambda j, i: (0, j)), pl.BlockSpec((8, tn), lambda j, i: (0, j))],
                  out_shape=[jax.ShapeDtypeStruct((2, S, F), BF16), jax.ShapeDtypeStruct((8, F), F32),
                             jax.ShapeDtypeStruct((8, F), F32)],
                  compiler_params=_params("parallel", "arbitrary"), name=name)(u, u, u, u, cwb, cwb, dy)


def _conv_bwd(dc, cwb, name, tm=512, tn=1408):
    _, S, F = dc.shape
    tm, tn = _tile(S, tm, HALO), _tile(F, tn)
    ncb, nrb = F // tn, S // tm
    hb = tm // HALO

    def body(c_ref, n_ref, w_ref, du_ref):
        cur = c_ref[...].astype(F32)
        nxt = jnp.where(pl.program_id(2) < nrb - 1, n_ref[...].astype(F32), 0.0)
        w = w_ref[...]
        du = cur * w[2:3, :] + _shift_up(cur, nxt, 1) * w[1:2, :] + _shift_up(cur, nxt, 2) * w[0:1, :]
        du_ref[...] = du.astype(BF16)

    return _pcall(body, grid=(2, ncb, nrb),
                  in_specs=[pl.BlockSpec((None, tm, tn), lambda c, j, i: (c, i, j)),
                            pl.BlockSpec((None, HALO, tn), lambda c, j, i: (c, jnp.minimum((i + 1) * hb, S // HALO - 1), j)),
                            pl.BlockSpec((8, tn), lambda c, j, i: (0, c * ncb + j))],
                  out_specs=pl.BlockSpec((tm, tn), lambda c, j, i: (i, c * ncb + j)),
                  out_shape=jax.ShapeDtypeStruct((S, 2 * F), BF16),
                  compiler_params=_params("parallel", "parallel", "parallel"), name=name)(dc, dc, cwb)


def _adam_math(w, g, m, v):
    m = ADAM_B1 * m + (1.0 - ADAM_B1) * g
    v = ADAM_B2 * v + (1.0 - ADAM_B2) * (g * g)
    m_hat = m / (1.0 - ADAM_B1 ** ADAM_STEP)
    v_hat = v / (1.0 - ADAM_B2 ** ADAM_STEP)
    return -ADAM_LR * (m_hat / (jnp.sqrt(v_hat) + ADAM_EPS) + ADAM_WD * w), m, v


def _adamw(w, parts, m, v, name, tr=256):
    R, C = w.shape
    n, _, Cp = parts.shape
    tr = _tile(R, tr, 8)

    def body(w_ref, p_ref, m_ref, v_ref, g_out, d_out, m_out, v_out):
        g = p_ref[0, :, 0:C].astype(F32)
        for k in range(1, n):
            g = g + p_ref[k, :, 0:C].astype(F32)
        d, mn, vn = _adam_math(w_ref[...], g, m_ref[...], v_ref[...])
        g_out[...] = g
        d_out[...] = d
        m_out[...] = mn
        v_out[...] = vn

    spec = pl.BlockSpec((tr, C), lambda i: (i, 0))
    shape = jax.ShapeDtypeStruct((R, C), F32)
    return _pcall(body, grid=(R // tr,), in_specs=[spec, pl.BlockSpec((n, tr, Cp), lambda i: (0, i, 0)), spec, spec],
                  out_specs=[spec] * 4, out_shape=[shape] * 4, compiler_params=_params("parallel"), name=name)(w, parts, m, v)


def _adamw_chips(w, pair, parts, chip_ids, m, v, name, tr=256):
    R, C = w.shape
    Cp = pair.shape[2]
    by_columns = C == Cp and _tile(R, tr, 16) < 64
    tr, tc = (R, _tile(C, 256)) if by_columns else (_tile(R, tr, 16), C)

    def body(ids_ref, w_ref, own_ref, p1_ref, p2_ref, p3_ref, m_ref, v_ref, g_out, d_out, m_out, v_out):
        g = own_ref[:, 0:tc].astype(F32)
        for ref in (p1_ref, p2_ref, p3_ref):
            g = g + ref[:, 0:tc].astype(F32)
        d, mn, vn = _adam_math(w_ref[...], g, m_ref[...], v_ref[...])
        g_out[...] = g
        d_out[...] = d
        m_out[...] = mn
        v_out[...] = vn

    if by_columns:
        spec = pl.BlockSpec((tr, tc), lambda j, ids: (0, j))
    else:
        spec = pl.BlockSpec((tr, tc), lambda i, ids: (i, 0))

    def chip(k):
        if by_columns:
            return pl.BlockSpec((None, tr, tc), lambda j, ids: (ids[k], 0, j))
        return pl.BlockSpec((None, tr, Cp), lambda i, ids: (ids[k], i, 0))

    shape = jax.ShapeDtypeStruct((R, C), F32)
    grid_spec = pltpu.PrefetchScalarGridSpec(
        num_scalar_prefetch=1, grid=(C // tc if by_columns else R // tr,),
        in_specs=[spec, chip(0), chip(1), chip(2), chip(3), spec, spec], out_specs=[spec] * 4)
    return _pcall(body, grid_spec=grid_spec, out_shape=[shape] * 4, compiler_params=_params("parallel"),
                  name=name)(chip_ids, w, pair, parts, parts, parts, m, v)


def _place():
    return lax.axis_index("x"), lax.axis_index("y"), lax.axis_index("c")


def _other_chips(x, y):
    return [(1 - x, y), (x, 1 - y), (1 - x, 1 - y)]


IN_HBM = pl.BlockSpec(memory_space=pltpu.HBM)
SEM = pl.BlockSpec(memory_space=pltpu.SEMAPHORE)
EFFECT = pltpu.SideEffectType.DATAFLOW_SIDE_EFFECTING
TOKEN = jax.ShapeDtypeStruct((8, LANES), F32)
TOKEN_SPEC = pl.BlockSpec(memory_space=pltpu.VMEM)


def _in_hbm(a):
    return pltpu.with_memory_space_constraint(a, pltpu.HBM)


def _landing(shape):
    return _in_hbm(lax.empty(shape.shape, shape.dtype))


def _hbm_like(a):
    return pltpu.HBM(a.shape, a.dtype)


def _gather_start(landing, slots, after, name):
    na = len(landing)

    def body(*refs):
        land = refs[:na]
        send_sems, recv_sems = refs[na + 1], refs[na + 2]
        token = refs[-1]
        x, y, c = _place()
        for a in range(na):
            own = slots[a](land[a], x, y, c)
            for k, to in enumerate([(x, y, 1 - c)] + [(*chip, c) for chip in _other_chips(x, y)]):
                pltpu.make_async_remote_copy(
                    src_ref=own, dst_ref=own, send_sem=send_sems.at[4 * a + k],
                    recv_sem=recv_sems.at[4 * a + k], device_id=to, device_id_type=MESH).start()
        token[...] = jnp.zeros_like(token)

    sems = pltpu.SemaphoreType.DMA((4 * na,))
    outs = _pcall(
        body, in_specs=[IN_HBM] * na + [HBM],
        out_specs=[SEM, SEM] + [IN_HBM] * na + [TOKEN_SPEC],
        out_shape=[sems, sems] + [_hbm_like(s) for s in landing] + [TOKEN],
        input_output_aliases={a: 2 + a for a in range(na)},
        compiler_params=pltpu.CompilerParams(has_side_effects=EFFECT), name=name,
    )(*[_in_hbm(s) for s in landing], after)
    return outs[0], outs[1], outs[2:2 + na], outs[-1]


def _gather_forward(gathered, send_sems, recv_sems, slots, after, name):
    na = len(gathered)

    def body(*refs):
        gath = refs[:na]
        send1, recv1 = refs[na], refs[na + 1]
        fsend, frecv = refs[na + 3], refs[na + 4]
        token = refs[-1]
        x, y, c = _place()
        chips = _other_chips(x, y)
        for a in range(na):
            for k, peer in enumerate([(x, y, 1 - c)] + [(*chip, c) for chip in chips]):
                arrival = pltpu.make_async_remote_copy(
                    src_ref=slots[a](gath[a], x, y, c), dst_ref=slots[a](gath[a], *peer), send_sem=send1.at[4 * a + k],
                    recv_sem=recv1.at[4 * a + k], device_id=peer, device_id_type=MESH)
                arrival.wait_send()
                arrival.wait_recv()
        for a in range(na):
            for j, chip in enumerate(chips):
                view = slots[a](gath[a], *chip, c)
                pltpu.make_async_remote_copy(
                    src_ref=view, dst_ref=view, send_sem=fsend.at[3 * a + j], recv_sem=frecv.at[3 * a + j],
                    device_id=(x, y, 1 - c), device_id_type=MESH).start()
        token[...] = jnp.zeros_like(token)

    sems = pltpu.SemaphoreType.DMA((3 * na,))
    outs = _pcall(
        body, in_specs=[IN_HBM] * na + [SEM, SEM, HBM],
        out_specs=[SEM, SEM] + [IN_HBM] * na + [TOKEN_SPEC],
        out_shape=[sems, sems] + [_hbm_like(g) for g in gathered] + [TOKEN],
        input_output_aliases={a: 2 + a for a in range(na)},
        compiler_params=pltpu.CompilerParams(has_side_effects=EFFECT), name=name,
    )(*gathered, send_sems, recv_sems, after)
    return outs[0], outs[1], outs[2:2 + na], outs[-1]


def _gather_finish(gathered, fsend, frecv, slots, after, name):
    na = len(gathered)

    def body(*refs):
        gath, fs, fr = refs[:na], refs[na], refs[na + 1]
        x, y, c = _place()
        for a in range(na):
            for j, chip in enumerate(_other_chips(x, y)):
                passed = pltpu.make_async_remote_copy(
                    src_ref=slots[a](gath[a], *chip, c), dst_ref=slots[a](gath[a], *chip, 1 - c),
                    send_sem=fs.at[3 * a + j], recv_sem=fr.at[3 * a + j], device_id=(x, y, 1 - c), device_id_type=MESH)
                passed.wait_send()
                passed.wait_recv()

    outs = _pcall(
        body, in_specs=[IN_HBM] * na + [SEM, SEM, HBM], out_specs=[IN_HBM] * na,
        out_shape=[_hbm_like(g) for g in gathered], input_output_aliases={a: a for a in range(na)},
        compiler_params=pltpu.CompilerParams(has_side_effects=EFFECT), name=name,
    )(*gathered, fsend, frecv, after)
    return list(outs)


def _pair_copy(view, src, land, send_sems, recv_sems, chip):
    x, y, c = _place()
    return pltpu.make_async_remote_copy(
        src_ref=view(src, chip, 1 - c), dst_ref=land.at[chip], send_sem=send_sems.at[chip], recv_sem=recv_sems.at[chip],
        device_id=(x, y, 1 - c), device_id_type=MESH)


def _pair_start(grad, view, block, after, name):
    def body(src, land, after_ref, send_sems, recv_sems, src_thru, land_thru, token):
        for chip in range(N_CHIP):
            _pair_copy(view, src, land, send_sems, recv_sems, chip).start()
        token[...] = jnp.zeros_like(token)

    sems = pltpu.SemaphoreType.DMA((N_CHIP,))
    land = jax.ShapeDtypeStruct((N_CHIP, *block), BF16)
    return _pcall(
        body, in_specs=[IN_HBM, IN_HBM, HBM], out_specs=[SEM, SEM, IN_HBM, IN_HBM, TOKEN_SPEC],
        out_shape=[sems, sems, _hbm_like(grad), _hbm_like(land), TOKEN], input_output_aliases={0: 2, 1: 3},
        compiler_params=pltpu.CompilerParams(has_side_effects=EFFECT), name=name,
    )(_in_hbm(grad), _landing(land), after)


def _pair_wait(grad, recv, send_sems, recv_sems, view, after, name):
    def body(src, land, send, recv_s, after_ref, src_thru, land_thru):
        for chip in range(N_CHIP):
            copy = _pair_copy(view, src, land, send, recv_s, chip)
            copy.wait_send()
            copy.wait_recv()

    return _pcall(
        body, in_specs=[IN_HBM, IN_HBM, SEM, SEM, HBM], out_specs=[IN_HBM, IN_HBM],
        out_shape=[_hbm_like(grad), _hbm_like(recv)], input_output_aliases={0: 0, 1: 1},
        compiler_params=pltpu.CompilerParams(has_side_effects=EFFECT), name=name,
    )(grad, recv, send_sems, recv_sems, after)


def _chip_start(pair, after, name):
    def body(src, land, after_ref, send_sems, recv_sems, src_thru, land_thru, token):
        x, y, c = _place()
        for j, (px, py) in enumerate(_other_chips(x, y)):
            pltpu.make_async_remote_copy(
                src_ref=src.at[2 * px + py], dst_ref=land.at[2 * x + y], send_sem=send_sems.at[j], recv_sem=recv_sems.at[j],
                device_id=(px, py, c), device_id_type=MESH).start()
        token[...] = jnp.zeros_like(token)

    sems = pltpu.SemaphoreType.DMA((3,))
    return _pcall(
        body, in_specs=[IN_HBM, IN_HBM, HBM], out_specs=[SEM, SEM, IN_HBM, IN_HBM, TOKEN_SPEC],
        out_shape=[sems, sems, _hbm_like(pair), _hbm_like(pair), TOKEN], input_output_aliases={0: 2, 1: 3},
        compiler_params=pltpu.CompilerParams(has_side_effects=EFFECT), name=name,
    )(_in_hbm(pair), _landing(pair), after)


def _chip_wait(pair, parts, send_sems, recv_sems, after, name):
    def body(src, land, send, recv, after_ref, src_thru, land_thru):
        x, y, c = _place()
        for j, (px, py) in enumerate(_other_chips(x, y)):
            copy = pltpu.make_async_remote_copy(
                src_ref=src.at[2 * px + py], dst_ref=land.at[2 * px + py], send_sem=send.at[j], recv_sem=recv.at[j],
                device_id=(px, py, c), device_id_type=MESH)
            copy.wait_send()
            copy.wait_recv()

    return _pcall(
        body, in_specs=[IN_HBM, IN_HBM, SEM, SEM, HBM], out_specs=[IN_HBM, IN_HBM],
        out_shape=[_hbm_like(pair), _hbm_like(parts)], input_output_aliases={0: 0, 1: 1},
        compiler_params=pltpu.CompilerParams(has_side_effects=EFFECT), name=name,
    )(pair, parts, send_sems, recv_sems, after)


def _pair_add(core, grad, recv, block, grad_spec, name):
    _, R, C = recv.shape
    tr = block

    def body(c_ref, g_ref, r_ref, o_ref):
        o_ref[...] = (g_ref[...].astype(F32) + r_ref[...].astype(F32)).astype(BF16)

    grid_spec = pltpu.PrefetchScalarGridSpec(
        num_scalar_prefetch=1, grid=(N_CHIP, R // tr),
        in_specs=[grad_spec, pl.BlockSpec((None, tr, C), lambda k, i, c: (k, i, 0))],
        out_specs=pl.BlockSpec((None, tr, C), lambda k, i, c: (k, i, 0)))
    return _pcall(body, grid_spec=grid_spec, out_shape=jax.ShapeDtypeStruct(recv.shape, BF16),
                  compiler_params=_params("parallel", "parallel"), name=name)(core, grad, recv)


def _small_copies(gath, send_sems, recv_sems):
    x, y, c = _place()
    peers = [(x, y, 1 - c)] + [(px, py, pc) for px, py in _other_chips(x, y) for pc in (c, 1 - c)]
    pairs = []
    for a, ref in enumerate(gath):
        mine = ref.at[4 * x + 2 * y + c]
        for k, (px, py, pc) in enumerate(peers):
            sems = dict(send_sem=send_sems.at[7 * a + k], recv_sem=recv_sems.at[7 * a + k], device_id=(px, py, pc),
                        device_id_type=MESH)
            pairs.append((pltpu.make_async_remote_copy(src_ref=mine, dst_ref=mine, **sems),
                          pltpu.make_async_remote_copy(src_ref=mine, dst_ref=ref.at[4 * px + 2 * py + pc], **sems)))
    return pairs


def _small_start(landing, after, name):
    na = len(landing)

    def body(*refs):
        for send, _ in _small_copies(refs[:na], refs[na + 1], refs[na + 2]):
            send.start()
        refs[-1][...] = jnp.zeros_like(refs[-1])

    sems = pltpu.SemaphoreType.DMA((7 * na,))
    outs = _pcall(
        body, in_specs=[IN_HBM] * na + [HBM], out_specs=[SEM, SEM] + [IN_HBM] * na + [TOKEN_SPEC],
        out_shape=[sems, sems] + [_hbm_like(s) for s in landing] + [TOKEN],
        input_output_aliases={a: 2 + a for a in range(na)},
        compiler_params=pltpu.CompilerParams(has_side_effects=EFFECT), name=name,
    )(*[_in_hbm(s) for s in landing], after)
    return outs[0], outs[1], outs[2:2 + na], outs[-1]


def _small_wait(gathered, send_sems, recv_sems, after, name):
    na = len(gathered)

    def body(*refs):
        for send, arrival in _small_copies(refs[:na], refs[na], refs[na + 1]):
            send.wait_send()
            arrival.wait_recv()

    return list(_pcall(
        body, in_specs=[IN_HBM] * na + [SEM, SEM, HBM], out_specs=[IN_HBM] * na,
        out_shape=[_hbm_like(g) for g in gathered], input_output_aliases={a: a for a in range(na)},
        compiler_params=pltpu.CompilerParams(has_side_effects=EFFECT), name=name,
    )(*gathered, send_sems, recv_sems, after))


def _small_finish(gathered, params, name):
    na, npar = len(gathered), len(params)

    def body(*refs):
        g_refs, wmv = refs[:na], refs[na:na + 3 * npar]
        o_sums, o_params = refs[na + 3 * npar:2 * na + 3 * npar], refs[2 * na + 3 * npar:]
        sums = []
        for a in range(na):
            acc = g_refs[a][0]
            for k in range(1, N_DEV):
                acc = acc + g_refs[a][k]
            o_sums[a][...] = acc
            sums.append(acc)
        for j, (a, row, _, _, _) in enumerate(params):
            g = sums[a][row:row + 1, :]
            d, mn, vn = _adam_math(wmv[3 * j][...], g, wmv[3 * j + 1][...], wmv[3 * j + 2][...])
            for out, val in zip(o_params[4 * j:4 * j + 4], (g, d, mn, vn)):
                out[...] = val

    vm = pl.BlockSpec(memory_space=pltpu.VMEM)
    flat = [t for p in params for t in p[2:]]
    out_shape = [jax.ShapeDtypeStruct(g.shape[1:], F32) for g in gathered]
    out_shape += [jax.ShapeDtypeStruct(p[2].shape, F32) for p in params for _ in range(4)]
    outs = _pcall(body, in_specs=[vm] * (na + 3 * npar), out_specs=[vm] * len(out_shape), out_shape=out_shape,
                  name=name)(*gathered, *flat)
    return outs[:na], [outs[na + 4 * j:na + 4 * j + 4] for j in range(npar)]


def _local_step(x, tgt, gains, weights):
    g_pre_mix, g_post_mix, g_pre_ffn, g_post_ffn, g_sb, g_dil = gains
    S, D = x.shape
    hs = g_sb.shape[1] // HEAD_DIM
    hd = g_dil.shape[1] // HEAD_DIM
    cos2, sin_signed = _rope_tables(S)

    h1 = _rms_fwd(x, g_pre_mix + weights.start(), "rms_in")
    w_in_g = weights.w_in(h1)
    proj = _mm_nn(h1, w_in_g, F32, "proj", tn=768)
    o_sb, ct_sb, mx_sb = _sb_fwd(proj, g_sb, hs, "sb_fwd")
    o_dl, lse_dl, mx_dl = _dil_fwd(proj, cos2, sin_signed, g_dil + weights.forward_out(o_sb), 3 * hs, hd, "dil_fwd")
    w_out_g, dep = weights.w_out(o_dl)
    mixed = jnp.concatenate([mx_sb, mx_dl], axis=1)
    mix = _mm_nn(mixed, w_out_g, F32, "mix_out", tn=1024)
    x2, h2 = _mid_fwd(x, mix, g_post_mix + dep, g_pre_ffn, "mid_fwd")
    w_up_g, cwb = weights.w_up(h2)
    u = _mm_nn(h2, w_up_g, BF16, "ffn_up", b_transposed=True)
    y = _geglu_fwd(u, cwb + weights.forward_down(u), "geglu_fwd")
    w_down_g = weights.w_down(y)
    f = _mm_nn(y, w_down_g, F32, "ffn_down", tn=1024, tk=2816)

    dy, df, dg_post_ffn, loss = _loss_bwd(x2, f, tgt, g_post_ffn, "loss_bwd")
    dyv = _mm_nt(df, w_down_g, BF16, "d_y", tn=1408)
    dw_down = _mm_tn(y, df, D, BF16, "dw_down", tm=1408, tn=1024)
    dc, dcw_g, dcw_v = _geglu_bwd(u, dyv, cwb + weights.grad("w_down", dw_down), "geglu_bwd")
    du = _conv_bwd(dc, cwb + weights.grad_reduce("w_down", dc), "conv_bwd")
    dh2 = _mm_nt(du, w_up_g, F32, "d_h2", tk=1408, b_transposed=True, per_step=2)
    dw_up = _mm_tn(du, h2, D, BF16, "dw_up", tm=1408, tn=1024)
    dx2, dmix, dg_pre_ffn, dg_post_mix = _mid_bwd(
        dy, dh2, x2, mix, g_pre_ffn + weights.grad("w_up", dw_up), g_post_mix, "mid_bwd")
    dmixed = _mm_nt(dmix, w_out_g, F32, "d_mixed", after=jnp.reshape(weights.grad_reduce("w_up", dmix), (1, 1)))
    dw_out = _mm_tn(mixed, dmix, D, BF16, "dw_out", tn=1024)
    dq_s, dk_s, dv_s, dg_sb = _sb_bwd(proj, g_sb + weights.grad("w_out", dw_out), o_sb, ct_sb, dmixed, 0, hs, "sb_bwd")
    dq_d, dk_d, dv_d, dg_dil = _dil_bwd(proj, cos2, sin_signed, g_dil + weights.grad_reduce("w_out", dq_s), o_dl, lse_dl,
                                        dmixed, hs, 3 * hs, hd, "dil_bwd")
    dproj = jnp.concatenate([dq_s, dk_s, dv_s, dq_d, dk_d, dv_d], axis=1)
    dw_in = _mm_tn(h1, dproj, w_in_g.shape[2], BF16, "dw_in", tn=768)
    weights.grad("w_in", dw_in)
    dep = weights.grad_reduce("w_in", dproj)
    dh1 = _mm_nt(dproj, w_in_g, F32, "d_h1", tk=768, after=jnp.reshape(dep, (1, 1)), per_step=4)
    grad_x, dg_pre_mix = _first_bwd(dx2, dh1, x, g_pre_mix, "first_bwd")
    small = (dg_pre_mix, dg_post_mix, dg_pre_ffn, dg_post_ffn, dg_sb[0:1], dg_dil[0:1], jnp.concatenate([dcw_g, dcw_v], axis=1))
    weights.small(small, loss)
    return loss, grad_x, small


def _pad_cols(a, to):
    return jnp.pad(a, ((0, 0), (0, to - a.shape[1])))


def kernel(x, pre_mix_gain, post_mix_gain, pre_ffn_gain, post_ffn_gain, w_in, sb_out_gain, dil_out_gain, w_out, w_up, conv_w, conv_b, w_down, loss_target, m_pre_mix_gain, m_post_mix_gain, m_pre_ffn_gain, m_post_ffn_gain, m_w_in, m_sb_out_gain, m_dil_out_gain, m_w_out, m_w_up, m_conv_w, m_conv_b, m_w_down, v_pre_mix_gain, v_post_mix_gain, v_pre_ffn_gain, v_post_ffn_gain, v_w_in, v_sb_out_gain, v_dil_out_gain, v_w_out, v_w_up, v_conv_w, v_conv_b, v_w_down):
    xb, tb = x[0], loss_target[0]
    S, D = xb.shape
    w_in, w_out, w_up, w_down, conv_w = w_in[0], w_out[0], w_up[0], w_down[0], conv_w[0]
    n_in, e_rows = w_in.shape[1], w_out.shape[0]
    cu, half = w_up.shape[1], w_down.shape[0]
    assert cu == 2 * half and half % 16 == 0
    cup = -(-cu // LANES) * LANES
    fp = N_CHIP * cup
    px, py, pc = _place()
    me = 4 * px + 2 * py + pc
    core = jnp.reshape(pc, (1,)).astype(jnp.int32)

    w_up_t, m_up_t, v_up_t = (jnp.swapaxes(t, 0, 1) for t in (w_up, m_w_up[0], v_w_up[0]))

    def by_dev(ref, qx, qy, qc):
        return ref.at[4 * qx + 2 * qy + qc]

    def down_slot(ref, qx, qy, qc):
        return ref.at[2 * qx + qy, pl.ds(qc * half, half)]

    def by_pair(ref, chip, k):
        return ref.at[chip, k]

    def down_pair(ref, chip, k):
        return ref.at[chip, pl.ds(k * half, half)]

    def pair_spec(tr, cols):
        return pl.BlockSpec((None, None, tr, cols), lambda k, i, c: (k, c[0], i, 0))

    tr_in, tr_up = _tile(D, 512, 16), _tile(cup, 256, 16)
    grad_plan = {
        "w_in": ((N_CHIP, 2, D, n_in), by_pair, (D, n_in), tr_in, pair_spec(tr_in, n_in)),
        "w_out": ((N_CHIP, 2, e_rows, D), by_pair, (e_rows, D), e_rows, pair_spec(e_rows, D)),
        "w_up": ((N_CHIP, 2, cup, D), by_pair, (cup, D), tr_up, pair_spec(tr_up, D)),
        "w_down": ((N_CHIP, cup, D), down_pair, (half, D), half,
                   pl.BlockSpec((None, half, D), lambda k, i, c: (k, c[0], 0))),
    }

    class Exchanges:
        def __init__(self):
            self.in_flight = {}

        def start(self):
            def own_slot(shard):
                return lax.dynamic_update_index_in_dim(lax.empty((N_DEV, *shard.shape), shard.dtype), shard, me, 0)

            self.g_in = _gather_start([own_slot(w_in.astype(BF16))], [by_dev], core, "gather_in_start")
            zero = self.g_in[3][0, 0]
            self.g_out = _gather_start([own_slot((w_out + zero).astype(BF16))], [by_dev], self.g_in[3], "gather_out_start")
            up = jnp.pad(w_up_t + zero, ((0, cup - cu), (0, 0))).astype(BF16)
            taps = jnp.pad(conv_w + zero, ((0, 8 - conv_w.shape[0]), (0, cup - cu)))
            self.g_up = _gather_start([own_slot(up), own_slot(taps)], [by_dev, by_dev], self.g_out[3], "gather_up_start")
            down = lax.dynamic_update_slice(jnp.zeros((N_CHIP, cup, D), BF16), (w_down + zero).astype(BF16)[None],
                                            (2 * px + py, pc * half, 0))
            self.g_down = _gather_start([down], [down_slot], self.g_up[3], "gather_down_start")
            return self.g_down[3][0, 0]

        def w_in(self, after):
            send, recv, gath, _ = self.g_in
            fsend, frecv, gath, token = _gather_forward(gath, send, recv, [by_dev], after, "gather_in_forward")
            return _gather_finish(gath, fsend, frecv, [by_dev], token, "gather_in_finish")[0]

        def forward_out(self, after):
            send, recv, gath, _ = self.g_out
            self.p_out = _gather_forward(gath, send, recv, [by_dev], after, "gather_out_forward")
            return self.p_out[3][0, 0]

        def w_out(self, after):
            fsend, frecv, gath, _ = self.p_out
            w_out_g = _gather_finish(gath, fsend, frecv, [by_dev], after, "gather_out_finish")[0]
            send, recv, gath, _ = self.g_up
            self.p_up = _gather_forward(gath, send, recv, [by_dev, by_dev], w_out_g, "gather_up_forward")
            return w_out_g.reshape(1, N_DEV * e_rows, D), self.p_up[3][0, 0]

        def w_up(self, after):
            fsend, frecv, gath, _ = self.p_up
            w_up_g, cw_g = _gather_finish(gath, fsend, frecv, [by_dev, by_dev], after, "gather_up_finish")
            cb = _pad_cols(conv_b.reshape(N_DEV, cu), cup).reshape(1, 2 * fp)
            cw_full = jnp.transpose(cw_g[:, :3, :], (1, 0, 2)).reshape(3, 2 * fp)
            cwb = jnp.concatenate([cw_full, cb, jnp.zeros((4, 2 * fp), F32)], axis=0)
            return w_up_g, cwb

        def forward_down(self, after):
            send, recv, gath, _ = self.g_down
            self.p_down = _gather_forward(gath, send, recv, [down_slot], after, "gather_down_forward")
            return self.p_down[3][0, 0]

        def w_down(self, after):
            fsend, frecv, gath, _ = self.p_down
            return _gather_finish(gath, fsend, frecv, [down_slot], after, "gather_down_finish")[0].reshape(1, fp, D)

        def small(self, small, loss):
            d_pre_mix, d_post_mix, d_pre_ffn, d_post_ffn, d_sb, d_dil, d_conv = small

            def rows_of(*vectors):
                n = vectors[0].shape[1]
                row = lax.broadcasted_iota(jnp.int32, (8, n), 0)
                out = jnp.zeros((8, n), F32)
                for k, vec in enumerate(vectors):
                    out = jnp.where(row == k, vec, out)
                return out

            parts = [rows_of(d_pre_mix, d_post_mix, d_pre_ffn, d_post_ffn, jnp.broadcast_to(loss[:, :1], (1, D))),
                     rows_of(d_sb, d_dil), d_conv]
            landing = [lax.dynamic_update_index_in_dim(lax.empty((N_DEV, *p.shape), F32), p, me, 0) for p in parts]
            self.small_flight = _small_start(landing, parts[0], "small_start")

        def small_sums(self, after):
            send, recv, gath, _ = self.small_flight
            gath = _small_wait(gath, send, recv, after, "small_wait")
            params = [(0, 0, pre_mix_gain, m_pre_mix_gain, v_pre_mix_gain), (0, 1, post_mix_gain, m_post_mix_gain, v_post_mix_gain),
                      (0, 2, pre_ffn_gain, m_pre_ffn_gain, v_pre_ffn_gain), (0, 3, post_ffn_gain, m_post_ffn_gain, v_post_ffn_gain),
                      (1, 0, sb_out_gain, m_sb_out_gain, v_sb_out_gain), (1, 1, dil_out_gain, m_dil_out_gain, v_dil_out_gain)]
            (gains_sum, _, conv_sum), gain_steps = _small_finish(gath, params, "small_finish")
            return gains_sum[4, 0], conv_sum, gain_steps

        def grad(self, name, dw):
            view_shape, view, block, tr, spec = grad_plan[name]
            send, recv_sems, dw, recv, token = _pair_start(dw.reshape(view_shape), view, block, core, "pair_start_" + name)
            self.in_flight[name] = (dw, recv, send, recv_sems)
            return token[0, 0]

        def grad_reduce(self, name, after):
            _, view, _, tr, spec = grad_plan[name]
            dw, recv = _pair_wait(*self.in_flight[name], view, after, "pair_wait_" + name)
            pair = _pair_add(core, dw, recv, tr, spec, "pair_add_" + name)
            send, recv_sems, pair, parts, token = _chip_start(pair, recv, "chip_start_" + name)
            self.in_flight[name] = (pair, parts, send, recv_sems)
            self.last_token = token
            return token[0, 0]

        def grad_parts(self, name, after):
            return _chip_wait(*self.in_flight[name], after, "chip_wait_" + name)

    exchanges = Exchanges()
    gains = (pre_mix_gain, post_mix_gain, pre_ffn_gain, post_ffn_gain, sb_out_gain, dil_out_gain)
    loss, grad_x, small = _local_step(xb, tb, gains, exchanges)

    def small_adam(w, g, m, v, name):
        one = w.shape[0] == 1
        if one:
            w, g, m, v = (jnp.broadcast_to(t, (8, t.shape[1])) for t in (w, g, m, v))
        outs = _adamw(w, g[None], m, v, name)
        return [o[0:1] for o in outs] if one else outs

    chip_ids = jnp.stack([2 * px + py, 2 * (1 - px) + py, 2 * px + 1 - py, 2 * (1 - px) + 1 - py]).astype(jnp.int32)
    out_w_down = _adamw_chips(w_down, *exchanges.grad_parts("w_down", exchanges.small_flight[3]), chip_ids, m_w_down[0], v_w_down[0], "adam_w_down")
    out_up_t = _adamw_chips(w_up_t, *exchanges.grad_parts("w_up", out_w_down[1]), chip_ids, m_up_t, v_up_t, "adam_w_up")
    out_w_up = [jnp.swapaxes(o, 0, 1) for o in out_up_t]
    out_w_out = _adamw_chips(w_out, *exchanges.grad_parts("w_out", out_up_t[1]), chip_ids, m_w_out[0], v_w_out[0], "adam_w_out")
    loss_out, g_conv, gain_steps = exchanges.small_sums(out_w_out[1])
    out_pre_mix, out_post_mix, out_pre_ffn, out_post_ffn, out_sb, out_dil = gain_steps
    g_conv_b = g_conv[3].reshape(N_DEV, cup)[:, :cu].reshape(1, N_DEV * cu)
    g_conv_w = lax.dynamic_index_in_dim(g_conv[0:3].reshape(3, N_DEV, cup), me, axis=1, keepdims=False)[:, :cu]
    out_conv_b = small_adam(conv_b, g_conv_b, m_conv_b, v_conv_b, "adam_conv_b")
    cw8 = [jnp.pad(t, ((0, 5), (0, 0))) for t in (conv_w, g_conv_w, m_conv_w[0], v_conv_w[0])]
    out_conv_w = [o[0:3] for o in _adamw(cw8[0], cw8[1][None], cw8[2], cw8[3], "adam_conv_w")]
    out_w_in = _adamw_chips(w_in, *exchanges.grad_parts("w_in", out_conv_w[1]), chip_ids, m_w_in[0], v_w_in[0], "adam_w_in")

    order = [out_pre_mix, out_post_mix, out_pre_ffn, out_post_ffn, [o[None] for o in out_w_in], out_sb, out_dil,
             [o[None] for o in out_w_out], [o[None] for o in out_w_up], [o[None] for o in out_conv_w], out_conv_b,
             [o[None] for o in out_w_down]]
    outs = [loss_out, grad_x[None]]
    for k in range(4):
        outs += [o[k] for o in order]
    return tuple(outs)
```

```python
import functools
import math

import jax
import jax.numpy as jnp
from jax import lax
from jax.experimental import pallas as pl
from jax.experimental.pallas import tpu as pltpu

F32 = jnp.float32
BF16 = jnp.bfloat16
HEAD_DIM = 128
LANES = 128
KEY_BLOCK = 128
DILATIONS = (1, 4, 16)
RMS_EPS = 1e-6
ROPE_THETA = 10000.0
NEG = -1e30
ADAM_LR, ADAM_B1, ADAM_B2, ADAM_EPS, ADAM_WD, ADAM_STEP = 0.001, 0.9, 0.999, 1e-08, 0.01, 10
MESH = pl.DeviceIdType.MESH
N_DEV = 8
N_CHIP = 4
HBM = pl.BlockSpec(memory_space=pl.ANY)
VMEM_LIMIT = 56 * 1024 * 1024

_pcall = pl.pallas_call


def _tile(n, pref, mult=LANES):
    best = None
    t = mult
    while t <= min(n, pref):
        if n % t == 0:
            best = t
        t += mult
    return n if best is None else best


def _params(*sem):
    return pltpu.CompilerParams(dimension_semantics=sem, vmem_limit_bytes=VMEM_LIMIT)


def _dot(a, b, dims):
    return lax.dot_general(a, b, (dims, ((), ())), preferred_element_type=F32)


NN = ((1,), (0,))
NT = ((1,), (1,))
TN = ((0,), (0,))


def _mm_body(dims, nk, tile):
    if nk == 1:
        def single(a_ref, b_ref, o_ref):
            o_ref[...] = _dot(a_ref[...].astype(BF16), b_ref[...].astype(BF16), dims).astype(o_ref.dtype)

        return single, []

    def body(a_ref, b_ref, o_ref, acc_ref):
        k = pl.program_id(2)

        @pl.when(k == 0)
        def _():
            acc_ref[...] = jnp.zeros_like(acc_ref)

        acc_ref[...] += _dot(a_ref[...].astype(BF16), b_ref[...].astype(BF16), dims)

        @pl.when(k == nk - 1)
        def _():
            o_ref[...] = acc_ref[...].astype(o_ref.dtype)

    return body, [pltpu.VMEM(tile, F32)]


def _mm_nn(a, b3, out_dtype, name, tm=1024, tn=1408, tk=2048, b_transposed=False):
    M, K = a.shape
    C, n = b3.shape[0], b3.shape[1 if b_transposed else 2]
    tm, tk, tn = _tile(M, tm, 8), _tile(K, tk), _tile(n, tn)
    npc, nk = n // tn, K // tk
    body, scratch = _mm_body(NT if b_transposed else NN, nk, (tm, tn))
    b_spec = (pl.BlockSpec((None, tn, tk), lambda i, j, k: (j // npc, j % npc, k)) if b_transposed
              else pl.BlockSpec((None, tk, tn), lambda i, j, k: (j // npc, k, j % npc)))
    return _pcall(
        body, grid=(M // tm, C * npc, nk),
        in_specs=[pl.BlockSpec((tm, tk), lambda i, j, k: (i, k)), b_spec],
        out_specs=pl.BlockSpec((tm, tn), lambda i, j, k: (i, j)),
        out_shape=jax.ShapeDtypeStruct((M, C * n), out_dtype), scratch_shapes=scratch,
        compiler_params=_params("parallel", "parallel", "arbitrary"), name=name)(a, b3)


def _mm_nt(a, b3, out_dtype, name, tm=1024, tn=1024, tk=2048, after=None, b_transposed=False, per_step=1):
    M, _ = a.shape
    C, N, n = (b3.shape[0], b3.shape[2], b3.shape[1]) if b_transposed else b3.shape
    tm, tn, tk = _tile(M, tm, 8), _tile(N, tn), _tile(n, tk)
    dims = NN if b_transposed else NT
    extra = [] if after is None else [after]
    if per_step > 1 and tk == n and C % per_step == 0:
        nk, scratch = C // per_step, [pltpu.VMEM((tm, tn), F32)]
        b3 = b3.reshape(nk, per_step, *b3.shape[1:])
        a_spec = pl.BlockSpec((tm, per_step * n), lambda i, j, k: (i, k))
        if b_transposed:
            b_spec = pl.BlockSpec((None, per_step, n, tn), lambda i, j, k: (k, 0, 0, j))
        else:
            b_spec = pl.BlockSpec((None, per_step, tn, n), lambda i, j, k: (k, 0, j, 0))

        def body(a_ref, b_ref, *rest):
            o_ref, acc_ref = rest[len(extra):]
            k = pl.program_id(2)

            @pl.when(k == 0)
            def _():
                acc_ref[...] = jnp.zeros_like(acc_ref)

            b = b_ref[...].astype(BF16)
            b = b.reshape(per_step * n, tn) if b_transposed else jnp.concatenate([b[u] for u in range(per_step)], axis=1)
            acc_ref[...] += _dot(a_ref[...].astype(BF16), b, dims)

            @pl.when(k == nk - 1)
            def _():
                o_ref[...] = acc_ref[...].astype(o_ref.dtype)
    else:
        kpc = n // tk
        nk = C * kpc
        inner, scratch = _mm_body(dims, nk, (tm, tn))
        a_spec = pl.BlockSpec((tm, tk), lambda i, j, k: (i, k))
        b_spec = (pl.BlockSpec((None, tk, tn), lambda i, j, k: (k // kpc, k % kpc, j)) if b_transposed
                  else pl.BlockSpec((None, tn, tk), lambda i, j, k: (k // kpc, j, k % kpc)))

        def body(a_ref, b_ref, *rest):
            inner(a_ref, b_ref, *rest[len(extra):])

    return _pcall(
        body, grid=(M // tm, N // tn, nk), in_specs=[a_spec, b_spec] + [HBM] * len(extra),
        out_specs=pl.BlockSpec((tm, tn), lambda i, j, k: (i, j)),
        out_shape=jax.ShapeDtypeStruct((M, N), out_dtype), scratch_shapes=scratch,
        compiler_params=_params("parallel", "parallel", "arbitrary"), name=name)(a, b3, *extra)


def _mm_tn(x, y, n, out_dtype, name, tm=1024, tn=1408, tk=2048, after=None):
    S, P = x.shape
    C = y.shape[1] // n
    tm, tn, tk = _tile(P, tm), _tile(n, tn), _tile(S, tk, 8)
    npc, nk = n // tn, S // tk
    inner, scratch = _mm_body(TN, nk, (tm, tn))
    extra = [] if after is None else [after]

    def body(x_ref, y_ref, *rest):
        inner(x_ref, y_ref, *rest[len(extra):])

    return _pcall(
        body, grid=(P // tm, C * npc, nk),
        in_specs=[pl.BlockSpec((tk, tm), lambda i, j, k: (k, i)),
                  pl.BlockSpec((tk, tn), lambda i, j, k: (k, j))] + [HBM] * len(extra),
        out_specs=pl.BlockSpec((None, tm, tn), lambda i, j, k: (j // npc, i, j % npc)),
        out_shape=jax.ShapeDtypeStruct((C, P, n), out_dtype), scratch_shapes=scratch,
        compiler_params=_params("parallel", "parallel", "arbitrary"), name=name)(x, y, *extra)


def _rms_scale(v):
    return lax.rsqrt(jnp.mean(v * v, axis=-1, keepdims=True) + RMS_EPS)


def _rms_bwd(gy, v, r):
    return r * gy - v * (r * r * r * jnp.mean(gy * v, axis=-1, keepdims=True))


def _rows_spec(tm, d):
    return pl.BlockSpec((tm, d), lambda i: (i, 0))


def _vec_spec(d):
    return pl.BlockSpec((1, d), lambda i: (0, 0))


def _rms_fwd(x, g, name, tm=256):
    S, D = x.shape

    def body(x_ref, g_ref, h_ref):
        v = x_ref[...]
        h_ref[...] = (v * _rms_scale(v) * g_ref[...]).astype(BF16)

    return _pcall(body, grid=(S // tm,), in_specs=[_rows_spec(tm, D), _vec_spec(D)], out_specs=_rows_spec(tm, D),
                  out_shape=jax.ShapeDtypeStruct((S, D), BF16), compiler_params=_params("parallel"), name=name)(x, g)


def _mid_fwd(x, mix, g_post, g_pre, name, tm=256):
    S, D = x.shape

    def body(x_ref, m_ref, gp_ref, gn_ref, x2_ref, h_ref):
        m = m_ref[...]
        x2 = x_ref[...] + m * _rms_scale(m) * gp_ref[...]
        x2_ref[...] = x2
        h_ref[...] = (x2 * _rms_scale(x2) * gn_ref[...]).astype(BF16)

    return _pcall(body, grid=(S // tm,), in_specs=[_rows_spec(tm, D), _rows_spec(tm, D), _vec_spec(D), _vec_spec(D)],
                  out_specs=[_rows_spec(tm, D), _rows_spec(tm, D)],
                  out_shape=[jax.ShapeDtypeStruct((S, D), F32), jax.ShapeDtypeStruct((S, D), BF16)],
                  compiler_params=_params("parallel"), name=name)(x, mix, g_post, g_pre)


def _loss_bwd(x2, f, tgt, g_post, name, tm=256):
    S, D = x2.shape

    def body(x2_ref, f_ref, t_ref, g_ref, dy_ref, df_ref, dg_ref, ls_ref):
        i = pl.program_id(0)

        @pl.when(i == 0)
        def _():
            dg_ref[...] = jnp.zeros_like(dg_ref)
            ls_ref[...] = jnp.zeros_like(ls_ref)

        fv = f_ref[...]
        r = _rms_scale(fv)
        g = g_ref[...]
        err = x2_ref[...] + fv * r * g - t_ref[...]
        ls_ref[...] += jnp.broadcast_to(0.5 * jnp.sum(jnp.mean(err * err, axis=-1, keepdims=True), axis=0, keepdims=True), ls_ref.shape)
        dy = err * (1.0 / D)
        dy_ref[...] = dy
        df_ref[...] = _rms_bwd(dy * g, fv, r).astype(BF16)
        dg_ref[...] += jnp.sum(dy * fv * r, axis=0, keepdims=True)

    return _pcall(body, grid=(S // tm,),
                  in_specs=[_rows_spec(tm, D), _rows_spec(tm, D), _rows_spec(tm, D), _vec_spec(D)],
                  out_specs=[_rows_spec(tm, D), _rows_spec(tm, D), _vec_spec(D), _vec_spec(LANES)],
                  out_shape=[jax.ShapeDtypeStruct((S, D), F32), jax.ShapeDtypeStruct((S, D), BF16),
                             jax.ShapeDtypeStruct((1, D), F32), jax.ShapeDtypeStruct((1, LANES), F32)],
                  compiler_params=_params("arbitrary"), name=name)(x2, f, tgt, g_post)


def _mid_bwd(dy, dh2, x2, mix, g_pre, g_post, name, tm=256):
    S, D = dy.shape

    def body(dy_ref, dh_ref, x2_ref, m_ref, gn_ref, gp_ref, dx2_ref, dm_ref, dgn_ref, dgp_ref):
        i = pl.program_id(0)

        @pl.when(i == 0)
        def _():
            dgn_ref[...] = jnp.zeros_like(dgn_ref)
            dgp_ref[...] = jnp.zeros_like(dgp_ref)

        x2, dh = x2_ref[...], dh_ref[...].astype(F32)
        r = _rms_scale(x2)
        dx2 = dy_ref[...] + _rms_bwd(dh * gn_ref[...], x2, r)
        dgn_ref[...] += jnp.sum(dh * x2 * r, axis=0, keepdims=True)
        dx2_ref[...] = dx2
        m = m_ref[...]
        rm = _rms_scale(m)
        dm_ref[...] = _rms_bwd(dx2 * gp_ref[...], m, rm).astype(BF16)
        dgp_ref[...] += jnp.sum(dx2 * m * rm, axis=0, keepdims=True)

    return _pcall(body, grid=(S // tm,),
                  in_specs=[_rows_spec(tm, D)] * 4 + [_vec_spec(D)] * 2,
                  out_specs=[_rows_spec(tm, D), _rows_spec(tm, D), _vec_spec(D), _vec_spec(D)],
                  out_shape=[jax.ShapeDtypeStruct((S, D), F32), jax.ShapeDtypeStruct((S, D), BF16),
                             jax.ShapeDtypeStruct((1, D), F32), jax.ShapeDtypeStruct((1, D), F32)],
                  compiler_params=_params("arbitrary"), name=name)(dy, dh2, x2, mix, g_pre, g_post)


def _first_bwd(dx2, dh1, x, g_pre, name, tm=256):
    S, D = x.shape

    def body(dx2_ref, dh_ref, x_ref, g_ref, gx_ref, dg_ref):
        i = pl.program_id(0)

        @pl.when(i == 0)
        def _():
            dg_ref[...] = jnp.zeros_like(dg_ref)

        xv, dh = x_ref[...], dh_ref[...].astype(F32)
        r = _rms_scale(xv)
        gx_ref[...] = dx2_ref[...] + _rms_bwd(dh * g_ref[...], xv, r)
        dg_ref[...] += jnp.sum(dh * xv * r, axis=0, keepdims=True)

    return _pcall(body, grid=(S // tm,), in_specs=[_rows_spec(tm, D)] * 3 + [_vec_spec(D)],
                  out_specs=[_rows_spec(tm, D), _vec_spec(D)],
                  out_shape=[jax.ShapeDtypeStruct((S, D), F32), jax.ShapeDtypeStruct((1, D), F32)],
                  compiler_params=_params("arbitrary"), name=name)(dx2, dh1, x, g_pre)


def _logsig_pair(z):
    lb = jnp.minimum(z, 0.0) - jnp.log(1.0 + jnp.exp(-jnp.abs(z)))
    return lb, lb - z


SB_KEY_BLOCK = 256


def _sum_matrix(strict):
    ia = lax.broadcasted_iota(jnp.int32, (SB_KEY_BLOCK, SB_KEY_BLOCK), 0)
    ib = lax.broadcasted_iota(jnp.int32, (SB_KEY_BLOCK, SB_KEY_BLOCK), 1)
    return ((ia > ib) if strict == ">" else (ia < ib)).astype(BF16)


def _row_total(sums, v, col):
    return jnp.broadcast_to(sums[:, col:col + 1] + v[:, col:col + 1], (v.shape[0], LANES))


def _lanes(c, width):
    return jnp.tile(c, (1, width // LANES))


def _split_dot(v, u):
    hi = v.astype(BF16)
    lo = (v - hi.astype(F32)).astype(BF16)
    return _dot(hi, u, NN) + _dot(lo, u, NN)


def _head_out(o, g):
    return o * _rms_scale(o) * g


def _sb_fwd(proj, gain, n_heads, name, tq=1024):
    S = proj.shape[0]
    H, tk = n_heads, SB_KEY_BLOCK
    tq = _tile(S, tq, 2 * tk)
    scale = HEAD_DIM ** -0.5

    def body(q_ref, k_ref, v_ref, g_ref, o_ref, ct_ref, mx_ref, oacc, cacc):
        i = pl.program_id(1)
        oacc[...] = jnp.zeros_like(oacc)
        cacc[...] = jnp.zeros_like(cacc)
        sums = _sum_matrix(">")

        def run(blocks):
            scored = []
            for k0, r0, diagonal in blocks:
                rows = pl.ds(r0, tq - r0)
                lb, lk = _logsig_pair(_dot(q_ref[rows, :].astype(BF16), k_ref[pl.ds(k0, tk), :].astype(BF16), NT) * scale)
                causal = None
                if diagonal:
                    causal = (lax.broadcasted_iota(jnp.int32, (tq - r0, tk), 1)
                              < lax.broadcasted_iota(jnp.int32, (tq - r0, tk), 0))
                    lk = jnp.where(causal, lk, 0.0)
                scored.append((k0, rows, causal, lb, lk))
            summed = [(k0, rows, causal, lb, lk, _split_dot(lk, sums)) for k0, rows, causal, lb, lk in scored]
            weights = []
            for k0, rows, causal, lb, lk, after in summed:
                c = cacc[rows, :]
                a = jnp.exp(lb + after + _lanes(c, tk))
                if causal is not None:
                    a = jnp.where(causal, a, 0.0)
                cacc[rows, :] = c + _row_total(after, lk, 0)
                weights.append((k0, rows, a.astype(BF16)))
            for k0, rows, a in weights:
                oacc[rows, :] += _dot(a, v_ref[pl.ds(k0, tk), :].astype(BF16), NN)

        for d in reversed(range(0, tq // tk, 2)):
            run([(pl.multiple_of(i * tq + e * tk, tk), e * tk, True) for e in (d + 1, d)])
        per_trip = tq // tk

        def step(it, carry):
            k0 = pl.multiple_of((i - 1 - it) * tq, tq)
            run([(pl.multiple_of(k0 + e * tk, tk), 0, False) for e in reversed(range(per_trip))])
            return carry

        lax.fori_loop(0, i, step, 0)
        o = oacc[...]
        o_ref[...] = o
        ct_ref[...] = cacc[...]
        mx_ref[...] = _head_out(o, g_ref[...]).astype(BF16)

    blk = pl.BlockSpec((tq, HEAD_DIM), lambda h, i: (i, h))
    return _pcall(
        body, grid=(H, S // tq),
        in_specs=[blk, pl.BlockSpec((S, HEAD_DIM), lambda h, i: (0, H + h)),
                  pl.BlockSpec((S, HEAD_DIM), lambda h, i: (0, 2 * H + h)), pl.BlockSpec((1, HEAD_DIM), lambda h, i: (0, h))],
        out_specs=[blk, blk, blk],
        out_shape=[jax.ShapeDtypeStruct((S, H * HEAD_DIM), F32), jax.ShapeDtypeStruct((S, H * HEAD_DIM), F32),
                   jax.ShapeDtypeStruct((S, H * HEAD_DIM), BF16)],
        scratch_shapes=[pltpu.VMEM((tq, HEAD_DIM), F32), pltpu.VMEM((tq, LANES), F32)],
        compiler_params=_params("parallel", "arbitrary"), name=name)(proj, proj, proj, gain)


def _sb_bwd(proj, gain, o_raw, ctot, dmixed, dm_col0, n_heads, name, tq=1024):
    S = proj.shape[0]
    H, tk = n_heads, SB_KEY_BLOCK
    tq = _tile(S, tq, 2 * tk)
    nq = S // tq
    scale = HEAD_DIM ** -0.5

    def body(q_ref, k_ref, v_ref, g_ref, o_ref, ct_ref, dm_ref, dq_ref, dk_ref, dv_ref, dg_ref,
             dkacc, dvacc, dqacc, pfx, gcar, dos):
        i = pl.program_id(1)

        @pl.when(i == 0)
        def _():
            dkacc[...] = jnp.zeros_like(dkacc)
            dvacc[...] = jnp.zeros_like(dvacc)
            dg_ref[...] = jnp.zeros_like(dg_ref)

        o, dm, g = o_ref[...], dm_ref[...].astype(F32), g_ref[...]
        r = _rms_scale(o)
        dos[...] = _rms_bwd(dm * g, o, r).astype(BF16)
        dg_ref[...] += jnp.broadcast_to(jnp.sum(dm * o * r, axis=0, keepdims=True), dg_ref.shape)
        dqacc[...] = jnp.zeros_like(dqacc)
        pfx[...] = jnp.zeros_like(pfx)
        gcar[...] = jnp.zeros_like(gcar)
        later, earlier = _sum_matrix(">"), _sum_matrix("<")

        def run(blocks):
            scored = []
            for k0, r0, diagonal in blocks:
                rows, keys = pl.ds(r0, tq - r0), pl.ds(k0, tk)
                lb, lk = _logsig_pair(_dot(q_ref[rows, :].astype(BF16), k_ref[keys, :].astype(BF16), NT) * scale)
                da = _dot(dos[rows, :], v_ref[keys, :].astype(BF16), NT)
                causal = None
                if diagonal:
                    causal = (lax.broadcasted_iota(jnp.int32, (tq - r0, tk), 1)
                              < lax.broadcasted_iota(jnp.int32, (tq - r0, tk), 0))
                    lk = jnp.where(causal, lk, 0.0)
                scored.append((rows, keys, causal, lb, lk, da))
            summed = [(*blk, _split_dot(blk[4], later)) for blk in scored]
            weighted = []
            for rows, keys, causal, lb, lk, da, after in summed:
                p = pfx[rows, :] + _row_total(after, lk, 0)
                pfx[rows, :] = p
                a = jnp.exp(lb + after + _lanes(ct_ref[rows, :] - p, tk))
                if causal is not None:
                    a = jnp.where(causal, a, 0.0)
                dl = da * a
                weighted.append((rows, keys, causal, lb, a.astype(BF16), dl, _dot(dl.astype(BF16), earlier, NN)))
            cotangents = []
            for rows, keys, causal, lb, a, dl, before in weighted:
                gc = gcar[rows, :]
                gcar[rows, :] = gc + _row_total(before, dl, tk - 1)
                sig = jnp.exp(lb)
                gsum = (before + _lanes(gc, tk)) * sig
                if causal is not None:
                    gsum = jnp.where(causal, gsum, 0.0)
                cotangents.append((rows, keys, a, ((dl * (1.0 - sig) - gsum) * scale).astype(BF16)))
            for rows, keys, a, dz in cotangents:
                q, do = q_ref[rows, :].astype(BF16), dos[rows, :]
                dvacc[keys, :] += _dot(a, do, TN)
                dqacc[rows, :] += _dot(dz, k_ref[keys, :].astype(BF16), NN)
                dkacc[keys, :] += _dot(dz, q, TN)

        def step(j, carry):
            k0 = pl.multiple_of(j * 2 * tk, 2 * tk)
            run([(k0, 0, False), (pl.multiple_of(k0 + tk, tk), 0, False)])
            return carry

        lax.fori_loop(0, i * (tq // tk // 2), step, 0)
        for d in range(0, tq // tk, 2):
            run([(pl.multiple_of(i * tq + e * tk, tk), e * tk, True) for e in (d, d + 1)])
        dq_ref[...] = dqacc[...].astype(BF16)

        @pl.when(i == nq - 1)
        def _():
            dk_ref[...] = dkacc[...].astype(BF16)
            dv_ref[...] = dvacc[...].astype(BF16)

    blk = pl.BlockSpec((tq, HEAD_DIM), lambda h, i: (i, h))
    full = pl.BlockSpec((S, HEAD_DIM), lambda h, i: (0, h))
    W = H * HEAD_DIM
    return _pcall(
        body, grid=(H, nq),
        in_specs=[blk, pl.BlockSpec((S, HEAD_DIM), lambda h, i: (0, H + h)),
                  pl.BlockSpec((S, HEAD_DIM), lambda h, i: (0, 2 * H + h)), pl.BlockSpec((1, HEAD_DIM), lambda h, i: (0, h)),
                  blk, blk, pl.BlockSpec((tq, HEAD_DIM), lambda h, i: (i, dm_col0 + h))],
        out_specs=[blk, full, full, pl.BlockSpec((8, HEAD_DIM), lambda h, i: (0, h))],
        out_shape=[jax.ShapeDtypeStruct((S, W), BF16), jax.ShapeDtypeStruct((S, W), BF16),
                   jax.ShapeDtypeStruct((S, W), BF16), jax.ShapeDtypeStruct((8, W), F32)],
        scratch_shapes=[pltpu.VMEM((S, HEAD_DIM), F32), pltpu.VMEM((S, HEAD_DIM), F32), pltpu.VMEM((tq, HEAD_DIM), F32),
                        pltpu.VMEM((tq, LANES), F32), pltpu.VMEM((tq, LANES), F32), pltpu.VMEM((tq, HEAD_DIM), BF16)],
        compiler_params=_params("arbitrary", "arbitrary"), name=name)(proj, proj, proj, gain, o_raw, ctot, dmixed)


def _rope_tables(S):
    inv_freq = ROPE_THETA ** (-jnp.arange(0, HEAD_DIM, 2, dtype=F32) / HEAD_DIM)
    ang = jnp.arange(S, dtype=F32)[:, None] * inv_freq[None, :]
    cos, sin = jnp.cos(ang), jnp.sin(ang)
    return jnp.concatenate([cos, cos], axis=1), jnp.concatenate([-sin, sin], axis=1)


def _rope(v, cos2, sin_signed):
    return v * cos2 + pltpu.roll(v, HEAD_DIM // 2, axis=1) * sin_signed


def _dil_rows(d, r, l0, n):
    if d == 1:
        return pl.ds(l0 if isinstance(l0, int) else pl.multiple_of(l0, KEY_BLOCK), n)
    return pl.ds(r + d * l0, n, stride=d)


def _dil_blocks(S, visit):
    B = KEY_BLOCK
    group = 16
    for b, d in enumerate(DILATIONS):
        nb = S // d // B
        if nb == 1:
            g = math.gcd(d, group)

            def trip(t, carry, b=b, d=d, g=g):
                visit([(b, d, t * g + u, 0, True) for u in range(g)])
                return carry

            lax.fori_loop(0, d // g, trip, 0)
        elif d == 1:
            visit([(b, d, 0, 0, True)])
            g = max(k for k in range(1, group + 2) if (nb - 1) % k == 0)

            def trip(t, carry, b=b, d=d, g=g):
                visit([(b, d, 0, (1 + t * g + u) * B, False) for u in range(g)])
                return carry

            lax.fori_loop(0, (nb - 1) // g, trip, 0)
        else:
            g = math.gcd(d, max(group // nb, 1))

            def trip(t, carry, b=b, d=d, nb=nb, g=g):
                visit([(b, d, t * g + u, n * B, n == 0) for u in range(g) for n in range(nb)])
                return carry

            lax.fori_loop(0, d // g, trip, 0)


def _dil_mask(first):
    B = KEY_BLOCK
    nk = B if first else 2 * B
    iq = lax.broadcasted_iota(jnp.int32, (B, nk), 0)
    ik = lax.broadcasted_iota(jnp.int32, (B, nk), 1)
    return (ik <= iq) if first else ((ik >= iq) & (ik <= iq + B))


def _dil_fwd(proj, cos2, sin_signed, gain, col0, n_heads, name):
    S = proj.shape[0]
    H, B = n_heads, KEY_BLOCK
    scale = HEAD_DIM ** -0.5
    rc = _tile(S, 256, 8)

    def body(q_ref, k_ref, v_ref, c_ref, s_ref, g_ref, o_ref, l_ref, mx_ref, qr, kr, vf, *per_branch):
        ob, lb = per_branch[:len(DILATIONS)], per_branch[len(DILATIONS):]

        def rope_rows(t, carry):
            rows = pl.ds(pl.multiple_of(t * rc, rc), rc)
            qr[rows, :] = _rope(q_ref[rows, :].astype(F32), c_ref[rows, :], s_ref[rows, :])
            kr[rows, :] = _rope(k_ref[rows, :].astype(F32), c_ref[rows, :], s_ref[rows, :])
            vf[rows, :] = v_ref[rows, :].astype(F32)
            return carry

        lax.fori_loop(0, S // rc, rope_rows, 0)

        def visit(blocks):
            scores = []
            for b, d, r, l0, first in blocks:
                qrows = _dil_rows(d, r, l0, B)
                krows = qrows if first else _dil_rows(d, r, l0 - B, 2 * B)
                s = _dot(qr[qrows, :].astype(BF16), kr[krows, :].astype(BF16), NT) * scale
                scores.append((b, qrows, krows, jnp.where(_dil_mask(first), s, NEG)))
            weights = []
            for b, qrows, krows, s in scores:
                m = jnp.max(s, axis=1, keepdims=True)
                p = jnp.exp(s - m)
                den = jnp.sum(p, axis=1, keepdims=True)
                lb[b][qrows, :] = jnp.broadcast_to(m + jnp.log(den), (B, LANES))
                weights.append((b, qrows, krows, p.astype(BF16), den))
            for b, qrows, krows, p, den in weights:
                ob[b][qrows, :] = _dot(p, vf[krows, :].astype(BF16), NN) / den

        _dil_blocks(S, visit)

        def combine(t, carry):
            rows = pl.ds(pl.multiple_of(t * rc, rc), rc)
            l0, l1, l2 = lb[0][rows, :], lb[1][rows, :], lb[2][rows, :]
            m = jnp.maximum(jnp.maximum(l0, l1), l2)
            w0, w1, w2 = jnp.exp(l0 - m), jnp.exp(l1 - m), jnp.exp(l2 - m)
            den = w0 + w1 + w2
            o = (w0 * ob[0][rows, :] + w1 * ob[1][rows, :] + w2 * ob[2][rows, :]) / den
            o_ref[rows, :] = o
            l_ref[rows, :] = m + jnp.log(den)
            mx_ref[rows, :] = _head_out(o, g_ref[...]).astype(BF16)
            return carry

        lax.fori_loop(0, S // rc, combine, 0)

    def col(k):
        return pl.BlockSpec((S, HEAD_DIM), lambda h: (0, col0 + k * H + h))

    tab = pl.BlockSpec((S, HEAD_DIM), lambda h: (0, 0))
    out = pl.BlockSpec((S, HEAD_DIM), lambda h: (0, h))
    W = H * HEAD_DIM
    return _pcall(
        body, grid=(H,),
        in_specs=[col(0), col(1), col(2), tab, tab, pl.BlockSpec((1, HEAD_DIM), lambda h: (0, h))],
        out_specs=[out, out, out],
        out_shape=[jax.ShapeDtypeStruct((S, W), F32), jax.ShapeDtypeStruct((S, W), F32), jax.ShapeDtypeStruct((S, W), BF16)],
        scratch_shapes=[pltpu.VMEM((S, HEAD_DIM), F32)] * (3 + 2 * len(DILATIONS)),
        compiler_params=_params("parallel"), name=name)(proj, proj, proj, cos2, sin_signed, gain)


def _dil_bwd(proj, cos2, sin_signed, gain, o_raw, lse, dmixed, dm_col0, col0, n_heads, name):
    S = proj.shape[0]
    H, B = n_heads, KEY_BLOCK
    scale = HEAD_DIM ** -0.5
    rc = _tile(S, 256, 8)

    def body(q_ref, k_ref, v_ref, c_ref, s_ref, g_ref, o_ref, l_ref, dm_ref, dq_ref, dk_ref, dv_ref, dg_ref,
             qr, kr, vf, dos, dsum, dqr, dkr, dvv):
        dg_ref[...] = jnp.zeros_like(dg_ref)

        def prep(t, carry):
            rows = pl.ds(pl.multiple_of(t * rc, rc), rc)
            qr[rows, :] = _rope(q_ref[rows, :].astype(F32), c_ref[rows, :], s_ref[rows, :])
            kr[rows, :] = _rope(k_ref[rows, :].astype(F32), c_ref[rows, :], s_ref[rows, :])
            vf[rows, :] = v_ref[rows, :].astype(F32)
            o, dm = o_ref[rows, :], dm_ref[rows, :].astype(F32)
            r = _rms_scale(o)
            do = _rms_bwd(dm * g_ref[...], o, r)
            dg_ref[...] += jnp.broadcast_to(jnp.sum(dm * o * r, axis=0, keepdims=True), dg_ref.shape)
            dos[rows, :] = do
            dsum[rows, :] = jnp.broadcast_to(jnp.sum(do * o, axis=1, keepdims=True), (rc, LANES))
            dqr[rows, :] = jnp.zeros((rc, HEAD_DIM), F32)
            dkr[rows, :] = jnp.zeros((rc, HEAD_DIM), F32)
            dvv[rows, :] = jnp.zeros((rc, HEAD_DIM), F32)
            return carry

        lax.fori_loop(0, S // rc, prep, 0)

        def visit(blocks):
            products = []
            for b, d, r, l0, first in blocks:
                qrows = _dil_rows(d, r, l0, B)
                krows = qrows if first else _dil_rows(d, r, l0 - B, 2 * B)
                qs, ks = qr[qrows, :].astype(BF16), kr[krows, :].astype(BF16)
                do = dos[qrows, :].astype(BF16)
                s = jnp.where(_dil_mask(first), _dot(qs, ks, NT) * scale, NEG)
                dp = _dot(do, vf[krows, :].astype(BF16), NT)
                products.append((qrows, krows, qs, ks, do, s, dp))
            cotangents = []
            for qrows, krows, qs, ks, do, s, dp in products:
                p = jnp.exp(s - l_ref[qrows, :][:, 0:1])
                ds = (p * (dp - dsum[qrows, :][:, 0:1]) * scale).astype(BF16)
                cotangents.append((qrows, krows, qs, ks, do, p.astype(BF16), ds))
            for qrows, krows, qs, ks, do, p, ds in cotangents:
                dqr[qrows, :] += _dot(ds, ks, NN)
                dkr[krows, :] += _dot(ds, qs, TN)
                dvv[krows, :] += _dot(p, do, TN)

        _dil_blocks(S, visit)

        def finish(t, carry):
            rows = pl.ds(pl.multiple_of(t * rc, rc), rc)
            c, s = c_ref[rows, :], s_ref[rows, :]
            dq, dk = dqr[rows, :], dkr[rows, :]
            dq_ref[rows, :] = (dq * c + pltpu.roll(dq * s, HEAD_DIM // 2, axis=1)).astype(BF16)
            dk_ref[rows, :] = (dk * c + pltpu.roll(dk * s, HEAD_DIM // 2, axis=1)).astype(BF16)
            dv_ref[rows, :] = dvv[rows, :].astype(BF16)
            return carry

        lax.fori_loop(0, S // rc, finish, 0)

    def col(k):
        return pl.BlockSpec((S, HEAD_DIM), lambda h: (0, col0 + k * H + h))

    tab = pl.BlockSpec((S, HEAD_DIM), lambda h: (0, 0))
    out = pl.BlockSpec((S, HEAD_DIM), lambda h: (0, h))
    W = H * HEAD_DIM
    big = pltpu.VMEM((S, HEAD_DIM), F32)
    return _pcall(
        body, grid=(H,),
        in_specs=[col(0), col(1), col(2), tab, tab, pl.BlockSpec((1, HEAD_DIM), lambda h: (0, h)), out, out,
                  pl.BlockSpec((S, HEAD_DIM), lambda h: (0, dm_col0 + h))],
        out_specs=[out, out, out, pl.BlockSpec((8, HEAD_DIM), lambda h: (0, h))],
        out_shape=[jax.ShapeDtypeStruct((S, W), BF16), jax.ShapeDtypeStruct((S, W), BF16),
                   jax.ShapeDtypeStruct((S, W), BF16), jax.ShapeDtypeStruct((8, W), F32)],
        scratch_shapes=[big, big, big, big, pltpu.VMEM((S, LANES), F32), big, big, big],
        compiler_params=_params("parallel"), name=name)(proj, proj, proj, cos2, sin_signed, gain, o_raw, lse, dmixed)


GELU_C = math.sqrt(2.0 / math.pi)
GELU_A = 0.044715
HALO = 16


def _shift_down(cur, halo, k):
    out = pltpu.roll(cur, k, axis=0)
    row = lax.broadcasted_iota(jnp.int32, cur.shape, 0)
    for t in range(k):
        out = jnp.where(row == t, halo[HALO - k + t:HALO - k + t + 1, :], out)
    return out


def _shift_up(cur, halo, k):
    n = cur.shape[0]
    out = pltpu.roll(cur, n - k, axis=0)
    row = lax.broadcasted_iota(jnp.int32, cur.shape, 0)
    for t in range(k):
        out = jnp.where(row == n - k + t, halo[t:t + 1, :], out)
    return out


def _conv3(cur, halo, cw):
    return _shift_down(cur, halo, 2) * cw[0:1, :] + _shift_down(cur, halo, 1) * cw[1:2, :] + cur * cw[2:3, :] + cw[3:4, :]


def _gelu_parts(x):
    t = jnp.tanh(GELU_C * (x + GELU_A * x * x * x))
    return 0.5 * x * (1.0 + t), t


def _geglu_specs(tm, tn, ncb):
    hb = tm // HALO

    def cur(off):
        return pl.BlockSpec((tm, tn), lambda j, i: (i, off + j))

    def prev(off):
        return pl.BlockSpec((HALO, tn), lambda j, i: (jnp.maximum(i * hb - 1, 0), off + j))

    def taps(off):
        return pl.BlockSpec((8, tn), lambda j, i: (0, off + j))

    return [cur(0), prev(0), cur(ncb), prev(ncb), taps(0), taps(ncb)]


def _geglu_fwd(u, cwb, name, tm=256, tn=1408):
    S, F2 = u.shape
    F = F2 // 2
    tm, tn = _tile(S, tm, HALO), _tile(F, tn)
    ncb = F // tn

    def body(g_ref, gp_ref, v_ref, vp_ref, cg_ref, cv_ref, y_ref):
        top = pl.program_id(1) > 0
        gp = jnp.where(top, gp_ref[...].astype(F32), 0.0)
        vp = jnp.where(top, vp_ref[...].astype(F32), 0.0)
        gc = _conv3(g_ref[...].astype(F32), gp, cg_ref[...])
        vc = _conv3(v_ref[...].astype(F32), vp, cv_ref[...])
        y_ref[...] = (_gelu_parts(gc)[0] * vc).astype(BF16)

    return _pcall(body, grid=(ncb, S // tm), in_specs=_geglu_specs(tm, tn, ncb),
                  out_specs=pl.BlockSpec((tm, tn), lambda j, i: (i, j)),
                  out_shape=jax.ShapeDtypeStruct((S, F), BF16),
                  compiler_params=_params("parallel", "parallel"), name=name)(u, u, u, u, cwb, cwb)


def _geglu_bwd(u, dy, cwb, name, tm=256, tn=512):
    S, F2 = u.shape
    F = F2 // 2
    tm, tn = _tile(S, tm, HALO), _tile(F, tn)
    ncb = F // tn

    def body(g_ref, gp_ref, v_ref, vp_ref, cg_ref, cv_ref, dy_ref, dc_ref, dwg_ref, dwv_ref):
        i = pl.program_id(1)

        @pl.when(i == 0)
        def _():
            dwg_ref[...] = jnp.zeros_like(dwg_ref)
            dwv_ref[...] = jnp.zeros_like(dwv_ref)

        top = i > 0
        g, v = g_ref[...].astype(F32), v_ref[...].astype(F32)
        gp = jnp.where(top, gp_ref[...].astype(F32), 0.0)
        vp = jnp.where(top, vp_ref[...].astype(F32), 0.0)
        gc = _conv3(g, gp, cg_ref[...])
        vc = _conv3(v, vp, cv_ref[...])
        act, t = _gelu_parts(gc)
        dact = 0.5 * (1.0 + t) + 0.5 * gc * (1.0 - t * t) * GELU_C * (1.0 + 3.0 * GELU_A * gc * gc)
        dyv = dy_ref[...].astype(F32)
        dgc = dyv * vc * dact
        dvc = dyv * act
        dc_ref[0] = dgc.astype(BF16)
        dc_ref[1] = dvc.astype(BF16)

        def taps(out_ref, dc, cur, halo):
            out_ref[0:1, :] += jnp.sum(dc * _shift_down(cur, halo, 2), axis=0, keepdims=True)
            out_ref[1:2, :] += jnp.sum(dc * _shift_down(cur, halo, 1), axis=0, keepdims=True)
            out_ref[2:3, :] += jnp.sum(dc * cur, axis=0, keepdims=True)
            out_ref[3:4, :] += jnp.sum(dc, axis=0, keepdims=True)

        taps(dwg_ref, dgc, g, gp)
        taps(dwv_ref, dvc, v, vp)

    return _pcall(body, grid=(ncb, S // tm),
                  in_specs=_geglu_specs(tm, tn, ncb) + [pl.BlockSpec((tm, tn), lambda j, i: (i, j))],
                  out_specs=[pl.BlockSpec((2, tm, tn), lambda j, i: (0, i, j)),
                             pl.BlockSpec((8, tn), lambda j, i: (0, j)), pl.BlockSpec((8, tn), lambda j, i: (0, j))],
                  out_shape=[jax.ShapeDtypeStruct((2, S, F), BF16), jax.ShapeDtypeStruct((8, F), F32),
                             jax.ShapeDtypeStruct((8, F), F32)],
                  compiler_params=_params("parallel", "arbitrary"), name=name)(u, u, u, u, cwb, cwb, dy)


def _conv_bwd(dc, cwb, name, tm=512, tn=1408):
    _, S, F = dc.shape
    tm, tn = _tile(S, tm, HALO), _tile(F, tn)
    ncb, nrb = F // tn, S // tm
    hb = tm // HALO

    def body(c_ref, n_ref, w_ref, du_ref):
        cur = c_ref[...].astype(F32)
        nxt = jnp.where(pl.program_id(2) < nrb - 1, n_ref[...].astype(F32), 0.0)
        w = w_ref[...]
        du = cur * w[2:3, :] + _shift_up(cur, nxt, 1) * w[1:2, :] + _shift_up(cur, nxt, 2) * w[0:1, :]
        du_ref[...] = du.astype(BF16)

    return _pcall(body, grid=(2, ncb, nrb),
                  in_specs=[pl.BlockSpec((None, tm, tn), lambda c, j, i: (c, i, j)),
                            pl.BlockSpec((None, HALO, tn), lambda c, j, i: (c, jnp.minimum((i + 1) * hb, S // HALO - 1), j)),
                            pl.BlockSpec((8, tn), lambda c, j, i: (0, c * ncb + j))],
                  out_specs=pl.BlockSpec((tm, tn), lambda c, j, i: (i, c * ncb + j)),
                  out_shape=jax.ShapeDtypeStruct((S, 2 * F), BF16),
                  compiler_params=_params("parallel", "parallel", "parallel"), name=name)(dc, dc, cwb)


def _adam_math(w, g, m, v):
    m = ADAM_B1 * m + (1.0 - ADAM_B1) * g
    v = ADAM_B2 * v + (1.0 - ADAM_B2) * (g * g)
    m_hat = m / (1.0 - ADAM_B1 ** ADAM_STEP)
    v_hat = v / (1.0 - ADAM_B2 ** ADAM_STEP)
    return -ADAM_LR * (m_hat / (jnp.sqrt(v_hat) + ADAM_EPS) + ADAM_WD * w), m, v


def _adamw(w, parts, m, v, name, tr=256):
    R, C = w.shape
    n, _, Cp = parts.shape
    tr = _tile(R, tr, 8)

    def body(w_ref, p_ref, m_ref, v_ref, g_out, d_out, m_out, v_out):
        g = p_ref[0, :, 0:C].astype(F32)
        for k in range(1, n):
            g = g + p_ref[k, :, 0:C].astype(F32)
        d, mn, vn = _adam_math(w_ref[...], g, m_ref[...], v_ref[...])
        g_out[...] = g
        d_out[...] = d
        m_out[...] = mn
        v_out[...] = vn

    spec = pl.BlockSpec((tr, C), lambda i: (i, 0))
    shape = jax.ShapeDtypeStruct((R, C), F32)
    return _pcall(body, grid=(R // tr,), in_specs=[spec, pl.BlockSpec((n, tr, Cp), lambda i: (0, i, 0)), spec, spec],
                  out_specs=[spec] * 4, out_shape=[shape] * 4, compiler_params=_params("parallel"), name=name)(w, parts, m, v)


def _adamw_chips(w, pair, parts, chip_ids, m, v, name, tr=256):
    R, C = w.shape
    Cp = pair.shape[2]
    by_columns = C == Cp and _tile(R, tr, 16) < 64
    tr, tc = (R, _tile(C, 256)) if by_columns else (_tile(R, tr, 16), C)

    def body(ids_ref, w_ref, own_ref, p1_ref, p2_ref, p3_ref, m_ref, v_ref, g_out, d_out, m_out, v_out):
        g = own_ref[:, 0:tc].astype(F32)
        for ref in (p1_ref, p2_ref, p3_ref):
            g = g + ref[:, 0:tc].astype(F32)
        d, mn, vn = _adam_math(w_ref[...], g, m_ref[...], v_ref[...])
        g_out[...] = g
        d_out[...] = d
        m_out[...] = mn
        v_out[...] = vn

    if by_columns:
        spec = pl.BlockSpec((tr, tc), lambda j, ids: (0, j))
    else:
        spec = pl.BlockSpec((tr, tc), lambda i, ids: (i, 0))

    def chip(k):
        if by_columns:
            return pl.BlockSpec((None, tr, tc), lambda j, ids: (ids[k], 0, j))
        return pl.BlockSpec((None, tr, Cp), lambda i, ids: (ids[k], i, 0))

    shape = jax.ShapeDtypeStruct((R, C), F32)
    grid_spec = pltpu.PrefetchScalarGridSpec(
        num_scalar_prefetch=1, grid=(C // tc if by_columns else R // tr,),
        in_specs=[spec, chip(0), chip(1), chip(2), chip(3), spec, spec], out_specs=[spec] * 4)
    return _pcall(body, grid_spec=grid_spec, out_shape=[shape] * 4, compiler_params=_params("parallel"),
                  name=name)(chip_ids, w, pair, parts, parts, parts, m, v)


def _place():
    return lax.axis_index("x"), lax.axis_index("y"), lax.axis_index("c")


def _other_chips(x, y):
    return [(1 - x, y), (x, 1 - y), (1 - x, 1 - y)]


IN_HBM = pl.BlockSpec(memory_space=pltpu.HBM)
SEM = pl.BlockSpec(memory_space=pltpu.SEMAPHORE)
EFFECT = pltpu.SideEffectType.DATAFLOW_SIDE_EFFECTING
TOKEN = jax.ShapeDtypeStruct((8, LANES), F32)
TOKEN_SPEC = pl.BlockSpec(memory_space=pltpu.VMEM)


def _in_hbm(a):
    return pltpu.with_memory_space_constraint(a, pltpu.HBM)


def _landing(shape):
    return _in_hbm(lax.empty(shape.shape, shape.dtype))


def _hbm_like(a):
    return pltpu.HBM(a.shape, a.dtype)


def _gather_start(landing, slots, after, name):
    na = len(landing)

    def body(*refs):
        land = refs[:na]
        send_sems, recv_sems = refs[na + 1], refs[na + 2]
        token = refs[-1]
        x, y, c = _place()
        for a in range(na):
            own = slots[a](land[a], x, y, c)
            for k, to in enumerate([(x, y, 1 - c)] + [(*chip, c) for chip in _other_chips(x, y)]):
                pltpu.make_async_remote_copy(
                    src_ref=own, dst_ref=own, send_sem=send_sems.at[4 * a + k],
                    recv_sem=recv_sems.at[4 * a + k], device_id=to, device_id_type=MESH).start()
        token[...] = jnp.zeros_like(token)

    sems = pltpu.SemaphoreType.DMA((4 * na,))
    outs = _pcall(
        body, in_specs=[IN_HBM] * na + [HBM],
        out_specs=[SEM, SEM] + [IN_HBM] * na + [TOKEN_SPEC],
        out_shape=[sems, sems] + [_hbm_like(s) for s in landing] + [TOKEN],
        input_output_aliases={a: 2 + a for a in range(na)},
        compiler_params=pltpu.CompilerParams(has_side_effects=EFFECT), name=name,
    )(*[_in_hbm(s) for s in landing], after)
    return outs[0], outs[1], outs[2:2 + na], outs[-1]


def _gather_forward(gathered, send_sems, recv_sems, slots, after, name):
    na = len(gathered)

    def body(*refs):
        gath = refs[:na]
        send1, recv1 = refs[na], refs[na + 1]
        fsend, frecv = refs[na + 3], refs[na + 4]
        token = refs[-1]
        x, y, c = _place()
        chips = _other_chips(x, y)
        for a in range(na):
            for k, peer in enumerate([(x, y, 1 - c)] + [(*chip, c) for chip in chips]):
                arrival = pltpu.make_async_remote_copy(
                    src_ref=slots[a](gath[a], x, y, c), dst_ref=slots[a](gath[a], *peer), send_sem=send1.at[4 * a + k],
                    recv_sem=recv1.at[4 * a + k], device_id=peer, device_id_type=MESH)
                arrival.wait_send()
                arrival.wait_recv()
        for a in range(na):
            for j, chip in enumerate(chips):
                view = slots[a](gath[a], *chip, c)
                pltpu.make_async_remote_copy(
                    src_ref=view, dst_ref=view, send_sem=fsend.at[3 * a + j], recv_sem=frecv.at[3 * a + j],
                    device_id=(x, y, 1 - c), device_id_type=MESH).start()
        token[...] = jnp.zeros_like(token)

    sems = pltpu.SemaphoreType.DMA((3 * na,))
    outs = _pcall(
        body, in_specs=[IN_HBM] * na + [SEM, SEM, HBM],
        out_specs=[SEM, SEM] + [IN_HBM] * na + [TOKEN_SPEC],
        out_shape=[sems, sems] + [_hbm_like(g) for g in gathered] + [TOKEN],
        input_output_aliases={a: 2 + a for a in range(na)},
        compiler_params=pltpu.CompilerParams(has_side_effects=EFFECT), name=name,
    )(*gathered, send_sems, recv_sems, after)
    return outs[0], outs[1], outs[2:2 + na], outs[-1]


def _gather_finish(gathered, fsend, frecv, slots, after, name):
    na = len(gathered)

    def body(*refs):
        gath, fs, fr = refs[:na], refs[na], refs[na + 1]
        x, y, c = _place()
        for a in range(na):
            for j, chip in enumerate(_other_chips(x, y)):
                passed = pltpu.make_async_remote_copy(
                    src_ref=slots[a](gath[a], *chip, c), dst_ref=slots[a](gath[a], *chip, 1 - c),
                    send_sem=fs.at[3 * a + j], recv_sem=fr.at[3 * a + j], device_id=(x, y, 1 - c), device_id_type=MESH)
                passed.wait_send()
                passed.wait_recv()

    outs = _pcall(
        body, in_specs=[IN_HBM] * na + [SEM, SEM, HBM], out_specs=[IN_HBM] * na,
        out_shape=[_hbm_like(g) for g in gathered], input_output_aliases={a: a for a in range(na)},
        compiler_params=pltpu.CompilerParams(has_side_effects=EFFECT), name=name,
    )(*gathered, fsend, frecv, after)
    return list(outs)


def _pair_copy(view, src, land, send_sems, recv_sems, chip):
    x, y, c = _place()
    return pltpu.make_async_remote_copy(
        src_ref=view(src, chip, 1 - c), dst_ref=land.at[chip], send_sem=send_sems.at[chip], recv_sem=recv_sems.at[chip],
        device_id=(x, y, 1 - c), device_id_type=MESH)


def _pair_start(grad, view, block, after, name):
    def body(src, land, after_ref, send_sems, recv_sems, src_thru, land_thru, token):
        for chip in range(N_CHIP):
            _pair_copy(view, src, land, send_sems, recv_sems, chip).start()
        token[...] = jnp.zeros_like(token)

    sems = pltpu.SemaphoreType.DMA((N_CHIP,))
    land = jax.ShapeDtypeStruct((N_CHIP, *block), BF16)
    return _pcall(
        body, in_specs=[IN_HBM, IN_HBM, HBM], out_specs=[SEM, SEM, IN_HBM, IN_HBM, TOKEN_SPEC],
        out_shape=[sems, sems, _hbm_like(grad), _hbm_like(land), TOKEN], input_output_aliases={0: 2, 1: 3},
        compiler_params=pltpu.CompilerParams(has_side_effects=EFFECT), name=name,
    )(_in_hbm(grad), _landing(land), after)


def _pair_wait(grad, recv, send_sems, recv_sems, view, after, name):
    def body(src, land, send, recv_s, after_ref, src_thru, land_thru):
        for chip in range(N_CHIP):
            copy = _pair_copy(view, src, land, send, recv_s, chip)
            copy.wait_send()
            copy.wait_recv()

    return _pcall(
        body, in_specs=[IN_HBM, IN_HBM, SEM, SEM, HBM], out_specs=[IN_HBM, IN_HBM],
        out_shape=[_hbm_like(grad), _hbm_like(recv)], input_output_aliases={0: 0, 1: 1},
        compiler_params=pltpu.CompilerParams(has_side_effects=EFFECT), name=name,
    )(grad, recv, send_sems, recv_sems, after)


def _chip_start(pair, after, name):
    def body(src, land, after_ref, send_sems, recv_sems, src_thru, land_thru, token):
        x, y, c = _place()
        for j, (px, py) in enumerate(_other_chips(x, y)):
            pltpu.make_async_remote_copy(
                src_ref=src.at[2 * px + py], dst_ref=land.at[2 * x + y], send_sem=send_sems.at[j], recv_sem=recv_sems.at[j],
                device_id=(px, py, c), device_id_type=MESH).start()
        token[...] = jnp.zeros_like(token)

    sems = pltpu.SemaphoreType.DMA((3,))
    return _pcall(
        body, in_specs=[IN_HBM, IN_HBM, HBM], out_specs=[SEM, SEM, IN_HBM, IN_HBM, TOKEN_SPEC],
        out_shape=[sems, sems, _hbm_like(pair), _hbm_like(pair), TOKEN], input_output_aliases={0: 2, 1: 3},
        compiler_params=pltpu.CompilerParams(has_side_effects=EFFECT), name=name,
    )(_in_hbm(pair), _landing(pair), after)


def _chip_wait(pair, parts, send_sems, recv_sems, after, name):
    def body(src, land, send, recv, after_ref, src_thru, land_thru):
        x, y, c = _place()
        for j, (px, py) in enumerate(_other_chips(x, y)):
            copy = pltpu.make_async_remote_copy(
                src_ref=src.at[2 * px + py], dst_ref=land.at[2 * px + py], send_sem=send.at[j], recv_sem=recv.at[j],
                device_id=(px, py, c), device_id_type=MESH)
            copy.wait_send()
            copy.wait_recv()

    return _pcall(
        body, in_specs=[IN_HBM, IN_HBM, SEM, SEM, HBM], out_specs=[IN_HBM, IN_HBM],
        out_shape=[_hbm_like(pair), _hbm_like(parts)], input_output_aliases={0: 0, 1: 1},
        compiler_params=pltpu.CompilerParams(has_side_effects=EFFECT), name=name,
    )(pair, parts, send_sems, recv_sems, after)


def _pair_add(core, grad, recv, block, grad_spec, name):
    _, R, C = recv.shape
    tr = block

    def body(c_ref, g_ref, r_ref, o_ref):
        o_ref[...] = (g_ref[...].astype(F32) + r_ref[...].astype(F32)).astype(BF16)

    grid_spec = pltpu.PrefetchScalarGridSpec(
        num_scalar_prefetch=1, grid=(N_CHIP, R // tr),
        in_specs=[grad_spec, pl.BlockSpec((None, tr, C), lambda k, i, c: (k, i, 0))],
        out_specs=pl.BlockSpec((None, tr, C), lambda k, i, c: (k, i, 0)))
    return _pcall(body, grid_spec=grid_spec, out_shape=jax.ShapeDtypeStruct(recv.shape, BF16),
                  compiler_params=_params("parallel", "parallel"), name=name)(core, grad, recv)


def _small_copies(gath, send_sems, recv_sems):
    x, y, c = _place()
    peers = [(x, y, 1 - c)] + [(px, py, pc) for px, py in _other_chips(x, y) for pc in (c, 1 - c)]
    pairs = []
    for a, ref in enumerate(gath):
        mine = ref.at[4 * x + 2 * y + c]
        for k, (px, py, pc) in enumerate(peers):
            sems = dict(send_sem=send_sems.at[7 * a + k], recv_sem=recv_sems.at[7 * a + k], device_id=(px, py, pc),
                        device_id_type=MESH)
            pairs.append((pltpu.make_async_remote_copy(src_ref=mine, dst_ref=mine, **sems),
                          pltpu.make_async_remote_copy(src_ref=mine, dst_ref=ref.at[4 * px + 2 * py + pc], **sems)))
    return pairs


def _small_start(landing, after, name):
    na = len(landing)

    def body(*refs):
        for send, _ in _small_copies(refs[:na], refs[na + 1], refs[na + 2]):
            send.start()
        refs[-1][...] = jnp.zeros_like(refs[-1])

    sems = pltpu.SemaphoreType.DMA((7 * na,))
    outs = _pcall(
        body, in_specs=[IN_HBM] * na + [HBM], out_specs=[SEM, SEM] + [IN_HBM] * na + [TOKEN_SPEC],
        out_shape=[sems, sems] + [_hbm_like(s) for s in landing] + [TOKEN],
        input_output_aliases={a: 2 + a for a in range(na)},
        compiler_params=pltpu.CompilerParams(has_side_effects=EFFECT), name=name,
    )(*[_in_hbm(s) for s in landing], after)
    return outs[0], outs[1], outs[2:2 + na], outs[-1]


def _small_wait(gathered, send_sems, recv_sems, after, name):
    na = len(gathered)

    def body(*refs):
        for send, arrival in _small_copies(refs[:na], refs[na], refs[na + 1]):
            send.wait_send()
            arrival.wait_recv()

    return list(_pcall(
        body, in_specs=[IN_HBM] * na + [SEM, SEM, HBM], out_specs=[IN_HBM] * na,
        out_shape=[_hbm_like(g) for g in gathered], input_output_aliases={a: a for a in range(na)},
        compiler_params=pltpu.CompilerParams(has_side_effects=EFFECT), name=name,
    )(*gathered, send_sems, recv_sems, after))


def _small_finish(gathered, params, name):
    na, npar = len(gathered), len(params)

    def body(*refs):
        g_refs, wmv = refs[:na], refs[na:na + 3 * npar]
        o_sums, o_params = refs[na + 3 * npar:2 * na + 3 * npar], refs[2 * na + 3 * npar:]
        sums = []
        for a in range(na):
            acc = g_refs[a][0]
            for k in range(1, N_DEV):
                acc = acc + g_refs[a][k]
            o_sums[a][...] = acc
            sums.append(acc)
        for j, (a, row, _, _, _) in enumerate(params):
            g = sums[a][row:row + 1, :]
            d, mn, vn = _adam_math(wmv[3 * j][...], g, wmv[3 * j + 1][...], wmv[3 * j + 2][...])
            for out, val in zip(o_params[4 * j:4 * j + 4], (g, d, mn, vn)):
                out[...] = val

    vm = pl.BlockSpec(memory_space=pltpu.VMEM)
    flat = [t for p in params for t in p[2:]]
    out_shape = [jax.ShapeDtypeStruct(g.shape[1:], F32) for g in gathered]
    out_shape += [jax.ShapeDtypeStruct(p[2].shape, F32) for p in params for _ in range(4)]
    outs = _pcall(body, in_specs=[vm] * (na + 3 * npar), out_specs=[vm] * len(out_shape), out_shape=out_shape,
                  name=name)(*gathered, *flat)
    return outs[:na], [outs[na + 4 * j:na + 4 * j + 4] for j in range(npar)]


def _local_step(x, tgt, gains, weights):
    g_pre_mix, g_post_mix, g_pre_ffn, g_post_ffn, g_sb, g_dil = gains
    S, D = x.shape
    hs = g_sb.shape[1] // HEAD_DIM
    hd = g_dil.shape[1] // HEAD_DIM
    cos2, sin_signed = _rope_tables(S)

    h1 = _rms_fwd(x, g_pre_mix + weights.start(), "rms_in")
    w_in_g = weights.w_in(h1)
    proj = _mm_nn(h1, w_in_g, BF16, "proj", tn=768)
    o_sb, ct_sb, mx_sb = _sb_fwd(proj, g_sb, hs, "sb_fwd")
    o_dl, lse_dl, mx_dl = _dil_fwd(proj, cos2, sin_signed, g_dil + weights.forward_out(o_sb), 3 * hs, hd, "dil_fwd")
    w_out_g, dep = weights.w_out(o_dl)
    mixed = jnp.concatenate([mx_sb, mx_dl], axis=1)
    mix = _mm_nn(mixed, w_out_g, F32, "mix_out", tn=1024)
    x2, h2 = _mid_fwd(x, mix, g_post_mix + dep, g_pre_ffn, "mid_fwd")
    w_up_g, cwb = weights.w_up(h2)
    u = _mm_nn(h2, w_up_g, BF16, "ffn_up", b_transposed=True)
    y = _geglu_fwd(u, cwb + weights.forward_down(u), "geglu_fwd")
    w_down_g = weights.w_down(y)
    f = _mm_nn(y, w_down_g, F32, "ffn_down", tn=1024, tk=2816)

    dy, df, dg_post_ffn, loss = _loss_bwd(x2, f, tgt, g_post_ffn, "loss_bwd")
    dyv = _mm_nt(df, w_down_g, BF16, "d_y", tn=1408)
    dw_down = _mm_tn(y, df, D, BF16, "dw_down", tm=1408, tn=1024)
    dc, dcw_g, dcw_v = _geglu_bwd(u, dyv, cwb + weights.grad("w_down", dw_down), "geglu_bwd")
    du = _conv_bwd(dc, cwb + weights.grad_reduce("w_down", dc), "conv_bwd")
    dh2 = _mm_nt(du, w_up_g, BF16, "d_h2", tk=1408, b_transposed=True, per_step=2)
    dw_up = _mm_tn(du, h2, D, BF16, "dw_up", tm=1408, tn=1024)
    dx2, dmix, dg_pre_ffn, dg_post_mix = _mid_bwd(
        dy, dh2, x2, mix, g_pre_ffn + weights.grad("w_up", dw_up), g_post_mix, "mid_bwd")
    dmixed = _mm_nt(dmix, w_out_g, BF16, "d_mixed", after=jnp.reshape(weights.grad_reduce("w_up", dmix), (1, 1)))
    dw_out = _mm_tn(mixed, dmix, D, BF16, "dw_out", tn=1024)
    dq_s, dk_s, dv_s, dg_sb = _sb_bwd(proj, g_sb + weights.grad("w_out", dw_out), o_sb, ct_sb, dmixed, 0, hs, "sb_bwd")
    dq_d, dk_d, dv_d, dg_dil = _dil_bwd(proj, cos2, sin_signed, g_dil + weights.grad_reduce("w_out", dq_s), o_dl, lse_dl,
                                        dmixed, hs, 3 * hs, hd, "dil_bwd")
    dproj = jnp.concatenate([dq_s, dk_s, dv_s, dq_d, dk_d, dv_d], axis=1)
    dw_in = _mm_tn(h1, dproj, w_in_g.shape[2], BF16, "dw_in", tn=768)
    weights.grad("w_in", dw_in)
    dep = weights.grad_reduce("w_in", dproj)
    dh1 = _mm_nt(dproj, w_in_g, BF16, "d_h1", tk=768, after=jnp.reshape(dep, (1, 1)), per_step=4)
    grad_x, dg_pre_mix = _first_bwd(dx2, dh1, x, g_pre_mix, "first_bwd")
    small = (dg_pre_mix, dg_post_mix, dg_pre_ffn, dg_post_ffn, dg_sb[0:1], dg_dil[0:1], jnp.concatenate([dcw_g, dcw_v], axis=1))
    weights.small(small, loss)
    return loss, grad_x, small


def _pad_cols(a, to):
    return jnp.pad(a, ((0, 0), (0, to - a.shape[1])))


def kernel(x, pre_mix_gain, post_mix_gain, pre_ffn_gain, post_ffn_gain, w_in, sb_out_gain, dil_out_gain, w_out, w_up, conv_w, conv_b, w_down, loss_target, m_pre_mix_gain, m_post_mix_gain, m_pre_ffn_gain, m_post_ffn_gain, m_w_in, m_sb_out_gain, m_dil_out_gain, m_w_out, m_w_up, m_conv_w, m_conv_b, m_w_down, v_pre_mix_gain, v_post_mix_gain, v_pre_ffn_gain, v_post_ffn_gain, v_w_in, v_sb_out_gain, v_dil_out_gain, v_w_out, v_w_up, v_conv_w, v_conv_b, v_w_down):
    xb, tb = x[0], loss_target[0]
    S, D = xb.shape
    w_in, w_out, w_up, w_down, conv_w = w_in[0], w_out[0], w_up[0], w_down[0], conv_w[0]
    n_in, e_rows = w_in.shape[1], w_out.shape[0]
    cu, half = w_up.shape[1], w_down.shape[0]
    assert cu == 2 * half and half % 16 == 0
    cup = -(-cu // LANES) * LANES
    fp = N_CHIP * cup
    px, py, pc = _place()
    me = 4 * px + 2 * py + pc
    core = jnp.reshape(pc, (1,)).astype(jnp.int32)

    w_up_t, m_up_t, v_up_t = (jnp.swapaxes(t, 0, 1) for t in (w_up, m_w_up[0], v_w_up[0]))

    def by_dev(ref, qx, qy, qc):
        return ref.at[4 * qx + 2 * qy + qc]

    def down_slot(ref, qx, qy, qc):
        return ref.at[2 * qx + qy, pl.ds(qc * half, half)]

    def by_pair(ref, chip, k):
        return ref.at[chip, k]

    def down_pair(ref, chip, k):
        return ref.at[chip, pl.ds(k * half, half)]

    def pair_spec(tr, cols):
        return pl.BlockSpec((None, None, tr, cols), lambda k, i, c: (k, c[0], i, 0))

    tr_in, tr_up = _tile(D, 512, 16), _tile(cup, 256, 16)
    grad_plan = {
        "w_in": ((N_CHIP, 2, D, n_in), by_pair, (D, n_in), tr_in, pair_spec(tr_in, n_in)),
        "w_out": ((N_CHIP, 2, e_rows, D), by_pair, (e_rows, D), e_rows, pair_spec(e_rows, D)),
        "w_up": ((N_CHIP, 2, cup, D), by_pair, (cup, D), tr_up, pair_spec(tr_up, D)),
        "w_down": ((N_CHIP, cup, D), down_pair, (half, D), half,
                   pl.BlockSpec((None, half, D), lambda k, i, c: (k, c[0], 0))),
    }

    class Exchanges:
        def __init__(self):
            self.in_flight = {}

        def start(self):
            def own_slot(shard):
                return lax.dynamic_update_index_in_dim(lax.empty((N_DEV, *shard.shape), shard.dtype), shard, me, 0)

            self.g_in = _gather_start([own_slot(w_in.astype(BF16))], [by_dev], core, "gather_in_start")
            zero = self.g_in[3][0, 0]
            self.g_out = _gather_start([own_slot((w_out + zero).astype(BF16))], [by_dev], self.g_in[3], "gather_out_start")
            up = jnp.pad(w_up_t + zero, ((0, cup - cu), (0, 0))).astype(BF16)
            taps = jnp.pad(conv_w + zero, ((0, 8 - conv_w.shape[0]), (0, cup - cu)))
            self.g_up = _gather_start([own_slot(up), own_slot(taps)], [by_dev, by_dev], self.g_out[3], "gather_up_start")
            down = lax.dynamic_update_slice(jnp.zeros((N_CHIP, cup, D), BF16), (w_down + zero).astype(BF16)[None],
                                            (2 * px + py, pc * half, 0))
            self.g_down = _gather_start([down], [down_slot], self.g_up[3], "gather_down_start")
            return self.g_down[3][0, 0]

        def w_in(self, after):
            send, recv, gath, _ = self.g_in
            fsend, frecv, gath, token = _gather_forward(gath, send, recv, [by_dev], after, "gather_in_forward")
            return _gather_finish(gath, fsend, frecv, [by_dev], token, "gather_in_finish")[0]

        def forward_out(self, after):
            send, recv, gath, _ = self.g_out
            self.p_out = _gather_forward(gath, send, recv, [by_dev], after, "gather_out_forward")
            return self.p_out[3][0, 0]

        def w_out(self, after):
            fsend, frecv, gath, _ = self.p_out
            w_out_g = _gather_finish(gath, fsend, frecv, [by_dev], after, "gather_out_finish")[0]
            send, recv, gath, _ = self.g_up
            self.p_up = _gather_forward(gath, send, recv, [by_dev, by_dev], w_out_g, "gather_up_forward")
            return w_out_g.reshape(1, N_DEV * e_rows, D), self.p_up[3][0, 0]

        def w_up(self, after):
            fsend, frecv, gath, _ = self.p_up
            w_up_g, cw_g = _gather_finish(gath, fsend, frecv, [by_dev, by_dev], after, "gather_up_finish")
            cb = _pad_cols(conv_b.reshape(N_DEV, cu), cup).reshape(1, 2 * fp)
            cw_full = jnp.transpose(cw_g[:, :3, :], (1, 0, 2)).reshape(3, 2 * fp)
            cwb = jnp.concatenate([cw_full, cb, jnp.zeros((4, 2 * fp), F32)], axis=0)
            return w_up_g, cwb

        def forward_down(self, after):
            send, recv, gath, _ = self.g_down
            self.p_down = _gather_forward(gath, send, recv, [down_slot], after, "gather_down_forward")
            return self.p_down[3][0, 0]

        def w_down(self, after):
            fsend, frecv, gath, _ = self.p_down
            return _gather_finish(gath, fsend, frecv, [down_slot], after, "gather_down_finish")[0].reshape(1, fp, D)

        def small(self, small, loss):
            d_pre_mix, d_post_mix, d_pre_ffn, d_post_ffn, d_sb, d_dil, d_conv = small

            def rows_of(*vectors):
                n = vectors[0].shape[1]
                row = lax.broadcasted_iota(jnp.int32, (8, n), 0)
                out = jnp.zeros((8, n), F32)
                for k, vec in enumerate(vectors):
                    out = jnp.where(row == k, vec, out)
                return out

            parts = [rows_of(d_pre_mix, d_post_mix, d_pre_ffn, d_post_ffn, jnp.broadcast_to(loss[:, :1], (1, D))),
                     rows_of(d_sb, d_dil), d_conv]
            landing = [lax.dynamic_update_index_in_dim(lax.empty((N_DEV, *p.shape), F32), p, me, 0) for p in parts]
            self.small_flight = _small_start(landing, parts[0], "small_start")

        def small_sums(self, after):
            send, recv, gath, _ = self.small_flight
            gath = _small_wait(gath, send, recv, after, "small_wait")
            params = [(0, 0, pre_mix_gain, m_pre_mix_gain, v_pre_mix_gain), (0, 1, post_mix_gain, m_post_mix_gain, v_post_mix_gain),
                      (0, 2, pre_ffn_gain, m_pre_ffn_gain, v_pre_ffn_gain), (0, 3, post_ffn_gain, m_post_ffn_gain, v_post_ffn_gain),
                      (1, 0, sb_out_gain, m_sb_out_gain, v_sb_out_gain), (1, 1, dil_out_gain, m_dil_out_gain, v_dil_out_gain)]
            (gains_sum, _, conv_sum), gain_steps = _small_finish(gath, params, "small_finish")
            return gains_sum[4, 0], conv_sum, gain_steps

        def grad(self, name, dw):
            view_shape, view, block, tr, spec = grad_plan[name]
            send, recv_sems, dw, recv, token = _pair_start(dw.reshape(view_shape), view, block, core, "pair_start_" + name)
            self.in_flight[name] = (dw, recv, send, recv_sems)
            return token[0, 0]

        def grad_reduce(self, name, after):
            _, view, _, tr, spec = grad_plan[name]
            dw, recv = _pair_wait(*self.in_flight[name], view, after, "pair_wait_" + name)
            pair = _pair_add(core, dw, recv, tr, spec, "pair_add_" + name)
            send, recv_sems, pair, parts, token = _chip_start(pair, recv, "chip_start_" + name)
            self.in_flight[name] = (pair, parts, send, recv_sems)
            self.last_token = token
            return token[0, 0]

        def grad_parts(self, name, after):
            return _chip_wait(*self.in_flight[name], after, "chip_wait_" + name)

    exchanges = Exchanges()
    gains = (pre_mix_gain, post_mix_gain, pre_ffn_gain, post_ffn_gain, sb_out_gain, dil_out_gain)
    loss, grad_x, small = _local_step(xb, tb, gains, exchanges)

    def small_adam(w, g, m, v, name):
        one = w.shape[0] == 1
        if one:
            w, g, m, v = (jnp.broadcast_to(t, (8, t.shape[1])) for t in (w, g, m, v))
        outs = _adamw(w, g[None], m, v, name)
        return [o[0:1] for o in outs] if one else outs

    chip_ids = jnp.stack([2 * px + py, 2 * (1 - px) + py, 2 * px + 1 - py, 2 * (1 - px) + 1 - py]).astype(jnp.int32)
    out_w_down = _adamw_chips(w_down, *exchanges.grad_parts("w_down", exchanges.small_flight[3]), chip_ids, m_w_down[0], v_w_down[0], "adam_w_down")
    out_up_t = _adamw_chips(w_up_t, *exchanges.grad_parts("w_up", out_w_down[1]), chip_ids, m_up_t, v_up_t, "adam_w_up")
    out_w_up = [jnp.swapaxes(o, 0, 1) for o in out_up_t]
    out_w_out = _adamw_chips(w_out, *exchanges.grad_parts("w_out", out_up_t[1]), chip_ids, m_w_out[0], v_w_out[0], "adam_w_out")
    loss_out, g_conv, gain_steps = exchanges.small_sums(out_w_out[1])
    out_pre_mix, out_post_mix, out_pre_ffn, out_post_ffn, out_sb, out_dil = gain_steps
    g_conv_b = g_conv[3].reshape(N_DEV, cup)[:, :cu].reshape(1, N_DEV * cu)
    g_conv_w = lax.dynamic_index_in_dim(g_conv[0:3].reshape(3, N_DEV, cup), me, axis=1, keepdims=False)[:, :cu]
    out_conv_b = small_adam(conv_b, g_conv_b, m_conv_b, v_conv_b, "adam_conv_b")
    cw8 = [jnp.pad(t, ((0, 5), (0, 0))) for t in (conv_w, g_conv_w, m_conv_w[0], v_conv_w[0])]
    out_conv_w = [o[0:3] for o in _adamw(cw8[0], cw8[1][None], cw8[2], cw8[3], "adam_conv_w")]
    out_w_in = _adamw_chips(w_in, *exchanges.grad_parts("w_in", out_conv_w[1]), chip_ids, m_w_in[0], v_w_in[0], "adam_w_in")

    order = [out_pre_mix, out_post_mix, out_pre_ffn, out_post_ffn, [o[None] for o in out_w_in], out_sb, out_dil,
             [o[None] for o in out_w_out], [o[None] for o in out_w_up], [o[None] for o in out_conv_w], out_conv_b,
             [o[None] for o in out_w_down]]
    outs = [loss_out, grad_x[None]]
    for k in range(4):
        outs += [o[k] for o in order]
    return tuple(outs)
```

```python
import functools
import math

import jax
import jax.numpy as jnp
from jax import lax
from jax.experimental import pallas as pl
from jax.experimental.pallas import tpu as pltpu

F32 = jnp.float32
BF16 = jnp.bfloat16
HEAD_DIM = 128
LANES = 128
KEY_BLOCK = 128
DILATIONS = (1, 4, 16)
RMS_EPS = 1e-6
ROPE_THETA = 10000.0
NEG = -1e30
ADAM_LR, ADAM_B1, ADAM_B2, ADAM_EPS, ADAM_WD, ADAM_STEP = 0.001, 0.9, 0.999, 1e-08, 0.01, 10
MESH = pl.DeviceIdType.MESH
N_DEV = 8
N_CHIP = 4
HBM = pl.BlockSpec(memory_space=pl.ANY)
VMEM_LIMIT = 56 * 1024 * 1024

_pcall = pl.pallas_call


def _tile(n, pref, mult=LANES):
    best = None
    t = mult
    while t <= min(n, pref):
        if n % t == 0:
            best = t
        t += mult
    return n if best is None else best


def _params(*sem):
    return pltpu.CompilerParams(dimension_semantics=sem, vmem_limit_bytes=VMEM_LIMIT)


def _dot(a, b, dims):
    return lax.dot_general(a, b, (dims, ((), ())), preferred_element_type=F32)


NN = ((1,), (0,))
NT = ((1,), (1,))
TN = ((0,), (0,))


def _mm_body(dims, nk, tile):
    if nk == 1:
        def single(a_ref, b_ref, o_ref):
            o_ref[...] = _dot(a_ref[...].astype(BF16), b_ref[...].astype(BF16), dims).astype(o_ref.dtype)

        return single, []

    def body(a_ref, b_ref, o_ref, acc_ref):
        k = pl.program_id(2)

        @pl.when(k == 0)
        def _():
            acc_ref[...] = jnp.zeros_like(acc_ref)

        acc_ref[...] += _dot(a_ref[...].astype(BF16), b_ref[...].astype(BF16), dims)

        @pl.when(k == nk - 1)
        def _():
            o_ref[...] = acc_ref[...].astype(o_ref.dtype)

    return body, [pltpu.VMEM(tile, F32)]


def _mm_nn(a, b3, out_dtype, name, tm=1024, tn=1408, tk=2048, b_transposed=False):
    M, K = a.shape
    C, n = b3.shape[0], b3.shape[1 if b_transposed else 2]
    tm, tk, tn = _tile(M, tm, 8), _tile(K, tk), _tile(n, tn)
    npc, nk = n // tn, K // tk
    body, scratch = _mm_body(NT if b_transposed else NN, nk, (tm, tn))
    b_spec = (pl.BlockSpec((None, tn, tk), lambda i, j, k: (j // npc, j % npc, k)) if b_transposed
              else pl.BlockSpec((None, tk, tn), lambda i, j, k: (j // npc, k, j % npc)))
    return _pcall(
        body, grid=(M // tm, C * npc, nk),
        in_specs=[pl.BlockSpec((tm, tk), lambda i, j, k: (i, k)), b_spec],
        out_specs=pl.BlockSpec((tm, tn), lambda i, j, k: (i, j)),
        out_shape=jax.ShapeDtypeStruct((M, C * n), out_dtype), scratch_shapes=scratch,
        compiler_params=_params("parallel", "parallel", "arbitrary"), name=name)(a, b3)


def _mm_nt(a, b3, out_dtype, name, tm=1024, tn=1024, tk=2048, after=None, b_transposed=False, per_step=1):
    M, _ = a.shape
    C, N, n = (b3.shape[0], b3.shape[2], b3.shape[1]) if b_transposed else b3.shape
    tm, tn, tk = _tile(M, tm, 8), _tile(N, tn), _tile(n, tk)
    dims = NN if b_transposed else NT
    extra = [] if after is None else [after]
    if per_step > 1 and tk == n and C % per_step == 0:
        nk, scratch = C // per_step, [pltpu.VMEM((tm, tn), F32)]
        b3 = b3.reshape(nk, per_step, *b3.shape[1:])
        a_spec = pl.BlockSpec((tm, per_step * n), lambda i, j, k: (i, k))
        if b_transposed:
            b_spec = pl.BlockSpec((None, per_step, n, tn), lambda i, j, k: (k, 0, 0, j))
        else:
            b_spec = pl.BlockSpec((None, per_step, tn, n), lambda i, j, k: (k, 0, j, 0))

        def body(a_ref, b_ref, *rest):
            o_ref, acc_ref = rest[len(extra):]
            k = pl.program_id(2)

            @pl.when(k == 0)
            def _():
                acc_ref[...] = jnp.zeros_like(acc_ref)

            b = b_ref[...].astype(BF16)
            b = b.reshape(per_step * n, tn) if b_transposed else jnp.concatenate([b[u] for u in range(per_step)], axis=1)
            acc_ref[...] += _dot(a_ref[...].astype(BF16), b, dims)

            @pl.when(k == nk - 1)
            def _():
                o_ref[...] = acc_ref[...].astype(o_ref.dtype)
    else:
        kpc = n // tk
        nk = C * kpc
        inner, scratch = _mm_body(dims, nk, (tm, tn))
        a_spec = pl.BlockSpec((tm, tk), lambda i, j, k: (i, k))
        b_spec = (pl.BlockSpec((None, tk, tn), lambda i, j, k: (k // kpc, k % kpc, j)) if b_transposed
                  else pl.BlockSpec((None, tn, tk), lambda i, j, k: (k // kpc, j, k % kpc)))

        def body(a_ref, b_ref, *rest):
            inner(a_ref, b_ref, *rest[len(extra):])

    return _pcall(
        body, grid=(M // tm, N // tn, nk), in_specs=[a_spec, b_spec] + [HBM] * len(extra),
        out_specs=pl.BlockSpec((tm, tn), lambda i, j, k: (i, j)),
        out_shape=jax.ShapeDtypeStruct((M, N), out_dtype), scratch_shapes=scratch,
        compiler_params=_params("parallel", "parallel", "arbitrary"), name=name)(a, b3, *extra)


def _mm_tn(x, y, n, out_dtype, name, tm=1024, tn=1408, tk=2048, after=None):
    S, P = x.shape
    C = y.shape[1] // n
    tm, tn, tk = _tile(P, tm), _tile(n, tn), _tile(S, tk, 8)
    npc, nk = n // tn, S // tk
    inner, scratch = _mm_body(TN, nk, (tm, tn))
    extra = [] if after is None else [after]

    def body(x_ref, y_ref, *rest):
        inner(x_ref, y_ref, *rest[len(extra):])

    return _pcall(
        body, grid=(P // tm, C * npc, nk),
        in_specs=[pl.BlockSpec((tk, tm), lambda i, j, k: (k, i)),
                  pl.BlockSpec((tk, tn), lambda i, j, k: (k, j))] + [HBM] * len(extra),
        out_specs=pl.BlockSpec((None, tm, tn), lambda i, j, k: (j // npc, i, j % npc)),
        out_shape=jax.ShapeDtypeStruct((C, P, n), out_dtype), scratch_shapes=scratch,
        compiler_params=_params("parallel", "parallel", "arbitrary"), name=name)(x, y, *extra)


def _rms_scale(v):
    return lax.rsqrt(jnp.mean(v * v, axis=-1, keepdims=True) + RMS_EPS)


def _rms_bwd(gy, v, r):
    return r * gy - v * (r * r * r * jnp.mean(gy * v, axis=-1, keepdims=True))


def _rows_spec(tm, d):
    return pl.BlockSpec((tm, d), lambda i: (i, 0))


def _vec_spec(d):
    return pl.BlockSpec((1, d), lambda i: (0, 0))


def _rms_fwd(x, g, name, tm=256):
    S, D = x.shape

    def body(x_ref, g_ref, h_ref):
        v = x_ref[...]
        h_ref[...] = (v * _rms_scale(v) * g_ref[...]).astype(BF16)

    return _pcall(body, grid=(S // tm,), in_specs=[_rows_spec(tm, D), _vec_spec(D)], out_specs=_rows_spec(tm, D),
                  out_shape=jax.ShapeDtypeStruct((S, D), BF16), compiler_params=_params("parallel"), name=name)(x, g)


def _mid_fwd(x, mix, g_post, g_pre, name, tm=256):
    S, D = x.shape

    def body(x_ref, m_ref, gp_ref, gn_ref, x2_ref, h_ref):
        m = m_ref[...]
        x2 = x_ref[...] + m * _rms_scale(m) * gp_ref[...]
        x2_ref[...] = x2
        h_ref[...] = (x2 * _rms_scale(x2) * gn_ref[...]).astype(BF16)

    return _pcall(body, grid=(S // tm,), in_specs=[_rows_spec(tm, D), _rows_spec(tm, D), _vec_spec(D), _vec_spec(D)],
                  out_specs=[_rows_spec(tm, D), _rows_spec(tm, D)],
                  out_shape=[jax.ShapeDtypeStruct((S, D), F32), jax.ShapeDtypeStruct((S, D), BF16)],
                  compiler_params=_params("parallel"), name=name)(x, mix, g_post, g_pre)


def _loss_bwd(x2, f, tgt, g_post, name, tm=256):
    S, D = x2.shape

    def body(x2_ref, f_ref, t_ref, g_ref, dy_ref, df_ref, dg_ref, ls_ref):
        i = pl.program_id(0)

        @pl.when(i == 0)
        def _():
            dg_ref[...] = jnp.zeros_like(dg_ref)
            ls_ref[...] = jnp.zeros_like(ls_ref)

        fv = f_ref[...]
        r = _rms_scale(fv)
        g = g_ref[...]
        err = x2_ref[...] + fv * r * g - t_ref[...]
        ls_ref[...] += jnp.broadcast_to(0.5 * jnp.sum(jnp.mean(err * err, axis=-1, keepdims=True), axis=0, keepdims=True), ls_ref.shape)
        dy = err * (1.0 / D)
        dy_ref[...] = dy
        df_ref[...] = _rms_bwd(dy * g, fv, r).astype(BF16)
        dg_ref[...] += jnp.sum(dy * fv * r, axis=0, keepdims=True)

    return _pcall(body, grid=(S // tm,),
                  in_specs=[_rows_spec(tm, D), _rows_spec(tm, D), _rows_spec(tm, D), _vec_spec(D)],
                  out_specs=[_rows_spec(tm, D), _rows_spec(tm, D), _vec_spec(D), _vec_spec(LANES)],
                  out_shape=[jax.ShapeDtypeStruct((S, D), F32), jax.ShapeDtypeStruct((S, D), BF16),
                             jax.ShapeDtypeStruct((1, D), F32), jax.ShapeDtypeStruct((1, LANES), F32)],
                  compiler_params=_params("arbitrary"), name=name)(x2, f, tgt, g_post)


def _mid_bwd(dy, dh2, x2, mix, g_pre, g_post, name, tm=256):
    S, D = dy.shape

    def body(dy_ref, dh_ref, x2_ref, m_ref, gn_ref, gp_ref, dx2_ref, dm_ref, dgn_ref, dgp_ref):
        i = pl.program_id(0)

        @pl.when(i == 0)
        def _():
            dgn_ref[...] = jnp.zeros_like(dgn_ref)
            dgp_ref[...] = jnp.zeros_like(dgp_ref)

        x2, dh = x2_ref[...], dh_ref[...].astype(F32)
        r = _rms_scale(x2)
        dx2 = dy_ref[...] + _rms_bwd(dh * gn_ref[...], x2, r)
        dgn_ref[...] += jnp.sum(dh * x2 * r, axis=0, keepdims=True)
        dx2_ref[...] = dx2
        m = m_ref[...]
        rm = _rms_scale(m)
        dm_ref[...] = _rms_bwd(dx2 * gp_ref[...], m, rm).astype(BF16)
        dgp_ref[...] += jnp.sum(dx2 * m * rm, axis=0, keepdims=True)

    return _pcall(body, grid=(S // tm,),
                  in_specs=[_rows_spec(tm, D)] * 4 + [_vec_spec(D)] * 2,
                  out_specs=[_rows_spec(tm, D), _rows_spec(tm, D), _vec_spec(D), _vec_spec(D)],
                  out_shape=[jax.ShapeDtypeStruct((S, D), F32), jax.ShapeDtypeStruct((S, D), BF16),
                             jax.ShapeDtypeStruct((1, D), F32), jax.ShapeDtypeStruct((1, D), F32)],
                  compiler_params=_params("arbitrary"), name=name)(dy, dh2, x2, mix, g_pre, g_post)


def _first_bwd(dx2, dh1, x, g_pre, name, tm=256):
    S, D = x.shape

    def body(dx2_ref, dh_ref, x_ref, g_ref, gx_ref, dg_ref):
        i = pl.program_id(0)

        @pl.when(i == 0)
        def _():
            dg_ref[...] = jnp.zeros_like(dg_ref)

        xv, dh = x_ref[...], dh_ref[...].astype(F32)
        r = _rms_scale(xv)
        gx_ref[...] = dx2_ref[...] + _rms_bwd(dh * g_ref[...], xv, r)
        dg_ref[...] += jnp.sum(dh * xv * r, axis=0, keepdims=True)

    return _pcall(body, grid=(S // tm,), in_specs=[_rows_spec(tm, D)] * 3 + [_vec_spec(D)],
                  out_specs=[_rows_spec(tm, D), _vec_spec(D)],
                  out_shape=[jax.ShapeDtypeStruct((S, D), F32), jax.ShapeDtypeStruct((1, D), F32)],
                  compiler_params=_params("arbitrary"), name=name)(dx2, dh1, x, g_pre)


def _logsig_pair(z):
    lb = jnp.minimum(z, 0.0) - jnp.log(1.0 + jnp.exp(-jnp.abs(z)))
    return lb, lb - z


SB_KEY_BLOCK = 256


def _sum_matrix(strict):
    ia = lax.broadcasted_iota(jnp.int32, (SB_KEY_BLOCK, SB_KEY_BLOCK), 0)
    ib = lax.broadcasted_iota(jnp.int32, (SB_KEY_BLOCK, SB_KEY_BLOCK), 1)
    return ((ia > ib) if strict == ">" else (ia < ib)).astype(BF16)


def _row_total(sums, v, col):
    return jnp.broadcast_to(sums[:, col:col + 1] + v[:, col:col + 1], (v.shape[0], LANES))


def _lanes(c, width):
    return jnp.tile(c, (1, width // LANES))


def _split_dot(v, u):
    hi = v.astype(BF16)
    lo = (v - hi.astype(F32)).astype(BF16)
    return _dot(hi, u, NN) + _dot(lo, u, NN)


def _head_out(o, g):
    return o * _rms_scale(o) * g


def _sb_fwd(proj, gain, n_heads, mixed_heads, name, tq=1024):
    S = proj.shape[0]
    H, tk = n_heads, SB_KEY_BLOCK
    tq = _tile(S, tq, 2 * tk)
    scale = HEAD_DIM ** -0.5

    def body(q_ref, k_ref, v_ref, g_ref, o_ref, ct_ref, mx_ref, oacc, cacc):
        i = pl.program_id(1)
        oacc[...] = jnp.zeros_like(oacc)
        cacc[...] = jnp.zeros_like(cacc)
        sums = _sum_matrix(">")

        def run(blocks):
            scored = []
            for k0, r0, diagonal in blocks:
                rows = pl.ds(r0, tq - r0)
                lb, lk = _logsig_pair(_dot(q_ref[rows, :].astype(BF16), k_ref[pl.ds(k0, tk), :].astype(BF16), NT) * scale)
                causal = None
                if diagonal:
                    causal = (lax.broadcasted_iota(jnp.int32, (tq - r0, tk), 1)
                              < lax.broadcasted_iota(jnp.int32, (tq - r0, tk), 0))
                    lk = jnp.where(causal, lk, 0.0)
                scored.append((k0, rows, causal, lb, lk))
            summed = [(k0, rows, causal, lb, lk, _split_dot(lk, sums)) for k0, rows, causal, lb, lk in scored]
            weights = []
            for k0, rows, causal, lb, lk, after in summed:
                c = cacc[rows, :]
                a = jnp.exp(lb + after + _lanes(c, tk))
                if causal is not None:
                    a = jnp.where(causal, a, 0.0)
                cacc[rows, :] = c + _row_total(after, lk, 0)
                weights.append((k0, rows, a.astype(BF16)))
            for k0, rows, a in weights:
                oacc[rows, :] += _dot(a, v_ref[pl.ds(k0, tk), :].astype(BF16), NN)

        for d in reversed(range(0, tq // tk, 2)):
            run([(pl.multiple_of(i * tq + e * tk, tk), e * tk, True) for e in (d + 1, d)])
        per_trip = tq // tk

        def step(it, carry):
            k0 = pl.multiple_of((i - 1 - it) * tq, tq)
            run([(pl.multiple_of(k0 + e * tk, tk), 0, False) for e in reversed(range(per_trip))])
            return carry

        lax.fori_loop(0, i, step, 0)
        o = oacc[...]
        o_ref[...] = o
        ct_ref[...] = cacc[...]
        mx_ref[...] = _head_out(o, g_ref[...]).astype(BF16)

    blk = pl.BlockSpec((tq, HEAD_DIM), lambda h, i: (i, h))
    return _pcall(
        body, grid=(H, S // tq),
        in_specs=[blk, pl.BlockSpec((S, HEAD_DIM), lambda h, i: (0, H + h)),
                  pl.BlockSpec((S, HEAD_DIM), lambda h, i: (0, 2 * H + h)), pl.BlockSpec((1, HEAD_DIM), lambda h, i: (0, h))],
        out_specs=[blk, blk, blk],
        out_shape=[jax.ShapeDtypeStruct((S, H * HEAD_DIM), F32), jax.ShapeDtypeStruct((S, H * HEAD_DIM), F32),
                   jax.ShapeDtypeStruct((S, mixed_heads * HEAD_DIM), BF16)],
        scratch_shapes=[pltpu.VMEM((tq, HEAD_DIM), F32), pltpu.VMEM((tq, LANES), F32)],
        compiler_params=_params("parallel", "arbitrary"), name=name)(proj, proj, proj, gain)


def _sb_bwd(proj, gain, o_raw, ctot, dmixed, dm_col0, n_heads, name, tq=1024):
    S = proj.shape[0]
    H, tk = n_heads, SB_KEY_BLOCK
    tq = _tile(S, tq, 2 * tk)
    nq = S // tq
    scale = HEAD_DIM ** -0.5

    def body(q_ref, k_ref, v_ref, g_ref, o_ref, ct_ref, dm_ref, dq_ref, dk_ref, dv_ref, dg_ref,
             dkacc, dvacc, dqacc, pfx, gcar, dos):
        i = pl.program_id(1)

        @pl.when(i == 0)
        def _():
            dkacc[...] = jnp.zeros_like(dkacc)
            dvacc[...] = jnp.zeros_like(dvacc)
            dg_ref[...] = jnp.zeros_like(dg_ref)

        o, dm, g = o_ref[...], dm_ref[...].astype(F32), g_ref[...]
        r = _rms_scale(o)
        dos[...] = _rms_bwd(dm * g, o, r).astype(BF16)
        dg_ref[...] += jnp.broadcast_to(jnp.sum(dm * o * r, axis=0, keepdims=True), dg_ref.shape)
        dqacc[...] = jnp.zeros_like(dqacc)
        pfx[...] = jnp.zeros_like(pfx)
        gcar[...] = jnp.zeros_like(gcar)
        later, earlier = _sum_matrix(">"), _sum_matrix("<")

        def run(blocks):
            scored = []
            for k0, r0, diagonal in blocks:
                rows, keys = pl.ds(r0, tq - r0), pl.ds(k0, tk)
                lb, lk = _logsig_pair(_dot(q_ref[rows, :].astype(BF16), k_ref[keys, :].astype(BF16), NT) * scale)
                da = _dot(dos[rows, :], v_ref[keys, :].astype(BF16), NT)
                causal = None
                if diagonal:
                    causal = (lax.broadcasted_iota(jnp.int32, (tq - r0, tk), 1)
                              < lax.broadcasted_iota(jnp.int32, (tq - r0, tk), 0))
                    lk = jnp.where(causal, lk, 0.0)
                scored.append((rows, keys, causal, lb, lk, da))
            summed = [(*blk, _split_dot(blk[4], later)) for blk in scored]
            weighted = []
            for rows, keys, causal, lb, lk, da, after in summed:
                p = pfx[rows, :] + _row_total(after, lk, 0)
                pfx[rows, :] = p
                a = jnp.exp(lb + after + _lanes(ct_ref[rows, :] - p, tk))
                if causal is not None:
                    a = jnp.where(causal, a, 0.0)
                dl = da * a
                weighted.append((rows, keys, causal, lb, a.astype(BF16), dl, _dot(dl.astype(BF16), earlier, NN)))
            cotangents = []
            for rows, keys, causal, lb, a, dl, before in weighted:
                gc = gcar[rows, :]
                gcar[rows, :] = gc + _row_total(before, dl, tk - 1)
                sig = jnp.exp(lb)
                gsum = (before + _lanes(gc, tk)) * sig
                if causal is not None:
                    gsum = jnp.where(causal, gsum, 0.0)
                cotangents.append((rows, keys, a, ((dl * (1.0 - sig) - gsum) * scale).astype(BF16)))
            for rows, keys, a, dz in cotangents:
                q, do = q_ref[rows, :].astype(BF16), dos[rows, :]
                dvacc[keys, :] += _dot(a, do, TN)
                dqacc[rows, :] += _dot(dz, k_ref[keys, :].astype(BF16), NN)
                dkacc[keys, :] += _dot(dz, q, TN)

        def step(j, carry):
            k0 = pl.multiple_of(j * 2 * tk, 2 * tk)
            run([(k0, 0, False), (pl.multiple_of(k0 + tk, tk), 0, False)])
            return carry

        lax.fori_loop(0, i * (tq // tk // 2), step, 0)
        for d in range(0, tq // tk, 2):
            run([(pl.multiple_of(i * tq + e * tk, tk), e * tk, True) for e in (d, d + 1)])
        dq_ref[...] = dqacc[...].astype(BF16)

        @pl.when(i == nq - 1)
        def _():
            dk_ref[...] = dkacc[...].astype(BF16)
            dv_ref[...] = dvacc[...].astype(BF16)

    blk = pl.BlockSpec((tq, HEAD_DIM), lambda h, i: (i, h))
    full = pl.BlockSpec((S, HEAD_DIM), lambda h, i: (0, h))
    W = H * HEAD_DIM
    return _pcall(
        body, grid=(H, nq),
        in_specs=[blk, pl.BlockSpec((S, HEAD_DIM), lambda h, i: (0, H + h)),
                  pl.BlockSpec((S, HEAD_DIM), lambda h, i: (0, 2 * H + h)), pl.BlockSpec((1, HEAD_DIM), lambda h, i: (0, h)),
                  blk, blk, pl.BlockSpec((tq, HEAD_DIM), lambda h, i: (i, dm_col0 + h))],
        out_specs=[blk, full, full, pl.BlockSpec((8, HEAD_DIM), lambda h, i: (0, h))],
        out_shape=[jax.ShapeDtypeStruct((S, W), BF16), jax.ShapeDtypeStruct((S, W), BF16),
                   jax.ShapeDtypeStruct((S, W), BF16), jax.ShapeDtypeStruct((8, W), F32)],
        scratch_shapes=[pltpu.VMEM((S, HEAD_DIM), F32), pltpu.VMEM((S, HEAD_DIM), F32), pltpu.VMEM((tq, HEAD_DIM), F32),
                        pltpu.VMEM((tq, LANES), F32), pltpu.VMEM((tq, LANES), F32), pltpu.VMEM((tq, HEAD_DIM), BF16)],
        compiler_params=_params("arbitrary", "arbitrary"), name=name)(proj, proj, proj, gain, o_raw, ctot, dmixed)


def _rope_tables(S):
    inv_freq = ROPE_THETA ** (-jnp.arange(0, HEAD_DIM, 2, dtype=F32) / HEAD_DIM)
    ang = jnp.arange(S, dtype=F32)[:, None] * inv_freq[None, :]
    cos, sin = jnp.cos(ang), jnp.sin(ang)
    return jnp.concatenate([cos, cos], axis=1), jnp.concatenate([-sin, sin], axis=1)


def _rope(v, cos2, sin_signed):
    return v * cos2 + pltpu.roll(v, HEAD_DIM // 2, axis=1) * sin_signed


def _dil_rows(d, r, l0, n):
    if d == 1:
        return pl.ds(l0 if isinstance(l0, int) else pl.multiple_of(l0, KEY_BLOCK), n)
    return pl.ds(r + d * l0, n, stride=d)


def _dil_blocks(S, visit):
    B = KEY_BLOCK
    group = 16
    for b, d in enumerate(DILATIONS):
        nb = S // d // B
        if nb == 1:
            g = math.gcd(d, group)

            def trip(t, carry, b=b, d=d, g=g):
                visit([(b, d, t * g + u, 0, True) for u in range(g)])
                return carry

            lax.fori_loop(0, d // g, trip, 0)
        elif d == 1:
            visit([(b, d, 0, 0, True)])
            g = max(k for k in range(1, group + 2) if (nb - 1) % k == 0)

            def trip(t, carry, b=b, d=d, g=g):
                visit([(b, d, 0, (1 + t * g + u) * B, False) for u in range(g)])
                return carry

            lax.fori_loop(0, (nb - 1) // g, trip, 0)
        else:
            g = math.gcd(d, max(group // nb, 1))

            def trip(t, carry, b=b, d=d, nb=nb, g=g):
                visit([(b, d, t * g + u, n * B, n == 0) for u in range(g) for n in range(nb)])
                return carry

            lax.fori_loop(0, d // g, trip, 0)


def _dil_mask(first):
    B = KEY_BLOCK
    nk = B if first else 2 * B
    iq = lax.broadcasted_iota(jnp.int32, (B, nk), 0)
    ik = lax.broadcasted_iota(jnp.int32, (B, nk), 1)
    return (ik <= iq) if first else ((ik >= iq) & (ik <= iq + B))


def _dil_fwd(proj, cos2, sin_signed, gain, mixed, col0, n_heads, name):
    S = proj.shape[0]
    H, B = n_heads, KEY_BLOCK
    scale = HEAD_DIM ** -0.5
    rc = _tile(S, 256, 8)

    def body(q_ref, k_ref, v_ref, c_ref, s_ref, g_ref, mixed_in, o_ref, l_ref, mx_ref, qr, kr, vf, *per_branch):
        ob, lb = per_branch[:len(DILATIONS)], per_branch[len(DILATIONS):]

        def rope_rows(t, carry):
            rows = pl.ds(pl.multiple_of(t * rc, rc), rc)
            qr[rows, :] = _rope(q_ref[rows, :].astype(F32), c_ref[rows, :], s_ref[rows, :])
            kr[rows, :] = _rope(k_ref[rows, :].astype(F32), c_ref[rows, :], s_ref[rows, :])
            vf[rows, :] = v_ref[rows, :].astype(F32)
            return carry

        lax.fori_loop(0, S // rc, rope_rows, 0)

        def visit(blocks):
            scores = []
            for b, d, r, l0, first in blocks:
                qrows = _dil_rows(d, r, l0, B)
                krows = qrows if first else _dil_rows(d, r, l0 - B, 2 * B)
                s = _dot(qr[qrows, :].astype(BF16), kr[krows, :].astype(BF16), NT) * scale
                scores.append((b, qrows, krows, jnp.where(_dil_mask(first), s, NEG)))
            weights = []
            for b, qrows, krows, s in scores:
                m = jnp.max(s, axis=1, keepdims=True)
                p = jnp.exp(s - m)
                den = jnp.sum(p, axis=1, keepdims=True)
                lb[b][qrows, :] = jnp.broadcast_to(m + jnp.log(den), (B, LANES))
                weights.append((b, qrows, krows, p.astype(BF16), den))
            for b, qrows, krows, p, den in weights:
                ob[b][qrows, :] = _dot(p, vf[krows, :].astype(BF16), NN) / den

        _dil_blocks(S, visit)

        def combine(t, carry):
            rows = pl.ds(pl.multiple_of(t * rc, rc), rc)
            l0, l1, l2 = lb[0][rows, :], lb[1][rows, :], lb[2][rows, :]
            m = jnp.maximum(jnp.maximum(l0, l1), l2)
            w0, w1, w2 = jnp.exp(l0 - m), jnp.exp(l1 - m), jnp.exp(l2 - m)
            den = w0 + w1 + w2
            o = (w0 * ob[0][rows, :] + w1 * ob[1][rows, :] + w2 * ob[2][rows, :]) / den
            o_ref[rows, :] = o
            l_ref[rows, :] = m + jnp.log(den)
            mx_ref[rows, :] = _head_out(o, g_ref[...]).astype(BF16)
            return carry

        lax.fori_loop(0, S // rc, combine, 0)

    def col(k):
        return pl.BlockSpec((S, HEAD_DIM), lambda h: (0, col0 + k * H + h))

    tab = pl.BlockSpec((S, HEAD_DIM), lambda h: (0, 0))
    out = pl.BlockSpec((S, HEAD_DIM), lambda h: (0, h))
    W = H * HEAD_DIM
    first = mixed.shape[1] // HEAD_DIM - H
    return _pcall(
        body, grid=(H,),
        in_specs=[col(0), col(1), col(2), tab, tab, pl.BlockSpec((1, HEAD_DIM), lambda h: (0, h)), HBM],
        out_specs=[out, out, pl.BlockSpec((S, HEAD_DIM), lambda h: (0, first + h))],
        out_shape=[jax.ShapeDtypeStruct((S, W), F32), jax.ShapeDtypeStruct((S, W), F32),
                   jax.ShapeDtypeStruct(mixed.shape, BF16)],
        input_output_aliases={6: 2},
        scratch_shapes=[pltpu.VMEM((S, HEAD_DIM), F32)] * (3 + 2 * len(DILATIONS)),
        compiler_params=_params("parallel"), name=name)(proj, proj, proj, cos2, sin_signed, gain, mixed)


def _dil_bwd(proj, cos2, sin_signed, gain, o_raw, lse, dmixed, dm_col0, col0, n_heads, name):
    S = proj.shape[0]
    H, B = n_heads, KEY_BLOCK
    scale = HEAD_DIM ** -0.5
    rc = _tile(S, 256, 8)

    def body(q_ref, k_ref, v_ref, c_ref, s_ref, g_ref, o_ref, l_ref, dm_ref, dq_ref, dk_ref, dv_ref, dg_ref,
             qr, kr, vf, dos, dsum, dqr, dkr, dvv):
        dg_ref[...] = jnp.zeros_like(dg_ref)

        def prep(t, carry):
            rows = pl.ds(pl.multiple_of(t * rc, rc), rc)
            qr[rows, :] = _rope(q_ref[rows, :].astype(F32), c_ref[rows, :], s_ref[rows, :])
            kr[rows, :] = _rope(k_ref[rows, :].astype(F32), c_ref[rows, :], s_ref[rows, :])
            vf[rows, :] = v_ref[rows, :].astype(F32)
            o, dm = o_ref[rows, :], dm_ref[rows, :].astype(F32)
            r = _rms_scale(o)
            do = _rms_bwd(dm * g_ref[...], o, r)
            dg_ref[...] += jnp.broadcast_to(jnp.sum(dm * o * r, axis=0, keepdims=True), dg_ref.shape)
            dos[rows, :] = do
            dsum[rows, :] = jnp.broadcast_to(jnp.sum(do * o, axis=1, keepdims=True), (rc, LANES))
            dqr[rows, :] = jnp.zeros((rc, HEAD_DIM), F32)
            dkr[rows, :] = jnp.zeros((rc, HEAD_DIM), F32)
            dvv[rows, :] = jnp.zeros((rc, HEAD_DIM), F32)
            return carry

        lax.fori_loop(0, S // rc, prep, 0)

        def visit(blocks):
            products = []
            for b, d, r, l0, first in blocks:
                qrows = _dil_rows(d, r, l0, B)
                krows = qrows if first else _dil_rows(d, r, l0 - B, 2 * B)
                qs, ks = qr[qrows, :].astype(BF16), kr[krows, :].astype(BF16)
                do = dos[qrows, :].astype(BF16)
                s = jnp.where(_dil_mask(first), _dot(qs, ks, NT) * scale, NEG)
                dp = _dot(do, vf[krows, :].astype(BF16), NT)
                products.append((qrows, krows, qs, ks, do, s, dp))
            cotangents = []
            for qrows, krows, qs, ks, do, s, dp in products:
                p = jnp.exp(s - l_ref[qrows, :][:, 0:1])
                ds = (p * (dp - dsum[qrows, :][:, 0:1]) * scale).astype(BF16)
                cotangents.append((qrows, krows, qs, ks, do, p.astype(BF16), ds))
            for qrows, krows, qs, ks, do, p, ds in cotangents:
                dqr[qrows, :] += _dot(ds, ks, NN)
                dkr[krows, :] += _dot(ds, qs, TN)
                dvv[krows, :] += _dot(p, do, TN)

        _dil_blocks(S, visit)

        def finish(t, carry):
            rows = pl.ds(pl.multiple_of(t * rc, rc), rc)
            c, s = c_ref[rows, :], s_ref[rows, :]
            dq, dk = dqr[rows, :], dkr[rows, :]
            dq_ref[rows, :] = (dq * c + pltpu.roll(dq * s, HEAD_DIM // 2, axis=1)).astype(BF16)
            dk_ref[rows, :] = (dk * c + pltpu.roll(dk * s, HEAD_DIM // 2, axis=1)).astype(BF16)
            dv_ref[rows, :] = dvv[rows, :].astype(BF16)
            return carry

        lax.fori_loop(0, S // rc, finish, 0)

    def col(k):
        return pl.BlockSpec((S, HEAD_DIM), lambda h: (0, col0 + k * H + h))

    tab = pl.BlockSpec((S, HEAD_DIM), lambda h: (0, 0))
    out = pl.BlockSpec((S, HEAD_DIM), lambda h: (0, h))
    W = H * HEAD_DIM
    big = pltpu.VMEM((S, HEAD_DIM), F32)
    return _pcall(
        body, grid=(H,),
        in_specs=[col(0), col(1), col(2), tab, tab, pl.BlockSpec((1, HEAD_DIM), lambda h: (0, h)), out, out,
                  pl.BlockSpec((S, HEAD_DIM), lambda h: (0, dm_col0 + h))],
        out_specs=[out, out, out, pl.BlockSpec((8, HEAD_DIM), lambda h: (0, h))],
        out_shape=[jax.ShapeDtypeStruct((S, W), BF16), jax.ShapeDtypeStruct((S, W), BF16),
                   jax.ShapeDtypeStruct((S, W), BF16), jax.ShapeDtypeStruct((8, W), F32)],
        scratch_shapes=[big, big, big, big, pltpu.VMEM((S, LANES), F32), big, big, big],
        compiler_params=_params("parallel"), name=name)(proj, proj, proj, cos2, sin_signed, gain, o_raw, lse, dmixed)


GELU_C = math.sqrt(2.0 / math.pi)
GELU_A = 0.044715
HALO = 16


def _shift_down(cur, halo, k):
    out = pltpu.roll(cur, k, axis=0)
    row = lax.broadcasted_iota(jnp.int32, cur.shape, 0)
    for t in range(k):
        out = jnp.where(row == t, halo[HALO - k + t:HALO - k + t + 1, :], out)
    return out


def _shift_up(cur, halo, k):
    n = cur.shape[0]
    out = pltpu.roll(cur, n - k, axis=0)
    row = lax.broadcasted_iota(jnp.int32, cur.shape, 0)
    for t in range(k):
        out = jnp.where(row == n - k + t, halo[t:t + 1, :], out)
    return out


def _conv3(cur, halo, cw):
    return _shift_down(cur, halo, 2) * cw[0:1, :] + _shift_down(cur, halo, 1) * cw[1:2, :] + cur * cw[2:3, :] + cw[3:4, :]


def _gelu_parts(x):
    t = jnp.tanh(GELU_C * (x + GELU_A * x * x * x))
    return 0.5 * x * (1.0 + t), t


def _geglu_specs(tm, tn, ncb):
    hb = tm // HALO

    def cur(off):
        return pl.BlockSpec((tm, tn), lambda j, i: (i, off + j))

    def prev(off):
        return pl.BlockSpec((HALO, tn), lambda j, i: (jnp.maximum(i * hb - 1, 0), off + j))

    def taps(off):
        return pl.BlockSpec((8, tn), lambda j, i: (0, off + j))

    return [cur(0), prev(0), cur(ncb), prev(ncb), taps(0), taps(ncb)]


def _geglu_fwd(u, cwb, name, tm=256, tn=1408):
    S, F2 = u.shape
    F = F2 // 2
    tm, tn = _tile(S, tm, HALO), _tile(F, tn)
    ncb = F // tn

    def body(g_ref, gp_ref, v_ref, vp_ref, cg_ref, cv_ref, y_ref):
        top = pl.program_id(1) > 0
        gp = jnp.where(top, gp_ref[...].astype(F32), 0.0)
        vp = jnp.where(top, vp_ref[...].astype(F32), 0.0)
        gc = _conv3(g_ref[...].astype(F32), gp, cg_ref[...])
        vc = _conv3(v_ref[...].astype(F32), vp, cv_ref[...])
        y_ref[...] = (_gelu_parts(gc)[0] * vc).astype(BF16)

    return _pcall(body, grid=(ncb, S // tm), in_specs=_geglu_specs(tm, tn, ncb),
                  out_specs=pl.BlockSpec((tm, tn), lambda j, i: (i, j)),
                  out_shape=jax.ShapeDtypeStruct((S, F), BF16),
                  compiler_params=_params("parallel", "parallel"), name=name)(u, u, u, u, cwb, cwb)


def _geglu_bwd(u, dy, cwb, name, tm=256, tn=512):
    S, F2 = u.shape
    F = F2 // 2
    tm, tn = _tile(S, tm, HALO), _tile(F, tn)
    ncb = F // tn

    def body(g_ref, gp_ref, v_ref, vp_ref, cg_ref, cv_ref, dy_ref, dc_ref, dwg_ref, dwv_ref):
        i = pl.program_id(1)

        @pl.when(i == 0)
        def _():
            dwg_ref[...] = jnp.zeros_like(dwg_ref)
            dwv_ref[...] = jnp.zeros_like(dwv_ref)

        top = i > 0
        g, v = g_ref[...].astype(F32), v_ref[...].astype(F32)
        gp = jnp.where(top, gp_ref[...].astype(F32), 0.0)
        vp = jnp.where(top, vp_ref[...].astype(F32), 0.0)
        gc = _conv3(g, gp, cg_ref[...])
        vc = _conv3(v, vp, cv_ref[...])
        act, t = _gelu_parts(gc)
        dact = 0.5 * (1.0 + t) + 0.5 * gc * (1.0 - t * t) * GELU_C * (1.0 + 3.0 * GELU_A * gc * gc)
        dyv = dy_ref[...].astype(F32)
        dgc = dyv * vc * dact
        dvc = dyv * act
        dc_ref[0] = dgc.astype(BF16)
        dc_ref[1] = dvc.astype(BF16)

        def taps(out_ref, dc, cur, halo):
            out_ref[0:1, :] += jnp.sum(dc * _shift_down(cur, halo, 2), axis=0, keepdims=True)
            out_ref[1:2, :] += jnp.sum(dc * _shift_down(cur, halo, 1), axis=0, keepdims=True)
            out_ref[2:3, :] += jnp.sum(dc * cur, axis=0, keepdims=True)
            out_ref[3:4, :] += jnp.sum(dc, axis=0, keepdims=True)

        taps(dwg_ref, dgc, g, gp)
        taps(dwv_ref, dvc, v, vp)

    return _pcall(body, grid=(ncb, S // tm),
                  in_specs=_geglu_specs(tm, tn, ncb) + [pl.BlockSpec((tm, tn), lambda j, i: (i, j))],
                  out_specs=[pl.BlockSpec((2, tm, tn), lambda j, i: (0, i, j)),
                             pl.BlockSpec((8, tn), lambda j, i: (0, j)), pl.BlockSpec((8, tn), lambda j, i: (0, j))],
                  out_shape=[jax.ShapeDtypeStruct((2, S, F), BF16), jax.ShapeDtypeStruct((8, F), F32),
                             jax.ShapeDtypeStruct((8, F), F32)],
                  compiler_params=_params("parallel", "arbitrary"), name=name)(u, u, u, u, cwb, cwb, dy)


def _conv_bwd(dc, cwb, name, tm=512, tn=1408):
    _, S, F = dc.shape
    tm, tn = _tile(S, tm, HALO), _tile(F, tn)
    ncb, nrb = F // tn, S // tm
    hb = tm // HALO

    def body(c_ref, n_ref, w_ref, du_ref):
        cur = c_ref[...].astype(F32)
        nxt = jnp.where(pl.program_id(2) < nrb - 1, n_ref[...].astype(F32), 0.0)
        w = w_ref[...]
        du = cur * w[2:3, :] + _shift_up(cur, nxt, 1) * w[1:2, :] + _shift_up(cur, nxt, 2) * w[0:1, :]
        du_ref[...] = du.astype(BF16)

    return _pcall(body, grid=(2, ncb, nrb),
                  in_specs=[pl.BlockSpec((None, tm, tn), lambda c, j, i: (c, i, j)),
                            pl.BlockSpec((None, HALO, tn), lambda c, j, i: (c, jnp.minimum((i + 1) * hb, S // HALO - 1), j)),
                            pl.BlockSpec((8, tn), lambda c, j, i: (0, c * ncb + j))],
                  out_specs=pl.BlockSpec((tm, tn), lambda c, j, i: (i, c * ncb + j)),
                  out_shape=jax.ShapeDtypeStruct((S, 2 * F), BF16),
                  compiler_params=_params("parallel", "parallel", "parallel"), name=name)(dc, dc, cwb)


def _adam_math(w, g, m, v):
    m = ADAM_B1 * m + (1.0 - ADAM_B1) * g
    v = ADAM_B2 * v + (1.0 - ADAM_B2) * (g * g)
    m_hat = m / (1.0 - ADAM_B1 ** ADAM_STEP)
    v_hat = v / (1.0 - ADAM_B2 ** ADAM_STEP)
    return -ADAM_LR * (m_hat / (jnp.sqrt(v_hat) + ADAM_EPS) + ADAM_WD * w), m, v


def _adamw(w, parts, m, v, name, tr=256):
    R, C = w.shape
    n, _, Cp = parts.shape
    tr = _tile(R, tr, 8)

    def body(w_ref, p_ref, m_ref, v_ref, g_out, d_out, m_out, v_out):
        g = p_ref[0, :, 0:C].astype(F32)
        for k in range(1, n):
            g = g + p_ref[k, :, 0:C].astype(F32)
        d, mn, vn = _adam_math(w_ref[...], g, m_ref[...], v_ref[...])
        g_out[...] = g
        d_out[...] = d
        m_out[...] = mn
        v_out[...] = vn

    spec = pl.BlockSpec((tr, C), lambda i: (i, 0))
    shape = jax.ShapeDtypeStruct((R, C), F32)
    return _pcall(body, grid=(R // tr,), in_specs=[spec, pl.BlockSpec((n, tr, Cp), lambda i: (0, i, 0)), spec, spec],
                  out_specs=[spec] * 4, out_shape=[shape] * 4, compiler_params=_params("parallel"), name=name)(w, parts, m, v)


def _adamw_chips(w, pair, parts, chip_ids, m, v, name, tr=256):
    R, C = w.shape
    Cp = pair.shape[2]
    by_columns = C == Cp and _tile(R, tr, 16) < 64
    tr, tc = (R, _tile(C, 256)) if by_columns else (_tile(R, tr, 16), C)

    def body(ids_ref, w_ref, own_ref, p1_ref, p2_ref, p3_ref, m_ref, v_ref, g_out, d_out, m_out, v_out):
        g = own_ref[:, 0:tc].astype(F32)
        for ref in (p1_ref, p2_ref, p3_ref):
            g = g + ref[:, 0:tc].astype(F32)
        d, mn, vn = _adam_math(w_ref[...], g, m_ref[...], v_ref[...])
        g_out[...] = g
        d_out[...] = d
        m_out[...] = mn
        v_out[...] = vn

    if by_columns:
        spec = pl.BlockSpec((tr, tc), lambda j, ids: (0, j))
    else:
        spec = pl.BlockSpec((tr, tc), lambda i, ids: (i, 0))

    def chip(k):
        if by_columns:
            return pl.BlockSpec((None, tr, tc), lambda j, ids: (ids[k], 0, j))
        return pl.BlockSpec((None, tr, Cp), lambda i, ids: (ids[k], i, 0))

    shape = jax.ShapeDtypeStruct((R, C), F32)
    grid_spec = pltpu.PrefetchScalarGridSpec(
        num_scalar_prefetch=1, grid=(C // tc if by_columns else R // tr,),
        in_specs=[spec, chip(0), chip(1), chip(2), chip(3), spec, spec], out_specs=[spec] * 4)
    return _pcall(body, grid_spec=grid_spec, out_shape=[shape] * 4, compiler_params=_params("parallel"),
                  name=name)(chip_ids, w, pair, parts, parts, parts, m, v)


def _place():
    return lax.axis_index("x"), lax.axis_index("y"), lax.axis_index("c")


def _other_chips(x, y):
    return [(1 - x, y), (x, 1 - y), (1 - x, 1 - y)]


IN_HBM = pl.BlockSpec(memory_space=pltpu.HBM)
SEM = pl.BlockSpec(memory_space=pltpu.SEMAPHORE)
EFFECT = pltpu.SideEffectType.DATAFLOW_SIDE_EFFECTING
TOKEN = jax.ShapeDtypeStruct((8, LANES), F32)
TOKEN_SPEC = pl.BlockSpec(memory_space=pltpu.VMEM)


def _in_hbm(a):
    return pltpu.with_memory_space_constraint(a, pltpu.HBM)


def _landing(shape):
    return _in_hbm(lax.empty(shape.shape, shape.dtype))


def _hbm_like(a):
    return pltpu.HBM(a.shape, a.dtype)


def _gather_start(landing, slots, after, name):
    na = len(landing)

    def body(*refs):
        land = refs[:na]
        send_sems, recv_sems = refs[na + 1], refs[na + 2]
        token = refs[-1]
        x, y, c = _place()
        for a in range(na):
            own = slots[a](land[a], x, y, c)
            for k, to in enumerate([(x, y, 1 - c)] + [(*chip, c) for chip in _other_chips(x, y)]):
                pltpu.make_async_remote_copy(
                    src_ref=own, dst_ref=own, send_sem=send_sems.at[4 * a + k],
                    recv_sem=recv_sems.at[4 * a + k], device_id=to, device_id_type=MESH).start()
        token[...] = jnp.zeros_like(token)

    sems = pltpu.SemaphoreType.DMA((4 * na,))
    outs = _pcall(
        body, in_specs=[IN_HBM] * na + [HBM],
        out_specs=[SEM, SEM] + [IN_HBM] * na + [TOKEN_SPEC],
        out_shape=[sems, sems] + [_hbm_like(s) for s in landing] + [TOKEN],
        input_output_aliases={a: 2 + a for a in range(na)},
        compiler_params=pltpu.CompilerParams(has_side_effects=EFFECT), name=name,
    )(*[_in_hbm(s) for s in landing], after)
    return outs[0], outs[1], outs[2:2 + na], outs[-1]


def _gather_forward(gathered, send_sems, recv_sems, slots, after, name):
    na = len(gathered)

    def body(*refs):
        gath = refs[:na]
        send1, recv1 = refs[na], refs[na + 1]
        fsend, frecv = refs[na + 3], refs[na + 4]
        token = refs[-1]
        x, y, c = _place()
        chips = _other_chips(x, y)
        for a in range(na):
            for k, peer in enumerate([(x, y, 1 - c)] + [(*chip, c) for chip in chips]):
                arrival = pltpu.make_async_remote_copy(
                    src_ref=slots[a](gath[a], x, y, c), dst_ref=slots[a](gath[a], *peer), send_sem=send1.at[4 * a + k],
                    recv_sem=recv1.at[4 * a + k], device_id=peer, device_id_type=MESH)
                arrival.wait_send()
                arrival.wait_recv()
        for a in range(na):
            for j, chip in enumerate(chips):
                view = slots[a](gath[a], *chip, c)
                pltpu.make_async_remote_copy(
                    src_ref=view, dst_ref=view, send_sem=fsend.at[3 * a + j], recv_sem=frecv.at[3 * a + j],
                    device_id=(x, y, 1 - c), device_id_type=MESH).start()
        token[...] = jnp.zeros_like(token)

    sems = pltpu.SemaphoreType.DMA((3 * na,))
    outs = _pcall(
        body, in_specs=[IN_HBM] * na + [SEM, SEM, HBM],
        out_specs=[SEM, SEM] + [IN_HBM] * na + [TOKEN_SPEC],
        out_shape=[sems, sems] + [_hbm_like(g) for g in gathered] + [TOKEN],
        input_output_aliases={a: 2 + a for a in range(na)},
        compiler_params=pltpu.CompilerParams(has_side_effects=EFFECT), name=name,
    )(*gathered, send_sems, recv_sems, after)
    return outs[0], outs[1], outs[2:2 + na], outs[-1]


def _gather_finish(gathered, fsend, frecv, slots, after, name):
    na = len(gathered)

    def body(*refs):
        gath, fs, fr = refs[:na], refs[na], refs[na + 1]
        x, y, c = _place()
        for a in range(na):
            for j, chip in enumerate(_other_chips(x, y)):
                passed = pltpu.make_async_remote_copy(
                    src_ref=slots[a](gath[a], *chip, c), dst_ref=slots[a](gath[a], *chip, 1 - c),
                    send_sem=fs.at[3 * a + j], recv_sem=fr.at[3 * a + j], device_id=(x, y, 1 - c), device_id_type=MESH)
                passed.wait_send()
                passed.wait_recv()

    outs = _pcall(
        body, in_specs=[IN_HBM] * na + [SEM, SEM, HBM], out_specs=[IN_HBM] * na,
        out_shape=[_hbm_like(g) for g in gathered], input_output_aliases={a: a for a in range(na)},
        compiler_params=pltpu.CompilerParams(has_side_effects=EFFECT), name=name,
    )(*gathered, fsend, frecv, after)
    return list(outs)


def _pair_copy(view, src, land, send_sems, recv_sems, chip):
    x, y, c = _place()
    return pltpu.make_async_remote_copy(
        src_ref=view(src, chip, 1 - c), dst_ref=land.at[chip], send_sem=send_sems.at[chip], recv_sem=recv_sems.at[chip],
        device_id=(x, y, 1 - c), device_id_type=MESH)


def _pair_start(grad, view, block, after, name):
    def body(src, land, after_ref, send_sems, recv_sems, src_thru, land_thru, token):
        for chip in range(N_CHIP):
            _pair_copy(view, src, land, send_sems, recv_sems, chip).start()
        token[...] = jnp.zeros_like(token)

    sems = pltpu.SemaphoreType.DMA((N_CHIP,))
    land = jax.ShapeDtypeStruct((N_CHIP, *block), BF16)
    return _pcall(
        body, in_specs=[IN_HBM, IN_HBM, HBM], out_specs=[SEM, SEM, IN_HBM, IN_HBM, TOKEN_SPEC],
        out_shape=[sems, sems, _hbm_like(grad), _hbm_like(land), TOKEN], input_output_aliases={0: 2, 1: 3},
        compiler_params=pltpu.CompilerParams(has_side_effects=EFFECT), name=name,
    )(_in_hbm(grad), _landing(land), after)


def _pair_wait(grad, recv, send_sems, recv_sems, view, after, name):
    def body(src, land, send, recv_s, after_ref, src_thru, land_thru):
        for chip in range(N_CHIP):
            copy = _pair_copy(view, src, land, send, recv_s, chip)
            copy.wait_send()
            copy.wait_recv()

    return _pcall(
        body, in_specs=[IN_HBM, IN_HBM, SEM, SEM, HBM], out_specs=[IN_HBM, IN_HBM],
        out_shape=[_hbm_like(grad), _hbm_like(recv)], input_output_aliases={0: 0, 1: 1},
        compiler_params=pltpu.CompilerParams(has_side_effects=EFFECT), name=name,
    )(grad, recv, send_sems, recv_sems, after)


def _chip_start(pair, after, name):
    def body(src, land, after_ref, send_sems, recv_sems, src_thru, land_thru, token):
        x, y, c = _place()
        for j, (px, py) in enumerate(_other_chips(x, y)):
            pltpu.make_async_remote_copy(
                src_ref=src.at[2 * px + py], dst_ref=land.at[2 * x + y], send_sem=send_sems.at[j], recv_sem=recv_sems.at[j],
                device_id=(px, py, c), device_id_type=MESH).start()
        token[...] = jnp.zeros_like(token)

    sems = pltpu.SemaphoreType.DMA((3,))
    return _pcall(
        body, in_specs=[IN_HBM, IN_HBM, HBM], out_specs=[SEM, SEM, IN_HBM, IN_HBM, TOKEN_SPEC],
        out_shape=[sems, sems, _hbm_like(pair), _hbm_like(pair), TOKEN], input_output_aliases={0: 2, 1: 3},
        compiler_params=pltpu.CompilerParams(has_side_effects=EFFECT), name=name,
    )(_in_hbm(pair), _landing(pair), after)


def _chip_wait(pair, parts, send_sems, recv_sems, after, name):
    def body(src, land, send, recv, after_ref, src_thru, land_thru):
        x, y, c = _place()
        for j, (px, py) in enumerate(_other_chips(x, y)):
            copy = pltpu.make_async_remote_copy(
                src_ref=src.at[2 * px + py], dst_ref=land.at[2 * px + py], send_sem=send.at[j], recv_sem=recv.at[j],
                device_id=(px, py, c), device_id_type=MESH)
            copy.wait_send()
            copy.wait_recv()

    return _pcall(
        body, in_specs=[IN_HBM, IN_HBM, SEM, SEM, HBM], out_specs=[IN_HBM, IN_HBM],
        out_shape=[_hbm_like(pair), _hbm_like(parts)], input_output_aliases={0: 0, 1: 1},
        compiler_params=pltpu.CompilerParams(has_side_effects=EFFECT), name=name,
    )(pair, parts, send_sems, recv_sems, after)


def _pair_add(core, grad, recv, block, grad_spec, name):
    _, R, C = recv.shape
    tr = block

    def body(c_ref, g_ref, r_ref, o_ref):
        o_ref[...] = (g_ref[...].astype(F32) + r_ref[...].astype(F32)).astype(BF16)

    grid_spec = pltpu.PrefetchScalarGridSpec(
        num_scalar_prefetch=1, grid=(N_CHIP, R // tr),
        in_specs=[grad_spec, pl.BlockSpec((None, tr, C), lambda k, i, c: (k, i, 0))],
        out_specs=pl.BlockSpec((None, tr, C), lambda k, i, c: (k, i, 0)))
    return _pcall(body, grid_spec=grid_spec, out_shape=jax.ShapeDtypeStruct(recv.shape, BF16),
                  compiler_params=_params("parallel", "parallel"), name=name)(core, grad, recv)


def _small_copies(gath, send_sems, recv_sems):
    x, y, c = _place()
    peers = [(x, y, 1 - c)] + [(px, py, pc) for px, py in _other_chips(x, y) for pc in (c, 1 - c)]
    pairs = []
    for a, ref in enumerate(gath):
        mine = ref.at[4 * x + 2 * y + c]
        for k, (px, py, pc) in enumerate(peers):
            sems = dict(send_sem=send_sems.at[7 * a + k], recv_sem=recv_sems.at[7 * a + k], device_id=(px, py, pc),
                        device_id_type=MESH)
            pairs.append((pltpu.make_async_remote_copy(src_ref=mine, dst_ref=mine, **sems),
                          pltpu.make_async_remote_copy(src_ref=mine, dst_ref=ref.at[4 * px + 2 * py + pc], **sems)))
    return pairs


def _small_start(landing, after, name):
    na = len(landing)

    def body(*refs):
        for send, _ in _small_copies(refs[:na], refs[na + 1], refs[na + 2]):
            send.start()
        refs[-1][...] = jnp.zeros_like(refs[-1])

    sems = pltpu.SemaphoreType.DMA((7 * na,))
    outs = _pcall(
        body, in_specs=[IN_HBM] * na + [HBM], out_specs=[SEM, SEM] + [IN_HBM] * na + [TOKEN_SPEC],
        out_shape=[sems, sems] + [_hbm_like(s) for s in landing] + [TOKEN],
        input_output_aliases={a: 2 + a for a in range(na)},
        compiler_params=pltpu.CompilerParams(has_side_effects=EFFECT), name=name,
    )(*[_in_hbm(s) for s in landing], after)
    return outs[0], outs[1], outs[2:2 + na], outs[-1]


def _small_wait(gathered, send_sems, recv_sems, after, name):
    na = len(gathered)

    def body(*refs):
        for send, arrival in _small_copies(refs[:na], refs[na], refs[na + 1]):
            send.wait_send()
            arrival.wait_recv()

    return list(_pcall(
        body, in_specs=[IN_HBM] * na + [SEM, SEM, HBM], out_specs=[IN_HBM] * na,
        out_shape=[_hbm_like(g) for g in gathered], input_output_aliases={a: a for a in range(na)},
        compiler_params=pltpu.CompilerParams(has_side_effects=EFFECT), name=name,
    )(*gathered, send_sems, recv_sems, after))


def _small_finish(gathered, params, name):
    na, npar = len(gathered), len(params)

    def body(*refs):
        g_refs, wmv = refs[:na], refs[na:na + 3 * npar]
        o_sums, o_params = refs[na + 3 * npar:2 * na + 3 * npar], refs[2 * na + 3 * npar:]
        sums = []
        for a in range(na):
            acc = g_refs[a][0]
            for k in range(1, N_DEV):
                acc = acc + g_refs[a][k]
            o_sums[a][...] = acc
            sums.append(acc)
        for j, (a, row, _, _, _) in enumerate(params):
            g = sums[a][row:row + 1, :]
            d, mn, vn = _adam_math(wmv[3 * j][...], g, wmv[3 * j + 1][...], wmv[3 * j + 2][...])
            for out, val in zip(o_params[4 * j:4 * j + 4], (g, d, mn, vn)):
                out[...] = val

    vm = pl.BlockSpec(memory_space=pltpu.VMEM)
    flat = [t for p in params for t in p[2:]]
    out_shape = [jax.ShapeDtypeStruct(g.shape[1:], F32) for g in gathered]
    out_shape += [jax.ShapeDtypeStruct(p[2].shape, F32) for p in params for _ in range(4)]
    outs = _pcall(body, in_specs=[vm] * (na + 3 * npar), out_specs=[vm] * len(out_shape), out_shape=out_shape,
                  name=name)(*gathered, *flat)
    return outs[:na], [outs[na + 4 * j:na + 4 * j + 4] for j in range(npar)]


def _local_step(x, tgt, gains, weights):
    g_pre_mix, g_post_mix, g_pre_ffn, g_post_ffn, g_sb, g_dil = gains
    S, D = x.shape
    hs = g_sb.shape[1] // HEAD_DIM
    hd = g_dil.shape[1] // HEAD_DIM
    cos2, sin_signed = _rope_tables(S)

    h1 = _rms_fwd(x, g_pre_mix + weights.start(), "rms_in")
    w_in_g = weights.w_in(h1)
    proj = _mm_nn(h1, w_in_g, BF16, "proj", tn=768)
    o_sb, ct_sb, mixed = _sb_fwd(proj, g_sb, hs, hs + hd, "sb_fwd")
    o_dl, lse_dl, mixed = _dil_fwd(proj, cos2, sin_signed, g_dil + weights.forward_out(o_sb), mixed, 3 * hs, hd, "dil_fwd")
    w_out_g = weights.w_out(o_dl)
    mix = _mm_nn(mixed, w_out_g, F32, "mix_out", tn=1024)
    x2, h2 = _mid_fwd(x, mix, g_post_mix + weights.forward_up(mix), g_pre_ffn, "mid_fwd")
    w_up_g, cwb = weights.w_up(h2)
    u = _mm_nn(h2, w_up_g, BF16, "ffn_up", b_transposed=True)
    y = _geglu_fwd(u, cwb + weights.forward_down(u), "geglu_fwd")
    w_down_g = weights.w_down(y)
    f = _mm_nn(y, w_down_g, F32, "ffn_down", tn=1024, tk=2816)

    dy, df, dg_post_ffn, loss = _loss_bwd(x2, f, tgt, g_post_ffn, "loss_bwd")
    dyv = _mm_nt(df, w_down_g, BF16, "d_y", tn=1408)
    dw_down = _mm_tn(y, df, D, BF16, "dw_down", tm=1408, tn=1024)
    dc, dcw_g, dcw_v = _geglu_bwd(u, dyv, cwb + weights.grad("w_down", dw_down), "geglu_bwd")
    du = _conv_bwd(dc, cwb + weights.grad_reduce("w_down", dc), "conv_bwd")
    dh2 = _mm_nt(du, w_up_g, BF16, "d_h2", tk=1408, b_transposed=True, per_step=2)
    dw_up = _mm_tn(du, h2, D, BF16, "dw_up", tm=1408, tn=1024)
    dx2, dmix, dg_pre_ffn, dg_post_mix = _mid_bwd(
        dy, dh2, x2, mix, g_pre_ffn + weights.grad("w_up", dw_up), g_post_mix, "mid_bwd")
    dmixed = _mm_nt(dmix, w_out_g, BF16, "d_mixed", after=jnp.reshape(weights.grad_reduce("w_up", dmix), (1, 1)))
    dw_out = _mm_tn(mixed, dmix, D, BF16, "dw_out", tn=1024)
    dq_s, dk_s, dv_s, dg_sb = _sb_bwd(proj, g_sb + weights.grad("w_out", dw_out), o_sb, ct_sb, dmixed, 0, hs, "sb_bwd")
    dq_d, dk_d, dv_d, dg_dil = _dil_bwd(proj, cos2, sin_signed, g_dil + weights.grad_reduce("w_out", dq_s), o_dl, lse_dl,
                                        dmixed, hs, 3 * hs, hd, "dil_bwd")
    dproj = jnp.concatenate([dq_s, dk_s, dv_s, dq_d, dk_d, dv_d], axis=1)
    dw_in = _mm_tn(h1, dproj, w_in_g.shape[2], BF16, "dw_in", tn=768)
    weights.grad("w_in", dw_in)
    dep = weights.grad_reduce("w_in", dproj)
    dh1 = _mm_nt(dproj, w_in_g, BF16, "d_h1", tk=768, after=jnp.reshape(dep, (1, 1)), per_step=4)
    grad_x, dg_pre_mix = _first_bwd(dx2, dh1, x, g_pre_mix, "first_bwd")
    small = (dg_pre_mix, dg_post_mix, dg_pre_ffn, dg_post_ffn, dg_sb[0:1], dg_dil[0:1], jnp.concatenate([dcw_g, dcw_v], axis=1))
    weights.small(small, loss)
    return loss, grad_x, small


def _pad_cols(a, to):
    return jnp.pad(a, ((0, 0), (0, to - a.shape[1])))


def kernel(x, pre_mix_gain, post_mix_gain, pre_ffn_gain, post_ffn_gain, w_in, sb_out_gain, dil_out_gain, w_out, w_up, conv_w, conv_b, w_down, loss_target, m_pre_mix_gain, m_post_mix_gain, m_pre_ffn_gain, m_post_ffn_gain, m_w_in, m_sb_out_gain, m_dil_out_gain, m_w_out, m_w_up, m_conv_w, m_conv_b, m_w_down, v_pre_mix_gain, v_post_mix_gain, v_pre_ffn_gain, v_post_ffn_gain, v_w_in, v_sb_out_gain, v_dil_out_gain, v_w_out, v_w_up, v_conv_w, v_conv_b, v_w_down):
    xb, tb = x[0], loss_target[0]
    S, D = xb.shape
    w_in, w_out, w_up, w_down, conv_w = w_in[0], w_out[0], w_up[0], w_down[0], conv_w[0]
    n_in, e_rows = w_in.shape[1], w_out.shape[0]
    cu, half = w_up.shape[1], w_down.shape[0]
    assert cu == 2 * half and half % 16 == 0
    cup = -(-cu // LANES) * LANES
    fp = N_CHIP * cup
    px, py, pc = _place()
    me = 4 * px + 2 * py + pc
    core = jnp.reshape(pc, (1,)).astype(jnp.int32)

    w_up_t, m_up_t, v_up_t = (jnp.swapaxes(t, 0, 1) for t in (w_up, m_w_up[0], v_w_up[0]))

    def by_dev(ref, qx, qy, qc):
        return ref.at[4 * qx + 2 * qy + qc]

    def down_slot(ref, qx, qy, qc):
        return ref.at[2 * qx + qy, pl.ds(qc * half, half)]

    def by_pair(ref, chip, k):
        return ref.at[chip, k]

    def down_pair(ref, chip, k):
        return ref.at[chip, pl.ds(k * half, half)]

    def pair_spec(tr, cols):
        return pl.BlockSpec((None, None, tr, cols), lambda k, i, c: (k, c[0], i, 0))

    tr_in, tr_up = _tile(D, 512, 16), _tile(cup, 256, 16)
    grad_plan = {
        "w_in": ((N_CHIP, 2, D, n_in), by_pair, (D, n_in), tr_in, pair_spec(tr_in, n_in)),
        "w_out": ((N_CHIP, 2, e_rows, D), by_pair, (e_rows, D), e_rows, pair_spec(e_rows, D)),
        "w_up": ((N_CHIP, 2, cup, D), by_pair, (cup, D), tr_up, pair_spec(tr_up, D)),
        "w_down": ((N_CHIP, cup, D), down_pair, (half, D), half,
                   pl.BlockSpec((None, half, D), lambda k, i, c: (k, c[0], 0))),
    }

    class Exchanges:
        def __init__(self):
            self.in_flight = {}

        def start(self):
            def own_slot(shard):
                return lax.dynamic_update_index_in_dim(lax.empty((N_DEV, *shard.shape), shard.dtype), shard, me, 0)

            self.g_in = _gather_start([own_slot(w_in.astype(BF16))], [by_dev], core, "gather_in_start")
            zero = self.g_in[3][0, 0]
            self.g_out = _gather_start([own_slot((w_out + zero).astype(BF16))], [by_dev], self.g_in[3], "gather_out_start")
            up = jnp.pad(w_up_t + zero, ((0, cup - cu), (0, 0))).astype(BF16)
            taps = jnp.pad(conv_w + zero, ((0, 8 - conv_w.shape[0]), (0, cup - cu)))
            self.g_up = _gather_start([own_slot(up), own_slot(taps)], [by_dev, by_dev], self.g_out[3], "gather_up_start")
            down = lax.dynamic_update_slice(jnp.zeros((N_CHIP, cup, D), BF16), (w_down + zero).astype(BF16)[None],
                                            (2 * px + py, pc * half, 0))
            self.g_down = _gather_start([down], [down_slot], self.g_up[3], "gather_down_start")
            return self.g_down[3][0, 0]

        def w_in(self, after):
            send, recv, gath, _ = self.g_in
            fsend, frecv, gath, token = _gather_forward(gath, send, recv, [by_dev], after, "gather_in_forward")
            return _gather_finish(gath, fsend, frecv, [by_dev], token, "gather_in_finish")[0]

        def forward_out(self, after):
            send, recv, gath, _ = self.g_out
            self.p_out = _gather_forward(gath, send, recv, [by_dev], after, "gather_out_forward")
            return self.p_out[3][0, 0]

        def w_out(self, after):
            fsend, frecv, gath, _ = self.p_out
            w_out_g = _gather_finish(gath, fsend, frecv, [by_dev], after, "gather_out_finish")[0]
            return w_out_g.reshape(1, N_DEV * e_rows, D)

        def forward_up(self, after):
            send, recv, gath, _ = self.g_up
            self.p_up = _gather_forward(gath, send, recv, [by_dev, by_dev], after, "gather_up_forward")
            return self.p_up[3][0, 0]

        def w_up(self, after):
            fsend, frecv, gath, _ = self.p_up
            w_up_g, cw_g = _gather_finish(gath, fsend, frecv, [by_dev, by_dev], after, "gather_up_finish")
            cb = _pad_cols(conv_b.reshape(N_DEV, cu), cup).reshape(1, 2 * fp)
            cw_full = jnp.transpose(cw_g[:, :3, :], (1, 0, 2)).reshape(3, 2 * fp)
            cwb = jnp.concatenate([cw_full, cb, jnp.zeros((4, 2 * fp), F32)], axis=0)
            return w_up_g, cwb

        def forward_down(self, after):
            send, recv, gath, _ = self.g_down
            self.p_down = _gather_forward(gath, send, recv, [down_slot], after, "gather_down_forward")
            return self.p_down[3][0, 0]

        def w_down(self, after):
            fsend, frecv, gath, _ = self.p_down
            return _gather_finish(gath, fsend, frecv, [down_slot], after, "gather_down_finish")[0].reshape(1, fp, D)

        def small(self, small, loss):
            d_pre_mix, d_post_mix, d_pre_ffn, d_post_ffn, d_sb, d_dil, d_conv = small

            def rows_of(*vectors):
                n = vectors[0].shape[1]
                row = lax.broadcasted_iota(jnp.int32, (8, n), 0)
                out = jnp.zeros((8, n), F32)
                for k, vec in enumerate(vectors):
                    out = jnp.where(row == k, vec, out)
                return out

            parts = [rows_of(d_pre_mix, d_post_mix, d_pre_ffn, d_post_ffn, jnp.broadcast_to(loss[:, :1], (1, D))),
                     rows_of(d_sb, d_dil), d_conv]
            landing = [lax.dynamic_update_index_in_dim(lax.empty((N_DEV, *p.shape), F32), p, me, 0) for p in parts]
            self.small_flight = _small_start(landing, parts[0], "small_start")

        def small_sums(self, after):
            send, recv, gath, _ = self.small_flight
            gath = _small_wait(gath, send, recv, after, "small_wait")
            params = [(0, 0, pre_mix_gain, m_pre_mix_gain, v_pre_mix_gain), (0, 1, post_mix_gain, m_post_mix_gain, v_post_mix_gain),
                      (0, 2, pre_ffn_gain, m_pre_ffn_gain, v_pre_ffn_gain), (0, 3, post_ffn_gain, m_post_ffn_gain, v_post_ffn_gain),
                      (1, 0, sb_out_gain, m_sb_out_gain, v_sb_out_gain), (1, 1, dil_out_gain, m_dil_out_gain, v_dil_out_gain)]
            (gains_sum, _, conv_sum), gain_steps = _small_finish(gath, params, "small_finish")
            return gains_sum[4, 0], conv_sum, gain_steps

        def grad(self, name, dw):
            view_shape, view, block, tr, spec = grad_plan[name]
            send, recv_sems, dw, recv, token = _pair_start(dw.reshape(view_shape), view, block, core, "pair_start_" + name)
            self.in_flight[name] = (dw, recv, send, recv_sems)
            return token[0, 0]

        def grad_reduce(self, name, after):
            _, view, _, tr, spec = grad_plan[name]
            dw, recv = _pair_wait(*self.in_flight[name], view, after, "pair_wait_" + name)
            pair = _pair_add(core, dw, recv, tr, spec, "pair_add_" + name)
            send, recv_sems, pair, parts, token = _chip_start(pair, recv, "chip_start_" + name)
            self.in_flight[name] = (pair, parts, send, recv_sems)
            self.last_token = token
            return token[0, 0]

        def grad_parts(self, name, after):
            return _chip_wait(*self.in_flight[name], after, "chip_wait_" + name)

    exchanges = Exchanges()
    gains = (pre_mix_gain, post_mix_gain, pre_ffn_gain, post_ffn_gain, sb_out_gain, dil_out_gain)
    loss, grad_x, small = _local_step(xb, tb, gains, exchanges)

    def small_adam(w, g, m, v, name):
        one = w.shape[0] == 1
        if one:
            w, g, m, v = (jnp.broadcast_to(t, (8, t.shape[1])) for t in (w, g, m, v))
        outs = _adamw(w, g[None], m, v, name)
        return [o[0:1] for o in outs] if one else outs

    chip_ids = jnp.stack([2 * px + py, 2 * (1 - px) + py, 2 * px + 1 - py, 2 * (1 - px) + 1 - py]).astype(jnp.int32)
    out_w_down = _adamw_chips(w_down, *exchanges.grad_parts("w_down", exchanges.small_flight[3]), chip_ids, m_w_down[0], v_w_down[0], "adam_w_down")
    out_up_t = _adamw_chips(w_up_t, *exchanges.grad_parts("w_up", out_w_down[1]), chip_ids, m_up_t, v_up_t, "adam_w_up")
    out_w_up = [jnp.swapaxes(o, 0, 1) for o in out_up_t]
    out_w_out = _adamw_chips(w_out, *exchanges.grad_parts("w_out", out_up_t[1]), chip_ids, m_w_out[0], v_w_out[0], "adam_w_out")
    loss_out, g_conv, gain_steps = exchanges.small_sums(out_w_out[1])
    out_pre_mix, out_post_mix, out_pre_ffn, out_post_ffn, out_sb, out_dil = gain_steps
    g_conv_b = g_conv[3].reshape(N_DEV, cup)[:, :cu].reshape(1, N_DEV * cu)
    g_conv_w = lax.dynamic_index_in_dim(g_conv[0:3].reshape(3, N_DEV, cup), me, axis=1, keepdims=False)[:, :cu]
    out_conv_b = small_adam(conv_b, g_conv_b, m_conv_b, v_conv_b, "adam_conv_b")
    cw8 = [jnp.pad(t, ((0, 5), (0, 0))) for t in (conv_w, g_conv_w, m_conv_w[0], v_conv_w[0])]
    out_conv_w = [o[0:3] for o in _adamw(cw8[0], cw8[1][None], cw8[2], cw8[3], "adam_conv_w")]
    out_w_in = _adamw_chips(w_in, *exchanges.grad_parts("w_in", out_conv_w[1]), chip_ids, m_w_in[0], v_w_in[0], "adam_w_in")

    order = [out_pre_mix, out_post_mix, out_pre_ffn, out_post_ffn, [o[None] for o in out_w_in], out_sb, out_dil,
             [o[None] for o in out_w_out], [o[None] for o in out_w_up], [o[None] for o in out_conv_w], out_conv_b,
             [o[None] for o in out_w_down]]
    outs = [loss_out, grad_x[None]]
    for k in range(4):
        outs += [o[k] for o in order]
    return tuple(outs)
```

```python
import functools
import math

import jax
import jax.numpy as jnp
from jax import lax
from jax.experimental import pallas as pl
from jax.experimental.pallas import tpu as pltpu

F32 = jnp.float32
BF16 = jnp.bfloat16
HEAD_DIM = 128
LANES = 128
KEY_BLOCK = 128
DILATIONS = (1, 4, 16)
RMS_EPS = 1e-6
ROPE_THETA = 10000.0
NEG = -1e30
ADAM_LR, ADAM_B1, ADAM_B2, ADAM_EPS, ADAM_WD, ADAM_STEP = 0.001, 0.9, 0.999, 1e-08, 0.01, 10
MESH = pl.DeviceIdType.MESH
N_DEV = 8
N_CHIP = 4
HBM = pl.BlockSpec(memory_space=pl.ANY)
VMEM_LIMIT = 56 * 1024 * 1024

_pcall = pl.pallas_call


def _tile(n, pref, mult=LANES):
    best = None
    t = mult
    while t <= min(n, pref):
        if n % t == 0:
            best = t
        t += mult
    return n if best is None else best


def _params(*sem):
    return pltpu.CompilerParams(dimension_semantics=sem, vmem_limit_bytes=VMEM_LIMIT)


def _dot(a, b, dims):
    return lax.dot_general(a, b, (dims, ((), ())), preferred_element_type=F32)


NN = ((1,), (0,))
NT = ((1,), (1,))
TN = ((0,), (0,))


def _mm_body(dims, nk, tile):
    if nk == 1:
        def single(a_ref, b_ref, o_ref):
            o_ref[...] = _dot(a_ref[...].astype(BF16), b_ref[...].astype(BF16), dims).astype(o_ref.dtype)

        return single, []

    def body(a_ref, b_ref, o_ref, acc_ref):
        k = pl.program_id(2)

        @pl.when(k == 0)
        def _():
            acc_ref[...] = jnp.zeros_like(acc_ref)

        acc_ref[...] += _dot(a_ref[...].astype(BF16), b_ref[...].astype(BF16), dims)

        @pl.when(k == nk - 1)
        def _():
            o_ref[...] = acc_ref[...].astype(o_ref.dtype)

    return body, [pltpu.VMEM(tile, F32)]


def _mm_nn(a, b3, out_dtype, name, tm=1024, tn=1408, tk=2048, b_transposed=False):
    M, K = a.shape
    C, n = b3.shape[0], b3.shape[1 if b_transposed else 2]
    tm, tk, tn = _tile(M, tm, 8), _tile(K, tk), _tile(n, tn)
    npc, nk = n // tn, K // tk
    body, scratch = _mm_body(NT if b_transposed else NN, nk, (tm, tn))
    b_spec = (pl.BlockSpec((None, tn, tk), lambda i, j, k: (j // npc, j % npc, k)) if b_transposed
              else pl.BlockSpec((None, tk, tn), lambda i, j, k: (j // npc, k, j % npc)))
    return _pcall(
        body, grid=(M // tm, C * npc, nk),
        in_specs=[pl.BlockSpec((tm, tk), lambda i, j, k: (i, k)), b_spec],
        out_specs=pl.BlockSpec((tm, tn), lambda i, j, k: (i, j)),
        out_shape=jax.ShapeDtypeStruct((M, C * n), out_dtype), scratch_shapes=scratch,
        compiler_params=_params("parallel", "parallel", "arbitrary"), name=name)(a, b3)


def _mm_nt(a, b3, out_dtype, name, tm=1024, tn=1024, tk=2048, after=None, b_transposed=False, per_step=1):
    M, _ = a.shape
    C, N, n = (b3.shape[0], b3.shape[2], b3.shape[1]) if b_transposed else b3.shape
    tm, tn, tk = _tile(M, tm, 8), _tile(N, tn), _tile(n, tk)
    dims = NN if b_transposed else NT
    extra = [] if after is None else [after]
    if per_step > 1 and tk == n and C % per_step == 0:
        nk, scratch = C // per_step, [pltpu.VMEM((tm, tn), F32)]
        b3 = b3.reshape(nk, per_step, *b3.shape[1:])
        a_spec = pl.BlockSpec((tm, per_step * n), lambda i, j, k: (i, k))
        if b_transposed:
            b_spec = pl.BlockSpec((None, per_step, n, tn), lambda i, j, k: (k, 0, 0, j))
        else:
            b_spec = pl.BlockSpec((None, per_step, tn, n), lambda i, j, k: (k, 0, j, 0))

        def body(a_ref, b_ref, *rest):
            o_ref, acc_ref = rest[len(extra):]
            k = pl.program_id(2)

            @pl.when(k == 0)
            def _():
                acc_ref[...] = jnp.zeros_like(acc_ref)

            b = b_ref[...].astype(BF16)
            b = b.reshape(per_step * n, tn) if b_transposed else jnp.concatenate([b[u] for u in range(per_step)], axis=1)
            acc_ref[...] += _dot(a_ref[...].astype(BF16), b, dims)

            @pl.when(k == nk - 1)
            def _():
                o_ref[...] = acc_ref[...].astype(o_ref.dtype)
    else:
        kpc = n // tk
        nk = C * kpc
        inner, scratch = _mm_body(dims, nk, (tm, tn))
        a_spec = pl.BlockSpec((tm, tk), lambda i, j, k: (i, k))
        b_spec = (pl.BlockSpec((None, tk, tn), lambda i, j, k: (k // kpc, k % kpc, j)) if b_transposed
                  else pl.BlockSpec((None, tn, tk), lambda i, j, k: (k // kpc, j, k % kpc)))

        def body(a_ref, b_ref, *rest):
            inner(a_ref, b_ref, *rest[len(extra):])

    return _pcall(
        body, grid=(M // tm, N // tn, nk), in_specs=[a_spec, b_spec] + [HBM] * len(extra),
        out_specs=pl.BlockSpec((tm, tn), lambda i, j, k: (i, j)),
        out_shape=jax.ShapeDtypeStruct((M, N), out_dtype), scratch_shapes=scratch,
        compiler_params=_params("parallel", "parallel", "arbitrary"), name=name)(a, b3, *extra)


def _mm_tn(x, y, n, out_dtype, name, tm=1024, tn=1408, tk=2048, after=None):
    S, P = x.shape
    C = y.shape[1] // n
    tm, tn, tk = _tile(P, tm), _tile(n, tn), _tile(S, tk, 8)
    npc, nk = n // tn, S // tk
    inner, scratch = _mm_body(TN, nk, (tm, tn))
    extra = [] if after is None else [after]

    def body(x_ref, y_ref, *rest):
        inner(x_ref, y_ref, *rest[len(extra):])

    return _pcall(
        body, grid=(P // tm, C * npc, nk),
        in_specs=[pl.BlockSpec((tk, tm), lambda i, j, k: (k, i)),
                  pl.BlockSpec((tk, tn), lambda i, j, k: (k, j))] + [HBM] * len(extra),
        out_specs=pl.BlockSpec((None, tm, tn), lambda i, j, k: (j // npc, i, j % npc)),
        out_shape=jax.ShapeDtypeStruct((C, P, n), out_dtype), scratch_shapes=scratch,
        compiler_params=_params("parallel", "parallel", "arbitrary"), name=name)(x, y, *extra)


def _rms_scale(v):
    return lax.rsqrt(jnp.mean(v * v, axis=-1, keepdims=True) + RMS_EPS)


def _rms_bwd(gy, v, r):
    return r * gy - v * (r * r * r * jnp.mean(gy * v, axis=-1, keepdims=True))


def _rows_spec(tm, d):
    return pl.BlockSpec((tm, d), lambda i: (i, 0))


def _vec_spec(d):
    return pl.BlockSpec((1, d), lambda i: (0, 0))


def _rms_fwd(x, g, name, tm=256):
    S, D = x.shape

    def body(x_ref, g_ref, h_ref):
        v = x_ref[...]
        h_ref[...] = (v * _rms_scale(v) * g_ref[...]).astype(BF16)

    return _pcall(body, grid=(S // tm,), in_specs=[_rows_spec(tm, D), _vec_spec(D)], out_specs=_rows_spec(tm, D),
                  out_shape=jax.ShapeDtypeStruct((S, D), BF16), compiler_params=_params("parallel"), name=name)(x, g)


def _mid_fwd(x, mix, g_post, g_pre, name, tm=256):
    S, D = x.shape

    def body(x_ref, m_ref, gp_ref, gn_ref, x2_ref, h_ref):
        m = m_ref[...]
        x2 = x_ref[...] + m * _rms_scale(m) * gp_ref[...]
        x2_ref[...] = x2
        h_ref[...] = (x2 * _rms_scale(x2) * gn_ref[...]).astype(BF16)

    return _pcall(body, grid=(S // tm,), in_specs=[_rows_spec(tm, D), _rows_spec(tm, D), _vec_spec(D), _vec_spec(D)],
                  out_specs=[_rows_spec(tm, D), _rows_spec(tm, D)],
                  out_shape=[jax.ShapeDtypeStruct((S, D), F32), jax.ShapeDtypeStruct((S, D), BF16)],
                  compiler_params=_params("parallel"), name=name)(x, mix, g_post, g_pre)


def _loss_bwd(x2, f, tgt, g_post, name, tm=256):
    S, D = x2.shape

    def body(x2_ref, f_ref, t_ref, g_ref, dy_ref, df_ref, dg_ref, ls_ref):
        i = pl.program_id(0)

        @pl.when(i == 0)
        def _():
            dg_ref[...] = jnp.zeros_like(dg_ref)
            ls_ref[...] = jnp.zeros_like(ls_ref)

        fv = f_ref[...]
        r = _rms_scale(fv)
        g = g_ref[...]
        err = x2_ref[...] + fv * r * g - t_ref[...]
        ls_ref[...] += jnp.broadcast_to(0.5 * jnp.sum(jnp.mean(err * err, axis=-1, keepdims=True), axis=0, keepdims=True), ls_ref.shape)
        dy = err * (1.0 / D)
        dy_ref[...] = dy
        df_ref[...] = _rms_bwd(dy * g, fv, r).astype(BF16)
        dg_ref[...] += jnp.sum(dy * fv * r, axis=0, keepdims=True)

    return _pcall(body, grid=(S // tm,),
                  in_specs=[_rows_spec(tm, D), _rows_spec(tm, D), _rows_spec(tm, D), _vec_spec(D)],
                  out_specs=[_rows_spec(tm, D), _rows_spec(tm, D), _vec_spec(D), _vec_spec(LANES)],
                  out_shape=[jax.ShapeDtypeStruct((S, D), F32), jax.ShapeDtypeStruct((S, D), BF16),
                             jax.ShapeDtypeStruct((1, D), F32), jax.ShapeDtypeStruct((1, LANES), F32)],
                  compiler_params=_params("arbitrary"), name=name)(x2, f, tgt, g_post)


def _mid_bwd(dy, dh2, x2, mix, g_pre, g_post, name, tm=256):
    S, D = dy.shape

    def body(dy_ref, dh_ref, x2_ref, m_ref, gn_ref, gp_ref, dx2_ref, dm_ref, dgn_ref, dgp_ref):
        i = pl.program_id(0)

        @pl.when(i == 0)
        def _():
            dgn_ref[...] = jnp.zeros_like(dgn_ref)
            dgp_ref[...] = jnp.zeros_like(dgp_ref)

        x2, dh = x2_ref[...], dh_ref[...].astype(F32)
        r = _rms_scale(x2)
        dx2 = dy_ref[...] + _rms_bwd(dh * gn_ref[...], x2, r)
        dgn_ref[...] += jnp.sum(dh * x2 * r, axis=0, keepdims=True)
        dx2_ref[...] = dx2
        m = m_ref[...]
        rm = _rms_scale(m)
        dm_ref[...] = _rms_bwd(dx2 * gp_ref[...], m, rm).astype(BF16)
        dgp_ref[...] += jnp.sum(dx2 * m * rm, axis=0, keepdims=True)

    return _pcall(body, grid=(S // tm,),
                  in_specs=[_rows_spec(tm, D)] * 4 + [_vec_spec(D)] * 2,
                  out_specs=[_rows_spec(tm, D), _rows_spec(tm, D), _vec_spec(D), _vec_spec(D)],
                  out_shape=[jax.ShapeDtypeStruct((S, D), F32), jax.ShapeDtypeStruct((S, D), BF16),
                             jax.ShapeDtypeStruct((1, D), F32), jax.ShapeDtypeStruct((1, D), F32)],
                  compiler_params=_params("arbitrary"), name=name)(dy, dh2, x2, mix, g_pre, g_post)


def _first_bwd(dx2, dh1, x, g_pre, name, tm=256):
    S, D = x.shape

    def body(dx2_ref, dh_ref, x_ref, g_ref, gx_ref, dg_ref):
        i = pl.program_id(0)

        @pl.when(i == 0)
        def _():
            dg_ref[...] = jnp.zeros_like(dg_ref)

        xv, dh = x_ref[...], dh_ref[...].astype(F32)
        r = _rms_scale(xv)
        gx_ref[...] = dx2_ref[...] + _rms_bwd(dh * g_ref[...], xv, r)
        dg_ref[...] += jnp.sum(dh * xv * r, axis=0, keepdims=True)

    return _pcall(body, grid=(S // tm,), in_specs=[_rows_spec(tm, D)] * 3 + [_vec_spec(D)],
                  out_specs=[_rows_spec(tm, D), _vec_spec(D)],
                  out_shape=[jax.ShapeDtypeStruct((S, D), F32), jax.ShapeDtypeStruct((1, D), F32)],
                  compiler_params=_params("arbitrary"), name=name)(dx2, dh1, x, g_pre)


def _logsig_pair(z):
    lb = jnp.minimum(z, 0.0) - jnp.log(1.0 + jnp.exp(-jnp.abs(z)))
    return lb, lb - z


SB_KEY_BLOCK = 256


def _sum_matrix(strict):
    ia = lax.broadcasted_iota(jnp.int32, (SB_KEY_BLOCK, SB_KEY_BLOCK), 0)
    ib = lax.broadcasted_iota(jnp.int32, (SB_KEY_BLOCK, SB_KEY_BLOCK), 1)
    return ((ia > ib) if strict == ">" else (ia < ib)).astype(BF16)


def _row_total(sums, v, col):
    return jnp.broadcast_to(sums[:, col:col + 1] + v[:, col:col + 1], (v.shape[0], LANES))


def _lanes(c, width):
    return jnp.tile(c, (1, width // LANES))


def _split_dot(v, u):
    hi = v.astype(BF16)
    lo = (v - hi.astype(F32)).astype(BF16)
    return _dot(hi, u, NN) + _dot(lo, u, NN)


def _head_out(o, g):
    return o * _rms_scale(o) * g


def _sb_fwd(proj, gain, n_heads, mixed_heads, name, tq=1024):
    S = proj.shape[0]
    H, tk = n_heads, SB_KEY_BLOCK
    tq = _tile(S, tq, 2 * tk)
    scale = HEAD_DIM ** -0.5

    def body(q_ref, k_ref, v_ref, g_ref, o_ref, ct_ref, mx_ref, oacc, cacc):
        i = pl.program_id(1)
        oacc[...] = jnp.zeros_like(oacc)
        cacc[...] = jnp.zeros_like(cacc)
        sums = _sum_matrix(">")

        def run(blocks):
            scored = []
            for k0, r0, diagonal in blocks:
                rows = pl.ds(r0, tq - r0)
                lb, lk = _logsig_pair(_dot(q_ref[rows, :].astype(BF16), k_ref[pl.ds(k0, tk), :].astype(BF16), NT) * scale)
                causal = None
                if diagonal:
                    causal = (lax.broadcasted_iota(jnp.int32, (tq - r0, tk), 1)
                              < lax.broadcasted_iota(jnp.int32, (tq - r0, tk), 0))
                    lk = jnp.where(causal, lk, 0.0)
                scored.append((k0, rows, causal, lb, lk))
            summed = [(k0, rows, causal, lb, lk, _split_dot(lk, sums)) for k0, rows, causal, lb, lk in scored]
            weights = []
            for k0, rows, causal, lb, lk, after in summed:
                c = cacc[rows, :]
                a = jnp.exp(lb + after + _lanes(c, tk))
                if causal is not None:
                    a = jnp.where(causal, a, 0.0)
                cacc[rows, :] = c + _row_total(after, lk, 0)
                weights.append((k0, rows, a.astype(BF16)))
            for k0, rows, a in weights:
                oacc[rows, :] += _dot(a, v_ref[pl.ds(k0, tk), :].astype(BF16), NN)

        for d in reversed(range(0, tq // tk, 2)):
            run([(pl.multiple_of(i * tq + e * tk, tk), e * tk, True) for e in (d + 1, d)])
        per_trip = tq // tk

        def step(it, carry):
            k0 = pl.multiple_of((i - 1 - it) * tq, tq)
            run([(pl.multiple_of(k0 + e * tk, tk), 0, False) for e in reversed(range(per_trip))])
            return carry

        lax.fori_loop(0, i, step, 0)
        o = oacc[...]
        o_ref[...] = o
        ct_ref[...] = cacc[...]
        mx_ref[...] = _head_out(o, g_ref[...]).astype(BF16)

    blk = pl.BlockSpec((tq, HEAD_DIM), lambda h, i: (i, h))
    return _pcall(
        body, grid=(H, S // tq),
        in_specs=[blk, pl.BlockSpec((S, HEAD_DIM), lambda h, i: (0, H + h)),
                  pl.BlockSpec((S, HEAD_DIM), lambda h, i: (0, 2 * H + h)), pl.BlockSpec((1, HEAD_DIM), lambda h, i: (0, h))],
        out_specs=[blk, blk, blk],
        out_shape=[jax.ShapeDtypeStruct((S, H * HEAD_DIM), F32), jax.ShapeDtypeStruct((S, H * HEAD_DIM), F32),
                   jax.ShapeDtypeStruct((S, mixed_heads * HEAD_DIM), BF16)],
        scratch_shapes=[pltpu.VMEM((tq, HEAD_DIM), F32), pltpu.VMEM((tq, LANES), F32)],
        compiler_params=_params("parallel", "arbitrary"), name=name)(proj, proj, proj, gain)


def _sb_bwd(proj, gain, o_raw, ctot, dmixed, dm_col0, n_heads, name, tq=1024):
    S = proj.shape[0]
    H, tk = n_heads, SB_KEY_BLOCK
    tq = _tile(S, tq, 2 * tk)
    nq = S // tq
    scale = HEAD_DIM ** -0.5

    def body(q_ref, k_ref, v_ref, g_ref, o_ref, ct_ref, dm_ref, dq_ref, dk_ref, dv_ref, dg_ref,
             dkacc, dvacc, dqacc, pfx, gcar, dos):
        i = pl.program_id(1)

        @pl.when(i == 0)
        def _():
            dkacc[...] = jnp.zeros_like(dkacc)
            dvacc[...] = jnp.zeros_like(dvacc)
            dg_ref[...] = jnp.zeros_like(dg_ref)

        o, dm, g = o_ref[...], dm_ref[...].astype(F32), g_ref[...]
        r = _rms_scale(o)
        dos[...] = _rms_bwd(dm * g, o, r).astype(BF16)
        dg_ref[...] += jnp.broadcast_to(jnp.sum(dm * o * r, axis=0, keepdims=True), dg_ref.shape)
        dqacc[...] = jnp.zeros_like(dqacc)
        pfx[...] = jnp.zeros_like(pfx)
        gcar[...] = jnp.zeros_like(gcar)
        later, earlier = _sum_matrix(">"), _sum_matrix("<")

        def run(blocks):
            scored = []
            for k0, r0, diagonal in blocks:
                rows, keys = pl.ds(r0, tq - r0), pl.ds(k0, tk)
                lb, lk = _logsig_pair(_dot(q_ref[rows, :].astype(BF16), k_ref[keys, :].astype(BF16), NT) * scale)
                da = _dot(dos[rows, :], v_ref[keys, :].astype(BF16), NT)
                causal = None
                if diagonal:
                    causal = (lax.broadcasted_iota(jnp.int32, (tq - r0, tk), 1)
                              < lax.broadcasted_iota(jnp.int32, (tq - r0, tk), 0))
                    lk = jnp.where(causal, lk, 0.0)
                scored.append((rows, keys, causal, lb, lk, da))
            summed = [(*blk, _split_dot(blk[4], later)) for blk in scored]
            weighted = []
            for rows, keys, causal, lb, lk, da, after in summed:
                p = pfx[rows, :] + _row_total(after, lk, 0)
                pfx[rows, :] = p
                a = jnp.exp(lb + after + _lanes(ct_ref[rows, :] - p, tk))
                if causal is not None:
                    a = jnp.where(causal, a, 0.0)
                dl = da * a
                weighted.append((rows, keys, causal, lb, a.astype(BF16), dl, _dot(dl.astype(BF16), earlier, NN)))
            cotangents = []
            for rows, keys, causal, lb, a, dl, before in weighted:
                gc = gcar[rows, :]
                gcar[rows, :] = gc + _row_total(before, dl, tk - 1)
                sig = jnp.exp(lb)
                gsum = (before + _lanes(gc, tk)) * sig
                if causal is not None:
                    gsum = jnp.where(causal, gsum, 0.0)
                cotangents.append((rows, keys, a, ((dl * (1.0 - sig) - gsum) * scale).astype(BF16)))
            for rows, keys, a, dz in cotangents:
                q, do = q_ref[rows, :].astype(BF16), dos[rows, :]
                dvacc[keys, :] += _dot(a, do, TN)
                dqacc[rows, :] += _dot(dz, k_ref[keys, :].astype(BF16), NN)
                dkacc[keys, :] += _dot(dz, q, TN)

        def step(j, carry):
            k0 = pl.multiple_of(j * 2 * tk, 2 * tk)
            run([(k0, 0, False), (pl.multiple_of(k0 + tk, tk), 0, False)])
            return carry

        lax.fori_loop(0, i * (tq // tk // 2), step, 0)
        for d in range(0, tq // tk, 2):
            run([(pl.multiple_of(i * tq + e * tk, tk), e * tk, True) for e in (d, d + 1)])
        dq_ref[...] = dqacc[...].astype(BF16)

        @pl.when(i == nq - 1)
        def _():
            dk_ref[...] = dkacc[...].astype(BF16)
            dv_ref[...] = dvacc[...].astype(BF16)

    blk = pl.BlockSpec((tq, HEAD_DIM), lambda h, i: (i, h))
    full = pl.BlockSpec((S, HEAD_DIM), lambda h, i: (0, h))
    W = H * HEAD_DIM
    return _pcall(
        body, grid=(H, nq),
        in_specs=[blk, pl.BlockSpec((S, HEAD_DIM), lambda h, i: (0, H + h)),
                  pl.BlockSpec((S, HEAD_DIM), lambda h, i: (0, 2 * H + h)), pl.BlockSpec((1, HEAD_DIM), lambda h, i: (0, h)),
                  blk, blk, pl.BlockSpec((tq, HEAD_DIM), lambda h, i: (i, dm_col0 + h))],
        out_specs=[blk, full, full, pl.BlockSpec((8, HEAD_DIM), lambda h, i: (0, h))],
        out_shape=[jax.ShapeDtypeStruct((S, W), BF16), jax.ShapeDtypeStruct((S, W), BF16),
                   jax.ShapeDtypeStruct((S, W), BF16), jax.ShapeDtypeStruct((8, W), F32)],
        scratch_shapes=[pltpu.VMEM((S, HEAD_DIM), F32), pltpu.VMEM((S, HEAD_DIM), F32), pltpu.VMEM((tq, HEAD_DIM), F32),
                        pltpu.VMEM((tq, LANES), F32), pltpu.VMEM((tq, LANES), F32), pltpu.VMEM((tq, HEAD_DIM), BF16)],
        compiler_params=_params("arbitrary", "arbitrary"), name=name)(proj, proj, proj, gain, o_raw, ctot, dmixed)


def _rope_tables(S):
    inv_freq = ROPE_THETA ** (-jnp.arange(0, HEAD_DIM, 2, dtype=F32) / HEAD_DIM)
    ang = jnp.arange(S, dtype=F32)[:, None] * inv_freq[None, :]
    cos, sin = jnp.cos(ang), jnp.sin(ang)
    return jnp.concatenate([cos, cos], axis=1), jnp.concatenate([-sin, sin], axis=1)


def _rope(v, cos2, sin_signed):
    return v * cos2 + pltpu.roll(v, HEAD_DIM // 2, axis=1) * sin_signed


def _dil_rows(d, r, l0, n):
    if d == 1:
        return pl.ds(l0 if isinstance(l0, int) else pl.multiple_of(l0, KEY_BLOCK), n)
    return pl.ds(r + d * l0, n, stride=d)


def _dil_blocks(S, visit):
    B = KEY_BLOCK
    group = 16
    for b, d in enumerate(DILATIONS):
        nb = S // d // B
        if nb == 1:
            g = math.gcd(d, group)

            def trip(t, carry, b=b, d=d, g=g):
                visit([(b, d, t * g + u, 0, True) for u in range(g)])
                return carry

            lax.fori_loop(0, d // g, trip, 0)
        elif d == 1:
            visit([(b, d, 0, 0, True)])
            g = max(k for k in range(1, group + 2) if (nb - 1) % k == 0)

            def trip(t, carry, b=b, d=d, g=g):
                visit([(b, d, 0, (1 + t * g + u) * B, False) for u in range(g)])
                return carry

            lax.fori_loop(0, (nb - 1) // g, trip, 0)
        else:
            g = math.gcd(d, max(group // nb, 1))

            def trip(t, carry, b=b, d=d, nb=nb, g=g):
                visit([(b, d, t * g + u, n * B, n == 0) for u in range(g) for n in range(nb)])
                return carry

            lax.fori_loop(0, d // g, trip, 0)


def _dil_mask(first):
    B = KEY_BLOCK
    nk = B if first else 2 * B
    iq = lax.broadcasted_iota(jnp.int32, (B, nk), 0)
    ik = lax.broadcasted_iota(jnp.int32, (B, nk), 1)
    return (ik <= iq) if first else ((ik >= iq) & (ik <= iq + B))


def _dil_fwd(proj, cos2, sin_signed, gain, mixed, col0, n_heads, name):
    S = proj.shape[0]
    H, B = n_heads, KEY_BLOCK
    scale = HEAD_DIM ** -0.5
    rc = _tile(S, 256, 8)

    def body(q_ref, k_ref, v_ref, c_ref, s_ref, g_ref, mixed_in, o_ref, l_ref, mx_ref, qr, kr, vf, *per_branch):
        ob, lb = per_branch[:len(DILATIONS)], per_branch[len(DILATIONS):]

        def rope_rows(t, carry):
            rows = pl.ds(pl.multiple_of(t * rc, rc), rc)
            qr[rows, :] = _rope(q_ref[rows, :].astype(F32), c_ref[rows, :], s_ref[rows, :])
            kr[rows, :] = _rope(k_ref[rows, :].astype(F32), c_ref[rows, :], s_ref[rows, :])
            vf[rows, :] = v_ref[rows, :].astype(F32)
            return carry

        lax.fori_loop(0, S // rc, rope_rows, 0)

        def visit(blocks):
            scores = []
            for b, d, r, l0, first in blocks:
                qrows = _dil_rows(d, r, l0, B)
                krows = qrows if first else _dil_rows(d, r, l0 - B, 2 * B)
                s = _dot(qr[qrows, :].astype(BF16), kr[krows, :].astype(BF16), NT) * scale
                scores.append((b, qrows, krows, jnp.where(_dil_mask(first), s, NEG)))
            weights = []
            for b, qrows, krows, s in scores:
                m = jnp.max(s, axis=1, keepdims=True)
                p = jnp.exp(s - m)
                den = jnp.sum(p, axis=1, keepdims=True)
                lb[b][qrows, :] = jnp.broadcast_to(m + jnp.log(den), (B, LANES))
                weights.append((b, qrows, krows, p.astype(BF16), den))
            for b, qrows, krows, p, den in weights:
                ob[b][qrows, :] = _dot(p, vf[krows, :].astype(BF16), NN) / den

        _dil_blocks(S, visit)

        def combine(t, carry):
            rows = pl.ds(pl.multiple_of(t * rc, rc), rc)
            l0, l1, l2 = lb[0][rows, :], lb[1][rows, :], lb[2][rows, :]
            m = jnp.maximum(jnp.maximum(l0, l1), l2)
            w0, w1, w2 = jnp.exp(l0 - m), jnp.exp(l1 - m), jnp.exp(l2 - m)
            den = w0 + w1 + w2
            o = (w0 * ob[0][rows, :] + w1 * ob[1][rows, :] + w2 * ob[2][rows, :]) / den
            o_ref[rows, :] = o
            l_ref[rows, :] = m + jnp.log(den)
            mx_ref[rows, :] = _head_out(o, g_ref[...]).astype(BF16)
            return carry

        lax.fori_loop(0, S // rc, combine, 0)

    def col(k):
        return pl.BlockSpec((S, HEAD_DIM), lambda h: (0, col0 + k * H + h))

    tab = pl.BlockSpec((S, HEAD_DIM), lambda h: (0, 0))
    out = pl.BlockSpec((S, HEAD_DIM), lambda h: (0, h))
    W = H * HEAD_DIM
    first = mixed.shape[1] // HEAD_DIM - H
    return _pcall(
        body, grid=(H,),
        in_specs=[col(0), col(1), col(2), tab, tab, pl.BlockSpec((1, HEAD_DIM), lambda h: (0, h)), HBM],
        out_specs=[out, out, pl.BlockSpec((S, HEAD_DIM), lambda h: (0, first + h))],
        out_shape=[jax.ShapeDtypeStruct((S, W), F32), jax.ShapeDtypeStruct((S, W), F32),
                   jax.ShapeDtypeStruct(mixed.shape, BF16)],
        input_output_aliases={6: 2},
        scratch_shapes=[pltpu.VMEM((S, HEAD_DIM), F32)] * (3 + 2 * len(DILATIONS)),
        compiler_params=_params("parallel"), name=name)(proj, proj, proj, cos2, sin_signed, gain, mixed)


def _dil_bwd(proj, cos2, sin_signed, gain, o_raw, lse, dmixed, dm_col0, col0, n_heads, name):
    S = proj.shape[0]
    H, B = n_heads, KEY_BLOCK
    scale = HEAD_DIM ** -0.5
    rc = _tile(S, 256, 8)

    def body(q_ref, k_ref, v_ref, c_ref, s_ref, g_ref, o_ref, l_ref, dm_ref, dq_ref, dk_ref, dv_ref, dg_ref,
             qr, kr, vf, dos, dsum, dqr, dkr, dvv):
        dg_ref[...] = jnp.zeros_like(dg_ref)

        def prep(t, carry):
            rows = pl.ds(pl.multiple_of(t * rc, rc), rc)
            qr[rows, :] = _rope(q_ref[rows, :].astype(F32), c_ref[rows, :], s_ref[rows, :])
            kr[rows, :] = _rope(k_ref[rows, :].astype(F32), c_ref[rows, :], s_ref[rows, :])
            vf[rows, :] = v_ref[rows, :].astype(F32)
            o, dm = o_ref[rows, :], dm_ref[rows, :].astype(F32)
            r = _rms_scale(o)
            do = _rms_bwd(dm * g_ref[...], o, r)
            dg_ref[...] += jnp.broadcast_to(jnp.sum(dm * o * r, axis=0, keepdims=True), dg_ref.shape)
            dos[rows, :] = do
            dsum[rows, :] = jnp.broadcast_to(jnp.sum(do * o, axis=1, keepdims=True), (rc, LANES))
            dqr[rows, :] = jnp.zeros((rc, HEAD_DIM), F32)
            dkr[rows, :] = jnp.zeros((rc, HEAD_DIM), F32)
            dvv[rows, :] = jnp.zeros((rc, HEAD_DIM), F32)
            return carry

        lax.fori_loop(0, S // rc, prep, 0)

        def visit(blocks):
            products = []
            for b, d, r, l0, first in blocks:
                qrows = _dil_rows(d, r, l0, B)
                krows = qrows if first else _dil_rows(d, r, l0 - B, 2 * B)
                qs, ks = qr[qrows, :].astype(BF16), kr[krows, :].astype(BF16)
                do = dos[qrows, :].astype(BF16)
                s = jnp.where(_dil_mask(first), _dot(qs, ks, NT) * scale, NEG)
                dp = _dot(do, vf[krows, :].astype(BF16), NT)
                products.append((qrows, krows, qs, ks, do, s, dp))
            cotangents = []
            for qrows, krows, qs, ks, do, s, dp in products:
                p = jnp.exp(s - l_ref[qrows, :][:, 0:1])
                ds = (p * (dp - dsum[qrows, :][:, 0:1]) * scale).astype(BF16)
                cotangents.append((qrows, krows, qs, ks, do, p.astype(BF16), ds))
            for qrows, krows, qs, ks, do, p, ds in cotangents:
                dqr[qrows, :] += _dot(ds, ks, NN)
                dkr[krows, :] += _dot(ds, qs, TN)
                dvv[krows, :] += _dot(p, do, TN)

        _dil_blocks(S, visit)

        def finish(t, carry):
            rows = pl.ds(pl.multiple_of(t * rc, rc), rc)
            c, s = c_ref[rows, :], s_ref[rows, :]
            dq, dk = dqr[rows, :], dkr[rows, :]
            dq_ref[rows, :] = (dq * c + pltpu.roll(dq * s, HEAD_DIM // 2, axis=1)).astype(BF16)
            dk_ref[rows, :] = (dk * c + pltpu.roll(dk * s, HEAD_DIM // 2, axis=1)).astype(BF16)
            dv_ref[rows, :] = dvv[rows, :].astype(BF16)
            return carry

        lax.fori_loop(0, S // rc, finish, 0)

    def col(k):
        return pl.BlockSpec((S, HEAD_DIM), lambda h: (0, col0 + k * H + h))

    tab = pl.BlockSpec((S, HEAD_DIM), lambda h: (0, 0))
    out = pl.BlockSpec((S, HEAD_DIM), lambda h: (0, h))
    W = H * HEAD_DIM
    big = pltpu.VMEM((S, HEAD_DIM), F32)
    return _pcall(
        body, grid=(H,),
        in_specs=[col(0), col(1), col(2), tab, tab, pl.BlockSpec((1, HEAD_DIM), lambda h: (0, h)), out, out,
                  pl.BlockSpec((S, HEAD_DIM), lambda h: (0, dm_col0 + h))],
        out_specs=[out, out, out, pl.BlockSpec((8, HEAD_DIM), lambda h: (0, h))],
        out_shape=[jax.ShapeDtypeStruct((S, W), BF16), jax.ShapeDtypeStruct((S, W), BF16),
                   jax.ShapeDtypeStruct((S, W), BF16), jax.ShapeDtypeStruct((8, W), F32)],
        scratch_shapes=[big, big, big, big, pltpu.VMEM((S, LANES), F32), big, big, big],
        compiler_params=_params("parallel"), name=name)(proj, proj, proj, cos2, sin_signed, gain, o_raw, lse, dmixed)


GELU_C = math.sqrt(2.0 / math.pi)
GELU_A = 0.044715
HALO = 16


def _shift_down(cur, halo, k):
    out = pltpu.roll(cur, k, axis=0)
    row = lax.broadcasted_iota(jnp.int32, cur.shape, 0)
    for t in range(k):
        out = jnp.where(row == t, halo[HALO - k + t:HALO - k + t + 1, :], out)
    return out


def _shift_up(cur, halo, k):
    n = cur.shape[0]
    out = pltpu.roll(cur, n - k, axis=0)
    row = lax.broadcasted_iota(jnp.int32, cur.shape, 0)
    for t in range(k):
        out = jnp.where(row == n - k + t, halo[t:t + 1, :], out)
    return out


def _conv3(cur, halo, cw):
    return _shift_down(cur, halo, 2) * cw[0:1, :] + _shift_down(cur, halo, 1) * cw[1:2, :] + cur * cw[2:3, :] + cw[3:4, :]


def _gelu_parts(x):
    t = jnp.tanh(GELU_C * (x + GELU_A * x * x * x))
    return 0.5 * x * (1.0 + t), t


def _geglu_specs(tm, tn, ncb):
    hb = tm // HALO

    def cur(off):
        return pl.BlockSpec((tm, tn), lambda j, i: (i, off + j))

    def prev(off):
        return pl.BlockSpec((HALO, tn), lambda j, i: (jnp.maximum(i * hb - 1, 0), off + j))

    def taps(off):
        return pl.BlockSpec((8, tn), lambda j, i: (0, off + j))

    return [cur(0), prev(0), cur(ncb), prev(ncb), taps(0), taps(ncb)]


def _geglu_fwd(u, cwb, name, tm=256, tn=1408):
    S, F2 = u.shape
    F = F2 // 2
    tm, tn = _tile(S, tm, HALO), _tile(F, tn)
    ncb = F // tn

    def body(g_ref, gp_ref, v_ref, vp_ref, cg_ref, cv_ref, y_ref):
        top = pl.program_id(1) > 0
        gp = jnp.where(top, gp_ref[...].astype(F32), 0.0)
        vp = jnp.where(top, vp_ref[...].astype(F32), 0.0)
        gc = _conv3(g_ref[...].astype(F32), gp, cg_ref[...])
        vc = _conv3(v_ref[...].astype(F32), vp, cv_ref[...])
        y_ref[...] = (_gelu_parts(gc)[0] * vc).astype(BF16)

    return _pcall(body, grid=(ncb, S // tm), in_specs=_geglu_specs(tm, tn, ncb),
                  out_specs=pl.BlockSpec((tm, tn), lambda j, i: (i, j)),
                  out_shape=jax.ShapeDtypeStruct((S, F), BF16),
                  compiler_params=_params("parallel", "parallel"), name=name)(u, u, u, u, cwb, cwb)


def _geglu_bwd(u, dy, cwb, name, tm=256, tn=512):
    S, F2 = u.shape
    F = F2 // 2
    tm, tn = _tile(S, tm, HALO), _tile(F, tn)
    ncb = F // tn

    def body(g_ref, gp_ref, v_ref, vp_ref, cg_ref, cv_ref, dy_ref, dc_ref, dwg_ref, dwv_ref):
        i = pl.program_id(1)

        @pl.when(i == 0)
        def _():
            dwg_ref[...] = jnp.zeros_like(dwg_ref)
            dwv_ref[...] = jnp.zeros_like(dwv_ref)

        top = i > 0
        g, v = g_ref[...].astype(F32), v_ref[...].astype(F32)
        gp = jnp.where(top, gp_ref[...].astype(F32), 0.0)
        vp = jnp.where(top, vp_ref[...].astype(F32), 0.0)
        gc = _conv3(g, gp, cg_ref[...])
        vc = _conv3(v, vp, cv_ref[...])
        act, t = _gelu_parts(gc)
        dact = 0.5 * (1.0 + t) + 0.5 * gc * (1.0 - t * t) * GELU_C * (1.0 + 3.0 * GELU_A * gc * gc)
        dyv = dy_ref[...].astype(F32)
        dgc = dyv * vc * dact
        dvc = dyv * act
        dc_ref[0] = dgc.astype(BF16)
        dc_ref[1] = dvc.astype(BF16)

        def taps(out_ref, dc, cur, halo):
            out_ref[0:1, :] += jnp.sum(dc * _shift_down(cur, halo, 2), axis=0, keepdims=True)
            out_ref[1:2, :] += jnp.sum(dc * _shift_down(cur, halo, 1), axis=0, keepdims=True)
            out_ref[2:3, :] += jnp.sum(dc * cur, axis=0, keepdims=True)
            out_ref[3:4, :] += jnp.sum(dc, axis=0, keepdims=True)

        taps(dwg_ref, dgc, g, gp)
        taps(dwv_ref, dvc, v, vp)

    return _pcall(body, grid=(ncb, S // tm),
                  in_specs=_geglu_specs(tm, tn, ncb) + [pl.BlockSpec((tm, tn), lambda j, i: (i, j))],
                  out_specs=[pl.BlockSpec((2, tm, tn), lambda j, i: (0, i, j)),
                             pl.BlockSpec((8, tn), lambda j, i: (0, j)), pl.BlockSpec((8, tn), lambda j, i: (0, j))],
                  out_shape=[jax.ShapeDtypeStruct((2, S, F), BF16), jax.ShapeDtypeStruct((8, F), F32),
                             jax.ShapeDtypeStruct((8, F), F32)],
                  compiler_params=_params("parallel", "arbitrary"), name=name)(u, u, u, u, cwb, cwb, dy)


def _conv_bwd(dc, cwb, name, tm=512, tn=1408):
    _, S, F = dc.shape
    tm, tn = _tile(S, tm, HALO), _tile(F, tn)
    ncb, nrb = F // tn, S // tm
    hb = tm // HALO

    def body(c_ref, n_ref, w_ref, du_ref):
        cur = c_ref[...].astype(F32)
        nxt = jnp.where(pl.program_id(2) < nrb - 1, n_ref[...].astype(F32), 0.0)
        w = w_ref[...]
        du = cur * w[2:3, :] + _shift_up(cur, nxt, 1) * w[1:2, :] + _shift_up(cur, nxt, 2) * w[0:1, :]
        du_ref[...] = du.astype(BF16)

    return _pcall(body, grid=(2, ncb, nrb),
                  in_specs=[pl.BlockSpec((None, tm, tn), lambda c, j, i: (c, i, j)),
                            pl.BlockSpec((None, HALO, tn), lambda c, j, i: (c, jnp.minimum((i + 1) * hb, S // HALO - 1), j)),
                            pl.BlockSpec((8, tn), lambda c, j, i: (0, c * ncb + j))],
                  out_specs=pl.BlockSpec((tm, tn), lambda c, j, i: (i, c * ncb + j)),
                  out_shape=jax.ShapeDtypeStruct((S, 2 * F), BF16),
                  compiler_params=_params("parallel", "parallel", "parallel"), name=name)(dc, dc, cwb)


def _adam_math(w, g, m, v):
    m = ADAM_B1 * m + (1.0 - ADAM_B1) * g
    v = ADAM_B2 * v + (1.0 - ADAM_B2) * (g * g)
    m_hat = m / (1.0 - ADAM_B1 ** ADAM_STEP)
    v_hat = v / (1.0 - ADAM_B2 ** ADAM_STEP)
    return -ADAM_LR * (m_hat / (jnp.sqrt(v_hat) + ADAM_EPS) + ADAM_WD * w), m, v


def _adamw(w, parts, m, v, name, tr=256):
    R, C = w.shape
    n, _, Cp = parts.shape
    tr = _tile(R, tr, 8)

    def body(w_ref, p_ref, m_ref, v_ref, g_out, d_out, m_out, v_out):
        g = p_ref[0, :, 0:C].astype(F32)
        for k in range(1, n):
            g = g + p_ref[k, :, 0:C].astype(F32)
        d, mn, vn = _adam_math(w_ref[...], g, m_ref[...], v_ref[...])
        g_out[...] = g
        d_out[...] = d
        m_out[...] = mn
        v_out[...] = vn

    spec = pl.BlockSpec((tr, C), lambda i: (i, 0))
    shape = jax.ShapeDtypeStruct((R, C), F32)
    return _pcall(body, grid=(R // tr,), in_specs=[spec, pl.BlockSpec((n, tr, Cp), lambda i: (0, i, 0)), spec, spec],
                  out_specs=[spec] * 4, out_shape=[shape] * 4, compiler_params=_params("parallel"), name=name)(w, parts, m, v)


def _adamw_chips(w, pair, parts, chip_ids, m, v, name, tr=256):
    R, C = w.shape
    Cp = pair.shape[2]
    by_columns = C == Cp and _tile(R, tr, 16) < 64
    tr, tc = (R, _tile(C, 256)) if by_columns else (_tile(R, tr, 16), C)

    def body(ids_ref, w_ref, own_ref, p1_ref, p2_ref, p3_ref, m_ref, v_ref, g_out, d_out, m_out, v_out):
        g = own_ref[:, 0:tc].astype(F32)
        for ref in (p1_ref, p2_ref, p3_ref):
            g = g + ref[:, 0:tc].astype(F32)
        d, mn, vn = _adam_math(w_ref[...], g, m_ref[...], v_ref[...])
        g_out[...] = g
        d_out[...] = d
        m_out[...] = mn
        v_out[...] = vn

    if by_columns:
        spec = pl.BlockSpec((tr, tc), lambda j, ids: (0, j))
    else:
        spec = pl.BlockSpec((tr, tc), lambda i, ids: (i, 0))

    def chip(k):
        if by_columns:
            return pl.BlockSpec((None, tr, tc), lambda j, ids: (ids[k], 0, j))
        return pl.BlockSpec((None, tr, Cp), lambda i, ids: (ids[k], i, 0))

    shape = jax.ShapeDtypeStruct((R, C), F32)
    grid_spec = pltpu.PrefetchScalarGridSpec(
        num_scalar_prefetch=1, grid=(C // tc if by_columns else R // tr,),
        in_specs=[spec, chip(0), chip(1), chip(2), chip(3), spec, spec], out_specs=[spec] * 4)
    return _pcall(body, grid_spec=grid_spec, out_shape=[shape] * 4, compiler_params=_params("parallel"),
                  name=name)(chip_ids, w, pair, parts, parts, parts, m, v)


def _place():
    return lax.axis_index("x"), lax.axis_index("y"), lax.axis_index("c")


def _other_chips(x, y):
    return [(1 - x, y), (x, 1 - y), (1 - x, 1 - y)]


IN_HBM = pl.BlockSpec(memory_space=pltpu.HBM)
SEM = pl.BlockSpec(memory_space=pltpu.SEMAPHORE)
EFFECT = pltpu.SideEffectType.DATAFLOW_SIDE_EFFECTING
TOKEN = jax.ShapeDtypeStruct((8, LANES), F32)
TOKEN_SPEC = pl.BlockSpec(memory_space=pltpu.VMEM)


def _in_hbm(a):
    return pltpu.with_memory_space_constraint(a, pltpu.HBM)


def _landing(shape):
    return _in_hbm(lax.empty(shape.shape, shape.dtype))


def _hbm_like(a):
    return pltpu.HBM(a.shape, a.dtype)


def _gather_places():
    x, y, c = _place()
    relay_from = (c * (1 - x) + (1 - c) * x, c * y + (1 - c) * (1 - y), c)
    relay_to = (c * x + (1 - c) * (1 - x), c * (1 - y) + (1 - c) * y, c)
    return (x, y, c), (x, y, 1 - c), (1 - x, y, c), (x, 1 - y, c), (1 - x, 1 - y, c), relay_from, relay_to


def _slot_copy(slot, ref, src, dst, send_sem, recv_sem, to):
    return pltpu.make_async_remote_copy(src_ref=slot(ref, *src), dst_ref=slot(ref, *dst), send_sem=send_sem,
                                        recv_sem=recv_sem, device_id=to, device_id_type=MESH)


def _split_call(body, arrays, sems_in, sems_out, after, name, token=True):
    na, ni, no = len(arrays), len(sems_in), len(sems_out)

    def wrapped(*refs):
        body(refs[:na], refs[na:na + ni], refs[na + ni + 1:na + ni + 1 + no])
        if token:
            refs[-1][...] = jnp.zeros_like(refs[-1])

    outs = _pcall(
        wrapped, in_specs=[IN_HBM] * na + [SEM] * ni + [HBM],
        out_specs=[SEM] * no + [IN_HBM] * na + ([TOKEN_SPEC] if token else []),
        out_shape=[pltpu.SemaphoreType.DMA((n,)) for n in sems_out] + [_hbm_like(s) for s in arrays] + ([TOKEN] if token else []),
        input_output_aliases={a: no + a for a in range(na)},
        compiler_params=pltpu.CompilerParams(has_side_effects=EFFECT), name=name,
    )(*[_in_hbm(s) for s in arrays], *sems_in, after)
    return list(outs[:no]), list(outs[no:no + na]), (outs[-1] if token else None)


def _gather_start(landing, slots, after, name):
    na = len(landing)

    def body(land, _, sems):
        me, sib, xn, yn, _, _, _ = _gather_places()
        for a in range(na):
            for k, to in enumerate((sib, xn, yn)):
                _slot_copy(slots[a], land[a], me, me, sems[0].at[3 * a + k], sems[1].at[3 * a + k], to).start()

    return _split_call(body, landing, [], [3 * na, 3 * na], after, name)


def _gather_relay(gathered, sems1, slots, after, name):
    na = len(gathered)

    def body(gath, taken, given):
        me, sib, xn, yn, _, relay_from, relay_to = _gather_places()
        for a in range(na):
            for k, peer in enumerate((sib, xn, yn)):
                arrival = _slot_copy(slots[a], gath[a], me, peer, taken[0].at[3 * a + k], taken[1].at[3 * a + k], peer)
                arrival.wait_send()
                arrival.wait_recv()
        for a in range(na):
            _slot_copy(slots[a], gath[a], relay_from, relay_from, given[0].at[a], given[1].at[a], relay_to).start()
            for k, peer in enumerate((xn, yn)):
                _slot_copy(slots[a], gath[a], peer, peer, given[2].at[2 * a + k], given[3].at[2 * a + k], sib).start()

    return _split_call(body, gathered, sems1, [na, na, 2 * na, 2 * na], after, name)


def _gather_pass(gathered, relay_sems, slots, after, name):
    na = len(gathered)

    def body(gath, taken, given):
        me, sib, xn, yn, diag, relay_from, relay_to = _gather_places()
        for a in range(na):
            _slot_copy(slots[a], gath[a], relay_from, relay_from, taken[0].at[a], taken[1].at[a], relay_to).wait_send()
            _slot_copy(slots[a], gath[a], me, diag, taken[0].at[a], taken[1].at[a], relay_to).wait_recv()
        for a in range(na):
            _slot_copy(slots[a], gath[a], diag, diag, given[0].at[a], given[1].at[a], sib).start()

    return _split_call(body, gathered, relay_sems, [na, na], after, name)


def _gather_finish(gathered, pass_sems, diag_sems, slots, after, name):
    na = len(gathered)

    def body(gath, taken, _):
        (x, y, c), sib, xn, yn, diag, _, _ = _gather_places()
        for a in range(na):
            for k, peer in enumerate((xn, yn)):
                passed = _slot_copy(slots[a], gath[a], peer, (peer[0], peer[1], 1 - c), taken[0].at[2 * a + k],
                                    taken[1].at[2 * a + k], sib)
                passed.wait_send()
                passed.wait_recv()
            passed = _slot_copy(slots[a], gath[a], diag, (diag[0], diag[1], 1 - c), taken[2].at[a], taken[3].at[a], sib)
            passed.wait_send()
            passed.wait_recv()

    return _split_call(body, gathered, list(pass_sems) + list(diag_sems), [], after, name, token=False)[1]


def _pair_copy(view, src, land, send_sems, recv_sems, chip):
    x, y, c = _place()
    return pltpu.make_async_remote_copy(
        src_ref=view(src, chip, 1 - c), dst_ref=land.at[chip], send_sem=send_sems.at[chip], recv_sem=recv_sems.at[chip],
        device_id=(x, y, 1 - c), device_id_type=MESH)


def _pair_start(grad, view, block, after, name):
    def body(src, land, after_ref, send_sems, recv_sems, src_thru, land_thru, token):
        for chip in range(N_CHIP):
            _pair_copy(view, src, land, send_sems, recv_sems, chip).start()
        token[...] = jnp.zeros_like(token)

    sems = pltpu.SemaphoreType.DMA((N_CHIP,))
    land = jax.ShapeDtypeStruct((N_CHIP, *block), BF16)
    return _pcall(
        body, in_specs=[IN_HBM, IN_HBM, HBM], out_specs=[SEM, SEM, IN_HBM, IN_HBM, TOKEN_SPEC],
        out_shape=[sems, sems, _hbm_like(grad), _hbm_like(land), TOKEN], input_output_aliases={0: 2, 1: 3},
        compiler_params=pltpu.CompilerParams(has_side_effects=EFFECT), name=name,
    )(_in_hbm(grad), _landing(land), after)


def _pair_wait(grad, recv, send_sems, recv_sems, view, after, name):
    def body(src, land, send, recv_s, after_ref, src_thru, land_thru):
        for chip in range(N_CHIP):
            copy = _pair_copy(view, src, land, send, recv_s, chip)
            copy.wait_send()
            copy.wait_recv()

    return _pcall(
        body, in_specs=[IN_HBM, IN_HBM, SEM, SEM, HBM], out_specs=[IN_HBM, IN_HBM],
        out_shape=[_hbm_like(grad), _hbm_like(recv)], input_output_aliases={0: 0, 1: 1},
        compiler_params=pltpu.CompilerParams(has_side_effects=EFFECT), name=name,
    )(grad, recv, send_sems, recv_sems, after)


def _chip_start(pair, after, name):
    def body(src, land, after_ref, send_sems, recv_sems, src_thru, land_thru, token):
        x, y, c = _place()
        for j, (px, py) in enumerate(_other_chips(x, y)):
            pltpu.make_async_remote_copy(
                src_ref=src.at[2 * px + py], dst_ref=land.at[2 * x + y], send_sem=send_sems.at[j], recv_sem=recv_sems.at[j],
                device_id=(px, py, c), device_id_type=MESH).start()
        token[...] = jnp.zeros_like(token)

    sems = pltpu.SemaphoreType.DMA((3,))
    return _pcall(
        body, in_specs=[IN_HBM, IN_HBM, HBM], out_specs=[SEM, SEM, IN_HBM, IN_HBM, TOKEN_SPEC],
        out_shape=[sems, sems, _hbm_like(pair), _hbm_like(pair), TOKEN], input_output_aliases={0: 2, 1: 3},
        compiler_params=pltpu.CompilerParams(has_side_effects=EFFECT), name=name,
    )(_in_hbm(pair), _landing(pair), after)


def _chip_wait(pair, parts, send_sems, recv_sems, after, name):
    def body(src, land, send, recv, after_ref, src_thru, land_thru):
        x, y, c = _place()
        for j, (px, py) in enumerate(_other_chips(x, y)):
            copy = pltpu.make_async_remote_copy(
                src_ref=src.at[2 * px + py], dst_ref=land.at[2 * px + py], send_sem=send.at[j], recv_sem=recv.at[j],
                device_id=(px, py, c), device_id_type=MESH)
            copy.wait_send()
            copy.wait_recv()

    return _pcall(
        body, in_specs=[IN_HBM, IN_HBM, SEM, SEM, HBM], out_specs=[IN_HBM, IN_HBM],
        out_shape=[_hbm_like(pair), _hbm_like(parts)], input_output_aliases={0: 0, 1: 1},
        compiler_params=pltpu.CompilerParams(has_side_effects=EFFECT), name=name,
    )(pair, parts, send_sems, recv_sems, after)


def _pair_add(core, grad, recv, block, grad_spec, name):
    _, R, C = recv.shape
    tr = block

    def body(c_ref, g_ref, r_ref, o_ref):
        o_ref[...] = (g_ref[...].astype(F32) + r_ref[...].astype(F32)).astype(BF16)

    grid_spec = pltpu.PrefetchScalarGridSpec(
        num_scalar_prefetch=1, grid=(N_CHIP, R // tr),
        in_specs=[grad_spec, pl.BlockSpec((None, tr, C), lambda k, i, c: (k, i, 0))],
        out_specs=pl.BlockSpec((None, tr, C), lambda k, i, c: (k, i, 0)))
    return _pcall(body, grid_spec=grid_spec, out_shape=jax.ShapeDtypeStruct(recv.shape, BF16),
                  compiler_params=_params("parallel", "parallel"), name=name)(core, grad, recv)


def _small_copies(gath, send_sems, recv_sems):
    x, y, c = _place()
    peers = [(x, y, 1 - c)] + [(px, py, pc) for px, py in _other_chips(x, y) for pc in (c, 1 - c)]
    pairs = []
    for a, ref in enumerate(gath):
        mine = ref.at[4 * x + 2 * y + c]
        for k, (px, py, pc) in enumerate(peers):
            sems = dict(send_sem=send_sems.at[7 * a + k], recv_sem=recv_sems.at[7 * a + k], device_id=(px, py, pc),
                        device_id_type=MESH)
            pairs.append((pltpu.make_async_remote_copy(src_ref=mine, dst_ref=mine, **sems),
                          pltpu.make_async_remote_copy(src_ref=mine, dst_ref=ref.at[4 * px + 2 * py + pc], **sems)))
    return pairs


def _small_start(landing, after, name):
    na = len(landing)

    def body(*refs):
        for send, _ in _small_copies(refs[:na], refs[na + 1], refs[na + 2]):
            send.start()
        refs[-1][...] = jnp.zeros_like(refs[-1])

    sems = pltpu.SemaphoreType.DMA((7 * na,))
    outs = _pcall(
        body, in_specs=[IN_HBM] * na + [HBM], out_specs=[SEM, SEM] + [IN_HBM] * na + [TOKEN_SPEC],
        out_shape=[sems, sems] + [_hbm_like(s) for s in landing] + [TOKEN],
        input_output_aliases={a: 2 + a for a in range(na)},
        compiler_params=pltpu.CompilerParams(has_side_effects=EFFECT), name=name,
    )(*[_in_hbm(s) for s in landing], after)
    return outs[0], outs[1], outs[2:2 + na], outs[-1]


def _small_wait(gathered, send_sems, recv_sems, after, name):
    na = len(gathered)

    def body(*refs):
        for send, arrival in _small_copies(refs[:na], refs[na], refs[na + 1]):
            send.wait_send()
            arrival.wait_recv()

    return list(_pcall(
        body, in_specs=[IN_HBM] * na + [SEM, SEM, HBM], out_specs=[IN_HBM] * na,
        out_shape=[_hbm_like(g) for g in gathered], input_output_aliases={a: a for a in range(na)},
        compiler_params=pltpu.CompilerParams(has_side_effects=EFFECT), name=name,
    )(*gathered, send_sems, recv_sems, after))


def _small_finish(gathered, params, name):
    na, npar = len(gathered), len(params)

    def body(*refs):
        g_refs, wmv = refs[:na], refs[na:na + 3 * npar]
        o_sums, o_params = refs[na + 3 * npar:2 * na + 3 * npar], refs[2 * na + 3 * npar:]
        sums = []
        for a in range(na):
            acc = g_refs[a][0]
            for k in range(1, N_DEV):
                acc = acc + g_refs[a][k]
            o_sums[a][...] = acc
            sums.append(acc)
        for j, (a, row, _, _, _) in enumerate(params):
            g = sums[a][row:row + 1, :]
            d, mn, vn = _adam_math(wmv[3 * j][...], g, wmv[3 * j + 1][...], wmv[3 * j + 2][...])
            for out, val in zip(o_params[4 * j:4 * j + 4], (g, d, mn, vn)):
                out[...] = val

    vm = pl.BlockSpec(memory_space=pltpu.VMEM)
    flat = [t for p in params for t in p[2:]]
    out_shape = [jax.ShapeDtypeStruct(g.shape[1:], F32) for g in gathered]
    out_shape += [jax.ShapeDtypeStruct(p[2].shape, F32) for p in params for _ in range(4)]
    outs = _pcall(body, in_specs=[vm] * (na + 3 * npar), out_specs=[vm] * len(out_shape), out_shape=out_shape,
                  name=name)(*gathered, *flat)
    return outs[:na], [outs[na + 4 * j:na + 4 * j + 4] for j in range(npar)]


def _local_step(x, tgt, gains, weights):
    g_pre_mix, g_post_mix, g_pre_ffn, g_post_ffn, g_sb, g_dil = gains
    S, D = x.shape
    hs = g_sb.shape[1] // HEAD_DIM
    hd = g_dil.shape[1] // HEAD_DIM
    cos2, sin_signed = _rope_tables(S)

    h1 = _rms_fwd(x, g_pre_mix + weights.start(), "rms_in")
    w_in_g = weights.w_in(h1)
    proj = _mm_nn(h1, w_in_g, BF16, "proj", tn=768)
    o_sb, ct_sb, mixed = _sb_fwd(proj, g_sb + weights.relay_out(proj), hs, hs + hd, "sb_fwd")
    o_dl, lse_dl, mixed = _dil_fwd(proj, cos2, sin_signed, g_dil + weights.after_sb(o_sb), mixed, 3 * hs, hd, "dil_fwd")
    w_out_g = weights.w_out(o_dl)
    mix = _mm_nn(mixed, w_out_g, F32, "mix_out", tn=1024)
    x2, h2 = _mid_fwd(x, mix, g_post_mix + weights.after_mix(mix), g_pre_ffn, "mid_fwd")
    w_up_g, cwb = weights.w_up(h2)
    u = _mm_nn(h2, w_up_g, BF16, "ffn_up", b_transposed=True)
    y = _geglu_fwd(u, cwb + weights.forward_down(u), "geglu_fwd")
    w_down_g = weights.w_down(y)
    f = _mm_nn(y, w_down_g, F32, "ffn_down", tn=1024, tk=2816)

    dy, df, dg_post_ffn, loss = _loss_bwd(x2, f, tgt, g_post_ffn, "loss_bwd")
    dyv = _mm_nt(df, w_down_g, BF16, "d_y", tn=1408)
    dw_down = _mm_tn(y, df, D, BF16, "dw_down", tm=1408, tn=1024)
    dc, dcw_g, dcw_v = _geglu_bwd(u, dyv, cwb + weights.grad("w_down", dw_down), "geglu_bwd")
    du = _conv_bwd(dc, cwb + weights.grad_reduce("w_down", dc), "conv_bwd")
    dh2 = _mm_nt(du, w_up_g, BF16, "d_h2", tk=1408, b_transposed=True, per_step=2)
    dw_up = _mm_tn(du, h2, D, BF16, "dw_up", tm=1408, tn=1024)
    dx2, dmix, dg_pre_ffn, dg_post_mix = _mid_bwd(
        dy, dh2, x2, mix, g_pre_ffn + weights.grad("w_up", dw_up), g_post_mix, "mid_bwd")
    dmixed = _mm_nt(dmix, w_out_g, BF16, "d_mixed", after=jnp.reshape(weights.grad_reduce("w_up", dmix), (1, 1)))
    dw_out = _mm_tn(mixed, dmix, D, BF16, "dw_out", tn=1024)
    dq_s, dk_s, dv_s, dg_sb = _sb_bwd(proj, g_sb + weights.grad("w_out", dw_out), o_sb, ct_sb, dmixed, 0, hs, "sb_bwd")
    dq_d, dk_d, dv_d, dg_dil = _dil_bwd(proj, cos2, sin_signed, g_dil + weights.grad_reduce("w_out", dq_s), o_dl, lse_dl,
                                        dmixed, hs, 3 * hs, hd, "dil_bwd")
    dproj = jnp.concatenate([dq_s, dk_s, dv_s, dq_d, dk_d, dv_d], axis=1)
    dw_in = _mm_tn(h1, dproj, w_in_g.shape[2], BF16, "dw_in", tn=768)
    weights.grad("w_in", dw_in)
    dep = weights.grad_reduce("w_in", dproj)
    dh1 = _mm_nt(dproj, w_in_g, BF16, "d_h1", tk=768, after=jnp.reshape(dep, (1, 1)), per_step=4)
    grad_x, dg_pre_mix = _first_bwd(dx2, dh1, x, g_pre_mix, "first_bwd")
    small = (dg_pre_mix, dg_post_mix, dg_pre_ffn, dg_post_ffn, dg_sb[0:1], dg_dil[0:1], jnp.concatenate([dcw_g, dcw_v], axis=1))
    weights.small(small, loss)
    return loss, grad_x, small


def _pad_cols(a, to):
    return jnp.pad(a, ((0, 0), (0, to - a.shape[1])))


def kernel(x, pre_mix_gain, post_mix_gain, pre_ffn_gain, post_ffn_gain, w_in, sb_out_gain, dil_out_gain, w_out, w_up, conv_w, conv_b, w_down, loss_target, m_pre_mix_gain, m_post_mix_gain, m_pre_ffn_gain, m_post_ffn_gain, m_w_in, m_sb_out_gain, m_dil_out_gain, m_w_out, m_w_up, m_conv_w, m_conv_b, m_w_down, v_pre_mix_gain, v_post_mix_gain, v_pre_ffn_gain, v_post_ffn_gain, v_w_in, v_sb_out_gain, v_dil_out_gain, v_w_out, v_w_up, v_conv_w, v_conv_b, v_w_down):
    xb, tb = x[0], loss_target[0]
    S, D = xb.shape
    w_in, w_out, w_up, w_down, conv_w = w_in[0], w_out[0], w_up[0], w_down[0], conv_w[0]
    n_in, e_rows = w_in.shape[1], w_out.shape[0]
    cu, half = w_up.shape[1], w_down.shape[0]
    assert cu == 2 * half and half % 16 == 0
    cup = -(-cu // LANES) * LANES
    fp = N_CHIP * cup
    px, py, pc = _place()
    me = 4 * px + 2 * py + pc
    core = jnp.reshape(pc, (1,)).astype(jnp.int32)

    w_up_t, m_up_t, v_up_t = (jnp.swapaxes(t, 0, 1) for t in (w_up, m_w_up[0], v_w_up[0]))

    def by_dev(ref, qx, qy, qc):
        return ref.at[4 * qx + 2 * qy + qc]

    def down_slot(ref, qx, qy, qc):
        return ref.at[2 * qx + qy, pl.ds(qc * half, half)]

    def by_pair(ref, chip, k):
        return ref.at[chip, k]

    def down_pair(ref, chip, k):
        return ref.at[chip, pl.ds(k * half, half)]

    def pair_spec(tr, cols):
        return pl.BlockSpec((None, None, tr, cols), lambda k, i, c: (k, c[0], i, 0))

    tr_in, tr_up = _tile(D, 512, 16), _tile(cup, 256, 16)
    grad_plan = {
        "w_in": ((N_CHIP, 2, D, n_in), by_pair, (D, n_in), tr_in, pair_spec(tr_in, n_in)),
        "w_out": ((N_CHIP, 2, e_rows, D), by_pair, (e_rows, D), e_rows, pair_spec(e_rows, D)),
        "w_up": ((N_CHIP, 2, cup, D), by_pair, (cup, D), tr_up, pair_spec(tr_up, D)),
        "w_down": ((N_CHIP, cup, D), down_pair, (half, D), half,
                   pl.BlockSpec((None, half, D), lambda k, i, c: (k, c[0], 0))),
    }

    class Exchanges:
        def __init__(self):
            self.in_flight = {}

        def start(self):
            def own_slot(shard):
                return lax.dynamic_update_index_in_dim(lax.empty((N_DEV, *shard.shape), shard.dtype), shard, me, 0)

            self.group_slots = {"in": [by_dev], "out": [by_dev], "up": [by_dev, by_dev], "down": [down_slot]}
            self.flight = {}
            sems, gath, token = _gather_start([own_slot(w_in.astype(BF16))], [by_dev], core, "gather_in_start")
            self.flight["in"] = (sems, gath)
            zero = token[0, 0]
            sems, gath, token = _gather_start([own_slot((w_out + zero).astype(BF16))], [by_dev], token, "gather_out_start")
            self.flight["out"] = (sems, gath)
            up = jnp.pad(w_up_t + zero, ((0, cup - cu), (0, 0))).astype(BF16)
            taps = jnp.pad(conv_w + zero, ((0, 8 - conv_w.shape[0]), (0, cup - cu)))
            sems, gath, token = _gather_start([own_slot(up), own_slot(taps)], [by_dev, by_dev], token, "gather_up_start")
            self.flight["up"] = (sems, gath)
            down = lax.dynamic_update_slice(jnp.zeros((N_CHIP, cup, D), BF16), (w_down + zero).astype(BF16)[None],
                                            (2 * px + py, pc * half, 0))
            sems, gath, token = _gather_start([down], [down_slot], token, "gather_down_start")
            self.flight["down"] = (sems, gath)
            return token[0, 0]

        def relay(self, group, after):
            sems, gath = self.flight[group]
            sems, gath, token = _gather_relay(gath, sems, self.group_slots[group], after, "gather_%s_relay" % group)
            self.flight[group] = (sems, gath)
            return token

        def pass_on(self, group, after):
            sems, gath = self.flight[group]
            diag_sems, gath, token = _gather_pass(gath, sems[:2], self.group_slots[group], after, "gather_%s_pass" % group)
            self.flight[group] = (sems[2:], diag_sems, gath)
            return token

        def finish(self, group, after):
            pass_sems, diag_sems, gath = self.flight[group]
            return _gather_finish(gath, pass_sems, diag_sems, self.group_slots[group], after, "gather_%s_finish" % group)

        def w_in(self, after):
            return self.finish("in", self.pass_on("in", self.relay("in", after)))[0]

        def relay_out(self, after):
            return self.relay("out", after)[0, 0]

        def after_sb(self, after):
            return self.relay("up", self.pass_on("out", after))[0, 0]

        def w_out(self, after):
            return self.finish("out", after)[0].reshape(1, N_DEV * e_rows, D)

        def after_mix(self, after):
            return self.relay("down", self.pass_on("up", after))[0, 0]

        def w_up(self, after):
            w_up_g, cw_g = self.finish("up", after)
            cb = _pad_cols(conv_b.reshape(N_DEV, cu), cup).reshape(1, 2 * fp)
            cw_full = jnp.transpose(cw_g[:, :3, :], (1, 0, 2)).reshape(3, 2 * fp)
            cwb = jnp.concatenate([cw_full, cb, jnp.zeros((4, 2 * fp), F32)], axis=0)
            return w_up_g, cwb

        def forward_down(self, after):
            return self.pass_on("down", after)[0, 0]

        def w_down(self, after):
            return self.finish("down", after)[0].reshape(1, fp, D)

        def small(self, small, loss):
            d_pre_mix, d_post_mix, d_pre_ffn, d_post_ffn, d_sb, d_dil, d_conv = small

            def rows_of(*vectors):
                n = vectors[0].shape[1]
                row = lax.broadcasted_iota(jnp.int32, (8, n), 0)
                out = jnp.zeros((8, n), F32)
                for k, vec in enumerate(vectors):
                    out = jnp.where(row == k, vec, out)
                return out

            parts = [rows_of(d_pre_mix, d_post_mix, d_pre_ffn, d_post_ffn, jnp.broadcast_to(loss[:, :1], (1, D))),
                     rows_of(d_sb, d_dil), d_conv]
            landing = [lax.dynamic_update_index_in_dim(lax.empty((N_DEV, *p.shape), F32), p, me, 0) for p in parts]
            self.small_flight = _small_start(landing, parts[0], "small_start")

        def small_sums(self, after):
            send, recv, gath, _ = self.small_flight
            gath = _small_wait(gath, send, recv, after, "small_wait")
            params = [(0, 0, pre_mix_gain, m_pre_mix_gain, v_pre_mix_gain), (0, 1, post_mix_gain, m_post_mix_gain, v_post_mix_gain),
                      (0, 2, pre_ffn_gain, m_pre_ffn_gain, v_pre_ffn_gain), (0, 3, post_ffn_gain, m_post_ffn_gain, v_post_ffn_gain),
                      (1, 0, sb_out_gain, m_sb_out_gain, v_sb_out_gain), (1, 1, dil_out_gain, m_dil_out_gain, v_dil_out_gain)]
            (gains_sum, _, conv_sum), gain_steps = _small_finish(gath, params, "small_finish")
            return gains_sum[4, 0], conv_sum, gain_steps

        def grad(self, name, dw):
            view_shape, view, block, tr, spec = grad_plan[name]
            send, recv_sems, dw, recv, token = _pair_start(dw.reshape(view_shape), view, block, core, "pair_start_" + name)
            self.in_flight[name] = (dw, recv, send, recv_sems)
            return token[0, 0]

        def grad_reduce(self, name, after):
            _, view, _, tr, spec = grad_plan[name]
            dw, recv = _pair_wait(*self.in_flight[name], view, after, "pair_wait_" + name)
            pair = _pair_add(core, dw, recv, tr, spec, "pair_add_" + name)
            send, recv_sems, pair, parts, token = _chip_start(pair, recv, "chip_start_" + name)
            self.in_flight[name] = (pair, parts, send, recv_sems)
            self.last_token = token
            return token[0, 0]

        def grad_parts(self, name, after):
            return _chip_wait(*self.in_flight[name], after, "chip_wait_" + name)

    exchanges = Exchanges()
    gains = (pre_mix_gain, post_mix_gain, pre_ffn_gain, post_ffn_gain, sb_out_gain, dil_out_gain)
    loss, grad_x, small = _local_step(xb, tb, gains, exchanges)

    def small_adam(w, g, m, v, name):
        one = w.shape[0] == 1
        if one:
            w, g, m, v = (jnp.broadcast_to(t, (8, t.shape[1])) for t in (w, g, m, v))
        outs = _adamw(w, g[None], m, v, name)
        return [o[0:1] for o in outs] if one else outs

    chip_ids = jnp.stack([2 * px + py, 2 * (1 - px) + py, 2 * px + 1 - py, 2 * (1 - px) + 1 - py]).astype(jnp.int32)
    out_w_down = _adamw_chips(w_down, *exchanges.grad_parts("w_down", exchanges.small_flight[3]), chip_ids, m_w_down[0], v_w_down[0], "adam_w_down")
    out_up_t = _adamw_chips(w_up_t, *exchanges.grad_parts("w_up", out_w_down[1]), chip_ids, m_up_t, v_up_t, "adam_w_up")
    out_w_up = [jnp.swapaxes(o, 0, 1) for o in out_up_t]
    out_w_out = _adamw_chips(w_out, *exchanges.grad_parts("w_out", out_up_t[1]), chip_ids, m_w_out[0], v_w_out[0], "adam_w_out")
    loss_out, g_conv, gain_steps = exchanges.small_sums(out_w_out[1])
    out_pre_mix, out_post_mix, out_pre_ffn, out_post_ffn, out_sb, out_dil = gain_steps
    g_conv_b = g_conv[3].reshape(N_DEV, cup)[:, :cu].reshape(1, N_DEV * cu)
    g_conv_w = lax.dynamic_index_in_dim(g_conv[0:3].reshape(3, N_DEV, cup), me, axis=1, keepdims=False)[:, :cu]
    out_conv_b = small_adam(conv_b, g_conv_b, m_conv_b, v_conv_b, "adam_conv_b")
    cw8 = [jnp.pad(t, ((0, 5), (0, 0))) for t in (conv_w, g_conv_w, m_conv_w[0], v_conv_w[0])]
    out_conv_w = [o[0:3] for o in _adamw(cw8[0], cw8[1][None], cw8[2], cw8[3], "adam_conv_w")]
    out_w_in = _adamw_chips(w_in, *exchanges.grad_parts("w_in", out_conv_w[1]), chip_ids, m_w_in[0], v_w_in[0], "adam_w_in")

    order = [out_pre_mix, out_post_mix, out_pre_ffn, out_post_ffn, [o[None] for o in out_w_in], out_sb, out_dil,
             [o[None] for o in out_w_out], [o[None] for o in out_w_up], [o[None] for o in out_conv_w], out_conv_b,
             [o[None] for o in out_w_down]]
    outs = [loss_out, grad_x[None]]
    for k in range(4):
        outs += [o[k] for o in order]
    return tuple(outs)
```

```python
import functools
import math

import jax
import jax.numpy as jnp
from jax import lax
from jax.experimental import pallas as pl
from jax.experimental.pallas import tpu as pltpu

F32 = jnp.float32
BF16 = jnp.bfloat16
HEAD_DIM = 128
LANES = 128
KEY_BLOCK = 128
DILATIONS = (1, 4, 16)
RMS_EPS = 1e-6
ROPE_THETA = 10000.0
NEG = -1e30
ADAM_LR, ADAM_B1, ADAM_B2, ADAM_EPS, ADAM_WD, ADAM_STEP = 0.001, 0.9, 0.999, 1e-08, 0.01, 10
MESH = pl.DeviceIdType.MESH
N_DEV = 8
N_CHIP = 4
HBM = pl.BlockSpec(memory_space=pl.ANY)
VMEM_LIMIT = 56 * 1024 * 1024

_pcall = pl.pallas_call


def _tile(n, pref, mult=LANES):
    best = None
    t = mult
    while t <= min(n, pref):
        if n % t == 0:
            best = t
        t += mult
    return n if best is None else best


def _params(*sem):
    return pltpu.CompilerParams(dimension_semantics=sem, vmem_limit_bytes=VMEM_LIMIT)


def _dot(a, b, dims):
    return lax.dot_general(a, b, (dims, ((), ())), preferred_element_type=F32)


NN = ((1,), (0,))
NT = ((1,), (1,))
TN = ((0,), (0,))


def _mm_body(dims, nk, tile):
    if nk == 1:
        def single(a_ref, b_ref, o_ref):
            o_ref[...] = _dot(a_ref[...].astype(BF16), b_ref[...].astype(BF16), dims).astype(o_ref.dtype)

        return single, []

    def body(a_ref, b_ref, o_ref, acc_ref):
        k = pl.program_id(2)

        @pl.when(k == 0)
        def _():
            acc_ref[...] = jnp.zeros_like(acc_ref)

        acc_ref[...] += _dot(a_ref[...].astype(BF16), b_ref[...].astype(BF16), dims)

        @pl.when(k == nk - 1)
        def _():
            o_ref[...] = acc_ref[...].astype(o_ref.dtype)

    return body, [pltpu.VMEM(tile, F32)]


def _mm_nn(a, b3, out_dtype, name, tm=1024, tn=1408, tk=2048, b_transposed=False):
    M, K = a.shape
    C, n = b3.shape[0], b3.shape[1 if b_transposed else 2]
    tm, tk, tn = _tile(M, tm, 8), _tile(K, tk), _tile(n, tn)
    npc, nk = n // tn, K // tk
    body, scratch = _mm_body(NT if b_transposed else NN, nk, (tm, tn))
    b_spec = (pl.BlockSpec((None, tn, tk), lambda i, j, k: (j // npc, j % npc, k)) if b_transposed
              else pl.BlockSpec((None, tk, tn), lambda i, j, k: (j // npc, k, j % npc)))
    return _pcall(
        body, grid=(M // tm, C * npc, nk),
        in_specs=[pl.BlockSpec((tm, tk), lambda i, j, k: (i, k)), b_spec],
        out_specs=pl.BlockSpec((tm, tn), lambda i, j, k: (i, j)),
        out_shape=jax.ShapeDtypeStruct((M, C * n), out_dtype), scratch_shapes=scratch,
        compiler_params=_params("parallel", "parallel", "arbitrary"), name=name)(a, b3)


def _mm_nt(a, b3, out_dtype, name, tm=1024, tn=1024, tk=2048, after=None, b_transposed=False, per_step=1):
    M, _ = a.shape
    C, N, n = (b3.shape[0], b3.shape[2], b3.shape[1]) if b_transposed else b3.shape
    tm, tn, tk = _tile(M, tm, 8), _tile(N, tn), _tile(n, tk)
    dims = NN if b_transposed else NT
    extra = [] if after is None else [after]
    if per_step > 1 and tk == n and C % per_step == 0:
        nk, scratch = C // per_step, [pltpu.VMEM((tm, tn), F32)]
        b3 = b3.reshape(nk, per_step, *b3.shape[1:])
        a_spec = pl.BlockSpec((tm, per_step * n), lambda i, j, k: (i, k))
        if b_transposed:
            b_spec = pl.BlockSpec((None, per_step, n, tn), lambda i, j, k: (k, 0, 0, j))
        else:
            b_spec = pl.BlockSpec((None, per_step, tn, n), lambda i, j, k: (k, 0, j, 0))

        def body(a_ref, b_ref, *rest):
            o_ref, acc_ref = rest[len(extra):]
            k = pl.program_id(2)

            @pl.when(k == 0)
            def _():
                acc_ref[...] = jnp.zeros_like(acc_ref)

            b = b_ref[...].astype(BF16)
            b = b.reshape(per_step * n, tn) if b_transposed else jnp.concatenate([b[u] for u in range(per_step)], axis=1)
            acc_ref[...] += _dot(a_ref[...].astype(BF16), b, dims)

            @pl.when(k == nk - 1)
            def _():
                o_ref[...] = acc_ref[...].astype(o_ref.dtype)
    else:
        kpc = n // tk
        nk = C * kpc
        inner, scratch = _mm_body(dims, nk, (tm, tn))
        a_spec = pl.BlockSpec((tm, tk), lambda i, j, k: (i, k))
        b_spec = (pl.BlockSpec((None, tk, tn), lambda i, j, k: (k // kpc, k % kpc, j)) if b_transposed
                  else pl.BlockSpec((None, tn, tk), lambda i, j, k: (k // kpc, j, k % kpc)))

        def body(a_ref, b_ref, *rest):
            inner(a_ref, b_ref, *rest[len(extra):])

    return _pcall(
        body, grid=(M // tm, N // tn, nk), in_specs=[a_spec, b_spec] + [HBM] * len(extra),
        out_specs=pl.BlockSpec((tm, tn), lambda i, j, k: (i, j)),
        out_shape=jax.ShapeDtypeStruct((M, N), out_dtype), scratch_shapes=scratch,
        compiler_params=_params("parallel", "parallel", "arbitrary"), name=name)(a, b3, *extra)


def _mm_tn(x, y, n, out_dtype, name, tm=1024, tn=1408, tk=2048, after=None):
    S, P = x.shape
    C = y.shape[1] // n
    tm, tn, tk = _tile(P, tm), _tile(n, tn), _tile(S, tk, 8)
    npc, nk = n // tn, S // tk
    inner, scratch = _mm_body(TN, nk, (tm, tn))
    extra = [] if after is None else [after]

    def body(x_ref, y_ref, *rest):
        inner(x_ref, y_ref, *rest[len(extra):])

    return _pcall(
        body, grid=(P // tm, C * npc, nk),
        in_specs=[pl.BlockSpec((tk, tm), lambda i, j, k: (k, i)),
                  pl.BlockSpec((tk, tn), lambda i, j, k: (k, j))] + [HBM] * len(extra),
        out_specs=pl.BlockSpec((None, tm, tn), lambda i, j, k: (j // npc, i, j % npc)),
        out_shape=jax.ShapeDtypeStruct((C, P, n), out_dtype), scratch_shapes=scratch,
        compiler_params=_params("parallel", "parallel", "arbitrary"), name=name)(x, y, *extra)


def _rms_scale(v):
    return lax.rsqrt(jnp.mean(v * v, axis=-1, keepdims=True) + RMS_EPS)


def _rms_bwd(gy, v, r):
    return r * gy - v * (r * r * r * jnp.mean(gy * v, axis=-1, keepdims=True))


def _rows_spec(tm, d):
    return pl.BlockSpec((tm, d), lambda i: (i, 0))


def _vec_spec(d):
    return pl.BlockSpec((1, d), lambda i: (0, 0))


def _rms_fwd(x, g, name, tm=256):
    S, D = x.shape

    def body(x_ref, g_ref, h_ref):
        v = x_ref[...]
        h_ref[...] = (v * _rms_scale(v) * g_ref[...]).astype(BF16)

    return _pcall(body, grid=(S // tm,), in_specs=[_rows_spec(tm, D), _vec_spec(D)], out_specs=_rows_spec(tm, D),
                  out_shape=jax.ShapeDtypeStruct((S, D), BF16), compiler_params=_params("parallel"), name=name)(x, g)


def _mid_fwd(x, mix, g_post, g_pre, name, tm=256):
    S, D = x.shape

    def body(x_ref, m_ref, gp_ref, gn_ref, x2_ref, h_ref):
        m = m_ref[...]
        x2 = x_ref[...] + m * _rms_scale(m) * gp_ref[...]
        x2_ref[...] = x2
        h_ref[...] = (x2 * _rms_scale(x2) * gn_ref[...]).astype(BF16)

    return _pcall(body, grid=(S // tm,), in_specs=[_rows_spec(tm, D), _rows_spec(tm, D), _vec_spec(D), _vec_spec(D)],
                  out_specs=[_rows_spec(tm, D), _rows_spec(tm, D)],
                  out_shape=[jax.ShapeDtypeStruct((S, D), F32), jax.ShapeDtypeStruct((S, D), BF16)],
                  compiler_params=_params("parallel"), name=name)(x, mix, g_post, g_pre)


def _loss_bwd(x2, f, tgt, g_post, name, tm=256):
    S, D = x2.shape

    def body(x2_ref, f_ref, t_ref, g_ref, dy_ref, df_ref, dg_ref, ls_ref):
        i = pl.program_id(0)

        @pl.when(i == 0)
        def _():
            dg_ref[...] = jnp.zeros_like(dg_ref)
            ls_ref[...] = jnp.zeros_like(ls_ref)

        fv = f_ref[...]
        r = _rms_scale(fv)
        g = g_ref[...]
        err = x2_ref[...] + fv * r * g - t_ref[...]
        ls_ref[...] += jnp.broadcast_to(0.5 * jnp.sum(jnp.mean(err * err, axis=-1, keepdims=True), axis=0, keepdims=True), ls_ref.shape)
        dy = err * (1.0 / D)
        dy_ref[...] = dy
        df_ref[...] = _rms_bwd(dy * g, fv, r).astype(BF16)
        dg_ref[...] += jnp.sum(dy * fv * r, axis=0, keepdims=True)

    return _pcall(body, grid=(S // tm,),
                  in_specs=[_rows_spec(tm, D), _rows_spec(tm, D), _rows_spec(tm, D), _vec_spec(D)],
                  out_specs=[_rows_spec(tm, D), _rows_spec(tm, D), _vec_spec(D), _vec_spec(LANES)],
                  out_shape=[jax.ShapeDtypeStruct((S, D), F32), jax.ShapeDtypeStruct((S, D), BF16),
                             jax.ShapeDtypeStruct((1, D), F32), jax.ShapeDtypeStruct((1, LANES), F32)],
                  compiler_params=_params("arbitrary"), name=name)(x2, f, tgt, g_post)


def _mid_bwd(dy, dh2, x2, mix, g_pre, g_post, name, tm=256):
    S, D = dy.shape

    def body(dy_ref, dh_ref, x2_ref, m_ref, gn_ref, gp_ref, dx2_ref, dm_ref, dgn_ref, dgp_ref):
        i = pl.program_id(0)

        @pl.when(i == 0)
        def _():
            dgn_ref[...] = jnp.zeros_like(dgn_ref)
            dgp_ref[...] = jnp.zeros_like(dgp_ref)

        x2, dh = x2_ref[...], dh_ref[...].astype(F32)
        r = _rms_scale(x2)
        dx2 = dy_ref[...] + _rms_bwd(dh * gn_ref[...], x2, r)
        dgn_ref[...] += jnp.sum(dh * x2 * r, axis=0, keepdims=True)
        dx2_ref[...] = dx2
        m = m_ref[...]
        rm = _rms_scale(m)
        dm_ref[...] = _rms_bwd(dx2 * gp_ref[...], m, rm).astype(BF16)
        dgp_ref[...] += jnp.sum(dx2 * m * rm, axis=0, keepdims=True)

    return _pcall(body, grid=(S // tm,),
                  in_specs=[_rows_spec(tm, D)] * 4 + [_vec_spec(D)] * 2,
                  out_specs=[_rows_spec(tm, D), _rows_spec(tm, D), _vec_spec(D), _vec_spec(D)],
                  out_shape=[jax.ShapeDtypeStruct((S, D), F32), jax.ShapeDtypeStruct((S, D), BF16),
                             jax.ShapeDtypeStruct((1, D), F32), jax.ShapeDtypeStruct((1, D), F32)],
                  compiler_params=_params("arbitrary"), name=name)(dy, dh2, x2, mix, g_pre, g_post)


def _first_bwd(dx2, dh1, x, g_pre, name, tm=256):
    S, D = x.shape

    def body(dx2_ref, dh_ref, x_ref, g_ref, gx_ref, dg_ref):
        i = pl.program_id(0)

        @pl.when(i == 0)
        def _():
            dg_ref[...] = jnp.zeros_like(dg_ref)

        xv, dh = x_ref[...], dh_ref[...].astype(F32)
        r = _rms_scale(xv)
        gx_ref[...] = dx2_ref[...] + _rms_bwd(dh * g_ref[...], xv, r)
        dg_ref[...] += jnp.sum(dh * xv * r, axis=0, keepdims=True)

    return _pcall(body, grid=(S // tm,), in_specs=[_rows_spec(tm, D)] * 3 + [_vec_spec(D)],
                  out_specs=[_rows_spec(tm, D), _vec_spec(D)],
                  out_shape=[jax.ShapeDtypeStruct((S, D), F32), jax.ShapeDtypeStruct((1, D), F32)],
                  compiler_params=_params("arbitrary"), name=name)(dx2, dh1, x, g_pre)


def _logsig_pair(z):
    lb = jnp.minimum(z, 0.0) - jnp.log(1.0 + jnp.exp(-jnp.abs(z)))
    return lb, lb - z


SB_KEY_BLOCK = 256


def _sum_matrix(strict):
    ia = lax.broadcasted_iota(jnp.int32, (SB_KEY_BLOCK, SB_KEY_BLOCK), 0)
    ib = lax.broadcasted_iota(jnp.int32, (SB_KEY_BLOCK, SB_KEY_BLOCK), 1)
    return ((ia > ib) if strict == ">" else (ia < ib)).astype(BF16)


def _row_total(sums, v, col):
    return jnp.broadcast_to(sums[:, col:col + 1] + v[:, col:col + 1], (v.shape[0], LANES))


def _lanes(c, width):
    return jnp.tile(c, (1, width // LANES))


def _split_dot(v, u):
    hi = v.astype(BF16)
    lo = (v - hi.astype(F32)).astype(BF16)
    return _dot(hi, u, NN) + _dot(lo, u, NN)


def _head_out(o, g):
    return o * _rms_scale(o) * g


def _sb_fwd(proj, gain, n_heads, mixed_heads, name, tq=1024):
    S = proj.shape[0]
    H, tk = n_heads, SB_KEY_BLOCK
    tq = _tile(S, tq, 2 * tk)
    scale = HEAD_DIM ** -0.5

    def body(q_ref, k_ref, v_ref, g_ref, o_ref, ct_ref, mx_ref, oacc, cacc):
        i = pl.program_id(1)
        oacc[...] = jnp.zeros_like(oacc)
        cacc[...] = jnp.zeros_like(cacc)
        sums = _sum_matrix(">")

        def run(blocks):
            scored = []
            for k0, r0, diagonal in blocks:
                rows = pl.ds(r0, tq - r0)
                lb, lk = _logsig_pair(_dot(q_ref[rows, :].astype(BF16), k_ref[pl.ds(k0, tk), :].astype(BF16), NT) * scale)
                causal = None
                if diagonal:
                    causal = (lax.broadcasted_iota(jnp.int32, (tq - r0, tk), 1)
                              < lax.broadcasted_iota(jnp.int32, (tq - r0, tk), 0))
                    lk = jnp.where(causal, lk, 0.0)
                scored.append((k0, rows, causal, lb, lk))
            summed = [(k0, rows, causal, lb, lk, _split_dot(lk, sums)) for k0, rows, causal, lb, lk in scored]
            weights = []
            for k0, rows, causal, lb, lk, after in summed:
                c = cacc[rows, :]
                a = jnp.exp(lb + after + _lanes(c, tk))
                if causal is not None:
                    a = jnp.where(causal, a, 0.0)
                cacc[rows, :] = c + _row_total(after, lk, 0)
                weights.append((k0, rows, a.astype(BF16)))
            for k0, rows, a in weights:
                oacc[rows, :] += _dot(a, v_ref[pl.ds(k0, tk), :].astype(BF16), NN)

        for d in reversed(range(0, tq // tk, 2)):
            run([(pl.multiple_of(i * tq + e * tk, tk), e * tk, True) for e in (d + 1, d)])
        per_trip = tq // tk

        def step(it, carry):
            k0 = pl.multiple_of((i - 1 - it) * tq, tq)
            run([(pl.multiple_of(k0 + e * tk, tk), 0, False) for e in reversed(range(per_trip))])
            return carry

        lax.fori_loop(0, i, step, 0)
        o = oacc[...]
        o_ref[...] = o
        ct_ref[...] = cacc[...]
        mx_ref[...] = _head_out(o, g_ref[...]).astype(BF16)

    blk = pl.BlockSpec((tq, HEAD_DIM), lambda h, i: (i, h))
    return _pcall(
        body, grid=(H, S // tq),
        in_specs=[blk, pl.BlockSpec((S, HEAD_DIM), lambda h, i: (0, H + h)),
                  pl.BlockSpec((S, HEAD_DIM), lambda h, i: (0, 2 * H + h)), pl.BlockSpec((1, HEAD_DIM), lambda h, i: (0, h))],
        out_specs=[blk, blk, blk],
        out_shape=[jax.ShapeDtypeStruct((S, H * HEAD_DIM), F32), jax.ShapeDtypeStruct((S, H * HEAD_DIM), F32),
                   jax.ShapeDtypeStruct((S, mixed_heads * HEAD_DIM), BF16)],
        scratch_shapes=[pltpu.VMEM((tq, HEAD_DIM), F32), pltpu.VMEM((tq, LANES), F32)],
        compiler_params=_params("parallel", "arbitrary"), name=name)(proj, proj, proj, gain)


def _sb_bwd(proj, gain, o_raw, ctot, dmixed, dm_col0, n_heads, name, tq=1024):
    S = proj.shape[0]
    H, tk = n_heads, SB_KEY_BLOCK
    tq = _tile(S, tq, 2 * tk)
    nq = S // tq
    scale = HEAD_DIM ** -0.5

    def body(q_ref, k_ref, v_ref, g_ref, o_ref, ct_ref, dm_ref, dq_ref, dk_ref, dv_ref, dg_ref,
             dkacc, dvacc, dqacc, pfx, gcar, dos):
        i = pl.program_id(1)

        @pl.when(i == 0)
        def _():
            dkacc[...] = jnp.zeros_like(dkacc)
            dvacc[...] = jnp.zeros_like(dvacc)
            dg_ref[...] = jnp.zeros_like(dg_ref)

        o, dm, g = o_ref[...], dm_ref[...].astype(F32), g_ref[...]
        r = _rms_scale(o)
        dos[...] = _rms_bwd(dm * g, o, r).astype(BF16)
        dg_ref[...] += jnp.broadcast_to(jnp.sum(dm * o * r, axis=0, keepdims=True), dg_ref.shape)
        dqacc[...] = jnp.zeros_like(dqacc)
        pfx[...] = jnp.zeros_like(pfx)
        gcar[...] = jnp.zeros_like(gcar)
        later, earlier = _sum_matrix(">"), _sum_matrix("<")

        def run(blocks):
            scored = []
            for k0, r0, diagonal in blocks:
                rows, keys = pl.ds(r0, tq - r0), pl.ds(k0, tk)
                lb, lk = _logsig_pair(_dot(q_ref[rows, :].astype(BF16), k_ref[keys, :].astype(BF16), NT) * scale)
                da = _dot(dos[rows, :], v_ref[keys, :].astype(BF16), NT)
                causal = None
                if diagonal:
                    causal = (lax.broadcasted_iota(jnp.int32, (tq - r0, tk), 1)
                              < lax.broadcasted_iota(jnp.int32, (tq - r0, tk), 0))
                    lk = jnp.where(causal, lk, 0.0)
                scored.append((rows, keys, causal, lb, lk, da))
            summed = [(*blk, _split_dot(blk[4], later)) for blk in scored]
            weighted = []
            for rows, keys, causal, lb, lk, da, after in summed:
                p = pfx[rows, :] + _row_total(after, lk, 0)
                pfx[rows, :] = p
                a = jnp.exp(lb + after + _lanes(ct_ref[rows, :] - p, tk))
                if causal is not None:
                    a = jnp.where(causal, a, 0.0)
                dl = da * a
                weighted.append((rows, keys, causal, lb, a.astype(BF16), dl, _dot(dl.astype(BF16), earlier, NN)))
            cotangents = []
            for rows, keys, causal, lb, a, dl, before in weighted:
                gc = gcar[rows, :]
                gcar[rows, :] = gc + _row_total(before, dl, tk - 1)
                sig = jnp.exp(lb)
                gsum = (before + _lanes(gc, tk)) * sig
                if causal is not None:
                    gsum = jnp.where(causal, gsum, 0.0)
                cotangents.append((rows, keys, a, ((dl * (1.0 - sig) - gsum) * scale).astype(BF16)))
            for rows, keys, a, dz in cotangents:
                q, do = q_ref[rows, :].astype(BF16), dos[rows, :]
                dvacc[keys, :] += _dot(a, do, TN)
                dqacc[rows, :] += _dot(dz, k_ref[keys, :].astype(BF16), NN)
                dkacc[keys, :] += _dot(dz, q, TN)

        def step(j, carry):
            k0 = pl.multiple_of(j * 2 * tk, 2 * tk)
            run([(k0, 0, False), (pl.multiple_of(k0 + tk, tk), 0, False)])
            return carry

        lax.fori_loop(0, i * (tq // tk // 2), step, 0)
        for d in range(0, tq // tk, 2):
            run([(pl.multiple_of(i * tq + e * tk, tk), e * tk, True) for e in (d, d + 1)])
        dq_ref[...] = dqacc[...].astype(BF16)

        @pl.when(i == nq - 1)
        def _():
            dk_ref[...] = dkacc[...].astype(BF16)
            dv_ref[...] = dvacc[...].astype(BF16)

    blk = pl.BlockSpec((tq, HEAD_DIM), lambda h, i: (i, h))
    full = pl.BlockSpec((S, HEAD_DIM), lambda h, i: (0, h))
    W = H * HEAD_DIM
    return _pcall(
        body, grid=(H, nq),
        in_specs=[blk, pl.BlockSpec((S, HEAD_DIM), lambda h, i: (0, H + h)),
                  pl.BlockSpec((S, HEAD_DIM), lambda h, i: (0, 2 * H + h)), pl.BlockSpec((1, HEAD_DIM), lambda h, i: (0, h)),
                  blk, blk, pl.BlockSpec((tq, HEAD_DIM), lambda h, i: (i, dm_col0 + h))],
        out_specs=[blk, full, full, pl.BlockSpec((8, HEAD_DIM), lambda h, i: (0, h))],
        out_shape=[jax.ShapeDtypeStruct((S, W), BF16), jax.ShapeDtypeStruct((S, W), BF16),
                   jax.ShapeDtypeStruct((S, W), BF16), jax.ShapeDtypeStruct((8, W), F32)],
        scratch_shapes=[pltpu.VMEM((S, HEAD_DIM), F32), pltpu.VMEM((S, HEAD_DIM), F32), pltpu.VMEM((tq, HEAD_DIM), F32),
                        pltpu.VMEM((tq, LANES), F32), pltpu.VMEM((tq, LANES), F32), pltpu.VMEM((tq, HEAD_DIM), BF16)],
        compiler_params=_params("arbitrary", "arbitrary"), name=name)(proj, proj, proj, gain, o_raw, ctot, dmixed)


def _rope_tables(S):
    inv_freq = ROPE_THETA ** (-jnp.arange(0, HEAD_DIM, 2, dtype=F32) / HEAD_DIM)
    ang = jnp.arange(S, dtype=F32)[:, None] * inv_freq[None, :]
    cos, sin = jnp.cos(ang), jnp.sin(ang)
    return jnp.concatenate([cos, cos], axis=1), jnp.concatenate([-sin, sin], axis=1)


def _rope(v, cos2, sin_signed):
    return v * cos2 + pltpu.roll(v, HEAD_DIM // 2, axis=1) * sin_signed


def _dil_rows(d, r, l0, n):
    if d == 1:
        return pl.ds(l0 if isinstance(l0, int) else pl.multiple_of(l0, KEY_BLOCK), n)
    return pl.ds(r + d * l0, n, stride=d)


def _dil_blocks(S, visit):
    B = KEY_BLOCK
    group = 16
    for b, d in enumerate(DILATIONS):
        nb = S // d // B
        if nb == 1:
            g = math.gcd(d, group)

            def trip(t, carry, b=b, d=d, g=g):
                visit([(b, d, t * g + u, 0, True) for u in range(g)])
                return carry

            lax.fori_loop(0, d // g, trip, 0)
        elif d == 1:
            visit([(b, d, 0, 0, True)])
            g = max(k for k in range(1, group + 2) if (nb - 1) % k == 0)

            def trip(t, carry, b=b, d=d, g=g):
                visit([(b, d, 0, (1 + t * g + u) * B, False) for u in range(g)])
                return carry

            lax.fori_loop(0, (nb - 1) // g, trip, 0)
        else:
            g = math.gcd(d, max(group // nb, 1))

            def trip(t, carry, b=b, d=d, nb=nb, g=g):
                visit([(b, d, t * g + u, n * B, n == 0) for u in range(g) for n in range(nb)])
                return carry

            lax.fori_loop(0, d // g, trip, 0)


def _dil_mask(first):
    B = KEY_BLOCK
    nk = B if first else 2 * B
    iq = lax.broadcasted_iota(jnp.int32, (B, nk), 0)
    ik = lax.broadcasted_iota(jnp.int32, (B, nk), 1)
    return (ik <= iq) if first else ((ik >= iq) & (ik <= iq + B))


def _dil_fwd(proj, cos2, sin_signed, gain, mixed, col0, n_heads, name):
    S = proj.shape[0]
    H, B = n_heads, KEY_BLOCK
    scale = HEAD_DIM ** -0.5
    rc = _tile(S, 256, 8)

    def body(q_ref, k_ref, v_ref, c_ref, s_ref, g_ref, mixed_in, o_ref, l_ref, mx_ref, qr, kr, vf, *per_branch):
        ob, lb = per_branch[:len(DILATIONS)], per_branch[len(DILATIONS):]

        def rope_rows(t, carry):
            rows = pl.ds(pl.multiple_of(t * rc, rc), rc)
            qr[rows, :] = _rope(q_ref[rows, :].astype(F32), c_ref[rows, :], s_ref[rows, :])
            kr[rows, :] = _rope(k_ref[rows, :].astype(F32), c_ref[rows, :], s_ref[rows, :])
            vf[rows, :] = v_ref[rows, :].astype(F32)
            return carry

        lax.fori_loop(0, S // rc, rope_rows, 0)

        def visit(blocks):
            scores = []
            for b, d, r, l0, first in blocks:
                qrows = _dil_rows(d, r, l0, B)
                krows = qrows if first else _dil_rows(d, r, l0 - B, 2 * B)
                s = _dot(qr[qrows, :].astype(BF16), kr[krows, :].astype(BF16), NT) * scale
                scores.append((b, qrows, krows, jnp.where(_dil_mask(first), s, NEG)))
            weights = []
            for b, qrows, krows, s in scores:
                m = jnp.max(s, axis=1, keepdims=True)
                p = jnp.exp(s - m)
                den = jnp.sum(p, axis=1, keepdims=True)
                lb[b][qrows, :] = jnp.broadcast_to(m + jnp.log(den), (B, LANES))
                weights.append((b, qrows, krows, p.astype(BF16), den))
            for b, qrows, krows, p, den in weights:
                ob[b][qrows, :] = _dot(p, vf[krows, :].astype(BF16), NN) / den

        _dil_blocks(S, visit)

        def combine(t, carry):
            rows = pl.ds(pl.multiple_of(t * rc, rc), rc)
            l0, l1, l2 = lb[0][rows, :], lb[1][rows, :], lb[2][rows, :]
            m = jnp.maximum(jnp.maximum(l0, l1), l2)
            w0, w1, w2 = jnp.exp(l0 - m), jnp.exp(l1 - m), jnp.exp(l2 - m)
            den = w0 + w1 + w2
            o = (w0 * ob[0][rows, :] + w1 * ob[1][rows, :] + w2 * ob[2][rows, :]) / den
            o_ref[rows, :] = o
            l_ref[rows, :] = m + jnp.log(den)
            mx_ref[rows, :] = _head_out(o, g_ref[...]).astype(BF16)
            return carry

        lax.fori_loop(0, S // rc, combine, 0)

    def col(k):
        return pl.BlockSpec((S, HEAD_DIM), lambda h: (0, col0 + k * H + h))

    tab = pl.BlockSpec((S, HEAD_DIM), lambda h: (0, 0))
    out = pl.BlockSpec((S, HEAD_DIM), lambda h: (0, h))
    W = H * HEAD_DIM
    first = mixed.shape[1] // HEAD_DIM - H
    return _pcall(
        body, grid=(H,),
        in_specs=[col(0), col(1), col(2), tab, tab, pl.BlockSpec((1, HEAD_DIM), lambda h: (0, h)), HBM],
        out_specs=[out, out, pl.BlockSpec((S, HEAD_DIM), lambda h: (0, first + h))],
        out_shape=[jax.ShapeDtypeStruct((S, W), F32), jax.ShapeDtypeStruct((S, W), F32),
                   jax.ShapeDtypeStruct(mixed.shape, BF16)],
        input_output_aliases={6: 2},
        scratch_shapes=[pltpu.VMEM((S, HEAD_DIM), F32)] * (3 + 2 * len(DILATIONS)),
        compiler_params=_params("parallel"), name=name)(proj, proj, proj, cos2, sin_signed, gain, mixed)


def _dil_bwd(proj, cos2, sin_signed, gain, o_raw, lse, dmixed, dm_col0, col0, n_heads, name):
    S = proj.shape[0]
    H, B = n_heads, KEY_BLOCK
    scale = HEAD_DIM ** -0.5
    rc = _tile(S, 256, 8)

    def body(q_ref, k_ref, v_ref, c_ref, s_ref, g_ref, o_ref, l_ref, dm_ref, dq_ref, dk_ref, dv_ref, dg_ref,
             qr, kr, vf, dos, dsum, dqr, dkr, dvv):
        dg_ref[...] = jnp.zeros_like(dg_ref)

        def prep(t, carry):
            rows = pl.ds(pl.multiple_of(t * rc, rc), rc)
            qr[rows, :] = _rope(q_ref[rows, :].astype(F32), c_ref[rows, :], s_ref[rows, :])
            kr[rows, :] = _rope(k_ref[rows, :].astype(F32), c_ref[rows, :], s_ref[rows, :])
            vf[rows, :] = v_ref[rows, :].astype(F32)
            o, dm = o_ref[rows, :], dm_ref[rows, :].astype(F32)
            r = _rms_scale(o)
            do = _rms_bwd(dm * g_ref[...], o, r)
            dg_ref[...] += jnp.broadcast_to(jnp.sum(dm * o * r, axis=0, keepdims=True), dg_ref.shape)
            dos[rows, :] = do
            dsum[rows, :] = jnp.broadcast_to(jnp.sum(do * o, axis=1, keepdims=True), (rc, LANES))
            dqr[rows, :] = jnp.zeros((rc, HEAD_DIM), F32)
            dkr[rows, :] = jnp.zeros((rc, HEAD_DIM), F32)
            dvv[rows, :] = jnp.zeros((rc, HEAD_DIM), F32)
            return carry

        lax.fori_loop(0, S // rc, prep, 0)

        def visit(blocks):
            products = []
            for b, d, r, l0, first in blocks:
                qrows = _dil_rows(d, r, l0, B)
                krows = qrows if first else _dil_rows(d, r, l0 - B, 2 * B)
                qs, ks = qr[qrows, :].astype(BF16), kr[krows, :].astype(BF16)
                do = dos[qrows, :].astype(BF16)
                s = jnp.where(_dil_mask(first), _dot(qs, ks, NT) * scale, NEG)
                dp = _dot(do, vf[krows, :].astype(BF16), NT)
                products.append((qrows, krows, qs, ks, do, s, dp))
            cotangents = []
            for qrows, krows, qs, ks, do, s, dp in products:
                p = jnp.exp(s - l_ref[qrows, :][:, 0:1])
                ds = (p * (dp - dsum[qrows, :][:, 0:1]) * scale).astype(BF16)
                cotangents.append((qrows, krows, qs, ks, do, p.astype(BF16), ds))
            for qrows, krows, qs, ks, do, p, ds in cotangents:
                dqr[qrows, :] += _dot(ds, ks, NN)
                dkr[krows, :] += _dot(ds, qs, TN)
                dvv[krows, :] += _dot(p, do, TN)

        _dil_blocks(S, visit)

        def finish(t, carry):
            rows = pl.ds(pl.multiple_of(t * rc, rc), rc)
            c, s = c_ref[rows, :], s_ref[rows, :]
            dq, dk = dqr[rows, :], dkr[rows, :]
            dq_ref[rows, :] = (dq * c + pltpu.roll(dq * s, HEAD_DIM // 2, axis=1)).astype(BF16)
            dk_ref[rows, :] = (dk * c + pltpu.roll(dk * s, HEAD_DIM // 2, axis=1)).astype(BF16)
            dv_ref[rows, :] = dvv[rows, :].astype(BF16)
            return carry

        lax.fori_loop(0, S // rc, finish, 0)

    def col(k):
        return pl.BlockSpec((S, HEAD_DIM), lambda h: (0, col0 + k * H + h))

    tab = pl.BlockSpec((S, HEAD_DIM), lambda h: (0, 0))
    out = pl.BlockSpec((S, HEAD_DIM), lambda h: (0, h))
    W = H * HEAD_DIM
    big = pltpu.VMEM((S, HEAD_DIM), F32)
    return _pcall(
        body, grid=(H,),
        in_specs=[col(0), col(1), col(2), tab, tab, pl.BlockSpec((1, HEAD_DIM), lambda h: (0, h)), out, out,
                  pl.BlockSpec((S, HEAD_DIM), lambda h: (0, dm_col0 + h))],
        out_specs=[out, out, out, pl.BlockSpec((8, HEAD_DIM), lambda h: (0, h))],
        out_shape=[jax.ShapeDtypeStruct((S, W), BF16), jax.ShapeDtypeStruct((S, W), BF16),
                   jax.ShapeDtypeStruct((S, W), BF16), jax.ShapeDtypeStruct((8, W), F32)],
        scratch_shapes=[big, big, big, big, pltpu.VMEM((S, LANES), F32), big, big, big],
        compiler_params=_params("parallel"), name=name)(proj, proj, proj, cos2, sin_signed, gain, o_raw, lse, dmixed)


GELU_C = math.sqrt(2.0 / math.pi)
GELU_A = 0.044715
HALO = 16


def _shift_down(cur, halo, k):
    out = pltpu.roll(cur, k, axis=0)
    row = lax.broadcasted_iota(jnp.int32, cur.shape, 0)
    for t in range(k):
        out = jnp.where(row == t, halo[HALO - k + t:HALO - k + t + 1, :], out)
    return out


def _shift_up(cur, halo, k):
    n = cur.shape[0]
    out = pltpu.roll(cur, n - k, axis=0)
    row = lax.broadcasted_iota(jnp.int32, cur.shape, 0)
    for t in range(k):
        out = jnp.where(row == n - k + t, halo[t:t + 1, :], out)
    return out


def _conv3(cur, halo, cw):
    return _shift_down(cur, halo, 2) * cw[0:1, :] + _shift_down(cur, halo, 1) * cw[1:2, :] + cur * cw[2:3, :] + cw[3:4, :]


def _gelu_parts(x):
    t = jnp.tanh(GELU_C * (x + GELU_A * x * x * x))
    return 0.5 * x * (1.0 + t), t


def _geglu_specs(tm, tn, ncb):
    hb = tm // HALO

    def cur(off):
        return pl.BlockSpec((tm, tn), lambda j, i: (i, off + j))

    def prev(off):
        return pl.BlockSpec((HALO, tn), lambda j, i: (jnp.maximum(i * hb - 1, 0), off + j))

    def taps(off):
        return pl.BlockSpec((8, tn), lambda j, i: (0, off + j))

    return [cur(0), prev(0), cur(ncb), prev(ncb), taps(0), taps(ncb)]


def _geglu_fwd(u, cwb, name, tm=256, tn=1408):
    S, F2 = u.shape
    F = F2 // 2
    tm, tn = _tile(S, tm, HALO), _tile(F, tn)
    ncb = F // tn

    def body(g_ref, gp_ref, v_ref, vp_ref, cg_ref, cv_ref, y_ref):
        top = pl.program_id(1) > 0
        gp = jnp.where(top, gp_ref[...].astype(F32), 0.0)
        vp = jnp.where(top, vp_ref[...].astype(F32), 0.0)
        gc = _conv3(g_ref[...].astype(F32), gp, cg_ref[...])
        vc = _conv3(v_ref[...].astype(F32), vp, cv_ref[...])
        y_ref[...] = (_gelu_parts(gc)[0] * vc).astype(BF16)

    return _pcall(body, grid=(ncb, S // tm), in_specs=_geglu_specs(tm, tn, ncb),
                  out_specs=pl.BlockSpec((tm, tn), lambda j, i: (i, j)),
                  out_shape=jax.ShapeDtypeStruct((S, F), BF16),
                  compiler_params=_params("parallel", "parallel"), name=name)(u, u, u, u, cwb, cwb)


def _geglu_bwd(u, dy, cwb, name, tm=256, tn=512):
    S, F2 = u.shape
    F = F2 // 2
    tm, tn = _tile(S, tm, HALO), _tile(F, tn)
    ncb = F // tn

    def body(g_ref, gp_ref, v_ref, vp_ref, cg_ref, cv_ref, dy_ref, dc_ref, dwg_ref, dwv_ref):
        i = pl.program_id(1)

        @pl.when(i == 0)
        def _():
            dwg_ref[...] = jnp.zeros_like(dwg_ref)
            dwv_ref[...] = jnp.zeros_like(dwv_ref)

        top = i > 0
        g, v = g_ref[...].astype(F32), v_ref[...].astype(F32)
        gp = jnp.where(top, gp_ref[...].astype(F32), 0.0)
        vp = jnp.where(top, vp_ref[...].astype(F32), 0.0)
        gc = _conv3(g, gp, cg_ref[...])
        vc = _conv3(v, vp, cv_ref[...])
        act, t = _gelu_parts(gc)
        dact = 0.5 * (1.0 + t) + 0.5 * gc * (1.0 - t * t) * GELU_C * (1.0 + 3.0 * GELU_A * gc * gc)
        dyv = dy_ref[...].astype(F32)
        dgc = dyv * vc * dact
        dvc = dyv * act
        dc_ref[0] = dgc.astype(BF16)
        dc_ref[1] = dvc.astype(BF16)

        def taps(out_ref, dc, cur, halo):
            out_ref[0:1, :] += jnp.sum(dc * _shift_down(cur, halo, 2), axis=0, keepdims=True)
            out_ref[1:2, :] += jnp.sum(dc * _shift_down(cur, halo, 1), axis=0, keepdims=True)
            out_ref[2:3, :] += jnp.sum(dc * cur, axis=0, keepdims=True)
            out_ref[3:4, :] += jnp.sum(dc, axis=0, keepdims=True)

        taps(dwg_ref, dgc, g, gp)
        taps(dwv_ref, dvc, v, vp)

    return _pcall(body, grid=(ncb, S // tm),
                  in_specs=_geglu_specs(tm, tn, ncb) + [pl.BlockSpec((tm, tn), lambda j, i: (i, j))],
                  out_specs=[pl.BlockSpec((2, tm, tn), lambda j, i: (0, i, j)),
                             pl.BlockSpec((8, tn), lambda j, i: (0, j)), pl.BlockSpec((8, tn), lambda j, i: (0, j))],
                  out_shape=[jax.ShapeDtypeStruct((2, S, F), BF16), jax.ShapeDtypeStruct((8, F), F32),
                             jax.ShapeDtypeStruct((8, F), F32)],
                  compiler_params=_params("parallel", "arbitrary"), name=name)(u, u, u, u, cwb, cwb, dy)


def _conv_bwd(dc, cwb, name, tm=512, tn=1408):
    _, S, F = dc.shape
    tm, tn = _tile(S, tm, HALO), _tile(F, tn)
    ncb, nrb = F // tn, S // tm
    hb = tm // HALO

    def body(c_ref, n_ref, w_ref, du_ref):
        cur = c_ref[...].astype(F32)
        nxt = jnp.where(pl.program_id(2) < nrb - 1, n_ref[...].astype(F32), 0.0)
        w = w_ref[...]
        du = cur * w[2:3, :] + _shift_up(cur, nxt, 1) * w[1:2, :] + _shift_up(cur, nxt, 2) * w[0:1, :]
        du_ref[...] = du.astype(BF16)

    return _pcall(body, grid=(2, ncb, nrb),
                  in_specs=[pl.BlockSpec((None, tm, tn), lambda c, j, i: (c, i, j)),
                            pl.BlockSpec((None, HALO, tn), lambda c, j, i: (c, jnp.minimum((i + 1) * hb, S // HALO - 1), j)),
                            pl.BlockSpec((8, tn), lambda c, j, i: (0, c * ncb + j))],
                  out_specs=pl.BlockSpec((tm, tn), lambda c, j, i: (i, c * ncb + j)),
                  out_shape=jax.ShapeDtypeStruct((S, 2 * F), BF16),
                  compiler_params=_params("parallel", "parallel", "parallel"), name=name)(dc, dc, cwb)


def _adam_math(w, g, m, v):
    m = ADAM_B1 * m + (1.0 - ADAM_B1) * g
    v = ADAM_B2 * v + (1.0 - ADAM_B2) * (g * g)
    m_hat = m / (1.0 - ADAM_B1 ** ADAM_STEP)
    v_hat = v / (1.0 - ADAM_B2 ** ADAM_STEP)
    return -ADAM_LR * (m_hat / (jnp.sqrt(v_hat) + ADAM_EPS) + ADAM_WD * w), m, v


def _adamw(w, parts, m, v, name, tr=256):
    R, C = w.shape
    n, _, Cp = parts.shape
    tr = _tile(R, tr, 8)

    def body(w_ref, p_ref, m_ref, v_ref, g_out, d_out, m_out, v_out):
        g = p_ref[0, :, 0:C].astype(F32)
        for k in range(1, n):
            g = g + p_ref[k, :, 0:C].astype(F32)
        d, mn, vn = _adam_math(w_ref[...], g, m_ref[...], v_ref[...])
        g_out[...] = g
        d_out[...] = d
        m_out[...] = mn
        v_out[...] = vn

    spec = pl.BlockSpec((tr, C), lambda i: (i, 0))
    shape = jax.ShapeDtypeStruct((R, C), F32)
    return _pcall(body, grid=(R // tr,), in_specs=[spec, pl.BlockSpec((n, tr, Cp), lambda i: (0, i, 0)), spec, spec],
                  out_specs=[spec] * 4, out_shape=[shape] * 4, compiler_params=_params("parallel"), name=name)(w, parts, m, v)


def _adamw_chips(w, pair, parts, chip_ids, m, v, name, tr=256):
    R, C = w.shape
    Cp = pair.shape[2]
    by_columns = C == Cp and _tile(R, tr, 16) < 64
    tr, tc = (R, _tile(C, 256)) if by_columns else (_tile(R, tr, 16), C)

    def body(ids_ref, w_ref, own_ref, p1_ref, p2_ref, p3_ref, m_ref, v_ref, g_out, d_out, m_out, v_out):
        g = own_ref[:, 0:tc].astype(F32)
        for ref in (p1_ref, p2_ref, p3_ref):
            g = g + ref[:, 0:tc].astype(F32)
        d, mn, vn = _adam_math(w_ref[...], g, m_ref[...], v_ref[...])
        g_out[...] = g
        d_out[...] = d
        m_out[...] = mn
        v_out[...] = vn

    if by_columns:
        spec = pl.BlockSpec((tr, tc), lambda j, ids: (0, j))
    else:
        spec = pl.BlockSpec((tr, tc), lambda i, ids: (i, 0))

    def chip(k):
        if by_columns:
            return pl.BlockSpec((None, tr, tc), lambda j, ids: (ids[k], 0, j))
        return pl.BlockSpec((None, tr, Cp), lambda i, ids: (ids[k], i, 0))

    shape = jax.ShapeDtypeStruct((R, C), F32)
    grid_spec = pltpu.PrefetchScalarGridSpec(
        num_scalar_prefetch=1, grid=(C // tc if by_columns else R // tr,),
        in_specs=[spec, chip(0), chip(1), chip(2), chip(3), spec, spec], out_specs=[spec] * 4)
    return _pcall(body, grid_spec=grid_spec, out_shape=[shape] * 4, compiler_params=_params("parallel"),
                  name=name)(chip_ids, w, pair, parts, parts, parts, m, v)


def _place():
    return lax.axis_index("x"), lax.axis_index("y"), lax.axis_index("c")


def _other_chips(x, y):
    return [(1 - x, y), (x, 1 - y), (1 - x, 1 - y)]


IN_HBM = pl.BlockSpec(memory_space=pltpu.HBM)
SEM = pl.BlockSpec(memory_space=pltpu.SEMAPHORE)
EFFECT = pltpu.SideEffectType.DATAFLOW_SIDE_EFFECTING
TOKEN = jax.ShapeDtypeStruct((8, LANES), F32)
TOKEN_SPEC = pl.BlockSpec(memory_space=pltpu.VMEM)


def _in_hbm(a):
    return pltpu.with_memory_space_constraint(a, pltpu.HBM)


def _landing(shape):
    return _in_hbm(lax.empty(shape.shape, shape.dtype))


def _hbm_like(a):
    return pltpu.HBM(a.shape, a.dtype)


def _gather_places():
    x, y, c = _place()
    relay_from = (c * (1 - x) + (1 - c) * x, c * y + (1 - c) * (1 - y), c)
    relay_to = (c * x + (1 - c) * (1 - x), c * (1 - y) + (1 - c) * y, c)
    return (x, y, c), (x, y, 1 - c), (1 - x, y, c), (x, 1 - y, c), (1 - x, 1 - y, c), relay_from, relay_to


def _slot_copy(slot, ref, src, dst, send_sem, recv_sem, to):
    return pltpu.make_async_remote_copy(src_ref=slot(ref, *src), dst_ref=slot(ref, *dst), send_sem=send_sem,
                                        recv_sem=recv_sem, device_id=to, device_id_type=MESH)


def _split_call(body, arrays, sems_in, sems_out, after, name, token=True):
    na, ni, no = len(arrays), len(sems_in), len(sems_out)

    def wrapped(*refs):
        body(refs[:na], refs[na:na + ni], refs[na + ni + 1:na + ni + 1 + no])
        if token:
            refs[-1][...] = jnp.zeros_like(refs[-1])

    outs = _pcall(
        wrapped, in_specs=[IN_HBM] * na + [SEM] * ni + [HBM],
        out_specs=[SEM] * no + [IN_HBM] * na + ([TOKEN_SPEC] if token else []),
        out_shape=[pltpu.SemaphoreType.DMA((n,)) for n in sems_out] + [_hbm_like(s) for s in arrays] + ([TOKEN] if token else []),
        input_output_aliases={a: no + a for a in range(na)},
        compiler_params=pltpu.CompilerParams(has_side_effects=EFFECT), name=name,
    )(*[_in_hbm(s) for s in arrays], *sems_in, after)
    return list(outs[:no]), list(outs[no:no + na]), (outs[-1] if token else None)


def _gather_start(landing, slots, after, name):
    na = len(landing)

    def body(land, _, sems):
        me, sib, xn, yn, _, _, _ = _gather_places()
        for a in range(na):
            for k, to in enumerate((sib, xn, yn)):
                _slot_copy(slots[a], land[a], me, me, sems[0].at[3 * a + k], sems[1].at[3 * a + k], to).start()

    return _split_call(body, landing, [], [3 * na, 3 * na], after, name)


def _gather_relay(gathered, sems1, slots, after, name):
    na = len(gathered)

    def body(gath, taken, given):
        me, sib, xn, yn, _, relay_from, relay_to = _gather_places()
        for a in range(na):
            for k, peer in enumerate((sib, xn, yn)):
                arrival = _slot_copy(slots[a], gath[a], me, peer, taken[0].at[3 * a + k], taken[1].at[3 * a + k], peer)
                arrival.wait_send()
                arrival.wait_recv()
        for a in range(na):
            _slot_copy(slots[a], gath[a], relay_from, relay_from, given[0].at[a], given[1].at[a], relay_to).start()
            for k, peer in enumerate((xn, yn)):
                _slot_copy(slots[a], gath[a], peer, peer, given[2].at[2 * a + k], given[3].at[2 * a + k], sib).start()

    return _split_call(body, gathered, sems1, [na, na, 2 * na, 2 * na], after, name)


def _gather_pass(gathered, relay_sems, slots, after, name):
    na = len(gathered)

    def body(gath, taken, given):
        me, sib, xn, yn, diag, relay_from, relay_to = _gather_places()
        for a in range(na):
            _slot_copy(slots[a], gath[a], relay_from, relay_from, taken[0].at[a], taken[1].at[a], relay_to).wait_send()
            _slot_copy(slots[a], gath[a], me, diag, taken[0].at[a], taken[1].at[a], relay_to).wait_recv()
        for a in range(na):
            _slot_copy(slots[a], gath[a], diag, diag, given[0].at[a], given[1].at[a], sib).start()

    return _split_call(body, gathered, relay_sems, [na, na], after, name)


def _gather_finish(gathered, pass_sems, diag_sems, slots, after, name):
    na = len(gathered)

    def body(gath, taken, _):
        (x, y, c), sib, xn, yn, diag, _, _ = _gather_places()
        for a in range(na):
            for k, peer in enumerate((xn, yn)):
                passed = _slot_copy(slots[a], gath[a], peer, (peer[0], peer[1], 1 - c), taken[0].at[2 * a + k],
                                    taken[1].at[2 * a + k], sib)
                passed.wait_send()
                passed.wait_recv()
            passed = _slot_copy(slots[a], gath[a], diag, (diag[0], diag[1], 1 - c), taken[2].at[a], taken[3].at[a], sib)
            passed.wait_send()
            passed.wait_recv()

    return _split_call(body, gathered, list(pass_sems) + list(diag_sems), [], after, name, token=False)[1]


def _pair_copy(view, src, land, send_sems, recv_sems, chip):
    x, y, c = _place()
    return pltpu.make_async_remote_copy(
        src_ref=view(src, chip, 1 - c), dst_ref=land.at[chip], send_sem=send_sems.at[chip], recv_sem=recv_sems.at[chip],
        device_id=(x, y, 1 - c), device_id_type=MESH)


def _pair_start(grad, view, block, after, name):
    def body(src, land, after_ref, send_sems, recv_sems, src_thru, land_thru, token):
        for chip in range(N_CHIP):
            _pair_copy(view, src, land, send_sems, recv_sems, chip).start()
        token[...] = jnp.zeros_like(token)

    sems = pltpu.SemaphoreType.DMA((N_CHIP,))
    land = jax.ShapeDtypeStruct((N_CHIP, *block), BF16)
    return _pcall(
        body, in_specs=[IN_HBM, IN_HBM, HBM], out_specs=[SEM, SEM, IN_HBM, IN_HBM, TOKEN_SPEC],
        out_shape=[sems, sems, _hbm_like(grad), _hbm_like(land), TOKEN], input_output_aliases={0: 2, 1: 3},
        compiler_params=pltpu.CompilerParams(has_side_effects=EFFECT), name=name,
    )(_in_hbm(grad), _landing(land), after)


def _pair_wait(grad, recv, send_sems, recv_sems, view, after, name):
    def body(src, land, send, recv_s, after_ref, src_thru, land_thru):
        for chip in range(N_CHIP):
            copy = _pair_copy(view, src, land, send, recv_s, chip)
            copy.wait_send()
            copy.wait_recv()

    return _pcall(
        body, in_specs=[IN_HBM, IN_HBM, SEM, SEM, HBM], out_specs=[IN_HBM, IN_HBM],
        out_shape=[_hbm_like(grad), _hbm_like(recv)], input_output_aliases={0: 0, 1: 1},
        compiler_params=pltpu.CompilerParams(has_side_effects=EFFECT), name=name,
    )(grad, recv, send_sems, recv_sems, after)


def _chip_start(pair, after, name):
    def body(src, land, after_ref, send_sems, recv_sems, src_thru, land_thru, token):
        x, y, c = _place()
        for j, (px, py) in enumerate(_other_chips(x, y)):
            pltpu.make_async_remote_copy(
                src_ref=src.at[2 * px + py], dst_ref=land.at[2 * x + y], send_sem=send_sems.at[j], recv_sem=recv_sems.at[j],
                device_id=(px, py, c), device_id_type=MESH).start()
        token[...] = jnp.zeros_like(token)

    sems = pltpu.SemaphoreType.DMA((3,))
    return _pcall(
        body, in_specs=[IN_HBM, IN_HBM, HBM], out_specs=[SEM, SEM, IN_HBM, IN_HBM, TOKEN_SPEC],
        out_shape=[sems, sems, _hbm_like(pair), _hbm_like(pair), TOKEN], input_output_aliases={0: 2, 1: 3},
        compiler_params=pltpu.CompilerParams(has_side_effects=EFFECT), name=name,
    )(_in_hbm(pair), _landing(pair), after)


def _chip_wait(pair, parts, send_sems, recv_sems, after, name):
    def body(src, land, send, recv, after_ref, src_thru, land_thru):
        x, y, c = _place()
        for j, (px, py) in enumerate(_other_chips(x, y)):
            copy = pltpu.make_async_remote_copy(
                src_ref=src.at[2 * px + py], dst_ref=land.at[2 * px + py], send_sem=send.at[j], recv_sem=recv.at[j],
                device_id=(px, py, c), device_id_type=MESH)
            copy.wait_send()
            copy.wait_recv()

    return _pcall(
        body, in_specs=[IN_HBM, IN_HBM, SEM, SEM, HBM], out_specs=[IN_HBM, IN_HBM],
        out_shape=[_hbm_like(pair), _hbm_like(parts)], input_output_aliases={0: 0, 1: 1},
        compiler_params=pltpu.CompilerParams(has_side_effects=EFFECT), name=name,
    )(pair, parts, send_sems, recv_sems, after)


def _pair_add(core, grad, recv, block, grad_spec, name):
    _, R, C = recv.shape
    tr = block

    def body(c_ref, g_ref, r_ref, o_ref):
        o_ref[...] = (g_ref[...].astype(F32) + r_ref[...].astype(F32)).astype(BF16)

    grid_spec = pltpu.PrefetchScalarGridSpec(
        num_scalar_prefetch=1, grid=(N_CHIP, R // tr),
        in_specs=[grad_spec, pl.BlockSpec((None, tr, C), lambda k, i, c: (k, i, 0))],
        out_specs=pl.BlockSpec((None, tr, C), lambda k, i, c: (k, i, 0)))
    return _pcall(body, grid_spec=grid_spec, out_shape=jax.ShapeDtypeStruct(recv.shape, BF16),
                  compiler_params=_params("parallel", "parallel"), name=name)(core, grad, recv)


def _small_copies(gath, send_sems, recv_sems):
    x, y, c = _place()
    peers = [(x, y, 1 - c)] + [(px, py, pc) for px, py in _other_chips(x, y) for pc in (c, 1 - c)]
    pairs = []
    for a, ref in enumerate(gath):
        mine = ref.at[4 * x + 2 * y + c]
        for k, (px, py, pc) in enumerate(peers):
            sems = dict(send_sem=send_sems.at[7 * a + k], recv_sem=recv_sems.at[7 * a + k], device_id=(px, py, pc),
                        device_id_type=MESH)
            pairs.append((pltpu.make_async_remote_copy(src_ref=mine, dst_ref=mine, **sems),
                          pltpu.make_async_remote_copy(src_ref=mine, dst_ref=ref.at[4 * px + 2 * py + pc], **sems)))
    return pairs


def _small_start(landing, after, name):
    na = len(landing)

    def body(*refs):
        for send, _ in _small_copies(refs[:na], refs[na + 1], refs[na + 2]):
            send.start()
        refs[-1][...] = jnp.zeros_like(refs[-1])

    sems = pltpu.SemaphoreType.DMA((7 * na,))
    outs = _pcall(
        body, in_specs=[IN_HBM] * na + [HBM], out_specs=[SEM, SEM] + [IN_HBM] * na + [TOKEN_SPEC],
        out_shape=[sems, sems] + [_hbm_like(s) for s in landing] + [TOKEN],
        input_output_aliases={a: 2 + a for a in range(na)},
        compiler_params=pltpu.CompilerParams(has_side_effects=EFFECT), name=name,
    )(*[_in_hbm(s) for s in landing], after)
    return outs[0], outs[1], outs[2:2 + na], outs[-1]


def _small_wait(gathered, send_sems, recv_sems, after, name):
    na = len(gathered)

    def body(*refs):
        for send, arrival in _small_copies(refs[:na], refs[na], refs[na + 1]):
            send.wait_send()
            arrival.wait_recv()

    return list(_pcall(
        body, in_specs=[IN_HBM] * na + [SEM, SEM, HBM], out_specs=[IN_HBM] * na,
        out_shape=[_hbm_like(g) for g in gathered], input_output_aliases={a: a for a in range(na)},
        compiler_params=pltpu.CompilerParams(has_side_effects=EFFECT), name=name,
    )(*gathered, send_sems, recv_sems, after))


def _small_finish(gathered, params, name):
    na, npar = len(gathered), len(params)

    def body(*refs):
        g_refs, wmv = refs[:na], refs[na:na + 3 * npar]
        o_sums, o_params = refs[na + 3 * npar:2 * na + 3 * npar], refs[2 * na + 3 * npar:]
        sums = []
        for a in range(na):
            acc = g_refs[a][0]
            for k in range(1, N_DEV):
                acc = acc + g_refs[a][k]
            o_sums[a][...] = acc
            sums.append(acc)
        for j, (a, row, _, _, _) in enumerate(params):
            g = sums[a][row:row + 1, :]
            d, mn, vn = _adam_math(wmv[3 * j][...], g, wmv[3 * j + 1][...], wmv[3 * j + 2][...])
            for out, val in zip(o_params[4 * j:4 * j + 4], (g, d, mn, vn)):
                out[...] = val

    vm = pl.BlockSpec(memory_space=pltpu.VMEM)
    flat = [t for p in params for t in p[2:]]
    out_shape = [jax.ShapeDtypeStruct(g.shape[1:], F32) for g in gathered]
    out_shape += [jax.ShapeDtypeStruct(p[2].shape, F32) for p in params for _ in range(4)]
    outs = _pcall(body, in_specs=[vm] * (na + 3 * npar), out_specs=[vm] * len(out_shape), out_shape=out_shape,
                  name=name)(*gathered, *flat)
    return outs[:na], [outs[na + 4 * j:na + 4 * j + 4] for j in range(npar)]


def _local_step(x, tgt, gains, weights):
    g_pre_mix, g_post_mix, g_pre_ffn, g_post_ffn, g_sb, g_dil = gains
    S, D = x.shape
    hs = g_sb.shape[1] // HEAD_DIM
    hd = g_dil.shape[1] // HEAD_DIM
    cos2, sin_signed = _rope_tables(S)

    h1 = _rms_fwd(x, g_pre_mix + weights.start(), "rms_in")
    w_in_g = weights.w_in(h1)
    proj = _mm_nn(h1, w_in_g, BF16, "proj", tn=768)
    o_sb, ct_sb, mixed = _sb_fwd(proj, g_sb + weights.relay_out(proj), hs, hs + hd, "sb_fwd")
    o_dl, lse_dl, mixed = _dil_fwd(proj, cos2, sin_signed, g_dil + weights.after_sb(o_sb), mixed, 3 * hs, hd, "dil_fwd")
    w_out_g = weights.w_out(o_dl)
    mix = _mm_nn(mixed, w_out_g, F32, "mix_out", tn=1024)
    x2, h2 = _mid_fwd(x, mix, g_post_mix + weights.after_mix(mix), g_pre_ffn, "mid_fwd")
    w_up_g, cwb = weights.w_up(h2)
    u = _mm_nn(h2, w_up_g, BF16, "ffn_up", b_transposed=True)
    y = _geglu_fwd(u, cwb + weights.forward_down(u), "geglu_fwd")
    w_down_g = weights.w_down(y)
    f = _mm_nn(y, w_down_g, F32, "ffn_down", tn=1024, tk=2816)

    dy, df, dg_post_ffn, loss = _loss_bwd(x2, f, tgt, g_post_ffn, "loss_bwd")
    dyv = _mm_nt(df, w_down_g, BF16, "d_y", tn=1408)
    dw_down = _mm_tn(y, df, D, BF16, "dw_down", tm=1408, tn=1024)
    dc, dcw_g, dcw_v = _geglu_bwd(u, dyv, cwb + weights.grad("w_down", dw_down), "geglu_bwd")
    du = _conv_bwd(dc, cwb + weights.grad_reduce("w_down", dc), "conv_bwd")
    dh2 = _mm_nt(du, w_up_g, BF16, "d_h2", tk=1408, b_transposed=True, per_step=2)
    dw_up = _mm_tn(du, h2, D, BF16, "dw_up", tm=1408, tn=1024)
    dx2, dmix, dg_pre_ffn, dg_post_mix = _mid_bwd(
        dy, dh2, x2, mix, g_pre_ffn + weights.grad("w_up", dw_up), g_post_mix, "mid_bwd")
    dmixed = _mm_nt(dmix, w_out_g, BF16, "d_mixed", after=jnp.reshape(weights.grad_reduce("w_up", dmix), (1, 1)))
    dw_out = _mm_tn(mixed, dmix, D, BF16, "dw_out", tn=1024)
    dq_s, dk_s, dv_s, dg_sb = _sb_bwd(proj, g_sb + weights.grad("w_out", dw_out), o_sb, ct_sb, dmixed, 0, hs, "sb_bwd")
    dq_d, dk_d, dv_d, dg_dil = _dil_bwd(proj, cos2, sin_signed, g_dil + weights.grad_reduce("w_out", dq_s), o_dl, lse_dl,
                                        dmixed, hs, 3 * hs, hd, "dil_bwd")
    dproj = jnp.concatenate([dq_s, dk_s, dv_s, dq_d, dk_d, dv_d], axis=1)
    dw_in = _mm_tn(h1, dproj, w_in_g.shape[2], BF16, "dw_in", tn=768)
    weights.grad("w_in", dw_in)
    dep = weights.grad_reduce("w_in", dproj)
    dh1 = _mm_nt(dproj, w_in_g, BF16, "d_h1", tk=768, after=jnp.reshape(dep, (1, 1)), per_step=4)
    grad_x, dg_pre_mix = _first_bwd(dx2, dh1, x, g_pre_mix, "first_bwd")
    small = (dg_pre_mix, dg_post_mix, dg_pre_ffn, dg_post_ffn, dg_sb[0:1], dg_dil[0:1], jnp.concatenate([dcw_g, dcw_v], axis=1))
    weights.small(small, loss)
    return loss, grad_x, small


def _pad_cols(a, to):
    return jnp.pad(a, ((0, 0), (0, to - a.shape[1])))


def kernel(x, pre_mix_gain, post_mix_gain, pre_ffn_gain, post_ffn_gain, w_in, sb_out_gain, dil_out_gain, w_out, w_up, conv_w, conv_b, w_down, loss_target, m_pre_mix_gain, m_post_mix_gain, m_pre_ffn_gain, m_post_ffn_gain, m_w_in, m_sb_out_gain, m_dil_out_gain, m_w_out, m_w_up, m_conv_w, m_conv_b, m_w_down, v_pre_mix_gain, v_post_mix_gain, v_pre_ffn_gain, v_post_ffn_gain, v_w_in, v_sb_out_gain, v_dil_out_gain, v_w_out, v_w_up, v_conv_w, v_conv_b, v_w_down):
    xb, tb = x[0], loss_target[0]
    S, D = xb.shape
    w_in, w_out, w_up, w_down, conv_w = w_in[0], w_out[0], w_up[0], w_down[0], conv_w[0]
    n_in, e_rows = w_in.shape[1], w_out.shape[0]
    cu, half = w_up.shape[1], w_down.shape[0]
    assert cu == 2 * half and half % 16 == 0
    cup = -(-cu // LANES) * LANES
    fp = N_CHIP * cup
    px, py, pc = _place()
    me = 4 * px + 2 * py + pc
    core = jnp.reshape(pc, (1,)).astype(jnp.int32)

    w_up_t, m_up_t, v_up_t = (jnp.swapaxes(t, 0, 1) for t in (w_up, m_w_up[0], v_w_up[0]))

    def by_dev(ref, qx, qy, qc):
        return ref.at[4 * qx + 2 * qy + qc]

    def down_slot(ref, qx, qy, qc):
        return ref.at[2 * qx + qy, pl.ds(qc * half, half)]

    def by_pair(ref, chip, k):
        return ref.at[chip, k]

    def down_pair(ref, chip, k):
        return ref.at[chip, pl.ds(k * half, half)]

    def pair_spec(tr, cols):
        return pl.BlockSpec((None, None, tr, cols), lambda k, i, c: (k, c[0], i, 0))

    tr_in, tr_up = _tile(D, 512, 16), _tile(cup, 256, 16)
    grad_plan = {
        "w_in": ((N_CHIP, 2, D, n_in), by_pair, (D, n_in), tr_in, pair_spec(tr_in, n_in)),
        "w_out": ((N_CHIP, 2, e_rows, D), by_pair, (e_rows, D), e_rows, pair_spec(e_rows, D)),
        "w_up": ((N_CHIP, 2, cup, D), by_pair, (cup, D), tr_up, pair_spec(tr_up, D)),
        "w_down": ((N_CHIP, cup, D), down_pair, (half, D), half,
                   pl.BlockSpec((None, half, D), lambda k, i, c: (k, c[0], 0))),
    }

    class Exchanges:
        def __init__(self):
            self.in_flight = {}

        def start(self):
            def own_slot(shard):
                return lax.dynamic_update_index_in_dim(lax.empty((N_DEV, *shard.shape), shard.dtype), shard, me, 0)

            self.group_slots = {"in": [by_dev], "out": [by_dev], "up": [by_dev, by_dev], "down": [down_slot]}
            self.flight = {}
            sems, gath, token = _gather_start([own_slot(w_in.astype(BF16))], [by_dev], core, "gather_in_start")
            self.flight["in"] = (sems, gath)
            zero = token[0, 0]
            self.landing = {
                "out": [own_slot((w_out + zero).astype(BF16))],
                "up": [own_slot(jnp.pad(w_up_t + zero, ((0, cup - cu), (0, 0))).astype(BF16)),
                       own_slot(jnp.pad(conv_w + zero, ((0, 8 - conv_w.shape[0]), (0, cup - cu))))],
                "down": [lax.dynamic_update_slice(jnp.zeros((N_CHIP, cup, D), BF16), (w_down + zero).astype(BF16)[None],
                                                  (2 * px + py, pc * half, 0))]}
            return zero

        def begin(self, group, after):
            sems, gath, token = _gather_start(self.landing[group], self.group_slots[group], after, "gather_%s_start" % group)
            self.flight[group] = (sems, gath)
            return token

        def relay(self, group, after):
            sems, gath = self.flight[group]
            sems, gath, token = _gather_relay(gath, sems, self.group_slots[group], after, "gather_%s_relay" % group)
            self.flight[group] = (sems, gath)
            return token

        def pass_on(self, group, after):
            sems, gath = self.flight[group]
            diag_sems, gath, token = _gather_pass(gath, sems[:2], self.group_slots[group], after, "gather_%s_pass" % group)
            self.flight[group] = (sems[2:], diag_sems, gath)
            return token

        def finish(self, group, after):
            pass_sems, diag_sems, gath = self.flight[group]
            return _gather_finish(gath, pass_sems, diag_sems, self.group_slots[group], after, "gather_%s_finish" % group)

        def w_in(self, after):
            token = self.begin("up", self.begin("out", self.relay("in", after)))
            return self.finish("in", self.pass_on("in", token))[0]

        def relay_out(self, after):
            return self.relay("out", after)[0, 0]

        def after_sb(self, after):
            return self.begin("down", self.relay("up", self.pass_on("out", after)))[0, 0]

        def w_out(self, after):
            return self.finish("out", after)[0].reshape(1, N_DEV * e_rows, D)

        def after_mix(self, after):
            return self.pass_on("up", after)[0, 0]

        def w_up(self, after):
            w_up_g, cw_g = self.finish("up", after)
            cb = _pad_cols(conv_b.reshape(N_DEV, cu), cup).reshape(1, 2 * fp)
            cw_full = jnp.transpose(cw_g[:, :3, :], (1, 0, 2)).reshape(3, 2 * fp)
            cwb = jnp.concatenate([cw_full, cb, jnp.zeros((4, 2 * fp), F32)], axis=0)
            return w_up_g, cwb

        def forward_down(self, after):
            return self.relay("down", after)[0, 0]

        def w_down(self, after):
            return self.finish("down", self.pass_on("down", after))[0].reshape(1, fp, D)

        def small(self, small, loss):
            d_pre_mix, d_post_mix, d_pre_ffn, d_post_ffn, d_sb, d_dil, d_conv = small

            def rows_of(*vectors):
                n = vectors[0].shape[1]
                row = lax.broadcasted_iota(jnp.int32, (8, n), 0)
                out = jnp.zeros((8, n), F32)
                for k, vec in enumerate(vectors):
                    out = jnp.where(row == k, vec, out)
                return out

            parts = [rows_of(d_pre_mix, d_post_mix, d_pre_ffn, d_post_ffn, jnp.broadcast_to(loss[:, :1], (1, D))),
                     rows_of(d_sb, d_dil), d_conv]
            landing = [lax.dynamic_update_index_in_dim(lax.empty((N_DEV, *p.shape), F32), p, me, 0) for p in parts]
            self.small_flight = _small_start(landing, parts[0], "small_start")

        def small_sums(self, after):
            send, recv, gath, _ = self.small_flight
            gath = _small_wait(gath, send, recv, after, "small_wait")
            params = [(0, 0, pre_mix_gain, m_pre_mix_gain, v_pre_mix_gain), (0, 1, post_mix_gain, m_post_mix_gain, v_post_mix_gain),
                      (0, 2, pre_ffn_gain, m_pre_ffn_gain, v_pre_ffn_gain), (0, 3, post_ffn_gain, m_post_ffn_gain, v_post_ffn_gain),
                      (1, 0, sb_out_gain, m_sb_out_gain, v_sb_out_gain), (1, 1, dil_out_gain, m_dil_out_gain, v_dil_out_gain)]
            (gains_sum, _, conv_sum), gain_steps = _small_finish(gath, params, "small_finish")
            return gains_sum[4, 0], conv_sum, gain_steps

        def grad(self, name, dw):
            view_shape, view, block, tr, spec = grad_plan[name]
            send, recv_sems, dw, recv, token = _pair_start(dw.reshape(view_shape), view, block, core, "pair_start_" + name)
            self.in_flight[name] = (dw, recv, send, recv_sems)
            return token[0, 0]

        def grad_reduce(self, name, after):
            _, view, _, tr, spec = grad_plan[name]
            dw, recv = _pair_wait(*self.in_flight[name], view, after, "pair_wait_" + name)
            pair = _pair_add(core, dw, recv, tr, spec, "pair_add_" + name)
            send, recv_sems, pair, parts, token = _chip_start(pair, recv, "chip_start_" + name)
            self.in_flight[name] = (pair, parts, send, recv_sems)
            self.last_token = token
            return token[0, 0]

        def grad_parts(self, name, after):
            return _chip_wait(*self.in_flight[name], after, "chip_wait_" + name)

    exchanges = Exchanges()
    gains = (pre_mix_gain, post_mix_gain, pre_ffn_gain, post_ffn_gain, sb_out_gain, dil_out_gain)
    loss, grad_x, small = _local_step(xb, tb, gains, exchanges)

    def small_adam(w, g, m, v, name):
        one = w.shape[0] == 1
        if one:
            w, g, m, v = (jnp.broadcast_to(t, (8, t.shape[1])) for t in (w, g, m, v))
        outs = _adamw(w, g[None], m, v, name)
        return [o[0:1] for o in outs] if one else outs

    chip_ids = jnp.stack([2 * px + py, 2 * (1 - px) + py, 2 * px + 1 - py, 2 * (1 - px) + 1 - py]).astype(jnp.int32)
    out_w_down = _adamw_chips(w_down, *exchanges.grad_parts("w_down", exchanges.small_flight[3]), chip_ids, m_w_down[0], v_w_down[0], "adam_w_down")
    out_up_t = _adamw_chips(w_up_t, *exchanges.grad_parts("w_up", out_w_down[1]), chip_ids, m_up_t, v_up_t, "adam_w_up")
    out_w_up = [jnp.swapaxes(o, 0, 1) for o in out_up_t]
    out_w_out = _adamw_chips(w_out, *exchanges.grad_parts("w_out", out_up_t[1]), chip_ids, m_w_out[0], v_w_out[0], "adam_w_out")
    loss_out, g_conv, gain_steps = exchanges.small_sums(out_w_out[1])
    out_pre_mix, out_post_mix, out_pre_ffn, out_post_ffn, out_sb, out_dil = gain_steps
    g_conv_b = g_conv[3].reshape(N_DEV, cup)[:, :cu].reshape(1, N_DEV * cu)
    g_conv_w = lax.dynamic_index_in_dim(g_conv[0:3].reshape(3, N_DEV, cup), me, axis=1, keepdims=False)[:, :cu]
    out_conv_b = small_adam(conv_b, g_conv_b, m_conv_b, v_conv_b, "adam_conv_b")
    cw8 = [jnp.pad(t, ((0, 5), (0, 0))) for t in (conv_w, g_conv_w, m_conv_w[0], v_conv_w[0])]
    out_conv_w = [o[0:3] for o in _adamw(cw8[0], cw8[1][None], cw8[2], cw8[3], "adam_conv_w")]
    out_w_in = _adamw_chips(w_in, *exchanges.grad_parts("w_in", out_conv_w[1]), chip_ids, m_w_in[0], v_w_in[0], "adam_w_in")

    order = [out_pre_mix, out_post_mix, out_pre_ffn, out_post_ffn, [o[None] for o in out_w_in], out_sb, out_dil,
             [o[None] for o in out_w_out], [o[None] for o in out_w_up], [o[None] for o in out_conv_w], out_conv_b,
             [o[None] for o in out_w_down]]
    outs = [loss_out, grad_x[None]]
    for k in range(4):
        outs += [o[k] for o in order]
    return tuple(outs)
```

```python
import functools
import math

import jax
import jax.numpy as jnp
from jax import lax
from jax.experimental import pallas as pl
from jax.experimental.pallas import tpu as pltpu

F32 = jnp.float32
BF16 = jnp.bfloat16
HEAD_DIM = 128
LANES = 128
KEY_BLOCK = 128
DILATIONS = (1, 4, 16)
RMS_EPS = 1e-6
ROPE_THETA = 10000.0
NEG = -1e30
ADAM_LR, ADAM_B1, ADAM_B2, ADAM_EPS, ADAM_WD, ADAM_STEP = 0.001, 0.9, 0.999, 1e-08, 0.01, 10
MESH = pl.DeviceIdType.MESH
N_DEV = 8
N_CHIP = 4
HBM = pl.BlockSpec(memory_space=pl.ANY)
VMEM_LIMIT = 56 * 1024 * 1024

_pcall = pl.pallas_call


def _tile(n, pref, mult=LANES):
    best = None
    t = mult
    while t <= min(n, pref):
        if n % t == 0:
            best = t
        t += mult
    return n if best is None else best


def _params(*sem):
    return pltpu.CompilerParams(dimension_semantics=sem, vmem_limit_bytes=VMEM_LIMIT)


def _dot(a, b, dims):
    return lax.dot_general(a, b, (dims, ((), ())), preferred_element_type=F32)


NN = ((1,), (0,))
NT = ((1,), (1,))
TN = ((0,), (0,))


def _mm_body(dims, nk, tile):
    if nk == 1:
        def single(a_ref, b_ref, o_ref):
            o_ref[...] = _dot(a_ref[...].astype(BF16), b_ref[...].astype(BF16), dims).astype(o_ref.dtype)

        return single, []

    def body(a_ref, b_ref, o_ref, acc_ref):
        k = pl.program_id(2)

        @pl.when(k == 0)
        def _():
            acc_ref[...] = jnp.zeros_like(acc_ref)

        acc_ref[...] += _dot(a_ref[...].astype(BF16), b_ref[...].astype(BF16), dims)

        @pl.when(k == nk - 1)
        def _():
            o_ref[...] = acc_ref[...].astype(o_ref.dtype)

    return body, [pltpu.VMEM(tile, F32)]


def _mm_nn(a, b3, out_dtype, name, tm=1024, tn=1408, tk=2048, b_transposed=False):
    M, K = a.shape
    C, n = b3.shape[0], b3.shape[1 if b_transposed else 2]
    tm, tk, tn = _tile(M, tm, 8), _tile(K, tk), _tile(n, tn)
    npc, nk = n // tn, K // tk
    body, scratch = _mm_body(NT if b_transposed else NN, nk, (tm, tn))
    b_spec = (pl.BlockSpec((None, tn, tk), lambda i, j, k: (j // npc, j % npc, k)) if b_transposed
              else pl.BlockSpec((None, tk, tn), lambda i, j, k: (j // npc, k, j % npc)))
    return _pcall(
        body, grid=(M // tm, C * npc, nk),
        in_specs=[pl.BlockSpec((tm, tk), lambda i, j, k: (i, k)), b_spec],
        out_specs=pl.BlockSpec((tm, tn), lambda i, j, k: (i, j)),
        out_shape=jax.ShapeDtypeStruct((M, C * n), out_dtype), scratch_shapes=scratch,
        compiler_params=_params("parallel", "parallel", "arbitrary"), name=name)(a, b3)


def _mm_nt(a, b3, out_dtype, name, tm=1024, tn=1024, tk=2048, after=None, b_transposed=False, per_step=1):
    M, _ = a.shape
    C, N, n = (b3.shape[0], b3.shape[2], b3.shape[1]) if b_transposed else b3.shape
    tm, tn, tk = _tile(M, tm, 8), _tile(N, tn), _tile(n, tk)
    dims = NN if b_transposed else NT
    extra = [] if after is None else [after]
    if per_step > 1 and tk == n and C % per_step == 0:
        nk, scratch = C // per_step, [pltpu.VMEM((tm, tn), F32)]
        b3 = b3.reshape(nk, per_step, *b3.shape[1:])
        a_spec = pl.BlockSpec((tm, per_step * n), lambda i, j, k: (i, k))
        if b_transposed:
            b_spec = pl.BlockSpec((None, per_step, n, tn), lambda i, j, k: (k, 0, 0, j))
        else:
            b_spec = pl.BlockSpec((None, per_step, tn, n), lambda i, j, k: (k, 0, j, 0))

        def body(a_ref, b_ref, *rest):
            o_ref, acc_ref = rest[len(extra):]
            k = pl.program_id(2)

            @pl.when(k == 0)
            def _():
                acc_ref[...] = jnp.zeros_like(acc_ref)

            b = b_ref[...].astype(BF16)
            b = b.reshape(per_step * n, tn) if b_transposed else jnp.concatenate([b[u] for u in range(per_step)], axis=1)
            acc_ref[...] += _dot(a_ref[...].astype(BF16), b, dims)

            @pl.when(k == nk - 1)
            def _():
                o_ref[...] = acc_ref[...].astype(o_ref.dtype)
    else:
        kpc = n // tk
        nk = C * kpc
        inner, scratch = _mm_body(dims, nk, (tm, tn))
        a_spec = pl.BlockSpec((tm, tk), lambda i, j, k: (i, k))
        b_spec = (pl.BlockSpec((None, tk, tn), lambda i, j, k: (k // kpc, k % kpc, j)) if b_transposed
                  else pl.BlockSpec((None, tn, tk), lambda i, j, k: (k // kpc, j, k % kpc)))

        def body(a_ref, b_ref, *rest):
            inner(a_ref, b_ref, *rest[len(extra):])

    return _pcall(
        body, grid=(M // tm, N // tn, nk), in_specs=[a_spec, b_spec] + [HBM] * len(extra),
        out_specs=pl.BlockSpec((tm, tn), lambda i, j, k: (i, j)),
        out_shape=jax.ShapeDtypeStruct((M, N), out_dtype), scratch_shapes=scratch,
        compiler_params=_params("parallel", "parallel", "arbitrary"), name=name)(a, b3, *extra)


def _mm_tn(x, y, n, out_dtype, name, tm=1024, tn=1408, tk=2048, after=None):
    S, P = x.shape
    C = y.shape[1] // n
    tm, tn, tk = _tile(P, tm), _tile(n, tn), _tile(S, tk, 8)
    npc, nk = n // tn, S // tk
    inner, scratch = _mm_body(TN, nk, (tm, tn))
    extra = [] if after is None else [after]

    def body(x_ref, y_ref, *rest):
        inner(x_ref, y_ref, *rest[len(extra):])

    return _pcall(
        body, grid=(P // tm, C * npc, nk),
        in_specs=[pl.BlockSpec((tk, tm), lambda i, j, k: (k, i)),
                  pl.BlockSpec((tk, tn), lambda i, j, k: (k, j))] + [HBM] * len(extra),
        out_specs=pl.BlockSpec((None, tm, tn), lambda i, j, k: (j // npc, i, j % npc)),
        out_shape=jax.ShapeDtypeStruct((C, P, n), out_dtype), scratch_shapes=scratch,
        compiler_params=_params("parallel", "parallel", "arbitrary"), name=name)(x, y, *extra)


def _rms_scale(v):
    return lax.rsqrt(jnp.mean(v * v, axis=-1, keepdims=True) + RMS_EPS)


def _rms_bwd(gy, v, r):
    return r * gy - v * (r * r * r * jnp.mean(gy * v, axis=-1, keepdims=True))


def _rows_spec(tm, d):
    return pl.BlockSpec((tm, d), lambda i: (i, 0))


def _vec_spec(d):
    return pl.BlockSpec((1, d), lambda i: (0, 0))


def _rms_fwd(x, g, name, tm=256):
    S, D = x.shape

    def body(x_ref, g_ref, h_ref):
        v = x_ref[...]
        h_ref[...] = (v * _rms_scale(v) * g_ref[...]).astype(BF16)

    return _pcall(body, grid=(S // tm,), in_specs=[_rows_spec(tm, D), _vec_spec(D)], out_specs=_rows_spec(tm, D),
                  out_shape=jax.ShapeDtypeStruct((S, D), BF16), compiler_params=_params("parallel"), name=name)(x, g)


def _mid_fwd(x, mix, g_post, g_pre, name, tm=256):
    S, D = x.shape

    def body(x_ref, m_ref, gp_ref, gn_ref, x2_ref, h_ref):
        m = m_ref[...]
        x2 = x_ref[...] + m * _rms_scale(m) * gp_ref[...]
        x2_ref[...] = x2
        h_ref[...] = (x2 * _rms_scale(x2) * gn_ref[...]).astype(BF16)

    return _pcall(body, grid=(S // tm,), in_specs=[_rows_spec(tm, D), _rows_spec(tm, D), _vec_spec(D), _vec_spec(D)],
                  out_specs=[_rows_spec(tm, D), _rows_spec(tm, D)],
                  out_shape=[jax.ShapeDtypeStruct((S, D), F32), jax.ShapeDtypeStruct((S, D), BF16)],
                  compiler_params=_params("parallel"), name=name)(x, mix, g_post, g_pre)


def _loss_bwd(x2, f, tgt, g_post, name, tm=256):
    S, D = x2.shape

    def body(x2_ref, f_ref, t_ref, g_ref, dy_ref, df_ref, dg_ref, ls_ref):
        i = pl.program_id(0)

        @pl.when(i == 0)
        def _():
            dg_ref[...] = jnp.zeros_like(dg_ref)
            ls_ref[...] = jnp.zeros_like(ls_ref)

        fv = f_ref[...]
        r = _rms_scale(fv)
        g = g_ref[...]
        err = x2_ref[...] + fv * r * g - t_ref[...]
        ls_ref[...] += jnp.broadcast_to(0.5 * jnp.sum(jnp.mean(err * err, axis=-1, keepdims=True), axis=0, keepdims=True), ls_ref.shape)
        dy = err * (1.0 / D)
        dy_ref[...] = dy
        df_ref[...] = _rms_bwd(dy * g, fv, r).astype(BF16)
        dg_ref[...] += jnp.sum(dy * fv * r, axis=0, keepdims=True)

    return _pcall(body, grid=(S // tm,),
                  in_specs=[_rows_spec(tm, D), _rows_spec(tm, D), _rows_spec(tm, D), _vec_spec(D)],
                  out_specs=[_rows_spec(tm, D), _rows_spec(tm, D), _vec_spec(D), _vec_spec(LANES)],
                  out_shape=[jax.ShapeDtypeStruct((S, D), F32), jax.ShapeDtypeStruct((S, D), BF16),
                             jax.ShapeDtypeStruct((1, D), F32), jax.ShapeDtypeStruct((1, LANES), F32)],
                  compiler_params=_params("arbitrary"), name=name)(x2, f, tgt, g_post)


def _mid_bwd(dy, dh2, x2, mix, g_pre, g_post, name, tm=256):
    S, D = dy.shape

    def body(dy_ref, dh_ref, x2_ref, m_ref, gn_ref, gp_ref, dx2_ref, dm_ref, dgn_ref, dgp_ref):
        i = pl.program_id(0)

        @pl.when(i == 0)
        def _():
            dgn_ref[...] = jnp.zeros_like(dgn_ref)
            dgp_ref[...] = jnp.zeros_like(dgp_ref)

        x2, dh = x2_ref[...], dh_ref[...].astype(F32)
        r = _rms_scale(x2)
        dx2 = dy_ref[...] + _rms_bwd(dh * gn_ref[...], x2, r)
        dgn_ref[...] += jnp.sum(dh * x2 * r, axis=0, keepdims=True)
        dx2_ref[...] = dx2
        m = m_ref[...]
        rm = _rms_scale(m)
        dm_ref[...] = _rms_bwd(dx2 * gp_ref[...], m, rm).astype(BF16)
        dgp_ref[...] += jnp.sum(dx2 * m * rm, axis=0, keepdims=True)

    return _pcall(body, grid=(S // tm,),
                  in_specs=[_rows_spec(tm, D)] * 4 + [_vec_spec(D)] * 2,
                  out_specs=[_rows_spec(tm, D), _rows_spec(tm, D), _vec_spec(D), _vec_spec(D)],
                  out_shape=[jax.ShapeDtypeStruct((S, D), F32), jax.ShapeDtypeStruct((S, D), BF16),
                             jax.ShapeDtypeStruct((1, D), F32), jax.ShapeDtypeStruct((1, D), F32)],
                  compiler_params=_params("arbitrary"), name=name)(dy, dh2, x2, mix, g_pre, g_post)


def _first_bwd(dx2, dh1, x, g_pre, name, tm=256):
    S, D = x.shape

    def body(dx2_ref, dh_ref, x_ref, g_ref, gx_ref, dg_ref):
        i = pl.program_id(0)

        @pl.when(i == 0)
        def _():
            dg_ref[...] = jnp.zeros_like(dg_ref)

        xv, dh = x_ref[...], dh_ref[...].astype(F32)
        r = _rms_scale(xv)
        gx_ref[...] = dx2_ref[...] + _rms_bwd(dh * g_ref[...], xv, r)
        dg_ref[...] += jnp.sum(dh * xv * r, axis=0, keepdims=True)

    return _pcall(body, grid=(S // tm,), in_specs=[_rows_spec(tm, D)] * 3 + [_vec_spec(D)],
                  out_specs=[_rows_spec(tm, D), _vec_spec(D)],
                  out_shape=[jax.ShapeDtypeStruct((S, D), F32), jax.ShapeDtypeStruct((1, D), F32)],
                  compiler_params=_params("arbitrary"), name=name)(dx2, dh1, x, g_pre)


def _logsig_pair(z):
    lb = jnp.minimum(z, 0.0) - jnp.log(1.0 + jnp.exp(-jnp.abs(z)))
    return lb, lb - z


SB_KEY_BLOCK = 256


def _sum_matrix(strict):
    ia = lax.broadcasted_iota(jnp.int32, (SB_KEY_BLOCK, SB_KEY_BLOCK), 0)
    ib = lax.broadcasted_iota(jnp.int32, (SB_KEY_BLOCK, SB_KEY_BLOCK), 1)
    return ((ia > ib) if strict == ">" else (ia < ib)).astype(BF16)


def _row_total(sums, v, col):
    return jnp.broadcast_to(sums[:, col:col + 1] + v[:, col:col + 1], (v.shape[0], LANES))


def _lanes(c, width):
    return jnp.tile(c, (1, width // LANES))


def _split_dot(v, u):
    hi = v.astype(BF16)
    lo = (v - hi.astype(F32)).astype(BF16)
    return _dot(hi, u, NN) + _dot(lo, u, NN)


def _head_out(o, g):
    return o * _rms_scale(o) * g


def _sb_fwd(proj, gain, n_heads, mixed_heads, name, tq=1024):
    S = proj.shape[0]
    H, tk = n_heads, SB_KEY_BLOCK
    tq = _tile(S, tq, 2 * tk)
    scale = HEAD_DIM ** -0.5

    def body(q_ref, k_ref, v_ref, g_ref, o_ref, ct_ref, mx_ref, oacc, cacc):
        i = pl.program_id(1)
        oacc[...] = jnp.zeros_like(oacc)
        cacc[...] = jnp.zeros_like(cacc)
        sums = _sum_matrix(">")

        def run(blocks):
            scored = []
            for k0, r0, diagonal in blocks:
                rows = pl.ds(r0, tq - r0)
                lb, lk = _logsig_pair(_dot(q_ref[rows, :].astype(BF16), k_ref[pl.ds(k0, tk), :].astype(BF16), NT) * scale)
                causal = None
                if diagonal:
                    causal = (lax.broadcasted_iota(jnp.int32, (tq - r0, tk), 1)
                              < lax.broadcasted_iota(jnp.int32, (tq - r0, tk), 0))
                    lk = jnp.where(causal, lk, 0.0)
                scored.append((k0, rows, causal, lb, lk))
            summed = [(k0, rows, causal, lb, lk, _split_dot(lk, sums)) for k0, rows, causal, lb, lk in scored]
            weights = []
            for k0, rows, causal, lb, lk, after in summed:
                c = cacc[rows, :]
                a = jnp.exp(lb + after + _lanes(c, tk))
                if causal is not None:
                    a = jnp.where(causal, a, 0.0)
                cacc[rows, :] = c + _row_total(after, lk, 0)
                weights.append((k0, rows, a.astype(BF16)))
            for k0, rows, a in weights:
                oacc[rows, :] += _dot(a, v_ref[pl.ds(k0, tk), :].astype(BF16), NN)

        for d in reversed(range(0, tq // tk, 2)):
            run([(pl.multiple_of(i * tq + e * tk, tk), e * tk, True) for e in (d + 1, d)])
        per_trip = tq // tk

        def step(it, carry):
            k0 = pl.multiple_of((i - 1 - it) * tq, tq)
            run([(pl.multiple_of(k0 + e * tk, tk), 0, False) for e in reversed(range(per_trip))])
            return carry

        lax.fori_loop(0, i, step, 0)
        o = oacc[...]
        o_ref[...] = o
        ct_ref[...] = cacc[...]
        mx_ref[...] = _head_out(o, g_ref[...]).astype(BF16)

    blk = pl.BlockSpec((tq, HEAD_DIM), lambda h, i: (i, h))
    return _pcall(
        body, grid=(H, S // tq),
        in_specs=[blk, pl.BlockSpec((S, HEAD_DIM), lambda h, i: (0, H + h)),
                  pl.BlockSpec((S, HEAD_DIM), lambda h, i: (0, 2 * H + h)), pl.BlockSpec((1, HEAD_DIM), lambda h, i: (0, h))],
        out_specs=[blk, blk, blk],
        out_shape=[jax.ShapeDtypeStruct((S, H * HEAD_DIM), F32), jax.ShapeDtypeStruct((S, H * HEAD_DIM), F32),
                   jax.ShapeDtypeStruct((S, mixed_heads * HEAD_DIM), BF16)],
        scratch_shapes=[pltpu.VMEM((tq, HEAD_DIM), F32), pltpu.VMEM((tq, LANES), F32)],
        compiler_params=_params("parallel", "arbitrary"), name=name)(proj, proj, proj, gain)


def _sb_bwd(proj, gain, o_raw, ctot, dmixed, dm_col0, n_heads, name, tq=1024):
    S = proj.shape[0]
    H, tk = n_heads, SB_KEY_BLOCK
    tq = _tile(S, tq, 2 * tk)
    nq = S // tq
    scale = HEAD_DIM ** -0.5

    def body(q_ref, k_ref, v_ref, g_ref, o_ref, ct_ref, dm_ref, dq_ref, dk_ref, dv_ref, dg_ref,
             dkacc, dvacc, dqacc, pfx, gcar, dos):
        i = pl.program_id(1)

        @pl.when(i == 0)
        def _():
            dkacc[...] = jnp.zeros_like(dkacc)
            dvacc[...] = jnp.zeros_like(dvacc)
            dg_ref[...] = jnp.zeros_like(dg_ref)

        o, dm, g = o_ref[...], dm_ref[...].astype(F32), g_ref[...]
        r = _rms_scale(o)
        dos[...] = _rms_bwd(dm * g, o, r).astype(BF16)
        dg_ref[...] += jnp.broadcast_to(jnp.sum(dm * o * r, axis=0, keepdims=True), dg_ref.shape)
        dqacc[...] = jnp.zeros_like(dqacc)
        pfx[...] = jnp.zeros_like(pfx)
        gcar[...] = jnp.zeros_like(gcar)
        later, earlier = _sum_matrix(">"), _sum_matrix("<")

        def run(blocks):
            scored = []
            for k0, r0, diagonal in blocks:
                rows, keys = pl.ds(r0, tq - r0), pl.ds(k0, tk)
                lb, lk = _logsig_pair(_dot(q_ref[rows, :].astype(BF16), k_ref[keys, :].astype(BF16), NT) * scale)
                da = _dot(dos[rows, :], v_ref[keys, :].astype(BF16), NT)
                causal = None
                if diagonal:
                    causal = (lax.broadcasted_iota(jnp.int32, (tq - r0, tk), 1)
                              < lax.broadcasted_iota(jnp.int32, (tq - r0, tk), 0))
                    lk = jnp.where(causal, lk, 0.0)
                scored.append((rows, keys, causal, lb, lk, da))
            summed = [(*blk, _split_dot(blk[4], later)) for blk in scored]
            weighted = []
            for rows, keys, causal, lb, lk, da, after in summed:
                p = pfx[rows, :] + _row_total(after, lk, 0)
                pfx[rows, :] = p
                a = jnp.exp(lb + after + _lanes(ct_ref[rows, :] - p, tk))
                if causal is not None:
                    a = jnp.where(causal, a, 0.0)
                dl = da * a
                weighted.append((rows, keys, causal, lb, a.astype(BF16), dl, _dot(dl.astype(BF16), earlier, NN)))
            cotangents = []
            for rows, keys, causal, lb, a, dl, before in weighted:
                gc = gcar[rows, :]
                gcar[rows, :] = gc + _row_total(before, dl, tk - 1)
                sig = jnp.exp(lb)
                gsum = (before + _lanes(gc, tk)) * sig
                if causal is not None:
                    gsum = jnp.where(causal, gsum, 0.0)
                cotangents.append((rows, keys, a, ((dl * (1.0 - sig) - gsum) * scale).astype(BF16)))
            for rows, keys, a, dz in cotangents:
                q, do = q_ref[rows, :].astype(BF16), dos[rows, :]
                dvacc[keys, :] += _dot(a, do, TN)
                dqacc[rows, :] += _dot(dz, k_ref[keys, :].astype(BF16), NN)
                dkacc[keys, :] += _dot(dz, q, TN)

        per_trip = tq // tk

        def step(j, carry):
            k0 = pl.multiple_of(j * tq, tq)
            run([(pl.multiple_of(k0 + e * tk, tk), 0, False) for e in range(per_trip)])
            return carry

        lax.fori_loop(0, i, step, 0)
        for d in range(0, tq // tk, 2):
            run([(pl.multiple_of(i * tq + e * tk, tk), e * tk, True) for e in (d, d + 1)])
        dq_ref[...] = dqacc[...].astype(BF16)

        @pl.when(i == nq - 1)
        def _():
            dk_ref[...] = dkacc[...].astype(BF16)
            dv_ref[...] = dvacc[...].astype(BF16)

    blk = pl.BlockSpec((tq, HEAD_DIM), lambda h, i: (i, h))
    full = pl.BlockSpec((S, HEAD_DIM), lambda h, i: (0, h))
    W = H * HEAD_DIM
    return _pcall(
        body, grid=(H, nq),
        in_specs=[blk, pl.BlockSpec((S, HEAD_DIM), lambda h, i: (0, H + h)),
                  pl.BlockSpec((S, HEAD_DIM), lambda h, i: (0, 2 * H + h)), pl.BlockSpec((1, HEAD_DIM), lambda h, i: (0, h)),
                  blk, blk, pl.BlockSpec((tq, HEAD_DIM), lambda h, i: (i, dm_col0 + h))],
        out_specs=[blk, full, full, pl.BlockSpec((8, HEAD_DIM), lambda h, i: (0, h))],
        out_shape=[jax.ShapeDtypeStruct((S, W), BF16), jax.ShapeDtypeStruct((S, W), BF16),
                   jax.ShapeDtypeStruct((S, W), BF16), jax.ShapeDtypeStruct((8, W), F32)],
        scratch_shapes=[pltpu.VMEM((S, HEAD_DIM), F32), pltpu.VMEM((S, HEAD_DIM), F32), pltpu.VMEM((tq, HEAD_DIM), F32),
                        pltpu.VMEM((tq, LANES), F32), pltpu.VMEM((tq, LANES), F32), pltpu.VMEM((tq, HEAD_DIM), BF16)],
        compiler_params=_params("arbitrary", "arbitrary"), name=name)(proj, proj, proj, gain, o_raw, ctot, dmixed)


def _rope_tables(S):
    inv_freq = ROPE_THETA ** (-jnp.arange(0, HEAD_DIM, 2, dtype=F32) / HEAD_DIM)
    ang = jnp.arange(S, dtype=F32)[:, None] * inv_freq[None, :]
    cos, sin = jnp.cos(ang), jnp.sin(ang)
    return jnp.concatenate([cos, cos], axis=1), jnp.concatenate([-sin, sin], axis=1)


def _rope(v, cos2, sin_signed):
    return v * cos2 + pltpu.roll(v, HEAD_DIM // 2, axis=1) * sin_signed


def _dil_rows(d, r, l0, n):
    if d == 1:
        return pl.ds(l0 if isinstance(l0, int) else pl.multiple_of(l0, KEY_BLOCK), n)
    return pl.ds(r + d * l0, n, stride=d)


def _dil_blocks(S, visit):
    B = KEY_BLOCK
    group = 16
    for b, d in enumerate(DILATIONS):
        nb = S // d // B
        if nb == 1:
            g = math.gcd(d, group)

            def trip(t, carry, b=b, d=d, g=g):
                visit([(b, d, t * g + u, 0, True) for u in range(g)])
                return carry

            lax.fori_loop(0, d // g, trip, 0)
        elif d == 1:
            visit([(b, d, 0, 0, True)])
            g = max(k for k in range(1, group + 2) if (nb - 1) % k == 0)

            def trip(t, carry, b=b, d=d, g=g):
                visit([(b, d, 0, (1 + t * g + u) * B, False) for u in range(g)])
                return carry

            lax.fori_loop(0, (nb - 1) // g, trip, 0)
        else:
            g = math.gcd(d, max(group // nb, 1))

            def trip(t, carry, b=b, d=d, nb=nb, g=g):
                visit([(b, d, t * g + u, n * B, n == 0) for u in range(g) for n in range(nb)])
                return carry

            lax.fori_loop(0, d // g, trip, 0)


def _dil_mask(first):
    B = KEY_BLOCK
    nk = B if first else 2 * B
    iq = lax.broadcasted_iota(jnp.int32, (B, nk), 0)
    ik = lax.broadcasted_iota(jnp.int32, (B, nk), 1)
    return (ik <= iq) if first else ((ik >= iq) & (ik <= iq + B))


def _dil_fwd(proj, cos2, sin_signed, gain, mixed, col0, n_heads, name):
    S = proj.shape[0]
    H, B = n_heads, KEY_BLOCK
    scale = HEAD_DIM ** -0.5
    rc = _tile(S, 256, 8)

    def body(q_ref, k_ref, v_ref, c_ref, s_ref, g_ref, mixed_in, o_ref, l_ref, mx_ref, qr, kr, vf, *per_branch):
        ob, lb = per_branch[:len(DILATIONS)], per_branch[len(DILATIONS):]

        def rope_rows(t, carry):
            rows = pl.ds(pl.multiple_of(t * rc, rc), rc)
            qr[rows, :] = _rope(q_ref[rows, :].astype(F32), c_ref[rows, :], s_ref[rows, :])
            kr[rows, :] = _rope(k_ref[rows, :].astype(F32), c_ref[rows, :], s_ref[rows, :])
            vf[rows, :] = v_ref[rows, :].astype(F32)
            return carry

        lax.fori_loop(0, S // rc, rope_rows, 0)

        def visit(blocks):
            scores = []
            for b, d, r, l0, first in blocks:
                qrows = _dil_rows(d, r, l0, B)
                krows = qrows if first else _dil_rows(d, r, l0 - B, 2 * B)
                s = _dot(qr[qrows, :].astype(BF16), kr[krows, :].astype(BF16), NT) * scale
                scores.append((b, qrows, krows, jnp.where(_dil_mask(first), s, NEG)))
            weights = []
            for b, qrows, krows, s in scores:
                m = jnp.max(s, axis=1, keepdims=True)
                p = jnp.exp(s - m)
                den = jnp.sum(p, axis=1, keepdims=True)
                lb[b][qrows, :] = jnp.broadcast_to(m + jnp.log(den), (B, LANES))
                weights.append((b, qrows, krows, p.astype(BF16), den))
            for b, qrows, krows, p, den in weights:
                ob[b][qrows, :] = _dot(p, vf[krows, :].astype(BF16), NN) / den

        _dil_blocks(S, visit)

        def combine(t, carry):
            rows = pl.ds(pl.multiple_of(t * rc, rc), rc)
            l0, l1, l2 = lb[0][rows, :], lb[1][rows, :], lb[2][rows, :]
            m = jnp.maximum(jnp.maximum(l0, l1), l2)
            w0, w1, w2 = jnp.exp(l0 - m), jnp.exp(l1 - m), jnp.exp(l2 - m)
            den = w0 + w1 + w2
            o = (w0 * ob[0][rows, :] + w1 * ob[1][rows, :] + w2 * ob[2][rows, :]) / den
            o_ref[rows, :] = o
            l_ref[rows, :] = m + jnp.log(den)
            mx_ref[rows, :] = _head_out(o, g_ref[...]).astype(BF16)
            return carry

        lax.fori_loop(0, S // rc, combine, 0)

    def col(k):
        return pl.BlockSpec((S, HEAD_DIM), lambda h: (0, col0 + k * H + h))

    tab = pl.BlockSpec((S, HEAD_DIM), lambda h: (0, 0))
    out = pl.BlockSpec((S, HEAD_DIM), lambda h: (0, h))
    W = H * HEAD_DIM
    first = mixed.shape[1] // HEAD_DIM - H
    return _pcall(
        body, grid=(H,),
        in_specs=[col(0), col(1), col(2), tab, tab, pl.BlockSpec((1, HEAD_DIM), lambda h: (0, h)), HBM],
        out_specs=[out, out, pl.BlockSpec((S, HEAD_DIM), lambda h: (0, first + h))],
        out_shape=[jax.ShapeDtypeStruct((S, W), F32), jax.ShapeDtypeStruct((S, W), F32),
                   jax.ShapeDtypeStruct(mixed.shape, BF16)],
        input_output_aliases={6: 2},
        scratch_shapes=[pltpu.VMEM((S, HEAD_DIM), F32)] * (3 + 2 * len(DILATIONS)),
        compiler_params=_params("parallel"), name=name)(proj, proj, proj, cos2, sin_signed, gain, mixed)


def _dil_bwd(proj, cos2, sin_signed, gain, o_raw, lse, dmixed, dm_col0, col0, n_heads, name):
    S = proj.shape[0]
    H, B = n_heads, KEY_BLOCK
    scale = HEAD_DIM ** -0.5
    rc = _tile(S, 256, 8)

    def body(q_ref, k_ref, v_ref, c_ref, s_ref, g_ref, o_ref, l_ref, dm_ref, dq_ref, dk_ref, dv_ref, dg_ref,
             qr, kr, vf, dos, dsum, dqr, dkr, dvv):
        dg_ref[...] = jnp.zeros_like(dg_ref)

        def prep(t, carry):
            rows = pl.ds(pl.multiple_of(t * rc, rc), rc)
            qr[rows, :] = _rope(q_ref[rows, :].astype(F32), c_ref[rows, :], s_ref[rows, :])
            kr[rows, :] = _rope(k_ref[rows, :].astype(F32), c_ref[rows, :], s_ref[rows, :])
            vf[rows, :] = v_ref[rows, :].astype(F32)
            o, dm = o_ref[rows, :], dm_ref[rows, :].astype(F32)
            r = _rms_scale(o)
            do = _rms_bwd(dm * g_ref[...], o, r)
            dg_ref[...] += jnp.broadcast_to(jnp.sum(dm * o * r, axis=0, keepdims=True), dg_ref.shape)
            dos[rows, :] = do
            dsum[rows, :] = jnp.broadcast_to(jnp.sum(do * o, axis=1, keepdims=True), (rc, LANES))
            dqr[rows, :] = jnp.zeros((rc, HEAD_DIM), F32)
            dkr[rows, :] = jnp.zeros((rc, HEAD_DIM), F32)
            dvv[rows, :] = jnp.zeros((rc, HEAD_DIM), F32)
            return carry

        lax.fori_loop(0, S // rc, prep, 0)

        def visit(blocks):
            products = []
            for b, d, r, l0, first in blocks:
                qrows = _dil_rows(d, r, l0, B)
                krows = qrows if first else _dil_rows(d, r, l0 - B, 2 * B)
                qs, ks = qr[qrows, :].astype(BF16), kr[krows, :].astype(BF16)
                do = dos[qrows, :].astype(BF16)
                s = jnp.where(_dil_mask(first), _dot(qs, ks, NT) * scale, NEG)
                dp = _dot(do, vf[krows, :].astype(BF16), NT)
                products.append((qrows, krows, qs, ks, do, s, dp))
            cotangents = []
            for qrows, krows, qs, ks, do, s, dp in products:
                p = jnp.exp(s - l_ref[qrows, :][:, 0:1])
                ds = (p * (dp - dsum[qrows, :][:, 0:1]) * scale).astype(BF16)
                cotangents.append((qrows, krows, qs, ks, do, p.astype(BF16), ds))
            for qrows, krows, qs, ks, do, p, ds in cotangents:
                dqr[qrows, :] += _dot(ds, ks, NN)
                dkr[krows, :] += _dot(ds, qs, TN)
                dvv[krows, :] += _dot(p, do, TN)

        _dil_blocks(S, visit)

        def finish(t, carry):
            rows = pl.ds(pl.multiple_of(t * rc, rc), rc)
            c, s = c_ref[rows, :], s_ref[rows, :]
            dq, dk = dqr[rows, :], dkr[rows, :]
            dq_ref[rows, :] = (dq * c + pltpu.roll(dq * s, HEAD_DIM // 2, axis=1)).astype(BF16)
            dk_ref[rows, :] = (dk * c + pltpu.roll(dk * s, HEAD_DIM // 2, axis=1)).astype(BF16)
            dv_ref[rows, :] = dvv[rows, :].astype(BF16)
            return carry

        lax.fori_loop(0, S // rc, finish, 0)

    def col(k):
        return pl.BlockSpec((S, HEAD_DIM), lambda h: (0, col0 + k * H + h))

    tab = pl.BlockSpec((S, HEAD_DIM), lambda h: (0, 0))
    out = pl.BlockSpec((S, HEAD_DIM), lambda h: (0, h))
    W = H * HEAD_DIM
    big = pltpu.VMEM((S, HEAD_DIM), F32)
    return _pcall(
        body, grid=(H,),
        in_specs=[col(0), col(1), col(2), tab, tab, pl.BlockSpec((1, HEAD_DIM), lambda h: (0, h)), out, out,
                  pl.BlockSpec((S, HEAD_DIM), lambda h: (0, dm_col0 + h))],
        out_specs=[out, out, out, pl.BlockSpec((8, HEAD_DIM), lambda h: (0, h))],
        out_shape=[jax.ShapeDtypeStruct((S, W), BF16), jax.ShapeDtypeStruct((S, W), BF16),
                   jax.ShapeDtypeStruct((S, W), BF16), jax.ShapeDtypeStruct((8, W), F32)],
        scratch_shapes=[big, big, big, big, pltpu.VMEM((S, LANES), F32), big, big, big],
        compiler_params=_params("parallel"), name=name)(proj, proj, proj, cos2, sin_signed, gain, o_raw, lse, dmixed)


GELU_C = math.sqrt(2.0 / math.pi)
GELU_A = 0.044715
HALO = 16


def _shift_down(cur, halo, k):
    out = pltpu.roll(cur, k, axis=0)
    row = lax.broadcasted_iota(jnp.int32, cur.shape, 0)
    for t in range(k):
        out = jnp.where(row == t, halo[HALO - k + t:HALO - k + t + 1, :], out)
    return out


def _shift_up(cur, halo, k):
    n = cur.shape[0]
    out = pltpu.roll(cur, n - k, axis=0)
    row = lax.broadcasted_iota(jnp.int32, cur.shape, 0)
    for t in range(k):
        out = jnp.where(row == n - k + t, halo[t:t + 1, :], out)
    return out


def _conv3(cur, halo, cw):
    return _shift_down(cur, halo, 2) * cw[0:1, :] + _shift_down(cur, halo, 1) * cw[1:2, :] + cur * cw[2:3, :] + cw[3:4, :]


def _gelu_parts(x):
    t = jnp.tanh(GELU_C * (x + GELU_A * x * x * x))
    return 0.5 * x * (1.0 + t), t


def _geglu_specs(tm, tn, ncb):
    hb = tm // HALO

    def cur(off):
        return pl.BlockSpec((tm, tn), lambda j, i: (i, off + j))

    def prev(off):
        return pl.BlockSpec((HALO, tn), lambda j, i: (jnp.maximum(i * hb - 1, 0), off + j))

    def taps(off):
        return pl.BlockSpec((8, tn), lambda j, i: (0, off + j))

    return [cur(0), prev(0), cur(ncb), prev(ncb), taps(0), taps(ncb)]


def _geglu_fwd(u, cwb, name, tm=256, tn=1408):
    S, F2 = u.shape
    F = F2 // 2
    tm, tn = _tile(S, tm, HALO), _tile(F, tn)
    ncb = F // tn

    def body(g_ref, gp_ref, v_ref, vp_ref, cg_ref, cv_ref, y_ref):
        top = pl.program_id(1) > 0
        gp = jnp.where(top, gp_ref[...].astype(F32), 0.0)
        vp = jnp.where(top, vp_ref[...].astype(F32), 0.0)
        gc = _conv3(g_ref[...].astype(F32), gp, cg_ref[...])
        vc = _conv3(v_ref[...].astype(F32), vp, cv_ref[...])
        y_ref[...] = (_gelu_parts(gc)[0] * vc).astype(BF16)

    return _pcall(body, grid=(ncb, S // tm), in_specs=_geglu_specs(tm, tn, ncb),
                  out_specs=pl.BlockSpec((tm, tn), lambda j, i: (i, j)),
                  out_shape=jax.ShapeDtypeStruct((S, F), BF16),
                  compiler_params=_params("parallel", "parallel"), name=name)(u, u, u, u, cwb, cwb)


def _geglu_bwd(u, dy, cwb, name, tm=256, tn=1408):
    S, F2 = u.shape
    F = F2 // 2
    tm, tn = _tile(S, tm, HALO), _tile(F, tn)
    ncb = F // tn

    def body(g_ref, gp_ref, v_ref, vp_ref, cg_ref, cv_ref, dy_ref, dc_ref, dwg_ref, dwv_ref):
        i = pl.program_id(1)

        @pl.when(i == 0)
        def _():
            dwg_ref[...] = jnp.zeros_like(dwg_ref)
            dwv_ref[...] = jnp.zeros_like(dwv_ref)

        top = i > 0
        g, v = g_ref[...].astype(F32), v_ref[...].astype(F32)
        gp = jnp.where(top, gp_ref[...].astype(F32), 0.0)
        vp = jnp.where(top, vp_ref[...].astype(F32), 0.0)
        gc = _conv3(g, gp, cg_ref[...])
        vc = _conv3(v, vp, cv_ref[...])
        act, t = _gelu_parts(gc)
        dact = 0.5 * (1.0 + t) + 0.5 * gc * (1.0 - t * t) * GELU_C * (1.0 + 3.0 * GELU_A * gc * gc)
        dyv = dy_ref[...].astype(F32)
        dgc = dyv * vc * dact
        dvc = dyv * act
        dc_ref[0] = dgc.astype(BF16)
        dc_ref[1] = dvc.astype(BF16)

        def taps(out_ref, dc, cur, halo):
            out_ref[0:1, :] += jnp.sum(dc * _shift_down(cur, halo, 2), axis=0, keepdims=True)
            out_ref[1:2, :] += jnp.sum(dc * _shift_down(cur, halo, 1), axis=0, keepdims=True)
            out_ref[2:3, :] += jnp.sum(dc * cur, axis=0, keepdims=True)
            out_ref[3:4, :] += jnp.sum(dc, axis=0, keepdims=True)

        taps(dwg_ref, dgc, g, gp)
        taps(dwv_ref, dvc, v, vp)

    return _pcall(body, grid=(ncb, S // tm),
                  in_specs=_geglu_specs(tm, tn, ncb) + [pl.BlockSpec((tm, tn), lambda j, i: (i, j))],
                  out_specs=[pl.BlockSpec((2, tm, tn), lambda j, i: (0, i, j)),
                             pl.BlockSpec((8, tn), lambda j, i: (0, j)), pl.BlockSpec((8, tn), lambda j, i: (0, j))],
                  out_shape=[jax.ShapeDtypeStruct((2, S, F), BF16), jax.ShapeDtypeStruct((8, F), F32),
                             jax.ShapeDtypeStruct((8, F), F32)],
                  compiler_params=_params("parallel", "arbitrary"), name=name)(u, u, u, u, cwb, cwb, dy)


def _conv_bwd(dc, cwb, name, tm=512, tn=1408):
    _, S, F = dc.shape
    tm, tn = _tile(S, tm, HALO), _tile(F, tn)
    ncb, nrb = F // tn, S // tm
    hb = tm // HALO

    def body(c_ref, n_ref, w_ref, du_ref):
        cur = c_ref[...].astype(F32)
        nxt = jnp.where(pl.program_id(2) < nrb - 1, n_ref[...].astype(F32), 0.0)
        w = w_ref[...]
        du = cur * w[2:3, :] + _shift_up(cur, nxt, 1) * w[1:2, :] + _shift_up(cur, nxt, 2) * w[0:1, :]
        du_ref[...] = du.astype(BF16)

    return _pcall(body, grid=(2, ncb, nrb),
                  in_specs=[pl.BlockSpec((None, tm, tn), lambda c, j, i: (c, i, j)),
                            pl.BlockSpec((None, HALO, tn), lambda c, j, i: (c, jnp.minimum((i + 1) * hb, S // HALO - 1), j)),
                            pl.BlockSpec((8, tn), lambda c, j, i: (0, c * ncb + j))],
                  out_specs=pl.BlockSpec((tm, tn), lambda c, j, i: (i, c * ncb + j)),
                  out_shape=jax.ShapeDtypeStruct((S, 2 * F), BF16),
                  compiler_params=_params("parallel", "parallel", "parallel"), name=name)(dc, dc, cwb)


def _adam_math(w, g, m, v):
    m = ADAM_B1 * m + (1.0 - ADAM_B1) * g
    v = ADAM_B2 * v + (1.0 - ADAM_B2) * (g * g)
    m_hat = m / (1.0 - ADAM_B1 ** ADAM_STEP)
    v_hat = v / (1.0 - ADAM_B2 ** ADAM_STEP)
    return -ADAM_LR * (m_hat / (jnp.sqrt(v_hat) + ADAM_EPS) + ADAM_WD * w), m, v


def _adamw(w, parts, m, v, name, tr=256):
    R, C = w.shape
    n, _, Cp = parts.shape
    tr = _tile(R, tr, 8)

    def body(w_ref, p_ref, m_ref, v_ref, g_out, d_out, m_out, v_out):
        g = p_ref[0, :, 0:C].astype(F32)
        for k in range(1, n):
            g = g + p_ref[k, :, 0:C].astype(F32)
        d, mn, vn = _adam_math(w_ref[...], g, m_ref[...], v_ref[...])
        g_out[...] = g
        d_out[...] = d
        m_out[...] = mn
        v_out[...] = vn

    spec = pl.BlockSpec((tr, C), lambda i: (i, 0))
    shape = jax.ShapeDtypeStruct((R, C), F32)
    return _pcall(body, grid=(R // tr,), in_specs=[spec, pl.BlockSpec((n, tr, Cp), lambda i: (0, i, 0)), spec, spec],
                  out_specs=[spec] * 4, out_shape=[shape] * 4, compiler_params=_params("parallel"), name=name)(w, parts, m, v)


def _adamw_chips(w, pair, parts, chip_ids, m, v, name, tr=256):
    R, C = w.shape
    Cp = pair.shape[2]
    by_columns = C == Cp and _tile(R, tr, 16) < 64
    tr, tc = (R, _tile(C, 256)) if by_columns else (_tile(R, tr, 16), C)

    def body(ids_ref, w_ref, own_ref, p1_ref, p2_ref, p3_ref, m_ref, v_ref, g_out, d_out, m_out, v_out):
        g = own_ref[:, 0:tc].astype(F32)
        for ref in (p1_ref, p2_ref, p3_ref):
            g = g + ref[:, 0:tc].astype(F32)
        d, mn, vn = _adam_math(w_ref[...], g, m_ref[...], v_ref[...])
        g_out[...] = g
        d_out[...] = d
        m_out[...] = mn
        v_out[...] = vn

    if by_columns:
        spec = pl.BlockSpec((tr, tc), lambda j, ids: (0, j))
    else:
        spec = pl.BlockSpec((tr, tc), lambda i, ids: (i, 0))

    def chip(k):
        if by_columns:
            return pl.BlockSpec((None, tr, tc), lambda j, ids: (ids[k], 0, j))
        return pl.BlockSpec((None, tr, Cp), lambda i, ids: (ids[k], i, 0))

    shape = jax.ShapeDtypeStruct((R, C), F32)
    grid_spec = pltpu.PrefetchScalarGridSpec(
        num_scalar_prefetch=1, grid=(C // tc if by_columns else R // tr,),
        in_specs=[spec, chip(0), chip(1), chip(2), chip(3), spec, spec], out_specs=[spec] * 4)
    return _pcall(body, grid_spec=grid_spec, out_shape=[shape] * 4, compiler_params=_params("parallel"),
                  name=name)(chip_ids, w, pair, parts, parts, parts, m, v)


def _place():
    return lax.axis_index("x"), lax.axis_index("y"), lax.axis_index("c")


def _other_chips(x, y):
    return [(1 - x, y), (x, 1 - y), (1 - x, 1 - y)]


IN_HBM = pl.BlockSpec(memory_space=pltpu.HBM)
SEM = pl.BlockSpec(memory_space=pltpu.SEMAPHORE)
EFFECT = pltpu.SideEffectType.DATAFLOW_SIDE_EFFECTING
TOKEN = jax.ShapeDtypeStruct((8, LANES), F32)
TOKEN_SPEC = pl.BlockSpec(memory_space=pltpu.VMEM)


def _in_hbm(a):
    return pltpu.with_memory_space_constraint(a, pltpu.HBM)


def _landing(shape):
    return _in_hbm(lax.empty(shape.shape, shape.dtype))


def _hbm_like(a):
    return pltpu.HBM(a.shape, a.dtype)


def _gather_places():
    x, y, c = _place()
    relay_from = (c * (1 - x) + (1 - c) * x, c * y + (1 - c) * (1 - y), c)
    relay_to = (c * x + (1 - c) * (1 - x), c * (1 - y) + (1 - c) * y, c)
    return (x, y, c), (x, y, 1 - c), (1 - x, y, c), (x, 1 - y, c), (1 - x, 1 - y, c), relay_from, relay_to


def _slot_copy(slot, ref, src, dst, send_sem, recv_sem, to):
    return pltpu.make_async_remote_copy(src_ref=slot(ref, *src), dst_ref=slot(ref, *dst), send_sem=send_sem,
                                        recv_sem=recv_sem, device_id=to, device_id_type=MESH)


def _split_call(body, arrays, sems_in, sems_out, after, name, token=True):
    na, ni, no = len(arrays), len(sems_in), len(sems_out)

    def wrapped(*refs):
        body(refs[:na], refs[na:na + ni], refs[na + ni + 1:na + ni + 1 + no])
        if token:
            refs[-1][...] = jnp.zeros_like(refs[-1])

    outs = _pcall(
        wrapped, in_specs=[IN_HBM] * na + [SEM] * ni + [HBM],
        out_specs=[SEM] * no + [IN_HBM] * na + ([TOKEN_SPEC] if token else []),
        out_shape=[pltpu.SemaphoreType.DMA((n,)) for n in sems_out] + [_hbm_like(s) for s in arrays] + ([TOKEN] if token else []),
        input_output_aliases={a: no + a for a in range(na)},
        compiler_params=pltpu.CompilerParams(has_side_effects=EFFECT), name=name,
    )(*[_in_hbm(s) for s in arrays], *sems_in, after)
    return list(outs[:no]), list(outs[no:no + na]), (outs[-1] if token else None)


def _gather_start(landing, slots, after, name):
    na = len(landing)

    def body(land, _, sems):
        me, sib, xn, yn, _, _, _ = _gather_places()
        for a in range(na):
            for k, to in enumerate((sib, xn, yn)):
                _slot_copy(slots[a], land[a], me, me, sems[0].at[3 * a + k], sems[1].at[3 * a + k], to).start()

    return _split_call(body, landing, [], [3 * na, 3 * na], after, name)


def _gather_relay(gathered, sems1, slots, after, name):
    na = len(gathered)

    def body(gath, taken, given):
        me, sib, xn, yn, _, relay_from, relay_to = _gather_places()
        for a in range(na):
            for k, peer in enumerate((sib, xn, yn)):
                arrival = _slot_copy(slots[a], gath[a], me, peer, taken[0].at[3 * a + k], taken[1].at[3 * a + k], peer)
                arrival.wait_send()
                arrival.wait_recv()
        for a in range(na):
            _slot_copy(slots[a], gath[a], relay_from, relay_from, given[0].at[a], given[1].at[a], relay_to).start()
            for k, peer in enumerate((xn, yn)):
                _slot_copy(slots[a], gath[a], peer, peer, given[2].at[2 * a + k], given[3].at[2 * a + k], sib).start()

    return _split_call(body, gathered, sems1, [na, na, 2 * na, 2 * na], after, name)


def _gather_pass(gathered, relay_sems, slots, after, name):
    na = len(gathered)

    def body(gath, taken, given):
        me, sib, xn, yn, diag, relay_from, relay_to = _gather_places()
        for a in range(na):
            _slot_copy(slots[a], gath[a], relay_from, relay_from, taken[0].at[a], taken[1].at[a], relay_to).wait_send()
            _slot_copy(slots[a], gath[a], me, diag, taken[0].at[a], taken[1].at[a], relay_to).wait_recv()
        for a in range(na):
            _slot_copy(slots[a], gath[a], diag, diag, given[0].at[a], given[1].at[a], sib).start()

    return _split_call(body, gathered, relay_sems, [na, na], after, name)


def _gather_finish(gathered, pass_sems, diag_sems, slots, after, name):
    na = len(gathered)

    def body(gath, taken, _):
        (x, y, c), sib, xn, yn, diag, _, _ = _gather_places()
        for a in range(na):
            for k, peer in enumerate((xn, yn)):
                passed = _slot_copy(slots[a], gath[a], peer, (peer[0], peer[1], 1 - c), taken[0].at[2 * a + k],
                                    taken[1].at[2 * a + k], sib)
                passed.wait_send()
                passed.wait_recv()
            passed = _slot_copy(slots[a], gath[a], diag, (diag[0], diag[1], 1 - c), taken[2].at[a], taken[3].at[a], sib)
            passed.wait_send()
            passed.wait_recv()

    return _split_call(body, gathered, list(pass_sems) + list(diag_sems), [], after, name, token=False)[1]


def _pair_copy(view, src, land, send_sems, recv_sems, chip):
    x, y, c = _place()
    return pltpu.make_async_remote_copy(
        src_ref=view(src, chip, 1 - c), dst_ref=land.at[chip], send_sem=send_sems.at[chip], recv_sem=recv_sems.at[chip],
        device_id=(x, y, 1 - c), device_id_type=MESH)


def _pair_start(grad, view, block, after, name):
    def body(src, land, after_ref, send_sems, recv_sems, src_thru, land_thru, token):
        for chip in range(N_CHIP):
            _pair_copy(view, src, land, send_sems, recv_sems, chip).start()
        token[...] = jnp.zeros_like(token)

    sems = pltpu.SemaphoreType.DMA((N_CHIP,))
    land = jax.ShapeDtypeStruct((N_CHIP, *block), BF16)
    return _pcall(
        body, in_specs=[IN_HBM, IN_HBM, HBM], out_specs=[SEM, SEM, IN_HBM, IN_HBM, TOKEN_SPEC],
        out_shape=[sems, sems, _hbm_like(grad), _hbm_like(land), TOKEN], input_output_aliases={0: 2, 1: 3},
        compiler_params=pltpu.CompilerParams(has_side_effects=EFFECT), name=name,
    )(_in_hbm(grad), _landing(land), after)


def _pair_wait(grad, recv, send_sems, recv_sems, view, after, name):
    def body(src, land, send, recv_s, after_ref, src_thru, land_thru):
        for chip in range(N_CHIP):
            copy = _pair_copy(view, src, land, send, recv_s, chip)
            copy.wait_send()
            copy.wait_recv()

    return _pcall(
        body, in_specs=[IN_HBM, IN_HBM, SEM, SEM, HBM], out_specs=[IN_HBM, IN_HBM],
        out_shape=[_hbm_like(grad), _hbm_like(recv)], input_output_aliases={0: 0, 1: 1},
        compiler_params=pltpu.CompilerParams(has_side_effects=EFFECT), name=name,
    )(grad, recv, send_sems, recv_sems, after)


def _chip_start(pair, after, name):
    def body(src, land, after_ref, send_sems, recv_sems, src_thru, land_thru, token):
        x, y, c = _place()
        for j, (px, py) in enumerate(_other_chips(x, y)):
            pltpu.make_async_remote_copy(
                src_ref=src.at[2 * px + py], dst_ref=land.at[2 * x + y], send_sem=send_sems.at[j], recv_sem=recv_sems.at[j],
                device_id=(px, py, c), device_id_type=MESH).start()
        token[...] = jnp.zeros_like(token)

    sems = pltpu.SemaphoreType.DMA((3,))
    return _pcall(
        body, in_specs=[IN_HBM, IN_HBM, HBM], out_specs=[SEM, SEM, IN_HBM, IN_HBM, TOKEN_SPEC],
        out_shape=[sems, sems, _hbm_like(pair), _hbm_like(pair), TOKEN], input_output_aliases={0: 2, 1: 3},
        compiler_params=pltpu.CompilerParams(has_side_effects=EFFECT), name=name,
    )(_in_hbm(pair), _landing(pair), after)


def _chip_wait(pair, parts, send_sems, recv_sems, after, name):
    def body(src, land, send, recv, after_ref, src_thru, land_thru):
        x, y, c = _place()
        for j, (px, py) in enumerate(_other_chips(x, y)):
            copy = pltpu.make_async_remote_copy(
                src_ref=src.at[2 * px + py], dst_ref=land.at[2 * px + py], send_sem=send.at[j], recv_sem=recv.at[j],
                device_id=(px, py, c), device_id_type=MESH)
            copy.wait_send()
            copy.wait_recv()

    return _pcall(
        body, in_specs=[IN_HBM, IN_HBM, SEM, SEM, HBM], out_specs=[IN_HBM, IN_HBM],
        out_shape=[_hbm_like(pair), _hbm_like(parts)], input_output_aliases={0: 0, 1: 1},
        compiler_params=pltpu.CompilerParams(has_side_effects=EFFECT), name=name,
    )(pair, parts, send_sems, recv_sems, after)


def _pair_add(core, grad, recv, block, grad_spec, name):
    _, R, C = recv.shape
    tr = block

    def body(c_ref, g_ref, r_ref, o_ref):
        o_ref[...] = (g_ref[...].astype(F32) + r_ref[...].astype(F32)).astype(BF16)

    grid_spec = pltpu.PrefetchScalarGridSpec(
        num_scalar_prefetch=1, grid=(N_CHIP, R // tr),
        in_specs=[grad_spec, pl.BlockSpec((None, tr, C), lambda k, i, c: (k, i, 0))],
        out_specs=pl.BlockSpec((None, tr, C), lambda k, i, c: (k, i, 0)))
    return _pcall(body, grid_spec=grid_spec, out_shape=jax.ShapeDtypeStruct(recv.shape, BF16),
                  compiler_params=_params("parallel", "parallel"), name=name)(core, grad, recv)


def _small_copies(gath, send_sems, recv_sems):
    x, y, c = _place()
    peers = [(x, y, 1 - c)] + [(px, py, pc) for px, py in _other_chips(x, y) for pc in (c, 1 - c)]
    pairs = []
    for a, ref in enumerate(gath):
        mine = ref.at[4 * x + 2 * y + c]
        for k, (px, py, pc) in enumerate(peers):
            sems = dict(send_sem=send_sems.at[7 * a + k], recv_sem=recv_sems.at[7 * a + k], device_id=(px, py, pc),
                        device_id_type=MESH)
            pairs.append((pltpu.make_async_remote_copy(src_ref=mine, dst_ref=mine, **sems),
                          pltpu.make_async_remote_copy(src_ref=mine, dst_ref=ref.at[4 * px + 2 * py + pc], **sems)))
    return pairs


def _small_start(landing, after, name):
    na = len(landing)

    def body(*refs):
        for send, _ in _small_copies(refs[:na], refs[na + 1], refs[na + 2]):
            send.start()
        refs[-1][...] = jnp.zeros_like(refs[-1])

    sems = pltpu.SemaphoreType.DMA((7 * na,))
    outs = _pcall(
        body, in_specs=[IN_HBM] * na + [HBM], out_specs=[SEM, SEM] + [IN_HBM] * na + [TOKEN_SPEC],
        out_shape=[sems, sems] + [_hbm_like(s) for s in landing] + [TOKEN],
        input_output_aliases={a: 2 + a for a in range(na)},
        compiler_params=pltpu.CompilerParams(has_side_effects=EFFECT), name=name,
    )(*[_in_hbm(s) for s in landing], after)
    return outs[0], outs[1], outs[2:2 + na], outs[-1]


def _small_wait(gathered, send_sems, recv_sems, after, name):
    na = len(gathered)

    def body(*refs):
        for send, arrival in _small_copies(refs[:na], refs[na], refs[na + 1]):
            send.wait_send()
            arrival.wait_recv()

    return list(_pcall(
        body, in_specs=[IN_HBM] * na + [SEM, SEM, HBM], out_specs=[IN_HBM] * na,
        out_shape=[_hbm_like(g) for g in gathered], input_output_aliases={a: a for a in range(na)},
        compiler_params=pltpu.CompilerParams(has_side_effects=EFFECT), name=name,
    )(*gathered, send_sems, recv_sems, after))


def _small_finish(gathered, params, name):
    na, npar = len(gathered), len(params)

    def body(*refs):
        g_refs, wmv = refs[:na], refs[na:na + 3 * npar]
        o_sums, o_params = refs[na + 3 * npar:2 * na + 3 * npar], refs[2 * na + 3 * npar:]
        sums = []
        for a in range(na):
            acc = g_refs[a][0]
            for k in range(1, N_DEV):
                acc = acc + g_refs[a][k]
            o_sums[a][...] = acc
            sums.append(acc)
        for j, (a, row, _, _, _) in enumerate(params):
            g = sums[a][row:row + 1, :]
            d, mn, vn = _adam_math(wmv[3 * j][...], g, wmv[3 * j + 1][...], wmv[3 * j + 2][...])
            for out, val in zip(o_params[4 * j:4 * j + 4], (g, d, mn, vn)):
                out[...] = val

    vm = pl.BlockSpec(memory_space=pltpu.VMEM)
    flat = [t for p in params for t in p[2:]]
    out_shape = [jax.ShapeDtypeStruct(g.shape[1:], F32) for g in gathered]
    out_shape += [jax.ShapeDtypeStruct(p[2].shape, F32) for p in params for _ in range(4)]
    outs = _pcall(body, in_specs=[vm] * (na + 3 * npar), out_specs=[vm] * len(out_shape), out_shape=out_shape,
                  name=name)(*gathered, *flat)
    return outs[:na], [outs[na + 4 * j:na + 4 * j + 4] for j in range(npar)]


def _local_step(x, tgt, gains, weights):
    g_pre_mix, g_post_mix, g_pre_ffn, g_post_ffn, g_sb, g_dil = gains
    S, D = x.shape
    hs = g_sb.shape[1] // HEAD_DIM
    hd = g_dil.shape[1] // HEAD_DIM
    cos2, sin_signed = _rope_tables(S)

    h1 = _rms_fwd(x, g_pre_mix + weights.start(), "rms_in")
    w_in_g = weights.w_in(h1)
    proj = _mm_nn(h1, w_in_g, BF16, "proj", tn=768)
    o_sb, ct_sb, mixed = _sb_fwd(proj, g_sb + weights.relay_out(proj), hs, hs + hd, "sb_fwd")
    o_dl, lse_dl, mixed = _dil_fwd(proj, cos2, sin_signed, g_dil + weights.after_sb(o_sb), mixed, 3 * hs, hd, "dil_fwd")
    w_out_g = weights.w_out(o_dl)
    mix = _mm_nn(mixed, w_out_g, F32, "mix_out", tn=1024)
    x2, h2 = _mid_fwd(x, mix, g_post_mix + weights.after_mix(mix), g_pre_ffn, "mid_fwd")
    w_up_g, cwb = weights.w_up(h2)
    u = _mm_nn(h2, w_up_g, BF16, "ffn_up", b_transposed=True)
    y = _geglu_fwd(u, cwb + weights.forward_down(u), "geglu_fwd")
    w_down_g = weights.w_down(y)
    f = _mm_nn(y, w_down_g, F32, "ffn_down", tn=1024, tk=2816)

    dy, df, dg_post_ffn, loss = _loss_bwd(x2, f, tgt, g_post_ffn, "loss_bwd")
    dyv = _mm_nt(df, w_down_g, BF16, "d_y", tn=1408)
    dw_down = _mm_tn(y, df, D, BF16, "dw_down", tm=1408, tn=1024)
    dc, dcw_g, dcw_v = _geglu_bwd(u, dyv, cwb + weights.grad("w_down", dw_down), "geglu_bwd")
    du = _conv_bwd(dc, cwb + weights.grad_reduce("w_down", dc), "conv_bwd")
    dh2 = _mm_nt(du, w_up_g, BF16, "d_h2", tk=1408, b_transposed=True, per_step=2)
    dw_up = _mm_tn(du, h2, D, BF16, "dw_up", tm=1408, tn=1024)
    dx2, dmix, dg_pre_ffn, dg_post_mix = _mid_bwd(
        dy, dh2, x2, mix, g_pre_ffn + weights.grad("w_up", dw_up), g_post_mix, "mid_bwd")
    dmixed = _mm_nt(dmix, w_out_g, BF16, "d_mixed", after=jnp.reshape(weights.grad_reduce("w_up", dmix), (1, 1)))
    dw_out = _mm_tn(mixed, dmix, D, BF16, "dw_out", tn=1024)
    dq_s, dk_s, dv_s, dg_sb = _sb_bwd(proj, g_sb + weights.grad("w_out", dw_out), o_sb, ct_sb, dmixed, 0, hs, "sb_bwd")
    dq_d, dk_d, dv_d, dg_dil = _dil_bwd(proj, cos2, sin_signed, g_dil + weights.grad_reduce("w_out", dq_s), o_dl, lse_dl,
                                        dmixed, hs, 3 * hs, hd, "dil_bwd")
    dproj = jnp.concatenate([dq_s, dk_s, dv_s, dq_d, dk_d, dv_d], axis=1)
    dw_in = _mm_tn(h1, dproj, w_in_g.shape[2], BF16, "dw_in", tn=768)
    weights.grad("w_in", dw_in)
    dep = weights.grad_reduce("w_in", dproj)
    dh1 = _mm_nt(dproj, w_in_g, BF16, "d_h1", tk=768, after=jnp.reshape(dep, (1, 1)), per_step=4)
    grad_x, dg_pre_mix = _first_bwd(dx2, dh1, x, g_pre_mix, "first_bwd")
    small = (dg_pre_mix, dg_post_mix, dg_pre_ffn, dg_post_ffn, dg_sb[0:1], dg_dil[0:1], jnp.concatenate([dcw_g, dcw_v], axis=1))
    weights.small(small, loss)
    return loss, grad_x, small


def _pad_cols(a, to):
    return jnp.pad(a, ((0, 0), (0, to - a.shape[1])))


def kernel(x, pre_mix_gain, post_mix_gain, pre_ffn_gain, post_ffn_gain, w_in, sb_out_gain, dil_out_gain, w_out, w_up, conv_w, conv_b, w_down, loss_target, m_pre_mix_gain, m_post_mix_gain, m_pre_ffn_gain, m_post_ffn_gain, m_w_in, m_sb_out_gain, m_dil_out_gain, m_w_out, m_w_up, m_conv_w, m_conv_b, m_w_down, v_pre_mix_gain, v_post_mix_gain, v_pre_ffn_gain, v_post_ffn_gain, v_w_in, v_sb_out_gain, v_dil_out_gain, v_w_out, v_w_up, v_conv_w, v_conv_b, v_w_down):
    xb, tb = x[0], loss_target[0]
    S, D = xb.shape
    w_in, w_out, w_up, w_down, conv_w = w_in[0], w_out[0], w_up[0], w_down[0], conv_w[0]
    n_in, e_rows = w_in.shape[1], w_out.shape[0]
    cu, half = w_up.shape[1], w_down.shape[0]
    assert cu == 2 * half and half % 16 == 0
    cup = -(-cu // LANES) * LANES
    fp = N_CHIP * cup
    px, py, pc = _place()
    me = 4 * px + 2 * py + pc
    core = jnp.reshape(pc, (1,)).astype(jnp.int32)

    w_up_t, m_up_t, v_up_t = (jnp.swapaxes(t, 0, 1) for t in (w_up, m_w_up[0], v_w_up[0]))

    def by_dev(ref, qx, qy, qc):
        return ref.at[4 * qx + 2 * qy + qc]

    def down_slot(ref, qx, qy, qc):
        return ref.at[2 * qx + qy, pl.ds(qc * half, half)]

    def by_pair(ref, chip, k):
        return ref.at[chip, k]

    def down_pair(ref, chip, k):
        return ref.at[chip, pl.ds(k * half, half)]

    def pair_spec(tr, cols):
        return pl.BlockSpec((None, None, tr, cols), lambda k, i, c: (k, c[0], i, 0))

    tr_in, tr_up = _tile(D, 512, 16), _tile(cup, 256, 16)
    grad_plan = {
        "w_in": ((N_CHIP, 2, D, n_in), by_pair, (D, n_in), tr_in, pair_spec(tr_in, n_in)),
        "w_out": ((N_CHIP, 2, e_rows, D), by_pair, (e_rows, D), e_rows, pair_spec(e_rows, D)),
        "w_up": ((N_CHIP, 2, cup, D), by_pair, (cup, D), tr_up, pair_spec(tr_up, D)),
        "w_down": ((N_CHIP, cup, D), down_pair, (half, D), half,
                   pl.BlockSpec((None, half, D), lambda k, i, c: (k, c[0], 0))),
    }

    class Exchanges:
        def __init__(self):
            self.in_flight = {}

        def start(self):
            def own_slot(shard):
                return lax.dynamic_update_index_in_dim(lax.empty((N_DEV, *shard.shape), shard.dtype), shard, me, 0)

            self.group_slots = {"in": [by_dev], "out": [by_dev], "up": [by_dev, by_dev], "down": [down_slot]}
            self.flight = {}
            sems, gath, token = _gather_start([own_slot(w_in.astype(BF16))], [by_dev], core, "gather_in_start")
            self.flight["in"] = (sems, gath)
            zero = token[0, 0]
            self.landing = {
                "out": [own_slot((w_out + zero).astype(BF16))],
                "up": [own_slot(jnp.pad(w_up_t + zero, ((0, cup - cu), (0, 0))).astype(BF16)),
                       own_slot(jnp.pad(conv_w + zero, ((0, 8 - conv_w.shape[0]), (0, cup - cu))))],
                "down": [lax.dynamic_update_slice(jnp.zeros((N_CHIP, cup, D), BF16), (w_down + zero).astype(BF16)[None],
                                                  (2 * px + py, pc * half, 0))]}
            return zero

        def begin(self, group, after):
            sems, gath, token = _gather_start(self.landing[group], self.group_slots[group], after, "gather_%s_start" % group)
            self.flight[group] = (sems, gath)
            return token

        def relay(self, group, after):
            sems, gath = self.flight[group]
            sems, gath, token = _gather_relay(gath, sems, self.group_slots[group], after, "gather_%s_relay" % group)
            self.flight[group] = (sems, gath)
            return token

        def pass_on(self, group, after):
            sems, gath = self.flight[group]
            diag_sems, gath, token = _gather_pass(gath, sems[:2], self.group_slots[group], after, "gather_%s_pass" % group)
            self.flight[group] = (sems[2:], diag_sems, gath)
            return token

        def finish(self, group, after):
            pass_sems, diag_sems, gath = self.flight[group]
            return _gather_finish(gath, pass_sems, diag_sems, self.group_slots[group], after, "gather_%s_finish" % group)

        def w_in(self, after):
            token = self.begin("up", self.begin("out", self.relay("in", after)))
            return self.finish("in", self.pass_on("in", token))[0]

        def relay_out(self, after):
            return self.relay("out", after)[0, 0]

        def after_sb(self, after):
            return self.begin("down", self.relay("up", self.pass_on("out", after)))[0, 0]

        def w_out(self, after):
            return self.finish("out", after)[0].reshape(1, N_DEV * e_rows, D)

        def after_mix(self, after):
            return self.pass_on("up", after)[0, 0]

        def w_up(self, after):
            w_up_g, cw_g = self.finish("up", after)
            cb = _pad_cols(conv_b.reshape(N_DEV, cu), cup).reshape(1, 2 * fp)
            cw_full = jnp.transpose(cw_g[:, :3, :], (1, 0, 2)).reshape(3, 2 * fp)
            cwb = jnp.concatenate([cw_full, cb, jnp.zeros((4, 2 * fp), F32)], axis=0)
            return w_up_g, cwb

        def forward_down(self, after):
            return self.relay("down", after)[0, 0]

        def w_down(self, after):
            return self.finish("down", self.pass_on("down", after))[0].reshape(1, fp, D)

        def small(self, small, loss):
            d_pre_mix, d_post_mix, d_pre_ffn, d_post_ffn, d_sb, d_dil, d_conv = small

            def rows_of(*vectors):
                n = vectors[0].shape[1]
                row = lax.broadcasted_iota(jnp.int32, (8, n), 0)
                out = jnp.zeros((8, n), F32)
                for k, vec in enumerate(vectors):
                    out = jnp.where(row == k, vec, out)
                return out

            parts = [rows_of(d_pre_mix, d_post_mix, d_pre_ffn, d_post_ffn, jnp.broadcast_to(loss[:, :1], (1, D))),
                     rows_of(d_sb, d_dil), d_conv]
            landing = [lax.dynamic_update_index_in_dim(lax.empty((N_DEV, *p.shape), F32), p, me, 0) for p in parts]
            self.small_flight = _small_start(landing, parts[0], "small_start")

        def small_sums(self, after):
            send, recv, gath, _ = self.small_flight
            gath = _small_wait(gath, send, recv, after, "small_wait")
            params = [(0, 0, pre_mix_gain, m_pre_mix_gain, v_pre_mix_gain), (0, 1, post_mix_gain, m_post_mix_gain, v_post_mix_gain),
                      (0, 2, pre_ffn_gain, m_pre_ffn_gain, v_pre_ffn_gain), (0, 3, post_ffn_gain, m_post_ffn_gain, v_post_ffn_gain),
                      (1, 0, sb_out_gain, m_sb_out_gain, v_sb_out_gain), (1, 1, dil_out_gain, m_dil_out_gain, v_dil_out_gain)]
            (gains_sum, _, conv_sum), gain_steps = _small_finish(gath, params, "small_finish")
            return gains_sum[4, 0], conv_sum, gain_steps

        def grad(self, name, dw):
            view_shape, view, block, tr, spec = grad_plan[name]
            send, recv_sems, dw, recv, token = _pair_start(dw.reshape(view_shape), view, block, core, "pair_start_" + name)
            self.in_flight[name] = (dw, recv, send, recv_sems)
            return token[0, 0]

        def grad_reduce(self, name, after):
            _, view, _, tr, spec = grad_plan[name]
            dw, recv = _pair_wait(*self.in_flight[name], view, after, "pair_wait_" + name)
            pair = _pair_add(core, dw, recv, tr, spec, "pair_add_" + name)
            send, recv_sems, pair, parts, token = _chip_start(pair, recv, "chip_start_" + name)
            self.in_flight[name] = (pair, parts, send, recv_sems)
            self.last_token = token
            return token[0, 0]

        def grad_parts(self, name, after):
            return _chip_wait(*self.in_flight[name], after, "chip_wait_" + name)

    exchanges = Exchanges()
    gains = (pre_mix_gain, post_mix_gain, pre_ffn_gain, post_ffn_gain, sb_out_gain, dil_out_gain)
    loss, grad_x, small = _local_step(xb, tb, gains, exchanges)

    def small_adam(w, g, m, v, name):
        one = w.shape[0] == 1
        if one:
            w, g, m, v = (jnp.broadcast_to(t, (8, t.shape[1])) for t in (w, g, m, v))
        outs = _adamw(w, g[None], m, v, name)
        return [o[0:1] for o in outs] if one else outs

    chip_ids = jnp.stack([2 * px + py, 2 * (1 - px) + py, 2 * px + 1 - py, 2 * (1 - px) + 1 - py]).astype(jnp.int32)
    out_w_down = _adamw_chips(w_down, *exchanges.grad_parts("w_down", exchanges.small_flight[3]), chip_ids, m_w_down[0], v_w_down[0], "adam_w_down")
    out_up_t = _adamw_chips(w_up_t, *exchanges.grad_parts("w_up", out_w_down[1]), chip_ids, m_up_t, v_up_t, "adam_w_up")
    out_w_up = [jnp.swapaxes(o, 0, 1) for o in out_up_t]
    out_w_out = _adamw_chips(w_out, *exchanges.grad_parts("w_out", out_up_t[1]), chip_ids, m_w_out[0], v_w_out[0], "adam_w_out")
    loss_out, g_conv, gain_steps = exchanges.small_sums(out_w_out[1])
    out_pre_mix, out_post_mix, out_pre_ffn, out_post_ffn, out_sb, out_dil = gain_steps
    g_conv_b = g_conv[3].reshape(N_DEV, cup)[:, :cu].reshape(1, N_DEV * cu)
    g_conv_w = lax.dynamic_index_in_dim(g_conv[0:3].reshape(3, N_DEV, cup), me, axis=1, keepdims=False)[:, :cu]
    out_conv_b = small_adam(conv_b, g_conv_b, m_conv_b, v_conv_b, "adam_conv_b")
    cw8 = [jnp.pad(t, ((0, 5), (0, 0))) for t in (conv_w, g_conv_w, m_conv_w[0], v_conv_w[0])]
    out_conv_w = [o[0:3] for o in _adamw(cw8[0], cw8[1][None], cw8[2], cw8[3], "adam_conv_w")]
    out_w_in = _adamw_chips(w_in, *exchanges.grad_parts("w_in", out_conv_w[1]), chip_ids, m_w_in[0], v_w_in[0], "adam_w_in")

    order = [out_pre_mix, out_post_mix, out_pre_ffn, out_post_ffn, [o[None] for o in out_w_in], out_sb, out_dil,
             [o[None] for o in out_w_out], [o[None] for o in out_w_up], [o[None] for o in out_conv_w], out_conv_b,
             [o[None] for o in out_w_down]]
    outs = [loss_out, grad_x[None]]
    for k in range(4):
        outs += [o[k] for o in order]
    return tuple(outs)
```

```python
import math

import jax
import jax.numpy as jnp
from jax import lax
from jax.experimental import pallas as pl
from jax.experimental.pallas import tpu as pltpu

F32 = jnp.float32
BF16 = jnp.bfloat16
HEAD_DIM = 128
LANES = 128
KEY_BLOCK = 128
DILATIONS = (1, 4, 16)
RMS_EPS = 1e-6
ROPE_THETA = 10000.0
NEG = -1e30
ADAM_LR, ADAM_B1, ADAM_B2, ADAM_EPS, ADAM_WD, ADAM_STEP = 0.001, 0.9, 0.999, 1e-08, 0.01, 10
MESH = pl.DeviceIdType.MESH
N_DEV = 8
N_CHIP = 4
HBM = pl.BlockSpec(memory_space=pl.ANY)
VMEM_LIMIT = 56 * 1024 * 1024

_pcall = pl.pallas_call


def _tile(n, pref, mult=LANES):
    best = None
    t = mult
    while t <= min(n, pref):
        if n % t == 0:
            best = t
        t += mult
    return n if best is None else best


def _params(*sem):
    return pltpu.CompilerParams(dimension_semantics=sem, vmem_limit_bytes=VMEM_LIMIT)


def _dot(a, b, dims):
    return lax.dot_general(a, b, (dims, ((), ())), preferred_element_type=F32)


NN = ((1,), (0,))
NT = ((1,), (1,))
TN = ((0,), (0,))


def _mm_body(dims, nk, tile):
    if nk == 1:
        def single(a_ref, b_ref, o_ref):
            o_ref[...] = _dot(a_ref[...].astype(BF16), b_ref[...].astype(BF16), dims).astype(o_ref.dtype)

        return single, []

    def body(a_ref, b_ref, o_ref, acc_ref):
        k = pl.program_id(2)

        @pl.when(k == 0)
        def _():
            acc_ref[...] = jnp.zeros_like(acc_ref)

        acc_ref[...] += _dot(a_ref[...].astype(BF16), b_ref[...].astype(BF16), dims)

        @pl.when(k == nk - 1)
        def _():
            o_ref[...] = acc_ref[...].astype(o_ref.dtype)

    return body, [pltpu.VMEM(tile, F32)]


def _mm_nn(a, b3, out_dtype, name, tm=1024, tn=1408, tk=2048, b_transposed=False):
    M, K = a.shape
    C, n = b3.shape[0], b3.shape[1 if b_transposed else 2]
    tm, tk, tn = _tile(M, tm, 8), _tile(K, tk), _tile(n, tn)
    npc, nk = n // tn, K // tk
    body, scratch = _mm_body(NT if b_transposed else NN, nk, (tm, tn))
    b_spec = (pl.BlockSpec((None, tn, tk), lambda i, j, k: (j // npc, j % npc, k)) if b_transposed
              else pl.BlockSpec((None, tk, tn), lambda i, j, k: (j // npc, k, j % npc)))
    return _pcall(
        body, grid=(M // tm, C * npc, nk),
        in_specs=[pl.BlockSpec((tm, tk), lambda i, j, k: (i, k)), b_spec],
        out_specs=pl.BlockSpec((tm, tn), lambda i, j, k: (i, j)),
        out_shape=jax.ShapeDtypeStruct((M, C * n), out_dtype), scratch_shapes=scratch,
        compiler_params=_params("parallel", "parallel", "arbitrary"), name=name)(a, b3)


def _mm_nt(a, b3, out_dtype, name, tm=1024, tn=1024, tk=2048, after=None, b_transposed=False, per_step=1):
    M, _ = a.shape
    C, N, n = (b3.shape[0], b3.shape[2], b3.shape[1]) if b_transposed else b3.shape
    tm, tn, tk = _tile(M, tm, 8), _tile(N, tn), _tile(n, tk)
    dims = NN if b_transposed else NT
    extra = [] if after is None else [after]
    if per_step > 1 and tk == n and C % per_step == 0:
        nk, scratch = C // per_step, [pltpu.VMEM((tm, tn), F32)]
        b3 = b3.reshape(nk, per_step, *b3.shape[1:])
        a_spec = pl.BlockSpec((tm, per_step * n), lambda i, j, k: (i, k))
        if b_transposed:
            b_spec = pl.BlockSpec((None, per_step, n, tn), lambda i, j, k: (k, 0, 0, j))
        else:
            b_spec = pl.BlockSpec((None, per_step, tn, n), lambda i, j, k: (k, 0, j, 0))

        def body(a_ref, b_ref, *rest):
            o_ref, acc_ref = rest[len(extra):]
            k = pl.program_id(2)

            @pl.when(k == 0)
            def _():
                acc_ref[...] = jnp.zeros_like(acc_ref)

            b = b_ref[...].astype(BF16)
            b = b.reshape(per_step * n, tn) if b_transposed else jnp.concatenate([b[u] for u in range(per_step)], axis=1)
            acc_ref[...] += _dot(a_ref[...].astype(BF16), b, dims)

            @pl.when(k == nk - 1)
            def _():
                o_ref[...] = acc_ref[...].astype(o_ref.dtype)
    else:
        kpc = n // tk
        nk = C * kpc
        inner, scratch = _mm_body(dims, nk, (tm, tn))
        a_spec = pl.BlockSpec((tm, tk), lambda i, j, k: (i, k))
        b_spec = (pl.BlockSpec((None, tk, tn), lambda i, j, k: (k // kpc, k % kpc, j)) if b_transposed
                  else pl.BlockSpec((None, tn, tk), lambda i, j, k: (k // kpc, j, k % kpc)))

        def body(a_ref, b_ref, *rest):
            inner(a_ref, b_ref, *rest[len(extra):])

    return _pcall(
        body, grid=(M // tm, N // tn, nk), in_specs=[a_spec, b_spec] + [HBM] * len(extra),
        out_specs=pl.BlockSpec((tm, tn), lambda i, j, k: (i, j)),
        out_shape=jax.ShapeDtypeStruct((M, N), out_dtype), scratch_shapes=scratch,
        compiler_params=_params("parallel", "parallel", "arbitrary"), name=name)(a, b3, *extra)


def _mm_tn(x, y, n, out_dtype, name, tm=1024, tn=1408, tk=2048, after=None):
    S, P = x.shape
    C = y.shape[1] // n
    tm, tn, tk = _tile(P, tm), _tile(n, tn), _tile(S, tk, 8)
    npc, nk = n // tn, S // tk
    inner, scratch = _mm_body(TN, nk, (tm, tn))
    extra = [] if after is None else [after]

    def body(x_ref, y_ref, *rest):
        inner(x_ref, y_ref, *rest[len(extra):])

    return _pcall(
        body, grid=(P // tm, C * npc, nk),
        in_specs=[pl.BlockSpec((tk, tm), lambda i, j, k: (k, i)),
                  pl.BlockSpec((tk, tn), lambda i, j, k: (k, j))] + [HBM] * len(extra),
        out_specs=pl.BlockSpec((None, tm, tn), lambda i, j, k: (j // npc, i, j % npc)),
        out_shape=jax.ShapeDtypeStruct((C, P, n), out_dtype), scratch_shapes=scratch,
        compiler_params=_params("parallel", "parallel", "arbitrary"), name=name)(x, y, *extra)


def _rms_scale(v):
    return lax.rsqrt(jnp.mean(v * v, axis=-1, keepdims=True) + RMS_EPS)


def _rms_bwd(gy, v, r):
    return r * gy - v * (r * r * r * jnp.mean(gy * v, axis=-1, keepdims=True))


def _rows_spec(tm, d):
    return pl.BlockSpec((tm, d), lambda i: (i, 0))


def _vec_spec(d):
    return pl.BlockSpec((1, d), lambda i: (0, 0))


def _rms_fwd(x, g, name, tm=256):
    S, D = x.shape

    def body(x_ref, g_ref, h_ref):
        v = x_ref[...]
        h_ref[...] = (v * _rms_scale(v) * g_ref[...]).astype(BF16)

    return _pcall(body, grid=(S // tm,), in_specs=[_rows_spec(tm, D), _vec_spec(D)], out_specs=_rows_spec(tm, D),
                  out_shape=jax.ShapeDtypeStruct((S, D), BF16), compiler_params=_params("parallel"), name=name)(x, g)


def _mid_fwd(x, mix, g_post, g_pre, name, tm=256):
    S, D = x.shape

    def body(x_ref, m_ref, gp_ref, gn_ref, x2_ref, h_ref):
        m = m_ref[...]
        x2 = x_ref[...] + m * _rms_scale(m) * gp_ref[...]
        x2_ref[...] = x2
        h_ref[...] = (x2 * _rms_scale(x2) * gn_ref[...]).astype(BF16)

    return _pcall(body, grid=(S // tm,), in_specs=[_rows_spec(tm, D), _rows_spec(tm, D), _vec_spec(D), _vec_spec(D)],
                  out_specs=[_rows_spec(tm, D), _rows_spec(tm, D)],
                  out_shape=[jax.ShapeDtypeStruct((S, D), F32), jax.ShapeDtypeStruct((S, D), BF16)],
                  compiler_params=_params("parallel"), name=name)(x, mix, g_post, g_pre)


def _loss_bwd(x2, f, tgt, g_post, name, tm=256):
    S, D = x2.shape

    def body(x2_ref, f_ref, t_ref, g_ref, dy_ref, df_ref, dg_ref, ls_ref):
        i = pl.program_id(0)

        @pl.when(i == 0)
        def _():
            dg_ref[...] = jnp.zeros_like(dg_ref)
            ls_ref[...] = jnp.zeros_like(ls_ref)

        fv = f_ref[...]
        r = _rms_scale(fv)
        g = g_ref[...]
        err = x2_ref[...] + fv * r * g - t_ref[...]
        ls_ref[...] += jnp.broadcast_to(0.5 * jnp.sum(jnp.mean(err * err, axis=-1, keepdims=True), axis=0, keepdims=True), ls_ref.shape)
        dy = err * (1.0 / D)
        dy_ref[...] = dy
        df_ref[...] = _rms_bwd(dy * g, fv, r).astype(BF16)
        dg_ref[...] += jnp.sum(dy * fv * r, axis=0, keepdims=True)

    return _pcall(body, grid=(S // tm,),
                  in_specs=[_rows_spec(tm, D), _rows_spec(tm, D), _rows_spec(tm, D), _vec_spec(D)],
                  out_specs=[_rows_spec(tm, D), _rows_spec(tm, D), _vec_spec(D), _vec_spec(LANES)],
                  out_shape=[jax.ShapeDtypeStruct((S, D), F32), jax.ShapeDtypeStruct((S, D), BF16),
                             jax.ShapeDtypeStruct((1, D), F32), jax.ShapeDtypeStruct((1, LANES), F32)],
                  compiler_params=_params("arbitrary"), name=name)(x2, f, tgt, g_post)


def _mid_bwd(dy, dh2, x2, mix, g_pre, g_post, name, tm=256):
    S, D = dy.shape

    def body(dy_ref, dh_ref, x2_ref, m_ref, gn_ref, gp_ref, dx2_ref, dm_ref, dgn_ref, dgp_ref):
        i = pl.program_id(0)

        @pl.when(i == 0)
        def _():
            dgn_ref[...] = jnp.zeros_like(dgn_ref)
            dgp_ref[...] = jnp.zeros_like(dgp_ref)

        x2, dh = x2_ref[...], dh_ref[...].astype(F32)
        r = _rms_scale(x2)
        dx2 = dy_ref[...] + _rms_bwd(dh * gn_ref[...], x2, r)
        dgn_ref[...] += jnp.sum(dh * x2 * r, axis=0, keepdims=True)
        dx2_ref[...] = dx2
        m = m_ref[...]
        rm = _rms_scale(m)
        dm_ref[...] = _rms_bwd(dx2 * gp_ref[...], m, rm).astype(BF16)
        dgp_ref[...] += jnp.sum(dx2 * m * rm, axis=0, keepdims=True)

    return _pcall(body, grid=(S // tm,),
                  in_specs=[_rows_spec(tm, D)] * 4 + [_vec_spec(D)] * 2,
                  out_specs=[_rows_spec(tm, D), _rows_spec(tm, D), _vec_spec(D), _vec_spec(D)],
                  out_shape=[jax.ShapeDtypeStruct((S, D), F32), jax.ShapeDtypeStruct((S, D), BF16),
                             jax.ShapeDtypeStruct((1, D), F32), jax.ShapeDtypeStruct((1, D), F32)],
                  compiler_params=_params("arbitrary"), name=name)(dy, dh2, x2, mix, g_pre, g_post)


def _first_bwd(dx2, dh1, x, g_pre, name, tm=256):
    S, D = x.shape

    def body(dx2_ref, dh_ref, x_ref, g_ref, gx_ref, dg_ref):
        i = pl.program_id(0)

        @pl.when(i == 0)
        def _():
            dg_ref[...] = jnp.zeros_like(dg_ref)

        xv, dh = x_ref[...], dh_ref[...].astype(F32)
        r = _rms_scale(xv)
        gx_ref[...] = dx2_ref[...] + _rms_bwd(dh * g_ref[...], xv, r)
        dg_ref[...] += jnp.sum(dh * xv * r, axis=0, keepdims=True)

    return _pcall(body, grid=(S // tm,), in_specs=[_rows_spec(tm, D)] * 3 + [_vec_spec(D)],
                  out_specs=[_rows_spec(tm, D), _vec_spec(D)],
                  out_shape=[jax.ShapeDtypeStruct((S, D), F32), jax.ShapeDtypeStruct((1, D), F32)],
                  compiler_params=_params("arbitrary"), name=name)(dx2, dh1, x, g_pre)


def _logsig_pair(z):
    lb = jnp.minimum(z, 0.0) - jnp.log(1.0 + jnp.exp(-jnp.abs(z)))
    return lb, lb - z


SB_KEY_BLOCK = 256


def _sum_matrix(strict):
    ia = lax.broadcasted_iota(jnp.int32, (SB_KEY_BLOCK, SB_KEY_BLOCK), 0)
    ib = lax.broadcasted_iota(jnp.int32, (SB_KEY_BLOCK, SB_KEY_BLOCK), 1)
    return ((ia > ib) if strict == ">" else (ia < ib)).astype(BF16)


def _row_total(sums, v, col):
    return jnp.broadcast_to(sums[:, col:col + 1] + v[:, col:col + 1], (v.shape[0], LANES))


def _lanes(c, width):
    return jnp.tile(c, (1, width // LANES))


def _split_dot(v, u):
    hi = v.astype(BF16)
    lo = (v - hi.astype(F32)).astype(BF16)
    return _dot(hi, u, NN) + _dot(lo, u, NN)


def _head_out(o, g):
    return o * _rms_scale(o) * g


def _sb_fwd(proj, gain, n_heads, mixed_heads, name, tq=1024):
    S = proj.shape[0]
    H, tk = n_heads, SB_KEY_BLOCK
    tq = _tile(S, tq, 2 * tk)
    scale = HEAD_DIM ** -0.5

    def body(q_ref, k_ref, v_ref, g_ref, o_ref, ct_ref, mx_ref, oacc, cacc):
        i = pl.program_id(1)
        oacc[...] = jnp.zeros_like(oacc)
        cacc[...] = jnp.zeros_like(cacc)
        sums = _sum_matrix(">")

        def run(blocks):
            scored = []
            for k0, r0, diagonal in blocks:
                rows = pl.ds(r0, tq - r0)
                lb, lk = _logsig_pair(_dot(q_ref[rows, :].astype(BF16), k_ref[pl.ds(k0, tk), :].astype(BF16), NT) * scale)
                causal = None
                if diagonal:
                    causal = (lax.broadcasted_iota(jnp.int32, (tq - r0, tk), 1)
                              < lax.broadcasted_iota(jnp.int32, (tq - r0, tk), 0))
                    lk = jnp.where(causal, lk, 0.0)
                scored.append((k0, rows, causal, lb, lk))
            summed = [(k0, rows, causal, lb, lk, _split_dot(lk, sums)) for k0, rows, causal, lb, lk in scored]
            weights = []
            for k0, rows, causal, lb, lk, after in summed:
                c = cacc[rows, :]
                a = jnp.exp(lb + after + _lanes(c, tk))
                if causal is not None:
                    a = jnp.where(causal, a, 0.0)
                cacc[rows, :] = c + _row_total(after, lk, 0)
                weights.append((k0, rows, a.astype(BF16)))
            for k0, rows, a in weights:
                oacc[rows, :] += _dot(a, v_ref[pl.ds(k0, tk), :].astype(BF16), NN)

        for d in reversed(range(0, tq // tk, 2)):
            run([(pl.multiple_of(i * tq + e * tk, tk), e * tk, True) for e in (d + 1, d)])
        per_trip = tq // tk

        def step(it, carry):
            k0 = pl.multiple_of((i - 1 - it) * tq, tq)
            run([(pl.multiple_of(k0 + e * tk, tk), 0, False) for e in reversed(range(per_trip))])
            return carry

        lax.fori_loop(0, i, step, 0)
        o = oacc[...]
        o_ref[...] = o
        ct_ref[...] = cacc[...]
        mx_ref[...] = _head_out(o, g_ref[...]).astype(BF16)

    blk = pl.BlockSpec((tq, HEAD_DIM), lambda h, i: (i, h))
    return _pcall(
        body, grid=(H, S // tq),
        in_specs=[blk, pl.BlockSpec((S, HEAD_DIM), lambda h, i: (0, H + h)),
                  pl.BlockSpec((S, HEAD_DIM), lambda h, i: (0, 2 * H + h)), pl.BlockSpec((1, HEAD_DIM), lambda h, i: (0, h))],
        out_specs=[blk, blk, blk],
        out_shape=[jax.ShapeDtypeStruct((S, H * HEAD_DIM), F32), jax.ShapeDtypeStruct((S, H * HEAD_DIM), F32),
                   jax.ShapeDtypeStruct((S, mixed_heads * HEAD_DIM), BF16)],
        scratch_shapes=[pltpu.VMEM((tq, HEAD_DIM), F32), pltpu.VMEM((tq, LANES), F32)],
        compiler_params=_params("parallel", "arbitrary"), name=name)(proj, proj, proj, gain)


def _sb_bwd(proj, gain, o_raw, ctot, dmixed, dm_col0, n_heads, name, tq=1024):
    S = proj.shape[0]
    H, tk = n_heads, SB_KEY_BLOCK
    tq = _tile(S, tq, 2 * tk)
    nq = S // tq
    scale = HEAD_DIM ** -0.5

    def body(q_ref, k_ref, v_ref, g_ref, o_ref, ct_ref, dm_ref, dq_ref, dk_ref, dv_ref, dg_ref,
             dkacc, dvacc, dqacc, pfx, gcar, dos):
        i = pl.program_id(1)

        @pl.when(i == 0)
        def _():
            dkacc[...] = jnp.zeros_like(dkacc)
            dvacc[...] = jnp.zeros_like(dvacc)
            dg_ref[...] = jnp.zeros_like(dg_ref)

        o, dm, g = o_ref[...], dm_ref[...].astype(F32), g_ref[...]
        r = _rms_scale(o)
        dos[...] = _rms_bwd(dm * g, o, r).astype(BF16)
        dg_ref[...] += jnp.broadcast_to(jnp.sum(dm * o * r, axis=0, keepdims=True), dg_ref.shape)
        dqacc[...] = jnp.zeros_like(dqacc)
        pfx[...] = jnp.zeros_like(pfx)
        gcar[...] = jnp.zeros_like(gcar)
        later, earlier = _sum_matrix(">"), _sum_matrix("<")

        def run(blocks):
            scored = []
            for k0, r0, diagonal in blocks:
                rows, keys = pl.ds(r0, tq - r0), pl.ds(k0, tk)
                lb, lk = _logsig_pair(_dot(q_ref[rows, :].astype(BF16), k_ref[keys, :].astype(BF16), NT) * scale)
                da = _dot(dos[rows, :], v_ref[keys, :].astype(BF16), NT)
                causal = None
                if diagonal:
                    causal = (lax.broadcasted_iota(jnp.int32, (tq - r0, tk), 1)
                              < lax.broadcasted_iota(jnp.int32, (tq - r0, tk), 0))
                    lk = jnp.where(causal, lk, 0.0)
                scored.append((rows, keys, causal, lb, lk, da))
            summed = [(*blk, _split_dot(blk[4], later)) for blk in scored]
            weighted = []
            for rows, keys, causal, lb, lk, da, after in summed:
                p = pfx[rows, :] + _row_total(after, lk, 0)
                pfx[rows, :] = p
                a = jnp.exp(lb + after + _lanes(ct_ref[rows, :] - p, tk))
                if causal is not None:
                    a = jnp.where(causal, a, 0.0)
                dl = da * a
                weighted.append((rows, keys, causal, lb, a.astype(BF16), dl, _dot(dl.astype(BF16), earlier, NN)))
            cotangents = []
            for rows, keys, causal, lb, a, dl, before in weighted:
                gc = gcar[rows, :]
                gcar[rows, :] = gc + _row_total(before, dl, tk - 1)
                sig = jnp.exp(lb)
                gsum = (before + _lanes(gc, tk)) * sig
                if causal is not None:
                    gsum = jnp.where(causal, gsum, 0.0)
                cotangents.append((rows, keys, a, ((dl * (1.0 - sig) - gsum) * scale).astype(BF16)))
            for rows, keys, a, dz in cotangents:
                q, do = q_ref[rows, :].astype(BF16), dos[rows, :]
                dvacc[keys, :] += _dot(a, do, TN)
                dqacc[rows, :] += _dot(dz, k_ref[keys, :].astype(BF16), NN)
                dkacc[keys, :] += _dot(dz, q, TN)

        per_trip = tq // tk

        def step(j, carry):
            k0 = pl.multiple_of(j * tq, tq)
            run([(pl.multiple_of(k0 + e * tk, tk), 0, False) for e in range(per_trip)])
            return carry

        lax.fori_loop(0, i, step, 0)
        for d in range(0, tq // tk, 2):
            run([(pl.multiple_of(i * tq + e * tk, tk), e * tk, True) for e in (d, d + 1)])
        dq_ref[...] = dqacc[...].astype(BF16)

        @pl.when(i == nq - 1)
        def _():
            dk_ref[...] = dkacc[...].astype(BF16)
            dv_ref[...] = dvacc[...].astype(BF16)

    blk = pl.BlockSpec((tq, HEAD_DIM), lambda h, i: (i, h))
    full = pl.BlockSpec((S, HEAD_DIM), lambda h, i: (0, h))
    W = H * HEAD_DIM
    return _pcall(
        body, grid=(H, nq),
        in_specs=[blk, pl.BlockSpec((S, HEAD_DIM), lambda h, i: (0, H + h)),
                  pl.BlockSpec((S, HEAD_DIM), lambda h, i: (0, 2 * H + h)), pl.BlockSpec((1, HEAD_DIM), lambda h, i: (0, h)),
                  blk, blk, pl.BlockSpec((tq, HEAD_DIM), lambda h, i: (i, dm_col0 + h))],
        out_specs=[blk, full, full, pl.BlockSpec((8, HEAD_DIM), lambda h, i: (0, h))],
        out_shape=[jax.ShapeDtypeStruct((S, W), BF16), jax.ShapeDtypeStruct((S, W), BF16),
                   jax.ShapeDtypeStruct((S, W), BF16), jax.ShapeDtypeStruct((8, W), F32)],
        scratch_shapes=[pltpu.VMEM((S, HEAD_DIM), F32), pltpu.VMEM((S, HEAD_DIM), F32), pltpu.VMEM((tq, HEAD_DIM), F32),
                        pltpu.VMEM((tq, LANES), F32), pltpu.VMEM((tq, LANES), F32), pltpu.VMEM((tq, HEAD_DIM), BF16)],
        compiler_params=_params("arbitrary", "arbitrary"), name=name)(proj, proj, proj, gain, o_raw, ctot, dmixed)


def _rope_tables(S):
    inv_freq = ROPE_THETA ** (-jnp.arange(0, HEAD_DIM, 2, dtype=F32) / HEAD_DIM)
    ang = jnp.arange(S, dtype=F32)[:, None] * inv_freq[None, :]
    cos, sin = jnp.cos(ang), jnp.sin(ang)
    return jnp.concatenate([cos, cos], axis=1), jnp.concatenate([-sin, sin], axis=1)


def _rope(v, cos2, sin_signed):
    return v * cos2 + pltpu.roll(v, HEAD_DIM // 2, axis=1) * sin_signed


def _dil_rows(d, r, l0, n):
    if d == 1:
        return pl.ds(l0 if isinstance(l0, int) else pl.multiple_of(l0, KEY_BLOCK), n)
    return pl.ds(r + d * l0, n, stride=d)


def _dil_blocks(S, visit):
    B = KEY_BLOCK
    group = 16
    for b, d in enumerate(DILATIONS):
        nb = S // d // B
        if nb == 1:
            g = math.gcd(d, group)

            def trip(t, carry, b=b, d=d, g=g):
                visit([(b, d, t * g + u, 0, True) for u in range(g)])
                return carry

            lax.fori_loop(0, d // g, trip, 0)
        elif d == 1:
            visit([(b, d, 0, 0, True)])
            g = max(k for k in range(1, group + 2) if (nb - 1) % k == 0)

            def trip(t, carry, b=b, d=d, g=g):
                visit([(b, d, 0, (1 + t * g + u) * B, False) for u in range(g)])
                return carry

            lax.fori_loop(0, (nb - 1) // g, trip, 0)
        else:
            g = math.gcd(d, max(group // nb, 1))

            def trip(t, carry, b=b, d=d, nb=nb, g=g):
                visit([(b, d, t * g + u, n * B, n == 0) for u in range(g) for n in range(nb)])
                return carry

            lax.fori_loop(0, d // g, trip, 0)


def _dil_mask(first):
    B = KEY_BLOCK
    nk = B if first else 2 * B
    iq = lax.broadcasted_iota(jnp.int32, (B, nk), 0)
    ik = lax.broadcasted_iota(jnp.int32, (B, nk), 1)
    return (ik <= iq) if first else ((ik >= iq) & (ik <= iq + B))


def _dil_fwd(proj, cos2, sin_signed, gain, mixed, col0, n_heads, name):
    S = proj.shape[0]
    H, B = n_heads, KEY_BLOCK
    scale = HEAD_DIM ** -0.5
    rc = _tile(S, 256, 8)

    def body(q_ref, k_ref, v_ref, c_ref, s_ref, g_ref, mixed_in, o_ref, l_ref, mx_ref, qr, kr, vf, *per_branch):
        ob, lb = per_branch[:len(DILATIONS)], per_branch[len(DILATIONS):]

        def rope_rows(t, carry):
            rows = pl.ds(pl.multiple_of(t * rc, rc), rc)
            qr[rows, :] = _rope(q_ref[rows, :].astype(F32), c_ref[rows, :], s_ref[rows, :])
            kr[rows, :] = _rope(k_ref[rows, :].astype(F32), c_ref[rows, :], s_ref[rows, :])
            vf[rows, :] = v_ref[rows, :].astype(F32)
            return carry

        lax.fori_loop(0, S // rc, rope_rows, 0)

        def visit(blocks):
            scores = []
            for b, d, r, l0, first in blocks:
                qrows = _dil_rows(d, r, l0, B)
                krows = qrows if first else _dil_rows(d, r, l0 - B, 2 * B)
                s = _dot(qr[qrows, :].astype(BF16), kr[krows, :].astype(BF16), NT) * scale
                scores.append((b, qrows, krows, jnp.where(_dil_mask(first), s, NEG)))
            weights = []
            for b, qrows, krows, s in scores:
                m = jnp.max(s, axis=1, keepdims=True)
                p = jnp.exp(s - m)
                den = jnp.sum(p, axis=1, keepdims=True)
                lb[b][qrows, :] = jnp.broadcast_to(m + jnp.log(den), (B, LANES))
                weights.append((b, qrows, krows, p.astype(BF16), den))
            for b, qrows, krows, p, den in weights:
                ob[b][qrows, :] = _dot(p, vf[krows, :].astype(BF16), NN) / den

        _dil_blocks(S, visit)

        def combine(t, carry):
            rows = pl.ds(pl.multiple_of(t * rc, rc), rc)
            l0, l1, l2 = lb[0][rows, :], lb[1][rows, :], lb[2][rows, :]
            m = jnp.maximum(jnp.maximum(l0, l1), l2)
            w0, w1, w2 = jnp.exp(l0 - m), jnp.exp(l1 - m), jnp.exp(l2 - m)
            den = w0 + w1 + w2
            o = (w0 * ob[0][rows, :] + w1 * ob[1][rows, :] + w2 * ob[2][rows, :]) / den
            o_ref[rows, :] = o
            l_ref[rows, :] = m + jnp.log(den)
            mx_ref[rows, :] = _head_out(o, g_ref[...]).astype(BF16)
            return carry

        lax.fori_loop(0, S // rc, combine, 0)

    def col(k):
        return pl.BlockSpec((S, HEAD_DIM), lambda h: (0, col0 + k * H + h))

    tab = pl.BlockSpec((S, HEAD_DIM), lambda h: (0, 0))
    out = pl.BlockSpec((S, HEAD_DIM), lambda h: (0, h))
    W = H * HEAD_DIM
    first = mixed.shape[1] // HEAD_DIM - H
    return _pcall(
        body, grid=(H,),
        in_specs=[col(0), col(1), col(2), tab, tab, pl.BlockSpec((1, HEAD_DIM), lambda h: (0, h)), HBM],
        out_specs=[out, out, pl.BlockSpec((S, HEAD_DIM), lambda h: (0, first + h))],
        out_shape=[jax.ShapeDtypeStruct((S, W), F32), jax.ShapeDtypeStruct((S, W), F32),
                   jax.ShapeDtypeStruct(mixed.shape, BF16)],
        input_output_aliases={6: 2},
        scratch_shapes=[pltpu.VMEM((S, HEAD_DIM), F32)] * (3 + 2 * len(DILATIONS)),
        compiler_params=_params("parallel"), name=name)(proj, proj, proj, cos2, sin_signed, gain, mixed)


def _dil_bwd(proj, cos2, sin_signed, gain, o_raw, lse, dmixed, dm_col0, col0, n_heads, name):
    S = proj.shape[0]
    H, B = n_heads, KEY_BLOCK
    scale = HEAD_DIM ** -0.5
    rc = _tile(S, 256, 8)

    def body(q_ref, k_ref, v_ref, c_ref, s_ref, g_ref, o_ref, l_ref, dm_ref, dq_ref, dk_ref, dv_ref, dg_ref,
             qr, kr, vf, dos, dsum, dqr, dkr, dvv):
        dg_ref[...] = jnp.zeros_like(dg_ref)

        def prep(t, carry):
            rows = pl.ds(pl.multiple_of(t * rc, rc), rc)
            qr[rows, :] = _rope(q_ref[rows, :].astype(F32), c_ref[rows, :], s_ref[rows, :])
            kr[rows, :] = _rope(k_ref[rows, :].astype(F32), c_ref[rows, :], s_ref[rows, :])
            vf[rows, :] = v_ref[rows, :].astype(F32)
            o, dm = o_ref[rows, :], dm_ref[rows, :].astype(F32)
            r = _rms_scale(o)
            do = _rms_bwd(dm * g_ref[...], o, r)
            dg_ref[...] += jnp.broadcast_to(jnp.sum(dm * o * r, axis=0, keepdims=True), dg_ref.shape)
            dos[rows, :] = do
            dsum[rows, :] = jnp.broadcast_to(jnp.sum(do * o, axis=1, keepdims=True), (rc, LANES))
            dqr[rows, :] = jnp.zeros((rc, HEAD_DIM), F32)
            dkr[rows, :] = jnp.zeros((rc, HEAD_DIM), F32)
            dvv[rows, :] = jnp.zeros((rc, HEAD_DIM), F32)
            return carry

        lax.fori_loop(0, S // rc, prep, 0)

        def visit(blocks):
            products = []
            for b, d, r, l0, first in blocks:
                qrows = _dil_rows(d, r, l0, B)
                krows = qrows if first else _dil_rows(d, r, l0 - B, 2 * B)
                qs, ks = qr[qrows, :].astype(BF16), kr[krows, :].astype(BF16)
                do = dos[qrows, :].astype(BF16)
                s = jnp.where(_dil_mask(first), _dot(qs, ks, NT) * scale, NEG)
                dp = _dot(do, vf[krows, :].astype(BF16), NT)
                products.append((qrows, krows, qs, ks, do, s, dp))
            cotangents = []
            for qrows, krows, qs, ks, do, s, dp in products:
                p = jnp.exp(s - l_ref[qrows, :][:, 0:1])
                ds = (p * (dp - dsum[qrows, :][:, 0:1]) * scale).astype(BF16)
                cotangents.append((qrows, krows, qs, ks, do, p.astype(BF16), ds))
            for qrows, krows, qs, ks, do, p, ds in cotangents:
                dqr[qrows, :] += _dot(ds, ks, NN)
                dkr[krows, :] += _dot(ds, qs, TN)
                dvv[krows, :] += _dot(p, do, TN)

        _dil_blocks(S, visit)

        def finish(t, carry):
            rows = pl.ds(pl.multiple_of(t * rc, rc), rc)
            c, s = c_ref[rows, :], s_ref[rows, :]
            dq, dk = dqr[rows, :], dkr[rows, :]
            dq_ref[rows, :] = (dq * c + pltpu.roll(dq * s, HEAD_DIM // 2, axis=1)).astype(BF16)
            dk_ref[rows, :] = (dk * c + pltpu.roll(dk * s, HEAD_DIM // 2, axis=1)).astype(BF16)
            dv_ref[rows, :] = dvv[rows, :].astype(BF16)
            return carry

        lax.fori_loop(0, S // rc, finish, 0)

    def col(k):
        return pl.BlockSpec((S, HEAD_DIM), lambda h: (0, col0 + k * H + h))

    tab = pl.BlockSpec((S, HEAD_DIM), lambda h: (0, 0))
    out = pl.BlockSpec((S, HEAD_DIM), lambda h: (0, h))
    W = H * HEAD_DIM
    big = pltpu.VMEM((S, HEAD_DIM), F32)
    return _pcall(
        body, grid=(H,),
        in_specs=[col(0), col(1), col(2), tab, tab, pl.BlockSpec((1, HEAD_DIM), lambda h: (0, h)), out, out,
                  pl.BlockSpec((S, HEAD_DIM), lambda h: (0, dm_col0 + h))],
        out_specs=[out, out, out, pl.BlockSpec((8, HEAD_DIM), lambda h: (0, h))],
        out_shape=[jax.ShapeDtypeStruct((S, W), BF16), jax.ShapeDtypeStruct((S, W), BF16),
                   jax.ShapeDtypeStruct((S, W), BF16), jax.ShapeDtypeStruct((8, W), F32)],
        scratch_shapes=[big, big, big, big, pltpu.VMEM((S, LANES), F32), big, big, big],
        compiler_params=_params("parallel"), name=name)(proj, proj, proj, cos2, sin_signed, gain, o_raw, lse, dmixed)


GELU_C = math.sqrt(2.0 / math.pi)
GELU_A = 0.044715
HALO = 16


def _shift_down(cur, halo, k):
    out = pltpu.roll(cur, k, axis=0)
    row = lax.broadcasted_iota(jnp.int32, cur.shape, 0)
    for t in range(k):
        out = jnp.where(row == t, halo[HALO - k + t:HALO - k + t + 1, :], out)
    return out


def _shift_up(cur, halo, k):
    n = cur.shape[0]
    out = pltpu.roll(cur, n - k, axis=0)
    row = lax.broadcasted_iota(jnp.int32, cur.shape, 0)
    for t in range(k):
        out = jnp.where(row == n - k + t, halo[t:t + 1, :], out)
    return out


def _conv3(cur, halo, cw):
    return _shift_down(cur, halo, 2) * cw[0:1, :] + _shift_down(cur, halo, 1) * cw[1:2, :] + cur * cw[2:3, :] + cw[3:4, :]


def _gelu_parts(x):
    t = jnp.tanh(GELU_C * (x + GELU_A * x * x * x))
    return 0.5 * x * (1.0 + t), t


def _geglu_specs(tm, tn, ncb):
    hb = tm // HALO

    def cur(off):
        return pl.BlockSpec((tm, tn), lambda j, i: (i, off + j))

    def prev(off):
        return pl.BlockSpec((HALO, tn), lambda j, i: (jnp.maximum(i * hb - 1, 0), off + j))

    def taps(off):
        return pl.BlockSpec((8, tn), lambda j, i: (0, off + j))

    return [cur(0), prev(0), cur(ncb), prev(ncb), taps(0), taps(ncb)]


def _geglu_fwd(u, cwb, name, tm=512, tn=1408):
    S, F2 = u.shape
    F = F2 // 2
    tm, tn = _tile(S, tm, HALO), _tile(F, tn)
    ncb = F // tn

    def body(g_ref, gp_ref, v_ref, vp_ref, cg_ref, cv_ref, y_ref):
        top = pl.program_id(1) > 0
        gp = jnp.where(top, gp_ref[...].astype(F32), 0.0)
        vp = jnp.where(top, vp_ref[...].astype(F32), 0.0)
        gc = _conv3(g_ref[...].astype(F32), gp, cg_ref[...])
        vc = _conv3(v_ref[...].astype(F32), vp, cv_ref[...])
        y_ref[...] = (_gelu_parts(gc)[0] * vc).astype(BF16)

    return _pcall(body, grid=(ncb, S // tm), in_specs=_geglu_specs(tm, tn, ncb),
                  out_specs=pl.BlockSpec((tm, tn), lambda j, i: (i, j)),
                  out_shape=jax.ShapeDtypeStruct((S, F), BF16),
                  compiler_params=_params("parallel", "parallel"), name=name)(u, u, u, u, cwb, cwb)


def _geglu_bwd(u, dy, cwb, name, tm=256, tn=1408):
    S, F2 = u.shape
    F = F2 // 2
    tm, tn = _tile(S, tm, HALO), _tile(F, tn)
    ncb = F // tn

    def body(g_ref, gp_ref, v_ref, vp_ref, cg_ref, cv_ref, dy_ref, dc_ref, dwg_ref, dwv_ref):
        i = pl.program_id(1)

        @pl.when(i == 0)
        def _():
            dwg_ref[...] = jnp.zeros_like(dwg_ref)
            dwv_ref[...] = jnp.zeros_like(dwv_ref)

        top = i > 0
        g, v = g_ref[...].astype(F32), v_ref[...].astype(F32)
        gp = jnp.where(top, gp_ref[...].astype(F32), 0.0)
        vp = jnp.where(top, vp_ref[...].astype(F32), 0.0)
        gc = _conv3(g, gp, cg_ref[...])
        vc = _conv3(v, vp, cv_ref[...])
        act, t = _gelu_parts(gc)
        dact = 0.5 * (1.0 + t) + 0.5 * gc * (1.0 - t * t) * GELU_C * (1.0 + 3.0 * GELU_A * gc * gc)
        dyv = dy_ref[...].astype(F32)
        dgc = dyv * vc * dact
        dvc = dyv * act
        dc_ref[0] = dgc.astype(BF16)
        dc_ref[1] = dvc.astype(BF16)

        def taps(out_ref, dc, cur, halo):
            out_ref[0:1, :] += jnp.sum(dc * _shift_down(cur, halo, 2), axis=0, keepdims=True)
            out_ref[1:2, :] += jnp.sum(dc * _shift_down(cur, halo, 1), axis=0, keepdims=True)
            out_ref[2:3, :] += jnp.sum(dc * cur, axis=0, keepdims=True)
            out_ref[3:4, :] += jnp.sum(dc, axis=0, keepdims=True)

        taps(dwg_ref, dgc, g, gp)
        taps(dwv_ref, dvc, v, vp)

    return _pcall(body, grid=(ncb, S // tm),
                  in_specs=_geglu_specs(tm, tn, ncb) + [pl.BlockSpec((tm, tn), lambda j, i: (i, j))],
                  out_specs=[pl.BlockSpec((2, tm, tn), lambda j, i: (0, i, j)),
                             pl.BlockSpec((8, tn), lambda j, i: (0, j)), pl.BlockSpec((8, tn), lambda j, i: (0, j))],
                  out_shape=[jax.ShapeDtypeStruct((2, S, F), BF16), jax.ShapeDtypeStruct((8, F), F32),
                             jax.ShapeDtypeStruct((8, F), F32)],
                  compiler_params=_params("parallel", "arbitrary"), name=name)(u, u, u, u, cwb, cwb, dy)


def _conv_bwd(dc, cwb, name, tm=512, tn=1408):
    _, S, F = dc.shape
    tm, tn = _tile(S, tm, HALO), _tile(F, tn)
    ncb, nrb = F // tn, S // tm
    hb = tm // HALO

    def body(c_ref, n_ref, w_ref, du_ref):
        cur = c_ref[...].astype(F32)
        nxt = jnp.where(pl.program_id(2) < nrb - 1, n_ref[...].astype(F32), 0.0)
        w = w_ref[...]
        du = cur * w[2:3, :] + _shift_up(cur, nxt, 1) * w[1:2, :] + _shift_up(cur, nxt, 2) * w[0:1, :]
        du_ref[...] = du.astype(BF16)

    return _pcall(body, grid=(2, ncb, nrb),
                  in_specs=[pl.BlockSpec((None, tm, tn), lambda c, j, i: (c, i, j)),
                            pl.BlockSpec((None, HALO, tn), lambda c, j, i: (c, jnp.minimum((i + 1) * hb, S // HALO - 1), j)),
                            pl.BlockSpec((8, tn), lambda c, j, i: (0, c * ncb + j))],
                  out_specs=pl.BlockSpec((tm, tn), lambda c, j, i: (i, c * ncb + j)),
                  out_shape=jax.ShapeDtypeStruct((S, 2 * F), BF16),
                  compiler_params=_params("parallel", "parallel", "parallel"), name=name)(dc, dc, cwb)


def _adam_math(w, g, m, v):
    m = ADAM_B1 * m + (1.0 - ADAM_B1) * g
    v = ADAM_B2 * v + (1.0 - ADAM_B2) * (g * g)
    m_hat = m / (1.0 - ADAM_B1 ** ADAM_STEP)
    v_hat = v / (1.0 - ADAM_B2 ** ADAM_STEP)
    return -ADAM_LR * (m_hat / (jnp.sqrt(v_hat) + ADAM_EPS) + ADAM_WD * w), m, v


def _adamw(w, parts, m, v, name, tr=256):
    R, C = w.shape
    n, _, Cp = parts.shape
    tr = _tile(R, tr, 8)

    def body(w_ref, p_ref, m_ref, v_ref, g_out, d_out, m_out, v_out):
        g = p_ref[0, :, 0:C].astype(F32)
        for k in range(1, n):
            g = g + p_ref[k, :, 0:C].astype(F32)
        d, mn, vn = _adam_math(w_ref[...], g, m_ref[...], v_ref[...])
        g_out[...] = g
        d_out[...] = d
        m_out[...] = mn
        v_out[...] = vn

    spec = pl.BlockSpec((tr, C), lambda i: (i, 0))
    shape = jax.ShapeDtypeStruct((R, C), F32)
    return _pcall(body, grid=(R // tr,), in_specs=[spec, pl.BlockSpec((n, tr, Cp), lambda i: (0, i, 0)), spec, spec],
                  out_specs=[spec] * 4, out_shape=[shape] * 4, compiler_params=_params("parallel"), name=name)(w, parts, m, v)


def _adamw_chips(w, pair, parts, chip_ids, m, v, name, tr=256):
    R, C = w.shape
    Cp = pair.shape[2]
    by_columns = C == Cp and _tile(R, tr, 16) < 64
    tr, tc = (R, _tile(C, 256)) if by_columns else (_tile(R, tr, 16), C)

    def body(ids_ref, w_ref, own_ref, p1_ref, p2_ref, p3_ref, m_ref, v_ref, g_out, d_out, m_out, v_out):
        g = own_ref[:, 0:tc].astype(F32)
        for ref in (p1_ref, p2_ref, p3_ref):
            g = g + ref[:, 0:tc].astype(F32)
        d, mn, vn = _adam_math(w_ref[...], g, m_ref[...], v_ref[...])
        g_out[...] = g
        d_out[...] = d
        m_out[...] = mn
        v_out[...] = vn

    if by_columns:
        spec = pl.BlockSpec((tr, tc), lambda j, ids: (0, j))
    else:
        spec = pl.BlockSpec((tr, tc), lambda i, ids: (i, 0))

    def chip(k):
        if by_columns:
            return pl.BlockSpec((None, tr, tc), lambda j, ids: (ids[k], 0, j))
        return pl.BlockSpec((None, tr, Cp), lambda i, ids: (ids[k], i, 0))

    shape = jax.ShapeDtypeStruct((R, C), F32)
    grid_spec = pltpu.PrefetchScalarGridSpec(
        num_scalar_prefetch=1, grid=(C // tc if by_columns else R // tr,),
        in_specs=[spec, chip(0), chip(1), chip(2), chip(3), spec, spec], out_specs=[spec] * 4)
    return _pcall(body, grid_spec=grid_spec, out_shape=[shape] * 4, compiler_params=_params("parallel"),
                  name=name)(chip_ids, w, pair, parts, parts, parts, m, v)


def _place():
    return lax.axis_index("x"), lax.axis_index("y"), lax.axis_index("c")


def _other_chips(x, y):
    return [(1 - x, y), (x, 1 - y), (1 - x, 1 - y)]


IN_HBM = pl.BlockSpec(memory_space=pltpu.HBM)
SEM = pl.BlockSpec(memory_space=pltpu.SEMAPHORE)
EFFECT = pltpu.SideEffectType.DATAFLOW_SIDE_EFFECTING
TOKEN = jax.ShapeDtypeStruct((8, LANES), F32)
TOKEN_SPEC = pl.BlockSpec(memory_space=pltpu.VMEM)


def _in_hbm(a):
    return pltpu.with_memory_space_constraint(a, pltpu.HBM)


def _landing(shape):
    return _in_hbm(lax.empty(shape.shape, shape.dtype))


def _hbm_like(a):
    return pltpu.HBM(a.shape, a.dtype)


def _gather_places():
    x, y, c = _place()
    relay_from = (c * (1 - x) + (1 - c) * x, c * y + (1 - c) * (1 - y), c)
    relay_to = (c * x + (1 - c) * (1 - x), c * (1 - y) + (1 - c) * y, c)
    return (x, y, c), (x, y, 1 - c), (1 - x, y, c), (x, 1 - y, c), (1 - x, 1 - y, c), relay_from, relay_to


def _slot_copy(slot, ref, src, dst, send_sem, recv_sem, to):
    return pltpu.make_async_remote_copy(src_ref=slot(ref, *src), dst_ref=slot(ref, *dst), send_sem=send_sem,
                                        recv_sem=recv_sem, device_id=to, device_id_type=MESH)


def _split_call(body, arrays, sems_in, sems_out, after, name, token=True):
    na, ni, no = len(arrays), len(sems_in), len(sems_out)

    def wrapped(*refs):
        body(refs[:na], refs[na:na + ni], refs[na + ni + 1:na + ni + 1 + no])
        if token:
            refs[-1][...] = jnp.zeros_like(refs[-1])

    outs = _pcall(
        wrapped, in_specs=[IN_HBM] * na + [SEM] * ni + [HBM],
        out_specs=[SEM] * no + [IN_HBM] * na + ([TOKEN_SPEC] if token else []),
        out_shape=[pltpu.SemaphoreType.DMA((n,)) for n in sems_out] + [_hbm_like(s) for s in arrays] + ([TOKEN] if token else []),
        input_output_aliases={a: no + a for a in range(na)},
        compiler_params=pltpu.CompilerParams(has_side_effects=EFFECT), name=name,
    )(*[_in_hbm(s) for s in arrays], *sems_in, after)
    return list(outs[:no]), list(outs[no:no + na]), (outs[-1] if token else None)


def _gather_start(landing, slots, after, name):
    na = len(landing)

    def body(land, _, sems):
        me, sib, xn, yn, _, _, _ = _gather_places()
        for a in range(na):
            for k, to in enumerate((sib, xn, yn)):
                _slot_copy(slots[a], land[a], me, me, sems[0].at[3 * a + k], sems[1].at[3 * a + k], to).start()

    return _split_call(body, landing, [], [3 * na, 3 * na], after, name)


def _gather_relay(gathered, sems1, slots, after, name):
    na = len(gathered)

    def body(gath, taken, given):
        me, sib, xn, yn, _, relay_from, relay_to = _gather_places()
        for a in range(na):
            for k, peer in enumerate((sib, xn, yn)):
                arrival = _slot_copy(slots[a], gath[a], me, peer, taken[0].at[3 * a + k], taken[1].at[3 * a + k], peer)
                arrival.wait_send()
                arrival.wait_recv()
        for a in range(na):
            _slot_copy(slots[a], gath[a], relay_from, relay_from, given[0].at[a], given[1].at[a], relay_to).start()
            for k, peer in enumerate((xn, yn)):
                _slot_copy(slots[a], gath[a], peer, peer, given[2].at[2 * a + k], given[3].at[2 * a + k], sib).start()

    return _split_call(body, gathered, sems1, [na, na, 2 * na, 2 * na], after, name)


def _gather_pass(gathered, relay_sems, slots, after, name):
    na = len(gathered)

    def body(gath, taken, given):
        me, sib, xn, yn, diag, relay_from, relay_to = _gather_places()
        for a in range(na):
            _slot_copy(slots[a], gath[a], relay_from, relay_from, taken[0].at[a], taken[1].at[a], relay_to).wait_send()
            _slot_copy(slots[a], gath[a], me, diag, taken[0].at[a], taken[1].at[a], relay_to).wait_recv()
        for a in range(na):
            _slot_copy(slots[a], gath[a], diag, diag, given[0].at[a], given[1].at[a], sib).start()

    return _split_call(body, gathered, relay_sems, [na, na], after, name)


def _gather_finish(gathered, pass_sems, diag_sems, slots, after, name):
    na = len(gathered)

    def body(gath, taken, _):
        (x, y, c), sib, xn, yn, diag, _, _ = _gather_places()
        for a in range(na):
            for k, peer in enumerate((xn, yn)):
                passed = _slot_copy(slots[a], gath[a], peer, (peer[0], peer[1], 1 - c), taken[0].at[2 * a + k],
                                    taken[1].at[2 * a + k], sib)
                passed.wait_send()
                passed.wait_recv()
            passed = _slot_copy(slots[a], gath[a], diag, (diag[0], diag[1], 1 - c), taken[2].at[a], taken[3].at[a], sib)
            passed.wait_send()
            passed.wait_recv()

    return _split_call(body, gathered, list(pass_sems) + list(diag_sems), [], after, name, token=False)[1]


def _pair_copy(view, src, land, send_sems, recv_sems, chip):
    x, y, c = _place()
    return pltpu.make_async_remote_copy(
        src_ref=view(src, chip, 1 - c), dst_ref=land.at[chip], send_sem=send_sems.at[chip], recv_sem=recv_sems.at[chip],
        device_id=(x, y, 1 - c), device_id_type=MESH)


def _pair_start(grad, view, block, after, name):
    def body(src, land, after_ref, send_sems, recv_sems, src_thru, land_thru, token):
        for chip in range(N_CHIP):
            _pair_copy(view, src, land, send_sems, recv_sems, chip).start()
        token[...] = jnp.zeros_like(token)

    sems = pltpu.SemaphoreType.DMA((N_CHIP,))
    land = jax.ShapeDtypeStruct((N_CHIP, *block), BF16)
    return _pcall(
        body, in_specs=[IN_HBM, IN_HBM, HBM], out_specs=[SEM, SEM, IN_HBM, IN_HBM, TOKEN_SPEC],
        out_shape=[sems, sems, _hbm_like(grad), _hbm_like(land), TOKEN], input_output_aliases={0: 2, 1: 3},
        compiler_params=pltpu.CompilerParams(has_side_effects=EFFECT), name=name,
    )(_in_hbm(grad), _landing(land), after)


def _pair_wait(grad, recv, send_sems, recv_sems, view, after, name):
    def body(src, land, send, recv_s, after_ref, src_thru, land_thru):
        for chip in range(N_CHIP):
            copy = _pair_copy(view, src, land, send, recv_s, chip)
            copy.wait_send()
            copy.wait_recv()

    return _pcall(
        body, in_specs=[IN_HBM, IN_HBM, SEM, SEM, HBM], out_specs=[IN_HBM, IN_HBM],
        out_shape=[_hbm_like(grad), _hbm_like(recv)], input_output_aliases={0: 0, 1: 1},
        compiler_params=pltpu.CompilerParams(has_side_effects=EFFECT), name=name,
    )(grad, recv, send_sems, recv_sems, after)


def _chip_start(pair, after, name):
    def body(src, land, after_ref, send_sems, recv_sems, src_thru, land_thru, token):
        x, y, c = _place()
        for j, (px, py) in enumerate(_other_chips(x, y)):
            pltpu.make_async_remote_copy(
                src_ref=src.at[2 * px + py], dst_ref=land.at[2 * x + y], send_sem=send_sems.at[j], recv_sem=recv_sems.at[j],
                device_id=(px, py, c), device_id_type=MESH).start()
        token[...] = jnp.zeros_like(token)

    sems = pltpu.SemaphoreType.DMA((3,))
    return _pcall(
        body, in_specs=[IN_HBM, IN_HBM, HBM], out_specs=[SEM, SEM, IN_HBM, IN_HBM, TOKEN_SPEC],
        out_shape=[sems, sems, _hbm_like(pair), _hbm_like(pair), TOKEN], input_output_aliases={0: 2, 1: 3},
        compiler_params=pltpu.CompilerParams(has_side_effects=EFFECT), name=name,
    )(_in_hbm(pair), _landing(pair), after)


def _chip_wait(pair, parts, send_sems, recv_sems, after, name):
    def body(src, land, send, recv, after_ref, src_thru, land_thru):
        x, y, c = _place()
        for j, (px, py) in enumerate(_other_chips(x, y)):
            copy = pltpu.make_async_remote_copy(
                src_ref=src.at[2 * px + py], dst_ref=land.at[2 * px + py], send_sem=send.at[j], recv_sem=recv.at[j],
                device_id=(px, py, c), device_id_type=MESH)
            copy.wait_send()
            copy.wait_recv()

    return _pcall(
        body, in_specs=[IN_HBM, IN_HBM, SEM, SEM, HBM], out_specs=[IN_HBM, IN_HBM],
        out_shape=[_hbm_like(pair), _hbm_like(parts)], input_output_aliases={0: 0, 1: 1},
        compiler_params=pltpu.CompilerParams(has_side_effects=EFFECT), name=name,
    )(pair, parts, send_sems, recv_sems, after)


def _pair_add(core, grad, recv, block, grad_spec, name):
    _, R, C = recv.shape
    tr = block

    def body(c_ref, g_ref, r_ref, o_ref):
        o_ref[...] = (g_ref[...].astype(F32) + r_ref[...].astype(F32)).astype(BF16)

    grid_spec = pltpu.PrefetchScalarGridSpec(
        num_scalar_prefetch=1, grid=(N_CHIP, R // tr),
        in_specs=[grad_spec, pl.BlockSpec((None, tr, C), lambda k, i, c: (k, i, 0))],
        out_specs=pl.BlockSpec((None, tr, C), lambda k, i, c: (k, i, 0)))
    return _pcall(body, grid_spec=grid_spec, out_shape=jax.ShapeDtypeStruct(recv.shape, BF16),
                  compiler_params=_params("parallel", "parallel"), name=name)(core, grad, recv)


def _small_copies(gath, send_sems, recv_sems):
    x, y, c = _place()
    peers = [(x, y, 1 - c)] + [(px, py, pc) for px, py in _other_chips(x, y) for pc in (c, 1 - c)]
    pairs = []
    for a, ref in enumerate(gath):
        mine = ref.at[4 * x + 2 * y + c]
        for k, (px, py, pc) in enumerate(peers):
            sems = dict(send_sem=send_sems.at[7 * a + k], recv_sem=recv_sems.at[7 * a + k], device_id=(px, py, pc),
                        device_id_type=MESH)
            pairs.append((pltpu.make_async_remote_copy(src_ref=mine, dst_ref=mine, **sems),
                          pltpu.make_async_remote_copy(src_ref=mine, dst_ref=ref.at[4 * px + 2 * py + pc], **sems)))
    return pairs


def _small_start(landing, after, name):
    na = len(landing)

    def body(*refs):
        for send, _ in _small_copies(refs[:na], refs[na + 1], refs[na + 2]):
            send.start()
        refs[-1][...] = jnp.zeros_like(refs[-1])

    sems = pltpu.SemaphoreType.DMA((7 * na,))
    outs = _pcall(
        body, in_specs=[IN_HBM] * na + [HBM], out_specs=[SEM, SEM] + [IN_HBM] * na + [TOKEN_SPEC],
        out_shape=[sems, sems] + [_hbm_like(s) for s in landing] + [TOKEN],
        input_output_aliases={a: 2 + a for a in range(na)},
        compiler_params=pltpu.CompilerParams(has_side_effects=EFFECT), name=name,
    )(*[_in_hbm(s) for s in landing], after)
    return outs[0], outs[1], outs[2:2 + na], outs[-1]


def _small_wait(gathered, send_sems, recv_sems, after, name):
    na = len(gathered)

    def body(*refs):
        for send, arrival in _small_copies(refs[:na], refs[na], refs[na + 1]):
            send.wait_send()
            arrival.wait_recv()

    return list(_pcall(
        body, in_specs=[IN_HBM] * na + [SEM, SEM, HBM], out_specs=[IN_HBM] * na,
        out_shape=[_hbm_like(g) for g in gathered], input_output_aliases={a: a for a in range(na)},
        compiler_params=pltpu.CompilerParams(has_side_effects=EFFECT), name=name,
    )(*gathered, send_sems, recv_sems, after))


def _small_finish(gathered, params, name):
    na, npar = len(gathered), len(params)

    def body(*refs):
        g_refs, wmv = refs[:na], refs[na:na + 3 * npar]
        o_sums, o_params = refs[na + 3 * npar:2 * na + 3 * npar], refs[2 * na + 3 * npar:]
        sums = []
        for a in range(na):
            acc = g_refs[a][0]
            for k in range(1, N_DEV):
                acc = acc + g_refs[a][k]
            o_sums[a][...] = acc
            sums.append(acc)
        for j, (a, row, _, _, _) in enumerate(params):
            g = sums[a][row:row + 1, :]
            d, mn, vn = _adam_math(wmv[3 * j][...], g, wmv[3 * j + 1][...], wmv[3 * j + 2][...])
            for out, val in zip(o_params[4 * j:4 * j + 4], (g, d, mn, vn)):
                out[...] = val

    vm = pl.BlockSpec(memory_space=pltpu.VMEM)
    flat = [t for p in params for t in p[2:]]
    out_shape = [jax.ShapeDtypeStruct(g.shape[1:], F32) for g in gathered]
    out_shape += [jax.ShapeDtypeStruct(p[2].shape, F32) for p in params for _ in range(4)]
    outs = _pcall(body, in_specs=[vm] * (na + 3 * npar), out_specs=[vm] * len(out_shape), out_shape=out_shape,
                  name=name)(*gathered, *flat)
    return outs[:na], [outs[na + 4 * j:na + 4 * j + 4] for j in range(npar)]


def _local_step(x, tgt, gains, weights):
    g_pre_mix, g_post_mix, g_pre_ffn, g_post_ffn, g_sb, g_dil = gains
    S, D = x.shape
    hs = g_sb.shape[1] // HEAD_DIM
    hd = g_dil.shape[1] // HEAD_DIM
    cos2, sin_signed = _rope_tables(S)

    h1 = _rms_fwd(x, g_pre_mix + weights.start(), "rms_in")
    w_in_g = weights.w_in(h1)
    proj = _mm_nn(h1, w_in_g, BF16, "proj", tn=768)
    o_sb, ct_sb, mixed = _sb_fwd(proj, g_sb + weights.relay_out(proj), hs, hs + hd, "sb_fwd")
    o_dl, lse_dl, mixed = _dil_fwd(proj, cos2, sin_signed, g_dil + weights.after_sb(o_sb), mixed, 3 * hs, hd, "dil_fwd")
    w_out_g = weights.w_out(o_dl)
    mix = _mm_nn(mixed, w_out_g, F32, "mix_out", tn=1024)
    x2, h2 = _mid_fwd(x, mix, g_post_mix + weights.after_mix(mix), g_pre_ffn, "mid_fwd")
    w_up_g, cwb = weights.w_up(h2)
    u = _mm_nn(h2, w_up_g, BF16, "ffn_up", b_transposed=True)
    y = _geglu_fwd(u, cwb + weights.forward_down(u), "geglu_fwd")
    w_down_g = weights.w_down(y)
    f = _mm_nn(y, w_down_g, F32, "ffn_down", tn=1024, tk=2816)

    dy, df, dg_post_ffn, loss = _loss_bwd(x2, f, tgt, g_post_ffn, "loss_bwd")
    dyv = _mm_nt(df, w_down_g, BF16, "d_y", tn=1408)
    dw_down = _mm_tn(y, df, D, BF16, "dw_down", tm=1408, tn=1024)
    dc, dcw_g, dcw_v = _geglu_bwd(u, dyv, cwb + weights.grad("w_down", dw_down), "geglu_bwd")
    du = _conv_bwd(dc, cwb + weights.grad_reduce("w_down", dc), "conv_bwd")
    dh2 = _mm_nt(du, w_up_g, BF16, "d_h2", tk=1408, b_transposed=True, per_step=2)
    dw_up = _mm_tn(du, h2, D, BF16, "dw_up", tm=1408, tn=1024)
    dx2, dmix, dg_pre_ffn, dg_post_mix = _mid_bwd(
        dy, dh2, x2, mix, g_pre_ffn + weights.grad("w_up", dw_up), g_post_mix, "mid_bwd")
    dmixed = _mm_nt(dmix, w_out_g, BF16, "d_mixed", after=jnp.reshape(weights.grad_reduce("w_up", dmix), (1, 1)))
    dw_out = _mm_tn(mixed, dmix, D, BF16, "dw_out", tn=1024)
    dq_s, dk_s, dv_s, dg_sb = _sb_bwd(proj, g_sb + weights.grad("w_out", dw_out), o_sb, ct_sb, dmixed, 0, hs, "sb_bwd")
    dq_d, dk_d, dv_d, dg_dil = _dil_bwd(proj, cos2, sin_signed, g_dil + weights.grad_reduce("w_out", dq_s), o_dl, lse_dl,
                                        dmixed, hs, 3 * hs, hd, "dil_bwd")
    dproj = jnp.concatenate([dq_s, dk_s, dv_s, dq_d, dk_d, dv_d], axis=1)
    dw_in = _mm_tn(h1, dproj, w_in_g.shape[2], BF16, "dw_in", tn=768)
    weights.grad("w_in", dw_in)
    dep = weights.grad_reduce("w_in", dproj)
    dh1 = _mm_nt(dproj, w_in_g, BF16, "d_h1", tk=768, after=jnp.reshape(dep, (1, 1)), per_step=4)
    grad_x, dg_pre_mix = _first_bwd(dx2, dh1, x, g_pre_mix, "first_bwd")
    small = (dg_pre_mix, dg_post_mix, dg_pre_ffn, dg_post_ffn, dg_sb[0:1], dg_dil[0:1], jnp.concatenate([dcw_g, dcw_v], axis=1))
    weights.small(small, loss)
    return loss, grad_x, small


def _pad_cols(a, to):
    return jnp.pad(a, ((0, 0), (0, to - a.shape[1])))


def kernel(x, pre_mix_gain, post_mix_gain, pre_ffn_gain, post_ffn_gain, w_in, sb_out_gain, dil_out_gain, w_out, w_up, conv_w, conv_b, w_down, loss_target, m_pre_mix_gain, m_post_mix_gain, m_pre_ffn_gain, m_post_ffn_gain, m_w_in, m_sb_out_gain, m_dil_out_gain, m_w_out, m_w_up, m_conv_w, m_conv_b, m_w_down, v_pre_mix_gain, v_post_mix_gain, v_pre_ffn_gain, v_post_ffn_gain, v_w_in, v_sb_out_gain, v_dil_out_gain, v_w_out, v_w_up, v_conv_w, v_conv_b, v_w_down):
    xb, tb = x[0], loss_target[0]
    S, D = xb.shape
    w_in, w_out, w_up, w_down, conv_w = w_in[0], w_out[0], w_up[0], w_down[0], conv_w[0]
    n_in, e_rows = w_in.shape[1], w_out.shape[0]
    cu, half = w_up.shape[1], w_down.shape[0]
    assert cu == 2 * half and half % 16 == 0
    cup = -(-cu // LANES) * LANES
    fp = N_CHIP * cup
    px, py, pc = _place()
    me = 4 * px + 2 * py + pc
    core = jnp.reshape(pc, (1,)).astype(jnp.int32)

    w_up_t, m_up_t, v_up_t = (jnp.swapaxes(t, 0, 1) for t in (w_up, m_w_up[0], v_w_up[0]))

    def by_dev(ref, qx, qy, qc):
        return ref.at[4 * qx + 2 * qy + qc]

    def down_slot(ref, qx, qy, qc):
        return ref.at[2 * qx + qy, pl.ds(qc * half, half)]

    def by_pair(ref, chip, k):
        return ref.at[chip, k]

    def down_pair(ref, chip, k):
        return ref.at[chip, pl.ds(k * half, half)]

    def pair_spec(tr, cols):
        return pl.BlockSpec((None, None, tr, cols), lambda k, i, c: (k, c[0], i, 0))

    tr_in, tr_up = _tile(D, 512, 16), _tile(cup, 256, 16)
    grad_plan = {
        "w_in": ((N_CHIP, 2, D, n_in), by_pair, (D, n_in), tr_in, pair_spec(tr_in, n_in)),
        "w_out": ((N_CHIP, 2, e_rows, D), by_pair, (e_rows, D), e_rows, pair_spec(e_rows, D)),
        "w_up": ((N_CHIP, 2, cup, D), by_pair, (cup, D), tr_up, pair_spec(tr_up, D)),
        "w_down": ((N_CHIP, cup, D), down_pair, (half, D), half,
                   pl.BlockSpec((None, half, D), lambda k, i, c: (k, c[0], 0))),
    }

    class Exchanges:
        def __init__(self):
            self.in_flight = {}

        def start(self):
            def own_slot(shard):
                return lax.dynamic_update_index_in_dim(lax.empty((N_DEV, *shard.shape), shard.dtype), shard, me, 0)

            self.group_slots = {"in": [by_dev], "out": [by_dev], "up": [by_dev, by_dev], "down": [down_slot]}
            self.flight = {}
            sems, gath, token = _gather_start([own_slot(w_in.astype(BF16))], [by_dev], core, "gather_in_start")
            self.flight["in"] = (sems, gath)
            zero = token[0, 0]
            self.landing = {
                "out": [own_slot((w_out + zero).astype(BF16))],
                "up": [own_slot(jnp.pad(w_up_t + zero, ((0, cup - cu), (0, 0))).astype(BF16)),
                       own_slot(jnp.pad(conv_w + zero, ((0, 8 - conv_w.shape[0]), (0, cup - cu))))],
                "down": [lax.dynamic_update_slice(jnp.zeros((N_CHIP, cup, D), BF16), (w_down + zero).astype(BF16)[None],
                                                  (2 * px + py, pc * half, 0))]}
            return zero

        def begin(self, group, after):
            sems, gath, token = _gather_start(self.landing[group], self.group_slots[group], after, "gather_%s_start" % group)
            self.flight[group] = (sems, gath)
            return token

        def relay(self, group, after):
            sems, gath = self.flight[group]
            sems, gath, token = _gather_relay(gath, sems, self.group_slots[group], after, "gather_%s_relay" % group)
            self.flight[group] = (sems, gath)
            return token

        def pass_on(self, group, after):
            sems, gath = self.flight[group]
            diag_sems, gath, token = _gather_pass(gath, sems[:2], self.group_slots[group], after, "gather_%s_pass" % group)
            self.flight[group] = (sems[2:], diag_sems, gath)
            return token

        def finish(self, group, after):
            pass_sems, diag_sems, gath = self.flight[group]
            return _gather_finish(gath, pass_sems, diag_sems, self.group_slots[group], after, "gather_%s_finish" % group)

        def w_in(self, after):
            token = self.begin("up", self.begin("out", self.relay("in", after)))
            return self.finish("in", self.pass_on("in", token))[0]

        def relay_out(self, after):
            return self.relay("out", after)[0, 0]

        def after_sb(self, after):
            return self.begin("down", self.relay("up", self.pass_on("out", after)))[0, 0]

        def w_out(self, after):
            return self.finish("out", after)[0].reshape(1, N_DEV * e_rows, D)

        def after_mix(self, after):
            return self.pass_on("up", after)[0, 0]

        def w_up(self, after):
            w_up_g, cw_g = self.finish("up", after)
            cb = _pad_cols(conv_b.reshape(N_DEV, cu), cup).reshape(1, 2 * fp)
            cw_full = jnp.transpose(cw_g[:, :3, :], (1, 0, 2)).reshape(3, 2 * fp)
            cwb = jnp.concatenate([cw_full, cb, jnp.zeros((4, 2 * fp), F32)], axis=0)
            return w_up_g, cwb

        def forward_down(self, after):
            return self.relay("down", after)[0, 0]

        def w_down(self, after):
            return self.finish("down", self.pass_on("down", after))[0].reshape(1, fp, D)

        def small(self, small, loss):
            d_pre_mix, d_post_mix, d_pre_ffn, d_post_ffn, d_sb, d_dil, d_conv = small

            def rows_of(*vectors):
                n = vectors[0].shape[1]
                row = lax.broadcasted_iota(jnp.int32, (8, n), 0)
                out = jnp.zeros((8, n), F32)
                for k, vec in enumerate(vectors):
                    out = jnp.where(row == k, vec, out)
                return out

            parts = [rows_of(d_pre_mix, d_post_mix, d_pre_ffn, d_post_ffn, jnp.broadcast_to(loss[:, :1], (1, D))),
                     rows_of(d_sb, d_dil), d_conv]
            landing = [lax.dynamic_update_index_in_dim(lax.empty((N_DEV, *p.shape), F32), p, me, 0) for p in parts]
            self.small_flight = _small_start(landing, parts[0], "small_start")

        def small_sums(self, after):
            send, recv, gath, _ = self.small_flight
            gath = _small_wait(gath, send, recv, after, "small_wait")
            params = [(0, 0, pre_mix_gain, m_pre_mix_gain, v_pre_mix_gain), (0, 1, post_mix_gain, m_post_mix_gain, v_post_mix_gain),
                      (0, 2, pre_ffn_gain, m_pre_ffn_gain, v_pre_ffn_gain), (0, 3, post_ffn_gain, m_post_ffn_gain, v_post_ffn_gain),
                      (1, 0, sb_out_gain, m_sb_out_gain, v_sb_out_gain), (1, 1, dil_out_gain, m_dil_out_gain, v_dil_out_gain)]
            (gains_sum, _, conv_sum), gain_steps = _small_finish(gath, params, "small_finish")
            return gains_sum[4, 0], conv_sum, gain_steps

        def grad(self, name, dw):
            view_shape, view, block, tr, spec = grad_plan[name]
            send, recv_sems, dw, recv, token = _pair_start(dw.reshape(view_shape), view, block, core, "pair_start_" + name)
            self.in_flight[name] = (dw, recv, send, recv_sems)
            return token[0, 0]

        def grad_reduce(self, name, after):
            _, view, _, tr, spec = grad_plan[name]
            dw, recv = _pair_wait(*self.in_flight[name], view, after, "pair_wait_" + name)
            pair = _pair_add(core, dw, recv, tr, spec, "pair_add_" + name)
            send, recv_sems, pair, parts, token = _chip_start(pair, recv, "chip_start_" + name)
            self.in_flight[name] = (pair, parts, send, recv_sems)
            self.last_token = token
            return token[0, 0]

        def grad_parts(self, name, after):
            return _chip_wait(*self.in_flight[name], after, "chip_wait_" + name)

    exchanges = Exchanges()
    gains = (pre_mix_gain, post_mix_gain, pre_ffn_gain, post_ffn_gain, sb_out_gain, dil_out_gain)
    loss, grad_x, small = _local_step(xb, tb, gains, exchanges)

    def small_adam(w, g, m, v, name):
        one = w.shape[0] == 1
        if one:
            w, g, m, v = (jnp.broadcast_to(t, (8, t.shape[1])) for t in (w, g, m, v))
        outs = _adamw(w, g[None], m, v, name)
        return [o[0:1] for o in outs] if one else outs

    chip_ids = jnp.stack([2 * px + py, 2 * (1 - px) + py, 2 * px + 1 - py, 2 * (1 - px) + 1 - py]).astype(jnp.int32)
    out_w_down = _adamw_chips(w_down, *exchanges.grad_parts("w_down", exchanges.small_flight[3]), chip_ids, m_w_down[0], v_w_down[0], "adam_w_down")
    out_up_t = _adamw_chips(w_up_t, *exchanges.grad_parts("w_up", out_w_down[1]), chip_ids, m_up_t, v_up_t, "adam_w_up")
    out_w_up = [jnp.swapaxes(o, 0, 1) for o in out_up_t]
    out_w_out = _adamw_chips(w_out, *exchanges.grad_parts("w_out", out_up_t[1]), chip_ids, m_w_out[0], v_w_out[0], "adam_w_out")
    loss_out, g_conv, gain_steps = exchanges.small_sums(out_w_out[1])
    out_pre_mix, out_post_mix, out_pre_ffn, out_post_ffn, out_sb, out_dil = gain_steps
    g_conv_b = g_conv[3].reshape(N_DEV, cup)[:, :cu].reshape(1, N_DEV * cu)
    g_conv_w = lax.dynamic_index_in_dim(g_conv[0:3].reshape(3, N_DEV, cup), me, axis=1, keepdims=False)[:, :cu]
    out_conv_b = small_adam(conv_b, g_conv_b, m_conv_b, v_conv_b, "adam_conv_b")
    cw8 = [jnp.pad(t, ((0, 5), (0, 0))) for t in (conv_w, g_conv_w, m_conv_w[0], v_conv_w[0])]
    out_conv_w = [o[0:3] for o in _adamw(cw8[0], cw8[1][None], cw8[2], cw8[3], "adam_conv_w")]
    out_w_in = _adamw_chips(w_in, *exchanges.grad_parts("w_in", out_conv_w[1]), chip_ids, m_w_in[0], v_w_in[0], "adam_w_in")

    order = [out_pre_mix, out_post_mix, out_pre_ffn, out_post_ffn, [o[None] for o in out_w_in], out_sb, out_dil,
             [o[None] for o in out_w_out], [o[None] for o in out_w_up], [o[None] for o in out_conv_w], out_conv_b,
             [o[None] for o in out_w_down]]
    outs = [loss_out, grad_x[None]]
    for k in range(4):
        outs += [o[k] for o in order]
    return tuple(outs)
```

```python
import math

import jax
import jax.numpy as jnp
from jax import lax
from jax.experimental import pallas as pl
from jax.experimental.pallas import tpu as pltpu

F32 = jnp.float32
BF16 = jnp.bfloat16
HEAD_DIM = 128
LANES = 128
KEY_BLOCK = 128
DILATIONS = (1, 4, 16)
RMS_EPS = 1e-6
ROPE_THETA = 10000.0
NEG = -1e30
ADAM_LR, ADAM_B1, ADAM_B2, ADAM_EPS, ADAM_WD, ADAM_STEP = 0.001, 0.9, 0.999, 1e-08, 0.01, 10
MESH = pl.DeviceIdType.MESH
N_DEV = 8
N_CHIP = 4
HBM = pl.BlockSpec(memory_space=pl.ANY)
VMEM_LIMIT = 56 * 1024 * 1024

_pcall = pl.pallas_call


def _tile(n, pref, mult=LANES):
    best = None
    t = mult
    while t <= min(n, pref):
        if n % t == 0:
            best = t
        t += mult
    return n if best is None else best


def _params(*sem):
    return pltpu.CompilerParams(dimension_semantics=sem, vmem_limit_bytes=VMEM_LIMIT)


def _dot(a, b, dims):
    return lax.dot_general(a, b, (dims, ((), ())), preferred_element_type=F32)


NN = ((1,), (0,))
NT = ((1,), (1,))
TN = ((0,), (0,))


def _mm_body(dims, nk, tile):
    if nk == 1:
        def single(a_ref, b_ref, o_ref):
            o_ref[...] = _dot(a_ref[...].astype(BF16), b_ref[...].astype(BF16), dims).astype(o_ref.dtype)

        return single, []

    def body(a_ref, b_ref, o_ref, acc_ref):
        k = pl.program_id(2)

        @pl.when(k == 0)
        def _():
            acc_ref[...] = jnp.zeros_like(acc_ref)

        acc_ref[...] += _dot(a_ref[...].astype(BF16), b_ref[...].astype(BF16), dims)

        @pl.when(k == nk - 1)
        def _():
            o_ref[...] = acc_ref[...].astype(o_ref.dtype)

    return body, [pltpu.VMEM(tile, F32)]


def _mm_nn(a, b3, out_dtype, name, tm=1024, tn=1408, tk=2048, b_transposed=False):
    M, K = a.shape
    C, n = b3.shape[0], b3.shape[1 if b_transposed else 2]
    tm, tk, tn = _tile(M, tm, 8), _tile(K, tk), _tile(n, tn)
    npc, nk = n // tn, K // tk
    body, scratch = _mm_body(NT if b_transposed else NN, nk, (tm, tn))
    b_spec = (pl.BlockSpec((None, tn, tk), lambda i, j, k: (j // npc, j % npc, k)) if b_transposed
              else pl.BlockSpec((None, tk, tn), lambda i, j, k: (j // npc, k, j % npc)))
    return _pcall(
        body, grid=(M // tm, C * npc, nk),
        in_specs=[pl.BlockSpec((tm, tk), lambda i, j, k: (i, k)), b_spec],
        out_specs=pl.BlockSpec((tm, tn), lambda i, j, k: (i, j)),
        out_shape=jax.ShapeDtypeStruct((M, C * n), out_dtype), scratch_shapes=scratch,
        compiler_params=_params("parallel", "parallel", "arbitrary"), name=name)(a, b3)


def _mm_nt(a, b3, out_dtype, name, tm=1024, tn=1024, tk=2048, after=None, b_transposed=False, per_step=1):
    M, _ = a.shape
    C, N, n = (b3.shape[0], b3.shape[2], b3.shape[1]) if b_transposed else b3.shape
    tm, tn, tk = _tile(M, tm, 8), _tile(N, tn), _tile(n, tk)
    dims = NN if b_transposed else NT
    extra = [] if after is None else [after]
    if per_step > 1 and tk == n and C % per_step == 0:
        nk, scratch = C // per_step, [pltpu.VMEM((tm, tn), F32)]
        b3 = b3.reshape(nk, per_step, *b3.shape[1:])
        a_spec = pl.BlockSpec((tm, per_step * n), lambda i, j, k: (i, k))
        if b_transposed:
            b_spec = pl.BlockSpec((None, per_step, n, tn), lambda i, j, k: (k, 0, 0, j))
        else:
            b_spec = pl.BlockSpec((None, per_step, tn, n), lambda i, j, k: (k, 0, j, 0))

        def body(a_ref, b_ref, *rest):
            o_ref, acc_ref = rest[len(extra):]
            k = pl.program_id(2)

            @pl.when(k == 0)
            def _():
                acc_ref[...] = jnp.zeros_like(acc_ref)

            b = b_ref[...].astype(BF16)
            b = b.reshape(per_step * n, tn) if b_transposed else jnp.concatenate([b[u] for u in range(per_step)], axis=1)
            acc_ref[...] += _dot(a_ref[...].astype(BF16), b, dims)

            @pl.when(k == nk - 1)
            def _():
                o_ref[...] = acc_ref[...].astype(o_ref.dtype)
    else:
        kpc = n // tk
        nk = C * kpc
        inner, scratch = _mm_body(dims, nk, (tm, tn))
        a_spec = pl.BlockSpec((tm, tk), lambda i, j, k: (i, k))
        b_spec = (pl.BlockSpec((None, tk, tn), lambda i, j, k: (k // kpc, k % kpc, j)) if b_transposed
                  else pl.BlockSpec((None, tn, tk), lambda i, j, k: (k // kpc, j, k % kpc)))

        def body(a_ref, b_ref, *rest):
            inner(a_ref, b_ref, *rest[len(extra):])

    return _pcall(
        body, grid=(M // tm, N // tn, nk), in_specs=[a_spec, b_spec] + [HBM] * len(extra),
        out_specs=pl.BlockSpec((tm, tn), lambda i, j, k: (i, j)),
        out_shape=jax.ShapeDtypeStruct((M, N), out_dtype), scratch_shapes=scratch,
        compiler_params=_params("parallel", "parallel", "arbitrary"), name=name)(a, b3, *extra)


def _mm_tn(x, y, n, out_dtype, name, tm=1024, tn=1408, tk=2048, after=None):
    S, P = x.shape
    C = y.shape[1] // n
    tm, tn, tk = _tile(P, tm), _tile(n, tn), _tile(S, tk, 8)
    npc, nk = n // tn, S // tk
    inner, scratch = _mm_body(TN, nk, (tm, tn))
    extra = [] if after is None else [after]

    def body(x_ref, y_ref, *rest):
        inner(x_ref, y_ref, *rest[len(extra):])

    return _pcall(
        body, grid=(P // tm, C * npc, nk),
        in_specs=[pl.BlockSpec((tk, tm), lambda i, j, k: (k, i)),
                  pl.BlockSpec((tk, tn), lambda i, j, k: (k, j))] + [HBM] * len(extra),
        out_specs=pl.BlockSpec((None, tm, tn), lambda i, j, k: (j // npc, i, j % npc)),
        out_shape=jax.ShapeDtypeStruct((C, P, n), out_dtype), scratch_shapes=scratch,
        compiler_params=_params("parallel", "parallel", "arbitrary"), name=name)(x, y, *extra)


def _rms_scale(v):
    return lax.rsqrt(jnp.mean(v * v, axis=-1, keepdims=True) + RMS_EPS)


def _rms_bwd(gy, v, r):
    return r * gy - v * (r * r * r * jnp.mean(gy * v, axis=-1, keepdims=True))


def _rows_spec(tm, d):
    return pl.BlockSpec((tm, d), lambda i: (i, 0))


def _vec_spec(d):
    return pl.BlockSpec((1, d), lambda i: (0, 0))


def _rms_fwd(x, g, name, tm=256):
    S, D = x.shape

    def body(x_ref, g_ref, h_ref):
        v = x_ref[...]
        h_ref[...] = (v * _rms_scale(v) * g_ref[...]).astype(BF16)

    return _pcall(body, grid=(S // tm,), in_specs=[_rows_spec(tm, D), _vec_spec(D)], out_specs=_rows_spec(tm, D),
                  out_shape=jax.ShapeDtypeStruct((S, D), BF16), compiler_params=_params("parallel"), name=name)(x, g)


def _mid_fwd(x, mix, g_post, g_pre, name, tm=256):
    S, D = x.shape

    def body(x_ref, m_ref, gp_ref, gn_ref, x2_ref, h_ref):
        m = m_ref[...]
        x2 = x_ref[...] + m * _rms_scale(m) * gp_ref[...]
        x2_ref[...] = x2
        h_ref[...] = (x2 * _rms_scale(x2) * gn_ref[...]).astype(BF16)

    return _pcall(body, grid=(S // tm,), in_specs=[_rows_spec(tm, D), _rows_spec(tm, D), _vec_spec(D), _vec_spec(D)],
                  out_specs=[_rows_spec(tm, D), _rows_spec(tm, D)],
                  out_shape=[jax.ShapeDtypeStruct((S, D), F32), jax.ShapeDtypeStruct((S, D), BF16)],
                  compiler_params=_params("parallel"), name=name)(x, mix, g_post, g_pre)


def _loss_bwd(x2, f, tgt, g_post, name, tm=256):
    S, D = x2.shape

    def body(x2_ref, f_ref, t_ref, g_ref, dy_ref, df_ref, dg_ref, ls_ref):
        i = pl.program_id(0)

        @pl.when(i == 0)
        def _():
            dg_ref[...] = jnp.zeros_like(dg_ref)
            ls_ref[...] = jnp.zeros_like(ls_ref)

        fv = f_ref[...]
        r = _rms_scale(fv)
        g = g_ref[...]
        err = x2_ref[...] + fv * r * g - t_ref[...]
        ls_ref[...] += jnp.broadcast_to(0.5 * jnp.sum(jnp.mean(err * err, axis=-1, keepdims=True), axis=0, keepdims=True), ls_ref.shape)
        dy = err * (1.0 / D)
        dy_ref[...] = dy
        df_ref[...] = _rms_bwd(dy * g, fv, r).astype(BF16)
        dg_ref[...] += jnp.sum(dy * fv * r, axis=0, keepdims=True)

    return _pcall(body, grid=(S // tm,),
                  in_specs=[_rows_spec(tm, D), _rows_spec(tm, D), _rows_spec(tm, D), _vec_spec(D)],
                  out_specs=[_rows_spec(tm, D), _rows_spec(tm, D), _vec_spec(D), _vec_spec(LANES)],
                  out_shape=[jax.ShapeDtypeStruct((S, D), F32), jax.ShapeDtypeStruct((S, D), BF16),
                             jax.ShapeDtypeStruct((1, D), F32), jax.ShapeDtypeStruct((1, LANES), F32)],
                  compiler_params=_params("arbitrary"), name=name)(x2, f, tgt, g_post)


def _mid_bwd(dy, dh2, x2, mix, g_pre, g_post, name, tm=256):
    S, D = dy.shape

    def body(dy_ref, dh_ref, x2_ref, m_ref, gn_ref, gp_ref, dx2_ref, dm_ref, dgn_ref, dgp_ref):
        i = pl.program_id(0)

        @pl.when(i == 0)
        def _():
            dgn_ref[...] = jnp.zeros_like(dgn_ref)
            dgp_ref[...] = jnp.zeros_like(dgp_ref)

        x2, dh = x2_ref[...], dh_ref[...].astype(F32)
        r = _rms_scale(x2)
        dx2 = dy_ref[...] + _rms_bwd(dh * gn_ref[...], x2, r)
        dgn_ref[...] += jnp.sum(dh * x2 * r, axis=0, keepdims=True)
        dx2_ref[...] = dx2
        m = m_ref[...]
        rm = _rms_scale(m)
        dm_ref[...] = _rms_bwd(dx2 * gp_ref[...], m, rm).astype(BF16)
        dgp_ref[...] += jnp.sum(dx2 * m * rm, axis=0, keepdims=True)

    return _pcall(body, grid=(S // tm,),
                  in_specs=[_rows_spec(tm, D)] * 4 + [_vec_spec(D)] * 2,
                  out_specs=[_rows_spec(tm, D), _rows_spec(tm, D), _vec_spec(D), _vec_spec(D)],
                  out_shape=[jax.ShapeDtypeStruct((S, D), F32), jax.ShapeDtypeStruct((S, D), BF16),
                             jax.ShapeDtypeStruct((1, D), F32), jax.ShapeDtypeStruct((1, D), F32)],
                  compiler_params=_params("arbitrary"), name=name)(dy, dh2, x2, mix, g_pre, g_post)


def _first_bwd(dx2, dh1, x, g_pre, name, tm=256):
    S, D = x.shape

    def body(dx2_ref, dh_ref, x_ref, g_ref, gx_ref, dg_ref):
        i = pl.program_id(0)

        @pl.when(i == 0)
        def _():
            dg_ref[...] = jnp.zeros_like(dg_ref)

        xv, dh = x_ref[...], dh_ref[...].astype(F32)
        r = _rms_scale(xv)
        gx_ref[...] = dx2_ref[...] + _rms_bwd(dh * g_ref[...], xv, r)
        dg_ref[...] += jnp.sum(dh * xv * r, axis=0, keepdims=True)

    return _pcall(body, grid=(S // tm,), in_specs=[_rows_spec(tm, D)] * 3 + [_vec_spec(D)],
                  out_specs=[_rows_spec(tm, D), _vec_spec(D)],
                  out_shape=[jax.ShapeDtypeStruct((S, D), F32), jax.ShapeDtypeStruct((1, D), F32)],
                  compiler_params=_params("arbitrary"), name=name)(dx2, dh1, x, g_pre)


def _logsig_pair(z):
    lb = jnp.minimum(z, 0.0) - jnp.log(1.0 + jnp.exp(-jnp.abs(z)))
    return lb, lb - z


SB_KEY_BLOCK = 256


def _sum_matrix(strict):
    ia = lax.broadcasted_iota(jnp.int32, (SB_KEY_BLOCK, SB_KEY_BLOCK), 0)
    ib = lax.broadcasted_iota(jnp.int32, (SB_KEY_BLOCK, SB_KEY_BLOCK), 1)
    return ((ia > ib) if strict == ">" else (ia < ib)).astype(BF16)


def _row_total(sums, v, col):
    return jnp.broadcast_to(sums[:, col:col + 1] + v[:, col:col + 1], (v.shape[0], LANES))


def _lanes(c, width):
    return jnp.tile(c, (1, width // LANES))


def _split_dot(v, u):
    hi = v.astype(BF16)
    lo = (v - hi.astype(F32)).astype(BF16)
    return _dot(hi, u, NN) + _dot(lo, u, NN)


def _head_out(o, g):
    return o * _rms_scale(o) * g


def _sb_fwd(proj, gain, n_heads, mixed_heads, name, tq=1024):
    S = proj.shape[0]
    H, tk = n_heads, SB_KEY_BLOCK
    tq = _tile(S, tq, 2 * tk)
    scale = HEAD_DIM ** -0.5

    def body(q_ref, k_ref, v_ref, g_ref, o_ref, ct_ref, mx_ref, oacc, cacc):
        i = pl.program_id(1)
        oacc[...] = jnp.zeros_like(oacc)
        cacc[...] = jnp.zeros_like(cacc)
        sums = _sum_matrix(">")

        def run(blocks):
            scored = []
            for k0, r0, diagonal in blocks:
                rows = pl.ds(r0, tq - r0)
                lb, lk = _logsig_pair(_dot(q_ref[rows, :].astype(BF16), k_ref[pl.ds(k0, tk), :].astype(BF16), NT) * scale)
                causal = None
                if diagonal:
                    causal = (lax.broadcasted_iota(jnp.int32, (tq - r0, tk), 1)
                              < lax.broadcasted_iota(jnp.int32, (tq - r0, tk), 0))
                    lk = jnp.where(causal, lk, 0.0)
                scored.append((k0, rows, causal, lb, lk))
            summed = [(k0, rows, causal, lb, lk, _split_dot(lk, sums)) for k0, rows, causal, lb, lk in scored]
            weights = []
            for k0, rows, causal, lb, lk, after in summed:
                c = cacc[rows, :]
                a = jnp.exp(lb + after + _lanes(c, tk))
                if causal is not None:
                    a = jnp.where(causal, a, 0.0)
                cacc[rows, :] = c + _row_total(after, lk, 0)
                weights.append((k0, rows, a.astype(BF16)))
            for k0, rows, a in weights:
                oacc[rows, :] += _dot(a, v_ref[pl.ds(k0, tk), :].astype(BF16), NN)

        for d in reversed(range(0, tq // tk, 2)):
            run([(pl.multiple_of(i * tq + e * tk, tk), e * tk, True) for e in (d + 1, d)])
        per_trip = tq // tk

        def step(it, carry):
            k0 = pl.multiple_of((i - 1 - it) * tq, tq)
            run([(pl.multiple_of(k0 + e * tk, tk), 0, False) for e in reversed(range(per_trip))])
            return carry

        lax.fori_loop(0, i, step, 0)
        o = oacc[...]
        o_ref[...] = o
        ct_ref[...] = cacc[...]
        mx_ref[...] = _head_out(o, g_ref[...]).astype(BF16)

    blk = pl.BlockSpec((tq, HEAD_DIM), lambda h, i: (i, h))
    return _pcall(
        body, grid=(H, S // tq),
        in_specs=[blk, pl.BlockSpec((S, HEAD_DIM), lambda h, i: (0, H + h)),
                  pl.BlockSpec((S, HEAD_DIM), lambda h, i: (0, 2 * H + h)), pl.BlockSpec((1, HEAD_DIM), lambda h, i: (0, h))],
        out_specs=[blk, blk, blk],
        out_shape=[jax.ShapeDtypeStruct((S, H * HEAD_DIM), F32), jax.ShapeDtypeStruct((S, H * HEAD_DIM), F32),
                   jax.ShapeDtypeStruct((S, mixed_heads * HEAD_DIM), BF16)],
        scratch_shapes=[pltpu.VMEM((tq, HEAD_DIM), F32), pltpu.VMEM((tq, LANES), F32)],
        compiler_params=_params("parallel", "arbitrary"), name=name)(proj, proj, proj, gain)


def _sb_bwd(proj, gain, o_raw, ctot, dmixed, dm_col0, n_heads, name, tq=1024):
    S = proj.shape[0]
    H, tk = n_heads, SB_KEY_BLOCK
    tq = _tile(S, tq, 2 * tk)
    nq = S // tq
    scale = HEAD_DIM ** -0.5

    def body(q_ref, k_ref, v_ref, g_ref, o_ref, ct_ref, dm_ref, dq_ref, dk_ref, dv_ref, dg_ref,
             dkacc, dvacc, dqacc, pfx, gcar, dos):
        i = pl.program_id(1)

        @pl.when(i == 0)
        def _():
            dkacc[...] = jnp.zeros_like(dkacc)
            dvacc[...] = jnp.zeros_like(dvacc)
            dg_ref[...] = jnp.zeros_like(dg_ref)

        o, dm, g = o_ref[...], dm_ref[...].astype(F32), g_ref[...]
        r = _rms_scale(o)
        dos[...] = _rms_bwd(dm * g, o, r).astype(BF16)
        dg_ref[...] += jnp.broadcast_to(jnp.sum(dm * o * r, axis=0, keepdims=True), dg_ref.shape)
        dqacc[...] = jnp.zeros_like(dqacc)
        pfx[...] = jnp.zeros_like(pfx)
        gcar[...] = jnp.zeros_like(gcar)
        later, earlier = _sum_matrix(">"), _sum_matrix("<")

        def run(blocks):
            scored = []
            for k0, r0, diagonal in blocks:
                rows, keys = pl.ds(r0, tq - r0), pl.ds(k0, tk)
                lb, lk = _logsig_pair(_dot(q_ref[rows, :].astype(BF16), k_ref[keys, :].astype(BF16), NT) * scale)
                da = _dot(dos[rows, :], v_ref[keys, :].astype(BF16), NT)
                causal = None
                if diagonal:
                    causal = (lax.broadcasted_iota(jnp.int32, (tq - r0, tk), 1)
                              < lax.broadcasted_iota(jnp.int32, (tq - r0, tk), 0))
                    lk = jnp.where(causal, lk, 0.0)
                scored.append((rows, keys, causal, lb, lk, da))
            summed = [(*blk, _split_dot(blk[4], later)) for blk in scored]
            weighted = []
            for rows, keys, causal, lb, lk, da, after in summed:
                p = pfx[rows, :] + _row_total(after, lk, 0)
                pfx[rows, :] = p
                a = jnp.exp(lb + after + _lanes(ct_ref[rows, :] - p, tk))
                if causal is not None:
                    a = jnp.where(causal, a, 0.0)
                dl = da * a
                weighted.append((rows, keys, causal, lb, a.astype(BF16), dl, _dot(dl.astype(BF16), earlier, NN)))
            cotangents = []
            for rows, keys, causal, lb, a, dl, before in weighted:
                gc = gcar[rows, :]
                gcar[rows, :] = gc + _row_total(before, dl, tk - 1)
                sig = jnp.exp(lb)
                gsum = (before + _lanes(gc, tk)) * sig
                if causal is not None:
                    gsum = jnp.where(causal, gsum, 0.0)
                cotangents.append((rows, keys, a, ((dl * (1.0 - sig) - gsum) * scale).astype(BF16)))
            for rows, keys, a, dz in cotangents:
                q, do = q_ref[rows, :].astype(BF16), dos[rows, :]
                dvacc[keys, :] += _dot(a, do, TN)
                dqacc[rows, :] += _dot(dz, k_ref[keys, :].astype(BF16), NN)
                dkacc[keys, :] += _dot(dz, q, TN)

        per_trip = tq // tk

        def step(j, carry):
            k0 = pl.multiple_of(j * tq, tq)
            run([(pl.multiple_of(k0 + e * tk, tk), 0, False) for e in range(per_trip)])
            return carry

        lax.fori_loop(0, i, step, 0)
        for d in range(0, tq // tk, 2):
            run([(pl.multiple_of(i * tq + e * tk, tk), e * tk, True) for e in (d, d + 1)])
        dq_ref[...] = dqacc[...].astype(BF16)

        @pl.when(i == nq - 1)
        def _():
            dk_ref[...] = dkacc[...].astype(BF16)
            dv_ref[...] = dvacc[...].astype(BF16)

    blk = pl.BlockSpec((tq, HEAD_DIM), lambda h, i: (i, h))
    full = pl.BlockSpec((S, HEAD_DIM), lambda h, i: (0, h))
    W = H * HEAD_DIM
    return _pcall(
        body, grid=(H, nq),
        in_specs=[blk, pl.BlockSpec((S, HEAD_DIM), lambda h, i: (0, H + h)),
                  pl.BlockSpec((S, HEAD_DIM), lambda h, i: (0, 2 * H + h)), pl.BlockSpec((1, HEAD_DIM), lambda h, i: (0, h)),
                  blk, blk, pl.BlockSpec((tq, HEAD_DIM), lambda h, i: (i, dm_col0 + h))],
        out_specs=[blk, full, full, pl.BlockSpec((8, HEAD_DIM), lambda h, i: (0, h))],
        out_shape=[jax.ShapeDtypeStruct((S, W), BF16), jax.ShapeDtypeStruct((S, W), BF16),
                   jax.ShapeDtypeStruct((S, W), BF16), jax.ShapeDtypeStruct((8, W), F32)],
        scratch_shapes=[pltpu.VMEM((S, HEAD_DIM), F32), pltpu.VMEM((S, HEAD_DIM), F32), pltpu.VMEM((tq, HEAD_DIM), F32),
                        pltpu.VMEM((tq, LANES), F32), pltpu.VMEM((tq, LANES), F32), pltpu.VMEM((tq, HEAD_DIM), BF16)],
        compiler_params=_params("arbitrary", "arbitrary"), name=name)(proj, proj, proj, gain, o_raw, ctot, dmixed)


def _rope_tables(S):
    inv_freq = ROPE_THETA ** (-jnp.arange(0, HEAD_DIM, 2, dtype=F32) / HEAD_DIM)
    ang = jnp.arange(S, dtype=F32)[:, None] * inv_freq[None, :]
    cos, sin = jnp.cos(ang), jnp.sin(ang)
    return jnp.concatenate([cos, cos], axis=1), jnp.concatenate([-sin, sin], axis=1)


def _rope(v, cos2, sin_signed):
    return v * cos2 + pltpu.roll(v, HEAD_DIM // 2, axis=1) * sin_signed


def _dil_rows(d, r, l0, n):
    if d == 1:
        return pl.ds(l0 if isinstance(l0, int) else pl.multiple_of(l0, KEY_BLOCK), n)
    return pl.ds(r + d * l0, n, stride=d)


def _dil_blocks(S, visit):
    B = KEY_BLOCK
    group = 16
    for b, d in enumerate(DILATIONS):
        nb = S // d // B
        if nb == 1:
            g = math.gcd(d, group)

            def trip(t, carry, b=b, d=d, g=g):
                visit([(b, d, t * g + u, 0, True) for u in range(g)])
                return carry

            lax.fori_loop(0, d // g, trip, 0)
        elif d == 1:
            visit([(b, d, 0, 0, True)])
            g = max(k for k in range(1, group + 2) if (nb - 1) % k == 0)

            def trip(t, carry, b=b, d=d, g=g):
                visit([(b, d, 0, (1 + t * g + u) * B, False) for u in range(g)])
                return carry

            lax.fori_loop(0, (nb - 1) // g, trip, 0)
        else:
            g = math.gcd(d, max(group // nb, 1))

            def trip(t, carry, b=b, d=d, nb=nb, g=g):
                visit([(b, d, t * g + u, n * B, n == 0) for u in range(g) for n in range(nb)])
                return carry

            lax.fori_loop(0, d // g, trip, 0)


def _dil_mask(first):
    B = KEY_BLOCK
    nk = B if first else 2 * B
    iq = lax.broadcasted_iota(jnp.int32, (B, nk), 0)
    ik = lax.broadcasted_iota(jnp.int32, (B, nk), 1)
    return (ik <= iq) if first else ((ik >= iq) & (ik <= iq + B))


def _dil_fwd(proj, cos2, sin_signed, gain, mixed, col0, n_heads, name):
    S = proj.shape[0]
    H, B = n_heads, KEY_BLOCK
    scale = HEAD_DIM ** -0.5
    rc = _tile(S, 256, 8)

    def body(q_ref, k_ref, v_ref, c_ref, s_ref, g_ref, mixed_in, o_ref, l_ref, mx_ref, qr, kr, vf, *per_branch):
        ob, lb = per_branch[:len(DILATIONS)], per_branch[len(DILATIONS):]

        def rope_rows(t, carry):
            rows = pl.ds(pl.multiple_of(t * rc, rc), rc)
            qr[rows, :] = _rope(q_ref[rows, :].astype(F32), c_ref[rows, :], s_ref[rows, :])
            kr[rows, :] = _rope(k_ref[rows, :].astype(F32), c_ref[rows, :], s_ref[rows, :])
            vf[rows, :] = v_ref[rows, :].astype(F32)
            return carry

        lax.fori_loop(0, S // rc, rope_rows, 0)

        def visit(blocks):
            scores = []
            for b, d, r, l0, first in blocks:
                qrows = _dil_rows(d, r, l0, B)
                krows = qrows if first else _dil_rows(d, r, l0 - B, 2 * B)
                s = _dot(qr[qrows, :].astype(BF16), kr[krows, :].astype(BF16), NT) * scale
                scores.append((b, qrows, krows, jnp.where(_dil_mask(first), s, NEG)))
            weights = []
            for b, qrows, krows, s in scores:
                m = jnp.max(s, axis=1, keepdims=True)
                p = jnp.exp(s - m)
                den = jnp.sum(p, axis=1, keepdims=True)
                lb[b][qrows, :] = jnp.broadcast_to(m + jnp.log(den), (B, LANES))
                weights.append((b, qrows, krows, p.astype(BF16), den))
            for b, qrows, krows, p, den in weights:
                ob[b][qrows, :] = _dot(p, vf[krows, :].astype(BF16), NN) / den

        _dil_blocks(S, visit)

        def combine(t, carry):
            rows = pl.ds(pl.multiple_of(t * rc, rc), rc)
            l0, l1, l2 = lb[0][rows, :], lb[1][rows, :], lb[2][rows, :]
            m = jnp.maximum(jnp.maximum(l0, l1), l2)
            w0, w1, w2 = jnp.exp(l0 - m), jnp.exp(l1 - m), jnp.exp(l2 - m)
            den = w0 + w1 + w2
            o = (w0 * ob[0][rows, :] + w1 * ob[1][rows, :] + w2 * ob[2][rows, :]) / den
            o_ref[rows, :] = o
            l_ref[rows, :] = m + jnp.log(den)
            mx_ref[rows, :] = _head_out(o, g_ref[...]).astype(BF16)
            return carry

        lax.fori_loop(0, S // rc, combine, 0)

    def col(k):
        return pl.BlockSpec((S, HEAD_DIM), lambda h: (0, col0 + k * H + h))

    tab = pl.BlockSpec((S, HEAD_DIM), lambda h: (0, 0))
    out = pl.BlockSpec((S, HEAD_DIM), lambda h: (0, h))
    W = H * HEAD_DIM
    first = mixed.shape[1] // HEAD_DIM - H
    return _pcall(
        body, grid=(H,),
        in_specs=[col(0), col(1), col(2), tab, tab, pl.BlockSpec((1, HEAD_DIM), lambda h: (0, h)), HBM],
        out_specs=[out, out, pl.BlockSpec((S, HEAD_DIM), lambda h: (0, first + h))],
        out_shape=[jax.ShapeDtypeStruct((S, W), F32), jax.ShapeDtypeStruct((S, W), F32),
                   jax.ShapeDtypeStruct(mixed.shape, BF16)],
        input_output_aliases={6: 2},
        scratch_shapes=[pltpu.VMEM((S, HEAD_DIM), F32)] * (3 + 2 * len(DILATIONS)),
        compiler_params=_params("parallel"), name=name)(proj, proj, proj, cos2, sin_signed, gain, mixed)


def _dil_bwd(proj, cos2, sin_signed, gain, o_raw, lse, dmixed, dm_col0, col0, n_heads, name):
    S = proj.shape[0]
    H, B = n_heads, KEY_BLOCK
    scale = HEAD_DIM ** -0.5
    rc = _tile(S, 256, 8)

    def body(q_ref, k_ref, v_ref, c_ref, s_ref, g_ref, o_ref, l_ref, dm_ref, dq_ref, dk_ref, dv_ref, dg_ref,
             qr, kr, vf, dos, dsum, dqr, dkr, dvv):
        dg_ref[...] = jnp.zeros_like(dg_ref)

        def prep(t, carry):
            rows = pl.ds(pl.multiple_of(t * rc, rc), rc)
            qr[rows, :] = _rope(q_ref[rows, :].astype(F32), c_ref[rows, :], s_ref[rows, :])
            kr[rows, :] = _rope(k_ref[rows, :].astype(F32), c_ref[rows, :], s_ref[rows, :])
            vf[rows, :] = v_ref[rows, :].astype(F32)
            o, dm = o_ref[rows, :], dm_ref[rows, :].astype(F32)
            r = _rms_scale(o)
            do = _rms_bwd(dm * g_ref[...], o, r)
            dg_ref[...] += jnp.broadcast_to(jnp.sum(dm * o * r, axis=0, keepdims=True), dg_ref.shape)
            dos[rows, :] = do
            dsum[rows, :] = jnp.broadcast_to(jnp.sum(do * o, axis=1, keepdims=True), (rc, LANES))
            dqr[rows, :] = jnp.zeros((rc, HEAD_DIM), F32)
            dkr[rows, :] = jnp.zeros((rc, HEAD_DIM), F32)
            dvv[rows, :] = jnp.zeros((rc, HEAD_DIM), F32)
            return carry

        lax.fori_loop(0, S // rc, prep, 0)

        def visit(blocks):
            products = []
            for b, d, r, l0, first in blocks:
                qrows = _dil_rows(d, r, l0, B)
                krows = qrows if first else _dil_rows(d, r, l0 - B, 2 * B)
                qs, ks = qr[qrows, :].astype(BF16), kr[krows, :].astype(BF16)
                do = dos[qrows, :].astype(BF16)
                s = jnp.where(_dil_mask(first), _dot(qs, ks, NT) * scale, NEG)
                dp = _dot(do, vf[krows, :].astype(BF16), NT)
                products.append((qrows, krows, qs, ks, do, s, dp))
            cotangents = []
            for qrows, krows, qs, ks, do, s, dp in products:
                p = jnp.exp(s - l_ref[qrows, :][:, 0:1])
                ds = (p * (dp - dsum[qrows, :][:, 0:1]) * scale).astype(BF16)
                cotangents.append((qrows, krows, qs, ks, do, p.astype(BF16), ds))
            for qrows, krows, qs, ks, do, p, ds in cotangents:
                dqr[qrows, :] += _dot(ds, ks, NN)
                dkr[krows, :] += _dot(ds, qs, TN)
                dvv[krows, :] += _dot(p, do, TN)

        _dil_blocks(S, visit)

        def finish(t, carry):
            rows = pl.ds(pl.multiple_of(t * rc, rc), rc)
            c, s = c_ref[rows, :], s_ref[rows, :]
            dq, dk = dqr[rows, :], dkr[rows, :]
            dq_ref[rows, :] = (dq * c + pltpu.roll(dq * s, HEAD_DIM // 2, axis=1)).astype(BF16)
            dk_ref[rows, :] = (dk * c + pltpu.roll(dk * s, HEAD_DIM // 2, axis=1)).astype(BF16)
            dv_ref[rows, :] = dvv[rows, :].astype(BF16)
            return carry

        lax.fori_loop(0, S // rc, finish, 0)

    def col(k):
        return pl.BlockSpec((S, HEAD_DIM), lambda h: (0, col0 + k * H + h))

    tab = pl.BlockSpec((S, HEAD_DIM), lambda h: (0, 0))
    out = pl.BlockSpec((S, HEAD_DIM), lambda h: (0, h))
    W = H * HEAD_DIM
    big = pltpu.VMEM((S, HEAD_DIM), F32)
    return _pcall(
        body, grid=(H,),
        in_specs=[col(0), col(1), col(2), tab, tab, pl.BlockSpec((1, HEAD_DIM), lambda h: (0, h)), out, out,
                  pl.BlockSpec((S, HEAD_DIM), lambda h: (0, dm_col0 + h))],
        out_specs=[out, out, out, pl.BlockSpec((8, HEAD_DIM), lambda h: (0, h))],
        out_shape=[jax.ShapeDtypeStruct((S, W), BF16), jax.ShapeDtypeStruct((S, W), BF16),
                   jax.ShapeDtypeStruct((S, W), BF16), jax.ShapeDtypeStruct((8, W), F32)],
        scratch_shapes=[big, big, big, big, pltpu.VMEM((S, LANES), F32), big, big, big],
        compiler_params=_params("parallel"), name=name)(proj, proj, proj, cos2, sin_signed, gain, o_raw, lse, dmixed)


GELU_C = math.sqrt(2.0 / math.pi)
GELU_A = 0.044715
HALO = 16


def _shifts_down(cur, halo):
    row = lax.broadcasted_iota(jnp.int32, cur.shape, 0)
    first, second = row == 0, row == 1
    last, before_last = halo[HALO - 1:HALO, :], halo[HALO - 2:HALO - 1, :]
    two = jnp.where(first, before_last, jnp.where(second, last, pltpu.roll(cur, 2, axis=0)))
    return two, jnp.where(first, last, pltpu.roll(cur, 1, axis=0))


def _shift_up(cur, halo, k):
    n = cur.shape[0]
    out = pltpu.roll(cur, n - k, axis=0)
    row = lax.broadcasted_iota(jnp.int32, cur.shape, 0)
    for t in range(k):
        out = jnp.where(row == n - k + t, halo[t:t + 1, :], out)
    return out


def _conv3(cur, halo, cw):
    rows = (*_shifts_down(cur, halo), cur)
    return rows[0] * cw[0:1, :] + rows[1] * cw[1:2, :] + cur * cw[2:3, :] + cw[3:4, :], rows


def _gelu_parts(x):
    xx = x * x
    t = jnp.tanh(x * (GELU_C + (GELU_C * GELU_A) * xx))
    half = 0.5 * x
    return half + half * t, t, xx, half


def _gelu_slope(t, xx, half):
    return (0.5 + 0.5 * t) + half * (1.0 - t * t) * (GELU_C + (3.0 * GELU_C * GELU_A) * xx)


def _geglu_specs(tm, tn, ncb):
    hb = tm // HALO

    def cur(off):
        return pl.BlockSpec((tm, tn), lambda j, i: (i, off + j))

    def prev(off):
        return pl.BlockSpec((HALO, tn), lambda j, i: (jnp.maximum(i * hb - 1, 0), off + j))

    def taps(off):
        return pl.BlockSpec((8, tn), lambda j, i: (0, off + j))

    return [cur(0), prev(0), cur(ncb), prev(ncb), taps(0), taps(ncb)]


def _geglu_fwd(u, cwb, name, tm=512, tn=1408):
    S, F2 = u.shape
    F = F2 // 2
    tm, tn = _tile(S, tm, HALO), _tile(F, tn)
    ncb = F // tn

    def body(g_ref, gp_ref, v_ref, vp_ref, cg_ref, cv_ref, y_ref):
        top = pl.program_id(1) > 0
        gp = jnp.where(top, gp_ref[...].astype(F32), 0.0)
        vp = jnp.where(top, vp_ref[...].astype(F32), 0.0)
        gc = _conv3(g_ref[...].astype(F32), gp, cg_ref[...])[0]
        vc = _conv3(v_ref[...].astype(F32), vp, cv_ref[...])[0]
        y_ref[...] = (_gelu_parts(gc)[0] * vc).astype(BF16)

    return _pcall(body, grid=(ncb, S // tm), in_specs=_geglu_specs(tm, tn, ncb),
                  out_specs=pl.BlockSpec((tm, tn), lambda j, i: (i, j)),
                  out_shape=jax.ShapeDtypeStruct((S, F), BF16),
                  compiler_params=_params("parallel", "parallel"), name=name)(u, u, u, u, cwb, cwb)


def _geglu_bwd(u, dy, cwb, name, tm=256, tn=1408):
    S, F2 = u.shape
    F = F2 // 2
    tm, tn = _tile(S, tm, HALO), _tile(F, tn)
    ncb = F // tn

    def body(g_ref, gp_ref, v_ref, vp_ref, cg_ref, cv_ref, dy_ref, dc_ref, dwg_ref, dwv_ref):
        i = pl.program_id(1)

        @pl.when(i == 0)
        def _():
            dwg_ref[...] = jnp.zeros_like(dwg_ref)
            dwv_ref[...] = jnp.zeros_like(dwv_ref)

        top = i > 0
        g, v = g_ref[...].astype(F32), v_ref[...].astype(F32)
        gp = jnp.where(top, gp_ref[...].astype(F32), 0.0)
        vp = jnp.where(top, vp_ref[...].astype(F32), 0.0)
        gc, g_rows = _conv3(g, gp, cg_ref[...])
        vc, v_rows = _conv3(v, vp, cv_ref[...])
        act, t, xx, half = _gelu_parts(gc)
        dact = _gelu_slope(t, xx, half)
        dyv = dy_ref[...].astype(F32)
        dgc = dyv * vc * dact
        dvc = dyv * act
        dc_ref[0] = dgc.astype(BF16)
        dc_ref[1] = dvc.astype(BF16)

        def taps(out_ref, dc, rows):
            for k, moved in enumerate(rows):
                out_ref[k:k + 1, :] += jnp.sum(dc * moved, axis=0, keepdims=True)
            out_ref[3:4, :] += jnp.sum(dc, axis=0, keepdims=True)

        taps(dwg_ref, dgc, g_rows)
        taps(dwv_ref, dvc, v_rows)

    return _pcall(body, grid=(ncb, S // tm),
                  in_specs=_geglu_specs(tm, tn, ncb) + [pl.BlockSpec((tm, tn), lambda j, i: (i, j))],
                  out_specs=[pl.BlockSpec((2, tm, tn), lambda j, i: (0, i, j)),
                             pl.BlockSpec((8, tn), lambda j, i: (0, j)), pl.BlockSpec((8, tn), lambda j, i: (0, j))],
                  out_shape=[jax.ShapeDtypeStruct((2, S, F), BF16), jax.ShapeDtypeStruct((8, F), F32),
                             jax.ShapeDtypeStruct((8, F), F32)],
                  compiler_params=_params("parallel", "arbitrary"), name=name)(u, u, u, u, cwb, cwb, dy)


def _conv_bwd(dc, cwb, name, tm=512, tn=1408):
    _, S, F = dc.shape
    tm, tn = _tile(S, tm, HALO), _tile(F, tn)
    ncb, nrb = F // tn, S // tm
    hb = tm // HALO

    def body(c_ref, n_ref, w_ref, du_ref):
        cur = c_ref[...].astype(F32)
        nxt = jnp.where(pl.program_id(2) < nrb - 1, n_ref[...].astype(F32), 0.0)
        w = w_ref[...]
        du = cur * w[2:3, :] + _shift_up(cur, nxt, 1) * w[1:2, :] + _shift_up(cur, nxt, 2) * w[0:1, :]
        du_ref[...] = du.astype(BF16)

    return _pcall(body, grid=(2, ncb, nrb),
                  in_specs=[pl.BlockSpec((None, tm, tn), lambda c, j, i: (c, i, j)),
                            pl.BlockSpec((None, HALO, tn), lambda c, j, i: (c, jnp.minimum((i + 1) * hb, S // HALO - 1), j)),
                            pl.BlockSpec((8, tn), lambda c, j, i: (0, c * ncb + j))],
                  out_specs=pl.BlockSpec((tm, tn), lambda c, j, i: (i, c * ncb + j)),
                  out_shape=jax.ShapeDtypeStruct((S, 2 * F), BF16),
                  compiler_params=_params("parallel", "parallel", "parallel"), name=name)(dc, dc, cwb)


def _adam_math(w, g, m, v):
    m = ADAM_B1 * m + (1.0 - ADAM_B1) * g
    v = ADAM_B2 * v + (1.0 - ADAM_B2) * (g * g)
    m_hat = m / (1.0 - ADAM_B1 ** ADAM_STEP)
    v_hat = v / (1.0 - ADAM_B2 ** ADAM_STEP)
    return -ADAM_LR * (m_hat / (jnp.sqrt(v_hat) + ADAM_EPS) + ADAM_WD * w), m, v


def _adamw(w, parts, m, v, name, tr=256):
    R, C = w.shape
    n, _, Cp = parts.shape
    tr = _tile(R, tr, 8)

    def body(w_ref, p_ref, m_ref, v_ref, g_out, d_out, m_out, v_out):
        g = p_ref[0, :, 0:C].astype(F32)
        for k in range(1, n):
            g = g + p_ref[k, :, 0:C].astype(F32)
        d, mn, vn = _adam_math(w_ref[...], g, m_ref[...], v_ref[...])
        g_out[...] = g
        d_out[...] = d
        m_out[...] = mn
        v_out[...] = vn

    spec = pl.BlockSpec((tr, C), lambda i: (i, 0))
    shape = jax.ShapeDtypeStruct((R, C), F32)
    return _pcall(body, grid=(R // tr,), in_specs=[spec, pl.BlockSpec((n, tr, Cp), lambda i: (0, i, 0)), spec, spec],
                  out_specs=[spec] * 4, out_shape=[shape] * 4, compiler_params=_params("parallel"), name=name)(w, parts, m, v)


def _adamw_chips(w, pair, parts, chip_ids, m, v, name, tr=256):
    R, C = w.shape
    Cp = pair.shape[2]
    by_columns = C == Cp and _tile(R, tr, 16) < 64
    tr, tc = (R, _tile(C, 256)) if by_columns else (_tile(R, tr, 16), C)

    def body(ids_ref, w_ref, own_ref, p1_ref, p2_ref, p3_ref, m_ref, v_ref, g_out, d_out, m_out, v_out):
        g = own_ref[:, 0:tc].astype(F32)
        for ref in (p1_ref, p2_ref, p3_ref):
            g = g + ref[:, 0:tc].astype(F32)
        d, mn, vn = _adam_math(w_ref[...], g, m_ref[...], v_ref[...])
        g_out[...] = g
        d_out[...] = d
        m_out[...] = mn
        v_out[...] = vn

    if by_columns:
        spec = pl.BlockSpec((tr, tc), lambda j, ids: (0, j))
    else:
        spec = pl.BlockSpec((tr, tc), lambda i, ids: (i, 0))

    def chip(k):
        if by_columns:
            return pl.BlockSpec((None, tr, tc), lambda j, ids: (ids[k], 0, j))
        return pl.BlockSpec((None, tr, Cp), lambda i, ids: (ids[k], i, 0))

    shape = jax.ShapeDtypeStruct((R, C), F32)
    grid_spec = pltpu.PrefetchScalarGridSpec(
        num_scalar_prefetch=1, grid=(C // tc if by_columns else R // tr,),
        in_specs=[spec, chip(0), chip(1), chip(2), chip(3), spec, spec], out_specs=[spec] * 4)
    return _pcall(body, grid_spec=grid_spec, out_shape=[shape] * 4, compiler_params=_params("parallel"),
                  name=name)(chip_ids, w, pair, parts, parts, parts, m, v)


def _place():
    return lax.axis_index("x"), lax.axis_index("y"), lax.axis_index("c")


def _other_chips(x, y):
    return [(1 - x, y), (x, 1 - y), (1 - x, 1 - y)]


IN_HBM = pl.BlockSpec(memory_space=pltpu.HBM)
SEM = pl.BlockSpec(memory_space=pltpu.SEMAPHORE)
EFFECT = pltpu.SideEffectType.DATAFLOW_SIDE_EFFECTING
TOKEN = jax.ShapeDtypeStruct((8, LANES), F32)
TOKEN_SPEC = pl.BlockSpec(memory_space=pltpu.VMEM)


def _in_hbm(a):
    return pltpu.with_memory_space_constraint(a, pltpu.HBM)


def _landing(shape):
    return _in_hbm(lax.empty(shape.shape, shape.dtype))


def _hbm_like(a):
    return pltpu.HBM(a.shape, a.dtype)


def _gather_places():
    x, y, c = _place()
    relay_from = (c * (1 - x) + (1 - c) * x, c * y + (1 - c) * (1 - y), c)
    relay_to = (c * x + (1 - c) * (1 - x), c * (1 - y) + (1 - c) * y, c)
    return (x, y, c), (x, y, 1 - c), (1 - x, y, c), (x, 1 - y, c), (1 - x, 1 - y, c), relay_from, relay_to


def _slot_copy(slot, ref, src, dst, send_sem, recv_sem, to):
    return pltpu.make_async_remote_copy(src_ref=slot(ref, *src), dst_ref=slot(ref, *dst), send_sem=send_sem,
                                        recv_sem=recv_sem, device_id=to, device_id_type=MESH)


def _split_call(body, arrays, sems_in, sems_out, after, name, token=True):
    na, ni, no = len(arrays), len(sems_in), len(sems_out)

    def wrapped(*refs):
        body(refs[:na], refs[na:na + ni], refs[na + ni + 1:na + ni + 1 + no])
        if token:
            refs[-1][...] = jnp.zeros_like(refs[-1])

    outs = _pcall(
        wrapped, in_specs=[IN_HBM] * na + [SEM] * ni + [HBM],
        out_specs=[SEM] * no + [IN_HBM] * na + ([TOKEN_SPEC] if token else []),
        out_shape=[pltpu.SemaphoreType.DMA((n,)) for n in sems_out] + [_hbm_like(s) for s in arrays] + ([TOKEN] if token else []),
        input_output_aliases={a: no + a for a in range(na)},
        compiler_params=pltpu.CompilerParams(has_side_effects=EFFECT), name=name,
    )(*[_in_hbm(s) for s in arrays], *sems_in, after)
    return list(outs[:no]), list(outs[no:no + na]), (outs[-1] if token else None)


def _gather_start(landing, slots, after, name):
    na = len(landing)

    def body(land, _, sems):
        me, sib, xn, yn, _, _, _ = _gather_places()
        for a in range(na):
            for k, to in enumerate((sib, xn, yn)):
                _slot_copy(slots[a], land[a], me, me, sems[0].at[3 * a + k], sems[1].at[3 * a + k], to).start()

    return _split_call(body, landing, [], [3 * na, 3 * na], after, name)


def _gather_relay(gathered, sems1, slots, after, name):
    na = len(gathered)

    def body(gath, taken, given):
        me, sib, xn, yn, _, relay_from, relay_to = _gather_places()
        for a in range(na):
            for k, peer in enumerate((sib, xn, yn)):
                arrival = _slot_copy(slots[a], gath[a], me, peer, taken[0].at[3 * a + k], taken[1].at[3 * a + k], peer)
                arrival.wait_send()
                arrival.wait_recv()
        for a in range(na):
            _slot_copy(slots[a], gath[a], relay_from, relay_from, given[0].at[a], given[1].at[a], relay_to).start()
            for k, peer in enumerate((xn, yn)):
                _slot_copy(slots[a], gath[a], peer, peer, given[2].at[2 * a + k], given[3].at[2 * a + k], sib).start()

    return _split_call(body, gathered, sems1, [na, na, 2 * na, 2 * na], after, name)


def _gather_pass(gathered, relay_sems, slots, after, name):
    na = len(gathered)

    def body(gath, taken, given):
        me, sib, xn, yn, diag, relay_from, relay_to = _gather_places()
        for a in range(na):
            _slot_copy(slots[a], gath[a], relay_from, relay_from, taken[0].at[a], taken[1].at[a], relay_to).wait_send()
            _slot_copy(slots[a], gath[a], me, diag, taken[0].at[a], taken[1].at[a], relay_to).wait_recv()
        for a in range(na):
            _slot_copy(slots[a], gath[a], diag, diag, given[0].at[a], given[1].at[a], sib).start()

    return _split_call(body, gathered, relay_sems, [na, na], after, name)


def _gather_finish(gathered, pass_sems, diag_sems, slots, after, name):
    na = len(gathered)

    def body(gath, taken, _):
        (x, y, c), sib, xn, yn, diag, _, _ = _gather_places()
        for a in range(na):
            for k, peer in enumerate((xn, yn)):
                passed = _slot_copy(slots[a], gath[a], peer, (peer[0], peer[1], 1 - c), taken[0].at[2 * a + k],
                                    taken[1].at[2 * a + k], sib)
                passed.wait_send()
                passed.wait_recv()
            passed = _slot_copy(slots[a], gath[a], diag, (diag[0], diag[1], 1 - c), taken[2].at[a], taken[3].at[a], sib)
            passed.wait_send()
            passed.wait_recv()

    return _split_call(body, gathered, list(pass_sems) + list(diag_sems), [], after, name, token=False)[1]


def _pair_copy(view, src, land, send_sems, recv_sems, chip):
    x, y, c = _place()
    return pltpu.make_async_remote_copy(
        src_ref=view(src, chip, 1 - c), dst_ref=land.at[chip], send_sem=send_sems.at[chip], recv_sem=recv_sems.at[chip],
        device_id=(x, y, 1 - c), device_id_type=MESH)


def _pair_start(grad, view, block, after, name):
    def body(src, land, after_ref, send_sems, recv_sems, src_thru, land_thru, token):
        for chip in range(N_CHIP):
            _pair_copy(view, src, land, send_sems, recv_sems, chip).start()
        token[...] = jnp.zeros_like(token)

    sems = pltpu.SemaphoreType.DMA((N_CHIP,))
    land = jax.ShapeDtypeStruct((N_CHIP, *block), BF16)
    return _pcall(
        body, in_specs=[IN_HBM, IN_HBM, HBM], out_specs=[SEM, SEM, IN_HBM, IN_HBM, TOKEN_SPEC],
        out_shape=[sems, sems, _hbm_like(grad), _hbm_like(land), TOKEN], input_output_aliases={0: 2, 1: 3},
        compiler_params=pltpu.CompilerParams(has_side_effects=EFFECT), name=name,
    )(_in_hbm(grad), _landing(land), after)


def _pair_wait(grad, recv, send_sems, recv_sems, view, after, name):
    def body(src, land, send, recv_s, after_ref, src_thru, land_thru):
        for chip in range(N_CHIP):
            copy = _pair_copy(view, src, land, send, recv_s, chip)
            copy.wait_send()
            copy.wait_recv()

    return _pcall(
        body, in_specs=[IN_HBM, IN_HBM, SEM, SEM, HBM], out_specs=[IN_HBM, IN_HBM],
        out_shape=[_hbm_like(grad), _hbm_like(recv)], input_output_aliases={0: 0, 1: 1},
        compiler_params=pltpu.CompilerParams(has_side_effects=EFFECT), name=name,
    )(grad, recv, send_sems, recv_sems, after)


def _chip_start(pair, after, name):
    def body(src, land, after_ref, send_sems, recv_sems, src_thru, land_thru, token):
        x, y, c = _place()
        for j, (px, py) in enumerate(_other_chips(x, y)):
            pltpu.make_async_remote_copy(
                src_ref=src.at[2 * px + py], dst_ref=land.at[2 * x + y], send_sem=send_sems.at[j], recv_sem=recv_sems.at[j],
                device_id=(px, py, c), device_id_type=MESH).start()
        token[...] = jnp.zeros_like(token)

    sems = pltpu.SemaphoreType.DMA((3,))
    return _pcall(
        body, in_specs=[IN_HBM, IN_HBM, HBM], out_specs=[SEM, SEM, IN_HBM, IN_HBM, TOKEN_SPEC],
        out_shape=[sems, sems, _hbm_like(pair), _hbm_like(pair), TOKEN], input_output_aliases={0: 2, 1: 3},
        compiler_params=pltpu.CompilerParams(has_side_effects=EFFECT), name=name,
    )(_in_hbm(pair), _landing(pair), after)


def _chip_wait(pair, parts, send_sems, recv_sems, after, name):
    def body(src, land, send, recv, after_ref, src_thru, land_thru):
        x, y, c = _place()
        for j, (px, py) in enumerate(_other_chips(x, y)):
            copy = pltpu.make_async_remote_copy(
                src_ref=src.at[2 * px + py], dst_ref=land.at[2 * px + py], send_sem=send.at[j], recv_sem=recv.at[j],
                device_id=(px, py, c), device_id_type=MESH)
            copy.wait_send()
            copy.wait_recv()

    return _pcall(
        body, in_specs=[IN_HBM, IN_HBM, SEM, SEM, HBM], out_specs=[IN_HBM, IN_HBM],
        out_shape=[_hbm_like(pair), _hbm_like(parts)], input_output_aliases={0: 0, 1: 1},
        compiler_params=pltpu.CompilerParams(has_side_effects=EFFECT), name=name,
    )(pair, parts, send_sems, recv_sems, after)


def _pair_add(core, grad, recv, block, grad_spec, name):
    _, R, C = recv.shape
    tr = block

    def body(c_ref, g_ref, r_ref, o_ref):
        o_ref[...] = (g_ref[...].astype(F32) + r_ref[...].astype(F32)).astype(BF16)

    grid_spec = pltpu.PrefetchScalarGridSpec(
        num_scalar_prefetch=1, grid=(N_CHIP, R // tr),
        in_specs=[grad_spec, pl.BlockSpec((None, tr, C), lambda k, i, c: (k, i, 0))],
        out_specs=pl.BlockSpec((None, tr, C), lambda k, i, c: (k, i, 0)))
    return _pcall(body, grid_spec=grid_spec, out_shape=jax.ShapeDtypeStruct(recv.shape, BF16),
                  compiler_params=_params("parallel", "parallel"), name=name)(core, grad, recv)


def _small_copies(gath, send_sems, recv_sems):
    x, y, c = _place()
    peers = [(x, y, 1 - c)] + [(px, py, pc) for px, py in _other_chips(x, y) for pc in (c, 1 - c)]
    pairs = []
    for a, ref in enumerate(gath):
        mine = ref.at[4 * x + 2 * y + c]
        for k, (px, py, pc) in enumerate(peers):
            sems = dict(send_sem=send_sems.at[7 * a + k], recv_sem=recv_sems.at[7 * a + k], device_id=(px, py, pc),
                        device_id_type=MESH)
            pairs.append((pltpu.make_async_remote_copy(src_ref=mine, dst_ref=mine, **sems),
                          pltpu.make_async_remote_copy(src_ref=mine, dst_ref=ref.at[4 * px + 2 * py + pc], **sems)))
    return pairs


def _small_start(landing, after, name):
    na = len(landing)

    def body(*refs):
        for send, _ in _small_copies(refs[:na], refs[na + 1], refs[na + 2]):
            send.start()
        refs[-1][...] = jnp.zeros_like(refs[-1])

    sems = pltpu.SemaphoreType.DMA((7 * na,))
    outs = _pcall(
        body, in_specs=[IN_HBM] * na + [HBM], out_specs=[SEM, SEM] + [IN_HBM] * na + [TOKEN_SPEC],
        out_shape=[sems, sems] + [_hbm_like(s) for s in landing] + [TOKEN],
        input_output_aliases={a: 2 + a for a in range(na)},
        compiler_params=pltpu.CompilerParams(has_side_effects=EFFECT), name=name,
    )(*[_in_hbm(s) for s in landing], after)
    return outs[0], outs[1], outs[2:2 + na], outs[-1]


def _small_wait(gathered, send_sems, recv_sems, after, name):
    na = len(gathered)

    def body(*refs):
        for send, arrival in _small_copies(refs[:na], refs[na], refs[na + 1]):
            send.wait_send()
            arrival.wait_recv()

    return list(_pcall(
        body, in_specs=[IN_HBM] * na + [SEM, SEM, HBM], out_specs=[IN_HBM] * na,
        out_shape=[_hbm_like(g) for g in gathered], input_output_aliases={a: a for a in range(na)},
        compiler_params=pltpu.CompilerParams(has_side_effects=EFFECT), name=name,
    )(*gathered, send_sems, recv_sems, after))


def _small_finish(gathered, params, name):
    na, npar = len(gathered), len(params)

    def body(*refs):
        g_refs, wmv = refs[:na], refs[na:na + 3 * npar]
        o_sums, o_params = refs[na + 3 * npar:2 * na + 3 * npar], refs[2 * na + 3 * npar:]
        sums = []
        for a in range(na):
            acc = g_refs[a][0]
            for k in range(1, N_DEV):
                acc = acc + g_refs[a][k]
            o_sums[a][...] = acc
            sums.append(acc)
        for j, (a, row, _, _, _) in enumerate(params):
            g = sums[a][row:row + 1, :]
            d, mn, vn = _adam_math(wmv[3 * j][...], g, wmv[3 * j + 1][...], wmv[3 * j + 2][...])
            for out, val in zip(o_params[4 * j:4 * j + 4], (g, d, mn, vn)):
                out[...] = val

    vm = pl.BlockSpec(memory_space=pltpu.VMEM)
    flat = [t for p in params for t in p[2:]]
    out_shape = [jax.ShapeDtypeStruct(g.shape[1:], F32) for g in gathered]
    out_shape += [jax.ShapeDtypeStruct(p[2].shape, F32) for p in params for _ in range(4)]
    outs = _pcall(body, in_specs=[vm] * (na + 3 * npar), out_specs=[vm] * len(out_shape), out_shape=out_shape,
                  name=name)(*gathered, *flat)
    return outs[:na], [outs[na + 4 * j:na + 4 * j + 4] for j in range(npar)]


def _local_step(x, tgt, gains, weights):
    g_pre_mix, g_post_mix, g_pre_ffn, g_post_ffn, g_sb, g_dil = gains
    S, D = x.shape
    hs = g_sb.shape[1] // HEAD_DIM
    hd = g_dil.shape[1] // HEAD_DIM
    cos2, sin_signed = _rope_tables(S)

    h1 = _rms_fwd(x, g_pre_mix + weights.start(), "rms_in")
    w_in_g = weights.w_in(h1)
    proj = _mm_nn(h1, w_in_g, BF16, "proj", tn=768)
    o_sb, ct_sb, mixed = _sb_fwd(proj, g_sb + weights.relay_out(proj), hs, hs + hd, "sb_fwd")
    o_dl, lse_dl, mixed = _dil_fwd(proj, cos2, sin_signed, g_dil + weights.after_sb(o_sb), mixed, 3 * hs, hd, "dil_fwd")
    w_out_g = weights.w_out(o_dl)
    mix = _mm_nn(mixed, w_out_g, F32, "mix_out", tn=1024)
    x2, h2 = _mid_fwd(x, mix, g_post_mix + weights.after_mix(mix), g_pre_ffn, "mid_fwd")
    w_up_g, cwb = weights.w_up(h2)
    u = _mm_nn(h2, w_up_g, BF16, "ffn_up", b_transposed=True)
    y = _geglu_fwd(u, cwb + weights.forward_down(u), "geglu_fwd")
    w_down_g = weights.w_down(y)
    f = _mm_nn(y, w_down_g, F32, "ffn_down", tn=1024, tk=2816)

    dy, df, dg_post_ffn, loss = _loss_bwd(x2, f, tgt, g_post_ffn, "loss_bwd")
    dyv = _mm_nt(df, w_down_g, BF16, "d_y", tn=1408)
    dw_down = _mm_tn(y, df, D, BF16, "dw_down", tm=1408, tn=1024)
    dc, dcw_g, dcw_v = _geglu_bwd(u, dyv, cwb + weights.grad("w_down", dw_down), "geglu_bwd")
    du = _conv_bwd(dc, cwb + weights.grad_reduce("w_down", dc), "conv_bwd")
    dh2 = _mm_nt(du, w_up_g, BF16, "d_h2", tk=1408, b_transposed=True, per_step=2)
    dw_up = _mm_tn(du, h2, D, BF16, "dw_up", tm=1408, tn=1024)
    dx2, dmix, dg_pre_ffn, dg_post_mix = _mid_bwd(
        dy, dh2, x2, mix, g_pre_ffn + weights.grad("w_up", dw_up), g_post_mix, "mid_bwd")
    dmixed = _mm_nt(dmix, w_out_g, BF16, "d_mixed", after=jnp.reshape(weights.grad_reduce("w_up", dmix), (1, 1)))
    dw_out = _mm_tn(mixed, dmix, D, BF16, "dw_out", tn=1024)
    dq_s, dk_s, dv_s, dg_sb = _sb_bwd(proj, g_sb + weights.grad("w_out", dw_out), o_sb, ct_sb, dmixed, 0, hs, "sb_bwd")
    dq_d, dk_d, dv_d, dg_dil = _dil_bwd(proj, cos2, sin_signed, g_dil + weights.grad_reduce("w_out", dq_s), o_dl, lse_dl,
                                        dmixed, hs, 3 * hs, hd, "dil_bwd")
    dproj = jnp.concatenate([dq_s, dk_s, dv_s, dq_d, dk_d, dv_d], axis=1)
    dw_in = _mm_tn(h1, dproj, w_in_g.shape[2], BF16, "dw_in", tn=768)
    weights.grad("w_in", dw_in)
    dep = weights.grad_reduce("w_in", dproj)
    dh1 = _mm_nt(dproj, w_in_g, BF16, "d_h1", tk=768, after=jnp.reshape(dep, (1, 1)), per_step=4)
    grad_x, dg_pre_mix = _first_bwd(dx2, dh1, x, g_pre_mix, "first_bwd")
    small = (dg_pre_mix, dg_post_mix, dg_pre_ffn, dg_post_ffn, dg_sb[0:1], dg_dil[0:1], jnp.concatenate([dcw_g, dcw_v], axis=1))
    weights.small(small, loss)
    return loss, grad_x, small


def _pad_cols(a, to):
    return jnp.pad(a, ((0, 0), (0, to - a.shape[1])))


def kernel(x, pre_mix_gain, post_mix_gain, pre_ffn_gain, post_ffn_gain, w_in, sb_out_gain, dil_out_gain, w_out, w_up, conv_w, conv_b, w_down, loss_target, m_pre_mix_gain, m_post_mix_gain, m_pre_ffn_gain, m_post_ffn_gain, m_w_in, m_sb_out_gain, m_dil_out_gain, m_w_out, m_w_up, m_conv_w, m_conv_b, m_w_down, v_pre_mix_gain, v_post_mix_gain, v_pre_ffn_gain, v_post_ffn_gain, v_w_in, v_sb_out_gain, v_dil_out_gain, v_w_out, v_w_up, v_conv_w, v_conv_b, v_w_down):
    xb, tb = x[0], loss_target[0]
    S, D = xb.shape
    w_in, w_out, w_up, w_down, conv_w = w_in[0], w_out[0], w_up[0], w_down[0], conv_w[0]
    n_in, e_rows = w_in.shape[1], w_out.shape[0]
    cu, half = w_up.shape[1], w_down.shape[0]
    assert cu == 2 * half and half % 16 == 0
    cup = -(-cu // LANES) * LANES
    fp = N_CHIP * cup
    px, py, pc = _place()
    me = 4 * px + 2 * py + pc
    core = jnp.reshape(pc, (1,)).astype(jnp.int32)

    w_up_t, m_up_t, v_up_t = (jnp.swapaxes(t, 0, 1) for t in (w_up, m_w_up[0], v_w_up[0]))

    def by_dev(ref, qx, qy, qc):
        return ref.at[4 * qx + 2 * qy + qc]

    def down_slot(ref, qx, qy, qc):
        return ref.at[2 * qx + qy, pl.ds(qc * half, half)]

    def by_pair(ref, chip, k):
        return ref.at[chip, k]

    def down_pair(ref, chip, k):
        return ref.at[chip, pl.ds(k * half, half)]

    def pair_spec(tr, cols):
        return pl.BlockSpec((None, None, tr, cols), lambda k, i, c: (k, c[0], i, 0))

    tr_in, tr_up = _tile(D, 512, 16), _tile(cup, 256, 16)
    grad_plan = {
        "w_in": ((N_CHIP, 2, D, n_in), by_pair, (D, n_in), tr_in, pair_spec(tr_in, n_in)),
        "w_out": ((N_CHIP, 2, e_rows, D), by_pair, (e_rows, D), e_rows, pair_spec(e_rows, D)),
        "w_up": ((N_CHIP, 2, cup, D), by_pair, (cup, D), tr_up, pair_spec(tr_up, D)),
        "w_down": ((N_CHIP, cup, D), down_pair, (half, D), half,
                   pl.BlockSpec((None, half, D), lambda k, i, c: (k, c[0], 0))),
    }

    class Exchanges:
        def __init__(self):
            self.in_flight = {}

        def start(self):
            def own_slot(shard):
                return lax.dynamic_update_index_in_dim(lax.empty((N_DEV, *shard.shape), shard.dtype), shard, me, 0)

            self.group_slots = {"in": [by_dev], "out": [by_dev], "up": [by_dev, by_dev], "down": [down_slot]}
            self.flight = {}
            sems, gath, token = _gather_start([own_slot(w_in.astype(BF16))], [by_dev], core, "gather_in_start")
            self.flight["in"] = (sems, gath)
            zero = token[0, 0]
            self.landing = {
                "out": [own_slot((w_out + zero).astype(BF16))],
                "up": [own_slot(jnp.pad(w_up_t + zero, ((0, cup - cu), (0, 0))).astype(BF16)),
                       own_slot(jnp.pad(conv_w + zero, ((0, 8 - conv_w.shape[0]), (0, cup - cu))))],
                "down": [lax.dynamic_update_slice(jnp.zeros((N_CHIP, cup, D), BF16), (w_down + zero).astype(BF16)[None],
                                                  (2 * px + py, pc * half, 0))]}
            return zero

        def begin(self, group, after):
            sems, gath, token = _gather_start(self.landing[group], self.group_slots[group], after, "gather_%s_start" % group)
            self.flight[group] = (sems, gath)
            return token

        def relay(self, group, after):
            sems, gath = self.flight[group]
            sems, gath, token = _gather_relay(gath, sems, self.group_slots[group], after, "gather_%s_relay" % group)
            self.flight[group] = (sems, gath)
            return token

        def pass_on(self, group, after):
            sems, gath = self.flight[group]
            diag_sems, gath, token = _gather_pass(gath, sems[:2], self.group_slots[group], after, "gather_%s_pass" % group)
            self.flight[group] = (sems[2:], diag_sems, gath)
            return token

        def finish(self, group, after):
            pass_sems, diag_sems, gath = self.flight[group]
            return _gather_finish(gath, pass_sems, diag_sems, self.group_slots[group], after, "gather_%s_finish" % group)

        def w_in(self, after):
            token = self.begin("up", self.begin("out", self.relay("in", after)))
            return self.finish("in", self.pass_on("in", token))[0]

        def relay_out(self, after):
            return self.relay("out", after)[0, 0]

        def after_sb(self, after):
            return self.begin("down", self.relay("up", self.pass_on("out", after)))[0, 0]

        def w_out(self, after):
            return self.finish("out", after)[0].reshape(1, N_DEV * e_rows, D)

        def after_mix(self, after):
            return self.pass_on("up", after)[0, 0]

        def w_up(self, after):
            w_up_g, cw_g = self.finish("up", after)
            cb = _pad_cols(conv_b.reshape(N_DEV, cu), cup).reshape(1, 2 * fp)
            cw_full = jnp.transpose(cw_g[:, :3, :], (1, 0, 2)).reshape(3, 2 * fp)
            cwb = jnp.concatenate([cw_full, cb, jnp.zeros((4, 2 * fp), F32)], axis=0)
            return w_up_g, cwb

        def forward_down(self, after):
            return self.relay("down", after)[0, 0]

        def w_down(self, after):
            return self.finish("down", self.pass_on("down", after))[0].reshape(1, fp, D)

        def small(self, small, loss):
            d_pre_mix, d_post_mix, d_pre_ffn, d_post_ffn, d_sb, d_dil, d_conv = small

            def rows_of(*vectors):
                n = vectors[0].shape[1]
                row = lax.broadcasted_iota(jnp.int32, (8, n), 0)
                out = jnp.zeros((8, n), F32)
                for k, vec in enumerate(vectors):
                    out = jnp.where(row == k, vec, out)
                return out

            parts = [rows_of(d_pre_mix, d_post_mix, d_pre_ffn, d_post_ffn, jnp.broadcast_to(loss[:, :1], (1, D))),
                     rows_of(d_sb, d_dil), d_conv]
            landing = [lax.dynamic_update_index_in_dim(lax.empty((N_DEV, *p.shape), F32), p, me, 0) for p in parts]
            self.small_flight = _small_start(landing, parts[0], "small_start")

        def small_sums(self, after):
            send, recv, gath, _ = self.small_flight
            gath = _small_wait(gath, send, recv, after, "small_wait")
            params = [(0, 0, pre_mix_gain, m_pre_mix_gain, v_pre_mix_gain), (0, 1, post_mix_gain, m_post_mix_gain, v_post_mix_gain),
                      (0, 2, pre_ffn_gain, m_pre_ffn_gain, v_pre_ffn_gain), (0, 3, post_ffn_gain, m_post_ffn_gain, v_post_ffn_gain),
                      (1, 0, sb_out_gain, m_sb_out_gain, v_sb_out_gain), (1, 1, dil_out_gain, m_dil_out_gain, v_dil_out_gain)]
            (gains_sum, _, conv_sum), gain_steps = _small_finish(gath, params, "small_finish")
            return gains_sum[4, 0], conv_sum, gain_steps

        def grad(self, name, dw):
            view_shape, view, block, tr, spec = grad_plan[name]
            send, recv_sems, dw, recv, token = _pair_start(dw.reshape(view_shape), view, block, core, "pair_start_" + name)
            self.in_flight[name] = (dw, recv, send, recv_sems)
            return token[0, 0]

        def grad_reduce(self, name, after):
            _, view, _, tr, spec = grad_plan[name]
            dw, recv = _pair_wait(*self.in_flight[name], view, after, "pair_wait_" + name)
            pair = _pair_add(core, dw, recv, tr, spec, "pair_add_" + name)
            send, recv_sems, pair, parts, token = _chip_start(pair, recv, "chip_start_" + name)
            self.in_flight[name] = (pair, parts, send, recv_sems)
            self.last_token = token
            return token[0, 0]

        def grad_parts(self, name, after):
            return _chip_wait(*self.in_flight[name], after, "chip_wait_" + name)

    exchanges = Exchanges()
    gains = (pre_mix_gain, post_mix_gain, pre_ffn_gain, post_ffn_gain, sb_out_gain, dil_out_gain)
    loss, grad_x, small = _local_step(xb, tb, gains, exchanges)

    def small_adam(w, g, m, v, name):
        one = w.shape[0] == 1
        if one:
            w, g, m, v = (jnp.broadcast_to(t, (8, t.shape[1])) for t in (w, g, m, v))
        outs = _adamw(w, g[None], m, v, name)
        return [o[0:1] for o in outs] if one else outs

    chip_ids = jnp.stack([2 * px + py, 2 * (1 - px) + py, 2 * px + 1 - py, 2 * (1 - px) + 1 - py]).astype(jnp.int32)
    out_w_down = _adamw_chips(w_down, *exchanges.grad_parts("w_down", exchanges.small_flight[3]), chip_ids, m_w_down[0], v_w_down[0], "adam_w_down")
    out_up_t = _adamw_chips(w_up_t, *exchanges.grad_parts("w_up", out_w_down[1]), chip_ids, m_up_t, v_up_t, "adam_w_up")
    out_w_up = [jnp.swapaxes(o, 0, 1) for o in out_up_t]
    out_w_out = _adamw_chips(w_out, *exchanges.grad_parts("w_out", out_up_t[1]), chip_ids, m_w_out[0], v_w_out[0], "adam_w_out")
    loss_out, g_conv, gain_steps = exchanges.small_sums(out_w_out[1])
    out_pre_mix, out_post_mix, out_pre_ffn, out_post_ffn, out_sb, out_dil = gain_steps
    g_conv_b = g_conv[3].reshape(N_DEV, cup)[:, :cu].reshape(1, N_DEV * cu)
    g_conv_w = lax.dynamic_index_in_dim(g_conv[0:3].reshape(3, N_DEV, cup), me, axis=1, keepdims=False)[:, :cu]
    out_conv_b = small_adam(conv_b, g_conv_b, m_conv_b, v_conv_b, "adam_conv_b")
    cw8 = [jnp.pad(t, ((0, 5), (0, 0))) for t in (conv_w, g_conv_w, m_conv_w[0], v_conv_w[0])]
    out_conv_w = [o[0:3] for o in _adamw(cw8[0], cw8[1][None], cw8[2], cw8[3], "adam_conv_w")]
    out_w_in = _adamw_chips(w_in, *exchanges.grad_parts("w_in", out_conv_w[1]), chip_ids, m_w_in[0], v_w_in[0], "adam_w_in")

    order = [out_pre_mix, out_post_mix, out_pre_ffn, out_post_ffn, [o[None] for o in out_w_in], out_sb, out_dil,
             [o[None] for o in out_w_out], [o[None] for o in out_w_up], [o[None] for o in out_conv_w], out_conv_b,
             [o[None] for o in out_w_down]]
    outs = [loss_out, grad_x[None]]
    for k in range(4):
        outs += [o[k] for o in order]
    return tuple(outs)
```

```python
import math

import jax
import jax.numpy as jnp
from jax import lax
from jax.experimental import pallas as pl
from jax.experimental.pallas import tpu as pltpu

F32 = jnp.float32
BF16 = jnp.bfloat16
HEAD_DIM = 128
LANES = 128
KEY_BLOCK = 128
DILATIONS = (1, 4, 16)
RMS_EPS = 1e-6
ROPE_THETA = 10000.0
NEG = -1e30
ADAM_LR, ADAM_B1, ADAM_B2, ADAM_EPS, ADAM_WD, ADAM_STEP = 0.001, 0.9, 0.999, 1e-08, 0.01, 10
MESH = pl.DeviceIdType.MESH
N_DEV = 8
N_CHIP = 4
HBM = pl.BlockSpec(memory_space=pl.ANY)
VMEM_LIMIT = 56 * 1024 * 1024

_pcall = pl.pallas_call


def _tile(n, pref, mult=LANES):
    best = None
    t = mult
    while t <= min(n, pref):
        if n % t == 0:
            best = t
        t += mult
    return n if best is None else best


def _params(*sem):
    return pltpu.CompilerParams(dimension_semantics=sem, vmem_limit_bytes=VMEM_LIMIT)


def _dot(a, b, dims):
    return lax.dot_general(a, b, (dims, ((), ())), preferred_element_type=F32)


NN = ((1,), (0,))
NT = ((1,), (1,))
TN = ((0,), (0,))


def _mm_body(dims, nk, tile):
    if nk == 1:
        def single(a_ref, b_ref, o_ref):
            o_ref[...] = _dot(a_ref[...].astype(BF16), b_ref[...].astype(BF16), dims).astype(o_ref.dtype)

        return single, []

    def body(a_ref, b_ref, o_ref, acc_ref):
        k = pl.program_id(2)

        @pl.when(k == 0)
        def _():
            acc_ref[...] = jnp.zeros_like(acc_ref)

        acc_ref[...] += _dot(a_ref[...].astype(BF16), b_ref[...].astype(BF16), dims)

        @pl.when(k == nk - 1)
        def _():
            o_ref[...] = acc_ref[...].astype(o_ref.dtype)

    return body, [pltpu.VMEM(tile, F32)]


def _mm_nn(a, b3, out_dtype, name, tm=1024, tn=1408, tk=2048, b_transposed=False):
    M, K = a.shape
    C, n = b3.shape[0], b3.shape[1 if b_transposed else 2]
    tm, tk, tn = _tile(M, tm, 8), _tile(K, tk), _tile(n, tn)
    npc, nk = n // tn, K // tk
    body, scratch = _mm_body(NT if b_transposed else NN, nk, (tm, tn))
    b_spec = (pl.BlockSpec((None, tn, tk), lambda i, j, k: (j // npc, j % npc, k)) if b_transposed
              else pl.BlockSpec((None, tk, tn), lambda i, j, k: (j // npc, k, j % npc)))
    return _pcall(
        body, grid=(M // tm, C * npc, nk),
        in_specs=[pl.BlockSpec((tm, tk), lambda i, j, k: (i, k)), b_spec],
        out_specs=pl.BlockSpec((tm, tn), lambda i, j, k: (i, j)),
        out_shape=jax.ShapeDtypeStruct((M, C * n), out_dtype), scratch_shapes=scratch,
        compiler_params=_params("parallel", "parallel", "arbitrary"), name=name)(a, b3)


def _mm_nt(a, b3, out_dtype, name, tm=1024, tn=1024, tk=2048, after=None, b_transposed=False, per_step=1):
    M, _ = a.shape
    C, N, n = (b3.shape[0], b3.shape[2], b3.shape[1]) if b_transposed else b3.shape
    tm, tn, tk = _tile(M, tm, 8), _tile(N, tn), _tile(n, tk)
    dims = NN if b_transposed else NT
    extra = [] if after is None else [after]
    if per_step > 1 and tk == n and C % per_step == 0:
        nk, scratch = C // per_step, [pltpu.VMEM((tm, tn), F32)]
        b3 = b3.reshape(nk, per_step, *b3.shape[1:])
        a_spec = pl.BlockSpec((tm, per_step * n), lambda i, j, k: (i, k))
        if b_transposed:
            b_spec = pl.BlockSpec((None, per_step, n, tn), lambda i, j, k: (k, 0, 0, j))
        else:
            b_spec = pl.BlockSpec((None, per_step, tn, n), lambda i, j, k: (k, 0, j, 0))

        def body(a_ref, b_ref, *rest):
            o_ref, acc_ref = rest[len(extra):]
            k = pl.program_id(2)

            @pl.when(k == 0)
            def _():
                acc_ref[...] = jnp.zeros_like(acc_ref)

            b = b_ref[...].astype(BF16)
            b = b.reshape(per_step * n, tn) if b_transposed else jnp.concatenate([b[u] for u in range(per_step)], axis=1)
            acc_ref[...] += _dot(a_ref[...].astype(BF16), b, dims)

            @pl.when(k == nk - 1)
            def _():
                o_ref[...] = acc_ref[...].astype(o_ref.dtype)
    else:
        kpc = n // tk
        nk = C * kpc
        inner, scratch = _mm_body(dims, nk, (tm, tn))
        a_spec = pl.BlockSpec((tm, tk), lambda i, j, k: (i, k))
        b_spec = (pl.BlockSpec((None, tk, tn), lambda i, j, k: (k // kpc, k % kpc, j)) if b_transposed
                  else pl.BlockSpec((None, tn, tk), lambda i, j, k: (k // kpc, j, k % kpc)))

        def body(a_ref, b_ref, *rest):
            inner(a_ref, b_ref, *rest[len(extra):])

    return _pcall(
        body, grid=(M // tm, N // tn, nk), in_specs=[a_spec, b_spec] + [HBM] * len(extra),
        out_specs=pl.BlockSpec((tm, tn), lambda i, j, k: (i, j)),
        out_shape=jax.ShapeDtypeStruct((M, N), out_dtype), scratch_shapes=scratch,
        compiler_params=_params("parallel", "parallel", "arbitrary"), name=name)(a, b3, *extra)


def _mm_tn(x, y, n, out_dtype, name, tm=1024, tn=1408, tk=2048, after=None):
    S, P = x.shape
    C = y.shape[1] // n
    tm, tn, tk = _tile(P, tm), _tile(n, tn), _tile(S, tk, 8)
    npc, nk = n // tn, S // tk
    inner, scratch = _mm_body(TN, nk, (tm, tn))
    extra = [] if after is None else [after]

    def body(x_ref, y_ref, *rest):
        inner(x_ref, y_ref, *rest[len(extra):])

    return _pcall(
        body, grid=(P // tm, C * npc, nk),
        in_specs=[pl.BlockSpec((tk, tm), lambda i, j, k: (k, i)),
                  pl.BlockSpec((tk, tn), lambda i, j, k: (k, j))] + [HBM] * len(extra),
        out_specs=pl.BlockSpec((None, tm, tn), lambda i, j, k: (j // npc, i, j % npc)),
        out_shape=jax.ShapeDtypeStruct((C, P, n), out_dtype), scratch_shapes=scratch,
        compiler_params=_params("parallel", "parallel", "arbitrary"), name=name)(x, y, *extra)


def _rms_scale(v):
    return lax.rsqrt(jnp.mean(v * v, axis=-1, keepdims=True) + RMS_EPS)


def _rms_bwd(gy, v, r):
    return r * gy - v * (r * r * r * jnp.mean(gy * v, axis=-1, keepdims=True))


def _rows_spec(tm, d):
    return pl.BlockSpec((tm, d), lambda i: (i, 0))


def _vec_spec(d):
    return pl.BlockSpec((1, d), lambda i: (0, 0))


def _rms_fwd(x, g, name, tm=256):
    S, D = x.shape

    def body(x_ref, g_ref, h_ref):
        v = x_ref[...]
        h_ref[...] = (v * _rms_scale(v) * g_ref[...]).astype(BF16)

    return _pcall(body, grid=(S // tm,), in_specs=[_rows_spec(tm, D), _vec_spec(D)], out_specs=_rows_spec(tm, D),
                  out_shape=jax.ShapeDtypeStruct((S, D), BF16), compiler_params=_params("parallel"), name=name)(x, g)


def _mid_fwd(x, mix, g_post, g_pre, name, tm=256):
    S, D = x.shape

    def body(x_ref, m_ref, gp_ref, gn_ref, x2_ref, h_ref):
        m = m_ref[...]
        x2 = x_ref[...] + m * _rms_scale(m) * gp_ref[...]
        x2_ref[...] = x2
        h_ref[...] = (x2 * _rms_scale(x2) * gn_ref[...]).astype(BF16)

    return _pcall(body, grid=(S // tm,), in_specs=[_rows_spec(tm, D), _rows_spec(tm, D), _vec_spec(D), _vec_spec(D)],
                  out_specs=[_rows_spec(tm, D), _rows_spec(tm, D)],
                  out_shape=[jax.ShapeDtypeStruct((S, D), F32), jax.ShapeDtypeStruct((S, D), BF16)],
                  compiler_params=_params("parallel"), name=name)(x, mix, g_post, g_pre)


def _loss_bwd(x2, f, tgt, g_post, name, tm=256):
    S, D = x2.shape

    def body(x2_ref, f_ref, t_ref, g_ref, dy_ref, df_ref, dg_ref, ls_ref):
        i = pl.program_id(0)

        @pl.when(i == 0)
        def _():
            dg_ref[...] = jnp.zeros_like(dg_ref)
            ls_ref[...] = jnp.zeros_like(ls_ref)

        fv = f_ref[...]
        r = _rms_scale(fv)
        g = g_ref[...]
        err = x2_ref[...] + fv * r * g - t_ref[...]
        ls_ref[...] += jnp.broadcast_to(0.5 * jnp.sum(jnp.mean(err * err, axis=-1, keepdims=True), axis=0, keepdims=True), ls_ref.shape)
        dy = err * (1.0 / D)
        dy_ref[...] = dy
        df_ref[...] = _rms_bwd(dy * g, fv, r).astype(BF16)
        dg_ref[...] += jnp.sum(dy * fv * r, axis=0, keepdims=True)

    return _pcall(body, grid=(S // tm,),
                  in_specs=[_rows_spec(tm, D), _rows_spec(tm, D), _rows_spec(tm, D), _vec_spec(D)],
                  out_specs=[_rows_spec(tm, D), _rows_spec(tm, D), _vec_spec(D), _vec_spec(LANES)],
                  out_shape=[jax.ShapeDtypeStruct((S, D), F32), jax.ShapeDtypeStruct((S, D), BF16),
                             jax.ShapeDtypeStruct((1, D), F32), jax.ShapeDtypeStruct((1, LANES), F32)],
                  compiler_params=_params("arbitrary"), name=name)(x2, f, tgt, g_post)


def _mid_bwd(dy, dh2, x2, mix, g_pre, g_post, name, tm=256):
    S, D = dy.shape

    def body(dy_ref, dh_ref, x2_ref, m_ref, gn_ref, gp_ref, dx2_ref, dm_ref, dgn_ref, dgp_ref):
        i = pl.program_id(0)

        @pl.when(i == 0)
        def _():
            dgn_ref[...] = jnp.zeros_like(dgn_ref)
            dgp_ref[...] = jnp.zeros_like(dgp_ref)

        x2, dh = x2_ref[...], dh_ref[...].astype(F32)
        r = _rms_scale(x2)
        dx2 = dy_ref[...] + _rms_bwd(dh * gn_ref[...], x2, r)
        dgn_ref[...] += jnp.sum(dh * x2 * r, axis=0, keepdims=True)
        dx2_ref[...] = dx2
        m = m_ref[...]
        rm = _rms_scale(m)
        dm_ref[...] = _rms_bwd(dx2 * gp_ref[...], m, rm).astype(BF16)
        dgp_ref[...] += jnp.sum(dx2 * m * rm, axis=0, keepdims=True)

    return _pcall(body, grid=(S // tm,),
                  in_specs=[_rows_spec(tm, D)] * 4 + [_vec_spec(D)] * 2,
                  out_specs=[_rows_spec(tm, D), _rows_spec(tm, D), _vec_spec(D), _vec_spec(D)],
                  out_shape=[jax.ShapeDtypeStruct((S, D), F32), jax.ShapeDtypeStruct((S, D), BF16),
                             jax.ShapeDtypeStruct((1, D), F32), jax.ShapeDtypeStruct((1, D), F32)],
                  compiler_params=_params("arbitrary"), name=name)(dy, dh2, x2, mix, g_pre, g_post)


def _first_bwd(dx2, dh1, x, g_pre, name, tm=256):
    S, D = x.shape

    def body(dx2_ref, dh_ref, x_ref, g_ref, gx_ref, dg_ref):
        i = pl.program_id(0)

        @pl.when(i == 0)
        def _():
            dg_ref[...] = jnp.zeros_like(dg_ref)

        xv, dh = x_ref[...], dh_ref[...].astype(F32)
        r = _rms_scale(xv)
        gx_ref[...] = dx2_ref[...] + _rms_bwd(dh * g_ref[...], xv, r)
        dg_ref[...] += jnp.sum(dh * xv * r, axis=0, keepdims=True)

    return _pcall(body, grid=(S // tm,), in_specs=[_rows_spec(tm, D)] * 3 + [_vec_spec(D)],
                  out_specs=[_rows_spec(tm, D), _vec_spec(D)],
                  out_shape=[jax.ShapeDtypeStruct((S, D), F32), jax.ShapeDtypeStruct((1, D), F32)],
                  compiler_params=_params("arbitrary"), name=name)(dx2, dh1, x, g_pre)


def _logsig_pair(z):
    lb = jnp.minimum(z, 0.0) - jnp.log(1.0 + jnp.exp(-jnp.abs(z)))
    return lb, lb - z


SB_KEY_BLOCK = 256


def _sum_matrix(strict):
    ia = lax.broadcasted_iota(jnp.int32, (SB_KEY_BLOCK, SB_KEY_BLOCK), 0)
    ib = lax.broadcasted_iota(jnp.int32, (SB_KEY_BLOCK, SB_KEY_BLOCK), 1)
    return ((ia > ib) if strict == ">" else (ia < ib)).astype(BF16)


def _row_total(sums, v, col):
    return jnp.broadcast_to(sums[:, col:col + 1] + v[:, col:col + 1], (v.shape[0], LANES))


def _lanes(c, width):
    return jnp.tile(c, (1, width // LANES))


def _split_dot(v, u):
    hi = v.astype(BF16)
    lo = (v - hi.astype(F32)).astype(BF16)
    return _dot(hi, u, NN) + _dot(lo, u, NN)


def _head_out(o, g):
    return o * _rms_scale(o) * g


def _sb_fwd(proj, gain, n_heads, mixed_heads, name, tq=1024):
    S = proj.shape[0]
    H, tk = n_heads, SB_KEY_BLOCK
    tq = _tile(S, tq, 2 * tk)
    scale = HEAD_DIM ** -0.5

    def body(q_ref, k_ref, v_ref, g_ref, o_ref, ct_ref, mx_ref, oacc, cacc):
        i = pl.program_id(1)
        oacc[...] = jnp.zeros_like(oacc)
        cacc[...] = jnp.zeros_like(cacc)
        sums = _sum_matrix(">")

        def run(blocks):
            scored = []
            for k0, r0, diagonal in blocks:
                rows = pl.ds(r0, tq - r0)
                lb, lk = _logsig_pair(_dot(q_ref[rows, :].astype(BF16), k_ref[pl.ds(k0, tk), :].astype(BF16), NT) * scale)
                causal = None
                if diagonal:
                    causal = (lax.broadcasted_iota(jnp.int32, (tq - r0, tk), 1)
                              < lax.broadcasted_iota(jnp.int32, (tq - r0, tk), 0))
                    lk = jnp.where(causal, lk, 0.0)
                scored.append((k0, rows, causal, lb, lk))
            summed = [(k0, rows, causal, lb, lk, _split_dot(lk, sums)) for k0, rows, causal, lb, lk in scored]
            weights = []
            for k0, rows, causal, lb, lk, after in summed:
                c = cacc[rows, :]
                a = jnp.exp(lb + after + _lanes(c, tk))
                if causal is not None:
                    a = jnp.where(causal, a, 0.0)
                cacc[rows, :] = c + _row_total(after, lk, 0)
                weights.append((k0, rows, a.astype(BF16)))
            for k0, rows, a in weights:
                oacc[rows, :] += _dot(a, v_ref[pl.ds(k0, tk), :].astype(BF16), NN)

        for d in reversed(range(0, tq // tk, 2)):
            run([(pl.multiple_of(i * tq + e * tk, tk), e * tk, True) for e in (d + 1, d)])
        per_trip = tq // tk

        def step(it, carry):
            k0 = pl.multiple_of((i - 1 - it) * tq, tq)
            run([(pl.multiple_of(k0 + e * tk, tk), 0, False) for e in reversed(range(per_trip))])
            return carry

        lax.fori_loop(0, i, step, 0)
        o = oacc[...]
        o_ref[...] = o
        ct_ref[...] = cacc[...]
        mx_ref[...] = _head_out(o, g_ref[...]).astype(BF16)

    blk = pl.BlockSpec((tq, HEAD_DIM), lambda h, i: (i, h))
    return _pcall(
        body, grid=(H, S // tq),
        in_specs=[blk, pl.BlockSpec((S, HEAD_DIM), lambda h, i: (0, H + h)),
                  pl.BlockSpec((S, HEAD_DIM), lambda h, i: (0, 2 * H + h)), pl.BlockSpec((1, HEAD_DIM), lambda h, i: (0, h))],
        out_specs=[blk, blk, blk],
        out_shape=[jax.ShapeDtypeStruct((S, H * HEAD_DIM), F32), jax.ShapeDtypeStruct((S, H * HEAD_DIM), F32),
                   jax.ShapeDtypeStruct((S, mixed_heads * HEAD_DIM), BF16)],
        scratch_shapes=[pltpu.VMEM((tq, HEAD_DIM), F32), pltpu.VMEM((tq, LANES), F32)],
        compiler_params=_params("parallel", "arbitrary"), name=name)(proj, proj, proj, gain)


def _sb_bwd(proj, gain, o_raw, ctot, dmixed, dm_col0, n_heads, name, tq=1024):
    S = proj.shape[0]
    H, tk = n_heads, SB_KEY_BLOCK
    tq = _tile(S, tq, 2 * tk)
    nq = S // tq
    scale = HEAD_DIM ** -0.5

    def body(q_ref, k_ref, v_ref, g_ref, o_ref, ct_ref, dm_ref, dq_ref, dk_ref, dv_ref, dg_ref,
             dkacc, dvacc, dqacc, pfx, gcar, dos):
        i = pl.program_id(1)

        @pl.when(i == 0)
        def _():
            dkacc[...] = jnp.zeros_like(dkacc)
            dvacc[...] = jnp.zeros_like(dvacc)
            dg_ref[...] = jnp.zeros_like(dg_ref)

        o, dm, g = o_ref[...], dm_ref[...].astype(F32), g_ref[...]
        r = _rms_scale(o)
        dos[...] = _rms_bwd(dm * g, o, r).astype(BF16)
        dg_ref[...] += jnp.broadcast_to(jnp.sum(dm * o * r, axis=0, keepdims=True), dg_ref.shape)
        dqacc[...] = jnp.zeros_like(dqacc)
        pfx[...] = jnp.zeros_like(pfx)
        gcar[...] = jnp.zeros_like(gcar)
        later, earlier = _sum_matrix(">"), _sum_matrix("<")

        def run(blocks):
            scored = []
            for k0, r0, diagonal in blocks:
                rows, keys = pl.ds(r0, tq - r0), pl.ds(k0, tk)
                lb, lk = _logsig_pair(_dot(q_ref[rows, :].astype(BF16), k_ref[keys, :].astype(BF16), NT) * scale)
                da = _dot(dos[rows, :], v_ref[keys, :].astype(BF16), NT)
                causal = None
                if diagonal:
                    causal = (lax.broadcasted_iota(jnp.int32, (tq - r0, tk), 1)
                              < lax.broadcasted_iota(jnp.int32, (tq - r0, tk), 0))
                    lk = jnp.where(causal, lk, 0.0)
                scored.append((rows, keys, causal, lb, lk, da))
            summed = [(*blk, _split_dot(blk[4], later)) for blk in scored]
            weighted = []
            for rows, keys, causal, lb, lk, da, after in summed:
                p = pfx[rows, :] + _row_total(after, lk, 0)
                pfx[rows, :] = p
                a = jnp.exp(lb + after + _lanes(ct_ref[rows, :] - p, tk))
                if causal is not None:
                    a = jnp.where(causal, a, 0.0)
                dl = da * a
                weighted.append((rows, keys, causal, lb, a.astype(BF16), dl, _dot(dl.astype(BF16), earlier, NN)))
            cotangents = []
            for rows, keys, causal, lb, a, dl, before in weighted:
                gc = gcar[rows, :]
                gcar[rows, :] = gc + _row_total(before, dl, tk - 1)
                sig = jnp.exp(lb)
                gsum = (before + _lanes(gc, tk)) * sig
                if causal is not None:
                    gsum = jnp.where(causal, gsum, 0.0)
                cotangents.append((rows, keys, a, ((dl * (1.0 - sig) - gsum) * scale).astype(BF16)))
            for rows, keys, a, dz in cotangents:
                q, do = q_ref[rows, :].astype(BF16), dos[rows, :]
                dvacc[keys, :] += _dot(a, do, TN)
                dqacc[rows, :] += _dot(dz, k_ref[keys, :].astype(BF16), NN)
                dkacc[keys, :] += _dot(dz, q, TN)

        per_trip = tq // tk

        def step(j, carry):
            k0 = pl.multiple_of(j * tq, tq)
            run([(pl.multiple_of(k0 + e * tk, tk), 0, False) for e in range(per_trip)])
            return carry

        lax.fori_loop(0, i, step, 0)
        for d in range(0, tq // tk, 2):
            run([(pl.multiple_of(i * tq + e * tk, tk), e * tk, True) for e in (d, d + 1)])
        dq_ref[...] = dqacc[...].astype(BF16)

        @pl.when(i == nq - 1)
        def _():
            dk_ref[...] = dkacc[...].astype(BF16)
            dv_ref[...] = dvacc[...].astype(BF16)

    blk = pl.BlockSpec((tq, HEAD_DIM), lambda h, i: (i, h))
    full = pl.BlockSpec((S, HEAD_DIM), lambda h, i: (0, h))
    W = H * HEAD_DIM
    return _pcall(
        body, grid=(H, nq),
        in_specs=[blk, pl.BlockSpec((S, HEAD_DIM), lambda h, i: (0, H + h)),
                  pl.BlockSpec((S, HEAD_DIM), lambda h, i: (0, 2 * H + h)), pl.BlockSpec((1, HEAD_DIM), lambda h, i: (0, h)),
                  blk, blk, pl.BlockSpec((tq, HEAD_DIM), lambda h, i: (i, dm_col0 + h))],
        out_specs=[blk, full, full, pl.BlockSpec((8, HEAD_DIM), lambda h, i: (0, h))],
        out_shape=[jax.ShapeDtypeStruct((S, W), BF16), jax.ShapeDtypeStruct((S, W), BF16),
                   jax.ShapeDtypeStruct((S, W), BF16), jax.ShapeDtypeStruct((8, W), F32)],
        scratch_shapes=[pltpu.VMEM((S, HEAD_DIM), F32), pltpu.VMEM((S, HEAD_DIM), F32), pltpu.VMEM((tq, HEAD_DIM), F32),
                        pltpu.VMEM((tq, LANES), F32), pltpu.VMEM((tq, LANES), F32), pltpu.VMEM((tq, HEAD_DIM), BF16)],
        compiler_params=_params("arbitrary", "arbitrary"), name=name)(proj, proj, proj, gain, o_raw, ctot, dmixed)


def _rope_tables(S):
    inv_freq = ROPE_THETA ** (-jnp.arange(0, HEAD_DIM, 2, dtype=F32) / HEAD_DIM)
    ang = jnp.arange(S, dtype=F32)[:, None] * inv_freq[None, :]
    cos, sin = jnp.cos(ang), jnp.sin(ang)
    return jnp.concatenate([cos, cos], axis=1), jnp.concatenate([-sin, sin], axis=1)


def _rope(v, cos2, sin_signed):
    return v * cos2 + pltpu.roll(v, HEAD_DIM // 2, axis=1) * sin_signed


def _dil_rows(d, r, l0, n):
    if d == 1:
        return pl.ds(l0 if isinstance(l0, int) else pl.multiple_of(l0, KEY_BLOCK), n)
    return pl.ds(r + d * l0, n, stride=d)


def _dil_blocks(S, visit):
    B = KEY_BLOCK
    group = 16
    for b, d in enumerate(DILATIONS):
        nb = S // d // B
        if nb == 1:
            g = math.gcd(d, group)

            def trip(t, carry, b=b, d=d, g=g):
                visit([(b, d, t * g + u, 0, True) for u in range(g)])
                return carry

            lax.fori_loop(0, d // g, trip, 0)
        elif d == 1:
            visit([(b, d, 0, 0, True)])
            g = max(k for k in range(1, group + 2) if (nb - 1) % k == 0)

            def trip(t, carry, b=b, d=d, g=g):
                visit([(b, d, 0, (1 + t * g + u) * B, False) for u in range(g)])
                return carry

            lax.fori_loop(0, (nb - 1) // g, trip, 0)
        else:
            g = math.gcd(d, max(group // nb, 1))

            def trip(t, carry, b=b, d=d, nb=nb, g=g):
                visit([(b, d, t * g + u, n * B, n == 0) for u in range(g) for n in range(nb)])
                return carry

            lax.fori_loop(0, d // g, trip, 0)


def _dil_mask(first):
    B = KEY_BLOCK
    nk = B if first else 2 * B
    iq = lax.broadcasted_iota(jnp.int32, (B, nk), 0)
    ik = lax.broadcasted_iota(jnp.int32, (B, nk), 1)
    return (ik <= iq) if first else ((ik >= iq) & (ik <= iq + B))


def _dil_fwd(proj, cos2, sin_signed, gain, mixed, col0, n_heads, name):
    S = proj.shape[0]
    H, B = n_heads, KEY_BLOCK
    scale = HEAD_DIM ** -0.5
    rc = _tile(S, 256, 8)

    def body(q_ref, k_ref, v_ref, c_ref, s_ref, g_ref, mixed_in, o_ref, l_ref, mx_ref, qr, kr, vf, *per_branch):
        ob, lb = per_branch[:len(DILATIONS)], per_branch[len(DILATIONS):]

        def rope_rows(t, carry):
            rows = pl.ds(pl.multiple_of(t * rc, rc), rc)
            qr[rows, :] = _rope(q_ref[rows, :].astype(F32), c_ref[rows, :], s_ref[rows, :])
            kr[rows, :] = _rope(k_ref[rows, :].astype(F32), c_ref[rows, :], s_ref[rows, :])
            vf[rows, :] = v_ref[rows, :].astype(F32)
            return carry

        lax.fori_loop(0, S // rc, rope_rows, 0)

        def visit(blocks):
            scores = []
            for b, d, r, l0, first in blocks:
                qrows = _dil_rows(d, r, l0, B)
                krows = qrows if first else _dil_rows(d, r, l0 - B, 2 * B)
                s = _dot(qr[qrows, :].astype(BF16), kr[krows, :].astype(BF16), NT) * scale
                scores.append((b, qrows, krows, jnp.where(_dil_mask(first), s, NEG)))
            weights = []
            for b, qrows, krows, s in scores:
                m = jnp.max(s, axis=1, keepdims=True)
                p = jnp.exp(s - m)
                den = jnp.sum(p, axis=1, keepdims=True)
                lb[b][qrows, :] = jnp.broadcast_to(m + jnp.log(den), (B, LANES))
                weights.append((b, qrows, krows, p.astype(BF16), den))
            for b, qrows, krows, p, den in weights:
                ob[b][qrows, :] = _dot(p, vf[krows, :].astype(BF16), NN) / den

        _dil_blocks(S, visit)

        def combine(t, carry):
            rows = pl.ds(pl.multiple_of(t * rc, rc), rc)
            l0, l1, l2 = lb[0][rows, :], lb[1][rows, :], lb[2][rows, :]
            m = jnp.maximum(jnp.maximum(l0, l1), l2)
            w0, w1, w2 = jnp.exp(l0 - m), jnp.exp(l1 - m), jnp.exp(l2 - m)
            den = w0 + w1 + w2
            o = (w0 * ob[0][rows, :] + w1 * ob[1][rows, :] + w2 * ob[2][rows, :]) / den
            o_ref[rows, :] = o
            l_ref[rows, :] = m + jnp.log(den)
            mx_ref[rows, :] = _head_out(o, g_ref[...]).astype(BF16)
            return carry

        lax.fori_loop(0, S // rc, combine, 0)

    def col(k):
        return pl.BlockSpec((S, HEAD_DIM), lambda h: (0, col0 + k * H + h))

    tab = pl.BlockSpec((S, HEAD_DIM), lambda h: (0, 0))
    out = pl.BlockSpec((S, HEAD_DIM), lambda h: (0, h))
    W = H * HEAD_DIM
    first = mixed.shape[1] // HEAD_DIM - H
    return _pcall(
        body, grid=(H,),
        in_specs=[col(0), col(1), col(2), tab, tab, pl.BlockSpec((1, HEAD_DIM), lambda h: (0, h)), HBM],
        out_specs=[out, out, pl.BlockSpec((S, HEAD_DIM), lambda h: (0, first + h))],
        out_shape=[jax.ShapeDtypeStruct((S, W), F32), jax.ShapeDtypeStruct((S, W), F32),
                   jax.ShapeDtypeStruct(mixed.shape, BF16)],
        input_output_aliases={6: 2},
        scratch_shapes=[pltpu.VMEM((S, HEAD_DIM), F32)] * (3 + 2 * len(DILATIONS)),
        compiler_params=_params("parallel"), name=name)(proj, proj, proj, cos2, sin_signed, gain, mixed)


def _dil_bwd(proj, cos2, sin_signed, gain, o_raw, lse, dmixed, dm_col0, col0, n_heads, name):
    S = proj.shape[0]
    H, B = n_heads, KEY_BLOCK
    scale = HEAD_DIM ** -0.5
    rc = _tile(S, 256, 8)

    def body(q_ref, k_ref, v_ref, c_ref, s_ref, g_ref, o_ref, l_ref, dm_ref, dq_ref, dk_ref, dv_ref, dg_ref,
             qr, kr, vf, dos, dsum, dqr, dkr, dvv):
        dg_ref[...] = jnp.zeros_like(dg_ref)

        def prep(t, carry):
            rows = pl.ds(pl.multiple_of(t * rc, rc), rc)
            qr[rows, :] = _rope(q_ref[rows, :].astype(F32), c_ref[rows, :], s_ref[rows, :])
            kr[rows, :] = _rope(k_ref[rows, :].astype(F32), c_ref[rows, :], s_ref[rows, :])
            vf[rows, :] = v_ref[rows, :].astype(F32)
            o, dm = o_ref[rows, :], dm_ref[rows, :].astype(F32)
            r = _rms_scale(o)
            do = _rms_bwd(dm * g_ref[...], o, r)
            dg_ref[...] += jnp.broadcast_to(jnp.sum(dm * o * r, axis=0, keepdims=True), dg_ref.shape)
            dos[rows, :] = do
            dsum[rows, :] = jnp.broadcast_to(jnp.sum(do * o, axis=1, keepdims=True), (rc, LANES))
            dqr[rows, :] = jnp.zeros((rc, HEAD_DIM), F32)
            dkr[rows, :] = jnp.zeros((rc, HEAD_DIM), F32)
            dvv[rows, :] = jnp.zeros((rc, HEAD_DIM), F32)
            return carry

        lax.fori_loop(0, S // rc, prep, 0)

        def visit(blocks):
            products = []
            for b, d, r, l0, first in blocks:
                qrows = _dil_rows(d, r, l0, B)
                krows = qrows if first else _dil_rows(d, r, l0 - B, 2 * B)
                qs, ks = qr[qrows, :].astype(BF16), kr[krows, :].astype(BF16)
                do = dos[qrows, :].astype(BF16)
                s = jnp.where(_dil_mask(first), _dot(qs, ks, NT) * scale, NEG)
                dp = _dot(do, vf[krows, :].astype(BF16), NT)
                products.append((qrows, krows, qs, ks, do, s, dp))
            cotangents = []
            for qrows, krows, qs, ks, do, s, dp in products:
                p = jnp.exp(s - l_ref[qrows, :][:, 0:1])
                ds = (p * (dp - dsum[qrows, :][:, 0:1]) * scale).astype(BF16)
                cotangents.append((qrows, krows, qs, ks, do, p.astype(BF16), ds))
            for qrows, krows, qs, ks, do, p, ds in cotangents:
                dqr[qrows, :] += _dot(ds, ks, NN)
                dkr[krows, :] += _dot(ds, qs, TN)
                dvv[krows, :] += _dot(p, do, TN)

        _dil_blocks(S, visit)

        def finish(t, carry):
            rows = pl.ds(pl.multiple_of(t * rc, rc), rc)
            c, s = c_ref[rows, :], s_ref[rows, :]
            dq, dk = dqr[rows, :], dkr[rows, :]
            dq_ref[rows, :] = (dq * c + pltpu.roll(dq * s, HEAD_DIM // 2, axis=1)).astype(BF16)
            dk_ref[rows, :] = (dk * c + pltpu.roll(dk * s, HEAD_DIM // 2, axis=1)).astype(BF16)
            dv_ref[rows, :] = dvv[rows, :].astype(BF16)
            return carry

        lax.fori_loop(0, S // rc, finish, 0)

    def col(k):
        return pl.BlockSpec((S, HEAD_DIM), lambda h: (0, col0 + k * H + h))

    tab = pl.BlockSpec((S, HEAD_DIM), lambda h: (0, 0))
    out = pl.BlockSpec((S, HEAD_DIM), lambda h: (0, h))
    W = H * HEAD_DIM
    big = pltpu.VMEM((S, HEAD_DIM), F32)
    return _pcall(
        body, grid=(H,),
        in_specs=[col(0), col(1), col(2), tab, tab, pl.BlockSpec((1, HEAD_DIM), lambda h: (0, h)), out, out,
                  pl.BlockSpec((S, HEAD_DIM), lambda h: (0, dm_col0 + h))],
        out_specs=[out, out, out, pl.BlockSpec((8, HEAD_DIM), lambda h: (0, h))],
        out_shape=[jax.ShapeDtypeStruct((S, W), BF16), jax.ShapeDtypeStruct((S, W), BF16),
                   jax.ShapeDtypeStruct((S, W), BF16), jax.ShapeDtypeStruct((8, W), F32)],
        scratch_shapes=[big, big, big, big, pltpu.VMEM((S, LANES), F32), big, big, big],
        compiler_params=_params("parallel"), name=name)(proj, proj, proj, cos2, sin_signed, gain, o_raw, lse, dmixed)


GELU_C = math.sqrt(2.0 / math.pi)
GELU_A = 0.044715
HALO = 16


def _shifts_down(cur, halo):
    row = lax.broadcasted_iota(jnp.int32, cur.shape, 0)
    first, second = row == 0, row == 1
    last, before_last = halo[HALO - 1:HALO, :], halo[HALO - 2:HALO - 1, :]
    two = jnp.where(first, before_last, jnp.where(second, last, pltpu.roll(cur, 2, axis=0)))
    return two, jnp.where(first, last, pltpu.roll(cur, 1, axis=0))


def _shift_up(cur, halo, k):
    n = cur.shape[0]
    out = pltpu.roll(cur, n - k, axis=0)
    row = lax.broadcasted_iota(jnp.int32, cur.shape, 0)
    for t in range(k):
        out = jnp.where(row == n - k + t, halo[t:t + 1, :], out)
    return out


def _conv3(cur, halo, cw):
    rows = (*_shifts_down(cur, halo), cur)
    return rows[0] * cw[0:1, :] + rows[1] * cw[1:2, :] + cur * cw[2:3, :] + cw[3:4, :], rows


def _gelu_parts(x):
    xx = x * x
    t = jnp.tanh(x * (GELU_C + (GELU_C * GELU_A) * xx))
    half = 0.5 * x
    return half + half * t, t, xx, half


def _gelu_slope(t, xx, half):
    return (0.5 + 0.5 * t) + half * (1.0 - t * t) * (GELU_C + (3.0 * GELU_C * GELU_A) * xx)


def _geglu_specs(tm, tn, ncb):
    hb = tm // HALO

    def cur(off):
        return pl.BlockSpec((tm, tn), lambda j, i: (i, off + j))

    def prev(off):
        return pl.BlockSpec((HALO, tn), lambda j, i: (jnp.maximum(i * hb - 1, 0), off + j))

    def taps(off):
        return pl.BlockSpec((8, tn), lambda j, i: (0, off + j))

    return [cur(0), prev(0), cur(ncb), prev(ncb), taps(0), taps(ncb)]


def _geglu_fwd(u, cwb, name, tm=512, tn=1408):
    S, F2 = u.shape
    F = F2 // 2
    tm, tn = _tile(S, tm, HALO), _tile(F, tn)
    ncb = F // tn

    def body(g_ref, gp_ref, v_ref, vp_ref, cg_ref, cv_ref, y_ref):
        top = pl.program_id(1) > 0
        gp = jnp.where(top, gp_ref[...].astype(F32), 0.0)
        vp = jnp.where(top, vp_ref[...].astype(F32), 0.0)
        gc = _conv3(g_ref[...].astype(F32), gp, cg_ref[...])[0]
        vc = _conv3(v_ref[...].astype(F32), vp, cv_ref[...])[0]
        y_ref[...] = (_gelu_parts(gc)[0] * vc).astype(BF16)

    return _pcall(body, grid=(ncb, S // tm), in_specs=_geglu_specs(tm, tn, ncb),
                  out_specs=pl.BlockSpec((tm, tn), lambda j, i: (i, j)),
                  out_shape=jax.ShapeDtypeStruct((S, F), BF16),
                  compiler_params=_params("parallel", "parallel"), name=name)(u, u, u, u, cwb, cwb)


def _geglu_bwd(u, dy, cwb, name, tm=256, tn=1408):
    S, F2 = u.shape
    F = F2 // 2
    tm, tn = _tile(S, tm, HALO), _tile(F, tn)
    ncb = F // tn

    def body(g_ref, gp_ref, v_ref, vp_ref, cg_ref, cv_ref, dy_ref, dc_ref, dwg_ref, dwv_ref):
        i = pl.program_id(1)

        @pl.when(i == 0)
        def _():
            dwg_ref[...] = jnp.zeros_like(dwg_ref)
            dwv_ref[...] = jnp.zeros_like(dwv_ref)

        top = i > 0
        g, v = g_ref[...].astype(F32), v_ref[...].astype(F32)
        gp = jnp.where(top, gp_ref[...].astype(F32), 0.0)
        vp = jnp.where(top, vp_ref[...].astype(F32), 0.0)
        gc, g_rows = _conv3(g, gp, cg_ref[...])
        vc, v_rows = _conv3(v, vp, cv_ref[...])
        act, t, xx, half = _gelu_parts(gc)
        dact = _gelu_slope(t, xx, half)
        dyv = dy_ref[...].astype(F32)
        dgc = dyv * vc * dact
        dvc = dyv * act
        dc_ref[0] = dgc.astype(BF16)
        dc_ref[1] = dvc.astype(BF16)

        def taps(out_ref, dc, rows):
            for k, moved in enumerate(rows):
                out_ref[k:k + 1, :] += jnp.sum(dc * moved, axis=0, keepdims=True)
            out_ref[3:4, :] += jnp.sum(dc, axis=0, keepdims=True)

        taps(dwg_ref, dgc, g_rows)
        taps(dwv_ref, dvc, v_rows)

    return _pcall(body, grid=(ncb, S // tm),
                  in_specs=_geglu_specs(tm, tn, ncb) + [pl.BlockSpec((tm, tn), lambda j, i: (i, j))],
                  out_specs=[pl.BlockSpec((2, tm, tn), lambda j, i: (0, i, j)),
                             pl.BlockSpec((8, tn), lambda j, i: (0, j)), pl.BlockSpec((8, tn), lambda j, i: (0, j))],
                  out_shape=[jax.ShapeDtypeStruct((2, S, F), BF16), jax.ShapeDtypeStruct((8, F), F32),
                             jax.ShapeDtypeStruct((8, F), F32)],
                  compiler_params=_params("parallel", "arbitrary"), name=name)(u, u, u, u, cwb, cwb, dy)


def _conv_bwd(dc, cwb, name, tm=512, tn=1408):
    _, S, F = dc.shape
    tm, tn = _tile(S, tm, HALO), _tile(F, tn)
    ncb, nrb = F // tn, S // tm
    hb = tm // HALO

    def body(c_ref, n_ref, w_ref, du_ref):
        cur = c_ref[...].astype(F32)
        nxt = jnp.where(pl.program_id(2) < nrb - 1, n_ref[...].astype(F32), 0.0)
        w = w_ref[...]
        du = cur * w[2:3, :] + _shift_up(cur, nxt, 1) * w[1:2, :] + _shift_up(cur, nxt, 2) * w[0:1, :]
        du_ref[...] = du.astype(BF16)

    return _pcall(body, grid=(2, ncb, nrb),
                  in_specs=[pl.BlockSpec((None, tm, tn), lambda c, j, i: (c, i, j)),
                            pl.BlockSpec((None, HALO, tn), lambda c, j, i: (c, jnp.minimum((i + 1) * hb, S // HALO - 1), j)),
                            pl.BlockSpec((8, tn), lambda c, j, i: (0, c * ncb + j))],
                  out_specs=pl.BlockSpec((tm, tn), lambda c, j, i: (i, c * ncb + j)),
                  out_shape=jax.ShapeDtypeStruct((S, 2 * F), BF16),
                  compiler_params=_params("parallel", "parallel", "parallel"), name=name)(dc, dc, cwb)


def _adam_math(w, g, m, v):
    m = ADAM_B1 * m + (1.0 - ADAM_B1) * g
    v = ADAM_B2 * v + (1.0 - ADAM_B2) * (g * g)
    m_hat = m / (1.0 - ADAM_B1 ** ADAM_STEP)
    v_hat = v / (1.0 - ADAM_B2 ** ADAM_STEP)
    return -ADAM_LR * (m_hat / (jnp.sqrt(v_hat) + ADAM_EPS) + ADAM_WD * w), m, v


def _adamw(w, parts, m, v, name, tr=256):
    R, C = w.shape
    n, _, Cp = parts.shape
    tr = _tile(R, tr, 8)

    def body(w_ref, p_ref, m_ref, v_ref, g_out, d_out, m_out, v_out):
        g = p_ref[0, :, 0:C].astype(F32)
        for k in range(1, n):
            g = g + p_ref[k, :, 0:C].astype(F32)
        d, mn, vn = _adam_math(w_ref[...], g, m_ref[...], v_ref[...])
        g_out[...] = g
        d_out[...] = d
        m_out[...] = mn
        v_out[...] = vn

    spec = pl.BlockSpec((tr, C), lambda i: (i, 0))
    shape = jax.ShapeDtypeStruct((R, C), F32)
    return _pcall(body, grid=(R // tr,), in_specs=[spec, pl.BlockSpec((n, tr, Cp), lambda i: (0, i, 0)), spec, spec],
                  out_specs=[spec] * 4, out_shape=[shape] * 4, compiler_params=_params("parallel"), name=name)(w, parts, m, v)


def _adamw_chips(w, pair, parts, chip_ids, m, v, name, tr=256):
    R, C = w.shape
    Cp = pair.shape[2]
    by_columns = C == Cp and _tile(R, tr, 16) < 64
    tr, tc = (R, _tile(C, 256)) if by_columns else (_tile(R, tr, 16), C)

    def body(ids_ref, w_ref, own_ref, p1_ref, p2_ref, p3_ref, m_ref, v_ref, g_out, d_out, m_out, v_out):
        g = own_ref[:, 0:tc].astype(F32)
        for ref in (p1_ref, p2_ref, p3_ref):
            g = g + ref[:, 0:tc].astype(F32)
        d, mn, vn = _adam_math(w_ref[...], g, m_ref[...], v_ref[...])
        g_out[...] = g
        d_out[...] = d
        m_out[...] = mn
        v_out[...] = vn

    if by_columns:
        spec = pl.BlockSpec((tr, tc), lambda j, ids: (0, j))
    else:
        spec = pl.BlockSpec((tr, tc), lambda i, ids: (i, 0))

    def chip(k):
        if by_columns:
            return pl.BlockSpec((None, tr, tc), lambda j, ids: (ids[k], 0, j))
        return pl.BlockSpec((None, tr, Cp), lambda i, ids: (ids[k], i, 0))

    shape = jax.ShapeDtypeStruct((R, C), F32)
    grid_spec = pltpu.PrefetchScalarGridSpec(
        num_scalar_prefetch=1, grid=(C // tc if by_columns else R // tr,),
        in_specs=[spec, chip(0), chip(1), chip(2), chip(3), spec, spec], out_specs=[spec] * 4)
    return _pcall(body, grid_spec=grid_spec, out_shape=[shape] * 4, compiler_params=_params("parallel"),
                  name=name)(chip_ids, w, pair, parts, parts, parts, m, v)


def _place():
    return lax.axis_index("x"), lax.axis_index("y"), lax.axis_index("c")


def _other_chips(x, y):
    return [(1 - x, y), (x, 1 - y), (1 - x, 1 - y)]


IN_HBM = pl.BlockSpec(memory_space=pltpu.HBM)
SEM = pl.BlockSpec(memory_space=pltpu.SEMAPHORE)
EFFECT = pltpu.SideEffectType.DATAFLOW_SIDE_EFFECTING
TOKEN = jax.ShapeDtypeStruct((8, LANES), F32)
TOKEN_SPEC = pl.BlockSpec(memory_space=pltpu.VMEM)


def _in_hbm(a):
    return pltpu.with_memory_space_constraint(a, pltpu.HBM)


def _landing(shape):
    return _in_hbm(lax.empty(shape.shape, shape.dtype))


def _hbm_like(a):
    return pltpu.HBM(a.shape, a.dtype)


def _gather_places():
    x, y, c = _place()
    relay_from = (c * (1 - x) + (1 - c) * x, c * y + (1 - c) * (1 - y), c)
    relay_to = (c * x + (1 - c) * (1 - x), c * (1 - y) + (1 - c) * y, c)
    return (x, y, c), (x, y, 1 - c), (1 - x, y, c), (x, 1 - y, c), (1 - x, 1 - y, c), relay_from, relay_to


def _slot_copy(slot, ref, src, dst, send_sem, recv_sem, to):
    return pltpu.make_async_remote_copy(src_ref=slot(ref, *src), dst_ref=slot(ref, *dst), send_sem=send_sem,
                                        recv_sem=recv_sem, device_id=to, device_id_type=MESH)


def _split_call(body, arrays, sems_in, sems_out, after, name, token=True):
    na, ni, no = len(arrays), len(sems_in), len(sems_out)

    def wrapped(*refs):
        body(refs[:na], refs[na:na + ni], refs[na + ni + 1:na + ni + 1 + no])
        if token:
            refs[-1][...] = jnp.zeros_like(refs[-1])

    outs = _pcall(
        wrapped, in_specs=[IN_HBM] * na + [SEM] * ni + [HBM],
        out_specs=[SEM] * no + [IN_HBM] * na + ([TOKEN_SPEC] if token else []),
        out_shape=[pltpu.SemaphoreType.DMA((n,)) for n in sems_out] + [_hbm_like(s) for s in arrays] + ([TOKEN] if token else []),
        input_output_aliases={a: no + a for a in range(na)},
        compiler_params=pltpu.CompilerParams(has_side_effects=EFFECT), name=name,
    )(*[_in_hbm(s) for s in arrays], *sems_in, after)
    return list(outs[:no]), list(outs[no:no + na]), (outs[-1] if token else None)


def _gather_start(landing, slots, after, name):
    na = len(landing)

    def body(land, _, sems):
        me, sib, xn, yn, _, _, _ = _gather_places()
        for a in range(na):
            for k, to in enumerate((sib, xn, yn)):
                _slot_copy(slots[a], land[a], me, me, sems[0].at[3 * a + k], sems[1].at[3 * a + k], to).start()

    return _split_call(body, landing, [], [3 * na, 3 * na], after, name)


def _gather_relay(gathered, sems1, slots, after, name):
    na = len(gathered)

    def body(gath, taken, given):
        me, sib, xn, yn, _, relay_from, relay_to = _gather_places()
        for a in range(na):
            for k, peer in enumerate((sib, xn, yn)):
                arrival = _slot_copy(slots[a], gath[a], me, peer, taken[0].at[3 * a + k], taken[1].at[3 * a + k], peer)
                arrival.wait_send()
                arrival.wait_recv()
        for a in range(na):
            _slot_copy(slots[a], gath[a], relay_from, relay_from, given[0].at[a], given[1].at[a], relay_to).start()
            for k, peer in enumerate((xn, yn)):
                _slot_copy(slots[a], gath[a], peer, peer, given[2].at[2 * a + k], given[3].at[2 * a + k], sib).start()

    return _split_call(body, gathered, sems1, [na, na, 2 * na, 2 * na], after, name)


def _gather_pass(gathered, relay_sems, slots, after, name):
    na = len(gathered)

    def body(gath, taken, given):
        me, sib, xn, yn, diag, relay_from, relay_to = _gather_places()
        for a in range(na):
            _slot_copy(slots[a], gath[a], relay_from, relay_from, taken[0].at[a], taken[1].at[a], relay_to).wait_send()
            _slot_copy(slots[a], gath[a], me, diag, taken[0].at[a], taken[1].at[a], relay_to).wait_recv()
        for a in range(na):
            _slot_copy(slots[a], gath[a], diag, diag, given[0].at[a], given[1].at[a], sib).start()

    return _split_call(body, gathered, relay_sems, [na, na], after, name)


def _gather_finish(gathered, pass_sems, diag_sems, slots, after, name):
    na = len(gathered)

    def body(gath, taken, _):
        (x, y, c), sib, xn, yn, diag, _, _ = _gather_places()
        for a in range(na):
            for k, peer in enumerate((xn, yn)):
                passed = _slot_copy(slots[a], gath[a], peer, (peer[0], peer[1], 1 - c), taken[0].at[2 * a + k],
                                    taken[1].at[2 * a + k], sib)
                passed.wait_send()
                passed.wait_recv()
            passed = _slot_copy(slots[a], gath[a], diag, (diag[0], diag[1], 1 - c), taken[2].at[a], taken[3].at[a], sib)
            passed.wait_send()
            passed.wait_recv()

    return _split_call(body, gathered, list(pass_sems) + list(diag_sems), [], after, name, token=False)[1]


def _pair_copy(view, src, land, send_sems, recv_sems, chip):
    x, y, c = _place()
    return pltpu.make_async_remote_copy(
        src_ref=view(src, chip, 1 - c), dst_ref=land.at[chip], send_sem=send_sems.at[chip], recv_sem=recv_sems.at[chip],
        device_id=(x, y, 1 - c), device_id_type=MESH)


def _pair_start(grad, view, block, after, name):
    def body(src, land, after_ref, send_sems, recv_sems, src_thru, land_thru, token):
        for chip in range(N_CHIP):
            _pair_copy(view, src, land, send_sems, recv_sems, chip).start()
        token[...] = jnp.zeros_like(token)

    sems = pltpu.SemaphoreType.DMA((N_CHIP,))
    land = jax.ShapeDtypeStruct((N_CHIP, *block), BF16)
    return _pcall(
        body, in_specs=[IN_HBM, IN_HBM, HBM], out_specs=[SEM, SEM, IN_HBM, IN_HBM, TOKEN_SPEC],
        out_shape=[sems, sems, _hbm_like(grad), _hbm_like(land), TOKEN], input_output_aliases={0: 2, 1: 3},
        compiler_params=pltpu.CompilerParams(has_side_effects=EFFECT), name=name,
    )(_in_hbm(grad), _landing(land), after)


def _pair_wait(grad, recv, send_sems, recv_sems, view, after, name):
    def body(src, land, send, recv_s, after_ref, src_thru, land_thru):
        for chip in range(N_CHIP):
            copy = _pair_copy(view, src, land, send, recv_s, chip)
            copy.wait_send()
            copy.wait_recv()

    return _pcall(
        body, in_specs=[IN_HBM, IN_HBM, SEM, SEM, HBM], out_specs=[IN_HBM, IN_HBM],
        out_shape=[_hbm_like(grad), _hbm_like(recv)], input_output_aliases={0: 0, 1: 1},
        compiler_params=pltpu.CompilerParams(has_side_effects=EFFECT), name=name,
    )(grad, recv, send_sems, recv_sems, after)


def _chip_start(pair, after, name):
    def body(src, land, after_ref, send_sems, recv_sems, src_thru, land_thru, token):
        x, y, c = _place()
        for j, (px, py) in enumerate(_other_chips(x, y)):
            pltpu.make_async_remote_copy(
                src_ref=src.at[2 * px + py], dst_ref=land.at[2 * x + y], send_sem=send_sems.at[j], recv_sem=recv_sems.at[j],
                device_id=(px, py, c), device_id_type=MESH).start()
        token[...] = jnp.zeros_like(token)

    sems = pltpu.SemaphoreType.DMA((3,))
    return _pcall(
        body, in_specs=[IN_HBM, IN_HBM, HBM], out_specs=[SEM, SEM, IN_HBM, IN_HBM, TOKEN_SPEC],
        out_shape=[sems, sems, _hbm_like(pair), _hbm_like(pair), TOKEN], input_output_aliases={0: 2, 1: 3},
        compiler_params=pltpu.CompilerParams(has_side_effects=EFFECT), name=name,
    )(_in_hbm(pair), _landing(pair), after)


def _chip_wait(pair, parts, send_sems, recv_sems, after, name):
    def body(src, land, send, recv, after_ref, src_thru, land_thru):
        x, y, c = _place()
        for j, (px, py) in enumerate(_other_chips(x, y)):
            copy = pltpu.make_async_remote_copy(
                src_ref=src.at[2 * px + py], dst_ref=land.at[2 * px + py], send_sem=send.at[j], recv_sem=recv.at[j],
                device_id=(px, py, c), device_id_type=MESH)
            copy.wait_send()
            copy.wait_recv()

    return _pcall(
        body, in_specs=[IN_HBM, IN_HBM, SEM, SEM, HBM], out_specs=[IN_HBM, IN_HBM],
        out_shape=[_hbm_like(pair), _hbm_like(parts)], input_output_aliases={0: 0, 1: 1},
        compiler_params=pltpu.CompilerParams(has_side_effects=EFFECT), name=name,
    )(pair, parts, send_sems, recv_sems, after)


def _pair_add(core, grad, recv, block, grad_spec, name):
    _, R, C = recv.shape
    tr = block

    def body(c_ref, g_ref, r_ref, o_ref):
        o_ref[...] = (g_ref[...].astype(F32) + r_ref[...].astype(F32)).astype(BF16)

    grid_spec = pltpu.PrefetchScalarGridSpec(
        num_scalar_prefetch=1, grid=(N_CHIP, R // tr),
        in_specs=[grad_spec, pl.BlockSpec((None, tr, C), lambda k, i, c: (k, i, 0))],
        out_specs=pl.BlockSpec((None, tr, C), lambda k, i, c: (k, i, 0)))
    return _pcall(body, grid_spec=grid_spec, out_shape=jax.ShapeDtypeStruct(recv.shape, BF16),
                  compiler_params=_params("parallel", "parallel"), name=name)(core, grad, recv)


def _small_copies(gath, send_sems, recv_sems):
    x, y, c = _place()
    peers = [(x, y, 1 - c)] + [(px, py, pc) for px, py in _other_chips(x, y) for pc in (c, 1 - c)]
    pairs = []
    for a, ref in enumerate(gath):
        mine = ref.at[4 * x + 2 * y + c]
        for k, (px, py, pc) in enumerate(peers):
            sems = dict(send_sem=send_sems.at[7 * a + k], recv_sem=recv_sems.at[7 * a + k], device_id=(px, py, pc),
                        device_id_type=MESH)
            pairs.append((pltpu.make_async_remote_copy(src_ref=mine, dst_ref=mine, **sems),
                          pltpu.make_async_remote_copy(src_ref=mine, dst_ref=ref.at[4 * px + 2 * py + pc], **sems)))
    return pairs


def _small_start(landing, after, name):
    na = len(landing)

    def body(*refs):
        for send, _ in _small_copies(refs[:na], refs[na + 1], refs[na + 2]):
            send.start()
        refs[-1][...] = jnp.zeros_like(refs[-1])

    sems = pltpu.SemaphoreType.DMA((7 * na,))
    outs = _pcall(
        body, in_specs=[IN_HBM] * na + [HBM], out_specs=[SEM, SEM] + [IN_HBM] * na + [TOKEN_SPEC],
        out_shape=[sems, sems] + [_hbm_like(s) for s in landing] + [TOKEN],
        input_output_aliases={a: 2 + a for a in range(na)},
        compiler_params=pltpu.CompilerParams(has_side_effects=EFFECT), name=name,
    )(*[_in_hbm(s) for s in landing], after)
    return outs[0], outs[1], outs[2:2 + na], outs[-1]


def _small_wait(gathered, send_sems, recv_sems, after, name):
    na = len(gathered)

    def body(*refs):
        for send, arrival in _small_copies(refs[:na], refs[na], refs[na + 1]):
            send.wait_send()
            arrival.wait_recv()

    return list(_pcall(
        body, in_specs=[IN_HBM] * na + [SEM, SEM, HBM], out_specs=[IN_HBM] * na,
        out_shape=[_hbm_like(g) for g in gathered], input_output_aliases={a: a for a in range(na)},
        compiler_params=pltpu.CompilerParams(has_side_effects=EFFECT), name=name,
    )(*gathered, send_sems, recv_sems, after))


def _small_finish(gathered, params, name):
    na, npar = len(gathered), len(params)

    def body(*refs):
        g_refs, wmv = refs[:na], refs[na:na + 3 * npar]
        o_sums, o_params = refs[na + 3 * npar:2 * na + 3 * npar], refs[2 * na + 3 * npar:]
        sums = []
        for a in range(na):
            acc = g_refs[a][0]
            for k in range(1, N_DEV):
                acc = acc + g_refs[a][k]
            o_sums[a][...] = acc
            sums.append(acc)
        for j, (a, row, _, _, _) in enumerate(params):
            g = sums[a][row:row + 1, :]
            d, mn, vn = _adam_math(wmv[3 * j][...], g, wmv[3 * j + 1][...], wmv[3 * j + 2][...])
            for out, val in zip(o_params[4 * j:4 * j + 4], (g, d, mn, vn)):
                out[...] = val

    vm = pl.BlockSpec(memory_space=pltpu.VMEM)
    flat = [t for p in params for t in p[2:]]
    out_shape = [jax.ShapeDtypeStruct(g.shape[1:], F32) for g in gathered]
    out_shape += [jax.ShapeDtypeStruct(p[2].shape, F32) for p in params for _ in range(4)]
    outs = _pcall(body, in_specs=[vm] * (na + 3 * npar), out_specs=[vm] * len(out_shape), out_shape=out_shape,
                  name=name)(*gathered, *flat)
    return outs[:na], [outs[na + 4 * j:na + 4 * j + 4] for j in range(npar)]


def _local_step(x, tgt, gains, weights):
    g_pre_mix, g_post_mix, g_pre_ffn, g_post_ffn, g_sb, g_dil = gains
    S, D = x.shape
    hs = g_sb.shape[1] // HEAD_DIM
    hd = g_dil.shape[1] // HEAD_DIM
    cos2, sin_signed = _rope_tables(S)

    h1 = _rms_fwd(x, g_pre_mix + weights.start(), "rms_in")
    w_in_g = weights.w_in(h1)
    proj = _mm_nn(h1, w_in_g, BF16, "proj", tn=768)
    o_sb, ct_sb, mixed = _sb_fwd(proj, g_sb + weights.relay_out(proj), hs, hs + hd, "sb_fwd")
    o_dl, lse_dl, mixed = _dil_fwd(proj, cos2, sin_signed, g_dil + weights.after_sb(o_sb), mixed, 3 * hs, hd, "dil_fwd")
    w_out_g = weights.w_out(o_dl)
    mix = _mm_nn(mixed, w_out_g, F32, "mix_out", tn=1024)
    x2, h2 = _mid_fwd(x, mix, g_post_mix + weights.after_mix(mix), g_pre_ffn, "mid_fwd")
    w_up_g, cwb = weights.w_up(h2)
    u = _mm_nn(h2, w_up_g, BF16, "ffn_up", b_transposed=True)
    y = _geglu_fwd(u, cwb + weights.forward_down(u), "geglu_fwd")
    w_down_g = weights.w_down(y)
    f = _mm_nn(y, w_down_g, F32, "ffn_down", tn=1024, tk=2816)

    dy, df, dg_post_ffn, loss = _loss_bwd(x2, f, tgt, g_post_ffn, "loss_bwd")
    dyv = _mm_nt(df, w_down_g, BF16, "d_y", tn=1408)
    dw_down = _mm_tn(y, df, D, BF16, "dw_down", tm=1408, tn=1024)
    dc, dcw_g, dcw_v = _geglu_bwd(u, dyv, cwb + weights.grad("w_down", dw_down), "geglu_bwd")
    du = _conv_bwd(dc, cwb + weights.grad_reduce("w_down", dc), "conv_bwd")
    dh2 = _mm_nt(du, w_up_g, BF16, "d_h2", tk=1408, b_transposed=True, per_step=2)
    dw_up = _mm_tn(du, h2, D, BF16, "dw_up", tm=1408, tn=1024)
    dx2, dmix, dg_pre_ffn, dg_post_mix = _mid_bwd(
        dy, dh2, x2, mix, g_pre_ffn + weights.grad("w_up", dw_up), g_post_mix, "mid_bwd")
    dmixed = _mm_nt(dmix, w_out_g, BF16, "d_mixed", after=jnp.reshape(weights.grad_reduce("w_up", dmix), (1, 1)))
    dw_out = _mm_tn(mixed, dmix, D, BF16, "dw_out", tn=1024)
    dq_s, dk_s, dv_s, dg_sb = _sb_bwd(proj, g_sb + weights.grad("w_out", dw_out), o_sb, ct_sb, dmixed, 0, hs, "sb_bwd")
    dq_d, dk_d, dv_d, dg_dil = _dil_bwd(proj, cos2, sin_signed, g_dil + weights.grad_reduce("w_out", dq_s), o_dl, lse_dl,
                                        dmixed, hs, 3 * hs, hd, "dil_bwd")
    dproj = jnp.concatenate([dq_s, dk_s, dv_s, dq_d, dk_d, dv_d], axis=1)
    dw_in = _mm_tn(h1, dproj, w_in_g.shape[2], BF16, "dw_in", tn=768)
    weights.grad("w_in", dw_in)
    dep = weights.grad_reduce("w_in", dproj)
    dh1 = _mm_nt(dproj, w_in_g, BF16, "d_h1", tk=768, after=jnp.reshape(dep, (1, 1)), per_step=4)
    grad_x, dg_pre_mix = _first_bwd(dx2, dh1, x, g_pre_mix, "first_bwd")
    small = (dg_pre_mix, dg_post_mix, dg_pre_ffn, dg_post_ffn, dg_sb[0:1], dg_dil[0:1], jnp.concatenate([dcw_g, dcw_v], axis=1))
    weights.small(small, loss)
    return loss, grad_x, small


def _pad_cols(a, to):
    return jnp.pad(a, ((0, 0), (0, to - a.shape[1])))


def kernel(x, pre_mix_gain, post_mix_gain, pre_ffn_gain, post_ffn_gain, w_in, sb_out_gain, dil_out_gain, w_out, w_up, conv_w, conv_b, w_down, loss_target, m_pre_mix_gain, m_post_mix_gain, m_pre_ffn_gain, m_post_ffn_gain, m_w_in, m_sb_out_gain, m_dil_out_gain, m_w_out, m_w_up, m_conv_w, m_conv_b, m_w_down, v_pre_mix_gain, v_post_mix_gain, v_pre_ffn_gain, v_post_ffn_gain, v_w_in, v_sb_out_gain, v_dil_out_gain, v_w_out, v_w_up, v_conv_w, v_conv_b, v_w_down):
    xb, tb = x[0], loss_target[0]
    S, D = xb.shape
    w_in, w_out, w_up, w_down, conv_w = w_in[0], w_out[0], w_up[0], w_down[0], conv_w[0]
    n_in, e_rows = w_in.shape[1], w_out.shape[0]
    cu, half = w_up.shape[1], w_down.shape[0]
    assert cu == 2 * half and half % 16 == 0
    cup = -(-cu // LANES) * LANES
    fp = N_CHIP * cup
    px, py, pc = _place()
    me = 4 * px + 2 * py + pc
    core = jnp.reshape(pc, (1,)).astype(jnp.int32)

    w_up_t, m_up_t, v_up_t = (jnp.swapaxes(t, 0, 1) for t in (w_up, m_w_up[0], v_w_up[0]))

    def by_dev(ref, qx, qy, qc):
        return ref.at[4 * qx + 2 * qy + qc]

    def down_slot(ref, qx, qy, qc):
        return ref.at[2 * qx + qy, pl.ds(qc * half, half)]

    def by_pair(ref, chip, k):
        return ref.at[chip, k]

    def down_pair(ref, chip, k):
        return ref.at[chip, pl.ds(k * half, half)]

    def pair_spec(tr, cols):
        return pl.BlockSpec((None, None, tr, cols), lambda k, i, c: (k, c[0], i, 0))

    tr_in, tr_up = _tile(D, 512, 16), _tile(cup, 256, 16)
    grad_plan = {
        "w_in": ((N_CHIP, 2, D, n_in), by_pair, (D, n_in), tr_in, pair_spec(tr_in, n_in)),
        "w_out": ((N_CHIP, 2, e_rows, D), by_pair, (e_rows, D), e_rows, pair_spec(e_rows, D)),
        "w_up": ((N_CHIP, 2, cup, D), by_pair, (cup, D), tr_up, pair_spec(tr_up, D)),
        "w_down": ((N_CHIP, cup, D), down_pair, (half, D), half,
                   pl.BlockSpec((None, half, D), lambda k, i, c: (k, c[0], 0))),
    }

    class Exchanges:
        def __init__(self):
            self.in_flight = {}

        def start(self):
            def own_slot(shard):
                return lax.dynamic_update_index_in_dim(lax.empty((N_DEV, *shard.shape), shard.dtype), shard, me, 0)

            self.group_slots = {"in": [by_dev], "out": [by_dev], "up": [by_dev, by_dev], "down": [down_slot]}
            self.flight = {}
            sems, gath, token = _gather_start([own_slot(w_in.astype(BF16))], [by_dev], core, "gather_in_start")
            self.flight["in"] = (sems, gath)
            zero = token[0, 0]
            self.landing = {
                "out": [own_slot((w_out + zero).astype(BF16))],
                "up": [own_slot(jnp.pad(w_up_t + zero, ((0, cup - cu), (0, 0))).astype(BF16)),
                       own_slot(jnp.pad(conv_w + zero, ((0, 8 - conv_w.shape[0]), (0, cup - cu))))],
                "down": [lax.dynamic_update_slice(jnp.zeros((N_CHIP, cup, D), BF16), (w_down + zero).astype(BF16)[None],
                                                  (2 * px + py, pc * half, 0))]}
            return zero

        def begin(self, group, after):
            sems, gath, token = _gather_start(self.landing[group], self.group_slots[group], after, "gather_%s_start" % group)
            self.flight[group] = (sems, gath)
            return token

        def relay(self, group, after):
            sems, gath = self.flight[group]
            sems, gath, token = _gather_relay(gath, sems, self.group_slots[group], after, "gather_%s_relay" % group)
            self.flight[group] = (sems, gath)
            return token

        def pass_on(self, group, after):
            sems, gath = self.flight[group]
            diag_sems, gath, token = _gather_pass(gath, sems[:2], self.group_slots[group], after, "gather_%s_pass" % group)
            self.flight[group] = (sems[2:], diag_sems, gath)
            return token

        def finish(self, group, after):
            pass_sems, diag_sems, gath = self.flight[group]
            return _gather_finish(gath, pass_sems, diag_sems, self.group_slots[group], after, "gather_%s_finish" % group)

        def w_in(self, after):
            token = self.begin("up", self.begin("out", self.relay("in", after)))
            return self.finish("in", self.pass_on("in", token))[0]

        def relay_out(self, after):
            return self.relay("out", after)[0, 0]

        def after_sb(self, after):
            return self.begin("down", self.relay("up", self.pass_on("out", after)))[0, 0]

        def w_out(self, after):
            return self.finish("out", after)[0].reshape(1, N_DEV * e_rows, D)

        def after_mix(self, after):
            return self.pass_on("up", after)[0, 0]

        def w_up(self, after):
            w_up_g, cw_g = self.finish("up", after)
            cb = _pad_cols(conv_b.reshape(N_DEV, cu), cup).reshape(1, 2 * fp)
            cw_full = jnp.transpose(cw_g[:, :3, :], (1, 0, 2)).reshape(3, 2 * fp)
            cwb = jnp.concatenate([cw_full, cb, jnp.zeros((4, 2 * fp), F32)], axis=0)
            return w_up_g, cwb

        def forward_down(self, after):
            return self.relay("down", after)[0, 0]

        def w_down(self, after):
            return self.finish("down", self.pass_on("down", after))[0].reshape(1, fp, D)

        def small(self, small, loss):
            d_pre_mix, d_post_mix, d_pre_ffn, d_post_ffn, d_sb, d_dil, d_conv = small

            def rows_of(*vectors):
                n = vectors[0].shape[1]
                row = lax.broadcasted_iota(jnp.int32, (8, n), 0)
                out = jnp.zeros((8, n), F32)
                for k, vec in enumerate(vectors):
                    out = jnp.where(row == k, vec, out)
                return out

            parts = [rows_of(d_pre_mix, d_post_mix, d_pre_ffn, d_post_ffn, jnp.broadcast_to(loss[:, :1], (1, D))),
                     rows_of(d_sb, d_dil), d_conv]
            landing = [lax.dynamic_update_index_in_dim(lax.empty((N_DEV, *p.shape), F32), p, me, 0) for p in parts]
            self.small_flight = _small_start(landing, parts[0], "small_start")

        def small_sums(self, after):
            send, recv, gath, _ = self.small_flight
            gath = _small_wait(gath, send, recv, after, "small_wait")
            params = [(0, 0, pre_mix_gain, m_pre_mix_gain, v_pre_mix_gain), (0, 1, post_mix_gain, m_post_mix_gain, v_post_mix_gain),
                      (0, 2, pre_ffn_gain, m_pre_ffn_gain, v_pre_ffn_gain), (0, 3, post_ffn_gain, m_post_ffn_gain, v_post_ffn_gain),
                      (1, 0, sb_out_gain, m_sb_out_gain, v_sb_out_gain), (1, 1, dil_out_gain, m_dil_out_gain, v_dil_out_gain)]
            (gains_sum, _, conv_sum), gain_steps = _small_finish(gath, params, "small_finish")
            return gains_sum[4, 0], conv_sum, gain_steps

        def grad(self, name, dw):
            view_shape, view, block, tr, spec = grad_plan[name]
            send, recv_sems, dw, recv, token = _pair_start(dw.reshape(view_shape), view, block, core, "pair_start_" + name)
            self.in_flight[name] = (dw, recv, send, recv_sems)
            return token[0, 0]

        def grad_reduce(self, name, after):
            _, view, _, tr, spec = grad_plan[name]
            dw, recv = _pair_wait(*self.in_flight[name], view, after, "pair_wait_" + name)
            pair = _pair_add(core, dw, recv, tr, spec, "pair_add_" + name)
            send, recv_sems, pair, parts, token = _chip_start(pair, recv, "chip_start_" + name)
            self.in_flight[name] = (pair, parts, send, recv_sems)
            self.last_token = token
            return token[0, 0]

        def grad_parts(self, name, after):
            return _chip_wait(*self.in_flight[name], after, "chip_wait_" + name)

    exchanges = Exchanges()
    gains = (pre_mix_gain, post_mix_gain, pre_ffn_gain, post_ffn_gain, sb_out_gain, dil_out_gain)
    loss, grad_x, small = _local_step(xb, tb, gains, exchanges)


    chip_ids = jnp.stack([2 * px + py, 2 * (1 - px) + py, 2 * px + 1 - py, 2 * (1 - px) + 1 - py]).astype(jnp.int32)
    out_w_down = _adamw_chips(w_down, *exchanges.grad_parts("w_down", exchanges.small_flight[3]), chip_ids, m_w_down[0], v_w_down[0], "adam_w_down")
    out_up_t = _adamw_chips(w_up_t, *exchanges.grad_parts("w_up", out_w_down[1]), chip_ids, m_up_t, v_up_t, "adam_w_up")
    out_w_up = [jnp.swapaxes(o, 0, 1) for o in out_up_t]
    out_w_out = _adamw_chips(w_out, *exchanges.grad_parts("w_out", out_up_t[1]), chip_ids, m_w_out[0], v_w_out[0], "adam_w_out")
    loss_out, g_conv, gain_steps = exchanges.small_sums(out_w_out[1])
    out_pre_mix, out_post_mix, out_pre_ffn, out_post_ffn, out_sb, out_dil = gain_steps
    g_conv_b = g_conv[3].reshape(N_DEV, cup)[:, :cu].reshape(1, N_DEV * cu)
    g_conv_w = lax.dynamic_index_in_dim(g_conv[0:3].reshape(3, N_DEV, cup), me, axis=1, keepdims=False)[:, :cu]
    out_conv_b = _adamw(conv_b, g_conv_b[None], m_conv_b, v_conv_b, "adam_conv_b")
    out_conv_w = _adamw(conv_w, g_conv_w[None], m_conv_w[0], v_conv_w[0], "adam_conv_w")
    out_w_in = _adamw_chips(w_in, *exchanges.grad_parts("w_in", out_conv_w[1]), chip_ids, m_w_in[0], v_w_in[0], "adam_w_in")

    order = [out_pre_mix, out_post_mix, out_pre_ffn, out_post_ffn, [o[None] for o in out_w_in], out_sb, out_dil,
             [o[None] for o in out_w_out], [o[None] for o in out_w_up], [o[None] for o in out_conv_w], out_conv_b,
             [o[None] for o in out_w_down]]
    outs = [loss_out, grad_x[None]]
    for k in range(4):
        outs += [o[k] for o in order]
    return tuple(outs)
```

```python
import math

import jax
import jax.numpy as jnp
from jax import lax
from jax.experimental import pallas as pl
from jax.experimental.pallas import tpu as pltpu

F32 = jnp.float32
BF16 = jnp.bfloat16
HEAD_DIM = 128
LANES = 128
KEY_BLOCK = 128
DILATIONS = (1, 4, 16)
RMS_EPS = 1e-6
ROPE_THETA = 10000.0
NEG = -1e30
ADAM_LR, ADAM_B1, ADAM_B2, ADAM_EPS, ADAM_WD, ADAM_STEP = 0.001, 0.9, 0.999, 1e-08, 0.01, 10
MESH = pl.DeviceIdType.MESH
N_DEV = 8
N_CHIP = 4
HBM = pl.BlockSpec(memory_space=pl.ANY)
VMEM_LIMIT = 56 * 1024 * 1024

_pcall = pl.pallas_call


def _tile(n, pref, mult=LANES):
    best = None
    t = mult
    while t <= min(n, pref):
        if n % t == 0:
            best = t
        t += mult
    return n if best is None else best


def _params(*sem):
    return pltpu.CompilerParams(dimension_semantics=sem, vmem_limit_bytes=VMEM_LIMIT)


def _dot(a, b, dims):
    return lax.dot_general(a, b, (dims, ((), ())), preferred_element_type=F32)


NN = ((1,), (0,))
NT = ((1,), (1,))
TN = ((0,), (0,))


def _mm_body(dims, nk, tile):
    if nk == 1:
        def single(a_ref, b_ref, o_ref):
            o_ref[...] = _dot(a_ref[...].astype(BF16), b_ref[...].astype(BF16), dims).astype(o_ref.dtype)

        return single, []

    def body(a_ref, b_ref, o_ref, acc_ref):
        k = pl.program_id(2)

        @pl.when(k == 0)
        def _():
            acc_ref[...] = jnp.zeros_like(acc_ref)

        acc_ref[...] += _dot(a_ref[...].astype(BF16), b_ref[...].astype(BF16), dims)

        @pl.when(k == nk - 1)
        def _():
            o_ref[...] = acc_ref[...].astype(o_ref.dtype)

    return body, [pltpu.VMEM(tile, F32)]


def _mm_nn(a, b3, out_dtype, name, tm=1024, tn=1408, tk=2048, b_transposed=False):
    M, K = a.shape
    C, n = b3.shape[0], b3.shape[1 if b_transposed else 2]
    tm, tk, tn = _tile(M, tm, 8), _tile(K, tk), _tile(n, tn)
    npc, nk = n // tn, K // tk
    body, scratch = _mm_body(NT if b_transposed else NN, nk, (tm, tn))
    b_spec = (pl.BlockSpec((None, tn, tk), lambda i, j, k: (j // npc, j % npc, k)) if b_transposed
              else pl.BlockSpec((None, tk, tn), lambda i, j, k: (j // npc, k, j % npc)))
    return _pcall(
        body, grid=(M // tm, C * npc, nk),
        in_specs=[pl.BlockSpec((tm, tk), lambda i, j, k: (i, k)), b_spec],
        out_specs=pl.BlockSpec((tm, tn), lambda i, j, k: (i, j)),
        out_shape=jax.ShapeDtypeStruct((M, C * n), out_dtype), scratch_shapes=scratch,
        compiler_params=_params("parallel", "parallel", "arbitrary"), name=name)(a, b3)


def _mm_nt(a, b3, out_dtype, name, tm=1024, tn=1024, tk=2048, after=None, b_transposed=False, per_step=1):
    M, _ = a.shape
    C, N, n = (b3.shape[0], b3.shape[2], b3.shape[1]) if b_transposed else b3.shape
    tm, tn, tk = _tile(M, tm, 8), _tile(N, tn), _tile(n, tk)
    dims = NN if b_transposed else NT
    extra = [] if after is None else [after]
    if per_step > 1 and tk == n and C % per_step == 0:
        nk, scratch = C // per_step, [pltpu.VMEM((tm, tn), F32)]
        b3 = b3.reshape(nk, per_step, *b3.shape[1:])
        a_spec = pl.BlockSpec((tm, per_step * n), lambda i, j, k: (i, k))
        if b_transposed:
            b_spec = pl.BlockSpec((None, per_step, n, tn), lambda i, j, k: (k, 0, 0, j))
        else:
            b_spec = pl.BlockSpec((None, per_step, tn, n), lambda i, j, k: (k, 0, j, 0))

        def body(a_ref, b_ref, *rest):
            o_ref, acc_ref = rest[len(extra):]
            k = pl.program_id(2)

            @pl.when(k == 0)
            def _():
                acc_ref[...] = jnp.zeros_like(acc_ref)

            b = b_ref[...].astype(BF16)
            b = b.reshape(per_step * n, tn) if b_transposed else jnp.concatenate([b[u] for u in range(per_step)], axis=1)
            acc_ref[...] += _dot(a_ref[...].astype(BF16), b, dims)

            @pl.when(k == nk - 1)
            def _():
                o_ref[...] = acc_ref[...].astype(o_ref.dtype)
    else:
        kpc = n // tk
        nk = C * kpc
        inner, scratch = _mm_body(dims, nk, (tm, tn))
        a_spec = pl.BlockSpec((tm, tk), lambda i, j, k: (i, k))
        b_spec = (pl.BlockSpec((None, tk, tn), lambda i, j, k: (k // kpc, k % kpc, j)) if b_transposed
                  else pl.BlockSpec((None, tn, tk), lambda i, j, k: (k // kpc, j, k % kpc)))

        def body(a_ref, b_ref, *rest):
            inner(a_ref, b_ref, *rest[len(extra):])

    return _pcall(
        body, grid=(M // tm, N // tn, nk), in_specs=[a_spec, b_spec] + [HBM] * len(extra),
        out_specs=pl.BlockSpec((tm, tn), lambda i, j, k: (i, j)),
        out_shape=jax.ShapeDtypeStruct((M, N), out_dtype), scratch_shapes=scratch,
        compiler_params=_params("parallel", "parallel", "arbitrary"), name=name)(a, b3, *extra)


def _mm_tn(x, y, n, out_dtype, name, tm=1024, tn=1408, tk=2048, after=None):
    S, P = x.shape
    C = y.shape[1] // n
    tm, tn, tk = _tile(P, tm), _tile(n, tn), _tile(S, tk, 8)
    npc, nk = n // tn, S // tk
    inner, scratch = _mm_body(TN, nk, (tm, tn))
    extra = [] if after is None else [after]

    def body(x_ref, y_ref, *rest):
        inner(x_ref, y_ref, *rest[len(extra):])

    return _pcall(
        body, grid=(P // tm, C * npc, nk),
        in_specs=[pl.BlockSpec((tk, tm), lambda i, j, k: (k, i)),
                  pl.BlockSpec((tk, tn), lambda i, j, k: (k, j))] + [HBM] * len(extra),
        out_specs=pl.BlockSpec((None, tm, tn), lambda i, j, k: (j // npc, i, j % npc)),
        out_shape=jax.ShapeDtypeStruct((C, P, n), out_dtype), scratch_shapes=scratch,
        compiler_params=_params("parallel", "parallel", "arbitrary"), name=name)(x, y, *extra)


def _rms_scale(v):
    return lax.rsqrt(jnp.mean(v * v, axis=-1, keepdims=True) + RMS_EPS)


def _rms_bwd(gy, v, r):
    return r * gy - v * (r * r * r * jnp.mean(gy * v, axis=-1, keepdims=True))


def _rows_spec(tm, d):
    return pl.BlockSpec((tm, d), lambda i: (i, 0))


def _vec_spec(d):
    return pl.BlockSpec((1, d), lambda i: (0, 0))


def _rms_fwd(x, g, name, tm=256):
    S, D = x.shape

    def body(x_ref, g_ref, h_ref):
        v = x_ref[...]
        h_ref[...] = (v * _rms_scale(v) * g_ref[...]).astype(BF16)

    return _pcall(body, grid=(S // tm,), in_specs=[_rows_spec(tm, D), _vec_spec(D)], out_specs=_rows_spec(tm, D),
                  out_shape=jax.ShapeDtypeStruct((S, D), BF16), compiler_params=_params("parallel"), name=name)(x, g)


def _mid_fwd(x, mix, g_post, g_pre, name, tm=256):
    S, D = x.shape

    def body(x_ref, m_ref, gp_ref, gn_ref, x2_ref, h_ref):
        m = m_ref[...]
        x2 = x_ref[...] + m * _rms_scale(m) * gp_ref[...]
        x2_ref[...] = x2
        h_ref[...] = (x2 * _rms_scale(x2) * gn_ref[...]).astype(BF16)

    return _pcall(body, grid=(S // tm,), in_specs=[_rows_spec(tm, D), _rows_spec(tm, D), _vec_spec(D), _vec_spec(D)],
                  out_specs=[_rows_spec(tm, D), _rows_spec(tm, D)],
                  out_shape=[jax.ShapeDtypeStruct((S, D), F32), jax.ShapeDtypeStruct((S, D), BF16)],
                  compiler_params=_params("parallel"), name=name)(x, mix, g_post, g_pre)


def _loss_bwd(x2, f, tgt, g_post, name, tm=256):
    S, D = x2.shape

    def body(x2_ref, f_ref, t_ref, g_ref, dy_ref, df_ref, dg_ref, ls_ref):
        i = pl.program_id(0)

        @pl.when(i == 0)
        def _():
            dg_ref[...] = jnp.zeros_like(dg_ref)
            ls_ref[...] = jnp.zeros_like(ls_ref)

        fv = f_ref[...]
        r = _rms_scale(fv)
        g = g_ref[...]
        err = x2_ref[...] + fv * r * g - t_ref[...]
        ls_ref[...] += jnp.broadcast_to(0.5 * jnp.sum(jnp.mean(err * err, axis=-1, keepdims=True), axis=0, keepdims=True), ls_ref.shape)
        dy = err * (1.0 / D)
        dy_ref[...] = dy
        df_ref[...] = _rms_bwd(dy * g, fv, r).astype(BF16)
        dg_ref[...] += jnp.sum(dy * fv * r, axis=0, keepdims=True)

    return _pcall(body, grid=(S // tm,),
                  in_specs=[_rows_spec(tm, D), _rows_spec(tm, D), _rows_spec(tm, D), _vec_spec(D)],
                  out_specs=[_rows_spec(tm, D), _rows_spec(tm, D), _vec_spec(D), _vec_spec(LANES)],
                  out_shape=[jax.ShapeDtypeStruct((S, D), F32), jax.ShapeDtypeStruct((S, D), BF16),
                             jax.ShapeDtypeStruct((1, D), F32), jax.ShapeDtypeStruct((1, LANES), F32)],
                  compiler_params=_params("arbitrary"), name=name)(x2, f, tgt, g_post)


def _mid_bwd(dy, dh2, x2, mix, g_pre, g_post, name, tm=256):
    S, D = dy.shape

    def body(dy_ref, dh_ref, x2_ref, m_ref, gn_ref, gp_ref, dx2_ref, dm_ref, dgn_ref, dgp_ref):
        i = pl.program_id(0)

        @pl.when(i == 0)
        def _():
            dgn_ref[...] = jnp.zeros_like(dgn_ref)
            dgp_ref[...] = jnp.zeros_like(dgp_ref)

        x2, dh = x2_ref[...], dh_ref[...].astype(F32)
        r = _rms_scale(x2)
        dx2 = dy_ref[...] + _rms_bwd(dh * gn_ref[...], x2, r)
        dgn_ref[...] += jnp.sum(dh * x2 * r, axis=0, keepdims=True)
        dx2_ref[...] = dx2
        m = m_ref[...]
        rm = _rms_scale(m)
        dm_ref[...] = _rms_bwd(dx2 * gp_ref[...], m, rm).astype(BF16)
        dgp_ref[...] += jnp.sum(dx2 * m * rm, axis=0, keepdims=True)

    return _pcall(body, grid=(S // tm,),
                  in_specs=[_rows_spec(tm, D)] * 4 + [_vec_spec(D)] * 2,
                  out_specs=[_rows_spec(tm, D), _rows_spec(tm, D), _vec_spec(D), _vec_spec(D)],
                  out_shape=[jax.ShapeDtypeStruct((S, D), F32), jax.ShapeDtypeStruct((S, D), BF16),
                             jax.ShapeDtypeStruct((1, D), F32), jax.ShapeDtypeStruct((1, D), F32)],
                  compiler_params=_params("arbitrary"), name=name)(dy, dh2, x2, mix, g_pre, g_post)


def _first_bwd(dx2, dh1, x, g_pre, name, tm=256):
    S, D = x.shape

    def body(dx2_ref, dh_ref, x_ref, g_ref, gx_ref, dg_ref):
        i = pl.program_id(0)

        @pl.when(i == 0)
        def _():
            dg_ref[...] = jnp.zeros_like(dg_ref)

        xv, dh = x_ref[...], dh_ref[...].astype(F32)
        r = _rms_scale(xv)
        gx_ref[...] = dx2_ref[...] + _rms_bwd(dh * g_ref[...], xv, r)
        dg_ref[...] += jnp.sum(dh * xv * r, axis=0, keepdims=True)

    return _pcall(body, grid=(S // tm,), in_specs=[_rows_spec(tm, D)] * 3 + [_vec_spec(D)],
                  out_specs=[_rows_spec(tm, D), _vec_spec(D)],
                  out_shape=[jax.ShapeDtypeStruct((S, D), F32), jax.ShapeDtypeStruct((1, D), F32)],
                  compiler_params=_params("arbitrary"), name=name)(dx2, dh1, x, g_pre)


def _logsig_pair(z):
    lb = jnp.minimum(z, 0.0) - jnp.log(1.0 + jnp.exp(-jnp.abs(z)))
    return lb, lb - z


SB_KEY_BLOCK = 256


def _sum_matrix(strict):
    ia = lax.broadcasted_iota(jnp.int32, (SB_KEY_BLOCK, SB_KEY_BLOCK), 0)
    ib = lax.broadcasted_iota(jnp.int32, (SB_KEY_BLOCK, SB_KEY_BLOCK), 1)
    return ((ia > ib) if strict == ">" else (ia < ib)).astype(BF16)


def _row_total(sums, v, col):
    return jnp.broadcast_to(sums[:, col:col + 1] + v[:, col:col + 1], (v.shape[0], LANES))


def _lanes(c, width):
    return jnp.tile(c, (1, width // LANES))


def _split_dot(v, u):
    hi = v.astype(BF16)
    lo = (v - hi.astype(F32)).astype(BF16)
    return _dot(hi, u, NN) + _dot(lo, u, NN)


def _head_out(o, g):
    return o * _rms_scale(o) * g


def _sb_fwd(proj, gain, n_heads, mixed_heads, name, tq=1024):
    S = proj.shape[0]
    H, tk = n_heads, SB_KEY_BLOCK
    tq = _tile(S, tq, 2 * tk)
    scale = HEAD_DIM ** -0.5

    def body(q_ref, k_ref, v_ref, g_ref, o_ref, ct_ref, mx_ref, oacc, cacc):
        i = pl.program_id(1)
        oacc[...] = jnp.zeros_like(oacc)
        cacc[...] = jnp.zeros_like(cacc)
        sums = _sum_matrix(">")

        def run(blocks):
            scored = []
            for k0, r0, diagonal in blocks:
                rows = pl.ds(r0, tq - r0)
                lb, lk = _logsig_pair(_dot(q_ref[rows, :].astype(BF16), k_ref[pl.ds(k0, tk), :].astype(BF16), NT) * scale)
                causal = None
                if diagonal:
                    causal = (lax.broadcasted_iota(jnp.int32, (tq - r0, tk), 1)
                              < lax.broadcasted_iota(jnp.int32, (tq - r0, tk), 0))
                    lk = jnp.where(causal, lk, 0.0)
                scored.append((k0, rows, causal, lb, lk))
            summed = [(k0, rows, causal, lb, lk, _split_dot(lk, sums)) for k0, rows, causal, lb, lk in scored]
            weights = []
            for k0, rows, causal, lb, lk, after in summed:
                c = cacc[rows, :]
                a = jnp.exp(lb + after + _lanes(c, tk))
                if causal is not None:
                    a = jnp.where(causal, a, 0.0)
                cacc[rows, :] = c + _row_total(after, lk, 0)
                weights.append((k0, rows, a.astype(BF16)))
            for k0, rows, a in weights:
                oacc[rows, :] += _dot(a, v_ref[pl.ds(k0, tk), :].astype(BF16), NN)

        for d in reversed(range(0, tq // tk, 2)):
            run([(pl.multiple_of(i * tq + e * tk, tk), e * tk, True) for e in (d + 1, d)])
        per_trip = tq // tk

        def step(it, carry):
            k0 = pl.multiple_of((i - 1 - it) * tq, tq)
            run([(pl.multiple_of(k0 + e * tk, tk), 0, False) for e in reversed(range(per_trip))])
            return carry

        lax.fori_loop(0, i, step, 0)
        o = oacc[...]
        o_ref[...] = o
        ct_ref[...] = cacc[...]
        mx_ref[...] = _head_out(o, g_ref[...]).astype(BF16)

    blk = pl.BlockSpec((tq, HEAD_DIM), lambda h, i: (i, h))
    return _pcall(
        body, grid=(H, S // tq),
        in_specs=[blk, pl.BlockSpec((S, HEAD_DIM), lambda h, i: (0, H + h)),
                  pl.BlockSpec((S, HEAD_DIM), lambda h, i: (0, 2 * H + h)), pl.BlockSpec((1, HEAD_DIM), lambda h, i: (0, h))],
        out_specs=[blk, blk, blk],
        out_shape=[jax.ShapeDtypeStruct((S, H * HEAD_DIM), F32), jax.ShapeDtypeStruct((S, H * HEAD_DIM), F32),
                   jax.ShapeDtypeStruct((S, mixed_heads * HEAD_DIM), BF16)],
        scratch_shapes=[pltpu.VMEM((tq, HEAD_DIM), F32), pltpu.VMEM((tq, LANES), F32)],
        compiler_params=_params("parallel", "arbitrary"), name=name)(proj, proj, proj, gain)


def _sb_bwd(proj, gain, o_raw, ctot, dmixed, dm_col0, n_heads, name, tq=1024):
    S = proj.shape[0]
    H, tk = n_heads, SB_KEY_BLOCK
    tq = _tile(S, tq, 2 * tk)
    nq = S // tq
    scale = HEAD_DIM ** -0.5

    def body(q_ref, k_ref, v_ref, g_ref, o_ref, ct_ref, dm_ref, dproj_ref, dg_ref,
             dkacc, dvacc, dqacc, pfx, gcar, dos, stage_q, stage_k, stage_v, out_sems):
        h, i = pl.program_id(0), pl.program_id(1)

        @pl.when(i == 0)
        def _():
            dkacc[...] = jnp.zeros_like(dkacc)
            dvacc[...] = jnp.zeros_like(dvacc)
            dg_ref[...] = jnp.zeros_like(dg_ref)

        o, dm, g = o_ref[...], dm_ref[...].astype(F32), g_ref[...]
        r = _rms_scale(o)
        dos[...] = _rms_bwd(dm * g, o, r).astype(BF16)
        dg_ref[...] += jnp.broadcast_to(jnp.sum(dm * o * r, axis=0, keepdims=True), dg_ref.shape)
        dqacc[...] = jnp.zeros_like(dqacc)
        pfx[...] = jnp.zeros_like(pfx)
        gcar[...] = jnp.zeros_like(gcar)
        later, earlier = _sum_matrix(">"), _sum_matrix("<")

        def run(blocks):
            scored = []
            for k0, r0, diagonal in blocks:
                rows, keys = pl.ds(r0, tq - r0), pl.ds(k0, tk)
                lb, lk = _logsig_pair(_dot(q_ref[rows, :].astype(BF16), k_ref[keys, :].astype(BF16), NT) * scale)
                da = _dot(dos[rows, :], v_ref[keys, :].astype(BF16), NT)
                causal = None
                if diagonal:
                    causal = (lax.broadcasted_iota(jnp.int32, (tq - r0, tk), 1)
                              < lax.broadcasted_iota(jnp.int32, (tq - r0, tk), 0))
                    lk = jnp.where(causal, lk, 0.0)
                scored.append((rows, keys, causal, lb, lk, da))
            summed = [(*blk, _split_dot(blk[4], later)) for blk in scored]
            weighted = []
            for rows, keys, causal, lb, lk, da, after in summed:
                p = pfx[rows, :] + _row_total(after, lk, 0)
                pfx[rows, :] = p
                a = jnp.exp(lb + after + _lanes(ct_ref[rows, :] - p, tk))
                if causal is not None:
                    a = jnp.where(causal, a, 0.0)
                dl = da * a
                weighted.append((rows, keys, causal, lb, a.astype(BF16), dl, _dot(dl.astype(BF16), earlier, NN)))
            cotangents = []
            for rows, keys, causal, lb, a, dl, before in weighted:
                gc = gcar[rows, :]
                gcar[rows, :] = gc + _row_total(before, dl, tk - 1)
                sig = jnp.exp(lb)
                gsum = (before + _lanes(gc, tk)) * sig
                if causal is not None:
                    gsum = jnp.where(causal, gsum, 0.0)
                cotangents.append((rows, keys, a, ((dl * (1.0 - sig) - gsum) * scale).astype(BF16)))
            for rows, keys, a, dz in cotangents:
                q, do = q_ref[rows, :].astype(BF16), dos[rows, :]
                dvacc[keys, :] += _dot(a, do, TN)
                dqacc[rows, :] += _dot(dz, k_ref[keys, :].astype(BF16), NN)
                dkacc[keys, :] += _dot(dz, q, TN)

        per_trip = tq // tk

        def step(j, carry):
            k0 = pl.multiple_of(j * tq, tq)
            run([(pl.multiple_of(k0 + e * tk, tk), 0, False) for e in range(per_trip)])
            return carry

        lax.fori_loop(0, i, step, 0)
        for d in range(0, tq // tk, 2):
            run([(pl.multiple_of(i * tq + e * tk, tk), e * tk, True) for e in (d, d + 1)])
        def columns(block):
            return pl.ds(pl.multiple_of(block * HEAD_DIM, HEAD_DIM), HEAD_DIM)

        stage_q[...] = dqacc[...].astype(BF16)
        dq_out = pltpu.make_async_copy(stage_q, dproj_ref.at[pl.ds(pl.multiple_of(i * tq, tq), tq), columns(h)], out_sems.at[0])
        dq_out.start()
        dq_out.wait()

        @pl.when(i == nq - 1)
        def _():
            stage_k[...] = dkacc[...].astype(BF16)
            stage_v[...] = dvacc[...].astype(BF16)
            outs = [pltpu.make_async_copy(stage_k, dproj_ref.at[:, columns(H + h)], out_sems.at[1]),
                    pltpu.make_async_copy(stage_v, dproj_ref.at[:, columns(2 * H + h)], out_sems.at[2])]
            for cp in outs:
                cp.start()
            for cp in outs:
                cp.wait()

    blk = pl.BlockSpec((tq, HEAD_DIM), lambda h, i: (i, h))
    W = H * HEAD_DIM
    return _pcall(
        body, grid=(H, nq),
        in_specs=[blk, pl.BlockSpec((S, HEAD_DIM), lambda h, i: (0, H + h)),
                  pl.BlockSpec((S, HEAD_DIM), lambda h, i: (0, 2 * H + h)), pl.BlockSpec((1, HEAD_DIM), lambda h, i: (0, h)),
                  blk, blk, pl.BlockSpec((tq, HEAD_DIM), lambda h, i: (i, dm_col0 + h))],
        out_specs=[HBM, pl.BlockSpec((8, HEAD_DIM), lambda h, i: (0, h))],
        out_shape=[jax.ShapeDtypeStruct(proj.shape, BF16), jax.ShapeDtypeStruct((8, W), F32)],
        scratch_shapes=[pltpu.VMEM((S, HEAD_DIM), F32), pltpu.VMEM((S, HEAD_DIM), F32), pltpu.VMEM((tq, HEAD_DIM), F32),
                        pltpu.VMEM((tq, LANES), F32), pltpu.VMEM((tq, LANES), F32), pltpu.VMEM((tq, HEAD_DIM), BF16),
                        pltpu.VMEM((tq, HEAD_DIM), BF16), pltpu.VMEM((S, HEAD_DIM), BF16), pltpu.VMEM((S, HEAD_DIM), BF16),
                        pltpu.SemaphoreType.DMA((3,))],
        compiler_params=_params("arbitrary", "arbitrary"), name=name)(proj, proj, proj, gain, o_raw, ctot, dmixed)


def _rope_tables(S):
    inv_freq = ROPE_THETA ** (-jnp.arange(0, HEAD_DIM, 2, dtype=F32) / HEAD_DIM)
    ang = jnp.arange(S, dtype=F32)[:, None] * inv_freq[None, :]
    cos, sin = jnp.cos(ang), jnp.sin(ang)
    return jnp.concatenate([cos, cos], axis=1), jnp.concatenate([-sin, sin], axis=1)


def _rope(v, cos2, sin_signed):
    return v * cos2 + pltpu.roll(v, HEAD_DIM // 2, axis=1) * sin_signed


def _dil_rows(d, r, l0, n):
    if d == 1:
        return pl.ds(l0 if isinstance(l0, int) else pl.multiple_of(l0, KEY_BLOCK), n)
    return pl.ds(r + d * l0, n, stride=d)


def _dil_blocks(S, visit):
    B = KEY_BLOCK
    group = 16
    for b, d in enumerate(DILATIONS):
        nb = S // d // B
        if nb == 1:
            g = math.gcd(d, group)

            def trip(t, carry, b=b, d=d, g=g):
                visit([(b, d, t * g + u, 0, True) for u in range(g)])
                return carry

            lax.fori_loop(0, d // g, trip, 0)
        elif d == 1:
            visit([(b, d, 0, 0, True)])
            g = max(k for k in range(1, group + 2) if (nb - 1) % k == 0)

            def trip(t, carry, b=b, d=d, g=g):
                visit([(b, d, 0, (1 + t * g + u) * B, False) for u in range(g)])
                return carry

            lax.fori_loop(0, (nb - 1) // g, trip, 0)
        else:
            g = math.gcd(d, max(group // nb, 1))

            def trip(t, carry, b=b, d=d, nb=nb, g=g):
                visit([(b, d, t * g + u, n * B, n == 0) for u in range(g) for n in range(nb)])
                return carry

            lax.fori_loop(0, d // g, trip, 0)


def _dil_mask(first):
    B = KEY_BLOCK
    nk = B if first else 2 * B
    iq = lax.broadcasted_iota(jnp.int32, (B, nk), 0)
    ik = lax.broadcasted_iota(jnp.int32, (B, nk), 1)
    return (ik <= iq) if first else ((ik >= iq) & (ik <= iq + B))


def _dil_fwd(proj, cos2, sin_signed, gain, mixed, col0, n_heads, name):
    S = proj.shape[0]
    H, B = n_heads, KEY_BLOCK
    scale = HEAD_DIM ** -0.5
    rc = _tile(S, 256, 8)

    def body(q_ref, k_ref, v_ref, c_ref, s_ref, g_ref, mixed_in, o_ref, l_ref, mx_ref, qr, kr, vf, *per_branch):
        ob, lb = per_branch[:len(DILATIONS)], per_branch[len(DILATIONS):]

        def rope_rows(t, carry):
            rows = pl.ds(pl.multiple_of(t * rc, rc), rc)
            qr[rows, :] = _rope(q_ref[rows, :].astype(F32), c_ref[rows, :], s_ref[rows, :])
            kr[rows, :] = _rope(k_ref[rows, :].astype(F32), c_ref[rows, :], s_ref[rows, :])
            vf[rows, :] = v_ref[rows, :].astype(F32)
            return carry

        lax.fori_loop(0, S // rc, rope_rows, 0)

        def visit(blocks):
            scores = []
            for b, d, r, l0, first in blocks:
                qrows = _dil_rows(d, r, l0, B)
                krows = qrows if first else _dil_rows(d, r, l0 - B, 2 * B)
                s = _dot(qr[qrows, :].astype(BF16), kr[krows, :].astype(BF16), NT) * scale
                scores.append((b, qrows, krows, jnp.where(_dil_mask(first), s, NEG)))
            weights = []
            for b, qrows, krows, s in scores:
                m = jnp.max(s, axis=1, keepdims=True)
                p = jnp.exp(s - m)
                den = jnp.sum(p, axis=1, keepdims=True)
                lb[b][qrows, :] = jnp.broadcast_to(m + jnp.log(den), (B, LANES))
                weights.append((b, qrows, krows, p.astype(BF16), den))
            for b, qrows, krows, p, den in weights:
                ob[b][qrows, :] = _dot(p, vf[krows, :].astype(BF16), NN) / den

        _dil_blocks(S, visit)

        def combine(t, carry):
            rows = pl.ds(pl.multiple_of(t * rc, rc), rc)
            l0, l1, l2 = lb[0][rows, :], lb[1][rows, :], lb[2][rows, :]
            m = jnp.maximum(jnp.maximum(l0, l1), l2)
            w0, w1, w2 = jnp.exp(l0 - m), jnp.exp(l1 - m), jnp.exp(l2 - m)
            den = w0 + w1 + w2
            o = (w0 * ob[0][rows, :] + w1 * ob[1][rows, :] + w2 * ob[2][rows, :]) / den
            o_ref[rows, :] = o
            l_ref[rows, :] = m + jnp.log(den)
            mx_ref[rows, :] = _head_out(o, g_ref[...]).astype(BF16)
            return carry

        lax.fori_loop(0, S // rc, combine, 0)

    def col(k):
        return pl.BlockSpec((S, HEAD_DIM), lambda h: (0, col0 + k * H + h))

    tab = pl.BlockSpec((S, HEAD_DIM), lambda h: (0, 0))
    out = pl.BlockSpec((S, HEAD_DIM), lambda h: (0, h))
    W = H * HEAD_DIM
    first = mixed.shape[1] // HEAD_DIM - H
    return _pcall(
        body, grid=(H,),
        in_specs=[col(0), col(1), col(2), tab, tab, pl.BlockSpec((1, HEAD_DIM), lambda h: (0, h)), HBM],
        out_specs=[out, out, pl.BlockSpec((S, HEAD_DIM), lambda h: (0, first + h))],
        out_shape=[jax.ShapeDtypeStruct((S, W), F32), jax.ShapeDtypeStruct((S, W), F32),
                   jax.ShapeDtypeStruct(mixed.shape, BF16)],
        input_output_aliases={6: 2},
        scratch_shapes=[pltpu.VMEM((S, HEAD_DIM), F32)] * (3 + 2 * len(DILATIONS)),
        compiler_params=_params("parallel"), name=name)(proj, proj, proj, cos2, sin_signed, gain, mixed)


def _dil_bwd(proj, cos2, sin_signed, gain, o_raw, lse, dmixed, dproj, dm_col0, col0, n_heads, name):
    S = proj.shape[0]
    H, B = n_heads, KEY_BLOCK
    scale = HEAD_DIM ** -0.5
    rc = _tile(S, 256, 8)

    def body(q_ref, k_ref, v_ref, c_ref, s_ref, g_ref, o_ref, l_ref, dm_ref, dproj_in, dproj_ref, dg_ref,
             qr, kr, vf, dos, dsum, dqr, dkr, dvv, stage_q, stage_k, stage_v, out_sems):
        dg_ref[...] = jnp.zeros_like(dg_ref)

        def prep(t, carry):
            rows = pl.ds(pl.multiple_of(t * rc, rc), rc)
            qr[rows, :] = _rope(q_ref[rows, :].astype(F32), c_ref[rows, :], s_ref[rows, :])
            kr[rows, :] = _rope(k_ref[rows, :].astype(F32), c_ref[rows, :], s_ref[rows, :])
            vf[rows, :] = v_ref[rows, :].astype(F32)
            o, dm = o_ref[rows, :], dm_ref[rows, :].astype(F32)
            r = _rms_scale(o)
            do = _rms_bwd(dm * g_ref[...], o, r)
            dg_ref[...] += jnp.broadcast_to(jnp.sum(dm * o * r, axis=0, keepdims=True), dg_ref.shape)
            dos[rows, :] = do
            dsum[rows, :] = jnp.broadcast_to(jnp.sum(do * o, axis=1, keepdims=True), (rc, LANES))
            dqr[rows, :] = jnp.zeros((rc, HEAD_DIM), F32)
            dkr[rows, :] = jnp.zeros((rc, HEAD_DIM), F32)
            dvv[rows, :] = jnp.zeros((rc, HEAD_DIM), F32)
            return carry

        lax.fori_loop(0, S // rc, prep, 0)

        def visit(blocks):
            products = []
            for b, d, r, l0, first in blocks:
                qrows = _dil_rows(d, r, l0, B)
                krows = qrows if first else _dil_rows(d, r, l0 - B, 2 * B)
                qs, ks = qr[qrows, :].astype(BF16), kr[krows, :].astype(BF16)
                do = dos[qrows, :].astype(BF16)
                s = jnp.where(_dil_mask(first), _dot(qs, ks, NT) * scale, NEG)
                dp = _dot(do, vf[krows, :].astype(BF16), NT)
                products.append((qrows, krows, qs, ks, do, s, dp))
            cotangents = []
            for qrows, krows, qs, ks, do, s, dp in products:
                p = jnp.exp(s - l_ref[qrows, :][:, 0:1])
                ds = (p * (dp - dsum[qrows, :][:, 0:1]) * scale).astype(BF16)
                cotangents.append((qrows, krows, qs, ks, do, p.astype(BF16), ds))
            for qrows, krows, qs, ks, do, p, ds in cotangents:
                dqr[qrows, :] += _dot(ds, ks, NN)
                dkr[krows, :] += _dot(ds, qs, TN)
                dvv[krows, :] += _dot(p, do, TN)

        _dil_blocks(S, visit)

        def finish(t, carry):
            rows = pl.ds(pl.multiple_of(t * rc, rc), rc)
            c, s = c_ref[rows, :], s_ref[rows, :]
            dq, dk = dqr[rows, :], dkr[rows, :]
            stage_q[rows, :] = (dq * c + pltpu.roll(dq * s, HEAD_DIM // 2, axis=1)).astype(BF16)
            stage_k[rows, :] = (dk * c + pltpu.roll(dk * s, HEAD_DIM // 2, axis=1)).astype(BF16)
            stage_v[rows, :] = dvv[rows, :].astype(BF16)
            return carry

        lax.fori_loop(0, S // rc, finish, 0)
        h = pl.program_id(0)
        outs = [pltpu.make_async_copy(
            stage, dproj_ref.at[:, pl.ds(pl.multiple_of((col0 + k * H + h) * HEAD_DIM, HEAD_DIM), HEAD_DIM)], out_sems.at[k])
            for k, stage in enumerate((stage_q, stage_k, stage_v))]
        for cp in outs:
            cp.start()
        for cp in outs:
            cp.wait()

    def col(k):
        return pl.BlockSpec((S, HEAD_DIM), lambda h: (0, col0 + k * H + h))

    tab = pl.BlockSpec((S, HEAD_DIM), lambda h: (0, 0))
    out = pl.BlockSpec((S, HEAD_DIM), lambda h: (0, h))
    W = H * HEAD_DIM
    big, half = pltpu.VMEM((S, HEAD_DIM), F32), pltpu.VMEM((S, HEAD_DIM), BF16)
    return _pcall(
        body, grid=(H,),
        in_specs=[col(0), col(1), col(2), tab, tab, pl.BlockSpec((1, HEAD_DIM), lambda h: (0, h)), out, out,
                  pl.BlockSpec((S, HEAD_DIM), lambda h: (0, dm_col0 + h)), HBM],
        out_specs=[HBM, pl.BlockSpec((8, HEAD_DIM), lambda h: (0, h))],
        out_shape=[jax.ShapeDtypeStruct(dproj.shape, BF16), jax.ShapeDtypeStruct((8, W), F32)],
        input_output_aliases={9: 0},
        scratch_shapes=[big, big, big, big, pltpu.VMEM((S, LANES), F32), big, big, big, half, half, half,
                        pltpu.SemaphoreType.DMA((3,))],
        compiler_params=_params("arbitrary"), name=name)(proj, proj, proj, cos2, sin_signed, gain, o_raw, lse, dmixed, dproj)


GELU_C = math.sqrt(2.0 / math.pi)
GELU_A = 0.044715
HALO = 16


def _shifts_down(cur, halo):
    row = lax.broadcasted_iota(jnp.int32, cur.shape, 0)
    first, second = row == 0, row == 1
    last, before_last = halo[HALO - 1:HALO, :], halo[HALO - 2:HALO - 1, :]
    two = jnp.where(first, before_last, jnp.where(second, last, pltpu.roll(cur, 2, axis=0)))
    return two, jnp.where(first, last, pltpu.roll(cur, 1, axis=0))


def _shift_up(cur, halo, k):
    n = cur.shape[0]
    out = pltpu.roll(cur, n - k, axis=0)
    row = lax.broadcasted_iota(jnp.int32, cur.shape, 0)
    for t in range(k):
        out = jnp.where(row == n - k + t, halo[t:t + 1, :], out)
    return out


def _conv3(cur, halo, cw):
    rows = (*_shifts_down(cur, halo), cur)
    return rows[0] * cw[0:1, :] + rows[1] * cw[1:2, :] + cur * cw[2:3, :] + cw[3:4, :], rows


def _gelu_parts(x):
    xx = x * x
    t = jnp.tanh(x * (GELU_C + (GELU_C * GELU_A) * xx))
    half = 0.5 * x
    return half + half * t, t, xx, half


def _gelu_slope(t, xx, half):
    return (0.5 + 0.5 * t) + half * (1.0 - t * t) * (GELU_C + (3.0 * GELU_C * GELU_A) * xx)


def _geglu_specs(tm, tn, ncb):
    hb = tm // HALO

    def cur(off):
        return pl.BlockSpec((tm, tn), lambda j, i: (i, off + j))

    def prev(off):
        return pl.BlockSpec((HALO, tn), lambda j, i: (jnp.maximum(i * hb - 1, 0), off + j))

    def taps(off):
        return pl.BlockSpec((8, tn), lambda j, i: (0, off + j))

    return [cur(0), prev(0), cur(ncb), prev(ncb), taps(0), taps(ncb)]


def _geglu_fwd(u, cwb, name, tm=512, tn=1408):
    S, F2 = u.shape
    F = F2 // 2
    tm, tn = _tile(S, tm, HALO), _tile(F, tn)
    ncb = F // tn

    def body(g_ref, gp_ref, v_ref, vp_ref, cg_ref, cv_ref, y_ref):
        top = pl.program_id(1) > 0
        gp = jnp.where(top, gp_ref[...].astype(F32), 0.0)
        vp = jnp.where(top, vp_ref[...].astype(F32), 0.0)
        gc = _conv3(g_ref[...].astype(F32), gp, cg_ref[...])[0]
        vc = _conv3(v_ref[...].astype(F32), vp, cv_ref[...])[0]
        y_ref[...] = (_gelu_parts(gc)[0] * vc).astype(BF16)

    return _pcall(body, grid=(ncb, S // tm), in_specs=_geglu_specs(tm, tn, ncb),
                  out_specs=pl.BlockSpec((tm, tn), lambda j, i: (i, j)),
                  out_shape=jax.ShapeDtypeStruct((S, F), BF16),
                  compiler_params=_params("parallel", "parallel"), name=name)(u, u, u, u, cwb, cwb)


def _geglu_bwd(u, dy, cwb, name, tm=256, tn=1408):
    S, F2 = u.shape
    F = F2 // 2
    tm, tn = _tile(S, tm, HALO), _tile(F, tn)
    ncb = F // tn

    def body(g_ref, gp_ref, v_ref, vp_ref, cg_ref, cv_ref, dy_ref, dc_ref, dwg_ref, dwv_ref):
        i = pl.program_id(1)

        @pl.when(i == 0)
        def _():
            dwg_ref[...] = jnp.zeros_like(dwg_ref)
            dwv_ref[...] = jnp.zeros_like(dwv_ref)

        top = i > 0
        g, v = g_ref[...].astype(F32), v_ref[...].astype(F32)
        gp = jnp.where(top, gp_ref[...].astype(F32), 0.0)
        vp = jnp.where(top, vp_ref[...].astype(F32), 0.0)
        gc, g_rows = _conv3(g, gp, cg_ref[...])
        vc, v_rows = _conv3(v, vp, cv_ref[...])
        act, t, xx, half = _gelu_parts(gc)
        dact = _gelu_slope(t, xx, half)
        dyv = dy_ref[...].astype(F32)
        dgc = dyv * vc * dact
        dvc = dyv * act
        dc_ref[0] = dgc.astype(BF16)
        dc_ref[1] = dvc.astype(BF16)

        def taps(out_ref, dc, rows):
            for k, moved in enumerate(rows):
                out_ref[k:k + 1, :] += jnp.sum(dc * moved, axis=0, keepdims=True)
            out_ref[3:4, :] += jnp.sum(dc, axis=0, keepdims=True)

        taps(dwg_ref, dgc, g_rows)
        taps(dwv_ref, dvc, v_rows)

    return _pcall(body, grid=(ncb, S // tm),
                  in_specs=_geglu_specs(tm, tn, ncb) + [pl.BlockSpec((tm, tn), lambda j, i: (i, j))],
                  out_specs=[pl.BlockSpec((2, tm, tn), lambda j, i: (0, i, j)),
                             pl.BlockSpec((8, tn), lambda j, i: (0, j)), pl.BlockSpec((8, tn), lambda j, i: (0, j))],
                  out_shape=[jax.ShapeDtypeStruct((2, S, F), BF16), jax.ShapeDtypeStruct((8, F), F32),
                             jax.ShapeDtypeStruct((8, F), F32)],
                  compiler_params=_params("parallel", "arbitrary"), name=name)(u, u, u, u, cwb, cwb, dy)


def _conv_bwd(dc, cwb, name, tm=512, tn=1408):
    _, S, F = dc.shape
    tm, tn = _tile(S, tm, HALO), _tile(F, tn)
    ncb, nrb = F // tn, S // tm
    hb = tm // HALO

    def body(c_ref, n_ref, w_ref, du_ref):
        cur = c_ref[...].astype(F32)
        nxt = jnp.where(pl.program_id(2) < nrb - 1, n_ref[...].astype(F32), 0.0)
        w = w_ref[...]
        du = cur * w[2:3, :] + _shift_up(cur, nxt, 1) * w[1:2, :] + _shift_up(cur, nxt, 2) * w[0:1, :]
        du_ref[...] = du.astype(BF16)

    return _pcall(body, grid=(2, ncb, nrb),
                  in_specs=[pl.BlockSpec((None, tm, tn), lambda c, j, i: (c, i, j)),
                            pl.BlockSpec((None, HALO, tn), lambda c, j, i: (c, jnp.minimum((i + 1) * hb, S // HALO - 1), j)),
                            pl.BlockSpec((8, tn), lambda c, j, i: (0, c * ncb + j))],
                  out_specs=pl.BlockSpec((tm, tn), lambda c, j, i: (i, c * ncb + j)),
                  out_shape=jax.ShapeDtypeStruct((S, 2 * F), BF16),
                  compiler_params=_params("parallel", "parallel", "parallel"), name=name)(dc, dc, cwb)


def _adam_math(w, g, m, v):
    m = ADAM_B1 * m + (1.0 - ADAM_B1) * g
    v = ADAM_B2 * v + (1.0 - ADAM_B2) * (g * g)
    m_hat = m / (1.0 - ADAM_B1 ** ADAM_STEP)
    v_hat = v / (1.0 - ADAM_B2 ** ADAM_STEP)
    return -ADAM_LR * (m_hat / (jnp.sqrt(v_hat) + ADAM_EPS) + ADAM_WD * w), m, v


def _adamw(w, parts, m, v, name, tr=256):
    R, C = w.shape
    n, _, Cp = parts.shape
    tr = _tile(R, tr, 8)

    def body(w_ref, p_ref, m_ref, v_ref, g_out, d_out, m_out, v_out):
        g = p_ref[0, :, 0:C].astype(F32)
        for k in range(1, n):
            g = g + p_ref[k, :, 0:C].astype(F32)
        d, mn, vn = _adam_math(w_ref[...], g, m_ref[...], v_ref[...])
        g_out[...] = g
        d_out[...] = d
        m_out[...] = mn
        v_out[...] = vn

    spec = pl.BlockSpec((tr, C), lambda i: (i, 0))
    shape = jax.ShapeDtypeStruct((R, C), F32)
    return _pcall(body, grid=(R // tr,), in_specs=[spec, pl.BlockSpec((n, tr, Cp), lambda i: (0, i, 0)), spec, spec],
                  out_specs=[spec] * 4, out_shape=[shape] * 4, compiler_params=_params("parallel"), name=name)(w, parts, m, v)


def _adamw_chips(w, pair, parts, chip_ids, m, v, name, tr=256):
    R, C = w.shape
    Cp = pair.shape[2]
    by_columns = C == Cp and _tile(R, tr, 16) < 64
    tr, tc = (R, _tile(C, 256)) if by_columns else (_tile(R, tr, 16), C)

    def body(ids_ref, w_ref, own_ref, p1_ref, p2_ref, p3_ref, m_ref, v_ref, g_out, d_out, m_out, v_out):
        g = own_ref[:, 0:tc].astype(F32)
        for ref in (p1_ref, p2_ref, p3_ref):
            g = g + ref[:, 0:tc].astype(F32)
        d, mn, vn = _adam_math(w_ref[...], g, m_ref[...], v_ref[...])
        g_out[...] = g
        d_out[...] = d
        m_out[...] = mn
        v_out[...] = vn

    if by_columns:
        spec = pl.BlockSpec((tr, tc), lambda j, ids: (0, j))
    else:
        spec = pl.BlockSpec((tr, tc), lambda i, ids: (i, 0))

    def chip(k):
        if by_columns:
            return pl.BlockSpec((None, tr, tc), lambda j, ids: (ids[k], 0, j))
        return pl.BlockSpec((None, tr, Cp), lambda i, ids: (ids[k], i, 0))

    shape = jax.ShapeDtypeStruct((R, C), F32)
    grid_spec = pltpu.PrefetchScalarGridSpec(
        num_scalar_prefetch=1, grid=(C // tc if by_columns else R // tr,),
        in_specs=[spec, chip(0), chip(1), chip(2), chip(3), spec, spec], out_specs=[spec] * 4)
    return _pcall(body, grid_spec=grid_spec, out_shape=[shape] * 4, compiler_params=_params("parallel"),
                  name=name)(chip_ids, w, pair, parts, parts, parts, m, v)


def _place():
    return lax.axis_index("x"), lax.axis_index("y"), lax.axis_index("c")


def _other_chips(x, y):
    return [(1 - x, y), (x, 1 - y), (1 - x, 1 - y)]


IN_HBM = pl.BlockSpec(memory_space=pltpu.HBM)
SEM = pl.BlockSpec(memory_space=pltpu.SEMAPHORE)
EFFECT = pltpu.SideEffectType.DATAFLOW_SIDE_EFFECTING
TOKEN = jax.ShapeDtypeStruct((8, LANES), F32)
TOKEN_SPEC = pl.BlockSpec(memory_space=pltpu.VMEM)


def _in_hbm(a):
    return pltpu.with_memory_space_constraint(a, pltpu.HBM)


def _landing(shape):
    return _in_hbm(lax.empty(shape.shape, shape.dtype))


def _hbm_like(a):
    return pltpu.HBM(a.shape, a.dtype)


def _gather_places():
    x, y, c = _place()
    relay_from = (c * (1 - x) + (1 - c) * x, c * y + (1 - c) * (1 - y), c)
    relay_to = (c * x + (1 - c) * (1 - x), c * (1 - y) + (1 - c) * y, c)
    return (x, y, c), (x, y, 1 - c), (1 - x, y, c), (x, 1 - y, c), (1 - x, 1 - y, c), relay_from, relay_to


def _slot_copy(slot, ref, src, dst, send_sem, recv_sem, to):
    return pltpu.make_async_remote_copy(src_ref=slot(ref, *src), dst_ref=slot(ref, *dst), send_sem=send_sem,
                                        recv_sem=recv_sem, device_id=to, device_id_type=MESH)


def _split_call(body, arrays, sems_in, sems_out, after, name, token=True):
    na, ni, no = len(arrays), len(sems_in), len(sems_out)

    def wrapped(*refs):
        body(refs[:na], refs[na:na + ni], refs[na + ni + 1:na + ni + 1 + no])
        if token:
            refs[-1][...] = jnp.zeros_like(refs[-1])

    outs = _pcall(
        wrapped, in_specs=[IN_HBM] * na + [SEM] * ni + [HBM],
        out_specs=[SEM] * no + [IN_HBM] * na + ([TOKEN_SPEC] if token else []),
        out_shape=[pltpu.SemaphoreType.DMA((n,)) for n in sems_out] + [_hbm_like(s) for s in arrays] + ([TOKEN] if token else []),
        input_output_aliases={a: no + a for a in range(na)},
        compiler_params=pltpu.CompilerParams(has_side_effects=EFFECT), name=name,
    )(*[_in_hbm(s) for s in arrays], *sems_in, after)
    return list(outs[:no]), list(outs[no:no + na]), (outs[-1] if token else None)


def _gather_start(landing, slots, after, name):
    na = len(landing)

    def body(land, _, sems):
        me, sib, xn, yn, _, _, _ = _gather_places()
        for a in range(na):
            for k, to in enumerate((sib, xn, yn)):
                _slot_copy(slots[a], land[a], me, me, sems[0].at[3 * a + k], sems[1].at[3 * a + k], to).start()

    return _split_call(body, landing, [], [3 * na, 3 * na], after, name)


def _gather_relay(gathered, sems1, slots, after, name):
    na = len(gathered)

    def body(gath, taken, given):
        me, sib, xn, yn, _, relay_from, relay_to = _gather_places()
        for a in range(na):
            for k, peer in enumerate((sib, xn, yn)):
                arrival = _slot_copy(slots[a], gath[a], me, peer, taken[0].at[3 * a + k], taken[1].at[3 * a + k], peer)
                arrival.wait_send()
                arrival.wait_recv()
        for a in range(na):
            _slot_copy(slots[a], gath[a], relay_from, relay_from, given[0].at[a], given[1].at[a], relay_to).start()
            for k, peer in enumerate((xn, yn)):
                _slot_copy(slots[a], gath[a], peer, peer, given[2].at[2 * a + k], given[3].at[2 * a + k], sib).start()

    return _split_call(body, gathered, sems1, [na, na, 2 * na, 2 * na], after, name)


def _gather_pass(gathered, relay_sems, slots, after, name):
    na = len(gathered)

    def body(gath, taken, given):
        me, sib, xn, yn, diag, relay_from, relay_to = _gather_places()
        for a in range(na):
            _slot_copy(slots[a], gath[a], relay_from, relay_from, taken[0].at[a], taken[1].at[a], relay_to).wait_send()
            _slot_copy(slots[a], gath[a], me, diag, taken[0].at[a], taken[1].at[a], relay_to).wait_recv()
        for a in range(na):
            _slot_copy(slots[a], gath[a], diag, diag, given[0].at[a], given[1].at[a], sib).start()

    return _split_call(body, gathered, relay_sems, [na, na], after, name)


def _gather_finish(gathered, pass_sems, diag_sems, slots, after, name):
    na = len(gathered)

    def body(gath, taken, _):
        (x, y, c), sib, xn, yn, diag, _, _ = _gather_places()
        for a in range(na):
            for k, peer in enumerate((xn, yn)):
                passed = _slot_copy(slots[a], gath[a], peer, (peer[0], peer[1], 1 - c), taken[0].at[2 * a + k],
                                    taken[1].at[2 * a + k], sib)
                passed.wait_send()
                passed.wait_recv()
            passed = _slot_copy(slots[a], gath[a], diag, (diag[0], diag[1], 1 - c), taken[2].at[a], taken[3].at[a], sib)
            passed.wait_send()
            passed.wait_recv()

    return _split_call(body, gathered, list(pass_sems) + list(diag_sems), [], after, name, token=False)[1]


def _pair_copy(view, src, land, send_sems, recv_sems, chip):
    x, y, c = _place()
    return pltpu.make_async_remote_copy(
        src_ref=view(src, chip, 1 - c), dst_ref=land.at[chip], send_sem=send_sems.at[chip], recv_sem=recv_sems.at[chip],
        device_id=(x, y, 1 - c), device_id_type=MESH)


def _pair_start(grad, view, block, after, name):
    def body(src, land, after_ref, send_sems, recv_sems, src_thru, land_thru, token):
        for chip in range(N_CHIP):
            _pair_copy(view, src, land, send_sems, recv_sems, chip).start()
        token[...] = jnp.zeros_like(token)

    sems = pltpu.SemaphoreType.DMA((N_CHIP,))
    land = jax.ShapeDtypeStruct((N_CHIP, *block), BF16)
    return _pcall(
        body, in_specs=[IN_HBM, IN_HBM, HBM], out_specs=[SEM, SEM, IN_HBM, IN_HBM, TOKEN_SPEC],
        out_shape=[sems, sems, _hbm_like(grad), _hbm_like(land), TOKEN], input_output_aliases={0: 2, 1: 3},
        compiler_params=pltpu.CompilerParams(has_side_effects=EFFECT), name=name,
    )(_in_hbm(grad), _landing(land), after)


def _pair_wait(grad, recv, send_sems, recv_sems, view, after, name):
    def body(src, land, send, recv_s, after_ref, src_thru, land_thru):
        for chip in range(N_CHIP):
            copy = _pair_copy(view, src, land, send, recv_s, chip)
            copy.wait_send()
            copy.wait_recv()

    return _pcall(
        body, in_specs=[IN_HBM, IN_HBM, SEM, SEM, HBM], out_specs=[IN_HBM, IN_HBM],
        out_shape=[_hbm_like(grad), _hbm_like(recv)], input_output_aliases={0: 0, 1: 1},
        compiler_params=pltpu.CompilerParams(has_side_effects=EFFECT), name=name,
    )(grad, recv, send_sems, recv_sems, after)


def _chip_start(pair, after, name):
    def body(src, land, after_ref, send_sems, recv_sems, src_thru, land_thru, token):
        x, y, c = _place()
        for j, (px, py) in enumerate(_other_chips(x, y)):
            pltpu.make_async_remote_copy(
                src_ref=src.at[2 * px + py], dst_ref=land.at[2 * x + y], send_sem=send_sems.at[j], recv_sem=recv_sems.at[j],
                device_id=(px, py, c), device_id_type=MESH).start()
        token[...] = jnp.zeros_like(token)

    sems = pltpu.SemaphoreType.DMA((3,))
    return _pcall(
        body, in_specs=[IN_HBM, IN_HBM, HBM], out_specs=[SEM, SEM, IN_HBM, IN_HBM, TOKEN_SPEC],
        out_shape=[sems, sems, _hbm_like(pair), _hbm_like(pair), TOKEN], input_output_aliases={0: 2, 1: 3},
        compiler_params=pltpu.CompilerParams(has_side_effects=EFFECT), name=name,
    )(_in_hbm(pair), _landing(pair), after)


def _chip_wait(pair, parts, send_sems, recv_sems, after, name):
    def body(src, land, send, recv, after_ref, src_thru, land_thru):
        x, y, c = _place()
        for j, (px, py) in enumerate(_other_chips(x, y)):
            copy = pltpu.make_async_remote_copy(
                src_ref=src.at[2 * px + py], dst_ref=land.at[2 * px + py], send_sem=send.at[j], recv_sem=recv.at[j],
                device_id=(px, py, c), device_id_type=MESH)
            copy.wait_send()
            copy.wait_recv()

    return _pcall(
        body, in_specs=[IN_HBM, IN_HBM, SEM, SEM, HBM], out_specs=[IN_HBM, IN_HBM],
        out_shape=[_hbm_like(pair), _hbm_like(parts)], input_output_aliases={0: 0, 1: 1},
        compiler_params=pltpu.CompilerParams(has_side_effects=EFFECT), name=name,
    )(pair, parts, send_sems, recv_sems, after)


def _pair_add(core, grad, recv, block, grad_spec, name):
    _, R, C = recv.shape
    tr = block

    def body(c_ref, g_ref, r_ref, o_ref):
        o_ref[...] = (g_ref[...].astype(F32) + r_ref[...].astype(F32)).astype(BF16)

    grid_spec = pltpu.PrefetchScalarGridSpec(
        num_scalar_prefetch=1, grid=(N_CHIP, R // tr),
        in_specs=[grad_spec, pl.BlockSpec((None, tr, C), lambda k, i, c: (k, i, 0))],
        out_specs=pl.BlockSpec((None, tr, C), lambda k, i, c: (k, i, 0)))
    return _pcall(body, grid_spec=grid_spec, out_shape=jax.ShapeDtypeStruct(recv.shape, BF16),
                  compiler_params=_params("parallel", "parallel"), name=name)(core, grad, recv)


def _small_copies(gath, send_sems, recv_sems):
    x, y, c = _place()
    peers = [(x, y, 1 - c)] + [(px, py, pc) for px, py in _other_chips(x, y) for pc in (c, 1 - c)]
    pairs = []
    for a, ref in enumerate(gath):
        mine = ref.at[4 * x + 2 * y + c]
        for k, (px, py, pc) in enumerate(peers):
            sems = dict(send_sem=send_sems.at[7 * a + k], recv_sem=recv_sems.at[7 * a + k], device_id=(px, py, pc),
                        device_id_type=MESH)
            pairs.append((pltpu.make_async_remote_copy(src_ref=mine, dst_ref=mine, **sems),
                          pltpu.make_async_remote_copy(src_ref=mine, dst_ref=ref.at[4 * px + 2 * py + pc], **sems)))
    return pairs


def _small_start(landing, after, name):
    na = len(landing)

    def body(*refs):
        for send, _ in _small_copies(refs[:na], refs[na + 1], refs[na + 2]):
            send.start()
        refs[-1][...] = jnp.zeros_like(refs[-1])

    sems = pltpu.SemaphoreType.DMA((7 * na,))
    outs = _pcall(
        body, in_specs=[IN_HBM] * na + [HBM], out_specs=[SEM, SEM] + [IN_HBM] * na + [TOKEN_SPEC],
        out_shape=[sems, sems] + [_hbm_like(s) for s in landing] + [TOKEN],
        input_output_aliases={a: 2 + a for a in range(na)},
        compiler_params=pltpu.CompilerParams(has_side_effects=EFFECT), name=name,
    )(*[_in_hbm(s) for s in landing], after)
    return outs[0], outs[1], outs[2:2 + na], outs[-1]


def _small_wait(gathered, send_sems, recv_sems, after, name):
    na = len(gathered)

    def body(*refs):
        for send, arrival in _small_copies(refs[:na], refs[na], refs[na + 1]):
            send.wait_send()
            arrival.wait_recv()

    return list(_pcall(
        body, in_specs=[IN_HBM] * na + [SEM, SEM, HBM], out_specs=[IN_HBM] * na,
        out_shape=[_hbm_like(g) for g in gathered], input_output_aliases={a: a for a in range(na)},
        compiler_params=pltpu.CompilerParams(has_side_effects=EFFECT), name=name,
    )(*gathered, send_sems, recv_sems, after))


def _small_finish(gathered, params, name):
    na, npar = len(gathered), len(params)

    def body(*refs):
        g_refs, wmv = refs[:na], refs[na:na + 3 * npar]
        o_sums, o_params = refs[na + 3 * npar:2 * na + 3 * npar], refs[2 * na + 3 * npar:]
        sums = []
        for a in range(na):
            acc = g_refs[a][0]
            for k in range(1, N_DEV):
                acc = acc + g_refs[a][k]
            o_sums[a][...] = acc
            sums.append(acc)
        for j, (a, row, _, _, _) in enumerate(params):
            g = sums[a][row:row + 1, :]
            d, mn, vn = _adam_math(wmv[3 * j][...], g, wmv[3 * j + 1][...], wmv[3 * j + 2][...])
            for out, val in zip(o_params[4 * j:4 * j + 4], (g, d, mn, vn)):
                out[...] = val

    vm = pl.BlockSpec(memory_space=pltpu.VMEM)
    flat = [t for p in params for t in p[2:]]
    out_shape = [jax.ShapeDtypeStruct(g.shape[1:], F32) for g in gathered]
    out_shape += [jax.ShapeDtypeStruct(p[2].shape, F32) for p in params for _ in range(4)]
    outs = _pcall(body, in_specs=[vm] * (na + 3 * npar), out_specs=[vm] * len(out_shape), out_shape=out_shape,
                  name=name)(*gathered, *flat)
    return outs[:na], [outs[na + 4 * j:na + 4 * j + 4] for j in range(npar)]


def _local_step(x, tgt, gains, weights):
    g_pre_mix, g_post_mix, g_pre_ffn, g_post_ffn, g_sb, g_dil = gains
    S, D = x.shape
    hs = g_sb.shape[1] // HEAD_DIM
    hd = g_dil.shape[1] // HEAD_DIM
    cos2, sin_signed = _rope_tables(S)

    h1 = _rms_fwd(x, g_pre_mix + weights.start(), "rms_in")
    w_in_g = weights.w_in(h1)
    proj = _mm_nn(h1, w_in_g, BF16, "proj", tn=768)
    o_sb, ct_sb, mixed = _sb_fwd(proj, g_sb + weights.relay_out(proj), hs, hs + hd, "sb_fwd")
    o_dl, lse_dl, mixed = _dil_fwd(proj, cos2, sin_signed, g_dil + weights.after_sb(o_sb), mixed, 3 * hs, hd, "dil_fwd")
    w_out_g = weights.w_out(o_dl)
    mix = _mm_nn(mixed, w_out_g, F32, "mix_out", tn=1024)
    x2, h2 = _mid_fwd(x, mix, g_post_mix + weights.after_mix(mix), g_pre_ffn, "mid_fwd")
    w_up_g, cwb = weights.w_up(h2)
    u = _mm_nn(h2, w_up_g, BF16, "ffn_up", b_transposed=True)
    y = _geglu_fwd(u, cwb + weights.forward_down(u), "geglu_fwd")
    w_down_g = weights.w_down(y)
    f = _mm_nn(y, w_down_g, F32, "ffn_down", tn=1024, tk=2816)

    dy, df, dg_post_ffn, loss = _loss_bwd(x2, f, tgt, g_post_ffn, "loss_bwd")
    dyv = _mm_nt(df, w_down_g, BF16, "d_y", tn=1408)
    dw_down = _mm_tn(y, df, D, BF16, "dw_down", tm=1408, tn=1024)
    dc, dcw_g, dcw_v = _geglu_bwd(u, dyv, cwb + weights.grad("w_down", dw_down), "geglu_bwd")
    du = _conv_bwd(dc, cwb + weights.grad_reduce("w_down", dc), "conv_bwd")
    dh2 = _mm_nt(du, w_up_g, BF16, "d_h2", tk=1408, b_transposed=True, per_step=2)
    dw_up = _mm_tn(du, h2, D, BF16, "dw_up", tm=1408, tn=1024)
    dx2, dmix, dg_pre_ffn, dg_post_mix = _mid_bwd(
        dy, dh2, x2, mix, g_pre_ffn + weights.grad("w_up", dw_up), g_post_mix, "mid_bwd")
    dmixed = _mm_nt(dmix, w_out_g, BF16, "d_mixed", after=jnp.reshape(weights.grad_reduce("w_up", dmix), (1, 1)))
    dw_out = _mm_tn(mixed, dmix, D, BF16, "dw_out", tn=1024)
    dproj, dg_sb = _sb_bwd(proj, g_sb + weights.grad("w_out", dw_out), o_sb, ct_sb, dmixed, 0, hs, "sb_bwd")
    dproj, dg_dil = _dil_bwd(proj, cos2, sin_signed, g_dil + weights.grad_reduce("w_out", dg_sb), o_dl, lse_dl,
                             dmixed, dproj, hs, 3 * hs, hd, "dil_bwd")
    dw_in = _mm_tn(h1, dproj, w_in_g.shape[2], BF16, "dw_in", tn=768)
    weights.grad("w_in", dw_in)
    dep = weights.grad_reduce("w_in", dproj)
    dh1 = _mm_nt(dproj, w_in_g, BF16, "d_h1", tk=768, after=jnp.reshape(dep, (1, 1)), per_step=4)
    grad_x, dg_pre_mix = _first_bwd(dx2, dh1, x, g_pre_mix, "first_bwd")
    small = (dg_pre_mix, dg_post_mix, dg_pre_ffn, dg_post_ffn, dg_sb[0:1], dg_dil[0:1], jnp.concatenate([dcw_g, dcw_v], axis=1))
    weights.small(small, loss)
    return loss, grad_x, small


def _pad_cols(a, to):
    return jnp.pad(a, ((0, 0), (0, to - a.shape[1])))


def kernel(x, pre_mix_gain, post_mix_gain, pre_ffn_gain, post_ffn_gain, w_in, sb_out_gain, dil_out_gain, w_out, w_up, conv_w, conv_b, w_down, loss_target, m_pre_mix_gain, m_post_mix_gain, m_pre_ffn_gain, m_post_ffn_gain, m_w_in, m_sb_out_gain, m_dil_out_gain, m_w_out, m_w_up, m_conv_w, m_conv_b, m_w_down, v_pre_mix_gain, v_post_mix_gain, v_pre_ffn_gain, v_post_ffn_gain, v_w_in, v_sb_out_gain, v_dil_out_gain, v_w_out, v_w_up, v_conv_w, v_conv_b, v_w_down):
    xb, tb = x[0], loss_target[0]
    S, D = xb.shape
    w_in, w_out, w_up, w_down, conv_w = w_in[0], w_out[0], w_up[0], w_down[0], conv_w[0]
    n_in, e_rows = w_in.shape[1], w_out.shape[0]
    cu, half = w_up.shape[1], w_down.shape[0]
    assert cu == 2 * half and half % 16 == 0
    cup = -(-cu // LANES) * LANES
    fp = N_CHIP * cup
    px, py, pc = _place()
    me = 4 * px + 2 * py + pc
    core = jnp.reshape(pc, (1,)).astype(jnp.int32)

    w_up_t, m_up_t, v_up_t = (jnp.swapaxes(t, 0, 1) for t in (w_up, m_w_up[0], v_w_up[0]))

    def by_dev(ref, qx, qy, qc):
        return ref.at[4 * qx + 2 * qy + qc]

    def down_slot(ref, qx, qy, qc):
        return ref.at[2 * qx + qy, pl.ds(qc * half, half)]

    def by_pair(ref, chip, k):
        return ref.at[chip, k]

    def down_pair(ref, chip, k):
        return ref.at[chip, pl.ds(k * half, half)]

    def pair_spec(tr, cols):
        return pl.BlockSpec((None, None, tr, cols), lambda k, i, c: (k, c[0], i, 0))

    tr_in, tr_up = _tile(D, 512, 16), _tile(cup, 256, 16)
    grad_plan = {
        "w_in": ((N_CHIP, 2, D, n_in), by_pair, (D, n_in), tr_in, pair_spec(tr_in, n_in)),
        "w_out": ((N_CHIP, 2, e_rows, D), by_pair, (e_rows, D), e_rows, pair_spec(e_rows, D)),
        "w_up": ((N_CHIP, 2, cup, D), by_pair, (cup, D), tr_up, pair_spec(tr_up, D)),
        "w_down": ((N_CHIP, cup, D), down_pair, (half, D), half,
                   pl.BlockSpec((None, half, D), lambda k, i, c: (k, c[0], 0))),
    }

    class Exchanges:
        def __init__(self):
            self.in_flight = {}

        def start(self):
            def own_slot(shard):
                return lax.dynamic_update_index_in_dim(lax.empty((N_DEV, *shard.shape), shard.dtype), shard, me, 0)

            self.group_slots = {"in": [by_dev], "out": [by_dev], "up": [by_dev, by_dev], "down": [down_slot]}
            self.flight = {}
            sems, gath, token = _gather_start([own_slot(w_in.astype(BF16))], [by_dev], core, "gather_in_start")
            self.flight["in"] = (sems, gath)
            zero = token[0, 0]
            self.landing = {
                "out": [own_slot((w_out + zero).astype(BF16))],
                "up": [own_slot(jnp.pad(w_up_t + zero, ((0, cup - cu), (0, 0))).astype(BF16)),
                       own_slot(jnp.pad(conv_w + zero, ((0, 8 - conv_w.shape[0]), (0, cup - cu))))],
                "down": [lax.dynamic_update_slice(jnp.zeros((N_CHIP, cup, D), BF16), (w_down + zero).astype(BF16)[None],
                                                  (2 * px + py, pc * half, 0))]}
            return zero

        def begin(self, group, after):
            sems, gath, token = _gather_start(self.landing[group], self.group_slots[group], after, "gather_%s_start" % group)
            self.flight[group] = (sems, gath)
            return token

        def relay(self, group, after):
            sems, gath = self.flight[group]
            sems, gath, token = _gather_relay(gath, sems, self.group_slots[group], after, "gather_%s_relay" % group)
            self.flight[group] = (sems, gath)
            return token

        def pass_on(self, group, after):
            sems, gath = self.flight[group]
            diag_sems, gath, token = _gather_pass(gath, sems[:2], self.group_slots[group], after, "gather_%s_pass" % group)
            self.flight[group] = (sems[2:], diag_sems, gath)
            return token

        def finish(self, group, after):
            pass_sems, diag_sems, gath = self.flight[group]
            return _gather_finish(gath, pass_sems, diag_sems, self.group_slots[group], after, "gather_%s_finish" % group)

        def w_in(self, after):
            token = self.begin("up", self.begin("out", self.relay("in", after)))
            return self.finish("in", self.pass_on("in", token))[0]

        def relay_out(self, after):
            return self.relay("out", after)[0, 0]

        def after_sb(self, after):
            return self.begin("down", self.relay("up", self.pass_on("out", after)))[0, 0]

        def w_out(self, after):
            return self.finish("out", after)[0].reshape(1, N_DEV * e_rows, D)

        def after_mix(self, after):
            return self.pass_on("up", after)[0, 0]

        def w_up(self, after):
            w_up_g, cw_g = self.finish("up", after)
            cb = _pad_cols(conv_b.reshape(N_DEV, cu), cup).reshape(1, 2 * fp)
            cw_full = jnp.transpose(cw_g[:, :3, :], (1, 0, 2)).reshape(3, 2 * fp)
            cwb = jnp.concatenate([cw_full, cb, jnp.zeros((4, 2 * fp), F32)], axis=0)
            return w_up_g, cwb

        def forward_down(self, after):
            return self.relay("down", after)[0, 0]

        def w_down(self, after):
            return self.finish("down", self.pass_on("down", after))[0].reshape(1, fp, D)

        def small(self, small, loss):
            d_pre_mix, d_post_mix, d_pre_ffn, d_post_ffn, d_sb, d_dil, d_conv = small

            def rows_of(*vectors):
                n = vectors[0].shape[1]
                row = lax.broadcasted_iota(jnp.int32, (8, n), 0)
                out = jnp.zeros((8, n), F32)
                for k, vec in enumerate(vectors):
                    out = jnp.where(row == k, vec, out)
                return out

            parts = [rows_of(d_pre_mix, d_post_mix, d_pre_ffn, d_post_ffn, jnp.broadcast_to(loss[:, :1], (1, D))),
                     rows_of(d_sb, d_dil), d_conv]
            landing = [lax.dynamic_update_index_in_dim(lax.empty((N_DEV, *p.shape), F32), p, me, 0) for p in parts]
            self.small_flight = _small_start(landing, parts[0], "small_start")

        def small_sums(self, after):
            send, recv, gath, _ = self.small_flight
            gath = _small_wait(gath, send, recv, after, "small_wait")
            params = [(0, 0, pre_mix_gain, m_pre_mix_gain, v_pre_mix_gain), (0, 1, post_mix_gain, m_post_mix_gain, v_post_mix_gain),
                      (0, 2, pre_ffn_gain, m_pre_ffn_gain, v_pre_ffn_gain), (0, 3, post_ffn_gain, m_post_ffn_gain, v_post_ffn_gain),
                      (1, 0, sb_out_gain, m_sb_out_gain, v_sb_out_gain), (1, 1, dil_out_gain, m_dil_out_gain, v_dil_out_gain)]
            (gains_sum, _, conv_sum), gain_steps = _small_finish(gath, params, "small_finish")
            return gains_sum[4, 0], conv_sum, gain_steps

        def grad(self, name, dw):
            view_shape, view, block, tr, spec = grad_plan[name]
            send, recv_sems, dw, recv, token = _pair_start(dw.reshape(view_shape), view, block, core, "pair_start_" + name)
            self.in_flight[name] = (dw, recv, send, recv_sems)
            return token[0, 0]

        def grad_reduce(self, name, after):
            _, view, _, tr, spec = grad_plan[name]
            dw, recv = _pair_wait(*self.in_flight[name], view, after, "pair_wait_" + name)
            pair = _pair_add(core, dw, recv, tr, spec, "pair_add_" + name)
            send, recv_sems, pair, parts, token = _chip_start(pair, recv, "chip_start_" + name)
            self.in_flight[name] = (pair, parts, send, recv_sems)
            self.last_token = token
            return token[0, 0]

        def grad_parts(self, name, after):
            return _chip_wait(*self.in_flight[name], after, "chip_wait_" + name)

    exchanges = Exchanges()
    gains = (pre_mix_gain, post_mix_gain, pre_ffn_gain, post_ffn_gain, sb_out_gain, dil_out_gain)
    loss, grad_x, small = _local_step(xb, tb, gains, exchanges)


    chip_ids = jnp.stack([2 * px + py, 2 * (1 - px) + py, 2 * px + 1 - py, 2 * (1 - px) + 1 - py]).astype(jnp.int32)
    out_w_down = _adamw_chips(w_down, *exchanges.grad_parts("w_down", exchanges.small_flight[3]), chip_ids, m_w_down[0], v_w_down[0], "adam_w_down")
    out_up_t = _adamw_chips(w_up_t, *exchanges.grad_parts("w_up", out_w_down[1]), chip_ids, m_up_t, v_up_t, "adam_w_up")
    out_w_up = [jnp.swapaxes(o, 0, 1) for o in out_up_t]
    out_w_out = _adamw_chips(w_out, *exchanges.grad_parts("w_out", out_up_t[1]), chip_ids, m_w_out[0], v_w_out[0], "adam_w_out")
    loss_out, g_conv, gain_steps = exchanges.small_sums(out_w_out[1])
    out_pre_mix, out_post_mix, out_pre_ffn, out_post_ffn, out_sb, out_dil = gain_steps
    g_conv_b = g_conv[3].reshape(N_DEV, cup)[:, :cu].reshape(1, N_DEV * cu)
    g_conv_w = lax.dynamic_index_in_dim(g_conv[0:3].reshape(3, N_DEV, cup), me, axis=1, keepdims=False)[:, :cu]
    out_conv_b = _adamw(conv_b, g_conv_b[None], m_conv_b, v_conv_b, "adam_conv_b")
    out_conv_w = _adamw(conv_w, g_conv_w[None], m_conv_w[0], v_conv_w[0], "adam_conv_w")
    out_w_in = _adamw_chips(w_in, *exchanges.grad_parts("w_in", out_conv_w[1]), chip_ids, m_w_in[0], v_w_in[0], "adam_w_in")

    order = [out_pre_mix, out_post_mix, out_pre_ffn, out_post_ffn, [o[None] for o in out_w_in], out_sb, out_dil,
             [o[None] for o in out_w_out], [o[None] for o in out_w_up], [o[None] for o in out_conv_w], out_conv_b,
             [o[None] for o in out_w_down]]
    outs = [loss_out, grad_x[None]]
    for k in range(4):
        outs += [o[k] for o in order]
    return tuple(outs)
```

```python
import math

import jax
import jax.numpy as jnp
from jax import lax
from jax.experimental import pallas as pl
from jax.experimental.pallas import tpu as pltpu

F32 = jnp.float32
BF16 = jnp.bfloat16
HEAD_DIM = 128
LANES = 128
KEY_BLOCK = 128
DILATIONS = (1, 4, 16)
RMS_EPS = 1e-6
ROPE_THETA = 10000.0
NEG = -1e30
ADAM_LR, ADAM_B1, ADAM_B2, ADAM_EPS, ADAM_WD, ADAM_STEP = 0.001, 0.9, 0.999, 1e-08, 0.01, 10
MESH = pl.DeviceIdType.MESH
N_DEV = 8
N_CHIP = 4
HBM = pl.BlockSpec(memory_space=pl.ANY)
VMEM_LIMIT = 56 * 1024 * 1024

_pcall = pl.pallas_call


def _tile(n, pref, mult=LANES):
    best = None
    t = mult
    while t <= min(n, pref):
        if n % t == 0:
            best = t
        t += mult
    return n if best is None else best


def _params(*sem):
    return pltpu.CompilerParams(dimension_semantics=sem, vmem_limit_bytes=VMEM_LIMIT)


def _dot(a, b, dims):
    return lax.dot_general(a, b, (dims, ((), ())), preferred_element_type=F32)


NN = ((1,), (0,))
NT = ((1,), (1,))
TN = ((0,), (0,))


def _mm_body(dims, nk, tile):
    if nk == 1:
        def single(a_ref, b_ref, o_ref):
            o_ref[...] = _dot(a_ref[...].astype(BF16), b_ref[...].astype(BF16), dims).astype(o_ref.dtype)

        return single, []

    def body(a_ref, b_ref, o_ref, acc_ref):
        k = pl.program_id(2)

        @pl.when(k == 0)
        def _():
            acc_ref[...] = jnp.zeros_like(acc_ref)

        acc_ref[...] += _dot(a_ref[...].astype(BF16), b_ref[...].astype(BF16), dims)

        @pl.when(k == nk - 1)
        def _():
            o_ref[...] = acc_ref[...].astype(o_ref.dtype)

    return body, [pltpu.VMEM(tile, F32)]


def _mm_nn(a, b3, out_dtype, name, tm=1024, tn=1408, tk=2048, b_transposed=False):
    M, K = a.shape
    C, n = b3.shape[0], b3.shape[1 if b_transposed else 2]
    tm, tk, tn = _tile(M, tm, 8), _tile(K, tk), _tile(n, tn)
    npc, nk = n // tn, K // tk
    body, scratch = _mm_body(NT if b_transposed else NN, nk, (tm, tn))
    b_spec = (pl.BlockSpec((None, tn, tk), lambda i, j, k: (j // npc, j % npc, k)) if b_transposed
              else pl.BlockSpec((None, tk, tn), lambda i, j, k: (j // npc, k, j % npc)))
    return _pcall(
        body, grid=(M // tm, C * npc, nk),
        in_specs=[pl.BlockSpec((tm, tk), lambda i, j, k: (i, k)), b_spec],
        out_specs=pl.BlockSpec((tm, tn), lambda i, j, k: (i, j)),
        out_shape=jax.ShapeDtypeStruct((M, C * n), out_dtype), scratch_shapes=scratch,
        compiler_params=_params("parallel", "parallel", "arbitrary"), name=name)(a, b3)


def _mm_nt(a, b3, out_dtype, name, tm=1024, tn=1024, tk=2048, after=None, b_transposed=False, per_step=1):
    M, _ = a.shape
    C, N, n = (b3.shape[0], b3.shape[2], b3.shape[1]) if b_transposed else b3.shape
    tm, tn, tk = _tile(M, tm, 8), _tile(N, tn), _tile(n, tk)
    dims = NN if b_transposed else NT
    extra = [] if after is None else [after]
    if per_step > 1 and tk == n and C % per_step == 0:
        nk, scratch = C // per_step, [pltpu.VMEM((tm, tn), F32)]
        b3 = b3.reshape(nk, per_step, *b3.shape[1:])
        a_spec = pl.BlockSpec((tm, per_step * n), lambda i, j, k: (i, k))
        if b_transposed:
            b_spec = pl.BlockSpec((None, per_step, n, tn), lambda i, j, k: (k, 0, 0, j))
        else:
            b_spec = pl.BlockSpec((None, per_step, tn, n), lambda i, j, k: (k, 0, j, 0))

        def body(a_ref, b_ref, *rest):
            o_ref, acc_ref = rest[len(extra):]
            k = pl.program_id(2)

            @pl.when(k == 0)
            def _():
                acc_ref[...] = jnp.zeros_like(acc_ref)

            b = b_ref[...].astype(BF16)
            b = b.reshape(per_step * n, tn) if b_transposed else jnp.concatenate([b[u] for u in range(per_step)], axis=1)
            acc_ref[...] += _dot(a_ref[...].astype(BF16), b, dims)

            @pl.when(k == nk - 1)
            def _():
                o_ref[...] = acc_ref[...].astype(o_ref.dtype)
    else:
        kpc = n // tk
        nk = C * kpc
        inner, scratch = _mm_body(dims, nk, (tm, tn))
        a_spec = pl.BlockSpec((tm, tk), lambda i, j, k: (i, k))
        b_spec = (pl.BlockSpec((None, tk, tn), lambda i, j, k: (k // kpc, k % kpc, j)) if b_transposed
                  else pl.BlockSpec((None, tn, tk), lambda i, j, k: (k // kpc, j, k % kpc)))

        def body(a_ref, b_ref, *rest):
            inner(a_ref, b_ref, *rest[len(extra):])

    return _pcall(
        body, grid=(M // tm, N // tn, nk), in_specs=[a_spec, b_spec] + [HBM] * len(extra),
        out_specs=pl.BlockSpec((tm, tn), lambda i, j, k: (i, j)),
        out_shape=jax.ShapeDtypeStruct((M, N), out_dtype), scratch_shapes=scratch,
        compiler_params=_params("parallel", "parallel", "arbitrary"), name=name)(a, b3, *extra)


def _mm_tn(x, y, n, out_dtype, name, tm=1024, tn=1408, tk=2048, after=None):
    S, P = x.shape
    C = y.shape[1] // n
    tm, tn, tk = _tile(P, tm), _tile(n, tn), _tile(S, tk, 8)
    npc, nk = n // tn, S // tk
    inner, scratch = _mm_body(TN, nk, (tm, tn))
    extra = [] if after is None else [after]

    def body(x_ref, y_ref, *rest):
        inner(x_ref, y_ref, *rest[len(extra):])

    return _pcall(
        body, grid=(P // tm, C * npc, nk),
        in_specs=[pl.BlockSpec((tk, tm), lambda i, j, k: (k, i)),
                  pl.BlockSpec((tk, tn), lambda i, j, k: (k, j))] + [HBM] * len(extra),
        out_specs=pl.BlockSpec((None, tm, tn), lambda i, j, k: (j // npc, i, j % npc)),
        out_shape=jax.ShapeDtypeStruct((C, P, n), out_dtype), scratch_shapes=scratch,
        compiler_params=_params("parallel", "parallel", "arbitrary"), name=name)(x, y, *extra)


def _rms_scale(v):
    return lax.rsqrt(jnp.mean(v * v, axis=-1, keepdims=True) + RMS_EPS)


def _rms_bwd(gy, v, r):
    return r * gy - v * (r * r * r * jnp.mean(gy * v, axis=-1, keepdims=True))


def _rows_spec(tm, d):
    return pl.BlockSpec((tm, d), lambda i: (i, 0))


def _vec_spec(d):
    return pl.BlockSpec((1, d), lambda i: (0, 0))


def _rms_fwd(x, g, name, tm=256):
    S, D = x.shape

    def body(x_ref, g_ref, h_ref):
        v = x_ref[...]
        h_ref[...] = (v * _rms_scale(v) * g_ref[...]).astype(BF16)

    return _pcall(body, grid=(S // tm,), in_specs=[_rows_spec(tm, D), _vec_spec(D)], out_specs=_rows_spec(tm, D),
                  out_shape=jax.ShapeDtypeStruct((S, D), BF16), compiler_params=_params("parallel"), name=name)(x, g)


def _mid_fwd(x, mix, g_post, g_pre, name, tm=256):
    S, D = x.shape

    def body(x_ref, m_ref, gp_ref, gn_ref, x2_ref, h_ref):
        m = m_ref[...]
        x2 = x_ref[...] + m * _rms_scale(m) * gp_ref[...]
        x2_ref[...] = x2
        h_ref[...] = (x2 * _rms_scale(x2) * gn_ref[...]).astype(BF16)

    return _pcall(body, grid=(S // tm,), in_specs=[_rows_spec(tm, D), _rows_spec(tm, D), _vec_spec(D), _vec_spec(D)],
                  out_specs=[_rows_spec(tm, D), _rows_spec(tm, D)],
                  out_shape=[jax.ShapeDtypeStruct((S, D), F32), jax.ShapeDtypeStruct((S, D), BF16)],
                  compiler_params=_params("parallel"), name=name)(x, mix, g_post, g_pre)


def _loss_bwd(x2, f, tgt, g_post, name, tm=256):
    S, D = x2.shape

    def body(x2_ref, f_ref, t_ref, g_ref, dy_ref, df_ref, dg_ref, ls_ref):
        i = pl.program_id(0)

        @pl.when(i == 0)
        def _():
            dg_ref[...] = jnp.zeros_like(dg_ref)
            ls_ref[...] = jnp.zeros_like(ls_ref)

        fv = f_ref[...]
        r = _rms_scale(fv)
        g = g_ref[...]
        err = x2_ref[...] + fv * r * g - t_ref[...]
        ls_ref[...] += jnp.broadcast_to(0.5 * jnp.sum(jnp.mean(err * err, axis=-1, keepdims=True), axis=0, keepdims=True), ls_ref.shape)
        dy = err * (1.0 / D)
        dy_ref[...] = dy
        df_ref[...] = _rms_bwd(dy * g, fv, r).astype(BF16)
        dg_ref[...] += jnp.sum(dy * fv * r, axis=0, keepdims=True)

    return _pcall(body, grid=(S // tm,),
                  in_specs=[_rows_spec(tm, D), _rows_spec(tm, D), _rows_spec(tm, D), _vec_spec(D)],
                  out_specs=[_rows_spec(tm, D), _rows_spec(tm, D), _vec_spec(D), _vec_spec(LANES)],
                  out_shape=[jax.ShapeDtypeStruct((S, D), F32), jax.ShapeDtypeStruct((S, D), BF16),
                             jax.ShapeDtypeStruct((1, D), F32), jax.ShapeDtypeStruct((1, LANES), F32)],
                  compiler_params=_params("arbitrary"), name=name)(x2, f, tgt, g_post)


def _mid_bwd(dy, dh2, x2, mix, g_pre, g_post, name, tm=256):
    S, D = dy.shape

    def body(dy_ref, dh_ref, x2_ref, m_ref, gn_ref, gp_ref, dx2_ref, dm_ref, dgn_ref, dgp_ref):
        i = pl.program_id(0)

        @pl.when(i == 0)
        def _():
            dgn_ref[...] = jnp.zeros_like(dgn_ref)
            dgp_ref[...] = jnp.zeros_like(dgp_ref)

        x2, dh = x2_ref[...], dh_ref[...].astype(F32)
        r = _rms_scale(x2)
        dx2 = dy_ref[...] + _rms_bwd(dh * gn_ref[...], x2, r)
        dgn_ref[...] += jnp.sum(dh * x2 * r, axis=0, keepdims=True)
        dx2_ref[...] = dx2
        m = m_ref[...]
        rm = _rms_scale(m)
        dm_ref[...] = _rms_bwd(dx2 * gp_ref[...], m, rm).astype(BF16)
        dgp_ref[...] += jnp.sum(dx2 * m * rm, axis=0, keepdims=True)

    return _pcall(body, grid=(S // tm,),
                  in_specs=[_rows_spec(tm, D)] * 4 + [_vec_spec(D)] * 2,
                  out_specs=[_rows_spec(tm, D), _rows_spec(tm, D), _vec_spec(D), _vec_spec(D)],
                  out_shape=[jax.ShapeDtypeStruct((S, D), F32), jax.ShapeDtypeStruct((S, D), BF16),
                             jax.ShapeDtypeStruct((1, D), F32), jax.ShapeDtypeStruct((1, D), F32)],
                  compiler_params=_params("arbitrary"), name=name)(dy, dh2, x2, mix, g_pre, g_post)


def _first_bwd(dx2, dh1, x, g_pre, name, tm=256):
    S, D = x.shape

    def body(dx2_ref, dh_ref, x_ref, g_ref, gx_ref, dg_ref):
        i = pl.program_id(0)

        @pl.when(i == 0)
        def _():
            dg_ref[...] = jnp.zeros_like(dg_ref)

        xv, dh = x_ref[...], dh_ref[...].astype(F32)
        r = _rms_scale(xv)
        gx_ref[...] = dx2_ref[...] + _rms_bwd(dh * g_ref[...], xv, r)
        dg_ref[...] += jnp.sum(dh * xv * r, axis=0, keepdims=True)

    return _pcall(body, grid=(S // tm,), in_specs=[_rows_spec(tm, D)] * 3 + [_vec_spec(D)],
                  out_specs=[_rows_spec(tm, D), _vec_spec(D)],
                  out_shape=[jax.ShapeDtypeStruct((S, D), F32), jax.ShapeDtypeStruct((1, D), F32)],
                  compiler_params=_params("arbitrary"), name=name)(dx2, dh1, x, g_pre)


def _logsig_pair(z):
    lb = jnp.minimum(z, 0.0) - jnp.log(1.0 + jnp.exp(-jnp.abs(z)))
    return lb, lb - z


SB_KEY_BLOCK = 256


def _sum_matrix(strict):
    ia = lax.broadcasted_iota(jnp.int32, (SB_KEY_BLOCK, SB_KEY_BLOCK), 0)
    ib = lax.broadcasted_iota(jnp.int32, (SB_KEY_BLOCK, SB_KEY_BLOCK), 1)
    return ((ia > ib) if strict == ">" else (ia < ib)).astype(BF16)


def _row_total(sums, v, col):
    return jnp.broadcast_to(sums[:, col:col + 1] + v[:, col:col + 1], (v.shape[0], LANES))


def _lanes(c, width):
    return jnp.tile(c, (1, width // LANES))


def _split_dot(v, u):
    hi = v.astype(BF16)
    lo = (v - hi.astype(F32)).astype(BF16)
    return _dot(hi, u, NN) + _dot(lo, u, NN)


def _head_out(o, g):
    return o * _rms_scale(o) * g


def _sb_fwd(proj, gain, n_heads, mixed_heads, name, tq=1024):
    S = proj.shape[0]
    H, tk = n_heads, SB_KEY_BLOCK
    tq = _tile(S, tq, 2 * tk)
    scale = HEAD_DIM ** -0.5

    def body(q_ref, k_ref, v_ref, g_ref, o_ref, ct_ref, mx_ref, oacc, cacc):
        i = pl.program_id(1)
        oacc[...] = jnp.zeros_like(oacc)
        cacc[...] = jnp.zeros_like(cacc)
        sums = _sum_matrix(">")

        def run(blocks):
            scored = []
            for k0, r0, diagonal in blocks:
                rows = pl.ds(r0, tq - r0)
                lb, lk = _logsig_pair(_dot(q_ref[rows, :].astype(BF16), k_ref[pl.ds(k0, tk), :].astype(BF16), NT) * scale)
                causal = None
                if diagonal:
                    causal = (lax.broadcasted_iota(jnp.int32, (tq - r0, tk), 1)
                              < lax.broadcasted_iota(jnp.int32, (tq - r0, tk), 0))
                    lk = jnp.where(causal, lk, 0.0)
                scored.append((k0, rows, causal, lb, lk))
            summed = [(k0, rows, causal, lb, lk, _split_dot(lk, sums)) for k0, rows, causal, lb, lk in scored]
            weights = []
            for k0, rows, causal, lb, lk, after in summed:
                c = cacc[rows, :]
                a = jnp.exp(lb + after + _lanes(c, tk))
                if causal is not None:
                    a = jnp.where(causal, a, 0.0)
                cacc[rows, :] = c + _row_total(after, lk, 0)
                weights.append((k0, rows, a.astype(BF16)))
            for k0, rows, a in weights:
                oacc[rows, :] += _dot(a, v_ref[pl.ds(k0, tk), :].astype(BF16), NN)

        for d in reversed(range(0, tq // tk, 2)):
            run([(pl.multiple_of(i * tq + e * tk, tk), e * tk, True) for e in (d + 1, d)])
        per_trip = tq // tk

        def step(it, carry):
            k0 = pl.multiple_of((i - 1 - it) * tq, tq)
            run([(pl.multiple_of(k0 + e * tk, tk), 0, False) for e in reversed(range(per_trip))])
            return carry

        lax.fori_loop(0, i, step, 0)
        o = oacc[...]
        o_ref[...] = o
        ct_ref[...] = cacc[...]
        mx_ref[...] = _head_out(o, g_ref[...]).astype(BF16)

    blk = pl.BlockSpec((tq, HEAD_DIM), lambda h, i: (i, h))
    return _pcall(
        body, grid=(H, S // tq),
        in_specs=[blk, pl.BlockSpec((S, HEAD_DIM), lambda h, i: (0, H + h)),
                  pl.BlockSpec((S, HEAD_DIM), lambda h, i: (0, 2 * H + h)), pl.BlockSpec((1, HEAD_DIM), lambda h, i: (0, h))],
        out_specs=[blk, blk, blk],
        out_shape=[jax.ShapeDtypeStruct((S, H * HEAD_DIM), F32), jax.ShapeDtypeStruct((S, H * HEAD_DIM), F32),
                   jax.ShapeDtypeStruct((S, mixed_heads * HEAD_DIM), BF16)],
        scratch_shapes=[pltpu.VMEM((tq, HEAD_DIM), F32), pltpu.VMEM((tq, LANES), F32)],
        compiler_params=_params("parallel", "arbitrary"), name=name)(proj, proj, proj, gain)


def _sb_bwd(proj, gain, o_raw, ctot, dmixed, dm_col0, n_heads, name, tq=1024):
    S = proj.shape[0]
    H, tk = n_heads, SB_KEY_BLOCK
    tq = _tile(S, tq, 2 * tk)
    nq = S // tq
    scale = HEAD_DIM ** -0.5

    def body(q_ref, k_ref, v_ref, g_ref, o_ref, ct_ref, dm_ref, dproj_ref, dg_ref,
             dkacc, dvacc, dqacc, pfx, gcar, dos, stage_q, stage_k, stage_v, out_sems):
        h, i = pl.program_id(0), pl.program_id(1)

        @pl.when(i == 0)
        def _():
            dkacc[...] = jnp.zeros_like(dkacc)
            dvacc[...] = jnp.zeros_like(dvacc)
            dg_ref[...] = jnp.zeros_like(dg_ref)

        o, dm, g = o_ref[...], dm_ref[...].astype(F32), g_ref[...]
        r = _rms_scale(o)
        dos[...] = _rms_bwd(dm * g, o, r).astype(BF16)
        dg_ref[...] += jnp.broadcast_to(jnp.sum(dm * o * r, axis=0, keepdims=True), dg_ref.shape)
        dqacc[...] = jnp.zeros_like(dqacc)
        pfx[...] = jnp.zeros_like(pfx)
        gcar[...] = jnp.zeros_like(gcar)
        later, earlier = _sum_matrix(">"), _sum_matrix("<")

        def run(blocks):
            scored = []
            for k0, r0, diagonal in blocks:
                rows, keys = pl.ds(r0, tq - r0), pl.ds(k0, tk)
                lb, lk = _logsig_pair(_dot(q_ref[rows, :].astype(BF16), k_ref[keys, :].astype(BF16), NT) * scale)
                da = _dot(dos[rows, :], v_ref[keys, :].astype(BF16), NT)
                causal = None
                if diagonal:
                    causal = (lax.broadcasted_iota(jnp.int32, (tq - r0, tk), 1)
                              < lax.broadcasted_iota(jnp.int32, (tq - r0, tk), 0))
                    lk = jnp.where(causal, lk, 0.0)
                scored.append((rows, keys, causal, lb, lk, da))
            summed = [(*blk, _split_dot(blk[4], later)) for blk in scored]
            weighted = []
            for rows, keys, causal, lb, lk, da, after in summed:
                p = pfx[rows, :] + _row_total(after, lk, 0)
                pfx[rows, :] = p
                a = jnp.exp(lb + after + _lanes(ct_ref[rows, :] - p, tk))
                if causal is not None:
                    a = jnp.where(causal, a, 0.0)
                dl = da * a
                weighted.append((rows, keys, causal, lb, a.astype(BF16), dl, _dot(dl.astype(BF16), earlier, NN)))
            cotangents = []
            for rows, keys, causal, lb, a, dl, before in weighted:
                gc = gcar[rows, :]
                gcar[rows, :] = gc + _row_total(before, dl, tk - 1)
                sig = jnp.exp(lb)
                gsum = (before + _lanes(gc, tk)) * sig
                if causal is not None:
                    gsum = jnp.where(causal, gsum, 0.0)
                cotangents.append((rows, keys, a, ((dl * (1.0 - sig) - gsum) * scale).astype(BF16)))
            for rows, keys, a, dz in cotangents:
                q, do = q_ref[rows, :].astype(BF16), dos[rows, :]
                dvacc[keys, :] += _dot(a, do, TN)
                dqacc[rows, :] += _dot(dz, k_ref[keys, :].astype(BF16), NN)
                dkacc[keys, :] += _dot(dz, q, TN)

        per_trip = tq // tk

        def step(j, carry):
            k0 = pl.multiple_of(j * tq, tq)
            run([(pl.multiple_of(k0 + e * tk, tk), 0, False) for e in range(per_trip)])
            return carry

        lax.fori_loop(0, i, step, 0)
        for d in range(0, tq // tk, 2):
            run([(pl.multiple_of(i * tq + e * tk, tk), e * tk, True) for e in (d, d + 1)])
        def columns(block):
            return pl.ds(pl.multiple_of(block * HEAD_DIM, HEAD_DIM), HEAD_DIM)

        dq_out = pltpu.make_async_copy(stage_q, dproj_ref.at[pl.ds(pl.multiple_of(i * tq, tq), tq), columns(h)], out_sems.at[0])
        dkv_out = [pltpu.make_async_copy(stage_k, dproj_ref.at[:, columns(H + h)], out_sems.at[1]),
                   pltpu.make_async_copy(stage_v, dproj_ref.at[:, columns(2 * H + h)], out_sems.at[2])]

        @pl.when((h > 0) | (i > 0))
        def _():
            dq_out.wait()

        stage_q[...] = dqacc[...].astype(BF16)
        dq_out.start()

        @pl.when(i == nq - 1)
        def _():
            @pl.when(h > 0)
            def _():
                for cp in dkv_out:
                    cp.wait()

            stage_k[...] = dkacc[...].astype(BF16)
            stage_v[...] = dvacc[...].astype(BF16)
            for cp in dkv_out:
                cp.start()

        @pl.when((h == H - 1) & (i == nq - 1))
        def _():
            dq_out.wait()
            for cp in dkv_out:
                cp.wait()

    blk = pl.BlockSpec((tq, HEAD_DIM), lambda h, i: (i, h))
    W = H * HEAD_DIM
    return _pcall(
        body, grid=(H, nq),
        in_specs=[blk, pl.BlockSpec((S, HEAD_DIM), lambda h, i: (0, H + h)),
                  pl.BlockSpec((S, HEAD_DIM), lambda h, i: (0, 2 * H + h)), pl.BlockSpec((1, HEAD_DIM), lambda h, i: (0, h)),
                  blk, blk, pl.BlockSpec((tq, HEAD_DIM), lambda h, i: (i, dm_col0 + h))],
        out_specs=[HBM, pl.BlockSpec((8, HEAD_DIM), lambda h, i: (0, h))],
        out_shape=[jax.ShapeDtypeStruct(proj.shape, BF16), jax.ShapeDtypeStruct((8, W), F32)],
        scratch_shapes=[pltpu.VMEM((S, HEAD_DIM), F32), pltpu.VMEM((S, HEAD_DIM), F32), pltpu.VMEM((tq, HEAD_DIM), F32),
                        pltpu.VMEM((tq, LANES), F32), pltpu.VMEM((tq, LANES), F32), pltpu.VMEM((tq, HEAD_DIM), BF16),
                        pltpu.VMEM((tq, HEAD_DIM), BF16), pltpu.VMEM((S, HEAD_DIM), BF16), pltpu.VMEM((S, HEAD_DIM), BF16),
                        pltpu.SemaphoreType.DMA((3,))],
        compiler_params=_params("arbitrary", "arbitrary"), name=name)(proj, proj, proj, gain, o_raw, ctot, dmixed)


def _rope_tables(S):
    inv_freq = ROPE_THETA ** (-jnp.arange(0, HEAD_DIM, 2, dtype=F32) / HEAD_DIM)
    ang = jnp.arange(S, dtype=F32)[:, None] * inv_freq[None, :]
    cos, sin = jnp.cos(ang), jnp.sin(ang)
    return jnp.concatenate([cos, cos], axis=1), jnp.concatenate([-sin, sin], axis=1)


def _rope(v, cos2, sin_signed):
    return v * cos2 + pltpu.roll(v, HEAD_DIM // 2, axis=1) * sin_signed


def _dil_rows(d, r, l0, n):
    if d == 1:
        return pl.ds(l0 if isinstance(l0, int) else pl.multiple_of(l0, KEY_BLOCK), n)
    return pl.ds(r + d * l0, n, stride=d)


def _dil_blocks(S, visit):
    B = KEY_BLOCK
    group = 16
    for b, d in enumerate(DILATIONS):
        nb = S // d // B
        if nb == 1:
            g = math.gcd(d, group)

            def trip(t, carry, b=b, d=d, g=g):
                visit([(b, d, t * g + u, 0, True) for u in range(g)])
                return carry

            lax.fori_loop(0, d // g, trip, 0)
        elif d == 1:
            visit([(b, d, 0, 0, True)])
            g = max(k for k in range(1, group + 2) if (nb - 1) % k == 0)

            def trip(t, carry, b=b, d=d, g=g):
                visit([(b, d, 0, (1 + t * g + u) * B, False) for u in range(g)])
                return carry

            lax.fori_loop(0, (nb - 1) // g, trip, 0)
        else:
            g = math.gcd(d, max(group // nb, 1))

            def trip(t, carry, b=b, d=d, nb=nb, g=g):
                visit([(b, d, t * g + u, n * B, n == 0) for u in range(g) for n in range(nb)])
                return carry

            lax.fori_loop(0, d // g, trip, 0)


def _dil_mask(first):
    B = KEY_BLOCK
    nk = B if first else 2 * B
    iq = lax.broadcasted_iota(jnp.int32, (B, nk), 0)
    ik = lax.broadcasted_iota(jnp.int32, (B, nk), 1)
    return (ik <= iq) if first else ((ik >= iq) & (ik <= iq + B))


def _dil_fwd(proj, cos2, sin_signed, gain, mixed, col0, n_heads, name):
    S = proj.shape[0]
    H, B = n_heads, KEY_BLOCK
    scale = HEAD_DIM ** -0.5
    rc = _tile(S, 256, 8)

    def body(q_ref, k_ref, v_ref, c_ref, s_ref, g_ref, mixed_in, o_ref, l_ref, mx_ref, qr, kr, vf, *per_branch):
        ob, lb = per_branch[:len(DILATIONS)], per_branch[len(DILATIONS):]

        def rope_rows(t, carry):
            rows = pl.ds(pl.multiple_of(t * rc, rc), rc)
            qr[rows, :] = _rope(q_ref[rows, :].astype(F32), c_ref[rows, :], s_ref[rows, :])
            kr[rows, :] = _rope(k_ref[rows, :].astype(F32), c_ref[rows, :], s_ref[rows, :])
            vf[rows, :] = v_ref[rows, :].astype(F32)
            return carry

        lax.fori_loop(0, S // rc, rope_rows, 0)

        def visit(blocks):
            scores = []
            for b, d, r, l0, first in blocks:
                qrows = _dil_rows(d, r, l0, B)
                krows = qrows if first else _dil_rows(d, r, l0 - B, 2 * B)
                s = _dot(qr[qrows, :].astype(BF16), kr[krows, :].astype(BF16), NT) * scale
                scores.append((b, qrows, krows, jnp.where(_dil_mask(first), s, NEG)))
            weights = []
            for b, qrows, krows, s in scores:
                m = jnp.max(s, axis=1, keepdims=True)
                p = jnp.exp(s - m)
                den = jnp.sum(p, axis=1, keepdims=True)
                lb[b][qrows, :] = jnp.broadcast_to(m + jnp.log(den), (B, LANES))
                weights.append((b, qrows, krows, p.astype(BF16), den))
            for b, qrows, krows, p, den in weights:
                ob[b][qrows, :] = _dot(p, vf[krows, :].astype(BF16), NN) / den

        _dil_blocks(S, visit)

        def combine(t, carry):
            rows = pl.ds(pl.multiple_of(t * rc, rc), rc)
            l0, l1, l2 = lb[0][rows, :], lb[1][rows, :], lb[2][rows, :]
            m = jnp.maximum(jnp.maximum(l0, l1), l2)
            w0, w1, w2 = jnp.exp(l0 - m), jnp.exp(l1 - m), jnp.exp(l2 - m)
            den = w0 + w1 + w2
            o = (w0 * ob[0][rows, :] + w1 * ob[1][rows, :] + w2 * ob[2][rows, :]) / den
            o_ref[rows, :] = o
            l_ref[rows, :] = m + jnp.log(den)
            mx_ref[rows, :] = _head_out(o, g_ref[...]).astype(BF16)
            return carry

        lax.fori_loop(0, S // rc, combine, 0)

    def col(k):
        return pl.BlockSpec((S, HEAD_DIM), lambda h: (0, col0 + k * H + h))

    tab = pl.BlockSpec((S, HEAD_DIM), lambda h: (0, 0))
    out = pl.BlockSpec((S, HEAD_DIM), lambda h: (0, h))
    W = H * HEAD_DIM
    first = mixed.shape[1] // HEAD_DIM - H
    return _pcall(
        body, grid=(H,),
        in_specs=[col(0), col(1), col(2), tab, tab, pl.BlockSpec((1, HEAD_DIM), lambda h: (0, h)), HBM],
        out_specs=[out, out, pl.BlockSpec((S, HEAD_DIM), lambda h: (0, first + h))],
        out_shape=[jax.ShapeDtypeStruct((S, W), F32), jax.ShapeDtypeStruct((S, W), F32),
                   jax.ShapeDtypeStruct(mixed.shape, BF16)],
        input_output_aliases={6: 2},
        scratch_shapes=[pltpu.VMEM((S, HEAD_DIM), F32)] * (3 + 2 * len(DILATIONS)),
        compiler_params=_params("parallel"), name=name)(proj, proj, proj, cos2, sin_signed, gain, mixed)


def _dil_bwd(proj, cos2, sin_signed, gain, o_raw, lse, dmixed, dproj, dm_col0, col0, n_heads, name):
    S = proj.shape[0]
    H, B = n_heads, KEY_BLOCK
    scale = HEAD_DIM ** -0.5
    rc = _tile(S, 256, 8)

    def body(q_ref, k_ref, v_ref, c_ref, s_ref, g_ref, o_ref, l_ref, dm_ref, dproj_in, dproj_ref, dg_ref,
             qr, kr, vf, dos, dsum, dqr, dkr, dvv, stage_q, stage_k, stage_v, out_sems):
        dg_ref[...] = jnp.zeros_like(dg_ref)

        def prep(t, carry):
            rows = pl.ds(pl.multiple_of(t * rc, rc), rc)
            qr[rows, :] = _rope(q_ref[rows, :].astype(F32), c_ref[rows, :], s_ref[rows, :])
            kr[rows, :] = _rope(k_ref[rows, :].astype(F32), c_ref[rows, :], s_ref[rows, :])
            vf[rows, :] = v_ref[rows, :].astype(F32)
            o, dm = o_ref[rows, :], dm_ref[rows, :].astype(F32)
            r = _rms_scale(o)
            do = _rms_bwd(dm * g_ref[...], o, r)
            dg_ref[...] += jnp.broadcast_to(jnp.sum(dm * o * r, axis=0, keepdims=True), dg_ref.shape)
            dos[rows, :] = do
            dsum[rows, :] = jnp.broadcast_to(jnp.sum(do * o, axis=1, keepdims=True), (rc, LANES))
            dqr[rows, :] = jnp.zeros((rc, HEAD_DIM), F32)
            dkr[rows, :] = jnp.zeros((rc, HEAD_DIM), F32)
            dvv[rows, :] = jnp.zeros((rc, HEAD_DIM), F32)
            return carry

        lax.fori_loop(0, S // rc, prep, 0)

        def visit(blocks):
            products = []
            for b, d, r, l0, first in blocks:
                qrows = _dil_rows(d, r, l0, B)
                krows = qrows if first else _dil_rows(d, r, l0 - B, 2 * B)
                qs, ks = qr[qrows, :].astype(BF16), kr[krows, :].astype(BF16)
                do = dos[qrows, :].astype(BF16)
                s = jnp.where(_dil_mask(first), _dot(qs, ks, NT) * scale, NEG)
                dp = _dot(do, vf[krows, :].astype(BF16), NT)
                products.append((qrows, krows, qs, ks, do, s, dp))
            cotangents = []
            for qrows, krows, qs, ks, do, s, dp in products:
                p = jnp.exp(s - l_ref[qrows, :][:, 0:1])
                ds = (p * (dp - dsum[qrows, :][:, 0:1]) * scale).astype(BF16)
                cotangents.append((qrows, krows, qs, ks, do, p.astype(BF16), ds))
            for qrows, krows, qs, ks, do, p, ds in cotangents:
                dqr[qrows, :] += _dot(ds, ks, NN)
                dkr[krows, :] += _dot(ds, qs, TN)
                dvv[krows, :] += _dot(p, do, TN)

        _dil_blocks(S, visit)

        def finish(t, carry):
            rows = pl.ds(pl.multiple_of(t * rc, rc), rc)
            c, s = c_ref[rows, :], s_ref[rows, :]
            dq, dk = dqr[rows, :], dkr[rows, :]
            stage_q[rows, :] = (dq * c + pltpu.roll(dq * s, HEAD_DIM // 2, axis=1)).astype(BF16)
            stage_k[rows, :] = (dk * c + pltpu.roll(dk * s, HEAD_DIM // 2, axis=1)).astype(BF16)
            stage_v[rows, :] = dvv[rows, :].astype(BF16)
            return carry

        h = pl.program_id(0)
        outs = [pltpu.make_async_copy(
            stage, dproj_ref.at[:, pl.ds(pl.multiple_of((col0 + k * H + h) * HEAD_DIM, HEAD_DIM), HEAD_DIM)], out_sems.at[k])
            for k, stage in enumerate((stage_q, stage_k, stage_v))]

        @pl.when(h > 0)
        def _():
            for cp in outs:
                cp.wait()

        lax.fori_loop(0, S // rc, finish, 0)
        for cp in outs:
            cp.start()

        @pl.when(h == H - 1)
        def _():
            for cp in outs:
                cp.wait()

    def col(k):
        return pl.BlockSpec((S, HEAD_DIM), lambda h: (0, col0 + k * H + h))

    tab = pl.BlockSpec((S, HEAD_DIM), lambda h: (0, 0))
    out = pl.BlockSpec((S, HEAD_DIM), lambda h: (0, h))
    W = H * HEAD_DIM
    big, half = pltpu.VMEM((S, HEAD_DIM), F32), pltpu.VMEM((S, HEAD_DIM), BF16)
    return _pcall(
        body, grid=(H,),
        in_specs=[col(0), col(1), col(2), tab, tab, pl.BlockSpec((1, HEAD_DIM), lambda h: (0, h)), out, out,
                  pl.BlockSpec((S, HEAD_DIM), lambda h: (0, dm_col0 + h)), HBM],
        out_specs=[HBM, pl.BlockSpec((8, HEAD_DIM), lambda h: (0, h))],
        out_shape=[jax.ShapeDtypeStruct(dproj.shape, BF16), jax.ShapeDtypeStruct((8, W), F32)],
        input_output_aliases={9: 0},
        scratch_shapes=[big, big, big, big, pltpu.VMEM((S, LANES), F32), big, big, big, half, half, half,
                        pltpu.SemaphoreType.DMA((3,))],
        compiler_params=_params("arbitrary"), name=name)(proj, proj, proj, cos2, sin_signed, gain, o_raw, lse, dmixed, dproj)


GELU_C = math.sqrt(2.0 / math.pi)
GELU_A = 0.044715
HALO = 16


def _shifts_down(cur, halo):
    row = lax.broadcasted_iota(jnp.int32, cur.shape, 0)
    first, second = row == 0, row == 1
    last, before_last = halo[HALO - 1:HALO, :], halo[HALO - 2:HALO - 1, :]
    two = jnp.where(first, before_last, jnp.where(second, last, pltpu.roll(cur, 2, axis=0)))
    return two, jnp.where(first, last, pltpu.roll(cur, 1, axis=0))


def _shift_up(cur, halo, k):
    n = cur.shape[0]
    out = pltpu.roll(cur, n - k, axis=0)
    row = lax.broadcasted_iota(jnp.int32, cur.shape, 0)
    for t in range(k):
        out = jnp.where(row == n - k + t, halo[t:t + 1, :], out)
    return out


def _conv3(cur, halo, cw):
    rows = (*_shifts_down(cur, halo), cur)
    return rows[0] * cw[0:1, :] + rows[1] * cw[1:2, :] + cur * cw[2:3, :] + cw[3:4, :], rows


def _gelu_parts(x):
    xx = x * x
    t = jnp.tanh(x * (GELU_C + (GELU_C * GELU_A) * xx))
    half = 0.5 * x
    return half + half * t, t, xx, half


def _gelu_slope(t, xx, half):
    return (0.5 + 0.5 * t) + half * (1.0 - t * t) * (GELU_C + (3.0 * GELU_C * GELU_A) * xx)


def _geglu_specs(tm, tn, ncb):
    hb = tm // HALO

    def cur(off):
        return pl.BlockSpec((tm, tn), lambda j, i: (i, off + j))

    def prev(off):
        return pl.BlockSpec((HALO, tn), lambda j, i: (jnp.maximum(i * hb - 1, 0), off + j))

    def taps(off):
        return pl.BlockSpec((8, tn), lambda j, i: (0, off + j))

    return [cur(0), prev(0), cur(ncb), prev(ncb), taps(0), taps(ncb)]


def _geglu_fwd(u, cwb, name, tm=512, tn=1408):
    S, F2 = u.shape
    F = F2 // 2
    tm, tn = _tile(S, tm, HALO), _tile(F, tn)
    ncb = F // tn

    def body(g_ref, gp_ref, v_ref, vp_ref, cg_ref, cv_ref, y_ref):
        top = pl.program_id(1) > 0
        gp = jnp.where(top, gp_ref[...].astype(F32), 0.0)
        vp = jnp.where(top, vp_ref[...].astype(F32), 0.0)
        gc = _conv3(g_ref[...].astype(F32), gp, cg_ref[...])[0]
        vc = _conv3(v_ref[...].astype(F32), vp, cv_ref[...])[0]
        y_ref[...] = (_gelu_parts(gc)[0] * vc).astype(BF16)

    return _pcall(body, grid=(ncb, S // tm), in_specs=_geglu_specs(tm, tn, ncb),
                  out_specs=pl.BlockSpec((tm, tn), lambda j, i: (i, j)),
                  out_shape=jax.ShapeDtypeStruct((S, F), BF16),
                  compiler_params=_params("parallel", "parallel"), name=name)(u, u, u, u, cwb, cwb)


def _geglu_bwd(u, dy, cwb, name, tm=256, tn=1408):
    S, F2 = u.shape
    F = F2 // 2
    tm, tn = _tile(S, tm, HALO), _tile(F, tn)
    ncb = F // tn

    def body(g_ref, gp_ref, v_ref, vp_ref, cg_ref, cv_ref, dy_ref, dc_ref, dwg_ref, dwv_ref):
        i = pl.program_id(1)

        @pl.when(i == 0)
        def _():
            dwg_ref[...] = jnp.zeros_like(dwg_ref)
            dwv_ref[...] = jnp.zeros_like(dwv_ref)

        top = i > 0
        g, v = g_ref[...].astype(F32), v_ref[...].astype(F32)
        gp = jnp.where(top, gp_ref[...].astype(F32), 0.0)
        vp = jnp.where(top, vp_ref[...].astype(F32), 0.0)
        gc, g_rows = _conv3(g, gp, cg_ref[...])
        vc, v_rows = _conv3(v, vp, cv_ref[...])
        act, t, xx, half = _gelu_parts(gc)
        dact = _gelu_slope(t, xx, half)
        dyv = dy_ref[...].astype(F32)
        dgc = dyv * vc * dact
        dvc = dyv * act
        dc_ref[0] = dgc.astype(BF16)
        dc_ref[1] = dvc.astype(BF16)

        def taps(out_ref, dc, rows):
            for k, moved in enumerate(rows):
                out_ref[k:k + 1, :] += jnp.sum(dc * moved, axis=0, keepdims=True)
            out_ref[3:4, :] += jnp.sum(dc, axis=0, keepdims=True)

        taps(dwg_ref, dgc, g_rows)
        taps(dwv_ref, dvc, v_rows)

    return _pcall(body, grid=(ncb, S // tm),
                  in_specs=_geglu_specs(tm, tn, ncb) + [pl.BlockSpec((tm, tn), lambda j, i: (i, j))],
                  out_specs=[pl.BlockSpec((2, tm, tn), lambda j, i: (0, i, j)),
                             pl.BlockSpec((8, tn), lambda j, i: (0, j)), pl.BlockSpec((8, tn), lambda j, i: (0, j))],
                  out_shape=[jax.ShapeDtypeStruct((2, S, F), BF16), jax.ShapeDtypeStruct((8, F), F32),
                             jax.ShapeDtypeStruct((8, F), F32)],
                  compiler_params=_params("parallel", "arbitrary"), name=name)(u, u, u, u, cwb, cwb, dy)


def _conv_bwd(dc, cwb, name, tm=512, tn=1408):
    _, S, F = dc.shape
    tm, tn = _tile(S, tm, HALO), _tile(F, tn)
    ncb, nrb = F // tn, S // tm
    hb = tm // HALO

    def body(c_ref, n_ref, w_ref, du_ref):
        cur = c_ref[...].astype(F32)
        nxt = jnp.where(pl.program_id(2) < nrb - 1, n_ref[...].astype(F32), 0.0)
        w = w_ref[...]
        du = cur * w[2:3, :] + _shift_up(cur, nxt, 1) * w[1:2, :] + _shift_up(cur, nxt, 2) * w[0:1, :]
        du_ref[...] = du.astype(BF16)

    return _pcall(body, grid=(2, ncb, nrb),
                  in_specs=[pl.BlockSpec((None, tm, tn), lambda c, j, i: (c, i, j)),
                            pl.BlockSpec((None, HALO, tn), lambda c, j, i: (c, jnp.minimum((i + 1) * hb, S // HALO - 1), j)),
                            pl.BlockSpec((8, tn), lambda c, j, i: (0, c * ncb + j))],
                  out_specs=pl.BlockSpec((tm, tn), lambda c, j, i: (i, c * ncb + j)),
                  out_shape=jax.ShapeDtypeStruct((S, 2 * F), BF16),
                  compiler_params=_params("parallel", "parallel", "parallel"), name=name)(dc, dc, cwb)


def _adam_math(w, g, m, v):
    m = ADAM_B1 * m + (1.0 - ADAM_B1) * g
    v = ADAM_B2 * v + (1.0 - ADAM_B2) * (g * g)
    m_hat = m / (1.0 - ADAM_B1 ** ADAM_STEP)
    v_hat = v / (1.0 - ADAM_B2 ** ADAM_STEP)
    return -ADAM_LR * (m_hat / (jnp.sqrt(v_hat) + ADAM_EPS) + ADAM_WD * w), m, v


def _adamw(w, parts, m, v, name, tr=256):
    R, C = w.shape
    n, _, Cp = parts.shape
    tr = _tile(R, tr, 8)

    def body(w_ref, p_ref, m_ref, v_ref, g_out, d_out, m_out, v_out):
        g = p_ref[0, :, 0:C].astype(F32)
        for k in range(1, n):
            g = g + p_ref[k, :, 0:C].astype(F32)
        d, mn, vn = _adam_math(w_ref[...], g, m_ref[...], v_ref[...])
        g_out[...] = g
        d_out[...] = d
        m_out[...] = mn
        v_out[...] = vn

    spec = pl.BlockSpec((tr, C), lambda i: (i, 0))
    shape = jax.ShapeDtypeStruct((R, C), F32)
    return _pcall(body, grid=(R // tr,), in_specs=[spec, pl.BlockSpec((n, tr, Cp), lambda i: (0, i, 0)), spec, spec],
                  out_specs=[spec] * 4, out_shape=[shape] * 4, compiler_params=_params("parallel"), name=name)(w, parts, m, v)


def _adamw_chips(w, pair, parts, chip_ids, m, v, name, tr=256):
    R, C = w.shape
    Cp = pair.shape[2]
    by_columns = C == Cp and _tile(R, tr, 16) < 64
    tr, tc = (R, _tile(C, 256)) if by_columns else (_tile(R, tr, 16), C)

    def body(ids_ref, w_ref, own_ref, p1_ref, p2_ref, p3_ref, m_ref, v_ref, g_out, d_out, m_out, v_out):
        g = own_ref[:, 0:tc].astype(F32)
        for ref in (p1_ref, p2_ref, p3_ref):
            g = g + ref[:, 0:tc].astype(F32)
        d, mn, vn = _adam_math(w_ref[...], g, m_ref[...], v_ref[...])
        g_out[...] = g
        d_out[...] = d
        m_out[...] = mn
        v_out[...] = vn

    if by_columns:
        spec = pl.BlockSpec((tr, tc), lambda j, ids: (0, j))
    else:
        spec = pl.BlockSpec((tr, tc), lambda i, ids: (i, 0))

    def chip(k):
        if by_columns:
            return pl.BlockSpec((None, tr, tc), lambda j, ids: (ids[k], 0, j))
        return pl.BlockSpec((None, tr, Cp), lambda i, ids: (ids[k], i, 0))

    shape = jax.ShapeDtypeStruct((R, C), F32)
    grid_spec = pltpu.PrefetchScalarGridSpec(
        num_scalar_prefetch=1, grid=(C // tc if by_columns else R // tr,),
        in_specs=[spec, chip(0), chip(1), chip(2), chip(3), spec, spec], out_specs=[spec] * 4)
    return _pcall(body, grid_spec=grid_spec, out_shape=[shape] * 4, compiler_params=_params("parallel"),
                  name=name)(chip_ids, w, pair, parts, parts, parts, m, v)


def _place():
    return lax.axis_index("x"), lax.axis_index("y"), lax.axis_index("c")


def _other_chips(x, y):
    return [(1 - x, y), (x, 1 - y), (1 - x, 1 - y)]


IN_HBM = pl.BlockSpec(memory_space=pltpu.HBM)
SEM = pl.BlockSpec(memory_space=pltpu.SEMAPHORE)
EFFECT = pltpu.SideEffectType.DATAFLOW_SIDE_EFFECTING
TOKEN = jax.ShapeDtypeStruct((8, LANES), F32)
TOKEN_SPEC = pl.BlockSpec(memory_space=pltpu.VMEM)


def _in_hbm(a):
    return pltpu.with_memory_space_constraint(a, pltpu.HBM)


def _landing(shape):
    return _in_hbm(lax.empty(shape.shape, shape.dtype))


def _hbm_like(a):
    return pltpu.HBM(a.shape, a.dtype)


def _gather_places():
    x, y, c = _place()
    relay_from = (c * (1 - x) + (1 - c) * x, c * y + (1 - c) * (1 - y), c)
    relay_to = (c * x + (1 - c) * (1 - x), c * (1 - y) + (1 - c) * y, c)
    return (x, y, c), (x, y, 1 - c), (1 - x, y, c), (x, 1 - y, c), (1 - x, 1 - y, c), relay_from, relay_to


def _slot_copy(slot, ref, src, dst, send_sem, recv_sem, to):
    return pltpu.make_async_remote_copy(src_ref=slot(ref, *src), dst_ref=slot(ref, *dst), send_sem=send_sem,
                                        recv_sem=recv_sem, device_id=to, device_id_type=MESH)


def _split_call(body, arrays, sems_in, sems_out, after, name, token=True):
    na, ni, no = len(arrays), len(sems_in), len(sems_out)

    def wrapped(*refs):
        body(refs[:na], refs[na:na + ni], refs[na + ni + 1:na + ni + 1 + no])
        if token:
            refs[-1][...] = jnp.zeros_like(refs[-1])

    outs = _pcall(
        wrapped, in_specs=[IN_HBM] * na + [SEM] * ni + [HBM],
        out_specs=[SEM] * no + [IN_HBM] * na + ([TOKEN_SPEC] if token else []),
        out_shape=[pltpu.SemaphoreType.DMA((n,)) for n in sems_out] + [_hbm_like(s) for s in arrays] + ([TOKEN] if token else []),
        input_output_aliases={a: no + a for a in range(na)},
        compiler_params=pltpu.CompilerParams(has_side_effects=EFFECT), name=name,
    )(*[_in_hbm(s) for s in arrays], *sems_in, after)
    return list(outs[:no]), list(outs[no:no + na]), (outs[-1] if token else None)


def _gather_start(landing, slots, after, name):
    na = len(landing)

    def body(land, _, sems):
        me, sib, xn, yn, _, _, _ = _gather_places()
        for a in range(na):
            for k, to in enumerate((sib, xn, yn)):
                _slot_copy(slots[a], land[a], me, me, sems[0].at[3 * a + k], sems[1].at[3 * a + k], to).start()

    return _split_call(body, landing, [], [3 * na, 3 * na], after, name)


def _gather_relay(gathered, sems1, slots, after, name):
    na = len(gathered)

    def body(gath, taken, given):
        me, sib, xn, yn, _, relay_from, relay_to = _gather_places()
        for a in range(na):
            for k, peer in enumerate((sib, xn, yn)):
                arrival = _slot_copy(slots[a], gath[a], me, peer, taken[0].at[3 * a + k], taken[1].at[3 * a + k], peer)
                arrival.wait_send()
                arrival.wait_recv()
        for a in range(na):
            _slot_copy(slots[a], gath[a], relay_from, relay_from, given[0].at[a], given[1].at[a], relay_to).start()
            for k, peer in enumerate((xn, yn)):
                _slot_copy(slots[a], gath[a], peer, peer, given[2].at[2 * a + k], given[3].at[2 * a + k], sib).start()

    return _split_call(body, gathered, sems1, [na, na, 2 * na, 2 * na], after, name)


def _gather_pass(gathered, relay_sems, slots, after, name):
    na = len(gathered)

    def body(gath, taken, given):
        me, sib, xn, yn, diag, relay_from, relay_to = _gather_places()
        for a in range(na):
            _slot_copy(slots[a], gath[a], relay_from, relay_from, taken[0].at[a], taken[1].at[a], relay_to).wait_send()
            _slot_copy(slots[a], gath[a], me, diag, taken[0].at[a], taken[1].at[a], relay_to).wait_recv()
        for a in range(na):
            _slot_copy(slots[a], gath[a], diag, diag, given[0].at[a], given[1].at[a], sib).start()

    return _split_call(body, gathered, relay_sems, [na, na], after, name)


def _gather_finish(gathered, pass_sems, diag_sems, slots, after, name):
    na = len(gathered)

    def body(gath, taken, _):
        (x, y, c), sib, xn, yn, diag, _, _ = _gather_places()
        for a in range(na):
            for k, peer in enumerate((xn, yn)):
                passed = _slot_copy(slots[a], gath[a], peer, (peer[0], peer[1], 1 - c), taken[0].at[2 * a + k],
                                    taken[1].at[2 * a + k], sib)
                passed.wait_send()
                passed.wait_recv()
            passed = _slot_copy(slots[a], gath[a], diag, (diag[0], diag[1], 1 - c), taken[2].at[a], taken[3].at[a], sib)
            passed.wait_send()
            passed.wait_recv()

    return _split_call(body, gathered, list(pass_sems) + list(diag_sems), [], after, name, token=False)[1]


def _pair_copy(view, src, land, send_sems, recv_sems, chip):
    x, y, c = _place()
    return pltpu.make_async_remote_copy(
        src_ref=view(src, chip, 1 - c), dst_ref=land.at[chip], send_sem=send_sems.at[chip], recv_sem=recv_sems.at[chip],
        device_id=(x, y, 1 - c), device_id_type=MESH)


def _pair_start(grad, view, block, after, name):
    def body(src, land, after_ref, send_sems, recv_sems, src_thru, land_thru, token):
        for chip in range(N_CHIP):
            _pair_copy(view, src, land, send_sems, recv_sems, chip).start()
        token[...] = jnp.zeros_like(token)

    sems = pltpu.SemaphoreType.DMA((N_CHIP,))
    land = jax.ShapeDtypeStruct((N_CHIP, *block), BF16)
    return _pcall(
        body, in_specs=[IN_HBM, IN_HBM, HBM], out_specs=[SEM, SEM, IN_HBM, IN_HBM, TOKEN_SPEC],
        out_shape=[sems, sems, _hbm_like(grad), _hbm_like(land), TOKEN], input_output_aliases={0: 2, 1: 3},
        compiler_params=pltpu.CompilerParams(has_side_effects=EFFECT), name=name,
    )(_in_hbm(grad), _landing(land), after)


def _pair_wait(grad, recv, send_sems, recv_sems, view, after, name):
    def body(src, land, send, recv_s, after_ref, src_thru, land_thru):
        for chip in range(N_CHIP):
            copy = _pair_copy(view, src, land, send, recv_s, chip)
            copy.wait_send()
            copy.wait_recv()

    return _pcall(
        body, in_specs=[IN_HBM, IN_HBM, SEM, SEM, HBM], out_specs=[IN_HBM, IN_HBM],
        out_shape=[_hbm_like(grad), _hbm_like(recv)], input_output_aliases={0: 0, 1: 1},
        compiler_params=pltpu.CompilerParams(has_side_effects=EFFECT), name=name,
    )(grad, recv, send_sems, recv_sems, after)


def _chip_start(pair, after, name):
    def body(src, land, after_ref, send_sems, recv_sems, src_thru, land_thru, token):
        x, y, c = _place()
        for j, (px, py) in enumerate(_other_chips(x, y)):
            pltpu.make_async_remote_copy(
                src_ref=src.at[2 * px + py], dst_ref=land.at[2 * x + y], send_sem=send_sems.at[j], recv_sem=recv_sems.at[j],
                device_id=(px, py, c), device_id_type=MESH).start()
        token[...] = jnp.zeros_like(token)

    sems = pltpu.SemaphoreType.DMA((3,))
    return _pcall(
        body, in_specs=[IN_HBM, IN_HBM, HBM], out_specs=[SEM, SEM, IN_HBM, IN_HBM, TOKEN_SPEC],
        out_shape=[sems, sems, _hbm_like(pair), _hbm_like(pair), TOKEN], input_output_aliases={0: 2, 1: 3},
        compiler_params=pltpu.CompilerParams(has_side_effects=EFFECT), name=name,
    )(_in_hbm(pair), _landing(pair), after)


def _chip_wait(pair, parts, send_sems, recv_sems, after, name):
    def body(src, land, send, recv, after_ref, src_thru, land_thru):
        x, y, c = _place()
        for j, (px, py) in enumerate(_other_chips(x, y)):
            copy = pltpu.make_async_remote_copy(
                src_ref=src.at[2 * px + py], dst_ref=land.at[2 * px + py], send_sem=send.at[j], recv_sem=recv.at[j],
                device_id=(px, py, c), device_id_type=MESH)
            copy.wait_send()
            copy.wait_recv()

    return _pcall(
        body, in_specs=[IN_HBM, IN_HBM, SEM, SEM, HBM], out_specs=[IN_HBM, IN_HBM],
        out_shape=[_hbm_like(pair), _hbm_like(parts)], input_output_aliases={0: 0, 1: 1},
        compiler_params=pltpu.CompilerParams(has_side_effects=EFFECT), name=name,
    )(pair, parts, send_sems, recv_sems, after)


def _pair_add(core, grad, recv, block, grad_spec, name):
    _, R, C = recv.shape
    tr = block

    def body(c_ref, g_ref, r_ref, o_ref):
        o_ref[...] = (g_ref[...].astype(F32) + r_ref[...].astype(F32)).astype(BF16)

    grid_spec = pltpu.PrefetchScalarGridSpec(
        num_scalar_prefetch=1, grid=(N_CHIP, R // tr),
        in_specs=[grad_spec, pl.BlockSpec((None, tr, C), lambda k, i, c: (k, i, 0))],
        out_specs=pl.BlockSpec((None, tr, C), lambda k, i, c: (k, i, 0)))
    return _pcall(body, grid_spec=grid_spec, out_shape=jax.ShapeDtypeStruct(recv.shape, BF16),
                  compiler_params=_params("parallel", "parallel"), name=name)(core, grad, recv)


def _small_copies(gath, send_sems, recv_sems):
    x, y, c = _place()
    peers = [(x, y, 1 - c)] + [(px, py, pc) for px, py in _other_chips(x, y) for pc in (c, 1 - c)]
    pairs = []
    for a, ref in enumerate(gath):
        mine = ref.at[4 * x + 2 * y + c]
        for k, (px, py, pc) in enumerate(peers):
            sems = dict(send_sem=send_sems.at[7 * a + k], recv_sem=recv_sems.at[7 * a + k], device_id=(px, py, pc),
                        device_id_type=MESH)
            pairs.append((pltpu.make_async_remote_copy(src_ref=mine, dst_ref=mine, **sems),
                          pltpu.make_async_remote_copy(src_ref=mine, dst_ref=ref.at[4 * px + 2 * py + pc], **sems)))
    return pairs


def _small_start(landing, after, name):
    na = len(landing)

    def body(*refs):
        for send, _ in _small_copies(refs[:na], refs[na + 1], refs[na + 2]):
            send.start()
        refs[-1][...] = jnp.zeros_like(refs[-1])

    sems = pltpu.SemaphoreType.DMA((7 * na,))
    outs = _pcall(
        body, in_specs=[IN_HBM] * na + [HBM], out_specs=[SEM, SEM] + [IN_HBM] * na + [TOKEN_SPEC],
        out_shape=[sems, sems] + [_hbm_like(s) for s in landing] + [TOKEN],
        input_output_aliases={a: 2 + a for a in range(na)},
        compiler_params=pltpu.CompilerParams(has_side_effects=EFFECT), name=name,
    )(*[_in_hbm(s) for s in landing], after)
    return outs[0], outs[1], outs[2:2 + na], outs[-1]


def _small_wait(gathered, send_sems, recv_sems, after, name):
    na = len(gathered)

    def body(*refs):
        for send, arrival in _small_copies(refs[:na], refs[na], refs[na + 1]):
            send.wait_send()
            arrival.wait_recv()

    return list(_pcall(
        body, in_specs=[IN_HBM] * na + [SEM, SEM, HBM], out_specs=[IN_HBM] * na,
        out_shape=[_hbm_like(g) for g in gathered], input_output_aliases={a: a for a in range(na)},
        compiler_params=pltpu.CompilerParams(has_side_effects=EFFECT), name=name,
    )(*gathered, send_sems, recv_sems, after))


def _small_finish(gathered, params, name):
    na, npar = len(gathered), len(params)

    def body(*refs):
        g_refs, wmv = refs[:na], refs[na:na + 3 * npar]
        o_sums, o_params = refs[na + 3 * npar:2 * na + 3 * npar], refs[2 * na + 3 * npar:]
        sums = []
        for a in range(na):
            acc = g_refs[a][0]
            for k in range(1, N_DEV):
                acc = acc + g_refs[a][k]
            o_sums[a][...] = acc
            sums.append(acc)
        for j, (a, row, _, _, _) in enumerate(params):
            g = sums[a][row:row + 1, :]
            d, mn, vn = _adam_math(wmv[3 * j][...], g, wmv[3 * j + 1][...], wmv[3 * j + 2][...])
            for out, val in zip(o_params[4 * j:4 * j + 4], (g, d, mn, vn)):
                out[...] = val

    vm = pl.BlockSpec(memory_space=pltpu.VMEM)
    flat = [t for p in params for t in p[2:]]
    out_shape = [jax.ShapeDtypeStruct(g.shape[1:], F32) for g in gathered]
    out_shape += [jax.ShapeDtypeStruct(p[2].shape, F32) for p in params for _ in range(4)]
    outs = _pcall(body, in_specs=[vm] * (na + 3 * npar), out_specs=[vm] * len(out_shape), out_shape=out_shape,
                  name=name)(*gathered, *flat)
    return outs[:na], [outs[na + 4 * j:na + 4 * j + 4] for j in range(npar)]


def _local_step(x, tgt, gains, weights):
    g_pre_mix, g_post_mix, g_pre_ffn, g_post_ffn, g_sb, g_dil = gains
    S, D = x.shape
    hs = g_sb.shape[1] // HEAD_DIM
    hd = g_dil.shape[1] // HEAD_DIM
    cos2, sin_signed = _rope_tables(S)

    h1 = _rms_fwd(x, g_pre_mix + weights.start(), "rms_in")
    w_in_g = weights.w_in(h1)
    proj = _mm_nn(h1, w_in_g, BF16, "proj", tn=768)
    o_sb, ct_sb, mixed = _sb_fwd(proj, g_sb + weights.relay_out(proj), hs, hs + hd, "sb_fwd")
    o_dl, lse_dl, mixed = _dil_fwd(proj, cos2, sin_signed, g_dil + weights.after_sb(o_sb), mixed, 3 * hs, hd, "dil_fwd")
    w_out_g = weights.w_out(o_dl)
    mix = _mm_nn(mixed, w_out_g, F32, "mix_out", tn=1024)
    x2, h2 = _mid_fwd(x, mix, g_post_mix + weights.after_mix(mix), g_pre_ffn, "mid_fwd")
    w_up_g, cwb = weights.w_up(h2)
    u = _mm_nn(h2, w_up_g, BF16, "ffn_up", b_transposed=True)
    y = _geglu_fwd(u, cwb + weights.forward_down(u), "geglu_fwd")
    w_down_g = weights.w_down(y)
    f = _mm_nn(y, w_down_g, F32, "ffn_down", tn=1024, tk=2816)

    dy, df, dg_post_ffn, loss = _loss_bwd(x2, f, tgt, g_post_ffn, "loss_bwd")
    dyv = _mm_nt(df, w_down_g, BF16, "d_y", tn=1408)
    dw_down = _mm_tn(y, df, D, BF16, "dw_down", tm=1408, tn=1024)
    dc, dcw_g, dcw_v = _geglu_bwd(u, dyv, cwb + weights.grad("w_down", dw_down), "geglu_bwd")
    du = _conv_bwd(dc, cwb + weights.grad_reduce("w_down", dc), "conv_bwd")
    dh2 = _mm_nt(du, w_up_g, BF16, "d_h2", tk=1408, b_transposed=True, per_step=2)
    dw_up = _mm_tn(du, h2, D, BF16, "dw_up", tm=1408, tn=1024)
    dx2, dmix, dg_pre_ffn, dg_post_mix = _mid_bwd(
        dy, dh2, x2, mix, g_pre_ffn + weights.grad("w_up", dw_up), g_post_mix, "mid_bwd")
    dmixed = _mm_nt(dmix, w_out_g, BF16, "d_mixed", after=jnp.reshape(weights.grad_reduce("w_up", dmix), (1, 1)))
    dw_out = _mm_tn(mixed, dmix, D, BF16, "dw_out", tn=1024)
    dproj, dg_sb = _sb_bwd(proj, g_sb + weights.grad("w_out", dw_out), o_sb, ct_sb, dmixed, 0, hs, "sb_bwd")
    dproj, dg_dil = _dil_bwd(proj, cos2, sin_signed, g_dil + weights.grad_reduce("w_out", dg_sb), o_dl, lse_dl,
                             dmixed, dproj, hs, 3 * hs, hd, "dil_bwd")
    dw_in = _mm_tn(h1, dproj, w_in_g.shape[2], BF16, "dw_in", tn=768)
    weights.grad("w_in", dw_in)
    dep = weights.grad_reduce("w_in", dproj)
    dh1 = _mm_nt(dproj, w_in_g, BF16, "d_h1", tk=768, after=jnp.reshape(dep, (1, 1)), per_step=4)
    grad_x, dg_pre_mix = _first_bwd(dx2, dh1, x, g_pre_mix, "first_bwd")
    small = (dg_pre_mix, dg_post_mix, dg_pre_ffn, dg_post_ffn, dg_sb[0:1], dg_dil[0:1], jnp.concatenate([dcw_g, dcw_v], axis=1))
    weights.small(small, loss)
    return loss, grad_x, small


def _pad_cols(a, to):
    return jnp.pad(a, ((0, 0), (0, to - a.shape[1])))


def kernel(x, pre_mix_gain, post_mix_gain, pre_ffn_gain, post_ffn_gain, w_in, sb_out_gain, dil_out_gain, w_out, w_up, conv_w, conv_b, w_down, loss_target, m_pre_mix_gain, m_post_mix_gain, m_pre_ffn_gain, m_post_ffn_gain, m_w_in, m_sb_out_gain, m_dil_out_gain, m_w_out, m_w_up, m_conv_w, m_conv_b, m_w_down, v_pre_mix_gain, v_post_mix_gain, v_pre_ffn_gain, v_post_ffn_gain, v_w_in, v_sb_out_gain, v_dil_out_gain, v_w_out, v_w_up, v_conv_w, v_conv_b, v_w_down):
    xb, tb = x[0], loss_target[0]
    S, D = xb.shape
    w_in, w_out, w_up, w_down, conv_w = w_in[0], w_out[0], w_up[0], w_down[0], conv_w[0]
    n_in, e_rows = w_in.shape[1], w_out.shape[0]
    cu, half = w_up.shape[1], w_down.shape[0]
    assert cu == 2 * half and half % 16 == 0
    cup = -(-cu // LANES) * LANES
    fp = N_CHIP * cup
    px, py, pc = _place()
    me = 4 * px + 2 * py + pc
    core = jnp.reshape(pc, (1,)).astype(jnp.int32)

    w_up_t, m_up_t, v_up_t = (jnp.swapaxes(t, 0, 1) for t in (w_up, m_w_up[0], v_w_up[0]))

    def by_dev(ref, qx, qy, qc):
        return ref.at[4 * qx + 2 * qy + qc]

    def down_slot(ref, qx, qy, qc):
        return ref.at[2 * qx + qy, pl.ds(qc * half, half)]

    def by_pair(ref, chip, k):
        return ref.at[chip, k]

    def down_pair(ref, chip, k):
        return ref.at[chip, pl.ds(k * half, half)]

    def pair_spec(tr, cols):
        return pl.BlockSpec((None, None, tr, cols), lambda k, i, c: (k, c[0], i, 0))

    tr_in, tr_up = _tile(D, 512, 16), _tile(cup, 256, 16)
    grad_plan = {
        "w_in": ((N_CHIP, 2, D, n_in), by_pair, (D, n_in), tr_in, pair_spec(tr_in, n_in)),
        "w_out": ((N_CHIP, 2, e_rows, D), by_pair, (e_rows, D), e_rows, pair_spec(e_rows, D)),
        "w_up": ((N_CHIP, 2, cup, D), by_pair, (cup, D), tr_up, pair_spec(tr_up, D)),
        "w_down": ((N_CHIP, cup, D), down_pair, (half, D), half,
                   pl.BlockSpec((None, half, D), lambda k, i, c: (k, c[0], 0))),
    }

    class Exchanges:
        def __init__(self):
            self.in_flight = {}

        def start(self):
            def own_slot(shard):
                return lax.dynamic_update_index_in_dim(lax.empty((N_DEV, *shard.shape), shard.dtype), shard, me, 0)

            self.group_slots = {"in": [by_dev], "out": [by_dev], "up": [by_dev, by_dev], "down": [down_slot]}
            self.flight = {}
            sems, gath, token = _gather_start([own_slot(w_in.astype(BF16))], [by_dev], core, "gather_in_start")
            self.flight["in"] = (sems, gath)
            zero = token[0, 0]
            self.landing = {
                "out": [own_slot((w_out + zero).astype(BF16))],
                "up": [own_slot(jnp.pad(w_up_t + zero, ((0, cup - cu), (0, 0))).astype(BF16)),
                       own_slot(jnp.pad(conv_w + zero, ((0, 8 - conv_w.shape[0]), (0, cup - cu))))],
                "down": [lax.dynamic_update_slice(jnp.zeros((N_CHIP, cup, D), BF16), (w_down + zero).astype(BF16)[None],
                                                  (2 * px + py, pc * half, 0))]}
            return zero

        def begin(self, group, after):
            sems, gath, token = _gather_start(self.landing[group], self.group_slots[group], after, "gather_%s_start" % group)
            self.flight[group] = (sems, gath)
            return token

        def relay(self, group, after):
            sems, gath = self.flight[group]
            sems, gath, token = _gather_relay(gath, sems, self.group_slots[group], after, "gather_%s_relay" % group)
            self.flight[group] = (sems, gath)
            return token

        def pass_on(self, group, after):
            sems, gath = self.flight[group]
            diag_sems, gath, token = _gather_pass(gath, sems[:2], self.group_slots[group], after, "gather_%s_pass" % group)
            self.flight[group] = (sems[2:], diag_sems, gath)
            return token

        def finish(self, group, after):
            pass_sems, diag_sems, gath = self.flight[group]
            return _gather_finish(gath, pass_sems, diag_sems, self.group_slots[group], after, "gather_%s_finish" % group)

        def w_in(self, after):
            token = self.begin("up", self.begin("out", self.relay("in", after)))
            return self.finish("in", self.pass_on("in", token))[0]

        def relay_out(self, after):
            return self.relay("out", after)[0, 0]

        def after_sb(self, after):
            return self.begin("down", self.relay("up", self.pass_on("out", after)))[0, 0]

        def w_out(self, after):
            return self.finish("out", after)[0].reshape(1, N_DEV * e_rows, D)

        def after_mix(self, after):
            return self.pass_on("up", after)[0, 0]

        def w_up(self, after):
            w_up_g, cw_g = self.finish("up", after)
            cb = _pad_cols(conv_b.reshape(N_DEV, cu), cup).reshape(1, 2 * fp)
            cw_full = jnp.transpose(cw_g[:, :3, :], (1, 0, 2)).reshape(3, 2 * fp)
            cwb = jnp.concatenate([cw_full, cb, jnp.zeros((4, 2 * fp), F32)], axis=0)
            return w_up_g, cwb

        def forward_down(self, after):
            return self.relay("down", after)[0, 0]

        def w_down(self, after):
            return self.finish("down", self.pass_on("down", after))[0].reshape(1, fp, D)

        def small(self, small, loss):
            d_pre_mix, d_post_mix, d_pre_ffn, d_post_ffn, d_sb, d_dil, d_conv = small

            def rows_of(*vectors):
                n = vectors[0].shape[1]
                row = lax.broadcasted_iota(jnp.int32, (8, n), 0)
                out = jnp.zeros((8, n), F32)
                for k, vec in enumerate(vectors):
                    out = jnp.where(row == k, vec, out)
                return out

            parts = [rows_of(d_pre_mix, d_post_mix, d_pre_ffn, d_post_ffn, jnp.broadcast_to(loss[:, :1], (1, D))),
                     rows_of(d_sb, d_dil), d_conv]
            landing = [lax.dynamic_update_index_in_dim(lax.empty((N_DEV, *p.shape), F32), p, me, 0) for p in parts]
            self.small_flight = _small_start(landing, parts[0], "small_start")

        def small_sums(self, after):
            send, recv, gath, _ = self.small_flight
            gath = _small_wait(gath, send, recv, after, "small_wait")
            params = [(0, 0, pre_mix_gain, m_pre_mix_gain, v_pre_mix_gain), (0, 1, post_mix_gain, m_post_mix_gain, v_post_mix_gain),
                      (0, 2, pre_ffn_gain, m_pre_ffn_gain, v_pre_ffn_gain), (0, 3, post_ffn_gain, m_post_ffn_gain, v_post_ffn_gain),
                      (1, 0, sb_out_gain, m_sb_out_gain, v_sb_out_gain), (1, 1, dil_out_gain, m_dil_out_gain, v_dil_out_gain)]
            (gains_sum, _, conv_sum), gain_steps = _small_finish(gath, params, "small_finish")
            return gains_sum[4, 0], conv_sum, gain_steps

        def grad(self, name, dw):
            view_shape, view, block, tr, spec = grad_plan[name]
            send, recv_sems, dw, recv, token = _pair_start(dw.reshape(view_shape), view, block, core, "pair_start_" + name)
            self.in_flight[name] = (dw, recv, send, recv_sems)
            return token[0, 0]

        def grad_reduce(self, name, after):
            _, view, _, tr, spec = grad_plan[name]
            dw, recv = _pair_wait(*self.in_flight[name], view, after, "pair_wait_" + name)
            pair = _pair_add(core, dw, recv, tr, spec, "pair_add_" + name)
            send, recv_sems, pair, parts, token = _chip_start(pair, recv, "chip_start_" + name)
            self.in_flight[name] = (pair, parts, send, recv_sems)
            self.last_token = token
            return token[0, 0]

        def grad_parts(self, name, after):
            return _chip_wait(*self.in_flight[name], after, "chip_wait_" + name)

    exchanges = Exchanges()
    gains = (pre_mix_gain, post_mix_gain, pre_ffn_gain, post_ffn_gain, sb_out_gain, dil_out_gain)
    loss, grad_x, small = _local_step(xb, tb, gains, exchanges)


    chip_ids = jnp.stack([2 * px + py, 2 * (1 - px) + py, 2 * px + 1 - py, 2 * (1 - px) + 1 - py]).astype(jnp.int32)
    out_w_down = _adamw_chips(w_down, *exchanges.grad_parts("w_down", exchanges.small_flight[3]), chip_ids, m_w_down[0], v_w_down[0], "adam_w_down")
    out_up_t = _adamw_chips(w_up_t, *exchanges.grad_parts("w_up", out_w_down[1]), chip_ids, m_up_t, v_up_t, "adam_w_up")
    out_w_up = [jnp.swapaxes(o, 0, 1) for o in out_up_t]
    out_w_out = _adamw_chips(w_out, *exchanges.grad_parts("w_out", out_up_t[1]), chip_ids, m_w_out[0], v_w_out[0], "adam_w_out")
    loss_out, g_conv, gain_steps = exchanges.small_sums(out_w_out[1])
    out_pre_mix, out_post_mix, out_pre_ffn, out_post_ffn, out_sb, out_dil = gain_steps
    g_conv_b = g_conv[3].reshape(N_DEV, cup)[:, :cu].reshape(1, N_DEV * cu)
    g_conv_w = lax.dynamic_index_in_dim(g_conv[0:3].reshape(3, N_DEV, cup), me, axis=1, keepdims=False)[:, :cu]
    out_conv_b = _adamw(conv_b, g_conv_b[None], m_conv_b, v_conv_b, "adam_conv_b")
    out_conv_w = _adamw(conv_w, g_conv_w[None], m_conv_w[0], v_conv_w[0], "adam_conv_w")
    out_w_in = _adamw_chips(w_in, *exchanges.grad_parts("w_in", out_conv_w[1]), chip_ids, m_w_in[0], v_w_in[0], "adam_w_in")

    order = [out_pre_mix, out_post_mix, out_pre_ffn, out_post_ffn, [o[None] for o in out_w_in], out_sb, out_dil,
             [o[None] for o in out_w_out], [o[None] for o in out_w_up], [o[None] for o in out_conv_w], out_conv_b,
             [o[None] for o in out_w_down]]
    outs = [loss_out, grad_x[None]]
    for k in range(4):
        outs += [o[k] for o in order]
    return tuple(outs)
```

```python
import math

import jax
import jax.numpy as jnp
from jax import lax
from jax.experimental import pallas as pl
from jax.experimental.pallas import tpu as pltpu

F32 = jnp.float32
BF16 = jnp.bfloat16
HEAD_DIM = 128
LANES = 128
KEY_BLOCK = 128
DILATIONS = (1, 4, 16)
RMS_EPS = 1e-6
ROPE_THETA = 10000.0
NEG = -1e30
ADAM_LR, ADAM_B1, ADAM_B2, ADAM_EPS, ADAM_WD, ADAM_STEP = 0.001, 0.9, 0.999, 1e-08, 0.01, 10
MESH = pl.DeviceIdType.MESH
N_DEV = 8
N_CHIP = 4
HBM = pl.BlockSpec(memory_space=pl.ANY)
VMEM_LIMIT = 56 * 1024 * 1024

_pcall = pl.pallas_call


def _tile(n, pref, mult=LANES):
    best = None
    t = mult
    while t <= min(n, pref):
        if n % t == 0:
            best = t
        t += mult
    return n if best is None else best


def _params(*sem):
    return pltpu.CompilerParams(dimension_semantics=sem, vmem_limit_bytes=VMEM_LIMIT)


def _following(after, body, in_specs, args):
    if after is None:
        return body, list(in_specs), list(args)
    n = len(args)

    def ordered(*refs):
        body(*refs[:n], *refs[n + 1:])

    return ordered, list(in_specs) + [HBM], list(args) + [after]


def _dot(a, b, dims):
    return lax.dot_general(a, b, (dims, ((), ())), preferred_element_type=F32)


NN = ((1,), (0,))
NT = ((1,), (1,))
TN = ((0,), (0,))


def _mm_body(dims, nk, tile):
    if nk == 1:
        def single(a_ref, b_ref, o_ref):
            o_ref[...] = _dot(a_ref[...].astype(BF16), b_ref[...].astype(BF16), dims).astype(o_ref.dtype)

        return single, []

    def body(a_ref, b_ref, o_ref, acc_ref):
        k = pl.program_id(2)

        @pl.when(k == 0)
        def _():
            acc_ref[...] = jnp.zeros_like(acc_ref)

        acc_ref[...] += _dot(a_ref[...].astype(BF16), b_ref[...].astype(BF16), dims)

        @pl.when(k == nk - 1)
        def _():
            o_ref[...] = acc_ref[...].astype(o_ref.dtype)

    return body, [pltpu.VMEM(tile, F32)]


def _mm_nn(a, b3, out_dtype, name, tm=1024, tn=1408, tk=2048, b_transposed=False):
    M, K = a.shape
    C, n = b3.shape[0], b3.shape[1 if b_transposed else 2]
    tm, tk, tn = _tile(M, tm, 8), _tile(K, tk), _tile(n, tn)
    npc, nk = n // tn, K // tk
    body, scratch = _mm_body(NT if b_transposed else NN, nk, (tm, tn))
    b_spec = (pl.BlockSpec((None, tn, tk), lambda i, j, k: (j // npc, j % npc, k)) if b_transposed
              else pl.BlockSpec((None, tk, tn), lambda i, j, k: (j // npc, k, j % npc)))
    return _pcall(
        body, grid=(M // tm, C * npc, nk),
        in_specs=[pl.BlockSpec((tm, tk), lambda i, j, k: (i, k)), b_spec],
        out_specs=pl.BlockSpec((tm, tn), lambda i, j, k: (i, j)),
        out_shape=jax.ShapeDtypeStruct((M, C * n), out_dtype), scratch_shapes=scratch,
        compiler_params=_params("parallel", "parallel", "arbitrary"), name=name)(a, b3)


def _mm_nt(a, b3, out_dtype, name, tm=1024, tn=1024, tk=2048, after=None, b_transposed=False, per_step=1):
    M, _ = a.shape
    C, N, n = (b3.shape[0], b3.shape[2], b3.shape[1]) if b_transposed else b3.shape
    tm, tn, tk = _tile(M, tm, 8), _tile(N, tn), _tile(n, tk)
    dims = NN if b_transposed else NT
    extra = [] if after is None else [after]
    if per_step > 1 and tk == n and C % per_step == 0:
        nk, scratch = C // per_step, [pltpu.VMEM((tm, tn), F32)]
        b3 = b3.reshape(nk, per_step, *b3.shape[1:])
        a_spec = pl.BlockSpec((tm, per_step * n), lambda i, j, k: (i, k))
        if b_transposed:
            b_spec = pl.BlockSpec((None, per_step, n, tn), lambda i, j, k: (k, 0, 0, j))
        else:
            b_spec = pl.BlockSpec((None, per_step, tn, n), lambda i, j, k: (k, 0, j, 0))

        def body(a_ref, b_ref, *rest):
            o_ref, acc_ref = rest[len(extra):]
            k = pl.program_id(2)

            @pl.when(k == 0)
            def _():
                acc_ref[...] = jnp.zeros_like(acc_ref)

            b = b_ref[...].astype(BF16)
            b = b.reshape(per_step * n, tn) if b_transposed else jnp.concatenate([b[u] for u in range(per_step)], axis=1)
            acc_ref[...] += _dot(a_ref[...].astype(BF16), b, dims)

            @pl.when(k == nk - 1)
            def _():
                o_ref[...] = acc_ref[...].astype(o_ref.dtype)
    else:
        kpc = n // tk
        nk = C * kpc
        inner, scratch = _mm_body(dims, nk, (tm, tn))
        a_spec = pl.BlockSpec((tm, tk), lambda i, j, k: (i, k))
        b_spec = (pl.BlockSpec((None, tk, tn), lambda i, j, k: (k // kpc, k % kpc, j)) if b_transposed
                  else pl.BlockSpec((None, tn, tk), lambda i, j, k: (k // kpc, j, k % kpc)))

        def body(a_ref, b_ref, *rest):
            inner(a_ref, b_ref, *rest[len(extra):])

    return _pcall(
        body, grid=(M // tm, N // tn, nk), in_specs=[a_spec, b_spec] + [HBM] * len(extra),
        out_specs=pl.BlockSpec((tm, tn), lambda i, j, k: (i, j)),
        out_shape=jax.ShapeDtypeStruct((M, N), out_dtype), scratch_shapes=scratch,
        compiler_params=_params("parallel", "parallel", "arbitrary"), name=name)(a, b3, *extra)


def _mm_tn(x, y, n, out_dtype, name, tm=1024, tn=1408, tk=2048, after=None):
    S, P = x.shape
    C = y.shape[1] // n
    tm, tn, tk = _tile(P, tm), _tile(n, tn), _tile(S, tk, 8)
    npc, nk = n // tn, S // tk
    inner, scratch = _mm_body(TN, nk, (tm, tn))
    extra = [] if after is None else [after]

    def body(x_ref, y_ref, *rest):
        inner(x_ref, y_ref, *rest[len(extra):])

    return _pcall(
        body, grid=(P // tm, C * npc, nk),
        in_specs=[pl.BlockSpec((tk, tm), lambda i, j, k: (k, i)),
                  pl.BlockSpec((tk, tn), lambda i, j, k: (k, j))] + [HBM] * len(extra),
        out_specs=pl.BlockSpec((None, tm, tn), lambda i, j, k: (j // npc, i, j % npc)),
        out_shape=jax.ShapeDtypeStruct((C, P, n), out_dtype), scratch_shapes=scratch,
        compiler_params=_params("parallel", "parallel", "arbitrary"), name=name)(x, y, *extra)


def _rms_scale(v):
    return lax.rsqrt(jnp.mean(v * v, axis=-1, keepdims=True) + RMS_EPS)


def _rms_bwd(gy, v, r):
    return r * gy - v * (r * r * r * jnp.mean(gy * v, axis=-1, keepdims=True))


def _rows_spec(tm, d):
    return pl.BlockSpec((tm, d), lambda i: (i, 0))


def _vec_spec(d):
    return pl.BlockSpec((1, d), lambda i: (0, 0))


def _rms_fwd(x, g, name, tm=256, after=None):
    S, D = x.shape

    def body(x_ref, g_ref, h_ref):
        v = x_ref[...]
        h_ref[...] = (v * _rms_scale(v) * g_ref[...]).astype(BF16)

    body, in_specs, args = _following(after, body, [_rows_spec(tm, D), _vec_spec(D)], [x, g])
    return _pcall(body, grid=(S // tm,), in_specs=in_specs, out_specs=_rows_spec(tm, D),
                  out_shape=jax.ShapeDtypeStruct((S, D), BF16), compiler_params=_params("parallel"), name=name)(*args)


def _mid_fwd(x, mix, g_post, g_pre, name, tm=256, after=None):
    S, D = x.shape

    def body(x_ref, m_ref, gp_ref, gn_ref, x2_ref, h_ref):
        m = m_ref[...]
        x2 = x_ref[...] + m * _rms_scale(m) * gp_ref[...]
        x2_ref[...] = x2
        h_ref[...] = (x2 * _rms_scale(x2) * gn_ref[...]).astype(BF16)

    body, in_specs, args = _following(
        after, body, [_rows_spec(tm, D), _rows_spec(tm, D), _vec_spec(D), _vec_spec(D)], [x, mix, g_post, g_pre])
    return _pcall(body, grid=(S // tm,), in_specs=in_specs,
                  out_specs=[_rows_spec(tm, D), _rows_spec(tm, D)],
                  out_shape=[jax.ShapeDtypeStruct((S, D), F32), jax.ShapeDtypeStruct((S, D), BF16)],
                  compiler_params=_params("parallel"), name=name)(*args)


def _loss_bwd(x2, f, tgt, g_post, name, tm=256):
    S, D = x2.shape

    def body(x2_ref, f_ref, t_ref, g_ref, dy_ref, df_ref, dg_ref, ls_ref):
        i = pl.program_id(0)

        @pl.when(i == 0)
        def _():
            dg_ref[...] = jnp.zeros_like(dg_ref)
            ls_ref[...] = jnp.zeros_like(ls_ref)

        fv = f_ref[...]
        r = _rms_scale(fv)
        g = g_ref[...]
        err = x2_ref[...] + fv * r * g - t_ref[...]
        ls_ref[...] += jnp.broadcast_to(0.5 * jnp.sum(jnp.mean(err * err, axis=-1, keepdims=True), axis=0, keepdims=True), ls_ref.shape)
        dy = err * (1.0 / D)
        dy_ref[...] = dy
        df_ref[...] = _rms_bwd(dy * g, fv, r).astype(BF16)
        dg_ref[...] += jnp.sum(dy * fv * r, axis=0, keepdims=True)

    return _pcall(body, grid=(S // tm,),
                  in_specs=[_rows_spec(tm, D), _rows_spec(tm, D), _rows_spec(tm, D), _vec_spec(D)],
                  out_specs=[_rows_spec(tm, D), _rows_spec(tm, D), _vec_spec(D), _vec_spec(LANES)],
                  out_shape=[jax.ShapeDtypeStruct((S, D), F32), jax.ShapeDtypeStruct((S, D), BF16),
                             jax.ShapeDtypeStruct((1, D), F32), jax.ShapeDtypeStruct((1, LANES), F32)],
                  compiler_params=_params("arbitrary"), name=name)(x2, f, tgt, g_post)


def _mid_bwd(dy, dh2, x2, mix, g_pre, g_post, name, tm=256, after=None):
    S, D = dy.shape

    def body(dy_ref, dh_ref, x2_ref, m_ref, gn_ref, gp_ref, dx2_ref, dm_ref, dgn_ref, dgp_ref):
        i = pl.program_id(0)

        @pl.when(i == 0)
        def _():
            dgn_ref[...] = jnp.zeros_like(dgn_ref)
            dgp_ref[...] = jnp.zeros_like(dgp_ref)

        x2, dh = x2_ref[...], dh_ref[...].astype(F32)
        r = _rms_scale(x2)
        dx2 = dy_ref[...] + _rms_bwd(dh * gn_ref[...], x2, r)
        dgn_ref[...] += jnp.sum(dh * x2 * r, axis=0, keepdims=True)
        dx2_ref[...] = dx2
        m = m_ref[...]
        rm = _rms_scale(m)
        dm_ref[...] = _rms_bwd(dx2 * gp_ref[...], m, rm).astype(BF16)
        dgp_ref[...] += jnp.sum(dx2 * m * rm, axis=0, keepdims=True)

    body, in_specs, args = _following(
        after, body, [_rows_spec(tm, D)] * 4 + [_vec_spec(D)] * 2, [dy, dh2, x2, mix, g_pre, g_post])
    return _pcall(body, grid=(S // tm,), in_specs=in_specs,
                  out_specs=[_rows_spec(tm, D), _rows_spec(tm, D), _vec_spec(D), _vec_spec(D)],
                  out_shape=[jax.ShapeDtypeStruct((S, D), F32), jax.ShapeDtypeStruct((S, D), BF16),
                             jax.ShapeDtypeStruct((1, D), F32), jax.ShapeDtypeStruct((1, D), F32)],
                  compiler_params=_params("arbitrary"), name=name)(*args)


def _first_bwd(dx2, dh1, x, g_pre, name, tm=256):
    S, D = x.shape

    def body(dx2_ref, dh_ref, x_ref, g_ref, gx_ref, dg_ref):
        i = pl.program_id(0)

        @pl.when(i == 0)
        def _():
            dg_ref[...] = jnp.zeros_like(dg_ref)

        xv, dh = x_ref[...], dh_ref[...].astype(F32)
        r = _rms_scale(xv)
        gx_ref[...] = dx2_ref[...] + _rms_bwd(dh * g_ref[...], xv, r)
        dg_ref[...] += jnp.sum(dh * xv * r, axis=0, keepdims=True)

    return _pcall(body, grid=(S // tm,), in_specs=[_rows_spec(tm, D)] * 3 + [_vec_spec(D)],
                  out_specs=[_rows_spec(tm, D), _vec_spec(D)],
                  out_shape=[jax.ShapeDtypeStruct((S, D), F32), jax.ShapeDtypeStruct((1, D), F32)],
                  compiler_params=_params("arbitrary"), name=name)(dx2, dh1, x, g_pre)


def _logsig_pair(z):
    lb = jnp.minimum(z, 0.0) - jnp.log(1.0 + jnp.exp(-jnp.abs(z)))
    return lb, lb - z


SB_KEY_BLOCK = 256


def _sum_matrix(strict):
    ia = lax.broadcasted_iota(jnp.int32, (SB_KEY_BLOCK, SB_KEY_BLOCK), 0)
    ib = lax.broadcasted_iota(jnp.int32, (SB_KEY_BLOCK, SB_KEY_BLOCK), 1)
    return ((ia > ib) if strict == ">" else (ia < ib)).astype(BF16)


def _row_total(sums, v, col):
    return jnp.broadcast_to(sums[:, col:col + 1] + v[:, col:col + 1], (v.shape[0], LANES))


def _lanes(c, width):
    return jnp.tile(c, (1, width // LANES))


def _split_dot(v, u):
    hi = v.astype(BF16)
    lo = (v - hi.astype(F32)).astype(BF16)
    return _dot(hi, u, NN) + _dot(lo, u, NN)


def _head_out(o, g):
    return o * _rms_scale(o) * g


def _sb_fwd(proj, gain, n_heads, mixed_heads, name, tq=1024, after=None):
    S = proj.shape[0]
    H, tk = n_heads, SB_KEY_BLOCK
    tq = _tile(S, tq, 2 * tk)
    scale = HEAD_DIM ** -0.5

    def body(q_ref, k_ref, v_ref, g_ref, o_ref, ct_ref, mx_ref, oacc, cacc):
        i = pl.program_id(1)
        oacc[...] = jnp.zeros_like(oacc)
        cacc[...] = jnp.zeros_like(cacc)
        sums = _sum_matrix(">")

        def run(blocks):
            scored = []
            for k0, r0, diagonal in blocks:
                rows = pl.ds(r0, tq - r0)
                lb, lk = _logsig_pair(_dot(q_ref[rows, :].astype(BF16), k_ref[pl.ds(k0, tk), :].astype(BF16), NT) * scale)
                causal = None
                if diagonal:
                    causal = (lax.broadcasted_iota(jnp.int32, (tq - r0, tk), 1)
                              < lax.broadcasted_iota(jnp.int32, (tq - r0, tk), 0))
                    lk = jnp.where(causal, lk, 0.0)
                scored.append((k0, rows, causal, lb, lk))
            summed = [(k0, rows, causal, lb, lk, _split_dot(lk, sums)) for k0, rows, causal, lb, lk in scored]
            weights = []
            for k0, rows, causal, lb, lk, after in summed:
                c = cacc[rows, :]
                a = jnp.exp(lb + after + _lanes(c, tk))
                if causal is not None:
                    a = jnp.where(causal, a, 0.0)
                cacc[rows, :] = c + _row_total(after, lk, 0)
                weights.append((k0, rows, a.astype(BF16)))
            for k0, rows, a in weights:
                oacc[rows, :] += _dot(a, v_ref[pl.ds(k0, tk), :].astype(BF16), NN)

        for d in reversed(range(0, tq // tk, 2)):
            run([(pl.multiple_of(i * tq + e * tk, tk), e * tk, True) for e in (d + 1, d)])
        per_trip = tq // tk

        def step(it, carry):
            k0 = pl.multiple_of((i - 1 - it) * tq, tq)
            run([(pl.multiple_of(k0 + e * tk, tk), 0, False) for e in reversed(range(per_trip))])
            return carry

        lax.fori_loop(0, i, step, 0)
        o = oacc[...]
        o_ref[...] = o
        ct_ref[...] = cacc[...]
        mx_ref[...] = _head_out(o, g_ref[...]).astype(BF16)

    blk = pl.BlockSpec((tq, HEAD_DIM), lambda h, i: (i, h))
    body, in_specs, args = _following(
        after, body,
        [blk, pl.BlockSpec((S, HEAD_DIM), lambda h, i: (0, H + h)),
         pl.BlockSpec((S, HEAD_DIM), lambda h, i: (0, 2 * H + h)), pl.BlockSpec((1, HEAD_DIM), lambda h, i: (0, h))],
        [proj, proj, proj, gain])
    return _pcall(
        body, grid=(H, S // tq), in_specs=in_specs,
        out_specs=[blk, blk, blk],
        out_shape=[jax.ShapeDtypeStruct((S, H * HEAD_DIM), F32), jax.ShapeDtypeStruct((S, H * HEAD_DIM), F32),
                   jax.ShapeDtypeStruct((S, mixed_heads * HEAD_DIM), BF16)],
        scratch_shapes=[pltpu.VMEM((tq, HEAD_DIM), F32), pltpu.VMEM((tq, LANES), F32)],
        compiler_params=_params("parallel", "arbitrary"), name=name)(*args)


def _sb_bwd(proj, gain, o_raw, ctot, dmixed, dm_col0, n_heads, name, tq=1024, after=None):
    S = proj.shape[0]
    H, tk = n_heads, SB_KEY_BLOCK
    tq = _tile(S, tq, 2 * tk)
    nq = S // tq
    scale = HEAD_DIM ** -0.5

    def body(q_ref, k_ref, v_ref, g_ref, o_ref, ct_ref, dm_ref, dproj_ref, dg_ref,
             dkacc, dvacc, dqacc, pfx, gcar, dos, stage_q, stage_k, stage_v, out_sems):
        h, i = pl.program_id(0), pl.program_id(1)

        @pl.when(i == 0)
        def _():
            dkacc[...] = jnp.zeros_like(dkacc)
            dvacc[...] = jnp.zeros_like(dvacc)
            dg_ref[...] = jnp.zeros_like(dg_ref)

        o, dm, g = o_ref[...], dm_ref[...].astype(F32), g_ref[...]
        r = _rms_scale(o)
        dos[...] = _rms_bwd(dm * g, o, r).astype(BF16)
        dg_ref[...] += jnp.broadcast_to(jnp.sum(dm * o * r, axis=0, keepdims=True), dg_ref.shape)
        dqacc[...] = jnp.zeros_like(dqacc)
        pfx[...] = jnp.zeros_like(pfx)
        gcar[...] = jnp.zeros_like(gcar)
        later, earlier = _sum_matrix(">"), _sum_matrix("<")

        def run(blocks):
            scored = []
            for k0, r0, diagonal in blocks:
                rows, keys = pl.ds(r0, tq - r0), pl.ds(k0, tk)
                lb, lk = _logsig_pair(_dot(q_ref[rows, :].astype(BF16), k_ref[keys, :].astype(BF16), NT) * scale)
                da = _dot(dos[rows, :], v_ref[keys, :].astype(BF16), NT)
                causal = None
                if diagonal:
                    causal = (lax.broadcasted_iota(jnp.int32, (tq - r0, tk), 1)
                              < lax.broadcasted_iota(jnp.int32, (tq - r0, tk), 0))
                    lk = jnp.where(causal, lk, 0.0)
                scored.append((rows, keys, causal, lb, lk, da))
            summed = [(*blk, _split_dot(blk[4], later)) for blk in scored]
            weighted = []
            for rows, keys, causal, lb, lk, da, after in summed:
                p = pfx[rows, :] + _row_total(after, lk, 0)
                pfx[rows, :] = p
                a = jnp.exp(lb + after + _lanes(ct_ref[rows, :] - p, tk))
                if causal is not None:
                    a = jnp.where(causal, a, 0.0)
                dl = da * a
                weighted.append((rows, keys, causal, lb, a.astype(BF16), dl, _dot(dl.astype(BF16), earlier, NN)))
            cotangents = []
            for rows, keys, causal, lb, a, dl, before in weighted:
                gc = gcar[rows, :]
                gcar[rows, :] = gc + _row_total(before, dl, tk - 1)
                sig = jnp.exp(lb)
                gsum = (before + _lanes(gc, tk)) * sig
                if causal is not None:
                    gsum = jnp.where(causal, gsum, 0.0)
                cotangents.append((rows, keys, a, ((dl * (1.0 - sig) - gsum) * scale).astype(BF16)))
            for rows, keys, a, dz in cotangents:
                q, do = q_ref[rows, :].astype(BF16), dos[rows, :]
                dvacc[keys, :] += _dot(a, do, TN)
                dqacc[rows, :] += _dot(dz, k_ref[keys, :].astype(BF16), NN)
                dkacc[keys, :] += _dot(dz, q, TN)

        per_trip = tq // tk

        def step(j, carry):
            k0 = pl.multiple_of(j * tq, tq)
            run([(pl.multiple_of(k0 + e * tk, tk), 0, False) for e in range(per_trip)])
            return carry

        lax.fori_loop(0, i, step, 0)
        for d in range(0, tq // tk, 2):
            run([(pl.multiple_of(i * tq + e * tk, tk), e * tk, True) for e in (d, d + 1)])
        def columns(block):
            return pl.ds(pl.multiple_of(block * HEAD_DIM, HEAD_DIM), HEAD_DIM)

        dq_out = pltpu.make_async_copy(stage_q, dproj_ref.at[pl.ds(pl.multiple_of(i * tq, tq), tq), columns(h)], out_sems.at[0])
        dkv_out = [pltpu.make_async_copy(stage_k, dproj_ref.at[:, columns(H + h)], out_sems.at[1]),
                   pltpu.make_async_copy(stage_v, dproj_ref.at[:, columns(2 * H + h)], out_sems.at[2])]

        @pl.when((h > 0) | (i > 0))
        def _():
            dq_out.wait()

        stage_q[...] = dqacc[...].astype(BF16)
        dq_out.start()

        @pl.when(i == nq - 1)
        def _():
            @pl.when(h > 0)
            def _():
                for cp in dkv_out:
                    cp.wait()

            stage_k[...] = dkacc[...].astype(BF16)
            stage_v[...] = dvacc[...].astype(BF16)
            for cp in dkv_out:
                cp.start()

        @pl.when((h == H - 1) & (i == nq - 1))
        def _():
            dq_out.wait()
            for cp in dkv_out:
                cp.wait()

    blk = pl.BlockSpec((tq, HEAD_DIM), lambda h, i: (i, h))
    W = H * HEAD_DIM
    body, in_specs, args = _following(
        after, body,
        [blk, pl.BlockSpec((S, HEAD_DIM), lambda h, i: (0, H + h)),
         pl.BlockSpec((S, HEAD_DIM), lambda h, i: (0, 2 * H + h)), pl.BlockSpec((1, HEAD_DIM), lambda h, i: (0, h)),
         blk, blk, pl.BlockSpec((tq, HEAD_DIM), lambda h, i: (i, dm_col0 + h))],
        [proj, proj, proj, gain, o_raw, ctot, dmixed])
    return _pcall(
        body, grid=(H, nq), in_specs=in_specs,
        out_specs=[HBM, pl.BlockSpec((8, HEAD_DIM), lambda h, i: (0, h))],
        out_shape=[jax.ShapeDtypeStruct(proj.shape, BF16), jax.ShapeDtypeStruct((8, W), F32)],
        scratch_shapes=[pltpu.VMEM((S, HEAD_DIM), F32), pltpu.VMEM((S, HEAD_DIM), F32), pltpu.VMEM((tq, HEAD_DIM), F32),
                        pltpu.VMEM((tq, LANES), F32), pltpu.VMEM((tq, LANES), F32), pltpu.VMEM((tq, HEAD_DIM), BF16),
                        pltpu.VMEM((tq, HEAD_DIM), BF16), pltpu.VMEM((S, HEAD_DIM), BF16), pltpu.VMEM((S, HEAD_DIM), BF16),
                        pltpu.SemaphoreType.DMA((3,))],
        compiler_params=_params("arbitrary", "arbitrary"), name=name)(*args)


def _rope_tables(S):
    inv_freq = ROPE_THETA ** (-jnp.arange(0, HEAD_DIM, 2, dtype=F32) / HEAD_DIM)
    ang = jnp.arange(S, dtype=F32)[:, None] * inv_freq[None, :]
    cos, sin = jnp.cos(ang), jnp.sin(ang)
    return jnp.concatenate([cos, cos], axis=1), jnp.concatenate([-sin, sin], axis=1)


def _rope(v, cos2, sin_signed):
    return v * cos2 + pltpu.roll(v, HEAD_DIM // 2, axis=1) * sin_signed


def _dil_rows(d, r, l0, n):
    if d == 1:
        return pl.ds(l0 if isinstance(l0, int) else pl.multiple_of(l0, KEY_BLOCK), n)
    return pl.ds(r + d * l0, n, stride=d)


def _dil_blocks(S, visit):
    B = KEY_BLOCK
    group = 16
    for b, d in enumerate(DILATIONS):
        nb = S // d // B
        if nb == 1:
            g = math.gcd(d, group)

            def trip(t, carry, b=b, d=d, g=g):
                visit([(b, d, t * g + u, 0, True) for u in range(g)])
                return carry

            lax.fori_loop(0, d // g, trip, 0)
        elif d == 1:
            visit([(b, d, 0, 0, True)])
            g = max(k for k in range(1, group + 2) if (nb - 1) % k == 0)

            def trip(t, carry, b=b, d=d, g=g):
                visit([(b, d, 0, (1 + t * g + u) * B, False) for u in range(g)])
                return carry

            lax.fori_loop(0, (nb - 1) // g, trip, 0)
        else:
            g = math.gcd(d, max(group // nb, 1))

            def trip(t, carry, b=b, d=d, nb=nb, g=g):
                visit([(b, d, t * g + u, n * B, n == 0) for u in range(g) for n in range(nb)])
                return carry

            lax.fori_loop(0, d // g, trip, 0)


def _dil_mask(first):
    B = KEY_BLOCK
    nk = B if first else 2 * B
    iq = lax.broadcasted_iota(jnp.int32, (B, nk), 0)
    ik = lax.broadcasted_iota(jnp.int32, (B, nk), 1)
    return (ik <= iq) if first else ((ik >= iq) & (ik <= iq + B))


def _dil_fwd(proj, cos2, sin_signed, gain, mixed, col0, n_heads, name, after=None):
    S = proj.shape[0]
    H, B = n_heads, KEY_BLOCK
    scale = HEAD_DIM ** -0.5
    rc = _tile(S, 256, 8)

    def body(q_ref, k_ref, v_ref, c_ref, s_ref, g_ref, mixed_in, o_ref, l_ref, mx_ref, qr, kr, vf, *per_branch):
        ob, lb = per_branch[:len(DILATIONS)], per_branch[len(DILATIONS):]

        def rope_rows(t, carry):
            rows = pl.ds(pl.multiple_of(t * rc, rc), rc)
            qr[rows, :] = _rope(q_ref[rows, :].astype(F32), c_ref[rows, :], s_ref[rows, :])
            kr[rows, :] = _rope(k_ref[rows, :].astype(F32), c_ref[rows, :], s_ref[rows, :])
            vf[rows, :] = v_ref[rows, :].astype(F32)
            return carry

        lax.fori_loop(0, S // rc, rope_rows, 0)

        def visit(blocks):
            scores = []
            for b, d, r, l0, first in blocks:
                qrows = _dil_rows(d, r, l0, B)
                krows = qrows if first else _dil_rows(d, r, l0 - B, 2 * B)
                s = _dot(qr[qrows, :].astype(BF16), kr[krows, :].astype(BF16), NT) * scale
                scores.append((b, qrows, krows, jnp.where(_dil_mask(first), s, NEG)))
            weights = []
            for b, qrows, krows, s in scores:
                m = jnp.max(s, axis=1, keepdims=True)
                p = jnp.exp(s - m)
                den = jnp.sum(p, axis=1, keepdims=True)
                lb[b][qrows, :] = jnp.broadcast_to(m + jnp.log(den), (B, LANES))
                weights.append((b, qrows, krows, p.astype(BF16), den))
            for b, qrows, krows, p, den in weights:
                ob[b][qrows, :] = _dot(p, vf[krows, :].astype(BF16), NN) / den

        _dil_blocks(S, visit)

        def combine(t, carry):
            rows = pl.ds(pl.multiple_of(t * rc, rc), rc)
            l0, l1, l2 = lb[0][rows, :], lb[1][rows, :], lb[2][rows, :]
            m = jnp.maximum(jnp.maximum(l0, l1), l2)
            w0, w1, w2 = jnp.exp(l0 - m), jnp.exp(l1 - m), jnp.exp(l2 - m)
            den = w0 + w1 + w2
            o = (w0 * ob[0][rows, :] + w1 * ob[1][rows, :] + w2 * ob[2][rows, :]) / den
            o_ref[rows, :] = o
            l_ref[rows, :] = m + jnp.log(den)
            mx_ref[rows, :] = _head_out(o, g_ref[...]).astype(BF16)
            return carry

        lax.fori_loop(0, S // rc, combine, 0)

    def col(k):
        return pl.BlockSpec((S, HEAD_DIM), lambda h: (0, col0 + k * H + h))

    tab = pl.BlockSpec((S, HEAD_DIM), lambda h: (0, 0))
    out = pl.BlockSpec((S, HEAD_DIM), lambda h: (0, h))
    W = H * HEAD_DIM
    first = mixed.shape[1] // HEAD_DIM - H
    body, in_specs, args = _following(
        after, body, [col(0), col(1), col(2), tab, tab, pl.BlockSpec((1, HEAD_DIM), lambda h: (0, h)), HBM],
        [proj, proj, proj, cos2, sin_signed, gain, mixed])
    return _pcall(
        body, grid=(H,), in_specs=in_specs,
        out_specs=[out, out, pl.BlockSpec((S, HEAD_DIM), lambda h: (0, first + h))],
        out_shape=[jax.ShapeDtypeStruct((S, W), F32), jax.ShapeDtypeStruct((S, W), F32),
                   jax.ShapeDtypeStruct(mixed.shape, BF16)],
        input_output_aliases={6: 2},
        scratch_shapes=[pltpu.VMEM((S, HEAD_DIM), F32)] * (3 + 2 * len(DILATIONS)),
        compiler_params=_params("parallel"), name=name)(*args)


def _dil_bwd(proj, cos2, sin_signed, gain, o_raw, lse, dmixed, dproj, dm_col0, col0, n_heads, name, after=None):
    S = proj.shape[0]
    H, B = n_heads, KEY_BLOCK
    scale = HEAD_DIM ** -0.5
    rc = _tile(S, 256, 8)

    def body(q_ref, k_ref, v_ref, c_ref, s_ref, g_ref, o_ref, l_ref, dm_ref, dproj_in, dproj_ref, dg_ref,
             qr, kr, vf, dos, dsum, dqr, dkr, dvv, stage_q, stage_k, stage_v, out_sems):
        dg_ref[...] = jnp.zeros_like(dg_ref)

        def prep(t, carry):
            rows = pl.ds(pl.multiple_of(t * rc, rc), rc)
            qr[rows, :] = _rope(q_ref[rows, :].astype(F32), c_ref[rows, :], s_ref[rows, :])
            kr[rows, :] = _rope(k_ref[rows, :].astype(F32), c_ref[rows, :], s_ref[rows, :])
            vf[rows, :] = v_ref[rows, :].astype(F32)
            o, dm = o_ref[rows, :], dm_ref[rows, :].astype(F32)
            r = _rms_scale(o)
            do = _rms_bwd(dm * g_ref[...], o, r)
            dg_ref[...] += jnp.broadcast_to(jnp.sum(dm * o * r, axis=0, keepdims=True), dg_ref.shape)
            dos[rows, :] = do
            dsum[rows, :] = jnp.broadcast_to(jnp.sum(do * o, axis=1, keepdims=True), (rc, LANES))
            dqr[rows, :] = jnp.zeros((rc, HEAD_DIM), F32)
            dkr[rows, :] = jnp.zeros((rc, HEAD_DIM), F32)
            dvv[rows, :] = jnp.zeros((rc, HEAD_DIM), F32)
            return carry

        lax.fori_loop(0, S // rc, prep, 0)

        def visit(blocks):
            products = []
            for b, d, r, l0, first in blocks:
                qrows = _dil_rows(d, r, l0, B)
                krows = qrows if first else _dil_rows(d, r, l0 - B, 2 * B)
                qs, ks = qr[qrows, :].astype(BF16), kr[krows, :].astype(BF16)
                do = dos[qrows, :].astype(BF16)
                s = jnp.where(_dil_mask(first), _dot(qs, ks, NT) * scale, NEG)
                dp = _dot(do, vf[krows, :].astype(BF16), NT)
                products.append((qrows, krows, qs, ks, do, s, dp))
            cotangents = []
            for qrows, krows, qs, ks, do, s, dp in products:
                p = jnp.exp(s - l_ref[qrows, :][:, 0:1])
                ds = (p * (dp - dsum[qrows, :][:, 0:1]) * scale).astype(BF16)
                cotangents.append((qrows, krows, qs, ks, do, p.astype(BF16), ds))
            for qrows, krows, qs, ks, do, p, ds in cotangents:
                dqr[qrows, :] += _dot(ds, ks, NN)
                dkr[krows, :] += _dot(ds, qs, TN)
                dvv[krows, :] += _dot(p, do, TN)

        _dil_blocks(S, visit)

        def finish(t, carry):
            rows = pl.ds(pl.multiple_of(t * rc, rc), rc)
            c, s = c_ref[rows, :], s_ref[rows, :]
            dq, dk = dqr[rows, :], dkr[rows, :]
            stage_q[rows, :] = (dq * c + pltpu.roll(dq * s, HEAD_DIM // 2, axis=1)).astype(BF16)
            stage_k[rows, :] = (dk * c + pltpu.roll(dk * s, HEAD_DIM // 2, axis=1)).astype(BF16)
            stage_v[rows, :] = dvv[rows, :].astype(BF16)
            return carry

        h = pl.program_id(0)
        outs = [pltpu.make_async_copy(
            stage, dproj_ref.at[:, pl.ds(pl.multiple_of((col0 + k * H + h) * HEAD_DIM, HEAD_DIM), HEAD_DIM)], out_sems.at[k])
            for k, stage in enumerate((stage_q, stage_k, stage_v))]

        @pl.when(h > 0)
        def _():
            for cp in outs:
                cp.wait()

        lax.fori_loop(0, S // rc, finish, 0)
        for cp in outs:
            cp.start()

        @pl.when(h == H - 1)
        def _():
            for cp in outs:
                cp.wait()

    def col(k):
        return pl.BlockSpec((S, HEAD_DIM), lambda h: (0, col0 + k * H + h))

    tab = pl.BlockSpec((S, HEAD_DIM), lambda h: (0, 0))
    out = pl.BlockSpec((S, HEAD_DIM), lambda h: (0, h))
    W = H * HEAD_DIM
    big, half = pltpu.VMEM((S, HEAD_DIM), F32), pltpu.VMEM((S, HEAD_DIM), BF16)
    body, in_specs, args = _following(
        after, body,
        [col(0), col(1), col(2), tab, tab, pl.BlockSpec((1, HEAD_DIM), lambda h: (0, h)), out, out,
         pl.BlockSpec((S, HEAD_DIM), lambda h: (0, dm_col0 + h)), HBM],
        [proj, proj, proj, cos2, sin_signed, gain, o_raw, lse, dmixed, dproj])
    return _pcall(
        body, grid=(H,), in_specs=in_specs,
        out_specs=[HBM, pl.BlockSpec((8, HEAD_DIM), lambda h: (0, h))],
        out_shape=[jax.ShapeDtypeStruct(dproj.shape, BF16), jax.ShapeDtypeStruct((8, W), F32)],
        input_output_aliases={9: 0},
        scratch_shapes=[big, big, big, big, pltpu.VMEM((S, LANES), F32), big, big, big, half, half, half,
                        pltpu.SemaphoreType.DMA((3,))],
        compiler_params=_params("arbitrary"), name=name)(*args)


GELU_C = math.sqrt(2.0 / math.pi)
GELU_A = 0.044715
HALO = 16


def _shifts_down(cur, halo):
    row = lax.broadcasted_iota(jnp.int32, cur.shape, 0)
    first, second = row == 0, row == 1
    last, before_last = halo[HALO - 1:HALO, :], halo[HALO - 2:HALO - 1, :]
    two = jnp.where(first, before_last, jnp.where(second, last, pltpu.roll(cur, 2, axis=0)))
    return two, jnp.where(first, last, pltpu.roll(cur, 1, axis=0))


def _shift_up(cur, halo, k):
    n = cur.shape[0]
    out = pltpu.roll(cur, n - k, axis=0)
    row = lax.broadcasted_iota(jnp.int32, cur.shape, 0)
    for t in range(k):
        out = jnp.where(row == n - k + t, halo[t:t + 1, :], out)
    return out


def _conv3(cur, halo, cw):
    rows = (*_shifts_down(cur, halo), cur)
    return rows[0] * cw[0:1, :] + rows[1] * cw[1:2, :] + cur * cw[2:3, :] + cw[3:4, :], rows


def _gelu_parts(x):
    xx = x * x
    t = jnp.tanh(x * (GELU_C + (GELU_C * GELU_A) * xx))
    half = 0.5 * x
    return half + half * t, t, xx, half


def _gelu_slope(t, xx, half):
    return (0.5 + 0.5 * t) + half * (1.0 - t * t) * (GELU_C + (3.0 * GELU_C * GELU_A) * xx)


def _geglu_specs(tm, tn, ncb):
    hb = tm // HALO

    def cur(off):
        return pl.BlockSpec((tm, tn), lambda j, i: (i, off + j))

    def prev(off):
        return pl.BlockSpec((HALO, tn), lambda j, i: (jnp.maximum(i * hb - 1, 0), off + j))

    def taps(off):
        return pl.BlockSpec((8, tn), lambda j, i: (0, off + j))

    return [cur(0), prev(0), cur(ncb), prev(ncb), taps(0), taps(ncb)]


def _geglu_fwd(u, cwb, name, tm=512, tn=1408, after=None):
    S, F2 = u.shape
    F = F2 // 2
    tm, tn = _tile(S, tm, HALO), _tile(F, tn)
    ncb = F // tn

    def body(g_ref, gp_ref, v_ref, vp_ref, cg_ref, cv_ref, y_ref):
        top = pl.program_id(1) > 0
        gp = jnp.where(top, gp_ref[...].astype(F32), 0.0)
        vp = jnp.where(top, vp_ref[...].astype(F32), 0.0)
        gc = _conv3(g_ref[...].astype(F32), gp, cg_ref[...])[0]
        vc = _conv3(v_ref[...].astype(F32), vp, cv_ref[...])[0]
        y_ref[...] = (_gelu_parts(gc)[0] * vc).astype(BF16)

    body, in_specs, args = _following(after, body, _geglu_specs(tm, tn, ncb), [u, u, u, u, cwb, cwb])
    return _pcall(body, grid=(ncb, S // tm), in_specs=in_specs,
                  out_specs=pl.BlockSpec((tm, tn), lambda j, i: (i, j)),
                  out_shape=jax.ShapeDtypeStruct((S, F), BF16),
                  compiler_params=_params("parallel", "parallel"), name=name)(*args)


def _geglu_bwd(u, dy, cwb, name, tm=256, tn=1408, after=None):
    S, F2 = u.shape
    F = F2 // 2
    tm, tn = _tile(S, tm, HALO), _tile(F, tn)
    ncb = F // tn

    def body(g_ref, gp_ref, v_ref, vp_ref, cg_ref, cv_ref, dy_ref, dc_ref, dwg_ref, dwv_ref):
        i = pl.program_id(1)

        @pl.when(i == 0)
        def _():
            dwg_ref[...] = jnp.zeros_like(dwg_ref)
            dwv_ref[...] = jnp.zeros_like(dwv_ref)

        top = i > 0
        g, v = g_ref[...].astype(F32), v_ref[...].astype(F32)
        gp = jnp.where(top, gp_ref[...].astype(F32), 0.0)
        vp = jnp.where(top, vp_ref[...].astype(F32), 0.0)
        gc, g_rows = _conv3(g, gp, cg_ref[...])
        vc, v_rows = _conv3(v, vp, cv_ref[...])
        act, t, xx, half = _gelu_parts(gc)
        dact = _gelu_slope(t, xx, half)
        dyv = dy_ref[...].astype(F32)
        dgc = dyv * vc * dact
        dvc = dyv * act
        dc_ref[0] = dgc.astype(BF16)
        dc_ref[1] = dvc.astype(BF16)

        def taps(out_ref, dc, rows):
            for k, moved in enumerate(rows):
                out_ref[k:k + 1, :] += jnp.sum(dc * moved, axis=0, keepdims=True)
            out_ref[3:4, :] += jnp.sum(dc, axis=0, keepdims=True)

        taps(dwg_ref, dgc, g_rows)
        taps(dwv_ref, dvc, v_rows)

    body, in_specs, args = _following(
        after, body, _geglu_specs(tm, tn, ncb) + [pl.BlockSpec((tm, tn), lambda j, i: (i, j))], [u, u, u, u, cwb, cwb, dy])
    return _pcall(body, grid=(ncb, S // tm), in_specs=in_specs,
                  out_specs=[pl.BlockSpec((2, tm, tn), lambda j, i: (0, i, j)),
                             pl.BlockSpec((8, tn), lambda j, i: (0, j)), pl.BlockSpec((8, tn), lambda j, i: (0, j))],
                  out_shape=[jax.ShapeDtypeStruct((2, S, F), BF16), jax.ShapeDtypeStruct((8, F), F32),
                             jax.ShapeDtypeStruct((8, F), F32)],
                  compiler_params=_params("parallel", "arbitrary"), name=name)(*args)


def _conv_bwd(dc, cwb, name, tm=512, tn=1408, after=None):
    _, S, F = dc.shape
    tm, tn = _tile(S, tm, HALO), _tile(F, tn)
    ncb, nrb = F // tn, S // tm
    hb = tm // HALO

    def body(c_ref, n_ref, w_ref, du_ref):
        cur = c_ref[...].astype(F32)
        nxt = jnp.where(pl.program_id(2) < nrb - 1, n_ref[...].astype(F32), 0.0)
        w = w_ref[...]
        du = cur * w[2:3, :] + _shift_up(cur, nxt, 1) * w[1:2, :] + _shift_up(cur, nxt, 2) * w[0:1, :]
        du_ref[...] = du.astype(BF16)

    body, in_specs, args = _following(
        after, body,
        [pl.BlockSpec((None, tm, tn), lambda c, j, i: (c, i, j)),
         pl.BlockSpec((None, HALO, tn), lambda c, j, i: (c, jnp.minimum((i + 1) * hb, S // HALO - 1), j)),
         pl.BlockSpec((8, tn), lambda c, j, i: (0, c * ncb + j))], [dc, dc, cwb])
    return _pcall(body, grid=(2, ncb, nrb), in_specs=in_specs,
                  out_specs=pl.BlockSpec((tm, tn), lambda c, j, i: (i, c * ncb + j)),
                  out_shape=jax.ShapeDtypeStruct((S, 2 * F), BF16),
                  compiler_params=_params("parallel", "parallel", "parallel"), name=name)(*args)


def _adam_math(w, g, m, v):
    m = ADAM_B1 * m + (1.0 - ADAM_B1) * g
    v = ADAM_B2 * v + (1.0 - ADAM_B2) * (g * g)
    m_hat = m / (1.0 - ADAM_B1 ** ADAM_STEP)
    v_hat = v / (1.0 - ADAM_B2 ** ADAM_STEP)
    return -ADAM_LR * (m_hat / (jnp.sqrt(v_hat) + ADAM_EPS) + ADAM_WD * w), m, v


def _adamw(w, parts, m, v, name, tr=256):
    R, C = w.shape
    n, _, Cp = parts.shape
    tr = _tile(R, tr, 8)

    def body(w_ref, p_ref, m_ref, v_ref, g_out, d_out, m_out, v_out):
        g = p_ref[0, :, 0:C].astype(F32)
        for k in range(1, n):
            g = g + p_ref[k, :, 0:C].astype(F32)
        d, mn, vn = _adam_math(w_ref[...], g, m_ref[...], v_ref[...])
        g_out[...] = g
        d_out[...] = d
        m_out[...] = mn
        v_out[...] = vn

    spec = pl.BlockSpec((tr, C), lambda i: (i, 0))
    shape = jax.ShapeDtypeStruct((R, C), F32)
    return _pcall(body, grid=(R // tr,), in_specs=[spec, pl.BlockSpec((n, tr, Cp), lambda i: (0, i, 0)), spec, spec],
                  out_specs=[spec] * 4, out_shape=[shape] * 4, compiler_params=_params("parallel"), name=name)(w, parts, m, v)


def _adamw_chips(w, pair, parts, chip_ids, m, v, name, tr=256):
    R, C = w.shape
    Cp = pair.shape[2]
    by_columns = C == Cp and _tile(R, tr, 16) < 64
    tr, tc = (R, _tile(C, 256)) if by_columns else (_tile(R, tr, 16), C)

    def body(ids_ref, w_ref, own_ref, p1_ref, p2_ref, p3_ref, m_ref, v_ref, g_out, d_out, m_out, v_out):
        g = own_ref[:, 0:tc].astype(F32)
        for ref in (p1_ref, p2_ref, p3_ref):
            g = g + ref[:, 0:tc].astype(F32)
        d, mn, vn = _adam_math(w_ref[...], g, m_ref[...], v_ref[...])
        g_out[...] = g
        d_out[...] = d
        m_out[...] = mn
        v_out[...] = vn

    if by_columns:
        spec = pl.BlockSpec((tr, tc), lambda j, ids: (0, j))
    else:
        spec = pl.BlockSpec((tr, tc), lambda i, ids: (i, 0))

    def chip(k):
        if by_columns:
            return pl.BlockSpec((None, tr, tc), lambda j, ids: (ids[k], 0, j))
        return pl.BlockSpec((None, tr, Cp), lambda i, ids: (ids[k], i, 0))

    shape = jax.ShapeDtypeStruct((R, C), F32)
    grid_spec = pltpu.PrefetchScalarGridSpec(
        num_scalar_prefetch=1, grid=(C // tc if by_columns else R // tr,),
        in_specs=[spec, chip(0), chip(1), chip(2), chip(3), spec, spec], out_specs=[spec] * 4)
    return _pcall(body, grid_spec=grid_spec, out_shape=[shape] * 4, compiler_params=_params("parallel"),
                  name=name)(chip_ids, w, pair, parts, parts, parts, m, v)


def _place():
    return lax.axis_index("x"), lax.axis_index("y"), lax.axis_index("c")


def _other_chips(x, y):
    return [(1 - x, y), (x, 1 - y), (1 - x, 1 - y)]


IN_HBM = pl.BlockSpec(memory_space=pltpu.HBM)
SEM = pl.BlockSpec(memory_space=pltpu.SEMAPHORE)
EFFECT = pltpu.SideEffectType.DATAFLOW_SIDE_EFFECTING
TOKEN = jax.ShapeDtypeStruct((8, LANES), F32)
TOKEN_SPEC = pl.BlockSpec(memory_space=pltpu.VMEM)


def _in_hbm(a):
    return pltpu.with_memory_space_constraint(a, pltpu.HBM)


def _landing(shape):
    return _in_hbm(lax.empty(shape.shape, shape.dtype))


def _hbm_like(a):
    return pltpu.HBM(a.shape, a.dtype)


def _gather_places():
    x, y, c = _place()
    relay_from = (c * (1 - x) + (1 - c) * x, c * y + (1 - c) * (1 - y), c)
    relay_to = (c * x + (1 - c) * (1 - x), c * (1 - y) + (1 - c) * y, c)
    return (x, y, c), (x, y, 1 - c), (1 - x, y, c), (x, 1 - y, c), (1 - x, 1 - y, c), relay_from, relay_to


def _slot_copy(slot, ref, src, dst, send_sem, recv_sem, to):
    return pltpu.make_async_remote_copy(src_ref=slot(ref, *src), dst_ref=slot(ref, *dst), send_sem=send_sem,
                                        recv_sem=recv_sem, device_id=to, device_id_type=MESH)


def _split_call(body, arrays, sems_in, sems_out, after, name, token=True):
    na, ni, no = len(arrays), len(sems_in), len(sems_out)

    def wrapped(*refs):
        body(refs[:na], refs[na:na + ni], refs[na + ni + 1:na + ni + 1 + no])
        if token:
            refs[-1][...] = jnp.zeros_like(refs[-1])

    outs = _pcall(
        wrapped, in_specs=[IN_HBM] * na + [SEM] * ni + [HBM],
        out_specs=[SEM] * no + [IN_HBM] * na + ([TOKEN_SPEC] if token else []),
        out_shape=[pltpu.SemaphoreType.DMA((n,)) for n in sems_out] + [_hbm_like(s) for s in arrays] + ([TOKEN] if token else []),
        input_output_aliases={a: no + a for a in range(na)},
        compiler_params=pltpu.CompilerParams(has_side_effects=EFFECT), name=name,
    )(*[_in_hbm(s) for s in arrays], *sems_in, after)
    return list(outs[:no]), list(outs[no:no + na]), (outs[-1] if token else None)


def _gather_start(landing, slots, after, name):
    na = len(landing)

    def body(land, _, sems):
        me, sib, xn, yn, _, _, _ = _gather_places()
        for a in range(na):
            for k, to in enumerate((sib, xn, yn)):
                _slot_copy(slots[a], land[a], me, me, sems[0].at[3 * a + k], sems[1].at[3 * a + k], to).start()

    return _split_call(body, landing, [], [3 * na, 3 * na], after, name)


def _gather_relay(gathered, sems1, slots, after, name):
    na = len(gathered)

    def body(gath, taken, given):
        me, sib, xn, yn, _, relay_from, relay_to = _gather_places()
        for a in range(na):
            for k, peer in enumerate((sib, xn, yn)):
                arrival = _slot_copy(slots[a], gath[a], me, peer, taken[0].at[3 * a + k], taken[1].at[3 * a + k], peer)
                arrival.wait_send()
                arrival.wait_recv()
        for a in range(na):
            _slot_copy(slots[a], gath[a], relay_from, relay_from, given[0].at[a], given[1].at[a], relay_to).start()
            for k, peer in enumerate((xn, yn)):
                _slot_copy(slots[a], gath[a], peer, peer, given[2].at[2 * a + k], given[3].at[2 * a + k], sib).start()

    return _split_call(body, gathered, sems1, [na, na, 2 * na, 2 * na], after, name)


def _gather_pass(gathered, relay_sems, slots, after, name):
    na = len(gathered)

    def body(gath, taken, given):
        me, sib, xn, yn, diag, relay_from, relay_to = _gather_places()
        for a in range(na):
            _slot_copy(slots[a], gath[a], relay_from, relay_from, taken[0].at[a], taken[1].at[a], relay_to).wait_send()
            _slot_copy(slots[a], gath[a], me, diag, taken[0].at[a], taken[1].at[a], relay_to).wait_recv()
        for a in range(na):
            _slot_copy(slots[a], gath[a], diag, diag, given[0].at[a], given[1].at[a], sib).start()

    return _split_call(body, gathered, relay_sems, [na, na], after, name)


def _gather_finish(gathered, pass_sems, diag_sems, slots, after, name):
    na = len(gathered)

    def body(gath, taken, _):
        (x, y, c), sib, xn, yn, diag, _, _ = _gather_places()
        for a in range(na):
            for k, peer in enumerate((xn, yn)):
                passed = _slot_copy(slots[a], gath[a], peer, (peer[0], peer[1], 1 - c), taken[0].at[2 * a + k],
                                    taken[1].at[2 * a + k], sib)
                passed.wait_send()
                passed.wait_recv()
            passed = _slot_copy(slots[a], gath[a], diag, (diag[0], diag[1], 1 - c), taken[2].at[a], taken[3].at[a], sib)
            passed.wait_send()
            passed.wait_recv()

    return _split_call(body, gathered, list(pass_sems) + list(diag_sems), [], after, name, token=False)[1]


def _pair_copy(view, src, land, send_sems, recv_sems, chip):
    x, y, c = _place()
    return pltpu.make_async_remote_copy(
        src_ref=view(src, chip, 1 - c), dst_ref=land.at[chip], send_sem=send_sems.at[chip], recv_sem=recv_sems.at[chip],
        device_id=(x, y, 1 - c), device_id_type=MESH)


def _pair_start(grad, view, block, after, name):
    def body(src, land, after_ref, send_sems, recv_sems, src_thru, land_thru, token):
        for chip in range(N_CHIP):
            _pair_copy(view, src, land, send_sems, recv_sems, chip).start()
        token[...] = jnp.zeros_like(token)

    sems = pltpu.SemaphoreType.DMA((N_CHIP,))
    land = jax.ShapeDtypeStruct((N_CHIP, *block), BF16)
    return _pcall(
        body, in_specs=[IN_HBM, IN_HBM, HBM], out_specs=[SEM, SEM, IN_HBM, IN_HBM, TOKEN_SPEC],
        out_shape=[sems, sems, _hbm_like(grad), _hbm_like(land), TOKEN], input_output_aliases={0: 2, 1: 3},
        compiler_params=pltpu.CompilerParams(has_side_effects=EFFECT), name=name,
    )(_in_hbm(grad), _landing(land), after)


def _pair_wait(grad, recv, send_sems, recv_sems, view, after, name):
    def body(src, land, send, recv_s, after_ref, src_thru, land_thru):
        for chip in range(N_CHIP):
            copy = _pair_copy(view, src, land, send, recv_s, chip)
            copy.wait_send()
            copy.wait_recv()

    return _pcall(
        body, in_specs=[IN_HBM, IN_HBM, SEM, SEM, HBM], out_specs=[IN_HBM, IN_HBM],
        out_shape=[_hbm_like(grad), _hbm_like(recv)], input_output_aliases={0: 0, 1: 1},
        compiler_params=pltpu.CompilerParams(has_side_effects=EFFECT), name=name,
    )(grad, recv, send_sems, recv_sems, after)


def _chip_start(pair, after, name):
    def body(src, land, after_ref, send_sems, recv_sems, src_thru, land_thru, token):
        x, y, c = _place()
        for j, (px, py) in enumerate(_other_chips(x, y)):
            pltpu.make_async_remote_copy(
                src_ref=src.at[2 * px + py], dst_ref=land.at[2 * x + y], send_sem=send_sems.at[j], recv_sem=recv_sems.at[j],
                device_id=(px, py, c), device_id_type=MESH).start()
        token[...] = jnp.zeros_like(token)

    sems = pltpu.SemaphoreType.DMA((3,))
    return _pcall(
        body, in_specs=[IN_HBM, IN_HBM, HBM], out_specs=[SEM, SEM, IN_HBM, IN_HBM, TOKEN_SPEC],
        out_shape=[sems, sems, _hbm_like(pair), _hbm_like(pair), TOKEN], input_output_aliases={0: 2, 1: 3},
        compiler_params=pltpu.CompilerParams(has_side_effects=EFFECT), name=name,
    )(_in_hbm(pair), _landing(pair), after)


def _chip_wait(pair, parts, send_sems, recv_sems, after, name):
    def body(src, land, send, recv, after_ref, src_thru, land_thru):
        x, y, c = _place()
        for j, (px, py) in enumerate(_other_chips(x, y)):
            copy = pltpu.make_async_remote_copy(
                src_ref=src.at[2 * px + py], dst_ref=land.at[2 * px + py], send_sem=send.at[j], recv_sem=recv.at[j],
                device_id=(px, py, c), device_id_type=MESH)
            copy.wait_send()
            copy.wait_recv()

    return _pcall(
        body, in_specs=[IN_HBM, IN_HBM, SEM, SEM, HBM], out_specs=[IN_HBM, IN_HBM],
        out_shape=[_hbm_like(pair), _hbm_like(parts)], input_output_aliases={0: 0, 1: 1},
        compiler_params=pltpu.CompilerParams(has_side_effects=EFFECT), name=name,
    )(pair, parts, send_sems, recv_sems, after)


def _pair_add(core, grad, recv, block, grad_spec, name):
    _, R, C = recv.shape
    tr = block

    def body(c_ref, g_ref, r_ref, o_ref):
        o_ref[...] = (g_ref[...].astype(F32) + r_ref[...].astype(F32)).astype(BF16)

    grid_spec = pltpu.PrefetchScalarGridSpec(
        num_scalar_prefetch=1, grid=(N_CHIP, R // tr),
        in_specs=[grad_spec, pl.BlockSpec((None, tr, C), lambda k, i, c: (k, i, 0))],
        out_specs=pl.BlockSpec((None, tr, C), lambda k, i, c: (k, i, 0)))
    return _pcall(body, grid_spec=grid_spec, out_shape=jax.ShapeDtypeStruct(recv.shape, BF16),
                  compiler_params=_params("parallel", "parallel"), name=name)(core, grad, recv)


def _small_copies(gath, send_sems, recv_sems):
    x, y, c = _place()
    peers = [(x, y, 1 - c)] + [(px, py, pc) for px, py in _other_chips(x, y) for pc in (c, 1 - c)]
    pairs = []
    for a, ref in enumerate(gath):
        mine = ref.at[4 * x + 2 * y + c]
        for k, (px, py, pc) in enumerate(peers):
            sems = dict(send_sem=send_sems.at[7 * a + k], recv_sem=recv_sems.at[7 * a + k], device_id=(px, py, pc),
                        device_id_type=MESH)
            pairs.append((pltpu.make_async_remote_copy(src_ref=mine, dst_ref=mine, **sems),
                          pltpu.make_async_remote_copy(src_ref=mine, dst_ref=ref.at[4 * px + 2 * py + pc], **sems)))
    return pairs


def _small_start(landing, after, name):
    na = len(landing)

    def body(*refs):
        for send, _ in _small_copies(refs[:na], refs[na + 1], refs[na + 2]):
            send.start()
        refs[-1][...] = jnp.zeros_like(refs[-1])

    sems = pltpu.SemaphoreType.DMA((7 * na,))
    outs = _pcall(
        body, in_specs=[IN_HBM] * na + [HBM], out_specs=[SEM, SEM] + [IN_HBM] * na + [TOKEN_SPEC],
        out_shape=[sems, sems] + [_hbm_like(s) for s in landing] + [TOKEN],
        input_output_aliases={a: 2 + a for a in range(na)},
        compiler_params=pltpu.CompilerParams(has_side_effects=EFFECT), name=name,
    )(*[_in_hbm(s) for s in landing], after)
    return outs[0], outs[1], outs[2:2 + na], outs[-1]


def _small_wait(gathered, send_sems, recv_sems, after, name):
    na = len(gathered)

    def body(*refs):
        for send, arrival in _small_copies(refs[:na], refs[na], refs[na + 1]):
            send.wait_send()
            arrival.wait_recv()

    return list(_pcall(
        body, in_specs=[IN_HBM] * na + [SEM, SEM, HBM], out_specs=[IN_HBM] * na,
        out_shape=[_hbm_like(g) for g in gathered], input_output_aliases={a: a for a in range(na)},
        compiler_params=pltpu.CompilerParams(has_side_effects=EFFECT), name=name,
    )(*gathered, send_sems, recv_sems, after))


def _small_finish(gathered, params, name):
    na, npar = len(gathered), len(params)

    def body(*refs):
        g_refs, wmv = refs[:na], refs[na:na + 3 * npar]
        o_sums, o_params = refs[na + 3 * npar:2 * na + 3 * npar], refs[2 * na + 3 * npar:]
        sums = []
        for a in range(na):
            acc = g_refs[a][0]
            for k in range(1, N_DEV):
                acc = acc + g_refs[a][k]
            o_sums[a][...] = acc
            sums.append(acc)
        for j, (a, row, _, _, _) in enumerate(params):
            g = sums[a][row:row + 1, :]
            d, mn, vn = _adam_math(wmv[3 * j][...], g, wmv[3 * j + 1][...], wmv[3 * j + 2][...])
            for out, val in zip(o_params[4 * j:4 * j + 4], (g, d, mn, vn)):
                out[...] = val

    vm = pl.BlockSpec(memory_space=pltpu.VMEM)
    flat = [t for p in params for t in p[2:]]
    out_shape = [jax.ShapeDtypeStruct(g.shape[1:], F32) for g in gathered]
    out_shape += [jax.ShapeDtypeStruct(p[2].shape, F32) for p in params for _ in range(4)]
    outs = _pcall(body, in_specs=[vm] * (na + 3 * npar), out_specs=[vm] * len(out_shape), out_shape=out_shape,
                  name=name)(*gathered, *flat)
    return outs[:na], [outs[na + 4 * j:na + 4 * j + 4] for j in range(npar)]


def _local_step(x, tgt, gains, weights):
    g_pre_mix, g_post_mix, g_pre_ffn, g_post_ffn, g_sb, g_dil = gains
    S, D = x.shape
    hs = g_sb.shape[1] // HEAD_DIM
    hd = g_dil.shape[1] // HEAD_DIM
    cos2, sin_signed = _rope_tables(S)

    h1 = _rms_fwd(x, g_pre_mix, "rms_in", after=weights.start())
    w_in_g = weights.w_in(h1)
    proj = _mm_nn(h1, w_in_g, BF16, "proj", tn=768)
    o_sb, ct_sb, mixed = _sb_fwd(proj, g_sb, hs, hs + hd, "sb_fwd", after=weights.relay_out(proj))
    o_dl, lse_dl, mixed = _dil_fwd(proj, cos2, sin_signed, g_dil, mixed, 3 * hs, hd, "dil_fwd", after=weights.after_sb(o_sb))
    w_out_g = weights.w_out(o_dl)
    mix = _mm_nn(mixed, w_out_g, F32, "mix_out", tn=1024)
    x2, h2 = _mid_fwd(x, mix, g_post_mix, g_pre_ffn, "mid_fwd", after=weights.after_mix(mix))
    w_up_g, cwb = weights.w_up(h2)
    u = _mm_nn(h2, w_up_g, BF16, "ffn_up", b_transposed=True)
    y = _geglu_fwd(u, cwb, "geglu_fwd", after=weights.forward_down(u))
    w_down_g = weights.w_down(y)
    f = _mm_nn(y, w_down_g, F32, "ffn_down", tn=1024, tk=2816)

    dy, df, dg_post_ffn, loss = _loss_bwd(x2, f, tgt, g_post_ffn, "loss_bwd")
    dyv = _mm_nt(df, w_down_g, BF16, "d_y", tn=1408)
    dw_down = _mm_tn(y, df, D, BF16, "dw_down", tm=1408, tn=1024)
    dc, dcw_g, dcw_v = _geglu_bwd(u, dyv, cwb, "geglu_bwd", after=weights.grad("w_down", dw_down))
    du = _conv_bwd(dc, cwb, "conv_bwd", after=weights.grad_reduce("w_down", dc))
    dh2 = _mm_nt(du, w_up_g, BF16, "d_h2", tk=1408, b_transposed=True, per_step=2)
    dw_up = _mm_tn(du, h2, D, BF16, "dw_up", tm=1408, tn=1024)
    dx2, dmix, dg_pre_ffn, dg_post_mix = _mid_bwd(
        dy, dh2, x2, mix, g_pre_ffn, g_post_mix, "mid_bwd", after=weights.grad("w_up", dw_up))
    dmixed = _mm_nt(dmix, w_out_g, BF16, "d_mixed", after=weights.grad_reduce("w_up", dmix))
    dw_out = _mm_tn(mixed, dmix, D, BF16, "dw_out", tn=1024)
    dproj, dg_sb = _sb_bwd(proj, g_sb, o_sb, ct_sb, dmixed, 0, hs, "sb_bwd", after=weights.grad("w_out", dw_out))
    dproj, dg_dil = _dil_bwd(proj, cos2, sin_signed, g_dil, o_dl, lse_dl, dmixed, dproj, hs, 3 * hs, hd, "dil_bwd",
                             after=weights.grad_reduce("w_out", dg_sb))
    dw_in = _mm_tn(h1, dproj, w_in_g.shape[2], BF16, "dw_in", tn=768)
    weights.grad("w_in", dw_in)
    dep = weights.grad_reduce("w_in", dproj)
    dh1 = _mm_nt(dproj, w_in_g, BF16, "d_h1", tk=768, after=dep, per_step=4)
    grad_x, dg_pre_mix = _first_bwd(dx2, dh1, x, g_pre_mix, "first_bwd")
    small = (dg_pre_mix, dg_post_mix, dg_pre_ffn, dg_post_ffn, dg_sb[0:1], dg_dil[0:1], jnp.concatenate([dcw_g, dcw_v], axis=1))
    weights.small(small, loss)
    return loss, grad_x, small


def _pad_cols(a, to):
    return jnp.pad(a, ((0, 0), (0, to - a.shape[1])))


def kernel(x, pre_mix_gain, post_mix_gain, pre_ffn_gain, post_ffn_gain, w_in, sb_out_gain, dil_out_gain, w_out, w_up, conv_w, conv_b, w_down, loss_target, m_pre_mix_gain, m_post_mix_gain, m_pre_ffn_gain, m_post_ffn_gain, m_w_in, m_sb_out_gain, m_dil_out_gain, m_w_out, m_w_up, m_conv_w, m_conv_b, m_w_down, v_pre_mix_gain, v_post_mix_gain, v_pre_ffn_gain, v_post_ffn_gain, v_w_in, v_sb_out_gain, v_dil_out_gain, v_w_out, v_w_up, v_conv_w, v_conv_b, v_w_down):
    xb, tb = x[0], loss_target[0]
    S, D = xb.shape
    w_in, w_out, w_up, w_down, conv_w = w_in[0], w_out[0], w_up[0], w_down[0], conv_w[0]
    n_in, e_rows = w_in.shape[1], w_out.shape[0]
    cu, half = w_up.shape[1], w_down.shape[0]
    assert cu == 2 * half and half % 16 == 0
    cup = -(-cu // LANES) * LANES
    fp = N_CHIP * cup
    px, py, pc = _place()
    me = 4 * px + 2 * py + pc
    core = jnp.reshape(pc, (1,)).astype(jnp.int32)

    w_up_t, m_up_t, v_up_t = (jnp.swapaxes(t, 0, 1) for t in (w_up, m_w_up[0], v_w_up[0]))

    def by_dev(ref, qx, qy, qc):
        return ref.at[4 * qx + 2 * qy + qc]

    def down_slot(ref, qx, qy, qc):
        return ref.at[2 * qx + qy, pl.ds(qc * half, half)]

    def by_pair(ref, chip, k):
        return ref.at[chip, k]

    def down_pair(ref, chip, k):
        return ref.at[chip, pl.ds(k * half, half)]

    def pair_spec(tr, cols):
        return pl.BlockSpec((None, None, tr, cols), lambda k, i, c: (k, c[0], i, 0))

    tr_in, tr_up = _tile(D, 512, 16), _tile(cup, 256, 16)
    grad_plan = {
        "w_in": ((N_CHIP, 2, D, n_in), by_pair, (D, n_in), tr_in, pair_spec(tr_in, n_in)),
        "w_out": ((N_CHIP, 2, e_rows, D), by_pair, (e_rows, D), e_rows, pair_spec(e_rows, D)),
        "w_up": ((N_CHIP, 2, cup, D), by_pair, (cup, D), tr_up, pair_spec(tr_up, D)),
        "w_down": ((N_CHIP, cup, D), down_pair, (half, D), half,
                   pl.BlockSpec((None, half, D), lambda k, i, c: (k, c[0], 0))),
    }

    class Exchanges:
        def __init__(self):
            self.in_flight = {}

        def start(self):
            def own_slot(shard):
                return lax.dynamic_update_index_in_dim(lax.empty((N_DEV, *shard.shape), shard.dtype), shard, me, 0)

            self.group_slots = {"in": [by_dev], "out": [by_dev], "up": [by_dev, by_dev], "down": [down_slot]}
            self.flight = {}
            sems, gath, token = _gather_start([own_slot(w_in.astype(BF16))], [by_dev], core, "gather_in_start")
            self.flight["in"] = (sems, gath)
            zero = token[0, 0]
            self.landing = {
                "out": [own_slot((w_out + zero).astype(BF16))],
                "up": [own_slot(jnp.pad(w_up_t + zero, ((0, cup - cu), (0, 0))).astype(BF16)),
                       own_slot(jnp.pad(conv_w + zero, ((0, 8 - conv_w.shape[0]), (0, cup - cu))))],
                "down": [lax.dynamic_update_slice(jnp.zeros((N_CHIP, cup, D), BF16), (w_down + zero).astype(BF16)[None],
                                                  (2 * px + py, pc * half, 0))]}
            return token

        def begin(self, group, after):
            sems, gath, token = _gather_start(self.landing[group], self.group_slots[group], after, "gather_%s_start" % group)
            self.flight[group] = (sems, gath)
            return token

        def relay(self, group, after):
            sems, gath = self.flight[group]
            sems, gath, token = _gather_relay(gath, sems, self.group_slots[group], after, "gather_%s_relay" % group)
            self.flight[group] = (sems, gath)
            return token

        def pass_on(self, group, after):
            sems, gath = self.flight[group]
            diag_sems, gath, token = _gather_pass(gath, sems[:2], self.group_slots[group], after, "gather_%s_pass" % group)
            self.flight[group] = (sems[2:], diag_sems, gath)
            return token

        def finish(self, group, after):
            pass_sems, diag_sems, gath = self.flight[group]
            return _gather_finish(gath, pass_sems, diag_sems, self.group_slots[group], after, "gather_%s_finish" % group)

        def w_in(self, after):
            token = self.begin("up", self.begin("out", self.relay("in", after)))
            return self.finish("in", self.pass_on("in", token))[0]

        def relay_out(self, after):
            return self.relay("out", after)

        def after_sb(self, after):
            return self.begin("down", self.relay("up", self.pass_on("out", after)))

        def w_out(self, after):
            return self.finish("out", after)[0].reshape(1, N_DEV * e_rows, D)

        def after_mix(self, after):
            return self.pass_on("up", after)

        def w_up(self, after):
            w_up_g, cw_g = self.finish("up", after)
            cb = _pad_cols(conv_b.reshape(N_DEV, cu), cup).reshape(1, 2 * fp)
            cw_full = jnp.transpose(cw_g[:, :3, :], (1, 0, 2)).reshape(3, 2 * fp)
            cwb = jnp.concatenate([cw_full, cb, jnp.zeros((4, 2 * fp), F32)], axis=0)
            return w_up_g, cwb

        def forward_down(self, after):
            return self.relay("down", after)

        def w_down(self, after):
            return self.finish("down", self.pass_on("down", after))[0].reshape(1, fp, D)

        def small(self, small, loss):
            d_pre_mix, d_post_mix, d_pre_ffn, d_post_ffn, d_sb, d_dil, d_conv = small

            def rows_of(*vectors):
                n = vectors[0].shape[1]
                row = lax.broadcasted_iota(jnp.int32, (8, n), 0)
                out = jnp.zeros((8, n), F32)
                for k, vec in enumerate(vectors):
                    out = jnp.where(row == k, vec, out)
                return out

            parts = [rows_of(d_pre_mix, d_post_mix, d_pre_ffn, d_post_ffn, jnp.broadcast_to(loss[:, :1], (1, D))),
                     rows_of(d_sb, d_dil), d_conv]
            landing = [lax.dynamic_update_index_in_dim(lax.empty((N_DEV, *p.shape), F32), p, me, 0) for p in parts]
            self.small_flight = _small_start(landing, parts[0], "small_start")

        def small_sums(self, after):
            send, recv, gath, _ = self.small_flight
            gath = _small_wait(gath, send, recv, after, "small_wait")
            params = [(0, 0, pre_mix_gain, m_pre_mix_gain, v_pre_mix_gain), (0, 1, post_mix_gain, m_post_mix_gain, v_post_mix_gain),
                      (0, 2, pre_ffn_gain, m_pre_ffn_gain, v_pre_ffn_gain), (0, 3, post_ffn_gain, m_post_ffn_gain, v_post_ffn_gain),
                      (1, 0, sb_out_gain, m_sb_out_gain, v_sb_out_gain), (1, 1, dil_out_gain, m_dil_out_gain, v_dil_out_gain)]
            (gains_sum, _, conv_sum), gain_steps = _small_finish(gath, params, "small_finish")
            return gains_sum[4, 0], conv_sum, gain_steps

        def grad(self, name, dw):
            view_shape, view, block, tr, spec = grad_plan[name]
            send, recv_sems, dw, recv, token = _pair_start(dw.reshape(view_shape), view, block, core, "pair_start_" + name)
            self.in_flight[name] = (dw, recv, send, recv_sems)
            return token

        def grad_reduce(self, name, after):
            _, view, _, tr, spec = grad_plan[name]
            dw, recv = _pair_wait(*self.in_flight[name], view, after, "pair_wait_" + name)
            pair = _pair_add(core, dw, recv, tr, spec, "pair_add_" + name)
            send, recv_sems, pair, parts, token = _chip_start(pair, recv, "chip_start_" + name)
            self.in_flight[name] = (pair, parts, send, recv_sems)
            self.last_token = token
            return token

        def grad_parts(self, name, after):
            return _chip_wait(*self.in_flight[name], after, "chip_wait_" + name)

    exchanges = Exchanges()
    gains = (pre_mix_gain, post_mix_gain, pre_ffn_gain, post_ffn_gain, sb_out_gain, dil_out_gain)
    loss, grad_x, small = _local_step(xb, tb, gains, exchanges)


    chip_ids = jnp.stack([2 * px + py, 2 * (1 - px) + py, 2 * px + 1 - py, 2 * (1 - px) + 1 - py]).astype(jnp.int32)
    out_w_down = _adamw_chips(w_down, *exchanges.grad_parts("w_down", exchanges.small_flight[3]), chip_ids, m_w_down[0], v_w_down[0], "adam_w_down")
    out_up_t = _adamw_chips(w_up_t, *exchanges.grad_parts("w_up", out_w_down[1]), chip_ids, m_up_t, v_up_t, "adam_w_up")
    out_w_up = [jnp.swapaxes(o, 0, 1) for o in out_up_t]
    out_w_out = _adamw_chips(w_out, *exchanges.grad_parts("w_out", out_up_t[1]), chip_ids, m_w_out[0], v_w_out[0], "adam_w_out")
    loss_out, g_conv, gain_steps = exchanges.small_sums(out_w_out[1])
    out_pre_mix, out_post_mix, out_pre_ffn, out_post_ffn, out_sb, out_dil = gain_steps
    g_conv_b = g_conv[3].reshape(N_DEV, cup)[:, :cu].reshape(1, N_DEV * cu)
    g_conv_w = lax.dynamic_index_in_dim(g_conv[0:3].reshape(3, N_DEV, cup), me, axis=1, keepdims=False)[:, :cu]
    out_conv_b = _adamw(conv_b, g_conv_b[None], m_conv_b, v_conv_b, "adam_conv_b")
    out_conv_w = _adamw(conv_w, g_conv_w[None], m_conv_w[0], v_conv_w[0], "adam_conv_w")
    out_w_in = _adamw_chips(w_in, *exchanges.grad_parts("w_in", out_conv_w[1]), chip_ids, m_w_in[0], v_w_in[0], "adam_w_in")

    order = [out_pre_mix, out_post_mix, out_pre_ffn, out_post_ffn, [o[None] for o in out_w_in], out_sb, out_dil,
             [o[None] for o in out_w_out], [o[None] for o in out_w_up], [o[None] for o in out_conv_w], out_conv_b,
             [o[None] for o in out_w_down]]
    outs = [loss_out, grad_x[None]]
    for k in range(4):
        outs += [o[k] for o in order]
    return tuple(outs)
```

```python
import math

import jax
import jax.numpy as jnp
from jax import lax
from jax.experimental import pallas as pl
from jax.experimental.pallas import tpu as pltpu

F32 = jnp.float32
BF16 = jnp.bfloat16
HEAD_DIM = 128
LANES = 128
KEY_BLOCK = 128
DILATIONS = (1, 4, 16)
RMS_EPS = 1e-6
ROPE_THETA = 10000.0
NEG = -1e30
ADAM_LR, ADAM_B1, ADAM_B2, ADAM_EPS, ADAM_WD, ADAM_STEP = 0.001, 0.9, 0.999, 1e-08, 0.01, 10
MESH = pl.DeviceIdType.MESH
N_DEV = 8
N_CHIP = 4
HBM = pl.BlockSpec(memory_space=pl.ANY)
VMEM_LIMIT = 56 * 1024 * 1024

_pcall = pl.pallas_call


def _tile(n, pref, mult=LANES):
    best = None
    t = mult
    while t <= min(n, pref):
        if n % t == 0:
            best = t
        t += mult
    return n if best is None else best


def _params(*sem):
    return pltpu.CompilerParams(dimension_semantics=sem, vmem_limit_bytes=VMEM_LIMIT)


def _following(after, body, in_specs, args):
    if after is None:
        return body, list(in_specs), list(args)
    n = len(args)

    def ordered(*refs):
        body(*refs[:n], *refs[n + 1:])

    return ordered, list(in_specs) + [HBM], list(args) + [after]


def _dot(a, b, dims):
    return lax.dot_general(a, b, (dims, ((), ())), preferred_element_type=F32)


NN = ((1,), (0,))
NT = ((1,), (1,))
TN = ((0,), (0,))


def _mm_body(dims, nk, tile):
    if nk == 1:
        def single(a_ref, b_ref, o_ref):
            o_ref[...] = _dot(a_ref[...].astype(BF16), b_ref[...].astype(BF16), dims).astype(o_ref.dtype)

        return single, []

    def body(a_ref, b_ref, o_ref, acc_ref):
        k = pl.program_id(2)

        @pl.when(k == 0)
        def _():
            acc_ref[...] = jnp.zeros_like(acc_ref)

        acc_ref[...] += _dot(a_ref[...].astype(BF16), b_ref[...].astype(BF16), dims)

        @pl.when(k == nk - 1)
        def _():
            o_ref[...] = acc_ref[...].astype(o_ref.dtype)

    return body, [pltpu.VMEM(tile, F32)]


def _mm_nn(a, b3, out_dtype, name, tm=1024, tn=1408, tk=2048, b_transposed=False):
    M, K = a.shape
    C, n = b3.shape[0], b3.shape[1 if b_transposed else 2]
    tm, tk, tn = _tile(M, tm, 8), _tile(K, tk), _tile(n, tn)
    npc, nk = n // tn, K // tk
    body, scratch = _mm_body(NT if b_transposed else NN, nk, (tm, tn))
    b_spec = (pl.BlockSpec((None, tn, tk), lambda i, j, k: (j // npc, j % npc, k)) if b_transposed
              else pl.BlockSpec((None, tk, tn), lambda i, j, k: (j // npc, k, j % npc)))
    return _pcall(
        body, grid=(M // tm, C * npc, nk),
        in_specs=[pl.BlockSpec((tm, tk), lambda i, j, k: (i, k)), b_spec],
        out_specs=pl.BlockSpec((tm, tn), lambda i, j, k: (i, j)),
        out_shape=jax.ShapeDtypeStruct((M, C * n), out_dtype), scratch_shapes=scratch,
        compiler_params=_params("parallel", "parallel", "arbitrary"), name=name)(a, b3)


def _mm_nt(a, b3, out_dtype, name, tm=1024, tn=1024, tk=2048, after=None, b_transposed=False, per_step=1):
    M, _ = a.shape
    C, N, n = (b3.shape[0], b3.shape[2], b3.shape[1]) if b_transposed else b3.shape
    tm, tn, tk = _tile(M, tm, 8), _tile(N, tn), _tile(n, tk)
    dims = NN if b_transposed else NT
    extra = [] if after is None else [after]
    if per_step > 1 and tk == n and C % per_step == 0:
        nk, scratch = C // per_step, [pltpu.VMEM((tm, tn), F32)]
        b3 = b3.reshape(nk, per_step, *b3.shape[1:])
        a_spec = pl.BlockSpec((tm, per_step * n), lambda i, j, k: (i, k))
        if b_transposed:
            b_spec = pl.BlockSpec((None, per_step, n, tn), lambda i, j, k: (k, 0, 0, j))
        else:
            b_spec = pl.BlockSpec((None, per_step, tn, n), lambda i, j, k: (k, 0, j, 0))

        def body(a_ref, b_ref, *rest):
            o_ref, acc_ref = rest[len(extra):]
            k = pl.program_id(2)

            @pl.when(k == 0)
            def _():
                acc_ref[...] = jnp.zeros_like(acc_ref)

            b = b_ref[...].astype(BF16)
            b = b.reshape(per_step * n, tn) if b_transposed else jnp.concatenate([b[u] for u in range(per_step)], axis=1)
            acc_ref[...] += _dot(a_ref[...].astype(BF16), b, dims)

            @pl.when(k == nk - 1)
            def _():
                o_ref[...] = acc_ref[...].astype(o_ref.dtype)
    else:
        kpc = n // tk
        nk = C * kpc
        inner, scratch = _mm_body(dims, nk, (tm, tn))
        a_spec = pl.BlockSpec((tm, tk), lambda i, j, k: (i, k))
        b_spec = (pl.BlockSpec((None, tk, tn), lambda i, j, k: (k // kpc, k % kpc, j)) if b_transposed
                  else pl.BlockSpec((None, tn, tk), lambda i, j, k: (k // kpc, j, k % kpc)))

        def body(a_ref, b_ref, *rest):
            inner(a_ref, b_ref, *rest[len(extra):])

    return _pcall(
        body, grid=(M // tm, N // tn, nk), in_specs=[a_spec, b_spec] + [HBM] * len(extra),
        out_specs=pl.BlockSpec((tm, tn), lambda i, j, k: (i, j)),
        out_shape=jax.ShapeDtypeStruct((M, N), out_dtype), scratch_shapes=scratch,
        compiler_params=_params("parallel", "parallel", "arbitrary"), name=name)(a, b3, *extra)


def _mm_tn(x, y, n, out_dtype, name, tm=1024, tn=1408, tk=2048, after=None):
    S, P = x.shape
    C = y.shape[1] // n
    tm, tn, tk = _tile(P, tm), _tile(n, tn), _tile(S, tk, 8)
    npc, nk = n // tn, S // tk
    inner, scratch = _mm_body(TN, nk, (tm, tn))
    extra = [] if after is None else [after]

    def body(x_ref, y_ref, *rest):
        inner(x_ref, y_ref, *rest[len(extra):])

    return _pcall(
        body, grid=(P // tm, C * npc, nk),
        in_specs=[pl.BlockSpec((tk, tm), lambda i, j, k: (k, i)),
                  pl.BlockSpec((tk, tn), lambda i, j, k: (k, j))] + [HBM] * len(extra),
        out_specs=pl.BlockSpec((None, tm, tn), lambda i, j, k: (j // npc, i, j % npc)),
        out_shape=jax.ShapeDtypeStruct((C, P, n), out_dtype), scratch_shapes=scratch,
        compiler_params=_params("parallel", "parallel", "arbitrary"), name=name)(x, y, *extra)


def _rms_scale(v):
    return lax.rsqrt(jnp.mean(v * v, axis=-1, keepdims=True) + RMS_EPS)


def _rms_bwd(gy, v, r):
    return r * gy - v * (r * r * r * jnp.mean(gy * v, axis=-1, keepdims=True))


def _rows_spec(tm, d):
    return pl.BlockSpec((tm, d), lambda i: (i, 0))


def _vec_spec(d):
    return pl.BlockSpec((1, d), lambda i: (0, 0))


def _rms_fwd(x, g, name, tm=256, after=None):
    S, D = x.shape

    def body(x_ref, g_ref, h_ref):
        v = x_ref[...]
        h_ref[...] = (v * _rms_scale(v) * g_ref[...]).astype(BF16)

    body, in_specs, args = _following(after, body, [_rows_spec(tm, D), _vec_spec(D)], [x, g])
    return _pcall(body, grid=(S // tm,), in_specs=in_specs, out_specs=_rows_spec(tm, D),
                  out_shape=jax.ShapeDtypeStruct((S, D), BF16), compiler_params=_params("parallel"), name=name)(*args)


def _mid_fwd(x, mix, g_post, g_pre, name, tm=256, after=None):
    S, D = x.shape

    def body(x_ref, m_ref, gp_ref, gn_ref, x2_ref, h_ref):
        m = m_ref[...]
        x2 = x_ref[...] + m * _rms_scale(m) * gp_ref[...]
        x2_ref[...] = x2
        h_ref[...] = (x2 * _rms_scale(x2) * gn_ref[...]).astype(BF16)

    body, in_specs, args = _following(
        after, body, [_rows_spec(tm, D), _rows_spec(tm, D), _vec_spec(D), _vec_spec(D)], [x, mix, g_post, g_pre])
    return _pcall(body, grid=(S // tm,), in_specs=in_specs,
                  out_specs=[_rows_spec(tm, D), _rows_spec(tm, D)],
                  out_shape=[jax.ShapeDtypeStruct((S, D), F32), jax.ShapeDtypeStruct((S, D), BF16)],
                  compiler_params=_params("parallel"), name=name)(*args)


def _loss_bwd(x2, f, tgt, g_post, name, tm=256):
    S, D = x2.shape

    def body(x2_ref, f_ref, t_ref, g_ref, dy_ref, df_ref, dg_ref, ls_ref):
        i = pl.program_id(0)

        @pl.when(i == 0)
        def _():
            dg_ref[...] = jnp.zeros_like(dg_ref)
            ls_ref[...] = jnp.zeros_like(ls_ref)

        fv = f_ref[...]
        r = _rms_scale(fv)
        g = g_ref[...]
        err = x2_ref[...] + fv * r * g - t_ref[...]
        ls_ref[...] += jnp.broadcast_to(0.5 * jnp.sum(jnp.mean(err * err, axis=-1, keepdims=True), axis=0, keepdims=True), ls_ref.shape)
        dy = err * (1.0 / D)
        dy_ref[...] = dy
        df_ref[...] = _rms_bwd(dy * g, fv, r).astype(BF16)
        dg_ref[...] += jnp.sum(dy * fv * r, axis=0, keepdims=True)

    return _pcall(body, grid=(S // tm,),
                  in_specs=[_rows_spec(tm, D), _rows_spec(tm, D), _rows_spec(tm, D), _vec_spec(D)],
                  out_specs=[_rows_spec(tm, D), _rows_spec(tm, D), _vec_spec(D), _vec_spec(LANES)],
                  out_shape=[jax.ShapeDtypeStruct((S, D), F32), jax.ShapeDtypeStruct((S, D), BF16),
                             jax.ShapeDtypeStruct((1, D), F32), jax.ShapeDtypeStruct((1, LANES), F32)],
                  compiler_params=_params("arbitrary"), name=name)(x2, f, tgt, g_post)


def _mid_bwd(dy, dh2, x2, mix, g_pre, g_post, name, tm=256, after=None):
    S, D = dy.shape

    def body(dy_ref, dh_ref, x2_ref, m_ref, gn_ref, gp_ref, dx2_ref, dm_ref, dgn_ref, dgp_ref):
        i = pl.program_id(0)

        @pl.when(i == 0)
        def _():
            dgn_ref[...] = jnp.zeros_like(dgn_ref)
            dgp_ref[...] = jnp.zeros_like(dgp_ref)

        x2, dh = x2_ref[...], dh_ref[...].astype(F32)
        r = _rms_scale(x2)
        dx2 = dy_ref[...] + _rms_bwd(dh * gn_ref[...], x2, r)
        dgn_ref[...] += jnp.sum(dh * x2 * r, axis=0, keepdims=True)
        dx2_ref[...] = dx2
        m = m_ref[...]
        rm = _rms_scale(m)
        dm_ref[...] = _rms_bwd(dx2 * gp_ref[...], m, rm).astype(BF16)
        dgp_ref[...] += jnp.sum(dx2 * m * rm, axis=0, keepdims=True)

    body, in_specs, args = _following(
        after, body, [_rows_spec(tm, D)] * 4 + [_vec_spec(D)] * 2, [dy, dh2, x2, mix, g_pre, g_post])
    return _pcall(body, grid=(S // tm,), in_specs=in_specs,
                  out_specs=[_rows_spec(tm, D), _rows_spec(tm, D), _vec_spec(D), _vec_spec(D)],
                  out_shape=[jax.ShapeDtypeStruct((S, D), F32), jax.ShapeDtypeStruct((S, D), BF16),
                             jax.ShapeDtypeStruct((1, D), F32), jax.ShapeDtypeStruct((1, D), F32)],
                  compiler_params=_params("arbitrary"), name=name)(*args)


def _first_bwd(dx2, dh1, x, g_pre, name, tm=256):
    S, D = x.shape

    def body(dx2_ref, dh_ref, x_ref, g_ref, gx_ref, dg_ref):
        i = pl.program_id(0)

        @pl.when(i == 0)
        def _():
            dg_ref[...] = jnp.zeros_like(dg_ref)

        xv, dh = x_ref[...], dh_ref[...].astype(F32)
        r = _rms_scale(xv)
        gx_ref[...] = dx2_ref[...] + _rms_bwd(dh * g_ref[...], xv, r)
        dg_ref[...] += jnp.sum(dh * xv * r, axis=0, keepdims=True)

    return _pcall(body, grid=(S // tm,), in_specs=[_rows_spec(tm, D)] * 3 + [_vec_spec(D)],
                  out_specs=[_rows_spec(tm, D), _vec_spec(D)],
                  out_shape=[jax.ShapeDtypeStruct((S, D), F32), jax.ShapeDtypeStruct((1, D), F32)],
                  compiler_params=_params("arbitrary"), name=name)(dx2, dh1, x, g_pre)


def _logsig_pair(z):
    lb = jnp.minimum(z, 0.0) - jnp.log(1.0 + jnp.exp(-jnp.abs(z)))
    return lb, lb - z


SB_KEY_BLOCK = 256


def _sum_matrix(strict):
    ia = lax.broadcasted_iota(jnp.int32, (SB_KEY_BLOCK, SB_KEY_BLOCK), 0)
    ib = lax.broadcasted_iota(jnp.int32, (SB_KEY_BLOCK, SB_KEY_BLOCK), 1)
    return ((ia > ib) if strict == ">" else (ia < ib)).astype(BF16)


def _row_total(sums, v, col):
    return jnp.broadcast_to(sums[:, col:col + 1] + v[:, col:col + 1], (v.shape[0], LANES))


def _lanes(c, width):
    return jnp.tile(c, (1, width // LANES))


def _split_dot(v, u):
    hi = v.astype(BF16)
    lo = (v - hi.astype(F32)).astype(BF16)
    return _dot(hi, u, NN) + _dot(lo, u, NN)


def _head_out(o, g):
    return o * _rms_scale(o) * g


def _sb_fwd(proj, gain, n_heads, mixed_heads, name, tq=1024, after=None):
    S = proj.shape[0]
    H, tk = n_heads, SB_KEY_BLOCK
    tq = _tile(S, tq, 2 * tk)
    scale = HEAD_DIM ** -0.5

    def body(q_ref, k_ref, v_ref, g_ref, o_ref, ct_ref, mx_ref, oacc, cacc):
        i = pl.program_id(1)
        oacc[...] = jnp.zeros_like(oacc)
        cacc[...] = jnp.zeros_like(cacc)
        sums = _sum_matrix(">")

        def run(blocks):
            scored = []
            for k0, r0, diagonal in blocks:
                rows = pl.ds(r0, tq - r0)
                lb, lk = _logsig_pair(_dot(q_ref[rows, :].astype(BF16), k_ref[pl.ds(k0, tk), :].astype(BF16), NT) * scale)
                causal = None
                if diagonal:
                    causal = (lax.broadcasted_iota(jnp.int32, (tq - r0, tk), 1)
                              < lax.broadcasted_iota(jnp.int32, (tq - r0, tk), 0))
                    lk = jnp.where(causal, lk, 0.0)
                scored.append((k0, rows, causal, lb, lk))
            summed = [(k0, rows, causal, lb, lk, _split_dot(lk, sums)) for k0, rows, causal, lb, lk in scored]
            weights = []
            for k0, rows, causal, lb, lk, after in summed:
                c = cacc[rows, :]
                a = jnp.exp(lb + after + _lanes(c, tk))
                if causal is not None:
                    a = jnp.where(causal, a, 0.0)
                cacc[rows, :] = c + _row_total(after, lk, 0)
                weights.append((k0, rows, a.astype(BF16)))
            for k0, rows, a in weights:
                oacc[rows, :] += _dot(a, v_ref[pl.ds(k0, tk), :].astype(BF16), NN)

        for d in reversed(range(0, tq // tk, 2)):
            run([(pl.multiple_of(i * tq + e * tk, tk), e * tk, True) for e in (d + 1, d)])
        per_trip = tq // tk

        def step(it, carry):
            k0 = pl.multiple_of((i - 1 - it) * tq, tq)
            run([(pl.multiple_of(k0 + e * tk, tk), 0, False) for e in reversed(range(per_trip))])
            return carry

        lax.fori_loop(0, i, step, 0)
        o = oacc[...]
        o_ref[...] = o
        ct_ref[...] = cacc[...]
        mx_ref[...] = _head_out(o, g_ref[...]).astype(BF16)

    blk = pl.BlockSpec((tq, HEAD_DIM), lambda h, i: (i, h))
    body, in_specs, args = _following(
        after, body,
        [blk, pl.BlockSpec((S, HEAD_DIM), lambda h, i: (0, H + h)),
         pl.BlockSpec((S, HEAD_DIM), lambda h, i: (0, 2 * H + h)), pl.BlockSpec((1, HEAD_DIM), lambda h, i: (0, h))],
        [proj, proj, proj, gain])
    return _pcall(
        body, grid=(H, S // tq), in_specs=in_specs,
        out_specs=[blk, blk, blk],
        out_shape=[jax.ShapeDtypeStruct((S, H * HEAD_DIM), F32), jax.ShapeDtypeStruct((S, H * HEAD_DIM), F32),
                   jax.ShapeDtypeStruct((S, mixed_heads * HEAD_DIM), BF16)],
        scratch_shapes=[pltpu.VMEM((tq, HEAD_DIM), F32), pltpu.VMEM((tq, LANES), F32)],
        compiler_params=_params("parallel", "arbitrary"), name=name)(*args)


def _sb_bwd(proj, gain, o_raw, ctot, dmixed, dm_col0, n_heads, name, tq=1024, after=None):
    S = proj.shape[0]
    H, tk = n_heads, SB_KEY_BLOCK
    tq = _tile(S, tq, 2 * tk)
    nq = S // tq
    scale = HEAD_DIM ** -0.5

    def body(q_ref, k_ref, v_ref, g_ref, o_ref, ct_ref, dm_ref, dproj_ref, dg_ref,
             dkacc, dvacc, dqacc, pfx, gcar, dos, stage_q, stage_k, stage_v, out_sems):
        h, i = pl.program_id(0), pl.program_id(1)

        @pl.when(i == 0)
        def _():
            dkacc[...] = jnp.zeros_like(dkacc)
            dvacc[...] = jnp.zeros_like(dvacc)
            dg_ref[...] = jnp.zeros_like(dg_ref)

        o, dm, g = o_ref[...], dm_ref[...].astype(F32), g_ref[...]
        r = _rms_scale(o)
        dos[...] = _rms_bwd(dm * g, o, r).astype(BF16)
        dg_ref[...] += jnp.broadcast_to(jnp.sum(dm * o * r, axis=0, keepdims=True), dg_ref.shape)
        dqacc[...] = jnp.zeros_like(dqacc)
        pfx[...] = jnp.zeros_like(pfx)
        gcar[...] = jnp.zeros_like(gcar)
        later, earlier = _sum_matrix(">"), _sum_matrix("<")

        def run(blocks):
            scored = []
            for k0, r0, diagonal in blocks:
                rows, keys = pl.ds(r0, tq - r0), pl.ds(k0, tk)
                lb, lk = _logsig_pair(_dot(q_ref[rows, :].astype(BF16), k_ref[keys, :].astype(BF16), NT) * scale)
                da = _dot(dos[rows, :], v_ref[keys, :].astype(BF16), NT)
                causal = None
                if diagonal:
                    causal = (lax.broadcasted_iota(jnp.int32, (tq - r0, tk), 1)
                              < lax.broadcasted_iota(jnp.int32, (tq - r0, tk), 0))
                    lk = jnp.where(causal, lk, 0.0)
                scored.append((rows, keys, causal, lb, lk, da))
            summed = [(*blk, _split_dot(blk[4], later)) for blk in scored]
            weighted = []
            for rows, keys, causal, lb, lk, da, after in summed:
                p = pfx[rows, :] + _row_total(after, lk, 0)
                pfx[rows, :] = p
                a = jnp.exp(lb + after + _lanes(ct_ref[rows, :] - p, tk))
                if causal is not None:
                    a = jnp.where(causal, a, 0.0)
                dl = da * a
                weighted.append((rows, keys, causal, lb, a.astype(BF16), dl, _dot(dl.astype(BF16), earlier, NN)))
            cotangents = []
            for rows, keys, causal, lb, a, dl, before in weighted:
                gc = gcar[rows, :]
                gcar[rows, :] = gc + _row_total(before, dl, tk - 1)
                sig = jnp.exp(lb)
                gsum = (before + _lanes(gc, tk)) * sig
                if causal is not None:
                    gsum = jnp.where(causal, gsum, 0.0)
                cotangents.append((rows, keys, a, ((dl * (1.0 - sig) - gsum) * scale).astype(BF16)))
            for rows, keys, a, dz in cotangents:
                q, do = q_ref[rows, :].astype(BF16), dos[rows, :]
                dvacc[keys, :] += _dot(a, do, TN)
                dqacc[rows, :] += _dot(dz, k_ref[keys, :].astype(BF16), NN)
                dkacc[keys, :] += _dot(dz, q, TN)

        per_trip = tq // tk

        def step(j, carry):
            k0 = pl.multiple_of(j * tq, tq)
            run([(pl.multiple_of(k0 + e * tk, tk), 0, False) for e in range(per_trip)])
            return carry

        lax.fori_loop(0, i, step, 0)
        for d in range(0, tq // tk, 2):
            run([(pl.multiple_of(i * tq + e * tk, tk), e * tk, True) for e in (d, d + 1)])
        def columns(block):
            return pl.ds(pl.multiple_of(block * HEAD_DIM, HEAD_DIM), HEAD_DIM)

        dq_out = pltpu.make_async_copy(stage_q, dproj_ref.at[pl.ds(pl.multiple_of(i * tq, tq), tq), columns(h)], out_sems.at[0])
        dkv_out = [pltpu.make_async_copy(stage_k, dproj_ref.at[:, columns(H + h)], out_sems.at[1]),
                   pltpu.make_async_copy(stage_v, dproj_ref.at[:, columns(2 * H + h)], out_sems.at[2])]

        @pl.when((h > 0) | (i > 0))
        def _():
            dq_out.wait()

        stage_q[...] = dqacc[...].astype(BF16)
        dq_out.start()

        @pl.when(i == nq - 1)
        def _():
            @pl.when(h > 0)
            def _():
                for cp in dkv_out:
                    cp.wait()

            stage_k[...] = dkacc[...].astype(BF16)
            stage_v[...] = dvacc[...].astype(BF16)
            for cp in dkv_out:
                cp.start()

        @pl.when((h == H - 1) & (i == nq - 1))
        def _():
            dq_out.wait()
            for cp in dkv_out:
                cp.wait()

    blk = pl.BlockSpec((tq, HEAD_DIM), lambda h, i: (i, h))
    W = H * HEAD_DIM
    body, in_specs, args = _following(
        after, body,
        [blk, pl.BlockSpec((S, HEAD_DIM), lambda h, i: (0, H + h)),
         pl.BlockSpec((S, HEAD_DIM), lambda h, i: (0, 2 * H + h)), pl.BlockSpec((1, HEAD_DIM), lambda h, i: (0, h)),
         blk, blk, pl.BlockSpec((tq, HEAD_DIM), lambda h, i: (i, dm_col0 + h))],
        [proj, proj, proj, gain, o_raw, ctot, dmixed])
    return _pcall(
        body, grid=(H, nq), in_specs=in_specs,
        out_specs=[HBM, pl.BlockSpec((8, HEAD_DIM), lambda h, i: (0, h))],
        out_shape=[jax.ShapeDtypeStruct(proj.shape, BF16), jax.ShapeDtypeStruct((8, W), F32)],
        scratch_shapes=[pltpu.VMEM((S, HEAD_DIM), F32), pltpu.VMEM((S, HEAD_DIM), F32), pltpu.VMEM((tq, HEAD_DIM), F32),
                        pltpu.VMEM((tq, LANES), F32), pltpu.VMEM((tq, LANES), F32), pltpu.VMEM((tq, HEAD_DIM), BF16),
                        pltpu.VMEM((tq, HEAD_DIM), BF16), pltpu.VMEM((S, HEAD_DIM), BF16), pltpu.VMEM((S, HEAD_DIM), BF16),
                        pltpu.SemaphoreType.DMA((3,))],
        compiler_params=_params("arbitrary", "arbitrary"), name=name)(*args)


def _rope_tables(S):
    inv_freq = ROPE_THETA ** (-jnp.arange(0, HEAD_DIM, 2, dtype=F32) / HEAD_DIM)
    ang = jnp.arange(S, dtype=F32)[:, None] * inv_freq[None, :]
    cos, sin = jnp.cos(ang), jnp.sin(ang)
    return jnp.concatenate([cos, cos], axis=1), jnp.concatenate([-sin, sin], axis=1)


def _rope(v, cos2, sin_signed):
    return v * cos2 + pltpu.roll(v, HEAD_DIM // 2, axis=1) * sin_signed


def _dil_rows(d, r, l0, n):
    if d == 1:
        return pl.ds(l0 if isinstance(l0, int) else pl.multiple_of(l0, KEY_BLOCK), n)
    return pl.ds(r + d * l0, n, stride=d)


def _dil_blocks(S, visit):
    B = KEY_BLOCK
    group = 16
    for b, d in enumerate(DILATIONS):
        nb = S // d // B
        if nb == 1:
            g = math.gcd(d, group)

            def trip(t, carry, b=b, d=d, g=g):
                visit([(b, d, t * g + u, 0, True) for u in range(g)])
                return carry

            lax.fori_loop(0, d // g, trip, 0)
        elif d == 1:
            visit([(b, d, 0, 0, True)])
            g = max(k for k in range(1, group + 2) if (nb - 1) % k == 0)

            def trip(t, carry, b=b, d=d, g=g):
                visit([(b, d, 0, (1 + t * g + u) * B, False) for u in range(g)])
                return carry

            lax.fori_loop(0, (nb - 1) // g, trip, 0)
        else:
            g = math.gcd(d, max(group // nb, 1))

            def trip(t, carry, b=b, d=d, nb=nb, g=g):
                visit([(b, d, t * g + u, n * B, n == 0) for u in range(g) for n in range(nb)])
                return carry

            lax.fori_loop(0, d // g, trip, 0)


def _dil_mask(first):
    B = KEY_BLOCK
    nk = B if first else 2 * B
    iq = lax.broadcasted_iota(jnp.int32, (B, nk), 0)
    ik = lax.broadcasted_iota(jnp.int32, (B, nk), 1)
    return (ik <= iq) if first else ((ik >= iq) & (ik <= iq + B))


def _dil_fwd(proj, cos2, sin_signed, gain, mixed, col0, n_heads, name, after=None):
    S = proj.shape[0]
    H, B = n_heads, KEY_BLOCK
    scale = HEAD_DIM ** -0.5
    rc = _tile(S, 256, 8)

    def body(q_ref, k_ref, v_ref, c_ref, s_ref, g_ref, mixed_in, o_ref, l_ref, mx_ref, qr, kr, vf, *per_branch):
        ob, lb = per_branch[:len(DILATIONS)], per_branch[len(DILATIONS):]

        def rope_rows(t, carry):
            rows = pl.ds(pl.multiple_of(t * rc, rc), rc)
            qr[rows, :] = _rope(q_ref[rows, :].astype(F32), c_ref[rows, :], s_ref[rows, :])
            kr[rows, :] = _rope(k_ref[rows, :].astype(F32), c_ref[rows, :], s_ref[rows, :])
            vf[rows, :] = v_ref[rows, :].astype(F32)
            return carry

        lax.fori_loop(0, S // rc, rope_rows, 0)

        def visit(blocks):
            scores = []
            for b, d, r, l0, first in blocks:
                qrows = _dil_rows(d, r, l0, B)
                krows = qrows if first else _dil_rows(d, r, l0 - B, 2 * B)
                s = _dot(qr[qrows, :].astype(BF16), kr[krows, :].astype(BF16), NT) * scale
                scores.append((b, qrows, krows, jnp.where(_dil_mask(first), s, NEG)))
            weights = []
            for b, qrows, krows, s in scores:
                m = jnp.max(s, axis=1, keepdims=True)
                p = jnp.exp(s - m)
                den = jnp.sum(p, axis=1, keepdims=True)
                lb[b][qrows, :] = jnp.broadcast_to(m + jnp.log(den), (B, LANES))
                weights.append((b, qrows, krows, p.astype(BF16), den))
            for b, qrows, krows, p, den in weights:
                ob[b][qrows, :] = _dot(p, vf[krows, :].astype(BF16), NN) / den

        _dil_blocks(S, visit)

        def combine(t, carry):
            rows = pl.ds(pl.multiple_of(t * rc, rc), rc)
            l0, l1, l2 = lb[0][rows, :], lb[1][rows, :], lb[2][rows, :]
            m = jnp.maximum(jnp.maximum(l0, l1), l2)
            w0, w1, w2 = jnp.exp(l0 - m), jnp.exp(l1 - m), jnp.exp(l2 - m)
            den = w0 + w1 + w2
            o = (w0 * ob[0][rows, :] + w1 * ob[1][rows, :] + w2 * ob[2][rows, :]) / den
            o_ref[rows, :] = o
            l_ref[rows, :] = m + jnp.log(den)
            mx_ref[rows, :] = _head_out(o, g_ref[...]).astype(BF16)
            return carry

        lax.fori_loop(0, S // rc, combine, 0)

    def col(k):
        return pl.BlockSpec((S, HEAD_DIM), lambda h: (0, col0 + k * H + h))

    tab = pl.BlockSpec((S, HEAD_DIM), lambda h: (0, 0))
    out = pl.BlockSpec((S, HEAD_DIM), lambda h: (0, h))
    W = H * HEAD_DIM
    first = mixed.shape[1] // HEAD_DIM - H
    body, in_specs, args = _following(
        after, body, [col(0), col(1), col(2), tab, tab, pl.BlockSpec((1, HEAD_DIM), lambda h: (0, h)), HBM],
        [proj, proj, proj, cos2, sin_signed, gain, mixed])
    return _pcall(
        body, grid=(H,), in_specs=in_specs,
        out_specs=[out, out, pl.BlockSpec((S, HEAD_DIM), lambda h: (0, first + h))],
        out_shape=[jax.ShapeDtypeStruct((S, W), F32), jax.ShapeDtypeStruct((S, W), F32),
                   jax.ShapeDtypeStruct(mixed.shape, BF16)],
        input_output_aliases={6: 2},
        scratch_shapes=[pltpu.VMEM((S, HEAD_DIM), F32)] * (3 + 2 * len(DILATIONS)),
        compiler_params=_params("parallel"), name=name)(*args)


def _dil_bwd(proj, cos2, sin_signed, gain, o_raw, lse, dmixed, dproj, dm_col0, col0, n_heads, name, after=None):
    S = proj.shape[0]
    H, B = n_heads, KEY_BLOCK
    scale = HEAD_DIM ** -0.5
    rc = _tile(S, 256, 8)

    def body(q_ref, k_ref, v_ref, c_ref, s_ref, g_ref, o_ref, l_ref, dm_ref, dproj_in, dproj_ref, dg_ref,
             qr, kr, vf, dos, dsum, dqr, dkr, dvv, stage_q, stage_k, stage_v, out_sems):
        dg_ref[...] = jnp.zeros_like(dg_ref)

        def prep(t, carry):
            rows = pl.ds(pl.multiple_of(t * rc, rc), rc)
            qr[rows, :] = _rope(q_ref[rows, :].astype(F32), c_ref[rows, :], s_ref[rows, :])
            kr[rows, :] = _rope(k_ref[rows, :].astype(F32), c_ref[rows, :], s_ref[rows, :])
            vf[rows, :] = v_ref[rows, :].astype(F32)
            o, dm = o_ref[rows, :], dm_ref[rows, :].astype(F32)
            r = _rms_scale(o)
            do = _rms_bwd(dm * g_ref[...], o, r)
            dg_ref[...] += jnp.broadcast_to(jnp.sum(dm * o * r, axis=0, keepdims=True), dg_ref.shape)
            dos[rows, :] = do
            dsum[rows, :] = jnp.broadcast_to(jnp.sum(do * o, axis=1, keepdims=True), (rc, LANES))
            dqr[rows, :] = jnp.zeros((rc, HEAD_DIM), F32)
            dkr[rows, :] = jnp.zeros((rc, HEAD_DIM), F32)
            dvv[rows, :] = jnp.zeros((rc, HEAD_DIM), F32)
            return carry

        lax.fori_loop(0, S // rc, prep, 0)

        def visit(blocks):
            products = []
            for b, d, r, l0, first in blocks:
                qrows = _dil_rows(d, r, l0, B)
                krows = qrows if first else _dil_rows(d, r, l0 - B, 2 * B)
                qs, ks = qr[qrows, :].astype(BF16), kr[krows, :].astype(BF16)
                do = dos[qrows, :].astype(BF16)
                s = jnp.where(_dil_mask(first), _dot(qs, ks, NT) * scale, NEG)
                dp = _dot(do, vf[krows, :].astype(BF16), NT)
                products.append((qrows, krows, qs, ks, do, s, dp))
            cotangents = []
            for qrows, krows, qs, ks, do, s, dp in products:
                p = jnp.exp(s - l_ref[qrows, :][:, 0:1])
                ds = (p * (dp - dsum[qrows, :][:, 0:1]) * scale).astype(BF16)
                cotangents.append((qrows, krows, qs, ks, do, p.astype(BF16), ds))
            for qrows, krows, qs, ks, do, p, ds in cotangents:
                dqr[qrows, :] += _dot(ds, ks, NN)
                dkr[krows, :] += _dot(ds, qs, TN)
                dvv[krows, :] += _dot(p, do, TN)

        _dil_blocks(S, visit)

        def finish(t, carry):
            rows = pl.ds(pl.multiple_of(t * rc, rc), rc)
            c, s = c_ref[rows, :], s_ref[rows, :]
            dq, dk = dqr[rows, :], dkr[rows, :]
            stage_q[rows, :] = (dq * c + pltpu.roll(dq * s, HEAD_DIM // 2, axis=1)).astype(BF16)
            stage_k[rows, :] = (dk * c + pltpu.roll(dk * s, HEAD_DIM // 2, axis=1)).astype(BF16)
            stage_v[rows, :] = dvv[rows, :].astype(BF16)
            return carry

        h = pl.program_id(0)
        outs = [pltpu.make_async_copy(
            stage, dproj_ref.at[:, pl.ds(pl.multiple_of((col0 + k * H + h) * HEAD_DIM, HEAD_DIM), HEAD_DIM)], out_sems.at[k])
            for k, stage in enumerate((stage_q, stage_k, stage_v))]

        @pl.when(h > 0)
        def _():
            for cp in outs:
                cp.wait()

        lax.fori_loop(0, S // rc, finish, 0)
        for cp in outs:
            cp.start()

        @pl.when(h == H - 1)
        def _():
            for cp in outs:
                cp.wait()

    def col(k):
        return pl.BlockSpec((S, HEAD_DIM), lambda h: (0, col0 + k * H + h))

    tab = pl.BlockSpec((S, HEAD_DIM), lambda h: (0, 0))
    out = pl.BlockSpec((S, HEAD_DIM), lambda h: (0, h))
    W = H * HEAD_DIM
    big, half = pltpu.VMEM((S, HEAD_DIM), F32), pltpu.VMEM((S, HEAD_DIM), BF16)
    body, in_specs, args = _following(
        after, body,
        [col(0), col(1), col(2), tab, tab, pl.BlockSpec((1, HEAD_DIM), lambda h: (0, h)), out, out,
         pl.BlockSpec((S, HEAD_DIM), lambda h: (0, dm_col0 + h)), HBM],
        [proj, proj, proj, cos2, sin_signed, gain, o_raw, lse, dmixed, dproj])
    return _pcall(
        body, grid=(H,), in_specs=in_specs,
        out_specs=[HBM, pl.BlockSpec((8, HEAD_DIM), lambda h: (0, h))],
        out_shape=[jax.ShapeDtypeStruct(dproj.shape, BF16), jax.ShapeDtypeStruct((8, W), F32)],
        input_output_aliases={9: 0},
        scratch_shapes=[big, big, big, big, pltpu.VMEM((S, LANES), F32), big, big, big, half, half, half,
                        pltpu.SemaphoreType.DMA((3,))],
        compiler_params=_params("arbitrary"), name=name)(*args)


GELU_C = math.sqrt(2.0 / math.pi)
GELU_A = 0.044715
HALO = 16


def _shifts_down(cur, halo):
    row = lax.broadcasted_iota(jnp.int32, cur.shape, 0)
    first, second = row == 0, row == 1
    last, before_last = halo[HALO - 1:HALO, :], halo[HALO - 2:HALO - 1, :]
    two = jnp.where(first, before_last, jnp.where(second, last, pltpu.roll(cur, 2, axis=0)))
    return two, jnp.where(first, last, pltpu.roll(cur, 1, axis=0))


def _shift_up(cur, halo, k):
    n = cur.shape[0]
    out = pltpu.roll(cur, n - k, axis=0)
    row = lax.broadcasted_iota(jnp.int32, cur.shape, 0)
    for t in range(k):
        out = jnp.where(row == n - k + t, halo[t:t + 1, :], out)
    return out


def _conv3(cur, halo, cw):
    rows = (*_shifts_down(cur, halo), cur)
    return rows[0] * cw[0:1, :] + rows[1] * cw[1:2, :] + cur * cw[2:3, :] + cw[3:4, :], rows


def _gelu_parts(x):
    xx = x * x
    t = jnp.tanh(x * (GELU_C + (GELU_C * GELU_A) * xx))
    half = 0.5 * x
    return half + half * t, t, xx, half


def _gelu_slope(t, xx, half):
    return (0.5 + 0.5 * t) + half * (1.0 - t * t) * (GELU_C + (3.0 * GELU_C * GELU_A) * xx)


def _geglu_specs(tm, tn, ncb):
    hb = tm // HALO

    def cur(off):
        return pl.BlockSpec((tm, tn), lambda j, i: (i, off + j))

    def prev(off):
        return pl.BlockSpec((HALO, tn), lambda j, i: (jnp.maximum(i * hb - 1, 0), off + j))

    def taps(off):
        return pl.BlockSpec((8, tn), lambda j, i: (0, off + j))

    return [cur(0), prev(0), cur(ncb), prev(ncb), taps(0), taps(ncb)]


def _geglu_fwd(u, cwb, name, tm=512, tn=1408, after=None):
    S, F2 = u.shape
    F = F2 // 2
    tm, tn = _tile(S, tm, HALO), _tile(F, tn)
    ncb = F // tn

    def body(g_ref, gp_ref, v_ref, vp_ref, cg_ref, cv_ref, y_ref):
        top = pl.program_id(1) > 0
        gp = jnp.where(top, gp_ref[...].astype(F32), 0.0)
        vp = jnp.where(top, vp_ref[...].astype(F32), 0.0)
        gc = _conv3(g_ref[...].astype(F32), gp, cg_ref[...])[0]
        vc = _conv3(v_ref[...].astype(F32), vp, cv_ref[...])[0]
        y_ref[...] = (_gelu_parts(gc)[0] * vc).astype(BF16)

    body, in_specs, args = _following(after, body, _geglu_specs(tm, tn, ncb), [u, u, u, u, cwb, cwb])
    return _pcall(body, grid=(ncb, S // tm), in_specs=in_specs,
                  out_specs=pl.BlockSpec((tm, tn), lambda j, i: (i, j)),
                  out_shape=jax.ShapeDtypeStruct((S, F), BF16),
                  compiler_params=_params("parallel", "parallel"), name=name)(*args)


def _geglu_bwd(u, dy, cwb, name, tm=256, tn=1408, after=None):
    S, F2 = u.shape
    F = F2 // 2
    tm, tn = _tile(S, tm, HALO), _tile(F, tn)
    ncb = F // tn

    def body(g_ref, gp_ref, v_ref, vp_ref, cg_ref, cv_ref, dy_ref, dc_ref, dwg_ref, dwv_ref):
        i = pl.program_id(1)

        @pl.when(i == 0)
        def _():
            dwg_ref[...] = jnp.zeros_like(dwg_ref)
            dwv_ref[...] = jnp.zeros_like(dwv_ref)

        top = i > 0
        g, v = g_ref[...].astype(F32), v_ref[...].astype(F32)
        gp = jnp.where(top, gp_ref[...].astype(F32), 0.0)
        vp = jnp.where(top, vp_ref[...].astype(F32), 0.0)
        gc, g_rows = _conv3(g, gp, cg_ref[...])
        vc, v_rows = _conv3(v, vp, cv_ref[...])
        act, t, xx, half = _gelu_parts(gc)
        dact = _gelu_slope(t, xx, half)
        dyv = dy_ref[...].astype(F32)
        dgc = dyv * vc * dact
        dvc = dyv * act
        dc_ref[0] = dgc.astype(BF16)
        dc_ref[1] = dvc.astype(BF16)

        def taps(out_ref, dc, rows):
            for k, moved in enumerate(rows):
                out_ref[k:k + 1, :] += jnp.sum(dc * moved, axis=0, keepdims=True)
            out_ref[3:4, :] += jnp.sum(dc, axis=0, keepdims=True)

        taps(dwg_ref, dgc, g_rows)
        taps(dwv_ref, dvc, v_rows)

    body, in_specs, args = _following(
        after, body, _geglu_specs(tm, tn, ncb) + [pl.BlockSpec((tm, tn), lambda j, i: (i, j))], [u, u, u, u, cwb, cwb, dy])
    return _pcall(body, grid=(ncb, S // tm), in_specs=in_specs,
                  out_specs=[pl.BlockSpec((2, tm, tn), lambda j, i: (0, i, j)),
                             pl.BlockSpec((8, tn), lambda j, i: (0, j)), pl.BlockSpec((8, tn), lambda j, i: (0, j))],
                  out_shape=[jax.ShapeDtypeStruct((2, S, F), BF16), jax.ShapeDtypeStruct((8, F), F32),
                             jax.ShapeDtypeStruct((8, F), F32)],
                  compiler_params=_params("parallel", "arbitrary"), name=name)(*args)


def _conv_bwd(dc, cwb, name, tm=512, tn=1408, after=None):
    _, S, F = dc.shape
    tm, tn = _tile(S, tm, HALO), _tile(F, tn)
    ncb, nrb = F // tn, S // tm
    hb = tm // HALO

    def body(c_ref, n_ref, w_ref, du_ref):
        cur = c_ref[...].astype(F32)
        nxt = jnp.where(pl.program_id(2) < nrb - 1, n_ref[...].astype(F32), 0.0)
        w = w_ref[...]
        du = cur * w[2:3, :] + _shift_up(cur, nxt, 1) * w[1:2, :] + _shift_up(cur, nxt, 2) * w[0:1, :]
        du_ref[...] = du.astype(BF16)

    body, in_specs, args = _following(
        after, body,
        [pl.BlockSpec((None, tm, tn), lambda c, j, i: (c, i, j)),
         pl.BlockSpec((None, HALO, tn), lambda c, j, i: (c, jnp.minimum((i + 1) * hb, S // HALO - 1), j)),
         pl.BlockSpec((8, tn), lambda c, j, i: (0, c * ncb + j))], [dc, dc, cwb])
    return _pcall(body, grid=(2, ncb, nrb), in_specs=in_specs,
                  out_specs=pl.BlockSpec((tm, tn), lambda c, j, i: (i, c * ncb + j)),
                  out_shape=jax.ShapeDtypeStruct((S, 2 * F), BF16),
                  compiler_params=_params("parallel", "parallel", "parallel"), name=name)(*args)


def _adam_math(w, g, m, v):
    m = ADAM_B1 * m + (1.0 - ADAM_B1) * g
    v = ADAM_B2 * v + (1.0 - ADAM_B2) * (g * g)
    m_hat = m / (1.0 - ADAM_B1 ** ADAM_STEP)
    v_hat = v / (1.0 - ADAM_B2 ** ADAM_STEP)
    return -ADAM_LR * (m_hat / (jnp.sqrt(v_hat) + ADAM_EPS) + ADAM_WD * w), m, v


def _adamw(w, parts, m, v, name, tr=256):
    R, C = w.shape
    n, _, Cp = parts.shape
    tr = _tile(R, tr, 8)

    def body(w_ref, p_ref, m_ref, v_ref, g_out, d_out, m_out, v_out):
        g = p_ref[0, :, 0:C].astype(F32)
        for k in range(1, n):
            g = g + p_ref[k, :, 0:C].astype(F32)
        d, mn, vn = _adam_math(w_ref[...], g, m_ref[...], v_ref[...])
        g_out[...] = g
        d_out[...] = d
        m_out[...] = mn
        v_out[...] = vn

    spec = pl.BlockSpec((tr, C), lambda i: (i, 0))
    shape = jax.ShapeDtypeStruct((R, C), F32)
    return _pcall(body, grid=(R // tr,), in_specs=[spec, pl.BlockSpec((n, tr, Cp), lambda i: (0, i, 0)), spec, spec],
                  out_specs=[spec] * 4, out_shape=[shape] * 4, compiler_params=_params("parallel"), name=name)(w, parts, m, v)


def _adamw_chips(w, pair, parts, chip_ids, m, v, name, tr=256):
    R, C = w.shape
    Cp = pair.shape[2]
    by_columns = C == Cp and _tile(R, tr, 16) < 64
    tr, tc = (R, _tile(C, 256)) if by_columns else (_tile(R, tr, 16), C)

    def body(ids_ref, w_ref, own_ref, p1_ref, p2_ref, p3_ref, m_ref, v_ref, g_out, d_out, m_out, v_out):
        g = own_ref[:, 0:tc].astype(F32)
        for ref in (p1_ref, p2_ref, p3_ref):
            g = g + ref[:, 0:tc].astype(F32)
        d, mn, vn = _adam_math(w_ref[...], g, m_ref[...], v_ref[...])
        g_out[...] = g
        d_out[...] = d
        m_out[...] = mn
        v_out[...] = vn

    if by_columns:
        spec = pl.BlockSpec((tr, tc), lambda j, ids: (0, j))
    else:
        spec = pl.BlockSpec((tr, tc), lambda i, ids: (i, 0))

    def chip(k):
        if by_columns:
            return pl.BlockSpec((None, tr, tc), lambda j, ids: (ids[k], 0, j))
        return pl.BlockSpec((None, tr, Cp), lambda i, ids: (ids[k], i, 0))

    shape = jax.ShapeDtypeStruct((R, C), F32)
    grid_spec = pltpu.PrefetchScalarGridSpec(
        num_scalar_prefetch=1, grid=(C // tc if by_columns else R // tr,),
        in_specs=[spec, chip(0), chip(1), chip(2), chip(3), spec, spec], out_specs=[spec] * 4)
    return _pcall(body, grid_spec=grid_spec, out_shape=[shape] * 4, compiler_params=_params("parallel"),
                  name=name)(chip_ids, w, pair, parts, parts, parts, m, v)


def _place():
    return lax.axis_index("x"), lax.axis_index("y"), lax.axis_index("c")


def _other_chips(x, y):
    return [(1 - x, y), (x, 1 - y), (1 - x, 1 - y)]


IN_HBM = pl.BlockSpec(memory_space=pltpu.HBM)
SEM = pl.BlockSpec(memory_space=pltpu.SEMAPHORE)
EFFECT = pltpu.SideEffectType.DATAFLOW_SIDE_EFFECTING
TOKEN = jax.ShapeDtypeStruct((8, LANES), F32)
TOKEN_SPEC = pl.BlockSpec(memory_space=pltpu.VMEM)


def _in_hbm(a):
    return pltpu.with_memory_space_constraint(a, pltpu.HBM)


def _landing(shape):
    return _in_hbm(lax.empty(shape.shape, shape.dtype))


def _hbm_like(a):
    return pltpu.HBM(a.shape, a.dtype)


def _gather_places():
    x, y, c = _place()
    relay_from = (c * (1 - x) + (1 - c) * x, c * y + (1 - c) * (1 - y), c)
    relay_to = (c * x + (1 - c) * (1 - x), c * (1 - y) + (1 - c) * y, c)
    return (x, y, c), (x, y, 1 - c), (1 - x, y, c), (x, 1 - y, c), (1 - x, 1 - y, c), relay_from, relay_to


def _slot_copy(slot, ref, src, dst, send_sem, recv_sem, to):
    return pltpu.make_async_remote_copy(src_ref=slot(ref, *src), dst_ref=slot(ref, *dst), send_sem=send_sem,
                                        recv_sem=recv_sem, device_id=to, device_id_type=MESH)


def _split_call(body, arrays, sems_in, sems_out, after, name, token=True):
    na, ni, no = len(arrays), len(sems_in), len(sems_out)

    def wrapped(*refs):
        body(refs[:na], refs[na:na + ni], refs[na + ni + 1:na + ni + 1 + no])
        if token:
            refs[-1][...] = jnp.zeros_like(refs[-1])

    outs = _pcall(
        wrapped, in_specs=[IN_HBM] * na + [SEM] * ni + [HBM],
        out_specs=[SEM] * no + [IN_HBM] * na + ([TOKEN_SPEC] if token else []),
        out_shape=[pltpu.SemaphoreType.DMA((n,)) for n in sems_out] + [_hbm_like(s) for s in arrays] + ([TOKEN] if token else []),
        input_output_aliases={a: no + a for a in range(na)},
        compiler_params=pltpu.CompilerParams(has_side_effects=EFFECT), name=name,
    )(*[_in_hbm(s) for s in arrays], *sems_in, after)
    return list(outs[:no]), list(outs[no:no + na]), (outs[-1] if token else None)


def _gather_start(landing, slots, after, name):
    na = len(landing)

    def body(land, _, sems):
        me, sib, xn, yn, _, _, _ = _gather_places()
        for a in range(na):
            for k, to in enumerate((sib, xn, yn)):
                _slot_copy(slots[a], land[a], me, me, sems[0].at[3 * a + k], sems[1].at[3 * a + k], to).start()

    return _split_call(body, landing, [], [3 * na, 3 * na], after, name)


def _gather_relay(gathered, sems1, slots, after, name):
    na = len(gathered)

    def body(gath, taken, given):
        me, sib, xn, yn, _, relay_from, relay_to = _gather_places()
        for a in range(na):
            for k, peer in enumerate((sib, xn, yn)):
                arrival = _slot_copy(slots[a], gath[a], me, peer, taken[0].at[3 * a + k], taken[1].at[3 * a + k], peer)
                arrival.wait_send()
                arrival.wait_recv()
        for a in range(na):
            _slot_copy(slots[a], gath[a], relay_from, relay_from, given[0].at[a], given[1].at[a], relay_to).start()
            for k, peer in enumerate((xn, yn)):
                _slot_copy(slots[a], gath[a], peer, peer, given[2].at[2 * a + k], given[3].at[2 * a + k], sib).start()

    return _split_call(body, gathered, sems1, [na, na, 2 * na, 2 * na], after, name)


def _gather_pass(gathered, relay_sems, slots, after, name):
    na = len(gathered)

    def body(gath, taken, given):
        me, sib, xn, yn, diag, relay_from, relay_to = _gather_places()
        for a in range(na):
            _slot_copy(slots[a], gath[a], relay_from, relay_from, taken[0].at[a], taken[1].at[a], relay_to).wait_send()
            _slot_copy(slots[a], gath[a], me, diag, taken[0].at[a], taken[1].at[a], relay_to).wait_recv()
        for a in range(na):
            _slot_copy(slots[a], gath[a], diag, diag, given[0].at[a], given[1].at[a], sib).start()

    return _split_call(body, gathered, relay_sems, [na, na], after, name)


def _gather_finish(gathered, pass_sems, diag_sems, slots, after, name):
    na = len(gathered)

    def body(gath, taken, _):
        (x, y, c), sib, xn, yn, diag, _, _ = _gather_places()
        for a in range(na):
            for k, peer in enumerate((xn, yn)):
                passed = _slot_copy(slots[a], gath[a], peer, (peer[0], peer[1], 1 - c), taken[0].at[2 * a + k],
                                    taken[1].at[2 * a + k], sib)
                passed.wait_send()
                passed.wait_recv()
            passed = _slot_copy(slots[a], gath[a], diag, (diag[0], diag[1], 1 - c), taken[2].at[a], taken[3].at[a], sib)
            passed.wait_send()
            passed.wait_recv()

    return _split_call(body, gathered, list(pass_sems) + list(diag_sems), [], after, name, token=False)[1]


def _pair_copy(view, src, land, send_sems, recv_sems, chip):
    x, y, c = _place()
    return pltpu.make_async_remote_copy(
        src_ref=view(src, chip, 1 - c), dst_ref=land.at[chip], send_sem=send_sems.at[chip], recv_sem=recv_sems.at[chip],
        device_id=(x, y, 1 - c), device_id_type=MESH)


def _pair_start(grad, view, block, after, name):
    def body(src, land, after_ref, send_sems, recv_sems, src_thru, land_thru, token):
        for chip in range(N_CHIP):
            _pair_copy(view, src, land, send_sems, recv_sems, chip).start()
        token[...] = jnp.zeros_like(token)

    sems = pltpu.SemaphoreType.DMA((N_CHIP,))
    land = jax.ShapeDtypeStruct((N_CHIP, *block), BF16)
    return _pcall(
        body, in_specs=[IN_HBM, IN_HBM, HBM], out_specs=[SEM, SEM, IN_HBM, IN_HBM, TOKEN_SPEC],
        out_shape=[sems, sems, _hbm_like(grad), _hbm_like(land), TOKEN], input_output_aliases={0: 2, 1: 3},
        compiler_params=pltpu.CompilerParams(has_side_effects=EFFECT), name=name,
    )(_in_hbm(grad), _landing(land), after)


def _pair_wait(grad, recv, send_sems, recv_sems, view, after, name):
    def body(src, land, send, recv_s, after_ref, src_thru, land_thru):
        for chip in range(N_CHIP):
            copy = _pair_copy(view, src, land, send, recv_s, chip)
            copy.wait_send()
            copy.wait_recv()

    return _pcall(
        body, in_specs=[IN_HBM, IN_HBM, SEM, SEM, HBM], out_specs=[IN_HBM, IN_HBM],
        out_shape=[_hbm_like(grad), _hbm_like(recv)], input_output_aliases={0: 0, 1: 1},
        compiler_params=pltpu.CompilerParams(has_side_effects=EFFECT), name=name,
    )(grad, recv, send_sems, recv_sems, after)


def _chip_start(pair, after, name):
    def body(src, land, after_ref, send_sems, recv_sems, src_thru, land_thru, token):
        x, y, c = _place()
        for j, (px, py) in enumerate(_other_chips(x, y)):
            pltpu.make_async_remote_copy(
                src_ref=src.at[2 * px + py], dst_ref=land.at[2 * x + y], send_sem=send_sems.at[j], recv_sem=recv_sems.at[j],
                device_id=(px, py, c), device_id_type=MESH).start()
        token[...] = jnp.zeros_like(token)

    sems = pltpu.SemaphoreType.DMA((3,))
    return _pcall(
        body, in_specs=[IN_HBM, IN_HBM, HBM], out_specs=[SEM, SEM, IN_HBM, IN_HBM, TOKEN_SPEC],
        out_shape=[sems, sems, _hbm_like(pair), _hbm_like(pair), TOKEN], input_output_aliases={0: 2, 1: 3},
        compiler_params=pltpu.CompilerParams(has_side_effects=EFFECT), name=name,
    )(_in_hbm(pair), _landing(pair), after)


def _chip_wait(pair, parts, send_sems, recv_sems, after, name):
    def body(src, land, send, recv, after_ref, src_thru, land_thru):
        x, y, c = _place()
        for j, (px, py) in enumerate(_other_chips(x, y)):
            copy = pltpu.make_async_remote_copy(
                src_ref=src.at[2 * px + py], dst_ref=land.at[2 * px + py], send_sem=send.at[j], recv_sem=recv.at[j],
                device_id=(px, py, c), device_id_type=MESH)
            copy.wait_send()
            copy.wait_recv()

    return _pcall(
        body, in_specs=[IN_HBM, IN_HBM, SEM, SEM, HBM], out_specs=[IN_HBM, IN_HBM],
        out_shape=[_hbm_like(pair), _hbm_like(parts)], input_output_aliases={0: 0, 1: 1},
        compiler_params=pltpu.CompilerParams(has_side_effects=EFFECT), name=name,
    )(pair, parts, send_sems, recv_sems, after)


def _pair_add(core, grad, recv, block, grad_spec, name):
    _, R, C = recv.shape
    tr = block

    def body(c_ref, g_ref, r_ref, o_ref):
        o_ref[...] = (g_ref[...].astype(F32) + r_ref[...].astype(F32)).astype(BF16)

    grid_spec = pltpu.PrefetchScalarGridSpec(
        num_scalar_prefetch=1, grid=(N_CHIP, R // tr),
        in_specs=[grad_spec, pl.BlockSpec((None, tr, C), lambda k, i, c: (k, i, 0))],
        out_specs=pl.BlockSpec((None, tr, C), lambda k, i, c: (k, i, 0)))
    return _pcall(body, grid_spec=grid_spec, out_shape=jax.ShapeDtypeStruct(recv.shape, BF16),
                  compiler_params=_params("parallel", "parallel"), name=name)(core, grad, recv)


def _small_copies(gath, send_sems, recv_sems):
    x, y, c = _place()
    peers = [(x, y, 1 - c)] + [(px, py, pc) for px, py in _other_chips(x, y) for pc in (c, 1 - c)]
    pairs = []
    for a, ref in enumerate(gath):
        mine = ref.at[4 * x + 2 * y + c]
        for k, (px, py, pc) in enumerate(peers):
            sems = dict(send_sem=send_sems.at[7 * a + k], recv_sem=recv_sems.at[7 * a + k], device_id=(px, py, pc),
                        device_id_type=MESH)
            pairs.append((pltpu.make_async_remote_copy(src_ref=mine, dst_ref=mine, **sems),
                          pltpu.make_async_remote_copy(src_ref=mine, dst_ref=ref.at[4 * px + 2 * py + pc], **sems)))
    return pairs


def _small_start(landing, after, name):
    na = len(landing)

    def body(*refs):
        for send, _ in _small_copies(refs[:na], refs[na + 1], refs[na + 2]):
            send.start()
        refs[-1][...] = jnp.zeros_like(refs[-1])

    sems = pltpu.SemaphoreType.DMA((7 * na,))
    outs = _pcall(
        body, in_specs=[IN_HBM] * na + [HBM], out_specs=[SEM, SEM] + [IN_HBM] * na + [TOKEN_SPEC],
        out_shape=[sems, sems] + [_hbm_like(s) for s in landing] + [TOKEN],
        input_output_aliases={a: 2 + a for a in range(na)},
        compiler_params=pltpu.CompilerParams(has_side_effects=EFFECT), name=name,
    )(*[_in_hbm(s) for s in landing], after)
    return outs[0], outs[1], outs[2:2 + na], outs[-1]


def _small_wait(gathered, send_sems, recv_sems, after, name):
    na = len(gathered)

    def body(*refs):
        for send, arrival in _small_copies(refs[:na], refs[na], refs[na + 1]):
            send.wait_send()
            arrival.wait_recv()

    return list(_pcall(
        body, in_specs=[IN_HBM] * na + [SEM, SEM, HBM], out_specs=[IN_HBM] * na,
        out_shape=[_hbm_like(g) for g in gathered], input_output_aliases={a: a for a in range(na)},
        compiler_params=pltpu.CompilerParams(has_side_effects=EFFECT), name=name,
    )(*gathered, send_sems, recv_sems, after))


def _small_finish(gathered, params, name):
    na, npar = len(gathered), len(params)

    def body(*refs):
        g_refs, wmv = refs[:na], refs[na:na + 3 * npar]
        o_sums, o_params = refs[na + 3 * npar:2 * na + 3 * npar], refs[2 * na + 3 * npar:]
        sums = []
        for a in range(na):
            acc = g_refs[a][0]
            for k in range(1, N_DEV):
                acc = acc + g_refs[a][k]
            o_sums[a][...] = acc
            sums.append(acc)
        for j, (a, row, _, _, _) in enumerate(params):
            g = sums[a][row:row + 1, :]
            d, mn, vn = _adam_math(wmv[3 * j][...], g, wmv[3 * j + 1][...], wmv[3 * j + 2][...])
            for out, val in zip(o_params[4 * j:4 * j + 4], (g, d, mn, vn)):
                out[...] = val

    vm = pl.BlockSpec(memory_space=pltpu.VMEM)
    flat = [t for p in params for t in p[2:]]
    out_shape = [jax.ShapeDtypeStruct(g.shape[1:], F32) for g in gathered]
    out_shape += [jax.ShapeDtypeStruct(p[2].shape, F32) for p in params for _ in range(4)]
    outs = _pcall(body, in_specs=[vm] * (na + 3 * npar), out_specs=[vm] * len(out_shape), out_shape=out_shape,
                  name=name)(*gathered, *flat)
    return outs[:na], [outs[na + 4 * j:na + 4 * j + 4] for j in range(npar)]


def _local_step(x, tgt, gains, weights):
    g_pre_mix, g_post_mix, g_pre_ffn, g_post_ffn, g_sb, g_dil = gains
    S, D = x.shape
    hs = g_sb.shape[1] // HEAD_DIM
    hd = g_dil.shape[1] // HEAD_DIM
    cos2, sin_signed = _rope_tables(S)

    h1 = _rms_fwd(x, g_pre_mix, "rms_in", after=weights.start())
    w_in_g = weights.w_in(h1)
    proj = _mm_nn(h1, w_in_g, BF16, "proj", tn=768)
    o_sb, ct_sb, mixed = _sb_fwd(proj, g_sb, hs, hs + hd, "sb_fwd", after=weights.relay_out(proj))
    o_dl, lse_dl, mixed = _dil_fwd(proj, cos2, sin_signed, g_dil, mixed, 3 * hs, hd, "dil_fwd", after=weights.after_sb(o_sb))
    w_out_g = weights.w_out(o_dl)
    mix = _mm_nn(mixed, w_out_g, F32, "mix_out", tn=1024)
    x2, h2 = _mid_fwd(x, mix, g_post_mix, g_pre_ffn, "mid_fwd", after=weights.after_mix(mix))
    w_up_g, cwb = weights.w_up(h2)
    u = _mm_nn(h2, w_up_g, BF16, "ffn_up", b_transposed=True)
    y = _geglu_fwd(u, cwb, "geglu_fwd", after=weights.forward_down(u))
    w_down_g = weights.w_down(y)
    f = _mm_nn(y, w_down_g, F32, "ffn_down", tn=1024, tk=2816)

    dy, df, dg_post_ffn, loss = _loss_bwd(x2, f, tgt, g_post_ffn, "loss_bwd")
    dyv = _mm_nt(df, w_down_g, BF16, "d_y", tn=1408)
    dw_down = _mm_tn(y, df, D, BF16, "dw_down", tm=1408, tn=1024)
    dc, dcw_g, dcw_v = _geglu_bwd(u, dyv, cwb, "geglu_bwd", after=weights.grad("w_down", dw_down))
    du = _conv_bwd(dc, cwb, "conv_bwd", after=weights.grad_reduce("w_down", dc))
    dh2 = _mm_nt(du, w_up_g, BF16, "d_h2", tk=1408, b_transposed=True, per_step=2)
    dw_up = _mm_tn(du, h2, D, BF16, "dw_up", tm=1408, tn=1024)
    dx2, dmix, dg_pre_ffn, dg_post_mix = _mid_bwd(
        dy, dh2, x2, mix, g_pre_ffn, g_post_mix, "mid_bwd", after=weights.grad("w_up", dw_up))
    dmixed = _mm_nt(dmix, w_out_g, BF16, "d_mixed", after=weights.grad_reduce("w_up", dmix))
    dw_out = _mm_tn(mixed, dmix, D, BF16, "dw_out", tn=1024)
    dproj, dg_sb = _sb_bwd(proj, g_sb, o_sb, ct_sb, dmixed, 0, hs, "sb_bwd", after=weights.grad("w_out", dw_out))
    dproj, dg_dil = _dil_bwd(proj, cos2, sin_signed, g_dil, o_dl, lse_dl, dmixed, dproj, hs, 3 * hs, hd, "dil_bwd",
                             after=weights.grad_reduce("w_out", dg_sb))
    dw_in = _mm_tn(h1, dproj, w_in_g.shape[2], BF16, "dw_in", tn=768)
    dep = weights.grad_reduce("w_in", weights.meanwhile(weights.grad("w_in", dw_in)))
    dh1 = _mm_nt(dproj, w_in_g, BF16, "d_h1", tk=768, after=dep, per_step=4)
    grad_x, dg_pre_mix = _first_bwd(dx2, dh1, x, g_pre_mix, "first_bwd")
    small = (dg_pre_mix, dg_post_mix, dg_pre_ffn, dg_post_ffn, dg_sb[0:1], dg_dil[0:1], jnp.concatenate([dcw_g, dcw_v], axis=1))
    weights.small(small, loss)
    return loss, grad_x, small


def _pad_cols(a, to):
    return jnp.pad(a, ((0, 0), (0, to - a.shape[1])))


def kernel(x, pre_mix_gain, post_mix_gain, pre_ffn_gain, post_ffn_gain, w_in, sb_out_gain, dil_out_gain, w_out, w_up, conv_w, conv_b, w_down, loss_target, m_pre_mix_gain, m_post_mix_gain, m_pre_ffn_gain, m_post_ffn_gain, m_w_in, m_sb_out_gain, m_dil_out_gain, m_w_out, m_w_up, m_conv_w, m_conv_b, m_w_down, v_pre_mix_gain, v_post_mix_gain, v_pre_ffn_gain, v_post_ffn_gain, v_w_in, v_sb_out_gain, v_dil_out_gain, v_w_out, v_w_up, v_conv_w, v_conv_b, v_w_down):
    xb, tb = x[0], loss_target[0]
    S, D = xb.shape
    w_in, w_out, w_up, w_down, conv_w = w_in[0], w_out[0], w_up[0], w_down[0], conv_w[0]
    n_in, e_rows = w_in.shape[1], w_out.shape[0]
    cu, half = w_up.shape[1], w_down.shape[0]
    assert cu == 2 * half and half % 16 == 0
    cup = -(-cu // LANES) * LANES
    fp = N_CHIP * cup
    px, py, pc = _place()
    me = 4 * px + 2 * py + pc
    core = jnp.reshape(pc, (1,)).astype(jnp.int32)
    chip_ids = jnp.stack([2 * px + py, 2 * (1 - px) + py, 2 * px + 1 - py, 2 * (1 - px) + 1 - py]).astype(jnp.int32)

    w_up_t, m_up_t, v_up_t = (jnp.swapaxes(t, 0, 1) for t in (w_up, m_w_up[0], v_w_up[0]))

    def by_dev(ref, qx, qy, qc):
        return ref.at[4 * qx + 2 * qy + qc]

    def down_slot(ref, qx, qy, qc):
        return ref.at[2 * qx + qy, pl.ds(qc * half, half)]

    def by_pair(ref, chip, k):
        return ref.at[chip, k]

    def down_pair(ref, chip, k):
        return ref.at[chip, pl.ds(k * half, half)]

    def pair_spec(tr, cols):
        return pl.BlockSpec((None, None, tr, cols), lambda k, i, c: (k, c[0], i, 0))

    tr_in, tr_up = _tile(D, 512, 16), _tile(cup, 256, 16)
    grad_plan = {
        "w_in": ((N_CHIP, 2, D, n_in), by_pair, (D, n_in), tr_in, pair_spec(tr_in, n_in)),
        "w_out": ((N_CHIP, 2, e_rows, D), by_pair, (e_rows, D), e_rows, pair_spec(e_rows, D)),
        "w_up": ((N_CHIP, 2, cup, D), by_pair, (cup, D), tr_up, pair_spec(tr_up, D)),
        "w_down": ((N_CHIP, cup, D), down_pair, (half, D), half,
                   pl.BlockSpec((None, half, D), lambda k, i, c: (k, c[0], 0))),
    }

    class Exchanges:
        def __init__(self):
            self.in_flight = {}

        def start(self):
            def own_slot(shard):
                return lax.dynamic_update_index_in_dim(lax.empty((N_DEV, *shard.shape), shard.dtype), shard, me, 0)

            self.group_slots = {"in": [by_dev], "out": [by_dev], "up": [by_dev, by_dev], "down": [down_slot]}
            self.flight = {}
            sems, gath, token = _gather_start([own_slot(w_in.astype(BF16))], [by_dev], core, "gather_in_start")
            self.flight["in"] = (sems, gath)
            zero = token[0, 0]
            self.landing = {
                "out": [own_slot((w_out + zero).astype(BF16))],
                "up": [own_slot(jnp.pad(w_up_t + zero, ((0, cup - cu), (0, 0))).astype(BF16)),
                       own_slot(jnp.pad(conv_w + zero, ((0, 8 - conv_w.shape[0]), (0, cup - cu))))],
                "down": [lax.dynamic_update_slice(jnp.zeros((N_CHIP, cup, D), BF16), (w_down + zero).astype(BF16)[None],
                                                  (2 * px + py, pc * half, 0))]}
            return token

        def begin(self, group, after):
            sems, gath, token = _gather_start(self.landing[group], self.group_slots[group], after, "gather_%s_start" % group)
            self.flight[group] = (sems, gath)
            return token

        def relay(self, group, after):
            sems, gath = self.flight[group]
            sems, gath, token = _gather_relay(gath, sems, self.group_slots[group], after, "gather_%s_relay" % group)
            self.flight[group] = (sems, gath)
            return token

        def pass_on(self, group, after):
            sems, gath = self.flight[group]
            diag_sems, gath, token = _gather_pass(gath, sems[:2], self.group_slots[group], after, "gather_%s_pass" % group)
            self.flight[group] = (sems[2:], diag_sems, gath)
            return token

        def finish(self, group, after):
            pass_sems, diag_sems, gath = self.flight[group]
            return _gather_finish(gath, pass_sems, diag_sems, self.group_slots[group], after, "gather_%s_finish" % group)

        def w_in(self, after):
            token = self.begin("up", self.begin("out", self.relay("in", after)))
            return self.finish("in", self.pass_on("in", token))[0]

        def relay_out(self, after):
            return self.relay("out", after)

        def after_sb(self, after):
            return self.begin("down", self.relay("up", self.pass_on("out", after)))

        def w_out(self, after):
            return self.finish("out", after)[0].reshape(1, N_DEV * e_rows, D)

        def after_mix(self, after):
            return self.pass_on("up", after)

        def w_up(self, after):
            w_up_g, cw_g = self.finish("up", after)
            cb = _pad_cols(conv_b.reshape(N_DEV, cu), cup).reshape(1, 2 * fp)
            cw_full = jnp.transpose(cw_g[:, :3, :], (1, 0, 2)).reshape(3, 2 * fp)
            cwb = jnp.concatenate([cw_full, cb, jnp.zeros((4, 2 * fp), F32)], axis=0)
            return w_up_g, cwb

        def forward_down(self, after):
            return self.relay("down", after)

        def w_down(self, after):
            return self.finish("down", self.pass_on("down", after))[0].reshape(1, fp, D)

        def small(self, small, loss):
            d_pre_mix, d_post_mix, d_pre_ffn, d_post_ffn, d_sb, d_dil, d_conv = small

            def rows_of(*vectors):
                n = vectors[0].shape[1]
                row = lax.broadcasted_iota(jnp.int32, (8, n), 0)
                out = jnp.zeros((8, n), F32)
                for k, vec in enumerate(vectors):
                    out = jnp.where(row == k, vec, out)
                return out

            parts = [rows_of(d_pre_mix, d_post_mix, d_pre_ffn, d_post_ffn, jnp.broadcast_to(loss[:, :1], (1, D))),
                     rows_of(d_sb, d_dil), d_conv]
            landing = [lax.dynamic_update_index_in_dim(lax.empty((N_DEV, *p.shape), F32), p, me, 0) for p in parts]
            self.small_flight = _small_start(landing, parts[0], "small_start")

        def small_sums(self, after):
            send, recv, gath, _ = self.small_flight
            gath = _small_wait(gath, send, recv, after, "small_wait")
            params = [(0, 0, pre_mix_gain, m_pre_mix_gain, v_pre_mix_gain), (0, 1, post_mix_gain, m_post_mix_gain, v_post_mix_gain),
                      (0, 2, pre_ffn_gain, m_pre_ffn_gain, v_pre_ffn_gain), (0, 3, post_ffn_gain, m_post_ffn_gain, v_post_ffn_gain),
                      (1, 0, sb_out_gain, m_sb_out_gain, v_sb_out_gain), (1, 1, dil_out_gain, m_dil_out_gain, v_dil_out_gain)]
            (gains_sum, _, conv_sum), gain_steps = _small_finish(gath, params, "small_finish")
            return gains_sum[4, 0], conv_sum, gain_steps

        def grad(self, name, dw):
            view_shape, view, block, tr, spec = grad_plan[name]
            send, recv_sems, dw, recv, token = _pair_start(dw.reshape(view_shape), view, block, core, "pair_start_" + name)
            self.in_flight[name] = (dw, recv, send, recv_sems)
            return token

        def grad_reduce(self, name, after):
            _, view, _, tr, spec = grad_plan[name]
            dw, recv = _pair_wait(*self.in_flight[name], view, after, "pair_wait_" + name)
            pair = _pair_add(core, dw, recv, tr, spec, "pair_add_" + name)
            send, recv_sems, pair, parts, token = _chip_start(pair, recv, "chip_start_" + name)
            self.in_flight[name] = (pair, parts, send, recv_sems)
            self.last_token = token
            return token

        def meanwhile(self, token):
            self.out_w_down = _adamw_chips(w_down, *self.grad_parts("w_down", token), chip_ids, m_w_down[0], v_w_down[0],
                                           "adam_w_down")
            return self.out_w_down[1]

        def grad_parts(self, name, after):
            return _chip_wait(*self.in_flight[name], after, "chip_wait_" + name)

    exchanges = Exchanges()
    gains = (pre_mix_gain, post_mix_gain, pre_ffn_gain, post_ffn_gain, sb_out_gain, dil_out_gain)
    loss, grad_x, small = _local_step(xb, tb, gains, exchanges)


    out_w_down = exchanges.out_w_down
    out_up_t = _adamw_chips(w_up_t, *exchanges.grad_parts("w_up", exchanges.small_flight[3]), chip_ids, m_up_t, v_up_t, "adam_w_up")
    out_w_up = [jnp.swapaxes(o, 0, 1) for o in out_up_t]
    out_w_out = _adamw_chips(w_out, *exchanges.grad_parts("w_out", out_up_t[1]), chip_ids, m_w_out[0], v_w_out[0], "adam_w_out")
    loss_out, g_conv, gain_steps = exchanges.small_sums(out_w_out[1])
    out_pre_mix, out_post_mix, out_pre_ffn, out_post_ffn, out_sb, out_dil = gain_steps
    g_conv_b = g_conv[3].reshape(N_DEV, cup)[:, :cu].reshape(1, N_DEV * cu)
    g_conv_w = lax.dynamic_index_in_dim(g_conv[0:3].reshape(3, N_DEV, cup), me, axis=1, keepdims=False)[:, :cu]
    out_conv_b = _adamw(conv_b, g_conv_b[None], m_conv_b, v_conv_b, "adam_conv_b")
    out_conv_w = _adamw(conv_w, g_conv_w[None], m_conv_w[0], v_conv_w[0], "adam_conv_w")
    out_w_in = _adamw_chips(w_in, *exchanges.grad_parts("w_in", out_conv_w[1]), chip_ids, m_w_in[0], v_w_in[0], "adam_w_in")

    order = [out_pre_mix, out_post_mix, out_pre_ffn, out_post_ffn, [o[None] for o in out_w_in], out_sb, out_dil,
             [o[None] for o in out_w_out], [o[None] for o in out_w_up], [o[None] for o in out_conv_w], out_conv_b,
             [o[None] for o in out_w_down]]
    outs = [loss_out, grad_x[None]]
    for k in range(4):
        outs += [o[k] for o in order]
    return tuple(outs)
```

```python
import math

import jax
import jax.numpy as jnp
from jax import lax
from jax.experimental import pallas as pl
from jax.experimental.pallas import tpu as pltpu

F32 = jnp.float32
BF16 = jnp.bfloat16
HEAD_DIM = 128
LANES = 128
KEY_BLOCK = 128
DILATIONS = (1, 4, 16)
RMS_EPS = 1e-6
ROPE_THETA = 10000.0
NEG = -1e30
ADAM_LR, ADAM_B1, ADAM_B2, ADAM_EPS, ADAM_WD, ADAM_STEP = 0.001, 0.9, 0.999, 1e-08, 0.01, 10
MESH = pl.DeviceIdType.MESH
N_DEV = 8
N_CHIP = 4
HBM = pl.BlockSpec(memory_space=pl.ANY)
VMEM_LIMIT = 56 * 1024 * 1024

_pcall = pl.pallas_call


def _tile(n, pref, mult=LANES):
    best = None
    t = mult
    while t <= min(n, pref):
        if n % t == 0:
            best = t
        t += mult
    return n if best is None else best


def _params(*sem):
    return pltpu.CompilerParams(dimension_semantics=sem, vmem_limit_bytes=VMEM_LIMIT)


def _following(after, body, in_specs, args):
    if after is None:
        return body, list(in_specs), list(args)
    n = len(args)

    def ordered(*refs):
        body(*refs[:n], *refs[n + 1:])

    return ordered, list(in_specs) + [HBM], list(args) + [after]


def _dot(a, b, dims):
    return lax.dot_general(a, b, (dims, ((), ())), preferred_element_type=F32)


NN = ((1,), (0,))
NT = ((1,), (1,))
TN = ((0,), (0,))


def _mm_body(dims, nk, tile):
    if nk == 1:
        def single(a_ref, b_ref, o_ref):
            o_ref[...] = _dot(a_ref[...].astype(BF16), b_ref[...].astype(BF16), dims).astype(o_ref.dtype)

        return single, []

    def body(a_ref, b_ref, o_ref, acc_ref):
        k = pl.program_id(2)

        @pl.when(k == 0)
        def _():
            acc_ref[...] = jnp.zeros_like(acc_ref)

        acc_ref[...] += _dot(a_ref[...].astype(BF16), b_ref[...].astype(BF16), dims)

        @pl.when(k == nk - 1)
        def _():
            o_ref[...] = acc_ref[...].astype(o_ref.dtype)

    return body, [pltpu.VMEM(tile, F32)]


def _mm_nn(a, b3, out_dtype, name, tm=1024, tn=1408, tk=2048, b_transposed=False):
    M, K = a.shape
    C, n = b3.shape[0], b3.shape[1 if b_transposed else 2]
    tm, tk, tn = _tile(M, tm, 8), _tile(K, tk), _tile(n, tn)
    npc, nk = n // tn, K // tk
    body, scratch = _mm_body(NT if b_transposed else NN, nk, (tm, tn))
    b_spec = (pl.BlockSpec((None, tn, tk), lambda i, j, k: (j // npc, j % npc, k)) if b_transposed
              else pl.BlockSpec((None, tk, tn), lambda i, j, k: (j // npc, k, j % npc)))
    return _pcall(
        body, grid=(M // tm, C * npc, nk),
        in_specs=[pl.BlockSpec((tm, tk), lambda i, j, k: (i, k)), b_spec],
        out_specs=pl.BlockSpec((tm, tn), lambda i, j, k: (i, j)),
        out_shape=jax.ShapeDtypeStruct((M, C * n), out_dtype), scratch_shapes=scratch,
        compiler_params=_params("parallel", "parallel", "arbitrary"), name=name)(a, b3)


def _mm_nt(a, b3, out_dtype, name, tm=1024, tn=1024, tk=2048, after=None, b_transposed=False, per_step=1):
    M, _ = a.shape
    C, N, n = (b3.shape[0], b3.shape[2], b3.shape[1]) if b_transposed else b3.shape
    tm, tn, tk = _tile(M, tm, 8), _tile(N, tn), _tile(n, tk)
    dims = NN if b_transposed else NT
    extra = [] if after is None else [after]
    if per_step > 1 and tk == n and C % per_step == 0:
        nk, scratch = C // per_step, [pltpu.VMEM((tm, tn), F32)]
        b3 = b3.reshape(nk, per_step, *b3.shape[1:])
        a_spec = pl.BlockSpec((tm, per_step * n), lambda i, j, k: (i, k))
        if b_transposed:
            b_spec = pl.BlockSpec((None, per_step, n, tn), lambda i, j, k: (k, 0, 0, j))
        else:
            b_spec = pl.BlockSpec((None, per_step, tn, n), lambda i, j, k: (k, 0, j, 0))

        def body(a_ref, b_ref, *rest):
            o_ref, acc_ref = rest[len(extra):]
            k = pl.program_id(2)

            @pl.when(k == 0)
            def _():
                acc_ref[...] = jnp.zeros_like(acc_ref)

            b = b_ref[...].astype(BF16)
            b = b.reshape(per_step * n, tn) if b_transposed else jnp.concatenate([b[u] for u in range(per_step)], axis=1)
            acc_ref[...] += _dot(a_ref[...].astype(BF16), b, dims)

            @pl.when(k == nk - 1)
            def _():
                o_ref[...] = acc_ref[...].astype(o_ref.dtype)
    else:
        kpc = n // tk
        nk = C * kpc
        inner, scratch = _mm_body(dims, nk, (tm, tn))
        a_spec = pl.BlockSpec((tm, tk), lambda i, j, k: (i, k))
        b_spec = (pl.BlockSpec((None, tk, tn), lambda i, j, k: (k // kpc, k % kpc, j)) if b_transposed
                  else pl.BlockSpec((None, tn, tk), lambda i, j, k: (k // kpc, j, k % kpc)))

        def body(a_ref, b_ref, *rest):
            inner(a_ref, b_ref, *rest[len(extra):])

    return _pcall(
        body, grid=(M // tm, N // tn, nk), in_specs=[a_spec, b_spec] + [HBM] * len(extra),
        out_specs=pl.BlockSpec((tm, tn), lambda i, j, k: (i, j)),
        out_shape=jax.ShapeDtypeStruct((M, N), out_dtype), scratch_shapes=scratch,
        compiler_params=_params("parallel", "parallel", "arbitrary"), name=name)(a, b3, *extra)


def _mm_tn(x, y, n, out_dtype, name, tm=1024, tn=1408, tk=2048, after=None):
    S, P = x.shape
    C = y.shape[1] // n
    tm, tn, tk = _tile(P, tm), _tile(n, tn), _tile(S, tk, 8)
    npc, nk = n // tn, S // tk
    inner, scratch = _mm_body(TN, nk, (tm, tn))
    extra = [] if after is None else [after]

    def body(x_ref, y_ref, *rest):
        inner(x_ref, y_ref, *rest[len(extra):])

    return _pcall(
        body, grid=(P // tm, C * npc, nk),
        in_specs=[pl.BlockSpec((tk, tm), lambda i, j, k: (k, i)),
                  pl.BlockSpec((tk, tn), lambda i, j, k: (k, j))] + [HBM] * len(extra),
        out_specs=pl.BlockSpec((None, tm, tn), lambda i, j, k: (j // npc, i, j % npc)),
        out_shape=jax.ShapeDtypeStruct((C, P, n), out_dtype), scratch_shapes=scratch,
        compiler_params=_params("parallel", "parallel", "arbitrary"), name=name)(x, y, *extra)


def _rms_scale(v):
    return lax.rsqrt(jnp.mean(v * v, axis=-1, keepdims=True) + RMS_EPS)


def _rms_bwd(gy, v, r):
    return r * gy - v * (r * r * r * jnp.mean(gy * v, axis=-1, keepdims=True))


def _rows_spec(tm, d):
    return pl.BlockSpec((tm, d), lambda i: (i, 0))


def _vec_spec(d):
    return pl.BlockSpec((1, d), lambda i: (0, 0))


def _rms_fwd(x, g, name, tm=256, after=None):
    S, D = x.shape

    def body(x_ref, g_ref, h_ref):
        v = x_ref[...]
        h_ref[...] = (v * _rms_scale(v) * g_ref[...]).astype(BF16)

    body, in_specs, args = _following(after, body, [_rows_spec(tm, D), _vec_spec(D)], [x, g])
    return _pcall(body, grid=(S // tm,), in_specs=in_specs, out_specs=_rows_spec(tm, D),
                  out_shape=jax.ShapeDtypeStruct((S, D), BF16), compiler_params=_params("parallel"), name=name)(*args)


def _mid_fwd(x, mix, g_post, g_pre, name, tm=256, after=None):
    S, D = x.shape

    def body(x_ref, m_ref, gp_ref, gn_ref, x2_ref, h_ref):
        m = m_ref[...]
        x2 = x_ref[...] + m * _rms_scale(m) * gp_ref[...]
        x2_ref[...] = x2
        h_ref[...] = (x2 * _rms_scale(x2) * gn_ref[...]).astype(BF16)

    body, in_specs, args = _following(
        after, body, [_rows_spec(tm, D), _rows_spec(tm, D), _vec_spec(D), _vec_spec(D)], [x, mix, g_post, g_pre])
    return _pcall(body, grid=(S // tm,), in_specs=in_specs,
                  out_specs=[_rows_spec(tm, D), _rows_spec(tm, D)],
                  out_shape=[jax.ShapeDtypeStruct((S, D), F32), jax.ShapeDtypeStruct((S, D), BF16)],
                  compiler_params=_params("parallel"), name=name)(*args)


def _loss_bwd(x2, f, tgt, g_post, name, tm=256):
    S, D = x2.shape

    def body(x2_ref, f_ref, t_ref, g_ref, dy_ref, df_ref, dg_ref, ls_ref):
        i = pl.program_id(0)

        @pl.when(i == 0)
        def _():
            dg_ref[...] = jnp.zeros_like(dg_ref)
            ls_ref[...] = jnp.zeros_like(ls_ref)

        fv = f_ref[...]
        r = _rms_scale(fv)
        g = g_ref[...]
        err = x2_ref[...] + fv * r * g - t_ref[...]
        ls_ref[...] += jnp.broadcast_to(0.5 * jnp.sum(jnp.mean(err * err, axis=-1, keepdims=True), axis=0, keepdims=True), ls_ref.shape)
        dy = err * (1.0 / D)
        dy_ref[...] = dy
        df_ref[...] = _rms_bwd(dy * g, fv, r).astype(BF16)
        dg_ref[...] += jnp.sum(dy * fv * r, axis=0, keepdims=True)

    return _pcall(body, grid=(S // tm,),
                  in_specs=[_rows_spec(tm, D), _rows_spec(tm, D), _rows_spec(tm, D), _vec_spec(D)],
                  out_specs=[_rows_spec(tm, D), _rows_spec(tm, D), _vec_spec(D), _vec_spec(LANES)],
                  out_shape=[jax.ShapeDtypeStruct((S, D), F32), jax.ShapeDtypeStruct((S, D), BF16),
                             jax.ShapeDtypeStruct((1, D), F32), jax.ShapeDtypeStruct((1, LANES), F32)],
                  compiler_params=_params("arbitrary"), name=name)(x2, f, tgt, g_post)


def _mid_bwd(dy, dh2, x2, mix, g_pre, g_post, name, tm=256, after=None):
    S, D = dy.shape

    def body(dy_ref, dh_ref, x2_ref, m_ref, gn_ref, gp_ref, dx2_ref, dm_ref, dgn_ref, dgp_ref):
        i = pl.program_id(0)

        @pl.when(i == 0)
        def _():
            dgn_ref[...] = jnp.zeros_like(dgn_ref)
            dgp_ref[...] = jnp.zeros_like(dgp_ref)

        x2, dh = x2_ref[...], dh_ref[...].astype(F32)
        r = _rms_scale(x2)
        dx2 = dy_ref[...] + _rms_bwd(dh * gn_ref[...], x2, r)
        dgn_ref[...] += jnp.sum(dh * x2 * r, axis=0, keepdims=True)
        dx2_ref[...] = dx2
        m = m_ref[...]
        rm = _rms_scale(m)
        dm_ref[...] = _rms_bwd(dx2 * gp_ref[...], m, rm).astype(BF16)
        dgp_ref[...] += jnp.sum(dx2 * m * rm, axis=0, keepdims=True)

    body, in_specs, args = _following(
        after, body, [_rows_spec(tm, D)] * 4 + [_vec_spec(D)] * 2, [dy, dh2, x2, mix, g_pre, g_post])
    return _pcall(body, grid=(S // tm,), in_specs=in_specs,
                  out_specs=[_rows_spec(tm, D), _rows_spec(tm, D), _vec_spec(D), _vec_spec(D)],
                  out_shape=[jax.ShapeDtypeStruct((S, D), F32), jax.ShapeDtypeStruct((S, D), BF16),
                             jax.ShapeDtypeStruct((1, D), F32), jax.ShapeDtypeStruct((1, D), F32)],
                  compiler_params=_params("arbitrary"), name=name)(*args)


def _first_bwd(dx2, dh1, x, g_pre, name, tm=256):
    S, D = x.shape

    def body(dx2_ref, dh_ref, x_ref, g_ref, gx_ref, dg_ref):
        i = pl.program_id(0)

        @pl.when(i == 0)
        def _():
            dg_ref[...] = jnp.zeros_like(dg_ref)

        xv, dh = x_ref[...], dh_ref[...].astype(F32)
        r = _rms_scale(xv)
        gx_ref[...] = dx2_ref[...] + _rms_bwd(dh * g_ref[...], xv, r)
        dg_ref[...] += jnp.sum(dh * xv * r, axis=0, keepdims=True)

    return _pcall(body, grid=(S // tm,), in_specs=[_rows_spec(tm, D)] * 3 + [_vec_spec(D)],
                  out_specs=[_rows_spec(tm, D), _vec_spec(D)],
                  out_shape=[jax.ShapeDtypeStruct((S, D), F32), jax.ShapeDtypeStruct((1, D), F32)],
                  compiler_params=_params("arbitrary"), name=name)(dx2, dh1, x, g_pre)


def _logsig_pair(z):
    lb = jnp.minimum(z, 0.0) - jnp.log(1.0 + jnp.exp(-jnp.abs(z)))
    return lb, lb - z


SB_KEY_BLOCK = 256


def _sum_matrix(strict):
    ia = lax.broadcasted_iota(jnp.int32, (SB_KEY_BLOCK, SB_KEY_BLOCK), 0)
    ib = lax.broadcasted_iota(jnp.int32, (SB_KEY_BLOCK, SB_KEY_BLOCK), 1)
    return ((ia > ib) if strict == ">" else (ia < ib)).astype(BF16)


def _row_total(sums, v, col):
    return jnp.broadcast_to(sums[:, col:col + 1] + v[:, col:col + 1], (v.shape[0], LANES))


def _lanes(c, width):
    return jnp.tile(c, (1, width // LANES))


def _split_dot(v, u):
    hi = v.astype(BF16)
    lo = (v - hi.astype(F32)).astype(BF16)
    return _dot(hi, u, NN) + _dot(lo, u, NN)


def _head_out(o, g):
    return o * _rms_scale(o) * g


def _sb_fwd(proj, gain, n_heads, mixed_heads, name, tq=1024, after=None):
    S = proj.shape[0]
    H, tk = n_heads, SB_KEY_BLOCK
    tq = _tile(S, tq, 2 * tk)
    scale = HEAD_DIM ** -0.5

    def body(q_ref, k_ref, v_ref, g_ref, o_ref, ct_ref, mx_ref, oacc, cacc):
        i = pl.program_id(1)
        oacc[...] = jnp.zeros_like(oacc)
        cacc[...] = jnp.zeros_like(cacc)
        sums = _sum_matrix(">")

        def run(blocks):
            scored = []
            for k0, r0, diagonal in blocks:
                rows = pl.ds(r0, tq - r0)
                lb, lk = _logsig_pair(_dot(q_ref[rows, :].astype(BF16), k_ref[pl.ds(k0, tk), :].astype(BF16), NT) * scale)
                causal = None
                if diagonal:
                    causal = (lax.broadcasted_iota(jnp.int32, (tq - r0, tk), 1)
                              < lax.broadcasted_iota(jnp.int32, (tq - r0, tk), 0))
                    lk = jnp.where(causal, lk, 0.0)
                scored.append((k0, rows, causal, lb, lk))
            summed = [(k0, rows, causal, lb, lk, _split_dot(lk, sums)) for k0, rows, causal, lb, lk in scored]
            weights = []
            for k0, rows, causal, lb, lk, after in summed:
                c = cacc[rows, :]
                a = jnp.exp(lb + after + _lanes(c, tk))
                if causal is not None:
                    a = jnp.where(causal, a, 0.0)
                cacc[rows, :] = c + _row_total(after, lk, 0)
                weights.append((k0, rows, a.astype(BF16)))
            for k0, rows, a in weights:
                oacc[rows, :] += _dot(a, v_ref[pl.ds(k0, tk), :].astype(BF16), NN)

        for d in reversed(range(0, tq // tk, 2)):
            run([(pl.multiple_of(i * tq + e * tk, tk), e * tk, True) for e in (d + 1, d)])
        per_trip = tq // tk

        def step(it, carry):
            k0 = pl.multiple_of((i - 1 - it) * tq, tq)
            run([(pl.multiple_of(k0 + e * tk, tk), 0, False) for e in reversed(range(per_trip))])
            return carry

        lax.fori_loop(0, i, step, 0)
        o = oacc[...]
        o_ref[...] = o
        ct_ref[...] = cacc[...]
        mx_ref[...] = _head_out(o, g_ref[...]).astype(BF16)

    blk = pl.BlockSpec((tq, HEAD_DIM), lambda h, i: (i, h))
    body, in_specs, args = _following(
        after, body,
        [blk, pl.BlockSpec((S, HEAD_DIM), lambda h, i: (0, H + h)),
         pl.BlockSpec((S, HEAD_DIM), lambda h, i: (0, 2 * H + h)), pl.BlockSpec((1, HEAD_DIM), lambda h, i: (0, h))],
        [proj, proj, proj, gain])
    return _pcall(
        body, grid=(H, S // tq), in_specs=in_specs,
        out_specs=[blk, blk, blk],
        out_shape=[jax.ShapeDtypeStruct((S, H * HEAD_DIM), F32), jax.ShapeDtypeStruct((S, H * HEAD_DIM), F32),
                   jax.ShapeDtypeStruct((S, mixed_heads * HEAD_DIM), BF16)],
        scratch_shapes=[pltpu.VMEM((tq, HEAD_DIM), F32), pltpu.VMEM((tq, LANES), F32)],
        compiler_params=_params("parallel", "arbitrary"), name=name)(*args)


def _sb_bwd(proj, gain, o_raw, ctot, dmixed, dm_col0, n_heads, name, tq=1024, after=None):
    S = proj.shape[0]
    H, tk = n_heads, SB_KEY_BLOCK
    tq = _tile(S, tq, 2 * tk)
    nq = S // tq
    scale = HEAD_DIM ** -0.5

    def body(q_ref, k_ref, v_ref, g_ref, o_ref, ct_ref, dm_ref, dproj_ref, dg_ref,
             dkacc, dvacc, dqacc, pfx, gcar, dos, stage_q, stage_k, stage_v, out_sems):
        h, i = pl.program_id(0), pl.program_id(1)

        @pl.when(i == 0)
        def _():
            dkacc[...] = jnp.zeros_like(dkacc)
            dvacc[...] = jnp.zeros_like(dvacc)
            dg_ref[...] = jnp.zeros_like(dg_ref)

        o, dm, g = o_ref[...], dm_ref[...].astype(F32), g_ref[...]
        r = _rms_scale(o)
        dos[...] = _rms_bwd(dm * g, o, r).astype(BF16)
        dg_ref[...] += jnp.broadcast_to(jnp.sum(dm * o * r, axis=0, keepdims=True), dg_ref.shape)
        dqacc[...] = jnp.zeros_like(dqacc)
        pfx[...] = jnp.zeros_like(pfx)
        gcar[...] = jnp.zeros_like(gcar)
        later, earlier = _sum_matrix(">"), _sum_matrix("<")

        def run(blocks):
            scored = []
            for k0, r0, diagonal in blocks:
                rows, keys = pl.ds(r0, tq - r0), pl.ds(k0, tk)
                lb, lk = _logsig_pair(_dot(q_ref[rows, :].astype(BF16), k_ref[keys, :].astype(BF16), NT) * scale)
                da = _dot(dos[rows, :], v_ref[keys, :].astype(BF16), NT)
                causal = None
                if diagonal:
                    causal = (lax.broadcasted_iota(jnp.int32, (tq - r0, tk), 1)
                              < lax.broadcasted_iota(jnp.int32, (tq - r0, tk), 0))
                    lk = jnp.where(causal, lk, 0.0)
                scored.append((rows, keys, causal, lb, lk, da))
            summed = [(*blk, _split_dot(blk[4], later)) for blk in scored]
            weighted = []
            for rows, keys, causal, lb, lk, da, after in summed:
                p = pfx[rows, :] + _row_total(after, lk, 0)
                pfx[rows, :] = p
                a = jnp.exp(lb + after + _lanes(ct_ref[rows, :] - p, tk))
                if causal is not None:
                    a = jnp.where(causal, a, 0.0)
                dl = da * a
                weighted.append((rows, keys, causal, lb, a.astype(BF16), dl, _dot(dl.astype(BF16), earlier, NN)))
            cotangents = []
            for rows, keys, causal, lb, a, dl, before in weighted:
                gc = gcar[rows, :]
                gcar[rows, :] = gc + _row_total(before, dl, tk - 1)
                sig = jnp.exp(lb)
                gsum = (before + _lanes(gc, tk)) * sig
                if causal is not None:
                    gsum = jnp.where(causal, gsum, 0.0)
                cotangents.append((rows, keys, a, ((dl * (1.0 - sig) - gsum) * scale).astype(BF16)))
            for rows, keys, a, dz in cotangents:
                q, do = q_ref[rows, :].astype(BF16), dos[rows, :]
                dvacc[keys, :] += _dot(a, do, TN)
                dqacc[rows, :] += _dot(dz, k_ref[keys, :].astype(BF16), NN)
                dkacc[keys, :] += _dot(dz, q, TN)

        per_trip = tq // tk

        def step(j, carry):
            k0 = pl.multiple_of(j * tq, tq)
            run([(pl.multiple_of(k0 + e * tk, tk), 0, False) for e in range(per_trip)])
            return carry

        lax.fori_loop(0, i, step, 0)
        for d in range(0, tq // tk, 2):
            run([(pl.multiple_of(i * tq + e * tk, tk), e * tk, True) for e in (d, d + 1)])
        def columns(block):
            return pl.ds(pl.multiple_of(block * HEAD_DIM, HEAD_DIM), HEAD_DIM)

        dq_out = pltpu.make_async_copy(stage_q, dproj_ref.at[pl.ds(pl.multiple_of(i * tq, tq), tq), columns(h)], out_sems.at[0])
        dkv_out = [pltpu.make_async_copy(stage_k, dproj_ref.at[:, columns(H + h)], out_sems.at[1]),
                   pltpu.make_async_copy(stage_v, dproj_ref.at[:, columns(2 * H + h)], out_sems.at[2])]

        @pl.when((h > 0) | (i > 0))
        def _():
            dq_out.wait()

        stage_q[...] = dqacc[...].astype(BF16)
        dq_out.start()

        @pl.when(i == nq - 1)
        def _():
            @pl.when(h > 0)
            def _():
                for cp in dkv_out:
                    cp.wait()

            stage_k[...] = dkacc[...].astype(BF16)
            stage_v[...] = dvacc[...].astype(BF16)
            for cp in dkv_out:
                cp.start()

        @pl.when((h == H - 1) & (i == nq - 1))
        def _():
            dq_out.wait()
            for cp in dkv_out:
                cp.wait()

    blk = pl.BlockSpec((tq, HEAD_DIM), lambda h, i: (i, h))
    W = H * HEAD_DIM
    body, in_specs, args = _following(
        after, body,
        [blk, pl.BlockSpec((S, HEAD_DIM), lambda h, i: (0, H + h)),
         pl.BlockSpec((S, HEAD_DIM), lambda h, i: (0, 2 * H + h)), pl.BlockSpec((1, HEAD_DIM), lambda h, i: (0, h)),
         blk, blk, pl.BlockSpec((tq, HEAD_DIM), lambda h, i: (i, dm_col0 + h))],
        [proj, proj, proj, gain, o_raw, ctot, dmixed])
    return _pcall(
        body, grid=(H, nq), in_specs=in_specs,
        out_specs=[HBM, pl.BlockSpec((8, HEAD_DIM), lambda h, i: (0, h))],
        out_shape=[jax.ShapeDtypeStruct(proj.shape, BF16), jax.ShapeDtypeStruct((8, W), F32)],
        scratch_shapes=[pltpu.VMEM((S, HEAD_DIM), F32), pltpu.VMEM((S, HEAD_DIM), F32), pltpu.VMEM((tq, HEAD_DIM), F32),
                        pltpu.VMEM((tq, LANES), F32), pltpu.VMEM((tq, LANES), F32), pltpu.VMEM((tq, HEAD_DIM), BF16),
                        pltpu.VMEM((tq, HEAD_DIM), BF16), pltpu.VMEM((S, HEAD_DIM), BF16), pltpu.VMEM((S, HEAD_DIM), BF16),
                        pltpu.SemaphoreType.DMA((3,))],
        compiler_params=_params("arbitrary", "arbitrary"), name=name)(*args)


def _rope_tables(S):
    inv_freq = ROPE_THETA ** (-jnp.arange(0, HEAD_DIM, 2, dtype=F32) / HEAD_DIM)
    ang = jnp.arange(S, dtype=F32)[:, None] * inv_freq[None, :]
    cos, sin = jnp.cos(ang), jnp.sin(ang)
    return jnp.concatenate([cos, cos], axis=1), jnp.concatenate([-sin, sin], axis=1)


def _rope(v, cos2, sin_signed):
    return v * cos2 + pltpu.roll(v, HEAD_DIM // 2, axis=1) * sin_signed


def _dil_rows(d, r, l0, n):
    if d == 1:
        return pl.ds(l0 if isinstance(l0, int) else pl.multiple_of(l0, KEY_BLOCK), n)
    return pl.ds(r + d * l0, n, stride=d)


def _dil_blocks(S, visit):
    B = KEY_BLOCK
    group = 16
    for b, d in enumerate(DILATIONS):
        nb = S // d // B
        if nb == 1:
            g = math.gcd(d, group)

            def trip(t, carry, b=b, d=d, g=g):
                visit([(b, d, t * g + u, 0, True) for u in range(g)])
                return carry

            lax.fori_loop(0, d // g, trip, 0)
        elif d == 1:
            visit([(b, d, 0, 0, True)])
            g = max(k for k in range(1, group + 2) if (nb - 1) % k == 0)

            def trip(t, carry, b=b, d=d, g=g):
                visit([(b, d, 0, (1 + t * g + u) * B, False) for u in range(g)])
                return carry

            lax.fori_loop(0, (nb - 1) // g, trip, 0)
        else:
            g = math.gcd(d, max(group // nb, 1))

            def trip(t, carry, b=b, d=d, nb=nb, g=g):
                visit([(b, d, t * g + u, n * B, n == 0) for u in range(g) for n in range(nb)])
                return carry

            lax.fori_loop(0, d // g, trip, 0)


def _dil_mask(first):
    B = KEY_BLOCK
    nk = B if first else 2 * B
    iq = lax.broadcasted_iota(jnp.int32, (B, nk), 0)
    ik = lax.broadcasted_iota(jnp.int32, (B, nk), 1)
    return (ik <= iq) if first else ((ik >= iq) & (ik <= iq + B))


def _dil_fwd(proj, cos2, sin_signed, gain, mixed, col0, n_heads, name, after=None):
    S = proj.shape[0]
    H, B = n_heads, KEY_BLOCK
    scale = HEAD_DIM ** -0.5
    rc = _tile(S, 256, 8)

    def body(q_ref, k_ref, v_ref, c_ref, s_ref, g_ref, mixed_in, o_ref, l_ref, mx_ref, qr, kr, vf, *per_branch):
        ob, lb = per_branch[:len(DILATIONS)], per_branch[len(DILATIONS):]

        def rope_rows(t, carry):
            rows = pl.ds(pl.multiple_of(t * rc, rc), rc)
            qr[rows, :] = _rope(q_ref[rows, :].astype(F32), c_ref[rows, :], s_ref[rows, :])
            kr[rows, :] = _rope(k_ref[rows, :].astype(F32), c_ref[rows, :], s_ref[rows, :])
            vf[rows, :] = v_ref[rows, :].astype(F32)
            return carry

        lax.fori_loop(0, S // rc, rope_rows, 0)

        def visit(blocks):
            scores = []
            for b, d, r, l0, first in blocks:
                qrows = _dil_rows(d, r, l0, B)
                krows = qrows if first else _dil_rows(d, r, l0 - B, 2 * B)
                s = _dot(qr[qrows, :].astype(BF16), kr[krows, :].astype(BF16), NT) * scale
                scores.append((b, qrows, krows, jnp.where(_dil_mask(first), s, NEG)))
            weights = []
            for b, qrows, krows, s in scores:
                m = jnp.max(s, axis=1, keepdims=True)
                p = jnp.exp(s - m)
                den = jnp.sum(p, axis=1, keepdims=True)
                lb[b][qrows, :] = jnp.broadcast_to(m + jnp.log(den), (B, LANES))
                weights.append((b, qrows, krows, p.astype(BF16), den))
            for b, qrows, krows, p, den in weights:
                ob[b][qrows, :] = _dot(p, vf[krows, :].astype(BF16), NN) / den

        _dil_blocks(S, visit)

        def combine(t, carry):
            rows = pl.ds(pl.multiple_of(t * rc, rc), rc)
            l0, l1, l2 = lb[0][rows, :], lb[1][rows, :], lb[2][rows, :]
            m = jnp.maximum(jnp.maximum(l0, l1), l2)
            w0, w1, w2 = jnp.exp(l0 - m), jnp.exp(l1 - m), jnp.exp(l2 - m)
            den = w0 + w1 + w2
            o = (w0 * ob[0][rows, :] + w1 * ob[1][rows, :] + w2 * ob[2][rows, :]) / den
            o_ref[rows, :] = o
            l_ref[rows, :] = m + jnp.log(den)
            mx_ref[rows, :] = _head_out(o, g_ref[...]).astype(BF16)
            return carry

        lax.fori_loop(0, S // rc, combine, 0)

    def col(k):
        return pl.BlockSpec((S, HEAD_DIM), lambda h: (0, col0 + k * H + h))

    tab = pl.BlockSpec((S, HEAD_DIM), lambda h: (0, 0))
    out = pl.BlockSpec((S, HEAD_DIM), lambda h: (0, h))
    W = H * HEAD_DIM
    first = mixed.shape[1] // HEAD_DIM - H
    body, in_specs, args = _following(
        after, body, [col(0), col(1), col(2), tab, tab, pl.BlockSpec((1, HEAD_DIM), lambda h: (0, h)), HBM],
        [proj, proj, proj, cos2, sin_signed, gain, mixed])
    return _pcall(
        body, grid=(H,), in_specs=in_specs,
        out_specs=[out, out, pl.BlockSpec((S, HEAD_DIM), lambda h: (0, first + h))],
        out_shape=[jax.ShapeDtypeStruct((S, W), F32), jax.ShapeDtypeStruct((S, W), F32),
                   jax.ShapeDtypeStruct(mixed.shape, BF16)],
        input_output_aliases={6: 2},
        scratch_shapes=[pltpu.VMEM((S, HEAD_DIM), F32)] * (3 + 2 * len(DILATIONS)),
        compiler_params=_params("parallel"), name=name)(*args)


def _dil_bwd(proj, cos2, sin_signed, gain, o_raw, lse, dmixed, dproj, dm_col0, col0, n_heads, name, after=None):
    S = proj.shape[0]
    H, B = n_heads, KEY_BLOCK
    scale = HEAD_DIM ** -0.5
    rc = _tile(S, 256, 8)

    def body(q_ref, k_ref, v_ref, c_ref, s_ref, g_ref, o_ref, l_ref, dm_ref, dproj_in, dproj_ref, dg_ref,
             qr, kr, vf, dos, dsum, dqr, dkr, dvv, stage_q, stage_k, stage_v, out_sems):
        dg_ref[...] = jnp.zeros_like(dg_ref)

        def prep(t, carry):
            rows = pl.ds(pl.multiple_of(t * rc, rc), rc)
            qr[rows, :] = _rope(q_ref[rows, :].astype(F32), c_ref[rows, :], s_ref[rows, :])
            kr[rows, :] = _rope(k_ref[rows, :].astype(F32), c_ref[rows, :], s_ref[rows, :])
            vf[rows, :] = v_ref[rows, :].astype(F32)
            o, dm = o_ref[rows, :], dm_ref[rows, :].astype(F32)
            r = _rms_scale(o)
            do = _rms_bwd(dm * g_ref[...], o, r)
            dg_ref[...] += jnp.broadcast_to(jnp.sum(dm * o * r, axis=0, keepdims=True), dg_ref.shape)
            dos[rows, :] = do
            dsum[rows, :] = jnp.broadcast_to(jnp.sum(do * o, axis=1, keepdims=True), (rc, LANES))
            dqr[rows, :] = jnp.zeros((rc, HEAD_DIM), F32)
            dkr[rows, :] = jnp.zeros((rc, HEAD_DIM), F32)
            dvv[rows, :] = jnp.zeros((rc, HEAD_DIM), F32)
            return carry

        lax.fori_loop(0, S // rc, prep, 0)

        def visit(blocks):
            products = []
            for b, d, r, l0, first in blocks:
                qrows = _dil_rows(d, r, l0, B)
                krows = qrows if first else _dil_rows(d, r, l0 - B, 2 * B)
                qs, ks = qr[qrows, :].astype(BF16), kr[krows, :].astype(BF16)
                do = dos[qrows, :].astype(BF16)
                s = jnp.where(_dil_mask(first), _dot(qs, ks, NT) * scale, NEG)
                dp = _dot(do, vf[krows, :].astype(BF16), NT)
                products.append((qrows, krows, qs, ks, do, s, dp))
            cotangents = []
            for qrows, krows, qs, ks, do, s, dp in products:
                p = jnp.exp(s - l_ref[qrows, :][:, 0:1])
                ds = (p * (dp - dsum[qrows, :][:, 0:1]) * scale).astype(BF16)
                cotangents.append((qrows, krows, qs, ks, do, p.astype(BF16), ds))
            for qrows, krows, qs, ks, do, p, ds in cotangents:
                dqr[qrows, :] += _dot(ds, ks, NN)
                dkr[krows, :] += _dot(ds, qs, TN)
                dvv[krows, :] += _dot(p, do, TN)

        _dil_blocks(S, visit)

        def finish(t, carry):
            rows = pl.ds(pl.multiple_of(t * rc, rc), rc)
            c, s = c_ref[rows, :], s_ref[rows, :]
            dq, dk = dqr[rows, :], dkr[rows, :]
            stage_q[rows, :] = (dq * c + pltpu.roll(dq * s, HEAD_DIM // 2, axis=1)).astype(BF16)
            stage_k[rows, :] = (dk * c + pltpu.roll(dk * s, HEAD_DIM // 2, axis=1)).astype(BF16)
            stage_v[rows, :] = dvv[rows, :].astype(BF16)
            return carry

        h = pl.program_id(0)
        outs = [pltpu.make_async_copy(
            stage, dproj_ref.at[:, pl.ds(pl.multiple_of((col0 + k * H + h) * HEAD_DIM, HEAD_DIM), HEAD_DIM)], out_sems.at[k])
            for k, stage in enumerate((stage_q, stage_k, stage_v))]

        @pl.when(h > 0)
        def _():
            for cp in outs:
                cp.wait()

        lax.fori_loop(0, S // rc, finish, 0)
        for cp in outs:
            cp.start()

        @pl.when(h == H - 1)
        def _():
            for cp in outs:
                cp.wait()

    def col(k):
        return pl.BlockSpec((S, HEAD_DIM), lambda h: (0, col0 + k * H + h))

    tab = pl.BlockSpec((S, HEAD_DIM), lambda h: (0, 0))
    out = pl.BlockSpec((S, HEAD_DIM), lambda h: (0, h))
    W = H * HEAD_DIM
    big, half = pltpu.VMEM((S, HEAD_DIM), F32), pltpu.VMEM((S, HEAD_DIM), BF16)
    body, in_specs, args = _following(
        after, body,
        [col(0), col(1), col(2), tab, tab, pl.BlockSpec((1, HEAD_DIM), lambda h: (0, h)), out, out,
         pl.BlockSpec((S, HEAD_DIM), lambda h: (0, dm_col0 + h)), HBM],
        [proj, proj, proj, cos2, sin_signed, gain, o_raw, lse, dmixed, dproj])
    return _pcall(
        body, grid=(H,), in_specs=in_specs,
        out_specs=[HBM, pl.BlockSpec((8, HEAD_DIM), lambda h: (0, h))],
        out_shape=[jax.ShapeDtypeStruct(dproj.shape, BF16), jax.ShapeDtypeStruct((8, W), F32)],
        input_output_aliases={9: 0},
        scratch_shapes=[big, big, big, big, pltpu.VMEM((S, LANES), F32), big, big, big, half, half, half,
                        pltpu.SemaphoreType.DMA((3,))],
        compiler_params=_params("arbitrary"), name=name)(*args)


GELU_C = math.sqrt(2.0 / math.pi)
GELU_A = 0.044715
HALO = 16


def _shifts_down(cur, halo):
    row = lax.broadcasted_iota(jnp.int32, cur.shape, 0)
    first, second = row == 0, row == 1
    last, before_last = halo[HALO - 1:HALO, :], halo[HALO - 2:HALO - 1, :]
    two = jnp.where(first, before_last, jnp.where(second, last, pltpu.roll(cur, 2, axis=0)))
    return two, jnp.where(first, last, pltpu.roll(cur, 1, axis=0))


def _shift_up(cur, halo, k):
    n = cur.shape[0]
    out = pltpu.roll(cur, n - k, axis=0)
    row = lax.broadcasted_iota(jnp.int32, cur.shape, 0)
    for t in range(k):
        out = jnp.where(row == n - k + t, halo[t:t + 1, :], out)
    return out


def _conv3(cur, halo, cw):
    rows = (*_shifts_down(cur, halo), cur)
    return rows[0] * cw[0:1, :] + rows[1] * cw[1:2, :] + cur * cw[2:3, :] + cw[3:4, :], rows


def _gelu_parts(x):
    xx = x * x
    t = jnp.tanh(x * (GELU_C + (GELU_C * GELU_A) * xx))
    half = 0.5 * x
    return half + half * t, t, xx, half


def _gelu_slope(t, xx, half):
    return (0.5 + 0.5 * t) + half * (1.0 - t * t) * (GELU_C + (3.0 * GELU_C * GELU_A) * xx)


def _geglu_specs(tm, tn, ncb):
    hb = tm // HALO

    def cur(off):
        return pl.BlockSpec((tm, tn), lambda j, i: (i, off + j))

    def prev(off):
        return pl.BlockSpec((HALO, tn), lambda j, i: (jnp.maximum(i * hb - 1, 0), off + j))

    def taps(off):
        return pl.BlockSpec((8, tn), lambda j, i: (0, off + j))

    return [cur(0), prev(0), cur(ncb), prev(ncb), taps(0), taps(ncb)]


def _geglu_fwd(u, cwb, name, tm=512, tn=1408, after=None):
    S, F2 = u.shape
    F = F2 // 2
    tm, tn = _tile(S, tm, HALO), _tile(F, tn)
    ncb = F // tn

    def body(g_ref, gp_ref, v_ref, vp_ref, cg_ref, cv_ref, y_ref):
        top = pl.program_id(1) > 0
        gp = jnp.where(top, gp_ref[...].astype(F32), 0.0)
        vp = jnp.where(top, vp_ref[...].astype(F32), 0.0)
        gc = _conv3(g_ref[...].astype(F32), gp, cg_ref[...])[0]
        vc = _conv3(v_ref[...].astype(F32), vp, cv_ref[...])[0]
        y_ref[...] = (_gelu_parts(gc)[0] * vc).astype(BF16)

    body, in_specs, args = _following(after, body, _geglu_specs(tm, tn, ncb), [u, u, u, u, cwb, cwb])
    return _pcall(body, grid=(ncb, S // tm), in_specs=in_specs,
                  out_specs=pl.BlockSpec((tm, tn), lambda j, i: (i, j)),
                  out_shape=jax.ShapeDtypeStruct((S, F), BF16),
                  compiler_params=_params("parallel", "parallel"), name=name)(*args)


def _geglu_bwd(u, dy, cwb, name, tm=256, tn=1408, after=None):
    S, F2 = u.shape
    F = F2 // 2
    tm, tn = _tile(S, tm, HALO), _tile(F, tn)
    ncb = F // tn

    def body(g_ref, gp_ref, v_ref, vp_ref, cg_ref, cv_ref, dy_ref, dc_ref, dwg_ref, dwv_ref):
        i = pl.program_id(1)

        @pl.when(i == 0)
        def _():
            dwg_ref[...] = jnp.zeros_like(dwg_ref)
            dwv_ref[...] = jnp.zeros_like(dwv_ref)

        top = i > 0
        g, v = g_ref[...].astype(F32), v_ref[...].astype(F32)
        gp = jnp.where(top, gp_ref[...].astype(F32), 0.0)
        vp = jnp.where(top, vp_ref[...].astype(F32), 0.0)
        gc, g_rows = _conv3(g, gp, cg_ref[...])
        vc, v_rows = _conv3(v, vp, cv_ref[...])
        act, t, xx, half = _gelu_parts(gc)
        dact = _gelu_slope(t, xx, half)
        dyv = dy_ref[...].astype(F32)
        dgc = dyv * vc * dact
        dvc = dyv * act
        dc_ref[0] = dgc.astype(BF16)
        dc_ref[1] = dvc.astype(BF16)

        def taps(out_ref, dc, rows):
            for k, moved in enumerate(rows):
                out_ref[k:k + 1, :] += jnp.sum(dc * moved, axis=0, keepdims=True)
            out_ref[3:4, :] += jnp.sum(dc, axis=0, keepdims=True)

        taps(dwg_ref, dgc, g_rows)
        taps(dwv_ref, dvc, v_rows)

    body, in_specs, args = _following(
        after, body, _geglu_specs(tm, tn, ncb) + [pl.BlockSpec((tm, tn), lambda j, i: (i, j))], [u, u, u, u, cwb, cwb, dy])
    return _pcall(body, grid=(ncb, S // tm), in_specs=in_specs,
                  out_specs=[pl.BlockSpec((2, tm, tn), lambda j, i: (0, i, j)),
                             pl.BlockSpec((8, tn), lambda j, i: (0, j)), pl.BlockSpec((8, tn), lambda j, i: (0, j))],
                  out_shape=[jax.ShapeDtypeStruct((2, S, F), BF16), jax.ShapeDtypeStruct((8, F), F32),
                             jax.ShapeDtypeStruct((8, F), F32)],
                  compiler_params=_params("parallel", "arbitrary"), name=name)(*args)


def _conv_bwd(dc, cwb, name, tm=512, tn=1408, after=None):
    _, S, F = dc.shape
    tm, tn = _tile(S, tm, HALO), _tile(F, tn)
    ncb, nrb = F // tn, S // tm
    hb = tm // HALO

    def body(c_ref, n_ref, w_ref, du_ref):
        cur = c_ref[...].astype(F32)
        nxt = jnp.where(pl.program_id(2) < nrb - 1, n_ref[...].astype(F32), 0.0)
        w = w_ref[...]
        du = cur * w[2:3, :] + _shift_up(cur, nxt, 1) * w[1:2, :] + _shift_up(cur, nxt, 2) * w[0:1, :]
        du_ref[...] = du.astype(BF16)

    body, in_specs, args = _following(
        after, body,
        [pl.BlockSpec((None, tm, tn), lambda c, j, i: (c, i, j)),
         pl.BlockSpec((None, HALO, tn), lambda c, j, i: (c, jnp.minimum((i + 1) * hb, S // HALO - 1), j)),
         pl.BlockSpec((8, tn), lambda c, j, i: (0, c * ncb + j))], [dc, dc, cwb])
    return _pcall(body, grid=(2, ncb, nrb), in_specs=in_specs,
                  out_specs=pl.BlockSpec((tm, tn), lambda c, j, i: (i, c * ncb + j)),
                  out_shape=jax.ShapeDtypeStruct((S, 2 * F), BF16),
                  compiler_params=_params("parallel", "parallel", "parallel"), name=name)(*args)


def _adam_math(w, g, m, v):
    m = ADAM_B1 * m + (1.0 - ADAM_B1) * g
    v = ADAM_B2 * v + (1.0 - ADAM_B2) * (g * g)
    m_hat = m / (1.0 - ADAM_B1 ** ADAM_STEP)
    v_hat = v / (1.0 - ADAM_B2 ** ADAM_STEP)
    return -ADAM_LR * (m_hat / (jnp.sqrt(v_hat) + ADAM_EPS) + ADAM_WD * w), m, v


def _adamw(w, parts, m, v, name, tr=256):
    R, C = w.shape
    n, _, Cp = parts.shape
    tr = _tile(R, tr, 8)

    def body(w_ref, p_ref, m_ref, v_ref, g_out, d_out, m_out, v_out):
        g = p_ref[0, :, 0:C].astype(F32)
        for k in range(1, n):
            g = g + p_ref[k, :, 0:C].astype(F32)
        d, mn, vn = _adam_math(w_ref[...], g, m_ref[...], v_ref[...])
        g_out[...] = g
        d_out[...] = d
        m_out[...] = mn
        v_out[...] = vn

    spec = pl.BlockSpec((tr, C), lambda i: (i, 0))
    shape = jax.ShapeDtypeStruct((R, C), F32)
    return _pcall(body, grid=(R // tr,), in_specs=[spec, pl.BlockSpec((n, tr, Cp), lambda i: (0, i, 0)), spec, spec],
                  out_specs=[spec] * 4, out_shape=[shape] * 4, compiler_params=_params("parallel"), name=name)(w, parts, m, v)


def _adamw_chips(w, pair, parts, chip_ids, m, v, name, tr=256):
    R, C = w.shape
    Cp = pair.shape[2]
    by_columns = C == Cp and _tile(R, tr, 16) < 64
    tr, tc = (R, _tile(C, 256)) if by_columns else (_tile(R, tr, 16), C)

    def body(ids_ref, w_ref, own_ref, p1_ref, p2_ref, p3_ref, m_ref, v_ref, g_out, d_out, m_out, v_out):
        g = own_ref[:, 0:tc].astype(F32)
        for ref in (p1_ref, p2_ref, p3_ref):
            g = g + ref[:, 0:tc].astype(F32)
        d, mn, vn = _adam_math(w_ref[...], g, m_ref[...], v_ref[...])
        g_out[...] = g
        d_out[...] = d
        m_out[...] = mn
        v_out[...] = vn

    if by_columns:
        spec = pl.BlockSpec((tr, tc), lambda j, ids: (0, j))
    else:
        spec = pl.BlockSpec((tr, tc), lambda i, ids: (i, 0))

    def chip(k):
        if by_columns:
            return pl.BlockSpec((None, tr, tc), lambda j, ids: (ids[k], 0, j))
        return pl.BlockSpec((None, tr, Cp), lambda i, ids: (ids[k], i, 0))

    shape = jax.ShapeDtypeStruct((R, C), F32)
    grid_spec = pltpu.PrefetchScalarGridSpec(
        num_scalar_prefetch=1, grid=(C // tc if by_columns else R // tr,),
        in_specs=[spec, chip(0), chip(1), chip(2), chip(3), spec, spec], out_specs=[spec] * 4)
    return _pcall(body, grid_spec=grid_spec, out_shape=[shape] * 4, compiler_params=_params("parallel"),
                  name=name)(chip_ids, w, pair, parts, parts, parts, m, v)


def _place():
    return lax.axis_index("x"), lax.axis_index("y"), lax.axis_index("c")


def _other_chips(x, y):
    return [(1 - x, y), (x, 1 - y), (1 - x, 1 - y)]


IN_HBM = pl.BlockSpec(memory_space=pltpu.HBM)
SEM = pl.BlockSpec(memory_space=pltpu.SEMAPHORE)
EFFECT = pltpu.SideEffectType.DATAFLOW_SIDE_EFFECTING
TOKEN = jax.ShapeDtypeStruct((8, LANES), F32)
TOKEN_SPEC = pl.BlockSpec(memory_space=pltpu.VMEM)


def _in_hbm(a):
    return pltpu.with_memory_space_constraint(a, pltpu.HBM)


def _landing(shape):
    return _in_hbm(lax.empty(shape.shape, shape.dtype))


def _hbm_like(a):
    return pltpu.HBM(a.shape, a.dtype)


def _gather_places():
    x, y, c = _place()
    relay_from = (c * (1 - x) + (1 - c) * x, c * y + (1 - c) * (1 - y), c)
    relay_to = (c * x + (1 - c) * (1 - x), c * (1 - y) + (1 - c) * y, c)
    return (x, y, c), (x, y, 1 - c), (1 - x, y, c), (x, 1 - y, c), (1 - x, 1 - y, c), relay_from, relay_to


def _slot_copy(slot, ref, src, dst, send_sem, recv_sem, to):
    return pltpu.make_async_remote_copy(src_ref=slot(ref, *src), dst_ref=slot(ref, *dst), send_sem=send_sem,
                                        recv_sem=recv_sem, device_id=to, device_id_type=MESH)


def _split_call(body, arrays, sems_in, sems_out, after, name, token=True):
    na, ni, no = len(arrays), len(sems_in), len(sems_out)

    def wrapped(*refs):
        body(refs[:na], refs[na:na + ni], refs[na + ni + 1:na + ni + 1 + no])
        if token:
            refs[-1][...] = jnp.zeros_like(refs[-1])

    outs = _pcall(
        wrapped, in_specs=[IN_HBM] * na + [SEM] * ni + [HBM],
        out_specs=[SEM] * no + [IN_HBM] * na + ([TOKEN_SPEC] if token else []),
        out_shape=[pltpu.SemaphoreType.DMA((n,)) for n in sems_out] + [_hbm_like(s) for s in arrays] + ([TOKEN] if token else []),
        input_output_aliases={a: no + a for a in range(na)},
        compiler_params=pltpu.CompilerParams(has_side_effects=EFFECT), name=name,
    )(*[_in_hbm(s) for s in arrays], *sems_in, after)
    return list(outs[:no]), list(outs[no:no + na]), (outs[-1] if token else None)


def _gather_start(landing, slots, after, name):
    na = len(landing)

    def body(land, _, sems):
        me, sib, xn, yn, _, _, _ = _gather_places()
        for a in range(na):
            for k, to in enumerate((sib, xn, yn)):
                _slot_copy(slots[a], land[a], me, me, sems[0].at[3 * a + k], sems[1].at[3 * a + k], to).start()

    return _split_call(body, landing, [], [3 * na, 3 * na], after, name)


def _gather_relay(gathered, sems1, slots, after, name):
    na = len(gathered)

    def body(gath, taken, given):
        me, sib, xn, yn, _, relay_from, relay_to = _gather_places()
        for a in range(na):
            for k, peer in enumerate((sib, xn, yn)):
                arrival = _slot_copy(slots[a], gath[a], me, peer, taken[0].at[3 * a + k], taken[1].at[3 * a + k], peer)
                arrival.wait_send()
                arrival.wait_recv()
        for a in range(na):
            _slot_copy(slots[a], gath[a], relay_from, relay_from, given[0].at[a], given[1].at[a], relay_to).start()
            for k, peer in enumerate((xn, yn)):
                _slot_copy(slots[a], gath[a], peer, peer, given[2].at[2 * a + k], given[3].at[2 * a + k], sib).start()

    return _split_call(body, gathered, sems1, [na, na, 2 * na, 2 * na], after, name)


def _gather_pass(gathered, relay_sems, slots, after, name):
    na = len(gathered)

    def body(gath, taken, given):
        me, sib, xn, yn, diag, relay_from, relay_to = _gather_places()
        for a in range(na):
            _slot_copy(slots[a], gath[a], relay_from, relay_from, taken[0].at[a], taken[1].at[a], relay_to).wait_send()
            _slot_copy(slots[a], gath[a], me, diag, taken[0].at[a], taken[1].at[a], relay_to).wait_recv()
        for a in range(na):
            _slot_copy(slots[a], gath[a], diag, diag, given[0].at[a], given[1].at[a], sib).start()

    return _split_call(body, gathered, relay_sems, [na, na], after, name)


def _gather_finish(gathered, pass_sems, diag_sems, slots, after, name):
    na = len(gathered)

    def body(gath, taken, _):
        (x, y, c), sib, xn, yn, diag, _, _ = _gather_places()
        for a in range(na):
            for k, peer in enumerate((xn, yn)):
                passed = _slot_copy(slots[a], gath[a], peer, (peer[0], peer[1], 1 - c), taken[0].at[2 * a + k],
                                    taken[1].at[2 * a + k], sib)
                passed.wait_send()
                passed.wait_recv()
            passed = _slot_copy(slots[a], gath[a], diag, (diag[0], diag[1], 1 - c), taken[2].at[a], taken[3].at[a], sib)
            passed.wait_send()
            passed.wait_recv()

    return _split_call(body, gathered, list(pass_sems) + list(diag_sems), [], after, name, token=False)[1]


def _pair_copy(view, src, land, send_sems, recv_sems, chip):
    x, y, c = _place()
    return pltpu.make_async_remote_copy(
        src_ref=view(src, chip, 1 - c), dst_ref=land.at[chip], send_sem=send_sems.at[chip], recv_sem=recv_sems.at[chip],
        device_id=(x, y, 1 - c), device_id_type=MESH)


def _pair_start(grad, view, block, after, name):
    def body(src, land, after_ref, send_sems, recv_sems, src_thru, land_thru, token):
        for chip in range(N_CHIP):
            _pair_copy(view, src, land, send_sems, recv_sems, chip).start()
        token[...] = jnp.zeros_like(token)

    sems = pltpu.SemaphoreType.DMA((N_CHIP,))
    land = jax.ShapeDtypeStruct((N_CHIP, *block), BF16)
    return _pcall(
        body, in_specs=[IN_HBM, IN_HBM, HBM], out_specs=[SEM, SEM, IN_HBM, IN_HBM, TOKEN_SPEC],
        out_shape=[sems, sems, _hbm_like(grad), _hbm_like(land), TOKEN], input_output_aliases={0: 2, 1: 3},
        compiler_params=pltpu.CompilerParams(has_side_effects=EFFECT), name=name,
    )(_in_hbm(grad), _landing(land), after)


def _pair_wait(grad, recv, send_sems, recv_sems, view, after, name):
    def body(src, land, send, recv_s, after_ref, src_thru, land_thru):
        for chip in range(N_CHIP):
            copy = _pair_copy(view, src, land, send, recv_s, chip)
            copy.wait_send()
            copy.wait_recv()

    return _pcall(
        body, in_specs=[IN_HBM, IN_HBM, SEM, SEM, HBM], out_specs=[IN_HBM, IN_HBM],
        out_shape=[_hbm_like(grad), _hbm_like(recv)], input_output_aliases={0: 0, 1: 1},
        compiler_params=pltpu.CompilerParams(has_side_effects=EFFECT), name=name,
    )(grad, recv, send_sems, recv_sems, after)


def _chip_start(pair, after, name):
    def body(src, land, after_ref, send_sems, recv_sems, src_thru, land_thru, token):
        x, y, c = _place()
        for j, (px, py) in enumerate(_other_chips(x, y)):
            pltpu.make_async_remote_copy(
                src_ref=src.at[2 * px + py], dst_ref=land.at[2 * x + y], send_sem=send_sems.at[j], recv_sem=recv_sems.at[j],
                device_id=(px, py, c), device_id_type=MESH).start()
        token[...] = jnp.zeros_like(token)

    sems = pltpu.SemaphoreType.DMA((3,))
    return _pcall(
        body, in_specs=[IN_HBM, IN_HBM, HBM], out_specs=[SEM, SEM, IN_HBM, IN_HBM, TOKEN_SPEC],
        out_shape=[sems, sems, _hbm_like(pair), _hbm_like(pair), TOKEN], input_output_aliases={0: 2, 1: 3},
        compiler_params=pltpu.CompilerParams(has_side_effects=EFFECT), name=name,
    )(_in_hbm(pair), _landing(pair), after)


def _chip_wait(pair, parts, send_sems, recv_sems, after, name):
    def body(src, land, send, recv, after_ref, src_thru, land_thru):
        x, y, c = _place()
        for j, (px, py) in enumerate(_other_chips(x, y)):
            copy = pltpu.make_async_remote_copy(
                src_ref=src.at[2 * px + py], dst_ref=land.at[2 * px + py], send_sem=send.at[j], recv_sem=recv.at[j],
                device_id=(px, py, c), device_id_type=MESH)
            copy.wait_send()
            copy.wait_recv()

    return _pcall(
        body, in_specs=[IN_HBM, IN_HBM, SEM, SEM, HBM], out_specs=[IN_HBM, IN_HBM],
        out_shape=[_hbm_like(pair), _hbm_like(parts)], input_output_aliases={0: 0, 1: 1},
        compiler_params=pltpu.CompilerParams(has_side_effects=EFFECT), name=name,
    )(pair, parts, send_sems, recv_sems, after)


def _pair_add(core, grad, recv, block, grad_spec, name):
    _, R, C = recv.shape
    tr = block

    def body(c_ref, g_ref, r_ref, o_ref):
        o_ref[...] = (g_ref[...].astype(F32) + r_ref[...].astype(F32)).astype(BF16)

    grid_spec = pltpu.PrefetchScalarGridSpec(
        num_scalar_prefetch=1, grid=(N_CHIP, R // tr),
        in_specs=[grad_spec, pl.BlockSpec((None, tr, C), lambda k, i, c: (k, i, 0))],
        out_specs=pl.BlockSpec((None, tr, C), lambda k, i, c: (k, i, 0)))
    return _pcall(body, grid_spec=grid_spec, out_shape=jax.ShapeDtypeStruct(recv.shape, BF16),
                  compiler_params=_params("parallel", "parallel"), name=name)(core, grad, recv)


def _small_copies(gath, send_sems, recv_sems):
    x, y, c = _place()
    peers = [(x, y, 1 - c)] + [(px, py, pc) for px, py in _other_chips(x, y) for pc in (c, 1 - c)]
    pairs = []
    for a, ref in enumerate(gath):
        mine = ref.at[4 * x + 2 * y + c]
        for k, (px, py, pc) in enumerate(peers):
            sems = dict(send_sem=send_sems.at[7 * a + k], recv_sem=recv_sems.at[7 * a + k], device_id=(px, py, pc),
                        device_id_type=MESH)
            pairs.append((pltpu.make_async_remote_copy(src_ref=mine, dst_ref=mine, **sems),
                          pltpu.make_async_remote_copy(src_ref=mine, dst_ref=ref.at[4 * px + 2 * py + pc], **sems)))
    return pairs


def _small_start(landing, after, name):
    na = len(landing)

    def body(*refs):
        for send, _ in _small_copies(refs[:na], refs[na + 1], refs[na + 2]):
            send.start()
        refs[-1][...] = jnp.zeros_like(refs[-1])

    sems = pltpu.SemaphoreType.DMA((7 * na,))
    outs = _pcall(
        body, in_specs=[IN_HBM] * na + [HBM], out_specs=[SEM, SEM] + [IN_HBM] * na + [TOKEN_SPEC],
        out_shape=[sems, sems] + [_hbm_like(s) for s in landing] + [TOKEN],
        input_output_aliases={a: 2 + a for a in range(na)},
        compiler_params=pltpu.CompilerParams(has_side_effects=EFFECT), name=name,
    )(*[_in_hbm(s) for s in landing], after)
    return outs[0], outs[1], outs[2:2 + na], outs[-1]


def _small_wait(gathered, send_sems, recv_sems, after, name):
    na = len(gathered)

    def body(*refs):
        for send, arrival in _small_copies(refs[:na], refs[na], refs[na + 1]):
            send.wait_send()
            arrival.wait_recv()

    return list(_pcall(
        body, in_specs=[IN_HBM] * na + [SEM, SEM, HBM], out_specs=[IN_HBM] * na,
        out_shape=[_hbm_like(g) for g in gathered], input_output_aliases={a: a for a in range(na)},
        compiler_params=pltpu.CompilerParams(has_side_effects=EFFECT), name=name,
    )(*gathered, send_sems, recv_sems, after))


def _small_finish(gathered, params, name):
    na, npar = len(gathered), len(params)

    def body(*refs):
        g_refs, wmv = refs[:na], refs[na:na + 3 * npar]
        o_sums, o_params = refs[na + 3 * npar:2 * na + 3 * npar], refs[2 * na + 3 * npar:]
        sums = []
        for a in range(na):
            acc = g_refs[a][0]
            for k in range(1, N_DEV):
                acc = acc + g_refs[a][k]
            o_sums[a][...] = acc
            sums.append(acc)
        for j, (a, row, _, _, _) in enumerate(params):
            g = sums[a][row:row + 1, :]
            d, mn, vn = _adam_math(wmv[3 * j][...], g, wmv[3 * j + 1][...], wmv[3 * j + 2][...])
            for out, val in zip(o_params[4 * j:4 * j + 4], (g, d, mn, vn)):
                out[...] = val

    vm = pl.BlockSpec(memory_space=pltpu.VMEM)
    flat = [t for p in params for t in p[2:]]
    out_shape = [jax.ShapeDtypeStruct(g.shape[1:], F32) for g in gathered]
    out_shape += [jax.ShapeDtypeStruct(p[2].shape, F32) for p in params for _ in range(4)]
    outs = _pcall(body, in_specs=[vm] * (na + 3 * npar), out_specs=[vm] * len(out_shape), out_shape=out_shape,
                  name=name)(*gathered, *flat)
    return outs[:na], [outs[na + 4 * j:na + 4 * j + 4] for j in range(npar)]


def _local_step(x, tgt, gains, weights):
    g_pre_mix, g_post_mix, g_pre_ffn, g_post_ffn, g_sb, g_dil = gains
    S, D = x.shape
    hs = g_sb.shape[1] // HEAD_DIM
    hd = g_dil.shape[1] // HEAD_DIM
    cos2, sin_signed = _rope_tables(S)

    h1 = _rms_fwd(x, g_pre_mix, "rms_in", after=weights.start())
    w_in_g = weights.w_in(h1)
    proj = _mm_nn(h1, w_in_g, BF16, "proj", tn=768)
    o_sb, ct_sb, mixed = _sb_fwd(proj, g_sb, hs, hs + hd, "sb_fwd", after=weights.relay_out(proj))
    o_dl, lse_dl, mixed = _dil_fwd(proj, cos2, sin_signed, g_dil, mixed, 3 * hs, hd, "dil_fwd", after=weights.after_sb(o_sb))
    w_out_g = weights.w_out(o_dl)
    mix = _mm_nn(mixed, w_out_g, F32, "mix_out", tn=1024)
    x2, h2 = _mid_fwd(x, mix, g_post_mix, g_pre_ffn, "mid_fwd", after=weights.after_mix(mix))
    w_up_g, cwb = weights.w_up(h2)
    u = _mm_nn(h2, w_up_g, BF16, "ffn_up", b_transposed=True)
    y = _geglu_fwd(u, cwb, "geglu_fwd", after=weights.forward_down(u))
    w_down_g = weights.w_down(y)
    f = _mm_nn(y, w_down_g, F32, "ffn_down", tn=1024, tk=2816)

    dy, df, dg_post_ffn, loss = _loss_bwd(x2, f, tgt, g_post_ffn, "loss_bwd")
    dyv = _mm_nt(df, w_down_g, BF16, "d_y", tn=1408)
    dw_down = _mm_tn(y, df, D, BF16, "dw_down", tm=1408, tn=1024)
    dc, dcw_g, dcw_v = _geglu_bwd(u, dyv, cwb, "geglu_bwd", after=weights.grad("w_down", dw_down))
    du = _conv_bwd(dc, cwb, "conv_bwd", after=weights.grad_reduce("w_down", dc))
    dh2 = _mm_nt(du, w_up_g, BF16, "d_h2", tk=1408, b_transposed=True, per_step=2)
    dw_up = _mm_tn(du, h2, D, BF16, "dw_up", tm=1408, tn=1024)
    dx2, dmix, dg_pre_ffn, dg_post_mix = _mid_bwd(
        dy, dh2, x2, mix, g_pre_ffn, g_post_mix, "mid_bwd", after=weights.grad("w_up", dw_up))
    dmixed = _mm_nt(dmix, w_out_g, BF16, "d_mixed", after=weights.grad_reduce("w_up", dmix))
    dw_out = _mm_tn(mixed, dmix, D, BF16, "dw_out", tn=1024)
    dproj, dg_sb = _sb_bwd(proj, g_sb, o_sb, ct_sb, dmixed, 0, hs, "sb_bwd", after=weights.grad("w_out", dw_out))
    dproj, dg_dil = _dil_bwd(proj, cos2, sin_signed, g_dil, o_dl, lse_dl, dmixed, dproj, hs, 3 * hs, hd, "dil_bwd",
                             after=weights.grad_reduce("w_out", dg_sb))
    dw_in = _mm_tn(h1, dproj, w_in_g.shape[2], BF16, "dw_in", tn=768)
    dep = weights.grad_reduce("w_in", weights.meanwhile(weights.grad("w_in", dw_in)))
    dh1 = _mm_nt(dproj, w_in_g, BF16, "d_h1", tk=768, after=dep, per_step=4)
    grad_x, dg_pre_mix = _first_bwd(dx2, dh1, x, g_pre_mix, "first_bwd")
    small = (dg_pre_mix, dg_post_mix, dg_pre_ffn, dg_post_ffn, dg_sb[0:1], dg_dil[0:1], jnp.concatenate([dcw_g, dcw_v], axis=1))
    weights.small(small, loss)
    return loss, grad_x, small


def _pad_cols(a, to):
    return jnp.pad(a, ((0, 0), (0, to - a.shape[1])))


def kernel(x, pre_mix_gain, post_mix_gain, pre_ffn_gain, post_ffn_gain, w_in, sb_out_gain, dil_out_gain, w_out, w_up, conv_w, conv_b, w_down, loss_target, m_pre_mix_gain, m_post_mix_gain, m_pre_ffn_gain, m_post_ffn_gain, m_w_in, m_sb_out_gain, m_dil_out_gain, m_w_out, m_w_up, m_conv_w, m_conv_b, m_w_down, v_pre_mix_gain, v_post_mix_gain, v_pre_ffn_gain, v_post_ffn_gain, v_w_in, v_sb_out_gain, v_dil_out_gain, v_w_out, v_w_up, v_conv_w, v_conv_b, v_w_down):
    xb, tb = x[0], loss_target[0]
    S, D = xb.shape
    w_in, w_out, w_up, w_down, conv_w = w_in[0], w_out[0], w_up[0], w_down[0], conv_w[0]
    n_in, e_rows = w_in.shape[1], w_out.shape[0]
    cu, half = w_up.shape[1], w_down.shape[0]
    assert cu == 2 * half and half % 16 == 0
    cup = -(-cu // LANES) * LANES
    fp = N_CHIP * cup
    px, py, pc = _place()
    me = 4 * px + 2 * py + pc
    core = jnp.reshape(pc, (1,)).astype(jnp.int32)
    chip_ids = jnp.stack([2 * px + py, 2 * (1 - px) + py, 2 * px + 1 - py, 2 * (1 - px) + 1 - py]).astype(jnp.int32)

    w_up_t, m_up_t, v_up_t = (jnp.swapaxes(t, 0, 1) for t in (w_up, m_w_up[0], v_w_up[0]))

    def by_dev(ref, qx, qy, qc):
        return ref.at[4 * qx + 2 * qy + qc]

    def down_slot(ref, qx, qy, qc):
        return ref.at[2 * qx + qy, pl.ds(qc * half, half)]

    def by_pair(ref, chip, k):
        return ref.at[chip, k]

    def down_pair(ref, chip, k):
        return ref.at[chip, pl.ds(k * half, half)]

    def pair_spec(tr, cols):
        return pl.BlockSpec((None, None, tr, cols), lambda k, i, c: (k, c[0], i, 0))

    tr_in, tr_up = _tile(D, 512, 16), _tile(cup, 256, 16)
    grad_plan = {
        "w_in": ((N_CHIP, 2, D, n_in), by_pair, (D, n_in), tr_in, pair_spec(tr_in, n_in)),
        "w_out": ((N_CHIP, 2, e_rows, D), by_pair, (e_rows, D), e_rows, pair_spec(e_rows, D)),
        "w_up": ((N_CHIP, 2, cup, D), by_pair, (cup, D), tr_up, pair_spec(tr_up, D)),
        "w_down": ((N_CHIP, cup, D), down_pair, (half, D), half,
                   pl.BlockSpec((None, half, D), lambda k, i, c: (k, c[0], 0))),
    }

    class Exchanges:
        def __init__(self):
            self.in_flight = {}

        def start(self):
            def own_slot(shard):
                return lax.dynamic_update_index_in_dim(lax.empty((N_DEV, *shard.shape), shard.dtype), shard, me, 0)

            self.group_slots = {"in": [by_dev], "out": [by_dev], "up": [by_dev, by_dev], "down": [down_slot]}
            self.flight = {}
            sems, gath, token = _gather_start([own_slot(w_in.astype(BF16))], [by_dev], core, "gather_in_start")
            self.flight["in"] = (sems, gath)
            zero = token[0, 0]
            self.landing = {
                "out": [own_slot((w_out + zero).astype(BF16))],
                "up": [own_slot(jnp.pad(w_up_t + zero, ((0, cup - cu), (0, 0))).astype(BF16)),
                       own_slot(jnp.pad(conv_w + zero, ((0, 8 - conv_w.shape[0]), (0, cup - cu))))],
                "down": [lax.dynamic_update_slice(jnp.zeros((N_CHIP, cup, D), BF16), (w_down + zero).astype(BF16)[None],
                                                  (2 * px + py, pc * half, 0))]}
            return token

        def begin(self, group, after):
            sems, gath, token = _gather_start(self.landing[group], self.group_slots[group], after, "gather_%s_start" % group)
            self.flight[group] = (sems, gath)
            return token

        def relay(self, group, after):
            sems, gath = self.flight[group]
            sems, gath, token = _gather_relay(gath, sems, self.group_slots[group], after, "gather_%s_relay" % group)
            self.flight[group] = (sems, gath)
            return token

        def pass_on(self, group, after):
            sems, gath = self.flight[group]
            diag_sems, gath, token = _gather_pass(gath, sems[:2], self.group_slots[group], after, "gather_%s_pass" % group)
            self.flight[group] = (sems[2:], diag_sems, gath)
            return token

        def finish(self, group, after):
            pass_sems, diag_sems, gath = self.flight[group]
            return _gather_finish(gath, pass_sems, diag_sems, self.group_slots[group], after, "gather_%s_finish" % group)

        def w_in(self, after):
            token = self.begin("up", self.begin("out", self.relay("in", after)))
            return self.finish("in", self.pass_on("in", token))[0]

        def relay_out(self, after):
            return self.relay("out", after)

        def after_sb(self, after):
            return self.begin("down", self.relay("up", self.pass_on("out", after)))

        def w_out(self, after):
            return self.finish("out", after)[0].reshape(1, N_DEV * e_rows, D)

        def after_mix(self, after):
            return self.pass_on("up", after)

        def w_up(self, after):
            w_up_g, cw_g = self.finish("up", after)
            cb = _pad_cols(conv_b.reshape(N_DEV, cu), cup).reshape(1, 2 * fp)
            cw_full = jnp.transpose(cw_g[:, :3, :], (1, 0, 2)).reshape(3, 2 * fp)
            cwb = jnp.concatenate([cw_full, cb, jnp.zeros((4, 2 * fp), F32)], axis=0)
            return w_up_g, cwb

        def forward_down(self, after):
            return self.relay("down", after)

        def w_down(self, after):
            return self.finish("down", self.pass_on("down", after))[0].reshape(1, fp, D)

        def small(self, small, loss):
            d_pre_mix, d_post_mix, d_pre_ffn, d_post_ffn, d_sb, d_dil, d_conv = small

            def rows_of(*vectors):
                n = vectors[0].shape[1]
                row = lax.broadcasted_iota(jnp.int32, (8, n), 0)
                out = jnp.zeros((8, n), F32)
                for k, vec in enumerate(vectors):
                    out = jnp.where(row == k, vec, out)
                return out

            parts = [rows_of(d_pre_mix, d_post_mix, d_pre_ffn, d_post_ffn, jnp.broadcast_to(loss[:, :1], (1, D))),
                     rows_of(d_sb, d_dil), d_conv]
            landing = [lax.dynamic_update_index_in_dim(lax.empty((N_DEV, *p.shape), F32), p, me, 0) for p in parts]
            self.small_flight = _small_start(landing, parts[0], "small_start")

        def small_sums(self, after):
            send, recv, gath, _ = self.small_flight
            gath = _small_wait(gath, send, recv, after, "small_wait")
            params = [(0, 0, pre_mix_gain, m_pre_mix_gain, v_pre_mix_gain), (0, 1, post_mix_gain, m_post_mix_gain, v_post_mix_gain),
                      (0, 2, pre_ffn_gain, m_pre_ffn_gain, v_pre_ffn_gain), (0, 3, post_ffn_gain, m_post_ffn_gain, v_post_ffn_gain),
                      (1, 0, sb_out_gain, m_sb_out_gain, v_sb_out_gain), (1, 1, dil_out_gain, m_dil_out_gain, v_dil_out_gain)]
            (gains_sum, _, conv_sum), gain_steps = _small_finish(gath, params, "small_finish")
            return gains_sum[4, 0], conv_sum, gain_steps

        def grad(self, name, dw):
            view_shape, view, block, tr, spec = grad_plan[name]
            send, recv_sems, dw, recv, token = _pair_start(dw.reshape(view_shape), view, block, core, "pair_start_" + name)
            self.in_flight[name] = (dw, recv, send, recv_sems)
            return token

        def grad_reduce(self, name, after):
            _, view, _, tr, spec = grad_plan[name]
            dw, recv = _pair_wait(*self.in_flight[name], view, after, "pair_wait_" + name)
            pair = _pair_add(core, dw, recv, tr, spec, "pair_add_" + name)
            send, recv_sems, pair, parts, token = _chip_start(pair, recv, "chip_start_" + name)
            self.in_flight[name] = (pair, parts, send, recv_sems)
            self.last_token = token
            return token

        def meanwhile(self, token):
            self.out_w_down = _adamw_chips(w_down, *self.grad_parts("w_down", token), chip_ids, m_w_down[0], v_w_down[0],
                                           "adam_w_down")
            return self.out_w_down[1]

        def grad_parts(self, name, after):
            return _chip_wait(*self.in_flight[name], after, "chip_wait_" + name)

    exchanges = Exchanges()
    gains = (pre_mix_gain, post_mix_gain, pre_ffn_gain, post_ffn_gain, sb_out_gain, dil_out_gain)
    loss, grad_x, small = _local_step(xb, tb, gains, exchanges)


    out_w_down = exchanges.out_w_down
    out_up_t = _adamw_chips(w_up_t, *exchanges.grad_parts("w_up", exchanges.small_flight[3]), chip_ids, m_up_t, v_up_t, "adam_w_up")
    out_w_up = [jnp.swapaxes(o, 0, 1) for o in out_up_t]
    out_w_out = _adamw_chips(w_out, *exchanges.grad_parts("w_out", out_up_t[1]), chip_ids, m_w_out[0], v_w_out[0], "adam_w_out")
    out_w_in = _adamw_chips(w_in, *exchanges.grad_parts("w_in", out_w_out[1]), chip_ids, m_w_in[0], v_w_in[0], "adam_w_in")
    loss_out, g_conv, gain_steps = exchanges.small_sums(out_w_in[1])
    out_pre_mix, out_post_mix, out_pre_ffn, out_post_ffn, out_sb, out_dil = gain_steps
    g_conv_b = g_conv[3].reshape(N_DEV, cup)[:, :cu].reshape(1, N_DEV * cu)
    g_conv_w = lax.dynamic_index_in_dim(g_conv[0:3].reshape(3, N_DEV, cup), me, axis=1, keepdims=False)[:, :cu]
    out_conv_b = _adamw(conv_b, g_conv_b[None], m_conv_b, v_conv_b, "adam_conv_b")
    out_conv_w = _adamw(conv_w, g_conv_w[None], m_conv_w[0], v_conv_w[0], "adam_conv_w")

    order = [out_pre_mix, out_post_mix, out_pre_ffn, out_post_ffn, [o[None] for o in out_w_in], out_sb, out_dil,
             [o[None] for o in out_w_out], [o[None] for o in out_w_up], [o[None] for o in out_conv_w], out_conv_b,
             [o[None] for o in out_w_down]]
    outs = [loss_out, grad_x[None]]
    for k in range(4):
        outs += [o[k] for o in order]
    return tuple(outs)
```

```python
import math

import jax
import jax.numpy as jnp
from jax import lax
from jax.experimental import pallas as pl
from jax.experimental.pallas import tpu as pltpu

F32 = jnp.float32
BF16 = jnp.bfloat16
HEAD_DIM = 128
LANES = 128
KEY_BLOCK = 128
DILATIONS = (1, 4, 16)
RMS_EPS = 1e-6
ROPE_THETA = 10000.0
NEG = -1e30
ADAM_LR, ADAM_B1, ADAM_B2, ADAM_EPS, ADAM_WD, ADAM_STEP = 0.001, 0.9, 0.999, 1e-08, 0.01, 10
MESH = pl.DeviceIdType.MESH
N_DEV = 8
N_CHIP = 4
HBM = pl.BlockSpec(memory_space=pl.ANY)
VMEM_LIMIT = 56 * 1024 * 1024

_pcall = pl.pallas_call


def _tile(n, pref, mult=LANES):
    best = None
    t = mult
    while t <= min(n, pref):
        if n % t == 0:
            best = t
        t += mult
    return n if best is None else best


def _params(*sem):
    return pltpu.CompilerParams(dimension_semantics=sem, vmem_limit_bytes=VMEM_LIMIT)


def _following(after, body, in_specs, args):
    afters = [a for a in (after if isinstance(after, (list, tuple)) else [after]) if a is not None]
    n = len(args)

    def ordered(*refs):
        body(*refs[:n], *refs[n + len(afters):])

    return ordered, list(in_specs) + [HBM] * len(afters), list(args) + afters


def _dot(a, b, dims):
    return lax.dot_general(a, b, (dims, ((), ())), preferred_element_type=F32)


NN = ((1,), (0,))
NT = ((1,), (1,))
TN = ((0,), (0,))


def _mm_body(dims, nk, tile):
    if nk == 1:
        def single(a_ref, b_ref, o_ref):
            o_ref[...] = _dot(a_ref[...].astype(BF16), b_ref[...].astype(BF16), dims).astype(o_ref.dtype)

        return single, []

    def body(a_ref, b_ref, o_ref, acc_ref):
        k = pl.program_id(2)

        @pl.when(k == 0)
        def _():
            acc_ref[...] = jnp.zeros_like(acc_ref)

        acc_ref[...] += _dot(a_ref[...].astype(BF16), b_ref[...].astype(BF16), dims)

        @pl.when(k == nk - 1)
        def _():
            o_ref[...] = acc_ref[...].astype(o_ref.dtype)

    return body, [pltpu.VMEM(tile, F32)]


def _mm_nn(a, b3, out_dtype, name, tm=1024, tn=1408, tk=2048, b_transposed=False):
    M, K = a.shape
    C, n = b3.shape[0], b3.shape[1 if b_transposed else 2]
    tm, tk, tn = _tile(M, tm, 8), _tile(K, tk), _tile(n, tn)
    npc, nk = n // tn, K // tk
    body, scratch = _mm_body(NT if b_transposed else NN, nk, (tm, tn))
    b_spec = (pl.BlockSpec((None, tn, tk), lambda i, j, k: (j // npc, j % npc, k)) if b_transposed
              else pl.BlockSpec((None, tk, tn), lambda i, j, k: (j // npc, k, j % npc)))
    return _pcall(
        body, grid=(M // tm, C * npc, nk),
        in_specs=[pl.BlockSpec((tm, tk), lambda i, j, k: (i, k)), b_spec],
        out_specs=pl.BlockSpec((tm, tn), lambda i, j, k: (i, j)),
        out_shape=jax.ShapeDtypeStruct((M, C * n), out_dtype), scratch_shapes=scratch,
        compiler_params=_params("parallel", "parallel", "arbitrary"), name=name)(a, b3)


def _mm_nt(a, b3, out_dtype, name, tm=1024, tn=1024, tk=2048, after=None, b_transposed=False, per_step=1):
    M, _ = a.shape
    C, N, n = (b3.shape[0], b3.shape[2], b3.shape[1]) if b_transposed else b3.shape
    tm, tn, tk = _tile(M, tm, 8), _tile(N, tn), _tile(n, tk)
    dims = NN if b_transposed else NT
    extra = [] if after is None else [after]
    if per_step > 1 and tk == n and C % per_step == 0:
        nk, scratch = C // per_step, [pltpu.VMEM((tm, tn), F32)]
        b3 = b3.reshape(nk, per_step, *b3.shape[1:])
        a_spec = pl.BlockSpec((tm, per_step * n), lambda i, j, k: (i, k))
        if b_transposed:
            b_spec = pl.BlockSpec((None, per_step, n, tn), lambda i, j, k: (k, 0, 0, j))
        else:
            b_spec = pl.BlockSpec((None, per_step, tn, n), lambda i, j, k: (k, 0, j, 0))

        def body(a_ref, b_ref, *rest):
            o_ref, acc_ref = rest[len(extra):]
            k = pl.program_id(2)

            @pl.when(k == 0)
            def _():
                acc_ref[...] = jnp.zeros_like(acc_ref)

            b = b_ref[...].astype(BF16)
            b = b.reshape(per_step * n, tn) if b_transposed else jnp.concatenate([b[u] for u in range(per_step)], axis=1)
            acc_ref[...] += _dot(a_ref[...].astype(BF16), b, dims)

            @pl.when(k == nk - 1)
            def _():
                o_ref[...] = acc_ref[...].astype(o_ref.dtype)
    else:
        kpc = n // tk
        nk = C * kpc
        inner, scratch = _mm_body(dims, nk, (tm, tn))
        a_spec = pl.BlockSpec((tm, tk), lambda i, j, k: (i, k))
        b_spec = (pl.BlockSpec((None, tk, tn), lambda i, j, k: (k // kpc, k % kpc, j)) if b_transposed
                  else pl.BlockSpec((None, tn, tk), lambda i, j, k: (k // kpc, j, k % kpc)))

        def body(a_ref, b_ref, *rest):
            inner(a_ref, b_ref, *rest[len(extra):])

    return _pcall(
        body, grid=(M // tm, N // tn, nk), in_specs=[a_spec, b_spec] + [HBM] * len(extra),
        out_specs=pl.BlockSpec((tm, tn), lambda i, j, k: (i, j)),
        out_shape=jax.ShapeDtypeStruct((M, N), out_dtype), scratch_shapes=scratch,
        compiler_params=_params("parallel", "parallel", "arbitrary"), name=name)(a, b3, *extra)


def _mm_tn(x, y, n, out_dtype, name, tm=1024, tn=1408, tk=2048, after=None):
    S, P = x.shape
    C = y.shape[1] // n
    tm, tn, tk = _tile(P, tm), _tile(n, tn), _tile(S, tk, 8)
    npc, nk = n // tn, S // tk
    inner, scratch = _mm_body(TN, nk, (tm, tn))
    extra = [] if after is None else [after]

    def body(x_ref, y_ref, *rest):
        inner(x_ref, y_ref, *rest[len(extra):])

    return _pcall(
        body, grid=(P // tm, C * npc, nk),
        in_specs=[pl.BlockSpec((tk, tm), lambda i, j, k: (k, i)),
                  pl.BlockSpec((tk, tn), lambda i, j, k: (k, j))] + [HBM] * len(extra),
        out_specs=pl.BlockSpec((None, tm, tn), lambda i, j, k: (j // npc, i, j % npc)),
        out_shape=jax.ShapeDtypeStruct((C, P, n), out_dtype), scratch_shapes=scratch,
        compiler_params=_params("parallel", "parallel", "arbitrary"), name=name)(x, y, *extra)


def _rms_scale(v):
    return lax.rsqrt(jnp.mean(v * v, axis=-1, keepdims=True) + RMS_EPS)


def _rms_bwd(gy, v, r):
    return r * gy - v * (r * r * r * jnp.mean(gy * v, axis=-1, keepdims=True))


def _rows_spec(tm, d):
    return pl.BlockSpec((tm, d), lambda i: (i, 0))


def _vec_spec(d):
    return pl.BlockSpec((1, d), lambda i: (0, 0))


def _rms_fwd(x, g, name, tm=256, after=None):
    S, D = x.shape

    def body(x_ref, g_ref, h_ref):
        v = x_ref[...]
        h_ref[...] = (v * _rms_scale(v) * g_ref[...]).astype(BF16)

    body, in_specs, args = _following(after, body, [_rows_spec(tm, D), _vec_spec(D)], [x, g])
    return _pcall(body, grid=(S // tm,), in_specs=in_specs, out_specs=_rows_spec(tm, D),
                  out_shape=jax.ShapeDtypeStruct((S, D), BF16), compiler_params=_params("parallel"), name=name)(*args)


def _mid_fwd(x, mix, g_post, g_pre, name, tm=256, after=None):
    S, D = x.shape

    def body(x_ref, m_ref, gp_ref, gn_ref, x2_ref, h_ref):
        m = m_ref[...]
        x2 = x_ref[...] + m * _rms_scale(m) * gp_ref[...]
        x2_ref[...] = x2
        h_ref[...] = (x2 * _rms_scale(x2) * gn_ref[...]).astype(BF16)

    body, in_specs, args = _following(
        after, body, [_rows_spec(tm, D), _rows_spec(tm, D), _vec_spec(D), _vec_spec(D)], [x, mix, g_post, g_pre])
    return _pcall(body, grid=(S // tm,), in_specs=in_specs,
                  out_specs=[_rows_spec(tm, D), _rows_spec(tm, D)],
                  out_shape=[jax.ShapeDtypeStruct((S, D), F32), jax.ShapeDtypeStruct((S, D), BF16)],
                  compiler_params=_params("parallel"), name=name)(*args)


def _loss_bwd(x2, f, tgt, g_post, name, tm=256):
    S, D = x2.shape

    def body(x2_ref, f_ref, t_ref, g_ref, dy_ref, df_ref, dg_ref, ls_ref):
        i = pl.program_id(0)

        @pl.when(i == 0)
        def _():
            dg_ref[...] = jnp.zeros_like(dg_ref)
            ls_ref[...] = jnp.zeros_like(ls_ref)

        fv = f_ref[...]
        r = _rms_scale(fv)
        g = g_ref[...]
        err = x2_ref[...] + fv * r * g - t_ref[...]
        ls_ref[...] += jnp.broadcast_to(0.5 * jnp.sum(jnp.mean(err * err, axis=-1, keepdims=True), axis=0, keepdims=True), ls_ref.shape)
        dy = err * (1.0 / D)
        dy_ref[...] = dy
        df_ref[...] = _rms_bwd(dy * g, fv, r).astype(BF16)
        dg_ref[...] += jnp.sum(dy * fv * r, axis=0, keepdims=True)

    return _pcall(body, grid=(S // tm,),
                  in_specs=[_rows_spec(tm, D), _rows_spec(tm, D), _rows_spec(tm, D), _vec_spec(D)],
                  out_specs=[_rows_spec(tm, D), _rows_spec(tm, D), _vec_spec(D), _vec_spec(LANES)],
                  out_shape=[jax.ShapeDtypeStruct((S, D), F32), jax.ShapeDtypeStruct((S, D), BF16),
                             jax.ShapeDtypeStruct((1, D), F32), jax.ShapeDtypeStruct((1, LANES), F32)],
                  compiler_params=_params("arbitrary"), name=name)(x2, f, tgt, g_post)


def _mid_bwd(dy, dh2, x2, mix, g_pre, g_post, name, tm=256, after=None):
    S, D = dy.shape

    def body(dy_ref, dh_ref, x2_ref, m_ref, gn_ref, gp_ref, dx2_ref, dm_ref, dgn_ref, dgp_ref):
        i = pl.program_id(0)

        @pl.when(i == 0)
        def _():
            dgn_ref[...] = jnp.zeros_like(dgn_ref)
            dgp_ref[...] = jnp.zeros_like(dgp_ref)

        x2, dh = x2_ref[...], dh_ref[...].astype(F32)
        r = _rms_scale(x2)
        dx2 = dy_ref[...] + _rms_bwd(dh * gn_ref[...], x2, r)
        dgn_ref[...] += jnp.sum(dh * x2 * r, axis=0, keepdims=True)
        dx2_ref[...] = dx2
        m = m_ref[...]
        rm = _rms_scale(m)
        dm_ref[...] = _rms_bwd(dx2 * gp_ref[...], m, rm).astype(BF16)
        dgp_ref[...] += jnp.sum(dx2 * m * rm, axis=0, keepdims=True)

    body, in_specs, args = _following(
        after, body, [_rows_spec(tm, D)] * 4 + [_vec_spec(D)] * 2, [dy, dh2, x2, mix, g_pre, g_post])
    return _pcall(body, grid=(S // tm,), in_specs=in_specs,
                  out_specs=[_rows_spec(tm, D), _rows_spec(tm, D), _vec_spec(D), _vec_spec(D)],
                  out_shape=[jax.ShapeDtypeStruct((S, D), F32), jax.ShapeDtypeStruct((S, D), BF16),
                             jax.ShapeDtypeStruct((1, D), F32), jax.ShapeDtypeStruct((1, D), F32)],
                  compiler_params=_params("arbitrary"), name=name)(*args)


def _first_bwd(dx2, dh1, x, g_pre, name, tm=256):
    S, D = x.shape

    def body(dx2_ref, dh_ref, x_ref, g_ref, gx_ref, dg_ref):
        i = pl.program_id(0)

        @pl.when(i == 0)
        def _():
            dg_ref[...] = jnp.zeros_like(dg_ref)

        xv, dh = x_ref[...], dh_ref[...].astype(F32)
        r = _rms_scale(xv)
        gx_ref[...] = dx2_ref[...] + _rms_bwd(dh * g_ref[...], xv, r)
        dg_ref[...] += jnp.sum(dh * xv * r, axis=0, keepdims=True)

    return _pcall(body, grid=(S // tm,), in_specs=[_rows_spec(tm, D)] * 3 + [_vec_spec(D)],
                  out_specs=[_rows_spec(tm, D), _vec_spec(D)],
                  out_shape=[jax.ShapeDtypeStruct((S, D), F32), jax.ShapeDtypeStruct((1, D), F32)],
                  compiler_params=_params("arbitrary"), name=name)(dx2, dh1, x, g_pre)


def _logsig_pair(z):
    lb = jnp.minimum(z, 0.0) - jnp.log(1.0 + jnp.exp(-jnp.abs(z)))
    return lb, lb - z


SB_KEY_BLOCK = 256


def _sum_matrix(strict):
    ia = lax.broadcasted_iota(jnp.int32, (SB_KEY_BLOCK, SB_KEY_BLOCK), 0)
    ib = lax.broadcasted_iota(jnp.int32, (SB_KEY_BLOCK, SB_KEY_BLOCK), 1)
    return ((ia > ib) if strict == ">" else (ia < ib)).astype(BF16)


def _row_total(sums, v, col):
    return jnp.broadcast_to(sums[:, col:col + 1] + v[:, col:col + 1], (v.shape[0], LANES))


def _lanes(c, width):
    return jnp.tile(c, (1, width // LANES))


def _split_dot(v, u):
    hi = v.astype(BF16)
    lo = (v - hi.astype(F32)).astype(BF16)
    return _dot(hi, u, NN) + _dot(lo, u, NN)


def _head_out(o, g):
    return o * _rms_scale(o) * g


def _sb_fwd(proj, gain, n_heads, mixed_heads, name, tq=1024, after=None):
    S = proj.shape[0]
    H, tk = n_heads, SB_KEY_BLOCK
    tq = _tile(S, tq, 2 * tk)
    scale = HEAD_DIM ** -0.5

    def body(q_ref, k_ref, v_ref, g_ref, o_ref, ct_ref, mx_ref, oacc, cacc):
        i = pl.program_id(1)
        oacc[...] = jnp.zeros_like(oacc)
        cacc[...] = jnp.zeros_like(cacc)
        sums = _sum_matrix(">")

        def run(blocks):
            scored = []
            for k0, r0, diagonal in blocks:
                rows = pl.ds(r0, tq - r0)
                lb, lk = _logsig_pair(_dot(q_ref[rows, :].astype(BF16), k_ref[pl.ds(k0, tk), :].astype(BF16), NT) * scale)
                causal = None
                if diagonal:
                    causal = (lax.broadcasted_iota(jnp.int32, (tq - r0, tk), 1)
                              < lax.broadcasted_iota(jnp.int32, (tq - r0, tk), 0))
                    lk = jnp.where(causal, lk, 0.0)
                scored.append((k0, rows, causal, lb, lk))
            summed = [(k0, rows, causal, lb, lk, _split_dot(lk, sums)) for k0, rows, causal, lb, lk in scored]
            weights = []
            for k0, rows, causal, lb, lk, after in summed:
                c = cacc[rows, :]
                a = jnp.exp(lb + after + _lanes(c, tk))
                if causal is not None:
                    a = jnp.where(causal, a, 0.0)
                cacc[rows, :] = c + _row_total(after, lk, 0)
                weights.append((k0, rows, a.astype(BF16)))
            for k0, rows, a in weights:
                oacc[rows, :] += _dot(a, v_ref[pl.ds(k0, tk), :].astype(BF16), NN)

        for d in reversed(range(0, tq // tk, 2)):
            run([(pl.multiple_of(i * tq + e * tk, tk), e * tk, True) for e in (d + 1, d)])
        per_trip = tq // tk

        def step(it, carry):
            k0 = pl.multiple_of((i - 1 - it) * tq, tq)
            run([(pl.multiple_of(k0 + e * tk, tk), 0, False) for e in reversed(range(per_trip))])
            return carry

        lax.fori_loop(0, i, step, 0)
        o = oacc[...]
        o_ref[...] = o
        ct_ref[...] = cacc[...]
        mx_ref[...] = _head_out(o, g_ref[...]).astype(BF16)

    blk = pl.BlockSpec((tq, HEAD_DIM), lambda h, i: (i, h))
    body, in_specs, args = _following(
        after, body,
        [blk, pl.BlockSpec((S, HEAD_DIM), lambda h, i: (0, H + h)),
         pl.BlockSpec((S, HEAD_DIM), lambda h, i: (0, 2 * H + h)), pl.BlockSpec((1, HEAD_DIM), lambda h, i: (0, h))],
        [proj, proj, proj, gain])
    return _pcall(
        body, grid=(H, S // tq), in_specs=in_specs,
        out_specs=[blk, blk, blk],
        out_shape=[jax.ShapeDtypeStruct((S, H * HEAD_DIM), F32), jax.ShapeDtypeStruct((S, H * HEAD_DIM), F32),
                   jax.ShapeDtypeStruct((S, mixed_heads * HEAD_DIM), BF16)],
        scratch_shapes=[pltpu.VMEM((tq, HEAD_DIM), F32), pltpu.VMEM((tq, LANES), F32)],
        compiler_params=_params("parallel", "arbitrary"), name=name)(*args)


def _sb_bwd(proj, gain, o_raw, ctot, dmixed, dm_col0, n_heads, name, tq=1024, after=None):
    S = proj.shape[0]
    H, tk = n_heads, SB_KEY_BLOCK
    tq = _tile(S, tq, 2 * tk)
    nq = S // tq
    scale = HEAD_DIM ** -0.5

    def body(q_ref, k_ref, v_ref, g_ref, o_ref, ct_ref, dm_ref, dproj_ref, dg_ref,
             dkacc, dvacc, dqacc, pfx, gcar, dos, stage_q, stage_k, stage_v, out_sems):
        h, i = pl.program_id(0), pl.program_id(1)

        @pl.when(i == 0)
        def _():
            dkacc[...] = jnp.zeros_like(dkacc)
            dvacc[...] = jnp.zeros_like(dvacc)
            dg_ref[...] = jnp.zeros_like(dg_ref)

        o, dm, g = o_ref[...], dm_ref[...].astype(F32), g_ref[...]
        r = _rms_scale(o)
        dos[...] = _rms_bwd(dm * g, o, r).astype(BF16)
        dg_ref[...] += jnp.broadcast_to(jnp.sum(dm * o * r, axis=0, keepdims=True), dg_ref.shape)
        dqacc[...] = jnp.zeros_like(dqacc)
        pfx[...] = jnp.zeros_like(pfx)
        gcar[...] = jnp.zeros_like(gcar)
        later, earlier = _sum_matrix(">"), _sum_matrix("<")

        def run(blocks):
            scored = []
            for k0, r0, diagonal in blocks:
                rows, keys = pl.ds(r0, tq - r0), pl.ds(k0, tk)
                lb, lk = _logsig_pair(_dot(q_ref[rows, :].astype(BF16), k_ref[keys, :].astype(BF16), NT) * scale)
                da = _dot(dos[rows, :], v_ref[keys, :].astype(BF16), NT)
                causal = None
                if diagonal:
                    causal = (lax.broadcasted_iota(jnp.int32, (tq - r0, tk), 1)
                              < lax.broadcasted_iota(jnp.int32, (tq - r0, tk), 0))
                    lk = jnp.where(causal, lk, 0.0)
                scored.append((rows, keys, causal, lb, lk, da))
            summed = [(*blk, _split_dot(blk[4], later)) for blk in scored]
            weighted = []
            for rows, keys, causal, lb, lk, da, after in summed:
                p = pfx[rows, :] + _row_total(after, lk, 0)
                pfx[rows, :] = p
                a = jnp.exp(lb + after + _lanes(ct_ref[rows, :] - p, tk))
                if causal is not None:
                    a = jnp.where(causal, a, 0.0)
                dl = da * a
                weighted.append((rows, keys, causal, lb, a.astype(BF16), dl, _dot(dl.astype(BF16), earlier, NN)))
            cotangents = []
            for rows, keys, causal, lb, a, dl, before in weighted:
                gc = gcar[rows, :]
                gcar[rows, :] = gc + _row_total(before, dl, tk - 1)
                sig = jnp.exp(lb)
                gsum = (before + _lanes(gc, tk)) * sig
                if causal is not None:
                    gsum = jnp.where(causal, gsum, 0.0)
                cotangents.append((rows, keys, a, ((dl * (1.0 - sig) - gsum) * scale).astype(BF16)))
            for rows, keys, a, dz in cotangents:
                q, do = q_ref[rows, :].astype(BF16), dos[rows, :]
                dvacc[keys, :] += _dot(a, do, TN)
                dqacc[rows, :] += _dot(dz, k_ref[keys, :].astype(BF16), NN)
                dkacc[keys, :] += _dot(dz, q, TN)

        per_trip = tq // tk

        def step(j, carry):
            k0 = pl.multiple_of(j * tq, tq)
            run([(pl.multiple_of(k0 + e * tk, tk), 0, False) for e in range(per_trip)])
            return carry

        lax.fori_loop(0, i, step, 0)
        for d in range(0, tq // tk, 2):
            run([(pl.multiple_of(i * tq + e * tk, tk), e * tk, True) for e in (d, d + 1)])
        def columns(block):
            return pl.ds(pl.multiple_of(block * HEAD_DIM, HEAD_DIM), HEAD_DIM)

        dq_out = pltpu.make_async_copy(stage_q, dproj_ref.at[pl.ds(pl.multiple_of(i * tq, tq), tq), columns(h)], out_sems.at[0])
        dkv_out = [pltpu.make_async_copy(stage_k, dproj_ref.at[:, columns(H + h)], out_sems.at[1]),
                   pltpu.make_async_copy(stage_v, dproj_ref.at[:, columns(2 * H + h)], out_sems.at[2])]

        @pl.when((h > 0) | (i > 0))
        def _():
            dq_out.wait()

        stage_q[...] = dqacc[...].astype(BF16)
        dq_out.start()

        @pl.when(i == nq - 1)
        def _():
            @pl.when(h > 0)
            def _():
                for cp in dkv_out:
                    cp.wait()

            stage_k[...] = dkacc[...].astype(BF16)
            stage_v[...] = dvacc[...].astype(BF16)
            for cp in dkv_out:
                cp.start()

        @pl.when((h == H - 1) & (i == nq - 1))
        def _():
            dq_out.wait()
            for cp in dkv_out:
                cp.wait()

    blk = pl.BlockSpec((tq, HEAD_DIM), lambda h, i: (i, h))
    W = H * HEAD_DIM
    body, in_specs, args = _following(
        after, body,
        [blk, pl.BlockSpec((S, HEAD_DIM), lambda h, i: (0, H + h)),
         pl.BlockSpec((S, HEAD_DIM), lambda h, i: (0, 2 * H + h)), pl.BlockSpec((1, HEAD_DIM), lambda h, i: (0, h)),
         blk, blk, pl.BlockSpec((tq, HEAD_DIM), lambda h, i: (i, dm_col0 + h))],
        [proj, proj, proj, gain, o_raw, ctot, dmixed])
    return _pcall(
        body, grid=(H, nq), in_specs=in_specs,
        out_specs=[HBM, pl.BlockSpec((8, HEAD_DIM), lambda h, i: (0, h))],
        out_shape=[jax.ShapeDtypeStruct(proj.shape, BF16), jax.ShapeDtypeStruct((8, W), F32)],
        scratch_shapes=[pltpu.VMEM((S, HEAD_DIM), F32), pltpu.VMEM((S, HEAD_DIM), F32), pltpu.VMEM((tq, HEAD_DIM), F32),
                        pltpu.VMEM((tq, LANES), F32), pltpu.VMEM((tq, LANES), F32), pltpu.VMEM((tq, HEAD_DIM), BF16),
                        pltpu.VMEM((tq, HEAD_DIM), BF16), pltpu.VMEM((S, HEAD_DIM), BF16), pltpu.VMEM((S, HEAD_DIM), BF16),
                        pltpu.SemaphoreType.DMA((3,))],
        compiler_params=_params("arbitrary", "arbitrary"), name=name)(*args)


def _rope_tables(S):
    inv_freq = ROPE_THETA ** (-jnp.arange(0, HEAD_DIM, 2, dtype=F32) / HEAD_DIM)
    ang = jnp.arange(S, dtype=F32)[:, None] * inv_freq[None, :]
    cos, sin = jnp.cos(ang), jnp.sin(ang)
    return jnp.concatenate([cos, cos], axis=1), jnp.concatenate([-sin, sin], axis=1)


def _rope(v, cos2, sin_signed):
    return v * cos2 + pltpu.roll(v, HEAD_DIM // 2, axis=1) * sin_signed


def _dil_rows(d, r, l0, n):
    if d == 1:
        return pl.ds(l0 if isinstance(l0, int) else pl.multiple_of(l0, KEY_BLOCK), n)
    return pl.ds(r + d * l0, n, stride=d)


def _dil_blocks(S, visit):
    B = KEY_BLOCK
    group = 16
    for b, d in enumerate(DILATIONS):
        nb = S // d // B
        if nb == 1:
            g = math.gcd(d, group)

            def trip(t, carry, b=b, d=d, g=g):
                visit([(b, d, t * g + u, 0, True) for u in range(g)])
                return carry

            lax.fori_loop(0, d // g, trip, 0)
        elif d == 1:
            visit([(b, d, 0, 0, True)])
            g = max(k for k in range(1, group + 2) if (nb - 1) % k == 0)

            def trip(t, carry, b=b, d=d, g=g):
                visit([(b, d, 0, (1 + t * g + u) * B, False) for u in range(g)])
                return carry

            lax.fori_loop(0, (nb - 1) // g, trip, 0)
        else:
            g = math.gcd(d, max(group // nb, 1))

            def trip(t, carry, b=b, d=d, nb=nb, g=g):
                visit([(b, d, t * g + u, n * B, n == 0) for u in range(g) for n in range(nb)])
                return carry

            lax.fori_loop(0, d // g, trip, 0)


def _dil_mask(first):
    B = KEY_BLOCK
    nk = B if first else 2 * B
    iq = lax.broadcasted_iota(jnp.int32, (B, nk), 0)
    ik = lax.broadcasted_iota(jnp.int32, (B, nk), 1)
    return (ik <= iq) if first else ((ik >= iq) & (ik <= iq + B))


def _dil_fwd(proj, cos2, sin_signed, gain, mixed, col0, n_heads, name, after=None):
    S = proj.shape[0]
    H, B = n_heads, KEY_BLOCK
    scale = HEAD_DIM ** -0.5
    rc = _tile(S, 256, 8)

    def body(q_ref, k_ref, v_ref, c_ref, s_ref, g_ref, mixed_in, o_ref, l_ref, mx_ref, qr, kr, vf, *per_branch):
        ob, lb = per_branch[:len(DILATIONS)], per_branch[len(DILATIONS):]

        def rope_rows(t, carry):
            rows = pl.ds(pl.multiple_of(t * rc, rc), rc)
            qr[rows, :] = _rope(q_ref[rows, :].astype(F32), c_ref[rows, :], s_ref[rows, :])
            kr[rows, :] = _rope(k_ref[rows, :].astype(F32), c_ref[rows, :], s_ref[rows, :])
            vf[rows, :] = v_ref[rows, :].astype(F32)
            return carry

        lax.fori_loop(0, S // rc, rope_rows, 0)

        def visit(blocks):
            scores = []
            for b, d, r, l0, first in blocks:
                qrows = _dil_rows(d, r, l0, B)
                krows = qrows if first else _dil_rows(d, r, l0 - B, 2 * B)
                s = _dot(qr[qrows, :].astype(BF16), kr[krows, :].astype(BF16), NT) * scale
                scores.append((b, qrows, krows, jnp.where(_dil_mask(first), s, NEG)))
            weights = []
            for b, qrows, krows, s in scores:
                m = jnp.max(s, axis=1, keepdims=True)
                p = jnp.exp(s - m)
                den = jnp.sum(p, axis=1, keepdims=True)
                lb[b][qrows, :] = jnp.broadcast_to(m + jnp.log(den), (B, LANES))
                weights.append((b, qrows, krows, p.astype(BF16), den))
            for b, qrows, krows, p, den in weights:
                ob[b][qrows, :] = _dot(p, vf[krows, :].astype(BF16), NN) / den

        _dil_blocks(S, visit)

        def combine(t, carry):
            rows = pl.ds(pl.multiple_of(t * rc, rc), rc)
            l0, l1, l2 = lb[0][rows, :], lb[1][rows, :], lb[2][rows, :]
            m = jnp.maximum(jnp.maximum(l0, l1), l2)
            w0, w1, w2 = jnp.exp(l0 - m), jnp.exp(l1 - m), jnp.exp(l2 - m)
            den = w0 + w1 + w2
            o = (w0 * ob[0][rows, :] + w1 * ob[1][rows, :] + w2 * ob[2][rows, :]) / den
            o_ref[rows, :] = o
            l_ref[rows, :] = m + jnp.log(den)
            mx_ref[rows, :] = _head_out(o, g_ref[...]).astype(BF16)
            return carry

        lax.fori_loop(0, S // rc, combine, 0)

    def col(k):
        return pl.BlockSpec((S, HEAD_DIM), lambda h: (0, col0 + k * H + h))

    tab = pl.BlockSpec((S, HEAD_DIM), lambda h: (0, 0))
    out = pl.BlockSpec((S, HEAD_DIM), lambda h: (0, h))
    W = H * HEAD_DIM
    first = mixed.shape[1] // HEAD_DIM - H
    body, in_specs, args = _following(
        after, body, [col(0), col(1), col(2), tab, tab, pl.BlockSpec((1, HEAD_DIM), lambda h: (0, h)), HBM],
        [proj, proj, proj, cos2, sin_signed, gain, mixed])
    return _pcall(
        body, grid=(H,), in_specs=in_specs,
        out_specs=[out, out, pl.BlockSpec((S, HEAD_DIM), lambda h: (0, first + h))],
        out_shape=[jax.ShapeDtypeStruct((S, W), F32), jax.ShapeDtypeStruct((S, W), F32),
                   jax.ShapeDtypeStruct(mixed.shape, BF16)],
        input_output_aliases={6: 2},
        scratch_shapes=[pltpu.VMEM((S, HEAD_DIM), F32)] * (3 + 2 * len(DILATIONS)),
        compiler_params=_params("parallel"), name=name)(*args)


def _dil_bwd(proj, cos2, sin_signed, gain, o_raw, lse, dmixed, dproj, dm_col0, col0, n_heads, name, after=None):
    S = proj.shape[0]
    H, B = n_heads, KEY_BLOCK
    scale = HEAD_DIM ** -0.5
    rc = _tile(S, 256, 8)

    def body(q_ref, k_ref, v_ref, c_ref, s_ref, g_ref, o_ref, l_ref, dm_ref, dproj_in, dproj_ref, dg_ref,
             qr, kr, vf, dos, dsum, dqr, dkr, dvv, stage_q, stage_k, stage_v, out_sems):
        dg_ref[...] = jnp.zeros_like(dg_ref)

        def prep(t, carry):
            rows = pl.ds(pl.multiple_of(t * rc, rc), rc)
            qr[rows, :] = _rope(q_ref[rows, :].astype(F32), c_ref[rows, :], s_ref[rows, :])
            kr[rows, :] = _rope(k_ref[rows, :].astype(F32), c_ref[rows, :], s_ref[rows, :])
            vf[rows, :] = v_ref[rows, :].astype(F32)
            o, dm = o_ref[rows, :], dm_ref[rows, :].astype(F32)
            r = _rms_scale(o)
            do = _rms_bwd(dm * g_ref[...], o, r)
            dg_ref[...] += jnp.broadcast_to(jnp.sum(dm * o * r, axis=0, keepdims=True), dg_ref.shape)
            dos[rows, :] = do
            dsum[rows, :] = jnp.broadcast_to(jnp.sum(do * o, axis=1, keepdims=True), (rc, LANES))
            dqr[rows, :] = jnp.zeros((rc, HEAD_DIM), F32)
            dkr[rows, :] = jnp.zeros((rc, HEAD_DIM), F32)
            dvv[rows, :] = jnp.zeros((rc, HEAD_DIM), F32)
            return carry

        lax.fori_loop(0, S // rc, prep, 0)

        def visit(blocks):
            products = []
            for b, d, r, l0, first in blocks:
                qrows = _dil_rows(d, r, l0, B)
                krows = qrows if first else _dil_rows(d, r, l0 - B, 2 * B)
                qs, ks = qr[qrows, :].astype(BF16), kr[krows, :].astype(BF16)
                do = dos[qrows, :].astype(BF16)
                s = jnp.where(_dil_mask(first), _dot(qs, ks, NT) * scale, NEG)
                dp = _dot(do, vf[krows, :].astype(BF16), NT)
                products.append((qrows, krows, qs, ks, do, s, dp))
            cotangents = []
            for qrows, krows, qs, ks, do, s, dp in products:
                p = jnp.exp(s - l_ref[qrows, :][:, 0:1])
                ds = (p * (dp - dsum[qrows, :][:, 0:1]) * scale).astype(BF16)
                cotangents.append((qrows, krows, qs, ks, do, p.astype(BF16), ds))
            for qrows, krows, qs, ks, do, p, ds in cotangents:
                dqr[qrows, :] += _dot(ds, ks, NN)
                dkr[krows, :] += _dot(ds, qs, TN)
                dvv[krows, :] += _dot(p, do, TN)

        _dil_blocks(S, visit)

        def finish(t, carry):
            rows = pl.ds(pl.multiple_of(t * rc, rc), rc)
            c, s = c_ref[rows, :], s_ref[rows, :]
            dq, dk = dqr[rows, :], dkr[rows, :]
            stage_q[rows, :] = (dq * c + pltpu.roll(dq * s, HEAD_DIM // 2, axis=1)).astype(BF16)
            stage_k[rows, :] = (dk * c + pltpu.roll(dk * s, HEAD_DIM // 2, axis=1)).astype(BF16)
            stage_v[rows, :] = dvv[rows, :].astype(BF16)
            return carry

        h = pl.program_id(0)
        outs = [pltpu.make_async_copy(
            stage, dproj_ref.at[:, pl.ds(pl.multiple_of((col0 + k * H + h) * HEAD_DIM, HEAD_DIM), HEAD_DIM)], out_sems.at[k])
            for k, stage in enumerate((stage_q, stage_k, stage_v))]

        @pl.when(h > 0)
        def _():
            for cp in outs:
                cp.wait()

        lax.fori_loop(0, S // rc, finish, 0)
        for cp in outs:
            cp.start()

        @pl.when(h == H - 1)
        def _():
            for cp in outs:
                cp.wait()

    def col(k):
        return pl.BlockSpec((S, HEAD_DIM), lambda h: (0, col0 + k * H + h))

    tab = pl.BlockSpec((S, HEAD_DIM), lambda h: (0, 0))
    out = pl.BlockSpec((S, HEAD_DIM), lambda h: (0, h))
    W = H * HEAD_DIM
    big, half = pltpu.VMEM((S, HEAD_DIM), F32), pltpu.VMEM((S, HEAD_DIM), BF16)
    body, in_specs, args = _following(
        after, body,
        [col(0), col(1), col(2), tab, tab, pl.BlockSpec((1, HEAD_DIM), lambda h: (0, h)), out, out,
         pl.BlockSpec((S, HEAD_DIM), lambda h: (0, dm_col0 + h)), HBM],
        [proj, proj, proj, cos2, sin_signed, gain, o_raw, lse, dmixed, dproj])
    return _pcall(
        body, grid=(H,), in_specs=in_specs,
        out_specs=[HBM, pl.BlockSpec((8, HEAD_DIM), lambda h: (0, h))],
        out_shape=[jax.ShapeDtypeStruct(dproj.shape, BF16), jax.ShapeDtypeStruct((8, W), F32)],
        input_output_aliases={9: 0},
        scratch_shapes=[big, big, big, big, pltpu.VMEM((S, LANES), F32), big, big, big, half, half, half,
                        pltpu.SemaphoreType.DMA((3,))],
        compiler_params=_params("arbitrary"), name=name)(*args)


GELU_C = math.sqrt(2.0 / math.pi)
GELU_A = 0.044715
HALO = 16


def _shifts_down(cur, halo):
    row = lax.broadcasted_iota(jnp.int32, cur.shape, 0)
    first, second = row == 0, row == 1
    last, before_last = halo[HALO - 1:HALO, :], halo[HALO - 2:HALO - 1, :]
    two = jnp.where(first, before_last, jnp.where(second, last, pltpu.roll(cur, 2, axis=0)))
    return two, jnp.where(first, last, pltpu.roll(cur, 1, axis=0))


def _shift_up(cur, halo, k):
    n = cur.shape[0]
    out = pltpu.roll(cur, n - k, axis=0)
    row = lax.broadcasted_iota(jnp.int32, cur.shape, 0)
    for t in range(k):
        out = jnp.where(row == n - k + t, halo[t:t + 1, :], out)
    return out


def _conv3(cur, halo, cw):
    rows = (*_shifts_down(cur, halo), cur)
    return rows[0] * cw[0:1, :] + rows[1] * cw[1:2, :] + cur * cw[2:3, :] + cw[3:4, :], rows


def _gelu_parts(x):
    xx = x * x
    t = jnp.tanh(x * (GELU_C + (GELU_C * GELU_A) * xx))
    half = 0.5 * x
    return half + half * t, t, xx, half


def _gelu_slope(t, xx, half):
    return (0.5 + 0.5 * t) + half * (1.0 - t * t) * (GELU_C + (3.0 * GELU_C * GELU_A) * xx)


def _geglu_specs(tm, tn, ncb):
    hb = tm // HALO

    def cur(off):
        return pl.BlockSpec((tm, tn), lambda j, i: (i, off + j))

    def prev(off):
        return pl.BlockSpec((HALO, tn), lambda j, i: (jnp.maximum(i * hb - 1, 0), off + j))

    def taps(off):
        return pl.BlockSpec((8, tn), lambda j, i: (0, off + j))

    return [cur(0), prev(0), cur(ncb), prev(ncb), taps(0), taps(ncb)]


def _geglu_fwd(u, cwb, name, tm=512, tn=1408, after=None):
    S, F2 = u.shape
    F = F2 // 2
    tm, tn = _tile(S, tm, HALO), _tile(F, tn)
    ncb = F // tn

    def body(g_ref, gp_ref, v_ref, vp_ref, cg_ref, cv_ref, y_ref):
        top = pl.program_id(1) > 0
        gp = jnp.where(top, gp_ref[...].astype(F32), 0.0)
        vp = jnp.where(top, vp_ref[...].astype(F32), 0.0)
        gc = _conv3(g_ref[...].astype(F32), gp, cg_ref[...])[0]
        vc = _conv3(v_ref[...].astype(F32), vp, cv_ref[...])[0]
        y_ref[...] = (_gelu_parts(gc)[0] * vc).astype(BF16)

    body, in_specs, args = _following(after, body, _geglu_specs(tm, tn, ncb), [u, u, u, u, cwb, cwb])
    return _pcall(body, grid=(ncb, S // tm), in_specs=in_specs,
                  out_specs=pl.BlockSpec((tm, tn), lambda j, i: (i, j)),
                  out_shape=jax.ShapeDtypeStruct((S, F), BF16),
                  compiler_params=_params("parallel", "parallel"), name=name)(*args)


def _geglu_bwd(u, dy, cwb, name, tm=256, tn=1408, after=None):
    S, F2 = u.shape
    F = F2 // 2
    tm, tn = _tile(S, tm, HALO), _tile(F, tn)
    ncb = F // tn

    def body(g_ref, gp_ref, v_ref, vp_ref, cg_ref, cv_ref, dy_ref, dc_ref, dwg_ref, dwv_ref):
        i = pl.program_id(1)

        @pl.when(i == 0)
        def _():
            dwg_ref[...] = jnp.zeros_like(dwg_ref)
            dwv_ref[...] = jnp.zeros_like(dwv_ref)

        top = i > 0
        g, v = g_ref[...].astype(F32), v_ref[...].astype(F32)
        gp = jnp.where(top, gp_ref[...].astype(F32), 0.0)
        vp = jnp.where(top, vp_ref[...].astype(F32), 0.0)
        gc, g_rows = _conv3(g, gp, cg_ref[...])
        vc, v_rows = _conv3(v, vp, cv_ref[...])
        act, t, xx, half = _gelu_parts(gc)
        dact = _gelu_slope(t, xx, half)
        dyv = dy_ref[...].astype(F32)
        dgc = dyv * vc * dact
        dvc = dyv * act
        dc_ref[0] = dgc.astype(BF16)
        dc_ref[1] = dvc.astype(BF16)

        def taps(out_ref, dc, rows):
            for k, moved in enumerate(rows):
                out_ref[k:k + 1, :] += jnp.sum(dc * moved, axis=0, keepdims=True)
            out_ref[3:4, :] += jnp.sum(dc, axis=0, keepdims=True)

        taps(dwg_ref, dgc, g_rows)
        taps(dwv_ref, dvc, v_rows)

    body, in_specs, args = _following(
        after, body, _geglu_specs(tm, tn, ncb) + [pl.BlockSpec((tm, tn), lambda j, i: (i, j))], [u, u, u, u, cwb, cwb, dy])
    return _pcall(body, grid=(ncb, S // tm), in_specs=in_specs,
                  out_specs=[pl.BlockSpec((2, tm, tn), lambda j, i: (0, i, j)),
                             pl.BlockSpec((8, tn), lambda j, i: (0, j)), pl.BlockSpec((8, tn), lambda j, i: (0, j))],
                  out_shape=[jax.ShapeDtypeStruct((2, S, F), BF16), jax.ShapeDtypeStruct((8, F), F32),
                             jax.ShapeDtypeStruct((8, F), F32)],
                  compiler_params=_params("parallel", "arbitrary"), name=name)(*args)


def _conv_bwd(dc, cwb, name, tm=512, tn=1408, after=None):
    _, S, F = dc.shape
    tm, tn = _tile(S, tm, HALO), _tile(F, tn)
    ncb, nrb = F // tn, S // tm
    hb = tm // HALO

    def body(c_ref, n_ref, w_ref, du_ref):
        cur = c_ref[...].astype(F32)
        nxt = jnp.where(pl.program_id(2) < nrb - 1, n_ref[...].astype(F32), 0.0)
        w = w_ref[...]
        du = cur * w[2:3, :] + _shift_up(cur, nxt, 1) * w[1:2, :] + _shift_up(cur, nxt, 2) * w[0:1, :]
        du_ref[...] = du.astype(BF16)

    body, in_specs, args = _following(
        after, body,
        [pl.BlockSpec((None, tm, tn), lambda c, j, i: (c, i, j)),
         pl.BlockSpec((None, HALO, tn), lambda c, j, i: (c, jnp.minimum((i + 1) * hb, S // HALO - 1), j)),
         pl.BlockSpec((8, tn), lambda c, j, i: (0, c * ncb + j))], [dc, dc, cwb])
    return _pcall(body, grid=(2, ncb, nrb), in_specs=in_specs,
                  out_specs=pl.BlockSpec((tm, tn), lambda c, j, i: (i, c * ncb + j)),
                  out_shape=jax.ShapeDtypeStruct((S, 2 * F), BF16),
                  compiler_params=_params("parallel", "parallel", "parallel"), name=name)(*args)


def _adam_math(w, g, m, v):
    m = ADAM_B1 * m + (1.0 - ADAM_B1) * g
    v = ADAM_B2 * v + (1.0 - ADAM_B2) * (g * g)
    m_hat = m / (1.0 - ADAM_B1 ** ADAM_STEP)
    v_hat = v / (1.0 - ADAM_B2 ** ADAM_STEP)
    return -ADAM_LR * (m_hat / (jnp.sqrt(v_hat) + ADAM_EPS) + ADAM_WD * w), m, v


def _adamw(w, parts, m, v, name, tr=256):
    R, C = w.shape
    n, _, Cp = parts.shape
    tr = _tile(R, tr, 8)

    def body(w_ref, p_ref, m_ref, v_ref, g_out, d_out, m_out, v_out):
        g = p_ref[0, :, 0:C].astype(F32)
        for k in range(1, n):
            g = g + p_ref[k, :, 0:C].astype(F32)
        d, mn, vn = _adam_math(w_ref[...], g, m_ref[...], v_ref[...])
        g_out[...] = g
        d_out[...] = d
        m_out[...] = mn
        v_out[...] = vn

    spec = pl.BlockSpec((tr, C), lambda i: (i, 0))
    shape = jax.ShapeDtypeStruct((R, C), F32)
    return _pcall(body, grid=(R // tr,), in_specs=[spec, pl.BlockSpec((n, tr, Cp), lambda i: (0, i, 0)), spec, spec],
                  out_specs=[spec] * 4, out_shape=[shape] * 4, compiler_params=_params("parallel"), name=name)(w, parts, m, v)


def _adamw_chips(w, pair, parts, chip_ids, m, v, name, tr=256):
    R, C = w.shape
    Cp = pair.shape[2]
    by_columns = C == Cp and _tile(R, tr, 16) < 64
    tr, tc = (R, _tile(C, 256)) if by_columns else (_tile(R, tr, 16), C)

    def body(ids_ref, w_ref, own_ref, p1_ref, p2_ref, p3_ref, m_ref, v_ref, g_out, d_out, m_out, v_out):
        g = own_ref[:, 0:tc].astype(F32)
        for ref in (p1_ref, p2_ref, p3_ref):
            g = g + ref[:, 0:tc].astype(F32)
        d, mn, vn = _adam_math(w_ref[...], g, m_ref[...], v_ref[...])
        g_out[...] = g
        d_out[...] = d
        m_out[...] = mn
        v_out[...] = vn

    if by_columns:
        spec = pl.BlockSpec((tr, tc), lambda j, ids: (0, j))
    else:
        spec = pl.BlockSpec((tr, tc), lambda i, ids: (i, 0))

    def chip(k):
        if by_columns:
            return pl.BlockSpec((None, tr, tc), lambda j, ids: (ids[k], 0, j))
        return pl.BlockSpec((None, tr, Cp), lambda i, ids: (ids[k], i, 0))

    shape = jax.ShapeDtypeStruct((R, C), F32)
    grid_spec = pltpu.PrefetchScalarGridSpec(
        num_scalar_prefetch=1, grid=(C // tc if by_columns else R // tr,),
        in_specs=[spec, chip(0), chip(1), chip(2), chip(3), spec, spec], out_specs=[spec] * 4)
    return _pcall(body, grid_spec=grid_spec, out_shape=[shape] * 4, compiler_params=_params("parallel"),
                  name=name)(chip_ids, w, pair, parts, parts, parts, m, v)


def _place():
    return lax.axis_index("x"), lax.axis_index("y"), lax.axis_index("c")


def _other_chips(x, y):
    return [(1 - x, y), (x, 1 - y), (1 - x, 1 - y)]


IN_HBM = pl.BlockSpec(memory_space=pltpu.HBM)
SEM = pl.BlockSpec(memory_space=pltpu.SEMAPHORE)
EFFECT = pltpu.SideEffectType.DATAFLOW_SIDE_EFFECTING
TOKEN = jax.ShapeDtypeStruct((8, LANES), F32)
TOKEN_SPEC = pl.BlockSpec(memory_space=pltpu.VMEM)


def _in_hbm(a):
    return pltpu.with_memory_space_constraint(a, pltpu.HBM)


def _landing(shape):
    return _in_hbm(lax.empty(shape.shape, shape.dtype))


def _hbm_like(a):
    return pltpu.HBM(a.shape, a.dtype)


def _gather_places():
    x, y, c = _place()
    relay_from = (c * (1 - x) + (1 - c) * x, c * y + (1 - c) * (1 - y), c)
    relay_to = (c * x + (1 - c) * (1 - x), c * (1 - y) + (1 - c) * y, c)
    return (x, y, c), (x, y, 1 - c), (1 - x, y, c), (x, 1 - y, c), (1 - x, 1 - y, c), relay_from, relay_to


def _slot_copy(slot, ref, src, dst, send_sem, recv_sem, to):
    return pltpu.make_async_remote_copy(src_ref=slot(ref, *src), dst_ref=slot(ref, *dst), send_sem=send_sem,
                                        recv_sem=recv_sem, device_id=to, device_id_type=MESH)


def _split_call(body, arrays, sems_in, sems_out, after, name, token=True):
    na, ni, no = len(arrays), len(sems_in), len(sems_out)

    def wrapped(*refs):
        body(refs[:na], refs[na:na + ni], refs[na + ni + 1:na + ni + 1 + no])
        if token:
            refs[-1][...] = jnp.zeros_like(refs[-1])

    outs = _pcall(
        wrapped, in_specs=[IN_HBM] * na + [SEM] * ni + [HBM],
        out_specs=[SEM] * no + [IN_HBM] * na + ([TOKEN_SPEC] if token else []),
        out_shape=[pltpu.SemaphoreType.DMA((n,)) for n in sems_out] + [_hbm_like(s) for s in arrays] + ([TOKEN] if token else []),
        input_output_aliases={a: no + a for a in range(na)},
        compiler_params=pltpu.CompilerParams(has_side_effects=EFFECT), name=name,
    )(*[_in_hbm(s) for s in arrays], *sems_in, after)
    return list(outs[:no]), list(outs[no:no + na]), (outs[-1] if token else None)


def _gather_start(landing, slots, after, name):
    na = len(landing)

    def body(land, _, sems):
        me, sib, xn, yn, _, _, _ = _gather_places()
        for a in range(na):
            for k, to in enumerate((sib, xn, yn)):
                _slot_copy(slots[a], land[a], me, me, sems[0].at[3 * a + k], sems[1].at[3 * a + k], to).start()

    return _split_call(body, landing, [], [3 * na, 3 * na], after, name)


def _gather_relay(gathered, sems1, slots, after, name):
    na = len(gathered)

    def body(gath, taken, given):
        me, sib, xn, yn, _, relay_from, relay_to = _gather_places()
        for a in range(na):
            for k, peer in enumerate((sib, xn, yn)):
                arrival = _slot_copy(slots[a], gath[a], me, peer, taken[0].at[3 * a + k], taken[1].at[3 * a + k], peer)
                arrival.wait_send()
                arrival.wait_recv()
        for a in range(na):
            _slot_copy(slots[a], gath[a], relay_from, relay_from, given[0].at[a], given[1].at[a], relay_to).start()
            for k, peer in enumerate((xn, yn)):
                _slot_copy(slots[a], gath[a], peer, peer, given[2].at[2 * a + k], given[3].at[2 * a + k], sib).start()

    return _split_call(body, gathered, sems1, [na, na, 2 * na, 2 * na], after, name)


def _gather_pass(gathered, relay_sems, slots, after, name):
    na = len(gathered)

    def body(gath, taken, given):
        me, sib, xn, yn, diag, relay_from, relay_to = _gather_places()
        for a in range(na):
            _slot_copy(slots[a], gath[a], relay_from, relay_from, taken[0].at[a], taken[1].at[a], relay_to).wait_send()
            _slot_copy(slots[a], gath[a], me, diag, taken[0].at[a], taken[1].at[a], relay_to).wait_recv()
        for a in range(na):
            _slot_copy(slots[a], gath[a], diag, diag, given[0].at[a], given[1].at[a], sib).start()

    return _split_call(body, gathered, relay_sems, [na, na], after, name)


def _gather_finish(gathered, pass_sems, diag_sems, slots, after, name):
    na = len(gathered)

    def body(gath, taken, _):
        (x, y, c), sib, xn, yn, diag, _, _ = _gather_places()
        for a in range(na):
            for k, peer in enumerate((xn, yn)):
                passed = _slot_copy(slots[a], gath[a], peer, (peer[0], peer[1], 1 - c), taken[0].at[2 * a + k],
                                    taken[1].at[2 * a + k], sib)
                passed.wait_send()
                passed.wait_recv()
            passed = _slot_copy(slots[a], gath[a], diag, (diag[0], diag[1], 1 - c), taken[2].at[a], taken[3].at[a], sib)
            passed.wait_send()
            passed.wait_recv()

    return _split_call(body, gathered, list(pass_sems) + list(diag_sems), [], after, name, token=False)[1]


def _pair_copy(view, src, land, send_sems, recv_sems, chip):
    x, y, c = _place()
    return pltpu.make_async_remote_copy(
        src_ref=view(src, chip, 1 - c), dst_ref=land.at[chip], send_sem=send_sems.at[chip], recv_sem=recv_sems.at[chip],
        device_id=(x, y, 1 - c), device_id_type=MESH)


def _pair_start(grad, view, block, after, name):
    def body(src, land, after_ref, send_sems, recv_sems, src_thru, land_thru, token):
        for chip in range(N_CHIP):
            _pair_copy(view, src, land, send_sems, recv_sems, chip).start()
        token[...] = jnp.zeros_like(token)

    sems = pltpu.SemaphoreType.DMA((N_CHIP,))
    land = jax.ShapeDtypeStruct((N_CHIP, *block), BF16)
    return _pcall(
        body, in_specs=[IN_HBM, IN_HBM, HBM], out_specs=[SEM, SEM, IN_HBM, IN_HBM, TOKEN_SPEC],
        out_shape=[sems, sems, _hbm_like(grad), _hbm_like(land), TOKEN], input_output_aliases={0: 2, 1: 3},
        compiler_params=pltpu.CompilerParams(has_side_effects=EFFECT), name=name,
    )(_in_hbm(grad), _landing(land), after)


def _pair_wait(grad, recv, send_sems, recv_sems, view, after, name):
    def body(src, land, send, recv_s, after_ref, src_thru, land_thru):
        for chip in range(N_CHIP):
            copy = _pair_copy(view, src, land, send, recv_s, chip)
            copy.wait_send()
            copy.wait_recv()

    return _pcall(
        body, in_specs=[IN_HBM, IN_HBM, SEM, SEM, HBM], out_specs=[IN_HBM, IN_HBM],
        out_shape=[_hbm_like(grad), _hbm_like(recv)], input_output_aliases={0: 0, 1: 1},
        compiler_params=pltpu.CompilerParams(has_side_effects=EFFECT), name=name,
    )(grad, recv, send_sems, recv_sems, after)


def _chip_start(pair, after, name):
    def body(src, land, after_ref, send_sems, recv_sems, src_thru, land_thru, token):
        x, y, c = _place()
        for j, (px, py) in enumerate(_other_chips(x, y)):
            pltpu.make_async_remote_copy(
                src_ref=src.at[2 * px + py], dst_ref=land.at[2 * x + y], send_sem=send_sems.at[j], recv_sem=recv_sems.at[j],
                device_id=(px, py, c), device_id_type=MESH).start()
        token[...] = jnp.zeros_like(token)

    sems = pltpu.SemaphoreType.DMA((3,))
    return _pcall(
        body, in_specs=[IN_HBM, IN_HBM, HBM], out_specs=[SEM, SEM, IN_HBM, IN_HBM, TOKEN_SPEC],
        out_shape=[sems, sems, _hbm_like(pair), _hbm_like(pair), TOKEN], input_output_aliases={0: 2, 1: 3},
        compiler_params=pltpu.CompilerParams(has_side_effects=EFFECT), name=name,
    )(_in_hbm(pair), _landing(pair), after)


def _chip_wait(pair, parts, send_sems, recv_sems, after, name):
    def body(src, land, send, recv, after_ref, src_thru, land_thru):
        x, y, c = _place()
        for j, (px, py) in enumerate(_other_chips(x, y)):
            copy = pltpu.make_async_remote_copy(
                src_ref=src.at[2 * px + py], dst_ref=land.at[2 * px + py], send_sem=send.at[j], recv_sem=recv.at[j],
                device_id=(px, py, c), device_id_type=MESH)
            copy.wait_send()
            copy.wait_recv()

    return _pcall(
        body, in_specs=[IN_HBM, IN_HBM, SEM, SEM, HBM], out_specs=[IN_HBM, IN_HBM],
        out_shape=[_hbm_like(pair), _hbm_like(parts)], input_output_aliases={0: 0, 1: 1},
        compiler_params=pltpu.CompilerParams(has_side_effects=EFFECT), name=name,
    )(pair, parts, send_sems, recv_sems, after)


def _pair_add(core, grad, recv, block, grad_spec, name):
    _, R, C = recv.shape
    tr = block

    def body(c_ref, g_ref, r_ref, o_ref):
        o_ref[...] = (g_ref[...].astype(F32) + r_ref[...].astype(F32)).astype(BF16)

    grid_spec = pltpu.PrefetchScalarGridSpec(
        num_scalar_prefetch=1, grid=(N_CHIP, R // tr),
        in_specs=[grad_spec, pl.BlockSpec((None, tr, C), lambda k, i, c: (k, i, 0))],
        out_specs=pl.BlockSpec((None, tr, C), lambda k, i, c: (k, i, 0)))
    return _pcall(body, grid_spec=grid_spec, out_shape=jax.ShapeDtypeStruct(recv.shape, BF16),
                  compiler_params=_params("parallel", "parallel"), name=name)(core, grad, recv)


def _small_copies(gath, send_sems, recv_sems):
    x, y, c = _place()
    peers = [(x, y, 1 - c)] + [(px, py, pc) for px, py in _other_chips(x, y) for pc in (c, 1 - c)]
    pairs = []
    for a, ref in enumerate(gath):
        mine = ref.at[4 * x + 2 * y + c]
        for k, (px, py, pc) in enumerate(peers):
            sems = dict(send_sem=send_sems.at[7 * a + k], recv_sem=recv_sems.at[7 * a + k], device_id=(px, py, pc),
                        device_id_type=MESH)
            pairs.append((pltpu.make_async_remote_copy(src_ref=mine, dst_ref=mine, **sems),
                          pltpu.make_async_remote_copy(src_ref=mine, dst_ref=ref.at[4 * px + 2 * py + pc], **sems)))
    return pairs


def _small_start(landing, after, name):
    na = len(landing)

    def body(*refs):
        for send, _ in _small_copies(refs[:na], refs[na + 1], refs[na + 2]):
            send.start()
        refs[-1][...] = jnp.zeros_like(refs[-1])

    sems = pltpu.SemaphoreType.DMA((7 * na,))
    outs = _pcall(
        body, in_specs=[IN_HBM] * na + [HBM], out_specs=[SEM, SEM] + [IN_HBM] * na + [TOKEN_SPEC],
        out_shape=[sems, sems] + [_hbm_like(s) for s in landing] + [TOKEN],
        input_output_aliases={a: 2 + a for a in range(na)},
        compiler_params=pltpu.CompilerParams(has_side_effects=EFFECT), name=name,
    )(*[_in_hbm(s) for s in landing], after)
    return outs[0], outs[1], outs[2:2 + na], outs[-1]


def _small_wait(gathered, send_sems, recv_sems, after, name):
    na = len(gathered)

    def body(*refs):
        for send, arrival in _small_copies(refs[:na], refs[na], refs[na + 1]):
            send.wait_send()
            arrival.wait_recv()

    return list(_pcall(
        body, in_specs=[IN_HBM] * na + [SEM, SEM, HBM], out_specs=[IN_HBM] * na,
        out_shape=[_hbm_like(g) for g in gathered], input_output_aliases={a: a for a in range(na)},
        compiler_params=pltpu.CompilerParams(has_side_effects=EFFECT), name=name,
    )(*gathered, send_sems, recv_sems, after))


def _small_finish(gathered, params, name):
    na, npar = len(gathered), len(params)

    def body(*refs):
        g_refs, wmv = refs[:na], refs[na:na + 3 * npar]
        o_sums, o_params = refs[na + 3 * npar:2 * na + 3 * npar], refs[2 * na + 3 * npar:]
        sums = []
        for a in range(na):
            acc = g_refs[a][0]
            for k in range(1, N_DEV):
                acc = acc + g_refs[a][k]
            o_sums[a][...] = acc
            sums.append(acc)
        for j, (a, row, _, _, _) in enumerate(params):
            g = sums[a][row:row + 1, :]
            d, mn, vn = _adam_math(wmv[3 * j][...], g, wmv[3 * j + 1][...], wmv[3 * j + 2][...])
            for out, val in zip(o_params[4 * j:4 * j + 4], (g, d, mn, vn)):
                out[...] = val

    vm = pl.BlockSpec(memory_space=pltpu.VMEM)
    flat = [t for p in params for t in p[2:]]
    out_shape = [jax.ShapeDtypeStruct(g.shape[1:], F32) for g in gathered]
    out_shape += [jax.ShapeDtypeStruct(p[2].shape, F32) for p in params for _ in range(4)]
    outs = _pcall(body, in_specs=[vm] * (na + 3 * npar), out_specs=[vm] * len(out_shape), out_shape=out_shape,
                  name=name)(*gathered, *flat)
    return outs[:na], [outs[na + 4 * j:na + 4 * j + 4] for j in range(npar)]


def _local_step(x, tgt, gains, weights):
    g_pre_mix, g_post_mix, g_pre_ffn, g_post_ffn, g_sb, g_dil = gains
    S, D = x.shape
    hs = g_sb.shape[1] // HEAD_DIM
    hd = g_dil.shape[1] // HEAD_DIM
    cos2, sin_signed = _rope_tables(S)

    h1 = _rms_fwd(x, g_pre_mix, "rms_in", after=[weights.start(), cos2, sin_signed])
    w_in_g = weights.w_in(h1)
    proj = _mm_nn(h1, w_in_g, BF16, "proj", tn=768)
    o_sb, ct_sb, mixed = _sb_fwd(proj, g_sb, hs, hs + hd, "sb_fwd", after=weights.relay_out(proj))
    o_dl, lse_dl, mixed = _dil_fwd(proj, cos2, sin_signed, g_dil, mixed, 3 * hs, hd, "dil_fwd", after=weights.after_sb(o_sb))
    w_out_g = weights.w_out(o_dl)
    mix = _mm_nn(mixed, w_out_g, F32, "mix_out", tn=1024)
    x2, h2 = _mid_fwd(x, mix, g_post_mix, g_pre_ffn, "mid_fwd", after=weights.after_mix(mix))
    w_up_g, cwb = weights.w_up(h2)
    u = _mm_nn(h2, w_up_g, BF16, "ffn_up", b_transposed=True)
    y = _geglu_fwd(u, cwb, "geglu_fwd", after=weights.forward_down(u))
    w_down_g = weights.w_down(y)
    f = _mm_nn(y, w_down_g, F32, "ffn_down", tn=1024, tk=2816)

    dy, df, dg_post_ffn, loss = _loss_bwd(x2, f, tgt, g_post_ffn, "loss_bwd")
    dyv = _mm_nt(df, w_down_g, BF16, "d_y", tn=1408)
    dw_down = _mm_tn(y, df, D, BF16, "dw_down", tm=1408, tn=1024)
    dc, dcw_g, dcw_v = _geglu_bwd(u, dyv, cwb, "geglu_bwd", after=weights.grad("w_down", dw_down))
    du = _conv_bwd(dc, cwb, "conv_bwd", after=weights.grad_reduce("w_down", dc))
    dh2 = _mm_nt(du, w_up_g, BF16, "d_h2", tk=1408, b_transposed=True, per_step=2)
    dw_up = _mm_tn(du, h2, D, BF16, "dw_up", tm=1408, tn=1024)
    dx2, dmix, dg_pre_ffn, dg_post_mix = _mid_bwd(
        dy, dh2, x2, mix, g_pre_ffn, g_post_mix, "mid_bwd", after=weights.grad("w_up", dw_up))
    dmixed = _mm_nt(dmix, w_out_g, BF16, "d_mixed", after=weights.grad_reduce("w_up", dmix))
    dw_out = _mm_tn(mixed, dmix, D, BF16, "dw_out", tn=1024)
    dproj, dg_sb = _sb_bwd(proj, g_sb, o_sb, ct_sb, dmixed, 0, hs, "sb_bwd", after=weights.grad("w_out", dw_out))
    dproj, dg_dil = _dil_bwd(proj, cos2, sin_signed, g_dil, o_dl, lse_dl, dmixed, dproj, hs, 3 * hs, hd, "dil_bwd",
                             after=weights.grad_reduce("w_out", dg_sb))
    dw_in = _mm_tn(h1, dproj, w_in_g.shape[2], BF16, "dw_in", tn=768)
    dep = weights.grad_reduce("w_in", weights.meanwhile(weights.grad("w_in", dw_in)))
    dh1 = _mm_nt(dproj, w_in_g, BF16, "d_h1", tk=768, after=dep, per_step=4)
    grad_x, dg_pre_mix = _first_bwd(dx2, dh1, x, g_pre_mix, "first_bwd")
    small = (dg_pre_mix, dg_post_mix, dg_pre_ffn, dg_post_ffn, dg_sb[0:1], dg_dil[0:1], jnp.concatenate([dcw_g, dcw_v], axis=1))
    weights.small(small, loss)
    return loss, grad_x, small


def _pad_cols(a, to):
    return jnp.pad(a, ((0, 0), (0, to - a.shape[1])))


def kernel(x, pre_mix_gain, post_mix_gain, pre_ffn_gain, post_ffn_gain, w_in, sb_out_gain, dil_out_gain, w_out, w_up, conv_w, conv_b, w_down, loss_target, m_pre_mix_gain, m_post_mix_gain, m_pre_ffn_gain, m_post_ffn_gain, m_w_in, m_sb_out_gain, m_dil_out_gain, m_w_out, m_w_up, m_conv_w, m_conv_b, m_w_down, v_pre_mix_gain, v_post_mix_gain, v_pre_ffn_gain, v_post_ffn_gain, v_w_in, v_sb_out_gain, v_dil_out_gain, v_w_out, v_w_up, v_conv_w, v_conv_b, v_w_down):
    xb, tb = x[0], loss_target[0]
    S, D = xb.shape
    w_in, w_out, w_up, w_down, conv_w = w_in[0], w_out[0], w_up[0], w_down[0], conv_w[0]
    n_in, e_rows = w_in.shape[1], w_out.shape[0]
    cu, half = w_up.shape[1], w_down.shape[0]
    assert cu == 2 * half and half % 16 == 0
    cup = -(-cu // LANES) * LANES
    fp = N_CHIP * cup
    px, py, pc = _place()
    me = 4 * px + 2 * py + pc
    core = jnp.reshape(pc, (1,)).astype(jnp.int32)
    chip_ids = jnp.stack([2 * px + py, 2 * (1 - px) + py, 2 * px + 1 - py, 2 * (1 - px) + 1 - py]).astype(jnp.int32)

    w_up_t, m_up_t, v_up_t = (jnp.swapaxes(t, 0, 1) for t in (w_up, m_w_up[0], v_w_up[0]))

    def by_dev(ref, qx, qy, qc):
        return ref.at[4 * qx + 2 * qy + qc]

    def down_slot(ref, qx, qy, qc):
        return ref.at[2 * qx + qy, pl.ds(qc * half, half)]

    def by_pair(ref, chip, k):
        return ref.at[chip, k]

    def down_pair(ref, chip, k):
        return ref.at[chip, pl.ds(k * half, half)]

    def pair_spec(tr, cols):
        return pl.BlockSpec((None, None, tr, cols), lambda k, i, c: (k, c[0], i, 0))

    tr_in, tr_up = _tile(D, 512, 16), _tile(cup, 256, 16)
    grad_plan = {
        "w_in": ((N_CHIP, 2, D, n_in), by_pair, (D, n_in), tr_in, pair_spec(tr_in, n_in)),
        "w_out": ((N_CHIP, 2, e_rows, D), by_pair, (e_rows, D), e_rows, pair_spec(e_rows, D)),
        "w_up": ((N_CHIP, 2, cup, D), by_pair, (cup, D), tr_up, pair_spec(tr_up, D)),
        "w_down": ((N_CHIP, cup, D), down_pair, (half, D), half,
                   pl.BlockSpec((None, half, D), lambda k, i, c: (k, c[0], 0))),
    }

    class Exchanges:
        def __init__(self):
            self.in_flight = {}

        def start(self):
            def own_slot(shard):
                return lax.dynamic_update_index_in_dim(lax.empty((N_DEV, *shard.shape), shard.dtype), shard, me, 0)

            self.group_slots = {"in": [by_dev], "out": [by_dev], "up": [by_dev, by_dev], "down": [down_slot]}
            self.flight = {}
            sems, gath, token = _gather_start([own_slot(w_in.astype(BF16))], [by_dev], core, "gather_in_start")
            self.flight["in"] = (sems, gath)
            zero = token[0, 0]
            self.landing = {
                "out": [own_slot((w_out + zero).astype(BF16))],
                "up": [own_slot(jnp.pad(w_up_t + zero, ((0, cup - cu), (0, 0))).astype(BF16)),
                       own_slot(jnp.pad(conv_w + zero, ((0, 8 - conv_w.shape[0]), (0, cup - cu))))],
                "down": [lax.dynamic_update_slice(jnp.zeros((N_CHIP, cup, D), BF16), (w_down + zero).astype(BF16)[None],
                                                  (2 * px + py, pc * half, 0))]}
            return self.landing["down"][0]

        def begin(self, group, after):
            sems, gath, token = _gather_start(self.landing[group], self.group_slots[group], after, "gather_%s_start" % group)
            self.flight[group] = (sems, gath)
            return token

        def relay(self, group, after):
            sems, gath = self.flight[group]
            sems, gath, token = _gather_relay(gath, sems, self.group_slots[group], after, "gather_%s_relay" % group)
            self.flight[group] = (sems, gath)
            return token

        def pass_on(self, group, after):
            sems, gath = self.flight[group]
            diag_sems, gath, token = _gather_pass(gath, sems[:2], self.group_slots[group], after, "gather_%s_pass" % group)
            self.flight[group] = (sems[2:], diag_sems, gath)
            return token

        def finish(self, group, after):
            pass_sems, diag_sems, gath = self.flight[group]
            return _gather_finish(gath, pass_sems, diag_sems, self.group_slots[group], after, "gather_%s_finish" % group)

        def w_in(self, after):
            token = self.begin("up", self.begin("out", self.relay("in", after)))
            return self.finish("in", self.pass_on("in", token))[0]

        def relay_out(self, after):
            return self.relay("out", after)

        def after_sb(self, after):
            return self.begin("down", self.relay("up", self.pass_on("out", after)))

        def w_out(self, after):
            return self.finish("out", after)[0].reshape(1, N_DEV * e_rows, D)

        def after_mix(self, after):
            return self.pass_on("up", after)

        def w_up(self, after):
            w_up_g, cw_g = self.finish("up", after)
            cb = _pad_cols(conv_b.reshape(N_DEV, cu), cup).reshape(1, 2 * fp)
            cw_full = jnp.transpose(cw_g[:, :3, :], (1, 0, 2)).reshape(3, 2 * fp)
            cwb = jnp.concatenate([cw_full, cb, jnp.zeros((4, 2 * fp), F32)], axis=0)
            return w_up_g, cwb

        def forward_down(self, after):
            return self.relay("down", after)

        def w_down(self, after):
            return self.finish("down", self.pass_on("down", after))[0].reshape(1, fp, D)

        def small(self, small, loss):
            d_pre_mix, d_post_mix, d_pre_ffn, d_post_ffn, d_sb, d_dil, d_conv = small

            def rows_of(*vectors):
                n = vectors[0].shape[1]
                row = lax.broadcasted_iota(jnp.int32, (8, n), 0)
                out = jnp.zeros((8, n), F32)
                for k, vec in enumerate(vectors):
                    out = jnp.where(row == k, vec, out)
                return out

            parts = [rows_of(d_pre_mix, d_post_mix, d_pre_ffn, d_post_ffn, jnp.broadcast_to(loss[:, :1], (1, D))),
                     rows_of(d_sb, d_dil), d_conv]
            landing = [lax.dynamic_update_index_in_dim(lax.empty((N_DEV, *p.shape), F32), p, me, 0) for p in parts]
            self.small_flight = _small_start(landing, parts[0], "small_start")

        def small_sums(self, after):
            send, recv, gath, _ = self.small_flight
            gath = _small_wait(gath, send, recv, after, "small_wait")
            params = [(0, 0, pre_mix_gain, m_pre_mix_gain, v_pre_mix_gain), (0, 1, post_mix_gain, m_post_mix_gain, v_post_mix_gain),
                      (0, 2, pre_ffn_gain, m_pre_ffn_gain, v_pre_ffn_gain), (0, 3, post_ffn_gain, m_post_ffn_gain, v_post_ffn_gain),
                      (1, 0, sb_out_gain, m_sb_out_gain, v_sb_out_gain), (1, 1, dil_out_gain, m_dil_out_gain, v_dil_out_gain)]
            (gains_sum, _, conv_sum), gain_steps = _small_finish(gath, params, "small_finish")
            return gains_sum[4, 0], conv_sum, gain_steps

        def grad(self, name, dw):
            view_shape, view, block, tr, spec = grad_plan[name]
            send, recv_sems, dw, recv, token = _pair_start(dw.reshape(view_shape), view, block, core, "pair_start_" + name)
            self.in_flight[name] = (dw, recv, send, recv_sems)
            return token

        def grad_reduce(self, name, after):
            _, view, _, tr, spec = grad_plan[name]
            dw, recv = _pair_wait(*self.in_flight[name], view, after, "pair_wait_" + name)
            pair = _pair_add(core, dw, recv, tr, spec, "pair_add_" + name)
            send, recv_sems, pair, parts, token = _chip_start(pair, recv, "chip_start_" + name)
            self.in_flight[name] = (pair, parts, send, recv_sems)
            self.last_token = token
            return token

        def meanwhile(self, token):
            self.out_w_down = _adamw_chips(w_down, *self.grad_parts("w_down", token), chip_ids, m_w_down[0], v_w_down[0],
                                           "adam_w_down")
            return self.out_w_down[1]

        def grad_parts(self, name, after):
            return _chip_wait(*self.in_flight[name], after, "chip_wait_" + name)

    exchanges = Exchanges()
    gains = (pre_mix_gain, post_mix_gain, pre_ffn_gain, post_ffn_gain, sb_out_gain, dil_out_gain)
    loss, grad_x, small = _local_step(xb, tb, gains, exchanges)


    out_w_down = exchanges.out_w_down
    out_up_t = _adamw_chips(w_up_t, *exchanges.grad_parts("w_up", exchanges.small_flight[3]), chip_ids, m_up_t, v_up_t, "adam_w_up")
    out_w_up = [jnp.swapaxes(o, 0, 1) for o in out_up_t]
    out_w_out = _adamw_chips(w_out, *exchanges.grad_parts("w_out", out_up_t[1]), chip_ids, m_w_out[0], v_w_out[0], "adam_w_out")
    out_w_in = _adamw_chips(w_in, *exchanges.grad_parts("w_in", out_w_out[1]), chip_ids, m_w_in[0], v_w_in[0], "adam_w_in")
    loss_out, g_conv, gain_steps = exchanges.small_sums(out_w_in[1])
    out_pre_mix, out_post_mix, out_pre_ffn, out_post_ffn, out_sb, out_dil = gain_steps
    g_conv_b = g_conv[3].reshape(N_DEV, cup)[:, :cu].reshape(1, N_DEV * cu)
    g_conv_w = lax.dynamic_index_in_dim(g_conv[0:3].reshape(3, N_DEV, cup), me, axis=1, keepdims=False)[:, :cu]
    out_conv_b = _adamw(conv_b, g_conv_b[None], m_conv_b, v_conv_b, "adam_conv_b")
    out_conv_w = _adamw(conv_w, g_conv_w[None], m_conv_w[0], v_conv_w[0], "adam_conv_w")

    order = [out_pre_mix, out_post_mix, out_pre_ffn, out_post_ffn, [o[None] for o in out_w_in], out_sb, out_dil,
             [o[None] for o in out_w_out], [o[None] for o in out_w_up], [o[None] for o in out_conv_w], out_conv_b,
             [o[None] for o in out_w_down]]
    outs = [loss_out, grad_x[None]]
    for k in range(4):
        outs += [o[k] for o in order]
    return tuple(outs)
```

```python
import math

import jax
import jax.numpy as jnp
from jax import lax
from jax.experimental import pallas as pl
from jax.experimental.pallas import tpu as pltpu

F32 = jnp.float32
BF16 = jnp.bfloat16
HEAD_DIM = 128
LANES = 128
KEY_BLOCK = 128
DILATIONS = (1, 4, 16)
RMS_EPS = 1e-6
ROPE_THETA = 10000.0
NEG = -1e30
ADAM_LR, ADAM_B1, ADAM_B2, ADAM_EPS, ADAM_WD, ADAM_STEP = 0.001, 0.9, 0.999, 1e-08, 0.01, 10
MESH = pl.DeviceIdType.MESH
N_DEV = 8
N_CHIP = 4
HBM = pl.BlockSpec(memory_space=pl.ANY)
VMEM_LIMIT = 56 * 1024 * 1024

_pcall = pl.pallas_call


def _tile(n, pref, mult=LANES):
    best = None
    t = mult
    while t <= min(n, pref):
        if n % t == 0:
            best = t
        t += mult
    return n if best is None else best


def _params(*sem):
    return pltpu.CompilerParams(dimension_semantics=sem, vmem_limit_bytes=VMEM_LIMIT)


def _following(after, body, in_specs, args):
    afters = [a for a in (after if isinstance(after, (list, tuple)) else [after]) if a is not None]
    n = len(args)

    def ordered(*refs):
        body(*refs[:n], *refs[n + len(afters):])

    return ordered, list(in_specs) + [HBM] * len(afters), list(args) + afters


def _dot(a, b, dims):
    return lax.dot_general(a, b, (dims, ((), ())), preferred_element_type=F32)


NN = ((1,), (0,))
NT = ((1,), (1,))
TN = ((0,), (0,))


def _mm_body(dims, nk, tile):
    if nk == 1:
        def single(a_ref, b_ref, o_ref):
            o_ref[...] = _dot(a_ref[...].astype(BF16), b_ref[...].astype(BF16), dims).astype(o_ref.dtype)

        return single, []

    def body(a_ref, b_ref, o_ref, acc_ref):
        k = pl.program_id(2)

        @pl.when(k == 0)
        def _():
            acc_ref[...] = jnp.zeros_like(acc_ref)

        acc_ref[...] += _dot(a_ref[...].astype(BF16), b_ref[...].astype(BF16), dims)

        @pl.when(k == nk - 1)
        def _():
            o_ref[...] = acc_ref[...].astype(o_ref.dtype)

    return body, [pltpu.VMEM(tile, F32)]


def _mm_nn(a, b3, out_dtype, name, tm=1024, tn=1408, tk=2048, b_transposed=False):
    M, K = a.shape
    C, n = b3.shape[0], b3.shape[1 if b_transposed else 2]
    tm, tk, tn = _tile(M, tm, 8), _tile(K, tk), _tile(n, tn)
    npc, nk = n // tn, K // tk
    body, scratch = _mm_body(NT if b_transposed else NN, nk, (tm, tn))
    b_spec = (pl.BlockSpec((None, tn, tk), lambda i, j, k: (j // npc, j % npc, k)) if b_transposed
              else pl.BlockSpec((None, tk, tn), lambda i, j, k: (j // npc, k, j % npc)))
    return _pcall(
        body, grid=(M // tm, C * npc, nk),
        in_specs=[pl.BlockSpec((tm, tk), lambda i, j, k: (i, k)), b_spec],
        out_specs=pl.BlockSpec((tm, tn), lambda i, j, k: (i, j)),
        out_shape=jax.ShapeDtypeStruct((M, C * n), out_dtype), scratch_shapes=scratch,
        compiler_params=_params("parallel", "parallel", "arbitrary"), name=name)(a, b3)


def _mm_nn_some(a, b3, chunks, into, name, after=None, tm=1024):
    M, K = a.shape
    C, _, n = b3.shape
    tm = _tile(M, tm, 8)
    kept = [] if into is None else [into]

    def body(ids_ref, a_ref, b_ref, *rest):
        rest[-1][...] = _dot(a_ref[...].astype(BF16), b_ref[...], NN).astype(BF16)

    body, in_specs, args = _following(
        after, body, [pl.BlockSpec((tm, K), lambda i, j, ids: (i, 0)),
                      pl.BlockSpec((None, K, n), lambda i, j, ids: (ids[j], 0, 0))] + [HBM] * len(kept),
        [chunks, a, b3] + kept)
    grid_spec = pltpu.PrefetchScalarGridSpec(
        num_scalar_prefetch=1, grid=(M // tm, chunks.shape[0]), in_specs=in_specs,
        out_specs=pl.BlockSpec((tm, n), lambda i, j, ids: (i, ids[j])))
    return _pcall(body, grid_spec=grid_spec, out_shape=jax.ShapeDtypeStruct((M, C * n), BF16),
                  input_output_aliases={3: 0} if kept else {},
                  compiler_params=_params("parallel", "arbitrary"), name=name)(*args)


def _mm_nt(a, b3, out_dtype, name, tm=1024, tn=1024, tk=2048, after=None, b_transposed=False, per_step=1):
    M, _ = a.shape
    C, N, n = (b3.shape[0], b3.shape[2], b3.shape[1]) if b_transposed else b3.shape
    tm, tn, tk = _tile(M, tm, 8), _tile(N, tn), _tile(n, tk)
    dims = NN if b_transposed else NT
    extra = [] if after is None else [after]
    if per_step > 1 and tk == n and C % per_step == 0:
        nk, scratch = C // per_step, [pltpu.VMEM((tm, tn), F32)]
        b3 = b3.reshape(nk, per_step, *b3.shape[1:])
        a_spec = pl.BlockSpec((tm, per_step * n), lambda i, j, k: (i, k))
        if b_transposed:
            b_spec = pl.BlockSpec((None, per_step, n, tn), lambda i, j, k: (k, 0, 0, j))
        else:
            b_spec = pl.BlockSpec((None, per_step, tn, n), lambda i, j, k: (k, 0, j, 0))

        def body(a_ref, b_ref, *rest):
            o_ref, acc_ref = rest[len(extra):]
            k = pl.program_id(2)

            @pl.when(k == 0)
            def _():
                acc_ref[...] = jnp.zeros_like(acc_ref)

            b = b_ref[...].astype(BF16)
            b = b.reshape(per_step * n, tn) if b_transposed else jnp.concatenate([b[u] for u in range(per_step)], axis=1)
            acc_ref[...] += _dot(a_ref[...].astype(BF16), b, dims)

            @pl.when(k == nk - 1)
            def _():
                o_ref[...] = acc_ref[...].astype(o_ref.dtype)
    else:
        kpc = n // tk
        nk = C * kpc
        inner, scratch = _mm_body(dims, nk, (tm, tn))
        a_spec = pl.BlockSpec((tm, tk), lambda i, j, k: (i, k))
        b_spec = (pl.BlockSpec((None, tk, tn), lambda i, j, k: (k // kpc, k % kpc, j)) if b_transposed
                  else pl.BlockSpec((None, tn, tk), lambda i, j, k: (k // kpc, j, k % kpc)))

        def body(a_ref, b_ref, *rest):
            inner(a_ref, b_ref, *rest[len(extra):])

    return _pcall(
        body, grid=(M // tm, N // tn, nk), in_specs=[a_spec, b_spec] + [HBM] * len(extra),
        out_specs=pl.BlockSpec((tm, tn), lambda i, j, k: (i, j)),
        out_shape=jax.ShapeDtypeStruct((M, N), out_dtype), scratch_shapes=scratch,
        compiler_params=_params("parallel", "parallel", "arbitrary"), name=name)(a, b3, *extra)


def _mm_tn(x, y, n, out_dtype, name, tm=1024, tn=1408, tk=2048, after=None):
    S, P = x.shape
    C = y.shape[1] // n
    tm, tn, tk = _tile(P, tm), _tile(n, tn), _tile(S, tk, 8)
    npc, nk = n // tn, S // tk
    inner, scratch = _mm_body(TN, nk, (tm, tn))
    extra = [] if after is None else [after]

    def body(x_ref, y_ref, *rest):
        inner(x_ref, y_ref, *rest[len(extra):])

    return _pcall(
        body, grid=(P // tm, C * npc, nk),
        in_specs=[pl.BlockSpec((tk, tm), lambda i, j, k: (k, i)),
                  pl.BlockSpec((tk, tn), lambda i, j, k: (k, j))] + [HBM] * len(extra),
        out_specs=pl.BlockSpec((None, tm, tn), lambda i, j, k: (j // npc, i, j % npc)),
        out_shape=jax.ShapeDtypeStruct((C, P, n), out_dtype), scratch_shapes=scratch,
        compiler_params=_params("parallel", "parallel", "arbitrary"), name=name)(x, y, *extra)


def _rms_scale(v):
    return lax.rsqrt(jnp.mean(v * v, axis=-1, keepdims=True) + RMS_EPS)


def _rms_bwd(gy, v, r):
    return r * gy - v * (r * r * r * jnp.mean(gy * v, axis=-1, keepdims=True))


def _rows_spec(tm, d):
    return pl.BlockSpec((tm, d), lambda i: (i, 0))


def _vec_spec(d):
    return pl.BlockSpec((1, d), lambda i: (0, 0))


def _rms_fwd(x, g, name, tm=256, after=None):
    S, D = x.shape

    def body(x_ref, g_ref, h_ref):
        v = x_ref[...]
        h_ref[...] = (v * _rms_scale(v) * g_ref[...]).astype(BF16)

    body, in_specs, args = _following(after, body, [_rows_spec(tm, D), _vec_spec(D)], [x, g])
    return _pcall(body, grid=(S // tm,), in_specs=in_specs, out_specs=_rows_spec(tm, D),
                  out_shape=jax.ShapeDtypeStruct((S, D), BF16), compiler_params=_params("parallel"), name=name)(*args)


def _mid_fwd(x, mix, g_post, g_pre, name, tm=256, after=None):
    S, D = x.shape

    def body(x_ref, m_ref, gp_ref, gn_ref, x2_ref, h_ref):
        m = m_ref[...]
        x2 = x_ref[...] + m * _rms_scale(m) * gp_ref[...]
        x2_ref[...] = x2
        h_ref[...] = (x2 * _rms_scale(x2) * gn_ref[...]).astype(BF16)

    body, in_specs, args = _following(
        after, body, [_rows_spec(tm, D), _rows_spec(tm, D), _vec_spec(D), _vec_spec(D)], [x, mix, g_post, g_pre])
    return _pcall(body, grid=(S // tm,), in_specs=in_specs,
                  out_specs=[_rows_spec(tm, D), _rows_spec(tm, D)],
                  out_shape=[jax.ShapeDtypeStruct((S, D), F32), jax.ShapeDtypeStruct((S, D), BF16)],
                  compiler_params=_params("parallel"), name=name)(*args)


def _loss_bwd(x2, f, tgt, g_post, name, tm=256):
    S, D = x2.shape

    def body(x2_ref, f_ref, t_ref, g_ref, dy_ref, df_ref, dg_ref, ls_ref):
        i = pl.program_id(0)

        @pl.when(i == 0)
        def _():
            dg_ref[...] = jnp.zeros_like(dg_ref)
            ls_ref[...] = jnp.zeros_like(ls_ref)

        fv = f_ref[...]
        r = _rms_scale(fv)
        g = g_ref[...]
        err = x2_ref[...] + fv * r * g - t_ref[...]
        ls_ref[...] += jnp.broadcast_to(0.5 * jnp.sum(jnp.mean(err * err, axis=-1, keepdims=True), axis=0, keepdims=True), ls_ref.shape)
        dy = err * (1.0 / D)
        dy_ref[...] = dy
        df_ref[...] = _rms_bwd(dy * g, fv, r).astype(BF16)
        dg_ref[...] += jnp.sum(dy * fv * r, axis=0, keepdims=True)

    return _pcall(body, grid=(S // tm,),
                  in_specs=[_rows_spec(tm, D), _rows_spec(tm, D), _rows_spec(tm, D), _vec_spec(D)],
                  out_specs=[_rows_spec(tm, D), _rows_spec(tm, D), _vec_spec(D), _vec_spec(LANES)],
                  out_shape=[jax.ShapeDtypeStruct((S, D), F32), jax.ShapeDtypeStruct((S, D), BF16),
                             jax.ShapeDtypeStruct((1, D), F32), jax.ShapeDtypeStruct((1, LANES), F32)],
                  compiler_params=_params("arbitrary"), name=name)(x2, f, tgt, g_post)


def _mid_bwd(dy, dh2, x2, mix, g_pre, g_post, name, tm=256, after=None):
    S, D = dy.shape

    def body(dy_ref, dh_ref, x2_ref, m_ref, gn_ref, gp_ref, dx2_ref, dm_ref, dgn_ref, dgp_ref):
        i = pl.program_id(0)

        @pl.when(i == 0)
        def _():
            dgn_ref[...] = jnp.zeros_like(dgn_ref)
            dgp_ref[...] = jnp.zeros_like(dgp_ref)

        x2, dh = x2_ref[...], dh_ref[...].astype(F32)
        r = _rms_scale(x2)
        dx2 = dy_ref[...] + _rms_bwd(dh * gn_ref[...], x2, r)
        dgn_ref[...] += jnp.sum(dh * x2 * r, axis=0, keepdims=True)
        dx2_ref[...] = dx2
        m = m_ref[...]
        rm = _rms_scale(m)
        dm_ref[...] = _rms_bwd(dx2 * gp_ref[...], m, rm).astype(BF16)
        dgp_ref[...] += jnp.sum(dx2 * m * rm, axis=0, keepdims=True)

    body, in_specs, args = _following(
        after, body, [_rows_spec(tm, D)] * 4 + [_vec_spec(D)] * 2, [dy, dh2, x2, mix, g_pre, g_post])
    return _pcall(body, grid=(S // tm,), in_specs=in_specs,
                  out_specs=[_rows_spec(tm, D), _rows_spec(tm, D), _vec_spec(D), _vec_spec(D)],
                  out_shape=[jax.ShapeDtypeStruct((S, D), F32), jax.ShapeDtypeStruct((S, D), BF16),
                             jax.ShapeDtypeStruct((1, D), F32), jax.ShapeDtypeStruct((1, D), F32)],
                  compiler_params=_params("arbitrary"), name=name)(*args)


def _first_bwd(dx2, dh1, x, g_pre, name, tm=256):
    S, D = x.shape

    def body(dx2_ref, dh_ref, x_ref, g_ref, gx_ref, dg_ref):
        i = pl.program_id(0)

        @pl.when(i == 0)
        def _():
            dg_ref[...] = jnp.zeros_like(dg_ref)

        xv, dh = x_ref[...], dh_ref[...].astype(F32)
        r = _rms_scale(xv)
        gx_ref[...] = dx2_ref[...] + _rms_bwd(dh * g_ref[...], xv, r)
        dg_ref[...] += jnp.sum(dh * xv * r, axis=0, keepdims=True)

    return _pcall(body, grid=(S // tm,), in_specs=[_rows_spec(tm, D)] * 3 + [_vec_spec(D)],
                  out_specs=[_rows_spec(tm, D), _vec_spec(D)],
                  out_shape=[jax.ShapeDtypeStruct((S, D), F32), jax.ShapeDtypeStruct((1, D), F32)],
                  compiler_params=_params("arbitrary"), name=name)(dx2, dh1, x, g_pre)


def _logsig_pair(z):
    lb = jnp.minimum(z, 0.0) - jnp.log(1.0 + jnp.exp(-jnp.abs(z)))
    return lb, lb - z


SB_KEY_BLOCK = 256


def _sum_matrix(strict):
    ia = lax.broadcasted_iota(jnp.int32, (SB_KEY_BLOCK, SB_KEY_BLOCK), 0)
    ib = lax.broadcasted_iota(jnp.int32, (SB_KEY_BLOCK, SB_KEY_BLOCK), 1)
    return ((ia > ib) if strict == ">" else (ia < ib)).astype(BF16)


def _row_total(sums, v, col):
    return jnp.broadcast_to(sums[:, col:col + 1] + v[:, col:col + 1], (v.shape[0], LANES))


def _lanes(c, width):
    return jnp.tile(c, (1, width // LANES))


def _split_dot(v, u):
    hi = v.astype(BF16)
    lo = (v - hi.astype(F32)).astype(BF16)
    return _dot(hi, u, NN) + _dot(lo, u, NN)


def _head_out(o, g):
    return o * _rms_scale(o) * g


def _sb_fwd(proj, gain, n_heads, mixed_heads, name, tq=1024, after=None):
    S = proj.shape[0]
    H, tk = n_heads, SB_KEY_BLOCK
    tq = _tile(S, tq, 2 * tk)
    scale = HEAD_DIM ** -0.5

    def body(q_ref, k_ref, v_ref, g_ref, o_ref, ct_ref, mx_ref, oacc, cacc):
        i = pl.program_id(1)
        oacc[...] = jnp.zeros_like(oacc)
        cacc[...] = jnp.zeros_like(cacc)
        sums = _sum_matrix(">")

        def run(blocks):
            scored = []
            for k0, r0, diagonal in blocks:
                rows = pl.ds(r0, tq - r0)
                lb, lk = _logsig_pair(_dot(q_ref[rows, :].astype(BF16), k_ref[pl.ds(k0, tk), :].astype(BF16), NT) * scale)
                causal = None
                if diagonal:
                    causal = (lax.broadcasted_iota(jnp.int32, (tq - r0, tk), 1)
                              < lax.broadcasted_iota(jnp.int32, (tq - r0, tk), 0))
                    lk = jnp.where(causal, lk, 0.0)
                scored.append((k0, rows, causal, lb, lk))
            summed = [(k0, rows, causal, lb, lk, _split_dot(lk, sums)) for k0, rows, causal, lb, lk in scored]
            weights = []
            for k0, rows, causal, lb, lk, after in summed:
                c = cacc[rows, :]
                a = jnp.exp(lb + after + _lanes(c, tk))
                if causal is not None:
                    a = jnp.where(causal, a, 0.0)
                cacc[rows, :] = c + _row_total(after, lk, 0)
                weights.append((k0, rows, a.astype(BF16)))
            for k0, rows, a in weights:
                oacc[rows, :] += _dot(a, v_ref[pl.ds(k0, tk), :].astype(BF16), NN)

        for d in reversed(range(0, tq // tk, 2)):
            run([(pl.multiple_of(i * tq + e * tk, tk), e * tk, True) for e in (d + 1, d)])
        per_trip = tq // tk

        def step(it, carry):
            k0 = pl.multiple_of((i - 1 - it) * tq, tq)
            run([(pl.multiple_of(k0 + e * tk, tk), 0, False) for e in reversed(range(per_trip))])
            return carry

        lax.fori_loop(0, i, step, 0)
        o = oacc[...]
        o_ref[...] = o
        ct_ref[...] = cacc[...]
        mx_ref[...] = _head_out(o, g_ref[...]).astype(BF16)

    blk = pl.BlockSpec((tq, HEAD_DIM), lambda h, i: (i, h))
    body, in_specs, args = _following(
        after, body,
        [blk, pl.BlockSpec((S, HEAD_DIM), lambda h, i: (0, H + h)),
         pl.BlockSpec((S, HEAD_DIM), lambda h, i: (0, 2 * H + h)), pl.BlockSpec((1, HEAD_DIM), lambda h, i: (0, h))],
        [proj, proj, proj, gain])
    return _pcall(
        body, grid=(H, S // tq), in_specs=in_specs,
        out_specs=[blk, blk, blk],
        out_shape=[jax.ShapeDtypeStruct((S, H * HEAD_DIM), F32), jax.ShapeDtypeStruct((S, H * HEAD_DIM), F32),
                   jax.ShapeDtypeStruct((S, mixed_heads * HEAD_DIM), BF16)],
        scratch_shapes=[pltpu.VMEM((tq, HEAD_DIM), F32), pltpu.VMEM((tq, LANES), F32)],
        compiler_params=_params("parallel", "arbitrary"), name=name)(*args)


def _sb_bwd(proj, gain, o_raw, ctot, dmixed, dm_col0, n_heads, name, tq=1024, after=None):
    S = proj.shape[0]
    H, tk = n_heads, SB_KEY_BLOCK
    tq = _tile(S, tq, 2 * tk)
    nq = S // tq
    scale = HEAD_DIM ** -0.5

    def body(q_ref, k_ref, v_ref, g_ref, o_ref, ct_ref, dm_ref, dproj_ref, dg_ref,
             dkacc, dvacc, dqacc, pfx, gcar, dos, stage_q, stage_k, stage_v, out_sems):
        h, i = pl.program_id(0), pl.program_id(1)

        @pl.when(i == 0)
        def _():
            dkacc[...] = jnp.zeros_like(dkacc)
            dvacc[...] = jnp.zeros_like(dvacc)
            dg_ref[...] = jnp.zeros_like(dg_ref)

        o, dm, g = o_ref[...], dm_ref[...].astype(F32), g_ref[...]
        r = _rms_scale(o)
        dos[...] = _rms_bwd(dm * g, o, r).astype(BF16)
        dg_ref[...] += jnp.broadcast_to(jnp.sum(dm * o * r, axis=0, keepdims=True), dg_ref.shape)
        dqacc[...] = jnp.zeros_like(dqacc)
        pfx[...] = jnp.zeros_like(pfx)
        gcar[...] = jnp.zeros_like(gcar)
        later, earlier = _sum_matrix(">"), _sum_matrix("<")

        def run(blocks):
            scored = []
            for k0, r0, diagonal in blocks:
                rows, keys = pl.ds(r0, tq - r0), pl.ds(k0, tk)
                lb, lk = _logsig_pair(_dot(q_ref[rows, :].astype(BF16), k_ref[keys, :].astype(BF16), NT) * scale)
                da = _dot(dos[rows, :], v_ref[keys, :].astype(BF16), NT)
                causal = None
                if diagonal:
                    causal = (lax.broadcasted_iota(jnp.int32, (tq - r0, tk), 1)
                              < lax.broadcasted_iota(jnp.int32, (tq - r0, tk), 0))
                    lk = jnp.where(causal, lk, 0.0)
                scored.append((rows, keys, causal, lb, lk, da))
            summed = [(*blk, _split_dot(blk[4], later)) for blk in scored]
            weighted = []
            for rows, keys, causal, lb, lk, da, after in summed:
                p = pfx[rows, :] + _row_total(after, lk, 0)
                pfx[rows, :] = p
                a = jnp.exp(lb + after + _lanes(ct_ref[rows, :] - p, tk))
                if causal is not None:
                    a = jnp.where(causal, a, 0.0)
                dl = da * a
                weighted.append((rows, keys, causal, lb, a.astype(BF16), dl, _dot(dl.astype(BF16), earlier, NN)))
            cotangents = []
            for rows, keys, causal, lb, a, dl, before in weighted:
                gc = gcar[rows, :]
                gcar[rows, :] = gc + _row_total(before, dl, tk - 1)
                sig = jnp.exp(lb)
                gsum = (before + _lanes(gc, tk)) * sig
                if causal is not None:
                    gsum = jnp.where(causal, gsum, 0.0)
                cotangents.append((rows, keys, a, ((dl * (1.0 - sig) - gsum) * scale).astype(BF16)))
            for rows, keys, a, dz in cotangents:
                q, do = q_ref[rows, :].astype(BF16), dos[rows, :]
                dvacc[keys, :] += _dot(a, do, TN)
                dqacc[rows, :] += _dot(dz, k_ref[keys, :].astype(BF16), NN)
                dkacc[keys, :] += _dot(dz, q, TN)

        per_trip = tq // tk

        def step(j, carry):
            k0 = pl.multiple_of(j * tq, tq)
            run([(pl.multiple_of(k0 + e * tk, tk), 0, False) for e in range(per_trip)])
            return carry

        lax.fori_loop(0, i, step, 0)
        for d in range(0, tq // tk, 2):
            run([(pl.multiple_of(i * tq + e * tk, tk), e * tk, True) for e in (d, d + 1)])
        def columns(block):
            return pl.ds(pl.multiple_of(block * HEAD_DIM, HEAD_DIM), HEAD_DIM)

        dq_out = pltpu.make_async_copy(stage_q, dproj_ref.at[pl.ds(pl.multiple_of(i * tq, tq), tq), columns(h)], out_sems.at[0])
        dkv_out = [pltpu.make_async_copy(stage_k, dproj_ref.at[:, columns(H + h)], out_sems.at[1]),
                   pltpu.make_async_copy(stage_v, dproj_ref.at[:, columns(2 * H + h)], out_sems.at[2])]

        @pl.when((h > 0) | (i > 0))
        def _():
            dq_out.wait()

        stage_q[...] = dqacc[...].astype(BF16)
        dq_out.start()

        @pl.when(i == nq - 1)
        def _():
            @pl.when(h > 0)
            def _():
                for cp in dkv_out:
                    cp.wait()

            stage_k[...] = dkacc[...].astype(BF16)
            stage_v[...] = dvacc[...].astype(BF16)
            for cp in dkv_out:
                cp.start()

        @pl.when((h == H - 1) & (i == nq - 1))
        def _():
            dq_out.wait()
            for cp in dkv_out:
                cp.wait()

    blk = pl.BlockSpec((tq, HEAD_DIM), lambda h, i: (i, h))
    W = H * HEAD_DIM
    body, in_specs, args = _following(
        after, body,
        [blk, pl.BlockSpec((S, HEAD_DIM), lambda h, i: (0, H + h)),
         pl.BlockSpec((S, HEAD_DIM), lambda h, i: (0, 2 * H + h)), pl.BlockSpec((1, HEAD_DIM), lambda h, i: (0, h)),
         blk, blk, pl.BlockSpec((tq, HEAD_DIM), lambda h, i: (i, dm_col0 + h))],
        [proj, proj, proj, gain, o_raw, ctot, dmixed])
    return _pcall(
        body, grid=(H, nq), in_specs=in_specs,
        out_specs=[HBM, pl.BlockSpec((8, HEAD_DIM), lambda h, i: (0, h))],
        out_shape=[jax.ShapeDtypeStruct(proj.shape, BF16), jax.ShapeDtypeStruct((8, W), F32)],
        scratch_shapes=[pltpu.VMEM((S, HEAD_DIM), F32), pltpu.VMEM((S, HEAD_DIM), F32), pltpu.VMEM((tq, HEAD_DIM), F32),
                        pltpu.VMEM((tq, LANES), F32), pltpu.VMEM((tq, LANES), F32), pltpu.VMEM((tq, HEAD_DIM), BF16),
                        pltpu.VMEM((tq, HEAD_DIM), BF16), pltpu.VMEM((S, HEAD_DIM), BF16), pltpu.VMEM((S, HEAD_DIM), BF16),
                        pltpu.SemaphoreType.DMA((3,))],
        compiler_params=_params("arbitrary", "arbitrary"), name=name)(*args)


def _rope_tables(S):
    inv_freq = ROPE_THETA ** (-jnp.arange(0, HEAD_DIM, 2, dtype=F32) / HEAD_DIM)
    ang = jnp.arange(S, dtype=F32)[:, None] * inv_freq[None, :]
    cos, sin = jnp.cos(ang), jnp.sin(ang)
    return jnp.concatenate([cos, cos], axis=1), jnp.concatenate([-sin, sin], axis=1)


def _rope(v, cos2, sin_signed):
    return v * cos2 + pltpu.roll(v, HEAD_DIM // 2, axis=1) * sin_signed


def _dil_rows(d, r, l0, n):
    if d == 1:
        return pl.ds(l0 if isinstance(l0, int) else pl.multiple_of(l0, KEY_BLOCK), n)
    return pl.ds(r + d * l0, n, stride=d)


def _dil_blocks(S, visit):
    B = KEY_BLOCK
    group = 16
    for b, d in enumerate(DILATIONS):
        nb = S // d // B
        if nb == 1:
            g = math.gcd(d, group)

            def trip(t, carry, b=b, d=d, g=g):
                visit([(b, d, t * g + u, 0, True) for u in range(g)])
                return carry

            lax.fori_loop(0, d // g, trip, 0)
        elif d == 1:
            visit([(b, d, 0, 0, True)])
            g = max(k for k in range(1, group + 2) if (nb - 1) % k == 0)

            def trip(t, carry, b=b, d=d, g=g):
                visit([(b, d, 0, (1 + t * g + u) * B, False) for u in range(g)])
                return carry

            lax.fori_loop(0, (nb - 1) // g, trip, 0)
        else:
            g = math.gcd(d, max(group // nb, 1))

            def trip(t, carry, b=b, d=d, nb=nb, g=g):
                visit([(b, d, t * g + u, n * B, n == 0) for u in range(g) for n in range(nb)])
                return carry

            lax.fori_loop(0, d // g, trip, 0)


def _dil_mask(first):
    B = KEY_BLOCK
    nk = B if first else 2 * B
    iq = lax.broadcasted_iota(jnp.int32, (B, nk), 0)
    ik = lax.broadcasted_iota(jnp.int32, (B, nk), 1)
    return (ik <= iq) if first else ((ik >= iq) & (ik <= iq + B))


def _dil_fwd(proj, cos2, sin_signed, gain, mixed, col0, n_heads, name, after=None):
    S = proj.shape[0]
    H, B = n_heads, KEY_BLOCK
    scale = HEAD_DIM ** -0.5
    rc = _tile(S, 256, 8)

    def body(q_ref, k_ref, v_ref, c_ref, s_ref, g_ref, mixed_in, o_ref, l_ref, mx_ref, qr, kr, vf, *per_branch):
        ob, lb = per_branch[:len(DILATIONS)], per_branch[len(DILATIONS):]

        def rope_rows(t, carry):
            rows = pl.ds(pl.multiple_of(t * rc, rc), rc)
            qr[rows, :] = _rope(q_ref[rows, :].astype(F32), c_ref[rows, :], s_ref[rows, :])
            kr[rows, :] = _rope(k_ref[rows, :].astype(F32), c_ref[rows, :], s_ref[rows, :])
            vf[rows, :] = v_ref[rows, :].astype(F32)
            return carry

        lax.fori_loop(0, S // rc, rope_rows, 0)

        def visit(blocks):
            scores = []
            for b, d, r, l0, first in blocks:
                qrows = _dil_rows(d, r, l0, B)
                krows = qrows if first else _dil_rows(d, r, l0 - B, 2 * B)
                s = _dot(qr[qrows, :].astype(BF16), kr[krows, :].astype(BF16), NT) * scale
                scores.append((b, qrows, krows, jnp.where(_dil_mask(first), s, NEG)))
            weights = []
            for b, qrows, krows, s in scores:
                m = jnp.max(s, axis=1, keepdims=True)
                p = jnp.exp(s - m)
                den = jnp.sum(p, axis=1, keepdims=True)
                lb[b][qrows, :] = jnp.broadcast_to(m + jnp.log(den), (B, LANES))
                weights.append((b, qrows, krows, p.astype(BF16), den))
            for b, qrows, krows, p, den in weights:
                ob[b][qrows, :] = _dot(p, vf[krows, :].astype(BF16), NN) / den

        _dil_blocks(S, visit)

        def combine(t, carry):
            rows = pl.ds(pl.multiple_of(t * rc, rc), rc)
            l0, l1, l2 = lb[0][rows, :], lb[1][rows, :], lb[2][rows, :]
            m = jnp.maximum(jnp.maximum(l0, l1), l2)
            w0, w1, w2 = jnp.exp(l0 - m), jnp.exp(l1 - m), jnp.exp(l2 - m)
            den = w0 + w1 + w2
            o = (w0 * ob[0][rows, :] + w1 * ob[1][rows, :] + w2 * ob[2][rows, :]) / den
            o_ref[rows, :] = o
            l_ref[rows, :] = m + jnp.log(den)
            mx_ref[rows, :] = _head_out(o, g_ref[...]).astype(BF16)
            return carry

        lax.fori_loop(0, S // rc, combine, 0)

    def col(k):
        return pl.BlockSpec((S, HEAD_DIM), lambda h: (0, col0 + k * H + h))

    tab = pl.BlockSpec((S, HEAD_DIM), lambda h: (0, 0))
    out = pl.BlockSpec((S, HEAD_DIM), lambda h: (0, h))
    W = H * HEAD_DIM
    first = mixed.shape[1] // HEAD_DIM - H
    body, in_specs, args = _following(
        after, body, [col(0), col(1), col(2), tab, tab, pl.BlockSpec((1, HEAD_DIM), lambda h: (0, h)), HBM],
        [proj, proj, proj, cos2, sin_signed, gain, mixed])
    return _pcall(
        body, grid=(H,), in_specs=in_specs,
        out_specs=[out, out, pl.BlockSpec((S, HEAD_DIM), lambda h: (0, first + h))],
        out_shape=[jax.ShapeDtypeStruct((S, W), F32), jax.ShapeDtypeStruct((S, W), F32),
                   jax.ShapeDtypeStruct(mixed.shape, BF16)],
        input_output_aliases={6: 2},
        scratch_shapes=[pltpu.VMEM((S, HEAD_DIM), F32)] * (3 + 2 * len(DILATIONS)),
        compiler_params=_params("parallel"), name=name)(*args)


def _dil_bwd(proj, cos2, sin_signed, gain, o_raw, lse, dmixed, dproj, dm_col0, col0, n_heads, name, after=None):
    S = proj.shape[0]
    H, B = n_heads, KEY_BLOCK
    scale = HEAD_DIM ** -0.5
    rc = _tile(S, 256, 8)

    def body(q_ref, k_ref, v_ref, c_ref, s_ref, g_ref, o_ref, l_ref, dm_ref, dproj_in, dproj_ref, dg_ref,
             qr, kr, vf, dos, dsum, dqr, dkr, dvv, stage_q, stage_k, stage_v, out_sems):
        dg_ref[...] = jnp.zeros_like(dg_ref)

        def prep(t, carry):
            rows = pl.ds(pl.multiple_of(t * rc, rc), rc)
            qr[rows, :] = _rope(q_ref[rows, :].astype(F32), c_ref[rows, :], s_ref[rows, :])
            kr[rows, :] = _rope(k_ref[rows, :].astype(F32), c_ref[rows, :], s_ref[rows, :])
            vf[rows, :] = v_ref[rows, :].astype(F32)
            o, dm = o_ref[rows, :], dm_ref[rows, :].astype(F32)
            r = _rms_scale(o)
            do = _rms_bwd(dm * g_ref[...], o, r)
            dg_ref[...] += jnp.broadcast_to(jnp.sum(dm * o * r, axis=0, keepdims=True), dg_ref.shape)
            dos[rows, :] = do
            dsum[rows, :] = jnp.broadcast_to(jnp.sum(do * o, axis=1, keepdims=True), (rc, LANES))
            dqr[rows, :] = jnp.zeros((rc, HEAD_DIM), F32)
            dkr[rows, :] = jnp.zeros((rc, HEAD_DIM), F32)
            dvv[rows, :] = jnp.zeros((rc, HEAD_DIM), F32)
            return carry

        lax.fori_loop(0, S // rc, prep, 0)

        def visit(blocks):
            products = []
            for b, d, r, l0, first in blocks:
                qrows = _dil_rows(d, r, l0, B)
                krows = qrows if first else _dil_rows(d, r, l0 - B, 2 * B)
                qs, ks = qr[qrows, :].astype(BF16), kr[krows, :].astype(BF16)
                do = dos[qrows, :].astype(BF16)
                s = jnp.where(_dil_mask(first), _dot(qs, ks, NT) * scale, NEG)
                dp = _dot(do, vf[krows, :].astype(BF16), NT)
                products.append((qrows, krows, qs, ks, do, s, dp))
            cotangents = []
            for qrows, krows, qs, ks, do, s, dp in products:
                p = jnp.exp(s - l_ref[qrows, :][:, 0:1])
                ds = (p * (dp - dsum[qrows, :][:, 0:1]) * scale).astype(BF16)
                cotangents.append((qrows, krows, qs, ks, do, p.astype(BF16), ds))
            for qrows, krows, qs, ks, do, p, ds in cotangents:
                dqr[qrows, :] += _dot(ds, ks, NN)
                dkr[krows, :] += _dot(ds, qs, TN)
                dvv[krows, :] += _dot(p, do, TN)

        _dil_blocks(S, visit)

        def finish(t, carry):
            rows = pl.ds(pl.multiple_of(t * rc, rc), rc)
            c, s = c_ref[rows, :], s_ref[rows, :]
            dq, dk = dqr[rows, :], dkr[rows, :]
            stage_q[rows, :] = (dq * c + pltpu.roll(dq * s, HEAD_DIM // 2, axis=1)).astype(BF16)
            stage_k[rows, :] = (dk * c + pltpu.roll(dk * s, HEAD_DIM // 2, axis=1)).astype(BF16)
            stage_v[rows, :] = dvv[rows, :].astype(BF16)
            return carry

        h = pl.program_id(0)
        outs = [pltpu.make_async_copy(
            stage, dproj_ref.at[:, pl.ds(pl.multiple_of((col0 + k * H + h) * HEAD_DIM, HEAD_DIM), HEAD_DIM)], out_sems.at[k])
            for k, stage in enumerate((stage_q, stage_k, stage_v))]

        @pl.when(h > 0)
        def _():
            for cp in outs:
                cp.wait()

        lax.fori_loop(0, S // rc, finish, 0)
        for cp in outs:
            cp.start()

        @pl.when(h == H - 1)
        def _():
            for cp in outs:
                cp.wait()

    def col(k):
        return pl.BlockSpec((S, HEAD_DIM), lambda h: (0, col0 + k * H + h))

    tab = pl.BlockSpec((S, HEAD_DIM), lambda h: (0, 0))
    out = pl.BlockSpec((S, HEAD_DIM), lambda h: (0, h))
    W = H * HEAD_DIM
    big, half = pltpu.VMEM((S, HEAD_DIM), F32), pltpu.VMEM((S, HEAD_DIM), BF16)
    body, in_specs, args = _following(
        after, body,
        [col(0), col(1), col(2), tab, tab, pl.BlockSpec((1, HEAD_DIM), lambda h: (0, h)), out, out,
         pl.BlockSpec((S, HEAD_DIM), lambda h: (0, dm_col0 + h)), HBM],
        [proj, proj, proj, cos2, sin_signed, gain, o_raw, lse, dmixed, dproj])
    return _pcall(
        body, grid=(H,), in_specs=in_specs,
        out_specs=[HBM, pl.BlockSpec((8, HEAD_DIM), lambda h: (0, h))],
        out_shape=[jax.ShapeDtypeStruct(dproj.shape, BF16), jax.ShapeDtypeStruct((8, W), F32)],
        input_output_aliases={9: 0},
        scratch_shapes=[big, big, big, big, pltpu.VMEM((S, LANES), F32), big, big, big, half, half, half,
                        pltpu.SemaphoreType.DMA((3,))],
        compiler_params=_params("arbitrary"), name=name)(*args)


GELU_C = math.sqrt(2.0 / math.pi)
GELU_A = 0.044715
HALO = 16


def _shifts_down(cur, halo):
    row = lax.broadcasted_iota(jnp.int32, cur.shape, 0)
    first, second = row == 0, row == 1
    last, before_last = halo[HALO - 1:HALO, :], halo[HALO - 2:HALO - 1, :]
    two = jnp.where(first, before_last, jnp.where(second, last, pltpu.roll(cur, 2, axis=0)))
    return two, jnp.where(first, last, pltpu.roll(cur, 1, axis=0))


def _shift_up(cur, halo, k):
    n = cur.shape[0]
    out = pltpu.roll(cur, n - k, axis=0)
    row = lax.broadcasted_iota(jnp.int32, cur.shape, 0)
    for t in range(k):
        out = jnp.where(row == n - k + t, halo[t:t + 1, :], out)
    return out


def _conv3(cur, halo, cw):
    rows = (*_shifts_down(cur, halo), cur)
    return rows[0] * cw[0:1, :] + rows[1] * cw[1:2, :] + cur * cw[2:3, :] + cw[3:4, :], rows


def _gelu_parts(x):
    xx = x * x
    t = jnp.tanh(x * (GELU_C + (GELU_C * GELU_A) * xx))
    half = 0.5 * x
    return half + half * t, t, xx, half


def _gelu_slope(t, xx, half):
    return (0.5 + 0.5 * t) + half * (1.0 - t * t) * (GELU_C + (3.0 * GELU_C * GELU_A) * xx)


def _geglu_specs(tm, tn, ncb):
    hb = tm // HALO

    def cur(off):
        return pl.BlockSpec((tm, tn), lambda j, i: (i, off + j))

    def prev(off):
        return pl.BlockSpec((HALO, tn), lambda j, i: (jnp.maximum(i * hb - 1, 0), off + j))

    def taps(off):
        return pl.BlockSpec((8, tn), lambda j, i: (0, off + j))

    return [cur(0), prev(0), cur(ncb), prev(ncb), taps(0), taps(ncb)]


def _geglu_fwd(u, cwb, name, tm=512, tn=1408, after=None):
    S, F2 = u.shape
    F = F2 // 2
    tm, tn = _tile(S, tm, HALO), _tile(F, tn)
    ncb = F // tn

    def body(g_ref, gp_ref, v_ref, vp_ref, cg_ref, cv_ref, y_ref):
        top = pl.program_id(1) > 0
        gp = jnp.where(top, gp_ref[...].astype(F32), 0.0)
        vp = jnp.where(top, vp_ref[...].astype(F32), 0.0)
        gc = _conv3(g_ref[...].astype(F32), gp, cg_ref[...])[0]
        vc = _conv3(v_ref[...].astype(F32), vp, cv_ref[...])[0]
        y_ref[...] = (_gelu_parts(gc)[0] * vc).astype(BF16)

    body, in_specs, args = _following(after, body, _geglu_specs(tm, tn, ncb), [u, u, u, u, cwb, cwb])
    return _pcall(body, grid=(ncb, S // tm), in_specs=in_specs,
                  out_specs=pl.BlockSpec((tm, tn), lambda j, i: (i, j)),
                  out_shape=jax.ShapeDtypeStruct((S, F), BF16),
                  compiler_params=_params("parallel", "parallel"), name=name)(*args)


def _geglu_bwd(u, dy, cwb, name, tm=256, tn=1408, after=None):
    S, F2 = u.shape
    F = F2 // 2
    tm, tn = _tile(S, tm, HALO), _tile(F, tn)
    ncb = F // tn

    def body(g_ref, gp_ref, v_ref, vp_ref, cg_ref, cv_ref, dy_ref, dc_ref, dwg_ref, dwv_ref):
        i = pl.program_id(1)

        @pl.when(i == 0)
        def _():
            dwg_ref[...] = jnp.zeros_like(dwg_ref)
            dwv_ref[...] = jnp.zeros_like(dwv_ref)

        top = i > 0
        g, v = g_ref[...].astype(F32), v_ref[...].astype(F32)
        gp = jnp.where(top, gp_ref[...].astype(F32), 0.0)
        vp = jnp.where(top, vp_ref[...].astype(F32), 0.0)
        gc, g_rows = _conv3(g, gp, cg_ref[...])
        vc, v_rows = _conv3(v, vp, cv_ref[...])
        act, t, xx, half = _gelu_parts(gc)
        dact = _gelu_slope(t, xx, half)
        dyv = dy_ref[...].astype(F32)
        dgc = dyv * vc * dact
        dvc = dyv * act
        dc_ref[0] = dgc.astype(BF16)
        dc_ref[1] = dvc.astype(BF16)

        def taps(out_ref, dc, rows):
            for k, moved in enumerate(rows):
                out_ref[k:k + 1, :] += jnp.sum(dc * moved, axis=0, keepdims=True)
            out_ref[3:4, :] += jnp.sum(dc, axis=0, keepdims=True)

        taps(dwg_ref, dgc, g_rows)
        taps(dwv_ref, dvc, v_rows)

    body, in_specs, args = _following(
        after, body, _geglu_specs(tm, tn, ncb) + [pl.BlockSpec((tm, tn), lambda j, i: (i, j))], [u, u, u, u, cwb, cwb, dy])
    return _pcall(body, grid=(ncb, S // tm), in_specs=in_specs,
                  out_specs=[pl.BlockSpec((2, tm, tn), lambda j, i: (0, i, j)),
                             pl.BlockSpec((8, tn), lambda j, i: (0, j)), pl.BlockSpec((8, tn), lambda j, i: (0, j))],
                  out_shape=[jax.ShapeDtypeStruct((2, S, F), BF16), jax.ShapeDtypeStruct((8, F), F32),
                             jax.ShapeDtypeStruct((8, F), F32)],
                  compiler_params=_params("parallel", "arbitrary"), name=name)(*args)


def _conv_bwd(dc, cwb, name, tm=512, tn=1408, after=None):
    _, S, F = dc.shape
    tm, tn = _tile(S, tm, HALO), _tile(F, tn)
    ncb, nrb = F // tn, S // tm
    hb = tm // HALO

    def body(c_ref, n_ref, w_ref, du_ref):
        cur = c_ref[...].astype(F32)
        nxt = jnp.where(pl.program_id(2) < nrb - 1, n_ref[...].astype(F32), 0.0)
        w = w_ref[...]
        du = cur * w[2:3, :] + _shift_up(cur, nxt, 1) * w[1:2, :] + _shift_up(cur, nxt, 2) * w[0:1, :]
        du_ref[...] = du.astype(BF16)

    body, in_specs, args = _following(
        after, body,
        [pl.BlockSpec((None, tm, tn), lambda c, j, i: (c, i, j)),
         pl.BlockSpec((None, HALO, tn), lambda c, j, i: (c, jnp.minimum((i + 1) * hb, S // HALO - 1), j)),
         pl.BlockSpec((8, tn), lambda c, j, i: (0, c * ncb + j))], [dc, dc, cwb])
    return _pcall(body, grid=(2, ncb, nrb), in_specs=in_specs,
                  out_specs=pl.BlockSpec((tm, tn), lambda c, j, i: (i, c * ncb + j)),
                  out_shape=jax.ShapeDtypeStruct((S, 2 * F), BF16),
                  compiler_params=_params("parallel", "parallel", "parallel"), name=name)(*args)


def _adam_math(w, g, m, v):
    m = ADAM_B1 * m + (1.0 - ADAM_B1) * g
    v = ADAM_B2 * v + (1.0 - ADAM_B2) * (g * g)
    m_hat = m / (1.0 - ADAM_B1 ** ADAM_STEP)
    v_hat = v / (1.0 - ADAM_B2 ** ADAM_STEP)
    return -ADAM_LR * (m_hat / (jnp.sqrt(v_hat) + ADAM_EPS) + ADAM_WD * w), m, v


def _adamw(w, parts, m, v, name, tr=256):
    R, C = w.shape
    n, _, Cp = parts.shape
    tr = _tile(R, tr, 8)

    def body(w_ref, p_ref, m_ref, v_ref, g_out, d_out, m_out, v_out):
        g = p_ref[0, :, 0:C].astype(F32)
        for k in range(1, n):
            g = g + p_ref[k, :, 0:C].astype(F32)
        d, mn, vn = _adam_math(w_ref[...], g, m_ref[...], v_ref[...])
        g_out[...] = g
        d_out[...] = d
        m_out[...] = mn
        v_out[...] = vn

    spec = pl.BlockSpec((tr, C), lambda i: (i, 0))
    shape = jax.ShapeDtypeStruct((R, C), F32)
    return _pcall(body, grid=(R // tr,), in_specs=[spec, pl.BlockSpec((n, tr, Cp), lambda i: (0, i, 0)), spec, spec],
                  out_specs=[spec] * 4, out_shape=[shape] * 4, compiler_params=_params("parallel"), name=name)(w, parts, m, v)


def _adamw_chips(w, pair, parts, chip_ids, m, v, name, tr=256):
    R, C = w.shape
    Cp = pair.shape[2]
    by_columns = C == Cp and _tile(R, tr, 16) < 64
    tr, tc = (R, _tile(C, 256)) if by_columns else (_tile(R, tr, 16), C)

    def body(ids_ref, w_ref, own_ref, p1_ref, p2_ref, p3_ref, m_ref, v_ref, g_out, d_out, m_out, v_out):
        g = own_ref[:, 0:tc].astype(F32)
        for ref in (p1_ref, p2_ref, p3_ref):
            g = g + ref[:, 0:tc].astype(F32)
        d, mn, vn = _adam_math(w_ref[...], g, m_ref[...], v_ref[...])
        g_out[...] = g
        d_out[...] = d
        m_out[...] = mn
        v_out[...] = vn

    if by_columns:
        spec = pl.BlockSpec((tr, tc), lambda j, ids: (0, j))
    else:
        spec = pl.BlockSpec((tr, tc), lambda i, ids: (i, 0))

    def chip(k):
        if by_columns:
            return pl.BlockSpec((None, tr, tc), lambda j, ids: (ids[k], 0, j))
        return pl.BlockSpec((None, tr, Cp), lambda i, ids: (ids[k], i, 0))

    shape = jax.ShapeDtypeStruct((R, C), F32)
    grid_spec = pltpu.PrefetchScalarGridSpec(
        num_scalar_prefetch=1, grid=(C // tc if by_columns else R // tr,),
        in_specs=[spec, chip(0), chip(1), chip(2), chip(3), spec, spec], out_specs=[spec] * 4)
    return _pcall(body, grid_spec=grid_spec, out_shape=[shape] * 4, compiler_params=_params("parallel"),
                  name=name)(chip_ids, w, pair, parts, parts, parts, m, v)


def _place():
    return lax.axis_index("x"), lax.axis_index("y"), lax.axis_index("c")


def _other_chips(x, y):
    return [(1 - x, y), (x, 1 - y), (1 - x, 1 - y)]


IN_HBM = pl.BlockSpec(memory_space=pltpu.HBM)
SEM = pl.BlockSpec(memory_space=pltpu.SEMAPHORE)
EFFECT = pltpu.SideEffectType.DATAFLOW_SIDE_EFFECTING
TOKEN = jax.ShapeDtypeStruct((8, LANES), F32)
TOKEN_SPEC = pl.BlockSpec(memory_space=pltpu.VMEM)


def _in_hbm(a):
    return pltpu.with_memory_space_constraint(a, pltpu.HBM)


def _landing(shape):
    return _in_hbm(lax.empty(shape.shape, shape.dtype))


def _hbm_like(a):
    return pltpu.HBM(a.shape, a.dtype)


def _gather_places():
    x, y, c = _place()
    relay_from = (c * (1 - x) + (1 - c) * x, c * y + (1 - c) * (1 - y), c)
    relay_to = (c * x + (1 - c) * (1 - x), c * (1 - y) + (1 - c) * y, c)
    return (x, y, c), (x, y, 1 - c), (1 - x, y, c), (x, 1 - y, c), (1 - x, 1 - y, c), relay_from, relay_to


def _slot_copy(slot, ref, src, dst, send_sem, recv_sem, to):
    return pltpu.make_async_remote_copy(src_ref=slot(ref, *src), dst_ref=slot(ref, *dst), send_sem=send_sem,
                                        recv_sem=recv_sem, device_id=to, device_id_type=MESH)


def _split_call(body, arrays, sems_in, sems_out, after, name, token=True):
    na, ni, no = len(arrays), len(sems_in), len(sems_out)

    def wrapped(*refs):
        body(refs[:na], refs[na:na + ni], refs[na + ni + 1:na + ni + 1 + no])
        if token:
            refs[-1][...] = jnp.zeros_like(refs[-1])

    outs = _pcall(
        wrapped, in_specs=[IN_HBM] * na + [SEM] * ni + [HBM],
        out_specs=[SEM] * no + [IN_HBM] * na + ([TOKEN_SPEC] if token else []),
        out_shape=[pltpu.SemaphoreType.DMA((n,)) for n in sems_out] + [_hbm_like(s) for s in arrays] + ([TOKEN] if token else []),
        input_output_aliases={a: no + a for a in range(na)},
        compiler_params=pltpu.CompilerParams(has_side_effects=EFFECT), name=name,
    )(*[_in_hbm(s) for s in arrays], *sems_in, after)
    return list(outs[:no]), list(outs[no:no + na]), (outs[-1] if token else None)


def _gather_start(landing, slots, after, name):
    na = len(landing)

    def body(land, _, sems):
        me, sib, xn, yn, _, _, _ = _gather_places()
        for a in range(na):
            for k, to in enumerate((sib, xn, yn)):
                _slot_copy(slots[a], land[a], me, me, sems[0].at[3 * a + k], sems[1].at[3 * a + k], to).start()

    return _split_call(body, landing, [], [3 * na, 3 * na], after, name)


def _gather_relay(gathered, sems1, slots, after, name):
    na = len(gathered)

    def body(gath, taken, given):
        me, sib, xn, yn, _, relay_from, relay_to = _gather_places()
        for a in range(na):
            for k, peer in enumerate((sib, xn, yn)):
                arrival = _slot_copy(slots[a], gath[a], me, peer, taken[0].at[3 * a + k], taken[1].at[3 * a + k], peer)
                arrival.wait_send()
                arrival.wait_recv()
        for a in range(na):
            _slot_copy(slots[a], gath[a], relay_from, relay_from, given[0].at[a], given[1].at[a], relay_to).start()
            for k, peer in enumerate((xn, yn)):
                _slot_copy(slots[a], gath[a], peer, peer, given[2].at[2 * a + k], given[3].at[2 * a + k], sib).start()

    return _split_call(body, gathered, sems1, [na, na, 2 * na, 2 * na], after, name)


def _gather_pass(gathered, relay_sems, slots, after, name):
    na = len(gathered)

    def body(gath, taken, given):
        me, sib, xn, yn, diag, relay_from, relay_to = _gather_places()
        for a in range(na):
            _slot_copy(slots[a], gath[a], relay_from, relay_from, taken[0].at[a], taken[1].at[a], relay_to).wait_send()
            _slot_copy(slots[a], gath[a], me, diag, taken[0].at[a], taken[1].at[a], relay_to).wait_recv()
        for a in range(na):
            _slot_copy(slots[a], gath[a], diag, diag, given[0].at[a], given[1].at[a], sib).start()

    return _split_call(body, gathered, relay_sems, [na, na], after, name)


def _gather_finish(gathered, pass_sems, diag_sems, slots, after, name):
    na = len(gathered)

    def body(gath, taken, _):
        (x, y, c), sib, xn, yn, diag, _, _ = _gather_places()
        for a in range(na):
            for k, peer in enumerate((xn, yn)):
                passed = _slot_copy(slots[a], gath[a], peer, (peer[0], peer[1], 1 - c), taken[0].at[2 * a + k],
                                    taken[1].at[2 * a + k], sib)
                passed.wait_send()
                passed.wait_recv()
            passed = _slot_copy(slots[a], gath[a], diag, (diag[0], diag[1], 1 - c), taken[2].at[a], taken[3].at[a], sib)
            passed.wait_send()
            passed.wait_recv()

    return _split_call(body, gathered, list(pass_sems) + list(diag_sems), [], after, name, token=False)[1]


def _pair_copy(view, src, land, send_sems, recv_sems, chip):
    x, y, c = _place()
    return pltpu.make_async_remote_copy(
        src_ref=view(src, chip, 1 - c), dst_ref=land.at[chip], send_sem=send_sems.at[chip], recv_sem=recv_sems.at[chip],
        device_id=(x, y, 1 - c), device_id_type=MESH)


def _pair_start(grad, view, block, after, name):
    def body(src, land, after_ref, send_sems, recv_sems, src_thru, land_thru, token):
        for chip in range(N_CHIP):
            _pair_copy(view, src, land, send_sems, recv_sems, chip).start()
        token[...] = jnp.zeros_like(token)

    sems = pltpu.SemaphoreType.DMA((N_CHIP,))
    land = jax.ShapeDtypeStruct((N_CHIP, *block), BF16)
    return _pcall(
        body, in_specs=[IN_HBM, IN_HBM, HBM], out_specs=[SEM, SEM, IN_HBM, IN_HBM, TOKEN_SPEC],
        out_shape=[sems, sems, _hbm_like(grad), _hbm_like(land), TOKEN], input_output_aliases={0: 2, 1: 3},
        compiler_params=pltpu.CompilerParams(has_side_effects=EFFECT), name=name,
    )(_in_hbm(grad), _landing(land), after)


def _pair_wait(grad, recv, send_sems, recv_sems, view, after, name):
    def body(src, land, send, recv_s, after_ref, src_thru, land_thru):
        for chip in range(N_CHIP):
            copy = _pair_copy(view, src, land, send, recv_s, chip)
            copy.wait_send()
            copy.wait_recv()

    return _pcall(
        body, in_specs=[IN_HBM, IN_HBM, SEM, SEM, HBM], out_specs=[IN_HBM, IN_HBM],
        out_shape=[_hbm_like(grad), _hbm_like(recv)], input_output_aliases={0: 0, 1: 1},
        compiler_params=pltpu.CompilerParams(has_side_effects=EFFECT), name=name,
    )(grad, recv, send_sems, recv_sems, after)


def _chip_start(pair, after, name):
    def body(src, land, after_ref, send_sems, recv_sems, src_thru, land_thru, token):
        x, y, c = _place()
        for j, (px, py) in enumerate(_other_chips(x, y)):
            pltpu.make_async_remote_copy(
                src_ref=src.at[2 * px + py], dst_ref=land.at[2 * x + y], send_sem=send_sems.at[j], recv_sem=recv_sems.at[j],
                device_id=(px, py, c), device_id_type=MESH).start()
        token[...] = jnp.zeros_like(token)

    sems = pltpu.SemaphoreType.DMA((3,))
    return _pcall(
        body, in_specs=[IN_HBM, IN_HBM, HBM], out_specs=[SEM, SEM, IN_HBM, IN_HBM, TOKEN_SPEC],
        out_shape=[sems, sems, _hbm_like(pair), _hbm_like(pair), TOKEN], input_output_aliases={0: 2, 1: 3},
        compiler_params=pltpu.CompilerParams(has_side_effects=EFFECT), name=name,
    )(_in_hbm(pair), _landing(pair), after)


def _chip_wait(pair, parts, send_sems, recv_sems, after, name):
    def body(src, land, send, recv, after_ref, src_thru, land_thru):
        x, y, c = _place()
        for j, (px, py) in enumerate(_other_chips(x, y)):
            copy = pltpu.make_async_remote_copy(
                src_ref=src.at[2 * px + py], dst_ref=land.at[2 * px + py], send_sem=send.at[j], recv_sem=recv.at[j],
                device_id=(px, py, c), device_id_type=MESH)
            copy.wait_send()
            copy.wait_recv()

    return _pcall(
        body, in_specs=[IN_HBM, IN_HBM, SEM, SEM, HBM], out_specs=[IN_HBM, IN_HBM],
        out_shape=[_hbm_like(pair), _hbm_like(parts)], input_output_aliases={0: 0, 1: 1},
        compiler_params=pltpu.CompilerParams(has_side_effects=EFFECT), name=name,
    )(pair, parts, send_sems, recv_sems, after)


def _pair_add(core, grad, recv, block, grad_spec, name):
    _, R, C = recv.shape
    tr = block

    def body(c_ref, g_ref, r_ref, o_ref):
        o_ref[...] = (g_ref[...].astype(F32) + r_ref[...].astype(F32)).astype(BF16)

    grid_spec = pltpu.PrefetchScalarGridSpec(
        num_scalar_prefetch=1, grid=(N_CHIP, R // tr),
        in_specs=[grad_spec, pl.BlockSpec((None, tr, C), lambda k, i, c: (k, i, 0))],
        out_specs=pl.BlockSpec((None, tr, C), lambda k, i, c: (k, i, 0)))
    return _pcall(body, grid_spec=grid_spec, out_shape=jax.ShapeDtypeStruct(recv.shape, BF16),
                  compiler_params=_params("parallel", "parallel"), name=name)(core, grad, recv)


def _small_copies(gath, send_sems, recv_sems):
    x, y, c = _place()
    peers = [(x, y, 1 - c)] + [(px, py, pc) for px, py in _other_chips(x, y) for pc in (c, 1 - c)]
    pairs = []
    for a, ref in enumerate(gath):
        mine = ref.at[4 * x + 2 * y + c]
        for k, (px, py, pc) in enumerate(peers):
            sems = dict(send_sem=send_sems.at[7 * a + k], recv_sem=recv_sems.at[7 * a + k], device_id=(px, py, pc),
                        device_id_type=MESH)
            pairs.append((pltpu.make_async_remote_copy(src_ref=mine, dst_ref=mine, **sems),
                          pltpu.make_async_remote_copy(src_ref=mine, dst_ref=ref.at[4 * px + 2 * py + pc], **sems)))
    return pairs


def _small_start(landing, after, name):
    na = len(landing)

    def body(*refs):
        for send, _ in _small_copies(refs[:na], refs[na + 1], refs[na + 2]):
            send.start()
        refs[-1][...] = jnp.zeros_like(refs[-1])

    sems = pltpu.SemaphoreType.DMA((7 * na,))
    outs = _pcall(
        body, in_specs=[IN_HBM] * na + [HBM], out_specs=[SEM, SEM] + [IN_HBM] * na + [TOKEN_SPEC],
        out_shape=[sems, sems] + [_hbm_like(s) for s in landing] + [TOKEN],
        input_output_aliases={a: 2 + a for a in range(na)},
        compiler_params=pltpu.CompilerParams(has_side_effects=EFFECT), name=name,
    )(*[_in_hbm(s) for s in landing], after)
    return outs[0], outs[1], outs[2:2 + na], outs[-1]


def _small_wait(gathered, send_sems, recv_sems, after, name):
    na = len(gathered)

    def body(*refs):
        for send, arrival in _small_copies(refs[:na], refs[na], refs[na + 1]):
            send.wait_send()
            arrival.wait_recv()

    return list(_pcall(
        body, in_specs=[IN_HBM] * na + [SEM, SEM, HBM], out_specs=[IN_HBM] * na,
        out_shape=[_hbm_like(g) for g in gathered], input_output_aliases={a: a for a in range(na)},
        compiler_params=pltpu.CompilerParams(has_side_effects=EFFECT), name=name,
    )(*gathered, send_sems, recv_sems, after))


def _small_finish(gathered, params, name):
    na, npar = len(gathered), len(params)

    def body(*refs):
        g_refs, wmv = refs[:na], refs[na:na + 3 * npar]
        o_sums, o_params = refs[na + 3 * npar:2 * na + 3 * npar], refs[2 * na + 3 * npar:]
        sums = []
        for a in range(na):
            acc = g_refs[a][0]
            for k in range(1, N_DEV):
                acc = acc + g_refs[a][k]
            o_sums[a][...] = acc
            sums.append(acc)
        for j, (a, row, _, _, _) in enumerate(params):
            g = sums[a][row:row + 1, :]
            d, mn, vn = _adam_math(wmv[3 * j][...], g, wmv[3 * j + 1][...], wmv[3 * j + 2][...])
            for out, val in zip(o_params[4 * j:4 * j + 4], (g, d, mn, vn)):
                out[...] = val

    vm = pl.BlockSpec(memory_space=pltpu.VMEM)
    flat = [t for p in params for t in p[2:]]
    out_shape = [jax.ShapeDtypeStruct(g.shape[1:], F32) for g in gathered]
    out_shape += [jax.ShapeDtypeStruct(p[2].shape, F32) for p in params for _ in range(4)]
    outs = _pcall(body, in_specs=[vm] * (na + 3 * npar), out_specs=[vm] * len(out_shape), out_shape=out_shape,
                  name=name)(*gathered, *flat)
    return outs[:na], [outs[na + 4 * j:na + 4 * j + 4] for j in range(npar)]


def _local_step(x, tgt, gains, weights):
    g_pre_mix, g_post_mix, g_pre_ffn, g_post_ffn, g_sb, g_dil = gains
    S, D = x.shape
    hs = g_sb.shape[1] // HEAD_DIM
    hd = g_dil.shape[1] // HEAD_DIM
    cos2, sin_signed = _rope_tables(S)

    h1 = _rms_fwd(x, g_pre_mix, "rms_in", after=[weights.start(), cos2, sin_signed])
    w_in_some, there, coming, token = weights.w_in_first(h1)
    proj = _mm_nn_some(h1, w_in_some, there, None, "proj_first", after=token)
    w_in_g = weights.w_in(proj)
    proj = _mm_nn_some(h1, w_in_g, coming, proj, "proj")
    o_sb, ct_sb, mixed = _sb_fwd(proj, g_sb, hs, hs + hd, "sb_fwd", after=weights.relay_out(proj))
    o_dl, lse_dl, mixed = _dil_fwd(proj, cos2, sin_signed, g_dil, mixed, 3 * hs, hd, "dil_fwd", after=weights.after_sb(o_sb))
    w_out_g = weights.w_out(o_dl)
    mix = _mm_nn(mixed, w_out_g, F32, "mix_out", tn=1024)
    x2, h2 = _mid_fwd(x, mix, g_post_mix, g_pre_ffn, "mid_fwd", after=weights.after_mix(mix))
    w_up_g, cwb = weights.w_up(h2)
    u = _mm_nn(h2, w_up_g, BF16, "ffn_up", b_transposed=True)
    y = _geglu_fwd(u, cwb, "geglu_fwd", after=weights.forward_down(u))
    w_down_g = weights.w_down(y)
    f = _mm_nn(y, w_down_g, F32, "ffn_down", tn=1024, tk=2816)

    dy, df, dg_post_ffn, loss = _loss_bwd(x2, f, tgt, g_post_ffn, "loss_bwd")
    dyv = _mm_nt(df, w_down_g, BF16, "d_y", tn=1408)
    dw_down = _mm_tn(y, df, D, BF16, "dw_down", tm=1408, tn=1024)
    dc, dcw_g, dcw_v = _geglu_bwd(u, dyv, cwb, "geglu_bwd", after=weights.grad("w_down", dw_down))
    du = _conv_bwd(dc, cwb, "conv_bwd", after=weights.grad_reduce("w_down", dc))
    dh2 = _mm_nt(du, w_up_g, BF16, "d_h2", tk=1408, b_transposed=True, per_step=2)
    dw_up = _mm_tn(du, h2, D, BF16, "dw_up", tm=1408, tn=1024)
    dx2, dmix, dg_pre_ffn, dg_post_mix = _mid_bwd(
        dy, dh2, x2, mix, g_pre_ffn, g_post_mix, "mid_bwd", after=weights.grad("w_up", dw_up))
    dmixed = _mm_nt(dmix, w_out_g, BF16, "d_mixed", after=weights.grad_reduce("w_up", dmix))
    dw_out = _mm_tn(mixed, dmix, D, BF16, "dw_out", tn=1024)
    dproj, dg_sb = _sb_bwd(proj, g_sb, o_sb, ct_sb, dmixed, 0, hs, "sb_bwd", after=weights.grad("w_out", dw_out))
    dproj, dg_dil = _dil_bwd(proj, cos2, sin_signed, g_dil, o_dl, lse_dl, dmixed, dproj, hs, 3 * hs, hd, "dil_bwd",
                             after=weights.grad_reduce("w_out", dg_sb))
    dw_in = _mm_tn(h1, dproj, w_in_g.shape[2], BF16, "dw_in", tn=768)
    dep = weights.grad_reduce("w_in", weights.meanwhile(weights.grad("w_in", dw_in)))
    dh1 = _mm_nt(dproj, w_in_g, BF16, "d_h1", tk=768, after=dep, per_step=4)
    grad_x, dg_pre_mix = _first_bwd(dx2, dh1, x, g_pre_mix, "first_bwd")
    small = (dg_pre_mix, dg_post_mix, dg_pre_ffn, dg_post_ffn, dg_sb[0:1], dg_dil[0:1], jnp.concatenate([dcw_g, dcw_v], axis=1))
    weights.small(small, loss)
    return loss, grad_x, small


def _pad_cols(a, to):
    return jnp.pad(a, ((0, 0), (0, to - a.shape[1])))


def kernel(x, pre_mix_gain, post_mix_gain, pre_ffn_gain, post_ffn_gain, w_in, sb_out_gain, dil_out_gain, w_out, w_up, conv_w, conv_b, w_down, loss_target, m_pre_mix_gain, m_post_mix_gain, m_pre_ffn_gain, m_post_ffn_gain, m_w_in, m_sb_out_gain, m_dil_out_gain, m_w_out, m_w_up, m_conv_w, m_conv_b, m_w_down, v_pre_mix_gain, v_post_mix_gain, v_pre_ffn_gain, v_post_ffn_gain, v_w_in, v_sb_out_gain, v_dil_out_gain, v_w_out, v_w_up, v_conv_w, v_conv_b, v_w_down):
    xb, tb = x[0], loss_target[0]
    S, D = xb.shape
    w_in, w_out, w_up, w_down, conv_w = w_in[0], w_out[0], w_up[0], w_down[0], conv_w[0]
    n_in, e_rows = w_in.shape[1], w_out.shape[0]
    cu, half = w_up.shape[1], w_down.shape[0]
    assert cu == 2 * half and half % 16 == 0
    cup = -(-cu // LANES) * LANES
    fp = N_CHIP * cup
    px, py, pc = _place()
    me = 4 * px + 2 * py + pc
    core = jnp.reshape(pc, (1,)).astype(jnp.int32)
    chip_ids = jnp.stack([2 * px + py, 2 * (1 - px) + py, 2 * px + 1 - py, 2 * (1 - px) + 1 - py]).astype(jnp.int32)

    w_up_t, m_up_t, v_up_t = (jnp.swapaxes(t, 0, 1) for t in (w_up, m_w_up[0], v_w_up[0]))

    def by_dev(ref, qx, qy, qc):
        return ref.at[4 * qx + 2 * qy + qc]

    def down_slot(ref, qx, qy, qc):
        return ref.at[2 * qx + qy, pl.ds(qc * half, half)]

    def by_pair(ref, chip, k):
        return ref.at[chip, k]

    def down_pair(ref, chip, k):
        return ref.at[chip, pl.ds(k * half, half)]

    def pair_spec(tr, cols):
        return pl.BlockSpec((None, None, tr, cols), lambda k, i, c: (k, c[0], i, 0))

    tr_in, tr_up = _tile(D, 512, 16), _tile(cup, 256, 16)
    grad_plan = {
        "w_in": ((N_CHIP, 2, D, n_in), by_pair, (D, n_in), tr_in, pair_spec(tr_in, n_in)),
        "w_out": ((N_CHIP, 2, e_rows, D), by_pair, (e_rows, D), e_rows, pair_spec(e_rows, D)),
        "w_up": ((N_CHIP, 2, cup, D), by_pair, (cup, D), tr_up, pair_spec(tr_up, D)),
        "w_down": ((N_CHIP, cup, D), down_pair, (half, D), half,
                   pl.BlockSpec((None, half, D), lambda k, i, c: (k, c[0], 0))),
    }

    class Exchanges:
        def __init__(self):
            self.in_flight = {}

        def start(self):
            def own_slot(shard):
                return lax.dynamic_update_index_in_dim(lax.empty((N_DEV, *shard.shape), shard.dtype), shard, me, 0)

            self.group_slots = {"in": [by_dev], "out": [by_dev], "up": [by_dev, by_dev], "down": [down_slot]}
            self.flight = {}
            sems, gath, token = _gather_start([own_slot(w_in.astype(BF16))], [by_dev], core, "gather_in_start")
            self.flight["in"] = (sems, gath)
            zero = token[0, 0]
            self.landing = {
                "out": [own_slot((w_out + zero).astype(BF16))],
                "up": [own_slot(jnp.pad(w_up_t + zero, ((0, cup - cu), (0, 0))).astype(BF16)),
                       own_slot(jnp.pad(conv_w + zero, ((0, 8 - conv_w.shape[0]), (0, cup - cu))))],
                "down": [lax.dynamic_update_slice(jnp.zeros((N_CHIP, cup, D), BF16), (w_down + zero).astype(BF16)[None],
                                                  (2 * px + py, pc * half, 0))]}
            return self.landing["down"][0]

        def begin(self, group, after):
            sems, gath, token = _gather_start(self.landing[group], self.group_slots[group], after, "gather_%s_start" % group)
            self.flight[group] = (sems, gath)
            return token

        def relay(self, group, after):
            sems, gath = self.flight[group]
            sems, gath, token = _gather_relay(gath, sems, self.group_slots[group], after, "gather_%s_relay" % group)
            self.flight[group] = (sems, gath)
            return token

        def pass_on(self, group, after):
            sems, gath = self.flight[group]
            diag_sems, gath, token = _gather_pass(gath, sems[:2], self.group_slots[group], after, "gather_%s_pass" % group)
            self.flight[group] = (sems[2:], diag_sems, gath)
            return token

        def finish(self, group, after):
            pass_sems, diag_sems, gath = self.flight[group]
            return _gather_finish(gath, pass_sems, diag_sems, self.group_slots[group], after, "gather_%s_finish" % group)

        def w_in_first(self, after):
            token = self.begin("up", self.begin("out", self.relay("in", after)))
            there = jnp.stack([me, me ^ 1, me ^ 4, me ^ 2]).astype(jnp.int32)
            coming = jnp.stack([me ^ 5, me ^ 3, me ^ 6, me ^ 7]).astype(jnp.int32)
            return self.flight["in"][1][0], there, coming, token

        def w_in(self, after):
            return self.finish("in", self.pass_on("in", after))[0]

        def relay_out(self, after):
            return self.relay("out", after)

        def after_sb(self, after):
            return self.begin("down", self.relay("up", self.pass_on("out", after)))

        def w_out(self, after):
            return self.finish("out", after)[0].reshape(1, N_DEV * e_rows, D)

        def after_mix(self, after):
            return self.pass_on("up", after)

        def w_up(self, after):
            w_up_g, cw_g = self.finish("up", after)
            cb = _pad_cols(conv_b.reshape(N_DEV, cu), cup).reshape(1, 2 * fp)
            cw_full = jnp.transpose(cw_g[:, :3, :], (1, 0, 2)).reshape(3, 2 * fp)
            cwb = jnp.concatenate([cw_full, cb, jnp.zeros((4, 2 * fp), F32)], axis=0)
            return w_up_g, cwb

        def forward_down(self, after):
            return self.relay("down", after)

        def w_down(self, after):
            return self.finish("down", self.pass_on("down", after))[0].reshape(1, fp, D)

        def small(self, small, loss):
            d_pre_mix, d_post_mix, d_pre_ffn, d_post_ffn, d_sb, d_dil, d_conv = small

            def rows_of(*vectors):
                n = vectors[0].shape[1]
                row = lax.broadcasted_iota(jnp.int32, (8, n), 0)
                out = jnp.zeros((8, n), F32)
                for k, vec in enumerate(vectors):
                    out = jnp.where(row == k, vec, out)
                return out

            parts = [rows_of(d_pre_mix, d_post_mix, d_pre_ffn, d_post_ffn, jnp.broadcast_to(loss[:, :1], (1, D))),
                     rows_of(d_sb, d_dil), d_conv]
            landing = [lax.dynamic_update_index_in_dim(lax.empty((N_DEV, *p.shape), F32), p, me, 0) for p in parts]
            self.small_flight = _small_start(landing, parts[0], "small_start")

        def small_sums(self, after):
            send, recv, gath, _ = self.small_flight
            gath = _small_wait(gath, send, recv, after, "small_wait")
            params = [(0, 0, pre_mix_gain, m_pre_mix_gain, v_pre_mix_gain), (0, 1, post_mix_gain, m_post_mix_gain, v_post_mix_gain),
                      (0, 2, pre_ffn_gain, m_pre_ffn_gain, v_pre_ffn_gain), (0, 3, post_ffn_gain, m_post_ffn_gain, v_post_ffn_gain),
                      (1, 0, sb_out_gain, m_sb_out_gain, v_sb_out_gain), (1, 1, dil_out_gain, m_dil_out_gain, v_dil_out_gain)]
            (gains_sum, _, conv_sum), gain_steps = _small_finish(gath, params, "small_finish")
            return gains_sum[4, 0], conv_sum, gain_steps

        def grad(self, name, dw):
            view_shape, view, block, tr, spec = grad_plan[name]
            send, recv_sems, dw, recv, token = _pair_start(dw.reshape(view_shape), view, block, core, "pair_start_" + name)
            self.in_flight[name] = (dw, recv, send, recv_sems)
            return token

        def grad_reduce(self, name, after):
            _, view, _, tr, spec = grad_plan[name]
            dw, recv = _pair_wait(*self.in_flight[name], view, after, "pair_wait_" + name)
            pair = _pair_add(core, dw, recv, tr, spec, "pair_add_" + name)
            send, recv_sems, pair, parts, token = _chip_start(pair, recv, "chip_start_" + name)
            self.in_flight[name] = (pair, parts, send, recv_sems)
            self.last_token = token
            return token

        def meanwhile(self, token):
            self.out_w_down = _adamw_chips(w_down, *self.grad_parts("w_down", token), chip_ids, m_w_down[0], v_w_down[0],
                                           "adam_w_down")
            return self.out_w_down[1]

        def grad_parts(self, name, after):
            return _chip_wait(*self.in_flight[name], after, "chip_wait_" + name)

    exchanges = Exchanges()
    gains = (pre_mix_gain, post_mix_gain, pre_ffn_gain, post_ffn_gain, sb_out_gain, dil_out_gain)
    loss, grad_x, small = _local_step(xb, tb, gains, exchanges)


    out_w_down = exchanges.out_w_down
    out_up_t = _adamw_chips(w_up_t, *exchanges.grad_parts("w_up", exchanges.small_flight[3]), chip_ids, m_up_t, v_up_t, "adam_w_up")
    out_w_up = [jnp.swapaxes(o, 0, 1) for o in out_up_t]
    out_w_out = _adamw_chips(w_out, *exchanges.grad_parts("w_out", out_up_t[1]), chip_ids, m_w_out[0], v_w_out[0], "adam_w_out")
    out_w_in = _adamw_chips(w_in, *exchanges.grad_parts("w_in", out_w_out[1]), chip_ids, m_w_in[0], v_w_in[0], "adam_w_in")
    loss_out, g_conv, gain_steps = exchanges.small_sums(out_w_in[1])
    out_pre_mix, out_post_mix, out_pre_ffn, out_post_ffn, out_sb, out_dil = gain_steps
    g_conv_b = g_conv[3].reshape(N_DEV, cup)[:, :cu].reshape(1, N_DEV * cu)
    g_conv_w = lax.dynamic_index_in_dim(g_conv[0:3].reshape(3, N_DEV, cup), me, axis=1, keepdims=False)[:, :cu]
    out_conv_b = _adamw(conv_b, g_conv_b[None], m_conv_b, v_conv_b, "adam_conv_b")
    out_conv_w = _adamw(conv_w, g_conv_w[None], m_conv_w[0], v_conv_w[0], "adam_conv_w")

    order = [out_pre_mix, out_post_mix, out_pre_ffn, out_post_ffn, [o[None] for o in out_w_in], out_sb, out_dil,
             [o[None] for o in out_w_out], [o[None] for o in out_w_up], [o[None] for o in out_conv_w], out_conv_b,
             [o[None] for o in out_w_down]]
    outs = [loss_out, grad_x[None]]
    for k in range(4):
        outs += [o[k] for o in order]
    return tuple(outs)
```

```python
import math

import jax
import jax.numpy as jnp
from jax import lax
from jax.experimental import pallas as pl
from jax.experimental.pallas import tpu as pltpu

F32 = jnp.float32
BF16 = jnp.bfloat16
HEAD_DIM = 128
LANES = 128
KEY_BLOCK = 128
DILATIONS = (1, 4, 16)
RMS_EPS = 1e-6
ROPE_THETA = 10000.0
NEG = -1e30
ADAM_LR, ADAM_B1, ADAM_B2, ADAM_EPS, ADAM_WD, ADAM_STEP = 0.001, 0.9, 0.999, 1e-08, 0.01, 10
MESH = pl.DeviceIdType.MESH
N_DEV = 8
N_CHIP = 4
HBM = pl.BlockSpec(memory_space=pl.ANY)
VMEM_LIMIT = 56 * 1024 * 1024

_pcall = pl.pallas_call


def _tile(n, pref, mult=LANES):
    best = None
    t = mult
    while t <= min(n, pref):
        if n % t == 0:
            best = t
        t += mult
    return n if best is None else best


def _params(*sem):
    return pltpu.CompilerParams(dimension_semantics=sem, vmem_limit_bytes=VMEM_LIMIT)


def _following(after, body, in_specs, args):
    afters = [a for a in (after if isinstance(after, (list, tuple)) else [after]) if a is not None]
    n = len(args)

    def ordered(*refs):
        body(*refs[:n], *refs[n + len(afters):])

    return ordered, list(in_specs) + [HBM] * len(afters), list(args) + afters


def _dot(a, b, dims):
    return lax.dot_general(a, b, (dims, ((), ())), preferred_element_type=F32)


NN = ((1,), (0,))
NT = ((1,), (1,))
TN = ((0,), (0,))


def _mm_body(dims, nk, tile):
    if nk == 1:
        def single(a_ref, b_ref, o_ref):
            o_ref[...] = _dot(a_ref[...].astype(BF16), b_ref[...].astype(BF16), dims).astype(o_ref.dtype)

        return single, []

    def body(a_ref, b_ref, o_ref, acc_ref):
        k = pl.program_id(2)

        @pl.when(k == 0)
        def _():
            acc_ref[...] = jnp.zeros_like(acc_ref)

        acc_ref[...] += _dot(a_ref[...].astype(BF16), b_ref[...].astype(BF16), dims)

        @pl.when(k == nk - 1)
        def _():
            o_ref[...] = acc_ref[...].astype(o_ref.dtype)

    return body, [pltpu.VMEM(tile, F32)]


def _mm_nn(a, b3, out_dtype, name, tm=1024, tn=1408, tk=2048, b_transposed=False):
    M, K = a.shape
    C, n = b3.shape[0], b3.shape[1 if b_transposed else 2]
    tm, tk, tn = _tile(M, tm, 8), _tile(K, tk), _tile(n, tn)
    npc, nk = n // tn, K // tk
    body, scratch = _mm_body(NT if b_transposed else NN, nk, (tm, tn))
    b_spec = (pl.BlockSpec((None, tn, tk), lambda i, j, k: (j // npc, j % npc, k)) if b_transposed
              else pl.BlockSpec((None, tk, tn), lambda i, j, k: (j // npc, k, j % npc)))
    return _pcall(
        body, grid=(M // tm, C * npc, nk),
        in_specs=[pl.BlockSpec((tm, tk), lambda i, j, k: (i, k)), b_spec],
        out_specs=pl.BlockSpec((tm, tn), lambda i, j, k: (i, j)),
        out_shape=jax.ShapeDtypeStruct((M, C * n), out_dtype), scratch_shapes=scratch,
        compiler_params=_params("parallel", "parallel", "arbitrary"), name=name)(a, b3)


def _mm_nn_some(a, b3, chunks, into, name, after=None, tm=1024):
    M, K = a.shape
    C, _, n = b3.shape
    tm = _tile(M, tm, 8)
    kept = [] if into is None else [into]

    def body(ids_ref, a_ref, b_ref, *rest):
        rest[-1][...] = _dot(a_ref[...].astype(BF16), b_ref[...], NN).astype(BF16)

    body, in_specs, args = _following(
        after, body, [pl.BlockSpec((tm, K), lambda i, j, ids: (i, 0)),
                      pl.BlockSpec((None, K, n), lambda i, j, ids: (ids[j], 0, 0))] + [HBM] * len(kept),
        [chunks, a, b3] + kept)
    grid_spec = pltpu.PrefetchScalarGridSpec(
        num_scalar_prefetch=1, grid=(M // tm, chunks.shape[0]), in_specs=in_specs,
        out_specs=pl.BlockSpec((tm, n), lambda i, j, ids: (i, ids[j])))
    return _pcall(body, grid_spec=grid_spec, out_shape=jax.ShapeDtypeStruct((M, C * n), BF16),
                  input_output_aliases={3: 0} if kept else {},
                  compiler_params=_params("parallel", "arbitrary"), name=name)(*args)


def _mm_nt(a, b3, out_dtype, name, tm=1024, tn=1024, tk=2048, after=None, b_transposed=False, per_step=1):
    M, _ = a.shape
    C, N, n = (b3.shape[0], b3.shape[2], b3.shape[1]) if b_transposed else b3.shape
    tm, tn, tk = _tile(M, tm, 8), _tile(N, tn), _tile(n, tk)
    dims = NN if b_transposed else NT
    extra = [] if after is None else [after]
    if per_step > 1 and tk == n and C % per_step == 0:
        nk, scratch = C // per_step, [pltpu.VMEM((tm, tn), F32)]
        b3 = b3.reshape(nk, per_step, *b3.shape[1:])
        a_spec = pl.BlockSpec((tm, per_step * n), lambda i, j, k: (i, k))
        if b_transposed:
            b_spec = pl.BlockSpec((None, per_step, n, tn), lambda i, j, k: (k, 0, 0, j))
        else:
            b_spec = pl.BlockSpec((None, per_step, tn, n), lambda i, j, k: (k, 0, j, 0))

        def body(a_ref, b_ref, *rest):
            o_ref, acc_ref = rest[len(extra):]
            k = pl.program_id(2)

            @pl.when(k == 0)
            def _():
                acc_ref[...] = jnp.zeros_like(acc_ref)

            b = b_ref[...].astype(BF16)
            b = b.reshape(per_step * n, tn) if b_transposed else jnp.concatenate([b[u] for u in range(per_step)], axis=1)
            acc_ref[...] += _dot(a_ref[...].astype(BF16), b, dims)

            @pl.when(k == nk - 1)
            def _():
                o_ref[...] = acc_ref[...].astype(o_ref.dtype)
    else:
        kpc = n // tk
        nk = C * kpc
        inner, scratch = _mm_body(dims, nk, (tm, tn))
        a_spec = pl.BlockSpec((tm, tk), lambda i, j, k: (i, k))
        b_spec = (pl.BlockSpec((None, tk, tn), lambda i, j, k: (k // kpc, k % kpc, j)) if b_transposed
                  else pl.BlockSpec((None, tn, tk), lambda i, j, k: (k // kpc, j, k % kpc)))

        def body(a_ref, b_ref, *rest):
            inner(a_ref, b_ref, *rest[len(extra):])

    return _pcall(
        body, grid=(M // tm, N // tn, nk), in_specs=[a_spec, b_spec] + [HBM] * len(extra),
        out_specs=pl.BlockSpec((tm, tn), lambda i, j, k: (i, j)),
        out_shape=jax.ShapeDtypeStruct((M, N), out_dtype), scratch_shapes=scratch,
        compiler_params=_params("parallel", "parallel", "arbitrary"), name=name)(a, b3, *extra)


def _mm_tn(x, y, n, out_dtype, name, tm=1024, tn=1408, tk=2048, after=None):
    S, P = x.shape
    C = y.shape[1] // n
    tm, tn, tk = _tile(P, tm), _tile(n, tn), _tile(S, tk, 8)
    npc, nk = n // tn, S // tk
    inner, scratch = _mm_body(TN, nk, (tm, tn))
    extra = [] if after is None else [after]

    def body(x_ref, y_ref, *rest):
        inner(x_ref, y_ref, *rest[len(extra):])

    return _pcall(
        body, grid=(P // tm, C * npc, nk),
        in_specs=[pl.BlockSpec((tk, tm), lambda i, j, k: (k, i)),
                  pl.BlockSpec((tk, tn), lambda i, j, k: (k, j))] + [HBM] * len(extra),
        out_specs=pl.BlockSpec((None, tm, tn), lambda i, j, k: (j // npc, i, j % npc)),
        out_shape=jax.ShapeDtypeStruct((C, P, n), out_dtype), scratch_shapes=scratch,
        compiler_params=_params("parallel", "parallel", "arbitrary"), name=name)(x, y, *extra)


def _rms_scale(v):
    return lax.rsqrt(jnp.mean(v * v, axis=-1, keepdims=True) + RMS_EPS)


def _rms_bwd(gy, v, r):
    return r * gy - v * (r * r * r * jnp.mean(gy * v, axis=-1, keepdims=True))


def _rows_spec(tm, d):
    return pl.BlockSpec((tm, d), lambda i: (i, 0))


def _vec_spec(d):
    return pl.BlockSpec((1, d), lambda i: (0, 0))


def _rms_fwd(x, g, name, tm=256, after=None):
    S, D = x.shape

    def body(x_ref, g_ref, h_ref):
        v = x_ref[...]
        h_ref[...] = (v * _rms_scale(v) * g_ref[...]).astype(BF16)

    body, in_specs, args = _following(after, body, [_rows_spec(tm, D), _vec_spec(D)], [x, g])
    return _pcall(body, grid=(S // tm,), in_specs=in_specs, out_specs=_rows_spec(tm, D),
                  out_shape=jax.ShapeDtypeStruct((S, D), BF16), compiler_params=_params("parallel"), name=name)(*args)


def _mid_fwd(x, mix, g_post, g_pre, name, tm=256, after=None):
    S, D = x.shape

    def body(x_ref, m_ref, gp_ref, gn_ref, x2_ref, h_ref):
        m = m_ref[...]
        x2 = x_ref[...] + m * _rms_scale(m) * gp_ref[...]
        x2_ref[...] = x2
        h_ref[...] = (x2 * _rms_scale(x2) * gn_ref[...]).astype(BF16)

    body, in_specs, args = _following(
        after, body, [_rows_spec(tm, D), _rows_spec(tm, D), _vec_spec(D), _vec_spec(D)], [x, mix, g_post, g_pre])
    return _pcall(body, grid=(S // tm,), in_specs=in_specs,
                  out_specs=[_rows_spec(tm, D), _rows_spec(tm, D)],
                  out_shape=[jax.ShapeDtypeStruct((S, D), F32), jax.ShapeDtypeStruct((S, D), BF16)],
                  compiler_params=_params("parallel"), name=name)(*args)


def _loss_bwd(x2, f, tgt, g_post, name, tm=256):
    S, D = x2.shape

    def body(x2_ref, f_ref, t_ref, g_ref, dy_ref, df_ref, dg_ref, ls_ref):
        i = pl.program_id(0)

        @pl.when(i == 0)
        def _():
            dg_ref[...] = jnp.zeros_like(dg_ref)
            ls_ref[...] = jnp.zeros_like(ls_ref)

        fv = f_ref[...]
        r = _rms_scale(fv)
        g = g_ref[...]
        err = x2_ref[...] + fv * r * g - t_ref[...]
        ls_ref[...] += jnp.broadcast_to(0.5 * jnp.sum(jnp.mean(err * err, axis=-1, keepdims=True), axis=0, keepdims=True), ls_ref.shape)
        dy = err * (1.0 / D)
        dy_ref[...] = dy
        df_ref[...] = _rms_bwd(dy * g, fv, r).astype(BF16)
        dg_ref[...] += jnp.sum(dy * fv * r, axis=0, keepdims=True)

    return _pcall(body, grid=(S // tm,),
                  in_specs=[_rows_spec(tm, D), _rows_spec(tm, D), _rows_spec(tm, D), _vec_spec(D)],
                  out_specs=[_rows_spec(tm, D), _rows_spec(tm, D), _vec_spec(D), _vec_spec(LANES)],
                  out_shape=[jax.ShapeDtypeStruct((S, D), F32), jax.ShapeDtypeStruct((S, D), BF16),
                             jax.ShapeDtypeStruct((1, D), F32), jax.ShapeDtypeStruct((1, LANES), F32)],
                  compiler_params=_params("arbitrary"), name=name)(x2, f, tgt, g_post)


def _mid_bwd(dy, dh2, x2, mix, g_pre, g_post, name, tm=256, after=None):
    S, D = dy.shape

    def body(dy_ref, dh_ref, x2_ref, m_ref, gn_ref, gp_ref, dx2_ref, dm_ref, dgn_ref, dgp_ref):
        i = pl.program_id(0)

        @pl.when(i == 0)
        def _():
            dgn_ref[...] = jnp.zeros_like(dgn_ref)
            dgp_ref[...] = jnp.zeros_like(dgp_ref)

        x2, dh = x2_ref[...], dh_ref[...].astype(F32)
        r = _rms_scale(x2)
        dx2 = dy_ref[...] + _rms_bwd(dh * gn_ref[...], x2, r)
        dgn_ref[...] += jnp.sum(dh * x2 * r, axis=0, keepdims=True)
        dx2_ref[...] = dx2
        m = m_ref[...]
        rm = _rms_scale(m)
        dm_ref[...] = _rms_bwd(dx2 * gp_ref[...], m, rm).astype(BF16)
        dgp_ref[...] += jnp.sum(dx2 * m * rm, axis=0, keepdims=True)

    body, in_specs, args = _following(
        after, body, [_rows_spec(tm, D)] * 4 + [_vec_spec(D)] * 2, [dy, dh2, x2, mix, g_pre, g_post])
    return _pcall(body, grid=(S // tm,), in_specs=in_specs,
                  out_specs=[_rows_spec(tm, D), _rows_spec(tm, D), _vec_spec(D), _vec_spec(D)],
                  out_shape=[jax.ShapeDtypeStruct((S, D), F32), jax.ShapeDtypeStruct((S, D), BF16),
                             jax.ShapeDtypeStruct((1, D), F32), jax.ShapeDtypeStruct((1, D), F32)],
                  compiler_params=_params("arbitrary"), name=name)(*args)


def _first_bwd(dx2, dh1, x, g_pre, name, tm=256):
    S, D = x.shape

    def body(dx2_ref, dh_ref, x_ref, g_ref, gx_ref, dg_ref):
        i = pl.program_id(0)

        @pl.when(i == 0)
        def _():
            dg_ref[...] = jnp.zeros_like(dg_ref)

        xv, dh = x_ref[...], dh_ref[...].astype(F32)
        r = _rms_scale(xv)
        gx_ref[...] = dx2_ref[...] + _rms_bwd(dh * g_ref[...], xv, r)
        dg_ref[...] += jnp.sum(dh * xv * r, axis=0, keepdims=True)

    return _pcall(body, grid=(S // tm,), in_specs=[_rows_spec(tm, D)] * 3 + [_vec_spec(D)],
                  out_specs=[_rows_spec(tm, D), _vec_spec(D)],
                  out_shape=[jax.ShapeDtypeStruct((S, D), F32), jax.ShapeDtypeStruct((1, D), F32)],
                  compiler_params=_params("arbitrary"), name=name)(dx2, dh1, x, g_pre)


def _logsig_pair(z):
    lb = jnp.minimum(z, 0.0) - jnp.log(1.0 + jnp.exp(-jnp.abs(z)))
    return lb, lb - z


SB_KEY_BLOCK = 256


def _sum_matrix(strict):
    ia = lax.broadcasted_iota(jnp.int32, (SB_KEY_BLOCK, SB_KEY_BLOCK), 0)
    ib = lax.broadcasted_iota(jnp.int32, (SB_KEY_BLOCK, SB_KEY_BLOCK), 1)
    return ((ia > ib) if strict == ">" else (ia < ib)).astype(BF16)


def _row_total(sums, v, col):
    return jnp.broadcast_to(sums[:, col:col + 1] + v[:, col:col + 1], (v.shape[0], LANES))


def _lanes(c, width):
    return jnp.tile(c, (1, width // LANES))


def _split_dot(v, u):
    hi = v.astype(BF16)
    lo = (v - hi.astype(F32)).astype(BF16)
    return _dot(hi, u, NN) + _dot(lo, u, NN)


def _head_out(o, g):
    return o * _rms_scale(o) * g


def _sb_fwd(proj, gain, n_heads, mixed_heads, name, tq=1024, after=None):
    S = proj.shape[0]
    H, tk = n_heads, SB_KEY_BLOCK
    tq = _tile(S, tq, 2 * tk)
    scale = HEAD_DIM ** -0.5

    def body(q_ref, k_ref, v_ref, g_ref, o_ref, ct_ref, mx_ref, oacc, cacc):
        i = pl.program_id(1)
        oacc[...] = jnp.zeros_like(oacc)
        cacc[...] = jnp.zeros_like(cacc)
        sums = _sum_matrix(">")

        def run(blocks):
            scored = []
            for k0, r0, diagonal in blocks:
                rows = pl.ds(r0, tq - r0)
                lb, lk = _logsig_pair(_dot(q_ref[rows, :].astype(BF16), k_ref[pl.ds(k0, tk), :].astype(BF16), NT) * scale)
                causal = None
                if diagonal:
                    causal = (lax.broadcasted_iota(jnp.int32, (tq - r0, tk), 1)
                              < lax.broadcasted_iota(jnp.int32, (tq - r0, tk), 0))
                    lk = jnp.where(causal, lk, 0.0)
                scored.append((k0, rows, causal, lb, lk))
            summed = [(k0, rows, causal, lb, lk, _split_dot(lk, sums)) for k0, rows, causal, lb, lk in scored]
            weights = []
            for k0, rows, causal, lb, lk, after in summed:
                c = cacc[rows, :]
                a = jnp.exp(lb + after + _lanes(c, tk))
                if causal is not None:
                    a = jnp.where(causal, a, 0.0)
                cacc[rows, :] = c + _row_total(after, lk, 0)
                weights.append((k0, rows, a.astype(BF16)))
            for k0, rows, a in weights:
                oacc[rows, :] += _dot(a, v_ref[pl.ds(k0, tk), :].astype(BF16), NN)

        for d in reversed(range(0, tq // tk, 2)):
            run([(pl.multiple_of(i * tq + e * tk, tk), e * tk, True) for e in (d + 1, d)])
        per_trip = tq // tk

        def step(it, carry):
            k0 = pl.multiple_of((i - 1 - it) * tq, tq)
            run([(pl.multiple_of(k0 + e * tk, tk), 0, False) for e in reversed(range(per_trip))])
            return carry

        lax.fori_loop(0, i, step, 0)
        o = oacc[...]
        o_ref[...] = o
        ct_ref[...] = cacc[...]
        mx_ref[...] = _head_out(o, g_ref[...]).astype(BF16)

    blk = pl.BlockSpec((tq, HEAD_DIM), lambda h, i: (i, h))
    body, in_specs, args = _following(
        after, body,
        [blk, pl.BlockSpec((S, HEAD_DIM), lambda h, i: (0, H + h)),
         pl.BlockSpec((S, HEAD_DIM), lambda h, i: (0, 2 * H + h)), pl.BlockSpec((1, HEAD_DIM), lambda h, i: (0, h))],
        [proj, proj, proj, gain])
    return _pcall(
        body, grid=(H, S // tq), in_specs=in_specs,
        out_specs=[blk, blk, blk],
        out_shape=[jax.ShapeDtypeStruct((S, H * HEAD_DIM), F32), jax.ShapeDtypeStruct((S, H * HEAD_DIM), F32),
                   jax.ShapeDtypeStruct((S, mixed_heads * HEAD_DIM), BF16)],
        scratch_shapes=[pltpu.VMEM((tq, HEAD_DIM), F32), pltpu.VMEM((tq, LANES), F32)],
        compiler_params=_params("parallel", "arbitrary"), name=name)(*args)


def _sb_bwd(proj, gain, o_raw, ctot, dmixed, dm_col0, n_heads, name, tq=1024, after=None):
    S = proj.shape[0]
    H, tk = n_heads, SB_KEY_BLOCK
    tq = _tile(S, tq, 2 * tk)
    nq = S // tq
    scale = HEAD_DIM ** -0.5

    def body(q_ref, k_ref, v_ref, g_ref, o_ref, ct_ref, dm_ref, dproj_ref, dg_ref,
             dkacc, dvacc, dqacc, pfx, gcar, dos, stage_q, stage_k, stage_v, out_sems):
        h, i = pl.program_id(0), pl.program_id(1)

        @pl.when(i == 0)
        def _():
            dkacc[...] = jnp.zeros_like(dkacc)
            dvacc[...] = jnp.zeros_like(dvacc)
            dg_ref[...] = jnp.zeros_like(dg_ref)

        o, dm, g = o_ref[...], dm_ref[...].astype(F32), g_ref[...]
        r = _rms_scale(o)
        dos[...] = _rms_bwd(dm * g, o, r).astype(BF16)
        dg_ref[...] += jnp.broadcast_to(jnp.sum(dm * o * r, axis=0, keepdims=True), dg_ref.shape)
        dqacc[...] = jnp.zeros_like(dqacc)
        pfx[...] = jnp.zeros_like(pfx)
        gcar[...] = jnp.zeros_like(gcar)
        later, earlier = _sum_matrix(">"), _sum_matrix("<")

        def run(blocks):
            scored = []
            for k0, r0, diagonal in blocks:
                rows, keys = pl.ds(r0, tq - r0), pl.ds(k0, tk)
                lb, lk = _logsig_pair(_dot(q_ref[rows, :].astype(BF16), k_ref[keys, :].astype(BF16), NT) * scale)
                da = _dot(dos[rows, :], v_ref[keys, :].astype(BF16), NT)
                causal = None
                if diagonal:
                    causal = (lax.broadcasted_iota(jnp.int32, (tq - r0, tk), 1)
                              < lax.broadcasted_iota(jnp.int32, (tq - r0, tk), 0))
                    lk = jnp.where(causal, lk, 0.0)
                scored.append((rows, keys, causal, lb, lk, da))
            summed = [(*blk, _split_dot(blk[4], later)) for blk in scored]
            weighted = []
            for rows, keys, causal, lb, lk, da, after in summed:
                p = pfx[rows, :] + _row_total(after, lk, 0)
                pfx[rows, :] = p
                a = jnp.exp(lb + after + _lanes(ct_ref[rows, :] - p, tk))
                if causal is not None:
                    a = jnp.where(causal, a, 0.0)
                dl = da * a
                weighted.append((rows, keys, causal, lb, a.astype(BF16), dl, _dot(dl.astype(BF16), earlier, NN)))
            cotangents = []
            for rows, keys, causal, lb, a, dl, before in weighted:
                gc = gcar[rows, :]
                gcar[rows, :] = gc + _row_total(before, dl, tk - 1)
                sig = jnp.exp(lb)
                gsum = (before + _lanes(gc, tk)) * sig
                if causal is not None:
                    gsum = jnp.where(causal, gsum, 0.0)
                cotangents.append((rows, keys, a, ((dl * (1.0 - sig) - gsum) * scale).astype(BF16)))
            for rows, keys, a, dz in cotangents:
                q, do = q_ref[rows, :].astype(BF16), dos[rows, :]
                dvacc[keys, :] += _dot(a, do, TN)
                dqacc[rows, :] += _dot(dz, k_ref[keys, :].astype(BF16), NN)
                dkacc[keys, :] += _dot(dz, q, TN)

        per_trip = tq // tk

        def step(j, carry):
            k0 = pl.multiple_of(j * tq, tq)
            run([(pl.multiple_of(k0 + e * tk, tk), 0, False) for e in range(per_trip)])
            return carry

        lax.fori_loop(0, i, step, 0)
        for d in range(0, tq // tk, 2):
            run([(pl.multiple_of(i * tq + e * tk, tk), e * tk, True) for e in (d, d + 1)])
        def columns(block):
            return pl.ds(pl.multiple_of(block * HEAD_DIM, HEAD_DIM), HEAD_DIM)

        dq_out = pltpu.make_async_copy(stage_q, dproj_ref.at[pl.ds(pl.multiple_of(i * tq, tq), tq), columns(h)], out_sems.at[0])
        dkv_out = [pltpu.make_async_copy(stage_k, dproj_ref.at[:, columns(H + h)], out_sems.at[1]),
                   pltpu.make_async_copy(stage_v, dproj_ref.at[:, columns(2 * H + h)], out_sems.at[2])]

        @pl.when((h > 0) | (i > 0))
        def _():
            dq_out.wait()

        stage_q[...] = dqacc[...].astype(BF16)
        dq_out.start()

        @pl.when(i == nq - 1)
        def _():
            @pl.when(h > 0)
            def _():
                for cp in dkv_out:
                    cp.wait()

            stage_k[...] = dkacc[...].astype(BF16)
            stage_v[...] = dvacc[...].astype(BF16)
            for cp in dkv_out:
                cp.start()

        @pl.when((h == H - 1) & (i == nq - 1))
        def _():
            dq_out.wait()
            for cp in dkv_out:
                cp.wait()

    blk = pl.BlockSpec((tq, HEAD_DIM), lambda h, i: (i, h))
    W = H * HEAD_DIM
    body, in_specs, args = _following(
        after, body,
        [blk, pl.BlockSpec((S, HEAD_DIM), lambda h, i: (0, H + h)),
         pl.BlockSpec((S, HEAD_DIM), lambda h, i: (0, 2 * H + h)), pl.BlockSpec((1, HEAD_DIM), lambda h, i: (0, h)),
         blk, blk, pl.BlockSpec((tq, HEAD_DIM), lambda h, i: (i, dm_col0 + h))],
        [proj, proj, proj, gain, o_raw, ctot, dmixed])
    return _pcall(
        body, grid=(H, nq), in_specs=in_specs,
        out_specs=[HBM, pl.BlockSpec((8, HEAD_DIM), lambda h, i: (0, h))],
        out_shape=[jax.ShapeDtypeStruct(proj.shape, BF16), jax.ShapeDtypeStruct((8, W), F32)],
        scratch_shapes=[pltpu.VMEM((S, HEAD_DIM), F32), pltpu.VMEM((S, HEAD_DIM), F32), pltpu.VMEM((tq, HEAD_DIM), F32),
                        pltpu.VMEM((tq, LANES), F32), pltpu.VMEM((tq, LANES), F32), pltpu.VMEM((tq, HEAD_DIM), BF16),
                        pltpu.VMEM((tq, HEAD_DIM), BF16), pltpu.VMEM((S, HEAD_DIM), BF16), pltpu.VMEM((S, HEAD_DIM), BF16),
                        pltpu.SemaphoreType.DMA((3,))],
        compiler_params=_params("arbitrary", "arbitrary"), name=name)(*args)


def _rope_tables(S):
    inv_freq = ROPE_THETA ** (-jnp.arange(0, HEAD_DIM, 2, dtype=F32) / HEAD_DIM)
    ang = jnp.arange(S, dtype=F32)[:, None] * inv_freq[None, :]
    cos, sin = jnp.cos(ang), jnp.sin(ang)
    return jnp.concatenate([cos, cos], axis=1), jnp.concatenate([-sin, sin], axis=1)


def _rope(v, cos2, sin_signed):
    return v * cos2 + pltpu.roll(v, HEAD_DIM // 2, axis=1) * sin_signed


def _dil_rows(d, r, l0, n):
    if d == 1:
        return pl.ds(l0 if isinstance(l0, int) else pl.multiple_of(l0, KEY_BLOCK), n)
    return pl.ds(r + d * l0, n, stride=d)


def _dil_blocks(S, visit):
    B = KEY_BLOCK
    group = 16
    for b, d in enumerate(DILATIONS):
        nb = S // d // B
        if nb == 1:
            g = math.gcd(d, group)

            def trip(t, carry, b=b, d=d, g=g):
                visit([(b, d, t * g + u, 0, True) for u in range(g)])
                return carry

            lax.fori_loop(0, d // g, trip, 0)
        elif d == 1:
            visit([(b, d, 0, 0, True)])
            g = max(k for k in range(1, group + 2) if (nb - 1) % k == 0)

            def trip(t, carry, b=b, d=d, g=g):
                visit([(b, d, 0, (1 + t * g + u) * B, False) for u in range(g)])
                return carry

            lax.fori_loop(0, (nb - 1) // g, trip, 0)
        else:
            g = math.gcd(d, max(group // nb, 1))

            def trip(t, carry, b=b, d=d, nb=nb, g=g):
                visit([(b, d, t * g + u, n * B, n == 0) for u in range(g) for n in range(nb)])
                return carry

            lax.fori_loop(0, d // g, trip, 0)


def _dil_mask(first):
    B = KEY_BLOCK
    nk = B if first else 2 * B
    iq = lax.broadcasted_iota(jnp.int32, (B, nk), 0)
    ik = lax.broadcasted_iota(jnp.int32, (B, nk), 1)
    return (ik <= iq) if first else ((ik >= iq) & (ik <= iq + B))


def _dil_fwd(proj, cos2, sin_signed, gain, mixed, col0, n_heads, name, after=None):
    S = proj.shape[0]
    H, B = n_heads, KEY_BLOCK
    scale = HEAD_DIM ** -0.5
    rc = _tile(S, 256, 8)

    def body(q_ref, k_ref, v_ref, c_ref, s_ref, g_ref, mixed_in, o_ref, l_ref, mx_ref, qr, kr, vf, *per_branch):
        ob, lb = per_branch[:len(DILATIONS)], per_branch[len(DILATIONS):]

        def rope_rows(t, carry):
            rows = pl.ds(pl.multiple_of(t * rc, rc), rc)
            qr[rows, :] = _rope(q_ref[rows, :].astype(F32), c_ref[rows, :], s_ref[rows, :])
            kr[rows, :] = _rope(k_ref[rows, :].astype(F32), c_ref[rows, :], s_ref[rows, :])
            vf[rows, :] = v_ref[rows, :].astype(F32)
            return carry

        lax.fori_loop(0, S // rc, rope_rows, 0)

        def visit(blocks):
            scores = []
            for b, d, r, l0, first in blocks:
                qrows = _dil_rows(d, r, l0, B)
                krows = qrows if first else _dil_rows(d, r, l0 - B, 2 * B)
                s = _dot(qr[qrows, :].astype(BF16), kr[krows, :].astype(BF16), NT) * scale
                scores.append((b, qrows, krows, jnp.where(_dil_mask(first), s, NEG)))
            weights = []
            for b, qrows, krows, s in scores:
                m = jnp.max(s, axis=1, keepdims=True)
                p = jnp.exp(s - m)
                den = jnp.sum(p, axis=1, keepdims=True)
                lb[b][qrows, :] = jnp.broadcast_to(m + jnp.log(den), (B, LANES))
                weights.append((b, qrows, krows, p.astype(BF16), den))
            for b, qrows, krows, p, den in weights:
                ob[b][qrows, :] = _dot(p, vf[krows, :].astype(BF16), NN) / den

        _dil_blocks(S, visit)

        def combine(t, carry):
            rows = pl.ds(pl.multiple_of(t * rc, rc), rc)
            l0, l1, l2 = lb[0][rows, :], lb[1][rows, :], lb[2][rows, :]
            m = jnp.maximum(jnp.maximum(l0, l1), l2)
            w0, w1, w2 = jnp.exp(l0 - m), jnp.exp(l1 - m), jnp.exp(l2 - m)
            den = w0 + w1 + w2
            o = (w0 * ob[0][rows, :] + w1 * ob[1][rows, :] + w2 * ob[2][rows, :]) / den
            o_ref[rows, :] = o
            l_ref[rows, :] = m + jnp.log(den)
            mx_ref[rows, :] = _head_out(o, g_ref[...]).astype(BF16)
            return carry

        lax.fori_loop(0, S // rc, combine, 0)

    def col(k):
        return pl.BlockSpec((S, HEAD_DIM), lambda h: (0, col0 + k * H + h))

    tab = pl.BlockSpec((S, HEAD_DIM), lambda h: (0, 0))
    out = pl.BlockSpec((S, HEAD_DIM), lambda h: (0, h))
    W = H * HEAD_DIM
    first = mixed.shape[1] // HEAD_DIM - H
    body, in_specs, args = _following(
        after, body, [col(0), col(1), col(2), tab, tab, pl.BlockSpec((1, HEAD_DIM), lambda h: (0, h)), HBM],
        [proj, proj, proj, cos2, sin_signed, gain, mixed])
    return _pcall(
        body, grid=(H,), in_specs=in_specs,
        out_specs=[out, out, pl.BlockSpec((S, HEAD_DIM), lambda h: (0, first + h))],
        out_shape=[jax.ShapeDtypeStruct((S, W), F32), jax.ShapeDtypeStruct((S, W), F32),
                   jax.ShapeDtypeStruct(mixed.shape, BF16)],
        input_output_aliases={6: 2},
        scratch_shapes=[pltpu.VMEM((S, HEAD_DIM), F32)] * (3 + 2 * len(DILATIONS)),
        compiler_params=_params("parallel"), name=name)(*args)


def _dil_bwd(proj, cos2, sin_signed, gain, o_raw, lse, dmixed, dproj, dm_col0, col0, n_heads, name, after=None):
    S = proj.shape[0]
    H, B = n_heads, KEY_BLOCK
    scale = HEAD_DIM ** -0.5
    rc = _tile(S, 256, 8)

    def body(q_ref, k_ref, v_ref, c_ref, s_ref, g_ref, o_ref, l_ref, dm_ref, dproj_in, dproj_ref, dg_ref,
             qr, kr, vf, dos, dsum, dqr, dkr, dvv, stage_q, stage_k, stage_v, out_sems):
        dg_ref[...] = jnp.zeros_like(dg_ref)

        def prep(t, carry):
            rows = pl.ds(pl.multiple_of(t * rc, rc), rc)
            qr[rows, :] = _rope(q_ref[rows, :].astype(F32), c_ref[rows, :], s_ref[rows, :])
            kr[rows, :] = _rope(k_ref[rows, :].astype(F32), c_ref[rows, :], s_ref[rows, :])
            vf[rows, :] = v_ref[rows, :].astype(F32)
            o, dm = o_ref[rows, :], dm_ref[rows, :].astype(F32)
            r = _rms_scale(o)
            do = _rms_bwd(dm * g_ref[...], o, r)
            dg_ref[...] += jnp.broadcast_to(jnp.sum(dm * o * r, axis=0, keepdims=True), dg_ref.shape)
            dos[rows, :] = do
            dsum[rows, :] = jnp.broadcast_to(jnp.sum(do * o, axis=1, keepdims=True), (rc, LANES))
            dqr[rows, :] = jnp.zeros((rc, HEAD_DIM), F32)
            dkr[rows, :] = jnp.zeros((rc, HEAD_DIM), F32)
            dvv[rows, :] = jnp.zeros((rc, HEAD_DIM), F32)
            return carry

        lax.fori_loop(0, S // rc, prep, 0)

        def visit(blocks):
            products = []
            for b, d, r, l0, first in blocks:
                qrows = _dil_rows(d, r, l0, B)
                krows = qrows if first else _dil_rows(d, r, l0 - B, 2 * B)
                qs, ks = qr[qrows, :].astype(BF16), kr[krows, :].astype(BF16)
                do = dos[qrows, :].astype(BF16)
                s = jnp.where(_dil_mask(first), _dot(qs, ks, NT) * scale, NEG)
                dp = _dot(do, vf[krows, :].astype(BF16), NT)
                products.append((qrows, krows, qs, ks, do, s, dp))
            cotangents = []
            for qrows, krows, qs, ks, do, s, dp in products:
                p = jnp.exp(s - l_ref[qrows, :][:, 0:1])
                ds = (p * (dp - dsum[qrows, :][:, 0:1]) * scale).astype(BF16)
                cotangents.append((qrows, krows, qs, ks, do, p.astype(BF16), ds))
            for qrows, krows, qs, ks, do, p, ds in cotangents:
                dqr[qrows, :] += _dot(ds, ks, NN)
                dkr[krows, :] += _dot(ds, qs, TN)
                dvv[krows, :] += _dot(p, do, TN)

        _dil_blocks(S, visit)

        def finish(t, carry):
            rows = pl.ds(pl.multiple_of(t * rc, rc), rc)
            c, s = c_ref[rows, :], s_ref[rows, :]
            dq, dk = dqr[rows, :], dkr[rows, :]
            stage_q[rows, :] = (dq * c + pltpu.roll(dq * s, HEAD_DIM // 2, axis=1)).astype(BF16)
            stage_k[rows, :] = (dk * c + pltpu.roll(dk * s, HEAD_DIM // 2, axis=1)).astype(BF16)
            stage_v[rows, :] = dvv[rows, :].astype(BF16)
            return carry

        h = pl.program_id(0)
        outs = [pltpu.make_async_copy(
            stage, dproj_ref.at[:, pl.ds(pl.multiple_of((col0 + k * H + h) * HEAD_DIM, HEAD_DIM), HEAD_DIM)], out_sems.at[k])
            for k, stage in enumerate((stage_q, stage_k, stage_v))]

        @pl.when(h > 0)
        def _():
            for cp in outs:
                cp.wait()

        lax.fori_loop(0, S // rc, finish, 0)
        for cp in outs:
            cp.start()

        @pl.when(h == H - 1)
        def _():
            for cp in outs:
                cp.wait()

    def col(k):
        return pl.BlockSpec((S, HEAD_DIM), lambda h: (0, col0 + k * H + h))

    tab = pl.BlockSpec((S, HEAD_DIM), lambda h: (0, 0))
    out = pl.BlockSpec((S, HEAD_DIM), lambda h: (0, h))
    W = H * HEAD_DIM
    big, half = pltpu.VMEM((S, HEAD_DIM), F32), pltpu.VMEM((S, HEAD_DIM), BF16)
    body, in_specs, args = _following(
        after, body,
        [col(0), col(1), col(2), tab, tab, pl.BlockSpec((1, HEAD_DIM), lambda h: (0, h)), out, out,
         pl.BlockSpec((S, HEAD_DIM), lambda h: (0, dm_col0 + h)), HBM],
        [proj, proj, proj, cos2, sin_signed, gain, o_raw, lse, dmixed, dproj])
    return _pcall(
        body, grid=(H,), in_specs=in_specs,
        out_specs=[HBM, pl.BlockSpec((8, HEAD_DIM), lambda h: (0, h))],
        out_shape=[jax.ShapeDtypeStruct(dproj.shape, BF16), jax.ShapeDtypeStruct((8, W), F32)],
        input_output_aliases={9: 0},
        scratch_shapes=[big, big, big, big, pltpu.VMEM((S, LANES), F32), big, big, big, half, half, half,
                        pltpu.SemaphoreType.DMA((3,))],
        compiler_params=_params("arbitrary"), name=name)(*args)


GELU_C = math.sqrt(2.0 / math.pi)
GELU_A = 0.044715
HALO = 16


def _shifts_down(cur, halo):
    row = lax.broadcasted_iota(jnp.int32, cur.shape, 0)
    first, second = row == 0, row == 1
    last, before_last = halo[HALO - 1:HALO, :], halo[HALO - 2:HALO - 1, :]
    two = jnp.where(first, before_last, jnp.where(second, last, pltpu.roll(cur, 2, axis=0)))
    return two, jnp.where(first, last, pltpu.roll(cur, 1, axis=0))


def _shift_up(cur, halo, k):
    n = cur.shape[0]
    out = pltpu.roll(cur, n - k, axis=0)
    row = lax.broadcasted_iota(jnp.int32, cur.shape, 0)
    for t in range(k):
        out = jnp.where(row == n - k + t, halo[t:t + 1, :], out)
    return out


def _conv3(cur, halo, cw):
    rows = (*_shifts_down(cur, halo), cur)
    return rows[0] * cw[0:1, :] + rows[1] * cw[1:2, :] + cur * cw[2:3, :] + cw[3:4, :], rows


def _gelu_parts(x):
    xx = x * x
    t = jnp.tanh(x * (GELU_C + (GELU_C * GELU_A) * xx))
    half = 0.5 * x
    return half + half * t, t, xx, half


def _gelu_slope(t, xx, half):
    return (0.5 + 0.5 * t) + half * (1.0 - t * t) * (GELU_C + (3.0 * GELU_C * GELU_A) * xx)


def _geglu_specs(tm, tn, ncb):
    hb = tm // HALO

    def cur(off):
        return pl.BlockSpec((tm, tn), lambda j, i: (i, off + j))

    def prev(off):
        return pl.BlockSpec((HALO, tn), lambda j, i: (jnp.maximum(i * hb - 1, 0), off + j))

    def taps(off):
        return pl.BlockSpec((8, tn), lambda j, i: (0, off + j))

    return [cur(0), prev(0), cur(ncb), prev(ncb), taps(0), taps(ncb)]


def _geglu_fwd(u, cwb, name, tm=512, tn=1408, after=None):
    S, F2 = u.shape
    F = F2 // 2
    tm, tn = _tile(S, tm, HALO), _tile(F, tn)
    ncb = F // tn

    def body(g_ref, gp_ref, v_ref, vp_ref, cg_ref, cv_ref, y_ref):
        top = pl.program_id(1) > 0
        gp = jnp.where(top, gp_ref[...].astype(F32), 0.0)
        vp = jnp.where(top, vp_ref[...].astype(F32), 0.0)
        gc = _conv3(g_ref[...].astype(F32), gp, cg_ref[...])[0]
        vc = _conv3(v_ref[...].astype(F32), vp, cv_ref[...])[0]
        y_ref[...] = (_gelu_parts(gc)[0] * vc).astype(BF16)

    body, in_specs, args = _following(after, body, _geglu_specs(tm, tn, ncb), [u, u, u, u, cwb, cwb])
    return _pcall(body, grid=(ncb, S // tm), in_specs=in_specs,
                  out_specs=pl.BlockSpec((tm, tn), lambda j, i: (i, j)),
                  out_shape=jax.ShapeDtypeStruct((S, F), BF16),
                  compiler_params=_params("parallel", "parallel"), name=name)(*args)


def _geglu_bwd(u, dy, cwb, name, tm=256, tn=1408, after=None):
    S, F2 = u.shape
    F = F2 // 2
    tm, tn = _tile(S, tm, HALO), _tile(F, tn)
    ncb = F // tn

    def body(g_ref, gp_ref, v_ref, vp_ref, cg_ref, cv_ref, dy_ref, dc_ref, dwg_ref, dwv_ref):
        i = pl.program_id(1)

        @pl.when(i == 0)
        def _():
            dwg_ref[...] = jnp.zeros_like(dwg_ref)
            dwv_ref[...] = jnp.zeros_like(dwv_ref)

        top = i > 0
        g, v = g_ref[...].astype(F32), v_ref[...].astype(F32)
        gp = jnp.where(top, gp_ref[...].astype(F32), 0.0)
        vp = jnp.where(top, vp_ref[...].astype(F32), 0.0)
        gc, g_rows = _conv3(g, gp, cg_ref[...])
        vc, v_rows = _conv3(v, vp, cv_ref[...])
        act, t, xx, half = _gelu_parts(gc)
        dact = _gelu_slope(t, xx, half)
        dyv = dy_ref[...].astype(F32)
        dgc = dyv * vc * dact
        dvc = dyv * act
        dc_ref[0] = dgc.astype(BF16)
        dc_ref[1] = dvc.astype(BF16)

        def taps(out_ref, dc, rows):
            for k, moved in enumerate(rows):
                out_ref[k:k + 1, :] += jnp.sum(dc * moved, axis=0, keepdims=True)
            out_ref[3:4, :] += jnp.sum(dc, axis=0, keepdims=True)

        taps(dwg_ref, dgc, g_rows)
        taps(dwv_ref, dvc, v_rows)

    body, in_specs, args = _following(
        after, body, _geglu_specs(tm, tn, ncb) + [pl.BlockSpec((tm, tn), lambda j, i: (i, j))], [u, u, u, u, cwb, cwb, dy])
    return _pcall(body, grid=(ncb, S // tm), in_specs=in_specs,
                  out_specs=[pl.BlockSpec((2, tm, tn), lambda j, i: (0, i, j)),
                             pl.BlockSpec((8, tn), lambda j, i: (0, j)), pl.BlockSpec((8, tn), lambda j, i: (0, j))],
                  out_shape=[jax.ShapeDtypeStruct((2, S, F), BF16), jax.ShapeDtypeStruct((8, F), F32),
                             jax.ShapeDtypeStruct((8, F), F32)],
                  compiler_params=_params("parallel", "arbitrary"), name=name)(*args)


def _conv_bwd(dc, cwb, name, tm=512, tn=1408, after=None):
    _, S, F = dc.shape
    tm, tn = _tile(S, tm, HALO), _tile(F, tn)
    ncb, nrb = F // tn, S // tm
    hb = tm // HALO

    def body(c_ref, n_ref, w_ref, du_ref):
        cur = c_ref[...].astype(F32)
        nxt = jnp.where(pl.program_id(2) < nrb - 1, n_ref[...].astype(F32), 0.0)
        w = w_ref[...]
        du = cur * w[2:3, :] + _shift_up(cur, nxt, 1) * w[1:2, :] + _shift_up(cur, nxt, 2) * w[0:1, :]
        du_ref[...] = du.astype(BF16)

    body, in_specs, args = _following(
        after, body,
        [pl.BlockSpec((None, tm, tn), lambda c, j, i: (c, i, j)),
         pl.BlockSpec((None, HALO, tn), lambda c, j, i: (c, jnp.minimum((i + 1) * hb, S // HALO - 1), j)),
         pl.BlockSpec((8, tn), lambda c, j, i: (0, c * ncb + j))], [dc, dc, cwb])
    return _pcall(body, grid=(2, ncb, nrb), in_specs=in_specs,
                  out_specs=pl.BlockSpec((tm, tn), lambda c, j, i: (i, c * ncb + j)),
                  out_shape=jax.ShapeDtypeStruct((S, 2 * F), BF16),
                  compiler_params=_params("parallel", "parallel", "parallel"), name=name)(*args)


def _adam_math(w, g, m, v):
    m = ADAM_B1 * m + (1.0 - ADAM_B1) * g
    v = ADAM_B2 * v + (1.0 - ADAM_B2) * (g * g)
    m_hat = m / (1.0 - ADAM_B1 ** ADAM_STEP)
    v_hat = v / (1.0 - ADAM_B2 ** ADAM_STEP)
    return -ADAM_LR * (m_hat / (jnp.sqrt(v_hat) + ADAM_EPS) + ADAM_WD * w), m, v


def _adamw(w, parts, m, v, name, tr=256):
    R, C = w.shape
    n, _, Cp = parts.shape
    tr = _tile(R, tr, 8)

    def body(w_ref, p_ref, m_ref, v_ref, g_out, d_out, m_out, v_out):
        g = p_ref[0, :, 0:C].astype(F32)
        for k in range(1, n):
            g = g + p_ref[k, :, 0:C].astype(F32)
        d, mn, vn = _adam_math(w_ref[...], g, m_ref[...], v_ref[...])
        g_out[...] = g
        d_out[...] = d
        m_out[...] = mn
        v_out[...] = vn

    spec = pl.BlockSpec((tr, C), lambda i: (i, 0))
    shape = jax.ShapeDtypeStruct((R, C), F32)
    return _pcall(body, grid=(R // tr,), in_specs=[spec, pl.BlockSpec((n, tr, Cp), lambda i: (0, i, 0)), spec, spec],
                  out_specs=[spec] * 4, out_shape=[shape] * 4, compiler_params=_params("parallel"), name=name)(w, parts, m, v)


def _adamw_chips(w, pair, parts, chip_ids, m, v, name, tr=256):
    R, C = w.shape
    Cp = pair.shape[2]
    by_columns = C == Cp and _tile(R, tr, 16) < 64
    tr, tc = (R, _tile(C, 256)) if by_columns else (_tile(R, tr, 16), C)

    def body(ids_ref, w_ref, own_ref, p1_ref, p2_ref, p3_ref, m_ref, v_ref, g_out, d_out, m_out, v_out):
        g = own_ref[:, 0:tc].astype(F32)
        for ref in (p1_ref, p2_ref, p3_ref):
            g = g + ref[:, 0:tc].astype(F32)
        d, mn, vn = _adam_math(w_ref[...], g, m_ref[...], v_ref[...])
        g_out[...] = g
        d_out[...] = d
        m_out[...] = mn
        v_out[...] = vn

    if by_columns:
        spec = pl.BlockSpec((tr, tc), lambda j, ids: (0, j))
    else:
        spec = pl.BlockSpec((tr, tc), lambda i, ids: (i, 0))

    def chip(k):
        if by_columns:
            return pl.BlockSpec((None, tr, tc), lambda j, ids: (ids[k], 0, j))
        return pl.BlockSpec((None, tr, Cp), lambda i, ids: (ids[k], i, 0))

    shape = jax.ShapeDtypeStruct((R, C), F32)
    grid_spec = pltpu.PrefetchScalarGridSpec(
        num_scalar_prefetch=1, grid=(C // tc if by_columns else R // tr,),
        in_specs=[spec, chip(0), chip(1), chip(2), chip(3), spec, spec], out_specs=[spec] * 4)
    return _pcall(body, grid_spec=grid_spec, out_shape=[shape] * 4, compiler_params=_params("parallel"),
                  name=name)(chip_ids, w, pair, parts, parts, parts, m, v)


def _place():
    return lax.axis_index("x"), lax.axis_index("y"), lax.axis_index("c")


def _other_chips(x, y):
    return [(1 - x, y), (x, 1 - y), (1 - x, 1 - y)]


IN_HBM = pl.BlockSpec(memory_space=pltpu.HBM)
SEM = pl.BlockSpec(memory_space=pltpu.SEMAPHORE)
EFFECT = pltpu.SideEffectType.DATAFLOW_SIDE_EFFECTING
TOKEN = jax.ShapeDtypeStruct((8, LANES), F32)
TOKEN_SPEC = pl.BlockSpec(memory_space=pltpu.VMEM)


def _in_hbm(a):
    return pltpu.with_memory_space_constraint(a, pltpu.HBM)


def _landing(shape):
    return _in_hbm(lax.empty(shape.shape, shape.dtype))


def _hbm_like(a):
    return pltpu.HBM(a.shape, a.dtype)


def _gather_places():
    x, y, c = _place()
    relay_from = (c * (1 - x) + (1 - c) * x, c * y + (1 - c) * (1 - y), c)
    relay_to = (c * x + (1 - c) * (1 - x), c * (1 - y) + (1 - c) * y, c)
    return (x, y, c), (x, y, 1 - c), (1 - x, y, c), (x, 1 - y, c), (1 - x, 1 - y, c), relay_from, relay_to


def _slot_copy(slot, ref, src, dst, send_sem, recv_sem, to):
    return pltpu.make_async_remote_copy(src_ref=slot(ref, *src), dst_ref=slot(ref, *dst), send_sem=send_sem,
                                        recv_sem=recv_sem, device_id=to, device_id_type=MESH)


def _split_call(body, arrays, sems_in, sems_out, after, name, token=True):
    na, ni, no = len(arrays), len(sems_in), len(sems_out)

    def wrapped(*refs):
        body(refs[:na], refs[na:na + ni], refs[na + ni + 1:na + ni + 1 + no])
        if token:
            refs[-1][...] = jnp.zeros_like(refs[-1])

    outs = _pcall(
        wrapped, in_specs=[IN_HBM] * na + [SEM] * ni + [HBM],
        out_specs=[SEM] * no + [IN_HBM] * na + ([TOKEN_SPEC] if token else []),
        out_shape=[pltpu.SemaphoreType.DMA((n,)) for n in sems_out] + [_hbm_like(s) for s in arrays] + ([TOKEN] if token else []),
        input_output_aliases={a: no + a for a in range(na)},
        compiler_params=pltpu.CompilerParams(has_side_effects=EFFECT), name=name,
    )(*[_in_hbm(s) for s in arrays], *sems_in, after)
    return list(outs[:no]), list(outs[no:no + na]), (outs[-1] if token else None)


def _gather_start(landing, slots, after, name):
    na = len(landing)

    def body(land, _, sems):
        me, sib, xn, yn, _, _, _ = _gather_places()
        for a in range(na):
            for k, to in enumerate((sib, xn, yn)):
                _slot_copy(slots[a], land[a], me, me, sems[0].at[3 * a + k], sems[1].at[3 * a + k], to).start()

    return _split_call(body, landing, [], [3 * na, 3 * na], after, name)


def _gather_relay(gathered, sems1, slots, after, name):
    na = len(gathered)

    def body(gath, taken, given):
        me, sib, xn, yn, _, relay_from, relay_to = _gather_places()
        for a in range(na):
            for k, peer in enumerate((sib, xn, yn)):
                arrival = _slot_copy(slots[a], gath[a], me, peer, taken[0].at[3 * a + k], taken[1].at[3 * a + k], peer)
                arrival.wait_send()
                arrival.wait_recv()
        for a in range(na):
            _slot_copy(slots[a], gath[a], relay_from, relay_from, given[0].at[a], given[1].at[a], relay_to).start()
            for k, peer in enumerate((xn, yn)):
                _slot_copy(slots[a], gath[a], peer, peer, given[2].at[2 * a + k], given[3].at[2 * a + k], sib).start()

    return _split_call(body, gathered, sems1, [na, na, 2 * na, 2 * na], after, name)


def _gather_pass(gathered, relay_sems, slots, after, name):
    na = len(gathered)

    def body(gath, taken, given):
        me, sib, xn, yn, diag, relay_from, relay_to = _gather_places()
        for a in range(na):
            _slot_copy(slots[a], gath[a], relay_from, relay_from, taken[0].at[a], taken[1].at[a], relay_to).wait_send()
            _slot_copy(slots[a], gath[a], me, diag, taken[0].at[a], taken[1].at[a], relay_to).wait_recv()
        for a in range(na):
            _slot_copy(slots[a], gath[a], diag, diag, given[0].at[a], given[1].at[a], sib).start()

    return _split_call(body, gathered, relay_sems, [na, na], after, name)


def _gather_finish(gathered, pass_sems, diag_sems, slots, after, name):
    na = len(gathered)

    def body(gath, taken, _):
        (x, y, c), sib, xn, yn, diag, _, _ = _gather_places()
        for a in range(na):
            for k, peer in enumerate((xn, yn)):
                passed = _slot_copy(slots[a], gath[a], peer, (peer[0], peer[1], 1 - c), taken[0].at[2 * a + k],
                                    taken[1].at[2 * a + k], sib)
                passed.wait_send()
                passed.wait_recv()
            passed = _slot_copy(slots[a], gath[a], diag, (diag[0], diag[1], 1 - c), taken[2].at[a], taken[3].at[a], sib)
            passed.wait_send()
            passed.wait_recv()

    return _split_call(body, gathered, list(pass_sems) + list(diag_sems), [], after, name, token=False)[1]


def _pair_copy(view, src, land, send_sems, recv_sems, chip):
    x, y, c = _place()
    return pltpu.make_async_remote_copy(
        src_ref=view(src, chip, 1 - c), dst_ref=land.at[chip], send_sem=send_sems.at[chip], recv_sem=recv_sems.at[chip],
        device_id=(x, y, 1 - c), device_id_type=MESH)


def _pair_start(grad, view, block, after, name):
    def body(src, land, after_ref, send_sems, recv_sems, src_thru, land_thru, token):
        for chip in range(N_CHIP):
            _pair_copy(view, src, land, send_sems, recv_sems, chip).start()
        token[...] = jnp.zeros_like(token)

    sems = pltpu.SemaphoreType.DMA((N_CHIP,))
    land = jax.ShapeDtypeStruct((N_CHIP, *block), BF16)
    return _pcall(
        body, in_specs=[IN_HBM, IN_HBM, HBM], out_specs=[SEM, SEM, IN_HBM, IN_HBM, TOKEN_SPEC],
        out_shape=[sems, sems, _hbm_like(grad), _hbm_like(land), TOKEN], input_output_aliases={0: 2, 1: 3},
        compiler_params=pltpu.CompilerParams(has_side_effects=EFFECT), name=name,
    )(_in_hbm(grad), _landing(land), after)


def _pair_wait(grad, recv, send_sems, recv_sems, view, after, name):
    def body(src, land, send, recv_s, after_ref, src_thru, land_thru):
        for chip in range(N_CHIP):
            copy = _pair_copy(view, src, land, send, recv_s, chip)
            copy.wait_send()
            copy.wait_recv()

    return _pcall(
        body, in_specs=[IN_HBM, IN_HBM, SEM, SEM, HBM], out_specs=[IN_HBM, IN_HBM],
        out_shape=[_hbm_like(grad), _hbm_like(recv)], input_output_aliases={0: 0, 1: 1},
        compiler_params=pltpu.CompilerParams(has_side_effects=EFFECT), name=name,
    )(grad, recv, send_sems, recv_sems, after)


def _chip_start(pair, after, name):
    def body(src, land, after_ref, send_sems, recv_sems, src_thru, land_thru, token):
        x, y, c = _place()
        for j, (px, py) in enumerate(_other_chips(x, y)):
            pltpu.make_async_remote_copy(
                src_ref=src.at[2 * px + py], dst_ref=land.at[2 * x + y], send_sem=send_sems.at[j], recv_sem=recv_sems.at[j],
                device_id=(px, py, c), device_id_type=MESH).start()
        token[...] = jnp.zeros_like(token)

    sems = pltpu.SemaphoreType.DMA((3,))
    return _pcall(
        body, in_specs=[IN_HBM, IN_HBM, HBM], out_specs=[SEM, SEM, IN_HBM, IN_HBM, TOKEN_SPEC],
        out_shape=[sems, sems, _hbm_like(pair), _hbm_like(pair), TOKEN], input_output_aliases={0: 2, 1: 3},
        compiler_params=pltpu.CompilerParams(has_side_effects=EFFECT), name=name,
    )(_in_hbm(pair), _landing(pair), after)


def _chip_wait(pair, parts, send_sems, recv_sems, after, name):
    def body(src, land, send, recv, after_ref, src_thru, land_thru):
        x, y, c = _place()
        for j, (px, py) in enumerate(_other_chips(x, y)):
            copy = pltpu.make_async_remote_copy(
                src_ref=src.at[2 * px + py], dst_ref=land.at[2 * px + py], send_sem=send.at[j], recv_sem=recv.at[j],
                device_id=(px, py, c), device_id_type=MESH)
            copy.wait_send()
            copy.wait_recv()

    return _pcall(
        body, in_specs=[IN_HBM, IN_HBM, SEM, SEM, HBM], out_specs=[IN_HBM, IN_HBM],
        out_shape=[_hbm_like(pair), _hbm_like(parts)], input_output_aliases={0: 0, 1: 1},
        compiler_params=pltpu.CompilerParams(has_side_effects=EFFECT), name=name,
    )(pair, parts, send_sems, recv_sems, after)


def _pair_add(core, grad, recv, block, grad_spec, name):
    _, R, C = recv.shape
    tr = block

    def body(c_ref, g_ref, r_ref, o_ref):
        o_ref[...] = (g_ref[...].astype(F32) + r_ref[...].astype(F32)).astype(BF16)

    grid_spec = pltpu.PrefetchScalarGridSpec(
        num_scalar_prefetch=1, grid=(N_CHIP, R // tr),
        in_specs=[grad_spec, pl.BlockSpec((None, tr, C), lambda k, i, c: (k, i, 0))],
        out_specs=pl.BlockSpec((None, tr, C), lambda k, i, c: (k, i, 0)))
    return _pcall(body, grid_spec=grid_spec, out_shape=jax.ShapeDtypeStruct(recv.shape, BF16),
                  compiler_params=_params("parallel", "parallel"), name=name)(core, grad, recv)


def _small_copies(gath, send_sems, recv_sems):
    x, y, c = _place()
    peers = [(x, y, 1 - c)] + [(px, py, pc) for px, py in _other_chips(x, y) for pc in (c, 1 - c)]
    pairs = []
    for a, ref in enumerate(gath):
        mine = ref.at[4 * x + 2 * y + c]
        for k, (px, py, pc) in enumerate(peers):
            sems = dict(send_sem=send_sems.at[7 * a + k], recv_sem=recv_sems.at[7 * a + k], device_id=(px, py, pc),
                        device_id_type=MESH)
            pairs.append((pltpu.make_async_remote_copy(src_ref=mine, dst_ref=mine, **sems),
                          pltpu.make_async_remote_copy(src_ref=mine, dst_ref=ref.at[4 * px + 2 * py + pc], **sems)))
    return pairs


def _small_start(landing, after, name):
    na = len(landing)

    def body(*refs):
        for send, _ in _small_copies(refs[:na], refs[na + 1], refs[na + 2]):
            send.start()
        refs[-1][...] = jnp.zeros_like(refs[-1])

    sems = pltpu.SemaphoreType.DMA((7 * na,))
    outs = _pcall(
        body, in_specs=[IN_HBM] * na + [HBM], out_specs=[SEM, SEM] + [IN_HBM] * na + [TOKEN_SPEC],
        out_shape=[sems, sems] + [_hbm_like(s) for s in landing] + [TOKEN],
        input_output_aliases={a: 2 + a for a in range(na)},
        compiler_params=pltpu.CompilerParams(has_side_effects=EFFECT), name=name,
    )(*[_in_hbm(s) for s in landing], after)
    return outs[0], outs[1], outs[2:2 + na], outs[-1]


def _small_wait(gathered, send_sems, recv_sems, after, name):
    na = len(gathered)

    def body(*refs):
        for send, arrival in _small_copies(refs[:na], refs[na], refs[na + 1]):
            send.wait_send()
            arrival.wait_recv()

    return list(_pcall(
        body, in_specs=[IN_HBM] * na + [SEM, SEM, HBM], out_specs=[IN_HBM] * na,
        out_shape=[_hbm_like(g) for g in gathered], input_output_aliases={a: a for a in range(na)},
        compiler_params=pltpu.CompilerParams(has_side_effects=EFFECT), name=name,
    )(*gathered, send_sems, recv_sems, after))


def _small_finish(gathered, params, name):
    na, npar = len(gathered), len(params)

    def body(*refs):
        g_refs, wmv = refs[:na], refs[na:na + 3 * npar]
        o_sums, o_params = refs[na + 3 * npar:2 * na + 3 * npar], refs[2 * na + 3 * npar:]
        sums = []
        for a in range(na):
            acc = g_refs[a][0]
            for k in range(1, N_DEV):
                acc = acc + g_refs[a][k]
            o_sums[a][...] = acc
            sums.append(acc)
        for j, (a, row, _, _, _) in enumerate(params):
            g = sums[a][row:row + 1, :]
            d, mn, vn = _adam_math(wmv[3 * j][...], g, wmv[3 * j + 1][...], wmv[3 * j + 2][...])
            for out, val in zip(o_params[4 * j:4 * j + 4], (g, d, mn, vn)):
                out[...] = val

    vm = pl.BlockSpec(memory_space=pltpu.VMEM)
    flat = [t for p in params for t in p[2:]]
    out_shape = [jax.ShapeDtypeStruct(g.shape[1:], F32) for g in gathered]
    out_shape += [jax.ShapeDtypeStruct(p[2].shape, F32) for p in params for _ in range(4)]
    outs = _pcall(body, in_specs=[vm] * (na + 3 * npar), out_specs=[vm] * len(out_shape), out_shape=out_shape,
                  name=name)(*gathered, *flat)
    return outs[:na], [outs[na + 4 * j:na + 4 * j + 4] for j in range(npar)]


def _local_step(x, tgt, gains, weights):
    g_pre_mix, g_post_mix, g_pre_ffn, g_post_ffn, g_sb, g_dil = gains
    S, D = x.shape
    hs = g_sb.shape[1] // HEAD_DIM
    hd = g_dil.shape[1] // HEAD_DIM
    cos2, sin_signed = _rope_tables(S)

    h1 = _rms_fwd(x, g_pre_mix, "rms_in", after=[*weights.start(), cos2, sin_signed])
    w_in_some, there, coming, token = weights.w_in_first(h1)
    proj = _mm_nn_some(h1, w_in_some, there, None, "proj_first", after=token)
    w_in_g = weights.w_in(proj)
    proj = _mm_nn_some(h1, w_in_g, coming, proj, "proj")
    o_sb, ct_sb, mixed = _sb_fwd(proj, g_sb, hs, hs + hd, "sb_fwd", after=weights.relay_out(proj))
    o_dl, lse_dl, mixed = _dil_fwd(proj, cos2, sin_signed, g_dil, mixed, 3 * hs, hd, "dil_fwd", after=weights.after_sb(o_sb))
    w_out_g = weights.w_out(o_dl)
    mix = _mm_nn(mixed, w_out_g, F32, "mix_out", tn=1024)
    x2, h2 = _mid_fwd(x, mix, g_post_mix, g_pre_ffn, "mid_fwd", after=weights.after_mix(mix))
    w_up_g, cwb = weights.w_up(h2)
    u = _mm_nn(h2, w_up_g, BF16, "ffn_up", b_transposed=True)
    y = _geglu_fwd(u, cwb, "geglu_fwd", after=weights.forward_down(u))
    w_down_g = weights.w_down(y)
    f = _mm_nn(y, w_down_g, F32, "ffn_down", tn=1024, tk=2816)

    dy, df, dg_post_ffn, loss = _loss_bwd(x2, f, tgt, g_post_ffn, "loss_bwd")
    dyv = _mm_nt(df, w_down_g, BF16, "d_y", tn=1408)
    dw_down = _mm_tn(y, df, D, BF16, "dw_down", tm=1408, tn=1024)
    dc, dcw_g, dcw_v = _geglu_bwd(u, dyv, cwb, "geglu_bwd", after=weights.grad("w_down", dw_down))
    du = _conv_bwd(dc, cwb, "conv_bwd", after=weights.grad_reduce("w_down", dc))
    dh2 = _mm_nt(du, w_up_g, BF16, "d_h2", tk=1408, b_transposed=True, per_step=2)
    dw_up = _mm_tn(du, h2, D, BF16, "dw_up", tm=1408, tn=1024)
    dx2, dmix, dg_pre_ffn, dg_post_mix = _mid_bwd(
        dy, dh2, x2, mix, g_pre_ffn, g_post_mix, "mid_bwd", after=weights.grad("w_up", dw_up))
    dmixed = _mm_nt(dmix, w_out_g, BF16, "d_mixed", after=weights.grad_reduce("w_up", dmix))
    dw_out = _mm_tn(mixed, dmix, D, BF16, "dw_out", tn=1024)
    dproj, dg_sb = _sb_bwd(proj, g_sb, o_sb, ct_sb, dmixed, 0, hs, "sb_bwd", after=weights.grad("w_out", dw_out))
    dproj, dg_dil = _dil_bwd(proj, cos2, sin_signed, g_dil, o_dl, lse_dl, dmixed, dproj, hs, 3 * hs, hd, "dil_bwd",
                             after=weights.grad_reduce("w_out", dg_sb))
    dw_in = _mm_tn(h1, dproj, w_in_g.shape[2], BF16, "dw_in", tn=768)
    dep = weights.grad_reduce("w_in", weights.meanwhile(weights.grad("w_in", dw_in)))
    dh1 = _mm_nt(dproj, w_in_g, BF16, "d_h1", tk=768, after=dep, per_step=4)
    grad_x, dg_pre_mix = _first_bwd(dx2, dh1, x, g_pre_mix, "first_bwd")
    small = (dg_pre_mix, dg_post_mix, dg_pre_ffn, dg_post_ffn, dg_sb[0:1], dg_dil[0:1], jnp.concatenate([dcw_g, dcw_v], axis=1))
    weights.small(small, loss)
    return loss, grad_x, small


def _pad_cols(a, to):
    return jnp.pad(a, ((0, 0), (0, to - a.shape[1])))


def kernel(x, pre_mix_gain, post_mix_gain, pre_ffn_gain, post_ffn_gain, w_in, sb_out_gain, dil_out_gain, w_out, w_up, conv_w, conv_b, w_down, loss_target, m_pre_mix_gain, m_post_mix_gain, m_pre_ffn_gain, m_post_ffn_gain, m_w_in, m_sb_out_gain, m_dil_out_gain, m_w_out, m_w_up, m_conv_w, m_conv_b, m_w_down, v_pre_mix_gain, v_post_mix_gain, v_pre_ffn_gain, v_post_ffn_gain, v_w_in, v_sb_out_gain, v_dil_out_gain, v_w_out, v_w_up, v_conv_w, v_conv_b, v_w_down):
    xb, tb = x[0], loss_target[0]
    S, D = xb.shape
    w_in, w_out, w_up, w_down, conv_w = w_in[0], w_out[0], w_up[0], w_down[0], conv_w[0]
    n_in, e_rows = w_in.shape[1], w_out.shape[0]
    cu, half = w_up.shape[1], w_down.shape[0]
    assert cu == 2 * half and half % 16 == 0
    cup = -(-cu // LANES) * LANES
    fp = N_CHIP * cup
    px, py, pc = _place()
    me = 4 * px + 2 * py + pc
    core = jnp.reshape(pc, (1,)).astype(jnp.int32)
    chip_ids = jnp.stack([2 * px + py, 2 * (1 - px) + py, 2 * px + 1 - py, 2 * (1 - px) + 1 - py]).astype(jnp.int32)

    w_up_t, m_up_t, v_up_t = (jnp.swapaxes(t, 0, 1) for t in (w_up, m_w_up[0], v_w_up[0]))

    def by_dev(ref, qx, qy, qc):
        return ref.at[4 * qx + 2 * qy + qc]

    def down_slot(ref, qx, qy, qc):
        return ref.at[2 * qx + qy, pl.ds(qc * half, half)]

    def by_pair(ref, chip, k):
        return ref.at[chip, k]

    def down_pair(ref, chip, k):
        return ref.at[chip, pl.ds(k * half, half)]

    def pair_spec(tr, cols):
        return pl.BlockSpec((None, None, tr, cols), lambda k, i, c: (k, c[0], i, 0))

    tr_in, tr_up = _tile(D, 512, 16), _tile(cup, 256, 16)
    grad_plan = {
        "w_in": ((N_CHIP, 2, D, n_in), by_pair, (D, n_in), tr_in, pair_spec(tr_in, n_in)),
        "w_out": ((N_CHIP, 2, e_rows, D), by_pair, (e_rows, D), e_rows, pair_spec(e_rows, D)),
        "w_up": ((N_CHIP, 2, cup, D), by_pair, (cup, D), tr_up, pair_spec(tr_up, D)),
        "w_down": ((N_CHIP, cup, D), down_pair, (half, D), half,
                   pl.BlockSpec((None, half, D), lambda k, i, c: (k, c[0], 0))),
    }

    class Exchanges:
        def __init__(self):
            self.in_flight = {}

        def start(self):
            def own_slot(shard):
                return lax.dynamic_update_index_in_dim(lax.empty((N_DEV, *shard.shape), shard.dtype), shard, me, 0)

            self.group_slots = {"in": [by_dev], "out": [by_dev], "up": [by_dev, by_dev], "down": [down_slot]}
            self.flight = {}
            sems, gath, token = _gather_start([own_slot(w_in.astype(BF16))], [by_dev], core, "gather_in_start")
            self.flight["in"] = (sems, gath)
            zero = token[0, 0]
            self.landing = {
                "out": [own_slot((w_out + zero).astype(BF16))],
                "up": [own_slot(jnp.pad(w_up_t + zero, ((0, cup - cu), (0, 0))).astype(BF16)),
                       own_slot(jnp.pad(conv_w + zero, ((0, 8 - conv_w.shape[0]), (0, cup - cu))))],
                "down": [lax.dynamic_update_slice(jnp.zeros((N_CHIP, cup, D), BF16), (w_down + zero).astype(BF16)[None],
                                                  (2 * px + py, pc * half, 0))]}
            return [buffer for group in ("out", "up", "down") for buffer in self.landing[group]]

        def begin(self, group, after):
            sems, gath, token = _gather_start(self.landing[group], self.group_slots[group], after, "gather_%s_start" % group)
            self.flight[group] = (sems, gath)
            return token

        def relay(self, group, after):
            sems, gath = self.flight[group]
            sems, gath, token = _gather_relay(gath, sems, self.group_slots[group], after, "gather_%s_relay" % group)
            self.flight[group] = (sems, gath)
            return token

        def pass_on(self, group, after):
            sems, gath = self.flight[group]
            diag_sems, gath, token = _gather_pass(gath, sems[:2], self.group_slots[group], after, "gather_%s_pass" % group)
            self.flight[group] = (sems[2:], diag_sems, gath)
            return token

        def finish(self, group, after):
            pass_sems, diag_sems, gath = self.flight[group]
            return _gather_finish(gath, pass_sems, diag_sems, self.group_slots[group], after, "gather_%s_finish" % group)

        def w_in_first(self, after):
            token = self.begin("up", self.begin("out", self.relay("in", after)))
            there = jnp.stack([me, me ^ 1, me ^ 4, me ^ 2]).astype(jnp.int32)
            coming = jnp.stack([me ^ 5, me ^ 3, me ^ 6, me ^ 7]).astype(jnp.int32)
            return self.flight["in"][1][0], there, coming, token

        def w_in(self, after):
            return self.finish("in", self.pass_on("in", after))[0]

        def relay_out(self, after):
            return self.relay("out", after)

        def after_sb(self, after):
            return self.begin("down", self.relay("up", self.pass_on("out", after)))

        def w_out(self, after):
            return self.finish("out", after)[0].reshape(1, N_DEV * e_rows, D)

        def after_mix(self, after):
            return self.pass_on("up", after)

        def w_up(self, after):
            w_up_g, cw_g = self.finish("up", after)
            cb = _pad_cols(conv_b.reshape(N_DEV, cu), cup).reshape(1, 2 * fp)
            cw_full = jnp.transpose(cw_g[:, :3, :], (1, 0, 2)).reshape(3, 2 * fp)
            cwb = jnp.concatenate([cw_full, cb, jnp.zeros((4, 2 * fp), F32)], axis=0)
            return w_up_g, cwb

        def forward_down(self, after):
            return self.relay("down", after)

        def w_down(self, after):
            return self.finish("down", self.pass_on("down", after))[0].reshape(1, fp, D)

        def small(self, small, loss):
            d_pre_mix, d_post_mix, d_pre_ffn, d_post_ffn, d_sb, d_dil, d_conv = small

            def rows_of(*vectors):
                n = vectors[0].shape[1]
                row = lax.broadcasted_iota(jnp.int32, (8, n), 0)
                out = jnp.zeros((8, n), F32)
                for k, vec in enumerate(vectors):
                    out = jnp.where(row == k, vec, out)
                return out

            parts = [rows_of(d_pre_mix, d_post_mix, d_pre_ffn, d_post_ffn, jnp.broadcast_to(loss[:, :1], (1, D))),
                     rows_of(d_sb, d_dil), d_conv]
            landing = [lax.dynamic_update_index_in_dim(lax.empty((N_DEV, *p.shape), F32), p, me, 0) for p in parts]
            self.small_flight = _small_start(landing, parts[0], "small_start")

        def small_sums(self, after):
            send, recv, gath, _ = self.small_flight
            gath = _small_wait(gath, send, recv, after, "small_wait")
            params = [(0, 0, pre_mix_gain, m_pre_mix_gain, v_pre_mix_gain), (0, 1, post_mix_gain, m_post_mix_gain, v_post_mix_gain),
                      (0, 2, pre_ffn_gain, m_pre_ffn_gain, v_pre_ffn_gain), (0, 3, post_ffn_gain, m_post_ffn_gain, v_post_ffn_gain),
                      (1, 0, sb_out_gain, m_sb_out_gain, v_sb_out_gain), (1, 1, dil_out_gain, m_dil_out_gain, v_dil_out_gain)]
            (gains_sum, _, conv_sum), gain_steps = _small_finish(gath, params, "small_finish")
            return gains_sum[4, 0], conv_sum, gain_steps

        def grad(self, name, dw):
            view_shape, view, block, tr, spec = grad_plan[name]
            send, recv_sems, dw, recv, token = _pair_start(dw.reshape(view_shape), view, block, core, "pair_start_" + name)
            self.in_flight[name] = (dw, recv, send, recv_sems)
            return token

        def grad_reduce(self, name, after):
            _, view, _, tr, spec = grad_plan[name]
            dw, recv = _pair_wait(*self.in_flight[name], view, after, "pair_wait_" + name)
            pair = _pair_add(core, dw, recv, tr, spec, "pair_add_" + name)
            send, recv_sems, pair, parts, token = _chip_start(pair, recv, "chip_start_" + name)
            self.in_flight[name] = (pair, parts, send, recv_sems)
            self.last_token = token
            return token

        def meanwhile(self, token):
            self.out_w_down = _adamw_chips(w_down, *self.grad_parts("w_down", token), chip_ids, m_w_down[0], v_w_down[0],
                                           "adam_w_down")
            return self.out_w_down[1]

        def grad_parts(self, name, after):
            return _chip_wait(*self.in_flight[name], after, "chip_wait_" + name)

    exchanges = Exchanges()
    gains = (pre_mix_gain, post_mix_gain, pre_ffn_gain, post_ffn_gain, sb_out_gain, dil_out_gain)
    loss, grad_x, small = _local_step(xb, tb, gains, exchanges)


    out_w_down = exchanges.out_w_down
    out_up_t = _adamw_chips(w_up_t, *exchanges.grad_parts("w_up", exchanges.small_flight[3]), chip_ids, m_up_t, v_up_t, "adam_w_up")
    out_w_up = [jnp.swapaxes(o, 0, 1) for o in out_up_t]
    out_w_out = _adamw_chips(w_out, *exchanges.grad_parts("w_out", out_up_t[1]), chip_ids, m_w_out[0], v_w_out[0], "adam_w_out")
    out_w_in = _adamw_chips(w_in, *exchanges.grad_parts("w_in", out_w_out[1]), chip_ids, m_w_in[0], v_w_in[0], "adam_w_in")
    loss_out, g_conv, gain_steps = exchanges.small_sums(out_w_in[1])
    out_pre_mix, out_post_mix, out_pre_ffn, out_post_ffn, out_sb, out_dil = gain_steps
    g_conv_b = g_conv[3].reshape(N_DEV, cup)[:, :cu].reshape(1, N_DEV * cu)
    g_conv_w = lax.dynamic_index_in_dim(g_conv[0:3].reshape(3, N_DEV, cup), me, axis=1, keepdims=False)[:, :cu]
    out_conv_b = _adamw(conv_b, g_conv_b[None], m_conv_b, v_conv_b, "adam_conv_b")
    out_conv_w = _adamw(conv_w, g_conv_w[None], m_conv_w[0], v_conv_w[0], "adam_conv_w")

    order = [out_pre_mix, out_post_mix, out_pre_ffn, out_post_ffn, [o[None] for o in out_w_in], out_sb, out_dil,
             [o[None] for o in out_w_out], [o[None] for o in out_w_up], [o[None] for o in out_conv_w], out_conv_b,
             [o[None] for o in out_w_down]]
    outs = [loss_out, grad_x[None]]
    for k in range(4):
        outs += [o[k] for o in order]
    return tuple(outs)
```

```python
import math

import jax
import jax.numpy as jnp
from jax import lax
from jax.experimental import pallas as pl
from jax.experimental.pallas import tpu as pltpu

F32 = jnp.float32
BF16 = jnp.bfloat16
HEAD_DIM = 128
LANES = 128
KEY_BLOCK = 128
DILATIONS = (1, 4, 16)
RMS_EPS = 1e-6
ROPE_THETA = 10000.0
NEG = -1e30
ADAM_LR, ADAM_B1, ADAM_B2, ADAM_EPS, ADAM_WD, ADAM_STEP = 0.001, 0.9, 0.999, 1e-08, 0.01, 10
MESH = pl.DeviceIdType.MESH
N_DEV = 8
N_CHIP = 4
HBM = pl.BlockSpec(memory_space=pl.ANY)
VMEM_LIMIT = 56 * 1024 * 1024

_pcall = pl.pallas_call


def _tile(n, pref, mult=LANES):
    best = None
    t = mult
    while t <= min(n, pref):
        if n % t == 0:
            best = t
        t += mult
    return n if best is None else best


def _params(*sem):
    return pltpu.CompilerParams(dimension_semantics=sem, vmem_limit_bytes=VMEM_LIMIT)


def _following(after, body, in_specs, args):
    afters = [a for a in (after if isinstance(after, (list, tuple)) else [after]) if a is not None]
    n = len(args)

    def ordered(*refs):
        body(*refs[:n], *refs[n + len(afters):])

    return ordered, list(in_specs) + [HBM] * len(afters), list(args) + afters


def _dot(a, b, dims):
    return lax.dot_general(a, b, (dims, ((), ())), preferred_element_type=F32)


NN = ((1,), (0,))
NT = ((1,), (1,))
TN = ((0,), (0,))


def _mm_body(dims, nk, tile):
    if nk == 1:
        def single(a_ref, b_ref, o_ref):
            o_ref[...] = _dot(a_ref[...].astype(BF16), b_ref[...].astype(BF16), dims).astype(o_ref.dtype)

        return single, []

    def body(a_ref, b_ref, o_ref, acc_ref):
        k = pl.program_id(2)

        @pl.when(k == 0)
        def _():
            acc_ref[...] = jnp.zeros_like(acc_ref)

        acc_ref[...] += _dot(a_ref[...].astype(BF16), b_ref[...].astype(BF16), dims)

        @pl.when(k == nk - 1)
        def _():
            o_ref[...] = acc_ref[...].astype(o_ref.dtype)

    return body, [pltpu.VMEM(tile, F32)]


def _mm_nn(a, b3, out_dtype, name, tm=1024, tn=1408, tk=2048, b_transposed=False):
    M, K = a.shape
    C, n = b3.shape[0], b3.shape[1 if b_transposed else 2]
    tm, tk, tn = _tile(M, tm, 8), _tile(K, tk), _tile(n, tn)
    npc, nk = n // tn, K // tk
    body, scratch = _mm_body(NT if b_transposed else NN, nk, (tm, tn))
    b_spec = (pl.BlockSpec((None, tn, tk), lambda i, j, k: (j // npc, j % npc, k)) if b_transposed
              else pl.BlockSpec((None, tk, tn), lambda i, j, k: (j // npc, k, j % npc)))
    return _pcall(
        body, grid=(M // tm, C * npc, nk),
        in_specs=[pl.BlockSpec((tm, tk), lambda i, j, k: (i, k)), b_spec],
        out_specs=pl.BlockSpec((tm, tn), lambda i, j, k: (i, j)),
        out_shape=jax.ShapeDtypeStruct((M, C * n), out_dtype), scratch_shapes=scratch,
        compiler_params=_params("parallel", "parallel", "arbitrary"), name=name)(a, b3)


def _mm_nn_some(a, b3, chunks, into, name, after=None, tm=2048):
    M, K = a.shape
    C, _, n = b3.shape
    tm = _tile(M, tm, 8)
    kept = [] if into is None else [into]

    def body(ids_ref, a_ref, b_ref, *rest):
        rest[-1][...] = _dot(a_ref[...].astype(BF16), b_ref[...], NN).astype(BF16)

    body, in_specs, args = _following(
        after, body, [pl.BlockSpec((tm, K), lambda i, j, ids: (i, 0)),
                      pl.BlockSpec((None, K, n), lambda i, j, ids: (ids[j], 0, 0))] + [HBM] * len(kept),
        [chunks, a, b3] + kept)
    grid_spec = pltpu.PrefetchScalarGridSpec(
        num_scalar_prefetch=1, grid=(M // tm, chunks.shape[0]), in_specs=in_specs,
        out_specs=pl.BlockSpec((tm, n), lambda i, j, ids: (i, ids[j])))
    return _pcall(body, grid_spec=grid_spec, out_shape=jax.ShapeDtypeStruct((M, C * n), BF16),
                  input_output_aliases={3: 0} if kept else {},
                  compiler_params=_params("parallel", "arbitrary"), name=name)(*args)


def _mm_nt(a, b3, out_dtype, name, tm=1024, tn=1024, tk=2048, after=None, b_transposed=False, per_step=1):
    M, _ = a.shape
    C, N, n = (b3.shape[0], b3.shape[2], b3.shape[1]) if b_transposed else b3.shape
    tm, tn, tk = _tile(M, tm, 8), _tile(N, tn), _tile(n, tk)
    dims = NN if b_transposed else NT
    extra = [] if after is None else [after]
    if per_step > 1 and tk == n and C % per_step == 0:
        nk, scratch = C // per_step, [pltpu.VMEM((tm, tn), F32)]
        b3 = b3.reshape(nk, per_step, *b3.shape[1:])
        a_spec = pl.BlockSpec((tm, per_step * n), lambda i, j, k: (i, k))
        if b_transposed:
            b_spec = pl.BlockSpec((None, per_step, n, tn), lambda i, j, k: (k, 0, 0, j))
        else:
            b_spec = pl.BlockSpec((None, per_step, tn, n), lambda i, j, k: (k, 0, j, 0))

        def body(a_ref, b_ref, *rest):
            o_ref, acc_ref = rest[len(extra):]
            k = pl.program_id(2)

            @pl.when(k == 0)
            def _():
                acc_ref[...] = jnp.zeros_like(acc_ref)

            b = b_ref[...].astype(BF16)
            b = b.reshape(per_step * n, tn) if b_transposed else jnp.concatenate([b[u] for u in range(per_step)], axis=1)
            acc_ref[...] += _dot(a_ref[...].astype(BF16), b, dims)

            @pl.when(k == nk - 1)
            def _():
                o_ref[...] = acc_ref[...].astype(o_ref.dtype)
    else:
        kpc = n // tk
        nk = C * kpc
        inner, scratch = _mm_body(dims, nk, (tm, tn))
        a_spec = pl.BlockSpec((tm, tk), lambda i, j, k: (i, k))
        b_spec = (pl.BlockSpec((None, tk, tn), lambda i, j, k: (k // kpc, k % kpc, j)) if b_transposed
                  else pl.BlockSpec((None, tn, tk), lambda i, j, k: (k // kpc, j, k % kpc)))

        def body(a_ref, b_ref, *rest):
            inner(a_ref, b_ref, *rest[len(extra):])

    return _pcall(
        body, grid=(M // tm, N // tn, nk), in_specs=[a_spec, b_spec] + [HBM] * len(extra),
        out_specs=pl.BlockSpec((tm, tn), lambda i, j, k: (i, j)),
        out_shape=jax.ShapeDtypeStruct((M, N), out_dtype), scratch_shapes=scratch,
        compiler_params=_params("parallel", "parallel", "arbitrary"), name=name)(a, b3, *extra)


def _mm_tn(x, y, n, out_dtype, name, tm=1024, tn=1408, tk=2048, after=None):
    S, P = x.shape
    C = y.shape[1] // n
    tm, tn, tk = _tile(P, tm), _tile(n, tn), _tile(S, tk, 8)
    npc, nk = n // tn, S // tk
    inner, scratch = _mm_body(TN, nk, (tm, tn))
    extra = [] if after is None else [after]

    def body(x_ref, y_ref, *rest):
        inner(x_ref, y_ref, *rest[len(extra):])

    return _pcall(
        body, grid=(P // tm, C * npc, nk),
        in_specs=[pl.BlockSpec((tk, tm), lambda i, j, k: (k, i)),
                  pl.BlockSpec((tk, tn), lambda i, j, k: (k, j))] + [HBM] * len(extra),
        out_specs=pl.BlockSpec((None, tm, tn), lambda i, j, k: (j // npc, i, j % npc)),
        out_shape=jax.ShapeDtypeStruct((C, P, n), out_dtype), scratch_shapes=scratch,
        compiler_params=_params("parallel", "parallel", "arbitrary"), name=name)(x, y, *extra)


def _rms_scale(v):
    return lax.rsqrt(jnp.mean(v * v, axis=-1, keepdims=True) + RMS_EPS)


def _rms_bwd(gy, v, r):
    return r * gy - v * (r * r * r * jnp.mean(gy * v, axis=-1, keepdims=True))


def _rows_spec(tm, d):
    return pl.BlockSpec((tm, d), lambda i: (i, 0))


def _vec_spec(d):
    return pl.BlockSpec((1, d), lambda i: (0, 0))


def _rms_fwd(x, g, name, tm=256, after=None):
    S, D = x.shape

    def body(x_ref, g_ref, h_ref):
        v = x_ref[...]
        h_ref[...] = (v * _rms_scale(v) * g_ref[...]).astype(BF16)

    body, in_specs, args = _following(after, body, [_rows_spec(tm, D), _vec_spec(D)], [x, g])
    return _pcall(body, grid=(S // tm,), in_specs=in_specs, out_specs=_rows_spec(tm, D),
                  out_shape=jax.ShapeDtypeStruct((S, D), BF16), compiler_params=_params("parallel"), name=name)(*args)


def _mid_fwd(x, mix, g_post, g_pre, name, tm=256, after=None):
    S, D = x.shape

    def body(x_ref, m_ref, gp_ref, gn_ref, x2_ref, h_ref):
        m = m_ref[...]
        x2 = x_ref[...] + m * _rms_scale(m) * gp_ref[...]
        x2_ref[...] = x2
        h_ref[...] = (x2 * _rms_scale(x2) * gn_ref[...]).astype(BF16)

    body, in_specs, args = _following(
        after, body, [_rows_spec(tm, D), _rows_spec(tm, D), _vec_spec(D), _vec_spec(D)], [x, mix, g_post, g_pre])
    return _pcall(body, grid=(S // tm,), in_specs=in_specs,
                  out_specs=[_rows_spec(tm, D), _rows_spec(tm, D)],
                  out_shape=[jax.ShapeDtypeStruct((S, D), F32), jax.ShapeDtypeStruct((S, D), BF16)],
                  compiler_params=_params("parallel"), name=name)(*args)


def _loss_bwd(x2, f, tgt, g_post, name, tm=256):
    S, D = x2.shape

    def body(x2_ref, f_ref, t_ref, g_ref, dy_ref, df_ref, dg_ref, ls_ref):
        i = pl.program_id(0)

        @pl.when(i == 0)
        def _():
            dg_ref[...] = jnp.zeros_like(dg_ref)
            ls_ref[...] = jnp.zeros_like(ls_ref)

        fv = f_ref[...]
        r = _rms_scale(fv)
        g = g_ref[...]
        err = x2_ref[...] + fv * r * g - t_ref[...]
        ls_ref[...] += jnp.broadcast_to(0.5 * jnp.sum(jnp.mean(err * err, axis=-1, keepdims=True), axis=0, keepdims=True), ls_ref.shape)
        dy = err * (1.0 / D)
        dy_ref[...] = dy
        df_ref[...] = _rms_bwd(dy * g, fv, r).astype(BF16)
        dg_ref[...] += jnp.sum(dy * fv * r, axis=0, keepdims=True)

    return _pcall(body, grid=(S // tm,),
                  in_specs=[_rows_spec(tm, D), _rows_spec(tm, D), _rows_spec(tm, D), _vec_spec(D)],
                  out_specs=[_rows_spec(tm, D), _rows_spec(tm, D), _vec_spec(D), _vec_spec(LANES)],
                  out_shape=[jax.ShapeDtypeStruct((S, D), F32), jax.ShapeDtypeStruct((S, D), BF16),
                             jax.ShapeDtypeStruct((1, D), F32), jax.ShapeDtypeStruct((1, LANES), F32)],
                  compiler_params=_params("arbitrary"), name=name)(x2, f, tgt, g_post)


def _mid_bwd(dy, dh2, x2, mix, g_pre, g_post, name, tm=256, after=None):
    S, D = dy.shape

    def body(dy_ref, dh_ref, x2_ref, m_ref, gn_ref, gp_ref, dx2_ref, dm_ref, dgn_ref, dgp_ref):
        i = pl.program_id(0)

        @pl.when(i == 0)
        def _():
            dgn_ref[...] = jnp.zeros_like(dgn_ref)
            dgp_ref[...] = jnp.zeros_like(dgp_ref)

        x2, dh = x2_ref[...], dh_ref[...].astype(F32)
        r = _rms_scale(x2)
        dx2 = dy_ref[...] + _rms_bwd(dh * gn_ref[...], x2, r)
        dgn_ref[...] += jnp.sum(dh * x2 * r, axis=0, keepdims=True)
        dx2_ref[...] = dx2
        m = m_ref[...]
        rm = _rms_scale(m)
        dm_ref[...] = _rms_bwd(dx2 * gp_ref[...], m, rm).astype(BF16)
        dgp_ref[...] += jnp.sum(dx2 * m * rm, axis=0, keepdims=True)

    body, in_specs, args = _following(
        after, body, [_rows_spec(tm, D)] * 4 + [_vec_spec(D)] * 2, [dy, dh2, x2, mix, g_pre, g_post])
    return _pcall(body, grid=(S // tm,), in_specs=in_specs,
                  out_specs=[_rows_spec(tm, D), _rows_spec(tm, D), _vec_spec(D), _vec_spec(D)],
                  out_shape=[jax.ShapeDtypeStruct((S, D), F32), jax.ShapeDtypeStruct((S, D), BF16),
                             jax.ShapeDtypeStruct((1, D), F32), jax.ShapeDtypeStruct((1, D), F32)],
                  compiler_params=_params("arbitrary"), name=name)(*args)


def _first_bwd(dx2, dh1, x, g_pre, name, tm=256):
    S, D = x.shape

    def body(dx2_ref, dh_ref, x_ref, g_ref, gx_ref, dg_ref):
        i = pl.program_id(0)

        @pl.when(i == 0)
        def _():
            dg_ref[...] = jnp.zeros_like(dg_ref)

        xv, dh = x_ref[...], dh_ref[...].astype(F32)
        r = _rms_scale(xv)
        gx_ref[...] = dx2_ref[...] + _rms_bwd(dh * g_ref[...], xv, r)
        dg_ref[...] += jnp.sum(dh * xv * r, axis=0, keepdims=True)

    return _pcall(body, grid=(S // tm,), in_specs=[_rows_spec(tm, D)] * 3 + [_vec_spec(D)],
                  out_specs=[_rows_spec(tm, D), _vec_spec(D)],
                  out_shape=[jax.ShapeDtypeStruct((S, D), F32), jax.ShapeDtypeStruct((1, D), F32)],
                  compiler_params=_params("arbitrary"), name=name)(dx2, dh1, x, g_pre)


def _logsig_pair(z):
    lb = jnp.minimum(z, 0.0) - jnp.log(1.0 + jnp.exp(-jnp.abs(z)))
    return lb, lb - z


SB_KEY_BLOCK = 256


def _sum_matrix(strict):
    ia = lax.broadcasted_iota(jnp.int32, (SB_KEY_BLOCK, SB_KEY_BLOCK), 0)
    ib = lax.broadcasted_iota(jnp.int32, (SB_KEY_BLOCK, SB_KEY_BLOCK), 1)
    return ((ia > ib) if strict == ">" else (ia < ib)).astype(BF16)


def _row_total(sums, v, col):
    return jnp.broadcast_to(sums[:, col:col + 1] + v[:, col:col + 1], (v.shape[0], LANES))


def _lanes(c, width):
    return jnp.tile(c, (1, width // LANES))


def _split_dot(v, u):
    hi = v.astype(BF16)
    lo = (v - hi.astype(F32)).astype(BF16)
    return _dot(hi, u, NN) + _dot(lo, u, NN)


def _head_out(o, g):
    return o * _rms_scale(o) * g


def _sb_fwd(proj, gain, n_heads, mixed_heads, name, tq=1024, after=None):
    S = proj.shape[0]
    H, tk = n_heads, SB_KEY_BLOCK
    tq = _tile(S, tq, 2 * tk)
    scale = HEAD_DIM ** -0.5

    def body(q_ref, k_ref, v_ref, g_ref, o_ref, ct_ref, mx_ref, oacc, cacc):
        i = pl.program_id(1)
        oacc[...] = jnp.zeros_like(oacc)
        cacc[...] = jnp.zeros_like(cacc)
        sums = _sum_matrix(">")

        def run(blocks):
            scored = []
            for k0, r0, diagonal in blocks:
                rows = pl.ds(r0, tq - r0)
                lb, lk = _logsig_pair(_dot(q_ref[rows, :].astype(BF16), k_ref[pl.ds(k0, tk), :].astype(BF16), NT) * scale)
                causal = None
                if diagonal:
                    causal = (lax.broadcasted_iota(jnp.int32, (tq - r0, tk), 1)
                              < lax.broadcasted_iota(jnp.int32, (tq - r0, tk), 0))
                    lk = jnp.where(causal, lk, 0.0)
                scored.append((k0, rows, causal, lb, lk))
            summed = [(k0, rows, causal, lb, lk, _split_dot(lk, sums)) for k0, rows, causal, lb, lk in scored]
            weights = []
            for k0, rows, causal, lb, lk, after in summed:
                c = cacc[rows, :]
                a = jnp.exp(lb + after + _lanes(c, tk))
                if causal is not None:
                    a = jnp.where(causal, a, 0.0)
                cacc[rows, :] = c + _row_total(after, lk, 0)
                weights.append((k0, rows, a.astype(BF16)))
            for k0, rows, a in weights:
                oacc[rows, :] += _dot(a, v_ref[pl.ds(k0, tk), :].astype(BF16), NN)

        for d in reversed(range(0, tq // tk, 2)):
            run([(pl.multiple_of(i * tq + e * tk, tk), e * tk, True) for e in (d + 1, d)])
        per_trip = tq // tk

        def step(it, carry):
            k0 = pl.multiple_of((i - 1 - it) * tq, tq)
            run([(pl.multiple_of(k0 + e * tk, tk), 0, False) for e in reversed(range(per_trip))])
            return carry

        lax.fori_loop(0, i, step, 0)
        o = oacc[...]
        o_ref[...] = o
        ct_ref[...] = cacc[...]
        mx_ref[...] = _head_out(o, g_ref[...]).astype(BF16)

    blk = pl.BlockSpec((tq, HEAD_DIM), lambda h, i: (i, h))
    body, in_specs, args = _following(
        after, body,
        [blk, pl.BlockSpec((S, HEAD_DIM), lambda h, i: (0, H + h)),
         pl.BlockSpec((S, HEAD_DIM), lambda h, i: (0, 2 * H + h)), pl.BlockSpec((1, HEAD_DIM), lambda h, i: (0, h))],
        [proj, proj, proj, gain])
    return _pcall(
        body, grid=(H, S // tq), in_specs=in_specs,
        out_specs=[blk, blk, blk],
        out_shape=[jax.ShapeDtypeStruct((S, H * HEAD_DIM), F32), jax.ShapeDtypeStruct((S, H * HEAD_DIM), F32),
                   jax.ShapeDtypeStruct((S, mixed_heads * HEAD_DIM), BF16)],
        scratch_shapes=[pltpu.VMEM((tq, HEAD_DIM), F32), pltpu.VMEM((tq, LANES), F32)],
        compiler_params=_params("parallel", "arbitrary"), name=name)(*args)


def _sb_bwd(proj, gain, o_raw, ctot, dmixed, dm_col0, n_heads, name, tq=1024, after=None):
    S = proj.shape[0]
    H, tk = n_heads, SB_KEY_BLOCK
    tq = _tile(S, tq, 2 * tk)
    nq = S // tq
    scale = HEAD_DIM ** -0.5

    def body(q_ref, k_ref, v_ref, g_ref, o_ref, ct_ref, dm_ref, dproj_ref, dg_ref,
             dkacc, dvacc, dqacc, pfx, gcar, dos, stage_q, stage_k, stage_v, out_sems):
        h, i = pl.program_id(0), pl.program_id(1)

        @pl.when(i == 0)
        def _():
            dkacc[...] = jnp.zeros_like(dkacc)
            dvacc[...] = jnp.zeros_like(dvacc)
            dg_ref[...] = jnp.zeros_like(dg_ref)

        o, dm, g = o_ref[...], dm_ref[...].astype(F32), g_ref[...]
        r = _rms_scale(o)
        dos[...] = _rms_bwd(dm * g, o, r).astype(BF16)
        dg_ref[...] += jnp.broadcast_to(jnp.sum(dm * o * r, axis=0, keepdims=True), dg_ref.shape)
        dqacc[...] = jnp.zeros_like(dqacc)
        pfx[...] = jnp.zeros_like(pfx)
        gcar[...] = jnp.zeros_like(gcar)
        later, earlier = _sum_matrix(">"), _sum_matrix("<")

        def run(blocks):
            scored = []
            for k0, r0, diagonal in blocks:
                rows, keys = pl.ds(r0, tq - r0), pl.ds(k0, tk)
                lb, lk = _logsig_pair(_dot(q_ref[rows, :].astype(BF16), k_ref[keys, :].astype(BF16), NT) * scale)
                da = _dot(dos[rows, :], v_ref[keys, :].astype(BF16), NT)
                causal = None
                if diagonal:
                    causal = (lax.broadcasted_iota(jnp.int32, (tq - r0, tk), 1)
                              < lax.broadcasted_iota(jnp.int32, (tq - r0, tk), 0))
                    lk = jnp.where(causal, lk, 0.0)
                scored.append((rows, keys, causal, lb, lk, da))
            summed = [(*blk, _split_dot(blk[4], later)) for blk in scored]
            weighted = []
            for rows, keys, causal, lb, lk, da, after in summed:
                p = pfx[rows, :] + _row_total(after, lk, 0)
                pfx[rows, :] = p
                a = jnp.exp(lb + after + _lanes(ct_ref[rows, :] - p, tk))
                if causal is not None:
                    a = jnp.where(causal, a, 0.0)
                dl = da * a
                weighted.append((rows, keys, causal, lb, a.astype(BF16), dl, _dot(dl.astype(BF16), earlier, NN)))
            cotangents = []
            for rows, keys, causal, lb, a, dl, before in weighted:
                gc = gcar[rows, :]
                gcar[rows, :] = gc + _row_total(before, dl, tk - 1)
                sig = jnp.exp(lb)
                gsum = (before + _lanes(gc, tk)) * sig
                if causal is not None:
                    gsum = jnp.where(causal, gsum, 0.0)
                cotangents.append((rows, keys, a, ((dl * (1.0 - sig) - gsum) * scale).astype(BF16)))
            for rows, keys, a, dz in cotangents:
                q, do = q_ref[rows, :].astype(BF16), dos[rows, :]
                dvacc[keys, :] += _dot(a, do, TN)
                dqacc[rows, :] += _dot(dz, k_ref[keys, :].astype(BF16), NN)
                dkacc[keys, :] += _dot(dz, q, TN)

        per_trip = tq // tk

        def step(j, carry):
            k0 = pl.multiple_of(j * tq, tq)
            run([(pl.multiple_of(k0 + e * tk, tk), 0, False) for e in range(per_trip)])
            return carry

        lax.fori_loop(0, i, step, 0)
        for d in range(0, tq // tk, 2):
            run([(pl.multiple_of(i * tq + e * tk, tk), e * tk, True) for e in (d, d + 1)])
        def columns(block):
            return pl.ds(pl.multiple_of(block * HEAD_DIM, HEAD_DIM), HEAD_DIM)

        dq_out = pltpu.make_async_copy(stage_q, dproj_ref.at[pl.ds(pl.multiple_of(i * tq, tq), tq), columns(h)], out_sems.at[0])
        dkv_out = [pltpu.make_async_copy(stage_k, dproj_ref.at[:, columns(H + h)], out_sems.at[1]),
                   pltpu.make_async_copy(stage_v, dproj_ref.at[:, columns(2 * H + h)], out_sems.at[2])]

        @pl.when((h > 0) | (i > 0))
        def _():
            dq_out.wait()

        stage_q[...] = dqacc[...].astype(BF16)
        dq_out.start()

        @pl.when(i == nq - 1)
        def _():
            @pl.when(h > 0)
            def _():
                for cp in dkv_out:
                    cp.wait()

            stage_k[...] = dkacc[...].astype(BF16)
            stage_v[...] = dvacc[...].astype(BF16)
            for cp in dkv_out:
                cp.start()

        @pl.when((h == H - 1) & (i == nq - 1))
        def _():
            dq_out.wait()
            for cp in dkv_out:
                cp.wait()

    blk = pl.BlockSpec((tq, HEAD_DIM), lambda h, i: (i, h))
    W = H * HEAD_DIM
    body, in_specs, args = _following(
        after, body,
        [blk, pl.BlockSpec((S, HEAD_DIM), lambda h, i: (0, H + h)),
         pl.BlockSpec((S, HEAD_DIM), lambda h, i: (0, 2 * H + h)), pl.BlockSpec((1, HEAD_DIM), lambda h, i: (0, h)),
         blk, blk, pl.BlockSpec((tq, HEAD_DIM), lambda h, i: (i, dm_col0 + h))],
        [proj, proj, proj, gain, o_raw, ctot, dmixed])
    return _pcall(
        body, grid=(H, nq), in_specs=in_specs,
        out_specs=[HBM, pl.BlockSpec((8, HEAD_DIM), lambda h, i: (0, h))],
        out_shape=[jax.ShapeDtypeStruct(proj.shape, BF16), jax.ShapeDtypeStruct((8, W), F32)],
        scratch_shapes=[pltpu.VMEM((S, HEAD_DIM), F32), pltpu.VMEM((S, HEAD_DIM), F32), pltpu.VMEM((tq, HEAD_DIM), F32),
                        pltpu.VMEM((tq, LANES), F32), pltpu.VMEM((tq, LANES), F32), pltpu.VMEM((tq, HEAD_DIM), BF16),
                        pltpu.VMEM((tq, HEAD_DIM), BF16), pltpu.VMEM((S, HEAD_DIM), BF16), pltpu.VMEM((S, HEAD_DIM), BF16),
                        pltpu.SemaphoreType.DMA((3,))],
        compiler_params=_params("arbitrary", "arbitrary"), name=name)(*args)


def _rope_tables(S):
    inv_freq = ROPE_THETA ** (-jnp.arange(0, HEAD_DIM, 2, dtype=F32) / HEAD_DIM)
    ang = jnp.arange(S, dtype=F32)[:, None] * inv_freq[None, :]
    cos, sin = jnp.cos(ang), jnp.sin(ang)
    return jnp.concatenate([cos, cos], axis=1), jnp.concatenate([-sin, sin], axis=1)


def _rope(v, cos2, sin_signed):
    return v * cos2 + pltpu.roll(v, HEAD_DIM // 2, axis=1) * sin_signed


def _dil_rows(d, r, l0, n):
    if d == 1:
        return pl.ds(l0 if isinstance(l0, int) else pl.multiple_of(l0, KEY_BLOCK), n)
    return pl.ds(r + d * l0, n, stride=d)


def _dil_blocks(S, visit):
    B = KEY_BLOCK
    group = 16
    for b, d in enumerate(DILATIONS):
        nb = S // d // B
        if nb == 1:
            g = math.gcd(d, group)

            def trip(t, carry, b=b, d=d, g=g):
                visit([(b, d, t * g + u, 0, True) for u in range(g)])
                return carry

            lax.fori_loop(0, d // g, trip, 0)
        elif d == 1:
            visit([(b, d, 0, 0, True)])
            g = max(k for k in range(1, group + 2) if (nb - 1) % k == 0)

            def trip(t, carry, b=b, d=d, g=g):
                visit([(b, d, 0, (1 + t * g + u) * B, False) for u in range(g)])
                return carry

            lax.fori_loop(0, (nb - 1) // g, trip, 0)
        else:
            g = math.gcd(d, max(group // nb, 1))

            def trip(t, carry, b=b, d=d, nb=nb, g=g):
                visit([(b, d, t * g + u, n * B, n == 0) for u in range(g) for n in range(nb)])
                return carry

            lax.fori_loop(0, d // g, trip, 0)


def _dil_mask(first):
    B = KEY_BLOCK
    nk = B if first else 2 * B
    iq = lax.broadcasted_iota(jnp.int32, (B, nk), 0)
    ik = lax.broadcasted_iota(jnp.int32, (B, nk), 1)
    return (ik <= iq) if first else ((ik >= iq) & (ik <= iq + B))


def _dil_fwd(proj, cos2, sin_signed, gain, mixed, col0, n_heads, name, after=None):
    S = proj.shape[0]
    H, B = n_heads, KEY_BLOCK
    scale = HEAD_DIM ** -0.5
    rc = _tile(S, 256, 8)

    def body(q_ref, k_ref, v_ref, c_ref, s_ref, g_ref, mixed_in, o_ref, l_ref, mx_ref, qr, kr, vf, *per_branch):
        ob, lb = per_branch[:len(DILATIONS)], per_branch[len(DILATIONS):]

        def rope_rows(t, carry):
            rows = pl.ds(pl.multiple_of(t * rc, rc), rc)
            qr[rows, :] = _rope(q_ref[rows, :].astype(F32), c_ref[rows, :], s_ref[rows, :])
            kr[rows, :] = _rope(k_ref[rows, :].astype(F32), c_ref[rows, :], s_ref[rows, :])
            vf[rows, :] = v_ref[rows, :].astype(F32)
            return carry

        lax.fori_loop(0, S // rc, rope_rows, 0)

        def visit(blocks):
            scores = []
            for b, d, r, l0, first in blocks:
                qrows = _dil_rows(d, r, l0, B)
                krows = qrows if first else _dil_rows(d, r, l0 - B, 2 * B)
                s = _dot(qr[qrows, :].astype(BF16), kr[krows, :].astype(BF16), NT) * scale
                scores.append((b, qrows, krows, jnp.where(_dil_mask(first), s, NEG)))
            weights = []
            for b, qrows, krows, s in scores:
                m = jnp.max(s, axis=1, keepdims=True)
                p = jnp.exp(s - m)
                den = jnp.sum(p, axis=1, keepdims=True)
                lb[b][qrows, :] = jnp.broadcast_to(m + jnp.log(den), (B, LANES))
                weights.append((b, qrows, krows, p.astype(BF16), den))
            for b, qrows, krows, p, den in weights:
                ob[b][qrows, :] = _dot(p, vf[krows, :].astype(BF16), NN) / den

        _dil_blocks(S, visit)

        def combine(t, carry):
            rows = pl.ds(pl.multiple_of(t * rc, rc), rc)
            l0, l1, l2 = lb[0][rows, :], lb[1][rows, :], lb[2][rows, :]
            m = jnp.maximum(jnp.maximum(l0, l1), l2)
            w0, w1, w2 = jnp.exp(l0 - m), jnp.exp(l1 - m), jnp.exp(l2 - m)
            den = w0 + w1 + w2
            o = (w0 * ob[0][rows, :] + w1 * ob[1][rows, :] + w2 * ob[2][rows, :]) / den
            o_ref[rows, :] = o
            l_ref[rows, :] = m + jnp.log(den)
            mx_ref[rows, :] = _head_out(o, g_ref[...]).astype(BF16)
            return carry

        lax.fori_loop(0, S // rc, combine, 0)

    def col(k):
        return pl.BlockSpec((S, HEAD_DIM), lambda h: (0, col0 + k * H + h))

    tab = pl.BlockSpec((S, HEAD_DIM), lambda h: (0, 0))
    out = pl.BlockSpec((S, HEAD_DIM), lambda h: (0, h))
    W = H * HEAD_DIM
    first = mixed.shape[1] // HEAD_DIM - H
    body, in_specs, args = _following(
        after, body, [col(0), col(1), col(2), tab, tab, pl.BlockSpec((1, HEAD_DIM), lambda h: (0, h)), HBM],
        [proj, proj, proj, cos2, sin_signed, gain, mixed])
    return _pcall(
        body, grid=(H,), in_specs=in_specs,
        out_specs=[out, out, pl.BlockSpec((S, HEAD_DIM), lambda h: (0, first + h))],
        out_shape=[jax.ShapeDtypeStruct((S, W), F32), jax.ShapeDtypeStruct((S, W), F32),
                   jax.ShapeDtypeStruct(mixed.shape, BF16)],
        input_output_aliases={6: 2},
        scratch_shapes=[pltpu.VMEM((S, HEAD_DIM), F32)] * (3 + 2 * len(DILATIONS)),
        compiler_params=_params("parallel"), name=name)(*args)


def _dil_bwd(proj, cos2, sin_signed, gain, o_raw, lse, dmixed, dproj, dm_col0, col0, n_heads, name, after=None):
    S = proj.shape[0]
    H, B = n_heads, KEY_BLOCK
    scale = HEAD_DIM ** -0.5
    rc = _tile(S, 256, 8)

    def body(q_ref, k_ref, v_ref, c_ref, s_ref, g_ref, o_ref, l_ref, dm_ref, dproj_in, dproj_ref, dg_ref,
             qr, kr, vf, dos, dsum, dqr, dkr, dvv, stage_q, stage_k, stage_v, out_sems):
        dg_ref[...] = jnp.zeros_like(dg_ref)

        def prep(t, carry):
            rows = pl.ds(pl.multiple_of(t * rc, rc), rc)
            qr[rows, :] = _rope(q_ref[rows, :].astype(F32), c_ref[rows, :], s_ref[rows, :])
            kr[rows, :] = _rope(k_ref[rows, :].astype(F32), c_ref[rows, :], s_ref[rows, :])
            vf[rows, :] = v_ref[rows, :].astype(F32)
            o, dm = o_ref[rows, :], dm_ref[rows, :].astype(F32)
            r = _rms_scale(o)
            do = _rms_bwd(dm * g_ref[...], o, r)
            dg_ref[...] += jnp.broadcast_to(jnp.sum(dm * o * r, axis=0, keepdims=True), dg_ref.shape)
            dos[rows, :] = do
            dsum[rows, :] = jnp.broadcast_to(jnp.sum(do * o, axis=1, keepdims=True), (rc, LANES))
            dqr[rows, :] = jnp.zeros((rc, HEAD_DIM), F32)
            dkr[rows, :] = jnp.zeros((rc, HEAD_DIM), F32)
            dvv[rows, :] = jnp.zeros((rc, HEAD_DIM), F32)
            return carry

        lax.fori_loop(0, S // rc, prep, 0)

        def visit(blocks):
            products = []
            for b, d, r, l0, first in blocks:
                qrows = _dil_rows(d, r, l0, B)
                krows = qrows if first else _dil_rows(d, r, l0 - B, 2 * B)
                qs, ks = qr[qrows, :].astype(BF16), kr[krows, :].astype(BF16)
                do = dos[qrows, :].astype(BF16)
                s = jnp.where(_dil_mask(first), _dot(qs, ks, NT) * scale, NEG)
                dp = _dot(do, vf[krows, :].astype(BF16), NT)
                products.append((qrows, krows, qs, ks, do, s, dp))
            cotangents = []
            for qrows, krows, qs, ks, do, s, dp in products:
                p = jnp.exp(s - l_ref[qrows, :][:, 0:1])
                ds = (p * (dp - dsum[qrows, :][:, 0:1]) * scale).astype(BF16)
                cotangents.append((qrows, krows, qs, ks, do, p.astype(BF16), ds))
            for qrows, krows, qs, ks, do, p, ds in cotangents:
                dqr[qrows, :] += _dot(ds, ks, NN)
                dkr[krows, :] += _dot(ds, qs, TN)
                dvv[krows, :] += _dot(p, do, TN)

        _dil_blocks(S, visit)

        def finish(t, carry):
            rows = pl.ds(pl.multiple_of(t * rc, rc), rc)
            c, s = c_ref[rows, :], s_ref[rows, :]
            dq, dk = dqr[rows, :], dkr[rows, :]
            stage_q[rows, :] = (dq * c + pltpu.roll(dq * s, HEAD_DIM // 2, axis=1)).astype(BF16)
            stage_k[rows, :] = (dk * c + pltpu.roll(dk * s, HEAD_DIM // 2, axis=1)).astype(BF16)
            stage_v[rows, :] = dvv[rows, :].astype(BF16)
            return carry

        h = pl.program_id(0)
        outs = [pltpu.make_async_copy(
            stage, dproj_ref.at[:, pl.ds(pl.multiple_of((col0 + k * H + h) * HEAD_DIM, HEAD_DIM), HEAD_DIM)], out_sems.at[k])
            for k, stage in enumerate((stage_q, stage_k, stage_v))]

        @pl.when(h > 0)
        def _():
            for cp in outs:
                cp.wait()

        lax.fori_loop(0, S // rc, finish, 0)
        for cp in outs:
            cp.start()

        @pl.when(h == H - 1)
        def _():
            for cp in outs:
                cp.wait()

    def col(k):
        return pl.BlockSpec((S, HEAD_DIM), lambda h: (0, col0 + k * H + h))

    tab = pl.BlockSpec((S, HEAD_DIM), lambda h: (0, 0))
    out = pl.BlockSpec((S, HEAD_DIM), lambda h: (0, h))
    W = H * HEAD_DIM
    big, half = pltpu.VMEM((S, HEAD_DIM), F32), pltpu.VMEM((S, HEAD_DIM), BF16)
    body, in_specs, args = _following(
        after, body,
        [col(0), col(1), col(2), tab, tab, pl.BlockSpec((1, HEAD_DIM), lambda h: (0, h)), out, out,
         pl.BlockSpec((S, HEAD_DIM), lambda h: (0, dm_col0 + h)), HBM],
        [proj, proj, proj, cos2, sin_signed, gain, o_raw, lse, dmixed, dproj])
    return _pcall(
        body, grid=(H,), in_specs=in_specs,
        out_specs=[HBM, pl.BlockSpec((8, HEAD_DIM), lambda h: (0, h))],
        out_shape=[jax.ShapeDtypeStruct(dproj.shape, BF16), jax.ShapeDtypeStruct((8, W), F32)],
        input_output_aliases={9: 0},
        scratch_shapes=[big, big, big, big, pltpu.VMEM((S, LANES), F32), big, big, big, half, half, half,
                        pltpu.SemaphoreType.DMA((3,))],
        compiler_params=_params("arbitrary"), name=name)(*args)


GELU_C = math.sqrt(2.0 / math.pi)
GELU_A = 0.044715
HALO = 16


def _shifts_down(cur, halo):
    row = lax.broadcasted_iota(jnp.int32, cur.shape, 0)
    first, second = row == 0, row == 1
    last, before_last = halo[HALO - 1:HALO, :], halo[HALO - 2:HALO - 1, :]
    two = jnp.where(first, before_last, jnp.where(second, last, pltpu.roll(cur, 2, axis=0)))
    return two, jnp.where(first, last, pltpu.roll(cur, 1, axis=0))


def _shift_up(cur, halo, k):
    n = cur.shape[0]
    out = pltpu.roll(cur, n - k, axis=0)
    row = lax.broadcasted_iota(jnp.int32, cur.shape, 0)
    for t in range(k):
        out = jnp.where(row == n - k + t, halo[t:t + 1, :], out)
    return out


def _conv3(cur, halo, cw):
    rows = (*_shifts_down(cur, halo), cur)
    return rows[0] * cw[0:1, :] + rows[1] * cw[1:2, :] + cur * cw[2:3, :] + cw[3:4, :], rows


def _gelu_parts(x):
    xx = x * x
    t = jnp.tanh(x * (GELU_C + (GELU_C * GELU_A) * xx))
    half = 0.5 * x
    return half + half * t, t, xx, half


def _gelu_slope(t, xx, half):
    return (0.5 + 0.5 * t) + half * (1.0 - t * t) * (GELU_C + (3.0 * GELU_C * GELU_A) * xx)


def _geglu_specs(tm, tn, ncb):
    hb = tm // HALO

    def cur(off):
        return pl.BlockSpec((tm, tn), lambda j, i: (i, off + j))

    def prev(off):
        return pl.BlockSpec((HALO, tn), lambda j, i: (jnp.maximum(i * hb - 1, 0), off + j))

    def taps(off):
        return pl.BlockSpec((8, tn), lambda j, i: (0, off + j))

    return [cur(0), prev(0), cur(ncb), prev(ncb), taps(0), taps(ncb)]


def _geglu_fwd(u, cwb, name, tm=512, tn=1408, after=None):
    S, F2 = u.shape
    F = F2 // 2
    tm, tn = _tile(S, tm, HALO), _tile(F, tn)
    ncb = F // tn

    def body(g_ref, gp_ref, v_ref, vp_ref, cg_ref, cv_ref, y_ref):
        top = pl.program_id(1) > 0
        gp = jnp.where(top, gp_ref[...].astype(F32), 0.0)
        vp = jnp.where(top, vp_ref[...].astype(F32), 0.0)
        gc = _conv3(g_ref[...].astype(F32), gp, cg_ref[...])[0]
        vc = _conv3(v_ref[...].astype(F32), vp, cv_ref[...])[0]
        y_ref[...] = (_gelu_parts(gc)[0] * vc).astype(BF16)

    body, in_specs, args = _following(after, body, _geglu_specs(tm, tn, ncb), [u, u, u, u, cwb, cwb])
    return _pcall(body, grid=(ncb, S // tm), in_specs=in_specs,
                  out_specs=pl.BlockSpec((tm, tn), lambda j, i: (i, j)),
                  out_shape=jax.ShapeDtypeStruct((S, F), BF16),
                  compiler_params=_params("parallel", "parallel"), name=name)(*args)


def _geglu_bwd(u, dy, cwb, name, tm=256, tn=1408, after=None):
    S, F2 = u.shape
    F = F2 // 2
    tm, tn = _tile(S, tm, HALO), _tile(F, tn)
    ncb = F // tn

    def body(g_ref, gp_ref, v_ref, vp_ref, cg_ref, cv_ref, dy_ref, dc_ref, dwg_ref, dwv_ref):
        i = pl.program_id(1)

        @pl.when(i == 0)
        def _():
            dwg_ref[...] = jnp.zeros_like(dwg_ref)
            dwv_ref[...] = jnp.zeros_like(dwv_ref)

        top = i > 0
        g, v = g_ref[...].astype(F32), v_ref[...].astype(F32)
        gp = jnp.where(top, gp_ref[...].astype(F32), 0.0)
        vp = jnp.where(top, vp_ref[...].astype(F32), 0.0)
        gc, g_rows = _conv3(g, gp, cg_ref[...])
        vc, v_rows = _conv3(v, vp, cv_ref[...])
        act, t, xx, half = _gelu_parts(gc)
        dact = _gelu_slope(t, xx, half)
        dyv = dy_ref[...].astype(F32)
        dgc = dyv * vc * dact
        dvc = dyv * act
        dc_ref[0] = dgc.astype(BF16)
        dc_ref[1] = dvc.astype(BF16)

        def taps(out_ref, dc, rows):
            for k, moved in enumerate(rows):
                out_ref[k:k + 1, :] += jnp.sum(dc * moved, axis=0, keepdims=True)
            out_ref[3:4, :] += jnp.sum(dc, axis=0, keepdims=True)

        taps(dwg_ref, dgc, g_rows)
        taps(dwv_ref, dvc, v_rows)

    body, in_specs, args = _following(
        after, body, _geglu_specs(tm, tn, ncb) + [pl.BlockSpec((tm, tn), lambda j, i: (i, j))], [u, u, u, u, cwb, cwb, dy])
    return _pcall(body, grid=(ncb, S // tm), in_specs=in_specs,
                  out_specs=[pl.BlockSpec((2, tm, tn), lambda j, i: (0, i, j)),
                             pl.BlockSpec((8, tn), lambda j, i: (0, j)), pl.BlockSpec((8, tn), lambda j, i: (0, j))],
                  out_shape=[jax.ShapeDtypeStruct((2, S, F), BF16), jax.ShapeDtypeStruct((8, F), F32),
                             jax.ShapeDtypeStruct((8, F), F32)],
                  compiler_params=_params("parallel", "arbitrary"), name=name)(*args)


def _conv_bwd(dc, cwb, name, tm=512, tn=1408, after=None):
    _, S, F = dc.shape
    tm, tn = _tile(S, tm, HALO), _tile(F, tn)
    ncb, nrb = F // tn, S // tm
    hb = tm // HALO

    def body(c_ref, n_ref, w_ref, du_ref):
        cur = c_ref[...].astype(F32)
        nxt = jnp.where(pl.program_id(2) < nrb - 1, n_ref[...].astype(F32), 0.0)
        w = w_ref[...]
        du = cur * w[2:3, :] + _shift_up(cur, nxt, 1) * w[1:2, :] + _shift_up(cur, nxt, 2) * w[0:1, :]
        du_ref[...] = du.astype(BF16)

    body, in_specs, args = _following(
        after, body,
        [pl.BlockSpec((None, tm, tn), lambda c, j, i: (c, i, j)),
         pl.BlockSpec((None, HALO, tn), lambda c, j, i: (c, jnp.minimum((i + 1) * hb, S // HALO - 1), j)),
         pl.BlockSpec((8, tn), lambda c, j, i: (0, c * ncb + j))], [dc, dc, cwb])
    return _pcall(body, grid=(2, ncb, nrb), in_specs=in_specs,
                  out_specs=pl.BlockSpec((tm, tn), lambda c, j, i: (i, c * ncb + j)),
                  out_shape=jax.ShapeDtypeStruct((S, 2 * F), BF16),
                  compiler_params=_params("parallel", "parallel", "parallel"), name=name)(*args)


def _adam_math(w, g, m, v):
    m = ADAM_B1 * m + (1.0 - ADAM_B1) * g
    v = ADAM_B2 * v + (1.0 - ADAM_B2) * (g * g)
    m_hat = m / (1.0 - ADAM_B1 ** ADAM_STEP)
    v_hat = v / (1.0 - ADAM_B2 ** ADAM_STEP)
    return -ADAM_LR * (m_hat / (jnp.sqrt(v_hat) + ADAM_EPS) + ADAM_WD * w), m, v


def _adamw(w, parts, m, v, name, tr=256):
    R, C = w.shape
    n, _, Cp = parts.shape
    tr = _tile(R, tr, 8)

    def body(w_ref, p_ref, m_ref, v_ref, g_out, d_out, m_out, v_out):
        g = p_ref[0, :, 0:C].astype(F32)
        for k in range(1, n):
            g = g + p_ref[k, :, 0:C].astype(F32)
        d, mn, vn = _adam_math(w_ref[...], g, m_ref[...], v_ref[...])
        g_out[...] = g
        d_out[...] = d
        m_out[...] = mn
        v_out[...] = vn

    spec = pl.BlockSpec((tr, C), lambda i: (i, 0))
    shape = jax.ShapeDtypeStruct((R, C), F32)
    return _pcall(body, grid=(R // tr,), in_specs=[spec, pl.BlockSpec((n, tr, Cp), lambda i: (0, i, 0)), spec, spec],
                  out_specs=[spec] * 4, out_shape=[shape] * 4, compiler_params=_params("parallel"), name=name)(w, parts, m, v)


def _adamw_chips(w, pair, parts, chip_ids, m, v, name, tr=256):
    R, C = w.shape
    Cp = pair.shape[2]
    by_columns = C == Cp and _tile(R, tr, 16) < 64
    tr, tc = (R, _tile(C, 256)) if by_columns else (_tile(R, tr, 16), C)

    def body(ids_ref, w_ref, own_ref, p1_ref, p2_ref, p3_ref, m_ref, v_ref, g_out, d_out, m_out, v_out):
        g = own_ref[:, 0:tc].astype(F32)
        for ref in (p1_ref, p2_ref, p3_ref):
            g = g + ref[:, 0:tc].astype(F32)
        d, mn, vn = _adam_math(w_ref[...], g, m_ref[...], v_ref[...])
        g_out[...] = g
        d_out[...] = d
        m_out[...] = mn
        v_out[...] = vn

    if by_columns:
        spec = pl.BlockSpec((tr, tc), lambda j, ids: (0, j))
    else:
        spec = pl.BlockSpec((tr, tc), lambda i, ids: (i, 0))

    def chip(k):
        if by_columns:
            return pl.BlockSpec((None, tr, tc), lambda j, ids: (ids[k], 0, j))
        return pl.BlockSpec((None, tr, Cp), lambda i, ids: (ids[k], i, 0))

    shape = jax.ShapeDtypeStruct((R, C), F32)
    grid_spec = pltpu.PrefetchScalarGridSpec(
        num_scalar_prefetch=1, grid=(C // tc if by_columns else R // tr,),
        in_specs=[spec, chip(0), chip(1), chip(2), chip(3), spec, spec], out_specs=[spec] * 4)
    return _pcall(body, grid_spec=grid_spec, out_shape=[shape] * 4, compiler_params=_params("parallel"),
                  name=name)(chip_ids, w, pair, parts, parts, parts, m, v)


def _place():
    return lax.axis_index("x"), lax.axis_index("y"), lax.axis_index("c")


def _other_chips(x, y):
    return [(1 - x, y), (x, 1 - y), (1 - x, 1 - y)]


IN_HBM = pl.BlockSpec(memory_space=pltpu.HBM)
SEM = pl.BlockSpec(memory_space=pltpu.SEMAPHORE)
EFFECT = pltpu.SideEffectType.DATAFLOW_SIDE_EFFECTING
TOKEN = jax.ShapeDtypeStruct((8, LANES), F32)
TOKEN_SPEC = pl.BlockSpec(memory_space=pltpu.VMEM)


def _in_hbm(a):
    return pltpu.with_memory_space_constraint(a, pltpu.HBM)


def _landing(shape):
    return _in_hbm(lax.empty(shape.shape, shape.dtype))


def _hbm_like(a):
    return pltpu.HBM(a.shape, a.dtype)


def _gather_places():
    x, y, c = _place()
    relay_from = (c * (1 - x) + (1 - c) * x, c * y + (1 - c) * (1 - y), c)
    relay_to = (c * x + (1 - c) * (1 - x), c * (1 - y) + (1 - c) * y, c)
    return (x, y, c), (x, y, 1 - c), (1 - x, y, c), (x, 1 - y, c), (1 - x, 1 - y, c), relay_from, relay_to


def _slot_copy(slot, ref, src, dst, send_sem, recv_sem, to):
    return pltpu.make_async_remote_copy(src_ref=slot(ref, *src), dst_ref=slot(ref, *dst), send_sem=send_sem,
                                        recv_sem=recv_sem, device_id=to, device_id_type=MESH)


def _split_call(body, arrays, sems_in, sems_out, after, name, token=True):
    na, ni, no = len(arrays), len(sems_in), len(sems_out)

    def wrapped(*refs):
        body(refs[:na], refs[na:na + ni], refs[na + ni + 1:na + ni + 1 + no])
        if token:
            refs[-1][...] = jnp.zeros_like(refs[-1])

    outs = _pcall(
        wrapped, in_specs=[IN_HBM] * na + [SEM] * ni + [HBM],
        out_specs=[SEM] * no + [IN_HBM] * na + ([TOKEN_SPEC] if token else []),
        out_shape=[pltpu.SemaphoreType.DMA((n,)) for n in sems_out] + [_hbm_like(s) for s in arrays] + ([TOKEN] if token else []),
        input_output_aliases={a: no + a for a in range(na)},
        compiler_params=pltpu.CompilerParams(has_side_effects=EFFECT), name=name,
    )(*[_in_hbm(s) for s in arrays], *sems_in, after)
    return list(outs[:no]), list(outs[no:no + na]), (outs[-1] if token else None)


def _gather_start(landing, slots, after, name):
    na = len(landing)

    def body(land, _, sems):
        me, sib, xn, yn, _, _, _ = _gather_places()
        for a in range(na):
            for k, to in enumerate((sib, xn, yn)):
                _slot_copy(slots[a], land[a], me, me, sems[0].at[3 * a + k], sems[1].at[3 * a + k], to).start()

    return _split_call(body, landing, [], [3 * na, 3 * na], after, name)


def _gather_relay(gathered, sems1, slots, after, name):
    na = len(gathered)

    def body(gath, taken, given):
        me, sib, xn, yn, _, relay_from, relay_to = _gather_places()
        for a in range(na):
            for k, peer in enumerate((sib, xn, yn)):
                arrival = _slot_copy(slots[a], gath[a], me, peer, taken[0].at[3 * a + k], taken[1].at[3 * a + k], peer)
                arrival.wait_send()
                arrival.wait_recv()
        for a in range(na):
            _slot_copy(slots[a], gath[a], relay_from, relay_from, given[0].at[a], given[1].at[a], relay_to).start()
            for k, peer in enumerate((xn, yn)):
                _slot_copy(slots[a], gath[a], peer, peer, given[2].at[2 * a + k], given[3].at[2 * a + k], sib).start()

    return _split_call(body, gathered, sems1, [na, na, 2 * na, 2 * na], after, name)


def _gather_pass(gathered, relay_sems, slots, after, name):
    na = len(gathered)

    def body(gath, taken, given):
        me, sib, xn, yn, diag, relay_from, relay_to = _gather_places()
        for a in range(na):
            _slot_copy(slots[a], gath[a], relay_from, relay_from, taken[0].at[a], taken[1].at[a], relay_to).wait_send()
            _slot_copy(slots[a], gath[a], me, diag, taken[0].at[a], taken[1].at[a], relay_to).wait_recv()
        for a in range(na):
            _slot_copy(slots[a], gath[a], diag, diag, given[0].at[a], given[1].at[a], sib).start()

    return _split_call(body, gathered, relay_sems, [na, na], after, name)


def _gather_finish(gathered, pass_sems, diag_sems, slots, after, name):
    na = len(gathered)

    def body(gath, taken, _):
        (x, y, c), sib, xn, yn, diag, _, _ = _gather_places()
        for a in range(na):
            for k, peer in enumerate((xn, yn)):
                passed = _slot_copy(slots[a], gath[a], peer, (peer[0], peer[1], 1 - c), taken[0].at[2 * a + k],
                                    taken[1].at[2 * a + k], sib)
                passed.wait_send()
                passed.wait_recv()
            passed = _slot_copy(slots[a], gath[a], diag, (diag[0], diag[1], 1 - c), taken[2].at[a], taken[3].at[a], sib)
            passed.wait_send()
            passed.wait_recv()

    return _split_call(body, gathered, list(pass_sems) + list(diag_sems), [], after, name, token=False)[1]


def _pair_copy(view, src, land, send_sems, recv_sems, chip):
    x, y, c = _place()
    return pltpu.make_async_remote_copy(
        src_ref=view(src, chip, 1 - c), dst_ref=land.at[chip], send_sem=send_sems.at[chip], recv_sem=recv_sems.at[chip],
        device_id=(x, y, 1 - c), device_id_type=MESH)


def _pair_start(grad, view, block, after, name):
    def body(src, land, after_ref, send_sems, recv_sems, src_thru, land_thru, token):
        for chip in range(N_CHIP):
            _pair_copy(view, src, land, send_sems, recv_sems, chip).start()
        token[...] = jnp.zeros_like(token)

    sems = pltpu.SemaphoreType.DMA((N_CHIP,))
    land = jax.ShapeDtypeStruct((N_CHIP, *block), BF16)
    return _pcall(
        body, in_specs=[IN_HBM, IN_HBM, HBM], out_specs=[SEM, SEM, IN_HBM, IN_HBM, TOKEN_SPEC],
        out_shape=[sems, sems, _hbm_like(grad), _hbm_like(land), TOKEN], input_output_aliases={0: 2, 1: 3},
        compiler_params=pltpu.CompilerParams(has_side_effects=EFFECT), name=name,
    )(_in_hbm(grad), _landing(land), after)


def _pair_wait(grad, recv, send_sems, recv_sems, view, after, name):
    def body(src, land, send, recv_s, after_ref, src_thru, land_thru):
        for chip in range(N_CHIP):
            copy = _pair_copy(view, src, land, send, recv_s, chip)
            copy.wait_send()
            copy.wait_recv()

    return _pcall(
        body, in_specs=[IN_HBM, IN_HBM, SEM, SEM, HBM], out_specs=[IN_HBM, IN_HBM],
        out_shape=[_hbm_like(grad), _hbm_like(recv)], input_output_aliases={0: 0, 1: 1},
        compiler_params=pltpu.CompilerParams(has_side_effects=EFFECT), name=name,
    )(grad, recv, send_sems, recv_sems, after)


def _chip_start(pair, after, name):
    def body(src, land, after_ref, send_sems, recv_sems, src_thru, land_thru, token):
        x, y, c = _place()
        for j, (px, py) in enumerate(_other_chips(x, y)):
            pltpu.make_async_remote_copy(
                src_ref=src.at[2 * px + py], dst_ref=land.at[2 * x + y], send_sem=send_sems.at[j], recv_sem=recv_sems.at[j],
                device_id=(px, py, c), device_id_type=MESH).start()
        token[...] = jnp.zeros_like(token)

    sems = pltpu.SemaphoreType.DMA((3,))
    return _pcall(
        body, in_specs=[IN_HBM, IN_HBM, HBM], out_specs=[SEM, SEM, IN_HBM, IN_HBM, TOKEN_SPEC],
        out_shape=[sems, sems, _hbm_like(pair), _hbm_like(pair), TOKEN], input_output_aliases={0: 2, 1: 3},
        compiler_params=pltpu.CompilerParams(has_side_effects=EFFECT), name=name,
    )(_in_hbm(pair), _landing(pair), after)


def _chip_wait(pair, parts, send_sems, recv_sems, after, name):
    def body(src, land, send, recv, after_ref, src_thru, land_thru):
        x, y, c = _place()
        for j, (px, py) in enumerate(_other_chips(x, y)):
            copy = pltpu.make_async_remote_copy(
                src_ref=src.at[2 * px + py], dst_ref=land.at[2 * px + py], send_sem=send.at[j], recv_sem=recv.at[j],
                device_id=(px, py, c), device_id_type=MESH)
            copy.wait_send()
            copy.wait_recv()

    return _pcall(
        body, in_specs=[IN_HBM, IN_HBM, SEM, SEM, HBM], out_specs=[IN_HBM, IN_HBM],
        out_shape=[_hbm_like(pair), _hbm_like(parts)], input_output_aliases={0: 0, 1: 1},
        compiler_params=pltpu.CompilerParams(has_side_effects=EFFECT), name=name,
    )(pair, parts, send_sems, recv_sems, after)


def _pair_add(core, grad, recv, block, grad_spec, name):
    _, R, C = recv.shape
    tr = block

    def body(c_ref, g_ref, r_ref, o_ref):
        o_ref[...] = (g_ref[...].astype(F32) + r_ref[...].astype(F32)).astype(BF16)

    grid_spec = pltpu.PrefetchScalarGridSpec(
        num_scalar_prefetch=1, grid=(N_CHIP, R // tr),
        in_specs=[grad_spec, pl.BlockSpec((None, tr, C), lambda k, i, c: (k, i, 0))],
        out_specs=pl.BlockSpec((None, tr, C), lambda k, i, c: (k, i, 0)))
    return _pcall(body, grid_spec=grid_spec, out_shape=jax.ShapeDtypeStruct(recv.shape, BF16),
                  compiler_params=_params("parallel", "parallel"), name=name)(core, grad, recv)


def _small_copies(gath, send_sems, recv_sems):
    x, y, c = _place()
    peers = [(x, y, 1 - c)] + [(px, py, pc) for px, py in _other_chips(x, y) for pc in (c, 1 - c)]
    pairs = []
    for a, ref in enumerate(gath):
        mine = ref.at[4 * x + 2 * y + c]
        for k, (px, py, pc) in enumerate(peers):
            sems = dict(send_sem=send_sems.at[7 * a + k], recv_sem=recv_sems.at[7 * a + k], device_id=(px, py, pc),
                        device_id_type=MESH)
            pairs.append((pltpu.make_async_remote_copy(src_ref=mine, dst_ref=mine, **sems),
                          pltpu.make_async_remote_copy(src_ref=mine, dst_ref=ref.at[4 * px + 2 * py + pc], **sems)))
    return pairs


def _small_start(landing, after, name):
    na = len(landing)

    def body(*refs):
        for send, _ in _small_copies(refs[:na], refs[na + 1], refs[na + 2]):
            send.start()
        refs[-1][...] = jnp.zeros_like(refs[-1])

    sems = pltpu.SemaphoreType.DMA((7 * na,))
    outs = _pcall(
        body, in_specs=[IN_HBM] * na + [HBM], out_specs=[SEM, SEM] + [IN_HBM] * na + [TOKEN_SPEC],
        out_shape=[sems, sems] + [_hbm_like(s) for s in landing] + [TOKEN],
        input_output_aliases={a: 2 + a for a in range(na)},
        compiler_params=pltpu.CompilerParams(has_side_effects=EFFECT), name=name,
    )(*[_in_hbm(s) for s in landing], after)
    return outs[0], outs[1], outs[2:2 + na], outs[-1]


def _small_wait(gathered, send_sems, recv_sems, after, name):
    na = len(gathered)

    def body(*refs):
        for send, arrival in _small_copies(refs[:na], refs[na], refs[na + 1]):
            send.wait_send()
            arrival.wait_recv()

    return list(_pcall(
        body, in_specs=[IN_HBM] * na + [SEM, SEM, HBM], out_specs=[IN_HBM] * na,
        out_shape=[_hbm_like(g) for g in gathered], input_output_aliases={a: a for a in range(na)},
        compiler_params=pltpu.CompilerParams(has_side_effects=EFFECT), name=name,
    )(*gathered, send_sems, recv_sems, after))


def _small_finish(gathered, params, name):
    na, npar = len(gathered), len(params)

    def body(*refs):
        g_refs, wmv = refs[:na], refs[na:na + 3 * npar]
        o_sums, o_params = refs[na + 3 * npar:2 * na + 3 * npar], refs[2 * na + 3 * npar:]
        sums = []
        for a in range(na):
            acc = g_refs[a][0]
            for k in range(1, N_DEV):
                acc = acc + g_refs[a][k]
            o_sums[a][...] = acc
            sums.append(acc)
        for j, (a, row, _, _, _) in enumerate(params):
            g = sums[a][row:row + 1, :]
            d, mn, vn = _adam_math(wmv[3 * j][...], g, wmv[3 * j + 1][...], wmv[3 * j + 2][...])
            for out, val in zip(o_params[4 * j:4 * j + 4], (g, d, mn, vn)):
                out[...] = val

    vm = pl.BlockSpec(memory_space=pltpu.VMEM)
    flat = [t for p in params for t in p[2:]]
    out_shape = [jax.ShapeDtypeStruct(g.shape[1:], F32) for g in gathered]
    out_shape += [jax.ShapeDtypeStruct(p[2].shape, F32) for p in params for _ in range(4)]
    outs = _pcall(body, in_specs=[vm] * (na + 3 * npar), out_specs=[vm] * len(out_shape), out_shape=out_shape,
                  name=name)(*gathered, *flat)
    return outs[:na], [outs[na + 4 * j:na + 4 * j + 4] for j in range(npar)]


def _local_step(x, tgt, gains, weights):
    g_pre_mix, g_post_mix, g_pre_ffn, g_post_ffn, g_sb, g_dil = gains
    S, D = x.shape
    hs = g_sb.shape[1] // HEAD_DIM
    hd = g_dil.shape[1] // HEAD_DIM
    cos2, sin_signed = _rope_tables(S)

    h1 = _rms_fwd(x, g_pre_mix, "rms_in", after=[weights.start(), cos2, sin_signed])
    w_in_some, there, coming, token = weights.w_in_first(h1)
    proj = _mm_nn_some(h1, w_in_some, there, None, "proj_first", after=token)
    w_in_g = weights.w_in(proj)
    proj = _mm_nn_some(h1, w_in_g, coming, proj, "proj")
    o_sb, ct_sb, mixed = _sb_fwd(proj, g_sb, hs, hs + hd, "sb_fwd", after=weights.relay_out(proj))
    o_dl, lse_dl, mixed = _dil_fwd(proj, cos2, sin_signed, g_dil, mixed, 3 * hs, hd, "dil_fwd", after=weights.after_sb(o_sb))
    w_out_g = weights.w_out(o_dl)
    mix = _mm_nn(mixed, w_out_g, F32, "mix_out", tn=1024)
    x2, h2 = _mid_fwd(x, mix, g_post_mix, g_pre_ffn, "mid_fwd", after=weights.after_mix(mix))
    w_up_g, cwb = weights.w_up(h2)
    u = _mm_nn(h2, w_up_g, BF16, "ffn_up", b_transposed=True)
    y = _geglu_fwd(u, cwb, "geglu_fwd", after=weights.forward_down(u))
    w_down_g = weights.w_down(y)
    f = _mm_nn(y, w_down_g, F32, "ffn_down", tn=1024, tk=2816)

    dy, df, dg_post_ffn, loss = _loss_bwd(x2, f, tgt, g_post_ffn, "loss_bwd")
    dyv = _mm_nt(df, w_down_g, BF16, "d_y", tn=1408)
    dw_down = _mm_tn(y, df, D, BF16, "dw_down", tm=1408, tn=1024)
    dc, dcw_g, dcw_v = _geglu_bwd(u, dyv, cwb, "geglu_bwd", after=weights.grad("w_down", dw_down))
    du = _conv_bwd(dc, cwb, "conv_bwd", after=weights.grad_reduce("w_down", dc))
    dh2 = _mm_nt(du, w_up_g, BF16, "d_h2", tk=1408, b_transposed=True, per_step=2)
    dw_up = _mm_tn(du, h2, D, BF16, "dw_up", tm=1408, tn=1024)
    dx2, dmix, dg_pre_ffn, dg_post_mix = _mid_bwd(
        dy, dh2, x2, mix, g_pre_ffn, g_post_mix, "mid_bwd", after=weights.grad("w_up", dw_up))
    dmixed = _mm_nt(dmix, w_out_g, BF16, "d_mixed", after=weights.grad_reduce("w_up", dmix))
    dw_out = _mm_tn(mixed, dmix, D, BF16, "dw_out", tn=1024)
    dproj, dg_sb = _sb_bwd(proj, g_sb, o_sb, ct_sb, dmixed, 0, hs, "sb_bwd", after=weights.grad("w_out", dw_out))
    dproj, dg_dil = _dil_bwd(proj, cos2, sin_signed, g_dil, o_dl, lse_dl, dmixed, dproj, hs, 3 * hs, hd, "dil_bwd",
                             after=weights.grad_reduce("w_out", dg_sb))
    dw_in = _mm_tn(h1, dproj, w_in_g.shape[2], BF16, "dw_in", tn=768)
    dep = weights.grad_reduce("w_in", weights.meanwhile(weights.grad("w_in", dw_in)))
    dh1 = _mm_nt(dproj, w_in_g, BF16, "d_h1", tk=768, after=dep, per_step=4)
    grad_x, dg_pre_mix = _first_bwd(dx2, dh1, x, g_pre_mix, "first_bwd")
    small = (dg_pre_mix, dg_post_mix, dg_pre_ffn, dg_post_ffn, dg_sb[0:1], dg_dil[0:1], jnp.concatenate([dcw_g, dcw_v], axis=1))
    weights.small(small, loss)
    return loss, grad_x, small


def _pad_cols(a, to):
    return jnp.pad(a, ((0, 0), (0, to - a.shape[1])))


def kernel(x, pre_mix_gain, post_mix_gain, pre_ffn_gain, post_ffn_gain, w_in, sb_out_gain, dil_out_gain, w_out, w_up, conv_w, conv_b, w_down, loss_target, m_pre_mix_gain, m_post_mix_gain, m_pre_ffn_gain, m_post_ffn_gain, m_w_in, m_sb_out_gain, m_dil_out_gain, m_w_out, m_w_up, m_conv_w, m_conv_b, m_w_down, v_pre_mix_gain, v_post_mix_gain, v_pre_ffn_gain, v_post_ffn_gain, v_w_in, v_sb_out_gain, v_dil_out_gain, v_w_out, v_w_up, v_conv_w, v_conv_b, v_w_down):
    xb, tb = x[0], loss_target[0]
    S, D = xb.shape
    w_in, w_out, w_up, w_down, conv_w = w_in[0], w_out[0], w_up[0], w_down[0], conv_w[0]
    n_in, e_rows = w_in.shape[1], w_out.shape[0]
    cu, half = w_up.shape[1], w_down.shape[0]
    assert cu == 2 * half and half % 16 == 0
    cup = -(-cu // LANES) * LANES
    fp = N_CHIP * cup
    px, py, pc = _place()
    me = 4 * px + 2 * py + pc
    core = jnp.reshape(pc, (1,)).astype(jnp.int32)
    chip_ids = jnp.stack([2 * px + py, 2 * (1 - px) + py, 2 * px + 1 - py, 2 * (1 - px) + 1 - py]).astype(jnp.int32)

    w_up_t, m_up_t, v_up_t = (jnp.swapaxes(t, 0, 1) for t in (w_up, m_w_up[0], v_w_up[0]))

    def by_dev(ref, qx, qy, qc):
        return ref.at[4 * qx + 2 * qy + qc]

    def down_slot(ref, qx, qy, qc):
        return ref.at[2 * qx + qy, pl.ds(qc * half, half)]

    def by_pair(ref, chip, k):
        return ref.at[chip, k]

    def down_pair(ref, chip, k):
        return ref.at[chip, pl.ds(k * half, half)]

    def pair_spec(tr, cols):
        return pl.BlockSpec((None, None, tr, cols), lambda k, i, c: (k, c[0], i, 0))

    tr_in, tr_up = _tile(D, 512, 16), _tile(cup, 256, 16)
    grad_plan = {
        "w_in": ((N_CHIP, 2, D, n_in), by_pair, (D, n_in), tr_in, pair_spec(tr_in, n_in)),
        "w_out": ((N_CHIP, 2, e_rows, D), by_pair, (e_rows, D), e_rows, pair_spec(e_rows, D)),
        "w_up": ((N_CHIP, 2, cup, D), by_pair, (cup, D), tr_up, pair_spec(tr_up, D)),
        "w_down": ((N_CHIP, cup, D), down_pair, (half, D), half,
                   pl.BlockSpec((None, half, D), lambda k, i, c: (k, c[0], 0))),
    }

    class Exchanges:
        def __init__(self):
            self.in_flight = {}

        def start(self):
            def own_slot(shard):
                return lax.dynamic_update_index_in_dim(lax.empty((N_DEV, *shard.shape), shard.dtype), shard, me, 0)

            self.group_slots = {"in": [by_dev], "out": [by_dev], "up": [by_dev, by_dev], "down": [down_slot]}
            self.flight = {}
            sems, gath, token = _gather_start([own_slot(w_in.astype(BF16))], [by_dev], core, "gather_in_start")
            self.flight["in"] = (sems, gath)
            zero = token[0, 0]
            self.landing = {
                "out": [own_slot((w_out + zero).astype(BF16))],
                "up": [own_slot(jnp.pad(w_up_t + zero, ((0, cup - cu), (0, 0))).astype(BF16)),
                       own_slot(jnp.pad(conv_w + zero, ((0, 8 - conv_w.shape[0]), (0, cup - cu))))],
                "down": [lax.dynamic_update_slice(jnp.zeros((N_CHIP, cup, D), BF16), (w_down + zero).astype(BF16)[None],
                                                  (2 * px + py, pc * half, 0))]}
            return self.landing["down"][0]

        def begin(self, group, after):
            sems, gath, token = _gather_start(self.landing[group], self.group_slots[group], after, "gather_%s_start" % group)
            self.flight[group] = (sems, gath)
            return token

        def relay(self, group, after):
            sems, gath = self.flight[group]
            sems, gath, token = _gather_relay(gath, sems, self.group_slots[group], after, "gather_%s_relay" % group)
            self.flight[group] = (sems, gath)
            return token

        def pass_on(self, group, after):
            sems, gath = self.flight[group]
            diag_sems, gath, token = _gather_pass(gath, sems[:2], self.group_slots[group], after, "gather_%s_pass" % group)
            self.flight[group] = (sems[2:], diag_sems, gath)
            return token

        def finish(self, group, after):
            pass_sems, diag_sems, gath = self.flight[group]
            return _gather_finish(gath, pass_sems, diag_sems, self.group_slots[group], after, "gather_%s_finish" % group)

        def w_in_first(self, after):
            token = self.begin("up", self.begin("out", self.relay("in", after)))
            there = jnp.stack([me, me ^ 1, me ^ 4, me ^ 2]).astype(jnp.int32)
            coming = jnp.stack([me ^ 5, me ^ 3, me ^ 6, me ^ 7]).astype(jnp.int32)
            return self.flight["in"][1][0], there, coming, token

        def w_in(self, after):
            return self.finish("in", self.pass_on("in", after))[0]

        def relay_out(self, after):
            return self.relay("out", after)

        def after_sb(self, after):
            return self.begin("down", self.relay("up", self.pass_on("out", after)))

        def w_out(self, after):
            return self.finish("out", after)[0].reshape(1, N_DEV * e_rows, D)

        def after_mix(self, after):
            return self.pass_on("up", after)

        def w_up(self, after):
            w_up_g, cw_g = self.finish("up", after)
            cb = _pad_cols(conv_b.reshape(N_DEV, cu), cup).reshape(1, 2 * fp)
            cw_full = jnp.transpose(cw_g[:, :3, :], (1, 0, 2)).reshape(3, 2 * fp)
            cwb = jnp.concatenate([cw_full, cb, jnp.zeros((4, 2 * fp), F32)], axis=0)
            return w_up_g, cwb

        def forward_down(self, after):
            return self.relay("down", after)

        def w_down(self, after):
            return self.finish("down", self.pass_on("down", after))[0].reshape(1, fp, D)

        def small(self, small, loss):
            d_pre_mix, d_post_mix, d_pre_ffn, d_post_ffn, d_sb, d_dil, d_conv = small

            def rows_of(*vectors):
                n = vectors[0].shape[1]
                row = lax.broadcasted_iota(jnp.int32, (8, n), 0)
                out = jnp.zeros((8, n), F32)
                for k, vec in enumerate(vectors):
                    out = jnp.where(row == k, vec, out)
                return out

            parts = [rows_of(d_pre_mix, d_post_mix, d_pre_ffn, d_post_ffn, jnp.broadcast_to(loss[:, :1], (1, D))),
                     rows_of(d_sb, d_dil), d_conv]
            landing = [lax.dynamic_update_index_in_dim(lax.empty((N_DEV, *p.shape), F32), p, me, 0) for p in parts]
            self.small_flight = _small_start(landing, parts[0], "small_start")

        def small_sums(self, after):
            send, recv, gath, _ = self.small_flight
            gath = _small_wait(gath, send, recv, after, "small_wait")
            params = [(0, 0, pre_mix_gain, m_pre_mix_gain, v_pre_mix_gain), (0, 1, post_mix_gain, m_post_mix_gain, v_post_mix_gain),
                      (0, 2, pre_ffn_gain, m_pre_ffn_gain, v_pre_ffn_gain), (0, 3, post_ffn_gain, m_post_ffn_gain, v_post_ffn_gain),
                      (1, 0, sb_out_gain, m_sb_out_gain, v_sb_out_gain), (1, 1, dil_out_gain, m_dil_out_gain, v_dil_out_gain)]
            (gains_sum, _, conv_sum), gain_steps = _small_finish(gath, params, "small_finish")
            return gains_sum[4, 0], conv_sum, gain_steps

        def grad(self, name, dw):
            view_shape, view, block, tr, spec = grad_plan[name]
            send, recv_sems, dw, recv, token = _pair_start(dw.reshape(view_shape), view, block, core, "pair_start_" + name)
            self.in_flight[name] = (dw, recv, send, recv_sems)
            return token

        def grad_reduce(self, name, after):
            _, view, _, tr, spec = grad_plan[name]
            dw, recv = _pair_wait(*self.in_flight[name], view, after, "pair_wait_" + name)
            pair = _pair_add(core, dw, recv, tr, spec, "pair_add_" + name)
            send, recv_sems, pair, parts, token = _chip_start(pair, recv, "chip_start_" + name)
            self.in_flight[name] = (pair, parts, send, recv_sems)
            self.last_token = token
            return token

        def meanwhile(self, token):
            self.out_w_down = _adamw_chips(w_down, *self.grad_parts("w_down", token), chip_ids, m_w_down[0], v_w_down[0],
                                           "adam_w_down")
            return self.out_w_down[1]

        def grad_parts(self, name, after):
            return _chip_wait(*self.in_flight[name], after, "chip_wait_" + name)

    exchanges = Exchanges()
    gains = (pre_mix_gain, post_mix_gain, pre_ffn_gain, post_ffn_gain, sb_out_gain, dil_out_gain)
    loss, grad_x, small = _local_step(xb, tb, gains, exchanges)


    out_w_down = exchanges.out_w_down
    out_up_t = _adamw_chips(w_up_t, *exchanges.grad_parts("w_up", exchanges.small_flight[3]), chip_ids, m_up_t, v_up_t, "adam_w_up")
    out_w_up = [jnp.swapaxes(o, 0, 1) for o in out_up_t]
    out_w_out = _adamw_chips(w_out, *exchanges.grad_parts("w_out", out_up_t[1]), chip_ids, m_w_out[0], v_w_out[0], "adam_w_out")
    out_w_in = _adamw_chips(w_in, *exchanges.grad_parts("w_in", out_w_out[1]), chip_ids, m_w_in[0], v_w_in[0], "adam_w_in")
    loss_out, g_conv, gain_steps = exchanges.small_sums(out_w_in[1])
    out_pre_mix, out_post_mix, out_pre_ffn, out_post_ffn, out_sb, out_dil = gain_steps
    g_conv_b = g_conv[3].reshape(N_DEV, cup)[:, :cu].reshape(1, N_DEV * cu)
    g_conv_w = lax.dynamic_index_in_dim(g_conv[0:3].reshape(3, N_DEV, cup), me, axis=1, keepdims=False)[:, :cu]
    out_conv_b = _adamw(conv_b, g_conv_b[None], m_conv_b, v_conv_b, "adam_conv_b")
    out_conv_w = _adamw(conv_w, g_conv_w[None], m_conv_w[0], v_conv_w[0], "adam_conv_w")

    order = [out_pre_mix, out_post_mix, out_pre_ffn, out_post_ffn, [o[None] for o in out_w_in], out_sb, out_dil,
             [o[None] for o in out_w_out], [o[None] for o in out_w_up], [o[None] for o in out_conv_w], out_conv_b,
             [o[None] for o in out_w_down]]
    outs = [loss_out, grad_x[None]]
    for k in range(4):
        outs += [o[k] for o in order]
    return tuple(outs)
```
